```python
import math
import jax, jax.numpy as jnp
from jax import lax
import numpy as np

D_MODEL = 1024
BATCH = 16
SEQ = 2048
DEPTH = 2

N_A = max(1, DEPTH // 2)
N_B = DEPTH - N_A

D_RNN = 1280
RNN_BLOCKS = 10
RNN_BW = D_RNN // RNN_BLOCKS
CONV_WIDTH = 4
LRU_C = 8.0

N_HEADS = 8
QK_NOPE = 128
QK_ROPE = 64
V_DIM = 128
KV_RANK = 256
Q_RANK = 384
ROPE_THETA = 10000.0
Q_BLOCK = 128
ATTN_SCALE = (QK_NOPE + QK_ROPE) ** -0.5
EPS = 1e-6

kernel_name = "yoco_rglru_mla_hybrid"


def rms_norm(x, g):
    xf = x.astype(jnp.float32)
    y = xf * lax.rsqrt(jnp.mean(xf * xf, axis=-1, keepdims=True) + EPS)
    return (y * g.astype(jnp.float32)).astype(x.dtype)


def rope_tables(seq_len):
    pos = jnp.arange(seq_len, dtype=jnp.float32)
    inv = ROPE_THETA ** (-jnp.arange(0, QK_ROPE, 2, dtype=jnp.float32) / QK_ROPE)
    ang = pos[:, None] * inv[None, :]
    return jnp.cos(ang), jnp.sin(ang)


def apply_rope(x, cos, sin):
    xf = x.astype(jnp.float32)
    x1, x2 = jnp.split(xf, 2, axis=-1)
    out = jnp.concatenate([x1 * cos - x2 * sin, x2 * cos + x1 * sin], axis=-1)
    return out.astype(x.dtype)


def causal_depthwise_conv(x, w, b):
    c = x.shape[-1]
    y = lax.conv_general_dilated(
        x, w[:, None, :].astype(x.dtype), window_strides=(1,),
        padding=[(CONV_WIDTH - 1, 0)], dimension_numbers=("NWC", "WIO", "NWC"),
        feature_group_count=c)
    return y + b


def _lin_combine(left, right):
    a1, b1 = left
    a2, b2 = right
    return a1 * a2, a2 * b1 + b2


def rglru_layer(x, norm_g, w_in, conv_w, conv_b, w_rg, b_rg, w_ig, b_ig, lam, w_out):
    bsz, seq, _ = x.shape
    h = rms_norm(x, norm_g)
    u = h @ w_in
    xb, gate = u[..., :D_RNN], u[..., D_RNN:]
    xb = causal_depthwise_conv(xb, conv_w, conv_b)
    xblk = xb.reshape(bsz, seq, RNN_BLOCKS, RNN_BW)
    r = jax.nn.sigmoid(jnp.einsum("bsnc,ncd->bsnd", xblk, w_rg).reshape(bsz, seq, D_RNN) + b_rg)
    i = jax.nn.sigmoid(jnp.einsum("bsnc,ncd->bsnd", xblk, w_ig).reshape(bsz, seq, D_RNN) + b_ig)
    log_a = -LRU_C * r.astype(jnp.float32) * jax.nn.softplus(-lam.astype(jnp.float32))
    a = jnp.exp(log_a)
    bterm = jnp.sqrt(-jnp.expm1(2.0 * log_a)) * (i * xb).astype(jnp.float32)
    _, hs = lax.associative_scan(_lin_combine, (a, bterm), axis=1)
    y = hs.astype(x.dtype) * jax.nn.silu(gate)
    return y @ w_out


def mla_shared_kv(x_stream, norm_kv, w_dkv, kv_norm, w_uk, w_uv, cos, sin):
    h = rms_norm(x_stream, norm_kv)
    ckr = h @ w_dkv
    c_kv = rms_norm(ckr[..., :KV_RANK], kv_norm)
    k_rope = apply_rope(ckr[..., KV_RANK:], cos[None], sin[None])
    k_nope = jnp.einsum("bsc,chd->bshd", c_kv, w_uk)
    v = jnp.einsum("bsc,chd->bshd", c_kv, w_uv)
    return k_nope, k_rope, v


def causal_block_attention(q_nope, q_rope, k_nope, k_rope, v):
    bsz, seq, nh, _ = q_nope.shape
    nb = seq // Q_BLOCK
    qn = q_nope.reshape(bsz, nb, Q_BLOCK, nh, QK_NOPE).transpose(1, 0, 2, 3, 4)
    qr = q_rope.reshape(bsz, nb, Q_BLOCK, nh, QK_ROPE).transpose(1, 0, 2, 3, 4)
    kpos = jnp.arange(seq)

    def one_block(args):
        qn_b, qr_b, bi = args
        s = jnp.einsum("bqhd,bkhd->bhqk", qn_b, k_nope, preferred_element_type=jnp.float32)
        s = s + jnp.einsum("bqhr,bkr->bhqk", qr_b, k_rope, preferred_element_type=jnp.float32)
        s = s * ATTN_SCALE
        qpos = bi * Q_BLOCK + jnp.arange(Q_BLOCK)
        mask = kpos[None, :] <= qpos[:, None]
        s = jnp.where(mask[None, None], s, -jnp.inf)
        p = jax.nn.softmax(s, axis=-1)
        return jnp.einsum("bhqk,bkhd->bqhd", p.astype(v.dtype), v)

    o = lax.map(one_block, (qn, qr, jnp.arange(nb)))
    return o.transpose(1, 0, 2, 3, 4).reshape(bsz, seq, nh, V_DIM)


def mla_layer(x, norm_g, w_in, q_norm, w_uq, w_out, k_nope, k_rope, v, cos, sin):
    bsz, seq, _ = x.shape
    h = rms_norm(x, norm_g)
    u = h @ w_in
    c_q = rms_norm(u[..., :Q_RANK], q_norm)
    gate = u[..., Q_RANK:]
    q = jnp.einsum("bsc,chd->bshd", c_q, w_uq)
    q_nope = q[..., :QK_NOPE]
    q_rope = apply_rope(q[..., QK_NOPE:], cos[None, :, None], sin[None, :, None])
    o = causal_block_attention(q_nope, q_rope, k_nope, k_rope, v)
    y = o.reshape(bsz, seq, N_HEADS * V_DIM) * jax.nn.silu(gate)
    return y @ w_out


def _fwd_setup_inputs(seed: int = 0) -> dict:
    key = jax.random.key(seed)
    ks = jax.random.split(key, 24)
    f32 = jnp.float32
    nrm = lambda k, shape, fan_in: jax.random.normal(k, shape, f32) * (fan_in ** -0.5)
    gain = lambda k, shape: 1.0 + 0.02 * jax.random.normal(k, shape, f32)
    small = lambda k, shape: 0.01 * jax.random.normal(k, shape, f32)

    u = jax.random.uniform(ks[8], (N_A, D_RNN), f32, 0.9, 0.999)
    a0 = u ** (1.0 / LRU_C)
    lam = jnp.log(a0) - jnp.log1p(-a0)

    return {
        "x": jax.random.normal(ks[0], (BATCH, SEQ, D_MODEL), f32),
        "norm_a": gain(ks[1], (N_A, D_MODEL)),
        "w_in_a": nrm(ks[2], (N_A, D_MODEL, 2 * D_RNN), D_MODEL),
        "conv_w": nrm(ks[3], (N_A, CONV_WIDTH, D_RNN), CONV_WIDTH),
        "conv_b": small(ks[4], (N_A, D_RNN)),
        "w_rg": nrm(ks[5], (N_A, RNN_BLOCKS, RNN_BW, RNN_BW), RNN_BW),
        "b_rg": small(ks[6], (N_A, D_RNN)),
        "w_ig": nrm(ks[7], (N_A, RNN_BLOCKS, RNN_BW, RNN_BW), RNN_BW),
        "b_ig": small(ks[9], (N_A, D_RNN)),
        "lru_lambda": lam,
        "w_out_a": nrm(ks[10], (N_A, D_RNN, D_MODEL), D_RNN),
        "norm_kv": gain(ks[11], (D_MODEL,)),
        "w_dkv": nrm(ks[12], (D_MODEL, KV_RANK + QK_ROPE), D_MODEL),
        "kv_norm": gain(ks[13], (KV_RANK,)),
        "w_uk": nrm(ks[14], (KV_RANK, N_HEADS, QK_NOPE), KV_RANK),
        "w_uv": nrm(ks[15], (KV_RANK, N_HEADS, V_DIM), KV_RANK),
        "norm_b": gain(ks[16], (N_B, D_MODEL)),
        "w_in_b": nrm(ks[17], (N_B, D_MODEL, Q_RANK + N_HEADS * V_DIM), D_MODEL),
        "q_norm": gain(ks[18], (N_B, Q_RANK)),
        "w_uq": nrm(ks[19], (N_B, Q_RANK, N_HEADS, QK_NOPE + QK_ROPE), Q_RANK),
        "w_out_b": nrm(ks[20], (N_B, N_HEADS * V_DIM, D_MODEL), N_HEADS * V_DIM),
        "final_norm": gain(ks[21], (D_MODEL,)),
    }


def _fwd_reference(x, norm_a, w_in_a, conv_w, conv_b, w_rg, b_rg, w_ig, b_ig, lru_lambda, w_out_a,
              norm_kv, w_dkv, kv_norm, w_uk, w_uv,
              norm_b, w_in_b, q_norm, w_uq, w_out_b, final_norm):
    seq = x.shape[1]
    cos, sin = rope_tables(seq)
    k_nope = k_rope = v = None
    for layer in range(DEPTH):
        if layer < N_A:
            x = x + rglru_layer(x, norm_a[layer], w_in_a[layer], conv_w[layer], conv_b[layer],
                                w_rg[layer], b_rg[layer], w_ig[layer], b_ig[layer],
                                lru_lambda[layer], w_out_a[layer])
        else:
            if layer == N_A:
                k_nope, k_rope, v = mla_shared_kv(x, norm_kv, w_dkv, kv_norm, w_uk, w_uv, cos, sin)
            j = layer - N_A
            x = x + mla_layer(x, norm_b[j], w_in_b[j], q_norm[j], w_uq[j], w_out_b[j],
                              k_nope, k_rope, v, cos, sin)
    return rms_norm(x, final_norm)


import jax as _jax
import jax.numpy as _jnp

TWIN_FORMAT = 'train_step'
FWD_PARAMS = ['x', 'norm_a', 'w_in_a', 'conv_w', 'conv_b', 'w_rg', 'b_rg', 'w_ig', 'b_ig', 'lru_lambda', 'w_out_a', 'norm_kv', 'w_dkv', 'kv_norm', 'w_uk', 'w_uv', 'norm_b', 'w_in_b', 'q_norm', 'w_uq', 'w_out_b', 'final_norm']
TWIN_WEIGHTS = ['norm_a', 'w_in_a', 'conv_w', 'conv_b', 'w_rg', 'b_rg', 'w_ig', 'b_ig', 'lru_lambda', 'w_out_a', 'norm_kv', 'w_dkv', 'kv_norm', 'w_uk', 'w_uv', 'norm_b', 'w_in_b', 'q_norm', 'w_uq', 'w_out_b', 'final_norm']
TWIN_DIFF_INPUT = 'x'
TWIN_INPUTS = ['x', 'norm_a', 'w_in_a', 'conv_w', 'conv_b', 'w_rg', 'b_rg', 'w_ig', 'b_ig', 'lru_lambda', 'w_out_a', 'norm_kv', 'w_dkv', 'kv_norm', 'w_uk', 'w_uv', 'norm_b', 'w_in_b', 'q_norm', 'w_uq', 'w_out_b', 'final_norm', 'loss_target', 'm_norm_a', 'm_w_in_a', 'm_conv_w', 'm_conv_b', 'm_w_rg', 'm_b_rg', 'm_w_ig', 'm_b_ig', 'm_lru_lambda', 'm_w_out_a', 'm_norm_kv', 'm_w_dkv', 'm_kv_norm', 'm_w_uk', 'm_w_uv', 'm_norm_b', 'm_w_in_b', 'm_q_norm', 'm_w_uq', 'm_w_out_b', 'm_final_norm', 'v_norm_a', 'v_w_in_a', 'v_conv_w', 'v_conv_b', 'v_w_rg', 'v_b_rg', 'v_w_ig', 'v_b_ig', 'v_lru_lambda', 'v_w_out_a', 'v_norm_kv', 'v_w_dkv', 'v_kv_norm', 'v_w_uk', 'v_w_uv', 'v_norm_b', 'v_w_in_b', 'v_q_norm', 'v_w_uq', 'v_w_out_b', 'v_final_norm']
TWIN_OUTPUTS = ['loss', 'grad_x', 'grad_norm_a', 'grad_w_in_a', 'grad_conv_w', 'grad_conv_b', 'grad_w_rg', 'grad_b_rg', 'grad_w_ig', 'grad_b_ig', 'grad_lru_lambda', 'grad_w_out_a', 'grad_norm_kv', 'grad_w_dkv', 'grad_kv_norm', 'grad_w_uk', 'grad_w_uv', 'grad_norm_b', 'grad_w_in_b', 'grad_q_norm', 'grad_w_uq', 'grad_w_out_b', 'grad_final_norm', 'delta_norm_a', 'delta_w_in_a', 'delta_conv_w', 'delta_conv_b', 'delta_w_rg', 'delta_b_rg', 'delta_w_ig', 'delta_b_ig', 'delta_lru_lambda', 'delta_w_out_a', 'delta_norm_kv', 'delta_w_dkv', 'delta_kv_norm', 'delta_w_uk', 'delta_w_uv', 'delta_norm_b', 'delta_w_in_b', 'delta_q_norm', 'delta_w_uq', 'delta_w_out_b', 'delta_final_norm', 'new_m_norm_a', 'new_m_w_in_a', 'new_m_conv_w', 'new_m_conv_b', 'new_m_w_rg', 'new_m_b_rg', 'new_m_w_ig', 'new_m_b_ig', 'new_m_lru_lambda', 'new_m_w_out_a', 'new_m_norm_kv', 'new_m_w_dkv', 'new_m_kv_norm', 'new_m_w_uk', 'new_m_w_uv', 'new_m_norm_b', 'new_m_w_in_b', 'new_m_q_norm', 'new_m_w_uq', 'new_m_w_out_b', 'new_m_final_norm', 'new_v_norm_a', 'new_v_w_in_a', 'new_v_conv_w', 'new_v_conv_b', 'new_v_w_rg', 'new_v_b_rg', 'new_v_w_ig', 'new_v_b_ig', 'new_v_lru_lambda', 'new_v_w_out_a', 'new_v_norm_kv', 'new_v_w_dkv', 'new_v_kv_norm', 'new_v_w_uk', 'new_v_w_uv', 'new_v_norm_b', 'new_v_w_in_b', 'new_v_q_norm', 'new_v_w_uq', 'new_v_w_out_b', 'new_v_final_norm']
TWIN_LEAF_KINDS = {'loss': 'loss', 'grad_x': 'grad_x', 'grad_norm_a': 'grad_w', 'grad_w_in_a': 'grad_w', 'grad_conv_w': 'grad_w', 'grad_conv_b': 'grad_w', 'grad_w_rg': 'grad_w', 'grad_b_rg': 'grad_w', 'grad_w_ig': 'grad_w', 'grad_b_ig': 'grad_w', 'grad_lru_lambda': 'grad_w', 'grad_w_out_a': 'grad_w', 'grad_norm_kv': 'grad_w', 'grad_w_dkv': 'grad_w', 'grad_kv_norm': 'grad_w', 'grad_w_uk': 'grad_w', 'grad_w_uv': 'grad_w', 'grad_norm_b': 'grad_w', 'grad_w_in_b': 'grad_w', 'grad_q_norm': 'grad_w', 'grad_w_uq': 'grad_w', 'grad_w_out_b': 'grad_w', 'grad_final_norm': 'grad_w', 'delta_norm_a': 'delta_w', 'delta_w_in_a': 'delta_w', 'delta_conv_w': 'delta_w', 'delta_conv_b': 'delta_w', 'delta_w_rg': 'delta_w', 'delta_b_rg': 'delta_w', 'delta_w_ig': 'delta_w', 'delta_b_ig': 'delta_w', 'delta_lru_lambda': 'delta_w', 'delta_w_out_a': 'delta_w', 'delta_norm_kv': 'delta_w', 'delta_w_dkv': 'delta_w', 'delta_kv_norm': 'delta_w', 'delta_w_uk': 'delta_w', 'delta_w_uv': 'delta_w', 'delta_norm_b': 'delta_w', 'delta_w_in_b': 'delta_w', 'delta_q_norm': 'delta_w', 'delta_w_uq': 'delta_w', 'delta_w_out_b': 'delta_w', 'delta_final_norm': 'delta_w', 'new_m_norm_a': 'new_m', 'new_m_w_in_a': 'new_m', 'new_m_conv_w': 'new_m', 'new_m_conv_b': 'new_m', 'new_m_w_rg': 'new_m', 'new_m_b_rg': 'new_m', 'new_m_w_ig': 'new_m', 'new_m_b_ig': 'new_m', 'new_m_lru_lambda': 'new_m', 'new_m_w_out_a': 'new_m', 'new_m_norm_kv': 'new_m', 'new_m_w_dkv': 'new_m', 'new_m_kv_norm': 'new_m', 'new_m_w_uk': 'new_m', 'new_m_w_uv': 'new_m', 'new_m_norm_b': 'new_m', 'new_m_w_in_b': 'new_m', 'new_m_q_norm': 'new_m', 'new_m_w_uq': 'new_m', 'new_m_w_out_b': 'new_m', 'new_m_final_norm': 'new_m', 'new_v_norm_a': 'new_v', 'new_v_w_in_a': 'new_v', 'new_v_conv_w': 'new_v', 'new_v_conv_b': 'new_v', 'new_v_w_rg': 'new_v', 'new_v_b_rg': 'new_v', 'new_v_w_ig': 'new_v', 'new_v_b_ig': 'new_v', 'new_v_lru_lambda': 'new_v', 'new_v_w_out_a': 'new_v', 'new_v_norm_kv': 'new_v', 'new_v_w_dkv': 'new_v', 'new_v_kv_norm': 'new_v', 'new_v_w_uk': 'new_v', 'new_v_w_uv': 'new_v', 'new_v_norm_b': 'new_v', 'new_v_w_in_b': 'new_v', 'new_v_q_norm': 'new_v', 'new_v_w_uq': 'new_v', 'new_v_w_out_b': 'new_v', 'new_v_final_norm': 'new_v'}


def _forward(args):
    return _fwd_reference(*[args[k] for k in FWD_PARAMS])


def _output_shape():
    out = _jax.eval_shape(lambda: _forward(_fwd_setup_inputs(0)))
    return out.shape, out.dtype

N_MICROBATCH = 1
ADAM_LR = 0.001
ADAM_B1 = 0.9
ADAM_B2 = 0.999
ADAM_EPS = 1e-08
ADAM_WD = 0.01
ADAM_STEP = 10
PER_EXAMPLE_BATCH_AXIS = {'x': 0, 'loss_target': 0}
SHARED_INPUTS = []
_WEIGHT_DTYPES = {'norm_a': _jnp.float32, 'w_in_a': _jnp.float32, 'conv_w': _jnp.float32, 'conv_b': _jnp.float32, 'w_rg': _jnp.float32, 'b_rg': _jnp.float32, 'w_ig': _jnp.float32, 'b_ig': _jnp.float32, 'lru_lambda': _jnp.float32, 'w_out_a': _jnp.float32, 'norm_kv': _jnp.float32, 'w_dkv': _jnp.float32, 'kv_norm': _jnp.float32, 'w_uk': _jnp.float32, 'w_uv': _jnp.float32, 'norm_b': _jnp.float32, 'w_in_b': _jnp.float32, 'q_norm': _jnp.float32, 'w_uq': _jnp.float32, 'w_out_b': _jnp.float32, 'final_norm': _jnp.float32}
MOMENT_SCALE = {'norm_a': 8.753354e-02, 'w_in_a': 5.914097e-02, 'conv_w': 6.113615e-02, 'conv_b': 7.827960e-01, 'w_rg': 2.077564e-02, 'b_rg': 1.588756e-02, 'w_ig': 3.733685e-02, 'b_ig': 2.294690e-02, 'lru_lambda': 3.131217e-02, 'w_out_a': 6.736548e-02, 'norm_kv': 2.764381e-02, 'w_dkv': 5.032046e-02, 'kv_norm': 5.655001e-02, 'w_uk': 1.517031e-02, 'w_uv': 2.258807e-02, 'norm_b': 3.017211e-02, 'w_in_b': 2.533704e-02, 'q_norm': 2.999762e-02, 'w_uq': 1.467283e-02, 'w_out_b': 2.263937e-02, 'final_norm': 3.199274e+01}


def _to_microbatches(a, axis):
    t = _jnp.moveaxis(a, axis, 0)
    t = t.reshape((N_MICROBATCH, t.shape[0] // N_MICROBATCH) + t.shape[1:])
    return _jnp.moveaxis(t, 1, axis + 1)


def setup_inputs(seed: int = 0) -> dict:
    inp = _fwd_setup_inputs(seed)
    key = _jax.random.fold_in(_jax.random.key(seed), 7919)
    shape, _ = _output_shape()
    out = dict(inp)
    out["loss_target"] = _jax.random.normal(_jax.random.fold_in(key, 0), shape, _jnp.float32)
    for i, name in enumerate(TWIN_WEIGHTS):
        w = inp[name].astype(_jnp.float32)
        if MOMENT_SCALE is None:
            s = _jnp.sqrt(_jnp.mean(_jnp.square(w)) + 1e-30)
        else:
            s = MOMENT_SCALE[name]
        km, kv = _jax.random.split(_jax.random.fold_in(key, i + 1))
        out[name] = w
        out["m_" + name] = s * _jax.random.normal(km, w.shape, _jnp.float32)
        out["v_" + name] = (s * s) * _jax.random.uniform(kv, w.shape, _jnp.float32, 0.5, 1.5)
    if N_MICROBATCH > 1:
        for name, axis in PER_EXAMPLE_BATCH_AXIS.items():
            out[name] = _to_microbatches(out[name], axis)
    return {'x': out['x'], 'norm_a': out['norm_a'], 'w_in_a': out['w_in_a'], 'conv_w': out['conv_w'], 'conv_b': out['conv_b'], 'w_rg': out['w_rg'], 'b_rg': out['b_rg'], 'w_ig': out['w_ig'], 'b_ig': out['b_ig'], 'lru_lambda': out['lru_lambda'], 'w_out_a': out['w_out_a'], 'norm_kv': out['norm_kv'], 'w_dkv': out['w_dkv'], 'kv_norm': out['kv_norm'], 'w_uk': out['w_uk'], 'w_uv': out['w_uv'], 'norm_b': out['norm_b'], 'w_in_b': out['w_in_b'], 'q_norm': out['q_norm'], 'w_uq': out['w_uq'], 'w_out_b': out['w_out_b'], 'final_norm': out['final_norm'], 'loss_target': out['loss_target'], 'm_norm_a': out['m_norm_a'], 'm_w_in_a': out['m_w_in_a'], 'm_conv_w': out['m_conv_w'], 'm_conv_b': out['m_conv_b'], 'm_w_rg': out['m_w_rg'], 'm_b_rg': out['m_b_rg'], 'm_w_ig': out['m_w_ig'], 'm_b_ig': out['m_b_ig'], 'm_lru_lambda': out['m_lru_lambda'], 'm_w_out_a': out['m_w_out_a'], 'm_norm_kv': out['m_norm_kv'], 'm_w_dkv': out['m_w_dkv'], 'm_kv_norm': out['m_kv_norm'], 'm_w_uk': out['m_w_uk'], 'm_w_uv': out['m_w_uv'], 'm_norm_b': out['m_norm_b'], 'm_w_in_b': out['m_w_in_b'], 'm_q_norm': out['m_q_norm'], 'm_w_uq': out['m_w_uq'], 'm_w_out_b': out['m_w_out_b'], 'm_final_norm': out['m_final_norm'], 'v_norm_a': out['v_norm_a'], 'v_w_in_a': out['v_w_in_a'], 'v_conv_w': out['v_conv_w'], 'v_conv_b': out['v_conv_b'], 'v_w_rg': out['v_w_rg'], 'v_b_rg': out['v_b_rg'], 'v_w_ig': out['v_w_ig'], 'v_b_ig': out['v_b_ig'], 'v_lru_lambda': out['v_lru_lambda'], 'v_w_out_a': out['v_w_out_a'], 'v_norm_kv': out['v_norm_kv'], 'v_w_dkv': out['v_w_dkv'], 'v_kv_norm': out['v_kv_norm'], 'v_w_uk': out['v_w_uk'], 'v_w_uv': out['v_w_uv'], 'v_norm_b': out['v_norm_b'], 'v_w_in_b': out['v_w_in_b'], 'v_q_norm': out['v_q_norm'], 'v_w_uq': out['v_w_uq'], 'v_w_out_b': out['v_w_out_b'], 'v_final_norm': out['v_final_norm']}


def _loss(weights, diff, rest, loss_target):
    with _jax.named_scope("forward"):
        args = {**rest, TWIN_DIFF_INPUT: diff, **{k: w.astype(_WEIGHT_DTYPES[k]) for k, w in weights.items()}}
        y = _forward(args)
    with _jax.named_scope("loss_head"):
        err = _jnp.square(y.astype(_jnp.float32) - loss_target)
        return 0.5 * _jnp.sum(_jnp.mean(err, axis=-1)) if err.ndim else 0.5 * err


def _adamw(w, g, m, v):
    m = ADAM_B1 * m + (1.0 - ADAM_B1) * g
    v = ADAM_B2 * v + (1.0 - ADAM_B2) * _jnp.square(g)
    m_hat = m / (1.0 - ADAM_B1 ** ADAM_STEP)
    v_hat = v / (1.0 - ADAM_B2 ** ADAM_STEP)
    delta = -ADAM_LR * (m_hat / (_jnp.sqrt(v_hat) + ADAM_EPS) + ADAM_WD * w)
    return delta, m, v


def reference(x, norm_a, w_in_a, conv_w, conv_b, w_rg, b_rg, w_ig, b_ig, lru_lambda, w_out_a, norm_kv, w_dkv, kv_norm, w_uk, w_uv, norm_b, w_in_b, q_norm, w_uq, w_out_b, final_norm, loss_target, m_norm_a, m_w_in_a, m_conv_w, m_conv_b, m_w_rg, m_b_rg, m_w_ig, m_b_ig, m_lru_lambda, m_w_out_a, m_norm_kv, m_w_dkv, m_kv_norm, m_w_uk, m_w_uv, m_norm_b, m_w_in_b, m_q_norm, m_w_uq, m_w_out_b, m_final_norm, v_norm_a, v_w_in_a, v_conv_w, v_conv_b, v_w_rg, v_b_rg, v_w_ig, v_b_ig, v_lru_lambda, v_w_out_a, v_norm_kv, v_w_dkv, v_kv_norm, v_w_uk, v_w_uv, v_norm_b, v_w_in_b, v_q_norm, v_w_uq, v_w_out_b, v_final_norm):
    given = dict(x=x, norm_a=norm_a, w_in_a=w_in_a, conv_w=conv_w, conv_b=conv_b, w_rg=w_rg, b_rg=b_rg, w_ig=w_ig, b_ig=b_ig, lru_lambda=lru_lambda, w_out_a=w_out_a, norm_kv=norm_kv, w_dkv=w_dkv, kv_norm=kv_norm, w_uk=w_uk, w_uv=w_uv, norm_b=norm_b, w_in_b=w_in_b, q_norm=q_norm, w_uq=w_uq, w_out_b=w_out_b, final_norm=final_norm, loss_target=loss_target, m_norm_a=m_norm_a, m_w_in_a=m_w_in_a, m_conv_w=m_conv_w, m_conv_b=m_conv_b, m_w_rg=m_w_rg, m_b_rg=m_b_rg, m_w_ig=m_w_ig, m_b_ig=m_b_ig, m_lru_lambda=m_lru_lambda, m_w_out_a=m_w_out_a, m_norm_kv=m_norm_kv, m_w_dkv=m_w_dkv, m_kv_norm=m_kv_norm, m_w_uk=m_w_uk, m_w_uv=m_w_uv, m_norm_b=m_norm_b, m_w_in_b=m_w_in_b, m_q_norm=m_q_norm, m_w_uq=m_w_uq, m_w_out_b=m_w_out_b, m_final_norm=m_final_norm, v_norm_a=v_norm_a, v_w_in_a=v_w_in_a, v_conv_w=v_conv_w, v_conv_b=v_conv_b, v_w_rg=v_w_rg, v_b_rg=v_b_rg, v_w_ig=v_w_ig, v_b_ig=v_b_ig, v_lru_lambda=v_lru_lambda, v_w_out_a=v_w_out_a, v_norm_kv=v_norm_kv, v_w_dkv=v_w_dkv, v_kv_norm=v_kv_norm, v_w_uk=v_w_uk, v_w_uv=v_w_uv, v_norm_b=v_norm_b, v_w_in_b=v_w_in_b, v_q_norm=v_q_norm, v_w_uq=v_w_uq, v_w_out_b=v_w_out_b, v_final_norm=v_final_norm)
    weights = {n: given[n] for n in TWIN_WEIGHTS}
    shared = {n: given[n] for n in SHARED_INPUTS}
    per_example = {n: given[n] for n in ['x']}
    grad_fn = _jax.value_and_grad(_loss, argnums=(0, 1))

    def one_microbatch(ex, loss_target):
        ex = dict(ex)
        diff = ex.pop(TWIN_DIFF_INPUT)
        return grad_fn(weights, diff, {**shared, **ex}, loss_target)

    if N_MICROBATCH == 1:
        loss, (grad_w, grad_x) = one_microbatch(per_example, given["loss_target"])
    else:
        def body(carry, xs):
            loss_sum, grad_sum = carry
            l_k, (gw_k, gx_k) = one_microbatch(xs[0], xs[1])
            with _jax.named_scope("update"):
                return (loss_sum + l_k, _jax.tree.map(_jnp.add, grad_sum, gw_k)), gx_k

        init = (_jnp.zeros((), _jnp.float32), _jax.tree.map(_jnp.zeros_like, weights))
        (loss, grad_w), grad_x = _jax.lax.scan(body, init, (per_example, given["loss_target"]))
    with _jax.named_scope("update"):
        delta_w, new_m, new_v = {}, {}, {}
        for n in TWIN_WEIGHTS:
            delta_w[n], new_m[n], new_v[n] = _adamw(weights[n], grad_w[n], given["m_" + n], given["v_" + n])
    return (loss, grad_x, *[grad_w[n] for n in TWIN_WEIGHTS], *[delta_w[n] for n in TWIN_WEIGHTS],
            *[new_m[n] for n in TWIN_WEIGHTS], *[new_v[n] for n in TWIN_WEIGHTS])
```

```python
import jax
import jax.numpy as jnp
from jax import lax
from jax.experimental import pallas as pl
from jax.experimental.pallas import tpu as pltpu

F32 = jnp.float32
BF16 = jnp.bfloat16
WIRE = jnp.bfloat16

D_MODEL = 1024
D_RNN = 1280
RNN_BLOCKS = 10
RNN_BW = 128
CONV_WIDTH = 4
LRU_C = 8.0
N_HEADS = 8
QK_NOPE = 128
QK_ROPE = 64
V_DIM = 128
KV_RANK = 256
Q_RANK = 384
ROPE_THETA = 10000.0
EPS = 1e-6
ATTN_SCALE = (QK_NOPE + QK_ROPE) ** -0.5
HEAD_PAD = 256
LANES = 128

ADAM_LR = 0.001
ADAM_B1 = 0.9
ADAM_B2 = 0.999
ADAM_EPS = 1e-08
ADAM_WD = 0.01
ADAM_STEP = 10

N_DEV = 8
MESH_AXES = ("x", "y", "c")
VMEM_LIMIT_BYTES = 56 * 2**20
PACK_W = 1024

_SHARD_PIECES = (("w_in_a", 320), ("w_out_a", 160), ("w_dkv", 40), ("w_uk", 32), ("w_uv", 32),
                 ("w_in_b", 176), ("w_uq", 96), ("w_out_b", 128), ("pad", 8), ("small", 8))
_SHARD_OFF = {}
_r = 0
for _n, _k in _SHARD_PIECES:
    _SHARD_OFF[_n] = (_r, _r + _k)
    _r += _k
SHARD_ROWS = _r
MATRIX_ROWS = _SHARD_OFF["small"][0]
_SMALL = (("norm_a", 128), ("conv_w", 640), ("conv_b", 160), ("b_rg", 160), ("b_ig", 160), ("lru_lambda", 160))
_REP = (("w_rg", 163840), ("w_ig", 163840), ("norm_kv", 1024), ("kv_norm", 256), ("norm_b", 1024),
        ("q_norm", 384), ("final_norm", 1024))
REP_ROWS = 384
REP_SLICE = REP_ROWS // N_DEV
GRAD_ROWS = SHARD_ROWS + REP_SLICE

WEIGHTS = ("norm_a", "w_in_a", "conv_w", "conv_b", "w_rg", "b_rg", "w_ig", "b_ig", "lru_lambda", "w_out_a",
           "norm_kv", "w_dkv", "kv_norm", "w_uk", "w_uv", "norm_b", "w_in_b", "q_norm", "w_uq", "w_out_b",
           "final_norm")


def _params(sem=None):
    return pltpu.CompilerParams(dimension_semantics=sem, vmem_limit_bytes=VMEM_LIMIT_BYTES)


def _sigmoid(z):
    return 1.0 / (1.0 + jnp.exp(-z))


def _col_block(n):
    return n if n <= 1408 else n // 2


def _matmul(a, b, *, name, nt=False, out_dtype=F32, residual=None, bm=512):
    m, k = a.shape
    n = b.shape[0] if nt else b.shape[1]
    bm = min(bm, m)
    bn = _col_block(n)
    dims = (((1,), (1,)), ((), ())) if nt else (((1,), (0,)), ((), ()))
    has_res = residual is not None

    def body(*refs):
        a_ref, b_ref, o_ref = refs[0], refs[1], refs[-1]
        acc = lax.dot_general(a_ref[...].astype(BF16), b_ref[...].astype(BF16), dims, preferred_element_type=F32)
        if has_res:
            acc = acc + refs[2][...]
        o_ref[...] = acc.astype(out_dtype)

    in_specs = [pl.BlockSpec((bm, k), lambda i, j: (i, 0)),
                pl.BlockSpec((bn, k), lambda i, j: (j, 0)) if nt else pl.BlockSpec((k, bn), lambda i, j: (0, j))]
    args = [a, b]
    if has_res:
        in_specs.append(pl.BlockSpec((bm, bn), lambda i, j: (i, j)))
        args.append(residual)
    return pl.pallas_call(
        body, grid=(m // bm, n // bn), in_specs=in_specs, out_specs=pl.BlockSpec((bm, bn), lambda i, j: (i, j)),
        out_shape=jax.ShapeDtypeStruct((m, n), out_dtype), compiler_params=_params(("parallel", "parallel")),
        name=name)(*args)


def _matmul_tn(a, b, *, name, bt=512):
    t, m = a.shape
    n = b.shape[1]
    bt = min(bt, t)
    bm, bn = _col_block(m), _col_block(n)

    def body(a_ref, b_ref, o_ref):
        @pl.when(pl.program_id(2) == 0)
        def _():
            o_ref[...] = jnp.zeros_like(o_ref)

        o_ref[...] += lax.dot_general(a_ref[...].astype(BF16), b_ref[...].astype(BF16),
                                      (((0,), (0,)), ((), ())), preferred_element_type=F32)

    return pl.pallas_call(
        body, grid=(m // bm, n // bn, t // bt),
        in_specs=[pl.BlockSpec((bt, bm), lambda i, j, s: (s, i)), pl.BlockSpec((bt, bn), lambda i, j, s: (s, j))],
        out_specs=pl.BlockSpec((bm, bn), lambda i, j, s: (i, j)),
        out_shape=jax.ShapeDtypeStruct((m, n), F32),
        compiler_params=_params(("parallel", "parallel", "arbitrary")), name=name)(a, b)


def _rms_fwd(x, gains, *, name, bt=512):
    t, d = x.shape
    bt = min(bt, t)
    ng = len(gains)

    def body(x_ref, *refs):
        xv = x_ref[...]
        nrm = xv * lax.rsqrt(jnp.mean(xv * xv, axis=-1, keepdims=True) + EPS)
        for g_ref, o_ref in zip(refs[:ng], refs[ng:]):
            o_ref[...] = (nrm * g_ref[...]).astype(BF16)

    row = pl.BlockSpec((bt, d), lambda i: (i, 0))
    vec = pl.BlockSpec((1, d), lambda i: (0, 0))
    return pl.pallas_call(
        body, grid=(t // bt,), in_specs=[row] + [vec] * ng, out_specs=[row] * ng,
        out_shape=[jax.ShapeDtypeStruct((t, d), BF16)] * ng, compiler_params=_params(("parallel",)), name=name)(x, *gains)


def _rms_bwd(x, dhs, gains, *, name, residual=None, out_dtype=F32, bt=512):
    t, d = x.shape
    bt = min(bt, t)
    ng = len(gains)
    has_res = residual is not None

    def body(*refs):
        x_ref = refs[0]
        dh_refs = refs[1:1 + ng]
        g_refs = refs[1 + ng:1 + 2 * ng]
        pos = 1 + 2 * ng
        res_ref = refs[pos] if has_res else None
        pos += int(has_res)
        dx_ref = refs[pos]
        dg_refs = refs[pos + 1:]

        @pl.when(pl.program_id(0) == 0)
        def _():
            for dg in dg_refs:
                dg[...] = jnp.zeros_like(dg)

        xv = x_ref[...]
        r = lax.rsqrt(jnp.mean(xv * xv, axis=-1, keepdims=True) + EPS)
        nrm = xv * r
        dn = jnp.zeros_like(xv)
        for dh_ref, g_ref, dg in zip(dh_refs, g_refs, dg_refs):
            dh = dh_ref[...]
            dg[...] += jnp.sum(dh * nrm, axis=0, keepdims=True)
            dn = dn + dh * g_ref[...]
        dx = r * (dn - nrm * jnp.mean(dn * nrm, axis=-1, keepdims=True))
        if has_res:
            dx = dx + res_ref[...]
        dx_ref[...] = dx.astype(out_dtype)

    row = pl.BlockSpec((bt, d), lambda i: (i, 0))
    vec = pl.BlockSpec((1, d), lambda i: (0, 0))
    args = [x, *dhs, *gains] + ([residual] if has_res else [])
    outs = pl.pallas_call(
        body, grid=(t // bt,), in_specs=[row] * (1 + ng) + [vec] * ng + [row] * int(has_res),
        out_specs=[row] + [vec] * ng,
        out_shape=[jax.ShapeDtypeStruct((t, d), out_dtype)] + [jax.ShapeDtypeStruct((1, d), F32)] * ng,
        compiler_params=_params(("arbitrary",)), name=name)(*args)
    return outs[0], outs[1:]


def _swap_halves(v):
    ax = v.ndim - 1
    lane = lax.broadcasted_iota(jnp.int32, v.shape, ax)
    up = pltpu.roll(v, LANES - QK_ROPE // 2, axis=ax)
    down = pltpu.roll(v, QK_ROPE // 2, axis=ax)
    return jnp.where(lane < QK_ROPE // 2, up, jnp.where(lane < QK_ROPE, down, 0.0))


def _rope(v, cos, sin):
    return v * cos + _swap_halves(v) * sin


def _rope_t(d, cos, sin):
    return d * cos + _swap_halves(d * sin)


def _rope_tables(seq):
    pos = jnp.arange(seq, dtype=F32)
    inv = ROPE_THETA ** (-jnp.arange(0, QK_ROPE, 2, dtype=F32) / QK_ROPE)
    ang = pos[:, None] * inv[None, :]
    cos, sin = jnp.cos(ang), jnp.sin(ang)
    zero = jnp.zeros((seq, LANES - QK_ROPE), F32)
    return jnp.concatenate([cos, cos, zero], axis=1), jnp.concatenate([-sin, sin, zero], axis=1)


def _rope_q(q, cos, sin, *, seq, name):
    t = q.shape[0]
    bt = min(512, seq)
    per_seq = seq // bt

    def body(q_ref, cos_ref, sin_ref, o_ref):
        o_ref[:, :QK_NOPE] = q_ref[:, :QK_NOPE].astype(BF16)
        o_ref[:, QK_NOPE:] = _rope(q_ref[:, QK_NOPE:], cos_ref[...], sin_ref[...]).astype(BF16)

    blk = pl.BlockSpec((bt, HEAD_PAD), lambda i, h: (i, h))
    tab = pl.BlockSpec((bt, LANES), lambda i, h: (i % per_seq, 0))
    return pl.pallas_call(
        body, grid=(t // bt, N_HEADS), in_specs=[blk, tab, tab], out_specs=blk,
        out_shape=jax.ShapeDtypeStruct(q.shape, BF16), compiler_params=_params(("parallel", "parallel")), name=name)(
            q, cos, sin)


def _rope_k(k, cos, sin, *, seq, name):
    t = k.shape[0]
    bt = min(512, seq)
    per_seq = seq // bt

    def body(k_ref, cos_ref, sin_ref, o_ref):
        o_ref[...] = _rope(k_ref[...], cos_ref[...], sin_ref[...]).astype(BF16)

    blk = pl.BlockSpec((bt, LANES), lambda i: (i, 0))
    tab = pl.BlockSpec((bt, LANES), lambda i: (i % per_seq, 0))
    return pl.pallas_call(
        body, grid=(t // bt,), in_specs=[blk, tab, tab], out_specs=blk,
        out_shape=jax.ShapeDtypeStruct(k.shape, BF16), compiler_params=_params(("parallel",)), name=name)(k, cos, sin)


def _softplus(z):
    return jnp.maximum(z, 0.0) + jnp.log1p(jnp.exp(-jnp.abs(z)))


def _neg_expm1(z):
    series = -z * (1.0 + z * (1.0 / 2) * (1.0 + z * (1.0 / 3) * (1.0 + z * (1.0 / 4) * (1.0 + z * (1.0 / 5) * (
        1.0 + z * (1.0 / 6))))))
    return jnp.where(z > -0.1, series, 1.0 - jnp.exp(z))


def _gates(xb, wrg, wig, brg, big, sp):
    xbb = xb.astype(BF16)
    r = _sigmoid(jnp.dot(xbb, wrg, preferred_element_type=F32) + brg)
    i = _sigmoid(jnp.dot(xbb, wig, preferred_element_type=F32) + big)
    la = (-LRU_C) * r * sp
    a = jnp.exp(la)
    mult = jnp.sqrt(_neg_expm1(2.0 * la))
    return r, i, a, mult


def _conv(xpad_ref, cw_ref, seq):
    acc = cw_ref[0:1, :] * xpad_ref[pl.ds(8 - (CONV_WIDTH - 1), seq), :]
    for k in range(1, CONV_WIDTH):
        acc = acc + cw_ref[k:k + 1, :] * xpad_ref[pl.ds(8 - (CONV_WIDTH - 1) + k, seq), :]
    return acc


def _seq_spec(seq):
    return pl.BlockSpec((None, seq, RNN_BW), lambda n, b: (b, 0, n))


def _chan_spec(rows):
    return pl.BlockSpec((rows, RNN_BW), lambda n, b: (0, n))


_GATE_W_SPEC = pl.BlockSpec((None, RNN_BW, RNN_BW), lambda n, b: (n, 0, 0))


def _lru_fwd(xp, ga, cw, vecs, wrg, wig, *, name):
    bsz, seq, _ = xp.shape
    groups = seq // 8

    def body(xp_ref, ga_ref, cw_ref, vec_ref, wrg_ref, wig_ref, xb_ref, hs_ref, y_ref, xpad, a_s, b_s):
        xpad[0:8, :] = jnp.zeros((8, RNN_BW), F32)
        xpad[pl.ds(8, seq), :] = xp_ref[...]
        xb = _conv(xpad, cw_ref, seq) + vec_ref[0:1, :]
        xb_ref[...] = xb
        sp = _softplus(-vec_ref[3:4, :])
        _, i, a, mult = _gates(xb, wrg_ref[...], wig_ref[...], vec_ref[1:2, :], vec_ref[2:3, :], sp)
        a_s[...] = a
        b_s[...] = mult * (i * xb)
        row = lax.broadcasted_iota(jnp.int32, (8, RNN_BW), 0)

        def group(g, h):
            r0 = pl.multiple_of(g * 8, 8)
            av = a_s[pl.ds(r0, 8), :]
            bv = b_s[pl.ds(r0, 8), :]
            for k in (1, 2, 4):
                m = row >= k
                bv = jnp.where(m, av * pltpu.roll(bv, k, axis=0) + bv, bv)
                av = jnp.where(m, av * pltpu.roll(av, k, axis=0), av)
            rows = av * h + bv
            hs_ref[pl.ds(r0, 8), :] = rows
            return rows[7:8, :]

        lax.fori_loop(0, groups, group, jnp.zeros((1, RNN_BW), F32))
        gav = ga_ref[...]
        y_ref[...] = (hs_ref[...] * (gav * _sigmoid(gav))).astype(BF16)

    sq = _seq_spec(seq)
    shape = (bsz, seq, D_RNN)
    return pl.pallas_call(
        body, grid=(RNN_BLOCKS, bsz), in_specs=[sq, sq, _chan_spec(8), _chan_spec(8), _GATE_W_SPEC, _GATE_W_SPEC],
        out_specs=[sq, sq, sq],
        out_shape=[jax.ShapeDtypeStruct(shape, F32), jax.ShapeDtypeStruct(shape, F32), jax.ShapeDtypeStruct(shape, BF16)],
        scratch_shapes=[pltpu.VMEM((seq + 8, RNN_BW), F32), pltpu.VMEM((seq, RNN_BW), F32), pltpu.VMEM((seq, RNN_BW), F32)],
        compiler_params=_params(("parallel", "parallel")), name=name)(xp, ga, cw, vecs, wrg, wig)


def _lru_bwd(dy, xp, xb, hs, ga, cw, vecs, wrg, wig, *, name):
    bsz, seq, _ = xp.shape
    groups = seq // 8

    def body(dy_ref, xp_ref, xb_ref, hs_ref, ga_ref, cw_ref, vec_ref, wrg_ref, wig_ref,
             dxp_ref, dga_ref, dwrg_ref, dwig_ref, dvec_ref, pad, a_s, d_s, lam_s):
        @pl.when(pl.program_id(1) == 0)
        def _():
            dwrg_ref[...] = jnp.zeros_like(dwrg_ref)
            dwig_ref[...] = jnp.zeros_like(dwig_ref)
            dvec_ref[...] = jnp.zeros_like(dvec_ref)

        xb = xb_ref[...]
        hs = hs_ref[...]
        gav = ga_ref[...]
        dy = dy_ref[...]
        sp = _softplus(-vec_ref[3:4, :])
        wrg = wrg_ref[...]
        wig = wig_ref[...]
        r, i, a, mult = _gates(xb, wrg, wig, vec_ref[1:2, :], vec_ref[2:3, :], sp)
        sg = _sigmoid(gav)
        dga_ref[...] = (dy * hs * (sg * (1.0 + gav * (1.0 - sg)))).astype(BF16)
        d_s[...] = dy * (gav * sg)

        pad[pl.ds(0, seq), :] = a
        pad[pl.ds(seq, 8), :] = jnp.zeros((8, RNN_BW), F32)
        a_s[...] = pad[pl.ds(1, seq), :]
        row = lax.broadcasted_iota(jnp.int32, (8, RNN_BW), 0)

        def group(g, nxt):
            r0 = pl.multiple_of((groups - 1 - g) * 8, 8)
            cv = a_s[pl.ds(r0, 8), :]
            bv = d_s[pl.ds(r0, 8), :]
            for k in (1, 2, 4):
                m = row < 8 - k
                bv = jnp.where(m, cv * pltpu.roll(bv, 8 - k, axis=0) + bv, bv)
                cv = jnp.where(m, cv * pltpu.roll(cv, 8 - k, axis=0), cv)
            rows = cv * nxt + bv
            lam_s[pl.ds(r0, 8), :] = rows
            return rows[0:1, :]

        lax.fori_loop(0, groups, group, jnp.zeros((1, RNN_BW), F32))
        dh = lam_s[...]

        pad[0:8, :] = jnp.zeros((8, RNN_BW), F32)
        pad[pl.ds(8, seq), :] = hs
        da = dh * pad[pl.ds(7, seq), :]
        ixb = i * xb
        dixb = dh * mult
        dla = da * a - (dh * ixb) * (a * a) / mult
        drp = (dla * ((-LRU_C) * sp)) * r * (1.0 - r)
        dip = (dixb * xb) * i * (1.0 - i)
        dvec_ref[0:1, :] += jnp.sum(drp, axis=0, keepdims=True)
        dvec_ref[1:2, :] += jnp.sum(dip, axis=0, keepdims=True)
        dvec_ref[2:3, :] += jnp.sum(dla * ((-LRU_C) * r), axis=0, keepdims=True)
        drpb = drp.astype(BF16)
        dipb = dip.astype(BF16)
        xbb = xb.astype(BF16)
        nt = (((1,), (1,)), ((), ()))
        tn = (((0,), (0,)), ((), ()))
        dxb = (dixb * i
               + lax.dot_general(drpb, wrg, nt, preferred_element_type=F32)
               + lax.dot_general(dipb, wig, nt, preferred_element_type=F32))
        dwrg_ref[...] += lax.dot_general(xbb, drpb, tn, preferred_element_type=F32)
        dwig_ref[...] += lax.dot_general(xbb, dipb, tn, preferred_element_type=F32)
        dvec_ref[3:4, :] += jnp.sum(dxb, axis=0, keepdims=True)

        pad[pl.ds(0, seq), :] = dxb
        pad[pl.ds(seq, 8), :] = jnp.zeros((8, RNN_BW), F32)
        dxp = cw_ref[0:1, :] * pad[pl.ds(CONV_WIDTH - 1, seq), :]
        for k in range(1, CONV_WIDTH):
            dxp = dxp + cw_ref[k:k + 1, :] * pad[pl.ds(CONV_WIDTH - 1 - k, seq), :]
        dxp_ref[...] = dxp.astype(BF16)
        pad[0:8, :] = jnp.zeros((8, RNN_BW), F32)
        pad[pl.ds(8, seq), :] = xp_ref[...]
        for k in range(CONV_WIDTH):
            dvec_ref[4 + k:5 + k, :] += jnp.sum(dxb * pad[pl.ds(8 - (CONV_WIDTH - 1) + k, seq), :], axis=0, keepdims=True)

    sq = _seq_spec(seq)
    shape = (bsz, seq, D_RNN)
    gshape = (RNN_BLOCKS, RNN_BW, RNN_BW)
    return pl.pallas_call(
        body, grid=(RNN_BLOCKS, bsz),
        in_specs=[sq, sq, sq, sq, sq, _chan_spec(8), _chan_spec(8), _GATE_W_SPEC, _GATE_W_SPEC],
        out_specs=[sq, sq, _GATE_W_SPEC, _GATE_W_SPEC, _chan_spec(8)],
        out_shape=[jax.ShapeDtypeStruct(shape, BF16), jax.ShapeDtypeStruct(shape, BF16),
                   jax.ShapeDtypeStruct(gshape, F32), jax.ShapeDtypeStruct(gshape, F32),
                   jax.ShapeDtypeStruct((8, D_RNN), F32)],
        scratch_shapes=[pltpu.VMEM((seq + 8, RNN_BW), F32), pltpu.VMEM((seq, RNN_BW), F32),
                        pltpu.VMEM((seq, RNN_BW), F32), pltpu.VMEM((seq, RNN_BW), F32)],
        compiler_params=_params(("parallel", "arbitrary")), name=name)(dy, xp, xb, hs, ga, cw, vecs, wrg, wig)


def _attn_block(seq):
    return min(256, seq)


def _causal_mask(qi, kj, blk):
    qpos = qi * blk + lax.broadcasted_iota(jnp.int32, (blk, blk), 0)
    kpos = kj * blk + lax.broadcasted_iota(jnp.int32, (blk, blk), 1)
    return kpos <= qpos


def _attn_fwd(q, kn, kr, v, *, bsz, seq, name):
    t = bsz * seq
    blk = _attn_block(seq)
    nq = seq // blk
    nt = (((1,), (1,)), ((), ()))

    def body(q_ref, kn_ref, kr_ref, v_ref, o_ref, lse_ref):
        qi = pl.program_id(2)
        qv = q_ref[...]

        def step(j, carry):
            m_i, l_i, acc = carry
            r0 = pl.multiple_of(j * blk, blk)
            kv = jnp.concatenate([kn_ref[pl.ds(r0, blk), :], kr_ref[pl.ds(r0, blk), :]], axis=1)
            s = lax.dot_general(qv, kv, nt, preferred_element_type=F32) * ATTN_SCALE
            s = jnp.where(_causal_mask(qi, j, blk), s, -jnp.inf)
            m_new = jnp.maximum(m_i, jnp.max(s, axis=-1, keepdims=True))
            p = jnp.exp(s - m_new)
            alpha = jnp.exp(m_i - m_new)
            l_new = alpha * l_i + jnp.sum(p, axis=-1, keepdims=True)
            acc = alpha * acc + jnp.dot(p.astype(BF16), v_ref[pl.ds(r0, blk), :], preferred_element_type=F32)
            return m_new, l_new, acc

        init = (jnp.full((blk, 1), -jnp.inf, F32), jnp.zeros((blk, 1), F32), jnp.zeros((blk, V_DIM), F32))
        m_i, l_i, acc = lax.fori_loop(0, qi + 1, step, init)
        o_ref[...] = acc / l_i
        lse_ref[...] = m_i + jnp.log(l_i)

    return pl.pallas_call(
        body, grid=(bsz, N_HEADS, nq),
        in_specs=[pl.BlockSpec((blk, HEAD_PAD), lambda b, h, i: (b * nq + i, h)),
                  pl.BlockSpec((seq, QK_NOPE), lambda b, h, i: (b, h)),
                  pl.BlockSpec((seq, LANES), lambda b, h, i: (b, 0)),
                  pl.BlockSpec((seq, V_DIM), lambda b, h, i: (b, h))],
        out_specs=[pl.BlockSpec((blk, V_DIM), lambda b, h, i: (b * nq + i, h)),
                   pl.BlockSpec((None, blk, 1), lambda b, h, i: (h, b * nq + i, 0))],
        out_shape=[jax.ShapeDtypeStruct((t, N_HEADS * V_DIM), F32), jax.ShapeDtypeStruct((N_HEADS, t, 1), F32)],
        compiler_params=_params(("parallel", "parallel", "parallel")), name=name)(q, kn, kr, v)


def _attn_bwd(q, kn, kr, v, o, lse, do, cos, sin, *, bsz, seq, name):
    t = bsz * seq
    blk = _attn_block(seq)
    nq = seq // blk
    nt = (((1,), (1,)), ((), ()))
    tn = (((0,), (0,)), ((), ()))

    def body(q_ref, kn_ref, kr_ref, v_ref, o_ref, lse_ref, do_ref, cos_ref, sin_ref,
             dq_ref, dkn_ref, dkr_ref, dv_ref, dq_acc, dk_acc, dv_acc):
        dq_acc[...] = jnp.zeros_like(dq_acc)
        dk_acc[...] = jnp.zeros_like(dk_acc)
        dv_acc[...] = jnp.zeros_like(dv_acc)

        def q_block(i, _):
            q0 = pl.multiple_of(i * blk, blk)
            qv = q_ref[pl.ds(q0, blk), :]
            dov = do_ref[pl.ds(q0, blk), :]
            lse_i = lse_ref[pl.ds(q0, blk), :]
            delta = jnp.sum(dov.astype(F32) * o_ref[pl.ds(q0, blk), :], axis=-1, keepdims=True)

            def k_block(j, _):
                k0 = pl.multiple_of(j * blk, blk)
                kv = jnp.concatenate([kn_ref[pl.ds(k0, blk), :], kr_ref[pl.ds(k0, blk), :]], axis=1)
                vv = v_ref[pl.ds(k0, blk), :]
                s = lax.dot_general(qv, kv, nt, preferred_element_type=F32) * ATTN_SCALE
                p = jnp.where(_causal_mask(i, j, blk), jnp.exp(s - lse_i), 0.0)
                pb = p.astype(BF16)
                dv_acc[pl.ds(k0, blk), :] += lax.dot_general(pb, dov, tn, preferred_element_type=F32)
                dp = lax.dot_general(dov, vv, nt, preferred_element_type=F32)
                ds = (p * (dp - delta) * ATTN_SCALE).astype(BF16)
                dq_acc[pl.ds(q0, blk), :] += jnp.dot(ds, kv, preferred_element_type=F32)
                dk_acc[pl.ds(k0, blk), :] += lax.dot_general(ds, qv, tn, preferred_element_type=F32)
                return 0

            lax.fori_loop(0, i + 1, k_block, 0)
            return 0

        lax.fori_loop(0, nq, q_block, 0)
        cosv = cos_ref[...]
        sinv = sin_ref[...]
        dq_ref[:, :QK_NOPE] = dq_acc[:, :QK_NOPE].astype(BF16)
        dq_ref[:, QK_NOPE:] = _rope_t(dq_acc[:, QK_NOPE:], cosv, sinv).astype(BF16)
        dkn_ref[...] = dk_acc[:, :QK_NOPE].astype(BF16)
        dv_ref[...] = dv_acc[...].astype(BF16)

        @pl.when(pl.program_id(1) == 0)
        def _():
            dkr_ref[...] = jnp.zeros_like(dkr_ref)

        dkr_ref[...] += _rope_t(dk_acc[:, QK_NOPE:], cosv, sinv)

    head = pl.BlockSpec((seq, V_DIM), lambda b, h: (b, h))
    shared = pl.BlockSpec((seq, LANES), lambda b, h: (b, 0))
    table = pl.BlockSpec((seq, LANES), lambda b, h: (0, 0))
    qspec = pl.BlockSpec((seq, HEAD_PAD), lambda b, h: (b, h))
    return pl.pallas_call(
        body, grid=(bsz, N_HEADS),
        in_specs=[qspec, head, shared, head, head, pl.BlockSpec((None, seq, 1), lambda b, h: (h, b, 0)), head, table, table],
        out_specs=[qspec, head, shared, head],
        out_shape=[jax.ShapeDtypeStruct((t, N_HEADS * HEAD_PAD), BF16), jax.ShapeDtypeStruct((t, N_HEADS * QK_NOPE), BF16),
                   jax.ShapeDtypeStruct((t, LANES), F32), jax.ShapeDtypeStruct((t, N_HEADS * V_DIM), BF16)],
        scratch_shapes=[pltpu.VMEM((seq, HEAD_PAD), F32), pltpu.VMEM((seq, HEAD_PAD), F32), pltpu.VMEM((seq, V_DIM), F32)],
        compiler_params=_params(("parallel", "arbitrary")), name=name)(q, kn, kr, v, o, lse, do, cos, sin)


def _head_and_loss(o, g2, x1, target, w_out, g_final, *, name, bt=256):
    t, d = x1.shape
    bt = min(bt, t)
    nt = (((1,), (1,)), ((), ()))

    def body(o_ref, g2_ref, x1_ref, tgt_ref, w_ref, gf_ref, loss_ref, dx2_ref, y2_ref, do_ref, dg2_ref, dgf_ref):
        @pl.when(pl.program_id(0) == 0)
        def _():
            loss_ref[...] = jnp.zeros_like(loss_ref)
            dgf_ref[...] = jnp.zeros_like(dgf_ref)

        ov = o_ref[...]
        gv = g2_ref[...]
        sg = _sigmoid(gv)
        silu = gv * sg
        y2 = (ov * silu).astype(BF16)
        y2_ref[...] = y2
        w = w_ref[...]
        x2 = x1_ref[...] + jnp.dot(y2, w, preferred_element_type=F32)
        r = lax.rsqrt(jnp.mean(x2 * x2, axis=-1, keepdims=True) + EPS)
        nrm = x2 * r
        gf = gf_ref[...]
        err = nrm * gf - tgt_ref[...]
        loss_ref[...] += 0.5 * jnp.sum(jnp.mean(err * err, axis=-1, keepdims=True))
        dyf = err * (1.0 / d)
        dgf_ref[...] += jnp.sum(dyf * nrm, axis=0, keepdims=True)
        dn = dyf * gf
        dx2 = r * (dn - nrm * jnp.mean(dn * nrm, axis=-1, keepdims=True))
        dx2_ref[...] = dx2
        dy2 = lax.dot_general(dx2.astype(BF16), w, nt, preferred_element_type=F32)
        do_ref[...] = (dy2 * silu).astype(BF16)
        dg2_ref[...] = (dy2 * ov * (sg * (1.0 + gv * (1.0 - sg)))).astype(BF16)

    row = pl.BlockSpec((bt, d), lambda i: (i, 0))
    vec = pl.BlockSpec((1, d), lambda i: (0, 0))
    return pl.pallas_call(
        body, grid=(t // bt,),
        in_specs=[row, row, row, row, pl.BlockSpec((d, d), lambda i: (0, 0)), vec],
        out_specs=[pl.BlockSpec((8, LANES), lambda i: (0, 0)), row, row, row, row, vec],
        out_shape=[jax.ShapeDtypeStruct((8, LANES), F32), jax.ShapeDtypeStruct((t, d), F32),
                   jax.ShapeDtypeStruct((t, d), BF16), jax.ShapeDtypeStruct((t, d), BF16),
                   jax.ShapeDtypeStruct((t, d), BF16), jax.ShapeDtypeStruct((1, d), F32)],
        compiler_params=_params(("arbitrary",)), name=name)(o, g2, x1, target, w_out, g_final)


def _sum_parts(parts, *, row0, rows, name, br=8):
    npart, _, w = parts.shape
    first = row0 // br

    def body(p_ref, o_ref):
        acc = p_ref[0]
        for j in range(1, npart):
            acc = acc + p_ref[j]
        o_ref[...] = acc

    return pl.pallas_call(
        body, grid=(rows // br,), in_specs=[pl.BlockSpec((npart, br, w), lambda i: (0, first + i, 0))],
        out_specs=pl.BlockSpec((br, w), lambda i: (i, 0)), out_shape=jax.ShapeDtypeStruct((rows, w), F32),
        compiler_params=_params(("parallel",)), name=name)(parts)


def _adamw(parts, w, m, v, *, name, br):
    npart = parts.shape[0]
    rows, width = w.shape

    def body(p_ref, w_ref, m_ref, v_ref, g_ref, d_ref, nm_ref, nv_ref):
        g = p_ref[0]
        for j in range(1, npart):
            g = g + p_ref[j]
        g_ref[...] = g
        nm = ADAM_B1 * m_ref[...] + (1.0 - ADAM_B1) * g
        nv = ADAM_B2 * v_ref[...] + (1.0 - ADAM_B2) * (g * g)
        nm_ref[...] = nm
        nv_ref[...] = nv
        m_hat = nm / (1.0 - ADAM_B1 ** ADAM_STEP)
        v_hat = nv / (1.0 - ADAM_B2 ** ADAM_STEP)
        d_ref[...] = (-ADAM_LR) * (m_hat / (jnp.sqrt(v_hat) + ADAM_EPS) + ADAM_WD * w_ref[...])

    row = pl.BlockSpec((br, width), lambda i: (i, 0))
    out = jax.ShapeDtypeStruct((rows, width), F32)
    return pl.pallas_call(
        body, grid=(rows // br,), in_specs=[pl.BlockSpec((npart, br, width), lambda i: (0, i, 0)), row, row, row],
        out_specs=[row] * 4, out_shape=[out] * 4, compiler_params=_params(("parallel",)), name=name)(parts, w, m, v)


def _mesh_pos():
    return lax.axis_index("x"), lax.axis_index("y"), lax.axis_index("c")


_ANY = pl.BlockSpec(memory_space=pl.ANY)


def _all_gather(block, *, name):
    m, n = block.shape

    def body(x_ref, out_ref, send_sems, recv_sems, local_sem):
        x, y, c = _mesh_pos()
        me, sibling = (x, y, c), (x, y, 1 - c)
        chips = [(1 - x, y), (x, 1 - y), (1 - x, 1 - y)]

        def slot(px, py, pc):
            return out_ref.at[4 * px + 2 * py + pc]

        def copy(k, blk, to, src=None):
            return pltpu.make_async_remote_copy(
                src_ref=slot(*blk) if src is None else src, dst_ref=slot(*blk),
                send_sem=send_sems.at[k], recv_sem=recv_sems.at[k], device_id=to, device_id_type=pl.DeviceIdType.MESH)

        mine = pltpu.make_async_copy(x_ref, slot(*me), local_sem)
        mine.start()
        first = [copy(0, me, sibling, src=x_ref)]
        first += [copy(1 + j, me, (*chip, c), src=x_ref) for j, chip in enumerate(chips)]
        for cp in first:
            cp.start()
        passed = [copy(4 + j, (*chip, c), sibling) for j, chip in enumerate(chips)]
        for j, chip in enumerate(chips):
            copy(1 + j, (*chip, c), me).wait_recv()
            passed[j].start()
        copy(0, sibling, me).wait_recv()
        for j, chip in enumerate(chips):
            copy(4 + j, (*chip, 1 - c), me).wait_recv()
        for cp in first + passed:
            cp.wait_send()
        mine.wait()

    return pl.pallas_call(
        body, out_shape=jax.ShapeDtypeStruct((N_DEV, m, n), block.dtype), in_specs=[_ANY], out_specs=_ANY,
        scratch_shapes=[pltpu.SemaphoreType.DMA((7,)), pltpu.SemaphoreType.DMA((7,)), pltpu.SemaphoreType.DMA(())],
        name=name)(block)


def _exchange(parts, *, name):
    def body(p_ref, land_ref, send_sems, recv_sems, local_sem):
        x, y, c = _mesh_pos()
        me = 4 * x + 2 * y + c
        mine = pltpu.make_async_copy(p_ref.at[me], land_ref.at[me], local_sem)
        mine.start()
        peers = []
        for k in range(1, N_DEV):
            px = 1 - x if k & 4 else x
            py = 1 - y if k & 2 else y
            pc = 1 - c if k & 1 else c
            peers.append((px, py, pc))
        sends = []
        for k, (px, py, pc) in enumerate(peers):
            cp = pltpu.make_async_remote_copy(
                src_ref=p_ref.at[4 * px + 2 * py + pc], dst_ref=land_ref.at[me],
                send_sem=send_sems.at[k], recv_sem=recv_sems.at[k], device_id=(px, py, pc),
                device_id_type=pl.DeviceIdType.MESH)
            cp.start()
            sends.append(cp)
        for k, (px, py, pc) in enumerate(peers):
            slot = land_ref.at[4 * px + 2 * py + pc]
            pltpu.make_async_remote_copy(
                src_ref=slot, dst_ref=slot, send_sem=send_sems.at[k], recv_sem=recv_sems.at[k],
                device_id=(px, py, pc), device_id_type=pl.DeviceIdType.MESH).wait_recv()
        for cp in sends:
            cp.wait_send()
        mine.wait()

    return pl.pallas_call(
        body, out_shape=jax.ShapeDtypeStruct(parts.shape, parts.dtype), in_specs=[_ANY], out_specs=_ANY,
        scratch_shapes=[pltpu.SemaphoreType.DMA((7,)), pltpu.SemaphoreType.DMA((7,)), pltpu.SemaphoreType.DMA(())],
        name=name)(parts)


def _rows(a):
    return a.reshape(-1, PACK_W)


def _pad_to(a, n):
    return jnp.pad(a, (0, n - a.shape[0]))


def _pack_shard(d):
    small = jnp.concatenate([d[n].reshape(-1) for n, _ in _SMALL])
    w_uq = jnp.pad(d["w_uq"][0], ((0, 0), (0, 0), (0, HEAD_PAD - QK_NOPE - QK_ROPE)))
    pieces = {"w_in_a": d["w_in_a"], "w_out_a": d["w_out_a"], "w_dkv": d["w_dkv"], "w_uk": d["w_uk"], "w_uv": d["w_uv"],
              "w_in_b": d["w_in_b"], "w_uq": w_uq, "w_out_b": d["w_out_b"], "pad": jnp.zeros((8, PACK_W), F32),
              "small": _pad_to(small, 8 * PACK_W)}
    return jnp.concatenate([_rows(pieces[n]) for n, _ in _SHARD_PIECES], axis=0)


def _unpack_shard(p):
    out = {}
    piece = {n: p[lo:hi] for n, (lo, hi) in _SHARD_OFF.items()}
    out["w_in_a"] = piece["w_in_a"].reshape(1, D_MODEL, 2 * D_RNN // N_DEV)
    out["w_out_a"] = piece["w_out_a"].reshape(1, D_RNN // N_DEV, D_MODEL)
    out["w_dkv"] = piece["w_dkv"].reshape(D_MODEL // N_DEV, KV_RANK + QK_ROPE)
    out["w_uk"] = piece["w_uk"].reshape(KV_RANK // N_DEV, N_HEADS, QK_NOPE)
    out["w_uv"] = piece["w_uv"].reshape(KV_RANK // N_DEV, N_HEADS, V_DIM)
    out["w_in_b"] = piece["w_in_b"].reshape(1, D_MODEL, (Q_RANK + N_HEADS * V_DIM) // N_DEV)
    out["w_uq"] = piece["w_uq"].reshape(1, Q_RANK // N_DEV, N_HEADS, HEAD_PAD)[..., :QK_NOPE + QK_ROPE]
    out["w_out_b"] = piece["w_out_b"].reshape(1, N_HEADS * V_DIM // N_DEV, D_MODEL)
    small = piece["small"].reshape(-1)
    off = 0
    shapes = {"norm_a": (1, D_MODEL // N_DEV), "conv_w": (1, CONV_WIDTH, D_RNN // N_DEV), "conv_b": (1, D_RNN // N_DEV),
              "b_rg": (1, D_RNN // N_DEV), "b_ig": (1, D_RNN // N_DEV), "lru_lambda": (1, D_RNN // N_DEV)}
    for n, k in _SMALL:
        out[n] = small[off:off + k].reshape(shapes[n])
        off += k
    return out


def _pack_rep(d):
    flat = jnp.concatenate([d[n].reshape(-1) for n, _ in _REP])
    return _rows(_pad_to(flat, REP_ROWS * PACK_W))


def _unpack_rep(p, like):
    flat = p.reshape(-1)
    out, off = {}, 0
    for n, k in _REP:
        out[n] = flat[off:off + k].reshape(like[n].shape)
        off += k
    return out


def _gathered_weights(wall):
    piece = {n: wall[:, lo:hi] for n, (lo, hi) in _SHARD_OFF.items() if n != "small"}
    w = {}
    w_in_a = piece["w_in_a"].reshape(N_DEV, D_MODEL, -1).transpose(1, 0, 2).reshape(D_MODEL, 2 * D_RNN)
    w["w_xp"], w["w_ga"] = w_in_a[:, :D_RNN], w_in_a[:, D_RNN:]
    w["w_out_a"] = piece["w_out_a"].reshape(D_RNN, D_MODEL)
    w_dkv = piece["w_dkv"].reshape(D_MODEL, KV_RANK + QK_ROPE)
    w["w_dkv_c"] = w_dkv[:, :KV_RANK]
    w["w_dkv_r"] = jnp.pad(w_dkv[:, KV_RANK:], ((0, 0), (0, LANES - QK_ROPE)))
    w["w_uk"] = piece["w_uk"].reshape(KV_RANK, N_HEADS * QK_NOPE)
    w["w_uv"] = piece["w_uv"].reshape(KV_RANK, N_HEADS * V_DIM)
    w_in_b = piece["w_in_b"].reshape(N_DEV, D_MODEL, -1).transpose(1, 0, 2).reshape(D_MODEL, Q_RANK + N_HEADS * V_DIM)
    w["w_cq"], w["w_g2"] = w_in_b[:, :Q_RANK], w_in_b[:, Q_RANK:]
    w["w_uq"] = piece["w_uq"].reshape(Q_RANK, N_HEADS * HEAD_PAD)
    w["w_out_b"] = piece["w_out_b"].reshape(N_HEADS * V_DIM, D_MODEL)
    small = lax.bitcast_convert_type(wall[:, MATRIX_ROWS:].reshape(N_DEV, 8 * PACK_W, 2), F32)
    off = dict(zip([n for n, _ in _SMALL], [0, 128, 768, 928, 1088, 1248]))
    w["norm_a"] = small[:, :128].reshape(1, D_MODEL)

    def by_channel(lo, rows):
        a = small[:, lo:lo + rows * (D_RNN // N_DEV)].reshape(N_DEV, rows, -1).transpose(1, 0, 2).reshape(rows, D_RNN)
        return jnp.pad(a, ((0, 8 - rows), (0, 0)))

    w["conv_taps"] = by_channel(off["conv_w"], CONV_WIDTH)
    w["lru_vecs"] = by_channel(off["conv_b"], 4)
    return w


def _pack_grads(g):
    def by_cols(a):
        r, n = a.shape
        return a.reshape(r, N_DEV, n // N_DEV).transpose(1, 0, 2).reshape(N_DEV, -1, PACK_W)

    def by_rows(a):
        return a.reshape(N_DEV, -1, PACK_W)

    small = jnp.concatenate([
        g["norm_a"].reshape(N_DEV, -1),
        g["conv_w"].reshape(CONV_WIDTH, N_DEV, -1).transpose(1, 0, 2).reshape(N_DEV, -1),
        g["conv_b"].reshape(N_DEV, -1), g["b_rg"].reshape(N_DEV, -1), g["b_ig"].reshape(N_DEV, -1),
        g["lru_lambda"].reshape(N_DEV, -1)], axis=1)
    small = jnp.pad(small, ((0, 0), (0, 8 * PACK_W - small.shape[1]))).reshape(N_DEV, 8, PACK_W)
    pieces = {"w_in_a": by_cols(g["w_in_a"]), "w_out_a": by_rows(g["w_out_a"]), "w_dkv": by_rows(g["w_dkv"]),
              "w_uk": by_rows(g["w_uk"]), "w_uv": by_rows(g["w_uv"]), "w_in_b": by_cols(g["w_in_b"]),
              "w_uq": by_rows(g["w_uq"]), "w_out_b": by_rows(g["w_out_b"]), "pad": jnp.zeros((N_DEV, 8, PACK_W), F32),
              "small": small}
    rep = _pack_rep(g).reshape(N_DEV, REP_SLICE, PACK_W)
    return jnp.concatenate([pieces[n] for n, _ in _SHARD_PIECES] + [rep], axis=1)


def _step(x, target, w, rep, *, bsz, seq):
    t = bsz * seq
    cos, sin = _rope_tables(seq)
    g_a = w["norm_a"]
    g_kv = rep["norm_kv"].reshape(1, -1)
    g_kvn = rep["kv_norm"].reshape(1, -1)
    g_b = rep["norm_b"].reshape(1, -1)
    g_q = rep["q_norm"].reshape(1, -1)
    g_f = rep["final_norm"].reshape(1, -1)
    wrg = rep["w_rg"][0].astype(BF16)
    wig = rep["w_ig"][0].astype(BF16)
    cw8, vecs = w["conv_taps"], w["lru_vecs"]

    def seq3(a):
        return a.reshape(bsz, seq, a.shape[-1])

    def flat(a):
        return a.reshape(t, a.shape[-1])

    (h0,) = _rms_fwd(x, [g_a], name="norm_a_fwd")
    xp = _matmul(h0, w["w_xp"], name="in_a_x")
    ga = _matmul(h0, w["w_ga"], name="in_a_gate")
    xb, hs, y = _lru_fwd(seq3(xp), seq3(ga), cw8, vecs, wrg, wig, name="lru_fwd")
    x1 = _matmul(flat(y), w["w_out_a"], residual=x, name="out_a")
    hk, hq = _rms_fwd(x1, [g_kv, g_b], name="norm_kvb_fwd")
    ck = _matmul(hk, w["w_dkv_c"], name="dkv_c")
    krp = _matmul(hk, w["w_dkv_r"], name="dkv_r")
    cqp = _matmul(hq, w["w_cq"], name="in_b_q")
    g2 = _matmul(hq, w["w_g2"], name="in_b_gate")
    (ckv,) = _rms_fwd(ck, [g_kvn], name="kv_norm_fwd")
    (cq,) = _rms_fwd(cqp, [g_q], name="q_norm_fwd")
    kr = _rope_k(krp, cos, sin, seq=seq, name="rope_k")
    kn = _matmul(ckv, w["w_uk"], out_dtype=BF16, name="uk")
    v = _matmul(ckv, w["w_uv"], out_dtype=BF16, name="uv")
    q = _rope_q(_matmul(cq, w["w_uq"], name="uq"), cos, sin, seq=seq, name="rope_q")
    o, lse = _attn_fwd(q, kn, kr, v, bsz=bsz, seq=seq, name="attn_fwd")
    loss, dx2, y2, do, dg2, dgf = _head_and_loss(o, g2, x1, target, w["w_out_b"], g_f, name="head_loss")
    grads = {"final_norm": dgf, "w_out_b": _matmul_tn(y2, dx2, name="d_w_out_b")}
    dq, dkn, dkr, dv = _attn_bwd(q, kn, kr, v, o, lse, do, cos, sin, bsz=bsz, seq=seq, name="attn_bwd")
    grads["w_uq"] = _matmul_tn(cq, dq, name="d_w_uq")
    dcq = _matmul(dq, w["w_uq"], nt=True, name="d_cq")
    dcqp, (dgq,) = _rms_bwd(cqp, [dcq], [g_q], out_dtype=BF16, name="q_norm_bwd")
    grads["q_norm"] = dgq
    grads["w_in_b"] = jnp.concatenate([_matmul_tn(hq, dcqp, name="d_w_in_b_q"), _matmul_tn(hq, dg2, name="d_w_in_b_g")], axis=1)
    dhq = _matmul(dg2, w["w_g2"], nt=True, residual=_matmul(dcqp, w["w_cq"], nt=True, name="d_hq_q"), name="d_hq")
    grads["w_uk"] = _matmul_tn(ckv, dkn, name="d_w_uk")
    grads["w_uv"] = _matmul_tn(ckv, dv, name="d_w_uv")
    dckv = _matmul(dv, w["w_uv"], nt=True, residual=_matmul(dkn, w["w_uk"], nt=True, name="d_ckv_k"), name="d_ckv")
    dck, (dgkvn,) = _rms_bwd(ck, [dckv], [g_kvn], out_dtype=BF16, name="kv_norm_bwd")
    grads["kv_norm"] = dgkvn
    grads["w_dkv"] = jnp.concatenate([_matmul_tn(hk, dck, name="d_w_dkv_c"),
                                      _matmul_tn(hk, dkr, name="d_w_dkv_r")[:, :QK_ROPE]], axis=1)
    dhk = _matmul(dkr, w["w_dkv_r"], nt=True, residual=_matmul(dck, w["w_dkv_c"], nt=True, name="d_hk_c"), name="d_hk")
    dx1, (dgb, dgkv) = _rms_bwd(x1, [dhq, dhk], [g_b, g_kv], residual=dx2, name="norm_kvb_bwd")
    grads["norm_b"], grads["norm_kv"] = dgb, dgkv
    grads["w_out_a"] = _matmul_tn(flat(y), dx1, name="d_w_out_a")
    dy = _matmul(dx1, w["w_out_a"], nt=True, name="d_y")
    dxp, dga, dwrg, dwig, dvec = _lru_bwd(seq3(dy), seq3(xp), xb, hs, seq3(ga), cw8, vecs, wrg, wig, name="lru_bwd")
    dxp, dga = flat(dxp), flat(dga)
    grads["w_rg"], grads["w_ig"] = dwrg, dwig
    grads["b_rg"], grads["b_ig"], grads["conv_b"] = dvec[0], dvec[1], dvec[3]
    lam = vecs[3]
    grads["lru_lambda"] = dvec[2] * (-1.0 / (1.0 + jnp.exp(lam)))
    grads["conv_w"] = dvec[4:4 + CONV_WIDTH]
    grads["w_in_a"] = jnp.concatenate([_matmul_tn(h0, dxp, name="d_w_in_a_x"), _matmul_tn(h0, dga, name="d_w_in_a_g")], axis=1)
    dh0 = _matmul(dga, w["w_ga"], nt=True, residual=_matmul(dxp, w["w_xp"], nt=True, name="d_h0_x"), name="d_h0")
    dx, (dga_norm,) = _rms_bwd(x, [dh0], [g_a], residual=dx1, name="norm_a_bwd")
    grads["norm_a"] = dga_norm
    return loss[0, 0], dx, grads


def kernel(x, norm_a, w_in_a, conv_w, conv_b, w_rg, b_rg, w_ig, b_ig, lru_lambda, w_out_a, norm_kv, w_dkv, kv_norm, w_uk, w_uv, norm_b, w_in_b, q_norm, w_uq, w_out_b, final_norm, loss_target, m_norm_a, m_w_in_a, m_conv_w, m_conv_b, m_w_rg, m_b_rg, m_w_ig, m_b_ig, m_lru_lambda, m_w_out_a, m_norm_kv, m_w_dkv, m_kv_norm, m_w_uk, m_w_uv, m_norm_b, m_w_in_b, m_q_norm, m_w_uq, m_w_out_b, m_final_norm, v_norm_a, v_w_in_a, v_conv_w, v_conv_b, v_w_rg, v_b_rg, v_w_ig, v_b_ig, v_lru_lambda, v_w_out_a, v_norm_kv, v_w_dkv, v_kv_norm, v_w_uk, v_w_uv, v_norm_b, v_w_in_b, v_q_norm, v_w_uq, v_w_out_b, v_final_norm):
    given = dict(locals())
    wts = {n: given[n] for n in WEIGHTS}
    mom1 = {n: given["m_" + n] for n in WEIGHTS}
    mom2 = {n: given["v_" + n] for n in WEIGHTS}
    bsz, seq, _ = x.shape
    t = bsz * seq

    w_pack = _pack_shard(wts)
    vec_bits = lax.bitcast_convert_type(w_pack[MATRIX_ROWS:], WIRE).reshape(16, PACK_W)
    send = jnp.concatenate([w_pack[:MATRIX_ROWS].astype(WIRE), vec_bits], axis=0)
    wall = _all_gather(send, name="gather_weights")
    w = _gathered_weights(wall)

    loss, dx, grads = _step(x.reshape(t, D_MODEL), loss_target.reshape(t, D_MODEL), w, wts, bsz=bsz, seq=seq)
    loss = lax.psum(loss, MESH_AXES)

    land = _exchange(_pack_grads(grads), name="exchange_grads")
    g_sh, d_sh, m_sh, v_sh = _adamw(land, w_pack, _pack_shard(mom1), _pack_shard(mom2), name="adamw_sharded", br=200)
    rep_slice = _sum_parts(land, row0=SHARD_ROWS, rows=REP_SLICE, name="sum_replicated")
    g_rep_all = _all_gather(rep_slice, name="gather_replicated").reshape(1, REP_ROWS, PACK_W)
    g_rep, d_rep, m_rep, v_rep = _adamw(g_rep_all, _pack_rep(wts), _pack_rep(mom1), _pack_rep(mom2),
                                        name="adamw_replicated", br=REP_ROWS)

    outs = []
    for sh, rp in ((g_sh, g_rep), (d_sh, d_rep), (m_sh, m_rep), (v_sh, v_rep)):
        d = _unpack_shard(sh)
        d.update(_unpack_rep(rp, wts))
        outs.append(d)
    result = [loss, dx.reshape(bsz, seq, D_MODEL)]
    for d in outs:
        result.extend(d[n] for n in WEIGHTS)
    return tuple(result)
```

```python
import jax
import jax.numpy as jnp
from jax import lax
from jax.experimental import pallas as pl
from jax.experimental.pallas import tpu as pltpu

F32 = jnp.float32
BF16 = jnp.bfloat16
WIRE = jnp.bfloat16

D_MODEL = 1024
D_RNN = 1280
RNN_BLOCKS = 10
RNN_BW = 128
CONV_WIDTH = 4
LRU_C = 8.0
N_HEADS = 8
QK_NOPE = 128
QK_ROPE = 64
V_DIM = 128
KV_RANK = 256
Q_RANK = 384
ROPE_THETA = 10000.0
EPS = 1e-6
ATTN_SCALE = (QK_NOPE + QK_ROPE) ** -0.5
HEAD_PAD = 256
LANES = 128

ADAM_LR = 0.001
ADAM_B1 = 0.9
ADAM_B2 = 0.999
ADAM_EPS = 1e-08
ADAM_WD = 0.01
ADAM_STEP = 10

N_DEV = 8
MESH_AXES = ("x", "y", "c")
VMEM_LIMIT_BYTES = 56 * 2**20
PACK_W = 1024

_SHARD_PIECES = (("w_in_a", 320), ("w_out_a", 160), ("w_dkv", 40), ("w_uk", 32), ("w_uv", 32),
                 ("w_in_b", 176), ("w_uq", 96), ("w_out_b", 128), ("pad", 8), ("small", 8))
_SHARD_OFF = {}
_r = 0
for _n, _k in _SHARD_PIECES:
    _SHARD_OFF[_n] = (_r, _r + _k)
    _r += _k
SHARD_ROWS = _r
MATRIX_ROWS = _SHARD_OFF["small"][0]
_SMALL = (("norm_a", 128), ("conv_w", 640), ("conv_b", 160), ("b_rg", 160), ("b_ig", 160), ("lru_lambda", 160))
_REP = (("w_rg", 163840), ("w_ig", 163840), ("norm_kv", 1024), ("kv_norm", 256), ("norm_b", 1024),
        ("q_norm", 384), ("final_norm", 1024))
REP_ROWS = 384
REP_SLICE = REP_ROWS // N_DEV
GRAD_ROWS = SHARD_ROWS + REP_SLICE + 8
GRAD_BLOCK = 352

WEIGHTS = ("norm_a", "w_in_a", "conv_w", "conv_b", "w_rg", "b_rg", "w_ig", "b_ig", "lru_lambda", "w_out_a",
           "norm_kv", "w_dkv", "kv_norm", "w_uk", "w_uv", "norm_b", "w_in_b", "q_norm", "w_uq", "w_out_b",
           "final_norm")


def _params(sem=None):
    return pltpu.CompilerParams(dimension_semantics=sem, vmem_limit_bytes=VMEM_LIMIT_BYTES)


def _sigmoid(z):
    return 1.0 / (1.0 + jnp.exp(-z))


def _col_block(n):
    return n if n <= 1408 else n // 2


def _matmul(a, b, *, name, nt=False, out_dtype=F32, residual=None, bm=512):
    m, k = a.shape
    n = b.shape[0] if nt else b.shape[1]
    bm = min(bm, m)
    bn = _col_block(n)
    dims = (((1,), (1,)), ((), ())) if nt else (((1,), (0,)), ((), ()))
    has_res = residual is not None

    def body(*refs):
        a_ref, b_ref, o_ref = refs[0], refs[1], refs[-1]
        acc = lax.dot_general(a_ref[...].astype(BF16), b_ref[...].astype(BF16), dims, preferred_element_type=F32)
        if has_res:
            acc = acc + refs[2][...]
        o_ref[...] = acc.astype(out_dtype)

    in_specs = [pl.BlockSpec((bm, k), lambda i, j: (i, 0)),
                pl.BlockSpec((bn, k), lambda i, j: (j, 0)) if nt else pl.BlockSpec((k, bn), lambda i, j: (0, j))]
    args = [a, b]
    if has_res:
        in_specs.append(pl.BlockSpec((bm, bn), lambda i, j: (i, j)))
        args.append(residual)
    return pl.pallas_call(
        body, grid=(m // bm, n // bn), in_specs=in_specs, out_specs=pl.BlockSpec((bm, bn), lambda i, j: (i, j)),
        out_shape=jax.ShapeDtypeStruct((m, n), out_dtype), compiler_params=_params(("parallel", "parallel")),
        name=name)(*args)


def _matmul_tn(a, b, *, name, bt=512):
    t, m = a.shape
    n = b.shape[1]
    bt = min(bt, t)
    bm, bn = _col_block(m), _col_block(n)

    def body(a_ref, b_ref, o_ref):
        @pl.when(pl.program_id(2) == 0)
        def _():
            o_ref[...] = jnp.zeros_like(o_ref)

        o_ref[...] += lax.dot_general(a_ref[...].astype(BF16), b_ref[...].astype(BF16),
                                      (((0,), (0,)), ((), ())), preferred_element_type=F32)

    return pl.pallas_call(
        body, grid=(m // bm, n // bn, t // bt),
        in_specs=[pl.BlockSpec((bt, bm), lambda i, j, s: (s, i)), pl.BlockSpec((bt, bn), lambda i, j, s: (s, j))],
        out_specs=pl.BlockSpec((bm, bn), lambda i, j, s: (i, j)),
        out_shape=jax.ShapeDtypeStruct((m, n), F32),
        compiler_params=_params(("parallel", "parallel", "arbitrary")), name=name)(a, b)


def _rms_fwd(x, gains, *, name, bt=512):
    t, d = x.shape
    bt = min(bt, t)
    ng = len(gains)

    def body(x_ref, *refs):
        xv = x_ref[...]
        nrm = xv * lax.rsqrt(jnp.mean(xv * xv, axis=-1, keepdims=True) + EPS)
        for g_ref, o_ref in zip(refs[:ng], refs[ng:]):
            o_ref[...] = (nrm * g_ref[...]).astype(BF16)

    row = pl.BlockSpec((bt, d), lambda i: (i, 0))
    vec = pl.BlockSpec((1, d), lambda i: (0, 0))
    return pl.pallas_call(
        body, grid=(t // bt,), in_specs=[row] + [vec] * ng, out_specs=[row] * ng,
        out_shape=[jax.ShapeDtypeStruct((t, d), BF16)] * ng, compiler_params=_params(("parallel",)), name=name)(x, *gains)


def _rms_bwd(x, dhs, gains, *, name, residual=None, out_dtype=F32, bt=512):
    t, d = x.shape
    bt = min(bt, t)
    ng = len(gains)
    has_res = residual is not None

    def body(*refs):
        x_ref = refs[0]
        dh_refs = refs[1:1 + ng]
        g_refs = refs[1 + ng:1 + 2 * ng]
        pos = 1 + 2 * ng
        res_ref = refs[pos] if has_res else None
        pos += int(has_res)
        dx_ref = refs[pos]
        dg_refs = refs[pos + 1:]

        @pl.when(pl.program_id(0) == 0)
        def _():
            for dg in dg_refs:
                dg[...] = jnp.zeros_like(dg)

        xv = x_ref[...]
        r = lax.rsqrt(jnp.mean(xv * xv, axis=-1, keepdims=True) + EPS)
        nrm = xv * r
        dn = jnp.zeros_like(xv)
        for dh_ref, g_ref, dg in zip(dh_refs, g_refs, dg_refs):
            dh = dh_ref[...]
            dg[...] += jnp.sum(dh * nrm, axis=0, keepdims=True)
            dn = dn + dh * g_ref[...]
        dx = r * (dn - nrm * jnp.mean(dn * nrm, axis=-1, keepdims=True))
        if has_res:
            dx = dx + res_ref[...]
        dx_ref[...] = dx.astype(out_dtype)

    row = pl.BlockSpec((bt, d), lambda i: (i, 0))
    vec = pl.BlockSpec((1, d), lambda i: (0, 0))
    args = [x, *dhs, *gains] + ([residual] if has_res else [])
    outs = pl.pallas_call(
        body, grid=(t // bt,), in_specs=[row] * (1 + ng) + [vec] * ng + [row] * int(has_res),
        out_specs=[row] + [vec] * ng,
        out_shape=[jax.ShapeDtypeStruct((t, d), out_dtype)] + [jax.ShapeDtypeStruct((1, d), F32)] * ng,
        compiler_params=_params(("arbitrary",)), name=name)(*args)
    return outs[0], outs[1:]


def _swap_halves(v):
    ax = v.ndim - 1
    lane = lax.broadcasted_iota(jnp.int32, v.shape, ax)
    up = pltpu.roll(v, LANES - QK_ROPE // 2, axis=ax)
    down = pltpu.roll(v, QK_ROPE // 2, axis=ax)
    return jnp.where(lane < QK_ROPE // 2, up, jnp.where(lane < QK_ROPE, down, 0.0))


def _rope(v, cos, sin):
    return v * cos + _swap_halves(v) * sin


def _rope_t(d, cos, sin):
    return d * cos + _swap_halves(d * sin)


def _rope_tables(seq):
    pos = jnp.arange(seq, dtype=F32)
    inv = ROPE_THETA ** (-jnp.arange(0, QK_ROPE, 2, dtype=F32) / QK_ROPE)
    ang = pos[:, None] * inv[None, :]
    cos, sin = jnp.cos(ang), jnp.sin(ang)
    zero = jnp.zeros((seq, LANES - QK_ROPE), F32)
    return jnp.concatenate([cos, cos, zero], axis=1), jnp.concatenate([-sin, sin, zero], axis=1)


def _rope_q(q, cos, sin, *, seq, name):
    t = q.shape[0]
    bt = min(512, seq)
    per_seq = seq // bt

    def body(q_ref, cos_ref, sin_ref, o_ref):
        o_ref[:, :QK_NOPE] = q_ref[:, :QK_NOPE].astype(BF16)
        o_ref[:, QK_NOPE:] = _rope(q_ref[:, QK_NOPE:], cos_ref[...], sin_ref[...]).astype(BF16)

    blk = pl.BlockSpec((bt, HEAD_PAD), lambda i, h: (i, h))
    tab = pl.BlockSpec((bt, LANES), lambda i, h: (i % per_seq, 0))
    return pl.pallas_call(
        body, grid=(t // bt, N_HEADS), in_specs=[blk, tab, tab], out_specs=blk,
        out_shape=jax.ShapeDtypeStruct(q.shape, BF16), compiler_params=_params(("parallel", "parallel")), name=name)(
            q, cos, sin)


def _rope_k(k, cos, sin, *, seq, name):
    t = k.shape[0]
    bt = min(512, seq)
    per_seq = seq // bt

    def body(k_ref, cos_ref, sin_ref, o_ref):
        o_ref[...] = _rope(k_ref[...], cos_ref[...], sin_ref[...]).astype(BF16)

    blk = pl.BlockSpec((bt, LANES), lambda i: (i, 0))
    tab = pl.BlockSpec((bt, LANES), lambda i: (i % per_seq, 0))
    return pl.pallas_call(
        body, grid=(t // bt,), in_specs=[blk, tab, tab], out_specs=blk,
        out_shape=jax.ShapeDtypeStruct(k.shape, BF16), compiler_params=_params(("parallel",)), name=name)(k, cos, sin)


def _softplus(z):
    return jnp.maximum(z, 0.0) + jnp.log1p(jnp.exp(-jnp.abs(z)))


def _neg_expm1(z):
    series = -z * (1.0 + z * (1.0 / 2) * (1.0 + z * (1.0 / 3) * (1.0 + z * (1.0 / 4) * (1.0 + z * (1.0 / 5) * (
        1.0 + z * (1.0 / 6))))))
    return jnp.where(z > -0.1, series, 1.0 - jnp.exp(z))


def _gates(xb, wrg, wig, brg, big, sp):
    xbb = xb.astype(BF16)
    r = _sigmoid(jnp.dot(xbb, wrg, preferred_element_type=F32) + brg)
    i = _sigmoid(jnp.dot(xbb, wig, preferred_element_type=F32) + big)
    la = (-LRU_C) * r * sp
    a = jnp.exp(la)
    mult = jnp.sqrt(_neg_expm1(2.0 * la))
    return r, i, a, mult


def _conv(xpad_ref, cw_ref, seq):
    acc = cw_ref[0:1, :] * xpad_ref[pl.ds(8 - (CONV_WIDTH - 1), seq), :]
    for k in range(1, CONV_WIDTH):
        acc = acc + cw_ref[k:k + 1, :] * xpad_ref[pl.ds(8 - (CONV_WIDTH - 1) + k, seq), :]
    return acc


def _seq_spec(seq):
    return pl.BlockSpec((None, seq, RNN_BW), lambda n, b: (b, 0, n))


def _chan_spec(rows):
    return pl.BlockSpec((rows, RNN_BW), lambda n, b: (0, n))


_GATE_W_SPEC = pl.BlockSpec((None, RNN_BW, RNN_BW), lambda n, b: (n, 0, 0))


def _lru_fwd(xp, ga, cw, vecs, wrg, wig, *, name):
    bsz, seq, _ = xp.shape
    groups = seq // 8

    def body(xp_ref, ga_ref, cw_ref, vec_ref, wrg_ref, wig_ref, xb_ref, hs_ref, y_ref, xpad, a_s, b_s):
        xpad[0:8, :] = jnp.zeros((8, RNN_BW), F32)
        xpad[pl.ds(8, seq), :] = xp_ref[...]
        xb = _conv(xpad, cw_ref, seq) + vec_ref[0:1, :]
        xb_ref[...] = xb
        sp = _softplus(-vec_ref[3:4, :])
        _, i, a, mult = _gates(xb, wrg_ref[...], wig_ref[...], vec_ref[1:2, :], vec_ref[2:3, :], sp)
        a_s[...] = a
        b_s[...] = mult * (i * xb)
        row = lax.broadcasted_iota(jnp.int32, (8, RNN_BW), 0)

        def group(g, h):
            r0 = pl.multiple_of(g * 8, 8)
            av = a_s[pl.ds(r0, 8), :]
            bv = b_s[pl.ds(r0, 8), :]
            for k in (1, 2, 4):
                m = row >= k
                bv = jnp.where(m, av * pltpu.roll(bv, k, axis=0) + bv, bv)
                av = jnp.where(m, av * pltpu.roll(av, k, axis=0), av)
            rows = av * h + bv
            hs_ref[pl.ds(r0, 8), :] = rows
            return rows[7:8, :]

        lax.fori_loop(0, groups, group, jnp.zeros((1, RNN_BW), F32))
        gav = ga_ref[...]
        y_ref[...] = (hs_ref[...] * (gav * _sigmoid(gav))).astype(BF16)

    sq = _seq_spec(seq)
    shape = (bsz, seq, D_RNN)
    return pl.pallas_call(
        body, grid=(RNN_BLOCKS, bsz), in_specs=[sq, sq, _chan_spec(8), _chan_spec(8), _GATE_W_SPEC, _GATE_W_SPEC],
        out_specs=[sq, sq, sq],
        out_shape=[jax.ShapeDtypeStruct(shape, F32), jax.ShapeDtypeStruct(shape, F32), jax.ShapeDtypeStruct(shape, BF16)],
        scratch_shapes=[pltpu.VMEM((seq + 8, RNN_BW), F32), pltpu.VMEM((seq, RNN_BW), F32), pltpu.VMEM((seq, RNN_BW), F32)],
        compiler_params=_params(("parallel", "parallel")), name=name)(xp, ga, cw, vecs, wrg, wig)


def _lru_bwd(dy, xp, xb, hs, ga, cw, vecs, wrg, wig, *, name):
    bsz, seq, _ = xp.shape
    groups = seq // 8

    def body(dy_ref, xp_ref, xb_ref, hs_ref, ga_ref, cw_ref, vec_ref, wrg_ref, wig_ref,
             dxp_ref, dga_ref, dwrg_ref, dwig_ref, dvec_ref, pad, a_s, d_s, lam_s):
        @pl.when(pl.program_id(1) == 0)
        def _():
            dwrg_ref[...] = jnp.zeros_like(dwrg_ref)
            dwig_ref[...] = jnp.zeros_like(dwig_ref)
            dvec_ref[...] = jnp.zeros_like(dvec_ref)

        xb = xb_ref[...]
        hs = hs_ref[...]
        gav = ga_ref[...]
        dy = dy_ref[...]
        sp = _softplus(-vec_ref[3:4, :])
        wrg = wrg_ref[...]
        wig = wig_ref[...]
        r, i, a, mult = _gates(xb, wrg, wig, vec_ref[1:2, :], vec_ref[2:3, :], sp)
        sg = _sigmoid(gav)
        dga_ref[...] = (dy * hs * (sg * (1.0 + gav * (1.0 - sg)))).astype(BF16)
        d_s[...] = dy * (gav * sg)

        pad[pl.ds(0, seq), :] = a
        pad[pl.ds(seq, 8), :] = jnp.zeros((8, RNN_BW), F32)
        a_s[...] = pad[pl.ds(1, seq), :]
        row = lax.broadcasted_iota(jnp.int32, (8, RNN_BW), 0)

        def group(g, nxt):
            r0 = pl.multiple_of((groups - 1 - g) * 8, 8)
            cv = a_s[pl.ds(r0, 8), :]
            bv = d_s[pl.ds(r0, 8), :]
            for k in (1, 2, 4):
                m = row < 8 - k
                bv = jnp.where(m, cv * pltpu.roll(bv, 8 - k, axis=0) + bv, bv)
                cv = jnp.where(m, cv * pltpu.roll(cv, 8 - k, axis=0), cv)
            rows = cv * nxt + bv
            lam_s[pl.ds(r0, 8), :] = rows
            return rows[0:1, :]

        lax.fori_loop(0, groups, group, jnp.zeros((1, RNN_BW), F32))
        dh = lam_s[...]

        pad[0:8, :] = jnp.zeros((8, RNN_BW), F32)
        pad[pl.ds(8, seq), :] = hs
        da = dh * pad[pl.ds(7, seq), :]
        ixb = i * xb
        dixb = dh * mult
        dla = da * a - (dh * ixb) * (a * a) / mult
        drp = (dla * ((-LRU_C) * sp)) * r * (1.0 - r)
        dip = (dixb * xb) * i * (1.0 - i)
        dvec_ref[0:1, :] += jnp.sum(drp, axis=0, keepdims=True)
        dvec_ref[1:2, :] += jnp.sum(dip, axis=0, keepdims=True)
        dvec_ref[2:3, :] += jnp.sum(dla * ((-LRU_C) * r), axis=0, keepdims=True)
        drpb = drp.astype(BF16)
        dipb = dip.astype(BF16)
        xbb = xb.astype(BF16)
        nt = (((1,), (1,)), ((), ()))
        tn = (((0,), (0,)), ((), ()))
        dxb = (dixb * i
               + lax.dot_general(drpb, wrg, nt, preferred_element_type=F32)
               + lax.dot_general(dipb, wig, nt, preferred_element_type=F32))
        dwrg_ref[...] += lax.dot_general(xbb, drpb, tn, preferred_element_type=F32)
        dwig_ref[...] += lax.dot_general(xbb, dipb, tn, preferred_element_type=F32)
        dvec_ref[3:4, :] += jnp.sum(dxb, axis=0, keepdims=True)

        pad[pl.ds(0, seq), :] = dxb
        pad[pl.ds(seq, 8), :] = jnp.zeros((8, RNN_BW), F32)
        dxp = cw_ref[0:1, :] * pad[pl.ds(CONV_WIDTH - 1, seq), :]
        for k in range(1, CONV_WIDTH):
            dxp = dxp + cw_ref[k:k + 1, :] * pad[pl.ds(CONV_WIDTH - 1 - k, seq), :]
        dxp_ref[...] = dxp.astype(BF16)
        pad[0:8, :] = jnp.zeros((8, RNN_BW), F32)
        pad[pl.ds(8, seq), :] = xp_ref[...]
        for k in range(CONV_WIDTH):
            dvec_ref[4 + k:5 + k, :] += jnp.sum(dxb * pad[pl.ds(8 - (CONV_WIDTH - 1) + k, seq), :], axis=0, keepdims=True)

    sq = _seq_spec(seq)
    shape = (bsz, seq, D_RNN)
    gshape = (RNN_BLOCKS, RNN_BW, RNN_BW)
    return pl.pallas_call(
        body, grid=(RNN_BLOCKS, bsz),
        in_specs=[sq, sq, sq, sq, sq, _chan_spec(8), _chan_spec(8), _GATE_W_SPEC, _GATE_W_SPEC],
        out_specs=[sq, sq, _GATE_W_SPEC, _GATE_W_SPEC, _chan_spec(8)],
        out_shape=[jax.ShapeDtypeStruct(shape, BF16), jax.ShapeDtypeStruct(shape, BF16),
                   jax.ShapeDtypeStruct(gshape, F32), jax.ShapeDtypeStruct(gshape, F32),
                   jax.ShapeDtypeStruct((8, D_RNN), F32)],
        scratch_shapes=[pltpu.VMEM((seq + 8, RNN_BW), F32), pltpu.VMEM((seq, RNN_BW), F32),
                        pltpu.VMEM((seq, RNN_BW), F32), pltpu.VMEM((seq, RNN_BW), F32)],
        compiler_params=_params(("parallel", "arbitrary")), name=name)(dy, xp, xb, hs, ga, cw, vecs, wrg, wig)


def _attn_block(seq):
    return min(256, seq)


def _causal_mask(qi, kj, blk):
    qpos = qi * blk + lax.broadcasted_iota(jnp.int32, (blk, blk), 0)
    kpos = kj * blk + lax.broadcasted_iota(jnp.int32, (blk, blk), 1)
    return kpos <= qpos


def _attn_fwd(q, kn, kr, v, *, bsz, seq, name):
    t = bsz * seq
    blk = _attn_block(seq)
    nq = seq // blk
    nt = (((1,), (1,)), ((), ()))

    def body(q_ref, kn_ref, kr_ref, v_ref, o_ref, lse_ref):
        qi = pl.program_id(2)
        qv = q_ref[...]

        def step(j, carry):
            m_i, l_i, acc = carry
            r0 = pl.multiple_of(j * blk, blk)
            kv = jnp.concatenate([kn_ref[pl.ds(r0, blk), :], kr_ref[pl.ds(r0, blk), :]], axis=1)
            s = lax.dot_general(qv, kv, nt, preferred_element_type=F32) * ATTN_SCALE
            s = jnp.where(_causal_mask(qi, j, blk), s, -jnp.inf)
            m_new = jnp.maximum(m_i, jnp.max(s, axis=-1, keepdims=True))
            p = jnp.exp(s - m_new)
            alpha = jnp.exp(m_i - m_new)
            l_new = alpha * l_i + jnp.sum(p, axis=-1, keepdims=True)
            acc = alpha * acc + jnp.dot(p.astype(BF16), v_ref[pl.ds(r0, blk), :], preferred_element_type=F32)
            return m_new, l_new, acc

        init = (jnp.full((blk, 1), -jnp.inf, F32), jnp.zeros((blk, 1), F32), jnp.zeros((blk, V_DIM), F32))
        m_i, l_i, acc = lax.fori_loop(0, qi + 1, step, init)
        o_ref[...] = acc / l_i
        lse_ref[...] = m_i + jnp.log(l_i)

    return pl.pallas_call(
        body, grid=(bsz, N_HEADS, nq),
        in_specs=[pl.BlockSpec((blk, HEAD_PAD), lambda b, h, i: (b * nq + i, h)),
                  pl.BlockSpec((seq, QK_NOPE), lambda b, h, i: (b, h)),
                  pl.BlockSpec((seq, LANES), lambda b, h, i: (b, 0)),
                  pl.BlockSpec((seq, V_DIM), lambda b, h, i: (b, h))],
        out_specs=[pl.BlockSpec((blk, V_DIM), lambda b, h, i: (b * nq + i, h)),
                   pl.BlockSpec((None, blk, 1), lambda b, h, i: (h, b * nq + i, 0))],
        out_shape=[jax.ShapeDtypeStruct((t, N_HEADS * V_DIM), F32), jax.ShapeDtypeStruct((N_HEADS, t, 1), F32)],
        compiler_params=_params(("parallel", "parallel", "parallel")), name=name)(q, kn, kr, v)


def _attn_bwd(q, kn, kr, v, o, lse, do, cos, sin, *, bsz, seq, name):
    t = bsz * seq
    blk = _attn_block(seq)
    nq = seq // blk
    nt = (((1,), (1,)), ((), ()))
    tn = (((0,), (0,)), ((), ()))

    def body(q_ref, kn_ref, kr_ref, v_ref, o_ref, lse_ref, do_ref, cos_ref, sin_ref,
             dq_ref, dkn_ref, dkr_ref, dv_ref, dq_acc, dk_acc, dv_acc):
        dq_acc[...] = jnp.zeros_like(dq_acc)
        dk_acc[...] = jnp.zeros_like(dk_acc)
        dv_acc[...] = jnp.zeros_like(dv_acc)

        def q_block(i, _):
            q0 = pl.multiple_of(i * blk, blk)
            qv = q_ref[pl.ds(q0, blk), :]
            dov = do_ref[pl.ds(q0, blk), :]
            lse_i = lse_ref[pl.ds(q0, blk), :]
            delta = jnp.sum(dov.astype(F32) * o_ref[pl.ds(q0, blk), :], axis=-1, keepdims=True)

            def k_block(j, _):
                k0 = pl.multiple_of(j * blk, blk)
                kv = jnp.concatenate([kn_ref[pl.ds(k0, blk), :], kr_ref[pl.ds(k0, blk), :]], axis=1)
                vv = v_ref[pl.ds(k0, blk), :]
                s = lax.dot_general(qv, kv, nt, preferred_element_type=F32) * ATTN_SCALE
                p = jnp.where(_causal_mask(i, j, blk), jnp.exp(s - lse_i), 0.0)
                pb = p.astype(BF16)
                dv_acc[pl.ds(k0, blk), :] += lax.dot_general(pb, dov, tn, preferred_element_type=F32)
                dp = lax.dot_general(dov, vv, nt, preferred_element_type=F32)
                ds = (p * (dp - delta) * ATTN_SCALE).astype(BF16)
                dq_acc[pl.ds(q0, blk), :] += jnp.dot(ds, kv, preferred_element_type=F32)
                dk_acc[pl.ds(k0, blk), :] += lax.dot_general(ds, qv, tn, preferred_element_type=F32)
                return 0

            lax.fori_loop(0, i + 1, k_block, 0)
            return 0

        lax.fori_loop(0, nq, q_block, 0)
        cosv = cos_ref[...]
        sinv = sin_ref[...]
        dq_ref[:, :QK_NOPE] = dq_acc[:, :QK_NOPE].astype(BF16)
        dq_ref[:, QK_NOPE:] = _rope_t(dq_acc[:, QK_NOPE:], cosv, sinv).astype(BF16)
        dkn_ref[...] = dk_acc[:, :QK_NOPE].astype(BF16)
        dv_ref[...] = dv_acc[...].astype(BF16)

        @pl.when(pl.program_id(1) == 0)
        def _():
            dkr_ref[...] = jnp.zeros_like(dkr_ref)

        dkr_ref[...] += _rope_t(dk_acc[:, QK_NOPE:], cosv, sinv)

    head = pl.BlockSpec((seq, V_DIM), lambda b, h: (b, h))
    shared = pl.BlockSpec((seq, LANES), lambda b, h: (b, 0))
    table = pl.BlockSpec((seq, LANES), lambda b, h: (0, 0))
    qspec = pl.BlockSpec((seq, HEAD_PAD), lambda b, h: (b, h))
    return pl.pallas_call(
        body, grid=(bsz, N_HEADS),
        in_specs=[qspec, head, shared, head, head, pl.BlockSpec((None, seq, 1), lambda b, h: (h, b, 0)), head, table, table],
        out_specs=[qspec, head, shared, head],
        out_shape=[jax.ShapeDtypeStruct((t, N_HEADS * HEAD_PAD), BF16), jax.ShapeDtypeStruct((t, N_HEADS * QK_NOPE), BF16),
                   jax.ShapeDtypeStruct((t, LANES), F32), jax.ShapeDtypeStruct((t, N_HEADS * V_DIM), BF16)],
        scratch_shapes=[pltpu.VMEM((seq, HEAD_PAD), F32), pltpu.VMEM((seq, HEAD_PAD), F32), pltpu.VMEM((seq, V_DIM), F32)],
        compiler_params=_params(("parallel", "arbitrary")), name=name)(q, kn, kr, v, o, lse, do, cos, sin)


def _head_and_loss(o, g2, x1, target, w_out, g_final, *, name, bt=256):
    t, d = x1.shape
    bt = min(bt, t)
    nt = (((1,), (1,)), ((), ()))

    def body(o_ref, g2_ref, x1_ref, tgt_ref, w_ref, gf_ref, loss_ref, dx2_ref, y2_ref, do_ref, dg2_ref, dgf_ref):
        @pl.when(pl.program_id(0) == 0)
        def _():
            loss_ref[...] = jnp.zeros_like(loss_ref)
            dgf_ref[...] = jnp.zeros_like(dgf_ref)

        ov = o_ref[...]
        gv = g2_ref[...]
        sg = _sigmoid(gv)
        silu = gv * sg
        y2 = (ov * silu).astype(BF16)
        y2_ref[...] = y2
        w = w_ref[...]
        x2 = x1_ref[...] + jnp.dot(y2, w, preferred_element_type=F32)
        r = lax.rsqrt(jnp.mean(x2 * x2, axis=-1, keepdims=True) + EPS)
        nrm = x2 * r
        gf = gf_ref[...]
        err = nrm * gf - tgt_ref[...]
        loss_ref[...] += 0.5 * jnp.sum(jnp.mean(err * err, axis=-1, keepdims=True))
        dyf = err * (1.0 / d)
        dgf_ref[...] += jnp.sum(dyf * nrm, axis=0, keepdims=True)
        dn = dyf * gf
        dx2 = r * (dn - nrm * jnp.mean(dn * nrm, axis=-1, keepdims=True))
        dx2_ref[...] = dx2
        dy2 = lax.dot_general(dx2.astype(BF16), w, nt, preferred_element_type=F32)
        do_ref[...] = (dy2 * silu).astype(BF16)
        dg2_ref[...] = (dy2 * ov * (sg * (1.0 + gv * (1.0 - sg)))).astype(BF16)

    row = pl.BlockSpec((bt, d), lambda i: (i, 0))
    vec = pl.BlockSpec((1, d), lambda i: (0, 0))
    return pl.pallas_call(
        body, grid=(t // bt,),
        in_specs=[row, row, row, row, pl.BlockSpec((d, d), lambda i: (0, 0)), vec],
        out_specs=[pl.BlockSpec((8, LANES), lambda i: (0, 0)), row, row, row, row, vec],
        out_shape=[jax.ShapeDtypeStruct((8, LANES), F32), jax.ShapeDtypeStruct((t, d), F32),
                   jax.ShapeDtypeStruct((t, d), BF16), jax.ShapeDtypeStruct((t, d), BF16),
                   jax.ShapeDtypeStruct((t, d), BF16), jax.ShapeDtypeStruct((1, d), F32)],
        compiler_params=_params(("arbitrary",)), name=name)(o, g2, x1, target, w_out, g_final)


def _sum_parts(parts, *, name, br=GRAD_BLOCK):
    npart, rows, w = parts.shape

    def body(p_ref, o_ref):
        acc = p_ref[0].astype(F32)
        for j in range(1, npart):
            acc = acc + p_ref[j].astype(F32)
        o_ref[...] = acc

    return pl.pallas_call(
        body, grid=(rows // br,), in_specs=[pl.BlockSpec((npart, br, w), lambda i: (0, i, 0))],
        out_specs=pl.BlockSpec((br, w), lambda i: (i, 0)), out_shape=jax.ShapeDtypeStruct((rows, w), F32),
        compiler_params=_params(("parallel",)), name=name)(parts)


def _chip_partial(parts, recv, *, name, br=GRAD_BLOCK):
    _, rows, w = parts.shape
    core = lax.axis_index("c").astype(jnp.int32).reshape(1)

    def body(c_ref, p_ref, r_ref, o_ref):
        o_ref[...] = (p_ref[...] + r_ref[...]).astype(BF16)

    grid_spec = pltpu.PrefetchScalarGridSpec(
        num_scalar_prefetch=1, grid=(4, rows // br),
        in_specs=[pl.BlockSpec((None, br, w), lambda k, i, c_ref: (2 * k + c_ref[0], i, 0)),
                  pl.BlockSpec((None, br, w), lambda k, i, c_ref: (k, i, 0))],
        out_specs=pl.BlockSpec((None, br, w), lambda k, i, c_ref: (k, i, 0)))
    return pl.pallas_call(
        body, grid_spec=grid_spec, out_shape=jax.ShapeDtypeStruct((4, rows, w), BF16),
        compiler_params=_params(("parallel", "parallel")), name=name)(core, parts, recv)


def _as_block(a):
    if a.ndim == 1:
        return a.reshape(1, -1)
    if a.ndim > 2 and a.shape[0] == 1:
        return a.reshape(a.shape[1:])
    return a


def _adamw(g, w, m, v, *, name):
    shape = w.shape
    g, w, m, v = (_as_block(a) for a in (g, w, m, v))

    def body(g_ref, w_ref, m_ref, v_ref, d_ref, nm_ref, nv_ref):
        gv = g_ref[...]
        nm = ADAM_B1 * m_ref[...] + (1.0 - ADAM_B1) * gv
        nv = ADAM_B2 * v_ref[...] + (1.0 - ADAM_B2) * (gv * gv)
        nm_ref[...] = nm
        nv_ref[...] = nv
        m_hat = nm / (1.0 - ADAM_B1 ** ADAM_STEP)
        v_hat = nv / (1.0 - ADAM_B2 ** ADAM_STEP)
        d_ref[...] = (-ADAM_LR) * (m_hat / (jnp.sqrt(v_hat) + ADAM_EPS) + ADAM_WD * w_ref[...])

    whole = pl.BlockSpec(memory_space=pltpu.VMEM)
    outs = pl.pallas_call(
        body, in_specs=[whole] * 4, out_specs=[whole] * 3, out_shape=[jax.ShapeDtypeStruct(w.shape, F32)] * 3,
        compiler_params=_params(), name=name)(g, w, m, v)
    return [o.reshape(shape) for o in outs]


def _mesh_pos():
    return lax.axis_index("x"), lax.axis_index("y"), lax.axis_index("c")


_ANY = pl.BlockSpec(memory_space=pl.ANY)


def _all_gather(block, *, name):
    m, n = block.shape

    def body(x_ref, out_ref, send_sems, recv_sems, local_sem):
        x, y, c = _mesh_pos()
        me, sibling = (x, y, c), (x, y, 1 - c)
        chips = [(1 - x, y), (x, 1 - y), (1 - x, 1 - y)]

        def slot(px, py, pc):
            return out_ref.at[4 * px + 2 * py + pc]

        def copy(k, blk, to, src=None):
            return pltpu.make_async_remote_copy(
                src_ref=slot(*blk) if src is None else src, dst_ref=slot(*blk),
                send_sem=send_sems.at[k], recv_sem=recv_sems.at[k], device_id=to, device_id_type=pl.DeviceIdType.MESH)

        mine = pltpu.make_async_copy(x_ref, slot(*me), local_sem)
        mine.start()
        first = [copy(0, me, sibling, src=x_ref)]
        first += [copy(1 + j, me, (*chip, c), src=x_ref) for j, chip in enumerate(chips)]
        for cp in first:
            cp.start()
        passed = [copy(4 + j, (*chip, c), sibling) for j, chip in enumerate(chips)]
        for j, chip in enumerate(chips):
            copy(1 + j, (*chip, c), me).wait_recv()
            passed[j].start()
        copy(0, sibling, me).wait_recv()
        for j, chip in enumerate(chips):
            copy(4 + j, (*chip, 1 - c), me).wait_recv()
        for cp in first + passed:
            cp.wait_send()
        mine.wait()

    return pl.pallas_call(
        body, out_shape=jax.ShapeDtypeStruct((N_DEV, m, n), block.dtype), in_specs=[_ANY], out_specs=_ANY,
        scratch_shapes=[pltpu.SemaphoreType.DMA((7,)), pltpu.SemaphoreType.DMA((7,)), pltpu.SemaphoreType.DMA(())],
        name=name)(block)


def _exchange_d2d(parts, *, name):
    _, rows, w = parts.shape

    def body(p_ref, land_ref, send_sems, recv_sems):
        x, y, c = _mesh_pos()
        sends = []
        for k in range(4):
            cp = pltpu.make_async_remote_copy(
                src_ref=p_ref.at[2 * k + (1 - c)], dst_ref=land_ref.at[k], send_sem=send_sems.at[k],
                recv_sem=recv_sems.at[k], device_id=(x, y, 1 - c), device_id_type=pl.DeviceIdType.MESH)
            cp.start()
            sends.append(cp)
        for cp in sends:
            cp.wait_recv()
        for cp in sends:
            cp.wait_send()

    return pl.pallas_call(
        body, out_shape=jax.ShapeDtypeStruct((4, rows, w), parts.dtype), in_specs=[_ANY], out_specs=_ANY,
        scratch_shapes=[pltpu.SemaphoreType.DMA((4,)), pltpu.SemaphoreType.DMA((4,))], name=name)(parts)


def _exchange_ici(parts, *, name):
    def body(p_ref, land_ref, send_sems, recv_sems, local_sem):
        x, y, c = _mesh_pos()
        mine = pltpu.make_async_copy(p_ref.at[2 * x + y], land_ref.at[3], local_sem)
        mine.start()
        sends = []
        for k, (px, py) in enumerate([(1 - x, y), (x, 1 - y), (1 - x, 1 - y)]):
            cp = pltpu.make_async_remote_copy(
                src_ref=p_ref.at[2 * px + py], dst_ref=land_ref.at[k], send_sem=send_sems.at[k],
                recv_sem=recv_sems.at[k], device_id=(px, py, c), device_id_type=pl.DeviceIdType.MESH)
            cp.start()
            sends.append(cp)
        for cp in sends:
            cp.wait_recv()
        for cp in sends:
            cp.wait_send()
        mine.wait()

    return pl.pallas_call(
        body, out_shape=jax.ShapeDtypeStruct(parts.shape, parts.dtype), in_specs=[_ANY], out_specs=_ANY,
        scratch_shapes=[pltpu.SemaphoreType.DMA((3,)), pltpu.SemaphoreType.DMA((3,)), pltpu.SemaphoreType.DMA(())],
        name=name)(parts)


def _rows(a):
    return a.reshape(-1, PACK_W)


def _pad_to(a, n):
    return jnp.pad(a, (0, n - a.shape[0]))


def _pack_shard(d):
    small = jnp.concatenate([d[n].reshape(-1) for n, _ in _SMALL])
    w_uq = jnp.pad(d["w_uq"][0], ((0, 0), (0, 0), (0, HEAD_PAD - QK_NOPE - QK_ROPE)))
    pieces = {"w_in_a": d["w_in_a"], "w_out_a": d["w_out_a"], "w_dkv": d["w_dkv"], "w_uk": d["w_uk"], "w_uv": d["w_uv"],
              "w_in_b": d["w_in_b"], "w_uq": w_uq, "w_out_b": d["w_out_b"], "pad": jnp.zeros((8, PACK_W), F32),
              "small": _pad_to(small, 8 * PACK_W)}
    return jnp.concatenate([_rows(pieces[n]) for n, _ in _SHARD_PIECES], axis=0)


def _unpack_shard(p):
    out = {}
    piece = {n: p[lo:hi] for n, (lo, hi) in _SHARD_OFF.items()}
    out["w_in_a"] = piece["w_in_a"].reshape(1, D_MODEL, 2 * D_RNN // N_DEV)
    out["w_out_a"] = piece["w_out_a"].reshape(1, D_RNN // N_DEV, D_MODEL)
    out["w_dkv"] = piece["w_dkv"].reshape(D_MODEL // N_DEV, KV_RANK + QK_ROPE)
    out["w_uk"] = piece["w_uk"].reshape(KV_RANK // N_DEV, N_HEADS, QK_NOPE)
    out["w_uv"] = piece["w_uv"].reshape(KV_RANK // N_DEV, N_HEADS, V_DIM)
    out["w_in_b"] = piece["w_in_b"].reshape(1, D_MODEL, (Q_RANK + N_HEADS * V_DIM) // N_DEV)
    out["w_uq"] = piece["w_uq"].reshape(1, Q_RANK // N_DEV, N_HEADS, HEAD_PAD)[..., :QK_NOPE + QK_ROPE]
    out["w_out_b"] = piece["w_out_b"].reshape(1, N_HEADS * V_DIM // N_DEV, D_MODEL)
    small = piece["small"].reshape(-1)
    off = 0
    shapes = {"norm_a": (1, D_MODEL // N_DEV), "conv_w": (1, CONV_WIDTH, D_RNN // N_DEV), "conv_b": (1, D_RNN // N_DEV),
              "b_rg": (1, D_RNN // N_DEV), "b_ig": (1, D_RNN // N_DEV), "lru_lambda": (1, D_RNN // N_DEV)}
    for n, k in _SMALL:
        out[n] = small[off:off + k].reshape(shapes[n])
        off += k
    return out


def _pack_rep(d):
    flat = jnp.concatenate([d[n].reshape(-1) for n, _ in _REP])
    return _rows(_pad_to(flat, REP_ROWS * PACK_W))


def _unpack_rep(p, like):
    flat = p.reshape(-1)
    out, off = {}, 0
    for n, k in _REP:
        out[n] = flat[off:off + k].reshape(like[n].shape)
        off += k
    return out


def _gathered_weights(wall):
    piece = {n: wall[:, lo:hi] for n, (lo, hi) in _SHARD_OFF.items() if n != "small"}
    w = {}
    w_in_a = piece["w_in_a"].reshape(N_DEV, D_MODEL, -1).transpose(1, 0, 2).reshape(D_MODEL, 2 * D_RNN)
    w["w_xp"], w["w_ga"] = w_in_a[:, :D_RNN], w_in_a[:, D_RNN:]
    w["w_out_a"] = piece["w_out_a"].reshape(D_RNN, D_MODEL)
    w_dkv = piece["w_dkv"].reshape(D_MODEL, KV_RANK + QK_ROPE)
    w["w_dkv_c"] = w_dkv[:, :KV_RANK]
    w["w_dkv_r"] = jnp.pad(w_dkv[:, KV_RANK:], ((0, 0), (0, LANES - QK_ROPE)))
    w["w_uk"] = piece["w_uk"].reshape(KV_RANK, N_HEADS * QK_NOPE)
    w["w_uv"] = piece["w_uv"].reshape(KV_RANK, N_HEADS * V_DIM)
    w_in_b = piece["w_in_b"].reshape(N_DEV, D_MODEL, -1).transpose(1, 0, 2).reshape(D_MODEL, Q_RANK + N_HEADS * V_DIM)
    w["w_cq"], w["w_g2"] = w_in_b[:, :Q_RANK], w_in_b[:, Q_RANK:]
    w["w_uq"] = piece["w_uq"].reshape(Q_RANK, N_HEADS * HEAD_PAD)
    w["w_out_b"] = piece["w_out_b"].reshape(N_HEADS * V_DIM, D_MODEL)
    small = lax.bitcast_convert_type(wall[:, MATRIX_ROWS:].reshape(N_DEV, 8 * PACK_W, 2), F32)
    off = dict(zip([n for n, _ in _SMALL], [0, 128, 768, 928, 1088, 1248]))
    w["norm_a"] = small[:, :128].reshape(1, D_MODEL)

    def by_channel(lo, rows):
        a = small[:, lo:lo + rows * (D_RNN // N_DEV)].reshape(N_DEV, rows, -1).transpose(1, 0, 2).reshape(rows, D_RNN)
        return jnp.pad(a, ((0, 8 - rows), (0, 0)))

    w["conv_taps"] = by_channel(off["conv_w"], CONV_WIDTH)
    w["lru_vecs"] = by_channel(off["conv_b"], 4)
    return w


def _pack_grads(g):
    def by_cols(a):
        r, n = a.shape
        return a.reshape(r, N_DEV, n // N_DEV).transpose(1, 0, 2).reshape(N_DEV, -1, PACK_W)

    def by_rows(a):
        return a.reshape(N_DEV, -1, PACK_W)

    small = jnp.concatenate([
        g["norm_a"].reshape(N_DEV, -1),
        g["conv_w"].reshape(CONV_WIDTH, N_DEV, -1).transpose(1, 0, 2).reshape(N_DEV, -1),
        g["conv_b"].reshape(N_DEV, -1), g["b_rg"].reshape(N_DEV, -1), g["b_ig"].reshape(N_DEV, -1),
        g["lru_lambda"].reshape(N_DEV, -1)], axis=1)
    small = jnp.pad(small, ((0, 0), (0, 8 * PACK_W - small.shape[1]))).reshape(N_DEV, 8, PACK_W)
    pieces = {"w_in_a": by_cols(g["w_in_a"]), "w_out_a": by_rows(g["w_out_a"]), "w_dkv": by_rows(g["w_dkv"]),
              "w_uk": by_rows(g["w_uk"]), "w_uv": by_rows(g["w_uv"]), "w_in_b": by_cols(g["w_in_b"]),
              "w_uq": by_rows(g["w_uq"]), "w_out_b": by_rows(g["w_out_b"]), "pad": jnp.zeros((N_DEV, 8, PACK_W), F32),
              "small": small}
    rep = _pack_rep(g).reshape(N_DEV, REP_SLICE, PACK_W)
    tail = jnp.zeros((N_DEV, GRAD_ROWS - SHARD_ROWS - REP_SLICE, PACK_W), F32)
    return jnp.concatenate([pieces[n] for n, _ in _SHARD_PIECES] + [rep, tail], axis=1)


def _step(x, target, w, rep, *, bsz, seq):
    t = bsz * seq
    cos, sin = _rope_tables(seq)
    g_a = w["norm_a"]
    g_kv = rep["norm_kv"].reshape(1, -1)
    g_kvn = rep["kv_norm"].reshape(1, -1)
    g_b = rep["norm_b"].reshape(1, -1)
    g_q = rep["q_norm"].reshape(1, -1)
    g_f = rep["final_norm"].reshape(1, -1)
    wrg = rep["w_rg"][0].astype(BF16)
    wig = rep["w_ig"][0].astype(BF16)
    cw8, vecs = w["conv_taps"], w["lru_vecs"]

    def seq3(a):
        return a.reshape(bsz, seq, a.shape[-1])

    def flat(a):
        return a.reshape(t, a.shape[-1])

    (h0,) = _rms_fwd(x, [g_a], name="norm_a_fwd")
    xp = _matmul(h0, w["w_xp"], name="in_a_x")
    ga = _matmul(h0, w["w_ga"], name="in_a_gate")
    xb, hs, y = _lru_fwd(seq3(xp), seq3(ga), cw8, vecs, wrg, wig, name="lru_fwd")
    x1 = _matmul(flat(y), w["w_out_a"], residual=x, name="out_a")
    hk, hq = _rms_fwd(x1, [g_kv, g_b], name="norm_kvb_fwd")
    ck = _matmul(hk, w["w_dkv_c"], name="dkv_c")
    krp = _matmul(hk, w["w_dkv_r"], name="dkv_r")
    cqp = _matmul(hq, w["w_cq"], name="in_b_q")
    g2 = _matmul(hq, w["w_g2"], name="in_b_gate")
    (ckv,) = _rms_fwd(ck, [g_kvn], name="kv_norm_fwd")
    (cq,) = _rms_fwd(cqp, [g_q], name="q_norm_fwd")
    kr = _rope_k(krp, cos, sin, seq=seq, name="rope_k")
    kn = _matmul(ckv, w["w_uk"], out_dtype=BF16, name="uk")
    v = _matmul(ckv, w["w_uv"], out_dtype=BF16, name="uv")
    q = _rope_q(_matmul(cq, w["w_uq"], name="uq"), cos, sin, seq=seq, name="rope_q")
    o, lse = _attn_fwd(q, kn, kr, v, bsz=bsz, seq=seq, name="attn_fwd")
    loss, dx2, y2, do, dg2, dgf = _head_and_loss(o, g2, x1, target, w["w_out_b"], g_f, name="head_loss")
    grads = {"final_norm": dgf, "w_out_b": _matmul_tn(y2, dx2, name="d_w_out_b")}
    dq, dkn, dkr, dv = _attn_bwd(q, kn, kr, v, o, lse, do, cos, sin, bsz=bsz, seq=seq, name="attn_bwd")
    grads["w_uq"] = _matmul_tn(cq, dq, name="d_w_uq")
    dcq = _matmul(dq, w["w_uq"], nt=True, name="d_cq")
    dcqp, (dgq,) = _rms_bwd(cqp, [dcq], [g_q], out_dtype=BF16, name="q_norm_bwd")
    grads["q_norm"] = dgq
    grads["w_in_b"] = jnp.concatenate([_matmul_tn(hq, dcqp, name="d_w_in_b_q"), _matmul_tn(hq, dg2, name="d_w_in_b_g")], axis=1)
    dhq = _matmul(dg2, w["w_g2"], nt=True, residual=_matmul(dcqp, w["w_cq"], nt=True, name="d_hq_q"), name="d_hq")
    grads["w_uk"] = _matmul_tn(ckv, dkn, name="d_w_uk")
    grads["w_uv"] = _matmul_tn(ckv, dv, name="d_w_uv")
    dckv = _matmul(dv, w["w_uv"], nt=True, residual=_matmul(dkn, w["w_uk"], nt=True, name="d_ckv_k"), name="d_ckv")
    dck, (dgkvn,) = _rms_bwd(ck, [dckv], [g_kvn], out_dtype=BF16, name="kv_norm_bwd")
    grads["kv_norm"] = dgkvn
    grads["w_dkv"] = jnp.concatenate([_matmul_tn(hk, dck, name="d_w_dkv_c"),
                                      _matmul_tn(hk, dkr, name="d_w_dkv_r")[:, :QK_ROPE]], axis=1)
    dhk = _matmul(dkr, w["w_dkv_r"], nt=True, residual=_matmul(dck, w["w_dkv_c"], nt=True, name="d_hk_c"), name="d_hk")
    dx1, (dgb, dgkv) = _rms_bwd(x1, [dhq, dhk], [g_b, g_kv], residual=dx2, name="norm_kvb_bwd")
    grads["norm_b"], grads["norm_kv"] = dgb, dgkv
    grads["w_out_a"] = _matmul_tn(flat(y), dx1, name="d_w_out_a")
    dy = _matmul(dx1, w["w_out_a"], nt=True, name="d_y")
    dxp, dga, dwrg, dwig, dvec = _lru_bwd(seq3(dy), seq3(xp), xb, hs, seq3(ga), cw8, vecs, wrg, wig, name="lru_bwd")
    dxp, dga = flat(dxp), flat(dga)
    grads["w_rg"], grads["w_ig"] = dwrg, dwig
    grads["b_rg"], grads["b_ig"], grads["conv_b"] = dvec[0], dvec[1], dvec[3]
    lam = vecs[3]
    grads["lru_lambda"] = dvec[2] * (-1.0 / (1.0 + jnp.exp(lam)))
    grads["conv_w"] = dvec[4:4 + CONV_WIDTH]
    grads["w_in_a"] = jnp.concatenate([_matmul_tn(h0, dxp, name="d_w_in_a_x"), _matmul_tn(h0, dga, name="d_w_in_a_g")], axis=1)
    dh0 = _matmul(dga, w["w_ga"], nt=True, residual=_matmul(dxp, w["w_xp"], nt=True, name="d_h0_x"), name="d_h0")
    dx, (dga_norm,) = _rms_bwd(x, [dh0], [g_a], residual=dx1, name="norm_a_bwd")
    grads["norm_a"] = dga_norm
    return loss[0, 0], dx, grads


def kernel(x, norm_a, w_in_a, conv_w, conv_b, w_rg, b_rg, w_ig, b_ig, lru_lambda, w_out_a, norm_kv, w_dkv, kv_norm, w_uk, w_uv, norm_b, w_in_b, q_norm, w_uq, w_out_b, final_norm, loss_target, m_norm_a, m_w_in_a, m_conv_w, m_conv_b, m_w_rg, m_b_rg, m_w_ig, m_b_ig, m_lru_lambda, m_w_out_a, m_norm_kv, m_w_dkv, m_kv_norm, m_w_uk, m_w_uv, m_norm_b, m_w_in_b, m_q_norm, m_w_uq, m_w_out_b, m_final_norm, v_norm_a, v_w_in_a, v_conv_w, v_conv_b, v_w_rg, v_b_rg, v_w_ig, v_b_ig, v_lru_lambda, v_w_out_a, v_norm_kv, v_w_dkv, v_kv_norm, v_w_uk, v_w_uv, v_norm_b, v_w_in_b, v_q_norm, v_w_uq, v_w_out_b, v_final_norm):
    given = dict(locals())
    wts = {n: given[n] for n in WEIGHTS}
    mom1 = {n: given["m_" + n] for n in WEIGHTS}
    mom2 = {n: given["v_" + n] for n in WEIGHTS}
    bsz, seq, _ = x.shape
    t = bsz * seq

    w_pack = _pack_shard(wts)
    vec_bits = lax.bitcast_convert_type(w_pack[MATRIX_ROWS:], WIRE).reshape(16, PACK_W)
    send = jnp.concatenate([w_pack[:MATRIX_ROWS].astype(WIRE), vec_bits], axis=0)
    wall = _all_gather(send, name="gather_weights")
    w = _gathered_weights(wall)

    loss, dx, grads = _step(x.reshape(t, D_MODEL), loss_target.reshape(t, D_MODEL), w, wts, bsz=bsz, seq=seq)
    loss = lax.psum(loss, MESH_AXES)

    parts = _pack_grads(grads)
    from_sibling = _exchange_d2d(parts, name="exchange_grads_d2d")
    chip_parts = _chip_partial(parts, from_sibling, name="chip_partial_grads")
    landed = _exchange_ici(chip_parts, name="exchange_grads_ici")
    g_pack = _sum_parts(landed, name="sum_grads")
    g_own = _unpack_shard(g_pack[:SHARD_ROWS])
    rep_slice = g_pack[SHARD_ROWS:SHARD_ROWS + REP_SLICE]
    g_rep = _all_gather(rep_slice, name="gather_replicated").reshape(REP_ROWS, PACK_W)
    g_own.update(_unpack_rep(g_rep, wts))

    deltas, new_m, new_v = {}, {}, {}
    for n in WEIGHTS:
        deltas[n], new_m[n], new_v[n] = _adamw(g_own[n], wts[n], mom1[n], mom2[n], name="adamw_" + n)
    result = [loss, dx.reshape(bsz, seq, D_MODEL)]
    for d in (g_own, deltas, new_m, new_v):
        result.extend(d[n] for n in WEIGHTS)
    return tuple(result)
```

```python
import jax
import jax.numpy as jnp
from jax import lax
from jax.experimental import pallas as pl
from jax.experimental.pallas import tpu as pltpu

F32 = jnp.float32
BF16 = jnp.bfloat16
WIRE = jnp.bfloat16

D_MODEL = 1024
D_RNN = 1280
RNN_BLOCKS = 10
RNN_BW = 128
CONV_WIDTH = 4
LRU_C = 8.0
N_HEADS = 8
QK_NOPE = 128
QK_ROPE = 64
V_DIM = 128
KV_RANK = 256
Q_RANK = 384
ROPE_THETA = 10000.0
EPS = 1e-6
ATTN_SCALE = (QK_NOPE + QK_ROPE) ** -0.5
HEAD_PAD = 256
LANES = 128

ADAM_LR = 0.001
ADAM_B1 = 0.9
ADAM_B2 = 0.999
ADAM_EPS = 1e-08
ADAM_WD = 0.01
ADAM_STEP = 10

N_DEV = 8
MESH_AXES = ("x", "y", "c")
VMEM_LIMIT_BYTES = 56 * 2**20
PACK_W = 1024

_SHARD_PIECES = (("w_in_a", 320), ("w_out_a", 160), ("w_dkv", 40), ("w_uk", 32), ("w_uv", 32),
                 ("w_in_b", 176), ("w_uq", 96), ("w_out_b", 128), ("pad", 8), ("small", 8))
_SHARD_OFF = {}
_r = 0
for _n, _k in _SHARD_PIECES:
    _SHARD_OFF[_n] = (_r, _r + _k)
    _r += _k
SHARD_ROWS = _r
MATRIX_ROWS = _SHARD_OFF["small"][0]
_SMALL = (("norm_a", 128), ("conv_w", 640), ("conv_b", 160), ("b_rg", 160), ("b_ig", 160), ("lru_lambda", 160))
_REP = (("w_rg", 163840), ("w_ig", 163840), ("norm_kv", 1024), ("kv_norm", 256), ("norm_b", 1024),
        ("q_norm", 384), ("final_norm", 1024))
REP_ROWS = 384
REP_SLICE = REP_ROWS // N_DEV
GRAD_ROWS = SHARD_ROWS + REP_SLICE + 8
GRAD_BLOCK = 352

WEIGHTS = ("norm_a", "w_in_a", "conv_w", "conv_b", "w_rg", "b_rg", "w_ig", "b_ig", "lru_lambda", "w_out_a",
           "norm_kv", "w_dkv", "kv_norm", "w_uk", "w_uv", "norm_b", "w_in_b", "q_norm", "w_uq", "w_out_b",
           "final_norm")


def _params(sem=None):
    return pltpu.CompilerParams(dimension_semantics=sem, vmem_limit_bytes=VMEM_LIMIT_BYTES)


def _sigmoid(z):
    return 1.0 / (1.0 + jnp.exp(-z))


def _col_block(n):
    return n if n <= 1408 else n // 2


def _matmul(a, b, *, name, nt=False, out_dtype=F32, residual=None, bm=512):
    m, k = a.shape
    n = b.shape[0] if nt else b.shape[1]
    bm = min(bm, m)
    bn = _col_block(n)
    dims = (((1,), (1,)), ((), ())) if nt else (((1,), (0,)), ((), ()))
    has_res = residual is not None

    def body(*refs):
        a_ref, b_ref, o_ref = refs[0], refs[1], refs[-1]
        acc = lax.dot_general(a_ref[...].astype(BF16), b_ref[...].astype(BF16), dims, preferred_element_type=F32)
        if has_res:
            acc = acc + refs[2][...]
        o_ref[...] = acc.astype(out_dtype)

    in_specs = [pl.BlockSpec((bm, k), lambda i, j: (i, 0)),
                pl.BlockSpec((bn, k), lambda i, j: (j, 0)) if nt else pl.BlockSpec((k, bn), lambda i, j: (0, j))]
    args = [a, b]
    if has_res:
        in_specs.append(pl.BlockSpec((bm, bn), lambda i, j: (i, j)))
        args.append(residual)
    return pl.pallas_call(
        body, grid=(m // bm, n // bn), in_specs=in_specs, out_specs=pl.BlockSpec((bm, bn), lambda i, j: (i, j)),
        out_shape=jax.ShapeDtypeStruct((m, n), out_dtype), compiler_params=_params(("parallel", "parallel")),
        name=name)(*args)


def _matmul_tn(a, b, *, name, bt=512):
    t, m = a.shape
    n = b.shape[1]
    bt = min(bt, t)
    bm, bn = _col_block(m), _col_block(n)

    def body(a_ref, b_ref, o_ref):
        @pl.when(pl.program_id(2) == 0)
        def _():
            o_ref[...] = jnp.zeros_like(o_ref)

        o_ref[...] += lax.dot_general(a_ref[...].astype(BF16), b_ref[...].astype(BF16),
                                      (((0,), (0,)), ((), ())), preferred_element_type=F32)

    return pl.pallas_call(
        body, grid=(m // bm, n // bn, t // bt),
        in_specs=[pl.BlockSpec((bt, bm), lambda i, j, s: (s, i)), pl.BlockSpec((bt, bn), lambda i, j, s: (s, j))],
        out_specs=pl.BlockSpec((bm, bn), lambda i, j, s: (i, j)),
        out_shape=jax.ShapeDtypeStruct((m, n), F32),
        compiler_params=_params(("parallel", "parallel", "arbitrary")), name=name)(a, b)


def _rms_fwd(x, gains, *, name, bt=512):
    t, d = x.shape
    bt = min(bt, t)
    ng = len(gains)

    def body(x_ref, *refs):
        xv = x_ref[...]
        nrm = xv * lax.rsqrt(jnp.mean(xv * xv, axis=-1, keepdims=True) + EPS)
        for g_ref, o_ref in zip(refs[:ng], refs[ng:]):
            o_ref[...] = (nrm * g_ref[...]).astype(BF16)

    row = pl.BlockSpec((bt, d), lambda i: (i, 0))
    vec = pl.BlockSpec((1, d), lambda i: (0, 0))
    return pl.pallas_call(
        body, grid=(t // bt,), in_specs=[row] + [vec] * ng, out_specs=[row] * ng,
        out_shape=[jax.ShapeDtypeStruct((t, d), BF16)] * ng, compiler_params=_params(("parallel",)), name=name)(x, *gains)


def _rms_bwd(x, dhs, gains, *, name, residual=None, out_dtype=F32, bt=512):
    t, d = x.shape
    bt = min(bt, t)
    ng = len(gains)
    has_res = residual is not None

    def body(*refs):
        x_ref = refs[0]
        dh_refs = refs[1:1 + ng]
        g_refs = refs[1 + ng:1 + 2 * ng]
        pos = 1 + 2 * ng
        res_ref = refs[pos] if has_res else None
        pos += int(has_res)
        dx_ref = refs[pos]
        dg_refs = refs[pos + 1:]

        @pl.when(pl.program_id(0) == 0)
        def _():
            for dg in dg_refs:
                dg[...] = jnp.zeros_like(dg)

        xv = x_ref[...]
        r = lax.rsqrt(jnp.mean(xv * xv, axis=-1, keepdims=True) + EPS)
        nrm = xv * r
        dn = jnp.zeros_like(xv)
        for dh_ref, g_ref, dg in zip(dh_refs, g_refs, dg_refs):
            dh = dh_ref[...]
            dg[...] += jnp.sum(dh * nrm, axis=0, keepdims=True)
            dn = dn + dh * g_ref[...]
        dx = r * (dn - nrm * jnp.mean(dn * nrm, axis=-1, keepdims=True))
        if has_res:
            dx = dx + res_ref[...]
        dx_ref[...] = dx.astype(out_dtype)

    row = pl.BlockSpec((bt, d), lambda i: (i, 0))
    vec = pl.BlockSpec((1, d), lambda i: (0, 0))
    args = [x, *dhs, *gains] + ([residual] if has_res else [])
    outs = pl.pallas_call(
        body, grid=(t // bt,), in_specs=[row] * (1 + ng) + [vec] * ng + [row] * int(has_res),
        out_specs=[row] + [vec] * ng,
        out_shape=[jax.ShapeDtypeStruct((t, d), out_dtype)] + [jax.ShapeDtypeStruct((1, d), F32)] * ng,
        compiler_params=_params(("arbitrary",)), name=name)(*args)
    return outs[0], outs[1:]


def _swap_halves(v):
    ax = v.ndim - 1
    lane = lax.broadcasted_iota(jnp.int32, v.shape, ax)
    up = pltpu.roll(v, LANES - QK_ROPE // 2, axis=ax)
    down = pltpu.roll(v, QK_ROPE // 2, axis=ax)
    return jnp.where(lane < QK_ROPE // 2, up, jnp.where(lane < QK_ROPE, down, 0.0))


def _rope(v, cos, sin):
    return v * cos + _swap_halves(v) * sin


def _rope_t(d, cos, sin):
    return d * cos + _swap_halves(d * sin)


def _rope_tables(seq):
    pos = jnp.arange(seq, dtype=F32)
    inv = ROPE_THETA ** (-jnp.arange(0, QK_ROPE, 2, dtype=F32) / QK_ROPE)
    ang = pos[:, None] * inv[None, :]
    cos, sin = jnp.cos(ang), jnp.sin(ang)
    zero = jnp.zeros((seq, LANES - QK_ROPE), F32)
    return jnp.concatenate([cos, cos, zero], axis=1), jnp.concatenate([-sin, sin, zero], axis=1)


def _rope_q(q, cos, sin, *, seq, name):
    t = q.shape[0]
    bt = min(512, seq)
    per_seq = seq // bt

    def body(q_ref, cos_ref, sin_ref, o_ref):
        o_ref[:, :QK_NOPE] = q_ref[:, :QK_NOPE].astype(BF16)
        o_ref[:, QK_NOPE:] = _rope(q_ref[:, QK_NOPE:], cos_ref[...], sin_ref[...]).astype(BF16)

    blk = pl.BlockSpec((bt, HEAD_PAD), lambda i, h: (i, h))
    tab = pl.BlockSpec((bt, LANES), lambda i, h: (i % per_seq, 0))
    return pl.pallas_call(
        body, grid=(t // bt, N_HEADS), in_specs=[blk, tab, tab], out_specs=blk,
        out_shape=jax.ShapeDtypeStruct(q.shape, BF16), compiler_params=_params(("parallel", "parallel")), name=name)(
            q, cos, sin)


def _rope_k(k, cos, sin, *, seq, name):
    t = k.shape[0]
    bt = min(512, seq)
    per_seq = seq // bt

    def body(k_ref, cos_ref, sin_ref, o_ref):
        o_ref[...] = _rope(k_ref[...], cos_ref[...], sin_ref[...]).astype(BF16)

    blk = pl.BlockSpec((bt, LANES), lambda i: (i, 0))
    tab = pl.BlockSpec((bt, LANES), lambda i: (i % per_seq, 0))
    return pl.pallas_call(
        body, grid=(t // bt,), in_specs=[blk, tab, tab], out_specs=blk,
        out_shape=jax.ShapeDtypeStruct(k.shape, BF16), compiler_params=_params(("parallel",)), name=name)(k, cos, sin)


def _softplus(z):
    return jnp.maximum(z, 0.0) + jnp.log1p(jnp.exp(-jnp.abs(z)))


def _neg_expm1(z):
    series = -z * (1.0 + z * (1.0 / 2) * (1.0 + z * (1.0 / 3) * (1.0 + z * (1.0 / 4) * (1.0 + z * (1.0 / 5) * (
        1.0 + z * (1.0 / 6))))))
    return jnp.where(z > -0.1, series, 1.0 - jnp.exp(z))


def _gates(xb, wrg, wig, brg, big, sp):
    xbb = xb.astype(BF16)
    r = _sigmoid(jnp.dot(xbb, wrg, preferred_element_type=F32) + brg)
    i = _sigmoid(jnp.dot(xbb, wig, preferred_element_type=F32) + big)
    la = (-LRU_C) * r * sp
    a = jnp.exp(la)
    mult = jnp.sqrt(_neg_expm1(2.0 * la))
    return r, i, a, mult


def _conv(xpad_ref, cw_ref, seq):
    acc = cw_ref[0:1, :] * xpad_ref[pl.ds(8 - (CONV_WIDTH - 1), seq), :]
    for k in range(1, CONV_WIDTH):
        acc = acc + cw_ref[k:k + 1, :] * xpad_ref[pl.ds(8 - (CONV_WIDTH - 1) + k, seq), :]
    return acc


def _seq_spec(seq):
    return pl.BlockSpec((None, seq, RNN_BW), lambda n, b: (b, 0, n))


def _chan_spec(rows):
    return pl.BlockSpec((rows, RNN_BW), lambda n, b: (0, n))


_GATE_W_SPEC = pl.BlockSpec((None, RNN_BW, RNN_BW), lambda n, b: (n, 0, 0))


def _lru_fwd(xp, ga, cw, vecs, wrg, wig, *, name):
    bsz, seq, _ = xp.shape
    groups = seq // 8

    def body(xp_ref, ga_ref, cw_ref, vec_ref, wrg_ref, wig_ref, xb_ref, hs_ref, y_ref, xpad, a_s, b_s):
        xpad[0:8, :] = jnp.zeros((8, RNN_BW), F32)
        xpad[pl.ds(8, seq), :] = xp_ref[...]
        xb = _conv(xpad, cw_ref, seq) + vec_ref[0:1, :]
        xb_ref[...] = xb
        sp = _softplus(-vec_ref[3:4, :])
        _, i, a, mult = _gates(xb, wrg_ref[...], wig_ref[...], vec_ref[1:2, :], vec_ref[2:3, :], sp)
        a_s[...] = a
        b_s[...] = mult * (i * xb)
        row = lax.broadcasted_iota(jnp.int32, (8, RNN_BW), 0)

        def group(g, h):
            r0 = pl.multiple_of(g * 8, 8)
            av = a_s[pl.ds(r0, 8), :]
            bv = b_s[pl.ds(r0, 8), :]
            for k in (1, 2, 4):
                m = row >= k
                bv = jnp.where(m, av * pltpu.roll(bv, k, axis=0) + bv, bv)
                av = jnp.where(m, av * pltpu.roll(av, k, axis=0), av)
            rows = av * h + bv
            hs_ref[pl.ds(r0, 8), :] = rows
            return rows[7:8, :]

        lax.fori_loop(0, groups, group, jnp.zeros((1, RNN_BW), F32))
        gav = ga_ref[...]
        y_ref[...] = (hs_ref[...] * (gav * _sigmoid(gav))).astype(BF16)

    sq = _seq_spec(seq)
    shape = (bsz, seq, D_RNN)
    return pl.pallas_call(
        body, grid=(RNN_BLOCKS, bsz), in_specs=[sq, sq, _chan_spec(8), _chan_spec(8), _GATE_W_SPEC, _GATE_W_SPEC],
        out_specs=[sq, sq, sq],
        out_shape=[jax.ShapeDtypeStruct(shape, F32), jax.ShapeDtypeStruct(shape, F32), jax.ShapeDtypeStruct(shape, BF16)],
        scratch_shapes=[pltpu.VMEM((seq + 8, RNN_BW), F32), pltpu.VMEM((seq, RNN_BW), F32), pltpu.VMEM((seq, RNN_BW), F32)],
        compiler_params=_params(("parallel", "parallel")), name=name)(xp, ga, cw, vecs, wrg, wig)


def _lru_bwd(dy, xp, xb, hs, ga, cw, vecs, wrg, wig, *, name):
    bsz, seq, _ = xp.shape
    groups = seq // 8

    def body(dy_ref, xp_ref, xb_ref, hs_ref, ga_ref, cw_ref, vec_ref, wrg_ref, wig_ref,
             dxp_ref, dga_ref, dwrg_ref, dwig_ref, dvec_ref, pad, a_s, d_s, lam_s):
        @pl.when(pl.program_id(1) == 0)
        def _():
            dwrg_ref[...] = jnp.zeros_like(dwrg_ref)
            dwig_ref[...] = jnp.zeros_like(dwig_ref)
            dvec_ref[...] = jnp.zeros_like(dvec_ref)

        xb = xb_ref[...]
        hs = hs_ref[...]
        gav = ga_ref[...]
        dy = dy_ref[...]
        sp = _softplus(-vec_ref[3:4, :])
        wrg = wrg_ref[...]
        wig = wig_ref[...]
        r, i, a, mult = _gates(xb, wrg, wig, vec_ref[1:2, :], vec_ref[2:3, :], sp)
        sg = _sigmoid(gav)
        dga_ref[...] = (dy * hs * (sg * (1.0 + gav * (1.0 - sg)))).astype(BF16)
        d_s[...] = dy * (gav * sg)

        pad[pl.ds(0, seq), :] = a
        pad[pl.ds(seq, 8), :] = jnp.zeros((8, RNN_BW), F32)
        a_s[...] = pad[pl.ds(1, seq), :]
        row = lax.broadcasted_iota(jnp.int32, (8, RNN_BW), 0)

        def group(g, nxt):
            r0 = pl.multiple_of((groups - 1 - g) * 8, 8)
            cv = a_s[pl.ds(r0, 8), :]
            bv = d_s[pl.ds(r0, 8), :]
            for k in (1, 2, 4):
                m = row < 8 - k
                bv = jnp.where(m, cv * pltpu.roll(bv, 8 - k, axis=0) + bv, bv)
                cv = jnp.where(m, cv * pltpu.roll(cv, 8 - k, axis=0), cv)
            rows = cv * nxt + bv
            lam_s[pl.ds(r0, 8), :] = rows
            return rows[0:1, :]

        lax.fori_loop(0, groups, group, jnp.zeros((1, RNN_BW), F32))
        dh = lam_s[...]

        pad[0:8, :] = jnp.zeros((8, RNN_BW), F32)
        pad[pl.ds(8, seq), :] = hs
        da = dh * pad[pl.ds(7, seq), :]
        ixb = i * xb
        dixb = dh * mult
        dla = da * a - (dh * ixb) * (a * a) / mult
        drp = (dla * ((-LRU_C) * sp)) * r * (1.0 - r)
        dip = (dixb * xb) * i * (1.0 - i)
        dvec_ref[0:1, :] += jnp.sum(drp, axis=0, keepdims=True)
        dvec_ref[1:2, :] += jnp.sum(dip, axis=0, keepdims=True)
        dvec_ref[2:3, :] += jnp.sum(dla * ((-LRU_C) * r), axis=0, keepdims=True)
        drpb = drp.astype(BF16)
        dipb = dip.astype(BF16)
        xbb = xb.astype(BF16)
        nt = (((1,), (1,)), ((), ()))
        tn = (((0,), (0,)), ((), ()))
        dxb = (dixb * i
               + lax.dot_general(drpb, wrg, nt, preferred_element_type=F32)
               + lax.dot_general(dipb, wig, nt, preferred_element_type=F32))
        dwrg_ref[...] += lax.dot_general(xbb, drpb, tn, preferred_element_type=F32)
        dwig_ref[...] += lax.dot_general(xbb, dipb, tn, preferred_element_type=F32)
        dvec_ref[3:4, :] += jnp.sum(dxb, axis=0, keepdims=True)

        pad[pl.ds(0, seq), :] = dxb
        pad[pl.ds(seq, 8), :] = jnp.zeros((8, RNN_BW), F32)
        dxp = cw_ref[0:1, :] * pad[pl.ds(CONV_WIDTH - 1, seq), :]
        for k in range(1, CONV_WIDTH):
            dxp = dxp + cw_ref[k:k + 1, :] * pad[pl.ds(CONV_WIDTH - 1 - k, seq), :]
        dxp_ref[...] = dxp.astype(BF16)
        pad[0:8, :] = jnp.zeros((8, RNN_BW), F32)
        pad[pl.ds(8, seq), :] = xp_ref[...]
        for k in range(CONV_WIDTH):
            dvec_ref[4 + k:5 + k, :] += jnp.sum(dxb * pad[pl.ds(8 - (CONV_WIDTH - 1) + k, seq), :], axis=0, keepdims=True)

    sq = _seq_spec(seq)
    shape = (bsz, seq, D_RNN)
    gshape = (RNN_BLOCKS, RNN_BW, RNN_BW)
    return pl.pallas_call(
        body, grid=(RNN_BLOCKS, bsz),
        in_specs=[sq, sq, sq, sq, sq, _chan_spec(8), _chan_spec(8), _GATE_W_SPEC, _GATE_W_SPEC],
        out_specs=[sq, sq, _GATE_W_SPEC, _GATE_W_SPEC, _chan_spec(8)],
        out_shape=[jax.ShapeDtypeStruct(shape, BF16), jax.ShapeDtypeStruct(shape, BF16),
                   jax.ShapeDtypeStruct(gshape, F32), jax.ShapeDtypeStruct(gshape, F32),
                   jax.ShapeDtypeStruct((8, D_RNN), F32)],
        scratch_shapes=[pltpu.VMEM((seq + 8, RNN_BW), F32), pltpu.VMEM((seq, RNN_BW), F32),
                        pltpu.VMEM((seq, RNN_BW), F32), pltpu.VMEM((seq, RNN_BW), F32)],
        compiler_params=_params(("parallel", "arbitrary")), name=name)(dy, xp, xb, hs, ga, cw, vecs, wrg, wig)


def _attn_block(seq):
    return min(256, seq)


def _diag_mask(blk):
    return lax.broadcasted_iota(jnp.int32, (blk, blk), 0) <= lax.broadcasted_iota(jnp.int32, (blk, blk), 1)


_NT = (((1,), (1,)), ((), ()))
FWD_HEADS = 4
BWD_HEADS = 2


def _attn_fwd(q, kn, kr, v_t, *, bsz, seq, name):
    t = bsz * seq
    blk = _attn_block(seq)
    nq = seq // blk
    hg = FWD_HEADS

    def body(q_ref, kn_ref, kr_ref, vt_ref, o_ref, lse_ref, acc):
        qi = pl.program_id(2)
        acc[...] = jnp.zeros_like(acc)

        def step(j, carry, diagonal):
            k0 = pl.multiple_of(j * blk, blk)
            kr_j = kr_ref[pl.ds(k0, blk), :]
            out = []
            for h in range(hg):
                m_i, l_i = carry[h]
                kv = jnp.concatenate([kn_ref[pl.ds(k0, blk), h * QK_NOPE:(h + 1) * QK_NOPE], kr_j], axis=1)
                qv = q_ref[:, h * HEAD_PAD:(h + 1) * HEAD_PAD]
                s = lax.dot_general(kv, qv, _NT, preferred_element_type=F32) * ATTN_SCALE
                if diagonal:
                    s = jnp.where(_diag_mask(blk), s, -jnp.inf)
                m_new = jnp.maximum(m_i, jnp.max(s, axis=0, keepdims=True))
                p = jnp.exp(s - m_new)
                alpha = jnp.exp(m_i - m_new)
                l_new = alpha * l_i + jnp.sum(p, axis=0, keepdims=True)
                acc[h] = alpha * acc[h] + jnp.dot(vt_ref[h * V_DIM:(h + 1) * V_DIM, pl.ds(k0, blk)], p.astype(BF16),
                                                  preferred_element_type=F32)
                out.append((m_new, l_new))
            return tuple(out)

        init = tuple((jnp.full((1, blk), -jnp.inf, F32), jnp.zeros((1, blk), F32)) for _ in range(hg))
        carry = lax.fori_loop(0, qi, lambda j, c: step(j, c, False), init)
        stats = step(qi, carry, True)
        for h in range(hg):
            m_i, l_i = stats[h]
            o_ref[:, h * V_DIM:(h + 1) * V_DIM] = (acc[h] / l_i).T
            lse_ref[h] = m_i + jnp.log(l_i)

    return pl.pallas_call(
        body, grid=(bsz, N_HEADS // hg, nq),
        in_specs=[pl.BlockSpec((blk, hg * HEAD_PAD), lambda b, g, i: (b * nq + i, g)),
                  pl.BlockSpec((seq, hg * QK_NOPE), lambda b, g, i: (b, g)),
                  pl.BlockSpec((seq, LANES), lambda b, g, i: (b, 0)),
                  pl.BlockSpec((hg * V_DIM, seq), lambda b, g, i: (g, b))],
        out_specs=[pl.BlockSpec((blk, hg * V_DIM), lambda b, g, i: (b * nq + i, g)),
                   pl.BlockSpec((hg, 1, blk), lambda b, g, i: (g, 0, b * nq + i))],
        out_shape=[jax.ShapeDtypeStruct((t, N_HEADS * V_DIM), F32), jax.ShapeDtypeStruct((N_HEADS, 1, t), F32)],
        scratch_shapes=[pltpu.VMEM((hg, V_DIM, blk), F32)],
        compiler_params=_params(("parallel", "parallel", "parallel")), name=name)(q, kn, kr, v_t)


def _attn_bwd(q, kn, kr, kn_t, kr_t, v, o, lse, do, cos, sin, *, bsz, seq, name):
    t = bsz * seq
    blk = _attn_block(seq)
    nq = seq // blk
    hg = BWD_HEADS

    def body(q_ref, kn_ref, kr_ref, knt_ref, krt_ref, v_ref, o_ref, lse_ref, do_ref, cos_ref, sin_ref,
             dq_ref, dkn_ref, dkr_ref, dv_ref, dqt_acc, dk_acc, dv_acc):
        dqt_acc[...] = jnp.zeros_like(dqt_acc)
        dk_acc[...] = jnp.zeros_like(dk_acc)
        dv_acc[...] = jnp.zeros_like(dv_acc)

        def q_block(i, _):
            q0 = pl.multiple_of(i * blk, blk)
            rows = []
            for h in range(hg):
                dov = do_ref[pl.ds(q0, blk), h * V_DIM:(h + 1) * V_DIM].astype(F32)
                dcol = jnp.sum(dov * o_ref[pl.ds(q0, blk), h * V_DIM:(h + 1) * V_DIM], axis=-1, keepdims=True)
                delta = jnp.broadcast_to(dcol, (blk, LANES)).T[0:1, :]
                rows.append((lse_ref[h, :, pl.ds(q0, blk)], delta))

            def pair(j, diagonal):
                k0 = pl.multiple_of(j * blk, blk)
                kr_j = kr_ref[pl.ds(k0, blk), :]
                krt_j = krt_ref[:, pl.ds(k0, blk)]
                for h in range(hg):
                    lse_i, delta = rows[h]
                    qv = q_ref[pl.ds(q0, blk), h * HEAD_PAD:(h + 1) * HEAD_PAD]
                    dov = do_ref[pl.ds(q0, blk), h * V_DIM:(h + 1) * V_DIM]
                    kv = jnp.concatenate([kn_ref[pl.ds(k0, blk), h * QK_NOPE:(h + 1) * QK_NOPE], kr_j], axis=1)
                    s = lax.dot_general(kv, qv, _NT, preferred_element_type=F32) * ATTN_SCALE
                    p = jnp.exp(s - lse_i)
                    if diagonal:
                        p = jnp.where(_diag_mask(blk), p, 0.0)
                    dv_acc[pl.ds(k0, blk), h * V_DIM:(h + 1) * V_DIM] += jnp.dot(
                        p.astype(BF16), dov, preferred_element_type=F32)
                    dp = lax.dot_general(v_ref[pl.ds(k0, blk), h * V_DIM:(h + 1) * V_DIM], dov, _NT,
                                         preferred_element_type=F32)
                    ds = (p * (dp - delta) * ATTN_SCALE).astype(BF16)
                    dk_acc[pl.ds(k0, blk), h * HEAD_PAD:(h + 1) * HEAD_PAD] += jnp.dot(ds, qv, preferred_element_type=F32)
                    base = h * HEAD_PAD
                    dqt_acc[base:base + QK_NOPE, pl.ds(q0, blk)] += jnp.dot(
                        knt_ref[h * QK_NOPE:(h + 1) * QK_NOPE, pl.ds(k0, blk)], ds, preferred_element_type=F32)
                    dqt_acc[base + QK_NOPE:base + HEAD_PAD, pl.ds(q0, blk)] += jnp.dot(
                        krt_j, ds, preferred_element_type=F32)

            def off_diagonal(j, _):
                pair(j, False)
                return 0

            lax.fori_loop(0, i, off_diagonal, 0)
            pair(i, True)
            return 0

        lax.fori_loop(0, nq, q_block, 0)
        dkr = jnp.zeros((seq, LANES), F32)
        for h in range(hg):
            base = h * HEAD_PAD
            for i in range(nq):
                rows = slice(i * blk, (i + 1) * blk)
                dq = dqt_acc[base:base + HEAD_PAD, rows].T
                dq_ref[rows, base:base + QK_NOPE] = dq[:, :QK_NOPE].astype(BF16)
                dq_ref[rows, base + QK_NOPE:base + HEAD_PAD] = _rope_t(
                    dq[:, QK_NOPE:], cos_ref[rows, :], sin_ref[rows, :]).astype(BF16)
            dkn_ref[:, h * QK_NOPE:(h + 1) * QK_NOPE] = dk_acc[:, base:base + QK_NOPE].astype(BF16)
            dkr = dkr + dk_acc[:, base + QK_NOPE:base + HEAD_PAD]
        dv_ref[...] = dv_acc[...].astype(BF16)

        @pl.when(pl.program_id(1) == 0)
        def _():
            dkr_ref[...] = jnp.zeros_like(dkr_ref)

        dkr_ref[...] += _rope_t(dkr, cos_ref[...], sin_ref[...])

    head = pl.BlockSpec((seq, hg * V_DIM), lambda b, g: (b, g))
    head_t = pl.BlockSpec((hg * V_DIM, seq), lambda b, g: (g, b))
    shared = pl.BlockSpec((seq, LANES), lambda b, g: (b, 0))
    shared_t = pl.BlockSpec((LANES, seq), lambda b, g: (0, b))
    table = pl.BlockSpec((seq, LANES), lambda b, g: (0, 0))
    qspec = pl.BlockSpec((seq, hg * HEAD_PAD), lambda b, g: (b, g))
    return pl.pallas_call(
        body, grid=(bsz, N_HEADS // hg),
        in_specs=[qspec, head, shared, head_t, shared_t, head, head,
                  pl.BlockSpec((hg, 1, seq), lambda b, g: (g, 0, b)), head, table, table],
        out_specs=[qspec, head, shared, head],
        out_shape=[jax.ShapeDtypeStruct((t, N_HEADS * HEAD_PAD), BF16), jax.ShapeDtypeStruct((t, N_HEADS * QK_NOPE), BF16),
                   jax.ShapeDtypeStruct((t, LANES), F32), jax.ShapeDtypeStruct((t, N_HEADS * V_DIM), BF16)],
        scratch_shapes=[pltpu.VMEM((hg * HEAD_PAD, seq), F32), pltpu.VMEM((seq, hg * HEAD_PAD), F32),
                        pltpu.VMEM((seq, hg * V_DIM), F32)],
        compiler_params=_params(("parallel", "arbitrary")), name=name)(q, kn, kr, kn_t, kr_t, v, o, lse, do, cos, sin)


def _head_and_loss(o, g2, x1, target, w_out, g_final, *, name, bt=256):
    t, d = x1.shape
    bt = min(bt, t)
    nt = (((1,), (1,)), ((), ()))

    def body(o_ref, g2_ref, x1_ref, tgt_ref, w_ref, gf_ref, loss_ref, dx2_ref, y2_ref, do_ref, dg2_ref, dgf_ref):
        @pl.when(pl.program_id(0) == 0)
        def _():
            loss_ref[...] = jnp.zeros_like(loss_ref)
            dgf_ref[...] = jnp.zeros_like(dgf_ref)

        ov = o_ref[...]
        gv = g2_ref[...]
        sg = _sigmoid(gv)
        silu = gv * sg
        y2 = (ov * silu).astype(BF16)
        y2_ref[...] = y2
        w = w_ref[...]
        x2 = x1_ref[...] + jnp.dot(y2, w, preferred_element_type=F32)
        r = lax.rsqrt(jnp.mean(x2 * x2, axis=-1, keepdims=True) + EPS)
        nrm = x2 * r
        gf = gf_ref[...]
        err = nrm * gf - tgt_ref[...]
        loss_ref[...] += 0.5 * jnp.sum(jnp.mean(err * err, axis=-1, keepdims=True))
        dyf = err * (1.0 / d)
        dgf_ref[...] += jnp.sum(dyf * nrm, axis=0, keepdims=True)
        dn = dyf * gf
        dx2 = r * (dn - nrm * jnp.mean(dn * nrm, axis=-1, keepdims=True))
        dx2_ref[...] = dx2
        dy2 = lax.dot_general(dx2.astype(BF16), w, nt, preferred_element_type=F32)
        do_ref[...] = (dy2 * silu).astype(BF16)
        dg2_ref[...] = (dy2 * ov * (sg * (1.0 + gv * (1.0 - sg)))).astype(BF16)

    row = pl.BlockSpec((bt, d), lambda i: (i, 0))
    vec = pl.BlockSpec((1, d), lambda i: (0, 0))
    return pl.pallas_call(
        body, grid=(t // bt,),
        in_specs=[row, row, row, row, pl.BlockSpec((d, d), lambda i: (0, 0)), vec],
        out_specs=[pl.BlockSpec((8, LANES), lambda i: (0, 0)), row, row, row, row, vec],
        out_shape=[jax.ShapeDtypeStruct((8, LANES), F32), jax.ShapeDtypeStruct((t, d), F32),
                   jax.ShapeDtypeStruct((t, d), BF16), jax.ShapeDtypeStruct((t, d), BF16),
                   jax.ShapeDtypeStruct((t, d), BF16), jax.ShapeDtypeStruct((1, d), F32)],
        compiler_params=_params(("arbitrary",)), name=name)(o, g2, x1, target, w_out, g_final)


def _sum_parts(parts, *, name, br=GRAD_BLOCK):
    npart, rows, w = parts.shape

    def body(p_ref, o_ref):
        acc = p_ref[0].astype(F32)
        for j in range(1, npart):
            acc = acc + p_ref[j].astype(F32)
        o_ref[...] = acc

    return pl.pallas_call(
        body, grid=(rows // br,), in_specs=[pl.BlockSpec((npart, br, w), lambda i: (0, i, 0))],
        out_specs=pl.BlockSpec((br, w), lambda i: (i, 0)), out_shape=jax.ShapeDtypeStruct((rows, w), F32),
        compiler_params=_params(("parallel",)), name=name)(parts)


def _chip_partial(parts, recv, *, name, br=GRAD_BLOCK):
    _, rows, w = parts.shape
    core = lax.axis_index("c").astype(jnp.int32).reshape(1)

    def body(c_ref, p_ref, r_ref, o_ref):
        o_ref[...] = (p_ref[...] + r_ref[...]).astype(BF16)

    grid_spec = pltpu.PrefetchScalarGridSpec(
        num_scalar_prefetch=1, grid=(4, rows // br),
        in_specs=[pl.BlockSpec((None, br, w), lambda k, i, c_ref: (2 * k + c_ref[0], i, 0)),
                  pl.BlockSpec((None, br, w), lambda k, i, c_ref: (k, i, 0))],
        out_specs=pl.BlockSpec((None, br, w), lambda k, i, c_ref: (k, i, 0)))
    return pl.pallas_call(
        body, grid_spec=grid_spec, out_shape=jax.ShapeDtypeStruct((4, rows, w), BF16),
        compiler_params=_params(("parallel", "parallel")), name=name)(core, parts, recv)


def _as_block(a):
    if a.ndim == 1:
        return a.reshape(1, -1)
    if a.ndim > 2 and a.shape[0] == 1:
        return a.reshape(a.shape[1:])
    return a


def _adamw(g, w, m, v, *, name):
    shape = w.shape
    g, w, m, v = (_as_block(a) for a in (g, w, m, v))

    def body(g_ref, w_ref, m_ref, v_ref, d_ref, nm_ref, nv_ref):
        gv = g_ref[...]
        nm = ADAM_B1 * m_ref[...] + (1.0 - ADAM_B1) * gv
        nv = ADAM_B2 * v_ref[...] + (1.0 - ADAM_B2) * (gv * gv)
        nm_ref[...] = nm
        nv_ref[...] = nv
        m_hat = nm / (1.0 - ADAM_B1 ** ADAM_STEP)
        v_hat = nv / (1.0 - ADAM_B2 ** ADAM_STEP)
        d_ref[...] = (-ADAM_LR) * (m_hat / (jnp.sqrt(v_hat) + ADAM_EPS) + ADAM_WD * w_ref[...])

    whole = pl.BlockSpec(memory_space=pltpu.VMEM)
    outs = pl.pallas_call(
        body, in_specs=[whole] * 4, out_specs=[whole] * 3, out_shape=[jax.ShapeDtypeStruct(w.shape, F32)] * 3,
        compiler_params=_params(), name=name)(g, w, m, v)
    return [o.reshape(shape) for o in outs]


def _mesh_pos():
    return lax.axis_index("x"), lax.axis_index("y"), lax.axis_index("c")


_ANY = pl.BlockSpec(memory_space=pl.ANY)


def _all_gather(block, *, name):
    m, n = block.shape

    def body(x_ref, out_ref, send_sems, recv_sems, local_sem):
        x, y, c = _mesh_pos()
        me, sibling = (x, y, c), (x, y, 1 - c)
        chips = [(1 - x, y), (x, 1 - y), (1 - x, 1 - y)]

        def slot(px, py, pc):
            return out_ref.at[4 * px + 2 * py + pc]

        def copy(k, blk, to, src=None):
            return pltpu.make_async_remote_copy(
                src_ref=slot(*blk) if src is None else src, dst_ref=slot(*blk),
                send_sem=send_sems.at[k], recv_sem=recv_sems.at[k], device_id=to, device_id_type=pl.DeviceIdType.MESH)

        mine = pltpu.make_async_copy(x_ref, slot(*me), local_sem)
        mine.start()
        first = [copy(0, me, sibling, src=x_ref)]
        first += [copy(1 + j, me, (*chip, c), src=x_ref) for j, chip in enumerate(chips)]
        for cp in first:
            cp.start()
        passed = [copy(4 + j, (*chip, c), sibling) for j, chip in enumerate(chips)]
        for j, chip in enumerate(chips):
            copy(1 + j, (*chip, c), me).wait_recv()
            passed[j].start()
        copy(0, sibling, me).wait_recv()
        for j, chip in enumerate(chips):
            copy(4 + j, (*chip, 1 - c), me).wait_recv()
        for cp in first + passed:
            cp.wait_send()
        mine.wait()

    return pl.pallas_call(
        body, out_shape=jax.ShapeDtypeStruct((N_DEV, m, n), block.dtype), in_specs=[_ANY], out_specs=_ANY,
        scratch_shapes=[pltpu.SemaphoreType.DMA((7,)), pltpu.SemaphoreType.DMA((7,)), pltpu.SemaphoreType.DMA(())],
        name=name)(block)


def _exchange_d2d(parts, *, name):
    _, rows, w = parts.shape

    def body(p_ref, land_ref, send_sems, recv_sems):
        x, y, c = _mesh_pos()
        sends = []
        for k in range(4):
            cp = pltpu.make_async_remote_copy(
                src_ref=p_ref.at[2 * k + (1 - c)], dst_ref=land_ref.at[k], send_sem=send_sems.at[k],
                recv_sem=recv_sems.at[k], device_id=(x, y, 1 - c), device_id_type=pl.DeviceIdType.MESH)
            cp.start()
            sends.append(cp)
        for cp in sends:
            cp.wait_recv()
        for cp in sends:
            cp.wait_send()

    return pl.pallas_call(
        body, out_shape=jax.ShapeDtypeStruct((4, rows, w), parts.dtype), in_specs=[_ANY], out_specs=_ANY,
        scratch_shapes=[pltpu.SemaphoreType.DMA((4,)), pltpu.SemaphoreType.DMA((4,))], name=name)(parts)


def _exchange_ici(parts, *, name):
    def body(p_ref, land_ref, send_sems, recv_sems, local_sem):
        x, y, c = _mesh_pos()
        mine = pltpu.make_async_copy(p_ref.at[2 * x + y], land_ref.at[3], local_sem)
        mine.start()
        sends = []
        for k, (px, py) in enumerate([(1 - x, y), (x, 1 - y), (1 - x, 1 - y)]):
            cp = pltpu.make_async_remote_copy(
                src_ref=p_ref.at[2 * px + py], dst_ref=land_ref.at[k], send_sem=send_sems.at[k],
                recv_sem=recv_sems.at[k], device_id=(px, py, c), device_id_type=pl.DeviceIdType.MESH)
            cp.start()
            sends.append(cp)
        for cp in sends:
            cp.wait_recv()
        for cp in sends:
            cp.wait_send()
        mine.wait()

    return pl.pallas_call(
        body, out_shape=jax.ShapeDtypeStruct(parts.shape, parts.dtype), in_specs=[_ANY], out_specs=_ANY,
        scratch_shapes=[pltpu.SemaphoreType.DMA((3,)), pltpu.SemaphoreType.DMA((3,)), pltpu.SemaphoreType.DMA(())],
        name=name)(parts)


def _rows(a):
    return a.reshape(-1, PACK_W)


def _pad_to(a, n):
    return jnp.pad(a, (0, n - a.shape[0]))


def _pack_shard(d):
    small = jnp.concatenate([d[n].reshape(-1) for n, _ in _SMALL])
    w_uq = jnp.pad(d["w_uq"][0], ((0, 0), (0, 0), (0, HEAD_PAD - QK_NOPE - QK_ROPE)))
    pieces = {"w_in_a": d["w_in_a"], "w_out_a": d["w_out_a"], "w_dkv": d["w_dkv"], "w_uk": d["w_uk"], "w_uv": d["w_uv"],
              "w_in_b": d["w_in_b"], "w_uq": w_uq, "w_out_b": d["w_out_b"], "pad": jnp.zeros((8, PACK_W), F32),
              "small": _pad_to(small, 8 * PACK_W)}
    return jnp.concatenate([_rows(pieces[n]) for n, _ in _SHARD_PIECES], axis=0)


def _unpack_shard(p):
    out = {}
    piece = {n: p[lo:hi] for n, (lo, hi) in _SHARD_OFF.items()}
    out["w_in_a"] = piece["w_in_a"].reshape(1, D_MODEL, 2 * D_RNN // N_DEV)
    out["w_out_a"] = piece["w_out_a"].reshape(1, D_RNN // N_DEV, D_MODEL)
    out["w_dkv"] = piece["w_dkv"].reshape(D_MODEL // N_DEV, KV_RANK + QK_ROPE)
    out["w_uk"] = piece["w_uk"].reshape(KV_RANK // N_DEV, N_HEADS, QK_NOPE)
    out["w_uv"] = piece["w_uv"].reshape(KV_RANK // N_DEV, N_HEADS, V_DIM)
    out["w_in_b"] = piece["w_in_b"].reshape(1, D_MODEL, (Q_RANK + N_HEADS * V_DIM) // N_DEV)
    out["w_uq"] = piece["w_uq"].reshape(1, Q_RANK // N_DEV, N_HEADS, HEAD_PAD)[..., :QK_NOPE + QK_ROPE]
    out["w_out_b"] = piece["w_out_b"].reshape(1, N_HEADS * V_DIM // N_DEV, D_MODEL)
    small = piece["small"].reshape(-1)
    off = 0
    shapes = {"norm_a": (1, D_MODEL // N_DEV), "conv_w": (1, CONV_WIDTH, D_RNN // N_DEV), "conv_b": (1, D_RNN // N_DEV),
              "b_rg": (1, D_RNN // N_DEV), "b_ig": (1, D_RNN // N_DEV), "lru_lambda": (1, D_RNN // N_DEV)}
    for n, k in _SMALL:
        out[n] = small[off:off + k].reshape(shapes[n])
        off += k
    return out


def _pack_rep(d):
    flat = jnp.concatenate([d[n].reshape(-1) for n, _ in _REP])
    return _rows(_pad_to(flat, REP_ROWS * PACK_W))


def _unpack_rep(p, like):
    flat = p.reshape(-1)
    out, off = {}, 0
    for n, k in _REP:
        out[n] = flat[off:off + k].reshape(like[n].shape)
        off += k
    return out


def _gathered_weights(wall):
    piece = {n: wall[:, lo:hi] for n, (lo, hi) in _SHARD_OFF.items() if n != "small"}
    w = {}
    w_in_a = piece["w_in_a"].reshape(N_DEV, D_MODEL, -1).transpose(1, 0, 2).reshape(D_MODEL, 2 * D_RNN)
    w["w_xp"], w["w_ga"] = w_in_a[:, :D_RNN], w_in_a[:, D_RNN:]
    w["w_out_a"] = piece["w_out_a"].reshape(D_RNN, D_MODEL)
    w_dkv = piece["w_dkv"].reshape(D_MODEL, KV_RANK + QK_ROPE)
    w["w_dkv_c"] = w_dkv[:, :KV_RANK]
    w["w_dkv_r"] = jnp.pad(w_dkv[:, KV_RANK:], ((0, 0), (0, LANES - QK_ROPE)))
    w["w_uk"] = piece["w_uk"].reshape(KV_RANK, N_HEADS * QK_NOPE)
    w["w_uv"] = piece["w_uv"].reshape(KV_RANK, N_HEADS * V_DIM)
    w_in_b = piece["w_in_b"].reshape(N_DEV, D_MODEL, -1).transpose(1, 0, 2).reshape(D_MODEL, Q_RANK + N_HEADS * V_DIM)
    w["w_cq"], w["w_g2"] = w_in_b[:, :Q_RANK], w_in_b[:, Q_RANK:]
    w["w_uq"] = piece["w_uq"].reshape(Q_RANK, N_HEADS * HEAD_PAD)
    w["w_out_b"] = piece["w_out_b"].reshape(N_HEADS * V_DIM, D_MODEL)
    small = lax.bitcast_convert_type(wall[:, MATRIX_ROWS:].reshape(N_DEV, 8 * PACK_W, 2), F32)
    off = dict(zip([n for n, _ in _SMALL], [0, 128, 768, 928, 1088, 1248]))
    w["norm_a"] = small[:, :128].reshape(1, D_MODEL)

    def by_channel(lo, rows):
        a = small[:, lo:lo + rows * (D_RNN // N_DEV)].reshape(N_DEV, rows, -1).transpose(1, 0, 2).reshape(rows, D_RNN)
        return jnp.pad(a, ((0, 8 - rows), (0, 0)))

    w["conv_taps"] = by_channel(off["conv_w"], CONV_WIDTH)
    w["lru_vecs"] = by_channel(off["conv_b"], 4)
    return w


def _pack_grads(g):
    def by_cols(a):
        r, n = a.shape
        return a.reshape(r, N_DEV, n // N_DEV).transpose(1, 0, 2).reshape(N_DEV, -1, PACK_W)

    def by_rows(a):
        return a.reshape(N_DEV, -1, PACK_W)

    small = jnp.concatenate([
        g["norm_a"].reshape(N_DEV, -1),
        g["conv_w"].reshape(CONV_WIDTH, N_DEV, -1).transpose(1, 0, 2).reshape(N_DEV, -1),
        g["conv_b"].reshape(N_DEV, -1), g["b_rg"].reshape(N_DEV, -1), g["b_ig"].reshape(N_DEV, -1),
        g["lru_lambda"].reshape(N_DEV, -1)], axis=1)
    small = jnp.pad(small, ((0, 0), (0, 8 * PACK_W - small.shape[1]))).reshape(N_DEV, 8, PACK_W)
    pieces = {"w_in_a": by_cols(g["w_in_a"]), "w_out_a": by_rows(g["w_out_a"]), "w_dkv": by_rows(g["w_dkv"]),
              "w_uk": by_rows(g["w_uk"]), "w_uv": by_rows(g["w_uv"]), "w_in_b": by_cols(g["w_in_b"]),
              "w_uq": by_rows(g["w_uq"]), "w_out_b": by_rows(g["w_out_b"]), "pad": jnp.zeros((N_DEV, 8, PACK_W), F32),
              "small": small}
    rep = _pack_rep(g).reshape(N_DEV, REP_SLICE, PACK_W)
    tail = jnp.zeros((N_DEV, GRAD_ROWS - SHARD_ROWS - REP_SLICE, PACK_W), F32)
    return jnp.concatenate([pieces[n] for n, _ in _SHARD_PIECES] + [rep, tail], axis=1)


def _step(x, target, w, rep, *, bsz, seq):
    t = bsz * seq
    cos, sin = _rope_tables(seq)
    g_a = w["norm_a"]
    g_kv = rep["norm_kv"].reshape(1, -1)
    g_kvn = rep["kv_norm"].reshape(1, -1)
    g_b = rep["norm_b"].reshape(1, -1)
    g_q = rep["q_norm"].reshape(1, -1)
    g_f = rep["final_norm"].reshape(1, -1)
    wrg = rep["w_rg"][0].astype(BF16)
    wig = rep["w_ig"][0].astype(BF16)
    cw8, vecs = w["conv_taps"], w["lru_vecs"]

    def seq3(a):
        return a.reshape(bsz, seq, a.shape[-1])

    def flat(a):
        return a.reshape(t, a.shape[-1])

    (h0,) = _rms_fwd(x, [g_a], name="norm_a_fwd")
    xp = _matmul(h0, w["w_xp"], name="in_a_x")
    ga = _matmul(h0, w["w_ga"], name="in_a_gate")
    xb, hs, y = _lru_fwd(seq3(xp), seq3(ga), cw8, vecs, wrg, wig, name="lru_fwd")
    x1 = _matmul(flat(y), w["w_out_a"], residual=x, name="out_a")
    hk, hq = _rms_fwd(x1, [g_kv, g_b], name="norm_kvb_fwd")
    ck = _matmul(hk, w["w_dkv_c"], name="dkv_c")
    krp = _matmul(hk, w["w_dkv_r"], name="dkv_r")
    cqp = _matmul(hq, w["w_cq"], name="in_b_q")
    g2 = _matmul(hq, w["w_g2"], name="in_b_gate")
    (ckv,) = _rms_fwd(ck, [g_kvn], name="kv_norm_fwd")
    (cq,) = _rms_fwd(cqp, [g_q], name="q_norm_fwd")
    kr = _rope_k(krp, cos, sin, seq=seq, name="rope_k")
    kn = _matmul(ckv, w["w_uk"], out_dtype=BF16, name="uk")
    v = _matmul(ckv, w["w_uv"], out_dtype=BF16, name="uv")
    q = _rope_q(_matmul(cq, w["w_uq"], name="uq"), cos, sin, seq=seq, name="rope_q")
    kn_t, kr_t, v_t = kn.T, kr.T, v.T
    o, lse = _attn_fwd(q, kn, kr, v_t, bsz=bsz, seq=seq, name="attn_fwd")
    loss, dx2, y2, do, dg2, dgf = _head_and_loss(o, g2, x1, target, w["w_out_b"], g_f, name="head_loss")
    grads = {"final_norm": dgf, "w_out_b": _matmul_tn(y2, dx2, name="d_w_out_b")}
    dq, dkn, dkr, dv = _attn_bwd(q, kn, kr, kn_t, kr_t, v, o, lse, do, cos, sin, bsz=bsz, seq=seq, name="attn_bwd")
    grads["w_uq"] = _matmul_tn(cq, dq, name="d_w_uq")
    dcq = _matmul(dq, w["w_uq"], nt=True, name="d_cq")
    dcqp, (dgq,) = _rms_bwd(cqp, [dcq], [g_q], out_dtype=BF16, name="q_norm_bwd")
    grads["q_norm"] = dgq
    grads["w_in_b"] = jnp.concatenate([_matmul_tn(hq, dcqp, name="d_w_in_b_q"), _matmul_tn(hq, dg2, name="d_w_in_b_g")], axis=1)
    dhq = _matmul(dg2, w["w_g2"], nt=True, residual=_matmul(dcqp, w["w_cq"], nt=True, name="d_hq_q"), name="d_hq")
    grads["w_uk"] = _matmul_tn(ckv, dkn, name="d_w_uk")
    grads["w_uv"] = _matmul_tn(ckv, dv, name="d_w_uv")
    dckv = _matmul(dv, w["w_uv"], nt=True, residual=_matmul(dkn, w["w_uk"], nt=True, name="d_ckv_k"), name="d_ckv")
    dck, (dgkvn,) = _rms_bwd(ck, [dckv], [g_kvn], out_dtype=BF16, name="kv_norm_bwd")
    grads["kv_norm"] = dgkvn
    grads["w_dkv"] = jnp.concatenate([_matmul_tn(hk, dck, name="d_w_dkv_c"),
                                      _matmul_tn(hk, dkr, name="d_w_dkv_r")[:, :QK_ROPE]], axis=1)
    dhk = _matmul(dkr, w["w_dkv_r"], nt=True, residual=_matmul(dck, w["w_dkv_c"], nt=True, name="d_hk_c"), name="d_hk")
    dx1, (dgb, dgkv) = _rms_bwd(x1, [dhq, dhk], [g_b, g_kv], residual=dx2, name="norm_kvb_bwd")
    grads["norm_b"], grads["norm_kv"] = dgb, dgkv
    grads["w_out_a"] = _matmul_tn(flat(y), dx1, name="d_w_out_a")
    dy = _matmul(dx1, w["w_out_a"], nt=True, name="d_y")
    dxp, dga, dwrg, dwig, dvec = _lru_bwd(seq3(dy), seq3(xp), xb, hs, seq3(ga), cw8, vecs, wrg, wig, name="lru_bwd")
    dxp, dga = flat(dxp), flat(dga)
    grads["w_rg"], grads["w_ig"] = dwrg, dwig
    grads["b_rg"], grads["b_ig"], grads["conv_b"] = dvec[0], dvec[1], dvec[3]
    lam = vecs[3]
    grads["lru_lambda"] = dvec[2] * (-1.0 / (1.0 + jnp.exp(lam)))
    grads["conv_w"] = dvec[4:4 + CONV_WIDTH]
    grads["w_in_a"] = jnp.concatenate([_matmul_tn(h0, dxp, name="d_w_in_a_x"), _matmul_tn(h0, dga, name="d_w_in_a_g")], axis=1)
    dh0 = _matmul(dga, w["w_ga"], nt=True, residual=_matmul(dxp, w["w_xp"], nt=True, name="d_h0_x"), name="d_h0")
    dx, (dga_norm,) = _rms_bwd(x, [dh0], [g_a], residual=dx1, name="norm_a_bwd")
    grads["norm_a"] = dga_norm
    return loss[0, 0], dx, grads


def kernel(x, norm_a, w_in_a, conv_w, conv_b, w_rg, b_rg, w_ig, b_ig, lru_lambda, w_out_a, norm_kv, w_dkv, kv_norm, w_uk, w_uv, norm_b, w_in_b, q_norm, w_uq, w_out_b, final_norm, loss_target, m_norm_a, m_w_in_a, m_conv_w, m_conv_b, m_w_rg, m_b_rg, m_w_ig, m_b_ig, m_lru_lambda, m_w_out_a, m_norm_kv, m_w_dkv, m_kv_norm, m_w_uk, m_w_uv, m_norm_b, m_w_in_b, m_q_norm, m_w_uq, m_w_out_b, m_final_norm, v_norm_a, v_w_in_a, v_conv_w, v_conv_b, v_w_rg, v_b_rg, v_w_ig, v_b_ig, v_lru_lambda, v_w_out_a, v_norm_kv, v_w_dkv, v_kv_norm, v_w_uk, v_w_uv, v_norm_b, v_w_in_b, v_q_norm, v_w_uq, v_w_out_b, v_final_norm):
    given = dict(locals())
    wts = {n: given[n] for n in WEIGHTS}
    mom1 = {n: given["m_" + n] for n in WEIGHTS}
    mom2 = {n: given["v_" + n] for n in WEIGHTS}
    bsz, seq, _ = x.shape
    t = bsz * seq

    w_pack = _pack_shard(wts)
    vec_bits = lax.bitcast_convert_type(w_pack[MATRIX_ROWS:], WIRE).reshape(16, PACK_W)
    send = jnp.concatenate([w_pack[:MATRIX_ROWS].astype(WIRE), vec_bits], axis=0)
    wall = _all_gather(send, name="gather_weights")
    w = _gathered_weights(wall)

    loss, dx, grads = _step(x.reshape(t, D_MODEL), loss_target.reshape(t, D_MODEL), w, wts, bsz=bsz, seq=seq)
    loss = lax.psum(loss, MESH_AXES)

    parts = _pack_grads(grads)
    from_sibling = _exchange_d2d(parts, name="exchange_grads_d2d")
    chip_parts = _chip_partial(parts, from_sibling, name="chip_partial_grads")
    landed = _exchange_ici(chip_parts, name="exchange_grads_ici")
    g_pack = _sum_parts(landed, name="sum_grads")
    g_own = _unpack_shard(g_pack[:SHARD_ROWS])
    rep_slice = g_pack[SHARD_ROWS:SHARD_ROWS + REP_SLICE]
    g_rep = _all_gather(rep_slice, name="gather_replicated").reshape(REP_ROWS, PACK_W)
    g_own.update(_unpack_rep(g_rep, wts))

    deltas, new_m, new_v = {}, {}, {}
    for n in WEIGHTS:
        deltas[n], new_m[n], new_v[n] = _adamw(g_own[n], wts[n], mom1[n], mom2[n], name="adamw_" + n)
    result = [loss, dx.reshape(bsz, seq, D_MODEL)]
    for d in (g_own, deltas, new_m, new_v):
        result.extend(d[n] for n in WEIGHTS)
    return tuple(result)
```

```python
import jax
import jax.numpy as jnp
from jax import lax
from jax.experimental import pallas as pl
from jax.experimental.pallas import tpu as pltpu

F32 = jnp.float32
BF16 = jnp.bfloat16
WIRE = jnp.bfloat16

D_MODEL = 1024
D_RNN = 1280
RNN_BLOCKS = 10
RNN_BW = 128
CONV_WIDTH = 4
LRU_C = 8.0
N_HEADS = 8
QK_NOPE = 128
QK_ROPE = 64
V_DIM = 128
KV_RANK = 256
Q_RANK = 384
ROPE_THETA = 10000.0
EPS = 1e-6
ATTN_SCALE = (QK_NOPE + QK_ROPE) ** -0.5
HEAD_PAD = 256
LANES = 128

ADAM_LR = 0.001
ADAM_B1 = 0.9
ADAM_B2 = 0.999
ADAM_EPS = 1e-08
ADAM_WD = 0.01
ADAM_STEP = 10

N_DEV = 8
MESH_AXES = ("x", "y", "c")
VMEM_LIMIT_BYTES = 56 * 2**20
PACK_W = 1024

_SHARD_PIECES = (("w_in_a", 320), ("w_out_a", 160), ("w_dkv", 40), ("w_uk", 32), ("w_uv", 32),
                 ("w_in_b", 176), ("w_uq", 96), ("w_out_b", 128), ("pad", 8), ("small", 8))
_SHARD_OFF = {}
_r = 0
for _n, _k in _SHARD_PIECES:
    _SHARD_OFF[_n] = (_r, _r + _k)
    _r += _k
SHARD_ROWS = _r
MATRIX_ROWS = _SHARD_OFF["small"][0]
_SMALL = (("norm_a", 128), ("conv_w", 640), ("conv_b", 160), ("b_rg", 160), ("b_ig", 160), ("lru_lambda", 160))
_REP = (("w_rg", 163840), ("w_ig", 163840), ("norm_kv", 1024), ("kv_norm", 256), ("norm_b", 1024),
        ("q_norm", 384), ("final_norm", 1024))
REP_ROWS = 384
REP_SLICE = REP_ROWS // N_DEV
GRAD_ROWS = SHARD_ROWS + REP_SLICE + 8
GRAD_BLOCK = 352

WEIGHTS = ("norm_a", "w_in_a", "conv_w", "conv_b", "w_rg", "b_rg", "w_ig", "b_ig", "lru_lambda", "w_out_a",
           "norm_kv", "w_dkv", "kv_norm", "w_uk", "w_uv", "norm_b", "w_in_b", "q_norm", "w_uq", "w_out_b",
           "final_norm")


def _params(sem=None):
    return pltpu.CompilerParams(dimension_semantics=sem, vmem_limit_bytes=VMEM_LIMIT_BYTES)


_NT = (((1,), (1,)), ((), ()))


def _sigmoid(z):
    return 1.0 / (1.0 + jnp.exp(-z))


def _col_block(n):
    return n if n <= 1408 else n // 2


def _matmul(a, b, *, name, nt=False, out_dtype=F32, residual=None, bm=512):
    m, k = a.shape
    n = b.shape[0] if nt else b.shape[1]
    bm = min(bm, m)
    bn = _col_block(n)
    dims = (((1,), (1,)), ((), ())) if nt else (((1,), (0,)), ((), ()))
    has_res = residual is not None

    def body(*refs):
        a_ref, b_ref, o_ref = refs[0], refs[1], refs[-1]
        acc = lax.dot_general(a_ref[...].astype(BF16), b_ref[...].astype(BF16), dims, preferred_element_type=F32)
        if has_res:
            acc = acc + refs[2][...]
        o_ref[...] = acc.astype(out_dtype)

    in_specs = [pl.BlockSpec((bm, k), lambda i, j: (i, 0)),
                pl.BlockSpec((bn, k), lambda i, j: (j, 0)) if nt else pl.BlockSpec((k, bn), lambda i, j: (0, j))]
    args = [a, b]
    if has_res:
        in_specs.append(pl.BlockSpec((bm, bn), lambda i, j: (i, j)))
        args.append(residual)
    return pl.pallas_call(
        body, grid=(m // bm, n // bn), in_specs=in_specs, out_specs=pl.BlockSpec((bm, bn), lambda i, j: (i, j)),
        out_shape=jax.ShapeDtypeStruct((m, n), out_dtype), compiler_params=_params(("parallel", "parallel")),
        name=name)(*args)


def _matmul_tn(a, b, *, name, bt=512):
    t, m = a.shape
    n = b.shape[1]
    bt = min(bt, t)
    bm, bn = _col_block(m), _col_block(n)

    def body(a_ref, b_ref, o_ref):
        @pl.when(pl.program_id(2) == 0)
        def _():
            o_ref[...] = jnp.zeros_like(o_ref)

        o_ref[...] += lax.dot_general(a_ref[...].astype(BF16), b_ref[...].astype(BF16),
                                      (((0,), (0,)), ((), ())), preferred_element_type=F32)

    return pl.pallas_call(
        body, grid=(m // bm, n // bn, t // bt),
        in_specs=[pl.BlockSpec((bt, bm), lambda i, j, s: (s, i)), pl.BlockSpec((bt, bn), lambda i, j, s: (s, j))],
        out_specs=pl.BlockSpec((bm, bn), lambda i, j, s: (i, j)),
        out_shape=jax.ShapeDtypeStruct((m, n), F32),
        compiler_params=_params(("parallel", "parallel", "arbitrary")), name=name)(a, b)


def _swap_halves(v):
    ax = v.ndim - 1
    lane = lax.broadcasted_iota(jnp.int32, v.shape, ax)
    up = pltpu.roll(v, LANES - QK_ROPE // 2, axis=ax)
    down = pltpu.roll(v, QK_ROPE // 2, axis=ax)
    return jnp.where(lane < QK_ROPE // 2, up, jnp.where(lane < QK_ROPE, down, 0.0))


def _rope(v, cos, sin):
    return v * cos + _swap_halves(v) * sin


def _rope_t(d, cos, sin):
    return d * cos + _swap_halves(d * sin)


def _rope_tables(seq):
    pos = jnp.arange(seq, dtype=F32)
    inv = ROPE_THETA ** (-jnp.arange(0, QK_ROPE, 2, dtype=F32) / QK_ROPE)
    ang = pos[:, None] * inv[None, :]
    cos, sin = jnp.cos(ang), jnp.sin(ang)
    zero = jnp.zeros((seq, LANES - QK_ROPE), F32)
    return jnp.concatenate([cos, cos, zero], axis=1), jnp.concatenate([-sin, sin, zero], axis=1)


def _rms(v):
    return v * lax.rsqrt(jnp.mean(v * v, axis=-1, keepdims=True) + EPS)


def _const_spec(a):
    return pl.BlockSpec(a.shape, lambda i: (0,) * a.ndim)


def _lru_proj_fwd(x, g_a, w_xp, w_ga, *, name, bt=256):
    t, d = x.shape
    bt = min(bt, t)
    n = w_xp.shape[1]

    def body(x_ref, g_ref, wx_ref, wg_ref, h_ref, xp_ref, ga_ref):
        h = (_rms(x_ref[...]) * g_ref[...]).astype(BF16)
        h_ref[...] = h
        xp_ref[...] = jnp.dot(h, wx_ref[...], preferred_element_type=F32)
        ga_ref[...] = jnp.dot(h, wg_ref[...], preferred_element_type=F32)

    row = lambda w: pl.BlockSpec((bt, w), lambda i: (i, 0))
    return pl.pallas_call(
        body, grid=(t // bt,), in_specs=[row(d), _const_spec(g_a), _const_spec(w_xp), _const_spec(w_ga)],
        out_specs=[row(d), row(n), row(n)],
        out_shape=[jax.ShapeDtypeStruct((t, d), BF16), jax.ShapeDtypeStruct((t, n), F32), jax.ShapeDtypeStruct((t, n), F32)],
        compiler_params=_params(("parallel",)), name=name)(x, g_a, w_xp, w_ga)


def _mla_proj_fwd(x1, gains, w, cos, sin, *, seq, name, bt=256):
    t, d = x1.shape
    bt = min(bt, seq)
    per_seq = seq // bt
    g_kv, g_b, g_kvn, g_q = gains
    consts = [g_kv, g_b, g_kvn, g_q, w["w_dkv_c"], w["w_dkv_r"], w["w_cq"], w["w_g2"], w["w_uk"], w["w_uv"],
              w["w_uk_t"], w["w_uv_t"], w["w_uq"]]

    def body(x_ref, cos_ref, sin_ref, gkv_ref, gb_ref, gkvn_ref, gq_ref, wdc_ref, wdr_ref, wcq_ref, wg2_ref,
             wuk_ref, wuv_ref, wukt_ref, wuvt_ref, wuq_ref,
             hk_ref, hq_ref, ck_ref, cqp_ref, g2_ref, ckv_ref, cq_ref, q_ref, kn_ref, v_ref, kr_ref, knt_ref, vt_ref, krt_ref):
        nrm = _rms(x_ref[...])
        hk = (nrm * gkv_ref[...]).astype(BF16)
        hq = (nrm * gb_ref[...]).astype(BF16)
        hk_ref[...] = hk
        hq_ref[...] = hq
        ck = jnp.dot(hk, wdc_ref[...], preferred_element_type=F32)
        ck_ref[...] = ck
        cqp = jnp.dot(hq, wcq_ref[...], preferred_element_type=F32)
        cqp_ref[...] = cqp
        g2_ref[...] = jnp.dot(hq, wg2_ref[...], preferred_element_type=F32)
        cosv, sinv = cos_ref[...], sin_ref[...]
        kr = _rope(jnp.dot(hk, wdr_ref[...], preferred_element_type=F32), cosv, sinv)
        kr_ref[...] = kr.astype(BF16)
        krt_ref[...] = kr.T.astype(BF16)
        ckv = (_rms(ck) * gkvn_ref[...]).astype(BF16)
        ckv_ref[...] = ckv
        kn_ref[...] = jnp.dot(ckv, wuk_ref[...], preferred_element_type=F32).astype(BF16)
        v_ref[...] = jnp.dot(ckv, wuv_ref[...], preferred_element_type=F32).astype(BF16)
        knt_ref[...] = lax.dot_general(wukt_ref[...], ckv, _NT, preferred_element_type=F32).astype(BF16)
        vt_ref[...] = lax.dot_general(wuvt_ref[...], ckv, _NT, preferred_element_type=F32).astype(BF16)
        cq = (_rms(cqp) * gq_ref[...]).astype(BF16)
        cq_ref[...] = cq
        for h in range(N_HEADS):
            qh = jnp.dot(cq, wuq_ref[:, h * HEAD_PAD:(h + 1) * HEAD_PAD], preferred_element_type=F32)
            q_ref[:, h * HEAD_PAD:h * HEAD_PAD + QK_NOPE] = qh[:, :QK_NOPE].astype(BF16)
            q_ref[:, h * HEAD_PAD + QK_NOPE:(h + 1) * HEAD_PAD] = _rope(qh[:, QK_NOPE:], cosv, sinv).astype(BF16)

    row = lambda w_: pl.BlockSpec((bt, w_), lambda i: (i, 0))
    col = lambda h_: pl.BlockSpec((h_, bt), lambda i: (0, i))
    tab = pl.BlockSpec((bt, LANES), lambda i: (i % per_seq, 0))
    nh = N_HEADS * V_DIM
    shapes = [((t, d), BF16), ((t, d), BF16), ((t, KV_RANK), F32), ((t, Q_RANK), F32), ((t, nh), F32), ((t, KV_RANK), BF16),
              ((t, Q_RANK), BF16), ((t, N_HEADS * HEAD_PAD), BF16), ((t, nh), BF16), ((t, nh), BF16), ((t, LANES), BF16),
              ((nh, t), BF16), ((nh, t), BF16), ((LANES, t), BF16)]
    out_specs = [row(d), row(d), row(KV_RANK), row(Q_RANK), row(nh), row(KV_RANK), row(Q_RANK), row(N_HEADS * HEAD_PAD),
                 row(nh), row(nh), row(LANES), col(nh), col(nh), col(LANES)]
    return pl.pallas_call(
        body, grid=(t // bt,), in_specs=[row(d), tab, tab] + [_const_spec(a) for a in consts], out_specs=out_specs,
        out_shape=[jax.ShapeDtypeStruct(s, dt) for s, dt in shapes],
        compiler_params=_params(("parallel",)), name=name)(x1, cos, sin, *consts)


def _rms_bwd_rows(xv, dn):
    r = lax.rsqrt(jnp.mean(xv * xv, axis=-1, keepdims=True) + EPS)
    nrm = xv * r
    return r * (dn - nrm * jnp.mean(dn * nrm, axis=-1, keepdims=True)), nrm


def _col_sum(v):
    return jnp.sum(v, axis=0, keepdims=True)


def _lru_proj_bwd(dxp, dga, x, dx1, g_a, w_xp, w_ga, *, name, bt=256):
    t, d = x.shape
    bt = min(bt, t)
    n = w_xp.shape[1]

    def body(dxp_ref, dga_ref, x_ref, dx1_ref, g_ref, wx_ref, wg_ref, dx_ref, dg_ref):
        @pl.when(pl.program_id(0) == 0)
        def _():
            dg_ref[...] = jnp.zeros_like(dg_ref)

        dh = (lax.dot_general(dxp_ref[...], wx_ref[...], _NT, preferred_element_type=F32)
              + lax.dot_general(dga_ref[...], wg_ref[...], _NT, preferred_element_type=F32))
        dxn, nrm = _rms_bwd_rows(x_ref[...], dh * g_ref[...])
        dg_ref[...] += _col_sum(dh * nrm)
        dx_ref[...] = dx1_ref[...] + dxn

    row = lambda w: pl.BlockSpec((bt, w), lambda i: (i, 0))
    return pl.pallas_call(
        body, grid=(t // bt,),
        in_specs=[row(n), row(n), row(d), row(d), _const_spec(g_a), _const_spec(w_xp), _const_spec(w_ga)],
        out_specs=[row(d), _const_spec(g_a)],
        out_shape=[jax.ShapeDtypeStruct((t, d), F32), jax.ShapeDtypeStruct((1, d), F32)],
        compiler_params=_params(("arbitrary",)), name=name)(dxp, dga, x, dx1, g_a, w_xp, w_ga)


def _mla_proj_bwd(x1, dx2, cqp, ck, dq, dkn, dv, dkr, dg2, gains, w, *, name, bt=256):
    t, d = x1.shape
    bt = min(bt, t)
    g_kv, g_b, g_kvn, g_q = gains
    consts = [g_kv, g_b, g_kvn, g_q, w["w_dkv_c"], w["w_dkv_r"], w["w_cq"], w["w_g2"], w["w_uk"], w["w_uv"], w["w_uq"]]
    nh = N_HEADS * V_DIM

    def body(x1_ref, dx2_ref, cqp_ref, ck_ref, dq_ref, dkn_ref, dv_ref, dkr_ref, dg2_ref,
             gkv_ref, gb_ref, gkvn_ref, gq_ref, wdc_ref, wdr_ref, wcq_ref, wg2_ref, wuk_ref, wuv_ref, wuq_ref,
             dx1_ref, du2_ref, dckr_ref, dgkv_ref, dgb_ref, dgkvn_ref, dgq_ref):
        @pl.when(pl.program_id(0) == 0)
        def _():
            for ref in (dgkv_ref, dgb_ref, dgkvn_ref, dgq_ref):
                ref[...] = jnp.zeros_like(ref)

        dot_nt = lambda a, b: lax.dot_general(a, b, _NT, preferred_element_type=F32)
        dcq = dot_nt(dq_ref[...], wuq_ref[...])
        dcqp, nq = _rms_bwd_rows(cqp_ref[...], dcq * gq_ref[...])
        dgq_ref[...] += _col_sum(dcq * nq)
        dcqp = dcqp.astype(BF16)
        dg2 = dg2_ref[...]
        du2_ref[:, :Q_RANK] = dcqp
        du2_ref[:, Q_RANK:] = dg2
        dhq = dot_nt(dcqp, wcq_ref[...]) + dot_nt(dg2, wg2_ref[...])
        dckv = dot_nt(dkn_ref[...], wuk_ref[...]) + dot_nt(dv_ref[...], wuv_ref[...])
        dck, nc = _rms_bwd_rows(ck_ref[...], dckv * gkvn_ref[...])
        dgkvn_ref[...] += _col_sum(dckv * nc)
        dck = dck.astype(BF16)
        dkr = dkr_ref[...].astype(BF16)
        dckr_ref[:, :KV_RANK] = dck
        dckr_ref[:, KV_RANK:] = dkr
        dhk = dot_nt(dck, wdc_ref[...]) + dot_nt(dkr, wdr_ref[...])
        dxn, n1 = _rms_bwd_rows(x1_ref[...], dhq * gb_ref[...] + dhk * gkv_ref[...])
        dgb_ref[...] += _col_sum(dhq * n1)
        dgkv_ref[...] += _col_sum(dhk * n1)
        dx1_ref[...] = dx2_ref[...] + dxn

    row = lambda w_: pl.BlockSpec((bt, w_), lambda i: (i, 0))
    vec = lambda w_: pl.BlockSpec((1, w_), lambda i: (0, 0))
    in_specs = [row(d), row(d), row(Q_RANK), row(KV_RANK), row(N_HEADS * HEAD_PAD), row(nh), row(nh), row(LANES), row(nh)]
    return pl.pallas_call(
        body, grid=(t // bt,), in_specs=in_specs + [_const_spec(a) for a in consts],
        out_specs=[row(d), row(Q_RANK + nh), row(KV_RANK + LANES), vec(d), vec(d), vec(KV_RANK), vec(Q_RANK)],
        out_shape=[jax.ShapeDtypeStruct((t, d), F32), jax.ShapeDtypeStruct((t, Q_RANK + nh), BF16),
                   jax.ShapeDtypeStruct((t, KV_RANK + LANES), BF16), jax.ShapeDtypeStruct((1, d), F32),
                   jax.ShapeDtypeStruct((1, d), F32), jax.ShapeDtypeStruct((1, KV_RANK), F32),
                   jax.ShapeDtypeStruct((1, Q_RANK), F32)],
        compiler_params=_params(("arbitrary",)), name=name)(x1, dx2, cqp, ck, dq, dkn, dv, dkr, dg2, *consts)


def _softplus(z):
    return jnp.maximum(z, 0.0) + jnp.log1p(jnp.exp(-jnp.abs(z)))


def _neg_expm1(z):
    series = -z * (1.0 + z * (1.0 / 2) * (1.0 + z * (1.0 / 3) * (1.0 + z * (1.0 / 4) * (1.0 + z * (1.0 / 5) * (
        1.0 + z * (1.0 / 6))))))
    return jnp.where(z > -0.1, series, 1.0 - jnp.exp(z))


def _gates(xb, wrg, wig, brg, big, sp):
    xbb = xb.astype(BF16)
    r = _sigmoid(jnp.dot(xbb, wrg, preferred_element_type=F32) + brg)
    i = _sigmoid(jnp.dot(xbb, wig, preferred_element_type=F32) + big)
    la = (-LRU_C) * r * sp
    a = jnp.exp(la)
    mult = jnp.sqrt(_neg_expm1(2.0 * la))
    return r, i, a, mult


def _conv(xpad_ref, cw_ref, seq):
    acc = cw_ref[0:1, :] * xpad_ref[pl.ds(8 - (CONV_WIDTH - 1), seq), :]
    for k in range(1, CONV_WIDTH):
        acc = acc + cw_ref[k:k + 1, :] * xpad_ref[pl.ds(8 - (CONV_WIDTH - 1) + k, seq), :]
    return acc


def _seq_spec(seq):
    return pl.BlockSpec((None, seq, RNN_BW), lambda n, b: (b, 0, n))


def _chan_spec(rows):
    return pl.BlockSpec((rows, RNN_BW), lambda n, b: (0, n))


_GATE_W_SPEC = pl.BlockSpec((None, RNN_BW, RNN_BW), lambda n, b: (n, 0, 0))


def _lru_fwd(xp, ga, cw, vecs, wrg, wig, *, name):
    bsz, seq, _ = xp.shape
    groups = seq // 8

    def body(xp_ref, ga_ref, cw_ref, vec_ref, wrg_ref, wig_ref, xb_ref, hs_ref, y_ref, xpad, a_s, b_s):
        xpad[0:8, :] = jnp.zeros((8, RNN_BW), F32)
        xpad[pl.ds(8, seq), :] = xp_ref[...]
        xb = _conv(xpad, cw_ref, seq) + vec_ref[0:1, :]
        xb_ref[...] = xb
        sp = _softplus(-vec_ref[3:4, :])
        _, i, a, mult = _gates(xb, wrg_ref[...], wig_ref[...], vec_ref[1:2, :], vec_ref[2:3, :], sp)
        a_s[...] = a
        b_s[...] = mult * (i * xb)
        row = lax.broadcasted_iota(jnp.int32, (8, RNN_BW), 0)

        def group(g, h):
            r0 = pl.multiple_of(g * 8, 8)
            av = a_s[pl.ds(r0, 8), :]
            bv = b_s[pl.ds(r0, 8), :]
            for k in (1, 2, 4):
                m = row >= k
                bv = jnp.where(m, av * pltpu.roll(bv, k, axis=0) + bv, bv)
                av = jnp.where(m, av * pltpu.roll(av, k, axis=0), av)
            rows = av * h + bv
            hs_ref[pl.ds(r0, 8), :] = rows
            return rows[7:8, :]

        lax.fori_loop(0, groups, group, jnp.zeros((1, RNN_BW), F32))
        gav = ga_ref[...]
        y_ref[...] = (hs_ref[...] * (gav * _sigmoid(gav))).astype(BF16)

    sq = _seq_spec(seq)
    shape = (bsz, seq, D_RNN)
    return pl.pallas_call(
        body, grid=(RNN_BLOCKS, bsz), in_specs=[sq, sq, _chan_spec(8), _chan_spec(8), _GATE_W_SPEC, _GATE_W_SPEC],
        out_specs=[sq, sq, sq],
        out_shape=[jax.ShapeDtypeStruct(shape, F32), jax.ShapeDtypeStruct(shape, F32), jax.ShapeDtypeStruct(shape, BF16)],
        scratch_shapes=[pltpu.VMEM((seq + 8, RNN_BW), F32), pltpu.VMEM((seq, RNN_BW), F32), pltpu.VMEM((seq, RNN_BW), F32)],
        compiler_params=_params(("parallel", "parallel")), name=name)(xp, ga, cw, vecs, wrg, wig)


def _lru_bwd(dy, xp, xb, hs, ga, cw, vecs, wrg, wig, *, name):
    bsz, seq, _ = xp.shape
    groups = seq // 8

    def body(dy_ref, xp_ref, xb_ref, hs_ref, ga_ref, cw_ref, vec_ref, wrg_ref, wig_ref,
             dxp_ref, dga_ref, dwrg_ref, dwig_ref, dvec_ref, pad, a_s, d_s, lam_s):
        @pl.when(pl.program_id(1) == 0)
        def _():
            dwrg_ref[...] = jnp.zeros_like(dwrg_ref)
            dwig_ref[...] = jnp.zeros_like(dwig_ref)
            dvec_ref[...] = jnp.zeros_like(dvec_ref)

        xb = xb_ref[...]
        hs = hs_ref[...]
        gav = ga_ref[...]
        dy = dy_ref[...]
        sp = _softplus(-vec_ref[3:4, :])
        wrg = wrg_ref[...]
        wig = wig_ref[...]
        r, i, a, mult = _gates(xb, wrg, wig, vec_ref[1:2, :], vec_ref[2:3, :], sp)
        sg = _sigmoid(gav)
        dga_ref[...] = (dy * hs * (sg * (1.0 + gav * (1.0 - sg)))).astype(BF16)
        d_s[...] = dy * (gav * sg)

        pad[pl.ds(0, seq), :] = a
        pad[pl.ds(seq, 8), :] = jnp.zeros((8, RNN_BW), F32)
        a_s[...] = pad[pl.ds(1, seq), :]
        row = lax.broadcasted_iota(jnp.int32, (8, RNN_BW), 0)

        def group(g, nxt):
            r0 = pl.multiple_of((groups - 1 - g) * 8, 8)
            cv = a_s[pl.ds(r0, 8), :]
            bv = d_s[pl.ds(r0, 8), :]
            for k in (1, 2, 4):
                m = row < 8 - k
                bv = jnp.where(m, cv * pltpu.roll(bv, 8 - k, axis=0) + bv, bv)
                cv = jnp.where(m, cv * pltpu.roll(cv, 8 - k, axis=0), cv)
            rows = cv * nxt + bv
            lam_s[pl.ds(r0, 8), :] = rows
            return rows[0:1, :]

        lax.fori_loop(0, groups, group, jnp.zeros((1, RNN_BW), F32))
        dh = lam_s[...]

        pad[0:8, :] = jnp.zeros((8, RNN_BW), F32)
        pad[pl.ds(8, seq), :] = hs
        da = dh * pad[pl.ds(7, seq), :]
        ixb = i * xb
        dixb = dh * mult
        dla = da * a - (dh * ixb) * (a * a) / mult
        drp = (dla * ((-LRU_C) * sp)) * r * (1.0 - r)
        dip = (dixb * xb) * i * (1.0 - i)
        dvec_ref[0:1, :] += jnp.sum(drp, axis=0, keepdims=True)
        dvec_ref[1:2, :] += jnp.sum(dip, axis=0, keepdims=True)
        dvec_ref[2:3, :] += jnp.sum(dla * ((-LRU_C) * r), axis=0, keepdims=True)
        drpb = drp.astype(BF16)
        dipb = dip.astype(BF16)
        xbb = xb.astype(BF16)
        nt = (((1,), (1,)), ((), ()))
        tn = (((0,), (0,)), ((), ()))
        dxb = (dixb * i
               + lax.dot_general(drpb, wrg, nt, preferred_element_type=F32)
               + lax.dot_general(dipb, wig, nt, preferred_element_type=F32))
        dwrg_ref[...] += lax.dot_general(xbb, drpb, tn, preferred_element_type=F32)
        dwig_ref[...] += lax.dot_general(xbb, dipb, tn, preferred_element_type=F32)
        dvec_ref[3:4, :] += jnp.sum(dxb, axis=0, keepdims=True)

        pad[pl.ds(0, seq), :] = dxb
        pad[pl.ds(seq, 8), :] = jnp.zeros((8, RNN_BW), F32)
        dxp = cw_ref[0:1, :] * pad[pl.ds(CONV_WIDTH - 1, seq), :]
        for k in range(1, CONV_WIDTH):
            dxp = dxp + cw_ref[k:k + 1, :] * pad[pl.ds(CONV_WIDTH - 1 - k, seq), :]
        dxp_ref[...] = dxp.astype(BF16)
        pad[0:8, :] = jnp.zeros((8, RNN_BW), F32)
        pad[pl.ds(8, seq), :] = xp_ref[...]
        for k in range(CONV_WIDTH):
            dvec_ref[4 + k:5 + k, :] += jnp.sum(dxb * pad[pl.ds(8 - (CONV_WIDTH - 1) + k, seq), :], axis=0, keepdims=True)

    sq = _seq_spec(seq)
    shape = (bsz, seq, D_RNN)
    gshape = (RNN_BLOCKS, RNN_BW, RNN_BW)
    return pl.pallas_call(
        body, grid=(RNN_BLOCKS, bsz),
        in_specs=[sq, sq, sq, sq, sq, _chan_spec(8), _chan_spec(8), _GATE_W_SPEC, _GATE_W_SPEC],
        out_specs=[sq, sq, _GATE_W_SPEC, _GATE_W_SPEC, _chan_spec(8)],
        out_shape=[jax.ShapeDtypeStruct(shape, BF16), jax.ShapeDtypeStruct(shape, BF16),
                   jax.ShapeDtypeStruct(gshape, F32), jax.ShapeDtypeStruct(gshape, F32),
                   jax.ShapeDtypeStruct((8, D_RNN), F32)],
        scratch_shapes=[pltpu.VMEM((seq + 8, RNN_BW), F32), pltpu.VMEM((seq, RNN_BW), F32),
                        pltpu.VMEM((seq, RNN_BW), F32), pltpu.VMEM((seq, RNN_BW), F32)],
        compiler_params=_params(("parallel", "arbitrary")), name=name)(dy, xp, xb, hs, ga, cw, vecs, wrg, wig)


def _attn_block(seq):
    return min(256, seq)


def _diag_mask(blk):
    return lax.broadcasted_iota(jnp.int32, (blk, blk), 0) <= lax.broadcasted_iota(jnp.int32, (blk, blk), 1)


FWD_HEADS = 4
BWD_HEADS = 2


def _attn_fwd(q, kn, kr, v_t, *, bsz, seq, name):
    t = bsz * seq
    blk = _attn_block(seq)
    nq = seq // blk
    hg = FWD_HEADS

    def body(q_ref, kn_ref, kr_ref, vt_ref, o_ref, lse_ref, acc):
        qi = pl.program_id(2)
        acc[...] = jnp.zeros_like(acc)

        def step(j, carry, diagonal):
            k0 = pl.multiple_of(j * blk, blk)
            kr_j = kr_ref[pl.ds(k0, blk), :]
            out = []
            for h in range(hg):
                m_i, l_i = carry[h]
                kv = jnp.concatenate([kn_ref[pl.ds(k0, blk), h * QK_NOPE:(h + 1) * QK_NOPE], kr_j], axis=1)
                qv = q_ref[:, h * HEAD_PAD:(h + 1) * HEAD_PAD]
                s = lax.dot_general(kv, qv, _NT, preferred_element_type=F32) * ATTN_SCALE
                if diagonal:
                    s = jnp.where(_diag_mask(blk), s, -jnp.inf)
                m_new = jnp.maximum(m_i, jnp.max(s, axis=0, keepdims=True))
                p = jnp.exp(s - m_new)
                alpha = jnp.exp(m_i - m_new)
                l_new = alpha * l_i + jnp.sum(p, axis=0, keepdims=True)
                acc[h] = alpha * acc[h] + jnp.dot(vt_ref[h * V_DIM:(h + 1) * V_DIM, pl.ds(k0, blk)], p.astype(BF16),
                                                  preferred_element_type=F32)
                out.append((m_new, l_new))
            return tuple(out)

        init = tuple((jnp.full((1, blk), -jnp.inf, F32), jnp.zeros((1, blk), F32)) for _ in range(hg))
        carry = lax.fori_loop(0, qi, lambda j, c: step(j, c, False), init)
        stats = step(qi, carry, True)
        for h in range(hg):
            m_i, l_i = stats[h]
            o_ref[:, h * V_DIM:(h + 1) * V_DIM] = (acc[h] / l_i).T
            lse_ref[h] = m_i + jnp.log(l_i)

    return pl.pallas_call(
        body, grid=(bsz, N_HEADS // hg, nq),
        in_specs=[pl.BlockSpec((blk, hg * HEAD_PAD), lambda b, g, i: (b * nq + i, g)),
                  pl.BlockSpec((seq, hg * QK_NOPE), lambda b, g, i: (b, g)),
                  pl.BlockSpec((seq, LANES), lambda b, g, i: (b, 0)),
                  pl.BlockSpec((hg * V_DIM, seq), lambda b, g, i: (g, b))],
        out_specs=[pl.BlockSpec((blk, hg * V_DIM), lambda b, g, i: (b * nq + i, g)),
                   pl.BlockSpec((hg, 1, blk), lambda b, g, i: (g, 0, b * nq + i))],
        out_shape=[jax.ShapeDtypeStruct((t, N_HEADS * V_DIM), F32), jax.ShapeDtypeStruct((N_HEADS, 1, t), F32)],
        scratch_shapes=[pltpu.VMEM((hg, V_DIM, blk), F32)],
        compiler_params=_params(("parallel", "parallel", "parallel")), name=name)(q, kn, kr, v_t)


def _attn_bwd(q, kn, kr, kn_t, kr_t, v, o, lse, do, cos, sin, *, bsz, seq, name):
    t = bsz * seq
    blk = _attn_block(seq)
    nq = seq // blk
    hg = BWD_HEADS

    def body(q_ref, kn_ref, kr_ref, knt_ref, krt_ref, v_ref, o_ref, lse_ref, do_ref, cos_ref, sin_ref,
             dq_ref, dkn_ref, dkr_ref, dv_ref, dqt_acc, dk_acc, dv_acc):
        dqt_acc[...] = jnp.zeros_like(dqt_acc)
        dk_acc[...] = jnp.zeros_like(dk_acc)
        dv_acc[...] = jnp.zeros_like(dv_acc)

        def q_block(i, _):
            q0 = pl.multiple_of(i * blk, blk)
            rows = []
            for h in range(hg):
                dov = do_ref[pl.ds(q0, blk), h * V_DIM:(h + 1) * V_DIM].astype(F32)
                dcol = jnp.sum(dov * o_ref[pl.ds(q0, blk), h * V_DIM:(h + 1) * V_DIM], axis=-1, keepdims=True)
                delta = jnp.broadcast_to(dcol, (blk, LANES)).T[0:1, :]
                rows.append((lse_ref[h, :, pl.ds(q0, blk)], delta))

            def pair(j, diagonal):
                k0 = pl.multiple_of(j * blk, blk)
                kr_j = kr_ref[pl.ds(k0, blk), :]
                krt_j = krt_ref[:, pl.ds(k0, blk)]
                for h in range(hg):
                    lse_i, delta = rows[h]
                    qv = q_ref[pl.ds(q0, blk), h * HEAD_PAD:(h + 1) * HEAD_PAD]
                    dov = do_ref[pl.ds(q0, blk), h * V_DIM:(h + 1) * V_DIM]
                    kv = jnp.concatenate([kn_ref[pl.ds(k0, blk), h * QK_NOPE:(h + 1) * QK_NOPE], kr_j], axis=1)
                    s = lax.dot_general(kv, qv, _NT, preferred_element_type=F32) * ATTN_SCALE
                    p = jnp.exp(s - lse_i)
                    if diagonal:
                        p = jnp.where(_diag_mask(blk), p, 0.0)
                    dv_acc[pl.ds(k0, blk), h * V_DIM:(h + 1) * V_DIM] += jnp.dot(
                        p.astype(BF16), dov, preferred_element_type=F32)
                    dp = lax.dot_general(v_ref[pl.ds(k0, blk), h * V_DIM:(h + 1) * V_DIM], dov, _NT,
                                         preferred_element_type=F32)
                    ds = (p * (dp - delta) * ATTN_SCALE).astype(BF16)
                    dk_acc[pl.ds(k0, blk), h * HEAD_PAD:(h + 1) * HEAD_PAD] += jnp.dot(ds, qv, preferred_element_type=F32)
                    base = h * HEAD_PAD
                    dqt_acc[base:base + QK_NOPE, pl.ds(q0, blk)] += jnp.dot(
                        knt_ref[h * QK_NOPE:(h + 1) * QK_NOPE, pl.ds(k0, blk)], ds, preferred_element_type=F32)
                    dqt_acc[base + QK_NOPE:base + HEAD_PAD, pl.ds(q0, blk)] += jnp.dot(
                        krt_j, ds, preferred_element_type=F32)

            def off_diagonal(j, _):
                pair(j, False)
                return 0

            lax.fori_loop(0, i, off_diagonal, 0)
            pair(i, True)
            return 0

        lax.fori_loop(0, nq, q_block, 0)
        dkr = jnp.zeros((seq, LANES), F32)
        for h in range(hg):
            base = h * HEAD_PAD
            for i in range(nq):
                rows = slice(i * blk, (i + 1) * blk)
                dq = dqt_acc[base:base + HEAD_PAD, rows].T
                dq_ref[rows, base:base + QK_NOPE] = dq[:, :QK_NOPE].astype(BF16)
                dq_ref[rows, base + QK_NOPE:base + HEAD_PAD] = _rope_t(
                    dq[:, QK_NOPE:], cos_ref[rows, :], sin_ref[rows, :]).astype(BF16)
            dkn_ref[:, h * QK_NOPE:(h + 1) * QK_NOPE] = dk_acc[:, base:base + QK_NOPE].astype(BF16)
            dkr = dkr + dk_acc[:, base + QK_NOPE:base + HEAD_PAD]
        dv_ref[...] = dv_acc[...].astype(BF16)

        @pl.when(pl.program_id(1) == 0)
        def _():
            dkr_ref[...] = jnp.zeros_like(dkr_ref)

        dkr_ref[...] += _rope_t(dkr, cos_ref[...], sin_ref[...])

    head = pl.BlockSpec((seq, hg * V_DIM), lambda b, g: (b, g))
    head_t = pl.BlockSpec((hg * V_DIM, seq), lambda b, g: (g, b))
    shared = pl.BlockSpec((seq, LANES), lambda b, g: (b, 0))
    shared_t = pl.BlockSpec((LANES, seq), lambda b, g: (0, b))
    table = pl.BlockSpec((seq, LANES), lambda b, g: (0, 0))
    qspec = pl.BlockSpec((seq, hg * HEAD_PAD), lambda b, g: (b, g))
    return pl.pallas_call(
        body, grid=(bsz, N_HEADS // hg),
        in_specs=[qspec, head, shared, head_t, shared_t, head, head,
                  pl.BlockSpec((hg, 1, seq), lambda b, g: (g, 0, b)), head, table, table],
        out_specs=[qspec, head, shared, head],
        out_shape=[jax.ShapeDtypeStruct((t, N_HEADS * HEAD_PAD), BF16), jax.ShapeDtypeStruct((t, N_HEADS * QK_NOPE), BF16),
                   jax.ShapeDtypeStruct((t, LANES), F32), jax.ShapeDtypeStruct((t, N_HEADS * V_DIM), BF16)],
        scratch_shapes=[pltpu.VMEM((hg * HEAD_PAD, seq), F32), pltpu.VMEM((seq, hg * HEAD_PAD), F32),
                        pltpu.VMEM((seq, hg * V_DIM), F32)],
        compiler_params=_params(("parallel", "arbitrary")), name=name)(q, kn, kr, kn_t, kr_t, v, o, lse, do, cos, sin)


def _head_and_loss(o, g2, x1, target, w_out, g_final, *, name, bt=256):
    t, d = x1.shape
    bt = min(bt, t)
    nt = (((1,), (1,)), ((), ()))

    def body(o_ref, g2_ref, x1_ref, tgt_ref, w_ref, gf_ref, loss_ref, dx2_ref, y2_ref, do_ref, dg2_ref, dgf_ref):
        @pl.when(pl.program_id(0) == 0)
        def _():
            loss_ref[...] = jnp.zeros_like(loss_ref)
            dgf_ref[...] = jnp.zeros_like(dgf_ref)

        ov = o_ref[...]
        gv = g2_ref[...]
        sg = _sigmoid(gv)
        silu = gv * sg
        y2 = (ov * silu).astype(BF16)
        y2_ref[...] = y2
        w = w_ref[...]
        x2 = x1_ref[...] + jnp.dot(y2, w, preferred_element_type=F32)
        r = lax.rsqrt(jnp.mean(x2 * x2, axis=-1, keepdims=True) + EPS)
        nrm = x2 * r
        gf = gf_ref[...]
        err = nrm * gf - tgt_ref[...]
        loss_ref[...] += 0.5 * jnp.sum(jnp.mean(err * err, axis=-1, keepdims=True))
        dyf = err * (1.0 / d)
        dgf_ref[...] += jnp.sum(dyf * nrm, axis=0, keepdims=True)
        dn = dyf * gf
        dx2 = r * (dn - nrm * jnp.mean(dn * nrm, axis=-1, keepdims=True))
        dx2_ref[...] = dx2
        dy2 = lax.dot_general(dx2.astype(BF16), w, nt, preferred_element_type=F32)
        do_ref[...] = (dy2 * silu).astype(BF16)
        dg2_ref[...] = (dy2 * ov * (sg * (1.0 + gv * (1.0 - sg)))).astype(BF16)

    row = pl.BlockSpec((bt, d), lambda i: (i, 0))
    vec = pl.BlockSpec((1, d), lambda i: (0, 0))
    return pl.pallas_call(
        body, grid=(t // bt,),
        in_specs=[row, row, row, row, pl.BlockSpec((d, d), lambda i: (0, 0)), vec],
        out_specs=[pl.BlockSpec((8, LANES), lambda i: (0, 0)), row, row, row, row, vec],
        out_shape=[jax.ShapeDtypeStruct((8, LANES), F32), jax.ShapeDtypeStruct((t, d), F32),
                   jax.ShapeDtypeStruct((t, d), BF16), jax.ShapeDtypeStruct((t, d), BF16),
                   jax.ShapeDtypeStruct((t, d), BF16), jax.ShapeDtypeStruct((1, d), F32)],
        compiler_params=_params(("arbitrary",)), name=name)(o, g2, x1, target, w_out, g_final)


def _sum_parts(parts, *, name, br=GRAD_BLOCK):
    npart, rows, w = parts.shape

    def body(p_ref, o_ref):
        acc = p_ref[0].astype(F32)
        for j in range(1, npart):
            acc = acc + p_ref[j].astype(F32)
        o_ref[...] = acc

    return pl.pallas_call(
        body, grid=(rows // br,), in_specs=[pl.BlockSpec((npart, br, w), lambda i: (0, i, 0))],
        out_specs=pl.BlockSpec((br, w), lambda i: (i, 0)), out_shape=jax.ShapeDtypeStruct((rows, w), F32),
        compiler_params=_params(("parallel",)), name=name)(parts)


def _chip_partial(parts, recv, *, name, br=GRAD_BLOCK):
    _, rows, w = parts.shape
    core = lax.axis_index("c").astype(jnp.int32).reshape(1)

    def body(c_ref, p_ref, r_ref, o_ref):
        o_ref[...] = (p_ref[...] + r_ref[...]).astype(BF16)

    grid_spec = pltpu.PrefetchScalarGridSpec(
        num_scalar_prefetch=1, grid=(4, rows // br),
        in_specs=[pl.BlockSpec((None, br, w), lambda k, i, c_ref: (2 * k + c_ref[0], i, 0)),
                  pl.BlockSpec((None, br, w), lambda k, i, c_ref: (k, i, 0))],
        out_specs=pl.BlockSpec((None, br, w), lambda k, i, c_ref: (k, i, 0)))
    return pl.pallas_call(
        body, grid_spec=grid_spec, out_shape=jax.ShapeDtypeStruct((4, rows, w), BF16),
        compiler_params=_params(("parallel", "parallel")), name=name)(core, parts, recv)


def _as_block(a):
    if a.ndim == 1:
        return a.reshape(1, -1)
    if a.ndim > 2 and a.shape[0] == 1:
        return a.reshape(a.shape[1:])
    return a


def _adamw(g, w, m, v, *, name):
    shape = w.shape
    g, w, m, v = (_as_block(a) for a in (g, w, m, v))

    def body(g_ref, w_ref, m_ref, v_ref, d_ref, nm_ref, nv_ref):
        gv = g_ref[...]
        nm = ADAM_B1 * m_ref[...] + (1.0 - ADAM_B1) * gv
        nv = ADAM_B2 * v_ref[...] + (1.0 - ADAM_B2) * (gv * gv)
        nm_ref[...] = nm
        nv_ref[...] = nv
        m_hat = nm / (1.0 - ADAM_B1 ** ADAM_STEP)
        v_hat = nv / (1.0 - ADAM_B2 ** ADAM_STEP)
        d_ref[...] = (-ADAM_LR) * (m_hat / (jnp.sqrt(v_hat) + ADAM_EPS) + ADAM_WD * w_ref[...])

    whole = pl.BlockSpec(memory_space=pltpu.VMEM)
    outs = pl.pallas_call(
        body, in_specs=[whole] * 4, out_specs=[whole] * 3, out_shape=[jax.ShapeDtypeStruct(w.shape, F32)] * 3,
        compiler_params=_params(), name=name)(g, w, m, v)
    return [o.reshape(shape) for o in outs]


def _mesh_pos():
    return lax.axis_index("x"), lax.axis_index("y"), lax.axis_index("c")


_ANY = pl.BlockSpec(memory_space=pl.ANY)


def _all_gather(block, *, name):
    m, n = block.shape

    def body(x_ref, out_ref, send_sems, recv_sems, local_sem):
        x, y, c = _mesh_pos()
        me, sibling = (x, y, c), (x, y, 1 - c)
        chips = [(1 - x, y), (x, 1 - y), (1 - x, 1 - y)]

        def slot(px, py, pc):
            return out_ref.at[4 * px + 2 * py + pc]

        def copy(k, blk, to, src=None):
            return pltpu.make_async_remote_copy(
                src_ref=slot(*blk) if src is None else src, dst_ref=slot(*blk),
                send_sem=send_sems.at[k], recv_sem=recv_sems.at[k], device_id=to, device_id_type=pl.DeviceIdType.MESH)

        mine = pltpu.make_async_copy(x_ref, slot(*me), local_sem)
        mine.start()
        first = [copy(0, me, sibling, src=x_ref)]
        first += [copy(1 + j, me, (*chip, c), src=x_ref) for j, chip in enumerate(chips)]
        for cp in first:
            cp.start()
        passed = [copy(4 + j, (*chip, c), sibling) for j, chip in enumerate(chips)]
        for j, chip in enumerate(chips):
            copy(1 + j, (*chip, c), me).wait_recv()
            passed[j].start()
        copy(0, sibling, me).wait_recv()
        for j, chip in enumerate(chips):
            copy(4 + j, (*chip, 1 - c), me).wait_recv()
        for cp in first + passed:
            cp.wait_send()
        mine.wait()

    return pl.pallas_call(
        body, out_shape=jax.ShapeDtypeStruct((N_DEV, m, n), block.dtype), in_specs=[_ANY], out_specs=_ANY,
        scratch_shapes=[pltpu.SemaphoreType.DMA((7,)), pltpu.SemaphoreType.DMA((7,)), pltpu.SemaphoreType.DMA(())],
        name=name)(block)


def _exchange_d2d(parts, *, name):
    _, rows, w = parts.shape

    def body(p_ref, land_ref, send_sems, recv_sems):
        x, y, c = _mesh_pos()
        sends = []
        for k in range(4):
            cp = pltpu.make_async_remote_copy(
                src_ref=p_ref.at[2 * k + (1 - c)], dst_ref=land_ref.at[k], send_sem=send_sems.at[k],
                recv_sem=recv_sems.at[k], device_id=(x, y, 1 - c), device_id_type=pl.DeviceIdType.MESH)
            cp.start()
            sends.append(cp)
        for cp in sends:
            cp.wait_recv()
        for cp in sends:
            cp.wait_send()

    return pl.pallas_call(
        body, out_shape=jax.ShapeDtypeStruct((4, rows, w), parts.dtype), in_specs=[_ANY], out_specs=_ANY,
        scratch_shapes=[pltpu.SemaphoreType.DMA((4,)), pltpu.SemaphoreType.DMA((4,))], name=name)(parts)


def _exchange_ici(parts, *, name):
    def body(p_ref, land_ref, send_sems, recv_sems, local_sem):
        x, y, c = _mesh_pos()
        mine = pltpu.make_async_copy(p_ref.at[2 * x + y], land_ref.at[3], local_sem)
        mine.start()
        sends = []
        for k, (px, py) in enumerate([(1 - x, y), (x, 1 - y), (1 - x, 1 - y)]):
            cp = pltpu.make_async_remote_copy(
                src_ref=p_ref.at[2 * px + py], dst_ref=land_ref.at[k], send_sem=send_sems.at[k],
                recv_sem=recv_sems.at[k], device_id=(px, py, c), device_id_type=pl.DeviceIdType.MESH)
            cp.start()
            sends.append(cp)
        for cp in sends:
            cp.wait_recv()
        for cp in sends:
            cp.wait_send()
        mine.wait()

    return pl.pallas_call(
        body, out_shape=jax.ShapeDtypeStruct(parts.shape, parts.dtype), in_specs=[_ANY], out_specs=_ANY,
        scratch_shapes=[pltpu.SemaphoreType.DMA((3,)), pltpu.SemaphoreType.DMA((3,)), pltpu.SemaphoreType.DMA(())],
        name=name)(parts)


def _rows(a):
    return a.reshape(-1, PACK_W)


def _pad_to(a, n):
    return jnp.pad(a, (0, n - a.shape[0]))


def _pack_shard(d):
    small = jnp.concatenate([d[n].reshape(-1) for n, _ in _SMALL])
    w_uq = jnp.pad(d["w_uq"][0], ((0, 0), (0, 0), (0, HEAD_PAD - QK_NOPE - QK_ROPE)))
    pieces = {"w_in_a": d["w_in_a"], "w_out_a": d["w_out_a"], "w_dkv": d["w_dkv"], "w_uk": d["w_uk"], "w_uv": d["w_uv"],
              "w_in_b": d["w_in_b"], "w_uq": w_uq, "w_out_b": d["w_out_b"], "pad": jnp.zeros((8, PACK_W), F32),
              "small": _pad_to(small, 8 * PACK_W)}
    return jnp.concatenate([_rows(pieces[n]) for n, _ in _SHARD_PIECES], axis=0)


def _unpack_shard(p):
    out = {}
    piece = {n: p[lo:hi] for n, (lo, hi) in _SHARD_OFF.items()}
    out["w_in_a"] = piece["w_in_a"].reshape(1, D_MODEL, 2 * D_RNN // N_DEV)
    out["w_out_a"] = piece["w_out_a"].reshape(1, D_RNN // N_DEV, D_MODEL)
    out["w_dkv"] = piece["w_dkv"].reshape(D_MODEL // N_DEV, KV_RANK + QK_ROPE)
    out["w_uk"] = piece["w_uk"].reshape(KV_RANK // N_DEV, N_HEADS, QK_NOPE)
    out["w_uv"] = piece["w_uv"].reshape(KV_RANK // N_DEV, N_HEADS, V_DIM)
    out["w_in_b"] = piece["w_in_b"].reshape(1, D_MODEL, (Q_RANK + N_HEADS * V_DIM) // N_DEV)
    out["w_uq"] = piece["w_uq"].reshape(1, Q_RANK // N_DEV, N_HEADS, HEAD_PAD)[..., :QK_NOPE + QK_ROPE]
    out["w_out_b"] = piece["w_out_b"].reshape(1, N_HEADS * V_DIM // N_DEV, D_MODEL)
    small = piece["small"].reshape(-1)
    off = 0
    shapes = {"norm_a": (1, D_MODEL // N_DEV), "conv_w": (1, CONV_WIDTH, D_RNN // N_DEV), "conv_b": (1, D_RNN // N_DEV),
              "b_rg": (1, D_RNN // N_DEV), "b_ig": (1, D_RNN // N_DEV), "lru_lambda": (1, D_RNN // N_DEV)}
    for n, k in _SMALL:
        out[n] = small[off:off + k].reshape(shapes[n])
        off += k
    return out


def _pack_rep(d):
    flat = jnp.concatenate([d[n].reshape(-1) for n, _ in _REP])
    return _rows(_pad_to(flat, REP_ROWS * PACK_W))


def _unpack_rep(p, like):
    flat = p.reshape(-1)
    out, off = {}, 0
    for n, k in _REP:
        out[n] = flat[off:off + k].reshape(like[n].shape)
        off += k
    return out


def _gathered_weights(wall):
    piece = {n: wall[:, lo:hi] for n, (lo, hi) in _SHARD_OFF.items() if n != "small"}
    w = {}
    w_in_a = piece["w_in_a"].reshape(N_DEV, D_MODEL, -1).transpose(1, 0, 2).reshape(D_MODEL, 2 * D_RNN)
    w["w_xp"], w["w_ga"] = w_in_a[:, :D_RNN], w_in_a[:, D_RNN:]
    w["w_out_a"] = piece["w_out_a"].reshape(D_RNN, D_MODEL)
    w_dkv = piece["w_dkv"].reshape(D_MODEL, KV_RANK + QK_ROPE)
    w["w_dkv_c"] = w_dkv[:, :KV_RANK]
    w["w_dkv_r"] = jnp.pad(w_dkv[:, KV_RANK:], ((0, 0), (0, LANES - QK_ROPE)))
    w["w_uk"] = piece["w_uk"].reshape(KV_RANK, N_HEADS * QK_NOPE)
    w["w_uv"] = piece["w_uv"].reshape(KV_RANK, N_HEADS * V_DIM)
    w["w_uk_t"], w["w_uv_t"] = w["w_uk"].T, w["w_uv"].T
    w_in_b = piece["w_in_b"].reshape(N_DEV, D_MODEL, -1).transpose(1, 0, 2).reshape(D_MODEL, Q_RANK + N_HEADS * V_DIM)
    w["w_cq"], w["w_g2"] = w_in_b[:, :Q_RANK], w_in_b[:, Q_RANK:]
    w["w_uq"] = piece["w_uq"].reshape(Q_RANK, N_HEADS * HEAD_PAD)
    w["w_out_b"] = piece["w_out_b"].reshape(N_HEADS * V_DIM, D_MODEL)
    small = lax.bitcast_convert_type(wall[:, MATRIX_ROWS:].reshape(N_DEV, 8 * PACK_W, 2), F32)
    off = dict(zip([n for n, _ in _SMALL], [0, 128, 768, 928, 1088, 1248]))
    w["norm_a"] = small[:, :128].reshape(1, D_MODEL)

    def by_channel(lo, rows):
        a = small[:, lo:lo + rows * (D_RNN // N_DEV)].reshape(N_DEV, rows, -1).transpose(1, 0, 2).reshape(rows, D_RNN)
        return jnp.pad(a, ((0, 8 - rows), (0, 0)))

    w["conv_taps"] = by_channel(off["conv_w"], CONV_WIDTH)
    w["lru_vecs"] = by_channel(off["conv_b"], 4)
    return w


def _pack_grads(g):
    def by_cols(a):
        r, n = a.shape
        return a.reshape(r, N_DEV, n // N_DEV).transpose(1, 0, 2).reshape(N_DEV, -1, PACK_W)

    def by_rows(a):
        return a.reshape(N_DEV, -1, PACK_W)

    small = jnp.concatenate([
        g["norm_a"].reshape(N_DEV, -1),
        g["conv_w"].reshape(CONV_WIDTH, N_DEV, -1).transpose(1, 0, 2).reshape(N_DEV, -1),
        g["conv_b"].reshape(N_DEV, -1), g["b_rg"].reshape(N_DEV, -1), g["b_ig"].reshape(N_DEV, -1),
        g["lru_lambda"].reshape(N_DEV, -1)], axis=1)
    small = jnp.pad(small, ((0, 0), (0, 8 * PACK_W - small.shape[1]))).reshape(N_DEV, 8, PACK_W)
    pieces = {"w_in_a": by_cols(g["w_in_a"]), "w_out_a": by_rows(g["w_out_a"]), "w_dkv": by_rows(g["w_dkv"]),
              "w_uk": by_rows(g["w_uk"]), "w_uv": by_rows(g["w_uv"]), "w_in_b": by_cols(g["w_in_b"]),
              "w_uq": by_rows(g["w_uq"]), "w_out_b": by_rows(g["w_out_b"]), "pad": jnp.zeros((N_DEV, 8, PACK_W), F32),
              "small": small}
    rep = _pack_rep(g).reshape(N_DEV, REP_SLICE, PACK_W)
    tail = jnp.zeros((N_DEV, GRAD_ROWS - SHARD_ROWS - REP_SLICE, PACK_W), F32)
    return jnp.concatenate([pieces[n] for n, _ in _SHARD_PIECES] + [rep, tail], axis=1)


def _step(x, target, w, rep, *, bsz, seq):
    t = bsz * seq
    cos, sin = _rope_tables(seq)
    g_a = w["norm_a"]
    g_kv = rep["norm_kv"].reshape(1, -1)
    g_kvn = rep["kv_norm"].reshape(1, -1)
    g_b = rep["norm_b"].reshape(1, -1)
    g_q = rep["q_norm"].reshape(1, -1)
    g_f = rep["final_norm"].reshape(1, -1)
    wrg = rep["w_rg"][0].astype(BF16)
    wig = rep["w_ig"][0].astype(BF16)
    cw8, vecs = w["conv_taps"], w["lru_vecs"]

    def seq3(a):
        return a.reshape(bsz, seq, a.shape[-1])

    def flat(a):
        return a.reshape(t, a.shape[-1])

    h0, xp, ga = _lru_proj_fwd(x, g_a, w["w_xp"], w["w_ga"], name="lru_proj_fwd")
    xb, hs, y = _lru_fwd(seq3(xp), seq3(ga), cw8, vecs, wrg, wig, name="lru_fwd")
    x1 = _matmul(flat(y), w["w_out_a"], residual=x, name="out_a")
    hk, hq, ck, cqp, g2, ckv, cq, q, kn, v, kr, kn_t, v_t, kr_t = _mla_proj_fwd(
        x1, (g_kv, g_b, g_kvn, g_q), w, cos, sin, seq=seq, name="mla_proj_fwd")
    o, lse = _attn_fwd(q, kn, kr, v_t, bsz=bsz, seq=seq, name="attn_fwd")
    loss, dx2, y2, do, dg2, dgf = _head_and_loss(o, g2, x1, target, w["w_out_b"], g_f, name="head_loss")
    grads = {"final_norm": dgf, "w_out_b": _matmul_tn(y2, dx2, name="d_w_out_b")}
    dq, dkn, dkr, dv = _attn_bwd(q, kn, kr, kn_t, kr_t, v, o, lse, do, cos, sin, bsz=bsz, seq=seq, name="attn_bwd")
    grads["w_uq"] = _matmul_tn(cq, dq, name="d_w_uq")
    dx1, du2, dckr, dgkv, dgb, dgkvn, dgq = _mla_proj_bwd(
        x1, dx2, cqp, ck, dq, dkn, dv, dkr, dg2, (g_kv, g_b, g_kvn, g_q), w, name="mla_proj_bwd")
    grads["norm_kv"], grads["norm_b"], grads["kv_norm"], grads["q_norm"] = dgkv, dgb, dgkvn, dgq
    grads["w_in_b"] = _matmul_tn(hq, du2, name="d_w_in_b")
    grads["w_uk"] = _matmul_tn(ckv, dkn, name="d_w_uk")
    grads["w_uv"] = _matmul_tn(ckv, dv, name="d_w_uv")
    grads["w_dkv"] = _matmul_tn(hk, dckr, name="d_w_dkv")[:, :KV_RANK + QK_ROPE]
    grads["w_out_a"] = _matmul_tn(flat(y), dx1, name="d_w_out_a")
    dy = _matmul(dx1, w["w_out_a"], nt=True, name="d_y")
    dxp, dga, dwrg, dwig, dvec = _lru_bwd(seq3(dy), seq3(xp), xb, hs, seq3(ga), cw8, vecs, wrg, wig, name="lru_bwd")
    dxp, dga = flat(dxp), flat(dga)
    grads["w_rg"], grads["w_ig"] = dwrg, dwig
    grads["b_rg"], grads["b_ig"], grads["conv_b"] = dvec[0], dvec[1], dvec[3]
    lam = vecs[3]
    grads["lru_lambda"] = dvec[2] * (-1.0 / (1.0 + jnp.exp(lam)))
    grads["conv_w"] = dvec[4:4 + CONV_WIDTH]
    grads["w_in_a"] = jnp.concatenate([_matmul_tn(h0, dxp, name="d_w_in_a_x"), _matmul_tn(h0, dga, name="d_w_in_a_g")], axis=1)
    dx, dga_norm = _lru_proj_bwd(dxp, dga, x, dx1, g_a, w["w_xp"], w["w_ga"], name="lru_proj_bwd")
    grads["norm_a"] = dga_norm
    return loss[0, 0], dx, grads


def kernel(x, norm_a, w_in_a, conv_w, conv_b, w_rg, b_rg, w_ig, b_ig, lru_lambda, w_out_a, norm_kv, w_dkv, kv_norm, w_uk, w_uv, norm_b, w_in_b, q_norm, w_uq, w_out_b, final_norm, loss_target, m_norm_a, m_w_in_a, m_conv_w, m_conv_b, m_w_rg, m_b_rg, m_w_ig, m_b_ig, m_lru_lambda, m_w_out_a, m_norm_kv, m_w_dkv, m_kv_norm, m_w_uk, m_w_uv, m_norm_b, m_w_in_b, m_q_norm, m_w_uq, m_w_out_b, m_final_norm, v_norm_a, v_w_in_a, v_conv_w, v_conv_b, v_w_rg, v_b_rg, v_w_ig, v_b_ig, v_lru_lambda, v_w_out_a, v_norm_kv, v_w_dkv, v_kv_norm, v_w_uk, v_w_uv, v_norm_b, v_w_in_b, v_q_norm, v_w_uq, v_w_out_b, v_final_norm):
    given = dict(locals())
    wts = {n: given[n] for n in WEIGHTS}
    mom1 = {n: given["m_" + n] for n in WEIGHTS}
    mom2 = {n: given["v_" + n] for n in WEIGHTS}
    bsz, seq, _ = x.shape
    t = bsz * seq

    w_pack = _pack_shard(wts)
    vec_bits = lax.bitcast_convert_type(w_pack[MATRIX_ROWS:], WIRE).reshape(16, PACK_W)
    send = jnp.concatenate([w_pack[:MATRIX_ROWS].astype(WIRE), vec_bits], axis=0)
    wall = _all_gather(send, name="gather_weights")
    w = _gathered_weights(wall)

    loss, dx, grads = _step(x.reshape(t, D_MODEL), loss_target.reshape(t, D_MODEL), w, wts, bsz=bsz, seq=seq)
    loss = lax.psum(loss, MESH_AXES)

    parts = _pack_grads(grads)
    from_sibling = _exchange_d2d(parts, name="exchange_grads_d2d")
    chip_parts = _chip_partial(parts, from_sibling, name="chip_partial_grads")
    landed = _exchange_ici(chip_parts, name="exchange_grads_ici")
    g_pack = _sum_parts(landed, name="sum_grads")
    g_own = _unpack_shard(g_pack[:SHARD_ROWS])
    rep_slice = g_pack[SHARD_ROWS:SHARD_ROWS + REP_SLICE]
    g_rep = _all_gather(rep_slice, name="gather_replicated").reshape(REP_ROWS, PACK_W)
    g_own.update(_unpack_rep(g_rep, wts))

    deltas, new_m, new_v = {}, {}, {}
    for n in WEIGHTS:
        deltas[n], new_m[n], new_v[n] = _adamw(g_own[n], wts[n], mom1[n], mom2[n], name="adamw_" + n)
    result = [loss, dx.reshape(bsz, seq, D_MODEL)]
    for d in (g_own, deltas, new_m, new_v):
        result.extend(d[n] for n in WEIGHTS)
    return tuple(result)
```

```python
import jax
import jax.numpy as jnp
from jax import lax
from jax.experimental import pallas as pl
from jax.experimental.pallas import tpu as pltpu

F32 = jnp.float32
BF16 = jnp.bfloat16
WIRE = jnp.bfloat16

D_MODEL = 1024
D_RNN = 1280
RNN_BLOCKS = 10
RNN_BW = 128
CONV_WIDTH = 4
LRU_C = 8.0
N_HEADS = 8
QK_NOPE = 128
QK_ROPE = 64
V_DIM = 128
KV_RANK = 256
Q_RANK = 384
ROPE_THETA = 10000.0
EPS = 1e-6
ATTN_SCALE = (QK_NOPE + QK_ROPE) ** -0.5
HEAD_PAD = 256
LANES = 128

ADAM_LR = 0.001
ADAM_B1 = 0.9
ADAM_B2 = 0.999
ADAM_EPS = 1e-08
ADAM_WD = 0.01
ADAM_STEP = 10

N_DEV = 8
MESH_AXES = ("x", "y", "c")
VMEM_LIMIT_BYTES = 56 * 2**20
PACK_W = 1024

_SHARD_PIECES = (("w_in_a", 320), ("w_out_a", 160), ("w_dkv", 40), ("w_uk", 32), ("w_uv", 32),
                 ("w_in_b", 176), ("w_uq", 96), ("w_out_b", 128), ("pad", 8), ("small", 8))
_SHARD_OFF = {}
_r = 0
for _n, _k in _SHARD_PIECES:
    _SHARD_OFF[_n] = (_r, _r + _k)
    _r += _k
SHARD_ROWS = _r
MATRIX_ROWS = _SHARD_OFF["small"][0]
_SMALL = (("norm_a", 128), ("conv_w", 640), ("conv_b", 160), ("b_rg", 160), ("b_ig", 160), ("lru_lambda", 160))
_REP = (("w_rg", 163840), ("w_ig", 163840), ("norm_kv", 1024), ("kv_norm", 256), ("norm_b", 1024),
        ("q_norm", 384), ("final_norm", 1024))
REP_ROWS = 384
REP_SLICE = REP_ROWS // N_DEV
GRAD_ROWS = SHARD_ROWS + REP_SLICE + 8
GRAD_BLOCK = 352

WEIGHTS = ("norm_a", "w_in_a", "conv_w", "conv_b", "w_rg", "b_rg", "w_ig", "b_ig", "lru_lambda", "w_out_a",
           "norm_kv", "w_dkv", "kv_norm", "w_uk", "w_uv", "norm_b", "w_in_b", "q_norm", "w_uq", "w_out_b",
           "final_norm")


def _params(sem=None):
    return pltpu.CompilerParams(dimension_semantics=sem, vmem_limit_bytes=VMEM_LIMIT_BYTES)


_NT = (((1,), (1,)), ((), ()))


def _sigmoid(z):
    return 0.5 * jnp.tanh(0.5 * z) + 0.5


def _col_block(n):
    return n if n <= 1408 else n // 2


def _matmul(a, b, *, name, nt=False, out_dtype=F32, residual=None, bm=512):
    m, k = a.shape
    n = b.shape[0] if nt else b.shape[1]
    bm = min(bm, m)
    bn = _col_block(n)
    dims = (((1,), (1,)), ((), ())) if nt else (((1,), (0,)), ((), ()))
    has_res = residual is not None

    def body(*refs):
        a_ref, b_ref, o_ref = refs[0], refs[1], refs[-1]
        acc = lax.dot_general(a_ref[...].astype(BF16), b_ref[...].astype(BF16), dims, preferred_element_type=F32)
        if has_res:
            acc = acc + refs[2][...]
        o_ref[...] = acc.astype(out_dtype)

    in_specs = [pl.BlockSpec((bm, k), lambda i, j: (i, 0)),
                pl.BlockSpec((bn, k), lambda i, j: (j, 0)) if nt else pl.BlockSpec((k, bn), lambda i, j: (0, j))]
    args = [a, b]
    if has_res:
        in_specs.append(pl.BlockSpec((bm, bn), lambda i, j: (i, j)))
        args.append(residual)
    return pl.pallas_call(
        body, grid=(m // bm, n // bn), in_specs=in_specs, out_specs=pl.BlockSpec((bm, bn), lambda i, j: (i, j)),
        out_shape=jax.ShapeDtypeStruct((m, n), out_dtype), compiler_params=_params(("parallel", "parallel")),
        name=name)(*args)


def _matmul_tn(a, b, *, name, bt=512):
    t, m = a.shape
    n = b.shape[1]
    bt = min(bt, t)
    bm, bn = _col_block(m), _col_block(n)

    def body(a_ref, b_ref, o_ref):
        @pl.when(pl.program_id(2) == 0)
        def _():
            o_ref[...] = jnp.zeros_like(o_ref)

        o_ref[...] += lax.dot_general(a_ref[...].astype(BF16), b_ref[...].astype(BF16),
                                      (((0,), (0,)), ((), ())), preferred_element_type=F32)

    return pl.pallas_call(
        body, grid=(m // bm, n // bn, t // bt),
        in_specs=[pl.BlockSpec((bt, bm), lambda i, j, s: (s, i)), pl.BlockSpec((bt, bn), lambda i, j, s: (s, j))],
        out_specs=pl.BlockSpec((bm, bn), lambda i, j, s: (i, j)),
        out_shape=jax.ShapeDtypeStruct((m, n), F32),
        compiler_params=_params(("parallel", "parallel", "arbitrary")), name=name)(a, b)


def _swap_halves(v):
    ax = v.ndim - 1
    lane = lax.broadcasted_iota(jnp.int32, v.shape, ax)
    up = pltpu.roll(v, LANES - QK_ROPE // 2, axis=ax)
    down = pltpu.roll(v, QK_ROPE // 2, axis=ax)
    return jnp.where(lane < QK_ROPE // 2, up, jnp.where(lane < QK_ROPE, down, 0.0))


def _rope(v, cos, sin):
    return v * cos + _swap_halves(v) * sin


def _rope_t(d, cos, sin):
    return d * cos + _swap_halves(d * sin)


def _rope_tables(seq):
    pos = jnp.arange(seq, dtype=F32)
    inv = ROPE_THETA ** (-jnp.arange(0, QK_ROPE, 2, dtype=F32) / QK_ROPE)
    ang = pos[:, None] * inv[None, :]
    cos, sin = jnp.cos(ang), jnp.sin(ang)
    zero = jnp.zeros((seq, LANES - QK_ROPE), F32)
    return jnp.concatenate([cos, cos, zero], axis=1), jnp.concatenate([-sin, sin, zero], axis=1)


def _rms(v):
    return v * lax.rsqrt(jnp.mean(v * v, axis=-1, keepdims=True) + EPS)


def _const_spec(a):
    return pl.BlockSpec(a.shape, lambda i: (0,) * a.ndim)


def _lru_proj_fwd(x, g_a, w_xp, w_ga, *, name, bt=256):
    t, d = x.shape
    bt = min(bt, t)
    n = w_xp.shape[1]

    def body(x_ref, g_ref, wx_ref, wg_ref, h_ref, xp_ref, ga_ref):
        h = (_rms(x_ref[...]) * g_ref[...]).astype(BF16)
        h_ref[...] = h
        xp_ref[...] = jnp.dot(h, wx_ref[...], preferred_element_type=F32)
        ga_ref[...] = jnp.dot(h, wg_ref[...], preferred_element_type=F32)

    row = lambda w: pl.BlockSpec((bt, w), lambda i: (i, 0))
    return pl.pallas_call(
        body, grid=(t // bt,), in_specs=[row(d), _const_spec(g_a), _const_spec(w_xp), _const_spec(w_ga)],
        out_specs=[row(d), row(n), row(n)],
        out_shape=[jax.ShapeDtypeStruct((t, d), BF16), jax.ShapeDtypeStruct((t, n), F32), jax.ShapeDtypeStruct((t, n), F32)],
        compiler_params=_params(("parallel",)), name=name)(x, g_a, w_xp, w_ga)


def _mla_proj_fwd(x1, gains, w, cos, sin, *, seq, name, bt=256):
    t, d = x1.shape
    bt = min(bt, seq)
    per_seq = seq // bt
    g_kv, g_b, g_kvn, g_q = gains
    consts = [g_kv, g_b, g_kvn, g_q, w["w_dkv_c"], w["w_dkv_r"], w["w_cq"], w["w_g2"], w["w_uk"], w["w_uv"],
              w["w_uk_t"], w["w_uv_t"], w["w_uq"]]

    def body(x_ref, cos_ref, sin_ref, gkv_ref, gb_ref, gkvn_ref, gq_ref, wdc_ref, wdr_ref, wcq_ref, wg2_ref,
             wuk_ref, wuv_ref, wukt_ref, wuvt_ref, wuq_ref,
             hk_ref, hq_ref, ck_ref, cqp_ref, g2_ref, ckv_ref, cq_ref, q_ref, kn_ref, v_ref, kr_ref, knt_ref, vt_ref, krt_ref):
        nrm = _rms(x_ref[...])
        hk = (nrm * gkv_ref[...]).astype(BF16)
        hq = (nrm * gb_ref[...]).astype(BF16)
        hk_ref[...] = hk
        hq_ref[...] = hq
        ck = jnp.dot(hk, wdc_ref[...], preferred_element_type=F32)
        ck_ref[...] = ck
        cqp = jnp.dot(hq, wcq_ref[...], preferred_element_type=F32)
        cqp_ref[...] = cqp
        g2_ref[...] = jnp.dot(hq, wg2_ref[...], preferred_element_type=F32)
        cosv, sinv = cos_ref[...], sin_ref[...]
        kr = _rope(jnp.dot(hk, wdr_ref[...], preferred_element_type=F32), cosv, sinv)
        kr_ref[...] = kr.astype(BF16)
        krt_ref[...] = kr.T.astype(BF16)
        ckv = (_rms(ck) * gkvn_ref[...]).astype(BF16)
        ckv_ref[...] = ckv
        kn_ref[...] = jnp.dot(ckv, wuk_ref[...], preferred_element_type=F32).astype(BF16)
        v_ref[...] = jnp.dot(ckv, wuv_ref[...], preferred_element_type=F32).astype(BF16)
        knt_ref[...] = lax.dot_general(wukt_ref[...], ckv, _NT, preferred_element_type=F32).astype(BF16)
        vt_ref[...] = lax.dot_general(wuvt_ref[...], ckv, _NT, preferred_element_type=F32).astype(BF16)
        cq = (_rms(cqp) * gq_ref[...]).astype(BF16)
        cq_ref[...] = cq
        for h in range(N_HEADS):
            qh = jnp.dot(cq, wuq_ref[:, h * HEAD_PAD:(h + 1) * HEAD_PAD], preferred_element_type=F32)
            q_ref[:, h * HEAD_PAD:h * HEAD_PAD + QK_NOPE] = qh[:, :QK_NOPE].astype(BF16)
            q_ref[:, h * HEAD_PAD + QK_NOPE:(h + 1) * HEAD_PAD] = _rope(qh[:, QK_NOPE:], cosv, sinv).astype(BF16)

    row = lambda w_: pl.BlockSpec((bt, w_), lambda i: (i, 0))
    col = lambda h_: pl.BlockSpec((h_, bt), lambda i: (0, i))
    tab = pl.BlockSpec((bt, LANES), lambda i: (i % per_seq, 0))
    nh = N_HEADS * V_DIM
    shapes = [((t, d), BF16), ((t, d), BF16), ((t, KV_RANK), F32), ((t, Q_RANK), F32), ((t, nh), F32), ((t, KV_RANK), BF16),
              ((t, Q_RANK), BF16), ((t, N_HEADS * HEAD_PAD), BF16), ((t, nh), BF16), ((t, nh), BF16), ((t, LANES), BF16),
              ((nh, t), BF16), ((nh, t), BF16), ((LANES, t), BF16)]
    out_specs = [row(d), row(d), row(KV_RANK), row(Q_RANK), row(nh), row(KV_RANK), row(Q_RANK), row(N_HEADS * HEAD_PAD),
                 row(nh), row(nh), row(LANES), col(nh), col(nh), col(LANES)]
    return pl.pallas_call(
        body, grid=(t // bt,), in_specs=[row(d), tab, tab] + [_const_spec(a) for a in consts], out_specs=out_specs,
        out_shape=[jax.ShapeDtypeStruct(s, dt) for s, dt in shapes],
        compiler_params=_params(("parallel",)), name=name)(x1, cos, sin, *consts)


def _rms_bwd_rows(xv, dn):
    r = lax.rsqrt(jnp.mean(xv * xv, axis=-1, keepdims=True) + EPS)
    nrm = xv * r
    return r * (dn - nrm * jnp.mean(dn * nrm, axis=-1, keepdims=True)), nrm


def _col_sum(v):
    return jnp.sum(v, axis=0, keepdims=True)


def _lru_proj_bwd(dxp, dga, x, dx1, g_a, w_xp, w_ga, *, name, bt=256):
    t, d = x.shape
    bt = min(bt, t)
    n = w_xp.shape[1]

    def body(dxp_ref, dga_ref, x_ref, dx1_ref, g_ref, wx_ref, wg_ref, dx_ref, dg_ref):
        @pl.when(pl.program_id(0) == 0)
        def _():
            dg_ref[...] = jnp.zeros_like(dg_ref)

        dh = (lax.dot_general(dxp_ref[...], wx_ref[...], _NT, preferred_element_type=F32)
              + lax.dot_general(dga_ref[...], wg_ref[...], _NT, preferred_element_type=F32))
        dxn, nrm = _rms_bwd_rows(x_ref[...], dh * g_ref[...])
        dg_ref[...] += _col_sum(dh * nrm)
        dx_ref[...] = dx1_ref[...] + dxn

    row = lambda w: pl.BlockSpec((bt, w), lambda i: (i, 0))
    return pl.pallas_call(
        body, grid=(t // bt,),
        in_specs=[row(n), row(n), row(d), row(d), _const_spec(g_a), _const_spec(w_xp), _const_spec(w_ga)],
        out_specs=[row(d), _const_spec(g_a)],
        out_shape=[jax.ShapeDtypeStruct((t, d), F32), jax.ShapeDtypeStruct((1, d), F32)],
        compiler_params=_params(("arbitrary",)), name=name)(dxp, dga, x, dx1, g_a, w_xp, w_ga)


def _mla_proj_bwd(x1, dx2, cqp, ck, dq, dkn, dv, dkr, dg2, gains, w, *, name, bt=256):
    t, d = x1.shape
    bt = min(bt, t)
    g_kv, g_b, g_kvn, g_q = gains
    consts = [g_kv, g_b, g_kvn, g_q, w["w_dkv_c"], w["w_dkv_r"], w["w_cq"], w["w_g2"], w["w_uk"], w["w_uv"], w["w_uq"]]
    nh = N_HEADS * V_DIM

    def body(x1_ref, dx2_ref, cqp_ref, ck_ref, dq_ref, dkn_ref, dv_ref, dkr_ref, dg2_ref,
             gkv_ref, gb_ref, gkvn_ref, gq_ref, wdc_ref, wdr_ref, wcq_ref, wg2_ref, wuk_ref, wuv_ref, wuq_ref,
             dx1_ref, du2_ref, dckr_ref, dgkv_ref, dgb_ref, dgkvn_ref, dgq_ref):
        @pl.when(pl.program_id(0) == 0)
        def _():
            for ref in (dgkv_ref, dgb_ref, dgkvn_ref, dgq_ref):
                ref[...] = jnp.zeros_like(ref)

        dot_nt = lambda a, b: lax.dot_general(a, b, _NT, preferred_element_type=F32)
        dcq = dot_nt(dq_ref[...], wuq_ref[...])
        dcqp, nq = _rms_bwd_rows(cqp_ref[...], dcq * gq_ref[...])
        dgq_ref[...] += _col_sum(dcq * nq)
        dcqp = dcqp.astype(BF16)
        dg2 = dg2_ref[...]
        du2_ref[:, :Q_RANK] = dcqp
        du2_ref[:, Q_RANK:] = dg2
        dhq = dot_nt(dcqp, wcq_ref[...]) + dot_nt(dg2, wg2_ref[...])
        dckv = dot_nt(dkn_ref[...], wuk_ref[...]) + dot_nt(dv_ref[...], wuv_ref[...])
        dck, nc = _rms_bwd_rows(ck_ref[...], dckv * gkvn_ref[...])
        dgkvn_ref[...] += _col_sum(dckv * nc)
        dck = dck.astype(BF16)
        dkr = dkr_ref[...].astype(BF16)
        dckr_ref[:, :KV_RANK] = dck
        dckr_ref[:, KV_RANK:] = dkr
        dhk = dot_nt(dck, wdc_ref[...]) + dot_nt(dkr, wdr_ref[...])
        dxn, n1 = _rms_bwd_rows(x1_ref[...], dhq * gb_ref[...] + dhk * gkv_ref[...])
        dgb_ref[...] += _col_sum(dhq * n1)
        dgkv_ref[...] += _col_sum(dhk * n1)
        dx1_ref[...] = dx2_ref[...] + dxn

    row = lambda w_: pl.BlockSpec((bt, w_), lambda i: (i, 0))
    vec = lambda w_: pl.BlockSpec((1, w_), lambda i: (0, 0))
    in_specs = [row(d), row(d), row(Q_RANK), row(KV_RANK), row(N_HEADS * HEAD_PAD), row(nh), row(nh), row(LANES), row(nh)]
    return pl.pallas_call(
        body, grid=(t // bt,), in_specs=in_specs + [_const_spec(a) for a in consts],
        out_specs=[row(d), row(Q_RANK + nh), row(KV_RANK + LANES), vec(d), vec(d), vec(KV_RANK), vec(Q_RANK)],
        out_shape=[jax.ShapeDtypeStruct((t, d), F32), jax.ShapeDtypeStruct((t, Q_RANK + nh), BF16),
                   jax.ShapeDtypeStruct((t, KV_RANK + LANES), BF16), jax.ShapeDtypeStruct((1, d), F32),
                   jax.ShapeDtypeStruct((1, d), F32), jax.ShapeDtypeStruct((1, KV_RANK), F32),
                   jax.ShapeDtypeStruct((1, Q_RANK), F32)],
        compiler_params=_params(("arbitrary",)), name=name)(x1, dx2, cqp, ck, dq, dkn, dv, dkr, dg2, *consts)


def _softplus(z):
    return jnp.maximum(z, 0.0) + jnp.log1p(jnp.exp(-jnp.abs(z)))


def _neg_expm1(z):
    series = -z * (1.0 + z * (1.0 / 2) * (1.0 + z * (1.0 / 3) * (1.0 + z * (1.0 / 4))))
    return jnp.where(z > -0.01, series, 1.0 - jnp.exp(z))


def _gates(xb, wrg, wig, brg, big, sp):
    xbb = xb.astype(BF16)
    r = _sigmoid(jnp.dot(xbb, wrg, preferred_element_type=F32) + brg)
    i = _sigmoid(jnp.dot(xbb, wig, preferred_element_type=F32) + big)
    la = (-LRU_C) * r * sp
    a = jnp.exp(la)
    mult = jnp.sqrt(_neg_expm1(2.0 * la))
    return r, i, a, mult


def _conv(xpad_ref, cw_ref, seq):
    acc = cw_ref[0:1, :] * xpad_ref[pl.ds(8 - (CONV_WIDTH - 1), seq), :]
    for k in range(1, CONV_WIDTH):
        acc = acc + cw_ref[k:k + 1, :] * xpad_ref[pl.ds(8 - (CONV_WIDTH - 1) + k, seq), :]
    return acc


def _seq_spec(seq):
    return pl.BlockSpec((None, seq, RNN_BW), lambda n, b: (b, 0, n))


def _chan_spec(rows):
    return pl.BlockSpec((rows, RNN_BW), lambda n, b: (0, n))


_GATE_W_SPEC = pl.BlockSpec((None, RNN_BW, RNN_BW), lambda n, b: (n, 0, 0))


def _lru_fwd(xp, ga, cw, vecs, wrg, wig, *, name):
    bsz, seq, _ = xp.shape
    groups = seq // 8

    def body(xp_ref, ga_ref, cw_ref, vec_ref, wrg_ref, wig_ref, xb_ref, hs_ref, y_ref, xpad, a_s, b_s):
        xpad[0:8, :] = jnp.zeros((8, RNN_BW), F32)
        xpad[pl.ds(8, seq), :] = xp_ref[...]
        xb = _conv(xpad, cw_ref, seq) + vec_ref[0:1, :]
        xb_ref[...] = xb
        sp = _softplus(-vec_ref[3:4, :])
        _, i, a, mult = _gates(xb, wrg_ref[...], wig_ref[...], vec_ref[1:2, :], vec_ref[2:3, :], sp)
        a_s[...] = a
        b_s[...] = mult * (i * xb)
        row = lax.broadcasted_iota(jnp.int32, (8, RNN_BW), 0)

        def group(g, h):
            r0 = pl.multiple_of(g * 8, 8)
            av = a_s[pl.ds(r0, 8), :]
            bv = b_s[pl.ds(r0, 8), :]
            for k in (1, 2, 4):
                m = row >= k
                bv = jnp.where(m, av * pltpu.roll(bv, k, axis=0) + bv, bv)
                av = jnp.where(m, av * pltpu.roll(av, k, axis=0), av)
            rows = av * h + bv
            hs_ref[pl.ds(r0, 8), :] = rows
            return rows[7:8, :]

        lax.fori_loop(0, groups, group, jnp.zeros((1, RNN_BW), F32))
        gav = ga_ref[...]
        y_ref[...] = (hs_ref[...] * (gav * _sigmoid(gav))).astype(BF16)

    sq = _seq_spec(seq)
    shape = (bsz, seq, D_RNN)
    return pl.pallas_call(
        body, grid=(RNN_BLOCKS, bsz), in_specs=[sq, sq, _chan_spec(8), _chan_spec(8), _GATE_W_SPEC, _GATE_W_SPEC],
        out_specs=[sq, sq, sq],
        out_shape=[jax.ShapeDtypeStruct(shape, F32), jax.ShapeDtypeStruct(shape, F32), jax.ShapeDtypeStruct(shape, BF16)],
        scratch_shapes=[pltpu.VMEM((seq + 8, RNN_BW), F32), pltpu.VMEM((seq, RNN_BW), F32), pltpu.VMEM((seq, RNN_BW), F32)],
        compiler_params=_params(("parallel", "parallel")), name=name)(xp, ga, cw, vecs, wrg, wig)


def _lru_bwd(dy, xp, xb, hs, ga, cw, vecs, wrg, wig, *, name):
    bsz, seq, _ = xp.shape
    groups = seq // 8

    def body(dy_ref, xp_ref, xb_ref, hs_ref, ga_ref, cw_ref, vec_ref, wrg_ref, wig_ref,
             dxp_ref, dga_ref, dwrg_ref, dwig_ref, dvec_ref, pad, a_s, d_s, lam_s):
        @pl.when(pl.program_id(1) == 0)
        def _():
            dwrg_ref[...] = jnp.zeros_like(dwrg_ref)
            dwig_ref[...] = jnp.zeros_like(dwig_ref)
            dvec_ref[...] = jnp.zeros_like(dvec_ref)

        xb = xb_ref[...]
        hs = hs_ref[...]
        gav = ga_ref[...]
        dy = dy_ref[...]
        sp = _softplus(-vec_ref[3:4, :])
        wrg = wrg_ref[...]
        wig = wig_ref[...]
        r, i, a, mult = _gates(xb, wrg, wig, vec_ref[1:2, :], vec_ref[2:3, :], sp)
        sg = _sigmoid(gav)
        dga_ref[...] = (dy * hs * (sg * (1.0 + gav * (1.0 - sg)))).astype(BF16)
        d_s[...] = dy * (gav * sg)

        pad[pl.ds(0, seq), :] = a
        pad[pl.ds(seq, 8), :] = jnp.zeros((8, RNN_BW), F32)
        a_s[...] = pad[pl.ds(1, seq), :]
        row = lax.broadcasted_iota(jnp.int32, (8, RNN_BW), 0)

        def group(g, nxt):
            r0 = pl.multiple_of((groups - 1 - g) * 8, 8)
            cv = a_s[pl.ds(r0, 8), :]
            bv = d_s[pl.ds(r0, 8), :]
            for k in (1, 2, 4):
                m = row < 8 - k
                bv = jnp.where(m, cv * pltpu.roll(bv, 8 - k, axis=0) + bv, bv)
                cv = jnp.where(m, cv * pltpu.roll(cv, 8 - k, axis=0), cv)
            rows = cv * nxt + bv
            lam_s[pl.ds(r0, 8), :] = rows
            return rows[0:1, :]

        lax.fori_loop(0, groups, group, jnp.zeros((1, RNN_BW), F32))
        dh = lam_s[...]

        pad[0:8, :] = jnp.zeros((8, RNN_BW), F32)
        pad[pl.ds(8, seq), :] = hs
        da = dh * pad[pl.ds(7, seq), :]
        ixb = i * xb
        dixb = dh * mult
        dla = da * a - (dh * ixb) * (a * a) / mult
        drp = (dla * ((-LRU_C) * sp)) * r * (1.0 - r)
        dip = (dixb * xb) * i * (1.0 - i)
        dvec_ref[0:1, :] += jnp.sum(drp, axis=0, keepdims=True)
        dvec_ref[1:2, :] += jnp.sum(dip, axis=0, keepdims=True)
        dvec_ref[2:3, :] += jnp.sum(dla * ((-LRU_C) * r), axis=0, keepdims=True)
        drpb = drp.astype(BF16)
        dipb = dip.astype(BF16)
        xbb = xb.astype(BF16)
        nt = (((1,), (1,)), ((), ()))
        tn = (((0,), (0,)), ((), ()))
        dxb = (dixb * i
               + lax.dot_general(drpb, wrg, nt, preferred_element_type=F32)
               + lax.dot_general(dipb, wig, nt, preferred_element_type=F32))
        dwrg_ref[...] += lax.dot_general(xbb, drpb, tn, preferred_element_type=F32)
        dwig_ref[...] += lax.dot_general(xbb, dipb, tn, preferred_element_type=F32)
        dvec_ref[3:4, :] += jnp.sum(dxb, axis=0, keepdims=True)

        pad[pl.ds(0, seq), :] = dxb
        pad[pl.ds(seq, 8), :] = jnp.zeros((8, RNN_BW), F32)
        dxp = cw_ref[0:1, :] * pad[pl.ds(CONV_WIDTH - 1, seq), :]
        for k in range(1, CONV_WIDTH):
            dxp = dxp + cw_ref[k:k + 1, :] * pad[pl.ds(CONV_WIDTH - 1 - k, seq), :]
        dxp_ref[...] = dxp.astype(BF16)
        pad[0:8, :] = jnp.zeros((8, RNN_BW), F32)
        pad[pl.ds(8, seq), :] = xp_ref[...]
        for k in range(CONV_WIDTH):
            dvec_ref[4 + k:5 + k, :] += jnp.sum(dxb * pad[pl.ds(8 - (CONV_WIDTH - 1) + k, seq), :], axis=0, keepdims=True)

    sq = _seq_spec(seq)
    shape = (bsz, seq, D_RNN)
    gshape = (RNN_BLOCKS, RNN_BW, RNN_BW)
    return pl.pallas_call(
        body, grid=(RNN_BLOCKS, bsz),
        in_specs=[sq, sq, sq, sq, sq, _chan_spec(8), _chan_spec(8), _GATE_W_SPEC, _GATE_W_SPEC],
        out_specs=[sq, sq, _GATE_W_SPEC, _GATE_W_SPEC, _chan_spec(8)],
        out_shape=[jax.ShapeDtypeStruct(shape, BF16), jax.ShapeDtypeStruct(shape, BF16),
                   jax.ShapeDtypeStruct(gshape, F32), jax.ShapeDtypeStruct(gshape, F32),
                   jax.ShapeDtypeStruct((8, D_RNN), F32)],
        scratch_shapes=[pltpu.VMEM((seq + 8, RNN_BW), F32), pltpu.VMEM((seq, RNN_BW), F32),
                        pltpu.VMEM((seq, RNN_BW), F32), pltpu.VMEM((seq, RNN_BW), F32)],
        compiler_params=_params(("parallel", "arbitrary")), name=name)(dy, xp, xb, hs, ga, cw, vecs, wrg, wig)


def _attn_block(seq):
    return min(512, seq)


def _diag_mask(blk):
    return lax.broadcasted_iota(jnp.int32, (blk, blk), 0) <= lax.broadcasted_iota(jnp.int32, (blk, blk), 1)


FWD_HEADS = 4
BWD_HEADS = 2


def _attn_fwd(q, kn, kr, v_t, *, bsz, seq, name):
    t = bsz * seq
    blk = _attn_block(seq)
    nq = seq // blk
    hg = FWD_HEADS

    def body(q_ref, kn_ref, kr_ref, vt_ref, o_ref, lse_ref, acc):
        qi = pl.program_id(2)
        acc[...] = jnp.zeros_like(acc)

        def step(j, carry, diagonal):
            k0 = pl.multiple_of(j * blk, blk)
            kr_j = kr_ref[pl.ds(k0, blk), :]
            out = []
            for h in range(hg):
                m_i, l_i = carry[h]
                kv = jnp.concatenate([kn_ref[pl.ds(k0, blk), h * QK_NOPE:(h + 1) * QK_NOPE], kr_j], axis=1)
                qv = q_ref[:, h * HEAD_PAD:(h + 1) * HEAD_PAD]
                s = lax.dot_general(kv, qv, _NT, preferred_element_type=F32) * ATTN_SCALE
                if diagonal:
                    s = jnp.where(_diag_mask(blk), s, -jnp.inf)
                m_new = jnp.maximum(m_i, jnp.max(s, axis=0, keepdims=True))
                p = jnp.exp(s - m_new)
                alpha = jnp.exp(m_i - m_new)
                l_new = alpha * l_i + jnp.sum(p, axis=0, keepdims=True)
                acc[h] = alpha * acc[h] + jnp.dot(vt_ref[h * V_DIM:(h + 1) * V_DIM, pl.ds(k0, blk)], p.astype(BF16),
                                                  preferred_element_type=F32)
                out.append((m_new, l_new))
            return tuple(out)

        init = tuple((jnp.full((1, blk), -jnp.inf, F32), jnp.zeros((1, blk), F32)) for _ in range(hg))
        carry = lax.fori_loop(0, qi, lambda j, c: step(j, c, False), init)
        stats = step(qi, carry, True)
        for h in range(hg):
            m_i, l_i = stats[h]
            o_ref[:, h * V_DIM:(h + 1) * V_DIM] = (acc[h] / l_i).T
            lse_ref[h] = m_i + jnp.log(l_i)

    return pl.pallas_call(
        body, grid=(bsz, N_HEADS // hg, nq),
        in_specs=[pl.BlockSpec((blk, hg * HEAD_PAD), lambda b, g, i: (b * nq + i, g)),
                  pl.BlockSpec((seq, hg * QK_NOPE), lambda b, g, i: (b, g)),
                  pl.BlockSpec((seq, LANES), lambda b, g, i: (b, 0)),
                  pl.BlockSpec((hg * V_DIM, seq), lambda b, g, i: (g, b))],
        out_specs=[pl.BlockSpec((blk, hg * V_DIM), lambda b, g, i: (b * nq + i, g)),
                   pl.BlockSpec((hg, 1, blk), lambda b, g, i: (g, 0, b * nq + i))],
        out_shape=[jax.ShapeDtypeStruct((t, N_HEADS * V_DIM), F32), jax.ShapeDtypeStruct((N_HEADS, 1, t), F32)],
        scratch_shapes=[pltpu.VMEM((hg, V_DIM, blk), F32)],
        compiler_params=_params(("parallel", "parallel", "parallel")), name=name)(q, kn, kr, v_t)


def _attn_bwd(q, kn, kr, kn_t, kr_t, v, o, lse, do, cos, sin, *, bsz, seq, name):
    t = bsz * seq
    blk = _attn_block(seq)
    nq = seq // blk
    hg = BWD_HEADS

    def body(q_ref, kn_ref, kr_ref, knt_ref, krt_ref, v_ref, o_ref, lse_ref, do_ref, cos_ref, sin_ref,
             dq_ref, dkn_ref, dkr_ref, dv_ref, dqt_acc, dk_acc, dv_acc):
        dqt_acc[...] = jnp.zeros_like(dqt_acc)
        dk_acc[...] = jnp.zeros_like(dk_acc)
        dv_acc[...] = jnp.zeros_like(dv_acc)

        def q_block(i, _):
            q0 = pl.multiple_of(i * blk, blk)
            rows = []
            for h in range(hg):
                dov = do_ref[pl.ds(q0, blk), h * V_DIM:(h + 1) * V_DIM].astype(F32)
                dcol = jnp.sum(dov * o_ref[pl.ds(q0, blk), h * V_DIM:(h + 1) * V_DIM], axis=-1, keepdims=True)
                delta = jnp.broadcast_to(dcol, (blk, LANES)).T[0:1, :]
                rows.append((lse_ref[h, :, pl.ds(q0, blk)], delta))

            def pair(j, diagonal):
                k0 = pl.multiple_of(j * blk, blk)
                kr_j = kr_ref[pl.ds(k0, blk), :]
                krt_j = krt_ref[:, pl.ds(k0, blk)]
                for h in range(hg):
                    lse_i, delta = rows[h]
                    qv = q_ref[pl.ds(q0, blk), h * HEAD_PAD:(h + 1) * HEAD_PAD]
                    dov = do_ref[pl.ds(q0, blk), h * V_DIM:(h + 1) * V_DIM]
                    kv = jnp.concatenate([kn_ref[pl.ds(k0, blk), h * QK_NOPE:(h + 1) * QK_NOPE], kr_j], axis=1)
                    s = lax.dot_general(kv, qv, _NT, preferred_element_type=F32) * ATTN_SCALE
                    p = jnp.exp(s - lse_i)
                    if diagonal:
                        p = jnp.where(_diag_mask(blk), p, 0.0)
                    dv_acc[pl.ds(k0, blk), h * V_DIM:(h + 1) * V_DIM] += jnp.dot(
                        p.astype(BF16), dov, preferred_element_type=F32)
                    dp = lax.dot_general(v_ref[pl.ds(k0, blk), h * V_DIM:(h + 1) * V_DIM], dov, _NT,
                                         preferred_element_type=F32)
                    ds = (p * (dp - delta) * ATTN_SCALE).astype(BF16)
                    dk_acc[pl.ds(k0, blk), h * HEAD_PAD:(h + 1) * HEAD_PAD] += jnp.dot(ds, qv, preferred_element_type=F32)
                    base = h * HEAD_PAD
                    dqt_acc[base:base + QK_NOPE, pl.ds(q0, blk)] += jnp.dot(
                        knt_ref[h * QK_NOPE:(h + 1) * QK_NOPE, pl.ds(k0, blk)], ds, preferred_element_type=F32)
                    dqt_acc[base + QK_NOPE:base + HEAD_PAD, pl.ds(q0, blk)] += jnp.dot(
                        krt_j, ds, preferred_element_type=F32)

            def off_diagonal(j, _):
                pair(j, False)
                return 0

            lax.fori_loop(0, i, off_diagonal, 0)
            pair(i, True)
            return 0

        lax.fori_loop(0, nq, q_block, 0)
        dkr = jnp.zeros((seq, LANES), F32)
        for h in range(hg):
            base = h * HEAD_PAD
            for i in range(nq):
                rows = slice(i * blk, (i + 1) * blk)
                dq = dqt_acc[base:base + HEAD_PAD, rows].T
                dq_ref[rows, base:base + QK_NOPE] = dq[:, :QK_NOPE].astype(BF16)
                dq_ref[rows, base + QK_NOPE:base + HEAD_PAD] = _rope_t(
                    dq[:, QK_NOPE:], cos_ref[rows, :], sin_ref[rows, :]).astype(BF16)
            dkn_ref[:, h * QK_NOPE:(h + 1) * QK_NOPE] = dk_acc[:, base:base + QK_NOPE].astype(BF16)
            dkr = dkr + dk_acc[:, base + QK_NOPE:base + HEAD_PAD]
        dv_ref[...] = dv_acc[...].astype(BF16)

        @pl.when(pl.program_id(1) == 0)
        def _():
            dkr_ref[...] = jnp.zeros_like(dkr_ref)

        dkr_ref[...] += _rope_t(dkr, cos_ref[...], sin_ref[...])

    head = pl.BlockSpec((seq, hg * V_DIM), lambda b, g: (b, g))
    head_t = pl.BlockSpec((hg * V_DIM, seq), lambda b, g: (g, b))
    shared = pl.BlockSpec((seq, LANES), lambda b, g: (b, 0))
    shared_t = pl.BlockSpec((LANES, seq), lambda b, g: (0, b))
    table = pl.BlockSpec((seq, LANES), lambda b, g: (0, 0))
    qspec = pl.BlockSpec((seq, hg * HEAD_PAD), lambda b, g: (b, g))
    return pl.pallas_call(
        body, grid=(bsz, N_HEADS // hg),
        in_specs=[qspec, head, shared, head_t, shared_t, head, head,
                  pl.BlockSpec((hg, 1, seq), lambda b, g: (g, 0, b)), head, table, table],
        out_specs=[qspec, head, shared, head],
        out_shape=[jax.ShapeDtypeStruct((t, N_HEADS * HEAD_PAD), BF16), jax.ShapeDtypeStruct((t, N_HEADS * QK_NOPE), BF16),
                   jax.ShapeDtypeStruct((t, LANES), F32), jax.ShapeDtypeStruct((t, N_HEADS * V_DIM), BF16)],
        scratch_shapes=[pltpu.VMEM((hg * HEAD_PAD, seq), F32), pltpu.VMEM((seq, hg * HEAD_PAD), F32),
                        pltpu.VMEM((seq, hg * V_DIM), F32)],
        compiler_params=_params(("parallel", "arbitrary")), name=name)(q, kn, kr, kn_t, kr_t, v, o, lse, do, cos, sin)


def _head_and_loss(o, g2, x1, target, w_out, g_final, *, name, bt=256):
    t, d = x1.shape
    bt = min(bt, t)
    nt = (((1,), (1,)), ((), ()))

    def body(o_ref, g2_ref, x1_ref, tgt_ref, w_ref, gf_ref, loss_ref, dx2_ref, y2_ref, do_ref, dg2_ref, dgf_ref):
        @pl.when(pl.program_id(0) == 0)
        def _():
            loss_ref[...] = jnp.zeros_like(loss_ref)
            dgf_ref[...] = jnp.zeros_like(dgf_ref)

        ov = o_ref[...]
        gv = g2_ref[...]
        sg = _sigmoid(gv)
        silu = gv * sg
        y2 = (ov * silu).astype(BF16)
        y2_ref[...] = y2
        w = w_ref[...]
        x2 = x1_ref[...] + jnp.dot(y2, w, preferred_element_type=F32)
        r = lax.rsqrt(jnp.mean(x2 * x2, axis=-1, keepdims=True) + EPS)
        nrm = x2 * r
        gf = gf_ref[...]
        err = nrm * gf - tgt_ref[...]
        loss_ref[...] += 0.5 * jnp.sum(jnp.mean(err * err, axis=-1, keepdims=True))
        dyf = err * (1.0 / d)
        dgf_ref[...] += jnp.sum(dyf * nrm, axis=0, keepdims=True)
        dn = dyf * gf
        dx2 = r * (dn - nrm * jnp.mean(dn * nrm, axis=-1, keepdims=True))
        dx2_ref[...] = dx2
        dy2 = lax.dot_general(dx2.astype(BF16), w, nt, preferred_element_type=F32)
        do_ref[...] = (dy2 * silu).astype(BF16)
        dg2_ref[...] = (dy2 * ov * (sg * (1.0 + gv * (1.0 - sg)))).astype(BF16)

    row = pl.BlockSpec((bt, d), lambda i: (i, 0))
    vec = pl.BlockSpec((1, d), lambda i: (0, 0))
    return pl.pallas_call(
        body, grid=(t // bt,),
        in_specs=[row, row, row, row, pl.BlockSpec((d, d), lambda i: (0, 0)), vec],
        out_specs=[pl.BlockSpec((8, LANES), lambda i: (0, 0)), row, row, row, row, vec],
        out_shape=[jax.ShapeDtypeStruct((8, LANES), F32), jax.ShapeDtypeStruct((t, d), F32),
                   jax.ShapeDtypeStruct((t, d), BF16), jax.ShapeDtypeStruct((t, d), BF16),
                   jax.ShapeDtypeStruct((t, d), BF16), jax.ShapeDtypeStruct((1, d), F32)],
        compiler_params=_params(("arbitrary",)), name=name)(o, g2, x1, target, w_out, g_final)


def _sum_parts(parts, *, name, br=GRAD_BLOCK):
    npart, rows, w = parts.shape

    def body(p_ref, o_ref):
        acc = p_ref[0].astype(F32)
        for j in range(1, npart):
            acc = acc + p_ref[j].astype(F32)
        o_ref[...] = acc

    return pl.pallas_call(
        body, grid=(rows // br,), in_specs=[pl.BlockSpec((npart, br, w), lambda i: (0, i, 0))],
        out_specs=pl.BlockSpec((br, w), lambda i: (i, 0)), out_shape=jax.ShapeDtypeStruct((rows, w), F32),
        compiler_params=_params(("parallel",)), name=name)(parts)


def _chip_partial(parts, recv, *, name, br=GRAD_BLOCK):
    _, rows, w = parts.shape
    core = lax.axis_index("c").astype(jnp.int32).reshape(1)

    def body(c_ref, p_ref, r_ref, o_ref):
        o_ref[...] = (p_ref[...] + r_ref[...]).astype(BF16)

    grid_spec = pltpu.PrefetchScalarGridSpec(
        num_scalar_prefetch=1, grid=(4, rows // br),
        in_specs=[pl.BlockSpec((None, br, w), lambda k, i, c_ref: (2 * k + c_ref[0], i, 0)),
                  pl.BlockSpec((None, br, w), lambda k, i, c_ref: (k, i, 0))],
        out_specs=pl.BlockSpec((None, br, w), lambda k, i, c_ref: (k, i, 0)))
    return pl.pallas_call(
        body, grid_spec=grid_spec, out_shape=jax.ShapeDtypeStruct((4, rows, w), BF16),
        compiler_params=_params(("parallel", "parallel")), name=name)(core, parts, recv)


def _as_block(a):
    if a.ndim == 1:
        return a.reshape(1, -1)
    if a.ndim > 2 and a.shape[0] == 1:
        return a.reshape(a.shape[1:])
    return a


def _adamw(g, w, m, v, *, name):
    shape = w.shape
    g, w, m, v = (_as_block(a) for a in (g, w, m, v))

    def body(g_ref, w_ref, m_ref, v_ref, d_ref, nm_ref, nv_ref):
        gv = g_ref[...]
        nm = ADAM_B1 * m_ref[...] + (1.0 - ADAM_B1) * gv
        nv = ADAM_B2 * v_ref[...] + (1.0 - ADAM_B2) * (gv * gv)
        nm_ref[...] = nm
        nv_ref[...] = nv
        m_hat = nm / (1.0 - ADAM_B1 ** ADAM_STEP)
        v_hat = nv / (1.0 - ADAM_B2 ** ADAM_STEP)
        d_ref[...] = (-ADAM_LR) * (m_hat / (jnp.sqrt(v_hat) + ADAM_EPS) + ADAM_WD * w_ref[...])

    whole = pl.BlockSpec(memory_space=pltpu.VMEM)
    outs = pl.pallas_call(
        body, in_specs=[whole] * 4, out_specs=[whole] * 3, out_shape=[jax.ShapeDtypeStruct(w.shape, F32)] * 3,
        compiler_params=_params(), name=name)(g, w, m, v)
    return [o.reshape(shape) for o in outs]


def _mesh_pos():
    return lax.axis_index("x"), lax.axis_index("y"), lax.axis_index("c")


_ANY = pl.BlockSpec(memory_space=pl.ANY)


def _all_gather(block, *, name):
    m, n = block.shape

    def body(x_ref, out_ref, send_sems, recv_sems, local_sem):
        x, y, c = _mesh_pos()
        me, sibling = (x, y, c), (x, y, 1 - c)
        chips = [(1 - x, y), (x, 1 - y), (1 - x, 1 - y)]

        def slot(px, py, pc):
            return out_ref.at[4 * px + 2 * py + pc]

        def copy(k, blk, to, src=None):
            return pltpu.make_async_remote_copy(
                src_ref=slot(*blk) if src is None else src, dst_ref=slot(*blk),
                send_sem=send_sems.at[k], recv_sem=recv_sems.at[k], device_id=to, device_id_type=pl.DeviceIdType.MESH)

        mine = pltpu.make_async_copy(x_ref, slot(*me), local_sem)
        mine.start()
        first = [copy(0, me, sibling, src=x_ref)]
        first += [copy(1 + j, me, (*chip, c), src=x_ref) for j, chip in enumerate(chips)]
        for cp in first:
            cp.start()
        passed = [copy(4 + j, (*chip, c), sibling) for j, chip in enumerate(chips)]
        for j, chip in enumerate(chips):
            copy(1 + j, (*chip, c), me).wait_recv()
            passed[j].start()
        copy(0, sibling, me).wait_recv()
        for j, chip in enumerate(chips):
            copy(4 + j, (*chip, 1 - c), me).wait_recv()
        for cp in first + passed:
            cp.wait_send()
        mine.wait()

    return pl.pallas_call(
        body, out_shape=jax.ShapeDtypeStruct((N_DEV, m, n), block.dtype), in_specs=[_ANY], out_specs=_ANY,
        scratch_shapes=[pltpu.SemaphoreType.DMA((7,)), pltpu.SemaphoreType.DMA((7,)), pltpu.SemaphoreType.DMA(())],
        name=name)(block)


def _exchange_d2d(parts, *, name):
    _, rows, w = parts.shape

    def body(p_ref, land_ref, send_sems, recv_sems):
        x, y, c = _mesh_pos()
        sends = []
        for k in range(4):
            cp = pltpu.make_async_remote_copy(
                src_ref=p_ref.at[2 * k + (1 - c)], dst_ref=land_ref.at[k], send_sem=send_sems.at[k],
                recv_sem=recv_sems.at[k], device_id=(x, y, 1 - c), device_id_type=pl.DeviceIdType.MESH)
            cp.start()
            sends.append(cp)
        for cp in sends:
            cp.wait_recv()
        for cp in sends:
            cp.wait_send()

    return pl.pallas_call(
        body, out_shape=jax.ShapeDtypeStruct((4, rows, w), parts.dtype), in_specs=[_ANY], out_specs=_ANY,
        scratch_shapes=[pltpu.SemaphoreType.DMA((4,)), pltpu.SemaphoreType.DMA((4,))], name=name)(parts)


def _exchange_ici(parts, *, name):
    def body(p_ref, land_ref, send_sems, recv_sems, local_sem):
        x, y, c = _mesh_pos()
        mine = pltpu.make_async_copy(p_ref.at[2 * x + y], land_ref.at[3], local_sem)
        mine.start()
        sends = []
        for k, (px, py) in enumerate([(1 - x, y), (x, 1 - y), (1 - x, 1 - y)]):
            cp = pltpu.make_async_remote_copy(
                src_ref=p_ref.at[2 * px + py], dst_ref=land_ref.at[k], send_sem=send_sems.at[k],
                recv_sem=recv_sems.at[k], device_id=(px, py, c), device_id_type=pl.DeviceIdType.MESH)
            cp.start()
            sends.append(cp)
        for cp in sends:
            cp.wait_recv()
        for cp in sends:
            cp.wait_send()
        mine.wait()

    return pl.pallas_call(
        body, out_shape=jax.ShapeDtypeStruct(parts.shape, parts.dtype), in_specs=[_ANY], out_specs=_ANY,
        scratch_shapes=[pltpu.SemaphoreType.DMA((3,)), pltpu.SemaphoreType.DMA((3,)), pltpu.SemaphoreType.DMA(())],
        name=name)(parts)


def _rows(a):
    return a.reshape(-1, PACK_W)


def _pad_to(a, n):
    return jnp.pad(a, (0, n - a.shape[0]))


def _pack_shard(d):
    small = jnp.concatenate([d[n].reshape(-1) for n, _ in _SMALL])
    w_uq = jnp.pad(d["w_uq"][0], ((0, 0), (0, 0), (0, HEAD_PAD - QK_NOPE - QK_ROPE)))
    pieces = {"w_in_a": d["w_in_a"], "w_out_a": d["w_out_a"], "w_dkv": d["w_dkv"], "w_uk": d["w_uk"], "w_uv": d["w_uv"],
              "w_in_b": d["w_in_b"], "w_uq": w_uq, "w_out_b": d["w_out_b"], "pad": jnp.zeros((8, PACK_W), F32),
              "small": _pad_to(small, 8 * PACK_W)}
    return jnp.concatenate([_rows(pieces[n]) for n, _ in _SHARD_PIECES], axis=0)


def _unpack_shard(p):
    out = {}
    piece = {n: p[lo:hi] for n, (lo, hi) in _SHARD_OFF.items()}
    out["w_in_a"] = piece["w_in_a"].reshape(1, D_MODEL, 2 * D_RNN // N_DEV)
    out["w_out_a"] = piece["w_out_a"].reshape(1, D_RNN // N_DEV, D_MODEL)
    out["w_dkv"] = piece["w_dkv"].reshape(D_MODEL // N_DEV, KV_RANK + QK_ROPE)
    out["w_uk"] = piece["w_uk"].reshape(KV_RANK // N_DEV, N_HEADS, QK_NOPE)
    out["w_uv"] = piece["w_uv"].reshape(KV_RANK // N_DEV, N_HEADS, V_DIM)
    out["w_in_b"] = piece["w_in_b"].reshape(1, D_MODEL, (Q_RANK + N_HEADS * V_DIM) // N_DEV)
    out["w_uq"] = piece["w_uq"].reshape(1, Q_RANK // N_DEV, N_HEADS, HEAD_PAD)[..., :QK_NOPE + QK_ROPE]
    out["w_out_b"] = piece["w_out_b"].reshape(1, N_HEADS * V_DIM // N_DEV, D_MODEL)
    small = piece["small"].reshape(-1)
    off = 0
    shapes = {"norm_a": (1, D_MODEL // N_DEV), "conv_w": (1, CONV_WIDTH, D_RNN // N_DEV), "conv_b": (1, D_RNN // N_DEV),
              "b_rg": (1, D_RNN // N_DEV), "b_ig": (1, D_RNN // N_DEV), "lru_lambda": (1, D_RNN // N_DEV)}
    for n, k in _SMALL:
        out[n] = small[off:off + k].reshape(shapes[n])
        off += k
    return out


def _pack_rep(d):
    flat = jnp.concatenate([d[n].reshape(-1) for n, _ in _REP])
    return _rows(_pad_to(flat, REP_ROWS * PACK_W))


def _unpack_rep(p, like):
    flat = p.reshape(-1)
    out, off = {}, 0
    for n, k in _REP:
        out[n] = flat[off:off + k].reshape(like[n].shape)
        off += k
    return out


def _gathered_weights(wall):
    piece = {n: wall[:, lo:hi] for n, (lo, hi) in _SHARD_OFF.items() if n != "small"}
    w = {}
    w_in_a = piece["w_in_a"].reshape(N_DEV, D_MODEL, -1).transpose(1, 0, 2).reshape(D_MODEL, 2 * D_RNN)
    w["w_xp"], w["w_ga"] = w_in_a[:, :D_RNN], w_in_a[:, D_RNN:]
    w["w_out_a"] = piece["w_out_a"].reshape(D_RNN, D_MODEL)
    w_dkv = piece["w_dkv"].reshape(D_MODEL, KV_RANK + QK_ROPE)
    w["w_dkv_c"] = w_dkv[:, :KV_RANK]
    w["w_dkv_r"] = jnp.pad(w_dkv[:, KV_RANK:], ((0, 0), (0, LANES - QK_ROPE)))
    w["w_uk"] = piece["w_uk"].reshape(KV_RANK, N_HEADS * QK_NOPE)
    w["w_uv"] = piece["w_uv"].reshape(KV_RANK, N_HEADS * V_DIM)
    w["w_uk_t"], w["w_uv_t"] = w["w_uk"].T, w["w_uv"].T
    w_in_b = piece["w_in_b"].reshape(N_DEV, D_MODEL, -1).transpose(1, 0, 2).reshape(D_MODEL, Q_RANK + N_HEADS * V_DIM)
    w["w_cq"], w["w_g2"] = w_in_b[:, :Q_RANK], w_in_b[:, Q_RANK:]
    w["w_uq"] = piece["w_uq"].reshape(Q_RANK, N_HEADS * HEAD_PAD)
    w["w_out_b"] = piece["w_out_b"].reshape(N_HEADS * V_DIM, D_MODEL)
    small = lax.bitcast_convert_type(wall[:, MATRIX_ROWS:].reshape(N_DEV, 8 * PACK_W, 2), F32)
    off = dict(zip([n for n, _ in _SMALL], [0, 128, 768, 928, 1088, 1248]))
    w["norm_a"] = small[:, :128].reshape(1, D_MODEL)

    def by_channel(lo, rows):
        a = small[:, lo:lo + rows * (D_RNN // N_DEV)].reshape(N_DEV, rows, -1).transpose(1, 0, 2).reshape(rows, D_RNN)
        return jnp.pad(a, ((0, 8 - rows), (0, 0)))

    w["conv_taps"] = by_channel(off["conv_w"], CONV_WIDTH)
    w["lru_vecs"] = by_channel(off["conv_b"], 4)
    return w


def _pack_grads(g):
    def by_cols(a):
        r, n = a.shape
        return a.reshape(r, N_DEV, n // N_DEV).transpose(1, 0, 2).reshape(N_DEV, -1, PACK_W)

    def by_rows(a):
        return a.reshape(N_DEV, -1, PACK_W)

    small = jnp.concatenate([
        g["norm_a"].reshape(N_DEV, -1),
        g["conv_w"].reshape(CONV_WIDTH, N_DEV, -1).transpose(1, 0, 2).reshape(N_DEV, -1),
        g["conv_b"].reshape(N_DEV, -1), g["b_rg"].reshape(N_DEV, -1), g["b_ig"].reshape(N_DEV, -1),
        g["lru_lambda"].reshape(N_DEV, -1)], axis=1)
    small = jnp.pad(small, ((0, 0), (0, 8 * PACK_W - small.shape[1]))).reshape(N_DEV, 8, PACK_W)
    pieces = {"w_in_a": by_cols(g["w_in_a"]), "w_out_a": by_rows(g["w_out_a"]), "w_dkv": by_rows(g["w_dkv"]),
              "w_uk": by_rows(g["w_uk"]), "w_uv": by_rows(g["w_uv"]), "w_in_b": by_cols(g["w_in_b"]),
              "w_uq": by_rows(g["w_uq"]), "w_out_b": by_rows(g["w_out_b"]), "pad": jnp.zeros((N_DEV, 8, PACK_W), F32),
              "small": small}
    rep = _pack_rep(g).reshape(N_DEV, REP_SLICE, PACK_W)
    tail = jnp.zeros((N_DEV, GRAD_ROWS - SHARD_ROWS - REP_SLICE, PACK_W), F32)
    return jnp.concatenate([pieces[n] for n, _ in _SHARD_PIECES] + [rep, tail], axis=1)


def _step(x, target, w, rep, *, bsz, seq):
    t = bsz * seq
    cos, sin = _rope_tables(seq)
    g_a = w["norm_a"]
    g_kv = rep["norm_kv"].reshape(1, -1)
    g_kvn = rep["kv_norm"].reshape(1, -1)
    g_b = rep["norm_b"].reshape(1, -1)
    g_q = rep["q_norm"].reshape(1, -1)
    g_f = rep["final_norm"].reshape(1, -1)
    wrg = rep["w_rg"][0].astype(BF16)
    wig = rep["w_ig"][0].astype(BF16)
    cw8, vecs = w["conv_taps"], w["lru_vecs"]

    def seq3(a):
        return a.reshape(bsz, seq, a.shape[-1])

    def flat(a):
        return a.reshape(t, a.shape[-1])

    h0, xp, ga = _lru_proj_fwd(x, g_a, w["w_xp"], w["w_ga"], name="lru_proj_fwd")
    xb, hs, y = _lru_fwd(seq3(xp), seq3(ga), cw8, vecs, wrg, wig, name="lru_fwd")
    x1 = _matmul(flat(y), w["w_out_a"], residual=x, name="out_a")
    hk, hq, ck, cqp, g2, ckv, cq, q, kn, v, kr, kn_t, v_t, kr_t = _mla_proj_fwd(
        x1, (g_kv, g_b, g_kvn, g_q), w, cos, sin, seq=seq, name="mla_proj_fwd")
    o, lse = _attn_fwd(q, kn, kr, v_t, bsz=bsz, seq=seq, name="attn_fwd")
    loss, dx2, y2, do, dg2, dgf = _head_and_loss(o, g2, x1, target, w["w_out_b"], g_f, name="head_loss")
    grads = {"final_norm": dgf, "w_out_b": _matmul_tn(y2, dx2, name="d_w_out_b")}
    dq, dkn, dkr, dv = _attn_bwd(q, kn, kr, kn_t, kr_t, v, o, lse, do, cos, sin, bsz=bsz, seq=seq, name="attn_bwd")
    grads["w_uq"] = _matmul_tn(cq, dq, name="d_w_uq")
    dx1, du2, dckr, dgkv, dgb, dgkvn, dgq = _mla_proj_bwd(
        x1, dx2, cqp, ck, dq, dkn, dv, dkr, dg2, (g_kv, g_b, g_kvn, g_q), w, name="mla_proj_bwd")
    grads["norm_kv"], grads["norm_b"], grads["kv_norm"], grads["q_norm"] = dgkv, dgb, dgkvn, dgq
    grads["w_in_b"] = _matmul_tn(hq, du2, name="d_w_in_b")
    grads["w_uk"] = _matmul_tn(ckv, dkn, name="d_w_uk")
    grads["w_uv"] = _matmul_tn(ckv, dv, name="d_w_uv")
    grads["w_dkv"] = _matmul_tn(hk, dckr, name="d_w_dkv")[:, :KV_RANK + QK_ROPE]
    grads["w_out_a"] = _matmul_tn(flat(y), dx1, name="d_w_out_a")
    dy = _matmul(dx1, w["w_out_a"], nt=True, name="d_y")
    dxp, dga, dwrg, dwig, dvec = _lru_bwd(seq3(dy), seq3(xp), xb, hs, seq3(ga), cw8, vecs, wrg, wig, name="lru_bwd")
    dxp, dga = flat(dxp), flat(dga)
    grads["w_rg"], grads["w_ig"] = dwrg, dwig
    grads["b_rg"], grads["b_ig"], grads["conv_b"] = dvec[0], dvec[1], dvec[3]
    lam = vecs[3]
    grads["lru_lambda"] = dvec[2] * (-1.0 / (1.0 + jnp.exp(lam)))
    grads["conv_w"] = dvec[4:4 + CONV_WIDTH]
    grads["w_in_a"] = jnp.concatenate([_matmul_tn(h0, dxp, name="d_w_in_a_x"), _matmul_tn(h0, dga, name="d_w_in_a_g")], axis=1)
    dx, dga_norm = _lru_proj_bwd(dxp, dga, x, dx1, g_a, w["w_xp"], w["w_ga"], name="lru_proj_bwd")
    grads["norm_a"] = dga_norm
    return loss[0, 0], dx, grads


def kernel(x, norm_a, w_in_a, conv_w, conv_b, w_rg, b_rg, w_ig, b_ig, lru_lambda, w_out_a, norm_kv, w_dkv, kv_norm, w_uk, w_uv, norm_b, w_in_b, q_norm, w_uq, w_out_b, final_norm, loss_target, m_norm_a, m_w_in_a, m_conv_w, m_conv_b, m_w_rg, m_b_rg, m_w_ig, m_b_ig, m_lru_lambda, m_w_out_a, m_norm_kv, m_w_dkv, m_kv_norm, m_w_uk, m_w_uv, m_norm_b, m_w_in_b, m_q_norm, m_w_uq, m_w_out_b, m_final_norm, v_norm_a, v_w_in_a, v_conv_w, v_conv_b, v_w_rg, v_b_rg, v_w_ig, v_b_ig, v_lru_lambda, v_w_out_a, v_norm_kv, v_w_dkv, v_kv_norm, v_w_uk, v_w_uv, v_norm_b, v_w_in_b, v_q_norm, v_w_uq, v_w_out_b, v_final_norm):
    given = dict(locals())
    wts = {n: given[n] for n in WEIGHTS}
    mom1 = {n: given["m_" + n] for n in WEIGHTS}
    mom2 = {n: given["v_" + n] for n in WEIGHTS}
    bsz, seq, _ = x.shape
    t = bsz * seq

    w_pack = _pack_shard(wts)
    vec_bits = lax.bitcast_convert_type(w_pack[MATRIX_ROWS:], WIRE).reshape(16, PACK_W)
    send = jnp.concatenate([w_pack[:MATRIX_ROWS].astype(WIRE), vec_bits], axis=0)
    wall = _all_gather(send, name="gather_weights")
    w = _gathered_weights(wall)

    loss, dx, grads = _step(x.reshape(t, D_MODEL), loss_target.reshape(t, D_MODEL), w, wts, bsz=bsz, seq=seq)
    loss = lax.psum(loss, MESH_AXES)

    parts = _pack_grads(grads)
    from_sibling = _exchange_d2d(parts, name="exchange_grads_d2d")
    chip_parts = _chip_partial(parts, from_sibling, name="chip_partial_grads")
    landed = _exchange_ici(chip_parts, name="exchange_grads_ici")
    g_pack = _sum_parts(landed, name="sum_grads")
    g_own = _unpack_shard(g_pack[:SHARD_ROWS])
    rep_slice = g_pack[SHARD_ROWS:SHARD_ROWS + REP_SLICE]
    g_rep = _all_gather(rep_slice, name="gather_replicated").reshape(REP_ROWS, PACK_W)
    g_own.update(_unpack_rep(g_rep, wts))

    deltas, new_m, new_v = {}, {}, {}
    for n in WEIGHTS:
        deltas[n], new_m[n], new_v[n] = _adamw(g_own[n], wts[n], mom1[n], mom2[n], name="adamw_" + n)
    result = [loss, dx.reshape(bsz, seq, D_MODEL)]
    for d in (g_own, deltas, new_m, new_v):
        result.extend(d[n] for n in WEIGHTS)
    return tuple(result)
```

```python
import jax
import jax.numpy as jnp
from jax import lax
from jax.experimental import pallas as pl
from jax.experimental.pallas import tpu as pltpu

F32 = jnp.float32
BF16 = jnp.bfloat16
WIRE = jnp.bfloat16

D_MODEL = 1024
D_RNN = 1280
RNN_BLOCKS = 10
RNN_BW = 128
CONV_WIDTH = 4
LRU_C = 8.0
N_HEADS = 8
QK_NOPE = 128
QK_ROPE = 64
V_DIM = 128
KV_RANK = 256
Q_RANK = 384
ROPE_THETA = 10000.0
EPS = 1e-6
ATTN_SCALE = (QK_NOPE + QK_ROPE) ** -0.5
HEAD_PAD = 256
LANES = 128

ADAM_LR = 0.001
ADAM_B1 = 0.9
ADAM_B2 = 0.999
ADAM_EPS = 1e-08
ADAM_WD = 0.01
ADAM_STEP = 10

N_DEV = 8
MESH_AXES = ("x", "y", "c")
VMEM_LIMIT_BYTES = 56 * 2**20
PACK_W = 1024

_PIECES_A = (("w_in_a", 320), ("w_out_a", 160))
_PIECES_B = (("w_dkv", 40), ("w_uk", 32), ("w_uv", 32), ("w_in_b", 176), ("w_uq", 96), ("w_out_b", 128))


def _offsets(pieces):
    off, r = {}, 0
    for n, k in pieces:
        off[n] = (r, r + k)
        r += k
    return off, r


_OFF_A, MATRIX_ROWS_A = _offsets(_PIECES_A)
_OFF_B, MATRIX_ROWS_B = _offsets(_PIECES_B)
WIRE_ROWS_A = MATRIX_ROWS_A + 16
WIRE_ROWS_B = 512
_SMALL = (("norm_a", 128), ("conv_w", 640), ("conv_b", 160), ("b_rg", 160), ("b_ig", 160), ("lru_lambda", 160))
_REP = (("w_rg", 163840), ("w_ig", 163840), ("norm_kv", 1024), ("kv_norm", 256), ("norm_b", 1024),
        ("q_norm", 384), ("final_norm", 1024))
REP_ROWS = 384
REP_SLICE = REP_ROWS // N_DEV
GRAD_ROWS_A = 544
GRAD_BLOCK = 272

WEIGHTS = ("norm_a", "w_in_a", "conv_w", "conv_b", "w_rg", "b_rg", "w_ig", "b_ig", "lru_lambda", "w_out_a",
           "norm_kv", "w_dkv", "kv_norm", "w_uk", "w_uv", "norm_b", "w_in_b", "q_norm", "w_uq", "w_out_b",
           "final_norm")


def _params(sem=None):
    return pltpu.CompilerParams(dimension_semantics=sem, vmem_limit_bytes=VMEM_LIMIT_BYTES)


_NT = (((1,), (1,)), ((), ()))
_ANY = pl.BlockSpec(memory_space=pl.ANY)


def _mesh_pos():
    return lax.axis_index("x"), lax.axis_index("y"), lax.axis_index("c")


def _sigmoid(z):
    return 0.5 * jnp.tanh(0.5 * z) + 0.5


def _col_block(n):
    return n if n <= 1408 else n // 2


def _matmul(a, b, *, name, nt=False, out_dtype=F32, residual=None, bm=512):
    m, k = a.shape
    n = b.shape[0] if nt else b.shape[1]
    bm = min(bm, m)
    bn = _col_block(n)
    dims = (((1,), (1,)), ((), ())) if nt else (((1,), (0,)), ((), ()))
    has_res = residual is not None

    def body(*refs):
        a_ref, b_ref, o_ref = refs[0], refs[1], refs[-1]
        acc = lax.dot_general(a_ref[...].astype(BF16), b_ref[...].astype(BF16), dims, preferred_element_type=F32)
        if has_res:
            acc = acc + refs[2][...]
        o_ref[...] = acc.astype(out_dtype)

    in_specs = [pl.BlockSpec((bm, k), lambda i, j: (i, 0)),
                pl.BlockSpec((bn, k), lambda i, j: (j, 0)) if nt else pl.BlockSpec((k, bn), lambda i, j: (0, j))]
    args = [a, b]
    if has_res:
        in_specs.append(pl.BlockSpec((bm, bn), lambda i, j: (i, j)))
        args.append(residual)
    return pl.pallas_call(
        body, grid=(m // bm, n // bn), in_specs=in_specs, out_specs=pl.BlockSpec((bm, bn), lambda i, j: (i, j)),
        out_shape=jax.ShapeDtypeStruct((m, n), out_dtype), compiler_params=_params(("parallel", "parallel")),
        name=name)(*args)


def _matmul_tn(a, b, *, name, bt=512):
    t, m = a.shape
    n = b.shape[1]
    bt = min(bt, t)
    bm, bn = _col_block(m), _col_block(n)

    def body(a_ref, b_ref, o_ref):
        @pl.when(pl.program_id(2) == 0)
        def _():
            o_ref[...] = jnp.zeros_like(o_ref)

        o_ref[...] += lax.dot_general(a_ref[...].astype(BF16), b_ref[...].astype(BF16),
                                      (((0,), (0,)), ((), ())), preferred_element_type=F32)

    return pl.pallas_call(
        body, grid=(m // bm, n // bn, t // bt),
        in_specs=[pl.BlockSpec((bt, bm), lambda i, j, s: (s, i)), pl.BlockSpec((bt, bn), lambda i, j, s: (s, j))],
        out_specs=pl.BlockSpec((bm, bn), lambda i, j, s: (i, j)),
        out_shape=jax.ShapeDtypeStruct((m, n), F32),
        compiler_params=_params(("parallel", "parallel", "arbitrary")), name=name)(a, b)


def _swap_halves(v):
    ax = v.ndim - 1
    lane = lax.broadcasted_iota(jnp.int32, v.shape, ax)
    up = pltpu.roll(v, LANES - QK_ROPE // 2, axis=ax)
    down = pltpu.roll(v, QK_ROPE // 2, axis=ax)
    return jnp.where(lane < QK_ROPE // 2, up, jnp.where(lane < QK_ROPE, down, 0.0))


def _rope(v, cos, sin):
    return v * cos + _swap_halves(v) * sin


def _rope_t(d, cos, sin):
    return d * cos + _swap_halves(d * sin)


def _rope_tables(seq):
    pos = jnp.arange(seq, dtype=F32)
    inv = ROPE_THETA ** (-jnp.arange(0, QK_ROPE, 2, dtype=F32) / QK_ROPE)
    ang = pos[:, None] * inv[None, :]
    cos, sin = jnp.cos(ang), jnp.sin(ang)
    zero = jnp.zeros((seq, LANES - QK_ROPE), F32)
    return jnp.concatenate([cos, cos, zero], axis=1), jnp.concatenate([-sin, sin, zero], axis=1)


def _rms(v):
    return v * lax.rsqrt(jnp.mean(v * v, axis=-1, keepdims=True) + EPS)


def _const_spec(a):
    return pl.BlockSpec(a.shape, lambda i: (0,) * a.ndim)


def _lru_proj_fwd(x, g_a, w_in_t, *, name, bt=256):
    t, d = x.shape
    bt = min(bt, t)
    n = w_in_t.shape[0] // 2

    def body(x_ref, g_ref, wt_ref, h_ref, xp_ref, ga_ref):
        h = (_rms(x_ref[...]) * g_ref[...]).astype(BF16)
        h_ref[...] = h
        xp_ref[...] = lax.dot_general(h, wt_ref[0:n, :], _NT, preferred_element_type=F32)
        ga_ref[...] = lax.dot_general(h, wt_ref[n:2 * n, :], _NT, preferred_element_type=F32)

    row = lambda w: pl.BlockSpec((bt, w), lambda i: (i, 0))
    return pl.pallas_call(
        body, grid=(t // bt,), in_specs=[row(d), _const_spec(g_a), _const_spec(w_in_t)],
        out_specs=[row(d), row(n), row(n)],
        out_shape=[jax.ShapeDtypeStruct((t, d), BF16), jax.ShapeDtypeStruct((t, n), F32), jax.ShapeDtypeStruct((t, n), F32)],
        compiler_params=_params(("parallel",)), name=name)(x, g_a, w_in_t)


def _mla_proj_fwd(x1, gains, w, cos, sin, *, seq, name, bt=256):
    t, d = x1.shape
    bt = min(bt, seq)
    per_seq = seq // bt
    g_kv, g_b, g_kvn, g_q = gains
    consts = [g_kv, g_b, g_kvn, g_q, w["w_dkv_c"], w["w_dkv_r"], w["w_in_b_t"], w["w_uk"], w["w_uv"],
              w["w_uk_t"], w["w_uv_t"], w["w_uq"]]

    def body(x_ref, cos_ref, sin_ref, gkv_ref, gb_ref, gkvn_ref, gq_ref, wdc_ref, wdr_ref, wbt_ref,
             wuk_ref, wuv_ref, wukt_ref, wuvt_ref, wuq_ref,
             hk_ref, hq_ref, ck_ref, cqp_ref, g2_ref, ckv_ref, cq_ref, q_ref, kn_ref, v_ref, kr_ref, knt_ref, vt_ref, krt_ref):
        nrm = _rms(x_ref[...])
        hk = (nrm * gkv_ref[...]).astype(BF16)
        hq = (nrm * gb_ref[...]).astype(BF16)
        hk_ref[...] = hk
        hq_ref[...] = hq
        ck = jnp.dot(hk, wdc_ref[...], preferred_element_type=F32)
        ck_ref[...] = ck
        cqp = lax.dot_general(hq, wbt_ref[0:Q_RANK, :], _NT, preferred_element_type=F32)
        cqp_ref[...] = cqp
        g2_ref[...] = lax.dot_general(hq, wbt_ref[Q_RANK:, :], _NT, preferred_element_type=F32)
        cosv, sinv = cos_ref[...], sin_ref[...]
        kr = _rope(jnp.dot(hk, wdr_ref[...], preferred_element_type=F32), cosv, sinv)
        kr_ref[...] = kr.astype(BF16)
        krt_ref[...] = kr.T.astype(BF16)
        ckv = (_rms(ck) * gkvn_ref[...]).astype(BF16)
        ckv_ref[...] = ckv
        kn_ref[...] = jnp.dot(ckv, wuk_ref[...], preferred_element_type=F32).astype(BF16)
        v_ref[...] = jnp.dot(ckv, wuv_ref[...], preferred_element_type=F32).astype(BF16)
        knt_ref[...] = lax.dot_general(wukt_ref[...], ckv, _NT, preferred_element_type=F32).astype(BF16)
        vt_ref[...] = lax.dot_general(wuvt_ref[...], ckv, _NT, preferred_element_type=F32).astype(BF16)
        cq = (_rms(cqp) * gq_ref[...]).astype(BF16)
        cq_ref[...] = cq
        for h in range(N_HEADS):
            qh = jnp.dot(cq, wuq_ref[:, h * HEAD_PAD:(h + 1) * HEAD_PAD], preferred_element_type=F32)
            q_ref[:, h * HEAD_PAD:h * HEAD_PAD + QK_NOPE] = qh[:, :QK_NOPE].astype(BF16)
            q_ref[:, h * HEAD_PAD + QK_NOPE:(h + 1) * HEAD_PAD] = _rope(qh[:, QK_NOPE:], cosv, sinv).astype(BF16)

    row = lambda w_: pl.BlockSpec((bt, w_), lambda i: (i, 0))
    col = lambda h_: pl.BlockSpec((h_, bt), lambda i: (0, i))
    tab = pl.BlockSpec((bt, LANES), lambda i: (i % per_seq, 0))
    nh = N_HEADS * V_DIM
    shapes = [((t, d), BF16), ((t, d), BF16), ((t, KV_RANK), F32), ((t, Q_RANK), F32), ((t, nh), F32), ((t, KV_RANK), BF16),
              ((t, Q_RANK), BF16), ((t, N_HEADS * HEAD_PAD), BF16), ((t, nh), BF16), ((t, nh), BF16), ((t, LANES), BF16),
              ((nh, t), BF16), ((nh, t), BF16), ((LANES, t), BF16)]
    out_specs = [row(d), row(d), row(KV_RANK), row(Q_RANK), row(nh), row(KV_RANK), row(Q_RANK), row(N_HEADS * HEAD_PAD),
                 row(nh), row(nh), row(LANES), col(nh), col(nh), col(LANES)]
    return pl.pallas_call(
        body, grid=(t // bt,), in_specs=[row(d), tab, tab] + [_const_spec(a) for a in consts], out_specs=out_specs,
        out_shape=[jax.ShapeDtypeStruct(s, dt) for s, dt in shapes],
        compiler_params=_params(("parallel",)), name=name)(x1, cos, sin, *consts)


def _rms_bwd_rows(xv, dn):
    r = lax.rsqrt(jnp.mean(xv * xv, axis=-1, keepdims=True) + EPS)
    nrm = xv * r
    return r * (dn - nrm * jnp.mean(dn * nrm, axis=-1, keepdims=True)), nrm


def _col_sum(v):
    return jnp.sum(v, axis=0, keepdims=True)


def _lru_proj_bwd(dxp, dga, x, dx1, g_a, w_in_t, *, name, bt=256):
    t, d = x.shape
    bt = min(bt, t)
    n = w_in_t.shape[0] // 2

    def body(dxp_ref, dga_ref, x_ref, dx1_ref, g_ref, wt_ref, dx_ref, dg_ref):
        @pl.when(pl.program_id(0) == 0)
        def _():
            dg_ref[...] = jnp.zeros_like(dg_ref)

        dh = (jnp.dot(dxp_ref[...], wt_ref[0:n, :], preferred_element_type=F32)
              + jnp.dot(dga_ref[...], wt_ref[n:2 * n, :], preferred_element_type=F32))
        dxn, nrm = _rms_bwd_rows(x_ref[...], dh * g_ref[...])
        dg_ref[...] += _col_sum(dh * nrm)
        dx_ref[...] = dx1_ref[...] + dxn

    row = lambda w: pl.BlockSpec((bt, w), lambda i: (i, 0))
    return pl.pallas_call(
        body, grid=(t // bt,),
        in_specs=[row(n), row(n), row(d), row(d), _const_spec(g_a), _const_spec(w_in_t)],
        out_specs=[row(d), _const_spec(g_a)],
        out_shape=[jax.ShapeDtypeStruct((t, d), F32), jax.ShapeDtypeStruct((1, d), F32)],
        compiler_params=_params(("arbitrary",)), name=name)(dxp, dga, x, dx1, g_a, w_in_t)


def _mla_proj_bwd(x1, dx2, cqp, ck, dq, dkn, dv, dkr, dg2, gains, w, *, name, bt=256):
    t, d = x1.shape
    bt = min(bt, t)
    g_kv, g_b, g_kvn, g_q = gains
    consts = [g_kv, g_b, g_kvn, g_q, w["w_dkv_c"], w["w_dkv_r"], w["w_in_b_t"], w["w_uk"], w["w_uv"], w["w_uq"]]
    nh = N_HEADS * V_DIM

    def body(x1_ref, dx2_ref, cqp_ref, ck_ref, dq_ref, dkn_ref, dv_ref, dkr_ref, dg2_ref,
             gkv_ref, gb_ref, gkvn_ref, gq_ref, wdc_ref, wdr_ref, wbt_ref, wuk_ref, wuv_ref, wuq_ref,
             dx1_ref, du2_ref, dckr_ref, dgkv_ref, dgb_ref, dgkvn_ref, dgq_ref):
        @pl.when(pl.program_id(0) == 0)
        def _():
            for ref in (dgkv_ref, dgb_ref, dgkvn_ref, dgq_ref):
                ref[...] = jnp.zeros_like(ref)

        dot_nt = lambda a, b: lax.dot_general(a, b, _NT, preferred_element_type=F32)
        dcq = dot_nt(dq_ref[...], wuq_ref[...])
        dcqp, nq = _rms_bwd_rows(cqp_ref[...], dcq * gq_ref[...])
        dgq_ref[...] += _col_sum(dcq * nq)
        dcqp = dcqp.astype(BF16)
        dg2 = dg2_ref[...]
        du2_ref[:, :Q_RANK] = dcqp
        du2_ref[:, Q_RANK:] = dg2
        dhq = (jnp.dot(dcqp, wbt_ref[0:Q_RANK, :], preferred_element_type=F32)
               + jnp.dot(dg2, wbt_ref[Q_RANK:, :], preferred_element_type=F32))
        dckv = dot_nt(dkn_ref[...], wuk_ref[...]) + dot_nt(dv_ref[...], wuv_ref[...])
        dck, nc = _rms_bwd_rows(ck_ref[...], dckv * gkvn_ref[...])
        dgkvn_ref[...] += _col_sum(dckv * nc)
        dck = dck.astype(BF16)
        dkr = dkr_ref[...].astype(BF16)
        dckr_ref[:, :KV_RANK] = dck
        dckr_ref[:, KV_RANK:] = dkr
        dhk = dot_nt(dck, wdc_ref[...]) + dot_nt(dkr, wdr_ref[...])
        dxn, n1 = _rms_bwd_rows(x1_ref[...], dhq * gb_ref[...] + dhk * gkv_ref[...])
        dgb_ref[...] += _col_sum(dhq * n1)
        dgkv_ref[...] += _col_sum(dhk * n1)
        dx1_ref[...] = dx2_ref[...] + dxn

    row = lambda w_: pl.BlockSpec((bt, w_), lambda i: (i, 0))
    vec = lambda w_: pl.BlockSpec((1, w_), lambda i: (0, 0))
    in_specs = [row(d), row(d), row(Q_RANK), row(KV_RANK), row(N_HEADS * HEAD_PAD), row(nh), row(nh), row(LANES), row(nh)]
    return pl.pallas_call(
        body, grid=(t // bt,), in_specs=in_specs + [_const_spec(a) for a in consts],
        out_specs=[row(d), row(Q_RANK + nh), row(KV_RANK + LANES), vec(d), vec(d), vec(KV_RANK), vec(Q_RANK)],
        out_shape=[jax.ShapeDtypeStruct((t, d), F32), jax.ShapeDtypeStruct((t, Q_RANK + nh), BF16),
                   jax.ShapeDtypeStruct((t, KV_RANK + LANES), BF16), jax.ShapeDtypeStruct((1, d), F32),
                   jax.ShapeDtypeStruct((1, d), F32), jax.ShapeDtypeStruct((1, KV_RANK), F32),
                   jax.ShapeDtypeStruct((1, Q_RANK), F32)],
        compiler_params=_params(("arbitrary",)), name=name)(x1, dx2, cqp, ck, dq, dkn, dv, dkr, dg2, *consts)


def _softplus(z):
    return jnp.maximum(z, 0.0) + jnp.log1p(jnp.exp(-jnp.abs(z)))


def _neg_expm1(z):
    series = -z * (1.0 + z * (1.0 / 2) * (1.0 + z * (1.0 / 3) * (1.0 + z * (1.0 / 4))))
    return jnp.where(z > -0.01, series, 1.0 - jnp.exp(z))


def _gates(xb, wrg, wig, brg, big, sp):
    xbb = xb.astype(BF16)
    r = _sigmoid(jnp.dot(xbb, wrg, preferred_element_type=F32) + brg)
    i = _sigmoid(jnp.dot(xbb, wig, preferred_element_type=F32) + big)
    la = (-LRU_C) * r * sp
    a = jnp.exp(la)
    mult = jnp.sqrt(_neg_expm1(2.0 * la))
    return r, i, a, mult


def _conv(xpad_ref, cw_ref, seq):
    acc = cw_ref[0:1, :] * xpad_ref[pl.ds(8 - (CONV_WIDTH - 1), seq), :]
    for k in range(1, CONV_WIDTH):
        acc = acc + cw_ref[k:k + 1, :] * xpad_ref[pl.ds(8 - (CONV_WIDTH - 1) + k, seq), :]
    return acc


def _seq_spec(seq):
    return pl.BlockSpec((None, seq, RNN_BW), lambda n, b: (b, 0, n))


def _chan_spec(rows):
    return pl.BlockSpec((rows, RNN_BW), lambda n, b: (0, n))


_GATE_W_SPEC = pl.BlockSpec((None, RNN_BW, RNN_BW), lambda n, b: (n, 0, 0))


def _peers():
    x, y, c = _mesh_pos()
    others = []
    for k in range(1, N_DEV):
        px = 1 - x if k & 4 else x
        py = 1 - y if k & 2 else y
        pc = 1 - c if k & 1 else c
        others.append(((px, py, pc), 4 * px + 2 * py + pc))
    return 4 * x + 2 * y + c, others


def _exchange(src_ref, dst_ref, send_sems, recv_sems, local_sem, *, gather, finish):
    me, others = _peers()

    def send(k, dev, slot):
        return pltpu.make_async_remote_copy(
            src_ref=src_ref if gather else src_ref.at[slot], dst_ref=dst_ref.at[me], send_sem=send_sems.at[k],
            recv_sem=recv_sems.at[k], device_id=dev, device_id_type=pl.DeviceIdType.MESH)

    local = pltpu.make_async_copy(src_ref if gather else src_ref.at[me], dst_ref.at[me], local_sem)
    if not finish:
        local.start()
        for k, (dev, slot) in enumerate(others):
            send(k, dev, slot).start()
        return
    for k, (dev, slot) in enumerate(others):
        pltpu.make_async_remote_copy(
            src_ref=dst_ref.at[slot], dst_ref=dst_ref.at[slot], send_sem=send_sems.at[k], recv_sem=recv_sems.at[k],
            device_id=dev, device_id_type=pl.DeviceIdType.MESH).wait_recv()
    for k, (dev, slot) in enumerate(others):
        send(k, dev, slot).wait_send()
    local.wait()


_EXCHANGE_SEMS = [pltpu.SemaphoreType.DMA((N_DEV - 1,)), pltpu.SemaphoreType.DMA((N_DEV - 1,)), pltpu.SemaphoreType.DMA(())]


def _grid_ends(bsz):
    n, b = pl.program_id(0), pl.program_id(1)
    return (n == 0) & (b == 0), (n == RNN_BLOCKS - 1) & (b == bsz - 1)


def _lru_fwd(xp, ga, cw, vecs, wrg, wig, block, *, name):
    bsz, seq, _ = xp.shape
    groups = seq // 8

    def body(xp_ref, ga_ref, cw_ref, vec_ref, wrg_ref, wig_ref, blk_ref, xb_ref, hs_ref, y_ref, all_ref,
             xpad, a_s, b_s, send_sems, recv_sems, local_sem):
        first, last = _grid_ends(bsz)

        @pl.when(first)
        def _():
            _exchange(blk_ref, all_ref, send_sems, recv_sems, local_sem, gather=True, finish=False)

        xpad[0:8, :] = jnp.zeros((8, RNN_BW), F32)
        xpad[pl.ds(8, seq), :] = xp_ref[...]
        xb = _conv(xpad, cw_ref, seq) + vec_ref[0:1, :]
        xb_ref[...] = xb
        sp = _softplus(-vec_ref[3:4, :])
        _, i, a, mult = _gates(xb, wrg_ref[...], wig_ref[...], vec_ref[1:2, :], vec_ref[2:3, :], sp)
        a_s[...] = a
        b_s[...] = mult * (i * xb)
        row = lax.broadcasted_iota(jnp.int32, (8, RNN_BW), 0)

        def group(g, h):
            r0 = pl.multiple_of(g * 8, 8)
            av = a_s[pl.ds(r0, 8), :]
            bv = b_s[pl.ds(r0, 8), :]
            for k in (1, 2, 4):
                m = row >= k
                bv = jnp.where(m, av * pltpu.roll(bv, k, axis=0) + bv, bv)
                av = jnp.where(m, av * pltpu.roll(av, k, axis=0), av)
            rows = av * h + bv
            hs_ref[pl.ds(r0, 8), :] = rows
            return rows[7:8, :]

        lax.fori_loop(0, groups, group, jnp.zeros((1, RNN_BW), F32))
        gav = ga_ref[...]
        y_ref[...] = (hs_ref[...] * (gav * _sigmoid(gav))).astype(BF16)

        @pl.when(last)
        def _():
            _exchange(blk_ref, all_ref, send_sems, recv_sems, local_sem, gather=True, finish=True)

    sq = _seq_spec(seq)
    shape = (bsz, seq, D_RNN)
    return pl.pallas_call(
        body, grid=(RNN_BLOCKS, bsz),
        in_specs=[sq, sq, _chan_spec(8), _chan_spec(8), _GATE_W_SPEC, _GATE_W_SPEC, _ANY],
        out_specs=[sq, sq, sq, _ANY],
        out_shape=[jax.ShapeDtypeStruct(shape, F32), jax.ShapeDtypeStruct(shape, F32), jax.ShapeDtypeStruct(shape, BF16),
                   jax.ShapeDtypeStruct((N_DEV,) + block.shape, block.dtype)],
        scratch_shapes=[pltpu.VMEM((seq + 8, RNN_BW), F32), pltpu.VMEM((seq, RNN_BW), F32), pltpu.VMEM((seq, RNN_BW), F32)]
        + _EXCHANGE_SEMS,
        compiler_params=_params(("arbitrary", "arbitrary")), name=name)(xp, ga, cw, vecs, wrg, wig, block)


def _lru_bwd(dy, xp, xb, hs, ga, cw, vecs, wrg, wig, parts, *, name):
    bsz, seq, _ = xp.shape
    groups = seq // 8

    def body(dy_ref, xp_ref, xb_ref, hs_ref, ga_ref, cw_ref, vec_ref, wrg_ref, wig_ref,
             parts_ref, dxp_ref, dga_ref, dwrg_ref, dwig_ref, dvec_ref, land_ref, pad, a_s, d_s, lam_s,
             send_sems, recv_sems, local_sem):
        first, last = _grid_ends(bsz)

        @pl.when(first)
        def _():
            _exchange(parts_ref, land_ref, send_sems, recv_sems, local_sem, gather=False, finish=False)

        @pl.when(pl.program_id(1) == 0)
        def _():
            dwrg_ref[...] = jnp.zeros_like(dwrg_ref)
            dwig_ref[...] = jnp.zeros_like(dwig_ref)
            dvec_ref[...] = jnp.zeros_like(dvec_ref)

        xb = xb_ref[...]
        hs = hs_ref[...]
        gav = ga_ref[...]
        dy = dy_ref[...]
        sp = _softplus(-vec_ref[3:4, :])
        wrg = wrg_ref[...]
        wig = wig_ref[...]
        r, i, a, mult = _gates(xb, wrg, wig, vec_ref[1:2, :], vec_ref[2:3, :], sp)
        sg = _sigmoid(gav)
        dga_ref[...] = (dy * hs * (sg * (1.0 + gav * (1.0 - sg)))).astype(BF16)
        d_s[...] = dy * (gav * sg)

        pad[pl.ds(0, seq), :] = a
        pad[pl.ds(seq, 8), :] = jnp.zeros((8, RNN_BW), F32)
        a_s[...] = pad[pl.ds(1, seq), :]
        row = lax.broadcasted_iota(jnp.int32, (8, RNN_BW), 0)

        def group(g, nxt):
            r0 = pl.multiple_of((groups - 1 - g) * 8, 8)
            cv = a_s[pl.ds(r0, 8), :]
            bv = d_s[pl.ds(r0, 8), :]
            for k in (1, 2, 4):
                m = row < 8 - k
                bv = jnp.where(m, cv * pltpu.roll(bv, 8 - k, axis=0) + bv, bv)
                cv = jnp.where(m, cv * pltpu.roll(cv, 8 - k, axis=0), cv)
            rows = cv * nxt + bv
            lam_s[pl.ds(r0, 8), :] = rows
            return rows[0:1, :]

        lax.fori_loop(0, groups, group, jnp.zeros((1, RNN_BW), F32))
        dh = lam_s[...]

        pad[0:8, :] = jnp.zeros((8, RNN_BW), F32)
        pad[pl.ds(8, seq), :] = hs
        da = dh * pad[pl.ds(7, seq), :]
        ixb = i * xb
        dixb = dh * mult
        dla = da * a - (dh * ixb) * (a * a) / mult
        drp = (dla * ((-LRU_C) * sp)) * r * (1.0 - r)
        dip = (dixb * xb) * i * (1.0 - i)
        dvec_ref[0:1, :] += jnp.sum(drp, axis=0, keepdims=True)
        dvec_ref[1:2, :] += jnp.sum(dip, axis=0, keepdims=True)
        dvec_ref[2:3, :] += jnp.sum(dla * ((-LRU_C) * r), axis=0, keepdims=True)
        drpb = drp.astype(BF16)
        dipb = dip.astype(BF16)
        xbb = xb.astype(BF16)
        nt = (((1,), (1,)), ((), ()))
        tn = (((0,), (0,)), ((), ()))
        dxb = (dixb * i
               + lax.dot_general(drpb, wrg, nt, preferred_element_type=F32)
               + lax.dot_general(dipb, wig, nt, preferred_element_type=F32))
        dwrg_ref[...] += lax.dot_general(xbb, drpb, tn, preferred_element_type=F32)
        dwig_ref[...] += lax.dot_general(xbb, dipb, tn, preferred_element_type=F32)
        dvec_ref[3:4, :] += jnp.sum(dxb, axis=0, keepdims=True)

        pad[pl.ds(0, seq), :] = dxb
        pad[pl.ds(seq, 8), :] = jnp.zeros((8, RNN_BW), F32)
        dxp = cw_ref[0:1, :] * pad[pl.ds(CONV_WIDTH - 1, seq), :]
        for k in range(1, CONV_WIDTH):
            dxp = dxp + cw_ref[k:k + 1, :] * pad[pl.ds(CONV_WIDTH - 1 - k, seq), :]
        dxp_ref[...] = dxp.astype(BF16)
        pad[0:8, :] = jnp.zeros((8, RNN_BW), F32)
        pad[pl.ds(8, seq), :] = xp_ref[...]
        for k in range(CONV_WIDTH):
            dvec_ref[4 + k:5 + k, :] += jnp.sum(dxb * pad[pl.ds(8 - (CONV_WIDTH - 1) + k, seq), :], axis=0, keepdims=True)

        @pl.when(last)
        def _():
            _exchange(parts_ref, land_ref, send_sems, recv_sems, local_sem, gather=False, finish=True)

    sq = _seq_spec(seq)
    shape = (bsz, seq, D_RNN)
    gshape = (RNN_BLOCKS, RNN_BW, RNN_BW)
    return pl.pallas_call(
        body, grid=(RNN_BLOCKS, bsz),
        in_specs=[sq, sq, sq, sq, sq, _chan_spec(8), _chan_spec(8), _GATE_W_SPEC, _GATE_W_SPEC, _ANY],
        out_specs=[sq, sq, _GATE_W_SPEC, _GATE_W_SPEC, _chan_spec(8), _ANY],
        out_shape=[jax.ShapeDtypeStruct(shape, BF16), jax.ShapeDtypeStruct(shape, BF16),
                   jax.ShapeDtypeStruct(gshape, F32), jax.ShapeDtypeStruct(gshape, F32),
                   jax.ShapeDtypeStruct((8, D_RNN), F32), jax.ShapeDtypeStruct(parts.shape, parts.dtype)],
        scratch_shapes=[pltpu.VMEM((seq + 8, RNN_BW), F32), pltpu.VMEM((seq, RNN_BW), F32),
                        pltpu.VMEM((seq, RNN_BW), F32), pltpu.VMEM((seq, RNN_BW), F32)] + _EXCHANGE_SEMS,
        compiler_params=_params(("arbitrary", "arbitrary")), name=name)(dy, xp, xb, hs, ga, cw, vecs, wrg, wig, parts)


def _attn_block(seq):
    return min(512, seq)


def _diag_mask(blk):
    return lax.broadcasted_iota(jnp.int32, (blk, blk), 0) <= lax.broadcasted_iota(jnp.int32, (blk, blk), 1)


FWD_HEADS = 4
BWD_HEADS = 2


def _attn_fwd(q, kn, kr, v_t, *, bsz, seq, name):
    t = bsz * seq
    blk = _attn_block(seq)
    nq = seq // blk
    hg = FWD_HEADS

    def body(q_ref, kn_ref, kr_ref, vt_ref, o_ref, lse_ref, acc):
        qi = pl.program_id(2)
        acc[...] = jnp.zeros_like(acc)

        def step(j, carry, diagonal):
            k0 = pl.multiple_of(j * blk, blk)
            kr_j = kr_ref[pl.ds(k0, blk), :]
            out = []
            for h in range(hg):
                m_i, l_i = carry[h]
                kv = jnp.concatenate([kn_ref[pl.ds(k0, blk), h * QK_NOPE:(h + 1) * QK_NOPE], kr_j], axis=1)
                qv = q_ref[:, h * HEAD_PAD:(h + 1) * HEAD_PAD]
                s = lax.dot_general(kv, qv, _NT, preferred_element_type=F32) * ATTN_SCALE
                if diagonal:
                    s = jnp.where(_diag_mask(blk), s, -jnp.inf)
                m_new = jnp.maximum(m_i, jnp.max(s, axis=0, keepdims=True))
                p = jnp.exp(s - m_new)
                alpha = jnp.exp(m_i - m_new)
                l_new = alpha * l_i + jnp.sum(p, axis=0, keepdims=True)
                acc[h] = alpha * acc[h] + jnp.dot(vt_ref[h * V_DIM:(h + 1) * V_DIM, pl.ds(k0, blk)], p.astype(BF16),
                                                  preferred_element_type=F32)
                out.append((m_new, l_new))
            return tuple(out)

        init = tuple((jnp.full((1, blk), -jnp.inf, F32), jnp.zeros((1, blk), F32)) for _ in range(hg))
        carry = lax.fori_loop(0, qi, lambda j, c: step(j, c, False), init)
        stats = step(qi, carry, True)
        for h in range(hg):
            m_i, l_i = stats[h]
            o_ref[:, h * V_DIM:(h + 1) * V_DIM] = (acc[h] / l_i).T
            lse_ref[h] = m_i + jnp.log(l_i)

    return pl.pallas_call(
        body, grid=(bsz, N_HEADS // hg, nq),
        in_specs=[pl.BlockSpec((blk, hg * HEAD_PAD), lambda b, g, i: (b * nq + i, g)),
                  pl.BlockSpec((seq, hg * QK_NOPE), lambda b, g, i: (b, g)),
                  pl.BlockSpec((seq, LANES), lambda b, g, i: (b, 0)),
                  pl.BlockSpec((hg * V_DIM, seq), lambda b, g, i: (g, b))],
        out_specs=[pl.BlockSpec((blk, hg * V_DIM), lambda b, g, i: (b * nq + i, g)),
                   pl.BlockSpec((hg, 1, blk), lambda b, g, i: (g, 0, b * nq + i))],
        out_shape=[jax.ShapeDtypeStruct((t, N_HEADS * V_DIM), F32), jax.ShapeDtypeStruct((N_HEADS, 1, t), F32)],
        scratch_shapes=[pltpu.VMEM((hg, V_DIM, blk), F32)],
        compiler_params=_params(("parallel", "parallel", "parallel")), name=name)(q, kn, kr, v_t)


def _attn_bwd(q, kn, kr, kn_t, kr_t, v, o, lse, do, cos, sin, *, bsz, seq, name):
    t = bsz * seq
    blk = _attn_block(seq)
    nq = seq // blk
    hg = BWD_HEADS

    def body(q_ref, kn_ref, kr_ref, knt_ref, krt_ref, v_ref, o_ref, lse_ref, do_ref, cos_ref, sin_ref,
             dq_ref, dkn_ref, dkr_ref, dv_ref, dqt_acc, dk_acc, dv_acc):
        dqt_acc[...] = jnp.zeros_like(dqt_acc)
        dk_acc[...] = jnp.zeros_like(dk_acc)
        dv_acc[...] = jnp.zeros_like(dv_acc)

        def q_block(i, _):
            q0 = pl.multiple_of(i * blk, blk)
            rows = []
            for h in range(hg):
                dov = do_ref[pl.ds(q0, blk), h * V_DIM:(h + 1) * V_DIM].astype(F32)
                dcol = jnp.sum(dov * o_ref[pl.ds(q0, blk), h * V_DIM:(h + 1) * V_DIM], axis=-1, keepdims=True)
                delta = jnp.broadcast_to(dcol, (blk, LANES)).T[0:1, :]
                rows.append((lse_ref[h, :, pl.ds(q0, blk)], delta))

            def pair(j, diagonal):
                k0 = pl.multiple_of(j * blk, blk)
                kr_j = kr_ref[pl.ds(k0, blk), :]
                krt_j = krt_ref[:, pl.ds(k0, blk)]
                for h in range(hg):
                    lse_i, delta = rows[h]
                    qv = q_ref[pl.ds(q0, blk), h * HEAD_PAD:(h + 1) * HEAD_PAD]
                    dov = do_ref[pl.ds(q0, blk), h * V_DIM:(h + 1) * V_DIM]
                    kv = jnp.concatenate([kn_ref[pl.ds(k0, blk), h * QK_NOPE:(h + 1) * QK_NOPE], kr_j], axis=1)
                    s = lax.dot_general(kv, qv, _NT, preferred_element_type=F32) * ATTN_SCALE
                    p = jnp.exp(s - lse_i)
                    if diagonal:
                        p = jnp.where(_diag_mask(blk), p, 0.0)
                    dv_acc[pl.ds(k0, blk), h * V_DIM:(h + 1) * V_DIM] += jnp.dot(
                        p.astype(BF16), dov, preferred_element_type=F32)
                    dp = lax.dot_general(v_ref[pl.ds(k0, blk), h * V_DIM:(h + 1) * V_DIM], dov, _NT,
                                         preferred_element_type=F32)
                    ds = (p * (dp - delta) * ATTN_SCALE).astype(BF16)
                    dk_acc[pl.ds(k0, blk), h * HEAD_PAD:(h + 1) * HEAD_PAD] += jnp.dot(ds, qv, preferred_element_type=F32)
                    base = h * HEAD_PAD
                    dqt_acc[base:base + QK_NOPE, pl.ds(q0, blk)] += jnp.dot(
                        knt_ref[h * QK_NOPE:(h + 1) * QK_NOPE, pl.ds(k0, blk)], ds, preferred_element_type=F32)
                    dqt_acc[base + QK_NOPE:base + HEAD_PAD, pl.ds(q0, blk)] += jnp.dot(
                        krt_j, ds, preferred_element_type=F32)

            def off_diagonal(j, _):
                pair(j, False)
                return 0

            lax.fori_loop(0, i, off_diagonal, 0)
            pair(i, True)
            return 0

        lax.fori_loop(0, nq, q_block, 0)
        dkr = jnp.zeros((seq, LANES), F32)
        for h in range(hg):
            base = h * HEAD_PAD
            for i in range(nq):
                rows = slice(i * blk, (i + 1) * blk)
                dq = dqt_acc[base:base + HEAD_PAD, rows].T
                dq_ref[rows, base:base + QK_NOPE] = dq[:, :QK_NOPE].astype(BF16)
                dq_ref[rows, base + QK_NOPE:base + HEAD_PAD] = _rope_t(
                    dq[:, QK_NOPE:], cos_ref[rows, :], sin_ref[rows, :]).astype(BF16)
            dkn_ref[:, h * QK_NOPE:(h + 1) * QK_NOPE] = dk_acc[:, base:base + QK_NOPE].astype(BF16)
            dkr = dkr + dk_acc[:, base + QK_NOPE:base + HEAD_PAD]
        dv_ref[...] = dv_acc[...].astype(BF16)

        @pl.when(pl.program_id(1) == 0)
        def _():
            dkr_ref[...] = jnp.zeros_like(dkr_ref)

        dkr_ref[...] += _rope_t(dkr, cos_ref[...], sin_ref[...])

    head = pl.BlockSpec((seq, hg * V_DIM), lambda b, g: (b, g))
    head_t = pl.BlockSpec((hg * V_DIM, seq), lambda b, g: (g, b))
    shared = pl.BlockSpec((seq, LANES), lambda b, g: (b, 0))
    shared_t = pl.BlockSpec((LANES, seq), lambda b, g: (0, b))
    table = pl.BlockSpec((seq, LANES), lambda b, g: (0, 0))
    qspec = pl.BlockSpec((seq, hg * HEAD_PAD), lambda b, g: (b, g))
    return pl.pallas_call(
        body, grid=(bsz, N_HEADS // hg),
        in_specs=[qspec, head, shared, head_t, shared_t, head, head,
                  pl.BlockSpec((hg, 1, seq), lambda b, g: (g, 0, b)), head, table, table],
        out_specs=[qspec, head, shared, head],
        out_shape=[jax.ShapeDtypeStruct((t, N_HEADS * HEAD_PAD), BF16), jax.ShapeDtypeStruct((t, N_HEADS * QK_NOPE), BF16),
                   jax.ShapeDtypeStruct((t, LANES), F32), jax.ShapeDtypeStruct((t, N_HEADS * V_DIM), BF16)],
        scratch_shapes=[pltpu.VMEM((hg * HEAD_PAD, seq), F32), pltpu.VMEM((seq, hg * HEAD_PAD), F32),
                        pltpu.VMEM((seq, hg * V_DIM), F32)],
        compiler_params=_params(("parallel", "arbitrary")), name=name)(q, kn, kr, kn_t, kr_t, v, o, lse, do, cos, sin)


def _head_and_loss(o, g2, x1, target, w_out, g_final, *, name, bt=256):
    t, d = x1.shape
    bt = min(bt, t)
    nt = (((1,), (1,)), ((), ()))

    def body(o_ref, g2_ref, x1_ref, tgt_ref, w_ref, gf_ref, loss_ref, dx2_ref, y2_ref, do_ref, dg2_ref, dgf_ref):
        @pl.when(pl.program_id(0) == 0)
        def _():
            loss_ref[...] = jnp.zeros_like(loss_ref)
            dgf_ref[...] = jnp.zeros_like(dgf_ref)

        ov = o_ref[...]
        gv = g2_ref[...]
        sg = _sigmoid(gv)
        silu = gv * sg
        y2 = (ov * silu).astype(BF16)
        y2_ref[...] = y2
        w = w_ref[...]
        x2 = x1_ref[...] + jnp.dot(y2, w, preferred_element_type=F32)
        r = lax.rsqrt(jnp.mean(x2 * x2, axis=-1, keepdims=True) + EPS)
        nrm = x2 * r
        gf = gf_ref[...]
        err = nrm * gf - tgt_ref[...]
        loss_ref[...] += 0.5 * jnp.sum(jnp.mean(err * err, axis=-1, keepdims=True))
        dyf = err * (1.0 / d)
        dgf_ref[...] += jnp.sum(dyf * nrm, axis=0, keepdims=True)
        dn = dyf * gf
        dx2 = r * (dn - nrm * jnp.mean(dn * nrm, axis=-1, keepdims=True))
        dx2_ref[...] = dx2
        dy2 = lax.dot_general(dx2.astype(BF16), w, nt, preferred_element_type=F32)
        do_ref[...] = (dy2 * silu).astype(BF16)
        dg2_ref[...] = (dy2 * ov * (sg * (1.0 + gv * (1.0 - sg)))).astype(BF16)

    row = pl.BlockSpec((bt, d), lambda i: (i, 0))
    vec = pl.BlockSpec((1, d), lambda i: (0, 0))
    return pl.pallas_call(
        body, grid=(t // bt,),
        in_specs=[row, row, row, row, pl.BlockSpec((d, d), lambda i: (0, 0)), vec],
        out_specs=[pl.BlockSpec((8, LANES), lambda i: (0, 0)), row, row, row, row, vec],
        out_shape=[jax.ShapeDtypeStruct((8, LANES), F32), jax.ShapeDtypeStruct((t, d), F32),
                   jax.ShapeDtypeStruct((t, d), BF16), jax.ShapeDtypeStruct((t, d), BF16),
                   jax.ShapeDtypeStruct((t, d), BF16), jax.ShapeDtypeStruct((1, d), F32)],
        compiler_params=_params(("arbitrary",)), name=name)(o, g2, x1, target, w_out, g_final)


def _sum_parts(parts, *, name, br=GRAD_BLOCK):
    npart, rows, w = parts.shape

    def body(p_ref, o_ref):
        acc = p_ref[0].astype(F32)
        for j in range(1, npart):
            acc = acc + p_ref[j].astype(F32)
        o_ref[...] = acc

    return pl.pallas_call(
        body, grid=(rows // br,), in_specs=[pl.BlockSpec((npart, br, w), lambda i: (0, i, 0))],
        out_specs=pl.BlockSpec((br, w), lambda i: (i, 0)), out_shape=jax.ShapeDtypeStruct((rows, w), F32),
        compiler_params=_params(("parallel",)), name=name)(parts)


def _chip_partial(parts, recv, *, name, br=GRAD_BLOCK):
    _, rows, w = parts.shape
    core = lax.axis_index("c").astype(jnp.int32).reshape(1)

    def body(c_ref, p_ref, r_ref, o_ref):
        o_ref[...] = (p_ref[...] + r_ref[...]).astype(BF16)

    grid_spec = pltpu.PrefetchScalarGridSpec(
        num_scalar_prefetch=1, grid=(4, rows // br),
        in_specs=[pl.BlockSpec((None, br, w), lambda k, i, c_ref: (2 * k + c_ref[0], i, 0)),
                  pl.BlockSpec((None, br, w), lambda k, i, c_ref: (k, i, 0))],
        out_specs=pl.BlockSpec((None, br, w), lambda k, i, c_ref: (k, i, 0)))
    return pl.pallas_call(
        body, grid_spec=grid_spec, out_shape=jax.ShapeDtypeStruct((4, rows, w), BF16),
        compiler_params=_params(("parallel", "parallel")), name=name)(core, parts, recv)


def _as_block(a):
    if a.ndim == 1:
        return a.reshape(1, -1)
    if a.ndim > 2 and a.shape[0] == 1:
        return a.reshape(a.shape[1:])
    return a


def _adamw(g, w, m, v, *, name):
    shape = w.shape
    g, w, m, v = (_as_block(a) for a in (g, w, m, v))

    def body(g_ref, w_ref, m_ref, v_ref, d_ref, nm_ref, nv_ref):
        gv = g_ref[...]
        nm = ADAM_B1 * m_ref[...] + (1.0 - ADAM_B1) * gv
        nv = ADAM_B2 * v_ref[...] + (1.0 - ADAM_B2) * (gv * gv)
        nm_ref[...] = nm
        nv_ref[...] = nv
        m_hat = nm / (1.0 - ADAM_B1 ** ADAM_STEP)
        v_hat = nv / (1.0 - ADAM_B2 ** ADAM_STEP)
        d_ref[...] = (-ADAM_LR) * (m_hat / (jnp.sqrt(v_hat) + ADAM_EPS) + ADAM_WD * w_ref[...])

    whole = pl.BlockSpec(memory_space=pltpu.VMEM)
    outs = pl.pallas_call(
        body, in_specs=[whole] * 4, out_specs=[whole] * 3, out_shape=[jax.ShapeDtypeStruct(w.shape, F32)] * 3,
        compiler_params=_params(), name=name)(g, w, m, v)
    return [o.reshape(shape) for o in outs]


def _all_gather(block, *, name):
    m, n = block.shape

    def body(x_ref, out_ref, send_sems, recv_sems, local_sem):
        x, y, c = _mesh_pos()
        me, sibling = (x, y, c), (x, y, 1 - c)
        chips = [(1 - x, y), (x, 1 - y), (1 - x, 1 - y)]

        def slot(px, py, pc):
            return out_ref.at[4 * px + 2 * py + pc]

        def copy(k, blk, to, src=None):
            return pltpu.make_async_remote_copy(
                src_ref=slot(*blk) if src is None else src, dst_ref=slot(*blk),
                send_sem=send_sems.at[k], recv_sem=recv_sems.at[k], device_id=to, device_id_type=pl.DeviceIdType.MESH)

        mine = pltpu.make_async_copy(x_ref, slot(*me), local_sem)
        mine.start()
        first = [copy(0, me, sibling, src=x_ref)]
        first += [copy(1 + j, me, (*chip, c), src=x_ref) for j, chip in enumerate(chips)]
        for cp in first:
            cp.start()
        passed = [copy(4 + j, (*chip, c), sibling) for j, chip in enumerate(chips)]
        for j, chip in enumerate(chips):
            copy(1 + j, (*chip, c), me).wait_recv()
            passed[j].start()
        copy(0, sibling, me).wait_recv()
        for j, chip in enumerate(chips):
            copy(4 + j, (*chip, 1 - c), me).wait_recv()
        for cp in first + passed:
            cp.wait_send()
        mine.wait()

    return pl.pallas_call(
        body, out_shape=jax.ShapeDtypeStruct((N_DEV, m, n), block.dtype), in_specs=[_ANY], out_specs=_ANY,
        scratch_shapes=[pltpu.SemaphoreType.DMA((7,)), pltpu.SemaphoreType.DMA((7,)), pltpu.SemaphoreType.DMA(())],
        name=name)(block)


def _exchange_d2d(parts, *, name):
    _, rows, w = parts.shape

    def body(p_ref, land_ref, send_sems, recv_sems):
        x, y, c = _mesh_pos()
        sends = []
        for k in range(4):
            cp = pltpu.make_async_remote_copy(
                src_ref=p_ref.at[2 * k + (1 - c)], dst_ref=land_ref.at[k], send_sem=send_sems.at[k],
                recv_sem=recv_sems.at[k], device_id=(x, y, 1 - c), device_id_type=pl.DeviceIdType.MESH)
            cp.start()
            sends.append(cp)
        for cp in sends:
            cp.wait_recv()
        for cp in sends:
            cp.wait_send()

    return pl.pallas_call(
        body, out_shape=jax.ShapeDtypeStruct((4, rows, w), parts.dtype), in_specs=[_ANY], out_specs=_ANY,
        scratch_shapes=[pltpu.SemaphoreType.DMA((4,)), pltpu.SemaphoreType.DMA((4,))], name=name)(parts)


def _exchange_ici(parts, *, name):
    def body(p_ref, land_ref, send_sems, recv_sems, local_sem):
        x, y, c = _mesh_pos()
        mine = pltpu.make_async_copy(p_ref.at[2 * x + y], land_ref.at[3], local_sem)
        mine.start()
        sends = []
        for k, (px, py) in enumerate([(1 - x, y), (x, 1 - y), (1 - x, 1 - y)]):
            cp = pltpu.make_async_remote_copy(
                src_ref=p_ref.at[2 * px + py], dst_ref=land_ref.at[k], send_sem=send_sems.at[k],
                recv_sem=recv_sems.at[k], device_id=(px, py, c), device_id_type=pl.DeviceIdType.MESH)
            cp.start()
            sends.append(cp)
        for cp in sends:
            cp.wait_recv()
        for cp in sends:
            cp.wait_send()
        mine.wait()

    return pl.pallas_call(
        body, out_shape=jax.ShapeDtypeStruct(parts.shape, parts.dtype), in_specs=[_ANY], out_specs=_ANY,
        scratch_shapes=[pltpu.SemaphoreType.DMA((3,)), pltpu.SemaphoreType.DMA((3,)), pltpu.SemaphoreType.DMA(())],
        name=name)(parts)


def _rows(a):
    return a.reshape(-1, PACK_W)


def _pad_to(a, n):
    return jnp.pad(a, (0, n - a.shape[0]))


def _weight_blocks(d):
    small = _rows(_pad_to(jnp.concatenate([d[n].reshape(-1) for n, _ in _SMALL]), 8 * PACK_W))
    block_a = jnp.concatenate([d["w_in_a"][0].T.astype(WIRE), d["w_out_a"][0].astype(WIRE),
                               lax.bitcast_convert_type(small, WIRE).reshape(16, PACK_W)], axis=0)
    w_uq = jnp.pad(d["w_uq"][0], ((0, 0), (0, 0), (0, HEAD_PAD - QK_NOPE - QK_ROPE)))
    pieces = {"w_dkv": d["w_dkv"], "w_uk": d["w_uk"], "w_uv": d["w_uv"], "w_in_b": d["w_in_b"][0].T, "w_uq": w_uq,
              "w_out_b": d["w_out_b"]}
    block_b = jnp.concatenate([_rows(pieces[n]) for n, _ in _PIECES_B]
                              + [jnp.zeros((WIRE_ROWS_B - MATRIX_ROWS_B, PACK_W), F32)], axis=0).astype(WIRE)
    return block_a, block_b


def _weights_a(wall):
    w = {}
    lo, hi = _OFF_A["w_in_a"]
    w["w_in_a_t"] = wall[:, lo:hi].reshape(2 * D_RNN, D_MODEL)
    lo, hi = _OFF_A["w_out_a"]
    w["w_out_a"] = wall[:, lo:hi].reshape(D_RNN, D_MODEL)
    small = lax.bitcast_convert_type(wall[:, MATRIX_ROWS_A:].reshape(N_DEV, 8 * PACK_W, 2), F32)
    off = dict(zip([n for n, _ in _SMALL], [0, 128, 768, 928, 1088, 1248]))
    w["norm_a"] = small[:, :128].reshape(1, D_MODEL)

    def by_channel(lo, rows):
        a = small[:, lo:lo + rows * (D_RNN // N_DEV)].reshape(N_DEV, rows, -1).transpose(1, 0, 2).reshape(rows, D_RNN)
        return jnp.pad(a, ((0, 8 - rows), (0, 0)))

    w["conv_taps"] = by_channel(off["conv_w"], CONV_WIDTH)
    w["lru_vecs"] = by_channel(off["conv_b"], 4)
    return w


def _weights_b(wall):
    piece = {n: wall[:, lo:hi] for n, (lo, hi) in _OFF_B.items()}
    w = {}
    w_dkv = piece["w_dkv"].reshape(D_MODEL, KV_RANK + QK_ROPE)
    w["w_dkv_c"] = w_dkv[:, :KV_RANK]
    w["w_dkv_r"] = jnp.pad(w_dkv[:, KV_RANK:], ((0, 0), (0, LANES - QK_ROPE)))
    w["w_uk"] = piece["w_uk"].reshape(KV_RANK, N_HEADS * QK_NOPE)
    w["w_uv"] = piece["w_uv"].reshape(KV_RANK, N_HEADS * V_DIM)
    w["w_uk_t"], w["w_uv_t"] = w["w_uk"].T, w["w_uv"].T
    w["w_in_b_t"] = piece["w_in_b"].reshape(Q_RANK + N_HEADS * V_DIM, D_MODEL)
    w["w_uq"] = piece["w_uq"].reshape(Q_RANK, N_HEADS * HEAD_PAD)
    w["w_out_b"] = piece["w_out_b"].reshape(N_HEADS * V_DIM, D_MODEL)
    return w


def _pack_rep(d):
    flat = jnp.concatenate([d[n].reshape(-1) for n, _ in _REP])
    return _rows(_pad_to(flat, REP_ROWS * PACK_W))


def _unpack_rep(p, like):
    flat = p.reshape(-1)
    out, off = {}, 0
    for n, k in _REP:
        out[n] = flat[off:off + k].reshape(like[n].shape)
        off += k
    return out


def _by_owner(a):
    return a.reshape(N_DEV, -1, PACK_W)


def _grad_parts_b(g):
    tail = jnp.zeros((N_DEV, WIRE_ROWS_B - MATRIX_ROWS_B, PACK_W), F32)
    return jnp.concatenate([_by_owner(g[n]) for n, _ in _PIECES_B] + [tail], axis=1).astype(BF16)


def _grad_parts_a(g):
    small = jnp.concatenate([
        g["norm_a"].reshape(N_DEV, -1),
        g["conv_w"].reshape(CONV_WIDTH, N_DEV, -1).transpose(1, 0, 2).reshape(N_DEV, -1),
        g["conv_b"].reshape(N_DEV, -1), g["b_rg"].reshape(N_DEV, -1), g["b_ig"].reshape(N_DEV, -1),
        g["lru_lambda"].reshape(N_DEV, -1)], axis=1)
    small = jnp.pad(small, ((0, 0), (0, 8 * PACK_W - small.shape[1]))).reshape(N_DEV, 8, PACK_W)
    half = N_DEV // 2
    w_in_a = jnp.concatenate([h.reshape(half, -1, PACK_W) for h in g["w_in_a_t"]], axis=0)
    rep = _pack_rep(g).reshape(N_DEV, REP_SLICE, PACK_W)
    tail = jnp.zeros((N_DEV, GRAD_ROWS_A - MATRIX_ROWS_A - 8 - REP_SLICE, PACK_W), F32)
    return jnp.concatenate([w_in_a, _by_owner(g["w_out_a"]), small, rep, tail], axis=1)


def _own_grads(sum_a, sum_b):
    out = {}
    lo, hi = _OFF_A["w_in_a"]
    out["w_in_a"] = sum_a[lo:hi].T.reshape(1, D_MODEL, 2 * D_RNN // N_DEV)
    lo, hi = _OFF_A["w_out_a"]
    out["w_out_a"] = sum_a[lo:hi].reshape(1, D_RNN // N_DEV, D_MODEL)
    small = sum_a[MATRIX_ROWS_A:MATRIX_ROWS_A + 8].reshape(-1)
    shapes = {"norm_a": (1, D_MODEL // N_DEV), "conv_w": (1, CONV_WIDTH, D_RNN // N_DEV), "conv_b": (1, D_RNN // N_DEV),
              "b_rg": (1, D_RNN // N_DEV), "b_ig": (1, D_RNN // N_DEV), "lru_lambda": (1, D_RNN // N_DEV)}
    off = 0
    for n, k in _SMALL:
        out[n] = small[off:off + k].reshape(shapes[n])
        off += k
    piece = {n: sum_b[lo:hi] for n, (lo, hi) in _OFF_B.items()}
    out["w_dkv"] = piece["w_dkv"].reshape(D_MODEL // N_DEV, KV_RANK + QK_ROPE)
    out["w_uk"] = piece["w_uk"].reshape(KV_RANK // N_DEV, N_HEADS, QK_NOPE)
    out["w_uv"] = piece["w_uv"].reshape(KV_RANK // N_DEV, N_HEADS, V_DIM)
    out["w_in_b"] = piece["w_in_b"].T.reshape(1, D_MODEL, (Q_RANK + N_HEADS * V_DIM) // N_DEV)
    out["w_uq"] = piece["w_uq"].reshape(1, Q_RANK // N_DEV, N_HEADS, HEAD_PAD)[..., :QK_NOPE + QK_ROPE]
    out["w_out_b"] = piece["w_out_b"].reshape(1, N_HEADS * V_DIM // N_DEV, D_MODEL)
    return out


def _step(x, target, w, rep, block_b, *, bsz, seq):
    t = bsz * seq
    cos, sin = _rope_tables(seq)
    g_a = w["norm_a"]
    g_kv = rep["norm_kv"].reshape(1, -1)
    g_kvn = rep["kv_norm"].reshape(1, -1)
    g_b = rep["norm_b"].reshape(1, -1)
    g_q = rep["q_norm"].reshape(1, -1)
    g_f = rep["final_norm"].reshape(1, -1)
    wrg = rep["w_rg"][0].astype(BF16)
    wig = rep["w_ig"][0].astype(BF16)
    cw8, vecs = w["conv_taps"], w["lru_vecs"]

    def seq3(a):
        return a.reshape(bsz, seq, a.shape[-1])

    def flat(a):
        return a.reshape(t, a.shape[-1])

    h0, xp, ga = _lru_proj_fwd(x, g_a, w["w_in_a_t"], name="lru_proj_fwd")
    xb, hs, y, wall_b = _lru_fwd(seq3(xp), seq3(ga), cw8, vecs, wrg, wig, block_b, name="lru_fwd")
    w = dict(w, **_weights_b(wall_b))
    x1 = _matmul(flat(y), w["w_out_a"], residual=x, name="out_a")
    hk, hq, ck, cqp, g2, ckv, cq, q, kn, v, kr, kn_t, v_t, kr_t = _mla_proj_fwd(
        x1, (g_kv, g_b, g_kvn, g_q), w, cos, sin, seq=seq, name="mla_proj_fwd")
    o, lse = _attn_fwd(q, kn, kr, v_t, bsz=bsz, seq=seq, name="attn_fwd")
    loss, dx2, y2, do, dg2, dgf = _head_and_loss(o, g2, x1, target, w["w_out_b"], g_f, name="head_loss")
    grads = {"final_norm": dgf, "w_out_b": _matmul_tn(y2, dx2, name="d_w_out_b")}
    dq, dkn, dkr, dv = _attn_bwd(q, kn, kr, kn_t, kr_t, v, o, lse, do, cos, sin, bsz=bsz, seq=seq, name="attn_bwd")
    grads["w_uq"] = _matmul_tn(cq, dq, name="d_w_uq")
    dx1, du2, dckr, dgkv, dgb, dgkvn, dgq = _mla_proj_bwd(
        x1, dx2, cqp, ck, dq, dkn, dv, dkr, dg2, (g_kv, g_b, g_kvn, g_q), w, name="mla_proj_bwd")
    grads["norm_kv"], grads["norm_b"], grads["kv_norm"], grads["q_norm"] = dgkv, dgb, dgkvn, dgq
    grads["w_in_b"] = _matmul_tn(du2, hq, name="d_w_in_b_t")
    grads["w_uk"] = _matmul_tn(ckv, dkn, name="d_w_uk")
    grads["w_uv"] = _matmul_tn(ckv, dv, name="d_w_uv")
    grads["w_dkv"] = _matmul_tn(hk, dckr, name="d_w_dkv")[:, :KV_RANK + QK_ROPE]
    parts_b = _grad_parts_b(grads)
    grads["w_out_a"] = _matmul_tn(flat(y), dx1, name="d_w_out_a")
    dy = _matmul(dx1, w["w_out_a"], nt=True, name="d_y")
    dxp, dga, dwrg, dwig, dvec, landed_b = _lru_bwd(
        seq3(dy), seq3(xp), xb, hs, seq3(ga), cw8, vecs, wrg, wig, parts_b, name="lru_bwd")
    dxp, dga = flat(dxp), flat(dga)
    grads["w_rg"], grads["w_ig"] = dwrg, dwig
    grads["b_rg"], grads["b_ig"], grads["conv_b"] = dvec[0], dvec[1], dvec[3]
    lam = vecs[3]
    grads["lru_lambda"] = dvec[2] * (-1.0 / (1.0 + jnp.exp(lam)))
    grads["conv_w"] = dvec[4:4 + CONV_WIDTH]
    grads["w_in_a_t"] = (_matmul_tn(dxp, h0, name="d_w_in_a_x_t"), _matmul_tn(dga, h0, name="d_w_in_a_g_t"))
    dx, dga_norm = _lru_proj_bwd(dxp, dga, x, dx1, g_a, w["w_in_a_t"], name="lru_proj_bwd")
    grads["norm_a"] = dga_norm
    return loss[0, 0], dx, grads, landed_b


def kernel(x, norm_a, w_in_a, conv_w, conv_b, w_rg, b_rg, w_ig, b_ig, lru_lambda, w_out_a, norm_kv, w_dkv, kv_norm, w_uk, w_uv, norm_b, w_in_b, q_norm, w_uq, w_out_b, final_norm, loss_target, m_norm_a, m_w_in_a, m_conv_w, m_conv_b, m_w_rg, m_b_rg, m_w_ig, m_b_ig, m_lru_lambda, m_w_out_a, m_norm_kv, m_w_dkv, m_kv_norm, m_w_uk, m_w_uv, m_norm_b, m_w_in_b, m_q_norm, m_w_uq, m_w_out_b, m_final_norm, v_norm_a, v_w_in_a, v_conv_w, v_conv_b, v_w_rg, v_b_rg, v_w_ig, v_b_ig, v_lru_lambda, v_w_out_a, v_norm_kv, v_w_dkv, v_kv_norm, v_w_uk, v_w_uv, v_norm_b, v_w_in_b, v_q_norm, v_w_uq, v_w_out_b, v_final_norm):
    given = dict(locals())
    wts = {n: given[n] for n in WEIGHTS}
    mom1 = {n: given["m_" + n] for n in WEIGHTS}
    mom2 = {n: given["v_" + n] for n in WEIGHTS}
    bsz, seq, _ = x.shape
    t = bsz * seq

    block_a, block_b = _weight_blocks(wts)
    w = _weights_a(_all_gather(block_a, name="gather_weights_a"))
    loss, dx, grads, landed_b = _step(x.reshape(t, D_MODEL), loss_target.reshape(t, D_MODEL), w, wts, block_b,
                                      bsz=bsz, seq=seq)
    loss = lax.psum(loss, MESH_AXES)

    parts_a = _grad_parts_a(grads)
    from_sibling = _exchange_d2d(parts_a, name="exchange_grads_d2d")
    chip_parts = _chip_partial(parts_a, from_sibling, name="chip_partial_grads")
    landed_a = _exchange_ici(chip_parts, name="exchange_grads_ici")
    sum_a = _sum_parts(landed_a, name="sum_grads_a", br=GRAD_BLOCK)
    sum_b = _sum_parts(landed_b, name="sum_grads_b", br=WIRE_ROWS_B // 2)
    g_own = _own_grads(sum_a, sum_b)
    rep_slice = sum_a[MATRIX_ROWS_A + 8:MATRIX_ROWS_A + 8 + REP_SLICE]
    g_rep = _all_gather(rep_slice, name="gather_replicated").reshape(REP_ROWS, PACK_W)
    g_own.update(_unpack_rep(g_rep, wts))

    deltas, new_m, new_v = {}, {}, {}
    for n in WEIGHTS:
        deltas[n], new_m[n], new_v[n] = _adamw(g_own[n], wts[n], mom1[n], mom2[n], name="adamw_" + n)
    result = [loss, dx.reshape(bsz, seq, D_MODEL)]
    for d in (g_own, deltas, new_m, new_v):
        result.extend(d[n] for n in WEIGHTS)
    return tuple(result)
```

```python
import jax
import jax.numpy as jnp
from jax import lax
from jax.experimental import pallas as pl
from jax.experimental.pallas import tpu as pltpu

F32 = jnp.float32
BF16 = jnp.bfloat16
WIRE = jnp.bfloat16

D_MODEL = 1024
D_RNN = 1280
RNN_BLOCKS = 10
RNN_BW = 128
CONV_WIDTH = 4
LRU_C = 8.0
N_HEADS = 8
QK_NOPE = 128
QK_ROPE = 64
V_DIM = 128
KV_RANK = 256
Q_RANK = 384
ROPE_THETA = 10000.0
EPS = 1e-6
ATTN_SCALE = (QK_NOPE + QK_ROPE) ** -0.5
HEAD_PAD = 256
LANES = 128

ADAM_LR = 0.001
ADAM_B1 = 0.9
ADAM_B2 = 0.999
ADAM_EPS = 1e-08
ADAM_WD = 0.01
ADAM_STEP = 10

N_DEV = 8
MESH_AXES = ("x", "y", "c")
VMEM_LIMIT_BYTES = 56 * 2**20
PACK_W = 1024

_PIECES_A = (("w_in_a", 320),)
_PIECES_B = (("w_out_a", 160), ("w_dkv", 40), ("w_uk", 32), ("w_uv", 32), ("w_in_b", 176), ("w_uq", 96), ("w_out_b", 128))


def _offsets(pieces):
    off, r = {}, 0
    for n, k in pieces:
        off[n] = (r, r + k)
        r += k
    return off, r


_OFF_A, MATRIX_ROWS_A = _offsets(_PIECES_A)
_OFF_B, MATRIX_ROWS_B = _offsets(_PIECES_B)
WIRE_ROWS_A = MATRIX_ROWS_A + 16
WIRE_ROWS_B = 672
_SMALL = (("norm_a", 128), ("conv_w", 640), ("conv_b", 160), ("b_rg", 160), ("b_ig", 160), ("lru_lambda", 160))
_REP = (("w_rg", 163840), ("w_ig", 163840), ("norm_kv", 1024), ("kv_norm", 256), ("norm_b", 1024),
        ("q_norm", 384), ("final_norm", 1024))
REP_ROWS = 384
REP_SLICE = REP_ROWS // N_DEV
GRAD_ROWS_A = 384
GRAD_BLOCK = 192

WEIGHTS = ("norm_a", "w_in_a", "conv_w", "conv_b", "w_rg", "b_rg", "w_ig", "b_ig", "lru_lambda", "w_out_a",
           "norm_kv", "w_dkv", "kv_norm", "w_uk", "w_uv", "norm_b", "w_in_b", "q_norm", "w_uq", "w_out_b",
           "final_norm")


def _params(sem=None):
    return pltpu.CompilerParams(dimension_semantics=sem, vmem_limit_bytes=VMEM_LIMIT_BYTES)


_NT = (((1,), (1,)), ((), ()))
_ANY = pl.BlockSpec(memory_space=pl.ANY)


def _mesh_pos():
    return lax.axis_index("x"), lax.axis_index("y"), lax.axis_index("c")


def _sigmoid(z):
    return 0.5 * jnp.tanh(0.5 * z) + 0.5


def _sigmoid_tail(z):
    return 1.0 / (1.0 + jnp.exp(-z))


def _col_block(n):
    return n if n <= 1408 else n // 2


def _matmul(a, b, *, name, nt=False, out_dtype=F32, residual=None, bm=512):
    m, k = a.shape
    n = b.shape[0] if nt else b.shape[1]
    bm = min(bm, m)
    bn = _col_block(n)
    dims = (((1,), (1,)), ((), ())) if nt else (((1,), (0,)), ((), ()))
    has_res = residual is not None

    def body(*refs):
        a_ref, b_ref, o_ref = refs[0], refs[1], refs[-1]
        acc = lax.dot_general(a_ref[...].astype(BF16), b_ref[...].astype(BF16), dims, preferred_element_type=F32)
        if has_res:
            acc = acc + refs[2][...]
        o_ref[...] = acc.astype(out_dtype)

    in_specs = [pl.BlockSpec((bm, k), lambda i, j: (i, 0)),
                pl.BlockSpec((bn, k), lambda i, j: (j, 0)) if nt else pl.BlockSpec((k, bn), lambda i, j: (0, j))]
    args = [a, b]
    if has_res:
        in_specs.append(pl.BlockSpec((bm, bn), lambda i, j: (i, j)))
        args.append(residual)
    return pl.pallas_call(
        body, grid=(m // bm, n // bn), in_specs=in_specs, out_specs=pl.BlockSpec((bm, bn), lambda i, j: (i, j)),
        out_shape=jax.ShapeDtypeStruct((m, n), out_dtype), compiler_params=_params(("parallel", "parallel")),
        name=name)(*args)


def _matmul_tn(a, b, *, name, bt=512):
    t, m = a.shape
    n = b.shape[1]
    bt = min(bt, t)
    bm, bn = _col_block(m), _col_block(n)

    def body(a_ref, b_ref, o_ref):
        @pl.when(pl.program_id(2) == 0)
        def _():
            o_ref[...] = jnp.zeros_like(o_ref)

        o_ref[...] += lax.dot_general(a_ref[...].astype(BF16), b_ref[...].astype(BF16),
                                      (((0,), (0,)), ((), ())), preferred_element_type=F32)

    return pl.pallas_call(
        body, grid=(m // bm, n // bn, t // bt),
        in_specs=[pl.BlockSpec((bt, bm), lambda i, j, s: (s, i)), pl.BlockSpec((bt, bn), lambda i, j, s: (s, j))],
        out_specs=pl.BlockSpec((bm, bn), lambda i, j, s: (i, j)),
        out_shape=jax.ShapeDtypeStruct((m, n), F32),
        compiler_params=_params(("parallel", "parallel", "arbitrary")), name=name)(a, b)


def _swap_halves(v):
    ax = v.ndim - 1
    lane = lax.broadcasted_iota(jnp.int32, v.shape, ax)
    up = pltpu.roll(v, LANES - QK_ROPE // 2, axis=ax)
    down = pltpu.roll(v, QK_ROPE // 2, axis=ax)
    return jnp.where(lane < QK_ROPE // 2, up, jnp.where(lane < QK_ROPE, down, 0.0))


def _rope(v, cos, sin):
    return v * cos + _swap_halves(v) * sin


def _rope_t(d, cos, sin):
    return d * cos + _swap_halves(d * sin)


def _rope_tables(seq):
    pos = jnp.arange(seq, dtype=F32)
    inv = ROPE_THETA ** (-jnp.arange(0, QK_ROPE, 2, dtype=F32) / QK_ROPE)
    ang = pos[:, None] * inv[None, :]
    cos, sin = jnp.cos(ang), jnp.sin(ang)
    zero = jnp.zeros((seq, LANES - QK_ROPE), F32)
    return jnp.concatenate([cos, cos, zero], axis=1), jnp.concatenate([-sin, sin, zero], axis=1)


def _rms(v):
    return v * lax.rsqrt(jnp.mean(v * v, axis=-1, keepdims=True) + EPS)


def _const_spec(a):
    return pl.BlockSpec(a.shape, lambda i: (0,) * a.ndim)


def _lru_proj_fwd(x, g_a, w_in_t, *, name, bt=256):
    t, d = x.shape
    bt = min(bt, t)
    n = w_in_t.shape[0] // 2

    def body(x_ref, g_ref, wt_ref, h_ref, xp_ref, ga_ref):
        h = (_rms(x_ref[...]) * g_ref[...]).astype(BF16)
        h_ref[...] = h
        xp_ref[...] = lax.dot_general(h, wt_ref[0:n, :], _NT, preferred_element_type=F32)
        ga_ref[...] = lax.dot_general(h, wt_ref[n:2 * n, :], _NT, preferred_element_type=F32)

    row = lambda w: pl.BlockSpec((bt, w), lambda i: (i, 0))
    return pl.pallas_call(
        body, grid=(t // bt,), in_specs=[row(d), _const_spec(g_a), _const_spec(w_in_t)],
        out_specs=[row(d), row(n), row(n)],
        out_shape=[jax.ShapeDtypeStruct((t, d), BF16), jax.ShapeDtypeStruct((t, n), F32), jax.ShapeDtypeStruct((t, n), F32)],
        compiler_params=_params(("parallel",)), name=name)(x, g_a, w_in_t)


def _mla_proj_fwd(x1, gains, w, cos, sin, *, seq, name, bt=256):
    t, d = x1.shape
    bt = min(bt, seq)
    per_seq = seq // bt
    g_kv, g_b, g_kvn, g_q = gains
    consts = [g_kv, g_b, g_kvn, g_q, w["w_dkv_c"], w["w_dkv_r"], w["w_in_b_t"], w["w_uk"], w["w_uv"],
              w["w_uk_t"], w["w_uv_t"], w["w_uq"]]

    def body(x_ref, cos_ref, sin_ref, gkv_ref, gb_ref, gkvn_ref, gq_ref, wdc_ref, wdr_ref, wbt_ref,
             wuk_ref, wuv_ref, wukt_ref, wuvt_ref, wuq_ref,
             hk_ref, hq_ref, ck_ref, cqp_ref, g2_ref, ckv_ref, cq_ref, q_ref, kn_ref, v_ref, kr_ref, knt_ref, vt_ref, krt_ref):
        nrm = _rms(x_ref[...])
        hk = (nrm * gkv_ref[...]).astype(BF16)
        hq = (nrm * gb_ref[...]).astype(BF16)
        hk_ref[...] = hk
        hq_ref[...] = hq
        ck = jnp.dot(hk, wdc_ref[...], preferred_element_type=F32)
        ck_ref[...] = ck
        cqp = lax.dot_general(hq, wbt_ref[0:Q_RANK, :], _NT, preferred_element_type=F32)
        cqp_ref[...] = cqp
        g2_ref[...] = lax.dot_general(hq, wbt_ref[Q_RANK:, :], _NT, preferred_element_type=F32)
        cosv, sinv = cos_ref[...], sin_ref[...]
        kr = _rope(jnp.dot(hk, wdr_ref[...], preferred_element_type=F32), cosv, sinv)
        kr_ref[...] = kr.astype(BF16)
        krt_ref[...] = kr.T.astype(BF16)
        ckv = (_rms(ck) * gkvn_ref[...]).astype(BF16)
        ckv_ref[...] = ckv
        kn_ref[...] = jnp.dot(ckv, wuk_ref[...], preferred_element_type=F32).astype(BF16)
        v_ref[...] = jnp.dot(ckv, wuv_ref[...], preferred_element_type=F32).astype(BF16)
        knt_ref[...] = lax.dot_general(wukt_ref[...], ckv, _NT, preferred_element_type=F32).astype(BF16)
        vt_ref[...] = lax.dot_general(wuvt_ref[...], ckv, _NT, preferred_element_type=F32).astype(BF16)
        cq = (_rms(cqp) * gq_ref[...]).astype(BF16)
        cq_ref[...] = cq
        for h in range(N_HEADS):
            qh = jnp.dot(cq, wuq_ref[:, h * HEAD_PAD:(h + 1) * HEAD_PAD], preferred_element_type=F32)
            q_ref[:, h * HEAD_PAD:h * HEAD_PAD + QK_NOPE] = qh[:, :QK_NOPE].astype(BF16)
            q_ref[:, h * HEAD_PAD + QK_NOPE:(h + 1) * HEAD_PAD] = _rope(qh[:, QK_NOPE:], cosv, sinv).astype(BF16)

    row = lambda w_: pl.BlockSpec((bt, w_), lambda i: (i, 0))
    col = lambda h_: pl.BlockSpec((h_, bt), lambda i: (0, i))
    tab = pl.BlockSpec((bt, LANES), lambda i: (i % per_seq, 0))
    nh = N_HEADS * V_DIM
    shapes = [((t, d), BF16), ((t, d), BF16), ((t, KV_RANK), F32), ((t, Q_RANK), F32), ((t, nh), F32), ((t, KV_RANK), BF16),
              ((t, Q_RANK), BF16), ((t, N_HEADS * HEAD_PAD), BF16), ((t, nh), BF16), ((t, nh), BF16), ((t, LANES), BF16),
              ((nh, t), BF16), ((nh, t), BF16), ((LANES, t), BF16)]
    out_specs = [row(d), row(d), row(KV_RANK), row(Q_RANK), row(nh), row(KV_RANK), row(Q_RANK), row(N_HEADS * HEAD_PAD),
                 row(nh), row(nh), row(LANES), col(nh), col(nh), col(LANES)]
    return pl.pallas_call(
        body, grid=(t // bt,), in_specs=[row(d), tab, tab] + [_const_spec(a) for a in consts], out_specs=out_specs,
        out_shape=[jax.ShapeDtypeStruct(s, dt) for s, dt in shapes],
        compiler_params=_params(("parallel",)), name=name)(x1, cos, sin, *consts)


def _rms_bwd_rows(xv, dn):
    r = lax.rsqrt(jnp.mean(xv * xv, axis=-1, keepdims=True) + EPS)
    nrm = xv * r
    return r * (dn - nrm * jnp.mean(dn * nrm, axis=-1, keepdims=True)), nrm


def _col_sum(v):
    return jnp.sum(v, axis=0, keepdims=True)


def _lru_proj_bwd(dxp, dga, x, dx1, g_a, w_in_t, *, name, bt=256):
    t, d = x.shape
    bt = min(bt, t)
    n = w_in_t.shape[0] // 2

    def body(dxp_ref, dga_ref, x_ref, dx1_ref, g_ref, wt_ref, dx_ref, dg_ref):
        @pl.when(pl.program_id(0) == 0)
        def _():
            dg_ref[...] = jnp.zeros_like(dg_ref)

        dh = (jnp.dot(dxp_ref[...], wt_ref[0:n, :], preferred_element_type=F32)
              + jnp.dot(dga_ref[...], wt_ref[n:2 * n, :], preferred_element_type=F32))
        dxn, nrm = _rms_bwd_rows(x_ref[...], dh * g_ref[...])
        dg_ref[...] += _col_sum(dh * nrm)
        dx_ref[...] = dx1_ref[...] + dxn

    row = lambda w: pl.BlockSpec((bt, w), lambda i: (i, 0))
    return pl.pallas_call(
        body, grid=(t // bt,),
        in_specs=[row(n), row(n), row(d), row(d), _const_spec(g_a), _const_spec(w_in_t)],
        out_specs=[row(d), _const_spec(g_a)],
        out_shape=[jax.ShapeDtypeStruct((t, d), F32), jax.ShapeDtypeStruct((1, d), F32)],
        compiler_params=_params(("arbitrary",)), name=name)(dxp, dga, x, dx1, g_a, w_in_t)


def _mla_proj_bwd(x1, dx2, cqp, ck, dq, dkn, dv, dkr, dg2, gains, w, *, name, bt=256):
    t, d = x1.shape
    bt = min(bt, t)
    g_kv, g_b, g_kvn, g_q = gains
    consts = [g_kv, g_b, g_kvn, g_q, w["w_dkv_c"], w["w_dkv_r"], w["w_in_b_t"], w["w_uk"], w["w_uv"], w["w_uq"]]
    nh = N_HEADS * V_DIM

    def body(x1_ref, dx2_ref, cqp_ref, ck_ref, dq_ref, dkn_ref, dv_ref, dkr_ref, dg2_ref,
             gkv_ref, gb_ref, gkvn_ref, gq_ref, wdc_ref, wdr_ref, wbt_ref, wuk_ref, wuv_ref, wuq_ref,
             dx1_ref, du2_ref, dckr_ref, dgkv_ref, dgb_ref, dgkvn_ref, dgq_ref):
        @pl.when(pl.program_id(0) == 0)
        def _():
            for ref in (dgkv_ref, dgb_ref, dgkvn_ref, dgq_ref):
                ref[...] = jnp.zeros_like(ref)

        dot_nt = lambda a, b: lax.dot_general(a, b, _NT, preferred_element_type=F32)
        dcq = dot_nt(dq_ref[...], wuq_ref[...])
        dcqp, nq = _rms_bwd_rows(cqp_ref[...], dcq * gq_ref[...])
        dgq_ref[...] += _col_sum(dcq * nq)
        dcqp = dcqp.astype(BF16)
        dg2 = dg2_ref[...]
        du2_ref[:, :Q_RANK] = dcqp
        du2_ref[:, Q_RANK:] = dg2
        dhq = (jnp.dot(dcqp, wbt_ref[0:Q_RANK, :], preferred_element_type=F32)
               + jnp.dot(dg2, wbt_ref[Q_RANK:, :], preferred_element_type=F32))
        dckv = dot_nt(dkn_ref[...], wuk_ref[...]) + dot_nt(dv_ref[...], wuv_ref[...])
        dck, nc = _rms_bwd_rows(ck_ref[...], dckv * gkvn_ref[...])
        dgkvn_ref[...] += _col_sum(dckv * nc)
        dck = dck.astype(BF16)
        dkr = dkr_ref[...].astype(BF16)
        dckr_ref[:, :KV_RANK] = dck
        dckr_ref[:, KV_RANK:] = dkr
        dhk = dot_nt(dck, wdc_ref[...]) + dot_nt(dkr, wdr_ref[...])
        dxn, n1 = _rms_bwd_rows(x1_ref[...], dhq * gb_ref[...] + dhk * gkv_ref[...])
        dgb_ref[...] += _col_sum(dhq * n1)
        dgkv_ref[...] += _col_sum(dhk * n1)
        dx1_ref[...] = dx2_ref[...] + dxn

    row = lambda w_: pl.BlockSpec((bt, w_), lambda i: (i, 0))
    vec = lambda w_: pl.BlockSpec((1, w_), lambda i: (0, 0))
    in_specs = [row(d), row(d), row(Q_RANK), row(KV_RANK), row(N_HEADS * HEAD_PAD), row(nh), row(nh), row(LANES), row(nh)]
    return pl.pallas_call(
        body, grid=(t // bt,), in_specs=in_specs + [_const_spec(a) for a in consts],
        out_specs=[row(d), row(Q_RANK + nh), row(KV_RANK + LANES), vec(d), vec(d), vec(KV_RANK), vec(Q_RANK)],
        out_shape=[jax.ShapeDtypeStruct((t, d), F32), jax.ShapeDtypeStruct((t, Q_RANK + nh), BF16),
                   jax.ShapeDtypeStruct((t, KV_RANK + LANES), BF16), jax.ShapeDtypeStruct((1, d), F32),
                   jax.ShapeDtypeStruct((1, d), F32), jax.ShapeDtypeStruct((1, KV_RANK), F32),
                   jax.ShapeDtypeStruct((1, Q_RANK), F32)],
        compiler_params=_params(("arbitrary",)), name=name)(x1, dx2, cqp, ck, dq, dkn, dv, dkr, dg2, *consts)


def _softplus(z):
    return jnp.maximum(z, 0.0) + jnp.log1p(jnp.exp(-jnp.abs(z)))


def _neg_expm1(z):
    series = -z * (1.0 + z * (1.0 / 2) * (1.0 + z * (1.0 / 3) * (1.0 + z * (1.0 / 4))))
    return jnp.where(z > -0.01, series, 1.0 - jnp.exp(z))


def _gates(xb, wrg, wig, brg, big, sp):
    xbb = xb.astype(BF16)
    r = _sigmoid_tail(jnp.dot(xbb, wrg, preferred_element_type=F32) + brg)
    i = _sigmoid(jnp.dot(xbb, wig, preferred_element_type=F32) + big)
    la = (-LRU_C) * r * sp
    a = jnp.exp(la)
    mult = jnp.sqrt(_neg_expm1(2.0 * la))
    return r, i, a, mult


def _conv(xpad_ref, cw_ref, seq):
    acc = cw_ref[0:1, :] * xpad_ref[pl.ds(8 - (CONV_WIDTH - 1), seq), :]
    for k in range(1, CONV_WIDTH):
        acc = acc + cw_ref[k:k + 1, :] * xpad_ref[pl.ds(8 - (CONV_WIDTH - 1) + k, seq), :]
    return acc


def _seq_spec(seq):
    return pl.BlockSpec((None, seq, RNN_BW), lambda n, b: (b, 0, n))


def _chan_spec(rows):
    return pl.BlockSpec((rows, RNN_BW), lambda n, b: (0, n))


_GATE_W_SPEC = pl.BlockSpec((None, RNN_BW, RNN_BW), lambda n, b: (n, 0, 0))


def _peers():
    x, y, c = _mesh_pos()
    others = []
    for k in range(1, N_DEV):
        px = 1 - x if k & 4 else x
        py = 1 - y if k & 2 else y
        pc = 1 - c if k & 1 else c
        others.append(((px, py, pc), 4 * px + 2 * py + pc))
    return 4 * x + 2 * y + c, others


def _exchange(src_ref, dst_ref, send_sems, recv_sems, local_sem, *, gather, finish):
    me, others = _peers()

    def send(k, dev, slot):
        return pltpu.make_async_remote_copy(
            src_ref=src_ref if gather else src_ref.at[slot], dst_ref=dst_ref.at[me], send_sem=send_sems.at[k],
            recv_sem=recv_sems.at[k], device_id=dev, device_id_type=pl.DeviceIdType.MESH)

    local = pltpu.make_async_copy(src_ref if gather else src_ref.at[me], dst_ref.at[me], local_sem)
    if not finish:
        local.start()
        for k, (dev, slot) in enumerate(others):
            send(k, dev, slot).start()
        return
    for k, (dev, slot) in enumerate(others):
        pltpu.make_async_remote_copy(
            src_ref=dst_ref.at[slot], dst_ref=dst_ref.at[slot], send_sem=send_sems.at[k], recv_sem=recv_sems.at[k],
            device_id=dev, device_id_type=pl.DeviceIdType.MESH).wait_recv()
    for k, (dev, slot) in enumerate(others):
        send(k, dev, slot).wait_send()
    local.wait()


_EXCHANGE_SEMS = [pltpu.SemaphoreType.DMA((N_DEV - 1,)), pltpu.SemaphoreType.DMA((N_DEV - 1,)), pltpu.SemaphoreType.DMA(())]


def _grid_ends(bsz):
    n, b = pl.program_id(0), pl.program_id(1)
    return (n == 0) & (b == 0), (n == RNN_BLOCKS - 1) & (b == bsz - 1)


def _lru_fwd(xp, ga, cw, vecs, wrg, wig, block, *, name):
    bsz, seq, _ = xp.shape
    groups = seq // 8

    def body(xp_ref, ga_ref, cw_ref, vec_ref, wrg_ref, wig_ref, blk_ref, xb_ref, hs_ref, y_ref, all_ref,
             xpad, a_s, b_s, send_sems, recv_sems, local_sem):
        first, last = _grid_ends(bsz)

        @pl.when(first)
        def _():
            _exchange(blk_ref, all_ref, send_sems, recv_sems, local_sem, gather=True, finish=False)

        xpad[0:8, :] = jnp.zeros((8, RNN_BW), F32)
        xpad[pl.ds(8, seq), :] = xp_ref[...]
        xb = _conv(xpad, cw_ref, seq) + vec_ref[0:1, :]
        xb_ref[...] = xb
        sp = _softplus(-vec_ref[3:4, :])
        _, i, a, mult = _gates(xb, wrg_ref[...], wig_ref[...], vec_ref[1:2, :], vec_ref[2:3, :], sp)
        a_s[...] = a
        b_s[...] = mult * (i * xb)
        row = lax.broadcasted_iota(jnp.int32, (8, RNN_BW), 0)

        def group(g, h):
            r0 = pl.multiple_of(g * 8, 8)
            av = a_s[pl.ds(r0, 8), :]
            bv = b_s[pl.ds(r0, 8), :]
            for k in (1, 2, 4):
                m = row >= k
                bv = jnp.where(m, av * pltpu.roll(bv, k, axis=0) + bv, bv)
                av = jnp.where(m, av * pltpu.roll(av, k, axis=0), av)
            rows = av * h + bv
            hs_ref[pl.ds(r0, 8), :] = rows
            return rows[7:8, :]

        lax.fori_loop(0, groups, group, jnp.zeros((1, RNN_BW), F32))
        gav = ga_ref[...]
        y_ref[...] = (hs_ref[...] * (gav * _sigmoid(gav))).astype(BF16)

        @pl.when(last)
        def _():
            _exchange(blk_ref, all_ref, send_sems, recv_sems, local_sem, gather=True, finish=True)

    sq = _seq_spec(seq)
    shape = (bsz, seq, D_RNN)
    return pl.pallas_call(
        body, grid=(RNN_BLOCKS, bsz),
        in_specs=[sq, sq, _chan_spec(8), _chan_spec(8), _GATE_W_SPEC, _GATE_W_SPEC, _ANY],
        out_specs=[sq, sq, sq, _ANY],
        out_shape=[jax.ShapeDtypeStruct(shape, F32), jax.ShapeDtypeStruct(shape, F32), jax.ShapeDtypeStruct(shape, BF16),
                   jax.ShapeDtypeStruct((N_DEV,) + block.shape, block.dtype)],
        scratch_shapes=[pltpu.VMEM((seq + 8, RNN_BW), F32), pltpu.VMEM((seq, RNN_BW), F32), pltpu.VMEM((seq, RNN_BW), F32)]
        + _EXCHANGE_SEMS,
        compiler_params=_params(("arbitrary", "arbitrary")), name=name)(xp, ga, cw, vecs, wrg, wig, block)


def _lru_bwd(dy, xp, xb, hs, ga, cw, vecs, wrg, wig, parts, *, name):
    bsz, seq, _ = xp.shape
    groups = seq // 8

    def body(dy_ref, xp_ref, xb_ref, hs_ref, ga_ref, cw_ref, vec_ref, wrg_ref, wig_ref,
             parts_ref, dxp_ref, dga_ref, dwrg_ref, dwig_ref, dvec_ref, land_ref, pad, a_s, d_s, lam_s,
             send_sems, recv_sems, local_sem):
        first, last = _grid_ends(bsz)

        @pl.when(first)
        def _():
            _exchange(parts_ref, land_ref, send_sems, recv_sems, local_sem, gather=False, finish=False)

        @pl.when(pl.program_id(1) == 0)
        def _():
            dwrg_ref[...] = jnp.zeros_like(dwrg_ref)
            dwig_ref[...] = jnp.zeros_like(dwig_ref)
            dvec_ref[...] = jnp.zeros_like(dvec_ref)

        xb = xb_ref[...]
        hs = hs_ref[...]
        gav = ga_ref[...]
        dy = dy_ref[...]
        sp = _softplus(-vec_ref[3:4, :])
        wrg = wrg_ref[...]
        wig = wig_ref[...]
        r, i, a, mult = _gates(xb, wrg, wig, vec_ref[1:2, :], vec_ref[2:3, :], sp)
        sg = _sigmoid(gav)
        dga_ref[...] = (dy * hs * (sg * (1.0 + gav * (1.0 - sg)))).astype(BF16)
        d_s[...] = dy * (gav * sg)

        pad[pl.ds(0, seq), :] = a
        pad[pl.ds(seq, 8), :] = jnp.zeros((8, RNN_BW), F32)
        a_s[...] = pad[pl.ds(1, seq), :]
        row = lax.broadcasted_iota(jnp.int32, (8, RNN_BW), 0)

        def group(g, nxt):
            r0 = pl.multiple_of((groups - 1 - g) * 8, 8)
            cv = a_s[pl.ds(r0, 8), :]
            bv = d_s[pl.ds(r0, 8), :]
            for k in (1, 2, 4):
                m = row < 8 - k
                bv = jnp.where(m, cv * pltpu.roll(bv, 8 - k, axis=0) + bv, bv)
                cv = jnp.where(m, cv * pltpu.roll(cv, 8 - k, axis=0), cv)
            rows = cv * nxt + bv
            lam_s[pl.ds(r0, 8), :] = rows
            return rows[0:1, :]

        lax.fori_loop(0, groups, group, jnp.zeros((1, RNN_BW), F32))
        dh = lam_s[...]

        pad[0:8, :] = jnp.zeros((8, RNN_BW), F32)
        pad[pl.ds(8, seq), :] = hs
        da = dh * pad[pl.ds(7, seq), :]
        ixb = i * xb
        dixb = dh * mult
        dla = da * a - (dh * ixb) * (a * a) / mult
        drp = (dla * ((-LRU_C) * sp)) * r * (1.0 - r)
        dip = (dixb * xb) * i * (1.0 - i)
        dvec_ref[0:1, :] += jnp.sum(drp, axis=0, keepdims=True)
        dvec_ref[1:2, :] += jnp.sum(dip, axis=0, keepdims=True)
        dvec_ref[2:3, :] += jnp.sum(dla * ((-LRU_C) * r), axis=0, keepdims=True)
        drpb = drp.astype(BF16)
        dipb = dip.astype(BF16)
        xbb = xb.astype(BF16)
        nt = (((1,), (1,)), ((), ()))
        tn = (((0,), (0,)), ((), ()))
        dxb = (dixb * i
               + lax.dot_general(drpb, wrg, nt, preferred_element_type=F32)
               + lax.dot_general(dipb, wig, nt, preferred_element_type=F32))
        dwrg_ref[...] += lax.dot_general(xbb, drpb, tn, preferred_element_type=F32)
        dwig_ref[...] += lax.dot_general(xbb, dipb, tn, preferred_element_type=F32)
        dvec_ref[3:4, :] += jnp.sum(dxb, axis=0, keepdims=True)

        pad[pl.ds(0, seq), :] = dxb
        pad[pl.ds(seq, 8), :] = jnp.zeros((8, RNN_BW), F32)
        dxp = cw_ref[0:1, :] * pad[pl.ds(CONV_WIDTH - 1, seq), :]
        for k in range(1, CONV_WIDTH):
            dxp = dxp + cw_ref[k:k + 1, :] * pad[pl.ds(CONV_WIDTH - 1 - k, seq), :]
        dxp_ref[...] = dxp.astype(BF16)
        pad[0:8, :] = jnp.zeros((8, RNN_BW), F32)
        pad[pl.ds(8, seq), :] = xp_ref[...]
        for k in range(CONV_WIDTH):
            dvec_ref[4 + k:5 + k, :] += jnp.sum(dxb * pad[pl.ds(8 - (CONV_WIDTH - 1) + k, seq), :], axis=0, keepdims=True)

        @pl.when(last)
        def _():
            _exchange(parts_ref, land_ref, send_sems, recv_sems, local_sem, gather=False, finish=True)

    sq = _seq_spec(seq)
    shape = (bsz, seq, D_RNN)
    gshape = (RNN_BLOCKS, RNN_BW, RNN_BW)
    return pl.pallas_call(
        body, grid=(RNN_BLOCKS, bsz),
        in_specs=[sq, sq, sq, sq, sq, _chan_spec(8), _chan_spec(8), _GATE_W_SPEC, _GATE_W_SPEC, _ANY],
        out_specs=[sq, sq, _GATE_W_SPEC, _GATE_W_SPEC, _chan_spec(8), _ANY],
        out_shape=[jax.ShapeDtypeStruct(shape, BF16), jax.ShapeDtypeStruct(shape, BF16),
                   jax.ShapeDtypeStruct(gshape, F32), jax.ShapeDtypeStruct(gshape, F32),
                   jax.ShapeDtypeStruct((8, D_RNN), F32), jax.ShapeDtypeStruct(parts.shape, parts.dtype)],
        scratch_shapes=[pltpu.VMEM((seq + 8, RNN_BW), F32), pltpu.VMEM((seq, RNN_BW), F32),
                        pltpu.VMEM((seq, RNN_BW), F32), pltpu.VMEM((seq, RNN_BW), F32)] + _EXCHANGE_SEMS,
        compiler_params=_params(("arbitrary", "arbitrary")), name=name)(dy, xp, xb, hs, ga, cw, vecs, wrg, wig, parts)


def _attn_block(seq):
    return min(512, seq)


def _diag_mask(blk):
    return lax.broadcasted_iota(jnp.int32, (blk, blk), 0) <= lax.broadcasted_iota(jnp.int32, (blk, blk), 1)


FWD_HEADS = 4
BWD_HEADS = 2


def _attn_fwd(q, kn, kr, v_t, *, bsz, seq, name):
    t = bsz * seq
    blk = _attn_block(seq)
    nq = seq // blk
    hg = FWD_HEADS

    def body(q_ref, kn_ref, kr_ref, vt_ref, o_ref, lse_ref, acc):
        qi = pl.program_id(2)
        acc[...] = jnp.zeros_like(acc)

        def step(j, carry, diagonal):
            k0 = pl.multiple_of(j * blk, blk)
            kr_j = kr_ref[pl.ds(k0, blk), :]
            out = []
            for h in range(hg):
                m_i, l_i = carry[h]
                kv = jnp.concatenate([kn_ref[pl.ds(k0, blk), h * QK_NOPE:(h + 1) * QK_NOPE], kr_j], axis=1)
                qv = q_ref[:, h * HEAD_PAD:(h + 1) * HEAD_PAD]
                s = lax.dot_general(kv, qv, _NT, preferred_element_type=F32) * ATTN_SCALE
                if diagonal:
                    s = jnp.where(_diag_mask(blk), s, -jnp.inf)
                m_new = jnp.maximum(m_i, jnp.max(s, axis=0, keepdims=True))
                p = jnp.exp(s - m_new)
                alpha = jnp.exp(m_i - m_new)
                l_new = alpha * l_i + jnp.sum(p, axis=0, keepdims=True)
                acc[h] = alpha * acc[h] + jnp.dot(vt_ref[h * V_DIM:(h + 1) * V_DIM, pl.ds(k0, blk)], p.astype(BF16),
                                                  preferred_element_type=F32)
                out.append((m_new, l_new))
            return tuple(out)

        init = tuple((jnp.full((1, blk), -jnp.inf, F32), jnp.zeros((1, blk), F32)) for _ in range(hg))
        carry = lax.fori_loop(0, qi, lambda j, c: step(j, c, False), init)
        stats = step(qi, carry, True)
        for h in range(hg):
            m_i, l_i = stats[h]
            o_ref[:, h * V_DIM:(h + 1) * V_DIM] = (acc[h] / l_i).T
            lse_ref[h] = m_i + jnp.log(l_i)

    return pl.pallas_call(
        body, grid=(bsz, N_HEADS // hg, nq),
        in_specs=[pl.BlockSpec((blk, hg * HEAD_PAD), lambda b, g, i: (b * nq + i, g)),
                  pl.BlockSpec((seq, hg * QK_NOPE), lambda b, g, i: (b, g)),
                  pl.BlockSpec((seq, LANES), lambda b, g, i: (b, 0)),
                  pl.BlockSpec((hg * V_DIM, seq), lambda b, g, i: (g, b))],
        out_specs=[pl.BlockSpec((blk, hg * V_DIM), lambda b, g, i: (b * nq + i, g)),
                   pl.BlockSpec((hg, 1, blk), lambda b, g, i: (g, 0, b * nq + i))],
        out_shape=[jax.ShapeDtypeStruct((t, N_HEADS * V_DIM), F32), jax.ShapeDtypeStruct((N_HEADS, 1, t), F32)],
        scratch_shapes=[pltpu.VMEM((hg, V_DIM, blk), F32)],
        compiler_params=_params(("parallel", "parallel", "parallel")), name=name)(q, kn, kr, v_t)


def _attn_bwd(q, kn, kr, kn_t, kr_t, v, o, lse, do, cos, sin, *, bsz, seq, name):
    t = bsz * seq
    blk = _attn_block(seq)
    nq = seq // blk
    hg = BWD_HEADS

    def body(q_ref, kn_ref, kr_ref, knt_ref, krt_ref, v_ref, o_ref, lse_ref, do_ref, cos_ref, sin_ref,
             dq_ref, dkn_ref, dkr_ref, dv_ref, dqt_acc, dk_acc, dv_acc):
        dqt_acc[...] = jnp.zeros_like(dqt_acc)
        dk_acc[...] = jnp.zeros_like(dk_acc)
        dv_acc[...] = jnp.zeros_like(dv_acc)

        def q_block(i, _):
            q0 = pl.multiple_of(i * blk, blk)
            rows = []
            for h in range(hg):
                dov = do_ref[pl.ds(q0, blk), h * V_DIM:(h + 1) * V_DIM].astype(F32)
                dcol = jnp.sum(dov * o_ref[pl.ds(q0, blk), h * V_DIM:(h + 1) * V_DIM], axis=-1, keepdims=True)
                delta = jnp.broadcast_to(dcol, (blk, LANES)).T[0:1, :]
                rows.append((lse_ref[h, :, pl.ds(q0, blk)], delta))

            def pair(j, diagonal):
                k0 = pl.multiple_of(j * blk, blk)
                kr_j = kr_ref[pl.ds(k0, blk), :]
                krt_j = krt_ref[:, pl.ds(k0, blk)]
                for h in range(hg):
                    lse_i, delta = rows[h]
                    qv = q_ref[pl.ds(q0, blk), h * HEAD_PAD:(h + 1) * HEAD_PAD]
                    dov = do_ref[pl.ds(q0, blk), h * V_DIM:(h + 1) * V_DIM]
                    kv = jnp.concatenate([kn_ref[pl.ds(k0, blk), h * QK_NOPE:(h + 1) * QK_NOPE], kr_j], axis=1)
                    s = lax.dot_general(kv, qv, _NT, preferred_element_type=F32) * ATTN_SCALE
                    p = jnp.exp(s - lse_i)
                    if diagonal:
                        p = jnp.where(_diag_mask(blk), p, 0.0)
                    dv_acc[pl.ds(k0, blk), h * V_DIM:(h + 1) * V_DIM] += jnp.dot(
                        p.astype(BF16), dov, preferred_element_type=F32)
                    dp = lax.dot_general(v_ref[pl.ds(k0, blk), h * V_DIM:(h + 1) * V_DIM], dov, _NT,
                                         preferred_element_type=F32)
                    ds = (p * (dp - delta) * ATTN_SCALE).astype(BF16)
                    dk_acc[pl.ds(k0, blk), h * HEAD_PAD:(h + 1) * HEAD_PAD] += jnp.dot(ds, qv, preferred_element_type=F32)
                    base = h * HEAD_PAD
                    dqt_acc[base:base + QK_NOPE, pl.ds(q0, blk)] += jnp.dot(
                        knt_ref[h * QK_NOPE:(h + 1) * QK_NOPE, pl.ds(k0, blk)], ds, preferred_element_type=F32)
                    dqt_acc[base + QK_NOPE:base + HEAD_PAD, pl.ds(q0, blk)] += jnp.dot(
                        krt_j, ds, preferred_element_type=F32)

            def off_diagonal(j, _):
                pair(j, False)
                return 0

            lax.fori_loop(0, i, off_diagonal, 0)
            pair(i, True)
            return 0

        lax.fori_loop(0, nq, q_block, 0)
        dkr = jnp.zeros((seq, LANES), F32)
        for h in range(hg):
            base = h * HEAD_PAD
            for i in range(nq):
                rows = slice(i * blk, (i + 1) * blk)
                dq = dqt_acc[base:base + HEAD_PAD, rows].T
                dq_ref[rows, base:base + QK_NOPE] = dq[:, :QK_NOPE].astype(BF16)
                dq_ref[rows, base + QK_NOPE:base + HEAD_PAD] = _rope_t(
                    dq[:, QK_NOPE:], cos_ref[rows, :], sin_ref[rows, :]).astype(BF16)
            dkn_ref[:, h * QK_NOPE:(h + 1) * QK_NOPE] = dk_acc[:, base:base + QK_NOPE].astype(BF16)
            dkr = dkr + dk_acc[:, base + QK_NOPE:base + HEAD_PAD]
        dv_ref[...] = dv_acc[...].astype(BF16)

        @pl.when(pl.program_id(1) == 0)
        def _():
            dkr_ref[...] = jnp.zeros_like(dkr_ref)

        dkr_ref[...] += _rope_t(dkr, cos_ref[...], sin_ref[...])

    head = pl.BlockSpec((seq, hg * V_DIM), lambda b, g: (b, g))
    head_t = pl.BlockSpec((hg * V_DIM, seq), lambda b, g: (g, b))
    shared = pl.BlockSpec((seq, LANES), lambda b, g: (b, 0))
    shared_t = pl.BlockSpec((LANES, seq), lambda b, g: (0, b))
    table = pl.BlockSpec((seq, LANES), lambda b, g: (0, 0))
    qspec = pl.BlockSpec((seq, hg * HEAD_PAD), lambda b, g: (b, g))
    return pl.pallas_call(
        body, grid=(bsz, N_HEADS // hg),
        in_specs=[qspec, head, shared, head_t, shared_t, head, head,
                  pl.BlockSpec((hg, 1, seq), lambda b, g: (g, 0, b)), head, table, table],
        out_specs=[qspec, head, shared, head],
        out_shape=[jax.ShapeDtypeStruct((t, N_HEADS * HEAD_PAD), BF16), jax.ShapeDtypeStruct((t, N_HEADS * QK_NOPE), BF16),
                   jax.ShapeDtypeStruct((t, LANES), F32), jax.ShapeDtypeStruct((t, N_HEADS * V_DIM), BF16)],
        scratch_shapes=[pltpu.VMEM((hg * HEAD_PAD, seq), F32), pltpu.VMEM((seq, hg * HEAD_PAD), F32),
                        pltpu.VMEM((seq, hg * V_DIM), F32)],
        compiler_params=_params(("parallel", "arbitrary")), name=name)(q, kn, kr, kn_t, kr_t, v, o, lse, do, cos, sin)


def _head_and_loss(o, g2, x1, target, w_out, g_final, *, name, bt=256):
    t, d = x1.shape
    bt = min(bt, t)
    nt = (((1,), (1,)), ((), ()))

    def body(o_ref, g2_ref, x1_ref, tgt_ref, w_ref, gf_ref, loss_ref, dx2_ref, y2_ref, do_ref, dg2_ref, dgf_ref):
        @pl.when(pl.program_id(0) == 0)
        def _():
            loss_ref[...] = jnp.zeros_like(loss_ref)
            dgf_ref[...] = jnp.zeros_like(dgf_ref)

        ov = o_ref[...]
        gv = g2_ref[...]
        sg = _sigmoid(gv)
        silu = gv * sg
        y2 = (ov * silu).astype(BF16)
        y2_ref[...] = y2
        w = w_ref[...]
        x2 = x1_ref[...] + jnp.dot(y2, w, preferred_element_type=F32)
        r = lax.rsqrt(jnp.mean(x2 * x2, axis=-1, keepdims=True) + EPS)
        nrm = x2 * r
        gf = gf_ref[...]
        err = nrm * gf - tgt_ref[...]
        loss_ref[...] += 0.5 * jnp.sum(jnp.mean(err * err, axis=-1, keepdims=True))
        dyf = err * (1.0 / d)
        dgf_ref[...] += jnp.sum(dyf * nrm, axis=0, keepdims=True)
        dn = dyf * gf
        dx2 = r * (dn - nrm * jnp.mean(dn * nrm, axis=-1, keepdims=True))
        dx2_ref[...] = dx2
        dy2 = lax.dot_general(dx2.astype(BF16), w, nt, preferred_element_type=F32)
        do_ref[...] = (dy2 * silu).astype(BF16)
        dg2_ref[...] = (dy2 * ov * (sg * (1.0 + gv * (1.0 - sg)))).astype(BF16)

    row = pl.BlockSpec((bt, d), lambda i: (i, 0))
    vec = pl.BlockSpec((1, d), lambda i: (0, 0))
    return pl.pallas_call(
        body, grid=(t // bt,),
        in_specs=[row, row, row, row, pl.BlockSpec((d, d), lambda i: (0, 0)), vec],
        out_specs=[pl.BlockSpec((8, LANES), lambda i: (0, 0)), row, row, row, row, vec],
        out_shape=[jax.ShapeDtypeStruct((8, LANES), F32), jax.ShapeDtypeStruct((t, d), F32),
                   jax.ShapeDtypeStruct((t, d), BF16), jax.ShapeDtypeStruct((t, d), BF16),
                   jax.ShapeDtypeStruct((t, d), BF16), jax.ShapeDtypeStruct((1, d), F32)],
        compiler_params=_params(("arbitrary",)), name=name)(o, g2, x1, target, w_out, g_final)


def _sum_parts(parts, *, name, br=GRAD_BLOCK):
    npart, rows, w = parts.shape

    def body(p_ref, o_ref):
        acc = p_ref[0].astype(F32)
        for j in range(1, npart):
            acc = acc + p_ref[j].astype(F32)
        o_ref[...] = acc

    return pl.pallas_call(
        body, grid=(rows // br,), in_specs=[pl.BlockSpec((npart, br, w), lambda i: (0, i, 0))],
        out_specs=pl.BlockSpec((br, w), lambda i: (i, 0)), out_shape=jax.ShapeDtypeStruct((rows, w), F32),
        compiler_params=_params(("parallel",)), name=name)(parts)


def _chip_partial(parts, recv, *, name, br=GRAD_BLOCK):
    _, rows, w = parts.shape
    core = lax.axis_index("c").astype(jnp.int32).reshape(1)

    def body(c_ref, p_ref, r_ref, o_ref):
        o_ref[...] = (p_ref[...] + r_ref[...]).astype(BF16)

    grid_spec = pltpu.PrefetchScalarGridSpec(
        num_scalar_prefetch=1, grid=(4, rows // br),
        in_specs=[pl.BlockSpec((None, br, w), lambda k, i, c_ref: (2 * k + c_ref[0], i, 0)),
                  pl.BlockSpec((None, br, w), lambda k, i, c_ref: (k, i, 0))],
        out_specs=pl.BlockSpec((None, br, w), lambda k, i, c_ref: (k, i, 0)))
    return pl.pallas_call(
        body, grid_spec=grid_spec, out_shape=jax.ShapeDtypeStruct((4, rows, w), BF16),
        compiler_params=_params(("parallel", "parallel")), name=name)(core, parts, recv)


def _as_block(a):
    if a.ndim == 1:
        return a.reshape(1, -1)
    if a.ndim > 2 and a.shape[0] == 1:
        return a.reshape(a.shape[1:])
    return a


def _adamw(g, w, m, v, *, name):
    shape = w.shape
    g, w, m, v = (_as_block(a) for a in (g, w, m, v))

    def body(g_ref, w_ref, m_ref, v_ref, d_ref, nm_ref, nv_ref):
        gv = g_ref[...]
        nm = ADAM_B1 * m_ref[...] + (1.0 - ADAM_B1) * gv
        nv = ADAM_B2 * v_ref[...] + (1.0 - ADAM_B2) * (gv * gv)
        nm_ref[...] = nm
        nv_ref[...] = nv
        m_hat = nm / (1.0 - ADAM_B1 ** ADAM_STEP)
        v_hat = nv / (1.0 - ADAM_B2 ** ADAM_STEP)
        d_ref[...] = (-ADAM_LR) * (m_hat / (jnp.sqrt(v_hat) + ADAM_EPS) + ADAM_WD * w_ref[...])

    whole = pl.BlockSpec(memory_space=pltpu.VMEM)
    outs = pl.pallas_call(
        body, in_specs=[whole] * 4, out_specs=[whole] * 3, out_shape=[jax.ShapeDtypeStruct(w.shape, F32)] * 3,
        compiler_params=_params(), name=name)(g, w, m, v)
    return [o.reshape(shape) for o in outs]


def _all_gather(block, *, name):
    m, n = block.shape

    def body(x_ref, out_ref, send_sems, recv_sems, local_sem):
        x, y, c = _mesh_pos()
        me, sibling = (x, y, c), (x, y, 1 - c)
        chips = [(1 - x, y), (x, 1 - y), (1 - x, 1 - y)]

        def slot(px, py, pc):
            return out_ref.at[4 * px + 2 * py + pc]

        def copy(k, blk, to, src=None):
            return pltpu.make_async_remote_copy(
                src_ref=slot(*blk) if src is None else src, dst_ref=slot(*blk),
                send_sem=send_sems.at[k], recv_sem=recv_sems.at[k], device_id=to, device_id_type=pl.DeviceIdType.MESH)

        mine = pltpu.make_async_copy(x_ref, slot(*me), local_sem)
        mine.start()
        first = [copy(0, me, sibling, src=x_ref)]
        first += [copy(1 + j, me, (*chip, c), src=x_ref) for j, chip in enumerate(chips)]
        for cp in first:
            cp.start()
        passed = [copy(4 + j, (*chip, c), sibling) for j, chip in enumerate(chips)]
        for j, chip in enumerate(chips):
            copy(1 + j, (*chip, c), me).wait_recv()
            passed[j].start()
        copy(0, sibling, me).wait_recv()
        for j, chip in enumerate(chips):
            copy(4 + j, (*chip, 1 - c), me).wait_recv()
        for cp in first + passed:
            cp.wait_send()
        mine.wait()

    return pl.pallas_call(
        body, out_shape=jax.ShapeDtypeStruct((N_DEV, m, n), block.dtype), in_specs=[_ANY], out_specs=_ANY,
        scratch_shapes=[pltpu.SemaphoreType.DMA((7,)), pltpu.SemaphoreType.DMA((7,)), pltpu.SemaphoreType.DMA(())],
        name=name)(block)


def _exchange_d2d(parts, *, name):
    _, rows, w = parts.shape

    def body(p_ref, land_ref, send_sems, recv_sems):
        x, y, c = _mesh_pos()
        sends = []
        for k in range(4):
            cp = pltpu.make_async_remote_copy(
                src_ref=p_ref.at[2 * k + (1 - c)], dst_ref=land_ref.at[k], send_sem=send_sems.at[k],
                recv_sem=recv_sems.at[k], device_id=(x, y, 1 - c), device_id_type=pl.DeviceIdType.MESH)
            cp.start()
            sends.append(cp)
        for cp in sends:
            cp.wait_recv()
        for cp in sends:
            cp.wait_send()

    return pl.pallas_call(
        body, out_shape=jax.ShapeDtypeStruct((4, rows, w), parts.dtype), in_specs=[_ANY], out_specs=_ANY,
        scratch_shapes=[pltpu.SemaphoreType.DMA((4,)), pltpu.SemaphoreType.DMA((4,))], name=name)(parts)


def _exchange_ici(parts, *, name):
    def body(p_ref, land_ref, send_sems, recv_sems, local_sem):
        x, y, c = _mesh_pos()
        mine = pltpu.make_async_copy(p_ref.at[2 * x + y], land_ref.at[3], local_sem)
        mine.start()
        sends = []
        for k, (px, py) in enumerate([(1 - x, y), (x, 1 - y), (1 - x, 1 - y)]):
            cp = pltpu.make_async_remote_copy(
                src_ref=p_ref.at[2 * px + py], dst_ref=land_ref.at[k], send_sem=send_sems.at[k],
                recv_sem=recv_sems.at[k], device_id=(px, py, c), device_id_type=pl.DeviceIdType.MESH)
            cp.start()
            sends.append(cp)
        for cp in sends:
            cp.wait_recv()
        for cp in sends:
            cp.wait_send()
        mine.wait()

    return pl.pallas_call(
        body, out_shape=jax.ShapeDtypeStruct(parts.shape, parts.dtype), in_specs=[_ANY], out_specs=_ANY,
        scratch_shapes=[pltpu.SemaphoreType.DMA((3,)), pltpu.SemaphoreType.DMA((3,)), pltpu.SemaphoreType.DMA(())],
        name=name)(parts)


def _rows(a):
    return a.reshape(-1, PACK_W)


def _pad_to(a, n):
    return jnp.pad(a, (0, n - a.shape[0]))


def _weight_blocks(d):
    small = _rows(_pad_to(jnp.concatenate([d[n].reshape(-1) for n, _ in _SMALL]), 8 * PACK_W))
    block_a = jnp.concatenate([d["w_in_a"][0].T.astype(WIRE), lax.bitcast_convert_type(small, WIRE).reshape(16, PACK_W)],
                              axis=0)
    w_uq = jnp.pad(d["w_uq"][0], ((0, 0), (0, 0), (0, HEAD_PAD - QK_NOPE - QK_ROPE)))
    pieces = {"w_out_a": d["w_out_a"], "w_dkv": d["w_dkv"], "w_uk": d["w_uk"], "w_uv": d["w_uv"],
              "w_in_b": d["w_in_b"][0].T, "w_uq": w_uq, "w_out_b": d["w_out_b"]}
    block_b = jnp.concatenate([_rows(pieces[n]) for n, _ in _PIECES_B]
                              + [jnp.zeros((WIRE_ROWS_B - MATRIX_ROWS_B, PACK_W), F32)], axis=0).astype(WIRE)
    return block_a, block_b


def _weights_a(wall):
    w = {}
    lo, hi = _OFF_A["w_in_a"]
    w["w_in_a_t"] = wall[:, lo:hi].reshape(2 * D_RNN, D_MODEL)
    small = lax.bitcast_convert_type(wall[:, MATRIX_ROWS_A:].reshape(N_DEV, 8 * PACK_W, 2), F32)
    off = dict(zip([n for n, _ in _SMALL], [0, 128, 768, 928, 1088, 1248]))
    w["norm_a"] = small[:, :128].reshape(1, D_MODEL)

    def by_channel(lo, rows):
        a = small[:, lo:lo + rows * (D_RNN // N_DEV)].reshape(N_DEV, rows, -1).transpose(1, 0, 2).reshape(rows, D_RNN)
        return jnp.pad(a, ((0, 8 - rows), (0, 0)))

    w["conv_taps"] = by_channel(off["conv_w"], CONV_WIDTH)
    w["lru_vecs"] = by_channel(off["conv_b"], 4)
    return w


def _weights_b(wall):
    piece = {n: wall[:, lo:hi] for n, (lo, hi) in _OFF_B.items()}
    w = {"w_out_a": piece["w_out_a"].reshape(D_RNN, D_MODEL)}
    w_dkv = piece["w_dkv"].reshape(D_MODEL, KV_RANK + QK_ROPE)
    w["w_dkv_c"] = w_dkv[:, :KV_RANK]
    w["w_dkv_r"] = jnp.pad(w_dkv[:, KV_RANK:], ((0, 0), (0, LANES - QK_ROPE)))
    w["w_uk"] = piece["w_uk"].reshape(KV_RANK, N_HEADS * QK_NOPE)
    w["w_uv"] = piece["w_uv"].reshape(KV_RANK, N_HEADS * V_DIM)
    w["w_uk_t"], w["w_uv_t"] = w["w_uk"].T, w["w_uv"].T
    w["w_in_b_t"] = piece["w_in_b"].reshape(Q_RANK + N_HEADS * V_DIM, D_MODEL)
    w["w_uq"] = piece["w_uq"].reshape(Q_RANK, N_HEADS * HEAD_PAD)
    w["w_out_b"] = piece["w_out_b"].reshape(N_HEADS * V_DIM, D_MODEL)
    return w


def _pack_rep(d):
    flat = jnp.concatenate([d[n].reshape(-1) for n, _ in _REP])
    return _rows(_pad_to(flat, REP_ROWS * PACK_W))


def _unpack_rep(p, like):
    flat = p.reshape(-1)
    out, off = {}, 0
    for n, k in _REP:
        out[n] = flat[off:off + k].reshape(like[n].shape)
        off += k
    return out


def _by_owner(a):
    return a.reshape(N_DEV, -1, PACK_W)


def _grad_parts_b(g):
    tail = jnp.zeros((N_DEV, WIRE_ROWS_B - MATRIX_ROWS_B, PACK_W), F32)
    return jnp.concatenate([_by_owner(g[n]) for n, _ in _PIECES_B] + [tail], axis=1).astype(BF16)


def _grad_parts_a(g):
    small = jnp.concatenate([
        g["norm_a"].reshape(N_DEV, -1),
        g["conv_w"].reshape(CONV_WIDTH, N_DEV, -1).transpose(1, 0, 2).reshape(N_DEV, -1),
        g["conv_b"].reshape(N_DEV, -1), g["b_rg"].reshape(N_DEV, -1), g["b_ig"].reshape(N_DEV, -1),
        g["lru_lambda"].reshape(N_DEV, -1)], axis=1)
    small = jnp.pad(small, ((0, 0), (0, 8 * PACK_W - small.shape[1]))).reshape(N_DEV, 8, PACK_W)
    half = N_DEV // 2
    w_in_a = jnp.concatenate([h.reshape(half, -1, PACK_W) for h in g["w_in_a_t"]], axis=0)
    rep = _pack_rep(g).reshape(N_DEV, REP_SLICE, PACK_W)
    tail = jnp.zeros((N_DEV, GRAD_ROWS_A - MATRIX_ROWS_A - 8 - REP_SLICE, PACK_W), F32)
    return jnp.concatenate([w_in_a, small, rep, tail], axis=1)


def _own_grads(sum_a, sum_b):
    out = {}
    lo, hi = _OFF_A["w_in_a"]
    out["w_in_a"] = sum_a[lo:hi].T.reshape(1, D_MODEL, 2 * D_RNN // N_DEV)
    small = sum_a[MATRIX_ROWS_A:MATRIX_ROWS_A + 8].reshape(-1)
    shapes = {"norm_a": (1, D_MODEL // N_DEV), "conv_w": (1, CONV_WIDTH, D_RNN // N_DEV), "conv_b": (1, D_RNN // N_DEV),
              "b_rg": (1, D_RNN // N_DEV), "b_ig": (1, D_RNN // N_DEV), "lru_lambda": (1, D_RNN // N_DEV)}
    off = 0
    for n, k in _SMALL:
        out[n] = small[off:off + k].reshape(shapes[n])
        off += k
    piece = {n: sum_b[lo:hi] for n, (lo, hi) in _OFF_B.items()}
    out["w_out_a"] = piece["w_out_a"].reshape(1, D_RNN // N_DEV, D_MODEL)
    out["w_dkv"] = piece["w_dkv"].reshape(D_MODEL // N_DEV, KV_RANK + QK_ROPE)
    out["w_uk"] = piece["w_uk"].reshape(KV_RANK // N_DEV, N_HEADS, QK_NOPE)
    out["w_uv"] = piece["w_uv"].reshape(KV_RANK // N_DEV, N_HEADS, V_DIM)
    out["w_in_b"] = piece["w_in_b"].T.reshape(1, D_MODEL, (Q_RANK + N_HEADS * V_DIM) // N_DEV)
    out["w_uq"] = piece["w_uq"].reshape(1, Q_RANK // N_DEV, N_HEADS, HEAD_PAD)[..., :QK_NOPE + QK_ROPE]
    out["w_out_b"] = piece["w_out_b"].reshape(1, N_HEADS * V_DIM // N_DEV, D_MODEL)
    return out


def _step(x, target, w, rep, block_b, *, bsz, seq):
    t = bsz * seq
    cos, sin = _rope_tables(seq)
    g_a = w["norm_a"]
    g_kv = rep["norm_kv"].reshape(1, -1)
    g_kvn = rep["kv_norm"].reshape(1, -1)
    g_b = rep["norm_b"].reshape(1, -1)
    g_q = rep["q_norm"].reshape(1, -1)
    g_f = rep["final_norm"].reshape(1, -1)
    wrg = rep["w_rg"][0].astype(BF16)
    wig = rep["w_ig"][0].astype(BF16)
    cw8, vecs = w["conv_taps"], w["lru_vecs"]

    def seq3(a):
        return a.reshape(bsz, seq, a.shape[-1])

    def flat(a):
        return a.reshape(t, a.shape[-1])

    h0, xp, ga = _lru_proj_fwd(x, g_a, w["w_in_a_t"], name="lru_proj_fwd")
    xb, hs, y, wall_b = _lru_fwd(seq3(xp), seq3(ga), cw8, vecs, wrg, wig, block_b, name="lru_fwd")
    w = dict(w, **_weights_b(wall_b))
    x1 = _matmul(flat(y), w["w_out_a"], residual=x, name="out_a")
    hk, hq, ck, cqp, g2, ckv, cq, q, kn, v, kr, kn_t, v_t, kr_t = _mla_proj_fwd(
        x1, (g_kv, g_b, g_kvn, g_q), w, cos, sin, seq=seq, name="mla_proj_fwd")
    o, lse = _attn_fwd(q, kn, kr, v_t, bsz=bsz, seq=seq, name="attn_fwd")
    loss, dx2, y2, do, dg2, dgf = _head_and_loss(o, g2, x1, target, w["w_out_b"], g_f, name="head_loss")
    grads = {"final_norm": dgf, "w_out_b": _matmul_tn(y2, dx2, name="d_w_out_b")}
    dq, dkn, dkr, dv = _attn_bwd(q, kn, kr, kn_t, kr_t, v, o, lse, do, cos, sin, bsz=bsz, seq=seq, name="attn_bwd")
    grads["w_uq"] = _matmul_tn(cq, dq, name="d_w_uq")
    dx1, du2, dckr, dgkv, dgb, dgkvn, dgq = _mla_proj_bwd(
        x1, dx2, cqp, ck, dq, dkn, dv, dkr, dg2, (g_kv, g_b, g_kvn, g_q), w, name="mla_proj_bwd")
    grads["norm_kv"], grads["norm_b"], grads["kv_norm"], grads["q_norm"] = dgkv, dgb, dgkvn, dgq
    grads["w_in_b"] = _matmul_tn(du2, hq, name="d_w_in_b_t")
    grads["w_uk"] = _matmul_tn(ckv, dkn, name="d_w_uk")
    grads["w_uv"] = _matmul_tn(ckv, dv, name="d_w_uv")
    grads["w_dkv"] = _matmul_tn(hk, dckr, name="d_w_dkv")[:, :KV_RANK + QK_ROPE]
    grads["w_out_a"] = _matmul_tn(flat(y), dx1, name="d_w_out_a")
    parts_b = _grad_parts_b(grads)
    dy = _matmul(dx1, w["w_out_a"], nt=True, name="d_y")
    dxp, dga, dwrg, dwig, dvec, landed_b = _lru_bwd(
        seq3(dy), seq3(xp), xb, hs, seq3(ga), cw8, vecs, wrg, wig, parts_b, name="lru_bwd")
    dxp, dga = flat(dxp), flat(dga)
    grads["w_rg"], grads["w_ig"] = dwrg, dwig
    grads["b_rg"], grads["b_ig"], grads["conv_b"] = dvec[0], dvec[1], dvec[3]
    lam = vecs[3]
    grads["lru_lambda"] = dvec[2] * (-1.0 / (1.0 + jnp.exp(lam)))
    grads["conv_w"] = dvec[4:4 + CONV_WIDTH]
    grads["w_in_a_t"] = (_matmul_tn(dxp, h0, name="d_w_in_a_x_t"), _matmul_tn(dga, h0, name="d_w_in_a_g_t"))
    dx, dga_norm = _lru_proj_bwd(dxp, dga, x, dx1, g_a, w["w_in_a_t"], name="lru_proj_bwd")
    grads["norm_a"] = dga_norm
    return loss[0, 0], dx, grads, landed_b


def kernel(x, norm_a, w_in_a, conv_w, conv_b, w_rg, b_rg, w_ig, b_ig, lru_lambda, w_out_a, norm_kv, w_dkv, kv_norm, w_uk, w_uv, norm_b, w_in_b, q_norm, w_uq, w_out_b, final_norm, loss_target, m_norm_a, m_w_in_a, m_conv_w, m_conv_b, m_w_rg, m_b_rg, m_w_ig, m_b_ig, m_lru_lambda, m_w_out_a, m_norm_kv, m_w_dkv, m_kv_norm, m_w_uk, m_w_uv, m_norm_b, m_w_in_b, m_q_norm, m_w_uq, m_w_out_b, m_final_norm, v_norm_a, v_w_in_a, v_conv_w, v_conv_b, v_w_rg, v_b_rg, v_w_ig, v_b_ig, v_lru_lambda, v_w_out_a, v_norm_kv, v_w_dkv, v_kv_norm, v_w_uk, v_w_uv, v_norm_b, v_w_in_b, v_q_norm, v_w_uq, v_w_out_b, v_final_norm):
    given = dict(locals())
    wts = {n: given[n] for n in WEIGHTS}
    mom1 = {n: given["m_" + n] for n in WEIGHTS}
    mom2 = {n: given["v_" + n] for n in WEIGHTS}
    bsz, seq, _ = x.shape
    t = bsz * seq

    block_a, block_b = _weight_blocks(wts)
    w = _weights_a(_all_gather(block_a, name="gather_weights_a"))
    loss, dx, grads, landed_b = _step(x.reshape(t, D_MODEL), loss_target.reshape(t, D_MODEL), w, wts, block_b,
                                      bsz=bsz, seq=seq)
    loss = lax.psum(loss, MESH_AXES)

    parts_a = _grad_parts_a(grads)
    from_sibling = _exchange_d2d(parts_a, name="exchange_grads_d2d")
    chip_parts = _chip_partial(parts_a, from_sibling, name="chip_partial_grads")
    landed_a = _exchange_ici(chip_parts, name="exchange_grads_ici")
    sum_a = _sum_parts(landed_a, name="sum_grads_a", br=GRAD_BLOCK)
    sum_b = _sum_parts(landed_b, name="sum_grads_b", br=WIRE_ROWS_B // 2)
    g_own = _own_grads(sum_a, sum_b)
    rep_slice = sum_a[MATRIX_ROWS_A + 8:MATRIX_ROWS_A + 8 + REP_SLICE]
    g_rep = _all_gather(rep_slice, name="gather_replicated").reshape(REP_ROWS, PACK_W)
    g_own.update(_unpack_rep(g_rep, wts))

    deltas, new_m, new_v = {}, {}, {}
    for n in WEIGHTS:
        deltas[n], new_m[n], new_v[n] = _adamw(g_own[n], wts[n], mom1[n], mom2[n], name="adamw_" + n)
    result = [loss, dx.reshape(bsz, seq, D_MODEL)]
    for d in (g_own, deltas, new_m, new_v):
        result.extend(d[n] for n in WEIGHTS)
    return tuple(result)
```

```python
import jax
import jax.numpy as jnp
from jax import lax
from jax.experimental import pallas as pl
from jax.experimental.pallas import tpu as pltpu

F32 = jnp.float32
BF16 = jnp.bfloat16
WIRE = jnp.bfloat16

D_MODEL = 1024
D_RNN = 1280
RNN_BLOCKS = 10
RNN_BW = 128
CONV_WIDTH = 4
LRU_C = 8.0
N_HEADS = 8
QK_NOPE = 128
QK_ROPE = 64
V_DIM = 128
KV_RANK = 256
Q_RANK = 384
ROPE_THETA = 10000.0
EPS = 1e-6
ATTN_SCALE = (QK_NOPE + QK_ROPE) ** -0.5
HEAD_PAD = 256
LANES = 128

ADAM_LR = 0.001
ADAM_B1 = 0.9
ADAM_B2 = 0.999
ADAM_EPS = 1e-08
ADAM_WD = 0.01
ADAM_STEP = 10

N_DEV = 8
MESH_AXES = ("x", "y", "c")
VMEM_LIMIT_BYTES = 56 * 2**20
PACK_W = 1024

_PIECES_A = (("w_in_a", 320),)
_PIECES_B = (("w_out_a", 160), ("w_dkv", 40), ("w_uk", 32), ("w_uv", 32), ("w_in_b", 176), ("w_uq", 96), ("w_out_b", 128))


def _offsets(pieces):
    off, r = {}, 0
    for n, k in pieces:
        off[n] = (r, r + k)
        r += k
    return off, r


_OFF_A, MATRIX_ROWS_A = _offsets(_PIECES_A)
_OFF_B, MATRIX_ROWS_B = _offsets(_PIECES_B)
WIRE_ROWS_A = MATRIX_ROWS_A + 16
WIRE_ROWS_B = 672
_SMALL = (("norm_a", 128), ("conv_w", 640), ("conv_b", 160), ("b_rg", 160), ("b_ig", 160), ("lru_lambda", 160))
_REP = (("w_rg", 163840), ("w_ig", 163840), ("norm_kv", 1024), ("kv_norm", 256), ("norm_b", 1024),
        ("q_norm", 384), ("final_norm", 1024))
REP_ROWS = 384
REP_SLICE = REP_ROWS // N_DEV
GRAD_ROWS_A = 384
GRAD_BLOCK = 192

WEIGHTS = ("norm_a", "w_in_a", "conv_w", "conv_b", "w_rg", "b_rg", "w_ig", "b_ig", "lru_lambda", "w_out_a",
           "norm_kv", "w_dkv", "kv_norm", "w_uk", "w_uv", "norm_b", "w_in_b", "q_norm", "w_uq", "w_out_b",
           "final_norm")


def _params(sem=None):
    return pltpu.CompilerParams(dimension_semantics=sem, vmem_limit_bytes=VMEM_LIMIT_BYTES)


_NT = (((1,), (1,)), ((), ()))
_ANY = pl.BlockSpec(memory_space=pl.ANY)


def _mesh_pos():
    return lax.axis_index("x"), lax.axis_index("y"), lax.axis_index("c")


def _sigmoid(z):
    return 0.5 * jnp.tanh(0.5 * z) + 0.5


def _sigmoid_tail(z):
    return 1.0 / (1.0 + jnp.exp(-z))


def _col_block(n):
    return n if n <= 1408 else n // 2


def _matmul(a, b, *, name, nt=False, out_dtype=F32, residual=None, bm=512):
    m, k = a.shape
    n = b.shape[0] if nt else b.shape[1]
    bm = min(bm, m)
    bn = _col_block(n)
    dims = (((1,), (1,)), ((), ())) if nt else (((1,), (0,)), ((), ()))
    has_res = residual is not None

    def body(*refs):
        a_ref, b_ref, o_ref = refs[0], refs[1], refs[-1]
        acc = lax.dot_general(a_ref[...].astype(BF16), b_ref[...].astype(BF16), dims, preferred_element_type=F32)
        if has_res:
            acc = acc + refs[2][...]
        o_ref[...] = acc.astype(out_dtype)

    in_specs = [pl.BlockSpec((bm, k), lambda i, j: (i, 0)),
                pl.BlockSpec((bn, k), lambda i, j: (j, 0)) if nt else pl.BlockSpec((k, bn), lambda i, j: (0, j))]
    args = [a, b]
    if has_res:
        in_specs.append(pl.BlockSpec((bm, bn), lambda i, j: (i, j)))
        args.append(residual)
    return pl.pallas_call(
        body, grid=(m // bm, n // bn), in_specs=in_specs, out_specs=pl.BlockSpec((bm, bn), lambda i, j: (i, j)),
        out_shape=jax.ShapeDtypeStruct((m, n), out_dtype), compiler_params=_params(("parallel", "parallel")),
        name=name)(*args)


def _matmul_tn(a, b, *, name, bt=512):
    t, m = a.shape
    n = b.shape[1]
    bt = min(bt, t)
    bm, bn = _col_block(m), _col_block(n)

    def body(a_ref, b_ref, o_ref):
        @pl.when(pl.program_id(2) == 0)
        def _():
            o_ref[...] = jnp.zeros_like(o_ref)

        o_ref[...] += lax.dot_general(a_ref[...].astype(BF16), b_ref[...].astype(BF16),
                                      (((0,), (0,)), ((), ())), preferred_element_type=F32)

    return pl.pallas_call(
        body, grid=(m // bm, n // bn, t // bt),
        in_specs=[pl.BlockSpec((bt, bm), lambda i, j, s: (s, i)), pl.BlockSpec((bt, bn), lambda i, j, s: (s, j))],
        out_specs=pl.BlockSpec((bm, bn), lambda i, j, s: (i, j)),
        out_shape=jax.ShapeDtypeStruct((m, n), F32),
        compiler_params=_params(("parallel", "parallel", "arbitrary")), name=name)(a, b)


def _swap_halves(v):
    ax = v.ndim - 1
    lane = lax.broadcasted_iota(jnp.int32, v.shape, ax)
    up = pltpu.roll(v, LANES - QK_ROPE // 2, axis=ax)
    down = pltpu.roll(v, QK_ROPE // 2, axis=ax)
    return jnp.where(lane < QK_ROPE // 2, up, jnp.where(lane < QK_ROPE, down, 0.0))


def _rope(v, cos, sin):
    return v * cos + _swap_halves(v) * sin


def _rope_t(d, cos, sin):
    return d * cos + _swap_halves(d * sin)


def _rope_tables(seq):
    pos = jnp.arange(seq, dtype=F32)
    inv = ROPE_THETA ** (-jnp.arange(0, QK_ROPE, 2, dtype=F32) / QK_ROPE)
    ang = pos[:, None] * inv[None, :]
    cos, sin = jnp.cos(ang), jnp.sin(ang)
    zero = jnp.zeros((seq, LANES - QK_ROPE), F32)
    return jnp.concatenate([cos, cos, zero], axis=1), jnp.concatenate([-sin, sin, zero], axis=1)


def _rms(v):
    return v * lax.rsqrt(jnp.mean(v * v, axis=-1, keepdims=True) + EPS)


def _const_spec(a):
    return pl.BlockSpec(a.shape, lambda i: (0,) * a.ndim)


def _lru_proj_fwd(x, g_a, w_in_t, *, name, bt=256):
    t, d = x.shape
    bt = min(bt, t)
    n = w_in_t.shape[0] // 2

    def body(x_ref, g_ref, wt_ref, h_ref, xp_ref, ga_ref):
        h = (_rms(x_ref[...]) * g_ref[...]).astype(BF16)
        h_ref[...] = h
        xp_ref[...] = lax.dot_general(h, wt_ref[0:n, :], _NT, preferred_element_type=F32)
        ga_ref[...] = lax.dot_general(h, wt_ref[n:2 * n, :], _NT, preferred_element_type=F32)

    row = lambda w: pl.BlockSpec((bt, w), lambda i: (i, 0))
    return pl.pallas_call(
        body, grid=(t // bt,), in_specs=[row(d), _const_spec(g_a), _const_spec(w_in_t)],
        out_specs=[row(d), row(n), row(n)],
        out_shape=[jax.ShapeDtypeStruct((t, d), BF16), jax.ShapeDtypeStruct((t, n), F32), jax.ShapeDtypeStruct((t, n), F32)],
        compiler_params=_params(("parallel",)), name=name)(x, g_a, w_in_t)


def _mla_proj_fwd(x1, gains, w, cos, sin, *, seq, name, bt=256):
    t, d = x1.shape
    bt = min(bt, seq)
    per_seq = seq // bt
    g_kv, g_b, g_kvn, g_q = gains
    consts = [g_kv, g_b, g_kvn, g_q, w["w_dkv_c"], w["w_dkv_r"], w["w_in_b_t"], w["w_uk"], w["w_uv"],
              w["w_uk_t"], w["w_uv_t"], w["w_uq"]]

    def body(x_ref, cos_ref, sin_ref, gkv_ref, gb_ref, gkvn_ref, gq_ref, wdc_ref, wdr_ref, wbt_ref,
             wuk_ref, wuv_ref, wukt_ref, wuvt_ref, wuq_ref,
             hk_ref, hq_ref, ck_ref, cqp_ref, g2_ref, ckv_ref, cq_ref, q_ref, kn_ref, v_ref, kr_ref, knt_ref, vt_ref, krt_ref):
        nrm = _rms(x_ref[...])
        hk = (nrm * gkv_ref[...]).astype(BF16)
        hq = (nrm * gb_ref[...]).astype(BF16)
        hk_ref[...] = hk
        hq_ref[...] = hq
        ck = jnp.dot(hk, wdc_ref[...], preferred_element_type=F32)
        ck_ref[...] = ck
        cqp = lax.dot_general(hq, wbt_ref[0:Q_RANK, :], _NT, preferred_element_type=F32)
        cqp_ref[...] = cqp
        g2_ref[...] = lax.dot_general(hq, wbt_ref[Q_RANK:, :], _NT, preferred_element_type=F32)
        cosv, sinv = cos_ref[...], sin_ref[...]
        kr = _rope(jnp.dot(hk, wdr_ref[...], preferred_element_type=F32), cosv, sinv)
        kr_ref[...] = kr.astype(BF16)
        krt_ref[...] = kr.T.astype(BF16)
        ckv = (_rms(ck) * gkvn_ref[...]).astype(BF16)
        ckv_ref[...] = ckv
        kn_ref[...] = jnp.dot(ckv, wuk_ref[...], preferred_element_type=F32).astype(BF16)
        v_ref[...] = jnp.dot(ckv, wuv_ref[...], preferred_element_type=F32).astype(BF16)
        knt_ref[...] = lax.dot_general(wukt_ref[...], ckv, _NT, preferred_element_type=F32).astype(BF16)
        vt_ref[...] = lax.dot_general(wuvt_ref[...], ckv, _NT, preferred_element_type=F32).astype(BF16)
        cq = (_rms(cqp) * gq_ref[...]).astype(BF16)
        cq_ref[...] = cq
        for h in range(N_HEADS):
            qh = jnp.dot(cq, wuq_ref[:, h * HEAD_PAD:(h + 1) * HEAD_PAD], preferred_element_type=F32)
            q_ref[:, h * HEAD_PAD:h * HEAD_PAD + QK_NOPE] = qh[:, :QK_NOPE].astype(BF16)
            q_ref[:, h * HEAD_PAD + QK_NOPE:(h + 1) * HEAD_PAD] = _rope(qh[:, QK_NOPE:], cosv, sinv).astype(BF16)

    row = lambda w_: pl.BlockSpec((bt, w_), lambda i: (i, 0))
    col = lambda h_: pl.BlockSpec((h_, bt), lambda i: (0, i))
    tab = pl.BlockSpec((bt, LANES), lambda i: (i % per_seq, 0))
    nh = N_HEADS * V_DIM
    shapes = [((t, d), BF16), ((t, d), BF16), ((t, KV_RANK), F32), ((t, Q_RANK), F32), ((t, nh), F32), ((t, KV_RANK), BF16),
              ((t, Q_RANK), BF16), ((t, N_HEADS * HEAD_PAD), BF16), ((t, nh), BF16), ((t, nh), BF16), ((t, LANES), BF16),
              ((nh, t), BF16), ((nh, t), BF16), ((LANES, t), BF16)]
    out_specs = [row(d), row(d), row(KV_RANK), row(Q_RANK), row(nh), row(KV_RANK), row(Q_RANK), row(N_HEADS * HEAD_PAD),
                 row(nh), row(nh), row(LANES), col(nh), col(nh), col(LANES)]
    return pl.pallas_call(
        body, grid=(t // bt,), in_specs=[row(d), tab, tab] + [_const_spec(a) for a in consts], out_specs=out_specs,
        out_shape=[jax.ShapeDtypeStruct(s, dt) for s, dt in shapes],
        compiler_params=_params(("parallel",)), name=name)(x1, cos, sin, *consts)


def _rms_bwd_rows(xv, dn):
    r = lax.rsqrt(jnp.mean(xv * xv, axis=-1, keepdims=True) + EPS)
    nrm = xv * r
    return r * (dn - nrm * jnp.mean(dn * nrm, axis=-1, keepdims=True)), nrm


def _col_sum(v):
    return jnp.sum(v, axis=0, keepdims=True)


def _lru_proj_bwd(dxp, dga, x, dx1, g_a, w_in_t, *, name, bt=256):
    t, d = x.shape
    bt = min(bt, t)
    n = w_in_t.shape[0] // 2

    def body(dxp_ref, dga_ref, x_ref, dx1_ref, g_ref, wt_ref, dx_ref, dg_ref):
        @pl.when(pl.program_id(0) == 0)
        def _():
            dg_ref[...] = jnp.zeros_like(dg_ref)

        dh = (jnp.dot(dxp_ref[...], wt_ref[0:n, :], preferred_element_type=F32)
              + jnp.dot(dga_ref[...], wt_ref[n:2 * n, :], preferred_element_type=F32))
        dxn, nrm = _rms_bwd_rows(x_ref[...], dh * g_ref[...])
        dg_ref[...] += _col_sum(dh * nrm)
        dx_ref[...] = dx1_ref[...] + dxn

    row = lambda w: pl.BlockSpec((bt, w), lambda i: (i, 0))
    return pl.pallas_call(
        body, grid=(t // bt,),
        in_specs=[row(n), row(n), row(d), row(d), _const_spec(g_a), _const_spec(w_in_t)],
        out_specs=[row(d), _const_spec(g_a)],
        out_shape=[jax.ShapeDtypeStruct((t, d), F32), jax.ShapeDtypeStruct((1, d), F32)],
        compiler_params=_params(("arbitrary",)), name=name)(dxp, dga, x, dx1, g_a, w_in_t)


def _mla_proj_bwd(x1, dx2, cqp, ck, dq, dkn, dv, dkr, dg2, gains, w, *, name, bt=256):
    t, d = x1.shape
    bt = min(bt, t)
    g_kv, g_b, g_kvn, g_q = gains
    consts = [g_kv, g_b, g_kvn, g_q, w["w_dkv_c"], w["w_dkv_r"], w["w_in_b_t"], w["w_uk"], w["w_uv"], w["w_uq"]]
    nh = N_HEADS * V_DIM

    def body(x1_ref, dx2_ref, cqp_ref, ck_ref, dq_ref, dkn_ref, dv_ref, dkr_ref, dg2_ref,
             gkv_ref, gb_ref, gkvn_ref, gq_ref, wdc_ref, wdr_ref, wbt_ref, wuk_ref, wuv_ref, wuq_ref,
             dx1_ref, du2_ref, dckr_ref, dgkv_ref, dgb_ref, dgkvn_ref, dgq_ref):
        @pl.when(pl.program_id(0) == 0)
        def _():
            for ref in (dgkv_ref, dgb_ref, dgkvn_ref, dgq_ref):
                ref[...] = jnp.zeros_like(ref)

        dot_nt = lambda a, b: lax.dot_general(a, b, _NT, preferred_element_type=F32)
        dcq = dot_nt(dq_ref[...], wuq_ref[...])
        dcqp, nq = _rms_bwd_rows(cqp_ref[...], dcq * gq_ref[...])
        dgq_ref[...] += _col_sum(dcq * nq)
        dcqp = dcqp.astype(BF16)
        dg2 = dg2_ref[...]
        du2_ref[:, :Q_RANK] = dcqp
        du2_ref[:, Q_RANK:] = dg2
        dhq = (jnp.dot(dcqp, wbt_ref[0:Q_RANK, :], preferred_element_type=F32)
               + jnp.dot(dg2, wbt_ref[Q_RANK:, :], preferred_element_type=F32))
        dckv = dot_nt(dkn_ref[...], wuk_ref[...]) + dot_nt(dv_ref[...], wuv_ref[...])
        dck, nc = _rms_bwd_rows(ck_ref[...], dckv * gkvn_ref[...])
        dgkvn_ref[...] += _col_sum(dckv * nc)
        dck = dck.astype(BF16)
        dkr = dkr_ref[...].astype(BF16)
        dckr_ref[:, :KV_RANK] = dck
        dckr_ref[:, KV_RANK:] = dkr
        dhk = dot_nt(dck, wdc_ref[...]) + dot_nt(dkr, wdr_ref[...])
        dxn, n1 = _rms_bwd_rows(x1_ref[...], dhq * gb_ref[...] + dhk * gkv_ref[...])
        dgb_ref[...] += _col_sum(dhq * n1)
        dgkv_ref[...] += _col_sum(dhk * n1)
        dx1_ref[...] = dx2_ref[...] + dxn

    row = lambda w_: pl.BlockSpec((bt, w_), lambda i: (i, 0))
    vec = lambda w_: pl.BlockSpec((1, w_), lambda i: (0, 0))
    in_specs = [row(d), row(d), row(Q_RANK), row(KV_RANK), row(N_HEADS * HEAD_PAD), row(nh), row(nh), row(LANES), row(nh)]
    return pl.pallas_call(
        body, grid=(t // bt,), in_specs=in_specs + [_const_spec(a) for a in consts],
        out_specs=[row(d), row(Q_RANK + nh), row(KV_RANK + LANES), vec(d), vec(d), vec(KV_RANK), vec(Q_RANK)],
        out_shape=[jax.ShapeDtypeStruct((t, d), F32), jax.ShapeDtypeStruct((t, Q_RANK + nh), BF16),
                   jax.ShapeDtypeStruct((t, KV_RANK + LANES), BF16), jax.ShapeDtypeStruct((1, d), F32),
                   jax.ShapeDtypeStruct((1, d), F32), jax.ShapeDtypeStruct((1, KV_RANK), F32),
                   jax.ShapeDtypeStruct((1, Q_RANK), F32)],
        compiler_params=_params(("arbitrary",)), name=name)(x1, dx2, cqp, ck, dq, dkn, dv, dkr, dg2, *consts)


def _softplus(z):
    return jnp.maximum(z, 0.0) + jnp.log1p(jnp.exp(-jnp.abs(z)))


def _neg_expm1(z):
    series = -z * (1.0 + z * (1.0 / 2) * (1.0 + z * (1.0 / 3) * (1.0 + z * (1.0 / 4))))
    return jnp.where(z > -0.01, series, 1.0 - jnp.exp(z))


def _gates(xb, wrg, wig, brg, big, sp):
    xbb = xb.astype(BF16)
    r = _sigmoid_tail(jnp.dot(xbb, wrg, preferred_element_type=F32) + brg)
    i = _sigmoid(jnp.dot(xbb, wig, preferred_element_type=F32) + big)
    la = (-LRU_C) * r * sp
    a = jnp.exp(la)
    mult = jnp.sqrt(_neg_expm1(2.0 * la))
    return r, i, a, mult


def _conv(xpad_ref, cw_ref, seq):
    acc = cw_ref[0:1, :] * xpad_ref[pl.ds(8 - (CONV_WIDTH - 1), seq), :]
    for k in range(1, CONV_WIDTH):
        acc = acc + cw_ref[k:k + 1, :] * xpad_ref[pl.ds(8 - (CONV_WIDTH - 1) + k, seq), :]
    return acc


def _seq_spec(seq):
    return pl.BlockSpec((None, seq, RNN_BW), lambda n, b: (b, 0, n))


def _chan_spec(rows):
    return pl.BlockSpec((rows, RNN_BW), lambda n, b: (0, n))


_GATE_W_SPEC = pl.BlockSpec((None, RNN_BW, RNN_BW), lambda n, b: (n, 0, 0))


SCAN_UNROLL = 4


def _peers():
    x, y, c = _mesh_pos()
    others = []
    for k in range(1, N_DEV):
        px = 1 - x if k & 4 else x
        py = 1 - y if k & 2 else y
        pc = 1 - c if k & 1 else c
        others.append(((px, py, pc), 4 * px + 2 * py + pc))
    return 4 * x + 2 * y + c, others


def _exchange(src_ref, dst_ref, send_sems, recv_sems, local_sem, *, gather, finish):
    me, others = _peers()

    def send(k, dev, slot):
        return pltpu.make_async_remote_copy(
            src_ref=src_ref if gather else src_ref.at[slot], dst_ref=dst_ref.at[me], send_sem=send_sems.at[k],
            recv_sem=recv_sems.at[k], device_id=dev, device_id_type=pl.DeviceIdType.MESH)

    local = pltpu.make_async_copy(src_ref if gather else src_ref.at[me], dst_ref.at[me], local_sem)
    if not finish:
        local.start()
        for k, (dev, slot) in enumerate(others):
            send(k, dev, slot).start()
        return
    for k, (dev, slot) in enumerate(others):
        pltpu.make_async_remote_copy(
            src_ref=dst_ref.at[slot], dst_ref=dst_ref.at[slot], send_sem=send_sems.at[k], recv_sem=recv_sems.at[k],
            device_id=dev, device_id_type=pl.DeviceIdType.MESH).wait_recv()
    for k, (dev, slot) in enumerate(others):
        send(k, dev, slot).wait_send()
    local.wait()


_EXCHANGE_SEMS = [pltpu.SemaphoreType.DMA((N_DEV - 1,)), pltpu.SemaphoreType.DMA((N_DEV - 1,)), pltpu.SemaphoreType.DMA(())]


def _grid_ends(bsz):
    n, b = pl.program_id(0), pl.program_id(1)
    return (n == 0) & (b == 0), (n == RNN_BLOCKS - 1) & (b == bsz - 1)


def _lru_fwd(xp, ga, cw, vecs, wrg, wig, block, *, name):
    bsz, seq, _ = xp.shape
    groups = seq // 8

    def body(xp_ref, ga_ref, cw_ref, vec_ref, wrg_ref, wig_ref, blk_ref, xb_ref, hs_ref, y_ref, all_ref,
             xpad, a_s, b_s, send_sems, recv_sems, local_sem):
        first, last = _grid_ends(bsz)

        @pl.when(first)
        def _():
            _exchange(blk_ref, all_ref, send_sems, recv_sems, local_sem, gather=True, finish=False)

        xpad[0:8, :] = jnp.zeros((8, RNN_BW), F32)
        xpad[pl.ds(8, seq), :] = xp_ref[...]
        xb = _conv(xpad, cw_ref, seq) + vec_ref[0:1, :]
        xb_ref[...] = xb
        sp = _softplus(-vec_ref[3:4, :])
        _, i, a, mult = _gates(xb, wrg_ref[...], wig_ref[...], vec_ref[1:2, :], vec_ref[2:3, :], sp)
        a_s[...] = a
        b_s[...] = mult * (i * xb)
        row = lax.broadcasted_iota(jnp.int32, (8, RNN_BW), 0)

        def group(g, h):
            r0 = pl.multiple_of(g * 8, 8)
            av = a_s[pl.ds(r0, 8), :]
            bv = b_s[pl.ds(r0, 8), :]
            for k in (1, 2, 4):
                m = row >= k
                bv = jnp.where(m, av * pltpu.roll(bv, k, axis=0) + bv, bv)
                av = jnp.where(m, av * pltpu.roll(av, k, axis=0), av)
            hs_ref[pl.ds(r0, 8), :] = av * h + bv
            return av[7:8, :] * h + bv[7:8, :]

        def groups_of(i, h):
            for u in range(SCAN_UNROLL):
                h = group(i * SCAN_UNROLL + u, h)
            return h

        lax.fori_loop(0, groups // SCAN_UNROLL, groups_of, jnp.zeros((1, RNN_BW), F32))
        gav = ga_ref[...]
        y_ref[...] = (hs_ref[...] * (gav * _sigmoid(gav))).astype(BF16)

        @pl.when(last)
        def _():
            _exchange(blk_ref, all_ref, send_sems, recv_sems, local_sem, gather=True, finish=True)

    sq = _seq_spec(seq)
    shape = (bsz, seq, D_RNN)
    return pl.pallas_call(
        body, grid=(RNN_BLOCKS, bsz),
        in_specs=[sq, sq, _chan_spec(8), _chan_spec(8), _GATE_W_SPEC, _GATE_W_SPEC, _ANY],
        out_specs=[sq, sq, sq, _ANY],
        out_shape=[jax.ShapeDtypeStruct(shape, F32), jax.ShapeDtypeStruct(shape, F32), jax.ShapeDtypeStruct(shape, BF16),
                   jax.ShapeDtypeStruct((N_DEV,) + block.shape, block.dtype)],
        scratch_shapes=[pltpu.VMEM((seq + 8, RNN_BW), F32), pltpu.VMEM((seq, RNN_BW), F32), pltpu.VMEM((seq, RNN_BW), F32)]
        + _EXCHANGE_SEMS,
        compiler_params=_params(("arbitrary", "arbitrary")), name=name)(xp, ga, cw, vecs, wrg, wig, block)


def _lru_bwd(dy, xp, xb, hs, ga, cw, vecs, wrg, wig, parts, *, name):
    bsz, seq, _ = xp.shape
    groups = seq // 8

    def body(dy_ref, xp_ref, xb_ref, hs_ref, ga_ref, cw_ref, vec_ref, wrg_ref, wig_ref,
             parts_ref, dxp_ref, dga_ref, dwrg_ref, dwig_ref, dvec_ref, land_ref, pad, a_s, d_s, lam_s,
             send_sems, recv_sems, local_sem):
        first, last = _grid_ends(bsz)

        @pl.when(first)
        def _():
            _exchange(parts_ref, land_ref, send_sems, recv_sems, local_sem, gather=False, finish=False)

        @pl.when(pl.program_id(1) == 0)
        def _():
            dwrg_ref[...] = jnp.zeros_like(dwrg_ref)
            dwig_ref[...] = jnp.zeros_like(dwig_ref)
            dvec_ref[...] = jnp.zeros_like(dvec_ref)

        xb = xb_ref[...]
        hs = hs_ref[...]
        gav = ga_ref[...]
        dy = dy_ref[...]
        sp = _softplus(-vec_ref[3:4, :])
        wrg = wrg_ref[...]
        wig = wig_ref[...]
        r, i, a, mult = _gates(xb, wrg, wig, vec_ref[1:2, :], vec_ref[2:3, :], sp)
        sg = _sigmoid(gav)
        dga_ref[...] = (dy * hs * (sg * (1.0 + gav * (1.0 - sg)))).astype(BF16)
        d_s[...] = dy * (gav * sg)

        pad[pl.ds(0, seq), :] = a
        pad[pl.ds(seq, 8), :] = jnp.zeros((8, RNN_BW), F32)
        a_s[...] = pad[pl.ds(1, seq), :]
        row = lax.broadcasted_iota(jnp.int32, (8, RNN_BW), 0)

        def group(g, nxt):
            r0 = pl.multiple_of((groups - 1 - g) * 8, 8)
            cv = a_s[pl.ds(r0, 8), :]
            bv = d_s[pl.ds(r0, 8), :]
            for k in (1, 2, 4):
                m = row < 8 - k
                bv = jnp.where(m, cv * pltpu.roll(bv, 8 - k, axis=0) + bv, bv)
                cv = jnp.where(m, cv * pltpu.roll(cv, 8 - k, axis=0), cv)
            lam_s[pl.ds(r0, 8), :] = cv * nxt + bv
            return cv[0:1, :] * nxt + bv[0:1, :]

        def groups_of(i, nxt):
            for u in range(SCAN_UNROLL):
                nxt = group(i * SCAN_UNROLL + u, nxt)
            return nxt

        lax.fori_loop(0, groups // SCAN_UNROLL, groups_of, jnp.zeros((1, RNN_BW), F32))
        dh = lam_s[...]

        pad[0:8, :] = jnp.zeros((8, RNN_BW), F32)
        pad[pl.ds(8, seq), :] = hs
        da = dh * pad[pl.ds(7, seq), :]
        ixb = i * xb
        dixb = dh * mult
        dla = da * a - (dh * ixb) * (a * a) / mult
        drp = (dla * ((-LRU_C) * sp)) * r * (1.0 - r)
        dip = (dixb * xb) * i * (1.0 - i)
        dvec_ref[0:1, :] += jnp.sum(drp, axis=0, keepdims=True)
        dvec_ref[1:2, :] += jnp.sum(dip, axis=0, keepdims=True)
        dvec_ref[2:3, :] += jnp.sum(dla * ((-LRU_C) * r), axis=0, keepdims=True)
        drpb = drp.astype(BF16)
        dipb = dip.astype(BF16)
        xbb = xb.astype(BF16)
        nt = (((1,), (1,)), ((), ()))
        tn = (((0,), (0,)), ((), ()))
        dxb = (dixb * i
               + lax.dot_general(drpb, wrg, nt, preferred_element_type=F32)
               + lax.dot_general(dipb, wig, nt, preferred_element_type=F32))
        dwrg_ref[...] += lax.dot_general(xbb, drpb, tn, preferred_element_type=F32)
        dwig_ref[...] += lax.dot_general(xbb, dipb, tn, preferred_element_type=F32)
        dvec_ref[3:4, :] += jnp.sum(dxb, axis=0, keepdims=True)

        pad[pl.ds(0, seq), :] = dxb
        pad[pl.ds(seq, 8), :] = jnp.zeros((8, RNN_BW), F32)
        dxp = cw_ref[0:1, :] * pad[pl.ds(CONV_WIDTH - 1, seq), :]
        for k in range(1, CONV_WIDTH):
            dxp = dxp + cw_ref[k:k + 1, :] * pad[pl.ds(CONV_WIDTH - 1 - k, seq), :]
        dxp_ref[...] = dxp.astype(BF16)
        pad[0:8, :] = jnp.zeros((8, RNN_BW), F32)
        pad[pl.ds(8, seq), :] = xp_ref[...]
        for k in range(CONV_WIDTH):
            dvec_ref[4 + k:5 + k, :] += jnp.sum(dxb * pad[pl.ds(8 - (CONV_WIDTH - 1) + k, seq), :], axis=0, keepdims=True)

        @pl.when(last)
        def _():
            _exchange(parts_ref, land_ref, send_sems, recv_sems, local_sem, gather=False, finish=True)

    sq = _seq_spec(seq)
    shape = (bsz, seq, D_RNN)
    gshape = (RNN_BLOCKS, RNN_BW, RNN_BW)
    return pl.pallas_call(
        body, grid=(RNN_BLOCKS, bsz),
        in_specs=[sq, sq, sq, sq, sq, _chan_spec(8), _chan_spec(8), _GATE_W_SPEC, _GATE_W_SPEC, _ANY],
        out_specs=[sq, sq, _GATE_W_SPEC, _GATE_W_SPEC, _chan_spec(8), _ANY],
        out_shape=[jax.ShapeDtypeStruct(shape, BF16), jax.ShapeDtypeStruct(shape, BF16),
                   jax.ShapeDtypeStruct(gshape, F32), jax.ShapeDtypeStruct(gshape, F32),
                   jax.ShapeDtypeStruct((8, D_RNN), F32), jax.ShapeDtypeStruct(parts.shape, parts.dtype)],
        scratch_shapes=[pltpu.VMEM((seq + 8, RNN_BW), F32), pltpu.VMEM((seq, RNN_BW), F32),
                        pltpu.VMEM((seq, RNN_BW), F32), pltpu.VMEM((seq, RNN_BW), F32)] + _EXCHANGE_SEMS,
        compiler_params=_params(("arbitrary", "arbitrary")), name=name)(dy, xp, xb, hs, ga, cw, vecs, wrg, wig, parts)


def _attn_block(seq):
    return min(512, seq)


def _diag_mask(blk):
    return lax.broadcasted_iota(jnp.int32, (blk, blk), 0) <= lax.broadcasted_iota(jnp.int32, (blk, blk), 1)


FWD_HEADS = 4
BWD_HEADS = 2


def _attn_fwd(q, kn, kr, v_t, *, bsz, seq, name):
    t = bsz * seq
    blk = _attn_block(seq)
    nq = seq // blk
    hg = FWD_HEADS

    def body(q_ref, kn_ref, kr_ref, vt_ref, o_ref, lse_ref, acc):
        qi = pl.program_id(2)
        acc[...] = jnp.zeros_like(acc)

        def step(j, carry, diagonal):
            k0 = pl.multiple_of(j * blk, blk)
            kr_j = kr_ref[pl.ds(k0, blk), :]
            out = []
            for h in range(hg):
                m_i, l_i = carry[h]
                kv = jnp.concatenate([kn_ref[pl.ds(k0, blk), h * QK_NOPE:(h + 1) * QK_NOPE], kr_j], axis=1)
                qv = q_ref[:, h * HEAD_PAD:(h + 1) * HEAD_PAD]
                s = lax.dot_general(kv, qv, _NT, preferred_element_type=F32) * ATTN_SCALE
                if diagonal:
                    s = jnp.where(_diag_mask(blk), s, -jnp.inf)
                m_new = jnp.maximum(m_i, jnp.max(s, axis=0, keepdims=True))
                p = jnp.exp(s - m_new)
                alpha = jnp.exp(m_i - m_new)
                l_new = alpha * l_i + jnp.sum(p, axis=0, keepdims=True)
                acc[h] = alpha * acc[h] + jnp.dot(vt_ref[h * V_DIM:(h + 1) * V_DIM, pl.ds(k0, blk)], p.astype(BF16),
                                                  preferred_element_type=F32)
                out.append((m_new, l_new))
            return tuple(out)

        init = tuple((jnp.full((1, blk), -jnp.inf, F32), jnp.zeros((1, blk), F32)) for _ in range(hg))
        carry = lax.fori_loop(0, qi, lambda j, c: step(j, c, False), init)
        stats = step(qi, carry, True)
        for h in range(hg):
            m_i, l_i = stats[h]
            o_ref[:, h * V_DIM:(h + 1) * V_DIM] = (acc[h] / l_i).T
            lse_ref[h] = m_i + jnp.log(l_i)

    return pl.pallas_call(
        body, grid=(bsz, N_HEADS // hg, nq),
        in_specs=[pl.BlockSpec((blk, hg * HEAD_PAD), lambda b, g, i: (b * nq + i, g)),
                  pl.BlockSpec((seq, hg * QK_NOPE), lambda b, g, i: (b, g)),
                  pl.BlockSpec((seq, LANES), lambda b, g, i: (b, 0)),
                  pl.BlockSpec((hg * V_DIM, seq), lambda b, g, i: (g, b))],
        out_specs=[pl.BlockSpec((blk, hg * V_DIM), lambda b, g, i: (b * nq + i, g)),
                   pl.BlockSpec((hg, 1, blk), lambda b, g, i: (g, 0, b * nq + i))],
        out_shape=[jax.ShapeDtypeStruct((t, N_HEADS * V_DIM), F32), jax.ShapeDtypeStruct((N_HEADS, 1, t), F32)],
        scratch_shapes=[pltpu.VMEM((hg, V_DIM, blk), F32)],
        compiler_params=_params(("parallel", "parallel", "parallel")), name=name)(q, kn, kr, v_t)


def _attn_bwd(q, kn, kr, kn_t, kr_t, v, o, lse, do, cos, sin, *, bsz, seq, name):
    t = bsz * seq
    blk = _attn_block(seq)
    nq = seq // blk
    hg = BWD_HEADS

    def body(q_ref, kn_ref, kr_ref, knt_ref, krt_ref, v_ref, o_ref, lse_ref, do_ref, cos_ref, sin_ref,
             dq_ref, dkn_ref, dkr_ref, dv_ref, dqt_acc, dk_acc, dv_acc):
        dqt_acc[...] = jnp.zeros_like(dqt_acc)
        dk_acc[...] = jnp.zeros_like(dk_acc)
        dv_acc[...] = jnp.zeros_like(dv_acc)

        def q_block(i, _):
            q0 = pl.multiple_of(i * blk, blk)
            rows = []
            for h in range(hg):
                dov = do_ref[pl.ds(q0, blk), h * V_DIM:(h + 1) * V_DIM].astype(F32)
                dcol = jnp.sum(dov * o_ref[pl.ds(q0, blk), h * V_DIM:(h + 1) * V_DIM], axis=-1, keepdims=True)
                delta = jnp.broadcast_to(dcol, (blk, LANES)).T[0:1, :]
                rows.append((lse_ref[h, :, pl.ds(q0, blk)], delta))

            def pair(j, diagonal):
                k0 = pl.multiple_of(j * blk, blk)
                kr_j = kr_ref[pl.ds(k0, blk), :]
                krt_j = krt_ref[:, pl.ds(k0, blk)]
                for h in range(hg):
                    lse_i, delta = rows[h]
                    qv = q_ref[pl.ds(q0, blk), h * HEAD_PAD:(h + 1) * HEAD_PAD]
                    dov = do_ref[pl.ds(q0, blk), h * V_DIM:(h + 1) * V_DIM]
                    kv = jnp.concatenate([kn_ref[pl.ds(k0, blk), h * QK_NOPE:(h + 1) * QK_NOPE], kr_j], axis=1)
                    s = lax.dot_general(kv, qv, _NT, preferred_element_type=F32) * ATTN_SCALE
                    p = jnp.exp(s - lse_i)
                    if diagonal:
                        p = jnp.where(_diag_mask(blk), p, 0.0)
                    dv_acc[pl.ds(k0, blk), h * V_DIM:(h + 1) * V_DIM] += jnp.dot(
                        p.astype(BF16), dov, preferred_element_type=F32)
                    dp = lax.dot_general(v_ref[pl.ds(k0, blk), h * V_DIM:(h + 1) * V_DIM], dov, _NT,
                                         preferred_element_type=F32)
                    ds = (p * (dp - delta) * ATTN_SCALE).astype(BF16)
                    dk_acc[pl.ds(k0, blk), h * HEAD_PAD:(h + 1) * HEAD_PAD] += jnp.dot(ds, qv, preferred_element_type=F32)
                    base = h * HEAD_PAD
                    dqt_acc[base:base + QK_NOPE, pl.ds(q0, blk)] += jnp.dot(
                        knt_ref[h * QK_NOPE:(h + 1) * QK_NOPE, pl.ds(k0, blk)], ds, preferred_element_type=F32)
                    dqt_acc[base + QK_NOPE:base + HEAD_PAD, pl.ds(q0, blk)] += jnp.dot(
                        krt_j, ds, preferred_element_type=F32)

            def off_diagonal(j, _):
                pair(j, False)
                return 0

            lax.fori_loop(0, i, off_diagonal, 0)
            pair(i, True)
            return 0

        lax.fori_loop(0, nq, q_block, 0)
        dkr = jnp.zeros((seq, LANES), F32)
        for h in range(hg):
            base = h * HEAD_PAD
            for i in range(nq):
                rows = slice(i * blk, (i + 1) * blk)
                dq = dqt_acc[base:base + HEAD_PAD, rows].T
                dq_ref[rows, base:base + QK_NOPE] = dq[:, :QK_NOPE].astype(BF16)
                dq_ref[rows, base + QK_NOPE:base + HEAD_PAD] = _rope_t(
                    dq[:, QK_NOPE:], cos_ref[rows, :], sin_ref[rows, :]).astype(BF16)
            dkn_ref[:, h * QK_NOPE:(h + 1) * QK_NOPE] = dk_acc[:, base:base + QK_NOPE].astype(BF16)
            dkr = dkr + dk_acc[:, base + QK_NOPE:base + HEAD_PAD]
        dv_ref[...] = dv_acc[...].astype(BF16)

        @pl.when(pl.program_id(1) == 0)
        def _():
            dkr_ref[...] = jnp.zeros_like(dkr_ref)

        dkr_ref[...] += _rope_t(dkr, cos_ref[...], sin_ref[...])

    head = pl.BlockSpec((seq, hg * V_DIM), lambda b, g: (b, g))
    head_t = pl.BlockSpec((hg * V_DIM, seq), lambda b, g: (g, b))
    shared = pl.BlockSpec((seq, LANES), lambda b, g: (b, 0))
    shared_t = pl.BlockSpec((LANES, seq), lambda b, g: (0, b))
    table = pl.BlockSpec((seq, LANES), lambda b, g: (0, 0))
    qspec = pl.BlockSpec((seq, hg * HEAD_PAD), lambda b, g: (b, g))
    return pl.pallas_call(
        body, grid=(bsz, N_HEADS // hg),
        in_specs=[qspec, head, shared, head_t, shared_t, head, head,
                  pl.BlockSpec((hg, 1, seq), lambda b, g: (g, 0, b)), head, table, table],
        out_specs=[qspec, head, shared, head],
        out_shape=[jax.ShapeDtypeStruct((t, N_HEADS * HEAD_PAD), BF16), jax.ShapeDtypeStruct((t, N_HEADS * QK_NOPE), BF16),
                   jax.ShapeDtypeStruct((t, LANES), F32), jax.ShapeDtypeStruct((t, N_HEADS * V_DIM), BF16)],
        scratch_shapes=[pltpu.VMEM((hg * HEAD_PAD, seq), F32), pltpu.VMEM((seq, hg * HEAD_PAD), F32),
                        pltpu.VMEM((seq, hg * V_DIM), F32)],
        compiler_params=_params(("parallel", "arbitrary")), name=name)(q, kn, kr, kn_t, kr_t, v, o, lse, do, cos, sin)


def _head_and_loss(o, g2, x1, target, w_out, g_final, *, name, bt=256):
    t, d = x1.shape
    bt = min(bt, t)
    nt = (((1,), (1,)), ((), ()))

    def body(o_ref, g2_ref, x1_ref, tgt_ref, w_ref, gf_ref, loss_ref, dx2_ref, y2_ref, do_ref, dg2_ref, dgf_ref):
        @pl.when(pl.program_id(0) == 0)
        def _():
            loss_ref[...] = jnp.zeros_like(loss_ref)
            dgf_ref[...] = jnp.zeros_like(dgf_ref)

        ov = o_ref[...]
        gv = g2_ref[...]
        sg = _sigmoid(gv)
        silu = gv * sg
        y2 = (ov * silu).astype(BF16)
        y2_ref[...] = y2
        w = w_ref[...]
        x2 = x1_ref[...] + jnp.dot(y2, w, preferred_element_type=F32)
        r = lax.rsqrt(jnp.mean(x2 * x2, axis=-1, keepdims=True) + EPS)
        nrm = x2 * r
        gf = gf_ref[...]
        err = nrm * gf - tgt_ref[...]
        loss_ref[...] += 0.5 * jnp.sum(jnp.mean(err * err, axis=-1, keepdims=True))
        dyf = err * (1.0 / d)
        dgf_ref[...] += jnp.sum(dyf * nrm, axis=0, keepdims=True)
        dn = dyf * gf
        dx2 = r * (dn - nrm * jnp.mean(dn * nrm, axis=-1, keepdims=True))
        dx2_ref[...] = dx2
        dy2 = lax.dot_general(dx2.astype(BF16), w, nt, preferred_element_type=F32)
        do_ref[...] = (dy2 * silu).astype(BF16)
        dg2_ref[...] = (dy2 * ov * (sg * (1.0 + gv * (1.0 - sg)))).astype(BF16)

    row = pl.BlockSpec((bt, d), lambda i: (i, 0))
    vec = pl.BlockSpec((1, d), lambda i: (0, 0))
    return pl.pallas_call(
        body, grid=(t // bt,),
        in_specs=[row, row, row, row, pl.BlockSpec((d, d), lambda i: (0, 0)), vec],
        out_specs=[pl.BlockSpec((8, LANES), lambda i: (0, 0)), row, row, row, row, vec],
        out_shape=[jax.ShapeDtypeStruct((8, LANES), F32), jax.ShapeDtypeStruct((t, d), F32),
                   jax.ShapeDtypeStruct((t, d), BF16), jax.ShapeDtypeStruct((t, d), BF16),
                   jax.ShapeDtypeStruct((t, d), BF16), jax.ShapeDtypeStruct((1, d), F32)],
        compiler_params=_params(("arbitrary",)), name=name)(o, g2, x1, target, w_out, g_final)


def _sum_parts(parts, *, name, br=GRAD_BLOCK):
    npart, rows, w = parts.shape

    def body(p_ref, o_ref):
        acc = p_ref[0].astype(F32)
        for j in range(1, npart):
            acc = acc + p_ref[j].astype(F32)
        o_ref[...] = acc

    return pl.pallas_call(
        body, grid=(rows // br,), in_specs=[pl.BlockSpec((npart, br, w), lambda i: (0, i, 0))],
        out_specs=pl.BlockSpec((br, w), lambda i: (i, 0)), out_shape=jax.ShapeDtypeStruct((rows, w), F32),
        compiler_params=_params(("parallel",)), name=name)(parts)


def _chip_partial(parts, recv, *, name, br=GRAD_BLOCK):
    _, rows, w = parts.shape
    core = lax.axis_index("c").astype(jnp.int32).reshape(1)

    def body(c_ref, p_ref, r_ref, o_ref):
        o_ref[...] = (p_ref[...] + r_ref[...]).astype(BF16)

    grid_spec = pltpu.PrefetchScalarGridSpec(
        num_scalar_prefetch=1, grid=(4, rows // br),
        in_specs=[pl.BlockSpec((None, br, w), lambda k, i, c_ref: (2 * k + c_ref[0], i, 0)),
                  pl.BlockSpec((None, br, w), lambda k, i, c_ref: (k, i, 0))],
        out_specs=pl.BlockSpec((None, br, w), lambda k, i, c_ref: (k, i, 0)))
    return pl.pallas_call(
        body, grid_spec=grid_spec, out_shape=jax.ShapeDtypeStruct((4, rows, w), BF16),
        compiler_params=_params(("parallel", "parallel")), name=name)(core, parts, recv)


def _as_block(a):
    if a.ndim == 1:
        return a.reshape(1, -1)
    if a.ndim > 2 and a.shape[0] == 1:
        return a.reshape(a.shape[1:])
    return a


def _adamw(g, w, m, v, *, name):
    shape = w.shape
    g, w, m, v = (_as_block(a) for a in (g, w, m, v))

    def body(g_ref, w_ref, m_ref, v_ref, d_ref, nm_ref, nv_ref):
        gv = g_ref[...]
        nm = ADAM_B1 * m_ref[...] + (1.0 - ADAM_B1) * gv
        nv = ADAM_B2 * v_ref[...] + (1.0 - ADAM_B2) * (gv * gv)
        nm_ref[...] = nm
        nv_ref[...] = nv
        m_hat = nm / (1.0 - ADAM_B1 ** ADAM_STEP)
        v_hat = nv / (1.0 - ADAM_B2 ** ADAM_STEP)
        d_ref[...] = (-ADAM_LR) * (m_hat / (jnp.sqrt(v_hat) + ADAM_EPS) + ADAM_WD * w_ref[...])

    whole = pl.BlockSpec(memory_space=pltpu.VMEM)
    outs = pl.pallas_call(
        body, in_specs=[whole] * 4, out_specs=[whole] * 3, out_shape=[jax.ShapeDtypeStruct(w.shape, F32)] * 3,
        compiler_params=_params(), name=name)(g, w, m, v)
    return [o.reshape(shape) for o in outs]


def _all_gather(block, *, name):
    m, n = block.shape

    def body(x_ref, out_ref, send_sems, recv_sems, local_sem):
        x, y, c = _mesh_pos()
        me, sibling = (x, y, c), (x, y, 1 - c)
        chips = [(1 - x, y), (x, 1 - y), (1 - x, 1 - y)]

        def slot(px, py, pc):
            return out_ref.at[4 * px + 2 * py + pc]

        def copy(k, blk, to, src=None):
            return pltpu.make_async_remote_copy(
                src_ref=slot(*blk) if src is None else src, dst_ref=slot(*blk),
                send_sem=send_sems.at[k], recv_sem=recv_sems.at[k], device_id=to, device_id_type=pl.DeviceIdType.MESH)

        mine = pltpu.make_async_copy(x_ref, slot(*me), local_sem)
        mine.start()
        first = [copy(0, me, sibling, src=x_ref)]
        first += [copy(1 + j, me, (*chip, c), src=x_ref) for j, chip in enumerate(chips)]
        for cp in first:
            cp.start()
        passed = [copy(4 + j, (*chip, c), sibling) for j, chip in enumerate(chips)]
        for j, chip in enumerate(chips):
            copy(1 + j, (*chip, c), me).wait_recv()
            passed[j].start()
        copy(0, sibling, me).wait_recv()
        for j, chip in enumerate(chips):
            copy(4 + j, (*chip, 1 - c), me).wait_recv()
        for cp in first + passed:
            cp.wait_send()
        mine.wait()

    return pl.pallas_call(
        body, out_shape=jax.ShapeDtypeStruct((N_DEV, m, n), block.dtype), in_specs=[_ANY], out_specs=_ANY,
        scratch_shapes=[pltpu.SemaphoreType.DMA((7,)), pltpu.SemaphoreType.DMA((7,)), pltpu.SemaphoreType.DMA(())],
        name=name)(block)


def _exchange_d2d(parts, *, name):
    _, rows, w = parts.shape

    def body(p_ref, land_ref, send_sems, recv_sems):
        x, y, c = _mesh_pos()
        sends = []
        for k in range(4):
            cp = pltpu.make_async_remote_copy(
                src_ref=p_ref.at[2 * k + (1 - c)], dst_ref=land_ref.at[k], send_sem=send_sems.at[k],
                recv_sem=recv_sems.at[k], device_id=(x, y, 1 - c), device_id_type=pl.DeviceIdType.MESH)
            cp.start()
            sends.append(cp)
        for cp in sends:
            cp.wait_recv()
        for cp in sends:
            cp.wait_send()

    return pl.pallas_call(
        body, out_shape=jax.ShapeDtypeStruct((4, rows, w), parts.dtype), in_specs=[_ANY], out_specs=_ANY,
        scratch_shapes=[pltpu.SemaphoreType.DMA((4,)), pltpu.SemaphoreType.DMA((4,))], name=name)(parts)


def _exchange_ici(parts, *, name):
    def body(p_ref, land_ref, send_sems, recv_sems, local_sem):
        x, y, c = _mesh_pos()
        mine = pltpu.make_async_copy(p_ref.at[2 * x + y], land_ref.at[3], local_sem)
        mine.start()
        sends = []
        for k, (px, py) in enumerate([(1 - x, y), (x, 1 - y), (1 - x, 1 - y)]):
            cp = pltpu.make_async_remote_copy(
                src_ref=p_ref.at[2 * px + py], dst_ref=land_ref.at[k], send_sem=send_sems.at[k],
                recv_sem=recv_sems.at[k], device_id=(px, py, c), device_id_type=pl.DeviceIdType.MESH)
            cp.start()
            sends.append(cp)
        for cp in sends:
            cp.wait_recv()
        for cp in sends:
            cp.wait_send()
        mine.wait()

    return pl.pallas_call(
        body, out_shape=jax.ShapeDtypeStruct(parts.shape, parts.dtype), in_specs=[_ANY], out_specs=_ANY,
        scratch_shapes=[pltpu.SemaphoreType.DMA((3,)), pltpu.SemaphoreType.DMA((3,)), pltpu.SemaphoreType.DMA(())],
        name=name)(parts)


def _rows(a):
    return a.reshape(-1, PACK_W)


def _pad_to(a, n):
    return jnp.pad(a, (0, n - a.shape[0]))


def _weight_blocks(d):
    small = _rows(_pad_to(jnp.concatenate([d[n].reshape(-1) for n, _ in _SMALL]), 8 * PACK_W))
    block_a = jnp.concatenate([d["w_in_a"][0].T.astype(WIRE), lax.bitcast_convert_type(small, WIRE).reshape(16, PACK_W)],
                              axis=0)
    w_uq = jnp.pad(d["w_uq"][0], ((0, 0), (0, 0), (0, HEAD_PAD - QK_NOPE - QK_ROPE)))
    pieces = {"w_out_a": d["w_out_a"], "w_dkv": d["w_dkv"], "w_uk": d["w_uk"], "w_uv": d["w_uv"],
              "w_in_b": d["w_in_b"][0].T, "w_uq": w_uq, "w_out_b": d["w_out_b"]}
    block_b = jnp.concatenate([_rows(pieces[n]) for n, _ in _PIECES_B]
                              + [jnp.zeros((WIRE_ROWS_B - MATRIX_ROWS_B, PACK_W), F32)], axis=0).astype(WIRE)
    return block_a, block_b


def _weights_a(wall):
    w = {}
    lo, hi = _OFF_A["w_in_a"]
    w["w_in_a_t"] = wall[:, lo:hi].reshape(2 * D_RNN, D_MODEL)
    small = lax.bitcast_convert_type(wall[:, MATRIX_ROWS_A:].reshape(N_DEV, 8 * PACK_W, 2), F32)
    off = dict(zip([n for n, _ in _SMALL], [0, 128, 768, 928, 1088, 1248]))
    w["norm_a"] = small[:, :128].reshape(1, D_MODEL)

    def by_channel(lo, rows):
        a = small[:, lo:lo + rows * (D_RNN // N_DEV)].reshape(N_DEV, rows, -1).transpose(1, 0, 2).reshape(rows, D_RNN)
        return jnp.pad(a, ((0, 8 - rows), (0, 0)))

    w["conv_taps"] = by_channel(off["conv_w"], CONV_WIDTH)
    w["lru_vecs"] = by_channel(off["conv_b"], 4)
    return w


def _weights_b(wall):
    piece = {n: wall[:, lo:hi] for n, (lo, hi) in _OFF_B.items()}
    w = {"w_out_a": piece["w_out_a"].reshape(D_RNN, D_MODEL)}
    w_dkv = piece["w_dkv"].reshape(D_MODEL, KV_RANK + QK_ROPE)
    w["w_dkv_c"] = w_dkv[:, :KV_RANK]
    w["w_dkv_r"] = jnp.pad(w_dkv[:, KV_RANK:], ((0, 0), (0, LANES - QK_ROPE)))
    w["w_uk"] = piece["w_uk"].reshape(KV_RANK, N_HEADS * QK_NOPE)
    w["w_uv"] = piece["w_uv"].reshape(KV_RANK, N_HEADS * V_DIM)
    w["w_uk_t"], w["w_uv_t"] = w["w_uk"].T, w["w_uv"].T
    w["w_in_b_t"] = piece["w_in_b"].reshape(Q_RANK + N_HEADS * V_DIM, D_MODEL)
    w["w_uq"] = piece["w_uq"].reshape(Q_RANK, N_HEADS * HEAD_PAD)
    w["w_out_b"] = piece["w_out_b"].reshape(N_HEADS * V_DIM, D_MODEL)
    return w


def _pack_rep(d):
    flat = jnp.concatenate([d[n].reshape(-1) for n, _ in _REP])
    return _rows(_pad_to(flat, REP_ROWS * PACK_W))


def _unpack_rep(p, like):
    flat = p.reshape(-1)
    out, off = {}, 0
    for n, k in _REP:
        out[n] = flat[off:off + k].reshape(like[n].shape)
        off += k
    return out


def _by_owner(a):
    return a.reshape(N_DEV, -1, PACK_W)


def _grad_parts_b(g):
    tail = jnp.zeros((N_DEV, WIRE_ROWS_B - MATRIX_ROWS_B, PACK_W), F32)
    return jnp.concatenate([_by_owner(g[n]) for n, _ in _PIECES_B] + [tail], axis=1).astype(BF16)


def _grad_parts_a(g):
    small = jnp.concatenate([
        g["norm_a"].reshape(N_DEV, -1),
        g["conv_w"].reshape(CONV_WIDTH, N_DEV, -1).transpose(1, 0, 2).reshape(N_DEV, -1),
        g["conv_b"].reshape(N_DEV, -1), g["b_rg"].reshape(N_DEV, -1), g["b_ig"].reshape(N_DEV, -1),
        g["lru_lambda"].reshape(N_DEV, -1)], axis=1)
    small = jnp.pad(small, ((0, 0), (0, 8 * PACK_W - small.shape[1]))).reshape(N_DEV, 8, PACK_W)
    half = N_DEV // 2
    w_in_a = jnp.concatenate([h.reshape(half, -1, PACK_W) for h in g["w_in_a_t"]], axis=0)
    rep = _pack_rep(g).reshape(N_DEV, REP_SLICE, PACK_W)
    tail = jnp.zeros((N_DEV, GRAD_ROWS_A - MATRIX_ROWS_A - 8 - REP_SLICE, PACK_W), F32)
    return jnp.concatenate([w_in_a, small, rep, tail], axis=1)


def _own_grads(sum_a, sum_b):
    out = {}
    lo, hi = _OFF_A["w_in_a"]
    out["w_in_a"] = sum_a[lo:hi].T.reshape(1, D_MODEL, 2 * D_RNN // N_DEV)
    small = sum_a[MATRIX_ROWS_A:MATRIX_ROWS_A + 8].reshape(-1)
    shapes = {"norm_a": (1, D_MODEL // N_DEV), "conv_w": (1, CONV_WIDTH, D_RNN // N_DEV), "conv_b": (1, D_RNN // N_DEV),
              "b_rg": (1, D_RNN // N_DEV), "b_ig": (1, D_RNN // N_DEV), "lru_lambda": (1, D_RNN // N_DEV)}
    off = 0
    for n, k in _SMALL:
        out[n] = small[off:off + k].reshape(shapes[n])
        off += k
    piece = {n: sum_b[lo:hi] for n, (lo, hi) in _OFF_B.items()}
    out["w_out_a"] = piece["w_out_a"].reshape(1, D_RNN // N_DEV, D_MODEL)
    out["w_dkv"] = piece["w_dkv"].reshape(D_MODEL // N_DEV, KV_RANK + QK_ROPE)
    out["w_uk"] = piece["w_uk"].reshape(KV_RANK // N_DEV, N_HEADS, QK_NOPE)
    out["w_uv"] = piece["w_uv"].reshape(KV_RANK // N_DEV, N_HEADS, V_DIM)
    out["w_in_b"] = piece["w_in_b"].T.reshape(1, D_MODEL, (Q_RANK + N_HEADS * V_DIM) // N_DEV)
    out["w_uq"] = piece["w_uq"].reshape(1, Q_RANK // N_DEV, N_HEADS, HEAD_PAD)[..., :QK_NOPE + QK_ROPE]
    out["w_out_b"] = piece["w_out_b"].reshape(1, N_HEADS * V_DIM // N_DEV, D_MODEL)
    return out


def _step(x, target, w, rep, block_b, *, bsz, seq):
    t = bsz * seq
    cos, sin = _rope_tables(seq)
    g_a = w["norm_a"]
    g_kv = rep["norm_kv"].reshape(1, -1)
    g_kvn = rep["kv_norm"].reshape(1, -1)
    g_b = rep["norm_b"].reshape(1, -1)
    g_q = rep["q_norm"].reshape(1, -1)
    g_f = rep["final_norm"].reshape(1, -1)
    wrg = rep["w_rg"][0].astype(BF16)
    wig = rep["w_ig"][0].astype(BF16)
    cw8, vecs = w["conv_taps"], w["lru_vecs"]

    def seq3(a):
        return a.reshape(bsz, seq, a.shape[-1])

    def flat(a):
        return a.reshape(t, a.shape[-1])

    h0, xp, ga = _lru_proj_fwd(x, g_a, w["w_in_a_t"], name="lru_proj_fwd")
    xb, hs, y, wall_b = _lru_fwd(seq3(xp), seq3(ga), cw8, vecs, wrg, wig, block_b, name="lru_fwd")
    w = dict(w, **_weights_b(wall_b))
    x1 = _matmul(flat(y), w["w_out_a"], residual=x, name="out_a")
    hk, hq, ck, cqp, g2, ckv, cq, q, kn, v, kr, kn_t, v_t, kr_t = _mla_proj_fwd(
        x1, (g_kv, g_b, g_kvn, g_q), w, cos, sin, seq=seq, name="mla_proj_fwd")
    o, lse = _attn_fwd(q, kn, kr, v_t, bsz=bsz, seq=seq, name="attn_fwd")
    loss, dx2, y2, do, dg2, dgf = _head_and_loss(o, g2, x1, target, w["w_out_b"], g_f, name="head_loss")
    grads = {"final_norm": dgf, "w_out_b": _matmul_tn(y2, dx2, name="d_w_out_b")}
    dq, dkn, dkr, dv = _attn_bwd(q, kn, kr, kn_t, kr_t, v, o, lse, do, cos, sin, bsz=bsz, seq=seq, name="attn_bwd")
    grads["w_uq"] = _matmul_tn(cq, dq, name="d_w_uq")
    dx1, du2, dckr, dgkv, dgb, dgkvn, dgq = _mla_proj_bwd(
        x1, dx2, cqp, ck, dq, dkn, dv, dkr, dg2, (g_kv, g_b, g_kvn, g_q), w, name="mla_proj_bwd")
    grads["norm_kv"], grads["norm_b"], grads["kv_norm"], grads["q_norm"] = dgkv, dgb, dgkvn, dgq
    grads["w_in_b"] = _matmul_tn(du2, hq, name="d_w_in_b_t")
    grads["w_uk"] = _matmul_tn(ckv, dkn, name="d_w_uk")
    grads["w_uv"] = _matmul_tn(ckv, dv, name="d_w_uv")
    grads["w_dkv"] = _matmul_tn(hk, dckr, name="d_w_dkv")[:, :KV_RANK + QK_ROPE]
    grads["w_out_a"] = _matmul_tn(flat(y), dx1, name="d_w_out_a")
    parts_b = _grad_parts_b(grads)
    dy = _matmul(dx1, w["w_out_a"], nt=True, name="d_y")
    dxp, dga, dwrg, dwig, dvec, landed_b = _lru_bwd(
        seq3(dy), seq3(xp), xb, hs, seq3(ga), cw8, vecs, wrg, wig, parts_b, name="lru_bwd")
    dxp, dga = flat(dxp), flat(dga)
    grads["w_rg"], grads["w_ig"] = dwrg, dwig
    grads["b_rg"], grads["b_ig"], grads["conv_b"] = dvec[0], dvec[1], dvec[3]
    lam = vecs[3]
    grads["lru_lambda"] = dvec[2] * (-1.0 / (1.0 + jnp.exp(lam)))
    grads["conv_w"] = dvec[4:4 + CONV_WIDTH]
    grads["w_in_a_t"] = (_matmul_tn(dxp, h0, name="d_w_in_a_x_t"), _matmul_tn(dga, h0, name="d_w_in_a_g_t"))
    dx, dga_norm = _lru_proj_bwd(dxp, dga, x, dx1, g_a, w["w_in_a_t"], name="lru_proj_bwd")
    grads["norm_a"] = dga_norm
    return loss[0, 0], dx, grads, landed_b


def kernel(x, norm_a, w_in_a, conv_w, conv_b, w_rg, b_rg, w_ig, b_ig, lru_lambda, w_out_a, norm_kv, w_dkv, kv_norm, w_uk, w_uv, norm_b, w_in_b, q_norm, w_uq, w_out_b, final_norm, loss_target, m_norm_a, m_w_in_a, m_conv_w, m_conv_b, m_w_rg, m_b_rg, m_w_ig, m_b_ig, m_lru_lambda, m_w_out_a, m_norm_kv, m_w_dkv, m_kv_norm, m_w_uk, m_w_uv, m_norm_b, m_w_in_b, m_q_norm, m_w_uq, m_w_out_b, m_final_norm, v_norm_a, v_w_in_a, v_conv_w, v_conv_b, v_w_rg, v_b_rg, v_w_ig, v_b_ig, v_lru_lambda, v_w_out_a, v_norm_kv, v_w_dkv, v_kv_norm, v_w_uk, v_w_uv, v_norm_b, v_w_in_b, v_q_norm, v_w_uq, v_w_out_b, v_final_norm):
    given = dict(locals())
    wts = {n: given[n] for n in WEIGHTS}
    mom1 = {n: given["m_" + n] for n in WEIGHTS}
    mom2 = {n: given["v_" + n] for n in WEIGHTS}
    bsz, seq, _ = x.shape
    t = bsz * seq

    block_a, block_b = _weight_blocks(wts)
    w = _weights_a(_all_gather(block_a, name="gather_weights_a"))
    loss, dx, grads, landed_b = _step(x.reshape(t, D_MODEL), loss_target.reshape(t, D_MODEL), w, wts, block_b,
                                      bsz=bsz, seq=seq)
    loss = lax.psum(loss, MESH_AXES)

    parts_a = _grad_parts_a(grads)
    from_sibling = _exchange_d2d(parts_a, name="exchange_grads_d2d")
    chip_parts = _chip_partial(parts_a, from_sibling, name="chip_partial_grads")
    landed_a = _exchange_ici(chip_parts, name="exchange_grads_ici")
    sum_a = _sum_parts(landed_a, name="sum_grads_a", br=GRAD_BLOCK)
    sum_b = _sum_parts(landed_b, name="sum_grads_b", br=WIRE_ROWS_B // 2)
    g_own = _own_grads(sum_a, sum_b)
    rep_slice = sum_a[MATRIX_ROWS_A + 8:MATRIX_ROWS_A + 8 + REP_SLICE]
    g_rep = _all_gather(rep_slice, name="gather_replicated").reshape(REP_ROWS, PACK_W)
    g_own.update(_unpack_rep(g_rep, wts))

    deltas, new_m, new_v = {}, {}, {}
    for n in WEIGHTS:
        deltas[n], new_m[n], new_v[n] = _adamw(g_own[n], wts[n], mom1[n], mom2[n], name="adamw_" + n)
    result = [loss, dx.reshape(bsz, seq, D_MODEL)]
    for d in (g_own, deltas, new_m, new_v):
        result.extend(d[n] for n in WEIGHTS)
    return tuple(result)
```

```python
import jax
import jax.numpy as jnp
from jax import lax
from jax.experimental import pallas as pl
from jax.experimental.pallas import tpu as pltpu

F32 = jnp.float32
BF16 = jnp.bfloat16
WIRE = jnp.bfloat16

D_MODEL = 1024
D_RNN = 1280
RNN_BLOCKS = 10
RNN_BW = 128
CONV_WIDTH = 4
LRU_C = 8.0
N_HEADS = 8
QK_NOPE = 128
QK_ROPE = 64
V_DIM = 128
KV_RANK = 256
Q_RANK = 384
ROPE_THETA = 10000.0
EPS = 1e-6
ATTN_SCALE = (QK_NOPE + QK_ROPE) ** -0.5
HEAD_PAD = 256
LANES = 128

ADAM_LR = 0.001
ADAM_B1 = 0.9
ADAM_B2 = 0.999
ADAM_EPS = 1e-08
ADAM_WD = 0.01
ADAM_STEP = 10

N_DEV = 8
MESH_AXES = ("x", "y", "c")
VMEM_LIMIT_BYTES = 56 * 2**20
PACK_W = 1024

_PIECES_A = (("w_in_a", 320),)
_PIECES_B = (("w_out_a", 160), ("w_dkv", 40), ("w_uk", 32), ("w_uv", 32), ("w_in_b", 176), ("w_uq", 96), ("w_out_b", 128))


def _offsets(pieces):
    off, r = {}, 0
    for n, k in pieces:
        off[n] = (r, r + k)
        r += k
    return off, r


_OFF_A, MATRIX_ROWS_A = _offsets(_PIECES_A)
_OFF_B, MATRIX_ROWS_B = _offsets(_PIECES_B)
WIRE_ROWS_A = MATRIX_ROWS_A + 16
WIRE_ROWS_B = 672
_SMALL = (("norm_a", 128), ("conv_w", 640), ("conv_b", 160), ("b_rg", 160), ("b_ig", 160), ("lru_lambda", 160))
_REP = (("w_rg", 163840), ("w_ig", 163840), ("norm_kv", 1024), ("kv_norm", 256), ("norm_b", 1024),
        ("q_norm", 384), ("final_norm", 1024))
REP_ROWS = 384
REP_SLICE = REP_ROWS // N_DEV
GRAD_ROWS_A = 384
GRAD_BLOCK = 192

WEIGHTS = ("norm_a", "w_in_a", "conv_w", "conv_b", "w_rg", "b_rg", "w_ig", "b_ig", "lru_lambda", "w_out_a",
           "norm_kv", "w_dkv", "kv_norm", "w_uk", "w_uv", "norm_b", "w_in_b", "q_norm", "w_uq", "w_out_b",
           "final_norm")


def _params(sem=None):
    return pltpu.CompilerParams(dimension_semantics=sem, vmem_limit_bytes=VMEM_LIMIT_BYTES)


_NT = (((1,), (1,)), ((), ()))
_ANY = pl.BlockSpec(memory_space=pl.ANY)


def _mesh_pos():
    return lax.axis_index("x"), lax.axis_index("y"), lax.axis_index("c")


def _sigmoid(z):
    return 0.5 * jnp.tanh(0.5 * z) + 0.5


def _sigmoid_tail(z):
    return 1.0 / (1.0 + jnp.exp(-z))


def _col_block(n):
    return n if n <= 1408 else n // 2


def _matmul(a, b, *, name, nt=False, out_dtype=F32, residual=None, bm=512):
    m, k = a.shape
    n = b.shape[0] if nt else b.shape[1]
    bm = min(bm, m)
    bn = _col_block(n)
    dims = (((1,), (1,)), ((), ())) if nt else (((1,), (0,)), ((), ()))
    has_res = residual is not None

    def body(*refs):
        a_ref, b_ref, o_ref = refs[0], refs[1], refs[-1]
        acc = lax.dot_general(a_ref[...].astype(BF16), b_ref[...].astype(BF16), dims, preferred_element_type=F32)
        if has_res:
            acc = acc + refs[2][...]
        o_ref[...] = acc.astype(out_dtype)

    in_specs = [pl.BlockSpec((bm, k), lambda i, j: (i, 0)),
                pl.BlockSpec((bn, k), lambda i, j: (j, 0)) if nt else pl.BlockSpec((k, bn), lambda i, j: (0, j))]
    args = [a, b]
    if has_res:
        in_specs.append(pl.BlockSpec((bm, bn), lambda i, j: (i, j)))
        args.append(residual)
    return pl.pallas_call(
        body, grid=(m // bm, n // bn), in_specs=in_specs, out_specs=pl.BlockSpec((bm, bn), lambda i, j: (i, j)),
        out_shape=jax.ShapeDtypeStruct((m, n), out_dtype), compiler_params=_params(("parallel", "parallel")),
        name=name)(*args)


def _matmul_tn(a, b, *, name, bt=512):
    t, m = a.shape
    n = b.shape[1]
    bt = min(bt, t)
    bm, bn = _col_block(m), _col_block(n)

    def body(a_ref, b_ref, o_ref):
        @pl.when(pl.program_id(2) == 0)
        def _():
            o_ref[...] = jnp.zeros_like(o_ref)

        o_ref[...] += lax.dot_general(a_ref[...].astype(BF16), b_ref[...].astype(BF16),
                                      (((0,), (0,)), ((), ())), preferred_element_type=F32)

    return pl.pallas_call(
        body, grid=(m // bm, n // bn, t // bt),
        in_specs=[pl.BlockSpec((bt, bm), lambda i, j, s: (s, i)), pl.BlockSpec((bt, bn), lambda i, j, s: (s, j))],
        out_specs=pl.BlockSpec((bm, bn), lambda i, j, s: (i, j)),
        out_shape=jax.ShapeDtypeStruct((m, n), F32),
        compiler_params=_params(("parallel", "parallel", "arbitrary")), name=name)(a, b)


def _swap_halves(v):
    ax = v.ndim - 1
    lane = lax.broadcasted_iota(jnp.int32, v.shape, ax)
    up = pltpu.roll(v, LANES - QK_ROPE // 2, axis=ax)
    down = pltpu.roll(v, QK_ROPE // 2, axis=ax)
    return jnp.where(lane < QK_ROPE // 2, up, jnp.where(lane < QK_ROPE, down, 0.0))


def _rope(v, cos, sin):
    return v * cos + _swap_halves(v) * sin


def _rope_t(d, cos, sin):
    return d * cos + _swap_halves(d * sin)


def _rope_tables(seq):
    pos = jnp.arange(seq, dtype=F32)
    inv = ROPE_THETA ** (-jnp.arange(0, QK_ROPE, 2, dtype=F32) / QK_ROPE)
    ang = pos[:, None] * inv[None, :]
    cos, sin = jnp.cos(ang), jnp.sin(ang)
    zero = jnp.zeros((seq, LANES - QK_ROPE), F32)
    return jnp.concatenate([cos, cos, zero], axis=1), jnp.concatenate([-sin, sin, zero], axis=1)


def _rms(v):
    return v * lax.rsqrt(jnp.mean(v * v, axis=-1, keepdims=True) + EPS)


def _const_spec(a):
    return pl.BlockSpec(a.shape, lambda i: (0,) * a.ndim)


def _lru_proj_fwd(x, g_a, w_in_t, *, name, bt=256):
    t, d = x.shape
    bt = min(bt, t)
    n = w_in_t.shape[0] // 2

    def body(x_ref, g_ref, wt_ref, h_ref, xp_ref, ga_ref):
        h = (_rms(x_ref[...]) * g_ref[...]).astype(BF16)
        h_ref[...] = h
        xp_ref[...] = lax.dot_general(h, wt_ref[0:n, :], _NT, preferred_element_type=F32)
        ga_ref[...] = lax.dot_general(h, wt_ref[n:2 * n, :], _NT, preferred_element_type=F32)

    row = lambda w: pl.BlockSpec((bt, w), lambda i: (i, 0))
    return pl.pallas_call(
        body, grid=(t // bt,), in_specs=[row(d), _const_spec(g_a), _const_spec(w_in_t)],
        out_specs=[row(d), row(n), row(n)],
        out_shape=[jax.ShapeDtypeStruct((t, d), BF16), jax.ShapeDtypeStruct((t, n), F32), jax.ShapeDtypeStruct((t, n), F32)],
        compiler_params=_params(("parallel",)), name=name)(x, g_a, w_in_t)


def _mla_proj_fwd(x1, gains, w, cos, sin, *, seq, name, bt=256):
    t, d = x1.shape
    bt = min(bt, seq)
    per_seq = seq // bt
    g_kv, g_b, g_kvn, g_q = gains
    consts = [g_kv, g_b, g_kvn, g_q, w["w_dkv_c"], w["w_dkv_r"], w["w_in_b_t"], w["w_uk"], w["w_uv"],
              w["w_uk_t"], w["w_uv_t"], w["w_uq"]]

    def body(x_ref, cos_ref, sin_ref, gkv_ref, gb_ref, gkvn_ref, gq_ref, wdc_ref, wdr_ref, wbt_ref,
             wuk_ref, wuv_ref, wukt_ref, wuvt_ref, wuq_ref,
             hk_ref, hq_ref, ck_ref, cqp_ref, g2_ref, ckv_ref, cq_ref, q_ref, kn_ref, v_ref, kr_ref, knt_ref, vt_ref, krt_ref):
        nrm = _rms(x_ref[...])
        hk = (nrm * gkv_ref[...]).astype(BF16)
        hq = (nrm * gb_ref[...]).astype(BF16)
        hk_ref[...] = hk
        hq_ref[...] = hq
        ck = jnp.dot(hk, wdc_ref[...], preferred_element_type=F32)
        ck_ref[...] = ck
        cqp = lax.dot_general(hq, wbt_ref[0:Q_RANK, :], _NT, preferred_element_type=F32)
        cqp_ref[...] = cqp
        g2_ref[...] = lax.dot_general(hq, wbt_ref[Q_RANK:, :], _NT, preferred_element_type=F32)
        cosv, sinv = cos_ref[...], sin_ref[...]
        kr = _rope(jnp.dot(hk, wdr_ref[...], preferred_element_type=F32), cosv, sinv)
        kr_ref[...] = kr.astype(BF16)
        krt_ref[...] = kr.T.astype(BF16)
        ckv = (_rms(ck) * gkvn_ref[...]).astype(BF16)
        ckv_ref[...] = ckv
        kn_ref[...] = jnp.dot(ckv, wuk_ref[...], preferred_element_type=F32).astype(BF16)
        v_ref[...] = jnp.dot(ckv, wuv_ref[...], preferred_element_type=F32).astype(BF16)
        knt_ref[...] = lax.dot_general(wukt_ref[...], ckv, _NT, preferred_element_type=F32).astype(BF16)
        vt_ref[...] = lax.dot_general(wuvt_ref[...], ckv, _NT, preferred_element_type=F32).astype(BF16)
        cq = (_rms(cqp) * gq_ref[...]).astype(BF16)
        cq_ref[...] = cq
        for h in range(N_HEADS):
            qh = jnp.dot(cq, wuq_ref[:, h * HEAD_PAD:(h + 1) * HEAD_PAD], preferred_element_type=F32)
            q_ref[:, h * HEAD_PAD:h * HEAD_PAD + QK_NOPE] = qh[:, :QK_NOPE].astype(BF16)
            q_ref[:, h * HEAD_PAD + QK_NOPE:(h + 1) * HEAD_PAD] = _rope(qh[:, QK_NOPE:], cosv, sinv).astype(BF16)

    row = lambda w_: pl.BlockSpec((bt, w_), lambda i: (i, 0))
    col = lambda h_: pl.BlockSpec((h_, bt), lambda i: (0, i))
    tab = pl.BlockSpec((bt, LANES), lambda i: (i % per_seq, 0))
    nh = N_HEADS * V_DIM
    shapes = [((t, d), BF16), ((t, d), BF16), ((t, KV_RANK), F32), ((t, Q_RANK), F32), ((t, nh), F32), ((t, KV_RANK), BF16),
              ((t, Q_RANK), BF16), ((t, N_HEADS * HEAD_PAD), BF16), ((t, nh), BF16), ((t, nh), BF16), ((t, LANES), BF16),
              ((nh, t), BF16), ((nh, t), BF16), ((LANES, t), BF16)]
    out_specs = [row(d), row(d), row(KV_RANK), row(Q_RANK), row(nh), row(KV_RANK), row(Q_RANK), row(N_HEADS * HEAD_PAD),
                 row(nh), row(nh), row(LANES), col(nh), col(nh), col(LANES)]
    return pl.pallas_call(
        body, grid=(t // bt,), in_specs=[row(d), tab, tab] + [_const_spec(a) for a in consts], out_specs=out_specs,
        out_shape=[jax.ShapeDtypeStruct(s, dt) for s, dt in shapes],
        compiler_params=_params(("parallel",)), name=name)(x1, cos, sin, *consts)


def _rms_bwd_rows(xv, dn):
    r = lax.rsqrt(jnp.mean(xv * xv, axis=-1, keepdims=True) + EPS)
    nrm = xv * r
    return r * (dn - nrm * jnp.mean(dn * nrm, axis=-1, keepdims=True)), nrm


def _col_sum(v):
    return jnp.sum(v, axis=0, keepdims=True)


def _lru_proj_bwd(dxp, dga, x, dx1, g_a, w_in_t, *, name, bt=256):
    t, d = x.shape
    bt = min(bt, t)
    n = w_in_t.shape[0] // 2

    def body(dxp_ref, dga_ref, x_ref, dx1_ref, g_ref, wt_ref, dx_ref, dg_ref):
        @pl.when(pl.program_id(0) == 0)
        def _():
            dg_ref[...] = jnp.zeros_like(dg_ref)

        dh = (jnp.dot(dxp_ref[...], wt_ref[0:n, :], preferred_element_type=F32)
              + jnp.dot(dga_ref[...], wt_ref[n:2 * n, :], preferred_element_type=F32))
        dxn, nrm = _rms_bwd_rows(x_ref[...], dh * g_ref[...])
        dg_ref[...] += _col_sum(dh * nrm)
        dx_ref[...] = dx1_ref[...] + dxn

    row = lambda w: pl.BlockSpec((bt, w), lambda i: (i, 0))
    return pl.pallas_call(
        body, grid=(t // bt,),
        in_specs=[row(n), row(n), row(d), row(d), _const_spec(g_a), _const_spec(w_in_t)],
        out_specs=[row(d), _const_spec(g_a)],
        out_shape=[jax.ShapeDtypeStruct((t, d), F32), jax.ShapeDtypeStruct((1, d), F32)],
        compiler_params=_params(("arbitrary",)), name=name)(dxp, dga, x, dx1, g_a, w_in_t)


def _mla_proj_bwd(x1, dx2, cqp, ck, dq, dkn, dv, dkr, dg2, gains, w, *, name, bt=256):
    t, d = x1.shape
    bt = min(bt, t)
    g_kv, g_b, g_kvn, g_q = gains
    consts = [g_kv, g_b, g_kvn, g_q, w["w_dkv_c"], w["w_dkv_r"], w["w_in_b_t"], w["w_uk"], w["w_uv"], w["w_uq"]]
    nh = N_HEADS * V_DIM

    def body(x1_ref, dx2_ref, cqp_ref, ck_ref, dq_ref, dkn_ref, dv_ref, dkr_ref, dg2_ref,
             gkv_ref, gb_ref, gkvn_ref, gq_ref, wdc_ref, wdr_ref, wbt_ref, wuk_ref, wuv_ref, wuq_ref,
             dx1_ref, du2_ref, dckr_ref, dgkv_ref, dgb_ref, dgkvn_ref, dgq_ref):
        @pl.when(pl.program_id(0) == 0)
        def _():
            for ref in (dgkv_ref, dgb_ref, dgkvn_ref, dgq_ref):
                ref[...] = jnp.zeros_like(ref)

        dot_nt = lambda a, b: lax.dot_general(a, b, _NT, preferred_element_type=F32)
        dcq = dot_nt(dq_ref[...], wuq_ref[...])
        dcqp, nq = _rms_bwd_rows(cqp_ref[...], dcq * gq_ref[...])
        dgq_ref[...] += _col_sum(dcq * nq)
        dcqp = dcqp.astype(BF16)
        dg2 = dg2_ref[...]
        du2_ref[:, :Q_RANK] = dcqp
        du2_ref[:, Q_RANK:] = dg2
        dhq = (jnp.dot(dcqp, wbt_ref[0:Q_RANK, :], preferred_element_type=F32)
               + jnp.dot(dg2, wbt_ref[Q_RANK:, :], preferred_element_type=F32))
        dckv = dot_nt(dkn_ref[...], wuk_ref[...]) + dot_nt(dv_ref[...], wuv_ref[...])
        dck, nc = _rms_bwd_rows(ck_ref[...], dckv * gkvn_ref[...])
        dgkvn_ref[...] += _col_sum(dckv * nc)
        dck = dck.astype(BF16)
        dkr = dkr_ref[...].astype(BF16)
        dckr_ref[:, :KV_RANK] = dck
        dckr_ref[:, KV_RANK:] = dkr
        dhk = dot_nt(dck, wdc_ref[...]) + dot_nt(dkr, wdr_ref[...])
        dxn, n1 = _rms_bwd_rows(x1_ref[...], dhq * gb_ref[...] + dhk * gkv_ref[...])
        dgb_ref[...] += _col_sum(dhq * n1)
        dgkv_ref[...] += _col_sum(dhk * n1)
        dx1_ref[...] = dx2_ref[...] + dxn

    row = lambda w_: pl.BlockSpec((bt, w_), lambda i: (i, 0))
    vec = lambda w_: pl.BlockSpec((1, w_), lambda i: (0, 0))
    in_specs = [row(d), row(d), row(Q_RANK), row(KV_RANK), row(N_HEADS * HEAD_PAD), row(nh), row(nh), row(LANES), row(nh)]
    return pl.pallas_call(
        body, grid=(t // bt,), in_specs=in_specs + [_const_spec(a) for a in consts],
        out_specs=[row(d), row(Q_RANK + nh), row(KV_RANK + LANES), vec(d), vec(d), vec(KV_RANK), vec(Q_RANK)],
        out_shape=[jax.ShapeDtypeStruct((t, d), F32), jax.ShapeDtypeStruct((t, Q_RANK + nh), BF16),
                   jax.ShapeDtypeStruct((t, KV_RANK + LANES), BF16), jax.ShapeDtypeStruct((1, d), F32),
                   jax.ShapeDtypeStruct((1, d), F32), jax.ShapeDtypeStruct((1, KV_RANK), F32),
                   jax.ShapeDtypeStruct((1, Q_RANK), F32)],
        compiler_params=_params(("arbitrary",)), name=name)(x1, dx2, cqp, ck, dq, dkn, dv, dkr, dg2, *consts)


def _softplus(z):
    return jnp.maximum(z, 0.0) + jnp.log1p(jnp.exp(-jnp.abs(z)))


def _neg_expm1(z):
    series = -z * (1.0 + z * (1.0 / 2) * (1.0 + z * (1.0 / 3) * (1.0 + z * (1.0 / 4))))
    return jnp.where(z > -0.01, series, 1.0 - jnp.exp(z))


def _gates(xb, wrg, wig, brg, big, sp):
    xbb = xb.astype(BF16)
    r = _sigmoid_tail(jnp.dot(xbb, wrg, preferred_element_type=F32) + brg)
    i = _sigmoid(jnp.dot(xbb, wig, preferred_element_type=F32) + big)
    la = (-LRU_C) * r * sp
    a = jnp.exp(la)
    mult = jnp.sqrt(_neg_expm1(2.0 * la))
    return r, i, a, mult


def _conv(xpad_ref, cw_ref, seq):
    acc = cw_ref[0:1, :] * xpad_ref[pl.ds(8 - (CONV_WIDTH - 1), seq), :]
    for k in range(1, CONV_WIDTH):
        acc = acc + cw_ref[k:k + 1, :] * xpad_ref[pl.ds(8 - (CONV_WIDTH - 1) + k, seq), :]
    return acc


def _seq_spec(seq):
    return pl.BlockSpec((None, seq, RNN_BW), lambda n, b: (b, 0, n))


def _chan_spec(rows):
    return pl.BlockSpec((rows, RNN_BW), lambda n, b: (0, n))


_GATE_W_SPEC = pl.BlockSpec((None, RNN_BW, RNN_BW), lambda n, b: (n, 0, 0))


SCAN_UNROLL = 4


def _peers():
    x, y, c = _mesh_pos()
    others = []
    for k in range(1, N_DEV):
        px = 1 - x if k & 4 else x
        py = 1 - y if k & 2 else y
        pc = 1 - c if k & 1 else c
        others.append(((px, py, pc), 4 * px + 2 * py + pc))
    return 4 * x + 2 * y + c, others


def _exchange(src_ref, dst_ref, send_sems, recv_sems, local_sem, *, finish):
    me, others = _peers()

    def send(k, dev, slot):
        return pltpu.make_async_remote_copy(
            src_ref=src_ref.at[slot], dst_ref=dst_ref.at[me], send_sem=send_sems.at[k],
            recv_sem=recv_sems.at[k], device_id=dev, device_id_type=pl.DeviceIdType.MESH)

    local = pltpu.make_async_copy(src_ref.at[me], dst_ref.at[me], local_sem)
    if not finish:
        local.start()
        for k, (dev, slot) in enumerate(others):
            send(k, dev, slot).start()
        return
    for k, (dev, slot) in enumerate(others):
        pltpu.make_async_remote_copy(
            src_ref=dst_ref.at[slot], dst_ref=dst_ref.at[slot], send_sem=send_sems.at[k], recv_sem=recv_sems.at[k],
            device_id=dev, device_id_type=pl.DeviceIdType.MESH).wait_recv()
    for k, (dev, slot) in enumerate(others):
        send(k, dev, slot).wait_send()
    local.wait()


def _gather_two_level(x_ref, out_ref, send_sems, recv_sems, local_sem, *, phase):
    x, y, c = _mesh_pos()
    me, sibling = (x, y, c), (x, y, 1 - c)
    chips = [(1 - x, y), (x, 1 - y), (1 - x, 1 - y)]

    def slot(px, py, pc):
        return out_ref.at[4 * px + 2 * py + pc]

    def copy(k, blk, to, src=None):
        return pltpu.make_async_remote_copy(
            src_ref=slot(*blk) if src is None else src, dst_ref=slot(*blk),
            send_sem=send_sems.at[k], recv_sem=recv_sems.at[k], device_id=to, device_id_type=pl.DeviceIdType.MESH)

    if phase == 0:
        pltpu.make_async_copy(x_ref, slot(*me), local_sem).start()
        copy(0, me, sibling, src=x_ref).start()
        for j, chip in enumerate(chips):
            copy(1 + j, me, (*chip, c), src=x_ref).start()
    elif phase == 1:
        for j, chip in enumerate(chips):
            copy(1 + j, (*chip, c), me).wait_recv()
            copy(4 + j, (*chip, c), sibling).start()
    else:
        copy(0, sibling, me).wait_recv()
        for j, chip in enumerate(chips):
            copy(4 + j, (*chip, 1 - c), me).wait_recv()
        copy(0, me, sibling, src=x_ref).wait_send()
        for j, chip in enumerate(chips):
            copy(1 + j, me, (*chip, c), src=x_ref).wait_send()
            copy(4 + j, (*chip, c), sibling).wait_send()
        pltpu.make_async_copy(x_ref, slot(*me), local_sem).wait()


GATHER_FORWARD_STEP = 7
_EXCHANGE_SEMS = [pltpu.SemaphoreType.DMA((N_DEV - 1,)), pltpu.SemaphoreType.DMA((N_DEV - 1,)), pltpu.SemaphoreType.DMA(())]


def _grid_ends(bsz):
    n, b = pl.program_id(0), pl.program_id(1)
    return (n == 0) & (b == 0), (n == RNN_BLOCKS - 1) & (b == bsz - 1)


def _lru_fwd(xp, ga, cw, vecs, wrg, wig, block, *, name):
    bsz, seq, _ = xp.shape
    groups = seq // 8

    def body(xp_ref, ga_ref, cw_ref, vec_ref, wrg_ref, wig_ref, blk_ref, xb_ref, hs_ref, y_ref, all_ref,
             xpad, a_s, b_s, send_sems, recv_sems, local_sem):
        first, last = _grid_ends(bsz)

        @pl.when(first)
        def _():
            _gather_two_level(blk_ref, all_ref, send_sems, recv_sems, local_sem, phase=0)

        @pl.when((pl.program_id(0) == GATHER_FORWARD_STEP) & (pl.program_id(1) == 0))
        def _():
            _gather_two_level(blk_ref, all_ref, send_sems, recv_sems, local_sem, phase=1)

        xpad[0:8, :] = jnp.zeros((8, RNN_BW), F32)
        xpad[pl.ds(8, seq), :] = xp_ref[...]
        xb = _conv(xpad, cw_ref, seq) + vec_ref[0:1, :]
        xb_ref[...] = xb
        sp = _softplus(-vec_ref[3:4, :])
        _, i, a, mult = _gates(xb, wrg_ref[...], wig_ref[...], vec_ref[1:2, :], vec_ref[2:3, :], sp)
        a_s[...] = a
        b_s[...] = mult * (i * xb)
        row = lax.broadcasted_iota(jnp.int32, (8, RNN_BW), 0)

        def group(g, h):
            r0 = pl.multiple_of(g * 8, 8)
            av = a_s[pl.ds(r0, 8), :]
            bv = b_s[pl.ds(r0, 8), :]
            for k in (1, 2, 4):
                m = row >= k
                bv = jnp.where(m, av * pltpu.roll(bv, k, axis=0) + bv, bv)
                av = jnp.where(m, av * pltpu.roll(av, k, axis=0), av)
            hs_ref[pl.ds(r0, 8), :] = av * h + bv
            return av[7:8, :] * h + bv[7:8, :]

        def groups_of(i, h):
            for u in range(SCAN_UNROLL):
                h = group(i * SCAN_UNROLL + u, h)
            return h

        lax.fori_loop(0, groups // SCAN_UNROLL, groups_of, jnp.zeros((1, RNN_BW), F32))
        gav = ga_ref[...]
        y_ref[...] = (hs_ref[...] * (gav * _sigmoid(gav))).astype(BF16)

        @pl.when(last)
        def _():
            _gather_two_level(blk_ref, all_ref, send_sems, recv_sems, local_sem, phase=2)

    sq = _seq_spec(seq)
    shape = (bsz, seq, D_RNN)
    return pl.pallas_call(
        body, grid=(RNN_BLOCKS, bsz),
        in_specs=[sq, sq, _chan_spec(8), _chan_spec(8), _GATE_W_SPEC, _GATE_W_SPEC, _ANY],
        out_specs=[sq, sq, sq, _ANY],
        out_shape=[jax.ShapeDtypeStruct(shape, F32), jax.ShapeDtypeStruct(shape, F32), jax.ShapeDtypeStruct(shape, BF16),
                   jax.ShapeDtypeStruct((N_DEV,) + block.shape, block.dtype)],
        scratch_shapes=[pltpu.VMEM((seq + 8, RNN_BW), F32), pltpu.VMEM((seq, RNN_BW), F32), pltpu.VMEM((seq, RNN_BW), F32)]
        + _EXCHANGE_SEMS,
        compiler_params=_params(("arbitrary", "arbitrary")), name=name)(xp, ga, cw, vecs, wrg, wig, block)


def _lru_bwd(dy, xp, xb, hs, ga, cw, vecs, wrg, wig, parts, *, name):
    bsz, seq, _ = xp.shape
    groups = seq // 8

    def body(dy_ref, xp_ref, xb_ref, hs_ref, ga_ref, cw_ref, vec_ref, wrg_ref, wig_ref,
             parts_ref, dxp_ref, dga_ref, dwrg_ref, dwig_ref, dvec_ref, land_ref, pad, a_s, d_s, lam_s,
             send_sems, recv_sems, local_sem):
        first, last = _grid_ends(bsz)

        @pl.when(first)
        def _():
            _exchange(parts_ref, land_ref, send_sems, recv_sems, local_sem, finish=False)

        @pl.when(pl.program_id(1) == 0)
        def _():
            dwrg_ref[...] = jnp.zeros_like(dwrg_ref)
            dwig_ref[...] = jnp.zeros_like(dwig_ref)
            dvec_ref[...] = jnp.zeros_like(dvec_ref)

        xb = xb_ref[...]
        hs = hs_ref[...]
        gav = ga_ref[...]
        dy = dy_ref[...]
        sp = _softplus(-vec_ref[3:4, :])
        wrg = wrg_ref[...]
        wig = wig_ref[...]
        r, i, a, mult = _gates(xb, wrg, wig, vec_ref[1:2, :], vec_ref[2:3, :], sp)
        sg = _sigmoid(gav)
        dga_ref[...] = (dy * hs * (sg * (1.0 + gav * (1.0 - sg)))).astype(BF16)
        d_s[...] = dy * (gav * sg)

        pad[pl.ds(0, seq), :] = a
        pad[pl.ds(seq, 8), :] = jnp.zeros((8, RNN_BW), F32)
        a_s[...] = pad[pl.ds(1, seq), :]
        row = lax.broadcasted_iota(jnp.int32, (8, RNN_BW), 0)

        def group(g, nxt):
            r0 = pl.multiple_of((groups - 1 - g) * 8, 8)
            cv = a_s[pl.ds(r0, 8), :]
            bv = d_s[pl.ds(r0, 8), :]
            for k in (1, 2, 4):
                m = row < 8 - k
                bv = jnp.where(m, cv * pltpu.roll(bv, 8 - k, axis=0) + bv, bv)
                cv = jnp.where(m, cv * pltpu.roll(cv, 8 - k, axis=0), cv)
            lam_s[pl.ds(r0, 8), :] = cv * nxt + bv
            return cv[0:1, :] * nxt + bv[0:1, :]

        def groups_of(i, nxt):
            for u in range(SCAN_UNROLL):
                nxt = group(i * SCAN_UNROLL + u, nxt)
            return nxt

        lax.fori_loop(0, groups // SCAN_UNROLL, groups_of, jnp.zeros((1, RNN_BW), F32))
        dh = lam_s[...]

        pad[0:8, :] = jnp.zeros((8, RNN_BW), F32)
        pad[pl.ds(8, seq), :] = hs
        da = dh * pad[pl.ds(7, seq), :]
        ixb = i * xb
        dixb = dh * mult
        dla = da * a - (dh * ixb) * (a * a) / mult
        drp = (dla * ((-LRU_C) * sp)) * r * (1.0 - r)
        dip = (dixb * xb) * i * (1.0 - i)
        dvec_ref[0:1, :] += jnp.sum(drp, axis=0, keepdims=True)
        dvec_ref[1:2, :] += jnp.sum(dip, axis=0, keepdims=True)
        dvec_ref[2:3, :] += jnp.sum(dla * ((-LRU_C) * r), axis=0, keepdims=True)
        drpb = drp.astype(BF16)
        dipb = dip.astype(BF16)
        xbb = xb.astype(BF16)
        nt = (((1,), (1,)), ((), ()))
        tn = (((0,), (0,)), ((), ()))
        dxb = (dixb * i
               + lax.dot_general(drpb, wrg, nt, preferred_element_type=F32)
               + lax.dot_general(dipb, wig, nt, preferred_element_type=F32))
        dwrg_ref[...] += lax.dot_general(xbb, drpb, tn, preferred_element_type=F32)
        dwig_ref[...] += lax.dot_general(xbb, dipb, tn, preferred_element_type=F32)
        dvec_ref[3:4, :] += jnp.sum(dxb, axis=0, keepdims=True)

        pad[pl.ds(0, seq), :] = dxb
        pad[pl.ds(seq, 8), :] = jnp.zeros((8, RNN_BW), F32)
        dxp = cw_ref[0:1, :] * pad[pl.ds(CONV_WIDTH - 1, seq), :]
        for k in range(1, CONV_WIDTH):
            dxp = dxp + cw_ref[k:k + 1, :] * pad[pl.ds(CONV_WIDTH - 1 - k, seq), :]
        dxp_ref[...] = dxp.astype(BF16)
        pad[0:8, :] = jnp.zeros((8, RNN_BW), F32)
        pad[pl.ds(8, seq), :] = xp_ref[...]
        for k in range(CONV_WIDTH):
            dvec_ref[4 + k:5 + k, :] += jnp.sum(dxb * pad[pl.ds(8 - (CONV_WIDTH - 1) + k, seq), :], axis=0, keepdims=True)

        @pl.when(last)
        def _():
            _exchange(parts_ref, land_ref, send_sems, recv_sems, local_sem, finish=True)

    sq = _seq_spec(seq)
    shape = (bsz, seq, D_RNN)
    gshape = (RNN_BLOCKS, RNN_BW, RNN_BW)
    return pl.pallas_call(
        body, grid=(RNN_BLOCKS, bsz),
        in_specs=[sq, sq, sq, sq, sq, _chan_spec(8), _chan_spec(8), _GATE_W_SPEC, _GATE_W_SPEC, _ANY],
        out_specs=[sq, sq, _GATE_W_SPEC, _GATE_W_SPEC, _chan_spec(8), _ANY],
        out_shape=[jax.ShapeDtypeStruct(shape, BF16), jax.ShapeDtypeStruct(shape, BF16),
                   jax.ShapeDtypeStruct(gshape, F32), jax.ShapeDtypeStruct(gshape, F32),
                   jax.ShapeDtypeStruct((8, D_RNN), F32), jax.ShapeDtypeStruct(parts.shape, parts.dtype)],
        scratch_shapes=[pltpu.VMEM((seq + 8, RNN_BW), F32), pltpu.VMEM((seq, RNN_BW), F32),
                        pltpu.VMEM((seq, RNN_BW), F32), pltpu.VMEM((seq, RNN_BW), F32)] + _EXCHANGE_SEMS,
        compiler_params=_params(("arbitrary", "arbitrary")), name=name)(dy, xp, xb, hs, ga, cw, vecs, wrg, wig, parts)


def _attn_block(seq):
    return min(512, seq)


def _diag_mask(blk):
    return lax.broadcasted_iota(jnp.int32, (blk, blk), 0) <= lax.broadcasted_iota(jnp.int32, (blk, blk), 1)


FWD_HEADS = 4
BWD_HEADS = 2


def _attn_fwd(q, kn, kr, v_t, *, bsz, seq, name):
    t = bsz * seq
    blk = _attn_block(seq)
    nq = seq // blk
    hg = FWD_HEADS

    def body(q_ref, kn_ref, kr_ref, vt_ref, o_ref, lse_ref, acc):
        qi = pl.program_id(2)
        acc[...] = jnp.zeros_like(acc)

        def step(j, carry, diagonal):
            k0 = pl.multiple_of(j * blk, blk)
            kr_j = kr_ref[pl.ds(k0, blk), :]
            out = []
            for h in range(hg):
                m_i, l_i = carry[h]
                kv = jnp.concatenate([kn_ref[pl.ds(k0, blk), h * QK_NOPE:(h + 1) * QK_NOPE], kr_j], axis=1)
                qv = q_ref[:, h * HEAD_PAD:(h + 1) * HEAD_PAD]
                s = lax.dot_general(kv, qv, _NT, preferred_element_type=F32) * ATTN_SCALE
                if diagonal:
                    s = jnp.where(_diag_mask(blk), s, -jnp.inf)
                m_new = jnp.maximum(m_i, jnp.max(s, axis=0, keepdims=True))
                p = jnp.exp(s - m_new)
                alpha = jnp.exp(m_i - m_new)
                l_new = alpha * l_i + jnp.sum(p, axis=0, keepdims=True)
                acc[h] = alpha * acc[h] + jnp.dot(vt_ref[h * V_DIM:(h + 1) * V_DIM, pl.ds(k0, blk)], p.astype(BF16),
                                                  preferred_element_type=F32)
                out.append((m_new, l_new))
            return tuple(out)

        init = tuple((jnp.full((1, blk), -jnp.inf, F32), jnp.zeros((1, blk), F32)) for _ in range(hg))
        carry = lax.fori_loop(0, qi, lambda j, c: step(j, c, False), init)
        stats = step(qi, carry, True)
        for h in range(hg):
            m_i, l_i = stats[h]
            o_ref[:, h * V_DIM:(h + 1) * V_DIM] = (acc[h] / l_i).T
            lse_ref[h] = m_i + jnp.log(l_i)

    return pl.pallas_call(
        body, grid=(bsz, N_HEADS // hg, nq),
        in_specs=[pl.BlockSpec((blk, hg * HEAD_PAD), lambda b, g, i: (b * nq + i, g)),
                  pl.BlockSpec((seq, hg * QK_NOPE), lambda b, g, i: (b, g)),
                  pl.BlockSpec((seq, LANES), lambda b, g, i: (b, 0)),
                  pl.BlockSpec((hg * V_DIM, seq), lambda b, g, i: (g, b))],
        out_specs=[pl.BlockSpec((blk, hg * V_DIM), lambda b, g, i: (b * nq + i, g)),
                   pl.BlockSpec((hg, 1, blk), lambda b, g, i: (g, 0, b * nq + i))],
        out_shape=[jax.ShapeDtypeStruct((t, N_HEADS * V_DIM), F32), jax.ShapeDtypeStruct((N_HEADS, 1, t), F32)],
        scratch_shapes=[pltpu.VMEM((hg, V_DIM, blk), F32)],
        compiler_params=_params(("parallel", "parallel", "parallel")), name=name)(q, kn, kr, v_t)


def _attn_bwd(q, kn, kr, kn_t, kr_t, v, o, lse, do, cos, sin, *, bsz, seq, name):
    t = bsz * seq
    blk = _attn_block(seq)
    nq = seq // blk
    hg = BWD_HEADS

    def body(q_ref, kn_ref, kr_ref, knt_ref, krt_ref, v_ref, o_ref, lse_ref, do_ref, cos_ref, sin_ref,
             dq_ref, dkn_ref, dkr_ref, dv_ref, dqt_acc, dk_acc, dv_acc):
        dqt_acc[...] = jnp.zeros_like(dqt_acc)
        dk_acc[...] = jnp.zeros_like(dk_acc)
        dv_acc[...] = jnp.zeros_like(dv_acc)

        def q_block(i, _):
            q0 = pl.multiple_of(i * blk, blk)
            rows = []
            for h in range(hg):
                dov = do_ref[pl.ds(q0, blk), h * V_DIM:(h + 1) * V_DIM].astype(F32)
                dcol = jnp.sum(dov * o_ref[pl.ds(q0, blk), h * V_DIM:(h + 1) * V_DIM], axis=-1, keepdims=True)
                delta = jnp.broadcast_to(dcol, (blk, LANES)).T[0:1, :]
                rows.append((lse_ref[h, :, pl.ds(q0, blk)], delta))

            def pair(j, diagonal):
                k0 = pl.multiple_of(j * blk, blk)
                kr_j = kr_ref[pl.ds(k0, blk), :]
                krt_j = krt_ref[:, pl.ds(k0, blk)]
                for h in range(hg):
                    lse_i, delta = rows[h]
                    qv = q_ref[pl.ds(q0, blk), h * HEAD_PAD:(h + 1) * HEAD_PAD]
                    dov = do_ref[pl.ds(q0, blk), h * V_DIM:(h + 1) * V_DIM]
                    kv = jnp.concatenate([kn_ref[pl.ds(k0, blk), h * QK_NOPE:(h + 1) * QK_NOPE], kr_j], axis=1)
                    s = lax.dot_general(kv, qv, _NT, preferred_element_type=F32) * ATTN_SCALE
                    p = jnp.exp(s - lse_i)
                    if diagonal:
                        p = jnp.where(_diag_mask(blk), p, 0.0)
                    dv_acc[pl.ds(k0, blk), h * V_DIM:(h + 1) * V_DIM] += jnp.dot(
                        p.astype(BF16), dov, preferred_element_type=F32)
                    dp = lax.dot_general(v_ref[pl.ds(k0, blk), h * V_DIM:(h + 1) * V_DIM], dov, _NT,
                                         preferred_element_type=F32)
                    ds = (p * (dp - delta) * ATTN_SCALE).astype(BF16)
                    dk_acc[pl.ds(k0, blk), h * HEAD_PAD:(h + 1) * HEAD_PAD] += jnp.dot(ds, qv, preferred_element_type=F32)
                    base = h * HEAD_PAD
                    dqt_acc[base:base + QK_NOPE, pl.ds(q0, blk)] += jnp.dot(
                        knt_ref[h * QK_NOPE:(h + 1) * QK_NOPE, pl.ds(k0, blk)], ds, preferred_element_type=F32)
                    dqt_acc[base + QK_NOPE:base + HEAD_PAD, pl.ds(q0, blk)] += jnp.dot(
                        krt_j, ds, preferred_element_type=F32)

            def off_diagonal(j, _):
                pair(j, False)
                return 0

            lax.fori_loop(0, i, off_diagonal, 0)
            pair(i, True)
            return 0

        lax.fori_loop(0, nq, q_block, 0)
        dkr = jnp.zeros((seq, LANES), F32)
        for h in range(hg):
            base = h * HEAD_PAD
            for i in range(nq):
                rows = slice(i * blk, (i + 1) * blk)
                dq = dqt_acc[base:base + HEAD_PAD, rows].T
                dq_ref[rows, base:base + QK_NOPE] = dq[:, :QK_NOPE].astype(BF16)
                dq_ref[rows, base + QK_NOPE:base + HEAD_PAD] = _rope_t(
                    dq[:, QK_NOPE:], cos_ref[rows, :], sin_ref[rows, :]).astype(BF16)
            dkn_ref[:, h * QK_NOPE:(h + 1) * QK_NOPE] = dk_acc[:, base:base + QK_NOPE].astype(BF16)
            dkr = dkr + dk_acc[:, base + QK_NOPE:base + HEAD_PAD]
        dv_ref[...] = dv_acc[...].astype(BF16)

        @pl.when(pl.program_id(1) == 0)
        def _():
            dkr_ref[...] = jnp.zeros_like(dkr_ref)

        dkr_ref[...] += _rope_t(dkr, cos_ref[...], sin_ref[...])

    head = pl.BlockSpec((seq, hg * V_DIM), lambda b, g: (b, g))
    head_t = pl.BlockSpec((hg * V_DIM, seq), lambda b, g: (g, b))
    shared = pl.BlockSpec((seq, LANES), lambda b, g: (b, 0))
    shared_t = pl.BlockSpec((LANES, seq), lambda b, g: (0, b))
    table = pl.BlockSpec((seq, LANES), lambda b, g: (0, 0))
    qspec = pl.BlockSpec((seq, hg * HEAD_PAD), lambda b, g: (b, g))
    return pl.pallas_call(
        body, grid=(bsz, N_HEADS // hg),
        in_specs=[qspec, head, shared, head_t, shared_t, head, head,
                  pl.BlockSpec((hg, 1, seq), lambda b, g: (g, 0, b)), head, table, table],
        out_specs=[qspec, head, shared, head],
        out_shape=[jax.ShapeDtypeStruct((t, N_HEADS * HEAD_PAD), BF16), jax.ShapeDtypeStruct((t, N_HEADS * QK_NOPE), BF16),
                   jax.ShapeDtypeStruct((t, LANES), F32), jax.ShapeDtypeStruct((t, N_HEADS * V_DIM), BF16)],
        scratch_shapes=[pltpu.VMEM((hg * HEAD_PAD, seq), F32), pltpu.VMEM((seq, hg * HEAD_PAD), F32),
                        pltpu.VMEM((seq, hg * V_DIM), F32)],
        compiler_params=_params(("parallel", "arbitrary")), name=name)(q, kn, kr, kn_t, kr_t, v, o, lse, do, cos, sin)


def _head_and_loss(o, g2, x1, target, w_out, g_final, *, name, bt=256):
    t, d = x1.shape
    bt = min(bt, t)
    nt = (((1,), (1,)), ((), ()))

    def body(o_ref, g2_ref, x1_ref, tgt_ref, w_ref, gf_ref, loss_ref, dx2_ref, y2_ref, do_ref, dg2_ref, dgf_ref):
        @pl.when(pl.program_id(0) == 0)
        def _():
            loss_ref[...] = jnp.zeros_like(loss_ref)
            dgf_ref[...] = jnp.zeros_like(dgf_ref)

        ov = o_ref[...]
        gv = g2_ref[...]
        sg = _sigmoid(gv)
        silu = gv * sg
        y2 = (ov * silu).astype(BF16)
        y2_ref[...] = y2
        w = w_ref[...]
        x2 = x1_ref[...] + jnp.dot(y2, w, preferred_element_type=F32)
        r = lax.rsqrt(jnp.mean(x2 * x2, axis=-1, keepdims=True) + EPS)
        nrm = x2 * r
        gf = gf_ref[...]
        err = nrm * gf - tgt_ref[...]
        loss_ref[...] += 0.5 * jnp.sum(jnp.mean(err * err, axis=-1, keepdims=True))
        dyf = err * (1.0 / d)
        dgf_ref[...] += jnp.sum(dyf * nrm, axis=0, keepdims=True)
        dn = dyf * gf
        dx2 = r * (dn - nrm * jnp.mean(dn * nrm, axis=-1, keepdims=True))
        dx2_ref[...] = dx2
        dy2 = lax.dot_general(dx2.astype(BF16), w, nt, preferred_element_type=F32)
        do_ref[...] = (dy2 * silu).astype(BF16)
        dg2_ref[...] = (dy2 * ov * (sg * (1.0 + gv * (1.0 - sg)))).astype(BF16)

    row = pl.BlockSpec((bt, d), lambda i: (i, 0))
    vec = pl.BlockSpec((1, d), lambda i: (0, 0))
    return pl.pallas_call(
        body, grid=(t // bt,),
        in_specs=[row, row, row, row, pl.BlockSpec((d, d), lambda i: (0, 0)), vec],
        out_specs=[pl.BlockSpec((8, LANES), lambda i: (0, 0)), row, row, row, row, vec],
        out_shape=[jax.ShapeDtypeStruct((8, LANES), F32), jax.ShapeDtypeStruct((t, d), F32),
                   jax.ShapeDtypeStruct((t, d), BF16), jax.ShapeDtypeStruct((t, d), BF16),
                   jax.ShapeDtypeStruct((t, d), BF16), jax.ShapeDtypeStruct((1, d), F32)],
        compiler_params=_params(("arbitrary",)), name=name)(o, g2, x1, target, w_out, g_final)


def _sum_parts(parts, *, name, br=GRAD_BLOCK):
    npart, rows, w = parts.shape

    def body(p_ref, o_ref):
        acc = p_ref[0].astype(F32)
        for j in range(1, npart):
            acc = acc + p_ref[j].astype(F32)
        o_ref[...] = acc

    return pl.pallas_call(
        body, grid=(rows // br,), in_specs=[pl.BlockSpec((npart, br, w), lambda i: (0, i, 0))],
        out_specs=pl.BlockSpec((br, w), lambda i: (i, 0)), out_shape=jax.ShapeDtypeStruct((rows, w), F32),
        compiler_params=_params(("parallel",)), name=name)(parts)


def _chip_partial(parts, recv, *, name, br=GRAD_BLOCK):
    _, rows, w = parts.shape
    core = lax.axis_index("c").astype(jnp.int32).reshape(1)

    def body(c_ref, p_ref, r_ref, o_ref):
        o_ref[...] = (p_ref[...] + r_ref[...]).astype(BF16)

    grid_spec = pltpu.PrefetchScalarGridSpec(
        num_scalar_prefetch=1, grid=(4, rows // br),
        in_specs=[pl.BlockSpec((None, br, w), lambda k, i, c_ref: (2 * k + c_ref[0], i, 0)),
                  pl.BlockSpec((None, br, w), lambda k, i, c_ref: (k, i, 0))],
        out_specs=pl.BlockSpec((None, br, w), lambda k, i, c_ref: (k, i, 0)))
    return pl.pallas_call(
        body, grid_spec=grid_spec, out_shape=jax.ShapeDtypeStruct((4, rows, w), BF16),
        compiler_params=_params(("parallel", "parallel")), name=name)(core, parts, recv)


def _as_block(a):
    if a.ndim == 1:
        return a.reshape(1, -1)
    if a.ndim > 2 and a.shape[0] == 1:
        return a.reshape(a.shape[1:])
    return a


def _adamw(g, w, m, v, *, name):
    shape = w.shape
    g, w, m, v = (_as_block(a) for a in (g, w, m, v))

    def body(g_ref, w_ref, m_ref, v_ref, d_ref, nm_ref, nv_ref):
        gv = g_ref[...]
        nm = ADAM_B1 * m_ref[...] + (1.0 - ADAM_B1) * gv
        nv = ADAM_B2 * v_ref[...] + (1.0 - ADAM_B2) * (gv * gv)
        nm_ref[...] = nm
        nv_ref[...] = nv
        m_hat = nm / (1.0 - ADAM_B1 ** ADAM_STEP)
        v_hat = nv / (1.0 - ADAM_B2 ** ADAM_STEP)
        d_ref[...] = (-ADAM_LR) * (m_hat / (jnp.sqrt(v_hat) + ADAM_EPS) + ADAM_WD * w_ref[...])

    whole = pl.BlockSpec(memory_space=pltpu.VMEM)
    outs = pl.pallas_call(
        body, in_specs=[whole] * 4, out_specs=[whole] * 3, out_shape=[jax.ShapeDtypeStruct(w.shape, F32)] * 3,
        compiler_params=_params(), name=name)(g, w, m, v)
    return [o.reshape(shape) for o in outs]


def _all_gather(block, *, name):
    m, n = block.shape

    def body(x_ref, out_ref, send_sems, recv_sems, local_sem):
        for phase in range(3):
            _gather_two_level(x_ref, out_ref, send_sems, recv_sems, local_sem, phase=phase)

    return pl.pallas_call(
        body, out_shape=jax.ShapeDtypeStruct((N_DEV, m, n), block.dtype), in_specs=[_ANY], out_specs=_ANY,
        scratch_shapes=_EXCHANGE_SEMS, name=name)(block)


def _exchange_d2d(parts, *, name):
    _, rows, w = parts.shape

    def body(p_ref, land_ref, send_sems, recv_sems):
        x, y, c = _mesh_pos()
        sends = []
        for k in range(4):
            cp = pltpu.make_async_remote_copy(
                src_ref=p_ref.at[2 * k + (1 - c)], dst_ref=land_ref.at[k], send_sem=send_sems.at[k],
                recv_sem=recv_sems.at[k], device_id=(x, y, 1 - c), device_id_type=pl.DeviceIdType.MESH)
            cp.start()
            sends.append(cp)
        for cp in sends:
            cp.wait_recv()
        for cp in sends:
            cp.wait_send()

    return pl.pallas_call(
        body, out_shape=jax.ShapeDtypeStruct((4, rows, w), parts.dtype), in_specs=[_ANY], out_specs=_ANY,
        scratch_shapes=[pltpu.SemaphoreType.DMA((4,)), pltpu.SemaphoreType.DMA((4,))], name=name)(parts)


def _exchange_ici(parts, *, name):
    def body(p_ref, land_ref, send_sems, recv_sems, local_sem):
        x, y, c = _mesh_pos()
        mine = pltpu.make_async_copy(p_ref.at[2 * x + y], land_ref.at[3], local_sem)
        mine.start()
        sends = []
        for k, (px, py) in enumerate([(1 - x, y), (x, 1 - y), (1 - x, 1 - y)]):
            cp = pltpu.make_async_remote_copy(
                src_ref=p_ref.at[2 * px + py], dst_ref=land_ref.at[k], send_sem=send_sems.at[k],
                recv_sem=recv_sems.at[k], device_id=(px, py, c), device_id_type=pl.DeviceIdType.MESH)
            cp.start()
            sends.append(cp)
        for cp in sends:
            cp.wait_recv()
        for cp in sends:
            cp.wait_send()
        mine.wait()

    return pl.pallas_call(
        body, out_shape=jax.ShapeDtypeStruct(parts.shape, parts.dtype), in_specs=[_ANY], out_specs=_ANY,
        scratch_shapes=[pltpu.SemaphoreType.DMA((3,)), pltpu.SemaphoreType.DMA((3,)), pltpu.SemaphoreType.DMA(())],
        name=name)(parts)


def _rows(a):
    return a.reshape(-1, PACK_W)


def _pad_to(a, n):
    return jnp.pad(a, (0, n - a.shape[0]))


def _weight_blocks(d):
    small = _rows(_pad_to(jnp.concatenate([d[n].reshape(-1) for n, _ in _SMALL]), 8 * PACK_W))
    block_a = jnp.concatenate([d["w_in_a"][0].T.astype(WIRE), lax.bitcast_convert_type(small, WIRE).reshape(16, PACK_W)],
                              axis=0)
    w_uq = jnp.pad(d["w_uq"][0], ((0, 0), (0, 0), (0, HEAD_PAD - QK_NOPE - QK_ROPE)))
    pieces = {"w_out_a": d["w_out_a"], "w_dkv": d["w_dkv"], "w_uk": d["w_uk"], "w_uv": d["w_uv"],
              "w_in_b": d["w_in_b"][0].T, "w_uq": w_uq, "w_out_b": d["w_out_b"]}
    block_b = jnp.concatenate([_rows(pieces[n]) for n, _ in _PIECES_B]
                              + [jnp.zeros((WIRE_ROWS_B - MATRIX_ROWS_B, PACK_W), F32)], axis=0).astype(WIRE)
    return block_a, block_b


def _weights_a(wall):
    w = {}
    lo, hi = _OFF_A["w_in_a"]
    w["w_in_a_t"] = wall[:, lo:hi].reshape(2 * D_RNN, D_MODEL)
    small = lax.bitcast_convert_type(wall[:, MATRIX_ROWS_A:].reshape(N_DEV, 8 * PACK_W, 2), F32)
    off = dict(zip([n for n, _ in _SMALL], [0, 128, 768, 928, 1088, 1248]))
    w["norm_a"] = small[:, :128].reshape(1, D_MODEL)

    def by_channel(lo, rows):
        a = small[:, lo:lo + rows * (D_RNN // N_DEV)].reshape(N_DEV, rows, -1).transpose(1, 0, 2).reshape(rows, D_RNN)
        return jnp.pad(a, ((0, 8 - rows), (0, 0)))

    w["conv_taps"] = by_channel(off["conv_w"], CONV_WIDTH)
    w["lru_vecs"] = by_channel(off["conv_b"], 4)
    return w


def _weights_b(wall):
    piece = {n: wall[:, lo:hi] for n, (lo, hi) in _OFF_B.items()}
    w = {"w_out_a": piece["w_out_a"].reshape(D_RNN, D_MODEL)}
    w_dkv = piece["w_dkv"].reshape(D_MODEL, KV_RANK + QK_ROPE)
    w["w_dkv_c"] = w_dkv[:, :KV_RANK]
    w["w_dkv_r"] = jnp.pad(w_dkv[:, KV_RANK:], ((0, 0), (0, LANES - QK_ROPE)))
    w["w_uk"] = piece["w_uk"].reshape(KV_RANK, N_HEADS * QK_NOPE)
    w["w_uv"] = piece["w_uv"].reshape(KV_RANK, N_HEADS * V_DIM)
    w["w_uk_t"], w["w_uv_t"] = w["w_uk"].T, w["w_uv"].T
    w["w_in_b_t"] = piece["w_in_b"].reshape(Q_RANK + N_HEADS * V_DIM, D_MODEL)
    w["w_uq"] = piece["w_uq"].reshape(Q_RANK, N_HEADS * HEAD_PAD)
    w["w_out_b"] = piece["w_out_b"].reshape(N_HEADS * V_DIM, D_MODEL)
    return w


def _pack_rep(d):
    flat = jnp.concatenate([d[n].reshape(-1) for n, _ in _REP])
    return _rows(_pad_to(flat, REP_ROWS * PACK_W))


def _unpack_rep(p, like):
    flat = p.reshape(-1)
    out, off = {}, 0
    for n, k in _REP:
        out[n] = flat[off:off + k].reshape(like[n].shape)
        off += k
    return out


def _by_owner(a):
    return a.reshape(N_DEV, -1, PACK_W)


def _grad_parts_b(g):
    tail = jnp.zeros((N_DEV, WIRE_ROWS_B - MATRIX_ROWS_B, PACK_W), F32)
    return jnp.concatenate([_by_owner(g[n]) for n, _ in _PIECES_B] + [tail], axis=1).astype(BF16)


def _grad_parts_a(g):
    small = jnp.concatenate([
        g["norm_a"].reshape(N_DEV, -1),
        g["conv_w"].reshape(CONV_WIDTH, N_DEV, -1).transpose(1, 0, 2).reshape(N_DEV, -1),
        g["conv_b"].reshape(N_DEV, -1), g["b_rg"].reshape(N_DEV, -1), g["b_ig"].reshape(N_DEV, -1),
        g["lru_lambda"].reshape(N_DEV, -1)], axis=1)
    small = jnp.pad(small, ((0, 0), (0, 8 * PACK_W - small.shape[1]))).reshape(N_DEV, 8, PACK_W)
    half = N_DEV // 2
    w_in_a = jnp.concatenate([h.reshape(half, -1, PACK_W) for h in g["w_in_a_t"]], axis=0)
    rep = _pack_rep(g).reshape(N_DEV, REP_SLICE, PACK_W)
    tail = jnp.zeros((N_DEV, GRAD_ROWS_A - MATRIX_ROWS_A - 8 - REP_SLICE, PACK_W), F32)
    return jnp.concatenate([w_in_a, small, rep, tail], axis=1)


def _own_grads(sum_a, sum_b):
    out = {}
    lo, hi = _OFF_A["w_in_a"]
    out["w_in_a"] = sum_a[lo:hi].T.reshape(1, D_MODEL, 2 * D_RNN // N_DEV)
    small = sum_a[MATRIX_ROWS_A:MATRIX_ROWS_A + 8].reshape(-1)
    shapes = {"norm_a": (1, D_MODEL // N_DEV), "conv_w": (1, CONV_WIDTH, D_RNN // N_DEV), "conv_b": (1, D_RNN // N_DEV),
              "b_rg": (1, D_RNN // N_DEV), "b_ig": (1, D_RNN // N_DEV), "lru_lambda": (1, D_RNN // N_DEV)}
    off = 0
    for n, k in _SMALL:
        out[n] = small[off:off + k].reshape(shapes[n])
        off += k
    piece = {n: sum_b[lo:hi] for n, (lo, hi) in _OFF_B.items()}
    out["w_out_a"] = piece["w_out_a"].reshape(1, D_RNN // N_DEV, D_MODEL)
    out["w_dkv"] = piece["w_dkv"].reshape(D_MODEL // N_DEV, KV_RANK + QK_ROPE)
    out["w_uk"] = piece["w_uk"].reshape(KV_RANK // N_DEV, N_HEADS, QK_NOPE)
    out["w_uv"] = piece["w_uv"].reshape(KV_RANK // N_DEV, N_HEADS, V_DIM)
    out["w_in_b"] = piece["w_in_b"].T.reshape(1, D_MODEL, (Q_RANK + N_HEADS * V_DIM) // N_DEV)
    out["w_uq"] = piece["w_uq"].reshape(1, Q_RANK // N_DEV, N_HEADS, HEAD_PAD)[..., :QK_NOPE + QK_ROPE]
    out["w_out_b"] = piece["w_out_b"].reshape(1, N_HEADS * V_DIM // N_DEV, D_MODEL)
    return out


def _step(x, target, w, rep, block_b, *, bsz, seq):
    t = bsz * seq
    cos, sin = _rope_tables(seq)
    g_a = w["norm_a"]
    g_kv = rep["norm_kv"].reshape(1, -1)
    g_kvn = rep["kv_norm"].reshape(1, -1)
    g_b = rep["norm_b"].reshape(1, -1)
    g_q = rep["q_norm"].reshape(1, -1)
    g_f = rep["final_norm"].reshape(1, -1)
    wrg = rep["w_rg"][0].astype(BF16)
    wig = rep["w_ig"][0].astype(BF16)
    cw8, vecs = w["conv_taps"], w["lru_vecs"]

    def seq3(a):
        return a.reshape(bsz, seq, a.shape[-1])

    def flat(a):
        return a.reshape(t, a.shape[-1])

    h0, xp, ga = _lru_proj_fwd(x, g_a, w["w_in_a_t"], name="lru_proj_fwd")
    xb, hs, y, wall_b = _lru_fwd(seq3(xp), seq3(ga), cw8, vecs, wrg, wig, block_b, name="lru_fwd")
    w = dict(w, **_weights_b(wall_b))
    x1 = _matmul(flat(y), w["w_out_a"], residual=x, name="out_a")
    hk, hq, ck, cqp, g2, ckv, cq, q, kn, v, kr, kn_t, v_t, kr_t = _mla_proj_fwd(
        x1, (g_kv, g_b, g_kvn, g_q), w, cos, sin, seq=seq, name="mla_proj_fwd")
    o, lse = _attn_fwd(q, kn, kr, v_t, bsz=bsz, seq=seq, name="attn_fwd")
    loss, dx2, y2, do, dg2, dgf = _head_and_loss(o, g2, x1, target, w["w_out_b"], g_f, name="head_loss")
    grads = {"final_norm": dgf, "w_out_b": _matmul_tn(y2, dx2, name="d_w_out_b")}
    dq, dkn, dkr, dv = _attn_bwd(q, kn, kr, kn_t, kr_t, v, o, lse, do, cos, sin, bsz=bsz, seq=seq, name="attn_bwd")
    grads["w_uq"] = _matmul_tn(cq, dq, name="d_w_uq")
    dx1, du2, dckr, dgkv, dgb, dgkvn, dgq = _mla_proj_bwd(
        x1, dx2, cqp, ck, dq, dkn, dv, dkr, dg2, (g_kv, g_b, g_kvn, g_q), w, name="mla_proj_bwd")
    grads["norm_kv"], grads["norm_b"], grads["kv_norm"], grads["q_norm"] = dgkv, dgb, dgkvn, dgq
    grads["w_in_b"] = _matmul_tn(du2, hq, name="d_w_in_b_t")
    grads["w_uk"] = _matmul_tn(ckv, dkn, name="d_w_uk")
    grads["w_uv"] = _matmul_tn(ckv, dv, name="d_w_uv")
    grads["w_dkv"] = _matmul_tn(hk, dckr, name="d_w_dkv")[:, :KV_RANK + QK_ROPE]
    grads["w_out_a"] = _matmul_tn(flat(y), dx1, name="d_w_out_a")
    parts_b = _grad_parts_b(grads)
    dy = _matmul(dx1, w["w_out_a"], nt=True, name="d_y")
    dxp, dga, dwrg, dwig, dvec, landed_b = _lru_bwd(
        seq3(dy), seq3(xp), xb, hs, seq3(ga), cw8, vecs, wrg, wig, parts_b, name="lru_bwd")
    dxp, dga = flat(dxp), flat(dga)
    grads["w_rg"], grads["w_ig"] = dwrg, dwig
    grads["b_rg"], grads["b_ig"], grads["conv_b"] = dvec[0], dvec[1], dvec[3]
    lam = vecs[3]
    grads["lru_lambda"] = dvec[2] * (-1.0 / (1.0 + jnp.exp(lam)))
    grads["conv_w"] = dvec[4:4 + CONV_WIDTH]
    grads["w_in_a_t"] = (_matmul_tn(dxp, h0, name="d_w_in_a_x_t"), _matmul_tn(dga, h0, name="d_w_in_a_g_t"))
    dx, dga_norm = _lru_proj_bwd(dxp, dga, x, dx1, g_a, w["w_in_a_t"], name="lru_proj_bwd")
    grads["norm_a"] = dga_norm
    return loss[0, 0], dx, grads, landed_b


def kernel(x, norm_a, w_in_a, conv_w, conv_b, w_rg, b_rg, w_ig, b_ig, lru_lambda, w_out_a, norm_kv, w_dkv, kv_norm, w_uk, w_uv, norm_b, w_in_b, q_norm, w_uq, w_out_b, final_norm, loss_target, m_norm_a, m_w_in_a, m_conv_w, m_conv_b, m_w_rg, m_b_rg, m_w_ig, m_b_ig, m_lru_lambda, m_w_out_a, m_norm_kv, m_w_dkv, m_kv_norm, m_w_uk, m_w_uv, m_norm_b, m_w_in_b, m_q_norm, m_w_uq, m_w_out_b, m_final_norm, v_norm_a, v_w_in_a, v_conv_w, v_conv_b, v_w_rg, v_b_rg, v_w_ig, v_b_ig, v_lru_lambda, v_w_out_a, v_norm_kv, v_w_dkv, v_kv_norm, v_w_uk, v_w_uv, v_norm_b, v_w_in_b, v_q_norm, v_w_uq, v_w_out_b, v_final_norm):
    given = dict(locals())
    wts = {n: given[n] for n in WEIGHTS}
    mom1 = {n: given["m_" + n] for n in WEIGHTS}
    mom2 = {n: given["v_" + n] for n in WEIGHTS}
    bsz, seq, _ = x.shape
    t = bsz * seq

    block_a, block_b = _weight_blocks(wts)
    w = _weights_a(_all_gather(block_a, name="gather_weights_a"))
    loss, dx, grads, landed_b = _step(x.reshape(t, D_MODEL), loss_target.reshape(t, D_MODEL), w, wts, block_b,
                                      bsz=bsz, seq=seq)
    loss = lax.psum(loss, MESH_AXES)

    parts_a = _grad_parts_a(grads)
    from_sibling = _exchange_d2d(parts_a, name="exchange_grads_d2d")
    chip_parts = _chip_partial(parts_a, from_sibling, name="chip_partial_grads")
    landed_a = _exchange_ici(chip_parts, name="exchange_grads_ici")
    sum_a = _sum_parts(landed_a, name="sum_grads_a", br=GRAD_BLOCK)
    sum_b = _sum_parts(landed_b, name="sum_grads_b", br=WIRE_ROWS_B // 2)
    g_own = _own_grads(sum_a, sum_b)
    rep_slice = sum_a[MATRIX_ROWS_A + 8:MATRIX_ROWS_A + 8 + REP_SLICE]
    g_rep = _all_gather(rep_slice, name="gather_replicated").reshape(REP_ROWS, PACK_W)
    g_own.update(_unpack_rep(g_rep, wts))

    deltas, new_m, new_v = {}, {}, {}
    for n in WEIGHTS:
        deltas[n], new_m[n], new_v[n] = _adamw(g_own[n], wts[n], mom1[n], mom2[n], name="adamw_" + n)
    result = [loss, dx.reshape(bsz, seq, D_MODEL)]
    for d in (g_own, deltas, new_m, new_v):
        result.extend(d[n] for n in WEIGHTS)
    return tuple(result)
```

```python
import jax
import jax.numpy as jnp
from jax import lax
from jax.experimental import pallas as pl
from jax.experimental.pallas import tpu as pltpu

F32 = jnp.float32
BF16 = jnp.bfloat16
WIRE = jnp.bfloat16

D_MODEL = 1024
D_RNN = 1280
RNN_BLOCKS = 10
RNN_BW = 128
CONV_WIDTH = 4
LRU_C = 8.0
N_HEADS = 8
QK_NOPE = 128
QK_ROPE = 64
V_DIM = 128
KV_RANK = 256
Q_RANK = 384
ROPE_THETA = 10000.0
EPS = 1e-6
ATTN_SCALE = (QK_NOPE + QK_ROPE) ** -0.5
HEAD_PAD = 256
LANES = 128

ADAM_LR = 0.001
ADAM_B1 = 0.9
ADAM_B2 = 0.999
ADAM_EPS = 1e-08
ADAM_WD = 0.01
ADAM_STEP = 10

N_DEV = 8
MESH_AXES = ("x", "y", "c")
VMEM_LIMIT_BYTES = 56 * 2**20
PACK_W = 1024

_PIECES_A = (("w_in_a", 320),)
_PIECES_B = (("w_out_a", 160), ("w_dkv", 40), ("w_uk", 32), ("w_uv", 32), ("w_in_b", 176), ("w_uq", 96))


def _offsets(pieces):
    off, r = {}, 0
    for n, k in pieces:
        off[n] = (r, r + k)
        r += k
    return off, r


_OFF_A, MATRIX_ROWS_A = _offsets(_PIECES_A)
_OFF_B, MATRIX_ROWS_B = _offsets(_PIECES_B)
WIRE_ROWS_A = MATRIX_ROWS_A + 16
WIRE_ROWS_B = 544
_SMALL = (("norm_a", 128), ("conv_w", 640), ("conv_b", 160), ("b_rg", 160), ("b_ig", 160), ("lru_lambda", 160))
_REP = (("w_rg", 163840), ("w_ig", 163840), ("norm_kv", 1024), ("kv_norm", 256), ("norm_b", 1024),
        ("q_norm", 384), ("final_norm", 1024))
REP_ROWS = 384
REP_SLICE = REP_ROWS // N_DEV
GRAD_ROWS_A = 384
GRAD_BLOCK = 192

WEIGHTS = ("norm_a", "w_in_a", "conv_w", "conv_b", "w_rg", "b_rg", "w_ig", "b_ig", "lru_lambda", "w_out_a",
           "norm_kv", "w_dkv", "kv_norm", "w_uk", "w_uv", "norm_b", "w_in_b", "q_norm", "w_uq", "w_out_b",
           "final_norm")


def _params(sem=None):
    return pltpu.CompilerParams(dimension_semantics=sem, vmem_limit_bytes=VMEM_LIMIT_BYTES)


_NT = (((1,), (1,)), ((), ()))
_ANY = pl.BlockSpec(memory_space=pl.ANY)


def _mesh_pos():
    return lax.axis_index("x"), lax.axis_index("y"), lax.axis_index("c")


def _sigmoid(z):
    return 0.5 * jnp.tanh(0.5 * z) + 0.5


def _sigmoid_tail(z):
    return 1.0 / (1.0 + jnp.exp(-z))


def _col_block(n):
    return n if n <= 1408 else n // 2


def _matmul(a, b, *, name, nt=False, out_dtype=F32, residual=None, bm=512):
    m, k = a.shape
    n = b.shape[0] if nt else b.shape[1]
    bm = min(bm, m)
    bn = _col_block(n)
    dims = (((1,), (1,)), ((), ())) if nt else (((1,), (0,)), ((), ()))
    has_res = residual is not None

    def body(*refs):
        a_ref, b_ref, o_ref = refs[0], refs[1], refs[-1]
        acc = lax.dot_general(a_ref[...].astype(BF16), b_ref[...].astype(BF16), dims, preferred_element_type=F32)
        if has_res:
            acc = acc + refs[2][...]
        o_ref[...] = acc.astype(out_dtype)

    in_specs = [pl.BlockSpec((bm, k), lambda i, j: (i, 0)),
                pl.BlockSpec((bn, k), lambda i, j: (j, 0)) if nt else pl.BlockSpec((k, bn), lambda i, j: (0, j))]
    args = [a, b]
    if has_res:
        in_specs.append(pl.BlockSpec((bm, bn), lambda i, j: (i, j)))
        args.append(residual)
    return pl.pallas_call(
        body, grid=(m // bm, n // bn), in_specs=in_specs, out_specs=pl.BlockSpec((bm, bn), lambda i, j: (i, j)),
        out_shape=jax.ShapeDtypeStruct((m, n), out_dtype), compiler_params=_params(("parallel", "parallel")),
        name=name)(*args)


def _matmul_tn(a, b, *, name, bt=512):
    t, m = a.shape
    n = b.shape[1]
    bt = min(bt, t)
    bm, bn = _col_block(m), _col_block(n)

    def body(a_ref, b_ref, o_ref):
        @pl.when(pl.program_id(2) == 0)
        def _():
            o_ref[...] = jnp.zeros_like(o_ref)

        o_ref[...] += lax.dot_general(a_ref[...].astype(BF16), b_ref[...].astype(BF16),
                                      (((0,), (0,)), ((), ())), preferred_element_type=F32)

    return pl.pallas_call(
        body, grid=(m // bm, n // bn, t // bt),
        in_specs=[pl.BlockSpec((bt, bm), lambda i, j, s: (s, i)), pl.BlockSpec((bt, bn), lambda i, j, s: (s, j))],
        out_specs=pl.BlockSpec((bm, bn), lambda i, j, s: (i, j)),
        out_shape=jax.ShapeDtypeStruct((m, n), F32),
        compiler_params=_params(("parallel", "parallel", "arbitrary")), name=name)(a, b)


def _swap_halves(v):
    ax = v.ndim - 1
    lane = lax.broadcasted_iota(jnp.int32, v.shape, ax)
    up = pltpu.roll(v, LANES - QK_ROPE // 2, axis=ax)
    down = pltpu.roll(v, QK_ROPE // 2, axis=ax)
    return jnp.where(lane < QK_ROPE // 2, up, jnp.where(lane < QK_ROPE, down, 0.0))


def _rope(v, cos, sin):
    return v * cos + _swap_halves(v) * sin


def _rope_t(d, cos, sin):
    return d * cos + _swap_halves(d * sin)


def _rope_tables(seq):
    pos = jnp.arange(seq, dtype=F32)
    inv = ROPE_THETA ** (-jnp.arange(0, QK_ROPE, 2, dtype=F32) / QK_ROPE)
    ang = pos[:, None] * inv[None, :]
    cos, sin = jnp.cos(ang), jnp.sin(ang)
    zero = jnp.zeros((seq, LANES - QK_ROPE), F32)
    return jnp.concatenate([cos, cos, zero], axis=1), jnp.concatenate([-sin, sin, zero], axis=1)


def _rms(v):
    return v * lax.rsqrt(jnp.mean(v * v, axis=-1, keepdims=True) + EPS)


def _const_spec(a):
    return pl.BlockSpec(a.shape, lambda i: (0,) * a.ndim)


def _lru_proj_fwd(x, g_a, w_in_t, *, name, bt=256):
    t, d = x.shape
    bt = min(bt, t)
    n = w_in_t.shape[0] // 2

    def body(x_ref, g_ref, wt_ref, h_ref, xp_ref, ga_ref):
        h = (_rms(x_ref[...]) * g_ref[...]).astype(BF16)
        h_ref[...] = h
        xp_ref[...] = lax.dot_general(h, wt_ref[0:n, :], _NT, preferred_element_type=F32)
        ga_ref[...] = lax.dot_general(h, wt_ref[n:2 * n, :], _NT, preferred_element_type=F32)

    row = lambda w: pl.BlockSpec((bt, w), lambda i: (i, 0))
    return pl.pallas_call(
        body, grid=(t // bt,), in_specs=[row(d), _const_spec(g_a), _const_spec(w_in_t)],
        out_specs=[row(d), row(n), row(n)],
        out_shape=[jax.ShapeDtypeStruct((t, d), BF16), jax.ShapeDtypeStruct((t, n), F32), jax.ShapeDtypeStruct((t, n), F32)],
        compiler_params=_params(("parallel",)), name=name)(x, g_a, w_in_t)


def _mla_proj_fwd(x1, gains, w, cos, sin, *, seq, name, bt=256):
    t, d = x1.shape
    bt = min(bt, seq)
    per_seq = seq // bt
    g_kv, g_b, g_kvn, g_q = gains
    consts = [g_kv, g_b, g_kvn, g_q, w["w_dkv_c"], w["w_dkv_r"], w["w_in_b_t"], w["w_uk"], w["w_uv"],
              w["w_uk_t"], w["w_uv_t"], w["w_uq"]]

    def body(x_ref, cos_ref, sin_ref, gkv_ref, gb_ref, gkvn_ref, gq_ref, wdc_ref, wdr_ref, wbt_ref,
             wuk_ref, wuv_ref, wukt_ref, wuvt_ref, wuq_ref,
             hk_ref, hq_ref, ck_ref, cqp_ref, g2_ref, ckv_ref, cq_ref, q_ref, kn_ref, v_ref, kr_ref, knt_ref, vt_ref, krt_ref):
        nrm = _rms(x_ref[...])
        hk = (nrm * gkv_ref[...]).astype(BF16)
        hq = (nrm * gb_ref[...]).astype(BF16)
        hk_ref[...] = hk
        hq_ref[...] = hq
        ck = jnp.dot(hk, wdc_ref[...], preferred_element_type=F32)
        ck_ref[...] = ck
        cqp = lax.dot_general(hq, wbt_ref[0:Q_RANK, :], _NT, preferred_element_type=F32)
        cqp_ref[...] = cqp
        g2_ref[...] = lax.dot_general(hq, wbt_ref[Q_RANK:, :], _NT, preferred_element_type=F32)
        cosv, sinv = cos_ref[...], sin_ref[...]
        kr = _rope(jnp.dot(hk, wdr_ref[...], preferred_element_type=F32), cosv, sinv)
        kr_ref[...] = kr.astype(BF16)
        krt_ref[...] = kr.T.astype(BF16)
        ckv = (_rms(ck) * gkvn_ref[...]).astype(BF16)
        ckv_ref[...] = ckv
        kn_ref[...] = jnp.dot(ckv, wuk_ref[...], preferred_element_type=F32).astype(BF16)
        v_ref[...] = jnp.dot(ckv, wuv_ref[...], preferred_element_type=F32).astype(BF16)
        knt_ref[...] = lax.dot_general(wukt_ref[...], ckv, _NT, preferred_element_type=F32).astype(BF16)
        vt_ref[...] = lax.dot_general(wuvt_ref[...], ckv, _NT, preferred_element_type=F32).astype(BF16)
        cq = (_rms(cqp) * gq_ref[...]).astype(BF16)
        cq_ref[...] = cq
        for h in range(N_HEADS):
            qh = jnp.dot(cq, wuq_ref[:, h * HEAD_PAD:(h + 1) * HEAD_PAD], preferred_element_type=F32)
            q_ref[:, h * HEAD_PAD:h * HEAD_PAD + QK_NOPE] = qh[:, :QK_NOPE].astype(BF16)
            q_ref[:, h * HEAD_PAD + QK_NOPE:(h + 1) * HEAD_PAD] = _rope(qh[:, QK_NOPE:], cosv, sinv).astype(BF16)

    row = lambda w_: pl.BlockSpec((bt, w_), lambda i: (i, 0))
    col = lambda h_: pl.BlockSpec((h_, bt), lambda i: (0, i))
    tab = pl.BlockSpec((bt, LANES), lambda i: (i % per_seq, 0))
    nh = N_HEADS * V_DIM
    shapes = [((t, d), BF16), ((t, d), BF16), ((t, KV_RANK), F32), ((t, Q_RANK), F32), ((t, nh), F32), ((t, KV_RANK), BF16),
              ((t, Q_RANK), BF16), ((t, N_HEADS * HEAD_PAD), BF16), ((t, nh), BF16), ((t, nh), BF16), ((t, LANES), BF16),
              ((nh, t), BF16), ((nh, t), BF16), ((LANES, t), BF16)]
    out_specs = [row(d), row(d), row(KV_RANK), row(Q_RANK), row(nh), row(KV_RANK), row(Q_RANK), row(N_HEADS * HEAD_PAD),
                 row(nh), row(nh), row(LANES), col(nh), col(nh), col(LANES)]
    return pl.pallas_call(
        body, grid=(t // bt,), in_specs=[row(d), tab, tab] + [_const_spec(a) for a in consts], out_specs=out_specs,
        out_shape=[jax.ShapeDtypeStruct(s, dt) for s, dt in shapes],
        compiler_params=_params(("parallel",)), name=name)(x1, cos, sin, *consts)


def _rms_bwd_rows(xv, dn):
    r = lax.rsqrt(jnp.mean(xv * xv, axis=-1, keepdims=True) + EPS)
    nrm = xv * r
    return r * (dn - nrm * jnp.mean(dn * nrm, axis=-1, keepdims=True)), nrm


def _col_sum(v):
    return jnp.sum(v, axis=0, keepdims=True)


def _lru_proj_bwd(dxp, dga, x, dx1, g_a, w_in_t, *, name, bt=256):
    t, d = x.shape
    bt = min(bt, t)
    n = w_in_t.shape[0] // 2

    def body(dxp_ref, dga_ref, x_ref, dx1_ref, g_ref, wt_ref, dx_ref, dg_ref):
        @pl.when(pl.program_id(0) == 0)
        def _():
            dg_ref[...] = jnp.zeros_like(dg_ref)

        dh = (jnp.dot(dxp_ref[...], wt_ref[0:n, :], preferred_element_type=F32)
              + jnp.dot(dga_ref[...], wt_ref[n:2 * n, :], preferred_element_type=F32))
        dxn, nrm = _rms_bwd_rows(x_ref[...], dh * g_ref[...])
        dg_ref[...] += _col_sum(dh * nrm)
        dx_ref[...] = dx1_ref[...] + dxn

    row = lambda w: pl.BlockSpec((bt, w), lambda i: (i, 0))
    return pl.pallas_call(
        body, grid=(t // bt,),
        in_specs=[row(n), row(n), row(d), row(d), _const_spec(g_a), _const_spec(w_in_t)],
        out_specs=[row(d), _const_spec(g_a)],
        out_shape=[jax.ShapeDtypeStruct((t, d), F32), jax.ShapeDtypeStruct((1, d), F32)],
        compiler_params=_params(("arbitrary",)), name=name)(dxp, dga, x, dx1, g_a, w_in_t)


def _mla_proj_bwd(x1, dx2, cqp, ck, dq, dkn, dv, dkr, dg2, gains, w, *, name, bt=256):
    t, d = x1.shape
    bt = min(bt, t)
    g_kv, g_b, g_kvn, g_q = gains
    consts = [g_kv, g_b, g_kvn, g_q, w["w_dkv_c"], w["w_dkv_r"], w["w_in_b_t"], w["w_uk"], w["w_uv"], w["w_uq"]]
    nh = N_HEADS * V_DIM

    def body(x1_ref, dx2_ref, cqp_ref, ck_ref, dq_ref, dkn_ref, dv_ref, dkr_ref, dg2_ref,
             gkv_ref, gb_ref, gkvn_ref, gq_ref, wdc_ref, wdr_ref, wbt_ref, wuk_ref, wuv_ref, wuq_ref,
             dx1_ref, du2_ref, dckr_ref, dgkv_ref, dgb_ref, dgkvn_ref, dgq_ref):
        @pl.when(pl.program_id(0) == 0)
        def _():
            for ref in (dgkv_ref, dgb_ref, dgkvn_ref, dgq_ref):
                ref[...] = jnp.zeros_like(ref)

        dot_nt = lambda a, b: lax.dot_general(a, b, _NT, preferred_element_type=F32)
        dcq = dot_nt(dq_ref[...], wuq_ref[...])
        dcqp, nq = _rms_bwd_rows(cqp_ref[...], dcq * gq_ref[...])
        dgq_ref[...] += _col_sum(dcq * nq)
        dcqp = dcqp.astype(BF16)
        dg2 = dg2_ref[...]
        du2_ref[:, :Q_RANK] = dcqp
        du2_ref[:, Q_RANK:] = dg2
        dhq = (jnp.dot(dcqp, wbt_ref[0:Q_RANK, :], preferred_element_type=F32)
               + jnp.dot(dg2, wbt_ref[Q_RANK:, :], preferred_element_type=F32))
        dckv = dot_nt(dkn_ref[...], wuk_ref[...]) + dot_nt(dv_ref[...], wuv_ref[...])
        dck, nc = _rms_bwd_rows(ck_ref[...], dckv * gkvn_ref[...])
        dgkvn_ref[...] += _col_sum(dckv * nc)
        dck = dck.astype(BF16)
        dkr = dkr_ref[...].astype(BF16)
        dckr_ref[:, :KV_RANK] = dck
        dckr_ref[:, KV_RANK:] = dkr
        dhk = dot_nt(dck, wdc_ref[...]) + dot_nt(dkr, wdr_ref[...])
        dxn, n1 = _rms_bwd_rows(x1_ref[...], dhq * gb_ref[...] + dhk * gkv_ref[...])
        dgb_ref[...] += _col_sum(dhq * n1)
        dgkv_ref[...] += _col_sum(dhk * n1)
        dx1_ref[...] = dx2_ref[...] + dxn

    row = lambda w_: pl.BlockSpec((bt, w_), lambda i: (i, 0))
    vec = lambda w_: pl.BlockSpec((1, w_), lambda i: (0, 0))
    in_specs = [row(d), row(d), row(Q_RANK), row(KV_RANK), row(N_HEADS * HEAD_PAD), row(nh), row(nh), row(LANES), row(nh)]
    return pl.pallas_call(
        body, grid=(t // bt,), in_specs=in_specs + [_const_spec(a) for a in consts],
        out_specs=[row(d), row(Q_RANK + nh), row(KV_RANK + LANES), vec(d), vec(d), vec(KV_RANK), vec(Q_RANK)],
        out_shape=[jax.ShapeDtypeStruct((t, d), F32), jax.ShapeDtypeStruct((t, Q_RANK + nh), BF16),
                   jax.ShapeDtypeStruct((t, KV_RANK + LANES), BF16), jax.ShapeDtypeStruct((1, d), F32),
                   jax.ShapeDtypeStruct((1, d), F32), jax.ShapeDtypeStruct((1, KV_RANK), F32),
                   jax.ShapeDtypeStruct((1, Q_RANK), F32)],
        compiler_params=_params(("arbitrary",)), name=name)(x1, dx2, cqp, ck, dq, dkn, dv, dkr, dg2, *consts)


def _softplus(z):
    return jnp.maximum(z, 0.0) + jnp.log1p(jnp.exp(-jnp.abs(z)))


def _neg_expm1(z):
    series = -z * (1.0 + z * (1.0 / 2) * (1.0 + z * (1.0 / 3) * (1.0 + z * (1.0 / 4))))
    return jnp.where(z > -0.01, series, 1.0 - jnp.exp(z))


def _gates(xb, wrg, wig, brg, big, sp):
    xbb = xb.astype(BF16)
    r = _sigmoid_tail(jnp.dot(xbb, wrg, preferred_element_type=F32) + brg)
    i = _sigmoid(jnp.dot(xbb, wig, preferred_element_type=F32) + big)
    la = (-LRU_C) * r * sp
    a = jnp.exp(la)
    mult = jnp.sqrt(_neg_expm1(2.0 * la))
    return r, i, a, mult


def _conv(xpad_ref, cw_ref, seq):
    acc = cw_ref[0:1, :] * xpad_ref[pl.ds(8 - (CONV_WIDTH - 1), seq), :]
    for k in range(1, CONV_WIDTH):
        acc = acc + cw_ref[k:k + 1, :] * xpad_ref[pl.ds(8 - (CONV_WIDTH - 1) + k, seq), :]
    return acc


def _seq_spec(seq):
    return pl.BlockSpec((None, seq, RNN_BW), lambda n, b: (b, 0, n))


def _chan_spec(rows):
    return pl.BlockSpec((rows, RNN_BW), lambda n, b: (0, n))


_GATE_W_SPEC = pl.BlockSpec((None, RNN_BW, RNN_BW), lambda n, b: (n, 0, 0))


SCAN_UNROLL = 4


def _peers():
    x, y, c = _mesh_pos()
    others = []
    for k in range(1, N_DEV):
        px = 1 - x if k & 4 else x
        py = 1 - y if k & 2 else y
        pc = 1 - c if k & 1 else c
        others.append(((px, py, pc), 4 * px + 2 * py + pc))
    return 4 * x + 2 * y + c, others


def _exchange(src_ref, dst_ref, send_sems, recv_sems, local_sem, *, finish, gather=False):
    me, others = _peers()

    def send(k, dev, slot):
        return pltpu.make_async_remote_copy(
            src_ref=src_ref if gather else src_ref.at[slot], dst_ref=dst_ref.at[me], send_sem=send_sems.at[k],
            recv_sem=recv_sems.at[k], device_id=dev, device_id_type=pl.DeviceIdType.MESH)

    local = pltpu.make_async_copy(src_ref if gather else src_ref.at[me], dst_ref.at[me], local_sem)
    if not finish:
        local.start()
        for k, (dev, slot) in enumerate(others):
            send(k, dev, slot).start()
        return
    for k, (dev, slot) in enumerate(others):
        pltpu.make_async_remote_copy(
            src_ref=dst_ref.at[slot], dst_ref=dst_ref.at[slot], send_sem=send_sems.at[k], recv_sem=recv_sems.at[k],
            device_id=dev, device_id_type=pl.DeviceIdType.MESH).wait_recv()
    for k, (dev, slot) in enumerate(others):
        send(k, dev, slot).wait_send()
    local.wait()


def _gather_two_level(x_ref, out_ref, send_sems, recv_sems, local_sem, *, phase):
    x, y, c = _mesh_pos()
    me, sibling = (x, y, c), (x, y, 1 - c)
    chips = [(1 - x, y), (x, 1 - y), (1 - x, 1 - y)]

    def slot(px, py, pc):
        return out_ref.at[4 * px + 2 * py + pc]

    def copy(k, blk, to, src=None):
        return pltpu.make_async_remote_copy(
            src_ref=slot(*blk) if src is None else src, dst_ref=slot(*blk),
            send_sem=send_sems.at[k], recv_sem=recv_sems.at[k], device_id=to, device_id_type=pl.DeviceIdType.MESH)

    if phase == 0:
        pltpu.make_async_copy(x_ref, slot(*me), local_sem).start()
        copy(0, me, sibling, src=x_ref).start()
        for j, chip in enumerate(chips):
            copy(1 + j, me, (*chip, c), src=x_ref).start()
    elif phase == 1:
        for j, chip in enumerate(chips):
            copy(1 + j, (*chip, c), me).wait_recv()
            copy(4 + j, (*chip, c), sibling).start()
    else:
        copy(0, sibling, me).wait_recv()
        for j, chip in enumerate(chips):
            copy(4 + j, (*chip, 1 - c), me).wait_recv()
        copy(0, me, sibling, src=x_ref).wait_send()
        for j, chip in enumerate(chips):
            copy(1 + j, me, (*chip, c), src=x_ref).wait_send()
            copy(4 + j, (*chip, c), sibling).wait_send()
        pltpu.make_async_copy(x_ref, slot(*me), local_sem).wait()


GATHER_FORWARD_STEP = 7
_EXCHANGE_SEMS = [pltpu.SemaphoreType.DMA((N_DEV - 1,)), pltpu.SemaphoreType.DMA((N_DEV - 1,)), pltpu.SemaphoreType.DMA(())]


def _first_last(steps):
    first = last = None
    for axis, n in enumerate(steps):
        i = pl.program_id(axis)
        first = (i == 0) if first is None else first & (i == 0)
        last = (i == n - 1) if last is None else last & (i == n - 1)
    return first, last


def _lru_fwd(xp, ga, cw, vecs, wrg, wig, block, *, name):
    bsz, seq, _ = xp.shape
    groups = seq // 8

    def body(xp_ref, ga_ref, cw_ref, vec_ref, wrg_ref, wig_ref, blk_ref, xb_ref, hs_ref, y_ref, all_ref,
             xpad, a_s, b_s, send_sems, recv_sems, local_sem):
        first, last = _first_last((RNN_BLOCKS, bsz))

        @pl.when(first)
        def _():
            _gather_two_level(blk_ref, all_ref, send_sems, recv_sems, local_sem, phase=0)

        @pl.when((pl.program_id(0) == GATHER_FORWARD_STEP) & (pl.program_id(1) == 0))
        def _():
            _gather_two_level(blk_ref, all_ref, send_sems, recv_sems, local_sem, phase=1)

        xpad[0:8, :] = jnp.zeros((8, RNN_BW), F32)
        xpad[pl.ds(8, seq), :] = xp_ref[...]
        xb = _conv(xpad, cw_ref, seq) + vec_ref[0:1, :]
        xb_ref[...] = xb
        sp = _softplus(-vec_ref[3:4, :])
        _, i, a, mult = _gates(xb, wrg_ref[...], wig_ref[...], vec_ref[1:2, :], vec_ref[2:3, :], sp)
        a_s[...] = a
        b_s[...] = mult * (i * xb)
        row = lax.broadcasted_iota(jnp.int32, (8, RNN_BW), 0)

        def group(g, h):
            r0 = pl.multiple_of(g * 8, 8)
            av = a_s[pl.ds(r0, 8), :]
            bv = b_s[pl.ds(r0, 8), :]
            for k in (1, 2, 4):
                m = row >= k
                bv = jnp.where(m, av * pltpu.roll(bv, k, axis=0) + bv, bv)
                av = jnp.where(m, av * pltpu.roll(av, k, axis=0), av)
            hs_ref[pl.ds(r0, 8), :] = av * h + bv
            return av[7:8, :] * h + bv[7:8, :]

        def groups_of(i, h):
            for u in range(SCAN_UNROLL):
                h = group(i * SCAN_UNROLL + u, h)
            return h

        lax.fori_loop(0, groups // SCAN_UNROLL, groups_of, jnp.zeros((1, RNN_BW), F32))
        gav = ga_ref[...]
        y_ref[...] = (hs_ref[...] * (gav * _sigmoid(gav))).astype(BF16)

        @pl.when(last)
        def _():
            _gather_two_level(blk_ref, all_ref, send_sems, recv_sems, local_sem, phase=2)

    sq = _seq_spec(seq)
    shape = (bsz, seq, D_RNN)
    return pl.pallas_call(
        body, grid=(RNN_BLOCKS, bsz),
        in_specs=[sq, sq, _chan_spec(8), _chan_spec(8), _GATE_W_SPEC, _GATE_W_SPEC, _ANY],
        out_specs=[sq, sq, sq, _ANY],
        out_shape=[jax.ShapeDtypeStruct(shape, F32), jax.ShapeDtypeStruct(shape, F32), jax.ShapeDtypeStruct(shape, BF16),
                   jax.ShapeDtypeStruct((N_DEV,) + block.shape, block.dtype)],
        scratch_shapes=[pltpu.VMEM((seq + 8, RNN_BW), F32), pltpu.VMEM((seq, RNN_BW), F32), pltpu.VMEM((seq, RNN_BW), F32)]
        + _EXCHANGE_SEMS,
        compiler_params=_params(("arbitrary", "arbitrary")), name=name)(xp, ga, cw, vecs, wrg, wig, block)


def _lru_bwd(dy, xp, xb, hs, ga, cw, vecs, wrg, wig, parts, *, name):
    bsz, seq, _ = xp.shape
    groups = seq // 8

    def body(dy_ref, xp_ref, xb_ref, hs_ref, ga_ref, cw_ref, vec_ref, wrg_ref, wig_ref,
             parts_ref, dxp_ref, dga_ref, dwrg_ref, dwig_ref, dvec_ref, land_ref, pad, a_s, d_s, lam_s,
             send_sems, recv_sems, local_sem):
        first, last = _first_last((RNN_BLOCKS, bsz))

        @pl.when(first)
        def _():
            _exchange(parts_ref, land_ref, send_sems, recv_sems, local_sem, finish=False)

        @pl.when(pl.program_id(1) == 0)
        def _():
            dwrg_ref[...] = jnp.zeros_like(dwrg_ref)
            dwig_ref[...] = jnp.zeros_like(dwig_ref)
            dvec_ref[...] = jnp.zeros_like(dvec_ref)

        xb = xb_ref[...]
        hs = hs_ref[...]
        gav = ga_ref[...]
        dy = dy_ref[...]
        sp = _softplus(-vec_ref[3:4, :])
        wrg = wrg_ref[...]
        wig = wig_ref[...]
        r, i, a, mult = _gates(xb, wrg, wig, vec_ref[1:2, :], vec_ref[2:3, :], sp)
        sg = _sigmoid(gav)
        dga_ref[...] = (dy * hs * (sg * (1.0 + gav * (1.0 - sg)))).astype(BF16)
        d_s[...] = dy * (gav * sg)

        pad[pl.ds(0, seq), :] = a
        pad[pl.ds(seq, 8), :] = jnp.zeros((8, RNN_BW), F32)
        a_s[...] = pad[pl.ds(1, seq), :]
        row = lax.broadcasted_iota(jnp.int32, (8, RNN_BW), 0)

        def group(g, nxt):
            r0 = pl.multiple_of((groups - 1 - g) * 8, 8)
            cv = a_s[pl.ds(r0, 8), :]
            bv = d_s[pl.ds(r0, 8), :]
            for k in (1, 2, 4):
                m = row < 8 - k
                bv = jnp.where(m, cv * pltpu.roll(bv, 8 - k, axis=0) + bv, bv)
                cv = jnp.where(m, cv * pltpu.roll(cv, 8 - k, axis=0), cv)
            lam_s[pl.ds(r0, 8), :] = cv * nxt + bv
            return cv[0:1, :] * nxt + bv[0:1, :]

        def groups_of(i, nxt):
            for u in range(SCAN_UNROLL):
                nxt = group(i * SCAN_UNROLL + u, nxt)
            return nxt

        lax.fori_loop(0, groups // SCAN_UNROLL, groups_of, jnp.zeros((1, RNN_BW), F32))
        dh = lam_s[...]

        pad[0:8, :] = jnp.zeros((8, RNN_BW), F32)
        pad[pl.ds(8, seq), :] = hs
        da = dh * pad[pl.ds(7, seq), :]
        ixb = i * xb
        dixb = dh * mult
        dla = da * a - (dh * ixb) * (a * a) / mult
        drp = (dla * ((-LRU_C) * sp)) * r * (1.0 - r)
        dip = (dixb * xb) * i * (1.0 - i)
        dvec_ref[0:1, :] += jnp.sum(drp, axis=0, keepdims=True)
        dvec_ref[1:2, :] += jnp.sum(dip, axis=0, keepdims=True)
        dvec_ref[2:3, :] += jnp.sum(dla * ((-LRU_C) * r), axis=0, keepdims=True)
        drpb = drp.astype(BF16)
        dipb = dip.astype(BF16)
        xbb = xb.astype(BF16)
        nt = (((1,), (1,)), ((), ()))
        tn = (((0,), (0,)), ((), ()))
        dxb = (dixb * i
               + lax.dot_general(drpb, wrg, nt, preferred_element_type=F32)
               + lax.dot_general(dipb, wig, nt, preferred_element_type=F32))
        dwrg_ref[...] += lax.dot_general(xbb, drpb, tn, preferred_element_type=F32)
        dwig_ref[...] += lax.dot_general(xbb, dipb, tn, preferred_element_type=F32)
        dvec_ref[3:4, :] += jnp.sum(dxb, axis=0, keepdims=True)

        pad[pl.ds(0, seq), :] = dxb
        pad[pl.ds(seq, 8), :] = jnp.zeros((8, RNN_BW), F32)
        dxp = cw_ref[0:1, :] * pad[pl.ds(CONV_WIDTH - 1, seq), :]
        for k in range(1, CONV_WIDTH):
            dxp = dxp + cw_ref[k:k + 1, :] * pad[pl.ds(CONV_WIDTH - 1 - k, seq), :]
        dxp_ref[...] = dxp.astype(BF16)
        pad[0:8, :] = jnp.zeros((8, RNN_BW), F32)
        pad[pl.ds(8, seq), :] = xp_ref[...]
        for k in range(CONV_WIDTH):
            dvec_ref[4 + k:5 + k, :] += jnp.sum(dxb * pad[pl.ds(8 - (CONV_WIDTH - 1) + k, seq), :], axis=0, keepdims=True)

        @pl.when(last)
        def _():
            _exchange(parts_ref, land_ref, send_sems, recv_sems, local_sem, finish=True)

    sq = _seq_spec(seq)
    shape = (bsz, seq, D_RNN)
    gshape = (RNN_BLOCKS, RNN_BW, RNN_BW)
    return pl.pallas_call(
        body, grid=(RNN_BLOCKS, bsz),
        in_specs=[sq, sq, sq, sq, sq, _chan_spec(8), _chan_spec(8), _GATE_W_SPEC, _GATE_W_SPEC, _ANY],
        out_specs=[sq, sq, _GATE_W_SPEC, _GATE_W_SPEC, _chan_spec(8), _ANY],
        out_shape=[jax.ShapeDtypeStruct(shape, BF16), jax.ShapeDtypeStruct(shape, BF16),
                   jax.ShapeDtypeStruct(gshape, F32), jax.ShapeDtypeStruct(gshape, F32),
                   jax.ShapeDtypeStruct((8, D_RNN), F32), jax.ShapeDtypeStruct(parts.shape, parts.dtype)],
        scratch_shapes=[pltpu.VMEM((seq + 8, RNN_BW), F32), pltpu.VMEM((seq, RNN_BW), F32),
                        pltpu.VMEM((seq, RNN_BW), F32), pltpu.VMEM((seq, RNN_BW), F32)] + _EXCHANGE_SEMS,
        compiler_params=_params(("arbitrary", "arbitrary")), name=name)(dy, xp, xb, hs, ga, cw, vecs, wrg, wig, parts)


def _attn_block(seq):
    return min(512, seq)


def _diag_mask(blk):
    return lax.broadcasted_iota(jnp.int32, (blk, blk), 0) <= lax.broadcasted_iota(jnp.int32, (blk, blk), 1)


FWD_HEADS = 4
BWD_HEADS = 2


def _attn_fwd(q, kn, kr, v_t, block, *, bsz, seq, name):
    t = bsz * seq
    blk = _attn_block(seq)
    nq = seq // blk
    hg = FWD_HEADS
    steps = (bsz, N_HEADS // hg, nq)

    def body(q_ref, kn_ref, kr_ref, vt_ref, blk_ref, o_ref, lse_ref, all_ref, acc, send_sems, recv_sems, local_sem):
        first, last = _first_last(steps)

        @pl.when(first)
        def _():
            _exchange(blk_ref, all_ref, send_sems, recv_sems, local_sem, finish=False, gather=True)

        qi = pl.program_id(2)
        acc[...] = jnp.zeros_like(acc)

        def step(j, carry, diagonal):
            k0 = pl.multiple_of(j * blk, blk)
            kr_j = kr_ref[pl.ds(k0, blk), :]
            out = []
            for h in range(hg):
                m_i, l_i = carry[h]
                kv = jnp.concatenate([kn_ref[pl.ds(k0, blk), h * QK_NOPE:(h + 1) * QK_NOPE], kr_j], axis=1)
                qv = q_ref[:, h * HEAD_PAD:(h + 1) * HEAD_PAD]
                s = lax.dot_general(kv, qv, _NT, preferred_element_type=F32) * ATTN_SCALE
                if diagonal:
                    s = jnp.where(_diag_mask(blk), s, -jnp.inf)
                m_new = jnp.maximum(m_i, jnp.max(s, axis=0, keepdims=True))
                p = jnp.exp(s - m_new)
                alpha = jnp.exp(m_i - m_new)
                l_new = alpha * l_i + jnp.sum(p, axis=0, keepdims=True)
                acc[h] = alpha * acc[h] + jnp.dot(vt_ref[h * V_DIM:(h + 1) * V_DIM, pl.ds(k0, blk)], p.astype(BF16),
                                                  preferred_element_type=F32)
                out.append((m_new, l_new))
            return tuple(out)

        init = tuple((jnp.full((1, blk), -jnp.inf, F32), jnp.zeros((1, blk), F32)) for _ in range(hg))
        carry = lax.fori_loop(0, qi, lambda j, c: step(j, c, False), init)
        stats = step(qi, carry, True)
        for h in range(hg):
            m_i, l_i = stats[h]
            o_ref[:, h * V_DIM:(h + 1) * V_DIM] = (acc[h] / l_i).T
            lse_ref[h] = m_i + jnp.log(l_i)

        @pl.when(last)
        def _():
            _exchange(blk_ref, all_ref, send_sems, recv_sems, local_sem, finish=True, gather=True)

    return pl.pallas_call(
        body, grid=steps,
        in_specs=[pl.BlockSpec((blk, hg * HEAD_PAD), lambda b, g, i: (b * nq + i, g)),
                  pl.BlockSpec((seq, hg * QK_NOPE), lambda b, g, i: (b, g)),
                  pl.BlockSpec((seq, LANES), lambda b, g, i: (b, 0)),
                  pl.BlockSpec((hg * V_DIM, seq), lambda b, g, i: (g, b)), _ANY],
        out_specs=[pl.BlockSpec((blk, hg * V_DIM), lambda b, g, i: (b * nq + i, g)),
                   pl.BlockSpec((hg, 1, blk), lambda b, g, i: (g, 0, b * nq + i)), _ANY],
        out_shape=[jax.ShapeDtypeStruct((t, N_HEADS * V_DIM), F32), jax.ShapeDtypeStruct((N_HEADS, 1, t), F32),
                   jax.ShapeDtypeStruct((N_DEV,) + block.shape, block.dtype)],
        scratch_shapes=[pltpu.VMEM((hg, V_DIM, blk), F32)] + _EXCHANGE_SEMS,
        compiler_params=_params(("arbitrary", "arbitrary", "arbitrary")), name=name)(q, kn, kr, v_t, block)


def _attn_bwd(q, kn, kr, kn_t, kr_t, v, o, lse, do, cos, sin, parts, *, bsz, seq, name):
    t = bsz * seq
    blk = _attn_block(seq)
    nq = seq // blk
    hg = BWD_HEADS
    steps = (bsz, N_HEADS // hg)

    def body(q_ref, kn_ref, kr_ref, knt_ref, krt_ref, v_ref, o_ref, lse_ref, do_ref, cos_ref, sin_ref, parts_ref,
             dq_ref, dkn_ref, dkr_ref, dv_ref, land_ref, dqt_acc, dk_acc, dv_acc, send_sems, recv_sems, local_sem):
        first, last = _first_last(steps)

        @pl.when(first)
        def _():
            _exchange(parts_ref, land_ref, send_sems, recv_sems, local_sem, finish=False)

        dqt_acc[...] = jnp.zeros_like(dqt_acc)
        dk_acc[...] = jnp.zeros_like(dk_acc)
        dv_acc[...] = jnp.zeros_like(dv_acc)

        def q_block(i, _):
            q0 = pl.multiple_of(i * blk, blk)
            rows = []
            for h in range(hg):
                dov = do_ref[pl.ds(q0, blk), h * V_DIM:(h + 1) * V_DIM].astype(F32)
                dcol = jnp.sum(dov * o_ref[pl.ds(q0, blk), h * V_DIM:(h + 1) * V_DIM], axis=-1, keepdims=True)
                delta = jnp.broadcast_to(dcol, (blk, LANES)).T[0:1, :]
                rows.append((lse_ref[h, :, pl.ds(q0, blk)], delta))

            def pair(j, diagonal):
                k0 = pl.multiple_of(j * blk, blk)
                kr_j = kr_ref[pl.ds(k0, blk), :]
                krt_j = krt_ref[:, pl.ds(k0, blk)]
                for h in range(hg):
                    lse_i, delta = rows[h]
                    qv = q_ref[pl.ds(q0, blk), h * HEAD_PAD:(h + 1) * HEAD_PAD]
                    dov = do_ref[pl.ds(q0, blk), h * V_DIM:(h + 1) * V_DIM]
                    kv = jnp.concatenate([kn_ref[pl.ds(k0, blk), h * QK_NOPE:(h + 1) * QK_NOPE], kr_j], axis=1)
                    s = lax.dot_general(kv, qv, _NT, preferred_element_type=F32) * ATTN_SCALE
                    p = jnp.exp(s - lse_i)
                    if diagonal:
                        p = jnp.where(_diag_mask(blk), p, 0.0)
                    dv_acc[pl.ds(k0, blk), h * V_DIM:(h + 1) * V_DIM] += jnp.dot(
                        p.astype(BF16), dov, preferred_element_type=F32)
                    dp = lax.dot_general(v_ref[pl.ds(k0, blk), h * V_DIM:(h + 1) * V_DIM], dov, _NT,
                                         preferred_element_type=F32)
                    ds = (p * (dp - delta) * ATTN_SCALE).astype(BF16)
                    dk_acc[pl.ds(k0, blk), h * HEAD_PAD:(h + 1) * HEAD_PAD] += jnp.dot(ds, qv, preferred_element_type=F32)
                    base = h * HEAD_PAD
                    dqt_acc[base:base + QK_NOPE, pl.ds(q0, blk)] += jnp.dot(
                        knt_ref[h * QK_NOPE:(h + 1) * QK_NOPE, pl.ds(k0, blk)], ds, preferred_element_type=F32)
                    dqt_acc[base + QK_NOPE:base + HEAD_PAD, pl.ds(q0, blk)] += jnp.dot(
                        krt_j, ds, preferred_element_type=F32)

            def off_diagonal(j, _):
                pair(j, False)
                return 0

            lax.fori_loop(0, i, off_diagonal, 0)
            pair(i, True)
            return 0

        lax.fori_loop(0, nq, q_block, 0)
        dkr = jnp.zeros((seq, LANES), F32)
        for h in range(hg):
            base = h * HEAD_PAD
            for i in range(nq):
                rows = slice(i * blk, (i + 1) * blk)
                dq = dqt_acc[base:base + HEAD_PAD, rows].T
                dq_ref[rows, base:base + QK_NOPE] = dq[:, :QK_NOPE].astype(BF16)
                dq_ref[rows, base + QK_NOPE:base + HEAD_PAD] = _rope_t(
                    dq[:, QK_NOPE:], cos_ref[rows, :], sin_ref[rows, :]).astype(BF16)
            dkn_ref[:, h * QK_NOPE:(h + 1) * QK_NOPE] = dk_acc[:, base:base + QK_NOPE].astype(BF16)
            dkr = dkr + dk_acc[:, base + QK_NOPE:base + HEAD_PAD]
        dv_ref[...] = dv_acc[...].astype(BF16)

        @pl.when(pl.program_id(1) == 0)
        def _():
            dkr_ref[...] = jnp.zeros_like(dkr_ref)

        dkr_ref[...] += _rope_t(dkr, cos_ref[...], sin_ref[...])

        @pl.when(last)
        def _():
            _exchange(parts_ref, land_ref, send_sems, recv_sems, local_sem, finish=True)

    head = pl.BlockSpec((seq, hg * V_DIM), lambda b, g: (b, g))
    head_t = pl.BlockSpec((hg * V_DIM, seq), lambda b, g: (g, b))
    shared = pl.BlockSpec((seq, LANES), lambda b, g: (b, 0))
    shared_t = pl.BlockSpec((LANES, seq), lambda b, g: (0, b))
    table = pl.BlockSpec((seq, LANES), lambda b, g: (0, 0))
    qspec = pl.BlockSpec((seq, hg * HEAD_PAD), lambda b, g: (b, g))
    return pl.pallas_call(
        body, grid=steps,
        in_specs=[qspec, head, shared, head_t, shared_t, head, head,
                  pl.BlockSpec((hg, 1, seq), lambda b, g: (g, 0, b)), head, table, table, _ANY],
        out_specs=[qspec, head, shared, head, _ANY],
        out_shape=[jax.ShapeDtypeStruct((t, N_HEADS * HEAD_PAD), BF16), jax.ShapeDtypeStruct((t, N_HEADS * QK_NOPE), BF16),
                   jax.ShapeDtypeStruct((t, LANES), F32), jax.ShapeDtypeStruct((t, N_HEADS * V_DIM), BF16),
                   jax.ShapeDtypeStruct(parts.shape, parts.dtype)],
        scratch_shapes=[pltpu.VMEM((hg * HEAD_PAD, seq), F32), pltpu.VMEM((seq, hg * HEAD_PAD), F32),
                        pltpu.VMEM((seq, hg * V_DIM), F32)] + _EXCHANGE_SEMS,
        compiler_params=_params(("arbitrary", "arbitrary")), name=name)(
            q, kn, kr, kn_t, kr_t, v, o, lse, do, cos, sin, parts)


def _head_and_loss(o, g2, x1, target, w_out, g_final, *, name, bt=256):
    t, d = x1.shape
    bt = min(bt, t)
    nt = (((1,), (1,)), ((), ()))

    def body(o_ref, g2_ref, x1_ref, tgt_ref, w_ref, gf_ref, loss_ref, dx2_ref, y2_ref, do_ref, dg2_ref, dgf_ref):
        @pl.when(pl.program_id(0) == 0)
        def _():
            loss_ref[...] = jnp.zeros_like(loss_ref)
            dgf_ref[...] = jnp.zeros_like(dgf_ref)

        ov = o_ref[...]
        gv = g2_ref[...]
        sg = _sigmoid(gv)
        silu = gv * sg
        y2 = (ov * silu).astype(BF16)
        y2_ref[...] = y2
        w = w_ref[...]
        x2 = x1_ref[...] + jnp.dot(y2, w, preferred_element_type=F32)
        r = lax.rsqrt(jnp.mean(x2 * x2, axis=-1, keepdims=True) + EPS)
        nrm = x2 * r
        gf = gf_ref[...]
        err = nrm * gf - tgt_ref[...]
        loss_ref[...] += 0.5 * jnp.sum(jnp.mean(err * err, axis=-1, keepdims=True))
        dyf = err * (1.0 / d)
        dgf_ref[...] += jnp.sum(dyf * nrm, axis=0, keepdims=True)
        dn = dyf * gf
        dx2 = r * (dn - nrm * jnp.mean(dn * nrm, axis=-1, keepdims=True))
        dx2_ref[...] = dx2
        dy2 = lax.dot_general(dx2.astype(BF16), w, nt, preferred_element_type=F32)
        do_ref[...] = (dy2 * silu).astype(BF16)
        dg2_ref[...] = (dy2 * ov * (sg * (1.0 + gv * (1.0 - sg)))).astype(BF16)

    row = pl.BlockSpec((bt, d), lambda i: (i, 0))
    vec = pl.BlockSpec((1, d), lambda i: (0, 0))
    return pl.pallas_call(
        body, grid=(t // bt,),
        in_specs=[row, row, row, row, pl.BlockSpec((d, d), lambda i: (0, 0)), vec],
        out_specs=[pl.BlockSpec((8, LANES), lambda i: (0, 0)), row, row, row, row, vec],
        out_shape=[jax.ShapeDtypeStruct((8, LANES), F32), jax.ShapeDtypeStruct((t, d), F32),
                   jax.ShapeDtypeStruct((t, d), BF16), jax.ShapeDtypeStruct((t, d), BF16),
                   jax.ShapeDtypeStruct((t, d), BF16), jax.ShapeDtypeStruct((1, d), F32)],
        compiler_params=_params(("arbitrary",)), name=name)(o, g2, x1, target, w_out, g_final)


def _sum_parts(parts, *, name, br=GRAD_BLOCK):
    npart, rows, w = parts.shape

    def body(p_ref, o_ref):
        acc = p_ref[0].astype(F32)
        for j in range(1, npart):
            acc = acc + p_ref[j].astype(F32)
        o_ref[...] = acc

    return pl.pallas_call(
        body, grid=(rows // br,), in_specs=[pl.BlockSpec((npart, br, w), lambda i: (0, i, 0))],
        out_specs=pl.BlockSpec((br, w), lambda i: (i, 0)), out_shape=jax.ShapeDtypeStruct((rows, w), F32),
        compiler_params=_params(("parallel",)), name=name)(parts)


def _chip_partial(parts, recv, *, name, br=GRAD_BLOCK):
    _, rows, w = parts.shape
    core = lax.axis_index("c").astype(jnp.int32).reshape(1)

    def body(c_ref, p_ref, r_ref, o_ref):
        o_ref[...] = (p_ref[...] + r_ref[...]).astype(BF16)

    grid_spec = pltpu.PrefetchScalarGridSpec(
        num_scalar_prefetch=1, grid=(4, rows // br),
        in_specs=[pl.BlockSpec((None, br, w), lambda k, i, c_ref: (2 * k + c_ref[0], i, 0)),
                  pl.BlockSpec((None, br, w), lambda k, i, c_ref: (k, i, 0))],
        out_specs=pl.BlockSpec((None, br, w), lambda k, i, c_ref: (k, i, 0)))
    return pl.pallas_call(
        body, grid_spec=grid_spec, out_shape=jax.ShapeDtypeStruct((4, rows, w), BF16),
        compiler_params=_params(("parallel", "parallel")), name=name)(core, parts, recv)


def _as_block(a):
    if a.ndim == 1:
        return a.reshape(1, -1)
    if a.ndim > 2 and a.shape[0] == 1:
        return a.reshape(a.shape[1:])
    return a


def _adamw(g, w, m, v, *, name):
    shape = w.shape
    g, w, m, v = (_as_block(a) for a in (g, w, m, v))

    def body(g_ref, w_ref, m_ref, v_ref, d_ref, nm_ref, nv_ref):
        gv = g_ref[...]
        nm = ADAM_B1 * m_ref[...] + (1.0 - ADAM_B1) * gv
        nv = ADAM_B2 * v_ref[...] + (1.0 - ADAM_B2) * (gv * gv)
        nm_ref[...] = nm
        nv_ref[...] = nv
        m_hat = nm / (1.0 - ADAM_B1 ** ADAM_STEP)
        v_hat = nv / (1.0 - ADAM_B2 ** ADAM_STEP)
        d_ref[...] = (-ADAM_LR) * (m_hat / (jnp.sqrt(v_hat) + ADAM_EPS) + ADAM_WD * w_ref[...])

    whole = pl.BlockSpec(memory_space=pltpu.VMEM)
    outs = pl.pallas_call(
        body, in_specs=[whole] * 4, out_specs=[whole] * 3, out_shape=[jax.ShapeDtypeStruct(w.shape, F32)] * 3,
        compiler_params=_params(), name=name)(g, w, m, v)
    return [o.reshape(shape) for o in outs]


def _all_gather(block, *, name):
    m, n = block.shape

    def body(x_ref, out_ref, send_sems, recv_sems, local_sem):
        for phase in range(3):
            _gather_two_level(x_ref, out_ref, send_sems, recv_sems, local_sem, phase=phase)

    return pl.pallas_call(
        body, out_shape=jax.ShapeDtypeStruct((N_DEV, m, n), block.dtype), in_specs=[_ANY], out_specs=_ANY,
        scratch_shapes=_EXCHANGE_SEMS, name=name)(block)


def _exchange_d2d(parts, *, name):
    _, rows, w = parts.shape

    def body(p_ref, land_ref, send_sems, recv_sems):
        x, y, c = _mesh_pos()
        sends = []
        for k in range(4):
            cp = pltpu.make_async_remote_copy(
                src_ref=p_ref.at[2 * k + (1 - c)], dst_ref=land_ref.at[k], send_sem=send_sems.at[k],
                recv_sem=recv_sems.at[k], device_id=(x, y, 1 - c), device_id_type=pl.DeviceIdType.MESH)
            cp.start()
            sends.append(cp)
        for cp in sends:
            cp.wait_recv()
        for cp in sends:
            cp.wait_send()

    return pl.pallas_call(
        body, out_shape=jax.ShapeDtypeStruct((4, rows, w), parts.dtype), in_specs=[_ANY], out_specs=_ANY,
        scratch_shapes=[pltpu.SemaphoreType.DMA((4,)), pltpu.SemaphoreType.DMA((4,))], name=name)(parts)


def _exchange_ici(parts, *, name):
    def body(p_ref, land_ref, send_sems, recv_sems, local_sem):
        x, y, c = _mesh_pos()
        mine = pltpu.make_async_copy(p_ref.at[2 * x + y], land_ref.at[3], local_sem)
        mine.start()
        sends = []
        for k, (px, py) in enumerate([(1 - x, y), (x, 1 - y), (1 - x, 1 - y)]):
            cp = pltpu.make_async_remote_copy(
                src_ref=p_ref.at[2 * px + py], dst_ref=land_ref.at[k], send_sem=send_sems.at[k],
                recv_sem=recv_sems.at[k], device_id=(px, py, c), device_id_type=pl.DeviceIdType.MESH)
            cp.start()
            sends.append(cp)
        for cp in sends:
            cp.wait_recv()
        for cp in sends:
            cp.wait_send()
        mine.wait()

    return pl.pallas_call(
        body, out_shape=jax.ShapeDtypeStruct(parts.shape, parts.dtype), in_specs=[_ANY], out_specs=_ANY,
        scratch_shapes=[pltpu.SemaphoreType.DMA((3,)), pltpu.SemaphoreType.DMA((3,)), pltpu.SemaphoreType.DMA(())],
        name=name)(parts)


def _rows(a):
    return a.reshape(-1, PACK_W)


def _pad_to(a, n):
    return jnp.pad(a, (0, n - a.shape[0]))


def _weight_blocks(d):
    small = _rows(_pad_to(jnp.concatenate([d[n].reshape(-1) for n, _ in _SMALL]), 8 * PACK_W))
    block_a = jnp.concatenate([d["w_in_a"][0].T.astype(WIRE), lax.bitcast_convert_type(small, WIRE).reshape(16, PACK_W)],
                              axis=0)
    w_uq = jnp.pad(d["w_uq"][0], ((0, 0), (0, 0), (0, HEAD_PAD - QK_NOPE - QK_ROPE)))
    pieces = {"w_out_a": d["w_out_a"], "w_dkv": d["w_dkv"], "w_uk": d["w_uk"], "w_uv": d["w_uv"],
              "w_in_b": d["w_in_b"][0].T, "w_uq": w_uq}
    block_b = jnp.concatenate([_rows(pieces[n]) for n, _ in _PIECES_B]
                              + [jnp.zeros((WIRE_ROWS_B - MATRIX_ROWS_B, PACK_W), F32)], axis=0).astype(WIRE)
    return block_a, block_b, d["w_out_b"][0].astype(WIRE)


def _weights_a(wall):
    w = {}
    lo, hi = _OFF_A["w_in_a"]
    w["w_in_a_t"] = wall[:, lo:hi].reshape(2 * D_RNN, D_MODEL)
    small = lax.bitcast_convert_type(wall[:, MATRIX_ROWS_A:].reshape(N_DEV, 8 * PACK_W, 2), F32)
    off = dict(zip([n for n, _ in _SMALL], [0, 128, 768, 928, 1088, 1248]))
    w["norm_a"] = small[:, :128].reshape(1, D_MODEL)

    def by_channel(lo, rows):
        a = small[:, lo:lo + rows * (D_RNN // N_DEV)].reshape(N_DEV, rows, -1).transpose(1, 0, 2).reshape(rows, D_RNN)
        return jnp.pad(a, ((0, 8 - rows), (0, 0)))

    w["conv_taps"] = by_channel(off["conv_w"], CONV_WIDTH)
    w["lru_vecs"] = by_channel(off["conv_b"], 4)
    return w


def _weights_b(wall):
    piece = {n: wall[:, lo:hi] for n, (lo, hi) in _OFF_B.items()}
    w = {"w_out_a": piece["w_out_a"].reshape(D_RNN, D_MODEL)}
    w_dkv = piece["w_dkv"].reshape(D_MODEL, KV_RANK + QK_ROPE)
    w["w_dkv_c"] = w_dkv[:, :KV_RANK]
    w["w_dkv_r"] = jnp.pad(w_dkv[:, KV_RANK:], ((0, 0), (0, LANES - QK_ROPE)))
    w["w_uk"] = piece["w_uk"].reshape(KV_RANK, N_HEADS * QK_NOPE)
    w["w_uv"] = piece["w_uv"].reshape(KV_RANK, N_HEADS * V_DIM)
    w["w_uk_t"], w["w_uv_t"] = w["w_uk"].T, w["w_uv"].T
    w["w_in_b_t"] = piece["w_in_b"].reshape(Q_RANK + N_HEADS * V_DIM, D_MODEL)
    w["w_uq"] = piece["w_uq"].reshape(Q_RANK, N_HEADS * HEAD_PAD)
    return w


def _pack_rep(d):
    flat = jnp.concatenate([d[n].reshape(-1) for n, _ in _REP])
    return _rows(_pad_to(flat, REP_ROWS * PACK_W))


def _unpack_rep(p, like):
    flat = p.reshape(-1)
    out, off = {}, 0
    for n, k in _REP:
        out[n] = flat[off:off + k].reshape(like[n].shape)
        off += k
    return out


def _by_owner(a):
    return a.reshape(N_DEV, -1, PACK_W)


def _grad_parts_b(g):
    tail = jnp.zeros((N_DEV, WIRE_ROWS_B - MATRIX_ROWS_B, PACK_W), F32)
    return jnp.concatenate([_by_owner(g[n]) for n, _ in _PIECES_B] + [tail], axis=1).astype(BF16)


def _grad_parts_a(g):
    small = jnp.concatenate([
        g["norm_a"].reshape(N_DEV, -1),
        g["conv_w"].reshape(CONV_WIDTH, N_DEV, -1).transpose(1, 0, 2).reshape(N_DEV, -1),
        g["conv_b"].reshape(N_DEV, -1), g["b_rg"].reshape(N_DEV, -1), g["b_ig"].reshape(N_DEV, -1),
        g["lru_lambda"].reshape(N_DEV, -1)], axis=1)
    small = jnp.pad(small, ((0, 0), (0, 8 * PACK_W - small.shape[1]))).reshape(N_DEV, 8, PACK_W)
    half = N_DEV // 2
    w_in_a = jnp.concatenate([h.reshape(half, -1, PACK_W) for h in g["w_in_a_t"]], axis=0)
    rep = _pack_rep(g).reshape(N_DEV, REP_SLICE, PACK_W)
    tail = jnp.zeros((N_DEV, GRAD_ROWS_A - MATRIX_ROWS_A - 8 - REP_SLICE, PACK_W), F32)
    return jnp.concatenate([w_in_a, small, rep, tail], axis=1)


def _own_grads(sum_a, sum_b, sum_c):
    out = {}
    lo, hi = _OFF_A["w_in_a"]
    out["w_in_a"] = sum_a[lo:hi].T.reshape(1, D_MODEL, 2 * D_RNN // N_DEV)
    small = sum_a[MATRIX_ROWS_A:MATRIX_ROWS_A + 8].reshape(-1)
    shapes = {"norm_a": (1, D_MODEL // N_DEV), "conv_w": (1, CONV_WIDTH, D_RNN // N_DEV), "conv_b": (1, D_RNN // N_DEV),
              "b_rg": (1, D_RNN // N_DEV), "b_ig": (1, D_RNN // N_DEV), "lru_lambda": (1, D_RNN // N_DEV)}
    off = 0
    for n, k in _SMALL:
        out[n] = small[off:off + k].reshape(shapes[n])
        off += k
    piece = {n: sum_b[lo:hi] for n, (lo, hi) in _OFF_B.items()}
    out["w_out_a"] = piece["w_out_a"].reshape(1, D_RNN // N_DEV, D_MODEL)
    out["w_dkv"] = piece["w_dkv"].reshape(D_MODEL // N_DEV, KV_RANK + QK_ROPE)
    out["w_uk"] = piece["w_uk"].reshape(KV_RANK // N_DEV, N_HEADS, QK_NOPE)
    out["w_uv"] = piece["w_uv"].reshape(KV_RANK // N_DEV, N_HEADS, V_DIM)
    out["w_in_b"] = piece["w_in_b"].T.reshape(1, D_MODEL, (Q_RANK + N_HEADS * V_DIM) // N_DEV)
    out["w_uq"] = piece["w_uq"].reshape(1, Q_RANK // N_DEV, N_HEADS, HEAD_PAD)[..., :QK_NOPE + QK_ROPE]
    out["w_out_b"] = sum_c.reshape(1, N_HEADS * V_DIM // N_DEV, D_MODEL)
    return out


def _step(x, target, w, rep, block_b, block_c, *, bsz, seq):
    t = bsz * seq
    cos, sin = _rope_tables(seq)
    g_a = w["norm_a"]
    g_kv = rep["norm_kv"].reshape(1, -1)
    g_kvn = rep["kv_norm"].reshape(1, -1)
    g_b = rep["norm_b"].reshape(1, -1)
    g_q = rep["q_norm"].reshape(1, -1)
    g_f = rep["final_norm"].reshape(1, -1)
    wrg = rep["w_rg"][0].astype(BF16)
    wig = rep["w_ig"][0].astype(BF16)
    cw8, vecs = w["conv_taps"], w["lru_vecs"]

    def seq3(a):
        return a.reshape(bsz, seq, a.shape[-1])

    def flat(a):
        return a.reshape(t, a.shape[-1])

    h0, xp, ga = _lru_proj_fwd(x, g_a, w["w_in_a_t"], name="lru_proj_fwd")
    xb, hs, y, wall_b = _lru_fwd(seq3(xp), seq3(ga), cw8, vecs, wrg, wig, block_b, name="lru_fwd")
    w = dict(w, **_weights_b(wall_b))
    x1 = _matmul(flat(y), w["w_out_a"], residual=x, name="out_a")
    hk, hq, ck, cqp, g2, ckv, cq, q, kn, v, kr, kn_t, v_t, kr_t = _mla_proj_fwd(
        x1, (g_kv, g_b, g_kvn, g_q), w, cos, sin, seq=seq, name="mla_proj_fwd")
    o, lse, wall_c = _attn_fwd(q, kn, kr, v_t, block_c, bsz=bsz, seq=seq, name="attn_fwd")
    w_out_b = wall_c.reshape(N_HEADS * V_DIM, D_MODEL)
    loss, dx2, y2, do, dg2, dgf = _head_and_loss(o, g2, x1, target, w_out_b, g_f, name="head_loss")
    grads = {"final_norm": dgf}
    parts_c = _by_owner(_matmul_tn(y2, dx2, name="d_w_out_b")).astype(BF16)
    dq, dkn, dkr, dv, landed_c = _attn_bwd(q, kn, kr, kn_t, kr_t, v, o, lse, do, cos, sin, parts_c,
                                           bsz=bsz, seq=seq, name="attn_bwd")
    grads["w_uq"] = _matmul_tn(cq, dq, name="d_w_uq")
    dx1, du2, dckr, dgkv, dgb, dgkvn, dgq = _mla_proj_bwd(
        x1, dx2, cqp, ck, dq, dkn, dv, dkr, dg2, (g_kv, g_b, g_kvn, g_q), w, name="mla_proj_bwd")
    grads["norm_kv"], grads["norm_b"], grads["kv_norm"], grads["q_norm"] = dgkv, dgb, dgkvn, dgq
    grads["w_in_b"] = _matmul_tn(du2, hq, name="d_w_in_b_t")
    grads["w_uk"] = _matmul_tn(ckv, dkn, name="d_w_uk")
    grads["w_uv"] = _matmul_tn(ckv, dv, name="d_w_uv")
    grads["w_dkv"] = _matmul_tn(hk, dckr, name="d_w_dkv")[:, :KV_RANK + QK_ROPE]
    grads["w_out_a"] = _matmul_tn(flat(y), dx1, name="d_w_out_a")
    parts_b = _grad_parts_b(grads)
    dy = _matmul(dx1, w["w_out_a"], nt=True, name="d_y")
    dxp, dga, dwrg, dwig, dvec, landed_b = _lru_bwd(
        seq3(dy), seq3(xp), xb, hs, seq3(ga), cw8, vecs, wrg, wig, parts_b, name="lru_bwd")
    dxp, dga = flat(dxp), flat(dga)
    grads["w_rg"], grads["w_ig"] = dwrg, dwig
    grads["b_rg"], grads["b_ig"], grads["conv_b"] = dvec[0], dvec[1], dvec[3]
    lam = vecs[3]
    grads["lru_lambda"] = dvec[2] * (-1.0 / (1.0 + jnp.exp(lam)))
    grads["conv_w"] = dvec[4:4 + CONV_WIDTH]
    grads["w_in_a_t"] = (_matmul_tn(dxp, h0, name="d_w_in_a_x_t"), _matmul_tn(dga, h0, name="d_w_in_a_g_t"))
    dx, dga_norm = _lru_proj_bwd(dxp, dga, x, dx1, g_a, w["w_in_a_t"], name="lru_proj_bwd")
    grads["norm_a"] = dga_norm
    return loss[0, 0], dx, grads, landed_b, landed_c


def kernel(x, norm_a, w_in_a, conv_w, conv_b, w_rg, b_rg, w_ig, b_ig, lru_lambda, w_out_a, norm_kv, w_dkv, kv_norm, w_uk, w_uv, norm_b, w_in_b, q_norm, w_uq, w_out_b, final_norm, loss_target, m_norm_a, m_w_in_a, m_conv_w, m_conv_b, m_w_rg, m_b_rg, m_w_ig, m_b_ig, m_lru_lambda, m_w_out_a, m_norm_kv, m_w_dkv, m_kv_norm, m_w_uk, m_w_uv, m_norm_b, m_w_in_b, m_q_norm, m_w_uq, m_w_out_b, m_final_norm, v_norm_a, v_w_in_a, v_conv_w, v_conv_b, v_w_rg, v_b_rg, v_w_ig, v_b_ig, v_lru_lambda, v_w_out_a, v_norm_kv, v_w_dkv, v_kv_norm, v_w_uk, v_w_uv, v_norm_b, v_w_in_b, v_q_norm, v_w_uq, v_w_out_b, v_final_norm):
    given = dict(locals())
    wts = {n: given[n] for n in WEIGHTS}
    mom1 = {n: given["m_" + n] for n in WEIGHTS}
    mom2 = {n: given["v_" + n] for n in WEIGHTS}
    bsz, seq, _ = x.shape
    t = bsz * seq

    block_a, block_b, block_c = _weight_blocks(wts)
    w = _weights_a(_all_gather(block_a, name="gather_weights_a"))
    loss, dx, grads, landed_b, landed_c = _step(x.reshape(t, D_MODEL), loss_target.reshape(t, D_MODEL), w, wts,
                                                block_b, block_c, bsz=bsz, seq=seq)
    loss = lax.psum(loss, MESH_AXES)

    parts_a = _grad_parts_a(grads)
    from_sibling = _exchange_d2d(parts_a, name="exchange_grads_d2d")
    chip_parts = _chip_partial(parts_a, from_sibling, name="chip_partial_grads")
    landed_a = _exchange_ici(chip_parts, name="exchange_grads_ici")
    sum_a = _sum_parts(landed_a, name="sum_grads_a", br=GRAD_BLOCK)
    sum_b = _sum_parts(landed_b, name="sum_grads_b", br=WIRE_ROWS_B // 2)
    sum_c = _sum_parts(landed_c, name="sum_grads_c", br=landed_c.shape[1])
    g_own = _own_grads(sum_a, sum_b, sum_c)
    rep_slice = sum_a[MATRIX_ROWS_A + 8:MATRIX_ROWS_A + 8 + REP_SLICE]
    g_rep = _all_gather(rep_slice, name="gather_replicated").reshape(REP_ROWS, PACK_W)
    g_own.update(_unpack_rep(g_rep, wts))

    deltas, new_m, new_v = {}, {}, {}
    for n in WEIGHTS:
        deltas[n], new_m[n], new_v[n] = _adamw(g_own[n], wts[n], mom1[n], mom2[n], name="adamw_" + n)
    result = [loss, dx.reshape(bsz, seq, D_MODEL)]
    for d in (g_own, deltas, new_m, new_v):
        result.extend(d[n] for n in WEIGHTS)
    return tuple(result)
```

```python
import jax
import jax.numpy as jnp
from jax import lax
from jax.experimental import pallas as pl
from jax.experimental.pallas import tpu as pltpu

F32 = jnp.float32
BF16 = jnp.bfloat16
WIRE = jnp.bfloat16

D_MODEL = 1024
D_RNN = 1280
RNN_BLOCKS = 10
RNN_BW = 128
CONV_WIDTH = 4
LRU_C = 8.0
N_HEADS = 8
QK_NOPE = 128
QK_ROPE = 64
V_DIM = 128
KV_RANK = 256
Q_RANK = 384
ROPE_THETA = 10000.0
EPS = 1e-6
ATTN_SCALE = (QK_NOPE + QK_ROPE) ** -0.5
HEAD_PAD = 256
LANES = 128

ADAM_LR = 0.001
ADAM_B1 = 0.9
ADAM_B2 = 0.999
ADAM_EPS = 1e-08
ADAM_WD = 0.01
ADAM_STEP = 10

N_DEV = 8
MESH_AXES = ("x", "y", "c")
VMEM_LIMIT_BYTES = 56 * 2**20
PACK_W = 1024

_PIECES_A = (("w_in_a", 320),)
_PIECES_B = (("w_out_a", 160), ("w_dkv", 40), ("w_uk", 32), ("w_uv", 32), ("w_in_b", 176), ("w_uq", 96))


def _offsets(pieces):
    off, r = {}, 0
    for n, k in pieces:
        off[n] = (r, r + k)
        r += k
    return off, r


_OFF_A, MATRIX_ROWS_A = _offsets(_PIECES_A)
_OFF_B, MATRIX_ROWS_B = _offsets(_PIECES_B)
WIRE_ROWS_A = MATRIX_ROWS_A + 16
WIRE_ROWS_B = 544
_SMALL = (("norm_a", 128), ("conv_w", 640), ("conv_b", 160), ("b_rg", 160), ("b_ig", 160), ("lru_lambda", 160))
_REP = (("w_rg", 163840), ("w_ig", 163840), ("norm_kv", 1024), ("kv_norm", 256), ("norm_b", 1024),
        ("q_norm", 384), ("final_norm", 1024))
REP_ROWS = 384
REP_SLICE = REP_ROWS // N_DEV
GRAD_ROWS_A = 384
GRAD_BLOCK = 192

WEIGHTS = ("norm_a", "w_in_a", "conv_w", "conv_b", "w_rg", "b_rg", "w_ig", "b_ig", "lru_lambda", "w_out_a",
           "norm_kv", "w_dkv", "kv_norm", "w_uk", "w_uv", "norm_b", "w_in_b", "q_norm", "w_uq", "w_out_b",
           "final_norm")


def _params(sem=None):
    return pltpu.CompilerParams(dimension_semantics=sem, vmem_limit_bytes=VMEM_LIMIT_BYTES)


_NT = (((1,), (1,)), ((), ()))
_ANY = pl.BlockSpec(memory_space=pl.ANY)


def _mesh_pos():
    return lax.axis_index("x"), lax.axis_index("y"), lax.axis_index("c")


def _sigmoid(z):
    return 0.5 * jnp.tanh(0.5 * z) + 0.5


def _sigmoid_tail(z):
    return 1.0 / (1.0 + jnp.exp(-z))


def _col_block(n):
    return n if n <= 1408 else n // 2


def _matmul(a, b, *, name, nt=False, out_dtype=F32, residual=None, bm=512):
    m, k = a.shape
    n = b.shape[0] if nt else b.shape[1]
    bm = min(bm, m)
    bn = _col_block(n)
    dims = (((1,), (1,)), ((), ())) if nt else (((1,), (0,)), ((), ()))
    has_res = residual is not None

    def body(*refs):
        a_ref, b_ref, o_ref = refs[0], refs[1], refs[-1]
        acc = lax.dot_general(a_ref[...].astype(BF16), b_ref[...].astype(BF16), dims, preferred_element_type=F32)
        if has_res:
            acc = acc + refs[2][...]
        o_ref[...] = acc.astype(out_dtype)

    in_specs = [pl.BlockSpec((bm, k), lambda i, j: (i, 0)),
                pl.BlockSpec((bn, k), lambda i, j: (j, 0)) if nt else pl.BlockSpec((k, bn), lambda i, j: (0, j))]
    args = [a, b]
    if has_res:
        in_specs.append(pl.BlockSpec((bm, bn), lambda i, j: (i, j)))
        args.append(residual)
    return pl.pallas_call(
        body, grid=(m // bm, n // bn), in_specs=in_specs, out_specs=pl.BlockSpec((bm, bn), lambda i, j: (i, j)),
        out_shape=jax.ShapeDtypeStruct((m, n), out_dtype), compiler_params=_params(("parallel", "parallel")),
        name=name)(*args)


def _matmul_tn(a, b, *, name, bt=512):
    t, m = a.shape
    n = b.shape[1]
    bt = min(bt, t)
    bm, bn = _col_block(m), _col_block(n)

    def body(a_ref, b_ref, o_ref):
        @pl.when(pl.program_id(2) == 0)
        def _():
            o_ref[...] = jnp.zeros_like(o_ref)

        o_ref[...] += lax.dot_general(a_ref[...].astype(BF16), b_ref[...].astype(BF16),
                                      (((0,), (0,)), ((), ())), preferred_element_type=F32)

    return pl.pallas_call(
        body, grid=(m // bm, n // bn, t // bt),
        in_specs=[pl.BlockSpec((bt, bm), lambda i, j, s: (s, i)), pl.BlockSpec((bt, bn), lambda i, j, s: (s, j))],
        out_specs=pl.BlockSpec((bm, bn), lambda i, j, s: (i, j)),
        out_shape=jax.ShapeDtypeStruct((m, n), F32),
        compiler_params=_params(("parallel", "parallel", "arbitrary")), name=name)(a, b)


def _swap_halves(v):
    ax = v.ndim - 1
    lane = lax.broadcasted_iota(jnp.int32, v.shape, ax)
    up = pltpu.roll(v, LANES - QK_ROPE // 2, axis=ax)
    down = pltpu.roll(v, QK_ROPE // 2, axis=ax)
    return jnp.where(lane < QK_ROPE // 2, up, jnp.where(lane < QK_ROPE, down, 0.0))


def _rope(v, cos, sin):
    return v * cos + _swap_halves(v) * sin


def _rope_t(d, cos, sin):
    return d * cos + _swap_halves(d * sin)


def _rope_tables(seq):
    pos = jnp.arange(seq, dtype=F32)
    inv = ROPE_THETA ** (-jnp.arange(0, QK_ROPE, 2, dtype=F32) / QK_ROPE)
    ang = pos[:, None] * inv[None, :]
    cos, sin = jnp.cos(ang), jnp.sin(ang)
    zero = jnp.zeros((seq, LANES - QK_ROPE), F32)
    return jnp.concatenate([cos, cos, zero], axis=1), jnp.concatenate([-sin, sin, zero], axis=1)


def _rms(v):
    return v * lax.rsqrt(jnp.mean(v * v, axis=-1, keepdims=True) + EPS)


def _const_spec(a):
    return pl.BlockSpec(a.shape, lambda i: (0,) * a.ndim)


def _lru_proj_fwd(x, g_a, w_in_t, *, name, bt=256):
    t, d = x.shape
    bt = min(bt, t)
    n = w_in_t.shape[0] // 2

    def body(x_ref, g_ref, wt_ref, h_ref, xp_ref, ga_ref):
        h = (_rms(x_ref[...]) * g_ref[...]).astype(BF16)
        h_ref[...] = h
        xp_ref[...] = lax.dot_general(h, wt_ref[0:n, :], _NT, preferred_element_type=F32)
        ga_ref[...] = lax.dot_general(h, wt_ref[n:2 * n, :], _NT, preferred_element_type=F32)

    row = lambda w: pl.BlockSpec((bt, w), lambda i: (i, 0))
    return pl.pallas_call(
        body, grid=(t // bt,), in_specs=[row(d), _const_spec(g_a), _const_spec(w_in_t)],
        out_specs=[row(d), row(n), row(n)],
        out_shape=[jax.ShapeDtypeStruct((t, d), BF16), jax.ShapeDtypeStruct((t, n), F32), jax.ShapeDtypeStruct((t, n), F32)],
        compiler_params=_params(("parallel",)), name=name)(x, g_a, w_in_t)


def _mla_proj_fwd(x1, gains, w, cos, sin, *, seq, name, bt=256):
    t, d = x1.shape
    bt = min(bt, seq)
    per_seq = seq // bt
    g_kv, g_b, g_kvn, g_q = gains
    consts = [g_kv, g_b, g_kvn, g_q, w["w_dkv_c"], w["w_dkv_r"], w["w_in_b_t"], w["w_uk"], w["w_uv"],
              w["w_uk_t"], w["w_uv_t"], w["w_uq"]]

    def body(x_ref, cos_ref, sin_ref, gkv_ref, gb_ref, gkvn_ref, gq_ref, wdc_ref, wdr_ref, wbt_ref,
             wuk_ref, wuv_ref, wukt_ref, wuvt_ref, wuq_ref,
             hk_ref, hq_ref, ck_ref, cqp_ref, g2_ref, ckv_ref, cq_ref, q_ref, kn_ref, v_ref, kr_ref, knt_ref, vt_ref, krt_ref):
        nrm = _rms(x_ref[...])
        hk = (nrm * gkv_ref[...]).astype(BF16)
        hq = (nrm * gb_ref[...]).astype(BF16)
        hk_ref[...] = hk
        hq_ref[...] = hq
        ck = jnp.dot(hk, wdc_ref[...], preferred_element_type=F32)
        ck_ref[...] = ck
        cqp = lax.dot_general(hq, wbt_ref[0:Q_RANK, :], _NT, preferred_element_type=F32)
        cqp_ref[...] = cqp
        g2_ref[...] = lax.dot_general(hq, wbt_ref[Q_RANK:, :], _NT, preferred_element_type=F32)
        cosv, sinv = cos_ref[...], sin_ref[...]
        kr = _rope(jnp.dot(hk, wdr_ref[...], preferred_element_type=F32), cosv, sinv)
        kr_ref[...] = kr.astype(BF16)
        krt_ref[...] = kr.T.astype(BF16)
        ckv = (_rms(ck) * gkvn_ref[...]).astype(BF16)
        ckv_ref[...] = ckv
        kn_ref[...] = jnp.dot(ckv, wuk_ref[...], preferred_element_type=F32).astype(BF16)
        v_ref[...] = jnp.dot(ckv, wuv_ref[...], preferred_element_type=F32).astype(BF16)
        knt_ref[...] = lax.dot_general(wukt_ref[...], ckv, _NT, preferred_element_type=F32).astype(BF16)
        vt_ref[...] = lax.dot_general(wuvt_ref[...], ckv, _NT, preferred_element_type=F32).astype(BF16)
        cq = (_rms(cqp) * gq_ref[...]).astype(BF16)
        cq_ref[...] = cq
        for h in range(N_HEADS):
            qh = jnp.dot(cq, wuq_ref[:, h * HEAD_PAD:(h + 1) * HEAD_PAD], preferred_element_type=F32)
            q_ref[:, h * HEAD_PAD:h * HEAD_PAD + QK_NOPE] = qh[:, :QK_NOPE].astype(BF16)
            q_ref[:, h * HEAD_PAD + QK_NOPE:(h + 1) * HEAD_PAD] = _rope(qh[:, QK_NOPE:], cosv, sinv).astype(BF16)

    row = lambda w_: pl.BlockSpec((bt, w_), lambda i: (i, 0))
    col = lambda h_: pl.BlockSpec((h_, bt), lambda i: (0, i))
    tab = pl.BlockSpec((bt, LANES), lambda i: (i % per_seq, 0))
    nh = N_HEADS * V_DIM
    shapes = [((t, d), BF16), ((t, d), BF16), ((t, KV_RANK), F32), ((t, Q_RANK), F32), ((t, nh), F32), ((t, KV_RANK), BF16),
              ((t, Q_RANK), BF16), ((t, N_HEADS * HEAD_PAD), BF16), ((t, nh), BF16), ((t, nh), BF16), ((t, LANES), BF16),
              ((nh, t), BF16), ((nh, t), BF16), ((LANES, t), BF16)]
    out_specs = [row(d), row(d), row(KV_RANK), row(Q_RANK), row(nh), row(KV_RANK), row(Q_RANK), row(N_HEADS * HEAD_PAD),
                 row(nh), row(nh), row(LANES), col(nh), col(nh), col(LANES)]
    return pl.pallas_call(
        body, grid=(t // bt,), in_specs=[row(d), tab, tab] + [_const_spec(a) for a in consts], out_specs=out_specs,
        out_shape=[jax.ShapeDtypeStruct(s, dt) for s, dt in shapes],
        compiler_params=_params(("parallel",)), name=name)(x1, cos, sin, *consts)


def _rms_bwd_rows(xv, dn):
    r = lax.rsqrt(jnp.mean(xv * xv, axis=-1, keepdims=True) + EPS)
    nrm = xv * r
    return r * (dn - nrm * jnp.mean(dn * nrm, axis=-1, keepdims=True)), nrm


def _col_sum(v):
    return jnp.sum(v, axis=0, keepdims=True)


def _lru_proj_bwd(dxp, dga, x, dx1, g_a, w_in_t, *, name, bt=256):
    t, d = x.shape
    bt = min(bt, t)
    n = w_in_t.shape[0] // 2

    def body(dxp_ref, dga_ref, x_ref, dx1_ref, g_ref, wt_ref, dx_ref, dg_ref):
        @pl.when(pl.program_id(0) == 0)
        def _():
            dg_ref[...] = jnp.zeros_like(dg_ref)

        dh = (jnp.dot(dxp_ref[...], wt_ref[0:n, :], preferred_element_type=F32)
              + jnp.dot(dga_ref[...], wt_ref[n:2 * n, :], preferred_element_type=F32))
        dxn, nrm = _rms_bwd_rows(x_ref[...], dh * g_ref[...])
        dg_ref[...] += _col_sum(dh * nrm)
        dx_ref[...] = dx1_ref[...] + dxn

    row = lambda w: pl.BlockSpec((bt, w), lambda i: (i, 0))
    return pl.pallas_call(
        body, grid=(t // bt,),
        in_specs=[row(n), row(n), row(d), row(d), _const_spec(g_a), _const_spec(w_in_t)],
        out_specs=[row(d), _const_spec(g_a)],
        out_shape=[jax.ShapeDtypeStruct((t, d), F32), jax.ShapeDtypeStruct((1, d), F32)],
        compiler_params=_params(("arbitrary",)), name=name)(dxp, dga, x, dx1, g_a, w_in_t)


def _mla_proj_bwd(x1, dx2, cqp, ck, dq, dkn, dv, dkr, dg2, gains, w, *, name, bt=256):
    t, d = x1.shape
    bt = min(bt, t)
    g_kv, g_b, g_kvn, g_q = gains
    consts = [g_kv, g_b, g_kvn, g_q, w["w_dkv_c"], w["w_dkv_r"], w["w_in_b_t"], w["w_uk"], w["w_uv"], w["w_uq"]]
    nh = N_HEADS * V_DIM

    def body(x1_ref, dx2_ref, cqp_ref, ck_ref, dq_ref, dkn_ref, dv_ref, dkr_ref, dg2_ref,
             gkv_ref, gb_ref, gkvn_ref, gq_ref, wdc_ref, wdr_ref, wbt_ref, wuk_ref, wuv_ref, wuq_ref,
             dx1_ref, du2_ref, dckr_ref, dgkv_ref, dgb_ref, dgkvn_ref, dgq_ref):
        @pl.when(pl.program_id(0) == 0)
        def _():
            for ref in (dgkv_ref, dgb_ref, dgkvn_ref, dgq_ref):
                ref[...] = jnp.zeros_like(ref)

        dot_nt = lambda a, b: lax.dot_general(a, b, _NT, preferred_element_type=F32)
        dcq = dot_nt(dq_ref[...], wuq_ref[...])
        dcqp, nq = _rms_bwd_rows(cqp_ref[...], dcq * gq_ref[...])
        dgq_ref[...] += _col_sum(dcq * nq)
        dcqp = dcqp.astype(BF16)
        dg2 = dg2_ref[...]
        du2_ref[:, :Q_RANK] = dcqp
        du2_ref[:, Q_RANK:] = dg2
        dhq = (jnp.dot(dcqp, wbt_ref[0:Q_RANK, :], preferred_element_type=F32)
               + jnp.dot(dg2, wbt_ref[Q_RANK:, :], preferred_element_type=F32))
        dckv = dot_nt(dkn_ref[...], wuk_ref[...]) + dot_nt(dv_ref[...], wuv_ref[...])
        dck, nc = _rms_bwd_rows(ck_ref[...], dckv * gkvn_ref[...])
        dgkvn_ref[...] += _col_sum(dckv * nc)
        dck = dck.astype(BF16)
        dkr = dkr_ref[...].astype(BF16)
        dckr_ref[:, :KV_RANK] = dck
        dckr_ref[:, KV_RANK:] = dkr
        dhk = dot_nt(dck, wdc_ref[...]) + dot_nt(dkr, wdr_ref[...])
        dxn, n1 = _rms_bwd_rows(x1_ref[...], dhq * gb_ref[...] + dhk * gkv_ref[...])
        dgb_ref[...] += _col_sum(dhq * n1)
        dgkv_ref[...] += _col_sum(dhk * n1)
        dx1_ref[...] = dx2_ref[...] + dxn

    row = lambda w_: pl.BlockSpec((bt, w_), lambda i: (i, 0))
    vec = lambda w_: pl.BlockSpec((1, w_), lambda i: (0, 0))
    in_specs = [row(d), row(d), row(Q_RANK), row(KV_RANK), row(N_HEADS * HEAD_PAD), row(nh), row(nh), row(LANES), row(nh)]
    return pl.pallas_call(
        body, grid=(t // bt,), in_specs=in_specs + [_const_spec(a) for a in consts],
        out_specs=[row(d), row(Q_RANK + nh), row(KV_RANK + LANES), vec(d), vec(d), vec(KV_RANK), vec(Q_RANK)],
        out_shape=[jax.ShapeDtypeStruct((t, d), F32), jax.ShapeDtypeStruct((t, Q_RANK + nh), BF16),
                   jax.ShapeDtypeStruct((t, KV_RANK + LANES), BF16), jax.ShapeDtypeStruct((1, d), F32),
                   jax.ShapeDtypeStruct((1, d), F32), jax.ShapeDtypeStruct((1, KV_RANK), F32),
                   jax.ShapeDtypeStruct((1, Q_RANK), F32)],
        compiler_params=_params(("arbitrary",)), name=name)(x1, dx2, cqp, ck, dq, dkn, dv, dkr, dg2, *consts)


def _softplus(z):
    return jnp.maximum(z, 0.0) + jnp.log1p(jnp.exp(-jnp.abs(z)))


def _one_minus_square(a, la):
    return jnp.tanh(-la) * (1.0 + a * a)


def _gates(xb, wrg, wig, brg, big, sp):
    xbb = xb.astype(BF16)
    r = _sigmoid_tail(jnp.dot(xbb, wrg, preferred_element_type=F32) + brg)
    i = _sigmoid(jnp.dot(xbb, wig, preferred_element_type=F32) + big)
    la = (-LRU_C) * r * sp
    a = jnp.exp(la)
    em = _one_minus_square(a, la)
    inv_mult = lax.rsqrt(em)
    mult = jnp.where(em > 0.0, em * inv_mult, 0.0)
    return r, i, a, mult, inv_mult


def _conv(xpad_ref, cw_ref, seq):
    acc = cw_ref[0:1, :] * xpad_ref[pl.ds(8 - (CONV_WIDTH - 1), seq), :]
    for k in range(1, CONV_WIDTH):
        acc = acc + cw_ref[k:k + 1, :] * xpad_ref[pl.ds(8 - (CONV_WIDTH - 1) + k, seq), :]
    return acc


def _seq_spec(seq):
    return pl.BlockSpec((None, seq, RNN_BW), lambda n, b: (b, 0, n))


def _chan_spec(rows):
    return pl.BlockSpec((rows, RNN_BW), lambda n, b: (0, n))


_GATE_W_SPEC = pl.BlockSpec((None, RNN_BW, RNN_BW), lambda n, b: (n, 0, 0))


SCAN_UNROLL = 4


def _peers():
    x, y, c = _mesh_pos()
    others = []
    for k in range(1, N_DEV):
        px = 1 - x if k & 4 else x
        py = 1 - y if k & 2 else y
        pc = 1 - c if k & 1 else c
        others.append(((px, py, pc), 4 * px + 2 * py + pc))
    return 4 * x + 2 * y + c, others


def _exchange(src_ref, dst_ref, send_sems, recv_sems, local_sem, *, finish, gather=False):
    me, others = _peers()

    def send(k, dev, slot):
        return pltpu.make_async_remote_copy(
            src_ref=src_ref if gather else src_ref.at[slot], dst_ref=dst_ref.at[me], send_sem=send_sems.at[k],
            recv_sem=recv_sems.at[k], device_id=dev, device_id_type=pl.DeviceIdType.MESH)

    local = pltpu.make_async_copy(src_ref if gather else src_ref.at[me], dst_ref.at[me], local_sem)
    if not finish:
        local.start()
        for k, (dev, slot) in enumerate(others):
            send(k, dev, slot).start()
        return
    for k, (dev, slot) in enumerate(others):
        pltpu.make_async_remote_copy(
            src_ref=dst_ref.at[slot], dst_ref=dst_ref.at[slot], send_sem=send_sems.at[k], recv_sem=recv_sems.at[k],
            device_id=dev, device_id_type=pl.DeviceIdType.MESH).wait_recv()
    for k, (dev, slot) in enumerate(others):
        send(k, dev, slot).wait_send()
    local.wait()


def _gather_two_level(x_ref, out_ref, send_sems, recv_sems, local_sem, *, phase):
    x, y, c = _mesh_pos()
    me, sibling = (x, y, c), (x, y, 1 - c)
    chips = [(1 - x, y), (x, 1 - y), (1 - x, 1 - y)]

    def slot(px, py, pc):
        return out_ref.at[4 * px + 2 * py + pc]

    def copy(k, blk, to, src=None):
        return pltpu.make_async_remote_copy(
            src_ref=slot(*blk) if src is None else src, dst_ref=slot(*blk),
            send_sem=send_sems.at[k], recv_sem=recv_sems.at[k], device_id=to, device_id_type=pl.DeviceIdType.MESH)

    if phase == 0:
        pltpu.make_async_copy(x_ref, slot(*me), local_sem).start()
        copy(0, me, sibling, src=x_ref).start()
        for j, chip in enumerate(chips):
            copy(1 + j, me, (*chip, c), src=x_ref).start()
    elif phase == 1:
        for j, chip in enumerate(chips):
            copy(1 + j, (*chip, c), me).wait_recv()
            copy(4 + j, (*chip, c), sibling).start()
    else:
        copy(0, sibling, me).wait_recv()
        for j, chip in enumerate(chips):
            copy(4 + j, (*chip, 1 - c), me).wait_recv()
        copy(0, me, sibling, src=x_ref).wait_send()
        for j, chip in enumerate(chips):
            copy(1 + j, me, (*chip, c), src=x_ref).wait_send()
            copy(4 + j, (*chip, c), sibling).wait_send()
        pltpu.make_async_copy(x_ref, slot(*me), local_sem).wait()


GATHER_FORWARD_STEP = 7
_EXCHANGE_SEMS = [pltpu.SemaphoreType.DMA((N_DEV - 1,)), pltpu.SemaphoreType.DMA((N_DEV - 1,)), pltpu.SemaphoreType.DMA(())]


def _first_last(steps):
    first = last = None
    for axis, n in enumerate(steps):
        i = pl.program_id(axis)
        first = (i == 0) if first is None else first & (i == 0)
        last = (i == n - 1) if last is None else last & (i == n - 1)
    return first, last


def _lru_fwd(xp, ga, cw, vecs, wrg, wig, block, *, name):
    bsz, seq, _ = xp.shape
    groups = seq // 8

    def body(xp_ref, ga_ref, cw_ref, vec_ref, wrg_ref, wig_ref, blk_ref, xb_ref, hs_ref, y_ref, all_ref,
             xpad, a_s, b_s, send_sems, recv_sems, local_sem):
        first, last = _first_last((RNN_BLOCKS, bsz))

        @pl.when(first)
        def _():
            _gather_two_level(blk_ref, all_ref, send_sems, recv_sems, local_sem, phase=0)

        @pl.when((pl.program_id(0) == GATHER_FORWARD_STEP) & (pl.program_id(1) == 0))
        def _():
            _gather_two_level(blk_ref, all_ref, send_sems, recv_sems, local_sem, phase=1)

        xpad[0:8, :] = jnp.zeros((8, RNN_BW), F32)
        xpad[pl.ds(8, seq), :] = xp_ref[...]
        xb = _conv(xpad, cw_ref, seq) + vec_ref[0:1, :]
        xb_ref[...] = xb
        sp = _softplus(-vec_ref[3:4, :])
        _, i, a, mult, _ = _gates(xb, wrg_ref[...], wig_ref[...], vec_ref[1:2, :], vec_ref[2:3, :], sp)
        a_s[...] = a
        b_s[...] = mult * (i * xb)
        row = lax.broadcasted_iota(jnp.int32, (8, RNN_BW), 0)

        def group(g, h):
            r0 = pl.multiple_of(g * 8, 8)
            av = a_s[pl.ds(r0, 8), :]
            bv = b_s[pl.ds(r0, 8), :]
            for k in (1, 2, 4):
                m = row >= k
                bv = jnp.where(m, av * pltpu.roll(bv, k, axis=0) + bv, bv)
                av = jnp.where(m, av * pltpu.roll(av, k, axis=0), av)
            hs_ref[pl.ds(r0, 8), :] = av * h + bv
            return av[7:8, :] * h + bv[7:8, :]

        def groups_of(i, h):
            for u in range(SCAN_UNROLL):
                h = group(i * SCAN_UNROLL + u, h)
            return h

        lax.fori_loop(0, groups // SCAN_UNROLL, groups_of, jnp.zeros((1, RNN_BW), F32))
        gav = ga_ref[...]
        y_ref[...] = (hs_ref[...] * (gav * _sigmoid(gav))).astype(BF16)

        @pl.when(last)
        def _():
            _gather_two_level(blk_ref, all_ref, send_sems, recv_sems, local_sem, phase=2)

    sq = _seq_spec(seq)
    shape = (bsz, seq, D_RNN)
    return pl.pallas_call(
        body, grid=(RNN_BLOCKS, bsz),
        in_specs=[sq, sq, _chan_spec(8), _chan_spec(8), _GATE_W_SPEC, _GATE_W_SPEC, _ANY],
        out_specs=[sq, sq, sq, _ANY],
        out_shape=[jax.ShapeDtypeStruct(shape, F32), jax.ShapeDtypeStruct(shape, F32), jax.ShapeDtypeStruct(shape, BF16),
                   jax.ShapeDtypeStruct((N_DEV,) + block.shape, block.dtype)],
        scratch_shapes=[pltpu.VMEM((seq + 8, RNN_BW), F32), pltpu.VMEM((seq, RNN_BW), F32), pltpu.VMEM((seq, RNN_BW), F32)]
        + _EXCHANGE_SEMS,
        compiler_params=_params(("arbitrary", "arbitrary")), name=name)(xp, ga, cw, vecs, wrg, wig, block)


def _lru_bwd(dy, xp, xb, hs, ga, cw, vecs, wrg, wig, parts, *, name):
    bsz, seq, _ = xp.shape
    groups = seq // 8

    def body(dy_ref, xp_ref, xb_ref, hs_ref, ga_ref, cw_ref, vec_ref, wrg_ref, wig_ref,
             parts_ref, dxp_ref, dga_ref, dwrg_ref, dwig_ref, dvec_ref, land_ref, pad, a_s, d_s, lam_s,
             send_sems, recv_sems, local_sem):
        first, last = _first_last((RNN_BLOCKS, bsz))

        @pl.when(first)
        def _():
            _exchange(parts_ref, land_ref, send_sems, recv_sems, local_sem, finish=False)

        @pl.when(pl.program_id(1) == 0)
        def _():
            dwrg_ref[...] = jnp.zeros_like(dwrg_ref)
            dwig_ref[...] = jnp.zeros_like(dwig_ref)
            dvec_ref[...] = jnp.zeros_like(dvec_ref)

        xb = xb_ref[...]
        hs = hs_ref[...]
        gav = ga_ref[...]
        dy = dy_ref[...]
        sp = _softplus(-vec_ref[3:4, :])
        wrg = wrg_ref[...]
        wig = wig_ref[...]
        r, i, a, mult, inv_mult = _gates(xb, wrg, wig, vec_ref[1:2, :], vec_ref[2:3, :], sp)
        sg = _sigmoid(gav)
        dga_ref[...] = (dy * hs * (sg * (1.0 + gav * (1.0 - sg)))).astype(BF16)
        d_s[...] = dy * (gav * sg)

        pad[pl.ds(0, seq), :] = a
        pad[pl.ds(seq, 8), :] = jnp.zeros((8, RNN_BW), F32)
        a_s[...] = pad[pl.ds(1, seq), :]
        row = lax.broadcasted_iota(jnp.int32, (8, RNN_BW), 0)

        def group(g, nxt):
            r0 = pl.multiple_of((groups - 1 - g) * 8, 8)
            cv = a_s[pl.ds(r0, 8), :]
            bv = d_s[pl.ds(r0, 8), :]
            for k in (1, 2, 4):
                m = row < 8 - k
                bv = jnp.where(m, cv * pltpu.roll(bv, 8 - k, axis=0) + bv, bv)
                cv = jnp.where(m, cv * pltpu.roll(cv, 8 - k, axis=0), cv)
            lam_s[pl.ds(r0, 8), :] = cv * nxt + bv
            return cv[0:1, :] * nxt + bv[0:1, :]

        def groups_of(i, nxt):
            for u in range(SCAN_UNROLL):
                nxt = group(i * SCAN_UNROLL + u, nxt)
            return nxt

        lax.fori_loop(0, groups // SCAN_UNROLL, groups_of, jnp.zeros((1, RNN_BW), F32))
        dh = lam_s[...]

        pad[0:8, :] = jnp.zeros((8, RNN_BW), F32)
        pad[pl.ds(8, seq), :] = hs
        da = dh * pad[pl.ds(7, seq), :]
        ixb = i * xb
        dixb = dh * mult
        dla = da * a - (dh * ixb) * (a * a) * inv_mult
        drp = (dla * ((-LRU_C) * sp)) * r * (1.0 - r)
        dip = (dixb * xb) * i * (1.0 - i)
        dvec_ref[0:1, :] += jnp.sum(drp, axis=0, keepdims=True)
        dvec_ref[1:2, :] += jnp.sum(dip, axis=0, keepdims=True)
        dvec_ref[2:3, :] += jnp.sum(dla * ((-LRU_C) * r), axis=0, keepdims=True)
        drpb = drp.astype(BF16)
        dipb = dip.astype(BF16)
        xbb = xb.astype(BF16)
        nt = (((1,), (1,)), ((), ()))
        tn = (((0,), (0,)), ((), ()))
        dxb = (dixb * i
               + lax.dot_general(drpb, wrg, nt, preferred_element_type=F32)
               + lax.dot_general(dipb, wig, nt, preferred_element_type=F32))
        dwrg_ref[...] += lax.dot_general(xbb, drpb, tn, preferred_element_type=F32)
        dwig_ref[...] += lax.dot_general(xbb, dipb, tn, preferred_element_type=F32)
        dvec_ref[3:4, :] += jnp.sum(dxb, axis=0, keepdims=True)

        pad[pl.ds(0, seq), :] = dxb
        pad[pl.ds(seq, 8), :] = jnp.zeros((8, RNN_BW), F32)
        dxp = cw_ref[0:1, :] * pad[pl.ds(CONV_WIDTH - 1, seq), :]
        for k in range(1, CONV_WIDTH):
            dxp = dxp + cw_ref[k:k + 1, :] * pad[pl.ds(CONV_WIDTH - 1 - k, seq), :]
        dxp_ref[...] = dxp.astype(BF16)
        pad[0:8, :] = jnp.zeros((8, RNN_BW), F32)
        pad[pl.ds(8, seq), :] = xp_ref[...]
        for k in range(CONV_WIDTH):
            dvec_ref[4 + k:5 + k, :] += jnp.sum(dxb * pad[pl.ds(8 - (CONV_WIDTH - 1) + k, seq), :], axis=0, keepdims=True)

        @pl.when(last)
        def _():
            _exchange(parts_ref, land_ref, send_sems, recv_sems, local_sem, finish=True)

    sq = _seq_spec(seq)
    shape = (bsz, seq, D_RNN)
    gshape = (RNN_BLOCKS, RNN_BW, RNN_BW)
    return pl.pallas_call(
        body, grid=(RNN_BLOCKS, bsz),
        in_specs=[sq, sq, sq, sq, sq, _chan_spec(8), _chan_spec(8), _GATE_W_SPEC, _GATE_W_SPEC, _ANY],
        out_specs=[sq, sq, _GATE_W_SPEC, _GATE_W_SPEC, _chan_spec(8), _ANY],
        out_shape=[jax.ShapeDtypeStruct(shape, BF16), jax.ShapeDtypeStruct(shape, BF16),
                   jax.ShapeDtypeStruct(gshape, F32), jax.ShapeDtypeStruct(gshape, F32),
                   jax.ShapeDtypeStruct((8, D_RNN), F32), jax.ShapeDtypeStruct(parts.shape, parts.dtype)],
        scratch_shapes=[pltpu.VMEM((seq + 8, RNN_BW), F32), pltpu.VMEM((seq, RNN_BW), F32),
                        pltpu.VMEM((seq, RNN_BW), F32), pltpu.VMEM((seq, RNN_BW), F32)] + _EXCHANGE_SEMS,
        compiler_params=_params(("arbitrary", "arbitrary")), name=name)(dy, xp, xb, hs, ga, cw, vecs, wrg, wig, parts)


def _attn_block(seq):
    return min(512, seq)


def _diag_mask(blk):
    return lax.broadcasted_iota(jnp.int32, (blk, blk), 0) <= lax.broadcasted_iota(jnp.int32, (blk, blk), 1)


FWD_HEADS = 4
BWD_HEADS = 2


def _attn_fwd(q, kn, kr, v_t, block, *, bsz, seq, name):
    t = bsz * seq
    blk = _attn_block(seq)
    nq = seq // blk
    hg = FWD_HEADS
    steps = (bsz, N_HEADS // hg, nq)

    def body(q_ref, kn_ref, kr_ref, vt_ref, blk_ref, o_ref, lse_ref, all_ref, acc, send_sems, recv_sems, local_sem):
        first, last = _first_last(steps)

        @pl.when(first)
        def _():
            _exchange(blk_ref, all_ref, send_sems, recv_sems, local_sem, finish=False, gather=True)

        qi = pl.program_id(2)
        acc[...] = jnp.zeros_like(acc)

        def step(j, carry, diagonal):
            k0 = pl.multiple_of(j * blk, blk)
            kr_j = kr_ref[pl.ds(k0, blk), :]
            out = []
            for h in range(hg):
                m_i, l_i = carry[h]
                kv = jnp.concatenate([kn_ref[pl.ds(k0, blk), h * QK_NOPE:(h + 1) * QK_NOPE], kr_j], axis=1)
                qv = q_ref[:, h * HEAD_PAD:(h + 1) * HEAD_PAD]
                s = lax.dot_general(kv, qv, _NT, preferred_element_type=F32) * ATTN_SCALE
                if diagonal:
                    s = jnp.where(_diag_mask(blk), s, -jnp.inf)
                m_new = jnp.maximum(m_i, jnp.max(s, axis=0, keepdims=True))
                p = jnp.exp(s - m_new)
                alpha = jnp.exp(m_i - m_new)
                l_new = alpha * l_i + jnp.sum(p, axis=0, keepdims=True)
                acc[h] = alpha * acc[h] + jnp.dot(vt_ref[h * V_DIM:(h + 1) * V_DIM, pl.ds(k0, blk)], p.astype(BF16),
                                                  preferred_element_type=F32)
                out.append((m_new, l_new))
            return tuple(out)

        init = tuple((jnp.full((1, blk), -jnp.inf, F32), jnp.zeros((1, blk), F32)) for _ in range(hg))
        carry = lax.fori_loop(0, qi, lambda j, c: step(j, c, False), init)
        stats = step(qi, carry, True)
        for h in range(hg):
            m_i, l_i = stats[h]
            o_ref[:, h * V_DIM:(h + 1) * V_DIM] = (acc[h] / l_i).T
            lse_ref[h] = m_i + jnp.log(l_i)

        @pl.when(last)
        def _():
            _exchange(blk_ref, all_ref, send_sems, recv_sems, local_sem, finish=True, gather=True)

    return pl.pallas_call(
        body, grid=steps,
        in_specs=[pl.BlockSpec((blk, hg * HEAD_PAD), lambda b, g, i: (b * nq + i, g)),
                  pl.BlockSpec((seq, hg * QK_NOPE), lambda b, g, i: (b, g)),
                  pl.BlockSpec((seq, LANES), lambda b, g, i: (b, 0)),
                  pl.BlockSpec((hg * V_DIM, seq), lambda b, g, i: (g, b)), _ANY],
        out_specs=[pl.BlockSpec((blk, hg * V_DIM), lambda b, g, i: (b * nq + i, g)),
                   pl.BlockSpec((hg, 1, blk), lambda b, g, i: (g, 0, b * nq + i)), _ANY],
        out_shape=[jax.ShapeDtypeStruct((t, N_HEADS * V_DIM), F32), jax.ShapeDtypeStruct((N_HEADS, 1, t), F32),
                   jax.ShapeDtypeStruct((N_DEV,) + block.shape, block.dtype)],
        scratch_shapes=[pltpu.VMEM((hg, V_DIM, blk), F32)] + _EXCHANGE_SEMS,
        compiler_params=_params(("arbitrary", "arbitrary", "arbitrary")), name=name)(q, kn, kr, v_t, block)


def _attn_bwd(q, kn, kr, kn_t, kr_t, v, o, lse, do, cos, sin, parts, *, bsz, seq, name):
    t = bsz * seq
    blk = _attn_block(seq)
    nq = seq // blk
    hg = BWD_HEADS
    steps = (bsz, N_HEADS // hg)

    def body(q_ref, kn_ref, kr_ref, knt_ref, krt_ref, v_ref, o_ref, lse_ref, do_ref, cos_ref, sin_ref, parts_ref,
             dq_ref, dkn_ref, dkr_ref, dv_ref, land_ref, dqt_acc, dk_acc, dv_acc, send_sems, recv_sems, local_sem):
        first, last = _first_last(steps)

        @pl.when(first)
        def _():
            _exchange(parts_ref, land_ref, send_sems, recv_sems, local_sem, finish=False)

        dqt_acc[...] = jnp.zeros_like(dqt_acc)
        dk_acc[...] = jnp.zeros_like(dk_acc)
        dv_acc[...] = jnp.zeros_like(dv_acc)

        def q_block(i, _):
            q0 = pl.multiple_of(i * blk, blk)
            rows = []
            for h in range(hg):
                dov = do_ref[pl.ds(q0, blk), h * V_DIM:(h + 1) * V_DIM].astype(F32)
                dcol = jnp.sum(dov * o_ref[pl.ds(q0, blk), h * V_DIM:(h + 1) * V_DIM], axis=-1, keepdims=True)
                delta = jnp.broadcast_to(dcol, (blk, LANES)).T[0:1, :]
                rows.append((lse_ref[h, :, pl.ds(q0, blk)], delta))

            def pair(j, diagonal):
                k0 = pl.multiple_of(j * blk, blk)
                kr_j = kr_ref[pl.ds(k0, blk), :]
                krt_j = krt_ref[:, pl.ds(k0, blk)]
                for h in range(hg):
                    lse_i, delta = rows[h]
                    qv = q_ref[pl.ds(q0, blk), h * HEAD_PAD:(h + 1) * HEAD_PAD]
                    dov = do_ref[pl.ds(q0, blk), h * V_DIM:(h + 1) * V_DIM]
                    kv = jnp.concatenate([kn_ref[pl.ds(k0, blk), h * QK_NOPE:(h + 1) * QK_NOPE], kr_j], axis=1)
                    s = lax.dot_general(kv, qv, _NT, preferred_element_type=F32) * ATTN_SCALE
                    p = jnp.exp(s - lse_i)
                    if diagonal:
                        p = jnp.where(_diag_mask(blk), p, 0.0)
                    dv_acc[pl.ds(k0, blk), h * V_DIM:(h + 1) * V_DIM] += jnp.dot(
                        p.astype(BF16), dov, preferred_element_type=F32)
                    dp = lax.dot_general(v_ref[pl.ds(k0, blk), h * V_DIM:(h + 1) * V_DIM], dov, _NT,
                                         preferred_element_type=F32)
                    ds = (p * (dp - delta) * ATTN_SCALE).astype(BF16)
                    dk_acc[pl.ds(k0, blk), h * HEAD_PAD:(h + 1) * HEAD_PAD] += jnp.dot(ds, qv, preferred_element_type=F32)
                    base = h * HEAD_PAD
                    dqt_acc[base:base + QK_NOPE, pl.ds(q0, blk)] += jnp.dot(
                        knt_ref[h * QK_NOPE:(h + 1) * QK_NOPE, pl.ds(k0, blk)], ds, preferred_element_type=F32)
                    dqt_acc[base + QK_NOPE:base + HEAD_PAD, pl.ds(q0, blk)] += jnp.dot(
                        krt_j, ds, preferred_element_type=F32)

            def off_diagonal(j, _):
                pair(j, False)
                return 0

            lax.fori_loop(0, i, off_diagonal, 0)
            pair(i, True)
            return 0

        lax.fori_loop(0, nq, q_block, 0)
        dkr = jnp.zeros((seq, LANES), F32)
        for h in range(hg):
            base = h * HEAD_PAD
            for i in range(nq):
                rows = slice(i * blk, (i + 1) * blk)
                dq = dqt_acc[base:base + HEAD_PAD, rows].T
                dq_ref[rows, base:base + QK_NOPE] = dq[:, :QK_NOPE].astype(BF16)
                dq_ref[rows, base + QK_NOPE:base + HEAD_PAD] = _rope_t(
                    dq[:, QK_NOPE:], cos_ref[rows, :], sin_ref[rows, :]).astype(BF16)
            dkn_ref[:, h * QK_NOPE:(h + 1) * QK_NOPE] = dk_acc[:, base:base + QK_NOPE].astype(BF16)
            dkr = dkr + dk_acc[:, base + QK_NOPE:base + HEAD_PAD]
        dv_ref[...] = dv_acc[...].astype(BF16)

        @pl.when(pl.program_id(1) == 0)
        def _():
            dkr_ref[...] = jnp.zeros_like(dkr_ref)

        dkr_ref[...] += _rope_t(dkr, cos_ref[...], sin_ref[...])

        @pl.when(last)
        def _():
            _exchange(parts_ref, land_ref, send_sems, recv_sems, local_sem, finish=True)

    head = pl.BlockSpec((seq, hg * V_DIM), lambda b, g: (b, g))
    head_t = pl.BlockSpec((hg * V_DIM, seq), lambda b, g: (g, b))
    shared = pl.BlockSpec((seq, LANES), lambda b, g: (b, 0))
    shared_t = pl.BlockSpec((LANES, seq), lambda b, g: (0, b))
    table = pl.BlockSpec((seq, LANES), lambda b, g: (0, 0))
    qspec = pl.BlockSpec((seq, hg * HEAD_PAD), lambda b, g: (b, g))
    return pl.pallas_call(
        body, grid=steps,
        in_specs=[qspec, head, shared, head_t, shared_t, head, head,
                  pl.BlockSpec((hg, 1, seq), lambda b, g: (g, 0, b)), head, table, table, _ANY],
        out_specs=[qspec, head, shared, head, _ANY],
        out_shape=[jax.ShapeDtypeStruct((t, N_HEADS * HEAD_PAD), BF16), jax.ShapeDtypeStruct((t, N_HEADS * QK_NOPE), BF16),
                   jax.ShapeDtypeStruct((t, LANES), F32), jax.ShapeDtypeStruct((t, N_HEADS * V_DIM), BF16),
                   jax.ShapeDtypeStruct(parts.shape, parts.dtype)],
        scratch_shapes=[pltpu.VMEM((hg * HEAD_PAD, seq), F32), pltpu.VMEM((seq, hg * HEAD_PAD), F32),
                        pltpu.VMEM((seq, hg * V_DIM), F32)] + _EXCHANGE_SEMS,
        compiler_params=_params(("arbitrary", "arbitrary")), name=name)(
            q, kn, kr, kn_t, kr_t, v, o, lse, do, cos, sin, parts)


def _head_and_loss(o, g2, x1, target, w_out, g_final, *, name, bt=256):
    t, d = x1.shape
    bt = min(bt, t)
    nt = (((1,), (1,)), ((), ()))

    def body(o_ref, g2_ref, x1_ref, tgt_ref, w_ref, gf_ref, loss_ref, dx2_ref, y2_ref, do_ref, dg2_ref, dgf_ref):
        @pl.when(pl.program_id(0) == 0)
        def _():
            loss_ref[...] = jnp.zeros_like(loss_ref)
            dgf_ref[...] = jnp.zeros_like(dgf_ref)

        ov = o_ref[...]
        gv = g2_ref[...]
        sg = _sigmoid(gv)
        silu = gv * sg
        y2 = (ov * silu).astype(BF16)
        y2_ref[...] = y2
        w = w_ref[...]
        x2 = x1_ref[...] + jnp.dot(y2, w, preferred_element_type=F32)
        r = lax.rsqrt(jnp.mean(x2 * x2, axis=-1, keepdims=True) + EPS)
        nrm = x2 * r
        gf = gf_ref[...]
        err = nrm * gf - tgt_ref[...]
        loss_ref[...] += 0.5 * jnp.sum(jnp.mean(err * err, axis=-1, keepdims=True))
        dyf = err * (1.0 / d)
        dgf_ref[...] += jnp.sum(dyf * nrm, axis=0, keepdims=True)
        dn = dyf * gf
        dx2 = r * (dn - nrm * jnp.mean(dn * nrm, axis=-1, keepdims=True))
        dx2_ref[...] = dx2
        dy2 = lax.dot_general(dx2.astype(BF16), w, nt, preferred_element_type=F32)
        do_ref[...] = (dy2 * silu).astype(BF16)
        dg2_ref[...] = (dy2 * ov * (sg * (1.0 + gv * (1.0 - sg)))).astype(BF16)

    row = pl.BlockSpec((bt, d), lambda i: (i, 0))
    vec = pl.BlockSpec((1, d), lambda i: (0, 0))
    return pl.pallas_call(
        body, grid=(t // bt,),
        in_specs=[row, row, row, row, pl.BlockSpec((d, d), lambda i: (0, 0)), vec],
        out_specs=[pl.BlockSpec((8, LANES), lambda i: (0, 0)), row, row, row, row, vec],
        out_shape=[jax.ShapeDtypeStruct((8, LANES), F32), jax.ShapeDtypeStruct((t, d), F32),
                   jax.ShapeDtypeStruct((t, d), BF16), jax.ShapeDtypeStruct((t, d), BF16),
                   jax.ShapeDtypeStruct((t, d), BF16), jax.ShapeDtypeStruct((1, d), F32)],
        compiler_params=_params(("arbitrary",)), name=name)(o, g2, x1, target, w_out, g_final)


def _sum_parts(parts, *, name, br=GRAD_BLOCK):
    npart, rows, w = parts.shape

    def body(p_ref, o_ref):
        acc = p_ref[0].astype(F32)
        for j in range(1, npart):
            acc = acc + p_ref[j].astype(F32)
        o_ref[...] = acc

    return pl.pallas_call(
        body, grid=(rows // br,), in_specs=[pl.BlockSpec((npart, br, w), lambda i: (0, i, 0))],
        out_specs=pl.BlockSpec((br, w), lambda i: (i, 0)), out_shape=jax.ShapeDtypeStruct((rows, w), F32),
        compiler_params=_params(("parallel",)), name=name)(parts)


def _chip_partial(parts, recv, *, name, br=GRAD_BLOCK):
    _, rows, w = parts.shape
    core = lax.axis_index("c").astype(jnp.int32).reshape(1)

    def body(c_ref, p_ref, r_ref, o_ref):
        o_ref[...] = (p_ref[...] + r_ref[...]).astype(BF16)

    grid_spec = pltpu.PrefetchScalarGridSpec(
        num_scalar_prefetch=1, grid=(4, rows // br),
        in_specs=[pl.BlockSpec((None, br, w), lambda k, i, c_ref: (2 * k + c_ref[0], i, 0)),
                  pl.BlockSpec((None, br, w), lambda k, i, c_ref: (k, i, 0))],
        out_specs=pl.BlockSpec((None, br, w), lambda k, i, c_ref: (k, i, 0)))
    return pl.pallas_call(
        body, grid_spec=grid_spec, out_shape=jax.ShapeDtypeStruct((4, rows, w), BF16),
        compiler_params=_params(("parallel", "parallel")), name=name)(core, parts, recv)


def _as_block(a):
    if a.ndim == 1:
        return a.reshape(1, -1)
    if a.ndim > 2 and a.shape[0] == 1:
        return a.reshape(a.shape[1:])
    return a


def _adamw(g, w, m, v, *, name):
    shape = w.shape
    g, w, m, v = (_as_block(a) for a in (g, w, m, v))

    def body(g_ref, w_ref, m_ref, v_ref, d_ref, nm_ref, nv_ref):
        gv = g_ref[...]
        nm = ADAM_B1 * m_ref[...] + (1.0 - ADAM_B1) * gv
        nv = ADAM_B2 * v_ref[...] + (1.0 - ADAM_B2) * (gv * gv)
        nm_ref[...] = nm
        nv_ref[...] = nv
        m_hat = nm / (1.0 - ADAM_B1 ** ADAM_STEP)
        v_hat = nv / (1.0 - ADAM_B2 ** ADAM_STEP)
        d_ref[...] = (-ADAM_LR) * (m_hat / (jnp.sqrt(v_hat) + ADAM_EPS) + ADAM_WD * w_ref[...])

    whole = pl.BlockSpec(memory_space=pltpu.VMEM)
    outs = pl.pallas_call(
        body, in_specs=[whole] * 4, out_specs=[whole] * 3, out_shape=[jax.ShapeDtypeStruct(w.shape, F32)] * 3,
        compiler_params=_params(), name=name)(g, w, m, v)
    return [o.reshape(shape) for o in outs]


def _all_gather(block, *, name):
    m, n = block.shape

    def body(x_ref, out_ref, send_sems, recv_sems, local_sem):
        for phase in range(3):
            _gather_two_level(x_ref, out_ref, send_sems, recv_sems, local_sem, phase=phase)

    return pl.pallas_call(
        body, out_shape=jax.ShapeDtypeStruct((N_DEV, m, n), block.dtype), in_specs=[_ANY], out_specs=_ANY,
        scratch_shapes=_EXCHANGE_SEMS, name=name)(block)


def _exchange_d2d(parts, *, name):
    _, rows, w = parts.shape

    def body(p_ref, land_ref, send_sems, recv_sems):
        x, y, c = _mesh_pos()
        sends = []
        for k in range(4):
            cp = pltpu.make_async_remote_copy(
                src_ref=p_ref.at[2 * k + (1 - c)], dst_ref=land_ref.at[k], send_sem=send_sems.at[k],
                recv_sem=recv_sems.at[k], device_id=(x, y, 1 - c), device_id_type=pl.DeviceIdType.MESH)
            cp.start()
            sends.append(cp)
        for cp in sends:
            cp.wait_recv()
        for cp in sends:
            cp.wait_send()

    return pl.pallas_call(
        body, out_shape=jax.ShapeDtypeStruct((4, rows, w), parts.dtype), in_specs=[_ANY], out_specs=_ANY,
        scratch_shapes=[pltpu.SemaphoreType.DMA((4,)), pltpu.SemaphoreType.DMA((4,))], name=name)(parts)


def _exchange_ici(parts, *, name):
    def body(p_ref, land_ref, send_sems, recv_sems, local_sem):
        x, y, c = _mesh_pos()
        mine = pltpu.make_async_copy(p_ref.at[2 * x + y], land_ref.at[3], local_sem)
        mine.start()
        sends = []
        for k, (px, py) in enumerate([(1 - x, y), (x, 1 - y), (1 - x, 1 - y)]):
            cp = pltpu.make_async_remote_copy(
                src_ref=p_ref.at[2 * px + py], dst_ref=land_ref.at[k], send_sem=send_sems.at[k],
                recv_sem=recv_sems.at[k], device_id=(px, py, c), device_id_type=pl.DeviceIdType.MESH)
            cp.start()
            sends.append(cp)
        for cp in sends:
            cp.wait_recv()
        for cp in sends:
            cp.wait_send()
        mine.wait()

    return pl.pallas_call(
        body, out_shape=jax.ShapeDtypeStruct(parts.shape, parts.dtype), in_specs=[_ANY], out_specs=_ANY,
        scratch_shapes=[pltpu.SemaphoreType.DMA((3,)), pltpu.SemaphoreType.DMA((3,)), pltpu.SemaphoreType.DMA(())],
        name=name)(parts)


def _rows(a):
    return a.reshape(-1, PACK_W)


def _pad_to(a, n):
    return jnp.pad(a, (0, n - a.shape[0]))


def _weight_blocks(d):
    small = _rows(_pad_to(jnp.concatenate([d[n].reshape(-1) for n, _ in _SMALL]), 8 * PACK_W))
    block_a = jnp.concatenate([d["w_in_a"][0].T.astype(WIRE), lax.bitcast_convert_type(small, WIRE).reshape(16, PACK_W)],
                              axis=0)
    w_uq = jnp.pad(d["w_uq"][0], ((0, 0), (0, 0), (0, HEAD_PAD - QK_NOPE - QK_ROPE)))
    pieces = {"w_out_a": d["w_out_a"], "w_dkv": d["w_dkv"], "w_uk": d["w_uk"], "w_uv": d["w_uv"],
              "w_in_b": d["w_in_b"][0].T, "w_uq": w_uq}
    block_b = jnp.concatenate([_rows(pieces[n]) for n, _ in _PIECES_B]
                              + [jnp.zeros((WIRE_ROWS_B - MATRIX_ROWS_B, PACK_W), F32)], axis=0).astype(WIRE)
    return block_a, block_b, d["w_out_b"][0].astype(WIRE)


def _weights_a(wall):
    w = {}
    lo, hi = _OFF_A["w_in_a"]
    w["w_in_a_t"] = wall[:, lo:hi].reshape(2 * D_RNN, D_MODEL)
    small = lax.bitcast_convert_type(wall[:, MATRIX_ROWS_A:].reshape(N_DEV, 8 * PACK_W, 2), F32)
    off = dict(zip([n for n, _ in _SMALL], [0, 128, 768, 928, 1088, 1248]))
    w["norm_a"] = small[:, :128].reshape(1, D_MODEL)

    def by_channel(lo, rows):
        a = small[:, lo:lo + rows * (D_RNN // N_DEV)].reshape(N_DEV, rows, -1).transpose(1, 0, 2).reshape(rows, D_RNN)
        return jnp.pad(a, ((0, 8 - rows), (0, 0)))

    w["conv_taps"] = by_channel(off["conv_w"], CONV_WIDTH)
    w["lru_vecs"] = by_channel(off["conv_b"], 4)
    return w


def _weights_b(wall):
    piece = {n: wall[:, lo:hi] for n, (lo, hi) in _OFF_B.items()}
    w = {"w_out_a": piece["w_out_a"].reshape(D_RNN, D_MODEL)}
    w_dkv = piece["w_dkv"].reshape(D_MODEL, KV_RANK + QK_ROPE)
    w["w_dkv_c"] = w_dkv[:, :KV_RANK]
    w["w_dkv_r"] = jnp.pad(w_dkv[:, KV_RANK:], ((0, 0), (0, LANES - QK_ROPE)))
    w["w_uk"] = piece["w_uk"].reshape(KV_RANK, N_HEADS * QK_NOPE)
    w["w_uv"] = piece["w_uv"].reshape(KV_RANK, N_HEADS * V_DIM)
    w["w_uk_t"], w["w_uv_t"] = w["w_uk"].T, w["w_uv"].T
    w["w_in_b_t"] = piece["w_in_b"].reshape(Q_RANK + N_HEADS * V_DIM, D_MODEL)
    w["w_uq"] = piece["w_uq"].reshape(Q_RANK, N_HEADS * HEAD_PAD)
    return w


def _pack_rep(d):
    flat = jnp.concatenate([d[n].reshape(-1) for n, _ in _REP])
    return _rows(_pad_to(flat, REP_ROWS * PACK_W))


def _unpack_rep(p, like):
    flat = p.reshape(-1)
    out, off = {}, 0
    for n, k in _REP:
        out[n] = flat[off:off + k].reshape(like[n].shape)
        off += k
    return out


def _by_owner(a):
    return a.reshape(N_DEV, -1, PACK_W)


def _grad_parts_b(g):
    tail = jnp.zeros((N_DEV, WIRE_ROWS_B - MATRIX_ROWS_B, PACK_W), F32)
    return jnp.concatenate([_by_owner(g[n]) for n, _ in _PIECES_B] + [tail], axis=1).astype(BF16)


def _grad_parts_a(g):
    small = jnp.concatenate([
        g["norm_a"].reshape(N_DEV, -1),
        g["conv_w"].reshape(CONV_WIDTH, N_DEV, -1).transpose(1, 0, 2).reshape(N_DEV, -1),
        g["conv_b"].reshape(N_DEV, -1), g["b_rg"].reshape(N_DEV, -1), g["b_ig"].reshape(N_DEV, -1),
        g["lru_lambda"].reshape(N_DEV, -1)], axis=1)
    small = jnp.pad(small, ((0, 0), (0, 8 * PACK_W - small.shape[1]))).reshape(N_DEV, 8, PACK_W)
    half = N_DEV // 2
    w_in_a = jnp.concatenate([h.reshape(half, -1, PACK_W) for h in g["w_in_a_t"]], axis=0)
    rep = _pack_rep(g).reshape(N_DEV, REP_SLICE, PACK_W)
    tail = jnp.zeros((N_DEV, GRAD_ROWS_A - MATRIX_ROWS_A - 8 - REP_SLICE, PACK_W), F32)
    return jnp.concatenate([w_in_a, small, rep, tail], axis=1)


def _own_grads(sum_a, sum_b, sum_c):
    out = {}
    lo, hi = _OFF_A["w_in_a"]
    out["w_in_a"] = sum_a[lo:hi].T.reshape(1, D_MODEL, 2 * D_RNN // N_DEV)
    small = sum_a[MATRIX_ROWS_A:MATRIX_ROWS_A + 8].reshape(-1)
    shapes = {"norm_a": (1, D_MODEL // N_DEV), "conv_w": (1, CONV_WIDTH, D_RNN // N_DEV), "conv_b": (1, D_RNN // N_DEV),
              "b_rg": (1, D_RNN // N_DEV), "b_ig": (1, D_RNN // N_DEV), "lru_lambda": (1, D_RNN // N_DEV)}
    off = 0
    for n, k in _SMALL:
        out[n] = small[off:off + k].reshape(shapes[n])
        off += k
    piece = {n: sum_b[lo:hi] for n, (lo, hi) in _OFF_B.items()}
    out["w_out_a"] = piece["w_out_a"].reshape(1, D_RNN // N_DEV, D_MODEL)
    out["w_dkv"] = piece["w_dkv"].reshape(D_MODEL // N_DEV, KV_RANK + QK_ROPE)
    out["w_uk"] = piece["w_uk"].reshape(KV_RANK // N_DEV, N_HEADS, QK_NOPE)
    out["w_uv"] = piece["w_uv"].reshape(KV_RANK // N_DEV, N_HEADS, V_DIM)
    out["w_in_b"] = piece["w_in_b"].T.reshape(1, D_MODEL, (Q_RANK + N_HEADS * V_DIM) // N_DEV)
    out["w_uq"] = piece["w_uq"].reshape(1, Q_RANK // N_DEV, N_HEADS, HEAD_PAD)[..., :QK_NOPE + QK_ROPE]
    out["w_out_b"] = sum_c.reshape(1, N_HEADS * V_DIM // N_DEV, D_MODEL)
    return out


def _step(x, target, w, rep, block_b, block_c, *, bsz, seq):
    t = bsz * seq
    cos, sin = _rope_tables(seq)
    g_a = w["norm_a"]
    g_kv = rep["norm_kv"].reshape(1, -1)
    g_kvn = rep["kv_norm"].reshape(1, -1)
    g_b = rep["norm_b"].reshape(1, -1)
    g_q = rep["q_norm"].reshape(1, -1)
    g_f = rep["final_norm"].reshape(1, -1)
    wrg = rep["w_rg"][0].astype(BF16)
    wig = rep["w_ig"][0].astype(BF16)
    cw8, vecs = w["conv_taps"], w["lru_vecs"]

    def seq3(a):
        return a.reshape(bsz, seq, a.shape[-1])

    def flat(a):
        return a.reshape(t, a.shape[-1])

    h0, xp, ga = _lru_proj_fwd(x, g_a, w["w_in_a_t"], name="lru_proj_fwd")
    xb, hs, y, wall_b = _lru_fwd(seq3(xp), seq3(ga), cw8, vecs, wrg, wig, block_b, name="lru_fwd")
    w = dict(w, **_weights_b(wall_b))
    x1 = _matmul(flat(y), w["w_out_a"], residual=x, name="out_a")
    hk, hq, ck, cqp, g2, ckv, cq, q, kn, v, kr, kn_t, v_t, kr_t = _mla_proj_fwd(
        x1, (g_kv, g_b, g_kvn, g_q), w, cos, sin, seq=seq, name="mla_proj_fwd")
    o, lse, wall_c = _attn_fwd(q, kn, kr, v_t, block_c, bsz=bsz, seq=seq, name="attn_fwd")
    w_out_b = wall_c.reshape(N_HEADS * V_DIM, D_MODEL)
    loss, dx2, y2, do, dg2, dgf = _head_and_loss(o, g2, x1, target, w_out_b, g_f, name="head_loss")
    grads = {"final_norm": dgf}
    parts_c = _by_owner(_matmul_tn(y2, dx2, name="d_w_out_b")).astype(BF16)
    dq, dkn, dkr, dv, landed_c = _attn_bwd(q, kn, kr, kn_t, kr_t, v, o, lse, do, cos, sin, parts_c,
                                           bsz=bsz, seq=seq, name="attn_bwd")
    grads["w_uq"] = _matmul_tn(cq, dq, name="d_w_uq")
    dx1, du2, dckr, dgkv, dgb, dgkvn, dgq = _mla_proj_bwd(
        x1, dx2, cqp, ck, dq, dkn, dv, dkr, dg2, (g_kv, g_b, g_kvn, g_q), w, name="mla_proj_bwd")
    grads["norm_kv"], grads["norm_b"], grads["kv_norm"], grads["q_norm"] = dgkv, dgb, dgkvn, dgq
    grads["w_in_b"] = _matmul_tn(du2, hq, name="d_w_in_b_t")
    grads["w_uk"] = _matmul_tn(ckv, dkn, name="d_w_uk")
    grads["w_uv"] = _matmul_tn(ckv, dv, name="d_w_uv")
    grads["w_dkv"] = _matmul_tn(hk, dckr, name="d_w_dkv")[:, :KV_RANK + QK_ROPE]
    grads["w_out_a"] = _matmul_tn(flat(y), dx1, name="d_w_out_a")
    parts_b = _grad_parts_b(grads)
    dy = _matmul(dx1, w["w_out_a"], nt=True, name="d_y")
    dxp, dga, dwrg, dwig, dvec, landed_b = _lru_bwd(
        seq3(dy), seq3(xp), xb, hs, seq3(ga), cw8, vecs, wrg, wig, parts_b, name="lru_bwd")
    dxp, dga = flat(dxp), flat(dga)
    grads["w_rg"], grads["w_ig"] = dwrg, dwig
    grads["b_rg"], grads["b_ig"], grads["conv_b"] = dvec[0], dvec[1], dvec[3]
    lam = vecs[3]
    grads["lru_lambda"] = dvec[2] * (-1.0 / (1.0 + jnp.exp(lam)))
    grads["conv_w"] = dvec[4:4 + CONV_WIDTH]
    grads["w_in_a_t"] = (_matmul_tn(dxp, h0, name="d_w_in_a_x_t"), _matmul_tn(dga, h0, name="d_w_in_a_g_t"))
    dx, dga_norm = _lru_proj_bwd(dxp, dga, x, dx1, g_a, w["w_in_a_t"], name="lru_proj_bwd")
    grads["norm_a"] = dga_norm
    return loss[0, 0], dx, grads, landed_b, landed_c


def kernel(x, norm_a, w_in_a, conv_w, conv_b, w_rg, b_rg, w_ig, b_ig, lru_lambda, w_out_a, norm_kv, w_dkv, kv_norm, w_uk, w_uv, norm_b, w_in_b, q_norm, w_uq, w_out_b, final_norm, loss_target, m_norm_a, m_w_in_a, m_conv_w, m_conv_b, m_w_rg, m_b_rg, m_w_ig, m_b_ig, m_lru_lambda, m_w_out_a, m_norm_kv, m_w_dkv, m_kv_norm, m_w_uk, m_w_uv, m_norm_b, m_w_in_b, m_q_norm, m_w_uq, m_w_out_b, m_final_norm, v_norm_a, v_w_in_a, v_conv_w, v_conv_b, v_w_rg, v_b_rg, v_w_ig, v_b_ig, v_lru_lambda, v_w_out_a, v_norm_kv, v_w_dkv, v_kv_norm, v_w_uk, v_w_uv, v_norm_b, v_w_in_b, v_q_norm, v_w_uq, v_w_out_b, v_final_norm):
    given = dict(locals())
    wts = {n: given[n] for n in WEIGHTS}
    mom1 = {n: given["m_" + n] for n in WEIGHTS}
    mom2 = {n: given["v_" + n] for n in WEIGHTS}
    bsz, seq, _ = x.shape
    t = bsz * seq

    block_a, block_b, block_c = _weight_blocks(wts)
    w = _weights_a(_all_gather(block_a, name="gather_weights_a"))
    loss, dx, grads, landed_b, landed_c = _step(x.reshape(t, D_MODEL), loss_target.reshape(t, D_MODEL), w, wts,
                                                block_b, block_c, bsz=bsz, seq=seq)
    loss = lax.psum(loss, MESH_AXES)

    parts_a = _grad_parts_a(grads)
    from_sibling = _exchange_d2d(parts_a, name="exchange_grads_d2d")
    chip_parts = _chip_partial(parts_a, from_sibling, name="chip_partial_grads")
    landed_a = _exchange_ici(chip_parts, name="exchange_grads_ici")
    sum_a = _sum_parts(landed_a, name="sum_grads_a", br=GRAD_BLOCK)
    sum_b = _sum_parts(landed_b, name="sum_grads_b", br=WIRE_ROWS_B // 2)
    sum_c = _sum_parts(landed_c, name="sum_grads_c", br=landed_c.shape[1])
    g_own = _own_grads(sum_a, sum_b, sum_c)
    rep_slice = sum_a[MATRIX_ROWS_A + 8:MATRIX_ROWS_A + 8 + REP_SLICE]
    g_rep = _all_gather(rep_slice, name="gather_replicated").reshape(REP_ROWS, PACK_W)
    g_own.update(_unpack_rep(g_rep, wts))

    deltas, new_m, new_v = {}, {}, {}
    for n in WEIGHTS:
        deltas[n], new_m[n], new_v[n] = _adamw(g_own[n], wts[n], mom1[n], mom2[n], name="adamw_" + n)
    result = [loss, dx.reshape(bsz, seq, D_MODEL)]
    for d in (g_own, deltas, new_m, new_v):
        result.extend(d[n] for n in WEIGHTS)
    return tuple(result)
```

```python
import jax
import jax.numpy as jnp
from jax import lax
from jax.experimental import pallas as pl
from jax.experimental.pallas import tpu as pltpu

F32 = jnp.float32
BF16 = jnp.bfloat16
WIRE = jnp.bfloat16

D_MODEL = 1024
D_RNN = 1280
RNN_BLOCKS = 10
RNN_BW = 128
CONV_WIDTH = 4
LRU_C = 8.0
N_HEADS = 8
QK_NOPE = 128
QK_ROPE = 64
V_DIM = 128
KV_RANK = 256
Q_RANK = 384
ROPE_THETA = 10000.0
EPS = 1e-6
ATTN_SCALE = (QK_NOPE + QK_ROPE) ** -0.5
HEAD_PAD = 256
LANES = 128

ADAM_LR = 0.001
ADAM_B1 = 0.9
ADAM_B2 = 0.999
ADAM_EPS = 1e-08
ADAM_WD = 0.01
ADAM_STEP = 10

N_DEV = 8
MESH_AXES = ("x", "y", "c")
VMEM_LIMIT_BYTES = 56 * 2**20
PACK_W = 1024

_PIECES_A = (("w_in_a", 320),)
_PIECES_B = (("w_out_a", 160), ("w_dkv", 40), ("w_uk", 32), ("w_uv", 32), ("w_in_b", 176), ("w_uq", 96))


def _offsets(pieces):
    off, r = {}, 0
    for n, k in pieces:
        off[n] = (r, r + k)
        r += k
    return off, r


_OFF_A, MATRIX_ROWS_A = _offsets(_PIECES_A)
_OFF_B, MATRIX_ROWS_B = _offsets(_PIECES_B)
WIRE_ROWS_A = MATRIX_ROWS_A + 16
WIRE_ROWS_B = 544
_SMALL = (("norm_a", 128), ("conv_w", 640), ("conv_b", 160), ("b_rg", 160), ("b_ig", 160), ("lru_lambda", 160))
_REP = (("w_rg", 163840), ("w_ig", 163840), ("norm_kv", 1024), ("kv_norm", 256), ("norm_b", 1024),
        ("q_norm", 384), ("final_norm", 1024))
REP_ROWS = 384
REP_SLICE = REP_ROWS // N_DEV
GRAD_ROWS_A = 384
GRAD_BLOCK = 192

WEIGHTS = ("norm_a", "w_in_a", "conv_w", "conv_b", "w_rg", "b_rg", "w_ig", "b_ig", "lru_lambda", "w_out_a",
           "norm_kv", "w_dkv", "kv_norm", "w_uk", "w_uv", "norm_b", "w_in_b", "q_norm", "w_uq", "w_out_b",
           "final_norm")


def _params(sem=None):
    return pltpu.CompilerParams(dimension_semantics=sem, vmem_limit_bytes=VMEM_LIMIT_BYTES)


_NT = (((1,), (1,)), ((), ()))
_ANY = pl.BlockSpec(memory_space=pl.ANY)


def _mesh_pos():
    return lax.axis_index("x"), lax.axis_index("y"), lax.axis_index("c")


def _sigmoid(z):
    return 0.5 * jnp.tanh(0.5 * z) + 0.5


def _sigmoid_tail(z):
    return 1.0 / (1.0 + jnp.exp(-z))


def _col_block(n):
    return n if n <= 1408 else n // 2


def _matmul(a, b, *, name, nt=False, out_dtype=F32, residual=None, bm=512):
    m, k = a.shape
    n = b.shape[0] if nt else b.shape[1]
    bm = min(bm, m)
    bn = _col_block(n)
    dims = (((1,), (1,)), ((), ())) if nt else (((1,), (0,)), ((), ()))
    has_res = residual is not None

    def body(*refs):
        a_ref, b_ref, o_ref = refs[0], refs[1], refs[-1]
        acc = lax.dot_general(a_ref[...].astype(BF16), b_ref[...].astype(BF16), dims, preferred_element_type=F32)
        if has_res:
            acc = acc + refs[2][...]
        o_ref[...] = acc.astype(out_dtype)

    in_specs = [pl.BlockSpec((bm, k), lambda i, j: (i, 0)),
                pl.BlockSpec((bn, k), lambda i, j: (j, 0)) if nt else pl.BlockSpec((k, bn), lambda i, j: (0, j))]
    args = [a, b]
    if has_res:
        in_specs.append(pl.BlockSpec((bm, bn), lambda i, j: (i, j)))
        args.append(residual)
    return pl.pallas_call(
        body, grid=(m // bm, n // bn), in_specs=in_specs, out_specs=pl.BlockSpec((bm, bn), lambda i, j: (i, j)),
        out_shape=jax.ShapeDtypeStruct((m, n), out_dtype), compiler_params=_params(("parallel", "parallel")),
        name=name)(*args)


def _matmul_tn(a, b, *, name, bt=512):
    t, m = a.shape
    n = b.shape[1]
    bt = min(bt, t)
    bm, bn = _col_block(m), _col_block(n)

    def body(a_ref, b_ref, o_ref):
        @pl.when(pl.program_id(2) == 0)
        def _():
            o_ref[...] = jnp.zeros_like(o_ref)

        o_ref[...] += lax.dot_general(a_ref[...].astype(BF16), b_ref[...].astype(BF16),
                                      (((0,), (0,)), ((), ())), preferred_element_type=F32)

    return pl.pallas_call(
        body, grid=(m // bm, n // bn, t // bt),
        in_specs=[pl.BlockSpec((bt, bm), lambda i, j, s: (s, i)), pl.BlockSpec((bt, bn), lambda i, j, s: (s, j))],
        out_specs=pl.BlockSpec((bm, bn), lambda i, j, s: (i, j)),
        out_shape=jax.ShapeDtypeStruct((m, n), F32),
        compiler_params=_params(("parallel", "parallel", "arbitrary")), name=name)(a, b)


def _swap_halves(v):
    ax = v.ndim - 1
    lane = lax.broadcasted_iota(jnp.int32, v.shape, ax)
    up = pltpu.roll(v, LANES - QK_ROPE // 2, axis=ax)
    down = pltpu.roll(v, QK_ROPE // 2, axis=ax)
    return jnp.where(lane < QK_ROPE // 2, up, jnp.where(lane < QK_ROPE, down, 0.0))


def _rope(v, cos, sin):
    return v * cos + _swap_halves(v) * sin


def _rope_t(d, cos, sin):
    return d * cos + _swap_halves(d * sin)


def _rope_tables(seq):
    pos = jnp.arange(seq, dtype=F32)
    inv = ROPE_THETA ** (-jnp.arange(0, QK_ROPE, 2, dtype=F32) / QK_ROPE)
    ang = pos[:, None] * inv[None, :]
    cos, sin = jnp.cos(ang), jnp.sin(ang)
    zero = jnp.zeros((seq, LANES - QK_ROPE), F32)
    return jnp.concatenate([cos, cos, zero], axis=1), jnp.concatenate([-sin, sin, zero], axis=1)


def _rms(v):
    return v * lax.rsqrt(jnp.mean(v * v, axis=-1, keepdims=True) + EPS)


def _const_spec(a):
    return pl.BlockSpec(a.shape, lambda i: (0,) * a.ndim)


def _lru_proj_fwd(x, g_a, w_in_t, *, name, bt=256):
    t, d = x.shape
    bt = min(bt, t)
    n = w_in_t.shape[0] // 2

    def body(x_ref, g_ref, wt_ref, h_ref, xp_ref, ga_ref):
        h = (_rms(x_ref[...]) * g_ref[...]).astype(BF16)
        h_ref[...] = h
        xp_ref[...] = lax.dot_general(h, wt_ref[0:n, :], _NT, preferred_element_type=F32)
        ga_ref[...] = lax.dot_general(h, wt_ref[n:2 * n, :], _NT, preferred_element_type=F32)

    row = lambda w: pl.BlockSpec((bt, w), lambda i: (i, 0))
    return pl.pallas_call(
        body, grid=(t // bt,), in_specs=[row(d), _const_spec(g_a), _const_spec(w_in_t)],
        out_specs=[row(d), row(n), row(n)],
        out_shape=[jax.ShapeDtypeStruct((t, d), BF16), jax.ShapeDtypeStruct((t, n), F32), jax.ShapeDtypeStruct((t, n), F32)],
        compiler_params=_params(("parallel",)), name=name)(x, g_a, w_in_t)


def _mla_proj_fwd(x1, gains, w, cos, sin, *, seq, name, bt=256):
    t, d = x1.shape
    bt = min(bt, seq)
    per_seq = seq // bt
    g_kv, g_b, g_kvn, g_q = gains
    consts = [g_kv, g_b, g_kvn, g_q, w["w_dkv_c"], w["w_dkv_r"], w["w_in_b_t"], w["w_uk"], w["w_uv"],
              w["w_uk_t"], w["w_uv_t"], w["w_uq"]]

    def body(x_ref, cos_ref, sin_ref, gkv_ref, gb_ref, gkvn_ref, gq_ref, wdc_ref, wdr_ref, wbt_ref,
             wuk_ref, wuv_ref, wukt_ref, wuvt_ref, wuq_ref,
             hk_ref, hq_ref, ck_ref, cqp_ref, g2_ref, ckv_ref, cq_ref, q_ref, kn_ref, v_ref, kr_ref, knt_ref, vt_ref, krt_ref):
        nrm = _rms(x_ref[...])
        hk = (nrm * gkv_ref[...]).astype(BF16)
        hq = (nrm * gb_ref[...]).astype(BF16)
        hk_ref[...] = hk
        hq_ref[...] = hq
        ck = jnp.dot(hk, wdc_ref[...], preferred_element_type=F32)
        ck_ref[...] = ck
        cqp = lax.dot_general(hq, wbt_ref[0:Q_RANK, :], _NT, preferred_element_type=F32)
        cqp_ref[...] = cqp
        g2_ref[...] = lax.dot_general(hq, wbt_ref[Q_RANK:, :], _NT, preferred_element_type=F32)
        cosv, sinv = cos_ref[...], sin_ref[...]
        kr = _rope(jnp.dot(hk, wdr_ref[...], preferred_element_type=F32), cosv, sinv)
        kr_ref[...] = kr.astype(BF16)
        krt_ref[...] = kr.T.astype(BF16)
        ckv = (_rms(ck) * gkvn_ref[...]).astype(BF16)
        ckv_ref[...] = ckv
        kn_ref[...] = jnp.dot(ckv, wuk_ref[...], preferred_element_type=F32).astype(BF16)
        v_ref[...] = jnp.dot(ckv, wuv_ref[...], preferred_element_type=F32).astype(BF16)
        knt_ref[...] = lax.dot_general(wukt_ref[...], ckv, _NT, preferred_element_type=F32).astype(BF16)
        vt_ref[...] = lax.dot_general(wuvt_ref[...], ckv, _NT, preferred_element_type=F32).astype(BF16)
        cq = (_rms(cqp) * gq_ref[...]).astype(BF16)
        cq_ref[...] = cq
        for h in range(N_HEADS):
            qh = jnp.dot(cq, wuq_ref[:, h * HEAD_PAD:(h + 1) * HEAD_PAD], preferred_element_type=F32)
            q_ref[:, h * HEAD_PAD:h * HEAD_PAD + QK_NOPE] = qh[:, :QK_NOPE].astype(BF16)
            q_ref[:, h * HEAD_PAD + QK_NOPE:(h + 1) * HEAD_PAD] = _rope(qh[:, QK_NOPE:], cosv, sinv).astype(BF16)

    row = lambda w_: pl.BlockSpec((bt, w_), lambda i: (i, 0))
    col = lambda h_: pl.BlockSpec((h_, bt), lambda i: (0, i))
    tab = pl.BlockSpec((bt, LANES), lambda i: (i % per_seq, 0))
    nh = N_HEADS * V_DIM
    shapes = [((t, d), BF16), ((t, d), BF16), ((t, KV_RANK), F32), ((t, Q_RANK), F32), ((t, nh), F32), ((t, KV_RANK), BF16),
              ((t, Q_RANK), BF16), ((t, N_HEADS * HEAD_PAD), BF16), ((t, nh), BF16), ((t, nh), BF16), ((t, LANES), BF16),
              ((nh, t), BF16), ((nh, t), BF16), ((LANES, t), BF16)]
    out_specs = [row(d), row(d), row(KV_RANK), row(Q_RANK), row(nh), row(KV_RANK), row(Q_RANK), row(N_HEADS * HEAD_PAD),
                 row(nh), row(nh), row(LANES), col(nh), col(nh), col(LANES)]
    return pl.pallas_call(
        body, grid=(t // bt,), in_specs=[row(d), tab, tab] + [_const_spec(a) for a in consts], out_specs=out_specs,
        out_shape=[jax.ShapeDtypeStruct(s, dt) for s, dt in shapes],
        compiler_params=_params(("parallel",)), name=name)(x1, cos, sin, *consts)


def _rms_bwd_rows(xv, dn):
    r = lax.rsqrt(jnp.mean(xv * xv, axis=-1, keepdims=True) + EPS)
    nrm = xv * r
    return r * (dn - nrm * jnp.mean(dn * nrm, axis=-1, keepdims=True)), nrm


def _col_sum(v):
    return jnp.sum(v, axis=0, keepdims=True)


def _lru_proj_bwd(dxp, dga, x, dx1, g_a, w_in_t, parts, *, name, bt=256):
    t, d = x.shape
    bt = min(bt, t)
    n = w_in_t.shape[0] // 2
    steps = (t // bt,)

    def body(dxp_ref, dga_ref, x_ref, dx1_ref, g_ref, wt_ref, parts_ref, dx_ref, dg_ref, land_ref,
             send_sems, recv_sems, local_sem):
        first, last = _first_last(steps)

        @pl.when(first)
        def _():
            dg_ref[...] = jnp.zeros_like(dg_ref)
            _exchange(parts_ref, land_ref, send_sems, recv_sems, local_sem, finish=False)

        dh = (jnp.dot(dxp_ref[...], wt_ref[0:n, :], preferred_element_type=F32)
              + jnp.dot(dga_ref[...], wt_ref[n:2 * n, :], preferred_element_type=F32))
        dxn, nrm = _rms_bwd_rows(x_ref[...], dh * g_ref[...])
        dg_ref[...] += _col_sum(dh * nrm)
        dx_ref[...] = dx1_ref[...] + dxn

        @pl.when(last)
        def _():
            _exchange(parts_ref, land_ref, send_sems, recv_sems, local_sem, finish=True)

    row = lambda w: pl.BlockSpec((bt, w), lambda i: (i, 0))
    return pl.pallas_call(
        body, grid=steps,
        in_specs=[row(n), row(n), row(d), row(d), _const_spec(g_a), _const_spec(w_in_t), _ANY],
        out_specs=[row(d), _const_spec(g_a), _ANY],
        out_shape=[jax.ShapeDtypeStruct((t, d), F32), jax.ShapeDtypeStruct((1, d), F32),
                   jax.ShapeDtypeStruct(parts.shape, parts.dtype)],
        scratch_shapes=_EXCHANGE_SEMS,
        compiler_params=_params(("arbitrary",)), name=name)(dxp, dga, x, dx1, g_a, w_in_t, parts)


def _mla_proj_bwd(x1, dx2, cqp, ck, dq, dkn, dv, dkr, dg2, gains, w, *, name, bt=256):
    t, d = x1.shape
    bt = min(bt, t)
    g_kv, g_b, g_kvn, g_q = gains
    consts = [g_kv, g_b, g_kvn, g_q, w["w_dkv_c"], w["w_dkv_r"], w["w_in_b_t"], w["w_uk"], w["w_uv"], w["w_uq"]]
    nh = N_HEADS * V_DIM

    def body(x1_ref, dx2_ref, cqp_ref, ck_ref, dq_ref, dkn_ref, dv_ref, dkr_ref, dg2_ref,
             gkv_ref, gb_ref, gkvn_ref, gq_ref, wdc_ref, wdr_ref, wbt_ref, wuk_ref, wuv_ref, wuq_ref,
             dx1_ref, du2_ref, dckr_ref, dgkv_ref, dgb_ref, dgkvn_ref, dgq_ref):
        @pl.when(pl.program_id(0) == 0)
        def _():
            for ref in (dgkv_ref, dgb_ref, dgkvn_ref, dgq_ref):
                ref[...] = jnp.zeros_like(ref)

        dot_nt = lambda a, b: lax.dot_general(a, b, _NT, preferred_element_type=F32)
        dcq = dot_nt(dq_ref[...], wuq_ref[...])
        dcqp, nq = _rms_bwd_rows(cqp_ref[...], dcq * gq_ref[...])
        dgq_ref[...] += _col_sum(dcq * nq)
        dcqp = dcqp.astype(BF16)
        dg2 = dg2_ref[...]
        du2_ref[:, :Q_RANK] = dcqp
        du2_ref[:, Q_RANK:] = dg2
        dhq = (jnp.dot(dcqp, wbt_ref[0:Q_RANK, :], preferred_element_type=F32)
               + jnp.dot(dg2, wbt_ref[Q_RANK:, :], preferred_element_type=F32))
        dckv = dot_nt(dkn_ref[...], wuk_ref[...]) + dot_nt(dv_ref[...], wuv_ref[...])
        dck, nc = _rms_bwd_rows(ck_ref[...], dckv * gkvn_ref[...])
        dgkvn_ref[...] += _col_sum(dckv * nc)
        dck = dck.astype(BF16)
        dkr = dkr_ref[...].astype(BF16)
        dckr_ref[:, :KV_RANK] = dck
        dckr_ref[:, KV_RANK:] = dkr
        dhk = dot_nt(dck, wdc_ref[...]) + dot_nt(dkr, wdr_ref[...])
        dxn, n1 = _rms_bwd_rows(x1_ref[...], dhq * gb_ref[...] + dhk * gkv_ref[...])
        dgb_ref[...] += _col_sum(dhq * n1)
        dgkv_ref[...] += _col_sum(dhk * n1)
        dx1_ref[...] = dx2_ref[...] + dxn

    row = lambda w_: pl.BlockSpec((bt, w_), lambda i: (i, 0))
    vec = lambda w_: pl.BlockSpec((1, w_), lambda i: (0, 0))
    in_specs = [row(d), row(d), row(Q_RANK), row(KV_RANK), row(N_HEADS * HEAD_PAD), row(nh), row(nh), row(LANES), row(nh)]
    return pl.pallas_call(
        body, grid=(t // bt,), in_specs=in_specs + [_const_spec(a) for a in consts],
        out_specs=[row(d), row(Q_RANK + nh), row(KV_RANK + LANES), vec(d), vec(d), vec(KV_RANK), vec(Q_RANK)],
        out_shape=[jax.ShapeDtypeStruct((t, d), F32), jax.ShapeDtypeStruct((t, Q_RANK + nh), BF16),
                   jax.ShapeDtypeStruct((t, KV_RANK + LANES), BF16), jax.ShapeDtypeStruct((1, d), F32),
                   jax.ShapeDtypeStruct((1, d), F32), jax.ShapeDtypeStruct((1, KV_RANK), F32),
                   jax.ShapeDtypeStruct((1, Q_RANK), F32)],
        compiler_params=_params(("arbitrary",)), name=name)(x1, dx2, cqp, ck, dq, dkn, dv, dkr, dg2, *consts)


def _softplus(z):
    return jnp.maximum(z, 0.0) + jnp.log1p(jnp.exp(-jnp.abs(z)))


def _one_minus_square(a, la):
    return jnp.tanh(-la) * (1.0 + a * a)


def _gates(xb, wrg, wig, brg, big, sp):
    xbb = xb.astype(BF16)
    r = _sigmoid_tail(jnp.dot(xbb, wrg, preferred_element_type=F32) + brg)
    i = _sigmoid(jnp.dot(xbb, wig, preferred_element_type=F32) + big)
    la = (-LRU_C) * r * sp
    a = jnp.exp(la)
    em = _one_minus_square(a, la)
    inv_mult = lax.rsqrt(em)
    mult = jnp.where(em > 0.0, em * inv_mult, 0.0)
    return r, i, a, mult, inv_mult


def _conv(xpad_ref, cw_ref, seq):
    acc = cw_ref[0:1, :] * xpad_ref[pl.ds(8 - (CONV_WIDTH - 1), seq), :]
    for k in range(1, CONV_WIDTH):
        acc = acc + cw_ref[k:k + 1, :] * xpad_ref[pl.ds(8 - (CONV_WIDTH - 1) + k, seq), :]
    return acc


def _seq_spec(seq):
    return pl.BlockSpec((None, seq, RNN_BW), lambda n, b: (b, 0, n))


def _chan_spec(rows):
    return pl.BlockSpec((rows, RNN_BW), lambda n, b: (0, n))


_GATE_W_SPEC = pl.BlockSpec((None, RNN_BW, RNN_BW), lambda n, b: (n, 0, 0))


SCAN_UNROLL = 4


def _peers():
    x, y, c = _mesh_pos()
    others = []
    for k in range(1, N_DEV):
        px = 1 - x if k & 4 else x
        py = 1 - y if k & 2 else y
        pc = 1 - c if k & 1 else c
        others.append(((px, py, pc), 4 * px + 2 * py + pc))
    return 4 * x + 2 * y + c, others


def _exchange(src_ref, dst_ref, send_sems, recv_sems, local_sem, *, finish, gather=False):
    me, others = _peers()

    def send(k, dev, slot):
        return pltpu.make_async_remote_copy(
            src_ref=src_ref if gather else src_ref.at[slot], dst_ref=dst_ref.at[me], send_sem=send_sems.at[k],
            recv_sem=recv_sems.at[k], device_id=dev, device_id_type=pl.DeviceIdType.MESH)

    local = pltpu.make_async_copy(src_ref if gather else src_ref.at[me], dst_ref.at[me], local_sem)
    if not finish:
        local.start()
        for k, (dev, slot) in enumerate(others):
            send(k, dev, slot).start()
        return
    for k, (dev, slot) in enumerate(others):
        pltpu.make_async_remote_copy(
            src_ref=dst_ref.at[slot], dst_ref=dst_ref.at[slot], send_sem=send_sems.at[k], recv_sem=recv_sems.at[k],
            device_id=dev, device_id_type=pl.DeviceIdType.MESH).wait_recv()
    for k, (dev, slot) in enumerate(others):
        send(k, dev, slot).wait_send()
    local.wait()


def _gather_two_level(x_ref, out_ref, send_sems, recv_sems, local_sem, *, phase):
    x, y, c = _mesh_pos()
    me, sibling = (x, y, c), (x, y, 1 - c)
    chips = [(1 - x, y), (x, 1 - y), (1 - x, 1 - y)]

    def slot(px, py, pc):
        return out_ref.at[4 * px + 2 * py + pc]

    def copy(k, blk, to, src=None):
        return pltpu.make_async_remote_copy(
            src_ref=slot(*blk) if src is None else src, dst_ref=slot(*blk),
            send_sem=send_sems.at[k], recv_sem=recv_sems.at[k], device_id=to, device_id_type=pl.DeviceIdType.MESH)

    if phase == 0:
        pltpu.make_async_copy(x_ref, slot(*me), local_sem).start()
        copy(0, me, sibling, src=x_ref).start()
        for j, chip in enumerate(chips):
            copy(1 + j, me, (*chip, c), src=x_ref).start()
    elif phase == 1:
        for j, chip in enumerate(chips):
            copy(1 + j, (*chip, c), me).wait_recv()
            copy(4 + j, (*chip, c), sibling).start()
    else:
        copy(0, sibling, me).wait_recv()
        for j, chip in enumerate(chips):
            copy(4 + j, (*chip, 1 - c), me).wait_recv()
        copy(0, me, sibling, src=x_ref).wait_send()
        for j, chip in enumerate(chips):
            copy(1 + j, me, (*chip, c), src=x_ref).wait_send()
            copy(4 + j, (*chip, c), sibling).wait_send()
        pltpu.make_async_copy(x_ref, slot(*me), local_sem).wait()


GATHER_FORWARD_STEP = 7
_EXCHANGE_SEMS = [pltpu.SemaphoreType.DMA((N_DEV - 1,)), pltpu.SemaphoreType.DMA((N_DEV - 1,)), pltpu.SemaphoreType.DMA(())]


def _first_last(steps):
    first = last = None
    for axis, n in enumerate(steps):
        i = pl.program_id(axis)
        first = (i == 0) if first is None else first & (i == 0)
        last = (i == n - 1) if last is None else last & (i == n - 1)
    return first, last


def _lru_fwd(xp, ga, cw, vecs, wrg, wig, block, *, name):
    bsz, seq, _ = xp.shape
    groups = seq // 8

    def body(xp_ref, ga_ref, cw_ref, vec_ref, wrg_ref, wig_ref, blk_ref, xb_ref, hs_ref, y_ref, all_ref,
             xpad, a_s, b_s, send_sems, recv_sems, local_sem):
        first, last = _first_last((RNN_BLOCKS, bsz))

        @pl.when(first)
        def _():
            _gather_two_level(blk_ref, all_ref, send_sems, recv_sems, local_sem, phase=0)

        @pl.when((pl.program_id(0) == GATHER_FORWARD_STEP) & (pl.program_id(1) == 0))
        def _():
            _gather_two_level(blk_ref, all_ref, send_sems, recv_sems, local_sem, phase=1)

        xpad[0:8, :] = jnp.zeros((8, RNN_BW), F32)
        xpad[pl.ds(8, seq), :] = xp_ref[...]
        xb = _conv(xpad, cw_ref, seq) + vec_ref[0:1, :]
        xb_ref[...] = xb
        sp = _softplus(-vec_ref[3:4, :])
        _, i, a, mult, _ = _gates(xb, wrg_ref[...], wig_ref[...], vec_ref[1:2, :], vec_ref[2:3, :], sp)
        a_s[...] = a
        b_s[...] = mult * (i * xb)
        row = lax.broadcasted_iota(jnp.int32, (8, RNN_BW), 0)

        def group(g, h):
            r0 = pl.multiple_of(g * 8, 8)
            av = a_s[pl.ds(r0, 8), :]
            bv = b_s[pl.ds(r0, 8), :]
            for k in (1, 2, 4):
                m = row >= k
                bv = jnp.where(m, av * pltpu.roll(bv, k, axis=0) + bv, bv)
                av = jnp.where(m, av * pltpu.roll(av, k, axis=0), av)
            hs_ref[pl.ds(r0, 8), :] = av * h + bv
            return av[7:8, :] * h + bv[7:8, :]

        def groups_of(i, h):
            for u in range(SCAN_UNROLL):
                h = group(i * SCAN_UNROLL + u, h)
            return h

        lax.fori_loop(0, groups // SCAN_UNROLL, groups_of, jnp.zeros((1, RNN_BW), F32))
        gav = ga_ref[...]
        y_ref[...] = (hs_ref[...] * (gav * _sigmoid(gav))).astype(BF16)

        @pl.when(last)
        def _():
            _gather_two_level(blk_ref, all_ref, send_sems, recv_sems, local_sem, phase=2)

    sq = _seq_spec(seq)
    shape = (bsz, seq, D_RNN)
    return pl.pallas_call(
        body, grid=(RNN_BLOCKS, bsz),
        in_specs=[sq, sq, _chan_spec(8), _chan_spec(8), _GATE_W_SPEC, _GATE_W_SPEC, _ANY],
        out_specs=[sq, sq, sq, _ANY],
        out_shape=[jax.ShapeDtypeStruct(shape, F32), jax.ShapeDtypeStruct(shape, F32), jax.ShapeDtypeStruct(shape, BF16),
                   jax.ShapeDtypeStruct((N_DEV,) + block.shape, block.dtype)],
        scratch_shapes=[pltpu.VMEM((seq + 8, RNN_BW), F32), pltpu.VMEM((seq, RNN_BW), F32), pltpu.VMEM((seq, RNN_BW), F32)]
        + _EXCHANGE_SEMS,
        compiler_params=_params(("arbitrary", "arbitrary")), name=name)(xp, ga, cw, vecs, wrg, wig, block)


def _lru_bwd(dy, xp, xb, hs, ga, cw, vecs, wrg, wig, parts, *, name):
    bsz, seq, _ = xp.shape
    groups = seq // 8

    def body(dy_ref, xp_ref, xb_ref, hs_ref, ga_ref, cw_ref, vec_ref, wrg_ref, wig_ref,
             parts_ref, dxp_ref, dga_ref, dwrg_ref, dwig_ref, dvec_ref, land_ref, pad, a_s, d_s, lam_s,
             send_sems, recv_sems, local_sem):
        first, last = _first_last((RNN_BLOCKS, bsz))

        @pl.when(first)
        def _():
            _exchange(parts_ref, land_ref, send_sems, recv_sems, local_sem, finish=False)

        @pl.when(pl.program_id(1) == 0)
        def _():
            dwrg_ref[...] = jnp.zeros_like(dwrg_ref)
            dwig_ref[...] = jnp.zeros_like(dwig_ref)
            dvec_ref[...] = jnp.zeros_like(dvec_ref)

        xb = xb_ref[...]
        hs = hs_ref[...]
        gav = ga_ref[...]
        dy = dy_ref[...]
        sp = _softplus(-vec_ref[3:4, :])
        wrg = wrg_ref[...]
        wig = wig_ref[...]
        r, i, a, mult, inv_mult = _gates(xb, wrg, wig, vec_ref[1:2, :], vec_ref[2:3, :], sp)
        sg = _sigmoid(gav)
        dga_ref[...] = (dy * hs * (sg * (1.0 + gav * (1.0 - sg)))).astype(BF16)
        d_s[...] = dy * (gav * sg)

        pad[pl.ds(0, seq), :] = a
        pad[pl.ds(seq, 8), :] = jnp.zeros((8, RNN_BW), F32)
        a_s[...] = pad[pl.ds(1, seq), :]
        row = lax.broadcasted_iota(jnp.int32, (8, RNN_BW), 0)

        def group(g, nxt):
            r0 = pl.multiple_of((groups - 1 - g) * 8, 8)
            cv = a_s[pl.ds(r0, 8), :]
            bv = d_s[pl.ds(r0, 8), :]
            for k in (1, 2, 4):
                m = row < 8 - k
                bv = jnp.where(m, cv * pltpu.roll(bv, 8 - k, axis=0) + bv, bv)
                cv = jnp.where(m, cv * pltpu.roll(cv, 8 - k, axis=0), cv)
            lam_s[pl.ds(r0, 8), :] = cv * nxt + bv
            return cv[0:1, :] * nxt + bv[0:1, :]

        def groups_of(i, nxt):
            for u in range(SCAN_UNROLL):
                nxt = group(i * SCAN_UNROLL + u, nxt)
            return nxt

        lax.fori_loop(0, groups // SCAN_UNROLL, groups_of, jnp.zeros((1, RNN_BW), F32))
        dh = lam_s[...]

        pad[0:8, :] = jnp.zeros((8, RNN_BW), F32)
        pad[pl.ds(8, seq), :] = hs
        da = dh * pad[pl.ds(7, seq), :]
        ixb = i * xb
        dixb = dh * mult
        dla = da * a - (dh * ixb) * (a * a) * inv_mult
        drp = (dla * ((-LRU_C) * sp)) * r * (1.0 - r)
        dip = (dixb * xb) * i * (1.0 - i)
        dvec_ref[0:1, :] += jnp.sum(drp, axis=0, keepdims=True)
        dvec_ref[1:2, :] += jnp.sum(dip, axis=0, keepdims=True)
        dvec_ref[2:3, :] += jnp.sum(dla * ((-LRU_C) * r), axis=0, keepdims=True)
        drpb = drp.astype(BF16)
        dipb = dip.astype(BF16)
        xbb = xb.astype(BF16)
        nt = (((1,), (1,)), ((), ()))
        tn = (((0,), (0,)), ((), ()))
        dxb = (dixb * i
               + lax.dot_general(drpb, wrg, nt, preferred_element_type=F32)
               + lax.dot_general(dipb, wig, nt, preferred_element_type=F32))
        dwrg_ref[...] += lax.dot_general(xbb, drpb, tn, preferred_element_type=F32)
        dwig_ref[...] += lax.dot_general(xbb, dipb, tn, preferred_element_type=F32)
        dvec_ref[3:4, :] += jnp.sum(dxb, axis=0, keepdims=True)

        pad[pl.ds(0, seq), :] = dxb
        pad[pl.ds(seq, 8), :] = jnp.zeros((8, RNN_BW), F32)
        dxp = cw_ref[0:1, :] * pad[pl.ds(CONV_WIDTH - 1, seq), :]
        for k in range(1, CONV_WIDTH):
            dxp = dxp + cw_ref[k:k + 1, :] * pad[pl.ds(CONV_WIDTH - 1 - k, seq), :]
        dxp_ref[...] = dxp.astype(BF16)
        pad[0:8, :] = jnp.zeros((8, RNN_BW), F32)
        pad[pl.ds(8, seq), :] = xp_ref[...]
        for k in range(CONV_WIDTH):
            dvec_ref[4 + k:5 + k, :] += jnp.sum(dxb * pad[pl.ds(8 - (CONV_WIDTH - 1) + k, seq), :], axis=0, keepdims=True)

        @pl.when(last)
        def _():
            _exchange(parts_ref, land_ref, send_sems, recv_sems, local_sem, finish=True)

    sq = _seq_spec(seq)
    shape = (bsz, seq, D_RNN)
    gshape = (RNN_BLOCKS, RNN_BW, RNN_BW)
    return pl.pallas_call(
        body, grid=(RNN_BLOCKS, bsz),
        in_specs=[sq, sq, sq, sq, sq, _chan_spec(8), _chan_spec(8), _GATE_W_SPEC, _GATE_W_SPEC, _ANY],
        out_specs=[sq, sq, _GATE_W_SPEC, _GATE_W_SPEC, _chan_spec(8), _ANY],
        out_shape=[jax.ShapeDtypeStruct(shape, BF16), jax.ShapeDtypeStruct(shape, BF16),
                   jax.ShapeDtypeStruct(gshape, F32), jax.ShapeDtypeStruct(gshape, F32),
                   jax.ShapeDtypeStruct((8, D_RNN), F32), jax.ShapeDtypeStruct(parts.shape, parts.dtype)],
        scratch_shapes=[pltpu.VMEM((seq + 8, RNN_BW), F32), pltpu.VMEM((seq, RNN_BW), F32),
                        pltpu.VMEM((seq, RNN_BW), F32), pltpu.VMEM((seq, RNN_BW), F32)] + _EXCHANGE_SEMS,
        compiler_params=_params(("arbitrary", "arbitrary")), name=name)(dy, xp, xb, hs, ga, cw, vecs, wrg, wig, parts)


def _attn_block(seq):
    return min(512, seq)


def _diag_mask(blk):
    return lax.broadcasted_iota(jnp.int32, (blk, blk), 0) <= lax.broadcasted_iota(jnp.int32, (blk, blk), 1)


FWD_HEADS = 4
BWD_HEADS = 2


def _attn_fwd(q, kn, kr, v_t, block, *, bsz, seq, name):
    t = bsz * seq
    blk = _attn_block(seq)
    nq = seq // blk
    hg = FWD_HEADS
    steps = (bsz, N_HEADS // hg, nq)

    def body(q_ref, kn_ref, kr_ref, vt_ref, blk_ref, o_ref, lse_ref, all_ref, acc, send_sems, recv_sems, local_sem):
        first, last = _first_last(steps)

        @pl.when(first)
        def _():
            _exchange(blk_ref, all_ref, send_sems, recv_sems, local_sem, finish=False, gather=True)

        qi = pl.program_id(2)
        acc[...] = jnp.zeros_like(acc)

        def step(j, carry, diagonal):
            k0 = pl.multiple_of(j * blk, blk)
            kr_j = kr_ref[pl.ds(k0, blk), :]
            out = []
            for h in range(hg):
                m_i, l_i = carry[h]
                kv = jnp.concatenate([kn_ref[pl.ds(k0, blk), h * QK_NOPE:(h + 1) * QK_NOPE], kr_j], axis=1)
                qv = q_ref[:, h * HEAD_PAD:(h + 1) * HEAD_PAD]
                s = lax.dot_general(kv, qv, _NT, preferred_element_type=F32) * ATTN_SCALE
                if diagonal:
                    s = jnp.where(_diag_mask(blk), s, -jnp.inf)
                m_new = jnp.maximum(m_i, jnp.max(s, axis=0, keepdims=True))
                p = jnp.exp(s - m_new)
                alpha = jnp.exp(m_i - m_new)
                l_new = alpha * l_i + jnp.sum(p, axis=0, keepdims=True)
                acc[h] = alpha * acc[h] + jnp.dot(vt_ref[h * V_DIM:(h + 1) * V_DIM, pl.ds(k0, blk)], p.astype(BF16),
                                                  preferred_element_type=F32)
                out.append((m_new, l_new))
            return tuple(out)

        init = tuple((jnp.full((1, blk), -jnp.inf, F32), jnp.zeros((1, blk), F32)) for _ in range(hg))
        carry = lax.fori_loop(0, qi, lambda j, c: step(j, c, False), init)
        stats = step(qi, carry, True)
        for h in range(hg):
            m_i, l_i = stats[h]
            o_ref[:, h * V_DIM:(h + 1) * V_DIM] = (acc[h] / l_i).T
            lse_ref[h] = m_i + jnp.log(l_i)

        @pl.when(last)
        def _():
            _exchange(blk_ref, all_ref, send_sems, recv_sems, local_sem, finish=True, gather=True)

    return pl.pallas_call(
        body, grid=steps,
        in_specs=[pl.BlockSpec((blk, hg * HEAD_PAD), lambda b, g, i: (b * nq + i, g)),
                  pl.BlockSpec((seq, hg * QK_NOPE), lambda b, g, i: (b, g)),
                  pl.BlockSpec((seq, LANES), lambda b, g, i: (b, 0)),
                  pl.BlockSpec((hg * V_DIM, seq), lambda b, g, i: (g, b)), _ANY],
        out_specs=[pl.BlockSpec((blk, hg * V_DIM), lambda b, g, i: (b * nq + i, g)),
                   pl.BlockSpec((hg, 1, blk), lambda b, g, i: (g, 0, b * nq + i)), _ANY],
        out_shape=[jax.ShapeDtypeStruct((t, N_HEADS * V_DIM), F32), jax.ShapeDtypeStruct((N_HEADS, 1, t), F32),
                   jax.ShapeDtypeStruct((N_DEV,) + block.shape, block.dtype)],
        scratch_shapes=[pltpu.VMEM((hg, V_DIM, blk), F32)] + _EXCHANGE_SEMS,
        compiler_params=_params(("arbitrary", "arbitrary", "arbitrary")), name=name)(q, kn, kr, v_t, block)


def _attn_bwd(q, kn, kr, kn_t, kr_t, v, o, lse, do, cos, sin, parts, *, bsz, seq, name):
    t = bsz * seq
    blk = _attn_block(seq)
    nq = seq // blk
    hg = BWD_HEADS
    steps = (bsz, N_HEADS // hg)

    def body(q_ref, kn_ref, kr_ref, knt_ref, krt_ref, v_ref, o_ref, lse_ref, do_ref, cos_ref, sin_ref, parts_ref,
             dq_ref, dkn_ref, dkr_ref, dv_ref, land_ref, dqt_acc, dk_acc, dv_acc, send_sems, recv_sems, local_sem):
        first, last = _first_last(steps)

        @pl.when(first)
        def _():
            _exchange(parts_ref, land_ref, send_sems, recv_sems, local_sem, finish=False)

        dqt_acc[...] = jnp.zeros_like(dqt_acc)
        dk_acc[...] = jnp.zeros_like(dk_acc)
        dv_acc[...] = jnp.zeros_like(dv_acc)

        def q_block(i, _):
            q0 = pl.multiple_of(i * blk, blk)
            rows = []
            for h in range(hg):
                dov = do_ref[pl.ds(q0, blk), h * V_DIM:(h + 1) * V_DIM].astype(F32)
                dcol = jnp.sum(dov * o_ref[pl.ds(q0, blk), h * V_DIM:(h + 1) * V_DIM], axis=-1, keepdims=True)
                delta = jnp.broadcast_to(dcol, (blk, LANES)).T[0:1, :]
                rows.append((lse_ref[h, :, pl.ds(q0, blk)], delta))

            def pair(j, diagonal):
                k0 = pl.multiple_of(j * blk, blk)
                kr_j = kr_ref[pl.ds(k0, blk), :]
                krt_j = krt_ref[:, pl.ds(k0, blk)]
                for h in range(hg):
                    lse_i, delta = rows[h]
                    qv = q_ref[pl.ds(q0, blk), h * HEAD_PAD:(h + 1) * HEAD_PAD]
                    dov = do_ref[pl.ds(q0, blk), h * V_DIM:(h + 1) * V_DIM]
                    kv = jnp.concatenate([kn_ref[pl.ds(k0, blk), h * QK_NOPE:(h + 1) * QK_NOPE], kr_j], axis=1)
                    s = lax.dot_general(kv, qv, _NT, preferred_element_type=F32) * ATTN_SCALE
                    p = jnp.exp(s - lse_i)
                    if diagonal:
                        p = jnp.where(_diag_mask(blk), p, 0.0)
                    dv_acc[pl.ds(k0, blk), h * V_DIM:(h + 1) * V_DIM] += jnp.dot(
                        p.astype(BF16), dov, preferred_element_type=F32)
                    dp = lax.dot_general(v_ref[pl.ds(k0, blk), h * V_DIM:(h + 1) * V_DIM], dov, _NT,
                                         preferred_element_type=F32)
                    ds = (p * (dp - delta) * ATTN_SCALE).astype(BF16)
                    dk_acc[pl.ds(k0, blk), h * HEAD_PAD:(h + 1) * HEAD_PAD] += jnp.dot(ds, qv, preferred_element_type=F32)
                    base = h * HEAD_PAD
                    dqt_acc[base:base + QK_NOPE, pl.ds(q0, blk)] += jnp.dot(
                        knt_ref[h * QK_NOPE:(h + 1) * QK_NOPE, pl.ds(k0, blk)], ds, preferred_element_type=F32)
                    dqt_acc[base + QK_NOPE:base + HEAD_PAD, pl.ds(q0, blk)] += jnp.dot(
                        krt_j, ds, preferred_element_type=F32)

            def off_diagonal(j, _):
                pair(j, False)
                return 0

            lax.fori_loop(0, i, off_diagonal, 0)
            pair(i, True)
            return 0

        lax.fori_loop(0, nq, q_block, 0)
        dkr = jnp.zeros((seq, LANES), F32)
        for h in range(hg):
            base = h * HEAD_PAD
            for i in range(nq):
                rows = slice(i * blk, (i + 1) * blk)
                dq = dqt_acc[base:base + HEAD_PAD, rows].T
                dq_ref[rows, base:base + QK_NOPE] = dq[:, :QK_NOPE].astype(BF16)
                dq_ref[rows, base + QK_NOPE:base + HEAD_PAD] = _rope_t(
                    dq[:, QK_NOPE:], cos_ref[rows, :], sin_ref[rows, :]).astype(BF16)
            dkn_ref[:, h * QK_NOPE:(h + 1) * QK_NOPE] = dk_acc[:, base:base + QK_NOPE].astype(BF16)
            dkr = dkr + dk_acc[:, base + QK_NOPE:base + HEAD_PAD]
        dv_ref[...] = dv_acc[...].astype(BF16)

        @pl.when(pl.program_id(1) == 0)
        def _():
            dkr_ref[...] = jnp.zeros_like(dkr_ref)

        dkr_ref[...] += _rope_t(dkr, cos_ref[...], sin_ref[...])

        @pl.when(last)
        def _():
            _exchange(parts_ref, land_ref, send_sems, recv_sems, local_sem, finish=True)

    head = pl.BlockSpec((seq, hg * V_DIM), lambda b, g: (b, g))
    head_t = pl.BlockSpec((hg * V_DIM, seq), lambda b, g: (g, b))
    shared = pl.BlockSpec((seq, LANES), lambda b, g: (b, 0))
    shared_t = pl.BlockSpec((LANES, seq), lambda b, g: (0, b))
    table = pl.BlockSpec((seq, LANES), lambda b, g: (0, 0))
    qspec = pl.BlockSpec((seq, hg * HEAD_PAD), lambda b, g: (b, g))
    return pl.pallas_call(
        body, grid=steps,
        in_specs=[qspec, head, shared, head_t, shared_t, head, head,
                  pl.BlockSpec((hg, 1, seq), lambda b, g: (g, 0, b)), head, table, table, _ANY],
        out_specs=[qspec, head, shared, head, _ANY],
        out_shape=[jax.ShapeDtypeStruct((t, N_HEADS * HEAD_PAD), BF16), jax.ShapeDtypeStruct((t, N_HEADS * QK_NOPE), BF16),
                   jax.ShapeDtypeStruct((t, LANES), F32), jax.ShapeDtypeStruct((t, N_HEADS * V_DIM), BF16),
                   jax.ShapeDtypeStruct(parts.shape, parts.dtype)],
        scratch_shapes=[pltpu.VMEM((hg * HEAD_PAD, seq), F32), pltpu.VMEM((seq, hg * HEAD_PAD), F32),
                        pltpu.VMEM((seq, hg * V_DIM), F32)] + _EXCHANGE_SEMS,
        compiler_params=_params(("arbitrary", "arbitrary")), name=name)(
            q, kn, kr, kn_t, kr_t, v, o, lse, do, cos, sin, parts)


def _head_and_loss(o, g2, x1, target, w_out, g_final, *, name, bt=256):
    t, d = x1.shape
    bt = min(bt, t)
    nt = (((1,), (1,)), ((), ()))

    def body(o_ref, g2_ref, x1_ref, tgt_ref, w_ref, gf_ref, loss_ref, dx2_ref, y2_ref, do_ref, dg2_ref, dgf_ref):
        @pl.when(pl.program_id(0) == 0)
        def _():
            loss_ref[...] = jnp.zeros_like(loss_ref)
            dgf_ref[...] = jnp.zeros_like(dgf_ref)

        ov = o_ref[...]
        gv = g2_ref[...]
        sg = _sigmoid(gv)
        silu = gv * sg
        y2 = (ov * silu).astype(BF16)
        y2_ref[...] = y2
        w = w_ref[...]
        x2 = x1_ref[...] + jnp.dot(y2, w, preferred_element_type=F32)
        r = lax.rsqrt(jnp.mean(x2 * x2, axis=-1, keepdims=True) + EPS)
        nrm = x2 * r
        gf = gf_ref[...]
        err = nrm * gf - tgt_ref[...]
        loss_ref[...] += 0.5 * jnp.sum(jnp.mean(err * err, axis=-1, keepdims=True))
        dyf = err * (1.0 / d)
        dgf_ref[...] += jnp.sum(dyf * nrm, axis=0, keepdims=True)
        dn = dyf * gf
        dx2 = r * (dn - nrm * jnp.mean(dn * nrm, axis=-1, keepdims=True))
        dx2_ref[...] = dx2
        dy2 = lax.dot_general(dx2.astype(BF16), w, nt, preferred_element_type=F32)
        do_ref[...] = (dy2 * silu).astype(BF16)
        dg2_ref[...] = (dy2 * ov * (sg * (1.0 + gv * (1.0 - sg)))).astype(BF16)

    row = pl.BlockSpec((bt, d), lambda i: (i, 0))
    vec = pl.BlockSpec((1, d), lambda i: (0, 0))
    return pl.pallas_call(
        body, grid=(t // bt,),
        in_specs=[row, row, row, row, pl.BlockSpec((d, d), lambda i: (0, 0)), vec],
        out_specs=[pl.BlockSpec((8, LANES), lambda i: (0, 0)), row, row, row, row, vec],
        out_shape=[jax.ShapeDtypeStruct((8, LANES), F32), jax.ShapeDtypeStruct((t, d), F32),
                   jax.ShapeDtypeStruct((t, d), BF16), jax.ShapeDtypeStruct((t, d), BF16),
                   jax.ShapeDtypeStruct((t, d), BF16), jax.ShapeDtypeStruct((1, d), F32)],
        compiler_params=_params(("arbitrary",)), name=name)(o, g2, x1, target, w_out, g_final)


def _sum_parts(parts, *, name, br=GRAD_BLOCK):
    npart, rows, w = parts.shape

    def body(p_ref, o_ref):
        acc = p_ref[0].astype(F32)
        for j in range(1, npart):
            acc = acc + p_ref[j].astype(F32)
        o_ref[...] = acc

    return pl.pallas_call(
        body, grid=(rows // br,), in_specs=[pl.BlockSpec((npart, br, w), lambda i: (0, i, 0))],
        out_specs=pl.BlockSpec((br, w), lambda i: (i, 0)), out_shape=jax.ShapeDtypeStruct((rows, w), F32),
        compiler_params=_params(("parallel",)), name=name)(parts)


def _as_block(a):
    if a.ndim == 1:
        return a.reshape(1, -1)
    if a.ndim > 2 and a.shape[0] == 1:
        return a.reshape(a.shape[1:])
    return a


def _adamw(g, w, m, v, *, name):
    shape = w.shape
    g, w, m, v = (_as_block(a) for a in (g, w, m, v))

    def body(g_ref, w_ref, m_ref, v_ref, d_ref, nm_ref, nv_ref):
        gv = g_ref[...]
        nm = ADAM_B1 * m_ref[...] + (1.0 - ADAM_B1) * gv
        nv = ADAM_B2 * v_ref[...] + (1.0 - ADAM_B2) * (gv * gv)
        nm_ref[...] = nm
        nv_ref[...] = nv
        m_hat = nm / (1.0 - ADAM_B1 ** ADAM_STEP)
        v_hat = nv / (1.0 - ADAM_B2 ** ADAM_STEP)
        d_ref[...] = (-ADAM_LR) * (m_hat / (jnp.sqrt(v_hat) + ADAM_EPS) + ADAM_WD * w_ref[...])

    whole = pl.BlockSpec(memory_space=pltpu.VMEM)
    outs = pl.pallas_call(
        body, in_specs=[whole] * 4, out_specs=[whole] * 3, out_shape=[jax.ShapeDtypeStruct(w.shape, F32)] * 3,
        compiler_params=_params(), name=name)(g, w, m, v)
    return [o.reshape(shape) for o in outs]


def _all_gather(block, *, name):
    m, n = block.shape

    def body(x_ref, out_ref, send_sems, recv_sems, local_sem):
        for phase in range(3):
            _gather_two_level(x_ref, out_ref, send_sems, recv_sems, local_sem, phase=phase)

    return pl.pallas_call(
        body, out_shape=jax.ShapeDtypeStruct((N_DEV, m, n), block.dtype), in_specs=[_ANY], out_specs=_ANY,
        scratch_shapes=_EXCHANGE_SEMS, name=name)(block)


def _rows(a):
    return a.reshape(-1, PACK_W)


def _pad_to(a, n):
    return jnp.pad(a, (0, n - a.shape[0]))


def _weight_blocks(d):
    small = _rows(_pad_to(jnp.concatenate([d[n].reshape(-1) for n, _ in _SMALL]), 8 * PACK_W))
    block_a = jnp.concatenate([d["w_in_a"][0].T.astype(WIRE), lax.bitcast_convert_type(small, WIRE).reshape(16, PACK_W)],
                              axis=0)
    w_uq = jnp.pad(d["w_uq"][0], ((0, 0), (0, 0), (0, HEAD_PAD - QK_NOPE - QK_ROPE)))
    pieces = {"w_out_a": d["w_out_a"], "w_dkv": d["w_dkv"], "w_uk": d["w_uk"], "w_uv": d["w_uv"],
              "w_in_b": d["w_in_b"][0].T, "w_uq": w_uq}
    block_b = jnp.concatenate([_rows(pieces[n]) for n, _ in _PIECES_B]
                              + [jnp.zeros((WIRE_ROWS_B - MATRIX_ROWS_B, PACK_W), F32)], axis=0).astype(WIRE)
    return block_a, block_b, d["w_out_b"][0].astype(WIRE)


def _weights_a(wall):
    w = {}
    lo, hi = _OFF_A["w_in_a"]
    w["w_in_a_t"] = wall[:, lo:hi].reshape(2 * D_RNN, D_MODEL)
    small = lax.bitcast_convert_type(wall[:, MATRIX_ROWS_A:].reshape(N_DEV, 8 * PACK_W, 2), F32)
    off = dict(zip([n for n, _ in _SMALL], [0, 128, 768, 928, 1088, 1248]))
    w["norm_a"] = small[:, :128].reshape(1, D_MODEL)

    def by_channel(lo, rows):
        a = small[:, lo:lo + rows * (D_RNN // N_DEV)].reshape(N_DEV, rows, -1).transpose(1, 0, 2).reshape(rows, D_RNN)
        return jnp.pad(a, ((0, 8 - rows), (0, 0)))

    w["conv_taps"] = by_channel(off["conv_w"], CONV_WIDTH)
    w["lru_vecs"] = by_channel(off["conv_b"], 4)
    return w


def _weights_b(wall):
    piece = {n: wall[:, lo:hi] for n, (lo, hi) in _OFF_B.items()}
    w = {"w_out_a": piece["w_out_a"].reshape(D_RNN, D_MODEL)}
    w_dkv = piece["w_dkv"].reshape(D_MODEL, KV_RANK + QK_ROPE)
    w["w_dkv_c"] = w_dkv[:, :KV_RANK]
    w["w_dkv_r"] = jnp.pad(w_dkv[:, KV_RANK:], ((0, 0), (0, LANES - QK_ROPE)))
    w["w_uk"] = piece["w_uk"].reshape(KV_RANK, N_HEADS * QK_NOPE)
    w["w_uv"] = piece["w_uv"].reshape(KV_RANK, N_HEADS * V_DIM)
    w["w_uk_t"], w["w_uv_t"] = w["w_uk"].T, w["w_uv"].T
    w["w_in_b_t"] = piece["w_in_b"].reshape(Q_RANK + N_HEADS * V_DIM, D_MODEL)
    w["w_uq"] = piece["w_uq"].reshape(Q_RANK, N_HEADS * HEAD_PAD)
    return w


def _pack_rep(d):
    flat = jnp.concatenate([d[n].reshape(-1) for n, _ in _REP])
    return _rows(_pad_to(flat, REP_ROWS * PACK_W))


def _unpack_rep(p, like):
    flat = p.reshape(-1)
    out, off = {}, 0
    for n, k in _REP:
        out[n] = flat[off:off + k].reshape(like[n].shape)
        off += k
    return out


def _by_owner(a):
    return a.reshape(N_DEV, -1, PACK_W)


def _grad_parts_b(g):
    tail = jnp.zeros((N_DEV, WIRE_ROWS_B - MATRIX_ROWS_B, PACK_W), F32)
    return jnp.concatenate([_by_owner(g[n]) for n, _ in _PIECES_B] + [tail], axis=1).astype(BF16)


def _grad_parts_a(g):
    small = jnp.concatenate([
        jnp.zeros((N_DEV, D_MODEL // N_DEV), F32),
        g["conv_w"].reshape(CONV_WIDTH, N_DEV, -1).transpose(1, 0, 2).reshape(N_DEV, -1),
        g["conv_b"].reshape(N_DEV, -1), g["b_rg"].reshape(N_DEV, -1), g["b_ig"].reshape(N_DEV, -1),
        g["lru_lambda"].reshape(N_DEV, -1)], axis=1)
    small = jnp.pad(small, ((0, 0), (0, 8 * PACK_W - small.shape[1]))).reshape(N_DEV, 8, PACK_W)
    half = N_DEV // 2
    w_in_a = jnp.concatenate([h.reshape(half, -1, PACK_W) for h in g["w_in_a_t"]], axis=0)
    rep = _pack_rep(g).reshape(N_DEV, REP_SLICE, PACK_W)
    tail = jnp.zeros((N_DEV, GRAD_ROWS_A - MATRIX_ROWS_A - 8 - REP_SLICE, PACK_W), F32)
    return jnp.concatenate([w_in_a, small, rep, tail], axis=1).astype(BF16)


def _own_grads(sum_a, sum_b, sum_c):
    out = {}
    lo, hi = _OFF_A["w_in_a"]
    out["w_in_a"] = sum_a[lo:hi].T.reshape(1, D_MODEL, 2 * D_RNN // N_DEV)
    small = sum_a[MATRIX_ROWS_A:MATRIX_ROWS_A + 8].reshape(-1)
    shapes = {"norm_a": (1, D_MODEL // N_DEV), "conv_w": (1, CONV_WIDTH, D_RNN // N_DEV), "conv_b": (1, D_RNN // N_DEV),
              "b_rg": (1, D_RNN // N_DEV), "b_ig": (1, D_RNN // N_DEV), "lru_lambda": (1, D_RNN // N_DEV)}
    off = 0
    for n, k in _SMALL:
        out[n] = small[off:off + k].reshape(shapes[n])
        off += k
    piece = {n: sum_b[lo:hi] for n, (lo, hi) in _OFF_B.items()}
    out["w_out_a"] = piece["w_out_a"].reshape(1, D_RNN // N_DEV, D_MODEL)
    out["w_dkv"] = piece["w_dkv"].reshape(D_MODEL // N_DEV, KV_RANK + QK_ROPE)
    out["w_uk"] = piece["w_uk"].reshape(KV_RANK // N_DEV, N_HEADS, QK_NOPE)
    out["w_uv"] = piece["w_uv"].reshape(KV_RANK // N_DEV, N_HEADS, V_DIM)
    out["w_in_b"] = piece["w_in_b"].T.reshape(1, D_MODEL, (Q_RANK + N_HEADS * V_DIM) // N_DEV)
    out["w_uq"] = piece["w_uq"].reshape(1, Q_RANK // N_DEV, N_HEADS, HEAD_PAD)[..., :QK_NOPE + QK_ROPE]
    out["w_out_b"] = sum_c.reshape(1, N_HEADS * V_DIM // N_DEV, D_MODEL)
    return out


def _step(x, target, w, rep, block_b, block_c, *, bsz, seq):
    t = bsz * seq
    cos, sin = _rope_tables(seq)
    g_a = w["norm_a"]
    g_kv = rep["norm_kv"].reshape(1, -1)
    g_kvn = rep["kv_norm"].reshape(1, -1)
    g_b = rep["norm_b"].reshape(1, -1)
    g_q = rep["q_norm"].reshape(1, -1)
    g_f = rep["final_norm"].reshape(1, -1)
    wrg = rep["w_rg"][0].astype(BF16)
    wig = rep["w_ig"][0].astype(BF16)
    cw8, vecs = w["conv_taps"], w["lru_vecs"]

    def seq3(a):
        return a.reshape(bsz, seq, a.shape[-1])

    def flat(a):
        return a.reshape(t, a.shape[-1])

    h0, xp, ga = _lru_proj_fwd(x, g_a, w["w_in_a_t"], name="lru_proj_fwd")
    xb, hs, y, wall_b = _lru_fwd(seq3(xp), seq3(ga), cw8, vecs, wrg, wig, block_b, name="lru_fwd")
    w = dict(w, **_weights_b(wall_b))
    x1 = _matmul(flat(y), w["w_out_a"], residual=x, name="out_a")
    hk, hq, ck, cqp, g2, ckv, cq, q, kn, v, kr, kn_t, v_t, kr_t = _mla_proj_fwd(
        x1, (g_kv, g_b, g_kvn, g_q), w, cos, sin, seq=seq, name="mla_proj_fwd")
    o, lse, wall_c = _attn_fwd(q, kn, kr, v_t, block_c, bsz=bsz, seq=seq, name="attn_fwd")
    w_out_b = wall_c.reshape(N_HEADS * V_DIM, D_MODEL)
    loss, dx2, y2, do, dg2, dgf = _head_and_loss(o, g2, x1, target, w_out_b, g_f, name="head_loss")
    grads = {"final_norm": dgf}
    parts_c = _by_owner(_matmul_tn(y2, dx2, name="d_w_out_b")).astype(BF16)
    dq, dkn, dkr, dv, landed_c = _attn_bwd(q, kn, kr, kn_t, kr_t, v, o, lse, do, cos, sin, parts_c,
                                           bsz=bsz, seq=seq, name="attn_bwd")
    grads["w_uq"] = _matmul_tn(cq, dq, name="d_w_uq")
    dx1, du2, dckr, dgkv, dgb, dgkvn, dgq = _mla_proj_bwd(
        x1, dx2, cqp, ck, dq, dkn, dv, dkr, dg2, (g_kv, g_b, g_kvn, g_q), w, name="mla_proj_bwd")
    grads["norm_kv"], grads["norm_b"], grads["kv_norm"], grads["q_norm"] = dgkv, dgb, dgkvn, dgq
    grads["w_in_b"] = _matmul_tn(du2, hq, name="d_w_in_b_t")
    grads["w_uk"] = _matmul_tn(ckv, dkn, name="d_w_uk")
    grads["w_uv"] = _matmul_tn(ckv, dv, name="d_w_uv")
    grads["w_dkv"] = _matmul_tn(hk, dckr, name="d_w_dkv")[:, :KV_RANK + QK_ROPE]
    grads["w_out_a"] = _matmul_tn(flat(y), dx1, name="d_w_out_a")
    parts_b = _grad_parts_b(grads)
    dy = _matmul(dx1, w["w_out_a"], nt=True, name="d_y")
    dxp, dga, dwrg, dwig, dvec, landed_b = _lru_bwd(
        seq3(dy), seq3(xp), xb, hs, seq3(ga), cw8, vecs, wrg, wig, parts_b, name="lru_bwd")
    dxp, dga = flat(dxp), flat(dga)
    grads["w_rg"], grads["w_ig"] = dwrg, dwig
    grads["b_rg"], grads["b_ig"], grads["conv_b"] = dvec[0], dvec[1], dvec[3]
    lam = vecs[3]
    grads["lru_lambda"] = dvec[2] * (-1.0 / (1.0 + jnp.exp(lam)))
    grads["conv_w"] = dvec[4:4 + CONV_WIDTH]
    grads["w_in_a_t"] = (_matmul_tn(dxp, h0, name="d_w_in_a_x_t"), _matmul_tn(dga, h0, name="d_w_in_a_g_t"))
    dx, d_norm_a, landed_a = _lru_proj_bwd(dxp, dga, x, dx1, g_a, w["w_in_a_t"], _grad_parts_a(grads), name="lru_proj_bwd")
    return loss[0, 0], dx, d_norm_a, landed_a, landed_b, landed_c


def kernel(x, norm_a, w_in_a, conv_w, conv_b, w_rg, b_rg, w_ig, b_ig, lru_lambda, w_out_a, norm_kv, w_dkv, kv_norm, w_uk, w_uv, norm_b, w_in_b, q_norm, w_uq, w_out_b, final_norm, loss_target, m_norm_a, m_w_in_a, m_conv_w, m_conv_b, m_w_rg, m_b_rg, m_w_ig, m_b_ig, m_lru_lambda, m_w_out_a, m_norm_kv, m_w_dkv, m_kv_norm, m_w_uk, m_w_uv, m_norm_b, m_w_in_b, m_q_norm, m_w_uq, m_w_out_b, m_final_norm, v_norm_a, v_w_in_a, v_conv_w, v_conv_b, v_w_rg, v_b_rg, v_w_ig, v_b_ig, v_lru_lambda, v_w_out_a, v_norm_kv, v_w_dkv, v_kv_norm, v_w_uk, v_w_uv, v_norm_b, v_w_in_b, v_q_norm, v_w_uq, v_w_out_b, v_final_norm):
    given = dict(locals())
    wts = {n: given[n] for n in WEIGHTS}
    mom1 = {n: given["m_" + n] for n in WEIGHTS}
    mom2 = {n: given["v_" + n] for n in WEIGHTS}
    bsz, seq, _ = x.shape
    t = bsz * seq

    block_a, block_b, block_c = _weight_blocks(wts)
    w = _weights_a(_all_gather(block_a, name="gather_weights_a"))
    loss, dx, d_norm_a, landed_a, landed_b, landed_c = _step(
        x.reshape(t, D_MODEL), loss_target.reshape(t, D_MODEL), w, wts, block_b, block_c, bsz=bsz, seq=seq)
    loss = lax.psum(loss, MESH_AXES)

    sum_a = _sum_parts(landed_a, name="sum_grads_a", br=GRAD_BLOCK)
    sum_b = _sum_parts(landed_b, name="sum_grads_b", br=WIRE_ROWS_B // 2)
    sum_c = _sum_parts(landed_c, name="sum_grads_c", br=landed_c.shape[1])
    g_own = _own_grads(sum_a, sum_b, sum_c)
    rep_slice = sum_a[MATRIX_ROWS_A + 8:MATRIX_ROWS_A + 8 + REP_SLICE]
    gathered = _all_gather(jnp.concatenate([rep_slice, jnp.pad(d_norm_a, ((0, 7), (0, 0)))], axis=0), name="gather_replicated")
    g_own.update(_unpack_rep(gathered[:, :REP_SLICE].reshape(REP_ROWS, PACK_W), wts))
    me = 4 * lax.axis_index("x") + 2 * lax.axis_index("y") + lax.axis_index("c")
    width = D_MODEL // N_DEV
    shares = lax.dynamic_slice_in_dim(gathered[:, REP_SLICE, :], me * width, width, axis=1)
    g_own["norm_a"] = _sum_parts(shares.reshape(N_DEV, 1, width), name="sum_norm_a", br=1)

    deltas, new_m, new_v = {}, {}, {}
    for n in WEIGHTS:
        deltas[n], new_m[n], new_v[n] = _adamw(g_own[n], wts[n], mom1[n], mom2[n], name="adamw_" + n)
    result = [loss, dx.reshape(bsz, seq, D_MODEL)]
    for d in (g_own, deltas, new_m, new_v):
        result.extend(d[n] for n in WEIGHTS)
    return tuple(result)
```

```python
import jax
import jax.numpy as jnp
from jax import lax
from jax.experimental import pallas as pl
from jax.experimental.pallas import tpu as pltpu

F32 = jnp.float32
BF16 = jnp.bfloat16
WIRE = jnp.bfloat16

D_MODEL = 1024
D_RNN = 1280
RNN_BLOCKS = 10
RNN_BW = 128
CONV_WIDTH = 4
LRU_C = 8.0
N_HEADS = 8
QK_NOPE = 128
QK_ROPE = 64
V_DIM = 128
KV_RANK = 256
Q_RANK = 384
ROPE_THETA = 10000.0
EPS = 1e-6
ATTN_SCALE = (QK_NOPE + QK_ROPE) ** -0.5
HEAD_PAD = 256
LANES = 128

ADAM_LR = 0.001
ADAM_B1 = 0.9
ADAM_B2 = 0.999
ADAM_EPS = 1e-08
ADAM_WD = 0.01
ADAM_STEP = 10

N_DEV = 8
VMEM_LIMIT_BYTES = 56 * 2**20
PACK_W = 1024

_PIECES_A = (("w_in_a", 320),)
_PIECES_B = (("w_out_a", 160), ("w_dkv", 40), ("w_uk", 32), ("w_uv", 32), ("w_in_b", 176), ("w_uq", 96))


def _offsets(pieces):
    off, r = {}, 0
    for n, k in pieces:
        off[n] = (r, r + k)
        r += k
    return off, r


_OFF_A, MATRIX_ROWS_A = _offsets(_PIECES_A)
_OFF_B, MATRIX_ROWS_B = _offsets(_PIECES_B)
WIRE_ROWS_A = MATRIX_ROWS_A + 16
WIRE_ROWS_B = 544
_SMALL = (("norm_a", 128), ("conv_w", 640), ("conv_b", 160), ("b_rg", 160), ("b_ig", 160), ("lru_lambda", 160))
_REP = (("w_rg", 163840), ("w_ig", 163840), ("norm_kv", 1024), ("kv_norm", 256), ("norm_b", 1024),
        ("q_norm", 384), ("final_norm", 1024))
REP_ROWS = 384
REP_SLICE = REP_ROWS // N_DEV
GRAD_ROWS_A = 384
GRAD_BLOCK = 192

WEIGHTS = ("norm_a", "w_in_a", "conv_w", "conv_b", "w_rg", "b_rg", "w_ig", "b_ig", "lru_lambda", "w_out_a",
           "norm_kv", "w_dkv", "kv_norm", "w_uk", "w_uv", "norm_b", "w_in_b", "q_norm", "w_uq", "w_out_b",
           "final_norm")


def _params(sem=None):
    return pltpu.CompilerParams(dimension_semantics=sem, vmem_limit_bytes=VMEM_LIMIT_BYTES)


_NT = (((1,), (1,)), ((), ()))
_ANY = pl.BlockSpec(memory_space=pl.ANY)


def _mesh_pos():
    return lax.axis_index("x"), lax.axis_index("y"), lax.axis_index("c")


def _sigmoid(z):
    return 0.5 * jnp.tanh(0.5 * z) + 0.5


def _sigmoid_tail(z):
    return 1.0 / (1.0 + jnp.exp(-z))


def _col_block(n):
    return n if n <= 1408 else n // 2


def _matmul(a, b, *, name, nt=False, out_dtype=F32, residual=None, bm=512):
    m, k = a.shape
    n = b.shape[0] if nt else b.shape[1]
    bm = min(bm, m)
    bn = _col_block(n)
    dims = (((1,), (1,)), ((), ())) if nt else (((1,), (0,)), ((), ()))
    has_res = residual is not None

    def body(*refs):
        a_ref, b_ref, o_ref = refs[0], refs[1], refs[-1]
        acc = lax.dot_general(a_ref[...].astype(BF16), b_ref[...].astype(BF16), dims, preferred_element_type=F32)
        if has_res:
            acc = acc + refs[2][...]
        o_ref[...] = acc.astype(out_dtype)

    in_specs = [pl.BlockSpec((bm, k), lambda i, j: (i, 0)),
                pl.BlockSpec((bn, k), lambda i, j: (j, 0)) if nt else pl.BlockSpec((k, bn), lambda i, j: (0, j))]
    args = [a, b]
    if has_res:
        in_specs.append(pl.BlockSpec((bm, bn), lambda i, j: (i, j)))
        args.append(residual)
    return pl.pallas_call(
        body, grid=(m // bm, n // bn), in_specs=in_specs, out_specs=pl.BlockSpec((bm, bn), lambda i, j: (i, j)),
        out_shape=jax.ShapeDtypeStruct((m, n), out_dtype), compiler_params=_params(("parallel", "parallel")),
        name=name)(*args)


def _matmul_tn(a, b, *, name, bt=512):
    t, m = a.shape
    n = b.shape[1]
    bt = min(bt, t)
    bm, bn = _col_block(m), _col_block(n)

    def body(a_ref, b_ref, o_ref):
        @pl.when(pl.program_id(2) == 0)
        def _():
            o_ref[...] = jnp.zeros_like(o_ref)

        o_ref[...] += lax.dot_general(a_ref[...].astype(BF16), b_ref[...].astype(BF16),
                                      (((0,), (0,)), ((), ())), preferred_element_type=F32)

    return pl.pallas_call(
        body, grid=(m // bm, n // bn, t // bt),
        in_specs=[pl.BlockSpec((bt, bm), lambda i, j, s: (s, i)), pl.BlockSpec((bt, bn), lambda i, j, s: (s, j))],
        out_specs=pl.BlockSpec((bm, bn), lambda i, j, s: (i, j)),
        out_shape=jax.ShapeDtypeStruct((m, n), F32),
        compiler_params=_params(("parallel", "parallel", "arbitrary")), name=name)(a, b)


def _swap_halves(v):
    ax = v.ndim - 1
    lane = lax.broadcasted_iota(jnp.int32, v.shape, ax)
    up = pltpu.roll(v, LANES - QK_ROPE // 2, axis=ax)
    down = pltpu.roll(v, QK_ROPE // 2, axis=ax)
    return jnp.where(lane < QK_ROPE // 2, up, jnp.where(lane < QK_ROPE, down, 0.0))


def _rope(v, cos, sin):
    return v * cos + _swap_halves(v) * sin


def _rope_t(d, cos, sin):
    return d * cos + _swap_halves(d * sin)


def _rope_tables(seq):
    pos = jnp.arange(seq, dtype=F32)
    inv = ROPE_THETA ** (-jnp.arange(0, QK_ROPE, 2, dtype=F32) / QK_ROPE)
    ang = pos[:, None] * inv[None, :]
    cos, sin = jnp.cos(ang), jnp.sin(ang)
    zero = jnp.zeros((seq, LANES - QK_ROPE), F32)
    return jnp.concatenate([cos, cos, zero], axis=1), jnp.concatenate([-sin, sin, zero], axis=1)


def _rms(v):
    return v * lax.rsqrt(jnp.mean(v * v, axis=-1, keepdims=True) + EPS)


def _const_spec(a):
    return pl.BlockSpec(a.shape, lambda i: (0,) * a.ndim)


def _lru_proj_fwd(x, g_a, w_in_t, *, name, bt=512):
    t, d = x.shape
    bt = min(bt, t)
    n = w_in_t.shape[0] // 2

    def body(x_ref, g_ref, wt_ref, h_ref, xp_ref, ga_ref):
        h = (_rms(x_ref[...]) * g_ref[...]).astype(BF16)
        h_ref[...] = h
        xp_ref[...] = lax.dot_general(h, wt_ref[0:n, :], _NT, preferred_element_type=F32)
        ga_ref[...] = lax.dot_general(h, wt_ref[n:2 * n, :], _NT, preferred_element_type=F32)

    row = lambda w: pl.BlockSpec((bt, w), lambda i: (i, 0))
    return pl.pallas_call(
        body, grid=(t // bt,), in_specs=[row(d), _const_spec(g_a), _const_spec(w_in_t)],
        out_specs=[row(d), row(n), row(n)],
        out_shape=[jax.ShapeDtypeStruct((t, d), BF16), jax.ShapeDtypeStruct((t, n), F32), jax.ShapeDtypeStruct((t, n), F32)],
        compiler_params=_params(("parallel",)), name=name)(x, g_a, w_in_t)


def _mla_proj_fwd(x1, gains, w, cos, sin, *, seq, name, bt=512):
    t, d = x1.shape
    bt = min(bt, seq)
    per_seq = seq // bt
    g_kv, g_b, g_kvn, g_q = gains
    consts = [g_kv, g_b, g_kvn, g_q, w["w_dkv_c"], w["w_dkv_r"], w["w_in_b_t"], w["w_uk"], w["w_uv"],
              w["w_uk_t"], w["w_uv_t"], w["w_uq"]]

    def body(x_ref, cos_ref, sin_ref, gkv_ref, gb_ref, gkvn_ref, gq_ref, wdc_ref, wdr_ref, wbt_ref,
             wuk_ref, wuv_ref, wukt_ref, wuvt_ref, wuq_ref,
             hk_ref, hq_ref, ck_ref, cqp_ref, g2_ref, ckv_ref, cq_ref, q_ref, kn_ref, v_ref, kr_ref, knt_ref, vt_ref, krt_ref):
        nrm = _rms(x_ref[...])
        hk = (nrm * gkv_ref[...]).astype(BF16)
        hq = (nrm * gb_ref[...]).astype(BF16)
        hk_ref[...] = hk
        hq_ref[...] = hq
        ck = jnp.dot(hk, wdc_ref[...], preferred_element_type=F32)
        ck_ref[...] = ck
        cqp = lax.dot_general(hq, wbt_ref[0:Q_RANK, :], _NT, preferred_element_type=F32)
        cqp_ref[...] = cqp
        g2_ref[...] = lax.dot_general(hq, wbt_ref[Q_RANK:, :], _NT, preferred_element_type=F32)
        cosv, sinv = cos_ref[...], sin_ref[...]
        kr = _rope(jnp.dot(hk, wdr_ref[...], preferred_element_type=F32), cosv, sinv)
        kr_ref[...] = kr.astype(BF16)
        krt_ref[...] = kr.T.astype(BF16)
        ckv = (_rms(ck) * gkvn_ref[...]).astype(BF16)
        ckv_ref[...] = ckv
        kn_ref[...] = jnp.dot(ckv, wuk_ref[...], preferred_element_type=F32).astype(BF16)
        v_ref[...] = jnp.dot(ckv, wuv_ref[...], preferred_element_type=F32).astype(BF16)
        knt_ref[...] = lax.dot_general(wukt_ref[...], ckv, _NT, preferred_element_type=F32).astype(BF16)
        vt_ref[...] = lax.dot_general(wuvt_ref[...], ckv, _NT, preferred_element_type=F32).astype(BF16)
        cq = (_rms(cqp) * gq_ref[...]).astype(BF16)
        cq_ref[...] = cq
        for h in range(N_HEADS):
            qh = jnp.dot(cq, wuq_ref[:, h * HEAD_PAD:(h + 1) * HEAD_PAD], preferred_element_type=F32)
            q_ref[:, h * HEAD_PAD:h * HEAD_PAD + QK_NOPE] = qh[:, :QK_NOPE].astype(BF16)
            q_ref[:, h * HEAD_PAD + QK_NOPE:(h + 1) * HEAD_PAD] = _rope(qh[:, QK_NOPE:], cosv, sinv).astype(BF16)

    row = lambda w_: pl.BlockSpec((bt, w_), lambda i: (i, 0))
    col = lambda h_: pl.BlockSpec((h_, bt), lambda i: (0, i))
    tab = pl.BlockSpec((bt, LANES), lambda i: (i % per_seq, 0))
    nh = N_HEADS * V_DIM
    shapes = [((t, d), BF16), ((t, d), BF16), ((t, KV_RANK), F32), ((t, Q_RANK), F32), ((t, nh), F32), ((t, KV_RANK), BF16),
              ((t, Q_RANK), BF16), ((t, N_HEADS * HEAD_PAD), BF16), ((t, nh), BF16), ((t, nh), BF16), ((t, LANES), BF16),
              ((nh, t), BF16), ((nh, t), BF16), ((LANES, t), BF16)]
    out_specs = [row(d), row(d), row(KV_RANK), row(Q_RANK), row(nh), row(KV_RANK), row(Q_RANK), row(N_HEADS * HEAD_PAD),
                 row(nh), row(nh), row(LANES), col(nh), col(nh), col(LANES)]
    return pl.pallas_call(
        body, grid=(t // bt,), in_specs=[row(d), tab, tab] + [_const_spec(a) for a in consts], out_specs=out_specs,
        out_shape=[jax.ShapeDtypeStruct(s, dt) for s, dt in shapes],
        compiler_params=_params(("parallel",)), name=name)(x1, cos, sin, *consts)


def _rms_bwd_rows(xv, dn):
    r = lax.rsqrt(jnp.mean(xv * xv, axis=-1, keepdims=True) + EPS)
    nrm = xv * r
    return r * (dn - nrm * jnp.mean(dn * nrm, axis=-1, keepdims=True)), nrm


def _col_sum(v):
    return jnp.sum(v, axis=0, keepdims=True)


def _lru_proj_bwd(dxp, dga, x, dx1, g_a, w_in_t, *, name, bt=512):
    t, d = x.shape
    bt = min(bt, t)
    n = w_in_t.shape[0] // 2

    def body(dxp_ref, dga_ref, x_ref, dx1_ref, g_ref, wt_ref, dx_ref, dg_ref):
        @pl.when(pl.program_id(0) == 0)
        def _():
            dg_ref[...] = jnp.zeros_like(dg_ref)

        dh = (jnp.dot(dxp_ref[...], wt_ref[0:n, :], preferred_element_type=F32)
              + jnp.dot(dga_ref[...], wt_ref[n:2 * n, :], preferred_element_type=F32))
        dxn, nrm = _rms_bwd_rows(x_ref[...], dh * g_ref[...])
        dg_ref[...] += _col_sum(dh * nrm)
        dx_ref[...] = dx1_ref[...] + dxn

    row = lambda w: pl.BlockSpec((bt, w), lambda i: (i, 0))
    return pl.pallas_call(
        body, grid=(t // bt,),
        in_specs=[row(n), row(n), row(d), row(d), _const_spec(g_a), _const_spec(w_in_t)],
        out_specs=[row(d), _const_spec(g_a)],
        out_shape=[jax.ShapeDtypeStruct((t, d), F32), jax.ShapeDtypeStruct((1, d), F32)],
        compiler_params=_params(("arbitrary",)), name=name)(dxp, dga, x, dx1, g_a, w_in_t)


def _mla_proj_bwd(x1, dx2, cqp, ck, dq, dkn, dv, dkr, dg2, gains, w, *, name, bt=512):
    t, d = x1.shape
    bt = min(bt, t)
    g_kv, g_b, g_kvn, g_q = gains
    consts = [g_kv, g_b, g_kvn, g_q, w["w_dkv_c"], w["w_dkv_r"], w["w_in_b_t"], w["w_uk"], w["w_uv"], w["w_uq"]]
    nh = N_HEADS * V_DIM

    def body(x1_ref, dx2_ref, cqp_ref, ck_ref, dq_ref, dkn_ref, dv_ref, dkr_ref, dg2_ref,
             gkv_ref, gb_ref, gkvn_ref, gq_ref, wdc_ref, wdr_ref, wbt_ref, wuk_ref, wuv_ref, wuq_ref,
             dx1_ref, du2_ref, dckr_ref, dgkv_ref, dgb_ref, dgkvn_ref, dgq_ref):
        @pl.when(pl.program_id(0) == 0)
        def _():
            for ref in (dgkv_ref, dgb_ref, dgkvn_ref, dgq_ref):
                ref[...] = jnp.zeros_like(ref)

        dot_nt = lambda a, b: lax.dot_general(a, b, _NT, preferred_element_type=F32)
        dcq = dot_nt(dq_ref[...], wuq_ref[...])
        dcqp, nq = _rms_bwd_rows(cqp_ref[...], dcq * gq_ref[...])
        dgq_ref[...] += _col_sum(dcq * nq)
        dcqp = dcqp.astype(BF16)
        dg2 = dg2_ref[...]
        du2_ref[:, :Q_RANK] = dcqp
        du2_ref[:, Q_RANK:] = dg2
        dhq = (jnp.dot(dcqp, wbt_ref[0:Q_RANK, :], preferred_element_type=F32)
               + jnp.dot(dg2, wbt_ref[Q_RANK:, :], preferred_element_type=F32))
        dckv = dot_nt(dkn_ref[...], wuk_ref[...]) + dot_nt(dv_ref[...], wuv_ref[...])
        dck, nc = _rms_bwd_rows(ck_ref[...], dckv * gkvn_ref[...])
        dgkvn_ref[...] += _col_sum(dckv * nc)
        dck = dck.astype(BF16)
        dkr = dkr_ref[...].astype(BF16)
        dckr_ref[:, :KV_RANK] = dck
        dckr_ref[:, KV_RANK:] = dkr
        dhk = dot_nt(dck, wdc_ref[...]) + dot_nt(dkr, wdr_ref[...])
        dxn, n1 = _rms_bwd_rows(x1_ref[...], dhq * gb_ref[...] + dhk * gkv_ref[...])
        dgb_ref[...] += _col_sum(dhq * n1)
        dgkv_ref[...] += _col_sum(dhk * n1)
        dx1_ref[...] = dx2_ref[...] + dxn

    row = lambda w_: pl.BlockSpec((bt, w_), lambda i: (i, 0))
    vec = lambda w_: pl.BlockSpec((1, w_), lambda i: (0, 0))
    in_specs = [row(d), row(d), row(Q_RANK), row(KV_RANK), row(N_HEADS * HEAD_PAD), row(nh), row(nh), row(LANES), row(nh)]
    return pl.pallas_call(
        body, grid=(t // bt,), in_specs=in_specs + [_const_spec(a) for a in consts],
        out_specs=[row(d), row(Q_RANK + nh), row(KV_RANK + LANES), vec(d), vec(d), vec(KV_RANK), vec(Q_RANK)],
        out_shape=[jax.ShapeDtypeStruct((t, d), F32), jax.ShapeDtypeStruct((t, Q_RANK + nh), BF16),
                   jax.ShapeDtypeStruct((t, KV_RANK + LANES), BF16), jax.ShapeDtypeStruct((1, d), F32),
                   jax.ShapeDtypeStruct((1, d), F32), jax.ShapeDtypeStruct((1, KV_RANK), F32),
                   jax.ShapeDtypeStruct((1, Q_RANK), F32)],
        compiler_params=_params(("arbitrary",)), name=name)(x1, dx2, cqp, ck, dq, dkn, dv, dkr, dg2, *consts)


def _softplus(z):
    return jnp.maximum(z, 0.0) + jnp.log1p(jnp.exp(-jnp.abs(z)))


def _one_minus_square(a, la):
    return jnp.tanh(-la) * (1.0 + a * a)


def _gates(xb, wrg, wig, brg, big, sp):
    xbb = xb.astype(BF16)
    r = _sigmoid_tail(jnp.dot(xbb, wrg, preferred_element_type=F32) + brg)
    i = _sigmoid(jnp.dot(xbb, wig, preferred_element_type=F32) + big)
    la = (-LRU_C) * r * sp
    a = jnp.exp(la)
    em = _one_minus_square(a, la)
    inv_mult = lax.rsqrt(em)
    mult = jnp.where(em > 0.0, em * inv_mult, 0.0)
    return r, i, a, mult, inv_mult


def _conv(xpad_ref, cw_ref, seq):
    acc = cw_ref[0:1, :] * xpad_ref[pl.ds(8 - (CONV_WIDTH - 1), seq), :]
    for k in range(1, CONV_WIDTH):
        acc = acc + cw_ref[k:k + 1, :] * xpad_ref[pl.ds(8 - (CONV_WIDTH - 1) + k, seq), :]
    return acc


def _seq_spec(seq):
    return pl.BlockSpec((None, seq, RNN_BW), lambda n, b: (b, 0, n))


def _chan_spec(rows):
    return pl.BlockSpec((rows, RNN_BW), lambda n, b: (0, n))


_GATE_W_SPEC = pl.BlockSpec((None, RNN_BW, RNN_BW), lambda n, b: (n, 0, 0))


SCAN_UNROLL = 4


def _peers():
    x, y, c = _mesh_pos()
    others = []
    for k in range(1, N_DEV):
        px = 1 - x if k & 4 else x
        py = 1 - y if k & 2 else y
        pc = 1 - c if k & 1 else c
        others.append(((px, py, pc), 4 * px + 2 * py + pc))
    return 4 * x + 2 * y + c, others


def _exchange(src_ref, dst_ref, send_sems, recv_sems, local_sem, *, finish, gather=False):
    me, others = _peers()

    def send(k, dev, slot):
        return pltpu.make_async_remote_copy(
            src_ref=src_ref if gather else src_ref.at[slot], dst_ref=dst_ref.at[me], send_sem=send_sems.at[k],
            recv_sem=recv_sems.at[k], device_id=dev, device_id_type=pl.DeviceIdType.MESH)

    local = pltpu.make_async_copy(src_ref if gather else src_ref.at[me], dst_ref.at[me], local_sem)
    if not finish:
        local.start()
        for k, (dev, slot) in enumerate(others):
            send(k, dev, slot).start()
        return
    for k, (dev, slot) in enumerate(others):
        pltpu.make_async_remote_copy(
            src_ref=dst_ref.at[slot], dst_ref=dst_ref.at[slot], send_sem=send_sems.at[k], recv_sem=recv_sems.at[k],
            device_id=dev, device_id_type=pl.DeviceIdType.MESH).wait_recv()
    for k, (dev, slot) in enumerate(others):
        send(k, dev, slot).wait_send()
    local.wait()


def _gather_two_level(x_ref, out_ref, send_sems, recv_sems, local_sem, *, phase):
    x, y, c = _mesh_pos()
    me, sibling = (x, y, c), (x, y, 1 - c)
    chips = [(1 - x, y), (x, 1 - y), (1 - x, 1 - y)]

    def slot(px, py, pc):
        return out_ref.at[4 * px + 2 * py + pc]

    def copy(k, blk, to, src=None):
        return pltpu.make_async_remote_copy(
            src_ref=slot(*blk) if src is None else src, dst_ref=slot(*blk),
            send_sem=send_sems.at[k], recv_sem=recv_sems.at[k], device_id=to, device_id_type=pl.DeviceIdType.MESH)

    if phase == 0:
        pltpu.make_async_copy(x_ref, slot(*me), local_sem).start()
        copy(0, me, sibling, src=x_ref).start()
        for j, chip in enumerate(chips):
            copy(1 + j, me, (*chip, c), src=x_ref).start()
    elif phase == 1:
        for j, chip in enumerate(chips):
            copy(1 + j, (*chip, c), me).wait_recv()
            copy(4 + j, (*chip, c), sibling).start()
    else:
        copy(0, sibling, me).wait_recv()
        for j, chip in enumerate(chips):
            copy(4 + j, (*chip, 1 - c), me).wait_recv()
        copy(0, me, sibling, src=x_ref).wait_send()
        for j, chip in enumerate(chips):
            copy(1 + j, me, (*chip, c), src=x_ref).wait_send()
            copy(4 + j, (*chip, c), sibling).wait_send()
        pltpu.make_async_copy(x_ref, slot(*me), local_sem).wait()


GATHER_FORWARD_STEP = 7
_EXCHANGE_SEMS = [pltpu.SemaphoreType.DMA((N_DEV - 1,)), pltpu.SemaphoreType.DMA((N_DEV - 1,)), pltpu.SemaphoreType.DMA(())]


def _first_last(steps):
    first = last = None
    for axis, n in enumerate(steps):
        i = pl.program_id(axis)
        first = (i == 0) if first is None else first & (i == 0)
        last = (i == n - 1) if last is None else last & (i == n - 1)
    return first, last


def _lru_fwd(xp, ga, cw, vecs, wrg, wig, block, *, name):
    bsz, seq, _ = xp.shape
    groups = seq // 8

    def body(xp_ref, ga_ref, cw_ref, vec_ref, wrg_ref, wig_ref, blk_ref, xb_ref, hs_ref, y_ref, all_ref,
             xpad, a_s, b_s, send_sems, recv_sems, local_sem):
        first, last = _first_last((RNN_BLOCKS, bsz))

        @pl.when(first)
        def _():
            _gather_two_level(blk_ref, all_ref, send_sems, recv_sems, local_sem, phase=0)

        @pl.when((pl.program_id(0) == GATHER_FORWARD_STEP) & (pl.program_id(1) == 0))
        def _():
            _gather_two_level(blk_ref, all_ref, send_sems, recv_sems, local_sem, phase=1)

        xpad[0:8, :] = jnp.zeros((8, RNN_BW), F32)
        xpad[pl.ds(8, seq), :] = xp_ref[...]
        xb = _conv(xpad, cw_ref, seq) + vec_ref[0:1, :]
        xb_ref[...] = xb
        sp = _softplus(-vec_ref[3:4, :])
        _, i, a, mult, _ = _gates(xb, wrg_ref[...], wig_ref[...], vec_ref[1:2, :], vec_ref[2:3, :], sp)
        a_s[...] = a
        b_s[...] = mult * (i * xb)
        row = lax.broadcasted_iota(jnp.int32, (8, RNN_BW), 0)

        def group(g, h):
            r0 = pl.multiple_of(g * 8, 8)
            av = a_s[pl.ds(r0, 8), :]
            bv = b_s[pl.ds(r0, 8), :]
            for k in (1, 2, 4):
                m = row >= k
                bv = jnp.where(m, av * pltpu.roll(bv, k, axis=0) + bv, bv)
                av = jnp.where(m, av * pltpu.roll(av, k, axis=0), av)
            hs_ref[pl.ds(r0, 8), :] = av * h + bv
            return av[7:8, :] * h + bv[7:8, :]

        def groups_of(i, h):
            for u in range(SCAN_UNROLL):
                h = group(i * SCAN_UNROLL + u, h)
            return h

        lax.fori_loop(0, groups // SCAN_UNROLL, groups_of, jnp.zeros((1, RNN_BW), F32))
        gav = ga_ref[...]
        y_ref[...] = (hs_ref[...] * (gav * _sigmoid(gav))).astype(BF16)

        @pl.when(last)
        def _():
            _gather_two_level(blk_ref, all_ref, send_sems, recv_sems, local_sem, phase=2)

    sq = _seq_spec(seq)
    shape = (bsz, seq, D_RNN)
    return pl.pallas_call(
        body, grid=(RNN_BLOCKS, bsz),
        in_specs=[sq, sq, _chan_spec(8), _chan_spec(8), _GATE_W_SPEC, _GATE_W_SPEC, _ANY],
        out_specs=[sq, sq, sq, _ANY],
        out_shape=[jax.ShapeDtypeStruct(shape, F32), jax.ShapeDtypeStruct(shape, F32), jax.ShapeDtypeStruct(shape, BF16),
                   jax.ShapeDtypeStruct((N_DEV,) + block.shape, block.dtype)],
        scratch_shapes=[pltpu.VMEM((seq + 8, RNN_BW), F32), pltpu.VMEM((seq, RNN_BW), F32), pltpu.VMEM((seq, RNN_BW), F32)]
        + _EXCHANGE_SEMS,
        compiler_params=_params(("arbitrary", "arbitrary")), name=name)(xp, ga, cw, vecs, wrg, wig, block)


def _lru_bwd(dy, xp, xb, hs, ga, cw, vecs, wrg, wig, parts, *, name):
    bsz, seq, _ = xp.shape
    groups = seq // 8

    def body(dy_ref, xp_ref, xb_ref, hs_ref, ga_ref, cw_ref, vec_ref, wrg_ref, wig_ref,
             parts_ref, dxp_ref, dga_ref, dwrg_ref, dwig_ref, dvec_ref, land_ref, pad, a_s, d_s, lam_s,
             send_sems, recv_sems, local_sem):
        first, last = _first_last((RNN_BLOCKS, bsz))

        @pl.when(first)
        def _():
            _exchange(parts_ref, land_ref, send_sems, recv_sems, local_sem, finish=False)

        @pl.when(pl.program_id(1) == 0)
        def _():
            dwrg_ref[...] = jnp.zeros_like(dwrg_ref)
            dwig_ref[...] = jnp.zeros_like(dwig_ref)
            dvec_ref[...] = jnp.zeros_like(dvec_ref)

        xb = xb_ref[...]
        hs = hs_ref[...]
        gav = ga_ref[...]
        dy = dy_ref[...]
        sp = _softplus(-vec_ref[3:4, :])
        wrg = wrg_ref[...]
        wig = wig_ref[...]
        r, i, a, mult, inv_mult = _gates(xb, wrg, wig, vec_ref[1:2, :], vec_ref[2:3, :], sp)
        sg = _sigmoid(gav)
        dga_ref[...] = (dy * hs * (sg * (1.0 + gav * (1.0 - sg)))).astype(BF16)
        d_s[...] = dy * (gav * sg)

        pad[pl.ds(0, seq), :] = a
        pad[pl.ds(seq, 8), :] = jnp.zeros((8, RNN_BW), F32)
        a_s[...] = pad[pl.ds(1, seq), :]
        row = lax.broadcasted_iota(jnp.int32, (8, RNN_BW), 0)

        def group(g, nxt):
            r0 = pl.multiple_of((groups - 1 - g) * 8, 8)
            cv = a_s[pl.ds(r0, 8), :]
            bv = d_s[pl.ds(r0, 8), :]
            for k in (1, 2, 4):
                m = row < 8 - k
                bv = jnp.where(m, cv * pltpu.roll(bv, 8 - k, axis=0) + bv, bv)
                cv = jnp.where(m, cv * pltpu.roll(cv, 8 - k, axis=0), cv)
            lam_s[pl.ds(r0, 8), :] = cv * nxt + bv
            return cv[0:1, :] * nxt + bv[0:1, :]

        def groups_of(i, nxt):
            for u in range(SCAN_UNROLL):
                nxt = group(i * SCAN_UNROLL + u, nxt)
            return nxt

        lax.fori_loop(0, groups // SCAN_UNROLL, groups_of, jnp.zeros((1, RNN_BW), F32))
        dh = lam_s[...]

        pad[0:8, :] = jnp.zeros((8, RNN_BW), F32)
        pad[pl.ds(8, seq), :] = hs
        da = dh * pad[pl.ds(7, seq), :]
        ixb = i * xb
        dixb = dh * mult
        dla = da * a - (dh * ixb) * (a * a) * inv_mult
        drp = (dla * ((-LRU_C) * sp)) * r * (1.0 - r)
        dip = (dixb * xb) * i * (1.0 - i)
        dvec_ref[0:1, :] += jnp.sum(drp, axis=0, keepdims=True)
        dvec_ref[1:2, :] += jnp.sum(dip, axis=0, keepdims=True)
        dvec_ref[2:3, :] += jnp.sum(dla * ((-LRU_C) * r), axis=0, keepdims=True)
        drpb = drp.astype(BF16)
        dipb = dip.astype(BF16)
        xbb = xb.astype(BF16)
        nt = (((1,), (1,)), ((), ()))
        tn = (((0,), (0,)), ((), ()))
        dxb = (dixb * i
               + lax.dot_general(drpb, wrg, nt, preferred_element_type=F32)
               + lax.dot_general(dipb, wig, nt, preferred_element_type=F32))
        dwrg_ref[...] += lax.dot_general(xbb, drpb, tn, preferred_element_type=F32)
        dwig_ref[...] += lax.dot_general(xbb, dipb, tn, preferred_element_type=F32)
        dvec_ref[3:4, :] += jnp.sum(dxb, axis=0, keepdims=True)

        pad[pl.ds(0, seq), :] = dxb
        pad[pl.ds(seq, 8), :] = jnp.zeros((8, RNN_BW), F32)
        dxp = cw_ref[0:1, :] * pad[pl.ds(CONV_WIDTH - 1, seq), :]
        for k in range(1, CONV_WIDTH):
            dxp = dxp + cw_ref[k:k + 1, :] * pad[pl.ds(CONV_WIDTH - 1 - k, seq), :]
        dxp_ref[...] = dxp.astype(BF16)
        pad[0:8, :] = jnp.zeros((8, RNN_BW), F32)
        pad[pl.ds(8, seq), :] = xp_ref[...]
        for k in range(CONV_WIDTH):
            dvec_ref[4 + k:5 + k, :] += jnp.sum(dxb * pad[pl.ds(8 - (CONV_WIDTH - 1) + k, seq), :], axis=0, keepdims=True)

        @pl.when(last)
        def _():
            _exchange(parts_ref, land_ref, send_sems, recv_sems, local_sem, finish=True)

    sq = _seq_spec(seq)
    shape = (bsz, seq, D_RNN)
    gshape = (RNN_BLOCKS, RNN_BW, RNN_BW)
    return pl.pallas_call(
        body, grid=(RNN_BLOCKS, bsz),
        in_specs=[sq, sq, sq, sq, sq, _chan_spec(8), _chan_spec(8), _GATE_W_SPEC, _GATE_W_SPEC, _ANY],
        out_specs=[sq, sq, _GATE_W_SPEC, _GATE_W_SPEC, _chan_spec(8), _ANY],
        out_shape=[jax.ShapeDtypeStruct(shape, BF16), jax.ShapeDtypeStruct(shape, BF16),
                   jax.ShapeDtypeStruct(gshape, F32), jax.ShapeDtypeStruct(gshape, F32),
                   jax.ShapeDtypeStruct((8, D_RNN), F32), jax.ShapeDtypeStruct(parts.shape, parts.dtype)],
        scratch_shapes=[pltpu.VMEM((seq + 8, RNN_BW), F32), pltpu.VMEM((seq, RNN_BW), F32),
                        pltpu.VMEM((seq, RNN_BW), F32), pltpu.VMEM((seq, RNN_BW), F32)] + _EXCHANGE_SEMS,
        compiler_params=_params(("arbitrary", "arbitrary")), name=name)(dy, xp, xb, hs, ga, cw, vecs, wrg, wig, parts)


def _attn_block(seq):
    return min(512, seq)


def _diag_mask(blk):
    return lax.broadcasted_iota(jnp.int32, (blk, blk), 0) <= lax.broadcasted_iota(jnp.int32, (blk, blk), 1)


FWD_HEADS = 4
BWD_HEADS = 2


def _attn_fwd(q, kn, kr, v_t, block, *, bsz, seq, name):
    t = bsz * seq
    blk = _attn_block(seq)
    nq = seq // blk
    hg = FWD_HEADS
    steps = (bsz, N_HEADS // hg, nq)

    def body(q_ref, kn_ref, kr_ref, vt_ref, blk_ref, o_ref, lse_ref, all_ref, acc, send_sems, recv_sems, local_sem):
        first, last = _first_last(steps)

        @pl.when(first)
        def _():
            _exchange(blk_ref, all_ref, send_sems, recv_sems, local_sem, finish=False, gather=True)

        qi = pl.program_id(2)
        acc[...] = jnp.zeros_like(acc)

        def step(j, carry, diagonal):
            k0 = pl.multiple_of(j * blk, blk)
            kr_j = kr_ref[pl.ds(k0, blk), :]
            out = []
            for h in range(hg):
                m_i, l_i = carry[h]
                kv = jnp.concatenate([kn_ref[pl.ds(k0, blk), h * QK_NOPE:(h + 1) * QK_NOPE], kr_j], axis=1)
                qv = q_ref[:, h * HEAD_PAD:(h + 1) * HEAD_PAD]
                s = lax.dot_general(kv, qv, _NT, preferred_element_type=F32) * ATTN_SCALE
                if diagonal:
                    s = jnp.where(_diag_mask(blk), s, -jnp.inf)
                m_new = jnp.maximum(m_i, jnp.max(s, axis=0, keepdims=True))
                p = jnp.exp(s - m_new)
                alpha = jnp.exp(m_i - m_new)
                l_new = alpha * l_i + jnp.sum(p, axis=0, keepdims=True)
                acc[h] = alpha * acc[h] + jnp.dot(vt_ref[h * V_DIM:(h + 1) * V_DIM, pl.ds(k0, blk)], p.astype(BF16),
                                                  preferred_element_type=F32)
                out.append((m_new, l_new))
            return tuple(out)

        init = tuple((jnp.full((1, blk), -jnp.inf, F32), jnp.zeros((1, blk), F32)) for _ in range(hg))
        carry = lax.fori_loop(0, qi, lambda j, c: step(j, c, False), init)
        stats = step(qi, carry, True)
        for h in range(hg):
            m_i, l_i = stats[h]
            o_ref[:, h * V_DIM:(h + 1) * V_DIM] = (acc[h] / l_i).T
            lse_ref[h] = m_i + jnp.log(l_i)

        @pl.when(last)
        def _():
            _exchange(blk_ref, all_ref, send_sems, recv_sems, local_sem, finish=True, gather=True)

    return pl.pallas_call(
        body, grid=steps,
        in_specs=[pl.BlockSpec((blk, hg * HEAD_PAD), lambda b, g, i: (b * nq + i, g)),
                  pl.BlockSpec((seq, hg * QK_NOPE), lambda b, g, i: (b, g)),
                  pl.BlockSpec((seq, LANES), lambda b, g, i: (b, 0)),
                  pl.BlockSpec((hg * V_DIM, seq), lambda b, g, i: (g, b)), _ANY],
        out_specs=[pl.BlockSpec((blk, hg * V_DIM), lambda b, g, i: (b * nq + i, g)),
                   pl.BlockSpec((hg, 1, blk), lambda b, g, i: (g, 0, b * nq + i)), _ANY],
        out_shape=[jax.ShapeDtypeStruct((t, N_HEADS * V_DIM), F32), jax.ShapeDtypeStruct((N_HEADS, 1, t), F32),
                   jax.ShapeDtypeStruct((N_DEV,) + block.shape, block.dtype)],
        scratch_shapes=[pltpu.VMEM((hg, V_DIM, blk), F32)] + _EXCHANGE_SEMS,
        compiler_params=_params(("arbitrary", "arbitrary", "arbitrary")), name=name)(q, kn, kr, v_t, block)


def _attn_bwd(q, kn, kr, kn_t, kr_t, v, o, lse, do, cos, sin, parts, *, bsz, seq, name):
    t = bsz * seq
    blk = _attn_block(seq)
    nq = seq // blk
    hg = BWD_HEADS
    steps = (bsz, N_HEADS // hg)

    def body(q_ref, kn_ref, kr_ref, knt_ref, krt_ref, v_ref, o_ref, lse_ref, do_ref, cos_ref, sin_ref, parts_ref,
             dq_ref, dkn_ref, dkr_ref, dv_ref, land_ref, dqt_acc, dk_acc, dv_acc, send_sems, recv_sems, local_sem):
        first, last = _first_last(steps)

        @pl.when(first)
        def _():
            _exchange(parts_ref, land_ref, send_sems, recv_sems, local_sem, finish=False)

        dqt_acc[...] = jnp.zeros_like(dqt_acc)
        dk_acc[...] = jnp.zeros_like(dk_acc)
        dv_acc[...] = jnp.zeros_like(dv_acc)

        def q_block(i, _):
            q0 = pl.multiple_of(i * blk, blk)
            rows = []
            for h in range(hg):
                dov = do_ref[pl.ds(q0, blk), h * V_DIM:(h + 1) * V_DIM].astype(F32)
                dcol = jnp.sum(dov * o_ref[pl.ds(q0, blk), h * V_DIM:(h + 1) * V_DIM], axis=-1, keepdims=True)
                delta = jnp.broadcast_to(dcol, (blk, LANES)).T[0:1, :]
                rows.append((lse_ref[h, :, pl.ds(q0, blk)], delta))

            def pair(j, diagonal):
                k0 = pl.multiple_of(j * blk, blk)
                kr_j = kr_ref[pl.ds(k0, blk), :]
                krt_j = krt_ref[:, pl.ds(k0, blk)]
                for h in range(hg):
                    lse_i, delta = rows[h]
                    qv = q_ref[pl.ds(q0, blk), h * HEAD_PAD:(h + 1) * HEAD_PAD]
                    dov = do_ref[pl.ds(q0, blk), h * V_DIM:(h + 1) * V_DIM]
                    kv = jnp.concatenate([kn_ref[pl.ds(k0, blk), h * QK_NOPE:(h + 1) * QK_NOPE], kr_j], axis=1)
                    s = lax.dot_general(kv, qv, _NT, preferred_element_type=F32) * ATTN_SCALE
                    p = jnp.exp(s - lse_i)
                    if diagonal:
                        p = jnp.where(_diag_mask(blk), p, 0.0)
                    dv_acc[pl.ds(k0, blk), h * V_DIM:(h + 1) * V_DIM] += jnp.dot(
                        p.astype(BF16), dov, preferred_element_type=F32)
                    dp = lax.dot_general(v_ref[pl.ds(k0, blk), h * V_DIM:(h + 1) * V_DIM], dov, _NT,
                                         preferred_element_type=F32)
                    ds = (p * (dp - delta) * ATTN_SCALE).astype(BF16)
                    dk_acc[pl.ds(k0, blk), h * HEAD_PAD:(h + 1) * HEAD_PAD] += jnp.dot(ds, qv, preferred_element_type=F32)
                    base = h * HEAD_PAD
                    dqt_acc[base:base + QK_NOPE, pl.ds(q0, blk)] += jnp.dot(
                        knt_ref[h * QK_NOPE:(h + 1) * QK_NOPE, pl.ds(k0, blk)], ds, preferred_element_type=F32)
                    dqt_acc[base + QK_NOPE:base + HEAD_PAD, pl.ds(q0, blk)] += jnp.dot(
                        krt_j, ds, preferred_element_type=F32)

            def off_diagonal(j, _):
                pair(j, False)
                return 0

            lax.fori_loop(0, i, off_diagonal, 0)
            pair(i, True)
            return 0

        lax.fori_loop(0, nq, q_block, 0)
        dkr = jnp.zeros((seq, LANES), F32)
        for h in range(hg):
            base = h * HEAD_PAD
            for i in range(nq):
                rows = slice(i * blk, (i + 1) * blk)
                dq = dqt_acc[base:base + HEAD_PAD, rows].T
                dq_ref[rows, base:base + QK_NOPE] = dq[:, :QK_NOPE].astype(BF16)
                dq_ref[rows, base + QK_NOPE:base + HEAD_PAD] = _rope_t(
                    dq[:, QK_NOPE:], cos_ref[rows, :], sin_ref[rows, :]).astype(BF16)
            dkn_ref[:, h * QK_NOPE:(h + 1) * QK_NOPE] = dk_acc[:, base:base + QK_NOPE].astype(BF16)
            dkr = dkr + dk_acc[:, base + QK_NOPE:base + HEAD_PAD]
        dv_ref[...] = dv_acc[...].astype(BF16)

        @pl.when(pl.program_id(1) == 0)
        def _():
            dkr_ref[...] = jnp.zeros_like(dkr_ref)

        dkr_ref[...] += _rope_t(dkr, cos_ref[...], sin_ref[...])

        @pl.when(last)
        def _():
            _exchange(parts_ref, land_ref, send_sems, recv_sems, local_sem, finish=True)

    head = pl.BlockSpec((seq, hg * V_DIM), lambda b, g: (b, g))
    head_t = pl.BlockSpec((hg * V_DIM, seq), lambda b, g: (g, b))
    shared = pl.BlockSpec((seq, LANES), lambda b, g: (b, 0))
    shared_t = pl.BlockSpec((LANES, seq), lambda b, g: (0, b))
    table = pl.BlockSpec((seq, LANES), lambda b, g: (0, 0))
    qspec = pl.BlockSpec((seq, hg * HEAD_PAD), lambda b, g: (b, g))
    return pl.pallas_call(
        body, grid=steps,
        in_specs=[qspec, head, shared, head_t, shared_t, head, head,
                  pl.BlockSpec((hg, 1, seq), lambda b, g: (g, 0, b)), head, table, table, _ANY],
        out_specs=[qspec, head, shared, head, _ANY],
        out_shape=[jax.ShapeDtypeStruct((t, N_HEADS * HEAD_PAD), BF16), jax.ShapeDtypeStruct((t, N_HEADS * QK_NOPE), BF16),
                   jax.ShapeDtypeStruct((t, LANES), F32), jax.ShapeDtypeStruct((t, N_HEADS * V_DIM), BF16),
                   jax.ShapeDtypeStruct(parts.shape, parts.dtype)],
        scratch_shapes=[pltpu.VMEM((hg * HEAD_PAD, seq), F32), pltpu.VMEM((seq, hg * HEAD_PAD), F32),
                        pltpu.VMEM((seq, hg * V_DIM), F32)] + _EXCHANGE_SEMS,
        compiler_params=_params(("arbitrary", "arbitrary")), name=name)(
            q, kn, kr, kn_t, kr_t, v, o, lse, do, cos, sin, parts)


def _head_and_loss(o, g2, x1, target, w_out, g_final, *, name, bt=512):
    t, d = x1.shape
    bt = min(bt, t)
    nt = (((1,), (1,)), ((), ()))

    def body(o_ref, g2_ref, x1_ref, tgt_ref, w_ref, gf_ref, loss_ref, dx2_ref, y2_ref, do_ref, dg2_ref, dgf_ref):
        @pl.when(pl.program_id(0) == 0)
        def _():
            loss_ref[...] = jnp.zeros_like(loss_ref)
            dgf_ref[...] = jnp.zeros_like(dgf_ref)

        ov = o_ref[...]
        gv = g2_ref[...]
        sg = _sigmoid(gv)
        silu = gv * sg
        y2 = (ov * silu).astype(BF16)
        y2_ref[...] = y2
        w = w_ref[...]
        x2 = x1_ref[...] + jnp.dot(y2, w, preferred_element_type=F32)
        r = lax.rsqrt(jnp.mean(x2 * x2, axis=-1, keepdims=True) + EPS)
        nrm = x2 * r
        gf = gf_ref[...]
        err = nrm * gf - tgt_ref[...]
        loss_ref[...] += 0.5 * jnp.sum(jnp.mean(err * err, axis=-1, keepdims=True))
        dyf = err * (1.0 / d)
        dgf_ref[...] += jnp.sum(dyf * nrm, axis=0, keepdims=True)
        dn = dyf * gf
        dx2 = r * (dn - nrm * jnp.mean(dn * nrm, axis=-1, keepdims=True))
        dx2_ref[...] = dx2
        dy2 = lax.dot_general(dx2.astype(BF16), w, nt, preferred_element_type=F32)
        do_ref[...] = (dy2 * silu).astype(BF16)
        dg2_ref[...] = (dy2 * ov * (sg * (1.0 + gv * (1.0 - sg)))).astype(BF16)

    row = pl.BlockSpec((bt, d), lambda i: (i, 0))
    vec = pl.BlockSpec((1, d), lambda i: (0, 0))
    return pl.pallas_call(
        body, grid=(t // bt,),
        in_specs=[row, row, row, row, pl.BlockSpec((d, d), lambda i: (0, 0)), vec],
        out_specs=[pl.BlockSpec((8, LANES), lambda i: (0, 0)), row, row, row, row, vec],
        out_shape=[jax.ShapeDtypeStruct((8, LANES), F32), jax.ShapeDtypeStruct((t, d), F32),
                   jax.ShapeDtypeStruct((t, d), BF16), jax.ShapeDtypeStruct((t, d), BF16),
                   jax.ShapeDtypeStruct((t, d), BF16), jax.ShapeDtypeStruct((1, d), F32)],
        compiler_params=_params(("arbitrary",)), name=name)(o, g2, x1, target, w_out, g_final)


def _sum_parts(parts, *, name, br=GRAD_BLOCK):
    npart, rows, w = parts.shape

    def body(p_ref, o_ref):
        acc = p_ref[0].astype(F32)
        for j in range(1, npart):
            acc = acc + p_ref[j].astype(F32)
        o_ref[...] = acc

    return pl.pallas_call(
        body, grid=(rows // br,), in_specs=[pl.BlockSpec((npart, br, w), lambda i: (0, i, 0))],
        out_specs=pl.BlockSpec((br, w), lambda i: (i, 0)), out_shape=jax.ShapeDtypeStruct((rows, w), F32),
        compiler_params=_params(("parallel",)), name=name)(parts)


def _chip_partial(parts, recv, *, name, br=GRAD_BLOCK):
    _, rows, w = parts.shape
    core = lax.axis_index("c").astype(jnp.int32).reshape(1)

    def body(c_ref, p_ref, r_ref, o_ref):
        o_ref[...] = (p_ref[...] + r_ref[...]).astype(BF16)

    grid_spec = pltpu.PrefetchScalarGridSpec(
        num_scalar_prefetch=1, grid=(4, rows // br),
        in_specs=[pl.BlockSpec((None, br, w), lambda k, i, c_ref: (2 * k + c_ref[0], i, 0)),
                  pl.BlockSpec((None, br, w), lambda k, i, c_ref: (k, i, 0))],
        out_specs=pl.BlockSpec((None, br, w), lambda k, i, c_ref: (k, i, 0)))
    return pl.pallas_call(
        body, grid_spec=grid_spec, out_shape=jax.ShapeDtypeStruct((4, rows, w), BF16),
        compiler_params=_params(("parallel", "parallel")), name=name)(core, parts, recv)


def _as_block(a):
    if a.ndim == 1:
        return a.reshape(1, -1)
    if a.ndim > 2 and a.shape[0] == 1:
        return a.reshape(a.shape[1:])
    return a


def _adamw(g, w, m, v, *, name):
    shape = w.shape
    g, w, m, v = (_as_block(a) for a in (g, w, m, v))

    def body(g_ref, w_ref, m_ref, v_ref, d_ref, nm_ref, nv_ref):
        gv = g_ref[...]
        nm = ADAM_B1 * m_ref[...] + (1.0 - ADAM_B1) * gv
        nv = ADAM_B2 * v_ref[...] + (1.0 - ADAM_B2) * (gv * gv)
        nm_ref[...] = nm
        nv_ref[...] = nv
        m_hat = nm / (1.0 - ADAM_B1 ** ADAM_STEP)
        v_hat = nv / (1.0 - ADAM_B2 ** ADAM_STEP)
        d_ref[...] = (-ADAM_LR) * (m_hat / (jnp.sqrt(v_hat) + ADAM_EPS) + ADAM_WD * w_ref[...])

    whole = pl.BlockSpec(memory_space=pltpu.VMEM)
    outs = pl.pallas_call(
        body, in_specs=[whole] * 4, out_specs=[whole] * 3, out_shape=[jax.ShapeDtypeStruct(w.shape, F32)] * 3,
        compiler_params=_params(), name=name)(g, w, m, v)
    return [o.reshape(shape) for o in outs]


def _all_gather(block, *, name):
    m, n = block.shape

    def body(x_ref, out_ref, send_sems, recv_sems, local_sem):
        for phase in range(3):
            _gather_two_level(x_ref, out_ref, send_sems, recv_sems, local_sem, phase=phase)

    return pl.pallas_call(
        body, out_shape=jax.ShapeDtypeStruct((N_DEV, m, n), block.dtype), in_specs=[_ANY], out_specs=_ANY,
        scratch_shapes=_EXCHANGE_SEMS, name=name)(block)


def _exchange_d2d(parts, *, name):
    _, rows, w = parts.shape

    def body(p_ref, land_ref, send_sems, recv_sems):
        x, y, c = _mesh_pos()
        sends = []
        for k in range(4):
            cp = pltpu.make_async_remote_copy(
                src_ref=p_ref.at[2 * k + (1 - c)], dst_ref=land_ref.at[k], send_sem=send_sems.at[k],
                recv_sem=recv_sems.at[k], device_id=(x, y, 1 - c), device_id_type=pl.DeviceIdType.MESH)
            cp.start()
            sends.append(cp)
        for cp in sends:
            cp.wait_recv()
        for cp in sends:
            cp.wait_send()

    return pl.pallas_call(
        body, out_shape=jax.ShapeDtypeStruct((4, rows, w), parts.dtype), in_specs=[_ANY], out_specs=_ANY,
        scratch_shapes=[pltpu.SemaphoreType.DMA((4,)), pltpu.SemaphoreType.DMA((4,))], name=name)(parts)


def _exchange_ici(parts, *, name):
    def body(p_ref, land_ref, send_sems, recv_sems, local_sem):
        x, y, c = _mesh_pos()
        mine = pltpu.make_async_copy(p_ref.at[2 * x + y], land_ref.at[3], local_sem)
        mine.start()
        sends = []
        for k, (px, py) in enumerate([(1 - x, y), (x, 1 - y), (1 - x, 1 - y)]):
            cp = pltpu.make_async_remote_copy(
                src_ref=p_ref.at[2 * px + py], dst_ref=land_ref.at[k], send_sem=send_sems.at[k],
                recv_sem=recv_sems.at[k], device_id=(px, py, c), device_id_type=pl.DeviceIdType.MESH)
            cp.start()
            sends.append(cp)
        for cp in sends:
            cp.wait_recv()
        for cp in sends:
            cp.wait_send()
        mine.wait()

    return pl.pallas_call(
        body, out_shape=jax.ShapeDtypeStruct(parts.shape, parts.dtype), in_specs=[_ANY], out_specs=_ANY,
        scratch_shapes=[pltpu.SemaphoreType.DMA((3,)), pltpu.SemaphoreType.DMA((3,)), pltpu.SemaphoreType.DMA(())],
        name=name)(parts)


def _rows(a):
    return a.reshape(-1, PACK_W)


def _pad_to(a, n):
    return jnp.pad(a, (0, n - a.shape[0]))


def _weight_blocks(d):
    small = _rows(_pad_to(jnp.concatenate([d[n].reshape(-1) for n, _ in _SMALL]), 8 * PACK_W))
    block_a = jnp.concatenate([d["w_in_a"][0].T.astype(WIRE), lax.bitcast_convert_type(small, WIRE).reshape(16, PACK_W)],
                              axis=0)
    w_uq = jnp.pad(d["w_uq"][0], ((0, 0), (0, 0), (0, HEAD_PAD - QK_NOPE - QK_ROPE)))
    pieces = {"w_out_a": d["w_out_a"], "w_dkv": d["w_dkv"], "w_uk": d["w_uk"], "w_uv": d["w_uv"],
              "w_in_b": d["w_in_b"][0].T, "w_uq": w_uq}
    block_b = jnp.concatenate([_rows(pieces[n]) for n, _ in _PIECES_B]
                              + [jnp.zeros((WIRE_ROWS_B - MATRIX_ROWS_B, PACK_W), F32)], axis=0).astype(WIRE)
    return block_a, block_b, d["w_out_b"][0].astype(WIRE)


def _weights_a(wall):
    w = {}
    lo, hi = _OFF_A["w_in_a"]
    w["w_in_a_t"] = wall[:, lo:hi].reshape(2 * D_RNN, D_MODEL)
    small = lax.bitcast_convert_type(wall[:, MATRIX_ROWS_A:].reshape(N_DEV, 8 * PACK_W, 2), F32)
    off = dict(zip([n for n, _ in _SMALL], [0, 128, 768, 928, 1088, 1248]))
    w["norm_a"] = small[:, :128].reshape(1, D_MODEL)

    def by_channel(lo, rows):
        a = small[:, lo:lo + rows * (D_RNN // N_DEV)].reshape(N_DEV, rows, -1).transpose(1, 0, 2).reshape(rows, D_RNN)
        return jnp.pad(a, ((0, 8 - rows), (0, 0)))

    w["conv_taps"] = by_channel(off["conv_w"], CONV_WIDTH)
    w["lru_vecs"] = by_channel(off["conv_b"], 4)
    return w


def _weights_b(wall):
    piece = {n: wall[:, lo:hi] for n, (lo, hi) in _OFF_B.items()}
    w = {"w_out_a": piece["w_out_a"].reshape(D_RNN, D_MODEL)}
    w_dkv = piece["w_dkv"].reshape(D_MODEL, KV_RANK + QK_ROPE)
    w["w_dkv_c"] = w_dkv[:, :KV_RANK]
    w["w_dkv_r"] = jnp.pad(w_dkv[:, KV_RANK:], ((0, 0), (0, LANES - QK_ROPE)))
    w["w_uk"] = piece["w_uk"].reshape(KV_RANK, N_HEADS * QK_NOPE)
    w["w_uv"] = piece["w_uv"].reshape(KV_RANK, N_HEADS * V_DIM)
    w["w_uk_t"], w["w_uv_t"] = w["w_uk"].T, w["w_uv"].T
    w["w_in_b_t"] = piece["w_in_b"].reshape(Q_RANK + N_HEADS * V_DIM, D_MODEL)
    w["w_uq"] = piece["w_uq"].reshape(Q_RANK, N_HEADS * HEAD_PAD)
    return w


def _pack_rep(d):
    flat = jnp.concatenate([d[n].reshape(-1) for n, _ in _REP])
    return _rows(_pad_to(flat, REP_ROWS * PACK_W))


def _unpack_rep(p, like):
    flat = p.reshape(-1)
    out, off = {}, 0
    for n, k in _REP:
        out[n] = flat[off:off + k].reshape(like[n].shape)
        off += k
    return out


def _by_owner(a):
    return a.reshape(N_DEV, -1, PACK_W)


def _grad_parts_b(g):
    tail = jnp.zeros((N_DEV, WIRE_ROWS_B - MATRIX_ROWS_B, PACK_W), F32)
    return jnp.concatenate([_by_owner(g[n]) for n, _ in _PIECES_B] + [tail], axis=1).astype(BF16)


def _grad_parts_a(g):
    small = jnp.concatenate([
        g["norm_a"].reshape(N_DEV, -1),
        g["conv_w"].reshape(CONV_WIDTH, N_DEV, -1).transpose(1, 0, 2).reshape(N_DEV, -1),
        g["conv_b"].reshape(N_DEV, -1), g["b_rg"].reshape(N_DEV, -1), g["b_ig"].reshape(N_DEV, -1),
        g["lru_lambda"].reshape(N_DEV, -1)], axis=1)
    small = jnp.pad(small, ((0, 0), (0, 8 * PACK_W - small.shape[1]))).reshape(N_DEV, 8, PACK_W)
    half = N_DEV // 2
    w_in_a = jnp.concatenate([h.reshape(half, -1, PACK_W) for h in g["w_in_a_t"]], axis=0)
    rep = _pack_rep(g).reshape(N_DEV, REP_SLICE, PACK_W)
    tail = jnp.zeros((N_DEV, GRAD_ROWS_A - MATRIX_ROWS_A - 8 - REP_SLICE, PACK_W), F32)
    return jnp.concatenate([w_in_a, small, rep, tail], axis=1)


def _own_grads(sum_a, sum_b, sum_c):
    out = {}
    lo, hi = _OFF_A["w_in_a"]
    out["w_in_a"] = sum_a[lo:hi].T.reshape(1, D_MODEL, 2 * D_RNN // N_DEV)
    small = sum_a[MATRIX_ROWS_A:MATRIX_ROWS_A + 8].reshape(-1)
    shapes = {"norm_a": (1, D_MODEL // N_DEV), "conv_w": (1, CONV_WIDTH, D_RNN // N_DEV), "conv_b": (1, D_RNN // N_DEV),
              "b_rg": (1, D_RNN // N_DEV), "b_ig": (1, D_RNN // N_DEV), "lru_lambda": (1, D_RNN // N_DEV)}
    off = 0
    for n, k in _SMALL:
        out[n] = small[off:off + k].reshape(shapes[n])
        off += k
    piece = {n: sum_b[lo:hi] for n, (lo, hi) in _OFF_B.items()}
    out["w_out_a"] = piece["w_out_a"].reshape(1, D_RNN // N_DEV, D_MODEL)
    out["w_dkv"] = piece["w_dkv"].reshape(D_MODEL // N_DEV, KV_RANK + QK_ROPE)
    out["w_uk"] = piece["w_uk"].reshape(KV_RANK // N_DEV, N_HEADS, QK_NOPE)
    out["w_uv"] = piece["w_uv"].reshape(KV_RANK // N_DEV, N_HEADS, V_DIM)
    out["w_in_b"] = piece["w_in_b"].T.reshape(1, D_MODEL, (Q_RANK + N_HEADS * V_DIM) // N_DEV)
    out["w_uq"] = piece["w_uq"].reshape(1, Q_RANK // N_DEV, N_HEADS, HEAD_PAD)[..., :QK_NOPE + QK_ROPE]
    out["w_out_b"] = sum_c.reshape(1, N_HEADS * V_DIM // N_DEV, D_MODEL)
    return out


def _step(x, target, w, rep, block_b, block_c, *, bsz, seq):
    t = bsz * seq
    cos, sin = _rope_tables(seq)
    g_a = w["norm_a"]
    g_kv = rep["norm_kv"].reshape(1, -1)
    g_kvn = rep["kv_norm"].reshape(1, -1)
    g_b = rep["norm_b"].reshape(1, -1)
    g_q = rep["q_norm"].reshape(1, -1)
    g_f = rep["final_norm"].reshape(1, -1)
    wrg = rep["w_rg"][0].astype(BF16)
    wig = rep["w_ig"][0].astype(BF16)
    cw8, vecs = w["conv_taps"], w["lru_vecs"]

    def seq3(a):
        return a.reshape(bsz, seq, a.shape[-1])

    def flat(a):
        return a.reshape(t, a.shape[-1])

    h0, xp, ga = _lru_proj_fwd(x, g_a, w["w_in_a_t"], name="lru_proj_fwd")
    xb, hs, y, wall_b = _lru_fwd(seq3(xp), seq3(ga), cw8, vecs, wrg, wig, block_b, name="lru_fwd")
    w = dict(w, **_weights_b(wall_b))
    x1 = _matmul(flat(y), w["w_out_a"], residual=x, name="out_a")
    hk, hq, ck, cqp, g2, ckv, cq, q, kn, v, kr, kn_t, v_t, kr_t = _mla_proj_fwd(
        x1, (g_kv, g_b, g_kvn, g_q), w, cos, sin, seq=seq, name="mla_proj_fwd")
    o, lse, wall_c = _attn_fwd(q, kn, kr, v_t, block_c, bsz=bsz, seq=seq, name="attn_fwd")
    w_out_b = wall_c.reshape(N_HEADS * V_DIM, D_MODEL)
    loss, dx2, y2, do, dg2, dgf = _head_and_loss(o, g2, x1, target, w_out_b, g_f, name="head_loss")
    grads = {"final_norm": dgf}
    parts_c = _by_owner(_matmul_tn(y2, dx2, name="d_w_out_b")).astype(BF16)
    dq, dkn, dkr, dv, landed_c = _attn_bwd(q, kn, kr, kn_t, kr_t, v, o, lse, do, cos, sin, parts_c,
                                           bsz=bsz, seq=seq, name="attn_bwd")
    grads["w_uq"] = _matmul_tn(cq, dq, name="d_w_uq")
    dx1, du2, dckr, dgkv, dgb, dgkvn, dgq = _mla_proj_bwd(
        x1, dx2, cqp, ck, dq, dkn, dv, dkr, dg2, (g_kv, g_b, g_kvn, g_q), w, name="mla_proj_bwd")
    grads["norm_kv"], grads["norm_b"], grads["kv_norm"], grads["q_norm"] = dgkv, dgb, dgkvn, dgq
    grads["w_in_b"] = _matmul_tn(du2, hq, name="d_w_in_b_t")
    grads["w_uk"] = _matmul_tn(ckv, dkn, name="d_w_uk")
    grads["w_uv"] = _matmul_tn(ckv, dv, name="d_w_uv")
    grads["w_dkv"] = _matmul_tn(hk, dckr, name="d_w_dkv")[:, :KV_RANK + QK_ROPE]
    grads["w_out_a"] = _matmul_tn(flat(y), dx1, name="d_w_out_a")
    parts_b = _grad_parts_b(grads)
    dy = _matmul(dx1, w["w_out_a"], nt=True, name="d_y")
    dxp, dga, dwrg, dwig, dvec, landed_b = _lru_bwd(
        seq3(dy), seq3(xp), xb, hs, seq3(ga), cw8, vecs, wrg, wig, parts_b, name="lru_bwd")
    dxp, dga = flat(dxp), flat(dga)
    grads["w_rg"], grads["w_ig"] = dwrg, dwig
    grads["b_rg"], grads["b_ig"], grads["conv_b"] = dvec[0], dvec[1], dvec[3]
    lam = vecs[3]
    grads["lru_lambda"] = dvec[2] * (-1.0 / (1.0 + jnp.exp(lam)))
    grads["conv_w"] = dvec[4:4 + CONV_WIDTH]
    grads["w_in_a_t"] = (_matmul_tn(dxp, h0, name="d_w_in_a_x_t"), _matmul_tn(dga, h0, name="d_w_in_a_g_t"))
    dx, dga_norm = _lru_proj_bwd(dxp, dga, x, dx1, g_a, w["w_in_a_t"], name="lru_proj_bwd")
    grads["norm_a"] = dga_norm
    return loss[0, 0], dx, grads, landed_b, landed_c


def kernel(x, norm_a, w_in_a, conv_w, conv_b, w_rg, b_rg, w_ig, b_ig, lru_lambda, w_out_a, norm_kv, w_dkv, kv_norm, w_uk, w_uv, norm_b, w_in_b, q_norm, w_uq, w_out_b, final_norm, loss_target, m_norm_a, m_w_in_a, m_conv_w, m_conv_b, m_w_rg, m_b_rg, m_w_ig, m_b_ig, m_lru_lambda, m_w_out_a, m_norm_kv, m_w_dkv, m_kv_norm, m_w_uk, m_w_uv, m_norm_b, m_w_in_b, m_q_norm, m_w_uq, m_w_out_b, m_final_norm, v_norm_a, v_w_in_a, v_conv_w, v_conv_b, v_w_rg, v_b_rg, v_w_ig, v_b_ig, v_lru_lambda, v_w_out_a, v_norm_kv, v_w_dkv, v_kv_norm, v_w_uk, v_w_uv, v_norm_b, v_w_in_b, v_q_norm, v_w_uq, v_w_out_b, v_final_norm):
    given = dict(locals())
    wts = {n: given[n] for n in WEIGHTS}
    mom1 = {n: given["m_" + n] for n in WEIGHTS}
    mom2 = {n: given["v_" + n] for n in WEIGHTS}
    bsz, seq, _ = x.shape
    t = bsz * seq

    block_a, block_b, block_c = _weight_blocks(wts)
    w = _weights_a(_all_gather(block_a, name="gather_weights_a"))
    loss, dx, grads, landed_b, landed_c = _step(x.reshape(t, D_MODEL), loss_target.reshape(t, D_MODEL), w, wts,
                                                block_b, block_c, bsz=bsz, seq=seq)

    parts_a = _grad_parts_a(grads)
    from_sibling = _exchange_d2d(parts_a, name="exchange_grads_d2d")
    chip_parts = _chip_partial(parts_a, from_sibling, name="chip_partial_grads")
    landed_a = _exchange_ici(chip_parts, name="exchange_grads_ici")
    sum_a = _sum_parts(landed_a, name="sum_grads_a", br=GRAD_BLOCK)
    sum_b = _sum_parts(landed_b, name="sum_grads_b", br=WIRE_ROWS_B // 2)
    sum_c = _sum_parts(landed_c, name="sum_grads_c", br=landed_c.shape[1])
    g_own = _own_grads(sum_a, sum_b, sum_c)
    rep_slice = sum_a[MATRIX_ROWS_A + 8:MATRIX_ROWS_A + 8 + REP_SLICE]
    loss_rows = jnp.pad(loss.reshape(1, 1), ((0, 7), (0, PACK_W - 1)))
    gathered = _all_gather(jnp.concatenate([rep_slice, loss_rows], axis=0), name="gather_replicated")
    g_own.update(_unpack_rep(gathered[:, :REP_SLICE].reshape(REP_ROWS, PACK_W), wts))
    loss = jnp.sum(gathered[:, REP_SLICE, 0])

    deltas, new_m, new_v = {}, {}, {}
    for n in WEIGHTS:
        deltas[n], new_m[n], new_v[n] = _adamw(g_own[n], wts[n], mom1[n], mom2[n], name="adamw_" + n)
    result = [loss, dx.reshape(bsz, seq, D_MODEL)]
    for d in (g_own, deltas, new_m, new_v):
        result.extend(d[n] for n in WEIGHTS)
    return tuple(result)
```

```python
import jax
import jax.numpy as jnp
from jax import lax
from jax.experimental import pallas as pl
from jax.experimental.pallas import tpu as pltpu

F32 = jnp.float32
BF16 = jnp.bfloat16
WIRE = jnp.bfloat16

D_MODEL = 1024
D_RNN = 1280
RNN_BLOCKS = 10
RNN_BW = 128
CONV_WIDTH = 4
LRU_C = 8.0
N_HEADS = 8
QK_NOPE = 128
QK_ROPE = 64
V_DIM = 128
KV_RANK = 256
Q_RANK = 384
ROPE_THETA = 10000.0
EPS = 1e-6
ATTN_SCALE = (QK_NOPE + QK_ROPE) ** -0.5
HEAD_PAD = 256
LANES = 128

ADAM_LR = 0.001
ADAM_B1 = 0.9
ADAM_B2 = 0.999
ADAM_EPS = 1e-08
ADAM_WD = 0.01
ADAM_STEP = 10

N_DEV = 8
VMEM_LIMIT_BYTES = 56 * 2**20
PACK_W = 1024

_PIECES_A = (("w_in_a", 320),)
_PIECES_B = (("w_out_a", 160), ("w_dkv", 40), ("w_uk", 32), ("w_uv", 32), ("w_in_b", 176), ("w_uq", 96))


def _offsets(pieces):
    off, r = {}, 0
    for n, k in pieces:
        off[n] = (r, r + k)
        r += k
    return off, r


_OFF_A, MATRIX_ROWS_A = _offsets(_PIECES_A)
_OFF_B, MATRIX_ROWS_B = _offsets(_PIECES_B)
WIRE_ROWS_A = MATRIX_ROWS_A + 16
WIRE_ROWS_B = 544
_SMALL = (("norm_a", 128), ("conv_w", 640), ("conv_b", 160), ("b_rg", 160), ("b_ig", 160), ("lru_lambda", 160))
_REP = (("w_rg", 163840), ("w_ig", 163840), ("norm_kv", 1024), ("kv_norm", 256), ("norm_b", 1024),
        ("q_norm", 384), ("final_norm", 1024))
REP_ROWS = 384
REP_SLICE = REP_ROWS // N_DEV
GRAD_ROWS_A = 384
GRAD_BLOCK = 192

WEIGHTS = ("norm_a", "w_in_a", "conv_w", "conv_b", "w_rg", "b_rg", "w_ig", "b_ig", "lru_lambda", "w_out_a",
           "norm_kv", "w_dkv", "kv_norm", "w_uk", "w_uv", "norm_b", "w_in_b", "q_norm", "w_uq", "w_out_b",
           "final_norm")


def _params(sem=None):
    return pltpu.CompilerParams(dimension_semantics=sem, vmem_limit_bytes=VMEM_LIMIT_BYTES)


_NT = (((1,), (1,)), ((), ()))
_ANY = pl.BlockSpec(memory_space=pl.ANY)


def _mesh_pos():
    return lax.axis_index("x"), lax.axis_index("y"), lax.axis_index("c")


def _sigmoid(z):
    return 0.5 * jnp.tanh(0.5 * z) + 0.5


def _sigmoid_tail(z):
    return 1.0 / (1.0 + jnp.exp(-z))


def _col_block(n):
    return n if n <= 1408 else n // 2


def _matmul(a, b, *, name, nt=False, out_dtype=F32, residual=None, bm=1024):
    m, k = a.shape
    n = b.shape[0] if nt else b.shape[1]
    bm = min(bm, m)
    bn = _col_block(n)
    dims = (((1,), (1,)), ((), ())) if nt else (((1,), (0,)), ((), ()))
    has_res = residual is not None

    def body(*refs):
        a_ref, b_ref, o_ref = refs[0], refs[1], refs[-1]
        acc = lax.dot_general(a_ref[...].astype(BF16), b_ref[...].astype(BF16), dims, preferred_element_type=F32)
        if has_res:
            acc = acc + refs[2][...]
        o_ref[...] = acc.astype(out_dtype)

    in_specs = [pl.BlockSpec((bm, k), lambda i, j: (i, 0)),
                pl.BlockSpec((bn, k), lambda i, j: (j, 0)) if nt else pl.BlockSpec((k, bn), lambda i, j: (0, j))]
    args = [a, b]
    if has_res:
        in_specs.append(pl.BlockSpec((bm, bn), lambda i, j: (i, j)))
        args.append(residual)
    return pl.pallas_call(
        body, grid=(m // bm, n // bn), in_specs=in_specs, out_specs=pl.BlockSpec((bm, bn), lambda i, j: (i, j)),
        out_shape=jax.ShapeDtypeStruct((m, n), out_dtype), compiler_params=_params(("parallel", "parallel")),
        name=name)(*args)


def _matmul_tn(a, b, *, name, bt=1024):
    t, m = a.shape
    n = b.shape[1]
    bt = min(bt, t)
    bm, bn = _col_block(m), _col_block(n)

    def body(a_ref, b_ref, o_ref):
        @pl.when(pl.program_id(2) == 0)
        def _():
            o_ref[...] = jnp.zeros_like(o_ref)

        o_ref[...] += lax.dot_general(a_ref[...].astype(BF16), b_ref[...].astype(BF16),
                                      (((0,), (0,)), ((), ())), preferred_element_type=F32)

    return pl.pallas_call(
        body, grid=(m // bm, n // bn, t // bt),
        in_specs=[pl.BlockSpec((bt, bm), lambda i, j, s: (s, i)), pl.BlockSpec((bt, bn), lambda i, j, s: (s, j))],
        out_specs=pl.BlockSpec((bm, bn), lambda i, j, s: (i, j)),
        out_shape=jax.ShapeDtypeStruct((m, n), F32),
        compiler_params=_params(("parallel", "parallel", "arbitrary")), name=name)(a, b)


def _swap_halves(v):
    ax = v.ndim - 1
    lane = lax.broadcasted_iota(jnp.int32, v.shape, ax)
    up = pltpu.roll(v, LANES - QK_ROPE // 2, axis=ax)
    down = pltpu.roll(v, QK_ROPE // 2, axis=ax)
    return jnp.where(lane < QK_ROPE // 2, up, jnp.where(lane < QK_ROPE, down, 0.0))


def _rope(v, cos, sin):
    return v * cos + _swap_halves(v) * sin


def _rope_t(d, cos, sin):
    return d * cos + _swap_halves(d * sin)


def _rope_tables(seq):
    pos = jnp.arange(seq, dtype=F32)
    inv = ROPE_THETA ** (-jnp.arange(0, QK_ROPE, 2, dtype=F32) / QK_ROPE)
    ang = pos[:, None] * inv[None, :]
    cos, sin = jnp.cos(ang), jnp.sin(ang)
    zero = jnp.zeros((seq, LANES - QK_ROPE), F32)
    return jnp.concatenate([cos, cos, zero], axis=1), jnp.concatenate([-sin, sin, zero], axis=1)


def _rms(v):
    return v * lax.rsqrt(jnp.mean(v * v, axis=-1, keepdims=True) + EPS)


def _const_spec(a):
    return pl.BlockSpec(a.shape, lambda i: (0,) * a.ndim)


def _lru_proj_fwd(x, g_a, w_in_t, *, name, bt=512):
    t, d = x.shape
    bt = min(bt, t)
    n = w_in_t.shape[0] // 2

    def body(x_ref, g_ref, wt_ref, h_ref, xp_ref, ga_ref):
        h = (_rms(x_ref[...]) * g_ref[...]).astype(BF16)
        h_ref[...] = h
        xp_ref[...] = lax.dot_general(h, wt_ref[0:n, :], _NT, preferred_element_type=F32)
        ga_ref[...] = lax.dot_general(h, wt_ref[n:2 * n, :], _NT, preferred_element_type=F32)

    row = lambda w: pl.BlockSpec((bt, w), lambda i: (i, 0))
    return pl.pallas_call(
        body, grid=(t // bt,), in_specs=[row(d), _const_spec(g_a), _const_spec(w_in_t)],
        out_specs=[row(d), row(n), row(n)],
        out_shape=[jax.ShapeDtypeStruct((t, d), BF16), jax.ShapeDtypeStruct((t, n), F32), jax.ShapeDtypeStruct((t, n), F32)],
        compiler_params=_params(("parallel",)), name=name)(x, g_a, w_in_t)


def _mla_proj_fwd(x1, gains, w, cos, sin, *, seq, name, bt=512):
    t, d = x1.shape
    bt = min(bt, seq)
    per_seq = seq // bt
    g_kv, g_b, g_kvn, g_q = gains
    consts = [g_kv, g_b, g_kvn, g_q, w["w_dkv_c"], w["w_dkv_r"], w["w_in_b_t"], w["w_uk"], w["w_uv"],
              w["w_uk_t"], w["w_uv_t"], w["w_uq"]]

    def body(x_ref, cos_ref, sin_ref, gkv_ref, gb_ref, gkvn_ref, gq_ref, wdc_ref, wdr_ref, wbt_ref,
             wuk_ref, wuv_ref, wukt_ref, wuvt_ref, wuq_ref,
             hk_ref, hq_ref, ck_ref, cqp_ref, g2_ref, ckv_ref, cq_ref, q_ref, kn_ref, v_ref, kr_ref, knt_ref, vt_ref, krt_ref):
        nrm = _rms(x_ref[...])
        hk = (nrm * gkv_ref[...]).astype(BF16)
        hq = (nrm * gb_ref[...]).astype(BF16)
        hk_ref[...] = hk
        hq_ref[...] = hq
        ck = jnp.dot(hk, wdc_ref[...], preferred_element_type=F32)
        ck_ref[...] = ck
        cqp = lax.dot_general(hq, wbt_ref[0:Q_RANK, :], _NT, preferred_element_type=F32)
        cqp_ref[...] = cqp
        g2_ref[...] = lax.dot_general(hq, wbt_ref[Q_RANK:, :], _NT, preferred_element_type=F32)
        cosv, sinv = cos_ref[...], sin_ref[...]
        kr = _rope(jnp.dot(hk, wdr_ref[...], preferred_element_type=F32), cosv, sinv)
        kr_ref[...] = kr.astype(BF16)
        krt_ref[...] = kr.T.astype(BF16)
        ckv = (_rms(ck) * gkvn_ref[...]).astype(BF16)
        ckv_ref[...] = ckv
        kn_ref[...] = jnp.dot(ckv, wuk_ref[...], preferred_element_type=F32).astype(BF16)
        v_ref[...] = jnp.dot(ckv, wuv_ref[...], preferred_element_type=F32).astype(BF16)
        knt_ref[...] = lax.dot_general(wukt_ref[...], ckv, _NT, preferred_element_type=F32).astype(BF16)
        vt_ref[...] = lax.dot_general(wuvt_ref[...], ckv, _NT, preferred_element_type=F32).astype(BF16)
        cq = (_rms(cqp) * gq_ref[...]).astype(BF16)
        cq_ref[...] = cq
        for h in range(N_HEADS):
            qh = jnp.dot(cq, wuq_ref[:, h * HEAD_PAD:(h + 1) * HEAD_PAD], preferred_element_type=F32)
            q_ref[:, h * HEAD_PAD:h * HEAD_PAD + QK_NOPE] = qh[:, :QK_NOPE].astype(BF16)
            q_ref[:, h * HEAD_PAD + QK_NOPE:(h + 1) * HEAD_PAD] = _rope(qh[:, QK_NOPE:], cosv, sinv).astype(BF16)

    row = lambda w_: pl.BlockSpec((bt, w_), lambda i: (i, 0))
    col = lambda h_: pl.BlockSpec((h_, bt), lambda i: (0, i))
    tab = pl.BlockSpec((bt, LANES), lambda i: (i % per_seq, 0))
    nh = N_HEADS * V_DIM
    shapes = [((t, d), BF16), ((t, d), BF16), ((t, KV_RANK), F32), ((t, Q_RANK), F32), ((t, nh), F32), ((t, KV_RANK), BF16),
              ((t, Q_RANK), BF16), ((t, N_HEADS * HEAD_PAD), BF16), ((t, nh), BF16), ((t, nh), BF16), ((t, LANES), BF16),
              ((nh, t), BF16), ((nh, t), BF16), ((LANES, t), BF16)]
    out_specs = [row(d), row(d), row(KV_RANK), row(Q_RANK), row(nh), row(KV_RANK), row(Q_RANK), row(N_HEADS * HEAD_PAD),
                 row(nh), row(nh), row(LANES), col(nh), col(nh), col(LANES)]
    return pl.pallas_call(
        body, grid=(t // bt,), in_specs=[row(d), tab, tab] + [_const_spec(a) for a in consts], out_specs=out_specs,
        out_shape=[jax.ShapeDtypeStruct(s, dt) for s, dt in shapes],
        compiler_params=_params(("parallel",)), name=name)(x1, cos, sin, *consts)


def _rms_bwd_rows(xv, dn):
    r = lax.rsqrt(jnp.mean(xv * xv, axis=-1, keepdims=True) + EPS)
    nrm = xv * r
    return r * (dn - nrm * jnp.mean(dn * nrm, axis=-1, keepdims=True)), nrm


def _col_sum(v):
    return jnp.sum(v, axis=0, keepdims=True)


def _lru_proj_bwd(dxp, dga, x, dx1, g_a, w_in_t, *, name, bt=512):
    t, d = x.shape
    bt = min(bt, t)
    n = w_in_t.shape[0] // 2

    def body(dxp_ref, dga_ref, x_ref, dx1_ref, g_ref, wt_ref, dx_ref, dg_ref):
        @pl.when(pl.program_id(0) == 0)
        def _():
            dg_ref[...] = jnp.zeros_like(dg_ref)

        dh = (jnp.dot(dxp_ref[...], wt_ref[0:n, :], preferred_element_type=F32)
              + jnp.dot(dga_ref[...], wt_ref[n:2 * n, :], preferred_element_type=F32))
        dxn, nrm = _rms_bwd_rows(x_ref[...], dh * g_ref[...])
        dg_ref[...] += _col_sum(dh * nrm)
        dx_ref[...] = dx1_ref[...] + dxn

    row = lambda w: pl.BlockSpec((bt, w), lambda i: (i, 0))
    return pl.pallas_call(
        body, grid=(t // bt,),
        in_specs=[row(n), row(n), row(d), row(d), _const_spec(g_a), _const_spec(w_in_t)],
        out_specs=[row(d), _const_spec(g_a)],
        out_shape=[jax.ShapeDtypeStruct((t, d), F32), jax.ShapeDtypeStruct((1, d), F32)],
        compiler_params=_params(("arbitrary",)), name=name)(dxp, dga, x, dx1, g_a, w_in_t)


def _mla_proj_bwd(x1, dx2, cqp, ck, dq, dkn, dv, dkr, dg2, gains, w, *, name, bt=512):
    t, d = x1.shape
    bt = min(bt, t)
    g_kv, g_b, g_kvn, g_q = gains
    consts = [g_kv, g_b, g_kvn, g_q, w["w_dkv_c"], w["w_dkv_r"], w["w_in_b_t"], w["w_uk"], w["w_uv"], w["w_uq"]]
    nh = N_HEADS * V_DIM

    def body(x1_ref, dx2_ref, cqp_ref, ck_ref, dq_ref, dkn_ref, dv_ref, dkr_ref, dg2_ref,
             gkv_ref, gb_ref, gkvn_ref, gq_ref, wdc_ref, wdr_ref, wbt_ref, wuk_ref, wuv_ref, wuq_ref,
             dx1_ref, du2_ref, dckr_ref, dgkv_ref, dgb_ref, dgkvn_ref, dgq_ref):
        @pl.when(pl.program_id(0) == 0)
        def _():
            for ref in (dgkv_ref, dgb_ref, dgkvn_ref, dgq_ref):
                ref[...] = jnp.zeros_like(ref)

        dot_nt = lambda a, b: lax.dot_general(a, b, _NT, preferred_element_type=F32)
        dcq = dot_nt(dq_ref[...], wuq_ref[...])
        dcqp, nq = _rms_bwd_rows(cqp_ref[...], dcq * gq_ref[...])
        dgq_ref[...] += _col_sum(dcq * nq)
        dcqp = dcqp.astype(BF16)
        dg2 = dg2_ref[...]
        du2_ref[:, :Q_RANK] = dcqp
        du2_ref[:, Q_RANK:] = dg2
        dhq = (jnp.dot(dcqp, wbt_ref[0:Q_RANK, :], preferred_element_type=F32)
               + jnp.dot(dg2, wbt_ref[Q_RANK:, :], preferred_element_type=F32))
        dckv = dot_nt(dkn_ref[...], wuk_ref[...]) + dot_nt(dv_ref[...], wuv_ref[...])
        dck, nc = _rms_bwd_rows(ck_ref[...], dckv * gkvn_ref[...])
        dgkvn_ref[...] += _col_sum(dckv * nc)
        dck = dck.astype(BF16)
        dkr = dkr_ref[...].astype(BF16)
        dckr_ref[:, :KV_RANK] = dck
        dckr_ref[:, KV_RANK:] = dkr
        dhk = dot_nt(dck, wdc_ref[...]) + dot_nt(dkr, wdr_ref[...])
        dxn, n1 = _rms_bwd_rows(x1_ref[...], dhq * gb_ref[...] + dhk * gkv_ref[...])
        dgb_ref[...] += _col_sum(dhq * n1)
        dgkv_ref[...] += _col_sum(dhk * n1)
        dx1_ref[...] = dx2_ref[...] + dxn

    row = lambda w_: pl.BlockSpec((bt, w_), lambda i: (i, 0))
    vec = lambda w_: pl.BlockSpec((1, w_), lambda i: (0, 0))
    in_specs = [row(d), row(d), row(Q_RANK), row(KV_RANK), row(N_HEADS * HEAD_PAD), row(nh), row(nh), row(LANES), row(nh)]
    return pl.pallas_call(
        body, grid=(t // bt,), in_specs=in_specs + [_const_spec(a) for a in consts],
        out_specs=[row(d), row(Q_RANK + nh), row(KV_RANK + LANES), vec(d), vec(d), vec(KV_RANK), vec(Q_RANK)],
        out_shape=[jax.ShapeDtypeStruct((t, d), F32), jax.ShapeDtypeStruct((t, Q_RANK + nh), BF16),
                   jax.ShapeDtypeStruct((t, KV_RANK + LANES), BF16), jax.ShapeDtypeStruct((1, d), F32),
                   jax.ShapeDtypeStruct((1, d), F32), jax.ShapeDtypeStruct((1, KV_RANK), F32),
                   jax.ShapeDtypeStruct((1, Q_RANK), F32)],
        compiler_params=_params(("arbitrary",)), name=name)(x1, dx2, cqp, ck, dq, dkn, dv, dkr, dg2, *consts)


def _softplus(z):
    return jnp.maximum(z, 0.0) + jnp.log1p(jnp.exp(-jnp.abs(z)))


def _one_minus_square(a, la):
    return jnp.tanh(-la) * (1.0 + a * a)


def _gates(xb, wrg, wig, brg, big, sp):
    xbb = xb.astype(BF16)
    r = _sigmoid_tail(jnp.dot(xbb, wrg, preferred_element_type=F32) + brg)
    i = _sigmoid(jnp.dot(xbb, wig, preferred_element_type=F32) + big)
    la = (-LRU_C) * r * sp
    a = jnp.exp(la)
    em = _one_minus_square(a, la)
    inv_mult = lax.rsqrt(em)
    mult = jnp.where(em > 0.0, em * inv_mult, 0.0)
    return r, i, a, mult, inv_mult


def _conv(xpad_ref, cw_ref, seq):
    acc = cw_ref[0:1, :] * xpad_ref[pl.ds(8 - (CONV_WIDTH - 1), seq), :]
    for k in range(1, CONV_WIDTH):
        acc = acc + cw_ref[k:k + 1, :] * xpad_ref[pl.ds(8 - (CONV_WIDTH - 1) + k, seq), :]
    return acc


def _seq_spec(seq):
    return pl.BlockSpec((None, seq, RNN_BW), lambda n, b: (b, 0, n))


def _chan_spec(rows):
    return pl.BlockSpec((rows, RNN_BW), lambda n, b: (0, n))


_GATE_W_SPEC = pl.BlockSpec((None, RNN_BW, RNN_BW), lambda n, b: (n, 0, 0))


SCAN_UNROLL = 4


def _peers():
    x, y, c = _mesh_pos()
    others = []
    for k in range(1, N_DEV):
        px = 1 - x if k & 4 else x
        py = 1 - y if k & 2 else y
        pc = 1 - c if k & 1 else c
        others.append(((px, py, pc), 4 * px + 2 * py + pc))
    return 4 * x + 2 * y + c, others


def _exchange(src_ref, dst_ref, send_sems, recv_sems, local_sem, *, finish, gather=False):
    me, others = _peers()

    def send(k, dev, slot):
        return pltpu.make_async_remote_copy(
            src_ref=src_ref if gather else src_ref.at[slot], dst_ref=dst_ref.at[me], send_sem=send_sems.at[k],
            recv_sem=recv_sems.at[k], device_id=dev, device_id_type=pl.DeviceIdType.MESH)

    local = pltpu.make_async_copy(src_ref if gather else src_ref.at[me], dst_ref.at[me], local_sem)
    if not finish:
        local.start()
        for k, (dev, slot) in enumerate(others):
            send(k, dev, slot).start()
        return
    for k, (dev, slot) in enumerate(others):
        pltpu.make_async_remote_copy(
            src_ref=dst_ref.at[slot], dst_ref=dst_ref.at[slot], send_sem=send_sems.at[k], recv_sem=recv_sems.at[k],
            device_id=dev, device_id_type=pl.DeviceIdType.MESH).wait_recv()
    for k, (dev, slot) in enumerate(others):
        send(k, dev, slot).wait_send()
    local.wait()


def _gather_two_level(x_ref, out_ref, send_sems, recv_sems, local_sem, *, phase):
    x, y, c = _mesh_pos()
    me, sibling = (x, y, c), (x, y, 1 - c)
    chips = [(1 - x, y), (x, 1 - y), (1 - x, 1 - y)]

    def slot(px, py, pc):
        return out_ref.at[4 * px + 2 * py + pc]

    def copy(k, blk, to, src=None):
        return pltpu.make_async_remote_copy(
            src_ref=slot(*blk) if src is None else src, dst_ref=slot(*blk),
            send_sem=send_sems.at[k], recv_sem=recv_sems.at[k], device_id=to, device_id_type=pl.DeviceIdType.MESH)

    if phase == 0:
        pltpu.make_async_copy(x_ref, slot(*me), local_sem).start()
        copy(0, me, sibling, src=x_ref).start()
        for j, chip in enumerate(chips):
            copy(1 + j, me, (*chip, c), src=x_ref).start()
    elif phase == 1:
        for j, chip in enumerate(chips):
            copy(1 + j, (*chip, c), me).wait_recv()
            copy(4 + j, (*chip, c), sibling).start()
    else:
        copy(0, sibling, me).wait_recv()
        for j, chip in enumerate(chips):
            copy(4 + j, (*chip, 1 - c), me).wait_recv()
        copy(0, me, sibling, src=x_ref).wait_send()
        for j, chip in enumerate(chips):
            copy(1 + j, me, (*chip, c), src=x_ref).wait_send()
            copy(4 + j, (*chip, c), sibling).wait_send()
        pltpu.make_async_copy(x_ref, slot(*me), local_sem).wait()


GATHER_FORWARD_STEP = 9
_EXCHANGE_SEMS = [pltpu.SemaphoreType.DMA((N_DEV - 1,)), pltpu.SemaphoreType.DMA((N_DEV - 1,)), pltpu.SemaphoreType.DMA(())]


def _first_last(steps):
    first = last = None
    for axis, n in enumerate(steps):
        i = pl.program_id(axis)
        first = (i == 0) if first is None else first & (i == 0)
        last = (i == n - 1) if last is None else last & (i == n - 1)
    return first, last


def _lru_fwd(xp, ga, cw, vecs, wrg, wig, block, *, name):
    bsz, seq, _ = xp.shape
    groups = seq // 8

    def body(xp_ref, ga_ref, cw_ref, vec_ref, wrg_ref, wig_ref, blk_ref, xb_ref, hs_ref, y_ref, all_ref,
             xpad, a_s, b_s, send_sems, recv_sems, local_sem):
        first, last = _first_last((RNN_BLOCKS, bsz))

        @pl.when(first)
        def _():
            _gather_two_level(blk_ref, all_ref, send_sems, recv_sems, local_sem, phase=0)

        @pl.when((pl.program_id(0) == GATHER_FORWARD_STEP) & (pl.program_id(1) == 0))
        def _():
            _gather_two_level(blk_ref, all_ref, send_sems, recv_sems, local_sem, phase=1)

        xpad[0:8, :] = jnp.zeros((8, RNN_BW), F32)
        xpad[pl.ds(8, seq), :] = xp_ref[...]
        xb = _conv(xpad, cw_ref, seq) + vec_ref[0:1, :]
        xb_ref[...] = xb
        sp = _softplus(-vec_ref[3:4, :])
        _, i, a, mult, _ = _gates(xb, wrg_ref[...], wig_ref[...], vec_ref[1:2, :], vec_ref[2:3, :], sp)
        a_s[...] = a
        b_s[...] = mult * (i * xb)
        row = lax.broadcasted_iota(jnp.int32, (8, RNN_BW), 0)

        def group(g, h):
            r0 = pl.multiple_of(g * 8, 8)
            av = a_s[pl.ds(r0, 8), :]
            bv = b_s[pl.ds(r0, 8), :]
            for k in (1, 2, 4):
                m = row >= k
                bv = jnp.where(m, av * pltpu.roll(bv, k, axis=0) + bv, bv)
                av = jnp.where(m, av * pltpu.roll(av, k, axis=0), av)
            hs_ref[pl.ds(r0, 8), :] = av * h + bv
            return av[7:8, :] * h + bv[7:8, :]

        def groups_of(i, h):
            for u in range(SCAN_UNROLL):
                h = group(i * SCAN_UNROLL + u, h)
            return h

        lax.fori_loop(0, groups // SCAN_UNROLL, groups_of, jnp.zeros((1, RNN_BW), F32))
        gav = ga_ref[...]
        y_ref[...] = (hs_ref[...] * (gav * _sigmoid(gav))).astype(BF16)

        @pl.when(last)
        def _():
            _gather_two_level(blk_ref, all_ref, send_sems, recv_sems, local_sem, phase=2)

    sq = _seq_spec(seq)
    shape = (bsz, seq, D_RNN)
    return pl.pallas_call(
        body, grid=(RNN_BLOCKS, bsz),
        in_specs=[sq, sq, _chan_spec(8), _chan_spec(8), _GATE_W_SPEC, _GATE_W_SPEC, _ANY],
        out_specs=[sq, sq, sq, _ANY],
        out_shape=[jax.ShapeDtypeStruct(shape, F32), jax.ShapeDtypeStruct(shape, F32), jax.ShapeDtypeStruct(shape, BF16),
                   jax.ShapeDtypeStruct((N_DEV,) + block.shape, block.dtype)],
        scratch_shapes=[pltpu.VMEM((seq + 8, RNN_BW), F32), pltpu.VMEM((seq, RNN_BW), F32), pltpu.VMEM((seq, RNN_BW), F32)]
        + _EXCHANGE_SEMS,
        compiler_params=_params(("arbitrary", "arbitrary")), name=name)(xp, ga, cw, vecs, wrg, wig, block)


def _lru_bwd(dy, xp, xb, hs, ga, cw, vecs, wrg, wig, parts, *, name):
    bsz, seq, _ = xp.shape
    groups = seq // 8

    def body(dy_ref, xp_ref, xb_ref, hs_ref, ga_ref, cw_ref, vec_ref, wrg_ref, wig_ref,
             parts_ref, dxp_ref, dga_ref, dwrg_ref, dwig_ref, dvec_ref, land_ref, pad, a_s, d_s, lam_s,
             send_sems, recv_sems, local_sem):
        first, last = _first_last((RNN_BLOCKS, bsz))

        @pl.when(first)
        def _():
            _exchange(parts_ref, land_ref, send_sems, recv_sems, local_sem, finish=False)

        @pl.when(pl.program_id(1) == 0)
        def _():
            dwrg_ref[...] = jnp.zeros_like(dwrg_ref)
            dwig_ref[...] = jnp.zeros_like(dwig_ref)
            dvec_ref[...] = jnp.zeros_like(dvec_ref)

        xb = xb_ref[...]
        hs = hs_ref[...]
        gav = ga_ref[...]
        dy = dy_ref[...]
        sp = _softplus(-vec_ref[3:4, :])
        wrg = wrg_ref[...]
        wig = wig_ref[...]
        r, i, a, mult, inv_mult = _gates(xb, wrg, wig, vec_ref[1:2, :], vec_ref[2:3, :], sp)
        sg = _sigmoid(gav)
        dga_ref[...] = (dy * hs * (sg * (1.0 + gav * (1.0 - sg)))).astype(BF16)
        d_s[...] = dy * (gav * sg)

        pad[pl.ds(0, seq), :] = a
        pad[pl.ds(seq, 8), :] = jnp.zeros((8, RNN_BW), F32)
        a_s[...] = pad[pl.ds(1, seq), :]
        row = lax.broadcasted_iota(jnp.int32, (8, RNN_BW), 0)

        def group(g, nxt):
            r0 = pl.multiple_of((groups - 1 - g) * 8, 8)
            cv = a_s[pl.ds(r0, 8), :]
            bv = d_s[pl.ds(r0, 8), :]
            for k in (1, 2, 4):
                m = row < 8 - k
                bv = jnp.where(m, cv * pltpu.roll(bv, 8 - k, axis=0) + bv, bv)
                cv = jnp.where(m, cv * pltpu.roll(cv, 8 - k, axis=0), cv)
            lam_s[pl.ds(r0, 8), :] = cv * nxt + bv
            return cv[0:1, :] * nxt + bv[0:1, :]

        def groups_of(i, nxt):
            for u in range(SCAN_UNROLL):
                nxt = group(i * SCAN_UNROLL + u, nxt)
            return nxt

        lax.fori_loop(0, groups // SCAN_UNROLL, groups_of, jnp.zeros((1, RNN_BW), F32))
        dh = lam_s[...]

        pad[0:8, :] = jnp.zeros((8, RNN_BW), F32)
        pad[pl.ds(8, seq), :] = hs
        da = dh * pad[pl.ds(7, seq), :]
        ixb = i * xb
        dixb = dh * mult
        dla = da * a - (dh * ixb) * (a * a) * inv_mult
        drp = (dla * ((-LRU_C) * sp)) * r * (1.0 - r)
        dip = (dixb * xb) * i * (1.0 - i)
        dvec_ref[0:1, :] += jnp.sum(drp, axis=0, keepdims=True)
        dvec_ref[1:2, :] += jnp.sum(dip, axis=0, keepdims=True)
        dvec_ref[2:3, :] += jnp.sum(dla * ((-LRU_C) * r), axis=0, keepdims=True)
        drpb = drp.astype(BF16)
        dipb = dip.astype(BF16)
        xbb = xb.astype(BF16)
        nt = (((1,), (1,)), ((), ()))
        tn = (((0,), (0,)), ((), ()))
        dxb = (dixb * i
               + lax.dot_general(drpb, wrg, nt, preferred_element_type=F32)
               + lax.dot_general(dipb, wig, nt, preferred_element_type=F32))
        dwrg_ref[...] += lax.dot_general(xbb, drpb, tn, preferred_element_type=F32)
        dwig_ref[...] += lax.dot_general(xbb, dipb, tn, preferred_element_type=F32)
        dvec_ref[3:4, :] += jnp.sum(dxb, axis=0, keepdims=True)

        pad[pl.ds(0, seq), :] = dxb
        pad[pl.ds(seq, 8), :] = jnp.zeros((8, RNN_BW), F32)
        dxp = cw_ref[0:1, :] * pad[pl.ds(CONV_WIDTH - 1, seq), :]
        for k in range(1, CONV_WIDTH):
            dxp = dxp + cw_ref[k:k + 1, :] * pad[pl.ds(CONV_WIDTH - 1 - k, seq), :]
        dxp_ref[...] = dxp.astype(BF16)
        pad[0:8, :] = jnp.zeros((8, RNN_BW), F32)
        pad[pl.ds(8, seq), :] = xp_ref[...]
        for k in range(CONV_WIDTH):
            dvec_ref[4 + k:5 + k, :] += jnp.sum(dxb * pad[pl.ds(8 - (CONV_WIDTH - 1) + k, seq), :], axis=0, keepdims=True)

        @pl.when(last)
        def _():
            _exchange(parts_ref, land_ref, send_sems, recv_sems, local_sem, finish=True)

    sq = _seq_spec(seq)
    shape = (bsz, seq, D_RNN)
    gshape = (RNN_BLOCKS, RNN_BW, RNN_BW)
    return pl.pallas_call(
        body, grid=(RNN_BLOCKS, bsz),
        in_specs=[sq, sq, sq, sq, sq, _chan_spec(8), _chan_spec(8), _GATE_W_SPEC, _GATE_W_SPEC, _ANY],
        out_specs=[sq, sq, _GATE_W_SPEC, _GATE_W_SPEC, _chan_spec(8), _ANY],
        out_shape=[jax.ShapeDtypeStruct(shape, BF16), jax.ShapeDtypeStruct(shape, BF16),
                   jax.ShapeDtypeStruct(gshape, F32), jax.ShapeDtypeStruct(gshape, F32),
                   jax.ShapeDtypeStruct((8, D_RNN), F32), jax.ShapeDtypeStruct(parts.shape, parts.dtype)],
        scratch_shapes=[pltpu.VMEM((seq + 8, RNN_BW), F32), pltpu.VMEM((seq, RNN_BW), F32),
                        pltpu.VMEM((seq, RNN_BW), F32), pltpu.VMEM((seq, RNN_BW), F32)] + _EXCHANGE_SEMS,
        compiler_params=_params(("arbitrary", "arbitrary")), name=name)(dy, xp, xb, hs, ga, cw, vecs, wrg, wig, parts)


def _attn_block(seq):
    return min(512, seq)


def _diag_mask(blk):
    return lax.broadcasted_iota(jnp.int32, (blk, blk), 0) <= lax.broadcasted_iota(jnp.int32, (blk, blk), 1)


FWD_HEADS = 4
BWD_HEADS = 2


def _attn_fwd(q, kn, kr, v_t, block, *, bsz, seq, name):
    t = bsz * seq
    blk = _attn_block(seq)
    nq = seq // blk
    hg = FWD_HEADS
    steps = (bsz, N_HEADS // hg, nq)

    def body(q_ref, kn_ref, kr_ref, vt_ref, blk_ref, o_ref, lse_ref, all_ref, acc, send_sems, recv_sems, local_sem):
        first, last = _first_last(steps)

        @pl.when(first)
        def _():
            _exchange(blk_ref, all_ref, send_sems, recv_sems, local_sem, finish=False, gather=True)

        qi = pl.program_id(2)
        acc[...] = jnp.zeros_like(acc)

        def step(j, carry, diagonal):
            k0 = pl.multiple_of(j * blk, blk)
            kr_j = kr_ref[pl.ds(k0, blk), :]
            out = []
            for h in range(hg):
                m_i, l_i = carry[h]
                kv = jnp.concatenate([kn_ref[pl.ds(k0, blk), h * QK_NOPE:(h + 1) * QK_NOPE], kr_j], axis=1)
                qv = q_ref[:, h * HEAD_PAD:(h + 1) * HEAD_PAD]
                s = lax.dot_general(kv, qv, _NT, preferred_element_type=F32) * ATTN_SCALE
                if diagonal:
                    s = jnp.where(_diag_mask(blk), s, -jnp.inf)
                m_new = jnp.maximum(m_i, jnp.max(s, axis=0, keepdims=True))
                p = jnp.exp(s - m_new)
                alpha = jnp.exp(m_i - m_new)
                l_new = alpha * l_i + jnp.sum(p, axis=0, keepdims=True)
                acc[h] = alpha * acc[h] + jnp.dot(vt_ref[h * V_DIM:(h + 1) * V_DIM, pl.ds(k0, blk)], p.astype(BF16),
                                                  preferred_element_type=F32)
                out.append((m_new, l_new))
            return tuple(out)

        init = tuple((jnp.full((1, blk), -jnp.inf, F32), jnp.zeros((1, blk), F32)) for _ in range(hg))
        carry = lax.fori_loop(0, qi, lambda j, c: step(j, c, False), init)
        stats = step(qi, carry, True)
        for h in range(hg):
            m_i, l_i = stats[h]
            o_ref[:, h * V_DIM:(h + 1) * V_DIM] = (acc[h] / l_i).T
            lse_ref[h] = m_i + jnp.log(l_i)

        @pl.when(last)
        def _():
            _exchange(blk_ref, all_ref, send_sems, recv_sems, local_sem, finish=True, gather=True)

    return pl.pallas_call(
        body, grid=steps,
        in_specs=[pl.BlockSpec((blk, hg * HEAD_PAD), lambda b, g, i: (b * nq + i, g)),
                  pl.BlockSpec((seq, hg * QK_NOPE), lambda b, g, i: (b, g)),
                  pl.BlockSpec((seq, LANES), lambda b, g, i: (b, 0)),
                  pl.BlockSpec((hg * V_DIM, seq), lambda b, g, i: (g, b)), _ANY],
        out_specs=[pl.BlockSpec((blk, hg * V_DIM), lambda b, g, i: (b * nq + i, g)),
                   pl.BlockSpec((hg, 1, blk), lambda b, g, i: (g, 0, b * nq + i)), _ANY],
        out_shape=[jax.ShapeDtypeStruct((t, N_HEADS * V_DIM), F32), jax.ShapeDtypeStruct((N_HEADS, 1, t), F32),
                   jax.ShapeDtypeStruct((N_DEV,) + block.shape, block.dtype)],
        scratch_shapes=[pltpu.VMEM((hg, V_DIM, blk), F32)] + _EXCHANGE_SEMS,
        compiler_params=_params(("arbitrary", "arbitrary", "arbitrary")), name=name)(q, kn, kr, v_t, block)


def _attn_bwd(q, kn, kr, kn_t, kr_t, v, o, lse, do, cos, sin, parts, *, bsz, seq, name):
    t = bsz * seq
    blk = _attn_block(seq)
    nq = seq // blk
    hg = BWD_HEADS
    steps = (bsz, N_HEADS // hg)

    def body(q_ref, kn_ref, kr_ref, knt_ref, krt_ref, v_ref, o_ref, lse_ref, do_ref, cos_ref, sin_ref, parts_ref,
             dq_ref, dkn_ref, dkr_ref, dv_ref, land_ref, dqt_acc, dk_acc, dv_acc, send_sems, recv_sems, local_sem):
        first, last = _first_last(steps)

        @pl.when(first)
        def _():
            _exchange(parts_ref, land_ref, send_sems, recv_sems, local_sem, finish=False)

        dqt_acc[...] = jnp.zeros_like(dqt_acc)
        dk_acc[...] = jnp.zeros_like(dk_acc)
        dv_acc[...] = jnp.zeros_like(dv_acc)

        def q_block(i, _):
            q0 = pl.multiple_of(i * blk, blk)
            rows = []
            for h in range(hg):
                dov = do_ref[pl.ds(q0, blk), h * V_DIM:(h + 1) * V_DIM].astype(F32)
                dcol = jnp.sum(dov * o_ref[pl.ds(q0, blk), h * V_DIM:(h + 1) * V_DIM], axis=-1, keepdims=True)
                delta = jnp.broadcast_to(dcol, (blk, LANES)).T[0:1, :]
                rows.append((lse_ref[h, :, pl.ds(q0, blk)], delta))

            def pair(j, diagonal):
                k0 = pl.multiple_of(j * blk, blk)
                kr_j = kr_ref[pl.ds(k0, blk), :]
                krt_j = krt_ref[:, pl.ds(k0, blk)]
                for h in range(hg):
                    lse_i, delta = rows[h]
                    qv = q_ref[pl.ds(q0, blk), h * HEAD_PAD:(h + 1) * HEAD_PAD]
                    dov = do_ref[pl.ds(q0, blk), h * V_DIM:(h + 1) * V_DIM]
                    kv = jnp.concatenate([kn_ref[pl.ds(k0, blk), h * QK_NOPE:(h + 1) * QK_NOPE], kr_j], axis=1)
                    s = lax.dot_general(kv, qv, _NT, preferred_element_type=F32) * ATTN_SCALE
                    p = jnp.exp(s - lse_i)
                    if diagonal:
                        p = jnp.where(_diag_mask(blk), p, 0.0)
                    dv_acc[pl.ds(k0, blk), h * V_DIM:(h + 1) * V_DIM] += jnp.dot(
                        p.astype(BF16), dov, preferred_element_type=F32)
                    dp = lax.dot_general(v_ref[pl.ds(k0, blk), h * V_DIM:(h + 1) * V_DIM], dov, _NT,
                                         preferred_element_type=F32)
                    ds = (p * (dp - delta) * ATTN_SCALE).astype(BF16)
                    dk_acc[pl.ds(k0, blk), h * HEAD_PAD:(h + 1) * HEAD_PAD] += jnp.dot(ds, qv, preferred_element_type=F32)
                    base = h * HEAD_PAD
                    dqt_acc[base:base + QK_NOPE, pl.ds(q0, blk)] += jnp.dot(
                        knt_ref[h * QK_NOPE:(h + 1) * QK_NOPE, pl.ds(k0, blk)], ds, preferred_element_type=F32)
                    dqt_acc[base + QK_NOPE:base + HEAD_PAD, pl.ds(q0, blk)] += jnp.dot(
                        krt_j, ds, preferred_element_type=F32)

            def off_diagonal(j, _):
                pair(j, False)
                return 0

            lax.fori_loop(0, i, off_diagonal, 0)
            pair(i, True)
            return 0

        lax.fori_loop(0, nq, q_block, 0)
        dkr = jnp.zeros((seq, LANES), F32)
        for h in range(hg):
            base = h * HEAD_PAD
            for i in range(nq):
                rows = slice(i * blk, (i + 1) * blk)
                dq = dqt_acc[base:base + HEAD_PAD, rows].T
                dq_ref[rows, base:base + QK_NOPE] = dq[:, :QK_NOPE].astype(BF16)
                dq_ref[rows, base + QK_NOPE:base + HEAD_PAD] = _rope_t(
                    dq[:, QK_NOPE:], cos_ref[rows, :], sin_ref[rows, :]).astype(BF16)
            dkn_ref[:, h * QK_NOPE:(h + 1) * QK_NOPE] = dk_acc[:, base:base + QK_NOPE].astype(BF16)
            dkr = dkr + dk_acc[:, base + QK_NOPE:base + HEAD_PAD]
        dv_ref[...] = dv_acc[...].astype(BF16)

        @pl.when(pl.program_id(1) == 0)
        def _():
            dkr_ref[...] = jnp.zeros_like(dkr_ref)

        dkr_ref[...] += _rope_t(dkr, cos_ref[...], sin_ref[...])

        @pl.when(last)
        def _():
            _exchange(parts_ref, land_ref, send_sems, recv_sems, local_sem, finish=True)

    head = pl.BlockSpec((seq, hg * V_DIM), lambda b, g: (b, g))
    head_t = pl.BlockSpec((hg * V_DIM, seq), lambda b, g: (g, b))
    shared = pl.BlockSpec((seq, LANES), lambda b, g: (b, 0))
    shared_t = pl.BlockSpec((LANES, seq), lambda b, g: (0, b))
    table = pl.BlockSpec((seq, LANES), lambda b, g: (0, 0))
    qspec = pl.BlockSpec((seq, hg * HEAD_PAD), lambda b, g: (b, g))
    return pl.pallas_call(
        body, grid=steps,
        in_specs=[qspec, head, shared, head_t, shared_t, head, head,
                  pl.BlockSpec((hg, 1, seq), lambda b, g: (g, 0, b)), head, table, table, _ANY],
        out_specs=[qspec, head, shared, head, _ANY],
        out_shape=[jax.ShapeDtypeStruct((t, N_HEADS * HEAD_PAD), BF16), jax.ShapeDtypeStruct((t, N_HEADS * QK_NOPE), BF16),
                   jax.ShapeDtypeStruct((t, LANES), F32), jax.ShapeDtypeStruct((t, N_HEADS * V_DIM), BF16),
                   jax.ShapeDtypeStruct(parts.shape, parts.dtype)],
        scratch_shapes=[pltpu.VMEM((hg * HEAD_PAD, seq), F32), pltpu.VMEM((seq, hg * HEAD_PAD), F32),
                        pltpu.VMEM((seq, hg * V_DIM), F32)] + _EXCHANGE_SEMS,
        compiler_params=_params(("arbitrary", "arbitrary")), name=name)(
            q, kn, kr, kn_t, kr_t, v, o, lse, do, cos, sin, parts)


def _head_and_loss(o, g2, x1, target, w_out, g_final, *, name, bt=512):
    t, d = x1.shape
    bt = min(bt, t)
    nt = (((1,), (1,)), ((), ()))

    def body(o_ref, g2_ref, x1_ref, tgt_ref, w_ref, gf_ref, loss_ref, dx2_ref, y2_ref, do_ref, dg2_ref, dgf_ref):
        @pl.when(pl.program_id(0) == 0)
        def _():
            loss_ref[...] = jnp.zeros_like(loss_ref)
            dgf_ref[...] = jnp.zeros_like(dgf_ref)

        ov = o_ref[...]
        gv = g2_ref[...]
        sg = _sigmoid(gv)
        silu = gv * sg
        y2 = (ov * silu).astype(BF16)
        y2_ref[...] = y2
        w = w_ref[...]
        x2 = x1_ref[...] + jnp.dot(y2, w, preferred_element_type=F32)
        r = lax.rsqrt(jnp.mean(x2 * x2, axis=-1, keepdims=True) + EPS)
        nrm = x2 * r
        gf = gf_ref[...]
        err = nrm * gf - tgt_ref[...]
        loss_ref[...] += 0.5 * jnp.sum(jnp.mean(err * err, axis=-1, keepdims=True))
        dyf = err * (1.0 / d)
        dgf_ref[...] += jnp.sum(dyf * nrm, axis=0, keepdims=True)
        dn = dyf * gf
        dx2 = r * (dn - nrm * jnp.mean(dn * nrm, axis=-1, keepdims=True))
        dx2_ref[...] = dx2
        dy2 = lax.dot_general(dx2.astype(BF16), w, nt, preferred_element_type=F32)
        do_ref[...] = (dy2 * silu).astype(BF16)
        dg2_ref[...] = (dy2 * ov * (sg * (1.0 + gv * (1.0 - sg)))).astype(BF16)

    row = pl.BlockSpec((bt, d), lambda i: (i, 0))
    vec = pl.BlockSpec((1, d), lambda i: (0, 0))
    return pl.pallas_call(
        body, grid=(t // bt,),
        in_specs=[row, row, row, row, pl.BlockSpec((d, d), lambda i: (0, 0)), vec],
        out_specs=[pl.BlockSpec((8, LANES), lambda i: (0, 0)), row, row, row, row, vec],
        out_shape=[jax.ShapeDtypeStruct((8, LANES), F32), jax.ShapeDtypeStruct((t, d), F32),
                   jax.ShapeDtypeStruct((t, d), BF16), jax.ShapeDtypeStruct((t, d), BF16),
                   jax.ShapeDtypeStruct((t, d), BF16), jax.ShapeDtypeStruct((1, d), F32)],
        compiler_params=_params(("arbitrary",)), name=name)(o, g2, x1, target, w_out, g_final)


def _sum_parts(parts, *, name, br=GRAD_BLOCK):
    npart, rows, w = parts.shape

    def body(p_ref, o_ref):
        acc = p_ref[0].astype(F32)
        for j in range(1, npart):
            acc = acc + p_ref[j].astype(F32)
        o_ref[...] = acc

    return pl.pallas_call(
        body, grid=(rows // br,), in_specs=[pl.BlockSpec((npart, br, w), lambda i: (0, i, 0))],
        out_specs=pl.BlockSpec((br, w), lambda i: (i, 0)), out_shape=jax.ShapeDtypeStruct((rows, w), F32),
        compiler_params=_params(("parallel",)), name=name)(parts)


def _chip_partial(parts, recv, *, name, br=GRAD_BLOCK):
    _, rows, w = parts.shape
    core = lax.axis_index("c").astype(jnp.int32).reshape(1)

    def body(c_ref, p_ref, r_ref, o_ref):
        o_ref[...] = (p_ref[...] + r_ref[...]).astype(BF16)

    grid_spec = pltpu.PrefetchScalarGridSpec(
        num_scalar_prefetch=1, grid=(4, rows // br),
        in_specs=[pl.BlockSpec((None, br, w), lambda k, i, c_ref: (2 * k + c_ref[0], i, 0)),
                  pl.BlockSpec((None, br, w), lambda k, i, c_ref: (k, i, 0))],
        out_specs=pl.BlockSpec((None, br, w), lambda k, i, c_ref: (k, i, 0)))
    return pl.pallas_call(
        body, grid_spec=grid_spec, out_shape=jax.ShapeDtypeStruct((4, rows, w), BF16),
        compiler_params=_params(("parallel", "parallel")), name=name)(core, parts, recv)


def _as_block(a):
    if a.ndim == 1:
        return a.reshape(1, -1)
    if a.ndim > 2 and a.shape[0] == 1:
        return a.reshape(a.shape[1:])
    return a


def _adamw(g, w, m, v, *, name):
    shape = w.shape
    g, w, m, v = (_as_block(a) for a in (g, w, m, v))

    def body(g_ref, w_ref, m_ref, v_ref, d_ref, nm_ref, nv_ref):
        gv = g_ref[...]
        nm = ADAM_B1 * m_ref[...] + (1.0 - ADAM_B1) * gv
        nv = ADAM_B2 * v_ref[...] + (1.0 - ADAM_B2) * (gv * gv)
        nm_ref[...] = nm
        nv_ref[...] = nv
        m_hat = nm / (1.0 - ADAM_B1 ** ADAM_STEP)
        v_hat = nv / (1.0 - ADAM_B2 ** ADAM_STEP)
        d_ref[...] = (-ADAM_LR) * (m_hat / (jnp.sqrt(v_hat) + ADAM_EPS) + ADAM_WD * w_ref[...])

    whole = pl.BlockSpec(memory_space=pltpu.VMEM)
    outs = pl.pallas_call(
        body, in_specs=[whole] * 4, out_specs=[whole] * 3, out_shape=[jax.ShapeDtypeStruct(w.shape, F32)] * 3,
        compiler_params=_params(), name=name)(g, w, m, v)
    return [o.reshape(shape) for o in outs]


def _all_gather(block, *, name):
    m, n = block.shape

    def body(x_ref, out_ref, send_sems, recv_sems, local_sem):
        for phase in range(3):
            _gather_two_level(x_ref, out_ref, send_sems, recv_sems, local_sem, phase=phase)

    return pl.pallas_call(
        body, out_shape=jax.ShapeDtypeStruct((N_DEV, m, n), block.dtype), in_specs=[_ANY], out_specs=_ANY,
        scratch_shapes=_EXCHANGE_SEMS, name=name)(block)


def _exchange_d2d(parts, *, name):
    _, rows, w = parts.shape

    def body(p_ref, land_ref, send_sems, recv_sems):
        x, y, c = _mesh_pos()
        sends = []
        for k in range(4):
            cp = pltpu.make_async_remote_copy(
                src_ref=p_ref.at[2 * k + (1 - c)], dst_ref=land_ref.at[k], send_sem=send_sems.at[k],
                recv_sem=recv_sems.at[k], device_id=(x, y, 1 - c), device_id_type=pl.DeviceIdType.MESH)
            cp.start()
            sends.append(cp)
        for cp in sends:
            cp.wait_recv()
        for cp in sends:
            cp.wait_send()

    return pl.pallas_call(
        body, out_shape=jax.ShapeDtypeStruct((4, rows, w), parts.dtype), in_specs=[_ANY], out_specs=_ANY,
        scratch_shapes=[pltpu.SemaphoreType.DMA((4,)), pltpu.SemaphoreType.DMA((4,))], name=name)(parts)


def _exchange_ici(parts, *, name):
    def body(p_ref, land_ref, send_sems, recv_sems, local_sem):
        x, y, c = _mesh_pos()
        mine = pltpu.make_async_copy(p_ref.at[2 * x + y], land_ref.at[3], local_sem)
        mine.start()
        sends = []
        for k, (px, py) in enumerate([(1 - x, y), (x, 1 - y), (1 - x, 1 - y)]):
            cp = pltpu.make_async_remote_copy(
                src_ref=p_ref.at[2 * px + py], dst_ref=land_ref.at[k], send_sem=send_sems.at[k],
                recv_sem=recv_sems.at[k], device_id=(px, py, c), device_id_type=pl.DeviceIdType.MESH)
            cp.start()
            sends.append(cp)
        for cp in sends:
            cp.wait_recv()
        for cp in sends:
            cp.wait_send()
        mine.wait()

    return pl.pallas_call(
        body, out_shape=jax.ShapeDtypeStruct(parts.shape, parts.dtype), in_specs=[_ANY], out_specs=_ANY,
        scratch_shapes=[pltpu.SemaphoreType.DMA((3,)), pltpu.SemaphoreType.DMA((3,)), pltpu.SemaphoreType.DMA(())],
        name=name)(parts)


def _rows(a):
    return a.reshape(-1, PACK_W)


def _pad_to(a, n):
    return jnp.pad(a, (0, n - a.shape[0]))


def _weight_blocks(d):
    small = _rows(_pad_to(jnp.concatenate([d[n].reshape(-1) for n, _ in _SMALL]), 8 * PACK_W))
    block_a = jnp.concatenate([d["w_in_a"][0].T.astype(WIRE), lax.bitcast_convert_type(small, WIRE).reshape(16, PACK_W)],
                              axis=0)
    w_uq = jnp.pad(d["w_uq"][0], ((0, 0), (0, 0), (0, HEAD_PAD - QK_NOPE - QK_ROPE)))
    pieces = {"w_out_a": d["w_out_a"], "w_dkv": d["w_dkv"], "w_uk": d["w_uk"], "w_uv": d["w_uv"],
              "w_in_b": d["w_in_b"][0].T, "w_uq": w_uq}
    block_b = jnp.concatenate([_rows(pieces[n]) for n, _ in _PIECES_B]
                              + [jnp.zeros((WIRE_ROWS_B - MATRIX_ROWS_B, PACK_W), F32)], axis=0).astype(WIRE)
    return block_a, block_b, d["w_out_b"][0].astype(WIRE)


def _weights_a(wall):
    w = {}
    lo, hi = _OFF_A["w_in_a"]
    w["w_in_a_t"] = wall[:, lo:hi].reshape(2 * D_RNN, D_MODEL)
    small = lax.bitcast_convert_type(wall[:, MATRIX_ROWS_A:].reshape(N_DEV, 8 * PACK_W, 2), F32)
    off = dict(zip([n for n, _ in _SMALL], [0, 128, 768, 928, 1088, 1248]))
    w["norm_a"] = small[:, :128].reshape(1, D_MODEL)

    def by_channel(lo, rows):
        a = small[:, lo:lo + rows * (D_RNN // N_DEV)].reshape(N_DEV, rows, -1).transpose(1, 0, 2).reshape(rows, D_RNN)
        return jnp.pad(a, ((0, 8 - rows), (0, 0)))

    w["conv_taps"] = by_channel(off["conv_w"], CONV_WIDTH)
    w["lru_vecs"] = by_channel(off["conv_b"], 4)
    return w


def _weights_b(wall):
    piece = {n: wall[:, lo:hi] for n, (lo, hi) in _OFF_B.items()}
    w = {"w_out_a": piece["w_out_a"].reshape(D_RNN, D_MODEL)}
    w_dkv = piece["w_dkv"].reshape(D_MODEL, KV_RANK + QK_ROPE)
    w["w_dkv_c"] = w_dkv[:, :KV_RANK]
    w["w_dkv_r"] = jnp.pad(w_dkv[:, KV_RANK:], ((0, 0), (0, LANES - QK_ROPE)))
    w["w_uk"] = piece["w_uk"].reshape(KV_RANK, N_HEADS * QK_NOPE)
    w["w_uv"] = piece["w_uv"].reshape(KV_RANK, N_HEADS * V_DIM)
    w["w_uk_t"], w["w_uv_t"] = w["w_uk"].T, w["w_uv"].T
    w["w_in_b_t"] = piece["w_in_b"].reshape(Q_RANK + N_HEADS * V_DIM, D_MODEL)
    w["w_uq"] = piece["w_uq"].reshape(Q_RANK, N_HEADS * HEAD_PAD)
    return w


def _pack_rep(d):
    flat = jnp.concatenate([d[n].reshape(-1) for n, _ in _REP])
    return _rows(_pad_to(flat, REP_ROWS * PACK_W))


def _unpack_rep(p, like):
    flat = p.reshape(-1)
    out, off = {}, 0
    for n, k in _REP:
        out[n] = flat[off:off + k].reshape(like[n].shape)
        off += k
    return out


def _by_owner(a):
    return a.reshape(N_DEV, -1, PACK_W)


def _grad_parts_b(g):
    tail = jnp.zeros((N_DEV, WIRE_ROWS_B - MATRIX_ROWS_B, PACK_W), F32)
    return jnp.concatenate([_by_owner(g[n]) for n, _ in _PIECES_B] + [tail], axis=1).astype(BF16)


def _grad_parts_a(g):
    small = jnp.concatenate([
        g["norm_a"].reshape(N_DEV, -1),
        g["conv_w"].reshape(CONV_WIDTH, N_DEV, -1).transpose(1, 0, 2).reshape(N_DEV, -1),
        g["conv_b"].reshape(N_DEV, -1), g["b_rg"].reshape(N_DEV, -1), g["b_ig"].reshape(N_DEV, -1),
        g["lru_lambda"].reshape(N_DEV, -1)], axis=1)
    small = jnp.pad(small, ((0, 0), (0, 8 * PACK_W - small.shape[1]))).reshape(N_DEV, 8, PACK_W)
    half = N_DEV // 2
    w_in_a = jnp.concatenate([h.reshape(half, -1, PACK_W) for h in g["w_in_a_t"]], axis=0)
    rep = _pack_rep(g).reshape(N_DEV, REP_SLICE, PACK_W)
    tail = jnp.zeros((N_DEV, GRAD_ROWS_A - MATRIX_ROWS_A - 8 - REP_SLICE, PACK_W), F32)
    return jnp.concatenate([w_in_a, small, rep, tail], axis=1)


def _own_grads(sum_a, sum_b, sum_c):
    out = {}
    lo, hi = _OFF_A["w_in_a"]
    out["w_in_a"] = sum_a[lo:hi].T.reshape(1, D_MODEL, 2 * D_RNN // N_DEV)
    small = sum_a[MATRIX_ROWS_A:MATRIX_ROWS_A + 8].reshape(-1)
    shapes = {"norm_a": (1, D_MODEL // N_DEV), "conv_w": (1, CONV_WIDTH, D_RNN // N_DEV), "conv_b": (1, D_RNN // N_DEV),
              "b_rg": (1, D_RNN // N_DEV), "b_ig": (1, D_RNN // N_DEV), "lru_lambda": (1, D_RNN // N_DEV)}
    off = 0
    for n, k in _SMALL:
        out[n] = small[off:off + k].reshape(shapes[n])
        off += k
    piece = {n: sum_b[lo:hi] for n, (lo, hi) in _OFF_B.items()}
    out["w_out_a"] = piece["w_out_a"].reshape(1, D_RNN // N_DEV, D_MODEL)
    out["w_dkv"] = piece["w_dkv"].reshape(D_MODEL // N_DEV, KV_RANK + QK_ROPE)
    out["w_uk"] = piece["w_uk"].reshape(KV_RANK // N_DEV, N_HEADS, QK_NOPE)
    out["w_uv"] = piece["w_uv"].reshape(KV_RANK // N_DEV, N_HEADS, V_DIM)
    out["w_in_b"] = piece["w_in_b"].T.reshape(1, D_MODEL, (Q_RANK + N_HEADS * V_DIM) // N_DEV)
    out["w_uq"] = piece["w_uq"].reshape(1, Q_RANK // N_DEV, N_HEADS, HEAD_PAD)[..., :QK_NOPE + QK_ROPE]
    out["w_out_b"] = sum_c.reshape(1, N_HEADS * V_DIM // N_DEV, D_MODEL)
    return out


def _step(x, target, w, rep, block_b, block_c, *, bsz, seq):
    t = bsz * seq
    cos, sin = _rope_tables(seq)
    g_a = w["norm_a"]
    g_kv = rep["norm_kv"].reshape(1, -1)
    g_kvn = rep["kv_norm"].reshape(1, -1)
    g_b = rep["norm_b"].reshape(1, -1)
    g_q = rep["q_norm"].reshape(1, -1)
    g_f = rep["final_norm"].reshape(1, -1)
    wrg = rep["w_rg"][0].astype(BF16)
    wig = rep["w_ig"][0].astype(BF16)
    cw8, vecs = w["conv_taps"], w["lru_vecs"]

    def seq3(a):
        return a.reshape(bsz, seq, a.shape[-1])

    def flat(a):
        return a.reshape(t, a.shape[-1])

    h0, xp, ga = _lru_proj_fwd(x, g_a, w["w_in_a_t"], name="lru_proj_fwd")
    xb, hs, y, wall_b = _lru_fwd(seq3(xp), seq3(ga), cw8, vecs, wrg, wig, block_b, name="lru_fwd")
    w = dict(w, **_weights_b(wall_b))
    x1 = _matmul(flat(y), w["w_out_a"], residual=x, name="out_a")
    hk, hq, ck, cqp, g2, ckv, cq, q, kn, v, kr, kn_t, v_t, kr_t = _mla_proj_fwd(
        x1, (g_kv, g_b, g_kvn, g_q), w, cos, sin, seq=seq, name="mla_proj_fwd")
    o, lse, wall_c = _attn_fwd(q, kn, kr, v_t, block_c, bsz=bsz, seq=seq, name="attn_fwd")
    w_out_b = wall_c.reshape(N_HEADS * V_DIM, D_MODEL)
    loss, dx2, y2, do, dg2, dgf = _head_and_loss(o, g2, x1, target, w_out_b, g_f, name="head_loss")
    grads = {"final_norm": dgf}
    parts_c = _by_owner(_matmul_tn(y2, dx2, name="d_w_out_b")).astype(BF16)
    dq, dkn, dkr, dv, landed_c = _attn_bwd(q, kn, kr, kn_t, kr_t, v, o, lse, do, cos, sin, parts_c,
                                           bsz=bsz, seq=seq, name="attn_bwd")
    grads["w_uq"] = _matmul_tn(cq, dq, name="d_w_uq")
    dx1, du2, dckr, dgkv, dgb, dgkvn, dgq = _mla_proj_bwd(
        x1, dx2, cqp, ck, dq, dkn, dv, dkr, dg2, (g_kv, g_b, g_kvn, g_q), w, name="mla_proj_bwd")
    grads["norm_kv"], grads["norm_b"], grads["kv_norm"], grads["q_norm"] = dgkv, dgb, dgkvn, dgq
    grads["w_in_b"] = _matmul_tn(du2, hq, name="d_w_in_b_t")
    grads["w_uk"] = _matmul_tn(ckv, dkn, name="d_w_uk")
    grads["w_uv"] = _matmul_tn(ckv, dv, name="d_w_uv")
    grads["w_dkv"] = _matmul_tn(hk, dckr, name="d_w_dkv")[:, :KV_RANK + QK_ROPE]
    grads["w_out_a"] = _matmul_tn(flat(y), dx1, name="d_w_out_a")
    parts_b = _grad_parts_b(grads)
    dy = _matmul(dx1, w["w_out_a"], nt=True, name="d_y")
    dxp, dga, dwrg, dwig, dvec, landed_b = _lru_bwd(
        seq3(dy), seq3(xp), xb, hs, seq3(ga), cw8, vecs, wrg, wig, parts_b, name="lru_bwd")
    dxp, dga = flat(dxp), flat(dga)
    grads["w_rg"], grads["w_ig"] = dwrg, dwig
    grads["b_rg"], grads["b_ig"], grads["conv_b"] = dvec[0], dvec[1], dvec[3]
    lam = vecs[3]
    grads["lru_lambda"] = dvec[2] * (-1.0 / (1.0 + jnp.exp(lam)))
    grads["conv_w"] = dvec[4:4 + CONV_WIDTH]
    grads["w_in_a_t"] = (_matmul_tn(dxp, h0, name="d_w_in_a_x_t"), _matmul_tn(dga, h0, name="d_w_in_a_g_t"))
    dx, dga_norm = _lru_proj_bwd(dxp, dga, x, dx1, g_a, w["w_in_a_t"], name="lru_proj_bwd")
    grads["norm_a"] = dga_norm
    return loss[0, 0], dx, grads, landed_b, landed_c


def kernel(x, norm_a, w_in_a, conv_w, conv_b, w_rg, b_rg, w_ig, b_ig, lru_lambda, w_out_a, norm_kv, w_dkv, kv_norm, w_uk, w_uv, norm_b, w_in_b, q_norm, w_uq, w_out_b, final_norm, loss_target, m_norm_a, m_w_in_a, m_conv_w, m_conv_b, m_w_rg, m_b_rg, m_w_ig, m_b_ig, m_lru_lambda, m_w_out_a, m_norm_kv, m_w_dkv, m_kv_norm, m_w_uk, m_w_uv, m_norm_b, m_w_in_b, m_q_norm, m_w_uq, m_w_out_b, m_final_norm, v_norm_a, v_w_in_a, v_conv_w, v_conv_b, v_w_rg, v_b_rg, v_w_ig, v_b_ig, v_lru_lambda, v_w_out_a, v_norm_kv, v_w_dkv, v_kv_norm, v_w_uk, v_w_uv, v_norm_b, v_w_in_b, v_q_norm, v_w_uq, v_w_out_b, v_final_norm):
    given = dict(locals())
    wts = {n: given[n] for n in WEIGHTS}
    mom1 = {n: given["m_" + n] for n in WEIGHTS}
    mom2 = {n: given["v_" + n] for n in WEIGHTS}
    bsz, seq, _ = x.shape
    t = bsz * seq

    block_a, block_b, block_c = _weight_blocks(wts)
    w = _weights_a(_all_gather(block_a, name="gather_weights_a"))
    loss, dx, grads, landed_b, landed_c = _step(x.reshape(t, D_MODEL), loss_target.reshape(t, D_MODEL), w, wts,
                                                block_b, block_c, bsz=bsz, seq=seq)

    parts_a = _grad_parts_a(grads)
    from_sibling = _exchange_d2d(parts_a, name="exchange_grads_d2d")
    chip_parts = _chip_partial(parts_a, from_sibling, name="chip_partial_grads")
    landed_a = _exchange_ici(chip_parts, name="exchange_grads_ici")
    sum_a = _sum_parts(landed_a, name="sum_grads_a", br=GRAD_BLOCK)
    sum_b = _sum_parts(landed_b, name="sum_grads_b", br=WIRE_ROWS_B // 2)
    sum_c = _sum_parts(landed_c, name="sum_grads_c", br=landed_c.shape[1])
    g_own = _own_grads(sum_a, sum_b, sum_c)
    rep_slice = sum_a[MATRIX_ROWS_A + 8:MATRIX_ROWS_A + 8 + REP_SLICE]
    loss_rows = jnp.pad(loss.reshape(1, 1), ((0, 7), (0, PACK_W - 1)))
    gathered = _all_gather(jnp.concatenate([rep_slice, loss_rows], axis=0), name="gather_replicated")
    g_own.update(_unpack_rep(gathered[:, :REP_SLICE].reshape(REP_ROWS, PACK_W), wts))
    loss = jnp.sum(gathered[:, REP_SLICE, 0])

    deltas, new_m, new_v = {}, {}, {}
    for n in WEIGHTS:
        deltas[n], new_m[n], new_v[n] = _adamw(g_own[n], wts[n], mom1[n], mom2[n], name="adamw_" + n)
    result = [loss, dx.reshape(bsz, seq, D_MODEL)]
    for d in (g_own, deltas, new_m, new_v):
        result.extend(d[n] for n in WEIGHTS)
    return tuple(result)
```

```python
import jax
import jax.numpy as jnp
from jax import lax
from jax.experimental import pallas as pl
from jax.experimental.pallas import tpu as pltpu

F32 = jnp.float32
BF16 = jnp.bfloat16
WIRE = jnp.bfloat16

D_MODEL = 1024
D_RNN = 1280
RNN_BLOCKS = 10
RNN_BW = 128
CONV_WIDTH = 4
LRU_C = 8.0
N_HEADS = 8
QK_NOPE = 128
QK_ROPE = 64
V_DIM = 128
KV_RANK = 256
Q_RANK = 384
ROPE_THETA = 10000.0
EPS = 1e-6
ATTN_SCALE = (QK_NOPE + QK_ROPE) ** -0.5
HEAD_PAD = 256
LANES = 128

ADAM_LR = 0.001
ADAM_B1 = 0.9
ADAM_B2 = 0.999
ADAM_EPS = 1e-08
ADAM_WD = 0.01
ADAM_STEP = 10

N_DEV = 8
VMEM_LIMIT_BYTES = 56 * 2**20
PACK_W = 1024

_PIECES_A = (("w_in_a", 320),)
_PIECES_B = (("w_out_a", 160), ("w_dkv", 40), ("w_uk", 32), ("w_uv", 32), ("w_in_b", 176), ("w_uq", 96))


def _offsets(pieces):
    off, r = {}, 0
    for n, k in pieces:
        off[n] = (r, r + k)
        r += k
    return off, r


_OFF_A, MATRIX_ROWS_A = _offsets(_PIECES_A)
_OFF_B, MATRIX_ROWS_B = _offsets(_PIECES_B)
WIRE_ROWS_A = MATRIX_ROWS_A + 32
WIRE_ROWS_B = 544
_SMALL = (("norm_a", 128), ("conv_w", 640), ("conv_b", 160), ("b_rg", 160), ("b_ig", 160), ("lru_lambda", 160))
_REP = (("w_rg", 163840), ("w_ig", 163840), ("norm_kv", 1024), ("kv_norm", 256), ("norm_b", 1024),
        ("q_norm", 384), ("final_norm", 1024))
REP_ROWS = 384
REP_SLICE = REP_ROWS // N_DEV
GRAD_ROWS_A = 384
GRAD_BLOCK = 192

WEIGHTS = ("norm_a", "w_in_a", "conv_w", "conv_b", "w_rg", "b_rg", "w_ig", "b_ig", "lru_lambda", "w_out_a",
           "norm_kv", "w_dkv", "kv_norm", "w_uk", "w_uv", "norm_b", "w_in_b", "q_norm", "w_uq", "w_out_b",
           "final_norm")


def _params(sem=None):
    return pltpu.CompilerParams(dimension_semantics=sem, vmem_limit_bytes=VMEM_LIMIT_BYTES)


_NT = (((1,), (1,)), ((), ()))
_ANY = pl.BlockSpec(memory_space=pl.ANY)


def _mesh_pos():
    return lax.axis_index("x"), lax.axis_index("y"), lax.axis_index("c")


def _sigmoid(z):
    return 0.5 * jnp.tanh(0.5 * z) + 0.5


def _sigmoid_tail(z):
    return 1.0 / (1.0 + jnp.exp(-z))


def _col_block(n):
    return n if n <= 1408 else n // 2


def _matmul(a, b, *, name, nt=False, out_dtype=F32, residual=None, bm=1024):
    m, k = a.shape
    n = b.shape[0] if nt else b.shape[1]
    bm = min(bm, m)
    bn = _col_block(n)
    dims = (((1,), (1,)), ((), ())) if nt else (((1,), (0,)), ((), ()))
    has_res = residual is not None

    def body(*refs):
        a_ref, b_ref, o_ref = refs[0], refs[1], refs[-1]
        acc = lax.dot_general(a_ref[...].astype(BF16), b_ref[...].astype(BF16), dims, preferred_element_type=F32)
        if has_res:
            acc = acc + refs[2][...]
        o_ref[...] = acc.astype(out_dtype)

    in_specs = [pl.BlockSpec((bm, k), lambda i, j: (i, 0)),
                pl.BlockSpec((bn, k), lambda i, j: (j, 0)) if nt else pl.BlockSpec((k, bn), lambda i, j: (0, j))]
    args = [a, b]
    if has_res:
        in_specs.append(pl.BlockSpec((bm, bn), lambda i, j: (i, j)))
        args.append(residual)
    return pl.pallas_call(
        body, grid=(m // bm, n // bn), in_specs=in_specs, out_specs=pl.BlockSpec((bm, bn), lambda i, j: (i, j)),
        out_shape=jax.ShapeDtypeStruct((m, n), out_dtype), compiler_params=_params(("parallel", "parallel")),
        name=name)(*args)


def _matmul_tn(a, b, *, name, bt=1024):
    t, m = a.shape
    n = b.shape[1]
    bt = min(bt, t)
    bm, bn = _col_block(m), _col_block(n)

    def body(a_ref, b_ref, o_ref):
        @pl.when(pl.program_id(2) == 0)
        def _():
            o_ref[...] = jnp.zeros_like(o_ref)

        o_ref[...] += lax.dot_general(a_ref[...].astype(BF16), b_ref[...].astype(BF16),
                                      (((0,), (0,)), ((), ())), preferred_element_type=F32)

    return pl.pallas_call(
        body, grid=(m // bm, n // bn, t // bt),
        in_specs=[pl.BlockSpec((bt, bm), lambda i, j, s: (s, i)), pl.BlockSpec((bt, bn), lambda i, j, s: (s, j))],
        out_specs=pl.BlockSpec((bm, bn), lambda i, j, s: (i, j)),
        out_shape=jax.ShapeDtypeStruct((m, n), F32),
        compiler_params=_params(("parallel", "parallel", "arbitrary")), name=name)(a, b)


def _swap_halves(v):
    ax = v.ndim - 1
    lane = lax.broadcasted_iota(jnp.int32, v.shape, ax)
    up = pltpu.roll(v, LANES - QK_ROPE // 2, axis=ax)
    down = pltpu.roll(v, QK_ROPE // 2, axis=ax)
    return jnp.where(lane < QK_ROPE // 2, up, jnp.where(lane < QK_ROPE, down, 0.0))


def _rope(v, cos, sin):
    return v * cos + _swap_halves(v) * sin


def _rope_t(d, cos, sin):
    return d * cos + _swap_halves(d * sin)


def _rope_tables(seq):
    pos = jnp.arange(seq, dtype=F32)
    inv = ROPE_THETA ** (-jnp.arange(0, QK_ROPE, 2, dtype=F32) / QK_ROPE)
    ang = pos[:, None] * inv[None, :]
    cos, sin = jnp.cos(ang), jnp.sin(ang)
    zero = jnp.zeros((seq, LANES - QK_ROPE), F32)
    return jnp.concatenate([cos, cos, zero], axis=1), jnp.concatenate([-sin, sin, zero], axis=1)


def _rms(v):
    return v * lax.rsqrt(jnp.mean(v * v, axis=-1, keepdims=True) + EPS)


def _const_spec(a):
    return pl.BlockSpec(a.shape, lambda i: (0,) * a.ndim)


def _lru_proj_fwd(x, g_a, w_in_t, *, name, bt=512):
    t, d = x.shape
    bt = min(bt, t)
    n = w_in_t.shape[0] // 2

    def body(x_ref, g_ref, wt_ref, h_ref, xp_ref, ga_ref):
        h = (_rms(x_ref[...]) * g_ref[...]).astype(BF16)
        h_ref[...] = h
        xp_ref[...] = lax.dot_general(h, wt_ref[0:n, :], _NT, preferred_element_type=F32)
        ga_ref[...] = lax.dot_general(h, wt_ref[n:2 * n, :], _NT, preferred_element_type=F32)

    row = lambda w: pl.BlockSpec((bt, w), lambda i: (i, 0))
    return pl.pallas_call(
        body, grid=(t // bt,), in_specs=[row(d), _const_spec(g_a), _const_spec(w_in_t)],
        out_specs=[row(d), row(n), row(n)],
        out_shape=[jax.ShapeDtypeStruct((t, d), BF16), jax.ShapeDtypeStruct((t, n), F32), jax.ShapeDtypeStruct((t, n), F32)],
        compiler_params=_params(("parallel",)), name=name)(x, g_a, w_in_t)


def _mla_proj_fwd(x1, gains, w, cos, sin, *, seq, name, bt=512):
    t, d = x1.shape
    bt = min(bt, seq)
    per_seq = seq // bt
    g_kv, g_b, g_kvn, g_q = gains
    consts = [g_kv, g_b, g_kvn, g_q, w["w_dkv_c"], w["w_dkv_r"], w["w_in_b_t"], w["w_uk"], w["w_uv"],
              w["w_uk_t"], w["w_uv_t"], w["w_uq"]]

    def body(x_ref, cos_ref, sin_ref, gkv_ref, gb_ref, gkvn_ref, gq_ref, wdc_ref, wdr_ref, wbt_ref,
             wuk_ref, wuv_ref, wukt_ref, wuvt_ref, wuq_ref,
             hk_ref, hq_ref, ck_ref, cqp_ref, g2_ref, ckv_ref, cq_ref, q_ref, kn_ref, v_ref, kr_ref, knt_ref, vt_ref, krt_ref):
        nrm = _rms(x_ref[...])
        hk = (nrm * gkv_ref[...]).astype(BF16)
        hq = (nrm * gb_ref[...]).astype(BF16)
        hk_ref[...] = hk
        hq_ref[...] = hq
        ck = jnp.dot(hk, wdc_ref[...], preferred_element_type=F32)
        ck_ref[...] = ck
        cqp = lax.dot_general(hq, wbt_ref[0:Q_RANK, :], _NT, preferred_element_type=F32)
        cqp_ref[...] = cqp
        g2_ref[...] = lax.dot_general(hq, wbt_ref[Q_RANK:, :], _NT, preferred_element_type=F32)
        cosv, sinv = cos_ref[...], sin_ref[...]
        kr = _rope(jnp.dot(hk, wdr_ref[...], preferred_element_type=F32), cosv, sinv)
        kr_ref[...] = kr.astype(BF16)
        krt_ref[...] = kr.T.astype(BF16)
        ckv = (_rms(ck) * gkvn_ref[...]).astype(BF16)
        ckv_ref[...] = ckv
        kn_ref[...] = jnp.dot(ckv, wuk_ref[...], preferred_element_type=F32).astype(BF16)
        v_ref[...] = jnp.dot(ckv, wuv_ref[...], preferred_element_type=F32).astype(BF16)
        knt_ref[...] = lax.dot_general(wukt_ref[...], ckv, _NT, preferred_element_type=F32).astype(BF16)
        vt_ref[...] = lax.dot_general(wuvt_ref[...], ckv, _NT, preferred_element_type=F32).astype(BF16)
        cq = (_rms(cqp) * gq_ref[...]).astype(BF16)
        cq_ref[...] = cq
        for h in range(N_HEADS):
            qh = jnp.dot(cq, wuq_ref[:, h * HEAD_PAD:(h + 1) * HEAD_PAD], preferred_element_type=F32)
            q_ref[:, h * HEAD_PAD:h * HEAD_PAD + QK_NOPE] = qh[:, :QK_NOPE].astype(BF16)
            q_ref[:, h * HEAD_PAD + QK_NOPE:(h + 1) * HEAD_PAD] = _rope(qh[:, QK_NOPE:], cosv, sinv).astype(BF16)

    row = lambda w_: pl.BlockSpec((bt, w_), lambda i: (i, 0))
    col = lambda h_: pl.BlockSpec((h_, bt), lambda i: (0, i))
    tab = pl.BlockSpec((bt, LANES), lambda i: (i % per_seq, 0))
    nh = N_HEADS * V_DIM
    shapes = [((t, d), BF16), ((t, d), BF16), ((t, KV_RANK), F32), ((t, Q_RANK), F32), ((t, nh), F32), ((t, KV_RANK), BF16),
              ((t, Q_RANK), BF16), ((t, N_HEADS * HEAD_PAD), BF16), ((t, nh), BF16), ((t, nh), BF16), ((t, LANES), BF16),
              ((nh, t), BF16), ((nh, t), BF16), ((LANES, t), BF16)]
    out_specs = [row(d), row(d), row(KV_RANK), row(Q_RANK), row(nh), row(KV_RANK), row(Q_RANK), row(N_HEADS * HEAD_PAD),
                 row(nh), row(nh), row(LANES), col(nh), col(nh), col(LANES)]
    return pl.pallas_call(
        body, grid=(t // bt,), in_specs=[row(d), tab, tab] + [_const_spec(a) for a in consts], out_specs=out_specs,
        out_shape=[jax.ShapeDtypeStruct(s, dt) for s, dt in shapes],
        compiler_params=_params(("parallel",)), name=name)(x1, cos, sin, *consts)


def _rms_bwd_rows(xv, dn):
    r = lax.rsqrt(jnp.mean(xv * xv, axis=-1, keepdims=True) + EPS)
    nrm = xv * r
    return r * (dn - nrm * jnp.mean(dn * nrm, axis=-1, keepdims=True)), nrm


def _col_sum(v):
    return jnp.sum(v, axis=0, keepdims=True)


def _lru_proj_bwd(dxp, dga, x, dx1, g_a, w_in_t, *, name, bt=512):
    t, d = x.shape
    bt = min(bt, t)
    n = w_in_t.shape[0] // 2

    def body(dxp_ref, dga_ref, x_ref, dx1_ref, g_ref, wt_ref, dx_ref, dg_ref):
        @pl.when(pl.program_id(0) == 0)
        def _():
            dg_ref[...] = jnp.zeros_like(dg_ref)

        dh = (jnp.dot(dxp_ref[...], wt_ref[0:n, :], preferred_element_type=F32)
              + jnp.dot(dga_ref[...], wt_ref[n:2 * n, :], preferred_element_type=F32))
        dxn, nrm = _rms_bwd_rows(x_ref[...], dh * g_ref[...])
        dg_ref[...] += _col_sum(dh * nrm)
        dx_ref[...] = dx1_ref[...] + dxn

    row = lambda w: pl.BlockSpec((bt, w), lambda i: (i, 0))
    return pl.pallas_call(
        body, grid=(t // bt,),
        in_specs=[row(n), row(n), row(d), row(d), _const_spec(g_a), _const_spec(w_in_t)],
        out_specs=[row(d), _const_spec(g_a)],
        out_shape=[jax.ShapeDtypeStruct((t, d), F32), jax.ShapeDtypeStruct((1, d), F32)],
        compiler_params=_params(("arbitrary",)), name=name)(dxp, dga, x, dx1, g_a, w_in_t)


def _mla_proj_bwd(x1, dx2, cqp, ck, dq, dkn, dv, dkr, dg2, gains, w, *, name, bt=512):
    t, d = x1.shape
    bt = min(bt, t)
    g_kv, g_b, g_kvn, g_q = gains
    consts = [g_kv, g_b, g_kvn, g_q, w["w_dkv_c"], w["w_dkv_r"], w["w_in_b_t"], w["w_uk"], w["w_uv"], w["w_uq"]]
    nh = N_HEADS * V_DIM

    def body(x1_ref, dx2_ref, cqp_ref, ck_ref, dq_ref, dkn_ref, dv_ref, dkr_ref, dg2_ref,
             gkv_ref, gb_ref, gkvn_ref, gq_ref, wdc_ref, wdr_ref, wbt_ref, wuk_ref, wuv_ref, wuq_ref,
             dx1_ref, du2_ref, dckr_ref, dgkv_ref, dgb_ref, dgkvn_ref, dgq_ref):
        @pl.when(pl.program_id(0) == 0)
        def _():
            for ref in (dgkv_ref, dgb_ref, dgkvn_ref, dgq_ref):
                ref[...] = jnp.zeros_like(ref)

        dot_nt = lambda a, b: lax.dot_general(a, b, _NT, preferred_element_type=F32)
        dcq = dot_nt(dq_ref[...], wuq_ref[...])
        dcqp, nq = _rms_bwd_rows(cqp_ref[...], dcq * gq_ref[...])
        dgq_ref[...] += _col_sum(dcq * nq)
        dcqp = dcqp.astype(BF16)
        dg2 = dg2_ref[...]
        du2_ref[:, :Q_RANK] = dcqp
        du2_ref[:, Q_RANK:] = dg2
        dhq = (jnp.dot(dcqp, wbt_ref[0:Q_RANK, :], preferred_element_type=F32)
               + jnp.dot(dg2, wbt_ref[Q_RANK:, :], preferred_element_type=F32))
        dckv = dot_nt(dkn_ref[...], wuk_ref[...]) + dot_nt(dv_ref[...], wuv_ref[...])
        dck, nc = _rms_bwd_rows(ck_ref[...], dckv * gkvn_ref[...])
        dgkvn_ref[...] += _col_sum(dckv * nc)
        dck = dck.astype(BF16)
        dkr = dkr_ref[...].astype(BF16)
        dckr_ref[:, :KV_RANK] = dck
        dckr_ref[:, KV_RANK:] = dkr
        dhk = dot_nt(dck, wdc_ref[...]) + dot_nt(dkr, wdr_ref[...])
        dxn, n1 = _rms_bwd_rows(x1_ref[...], dhq * gb_ref[...] + dhk * gkv_ref[...])
        dgb_ref[...] += _col_sum(dhq * n1)
        dgkv_ref[...] += _col_sum(dhk * n1)
        dx1_ref[...] = dx2_ref[...] + dxn

    row = lambda w_: pl.BlockSpec((bt, w_), lambda i: (i, 0))
    vec = lambda w_: pl.BlockSpec((1, w_), lambda i: (0, 0))
    in_specs = [row(d), row(d), row(Q_RANK), row(KV_RANK), row(N_HEADS * HEAD_PAD), row(nh), row(nh), row(LANES), row(nh)]
    return pl.pallas_call(
        body, grid=(t // bt,), in_specs=in_specs + [_const_spec(a) for a in consts],
        out_specs=[row(d), row(Q_RANK + nh), row(KV_RANK + LANES), vec(d), vec(d), vec(KV_RANK), vec(Q_RANK)],
        out_shape=[jax.ShapeDtypeStruct((t, d), F32), jax.ShapeDtypeStruct((t, Q_RANK + nh), BF16),
                   jax.ShapeDtypeStruct((t, KV_RANK + LANES), BF16), jax.ShapeDtypeStruct((1, d), F32),
                   jax.ShapeDtypeStruct((1, d), F32), jax.ShapeDtypeStruct((1, KV_RANK), F32),
                   jax.ShapeDtypeStruct((1, Q_RANK), F32)],
        compiler_params=_params(("arbitrary",)), name=name)(x1, dx2, cqp, ck, dq, dkn, dv, dkr, dg2, *consts)


def _softplus(z):
    return jnp.maximum(z, 0.0) + jnp.log1p(jnp.exp(-jnp.abs(z)))


def _one_minus_square(a, la):
    return jnp.tanh(-la) * (1.0 + a * a)


def _gates(xb, wrg, wig, brg, big, sp):
    xbb = xb.astype(BF16)
    r = _sigmoid_tail(jnp.dot(xbb, wrg, preferred_element_type=F32) + brg)
    i = _sigmoid(jnp.dot(xbb, wig, preferred_element_type=F32) + big)
    la = (-LRU_C) * r * sp
    a = jnp.exp(la)
    em = _one_minus_square(a, la)
    inv_mult = lax.rsqrt(em)
    mult = jnp.where(em > 0.0, em * inv_mult, 0.0)
    return r, i, a, mult, inv_mult


def _conv(xpad_ref, cw_ref, seq):
    acc = cw_ref[0:1, :] * xpad_ref[pl.ds(8 - (CONV_WIDTH - 1), seq), :]
    for k in range(1, CONV_WIDTH):
        acc = acc + cw_ref[k:k + 1, :] * xpad_ref[pl.ds(8 - (CONV_WIDTH - 1) + k, seq), :]
    return acc


def _seq_spec(seq):
    return pl.BlockSpec((None, seq, RNN_BW), lambda n, b: (b, 0, n))


def _chan_spec(rows):
    return pl.BlockSpec((rows, RNN_BW), lambda n, b: (0, n))


_GATE_W_SPEC = pl.BlockSpec((None, RNN_BW, RNN_BW), lambda n, b: (n, 0, 0))


SCAN_UNROLL = 4


def _peers():
    x, y, c = _mesh_pos()
    others = []
    for k in range(1, N_DEV):
        px = 1 - x if k & 4 else x
        py = 1 - y if k & 2 else y
        pc = 1 - c if k & 1 else c
        others.append(((px, py, pc), 4 * px + 2 * py + pc))
    return 4 * x + 2 * y + c, others


def _exchange(src_ref, dst_ref, send_sems, recv_sems, local_sem, *, finish, gather=False):
    me, others = _peers()

    def send(k, dev, slot):
        return pltpu.make_async_remote_copy(
            src_ref=src_ref if gather else src_ref.at[slot], dst_ref=dst_ref.at[me], send_sem=send_sems.at[k],
            recv_sem=recv_sems.at[k], device_id=dev, device_id_type=pl.DeviceIdType.MESH)

    local = pltpu.make_async_copy(src_ref if gather else src_ref.at[me], dst_ref.at[me], local_sem)
    if not finish:
        local.start()
        for k, (dev, slot) in enumerate(others):
            send(k, dev, slot).start()
        return
    for k, (dev, slot) in enumerate(others):
        pltpu.make_async_remote_copy(
            src_ref=dst_ref.at[slot], dst_ref=dst_ref.at[slot], send_sem=send_sems.at[k], recv_sem=recv_sems.at[k],
            device_id=dev, device_id_type=pl.DeviceIdType.MESH).wait_recv()
    for k, (dev, slot) in enumerate(others):
        send(k, dev, slot).wait_send()
    local.wait()


def _gather_two_level(x_ref, out_ref, send_sems, recv_sems, local_sem, *, phase):
    x, y, c = _mesh_pos()
    me, sibling = (x, y, c), (x, y, 1 - c)
    chips = [(1 - x, y), (x, 1 - y), (1 - x, 1 - y)]

    def slot(px, py, pc):
        return out_ref.at[4 * px + 2 * py + pc]

    def copy(k, blk, to, src=None):
        return pltpu.make_async_remote_copy(
            src_ref=slot(*blk) if src is None else src, dst_ref=slot(*blk),
            send_sem=send_sems.at[k], recv_sem=recv_sems.at[k], device_id=to, device_id_type=pl.DeviceIdType.MESH)

    if phase == 0:
        pltpu.make_async_copy(x_ref, slot(*me), local_sem).start()
        copy(0, me, sibling, src=x_ref).start()
        for j, chip in enumerate(chips):
            copy(1 + j, me, (*chip, c), src=x_ref).start()
    elif phase == 1:
        for j, chip in enumerate(chips):
            copy(1 + j, (*chip, c), me).wait_recv()
            copy(4 + j, (*chip, c), sibling).start()
    else:
        copy(0, sibling, me).wait_recv()
        for j, chip in enumerate(chips):
            copy(4 + j, (*chip, 1 - c), me).wait_recv()
        copy(0, me, sibling, src=x_ref).wait_send()
        for j, chip in enumerate(chips):
            copy(1 + j, me, (*chip, c), src=x_ref).wait_send()
            copy(4 + j, (*chip, c), sibling).wait_send()
        pltpu.make_async_copy(x_ref, slot(*me), local_sem).wait()


GATHER_FORWARD_STEP = 9
_EXCHANGE_SEMS = [pltpu.SemaphoreType.DMA((N_DEV - 1,)), pltpu.SemaphoreType.DMA((N_DEV - 1,)), pltpu.SemaphoreType.DMA(())]


def _first_last(steps):
    first = last = None
    for axis, n in enumerate(steps):
        i = pl.program_id(axis)
        first = (i == 0) if first is None else first & (i == 0)
        last = (i == n - 1) if last is None else last & (i == n - 1)
    return first, last


def _lru_fwd(xp, ga, cw, vecs, wrg, wig, block, *, name):
    bsz, seq, _ = xp.shape
    groups = seq // 8

    def body(xp_ref, ga_ref, cw_ref, vec_ref, wrg_ref, wig_ref, blk_ref, xb_ref, hs_ref, y_ref, all_ref,
             xpad, a_s, b_s, send_sems, recv_sems, local_sem):
        first, last = _first_last((RNN_BLOCKS, bsz))

        @pl.when(first)
        def _():
            _gather_two_level(blk_ref, all_ref, send_sems, recv_sems, local_sem, phase=0)

        @pl.when((pl.program_id(0) == GATHER_FORWARD_STEP) & (pl.program_id(1) == 0))
        def _():
            _gather_two_level(blk_ref, all_ref, send_sems, recv_sems, local_sem, phase=1)

        xpad[0:8, :] = jnp.zeros((8, RNN_BW), F32)
        xpad[pl.ds(8, seq), :] = xp_ref[...]
        xb = _conv(xpad, cw_ref, seq) + vec_ref[0:1, :]
        xb_ref[...] = xb
        sp = _softplus(-vec_ref[3:4, :])
        _, i, a, mult, _ = _gates(xb, wrg_ref[...], wig_ref[...], vec_ref[1:2, :], vec_ref[2:3, :], sp)
        a_s[...] = a
        b_s[...] = mult * (i * xb)
        row = lax.broadcasted_iota(jnp.int32, (8, RNN_BW), 0)

        def group(g, h):
            r0 = pl.multiple_of(g * 8, 8)
            av = a_s[pl.ds(r0, 8), :]
            bv = b_s[pl.ds(r0, 8), :]
            for k in (1, 2, 4):
                m = row >= k
                bv = jnp.where(m, av * pltpu.roll(bv, k, axis=0) + bv, bv)
                av = jnp.where(m, av * pltpu.roll(av, k, axis=0), av)
            hs_ref[pl.ds(r0, 8), :] = av * h + bv
            return av[7:8, :] * h + bv[7:8, :]

        def groups_of(i, h):
            for u in range(SCAN_UNROLL):
                h = group(i * SCAN_UNROLL + u, h)
            return h

        lax.fori_loop(0, groups // SCAN_UNROLL, groups_of, jnp.zeros((1, RNN_BW), F32))
        gav = ga_ref[...]
        y_ref[...] = (hs_ref[...] * (gav * _sigmoid(gav))).astype(BF16)

        @pl.when(last)
        def _():
            _gather_two_level(blk_ref, all_ref, send_sems, recv_sems, local_sem, phase=2)

    sq = _seq_spec(seq)
    shape = (bsz, seq, D_RNN)
    return pl.pallas_call(
        body, grid=(RNN_BLOCKS, bsz),
        in_specs=[sq, sq, _chan_spec(8), _chan_spec(8), _GATE_W_SPEC, _GATE_W_SPEC, _ANY],
        out_specs=[sq, sq, sq, _ANY],
        out_shape=[jax.ShapeDtypeStruct(shape, F32), jax.ShapeDtypeStruct(shape, F32), jax.ShapeDtypeStruct(shape, BF16),
                   jax.ShapeDtypeStruct((N_DEV,) + block.shape, block.dtype)],
        scratch_shapes=[pltpu.VMEM((seq + 8, RNN_BW), F32), pltpu.VMEM((seq, RNN_BW), F32), pltpu.VMEM((seq, RNN_BW), F32)]
        + _EXCHANGE_SEMS,
        compiler_params=_params(("arbitrary", "arbitrary")), name=name)(xp, ga, cw, vecs, wrg, wig, block)


def _lru_bwd(dy, xp, xb, hs, ga, cw, vecs, wrg, wig, parts, *, name):
    bsz, seq, _ = xp.shape
    groups = seq // 8

    def body(dy_ref, xp_ref, xb_ref, hs_ref, ga_ref, cw_ref, vec_ref, wrg_ref, wig_ref,
             parts_ref, dxp_ref, dga_ref, dwrg_ref, dwig_ref, dvec_ref, land_ref, pad, a_s, d_s, lam_s,
             send_sems, recv_sems, local_sem):
        first, last = _first_last((RNN_BLOCKS, bsz))

        @pl.when(first)
        def _():
            _exchange(parts_ref, land_ref, send_sems, recv_sems, local_sem, finish=False)

        @pl.when(pl.program_id(1) == 0)
        def _():
            dwrg_ref[...] = jnp.zeros_like(dwrg_ref)
            dwig_ref[...] = jnp.zeros_like(dwig_ref)
            dvec_ref[...] = jnp.zeros_like(dvec_ref)

        xb = xb_ref[...]
        hs = hs_ref[...]
        gav = ga_ref[...]
        dy = dy_ref[...]
        sp = _softplus(-vec_ref[3:4, :])
        wrg = wrg_ref[...]
        wig = wig_ref[...]
        r, i, a, mult, inv_mult = _gates(xb, wrg, wig, vec_ref[1:2, :], vec_ref[2:3, :], sp)
        sg = _sigmoid(gav)
        dga_ref[...] = (dy * hs * (sg * (1.0 + gav * (1.0 - sg)))).astype(BF16)
        d_s[...] = dy * (gav * sg)

        pad[pl.ds(0, seq), :] = a
        pad[pl.ds(seq, 8), :] = jnp.zeros((8, RNN_BW), F32)
        a_s[...] = pad[pl.ds(1, seq), :]
        row = lax.broadcasted_iota(jnp.int32, (8, RNN_BW), 0)

        def group(g, nxt):
            r0 = pl.multiple_of((groups - 1 - g) * 8, 8)
            cv = a_s[pl.ds(r0, 8), :]
            bv = d_s[pl.ds(r0, 8), :]
            for k in (1, 2, 4):
                m = row < 8 - k
                bv = jnp.where(m, cv * pltpu.roll(bv, 8 - k, axis=0) + bv, bv)
                cv = jnp.where(m, cv * pltpu.roll(cv, 8 - k, axis=0), cv)
            lam_s[pl.ds(r0, 8), :] = cv * nxt + bv
            return cv[0:1, :] * nxt + bv[0:1, :]

        def groups_of(i, nxt):
            for u in range(SCAN_UNROLL):
                nxt = group(i * SCAN_UNROLL + u, nxt)
            return nxt

        lax.fori_loop(0, groups // SCAN_UNROLL, groups_of, jnp.zeros((1, RNN_BW), F32))
        dh = lam_s[...]

        pad[0:8, :] = jnp.zeros((8, RNN_BW), F32)
        pad[pl.ds(8, seq), :] = hs
        da = dh * pad[pl.ds(7, seq), :]
        ixb = i * xb
        dixb = dh * mult
        dla = da * a - (dh * ixb) * (a * a) * inv_mult
        drp = (dla * ((-LRU_C) * sp)) * r * (1.0 - r)
        dip = (dixb * xb) * i * (1.0 - i)
        dvec_ref[0:1, :] += jnp.sum(drp, axis=0, keepdims=True)
        dvec_ref[1:2, :] += jnp.sum(dip, axis=0, keepdims=True)
        dvec_ref[2:3, :] += jnp.sum(dla * ((-LRU_C) * r), axis=0, keepdims=True)
        drpb = drp.astype(BF16)
        dipb = dip.astype(BF16)
        xbb = xb.astype(BF16)
        nt = (((1,), (1,)), ((), ()))
        tn = (((0,), (0,)), ((), ()))
        dxb = (dixb * i
               + lax.dot_general(drpb, wrg, nt, preferred_element_type=F32)
               + lax.dot_general(dipb, wig, nt, preferred_element_type=F32))
        dwrg_ref[...] += lax.dot_general(xbb, drpb, tn, preferred_element_type=F32)
        dwig_ref[...] += lax.dot_general(xbb, dipb, tn, preferred_element_type=F32)
        dvec_ref[3:4, :] += jnp.sum(dxb, axis=0, keepdims=True)

        pad[pl.ds(0, seq), :] = dxb
        pad[pl.ds(seq, 8), :] = jnp.zeros((8, RNN_BW), F32)
        dxp = cw_ref[0:1, :] * pad[pl.ds(CONV_WIDTH - 1, seq), :]
        for k in range(1, CONV_WIDTH):
            dxp = dxp + cw_ref[k:k + 1, :] * pad[pl.ds(CONV_WIDTH - 1 - k, seq), :]
        dxp_ref[...] = dxp.astype(BF16)
        pad[0:8, :] = jnp.zeros((8, RNN_BW), F32)
        pad[pl.ds(8, seq), :] = xp_ref[...]
        for k in range(CONV_WIDTH):
            dvec_ref[4 + k:5 + k, :] += jnp.sum(dxb * pad[pl.ds(8 - (CONV_WIDTH - 1) + k, seq), :], axis=0, keepdims=True)

        @pl.when(last)
        def _():
            _exchange(parts_ref, land_ref, send_sems, recv_sems, local_sem, finish=True)

    sq = _seq_spec(seq)
    shape = (bsz, seq, D_RNN)
    gshape = (RNN_BLOCKS, RNN_BW, RNN_BW)
    return pl.pallas_call(
        body, grid=(RNN_BLOCKS, bsz),
        in_specs=[sq, sq, sq, sq, sq, _chan_spec(8), _chan_spec(8), _GATE_W_SPEC, _GATE_W_SPEC, _ANY],
        out_specs=[sq, sq, _GATE_W_SPEC, _GATE_W_SPEC, _chan_spec(8), _ANY],
        out_shape=[jax.ShapeDtypeStruct(shape, BF16), jax.ShapeDtypeStruct(shape, BF16),
                   jax.ShapeDtypeStruct(gshape, F32), jax.ShapeDtypeStruct(gshape, F32),
                   jax.ShapeDtypeStruct((8, D_RNN), F32), jax.ShapeDtypeStruct(parts.shape, parts.dtype)],
        scratch_shapes=[pltpu.VMEM((seq + 8, RNN_BW), F32), pltpu.VMEM((seq, RNN_BW), F32),
                        pltpu.VMEM((seq, RNN_BW), F32), pltpu.VMEM((seq, RNN_BW), F32)] + _EXCHANGE_SEMS,
        compiler_params=_params(("arbitrary", "arbitrary")), name=name)(dy, xp, xb, hs, ga, cw, vecs, wrg, wig, parts)


def _attn_block(seq):
    return min(512, seq)


def _diag_mask(blk):
    return lax.broadcasted_iota(jnp.int32, (blk, blk), 0) <= lax.broadcasted_iota(jnp.int32, (blk, blk), 1)


FWD_HEADS = 4
BWD_HEADS = 2


def _attn_fwd(q, kn, kr, v_t, block, *, bsz, seq, name):
    t = bsz * seq
    blk = _attn_block(seq)
    nq = seq // blk
    hg = FWD_HEADS
    steps = (bsz, N_HEADS // hg, nq)

    def body(q_ref, kn_ref, kr_ref, vt_ref, blk_ref, o_ref, lse_ref, all_ref, acc, send_sems, recv_sems, local_sem):
        first, last = _first_last(steps)

        @pl.when(first)
        def _():
            _exchange(blk_ref, all_ref, send_sems, recv_sems, local_sem, finish=False, gather=True)

        qi = pl.program_id(2)
        acc[...] = jnp.zeros_like(acc)

        def step(j, carry, diagonal):
            k0 = pl.multiple_of(j * blk, blk)
            kr_j = kr_ref[pl.ds(k0, blk), :]
            out = []
            for h in range(hg):
                m_i, l_i = carry[h]
                kv = jnp.concatenate([kn_ref[pl.ds(k0, blk), h * QK_NOPE:(h + 1) * QK_NOPE], kr_j], axis=1)
                qv = q_ref[:, h * HEAD_PAD:(h + 1) * HEAD_PAD]
                s = lax.dot_general(kv, qv, _NT, preferred_element_type=F32) * ATTN_SCALE
                if diagonal:
                    s = jnp.where(_diag_mask(blk), s, -jnp.inf)
                m_new = jnp.maximum(m_i, jnp.max(s, axis=0, keepdims=True))
                p = jnp.exp(s - m_new)
                alpha = jnp.exp(m_i - m_new)
                l_new = alpha * l_i + jnp.sum(p, axis=0, keepdims=True)
                acc[h] = alpha * acc[h] + jnp.dot(vt_ref[h * V_DIM:(h + 1) * V_DIM, pl.ds(k0, blk)], p.astype(BF16),
                                                  preferred_element_type=F32)
                out.append((m_new, l_new))
            return tuple(out)

        init = tuple((jnp.full((1, blk), -jnp.inf, F32), jnp.zeros((1, blk), F32)) for _ in range(hg))
        carry = lax.fori_loop(0, qi, lambda j, c: step(j, c, False), init)
        stats = step(qi, carry, True)
        for h in range(hg):
            m_i, l_i = stats[h]
            o_ref[:, h * V_DIM:(h + 1) * V_DIM] = (acc[h] / l_i).T
            lse_ref[h] = m_i + jnp.log(l_i)

        @pl.when(last)
        def _():
            _exchange(blk_ref, all_ref, send_sems, recv_sems, local_sem, finish=True, gather=True)

    return pl.pallas_call(
        body, grid=steps,
        in_specs=[pl.BlockSpec((blk, hg * HEAD_PAD), lambda b, g, i: (b * nq + i, g)),
                  pl.BlockSpec((seq, hg * QK_NOPE), lambda b, g, i: (b, g)),
                  pl.BlockSpec((seq, LANES), lambda b, g, i: (b, 0)),
                  pl.BlockSpec((hg * V_DIM, seq), lambda b, g, i: (g, b)), _ANY],
        out_specs=[pl.BlockSpec((blk, hg * V_DIM), lambda b, g, i: (b * nq + i, g)),
                   pl.BlockSpec((hg, 1, blk), lambda b, g, i: (g, 0, b * nq + i)), _ANY],
        out_shape=[jax.ShapeDtypeStruct((t, N_HEADS * V_DIM), F32), jax.ShapeDtypeStruct((N_HEADS, 1, t), F32),
                   jax.ShapeDtypeStruct((N_DEV,) + block.shape, block.dtype)],
        scratch_shapes=[pltpu.VMEM((hg, V_DIM, blk), F32)] + _EXCHANGE_SEMS,
        compiler_params=_params(("arbitrary", "arbitrary", "arbitrary")), name=name)(q, kn, kr, v_t, block)


def _attn_bwd(q, kn, kr, kn_t, kr_t, v, o, lse, do, cos, sin, parts, *, bsz, seq, name):
    t = bsz * seq
    blk = _attn_block(seq)
    nq = seq // blk
    hg = BWD_HEADS
    steps = (bsz, N_HEADS // hg)

    def body(q_ref, kn_ref, kr_ref, knt_ref, krt_ref, v_ref, o_ref, lse_ref, do_ref, cos_ref, sin_ref, parts_ref,
             dq_ref, dkn_ref, dkr_ref, dv_ref, land_ref, dqt_acc, dk_acc, dv_acc, send_sems, recv_sems, local_sem):
        first, last = _first_last(steps)

        @pl.when(first)
        def _():
            _exchange(parts_ref, land_ref, send_sems, recv_sems, local_sem, finish=False)

        dqt_acc[...] = jnp.zeros_like(dqt_acc)
        dk_acc[...] = jnp.zeros_like(dk_acc)
        dv_acc[...] = jnp.zeros_like(dv_acc)

        def q_block(i, _):
            q0 = pl.multiple_of(i * blk, blk)
            rows = []
            for h in range(hg):
                dov = do_ref[pl.ds(q0, blk), h * V_DIM:(h + 1) * V_DIM].astype(F32)
                dcol = jnp.sum(dov * o_ref[pl.ds(q0, blk), h * V_DIM:(h + 1) * V_DIM], axis=-1, keepdims=True)
                delta = jnp.broadcast_to(dcol, (blk, LANES)).T[0:1, :]
                rows.append((lse_ref[h, :, pl.ds(q0, blk)], delta))

            def pair(j, diagonal):
                k0 = pl.multiple_of(j * blk, blk)
                kr_j = kr_ref[pl.ds(k0, blk), :]
                krt_j = krt_ref[:, pl.ds(k0, blk)]
                for h in range(hg):
                    lse_i, delta = rows[h]
                    qv = q_ref[pl.ds(q0, blk), h * HEAD_PAD:(h + 1) * HEAD_PAD]
                    dov = do_ref[pl.ds(q0, blk), h * V_DIM:(h + 1) * V_DIM]
                    kv = jnp.concatenate([kn_ref[pl.ds(k0, blk), h * QK_NOPE:(h + 1) * QK_NOPE], kr_j], axis=1)
                    s = lax.dot_general(kv, qv, _NT, preferred_element_type=F32) * ATTN_SCALE
                    p = jnp.exp(s - lse_i)
                    if diagonal:
                        p = jnp.where(_diag_mask(blk), p, 0.0)
                    dv_acc[pl.ds(k0, blk), h * V_DIM:(h + 1) * V_DIM] += jnp.dot(
                        p.astype(BF16), dov, preferred_element_type=F32)
                    dp = lax.dot_general(v_ref[pl.ds(k0, blk), h * V_DIM:(h + 1) * V_DIM], dov, _NT,
                                         preferred_element_type=F32)
                    ds = (p * (dp - delta) * ATTN_SCALE).astype(BF16)
                    dk_acc[pl.ds(k0, blk), h * HEAD_PAD:(h + 1) * HEAD_PAD] += jnp.dot(ds, qv, preferred_element_type=F32)
                    base = h * HEAD_PAD
                    dqt_acc[base:base + QK_NOPE, pl.ds(q0, blk)] += jnp.dot(
                        knt_ref[h * QK_NOPE:(h + 1) * QK_NOPE, pl.ds(k0, blk)], ds, preferred_element_type=F32)
                    dqt_acc[base + QK_NOPE:base + HEAD_PAD, pl.ds(q0, blk)] += jnp.dot(
                        krt_j, ds, preferred_element_type=F32)

            def off_diagonal(j, _):
                pair(j, False)
                return 0

            lax.fori_loop(0, i, off_diagonal, 0)
            pair(i, True)
            return 0

        lax.fori_loop(0, nq, q_block, 0)
        dkr = jnp.zeros((seq, LANES), F32)
        for h in range(hg):
            base = h * HEAD_PAD
            for i in range(nq):
                rows = slice(i * blk, (i + 1) * blk)
                dq = dqt_acc[base:base + HEAD_PAD, rows].T
                dq_ref[rows, base:base + QK_NOPE] = dq[:, :QK_NOPE].astype(BF16)
                dq_ref[rows, base + QK_NOPE:base + HEAD_PAD] = _rope_t(
                    dq[:, QK_NOPE:], cos_ref[rows, :], sin_ref[rows, :]).astype(BF16)
            dkn_ref[:, h * QK_NOPE:(h + 1) * QK_NOPE] = dk_acc[:, base:base + QK_NOPE].astype(BF16)
            dkr = dkr + dk_acc[:, base + QK_NOPE:base + HEAD_PAD]
        dv_ref[...] = dv_acc[...].astype(BF16)

        @pl.when(pl.program_id(1) == 0)
        def _():
            dkr_ref[...] = jnp.zeros_like(dkr_ref)

        dkr_ref[...] += _rope_t(dkr, cos_ref[...], sin_ref[...])

        @pl.when(last)
        def _():
            _exchange(parts_ref, land_ref, send_sems, recv_sems, local_sem, finish=True)

    head = pl.BlockSpec((seq, hg * V_DIM), lambda b, g: (b, g))
    head_t = pl.BlockSpec((hg * V_DIM, seq), lambda b, g: (g, b))
    shared = pl.BlockSpec((seq, LANES), lambda b, g: (b, 0))
    shared_t = pl.BlockSpec((LANES, seq), lambda b, g: (0, b))
    table = pl.BlockSpec((seq, LANES), lambda b, g: (0, 0))
    qspec = pl.BlockSpec((seq, hg * HEAD_PAD), lambda b, g: (b, g))
    return pl.pallas_call(
        body, grid=steps,
        in_specs=[qspec, head, shared, head_t, shared_t, head, head,
                  pl.BlockSpec((hg, 1, seq), lambda b, g: (g, 0, b)), head, table, table, _ANY],
        out_specs=[qspec, head, shared, head, _ANY],
        out_shape=[jax.ShapeDtypeStruct((t, N_HEADS * HEAD_PAD), BF16), jax.ShapeDtypeStruct((t, N_HEADS * QK_NOPE), BF16),
                   jax.ShapeDtypeStruct((t, LANES), F32), jax.ShapeDtypeStruct((t, N_HEADS * V_DIM), BF16),
                   jax.ShapeDtypeStruct(parts.shape, parts.dtype)],
        scratch_shapes=[pltpu.VMEM((hg * HEAD_PAD, seq), F32), pltpu.VMEM((seq, hg * HEAD_PAD), F32),
                        pltpu.VMEM((seq, hg * V_DIM), F32)] + _EXCHANGE_SEMS,
        compiler_params=_params(("arbitrary", "arbitrary")), name=name)(
            q, kn, kr, kn_t, kr_t, v, o, lse, do, cos, sin, parts)


def _head_and_loss(o, g2, x1, target, w_out, g_final, *, name, bt=512):
    t, d = x1.shape
    bt = min(bt, t)
    nt = (((1,), (1,)), ((), ()))

    def body(o_ref, g2_ref, x1_ref, tgt_ref, w_ref, gf_ref, loss_ref, dx2_ref, y2_ref, do_ref, dg2_ref, dgf_ref):
        @pl.when(pl.program_id(0) == 0)
        def _():
            loss_ref[...] = jnp.zeros_like(loss_ref)
            dgf_ref[...] = jnp.zeros_like(dgf_ref)

        ov = o_ref[...]
        gv = g2_ref[...]
        sg = _sigmoid(gv)
        silu = gv * sg
        y2 = (ov * silu).astype(BF16)
        y2_ref[...] = y2
        w = w_ref[...]
        x2 = x1_ref[...] + jnp.dot(y2, w, preferred_element_type=F32)
        r = lax.rsqrt(jnp.mean(x2 * x2, axis=-1, keepdims=True) + EPS)
        nrm = x2 * r
        gf = gf_ref[...]
        err = nrm * gf - tgt_ref[...]
        loss_ref[...] += 0.5 * jnp.sum(jnp.mean(err * err, axis=-1, keepdims=True))
        dyf = err * (1.0 / d)
        dgf_ref[...] += jnp.sum(dyf * nrm, axis=0, keepdims=True)
        dn = dyf * gf
        dx2 = r * (dn - nrm * jnp.mean(dn * nrm, axis=-1, keepdims=True))
        dx2_ref[...] = dx2
        dy2 = lax.dot_general(dx2.astype(BF16), w, nt, preferred_element_type=F32)
        do_ref[...] = (dy2 * silu).astype(BF16)
        dg2_ref[...] = (dy2 * ov * (sg * (1.0 + gv * (1.0 - sg)))).astype(BF16)

    row = pl.BlockSpec((bt, d), lambda i: (i, 0))
    vec = pl.BlockSpec((1, d), lambda i: (0, 0))
    return pl.pallas_call(
        body, grid=(t // bt,),
        in_specs=[row, row, row, row, pl.BlockSpec((d, d), lambda i: (0, 0)), vec],
        out_specs=[pl.BlockSpec((8, LANES), lambda i: (0, 0)), row, row, row, row, vec],
        out_shape=[jax.ShapeDtypeStruct((8, LANES), F32), jax.ShapeDtypeStruct((t, d), F32),
                   jax.ShapeDtypeStruct((t, d), BF16), jax.ShapeDtypeStruct((t, d), BF16),
                   jax.ShapeDtypeStruct((t, d), BF16), jax.ShapeDtypeStruct((1, d), F32)],
        compiler_params=_params(("arbitrary",)), name=name)(o, g2, x1, target, w_out, g_final)


def _sum_parts(parts, *, name, br=GRAD_BLOCK):
    npart, rows, w = parts.shape

    def body(p_ref, o_ref):
        acc = p_ref[0].astype(F32)
        for j in range(1, npart):
            acc = acc + p_ref[j].astype(F32)
        o_ref[...] = acc

    return pl.pallas_call(
        body, grid=(rows // br,), in_specs=[pl.BlockSpec((npart, br, w), lambda i: (0, i, 0))],
        out_specs=pl.BlockSpec((br, w), lambda i: (i, 0)), out_shape=jax.ShapeDtypeStruct((rows, w), F32),
        compiler_params=_params(("parallel",)), name=name)(parts)


def _chip_partial(parts, recv, *, name, br=GRAD_BLOCK):
    _, rows, w = parts.shape
    core = lax.axis_index("c").astype(jnp.int32).reshape(1)

    def body(c_ref, p_ref, r_ref, o_ref):
        o_ref[...] = (p_ref[...] + r_ref[...]).astype(BF16)

    grid_spec = pltpu.PrefetchScalarGridSpec(
        num_scalar_prefetch=1, grid=(4, rows // br),
        in_specs=[pl.BlockSpec((None, br, w), lambda k, i, c_ref: (2 * k + c_ref[0], i, 0)),
                  pl.BlockSpec((None, br, w), lambda k, i, c_ref: (k, i, 0))],
        out_specs=pl.BlockSpec((None, br, w), lambda k, i, c_ref: (k, i, 0)))
    return pl.pallas_call(
        body, grid_spec=grid_spec, out_shape=jax.ShapeDtypeStruct((4, rows, w), BF16),
        compiler_params=_params(("parallel", "parallel")), name=name)(core, parts, recv)


def _as_block(a):
    if a.ndim == 1:
        return a.reshape(1, -1)
    if a.ndim > 2 and a.shape[0] == 1:
        return a.reshape(a.shape[1:])
    return a


def _adamw(g, w, m, v, *, name):
    shape = w.shape
    g, w, m, v = (_as_block(a) for a in (g, w, m, v))

    def body(g_ref, w_ref, m_ref, v_ref, d_ref, nm_ref, nv_ref):
        gv = g_ref[...]
        nm = ADAM_B1 * m_ref[...] + (1.0 - ADAM_B1) * gv
        nv = ADAM_B2 * v_ref[...] + (1.0 - ADAM_B2) * (gv * gv)
        nm_ref[...] = nm
        nv_ref[...] = nv
        m_hat = nm / (1.0 - ADAM_B1 ** ADAM_STEP)
        v_hat = nv / (1.0 - ADAM_B2 ** ADAM_STEP)
        d_ref[...] = (-ADAM_LR) * (m_hat / (jnp.sqrt(v_hat) + ADAM_EPS) + ADAM_WD * w_ref[...])

    whole = pl.BlockSpec(memory_space=pltpu.VMEM)
    outs = pl.pallas_call(
        body, in_specs=[whole] * 4, out_specs=[whole] * 3, out_shape=[jax.ShapeDtypeStruct(w.shape, F32)] * 3,
        compiler_params=_params(), name=name)(g, w, m, v)
    return [o.reshape(shape) for o in outs]


def _all_gather(block, *, name):
    m, n = block.shape

    def body(x_ref, out_ref, send_sems, recv_sems, local_sem):
        for phase in range(3):
            _gather_two_level(x_ref, out_ref, send_sems, recv_sems, local_sem, phase=phase)

    return pl.pallas_call(
        body, out_shape=jax.ShapeDtypeStruct((N_DEV, m, n), block.dtype), in_specs=[_ANY], out_specs=_ANY,
        scratch_shapes=_EXCHANGE_SEMS, name=name)(block)


def _exchange_d2d(parts, *, name):
    _, rows, w = parts.shape

    def body(p_ref, land_ref, send_sems, recv_sems):
        x, y, c = _mesh_pos()
        sends = []
        for k in range(4):
            cp = pltpu.make_async_remote_copy(
                src_ref=p_ref.at[2 * k + (1 - c)], dst_ref=land_ref.at[k], send_sem=send_sems.at[k],
                recv_sem=recv_sems.at[k], device_id=(x, y, 1 - c), device_id_type=pl.DeviceIdType.MESH)
            cp.start()
            sends.append(cp)
        for cp in sends:
            cp.wait_recv()
        for cp in sends:
            cp.wait_send()

    return pl.pallas_call(
        body, out_shape=jax.ShapeDtypeStruct((4, rows, w), parts.dtype), in_specs=[_ANY], out_specs=_ANY,
        scratch_shapes=[pltpu.SemaphoreType.DMA((4,)), pltpu.SemaphoreType.DMA((4,))], name=name)(parts)


def _exchange_ici(parts, *, name):
    def body(p_ref, land_ref, send_sems, recv_sems, local_sem):
        x, y, c = _mesh_pos()
        mine = pltpu.make_async_copy(p_ref.at[2 * x + y], land_ref.at[3], local_sem)
        mine.start()
        sends = []
        for k, (px, py) in enumerate([(1 - x, y), (x, 1 - y), (1 - x, 1 - y)]):
            cp = pltpu.make_async_remote_copy(
                src_ref=p_ref.at[2 * px + py], dst_ref=land_ref.at[k], send_sem=send_sems.at[k],
                recv_sem=recv_sems.at[k], device_id=(px, py, c), device_id_type=pl.DeviceIdType.MESH)
            cp.start()
            sends.append(cp)
        for cp in sends:
            cp.wait_recv()
        for cp in sends:
            cp.wait_send()
        mine.wait()

    return pl.pallas_call(
        body, out_shape=jax.ShapeDtypeStruct(parts.shape, parts.dtype), in_specs=[_ANY], out_specs=_ANY,
        scratch_shapes=[pltpu.SemaphoreType.DMA((3,)), pltpu.SemaphoreType.DMA((3,)), pltpu.SemaphoreType.DMA(())],
        name=name)(parts)


def _rows(a):
    return a.reshape(-1, PACK_W)


def _pad_to(a, n):
    return jnp.pad(a, (0, n - a.shape[0]))


def _weight_blocks(d):
    small = _rows(_pad_to(jnp.concatenate([d[n].reshape(-1) for n, _ in _SMALL]), 16 * PACK_W))
    bits = lax.bitcast_convert_type(small, jnp.uint32)
    halves = [lax.bitcast_convert_type(h.astype(jnp.uint16), WIRE) for h in (bits >> 16, bits & 0xFFFF)]
    block_a = jnp.concatenate([d["w_in_a"][0].T.astype(WIRE)] + halves, axis=0)
    w_uq = jnp.pad(d["w_uq"][0], ((0, 0), (0, 0), (0, HEAD_PAD - QK_NOPE - QK_ROPE)))
    pieces = {"w_out_a": d["w_out_a"], "w_dkv": d["w_dkv"], "w_uk": d["w_uk"], "w_uv": d["w_uv"],
              "w_in_b": d["w_in_b"][0].T, "w_uq": w_uq}
    block_b = jnp.concatenate([_rows(pieces[n]) for n, _ in _PIECES_B]
                              + [jnp.zeros((WIRE_ROWS_B - MATRIX_ROWS_B, PACK_W), F32)], axis=0).astype(WIRE)
    return block_a, block_b, d["w_out_b"][0].astype(WIRE)


def _weights_a(wall):
    w = {}
    lo, hi = _OFF_A["w_in_a"]
    w["w_in_a_t"] = wall[:, lo:hi].reshape(2 * D_RNN, D_MODEL)
    high, low = (lax.bitcast_convert_type(wall[:, r:r + 16], jnp.uint16).astype(jnp.uint32)
                 for r in (MATRIX_ROWS_A, MATRIX_ROWS_A + 16))
    small = lax.bitcast_convert_type((high << 16) | low, F32)[:, :8].reshape(N_DEV, 8 * PACK_W)
    off = dict(zip([n for n, _ in _SMALL], [0, 128, 768, 928, 1088, 1248]))
    w["norm_a"] = small[:, :128].reshape(1, D_MODEL)

    def by_channel(lo, rows):
        a = small[:, lo:lo + rows * (D_RNN // N_DEV)].reshape(N_DEV, rows, -1).transpose(1, 0, 2).reshape(rows, D_RNN)
        return jnp.pad(a, ((0, 8 - rows), (0, 0)))

    w["conv_taps"] = by_channel(off["conv_w"], CONV_WIDTH)
    w["lru_vecs"] = by_channel(off["conv_b"], 4)
    return w


def _weights_b(wall):
    piece = {n: wall[:, lo:hi] for n, (lo, hi) in _OFF_B.items()}
    w = {"w_out_a": piece["w_out_a"].reshape(D_RNN, D_MODEL)}
    w_dkv = piece["w_dkv"].reshape(D_MODEL, KV_RANK + QK_ROPE)
    w["w_dkv_c"] = w_dkv[:, :KV_RANK]
    w["w_dkv_r"] = jnp.pad(w_dkv[:, KV_RANK:], ((0, 0), (0, LANES - QK_ROPE)))
    w["w_uk"] = piece["w_uk"].reshape(KV_RANK, N_HEADS * QK_NOPE)
    w["w_uv"] = piece["w_uv"].reshape(KV_RANK, N_HEADS * V_DIM)
    w["w_uk_t"], w["w_uv_t"] = w["w_uk"].T, w["w_uv"].T
    w["w_in_b_t"] = piece["w_in_b"].reshape(Q_RANK + N_HEADS * V_DIM, D_MODEL)
    w["w_uq"] = piece["w_uq"].reshape(Q_RANK, N_HEADS * HEAD_PAD)
    return w


def _pack_rep(d):
    flat = jnp.concatenate([d[n].reshape(-1) for n, _ in _REP])
    return _rows(_pad_to(flat, REP_ROWS * PACK_W))


def _unpack_rep(p, like):
    flat = p.reshape(-1)
    out, off = {}, 0
    for n, k in _REP:
        out[n] = flat[off:off + k].reshape(like[n].shape)
        off += k
    return out


def _by_owner(a):
    return a.reshape(N_DEV, -1, PACK_W)


def _grad_parts_b(g):
    tail = jnp.zeros((N_DEV, WIRE_ROWS_B - MATRIX_ROWS_B, PACK_W), F32)
    return jnp.concatenate([_by_owner(g[n]) for n, _ in _PIECES_B] + [tail], axis=1).astype(BF16)


def _grad_parts_a(g):
    small = jnp.concatenate([
        g["norm_a"].reshape(N_DEV, -1),
        g["conv_w"].reshape(CONV_WIDTH, N_DEV, -1).transpose(1, 0, 2).reshape(N_DEV, -1),
        g["conv_b"].reshape(N_DEV, -1), g["b_rg"].reshape(N_DEV, -1), g["b_ig"].reshape(N_DEV, -1),
        g["lru_lambda"].reshape(N_DEV, -1)], axis=1)
    small = jnp.pad(small, ((0, 0), (0, 8 * PACK_W - small.shape[1]))).reshape(N_DEV, 8, PACK_W)
    half = N_DEV // 2
    w_in_a = jnp.concatenate([h.reshape(half, -1, PACK_W) for h in g["w_in_a_t"]], axis=0)
    rep = _pack_rep(g).reshape(N_DEV, REP_SLICE, PACK_W)
    tail = jnp.zeros((N_DEV, GRAD_ROWS_A - MATRIX_ROWS_A - 8 - REP_SLICE, PACK_W), F32)
    return jnp.concatenate([w_in_a, small, rep, tail], axis=1)


def _own_grads(sum_a, sum_b, sum_c):
    out = {}
    lo, hi = _OFF_A["w_in_a"]
    out["w_in_a"] = sum_a[lo:hi].T.reshape(1, D_MODEL, 2 * D_RNN // N_DEV)
    small = sum_a[MATRIX_ROWS_A:MATRIX_ROWS_A + 8].reshape(-1)
    shapes = {"norm_a": (1, D_MODEL // N_DEV), "conv_w": (1, CONV_WIDTH, D_RNN // N_DEV), "conv_b": (1, D_RNN // N_DEV),
              "b_rg": (1, D_RNN // N_DEV), "b_ig": (1, D_RNN // N_DEV), "lru_lambda": (1, D_RNN // N_DEV)}
    off = 0
    for n, k in _SMALL:
        out[n] = small[off:off + k].reshape(shapes[n])
        off += k
    piece = {n: sum_b[lo:hi] for n, (lo, hi) in _OFF_B.items()}
    out["w_out_a"] = piece["w_out_a"].reshape(1, D_RNN // N_DEV, D_MODEL)
    out["w_dkv"] = piece["w_dkv"].reshape(D_MODEL // N_DEV, KV_RANK + QK_ROPE)
    out["w_uk"] = piece["w_uk"].reshape(KV_RANK // N_DEV, N_HEADS, QK_NOPE)
    out["w_uv"] = piece["w_uv"].reshape(KV_RANK // N_DEV, N_HEADS, V_DIM)
    out["w_in_b"] = piece["w_in_b"].T.reshape(1, D_MODEL, (Q_RANK + N_HEADS * V_DIM) // N_DEV)
    out["w_uq"] = piece["w_uq"].reshape(1, Q_RANK // N_DEV, N_HEADS, HEAD_PAD)[..., :QK_NOPE + QK_ROPE]
    out["w_out_b"] = sum_c.reshape(1, N_HEADS * V_DIM // N_DEV, D_MODEL)
    return out


def _step(x, target, w, rep, block_b, block_c, *, bsz, seq):
    t = bsz * seq
    cos, sin = _rope_tables(seq)
    g_a = w["norm_a"]
    g_kv = rep["norm_kv"].reshape(1, -1)
    g_kvn = rep["kv_norm"].reshape(1, -1)
    g_b = rep["norm_b"].reshape(1, -1)
    g_q = rep["q_norm"].reshape(1, -1)
    g_f = rep["final_norm"].reshape(1, -1)
    wrg = rep["w_rg"][0].astype(BF16)
    wig = rep["w_ig"][0].astype(BF16)
    cw8, vecs = w["conv_taps"], w["lru_vecs"]

    def seq3(a):
        return a.reshape(bsz, seq, a.shape[-1])

    def flat(a):
        return a.reshape(t, a.shape[-1])

    h0, xp, ga = _lru_proj_fwd(x, g_a, w["w_in_a_t"], name="lru_proj_fwd")
    xb, hs, y, wall_b = _lru_fwd(seq3(xp), seq3(ga), cw8, vecs, wrg, wig, block_b, name="lru_fwd")
    w = dict(w, **_weights_b(wall_b))
    x1 = _matmul(flat(y), w["w_out_a"], residual=x, name="out_a")
    hk, hq, ck, cqp, g2, ckv, cq, q, kn, v, kr, kn_t, v_t, kr_t = _mla_proj_fwd(
        x1, (g_kv, g_b, g_kvn, g_q), w, cos, sin, seq=seq, name="mla_proj_fwd")
    o, lse, wall_c = _attn_fwd(q, kn, kr, v_t, block_c, bsz=bsz, seq=seq, name="attn_fwd")
    w_out_b = wall_c.reshape(N_HEADS * V_DIM, D_MODEL)
    loss, dx2, y2, do, dg2, dgf = _head_and_loss(o, g2, x1, target, w_out_b, g_f, name="head_loss")
    grads = {"final_norm": dgf}
    parts_c = _by_owner(_matmul_tn(y2, dx2, name="d_w_out_b")).astype(BF16)
    dq, dkn, dkr, dv, landed_c = _attn_bwd(q, kn, kr, kn_t, kr_t, v, o, lse, do, cos, sin, parts_c,
                                           bsz=bsz, seq=seq, name="attn_bwd")
    grads["w_uq"] = _matmul_tn(cq, dq, name="d_w_uq")
    dx1, du2, dckr, dgkv, dgb, dgkvn, dgq = _mla_proj_bwd(
        x1, dx2, cqp, ck, dq, dkn, dv, dkr, dg2, (g_kv, g_b, g_kvn, g_q), w, name="mla_proj_bwd")
    grads["norm_kv"], grads["norm_b"], grads["kv_norm"], grads["q_norm"] = dgkv, dgb, dgkvn, dgq
    grads["w_in_b"] = _matmul_tn(du2, hq, name="d_w_in_b_t")
    grads["w_uk"] = _matmul_tn(ckv, dkn, name="d_w_uk")
    grads["w_uv"] = _matmul_tn(ckv, dv, name="d_w_uv")
    grads["w_dkv"] = _matmul_tn(hk, dckr, name="d_w_dkv")[:, :KV_RANK + QK_ROPE]
    grads["w_out_a"] = _matmul_tn(flat(y), dx1, name="d_w_out_a")
    parts_b = _grad_parts_b(grads)
    dy = _matmul(dx1, w["w_out_a"], nt=True, name="d_y")
    dxp, dga, dwrg, dwig, dvec, landed_b = _lru_bwd(
        seq3(dy), seq3(xp), xb, hs, seq3(ga), cw8, vecs, wrg, wig, parts_b, name="lru_bwd")
    dxp, dga = flat(dxp), flat(dga)
    grads["w_rg"], grads["w_ig"] = dwrg, dwig
    grads["b_rg"], grads["b_ig"], grads["conv_b"] = dvec[0], dvec[1], dvec[3]
    lam = vecs[3]
    grads["lru_lambda"] = dvec[2] * (-1.0 / (1.0 + jnp.exp(lam)))
    grads["conv_w"] = dvec[4:4 + CONV_WIDTH]
    grads["w_in_a_t"] = (_matmul_tn(dxp, h0, name="d_w_in_a_x_t"), _matmul_tn(dga, h0, name="d_w_in_a_g_t"))
    dx, dga_norm = _lru_proj_bwd(dxp, dga, x, dx1, g_a, w["w_in_a_t"], name="lru_proj_bwd")
    grads["norm_a"] = dga_norm
    return loss[0, 0], dx, grads, landed_b, landed_c


def kernel(x, norm_a, w_in_a, conv_w, conv_b, w_rg, b_rg, w_ig, b_ig, lru_lambda, w_out_a, norm_kv, w_dkv, kv_norm, w_uk, w_uv, norm_b, w_in_b, q_norm, w_uq, w_out_b, final_norm, loss_target, m_norm_a, m_w_in_a, m_conv_w, m_conv_b, m_w_rg, m_b_rg, m_w_ig, m_b_ig, m_lru_lambda, m_w_out_a, m_norm_kv, m_w_dkv, m_kv_norm, m_w_uk, m_w_uv, m_norm_b, m_w_in_b, m_q_norm, m_w_uq, m_w_out_b, m_final_norm, v_norm_a, v_w_in_a, v_conv_w, v_conv_b, v_w_rg, v_b_rg, v_w_ig, v_b_ig, v_lru_lambda, v_w_out_a, v_norm_kv, v_w_dkv, v_kv_norm, v_w_uk, v_w_uv, v_norm_b, v_w_in_b, v_q_norm, v_w_uq, v_w_out_b, v_final_norm):
    given = dict(locals())
    wts = {n: given[n] for n in WEIGHTS}
    mom1 = {n: given["m_" + n] for n in WEIGHTS}
    mom2 = {n: given["v_" + n] for n in WEIGHTS}
    bsz, seq, _ = x.shape
    t = bsz * seq

    block_a, block_b, block_c = _weight_blocks(wts)
    w = _weights_a(_all_gather(block_a, name="gather_weights_a"))
    loss, dx, grads, landed_b, landed_c = _step(x.reshape(t, D_MODEL), loss_target.reshape(t, D_MODEL), w, wts,
                                                block_b, block_c, bsz=bsz, seq=seq)

    parts_a = _grad_parts_a(grads)
    from_sibling = _exchange_d2d(parts_a, name="exchange_grads_d2d")
    chip_parts = _chip_partial(parts_a, from_sibling, name="chip_partial_grads")
    landed_a = _exchange_ici(chip_parts, name="exchange_grads_ici")
    sum_a = _sum_parts(landed_a, name="sum_grads_a", br=GRAD_BLOCK)
    sum_b = _sum_parts(landed_b, name="sum_grads_b", br=WIRE_ROWS_B // 2)
    sum_c = _sum_parts(landed_c, name="sum_grads_c", br=landed_c.shape[1])
    g_own = _own_grads(sum_a, sum_b, sum_c)
    rep_slice = sum_a[MATRIX_ROWS_A + 8:MATRIX_ROWS_A + 8 + REP_SLICE]
    loss_rows = jnp.pad(loss.reshape(1, 1), ((0, 7), (0, PACK_W - 1)))
    gathered = _all_gather(jnp.concatenate([rep_slice, loss_rows], axis=0), name="gather_replicated")
    g_own.update(_unpack_rep(gathered[:, :REP_SLICE].reshape(REP_ROWS, PACK_W), wts))
    loss = jnp.sum(gathered[:, REP_SLICE, 0])

    deltas, new_m, new_v = {}, {}, {}
    for n in WEIGHTS:
        deltas[n], new_m[n], new_v[n] = _adamw(g_own[n], wts[n], mom1[n], mom2[n], name="adamw_" + n)
    result = [loss, dx.reshape(bsz, seq, D_MODEL)]
    for d in (g_own, deltas, new_m, new_v):
        result.extend(d[n] for n in WEIGHTS)
    return tuple(result)
```

```python
import jax
import jax.numpy as jnp
from jax import lax
from jax.experimental import pallas as pl
from jax.experimental.pallas import tpu as pltpu

F32 = jnp.float32
BF16 = jnp.bfloat16
WIRE = jnp.bfloat16

D_MODEL = 1024
D_RNN = 1280
RNN_BLOCKS = 10
RNN_BW = 128
CONV_WIDTH = 4
LRU_C = 8.0
N_HEADS = 8
QK_NOPE = 128
QK_ROPE = 64
V_DIM = 128
KV_RANK = 256
Q_RANK = 384
ROPE_THETA = 10000.0
EPS = 1e-6
ATTN_SCALE = (QK_NOPE + QK_ROPE) ** -0.5
HEAD_PAD = 256
LANES = 128

ADAM_LR = 0.001
ADAM_B1 = 0.9
ADAM_B2 = 0.999
ADAM_EPS = 1e-08
ADAM_WD = 0.01
ADAM_STEP = 10

N_DEV = 8
VMEM_LIMIT_BYTES = 56 * 2**20
PACK_W = 1024

_PIECES_A = (("w_in_a", 320),)
_PIECES_B = (("w_out_a", 160), ("w_dkv", 40), ("w_uk", 32), ("w_uv", 32), ("w_in_b", 176), ("w_uq", 96))


def _offsets(pieces):
    off, r = {}, 0
    for n, k in pieces:
        off[n] = (r, r + k)
        r += k
    return off, r


_OFF_A, MATRIX_ROWS_A = _offsets(_PIECES_A)
_OFF_B, MATRIX_ROWS_B = _offsets(_PIECES_B)
WIRE_ROWS_A = MATRIX_ROWS_A + 32
WIRE_ROWS_B = 544
_SMALL = (("norm_a", 128), ("conv_w", 640), ("conv_b", 160), ("b_rg", 160), ("b_ig", 160), ("lru_lambda", 160))
_REP = (("w_rg", 163840), ("w_ig", 163840), ("norm_kv", 1024), ("kv_norm", 256), ("norm_b", 1024),
        ("q_norm", 384), ("final_norm", 1024))
REP_ROWS = 384
REP_SLICE = REP_ROWS // N_DEV
GRAD_ROWS_A = 384
GRAD_BLOCK = 192

WEIGHTS = ("norm_a", "w_in_a", "conv_w", "conv_b", "w_rg", "b_rg", "w_ig", "b_ig", "lru_lambda", "w_out_a",
           "norm_kv", "w_dkv", "kv_norm", "w_uk", "w_uv", "norm_b", "w_in_b", "q_norm", "w_uq", "w_out_b",
           "final_norm")


def _params(sem=None):
    return pltpu.CompilerParams(dimension_semantics=sem, vmem_limit_bytes=VMEM_LIMIT_BYTES)


_NT = (((1,), (1,)), ((), ()))
_ANY = pl.BlockSpec(memory_space=pl.ANY)


def _mesh_pos():
    return lax.axis_index("x"), lax.axis_index("y"), lax.axis_index("c")


def _sigmoid(z):
    return 0.5 * jnp.tanh(0.5 * z) + 0.5


def _sigmoid_tail(z):
    return 1.0 / (1.0 + jnp.exp(-z))


def _col_block(n):
    return n if n <= 1408 else n // 2


def _matmul(a, b, *, name, nt=False, out_dtype=F32, residual=None, bm=1024):
    m, k = a.shape
    n = b.shape[0] if nt else b.shape[1]
    bm = min(bm, m)
    bn = _col_block(n)
    dims = (((1,), (1,)), ((), ())) if nt else (((1,), (0,)), ((), ()))
    has_res = residual is not None

    def body(*refs):
        a_ref, b_ref, o_ref = refs[0], refs[1], refs[-1]
        acc = lax.dot_general(a_ref[...].astype(BF16), b_ref[...].astype(BF16), dims, preferred_element_type=F32)
        if has_res:
            acc = acc + refs[2][...]
        o_ref[...] = acc.astype(out_dtype)

    in_specs = [pl.BlockSpec((bm, k), lambda i, j: (i, 0)),
                pl.BlockSpec((bn, k), lambda i, j: (j, 0)) if nt else pl.BlockSpec((k, bn), lambda i, j: (0, j))]
    args = [a, b]
    if has_res:
        in_specs.append(pl.BlockSpec((bm, bn), lambda i, j: (i, j)))
        args.append(residual)
    return pl.pallas_call(
        body, grid=(m // bm, n // bn), in_specs=in_specs, out_specs=pl.BlockSpec((bm, bn), lambda i, j: (i, j)),
        out_shape=jax.ShapeDtypeStruct((m, n), out_dtype), compiler_params=_params(("parallel", "parallel")),
        name=name)(*args)


def _matmul_tn(a, b, *, name, bt=1024):
    t, m = a.shape
    n = b.shape[1]
    bt = min(bt, t)
    bm, bn = _col_block(m), _col_block(n)

    def body(a_ref, b_ref, o_ref):
        @pl.when(pl.program_id(2) == 0)
        def _():
            o_ref[...] = jnp.zeros_like(o_ref)

        o_ref[...] += lax.dot_general(a_ref[...].astype(BF16), b_ref[...].astype(BF16),
                                      (((0,), (0,)), ((), ())), preferred_element_type=F32)

    return pl.pallas_call(
        body, grid=(m // bm, n // bn, t // bt),
        in_specs=[pl.BlockSpec((bt, bm), lambda i, j, s: (s, i)), pl.BlockSpec((bt, bn), lambda i, j, s: (s, j))],
        out_specs=pl.BlockSpec((bm, bn), lambda i, j, s: (i, j)),
        out_shape=jax.ShapeDtypeStruct((m, n), F32),
        compiler_params=_params(("parallel", "parallel", "arbitrary")), name=name)(a, b)


def _swap_halves(v):
    ax = v.ndim - 1
    lane = lax.broadcasted_iota(jnp.int32, v.shape, ax)
    up = pltpu.roll(v, LANES - QK_ROPE // 2, axis=ax)
    down = pltpu.roll(v, QK_ROPE // 2, axis=ax)
    return jnp.where(lane < QK_ROPE // 2, up, jnp.where(lane < QK_ROPE, down, 0.0))


def _rope(v, cos, sin):
    return v * cos + _swap_halves(v) * sin


def _rope_t(d, cos, sin):
    return d * cos + _swap_halves(d * sin)


def _rope_tables(seq):
    pos = jnp.arange(seq, dtype=F32)
    inv = ROPE_THETA ** (-jnp.arange(0, QK_ROPE, 2, dtype=F32) / QK_ROPE)
    ang = pos[:, None] * inv[None, :]
    cos, sin = jnp.cos(ang), jnp.sin(ang)
    zero = jnp.zeros((seq, LANES - QK_ROPE), F32)
    return jnp.concatenate([cos, cos, zero], axis=1), jnp.concatenate([-sin, sin, zero], axis=1)


def _rms(v):
    return v * lax.rsqrt(jnp.mean(v * v, axis=-1, keepdims=True) + EPS)


def _const_spec(a):
    return pl.BlockSpec(a.shape, lambda i: (0,) * a.ndim)


def _lru_proj_fwd(x, g_a, w_in_t, *, name, bt=512):
    t, d = x.shape
    bt = min(bt, t)
    n = w_in_t.shape[0] // 2

    def body(x_ref, g_ref, wt_ref, h_ref, xp_ref, ga_ref):
        h = (_rms(x_ref[...]) * g_ref[...]).astype(BF16)
        h_ref[...] = h
        xp_ref[...] = lax.dot_general(h, wt_ref[0:n, :], _NT, preferred_element_type=F32)
        ga_ref[...] = lax.dot_general(h, wt_ref[n:2 * n, :], _NT, preferred_element_type=F32)

    row = lambda w: pl.BlockSpec((bt, w), lambda i: (i, 0))
    return pl.pallas_call(
        body, grid=(t // bt,), in_specs=[row(d), _const_spec(g_a), _const_spec(w_in_t)],
        out_specs=[row(d), row(n), row(n)],
        out_shape=[jax.ShapeDtypeStruct((t, d), BF16), jax.ShapeDtypeStruct((t, n), F32), jax.ShapeDtypeStruct((t, n), F32)],
        compiler_params=_params(("parallel",)), name=name)(x, g_a, w_in_t)


def _mla_proj_fwd(x1, gains, w, cos, sin, *, seq, name, bt=512):
    t, d = x1.shape
    bt = min(bt, seq)
    per_seq = seq // bt
    g_kv, g_b, g_kvn, g_q = gains
    consts = [g_kv, g_b, g_kvn, g_q, w["w_dkv_c"], w["w_dkv_r"], w["w_in_b_t"], w["w_uk"], w["w_uv"], w["w_uq"]]

    def body(x_ref, cos_ref, sin_ref, gkv_ref, gb_ref, gkvn_ref, gq_ref, wdc_ref, wdr_ref, wbt_ref,
             wuk_ref, wuv_ref, wuq_ref,
             hk_ref, hq_ref, ck_ref, cqp_ref, g2_ref, ckv_ref, cq_ref, q_ref, kn_ref, v_ref, kr_ref, knt_ref, vt_ref, krt_ref):
        nrm = _rms(x_ref[...])
        hk = (nrm * gkv_ref[...]).astype(BF16)
        hq = (nrm * gb_ref[...]).astype(BF16)
        hk_ref[...] = hk
        hq_ref[...] = hq
        ck = jnp.dot(hk, wdc_ref[...], preferred_element_type=F32)
        ck_ref[...] = ck
        cqp = lax.dot_general(hq, wbt_ref[0:Q_RANK, :], _NT, preferred_element_type=F32)
        cqp_ref[...] = cqp
        g2_ref[...] = lax.dot_general(hq, wbt_ref[Q_RANK:, :], _NT, preferred_element_type=F32)
        cosv, sinv = cos_ref[...], sin_ref[...]
        kr = _rope(jnp.dot(hk, wdr_ref[...], preferred_element_type=F32), cosv, sinv)
        kr_ref[...] = kr.astype(BF16)
        krt_ref[...] = kr.T.astype(BF16)
        ckv = (_rms(ck) * gkvn_ref[...]).astype(BF16)
        ckv_ref[...] = ckv
        kn = jnp.dot(ckv, wuk_ref[...], preferred_element_type=F32)
        v = jnp.dot(ckv, wuv_ref[...], preferred_element_type=F32)
        kn_ref[...] = kn.astype(BF16)
        v_ref[...] = v.astype(BF16)
        knt_ref[...] = kn.T.astype(BF16)
        vt_ref[...] = v.T.astype(BF16)
        cq = (_rms(cqp) * gq_ref[...]).astype(BF16)
        cq_ref[...] = cq
        for h in range(N_HEADS):
            qh = jnp.dot(cq, wuq_ref[:, h * HEAD_PAD:(h + 1) * HEAD_PAD], preferred_element_type=F32)
            q_ref[:, h * HEAD_PAD:h * HEAD_PAD + QK_NOPE] = qh[:, :QK_NOPE].astype(BF16)
            q_ref[:, h * HEAD_PAD + QK_NOPE:(h + 1) * HEAD_PAD] = _rope(qh[:, QK_NOPE:], cosv, sinv).astype(BF16)

    row = lambda w_: pl.BlockSpec((bt, w_), lambda i: (i, 0))
    col = lambda h_: pl.BlockSpec((h_, bt), lambda i: (0, i))
    tab = pl.BlockSpec((bt, LANES), lambda i: (i % per_seq, 0))
    nh = N_HEADS * V_DIM
    shapes = [((t, d), BF16), ((t, d), BF16), ((t, KV_RANK), F32), ((t, Q_RANK), F32), ((t, nh), F32), ((t, KV_RANK), BF16),
              ((t, Q_RANK), BF16), ((t, N_HEADS * HEAD_PAD), BF16), ((t, nh), BF16), ((t, nh), BF16), ((t, LANES), BF16),
              ((nh, t), BF16), ((nh, t), BF16), ((LANES, t), BF16)]
    out_specs = [row(d), row(d), row(KV_RANK), row(Q_RANK), row(nh), row(KV_RANK), row(Q_RANK), row(N_HEADS * HEAD_PAD),
                 row(nh), row(nh), row(LANES), col(nh), col(nh), col(LANES)]
    return pl.pallas_call(
        body, grid=(t // bt,), in_specs=[row(d), tab, tab] + [_const_spec(a) for a in consts], out_specs=out_specs,
        out_shape=[jax.ShapeDtypeStruct(s, dt) for s, dt in shapes],
        compiler_params=_params(("parallel",)), name=name)(x1, cos, sin, *consts)


def _rms_bwd_rows(xv, dn):
    r = lax.rsqrt(jnp.mean(xv * xv, axis=-1, keepdims=True) + EPS)
    nrm = xv * r
    return r * (dn - nrm * jnp.mean(dn * nrm, axis=-1, keepdims=True)), nrm


def _col_sum(v):
    return jnp.sum(v, axis=0, keepdims=True)


def _lru_proj_bwd(dxp, dga, x, dx1, g_a, w_in_t, *, name, bt=512):
    t, d = x.shape
    bt = min(bt, t)
    n = w_in_t.shape[0] // 2

    def body(dxp_ref, dga_ref, x_ref, dx1_ref, g_ref, wt_ref, dx_ref, dg_ref):
        @pl.when(pl.program_id(0) == 0)
        def _():
            dg_ref[...] = jnp.zeros_like(dg_ref)

        dh = (jnp.dot(dxp_ref[...], wt_ref[0:n, :], preferred_element_type=F32)
              + jnp.dot(dga_ref[...], wt_ref[n:2 * n, :], preferred_element_type=F32))
        dxn, nrm = _rms_bwd_rows(x_ref[...], dh * g_ref[...])
        dg_ref[...] += _col_sum(dh * nrm)
        dx_ref[...] = dx1_ref[...] + dxn

    row = lambda w: pl.BlockSpec((bt, w), lambda i: (i, 0))
    return pl.pallas_call(
        body, grid=(t // bt,),
        in_specs=[row(n), row(n), row(d), row(d), _const_spec(g_a), _const_spec(w_in_t)],
        out_specs=[row(d), _const_spec(g_a)],
        out_shape=[jax.ShapeDtypeStruct((t, d), F32), jax.ShapeDtypeStruct((1, d), F32)],
        compiler_params=_params(("arbitrary",)), name=name)(dxp, dga, x, dx1, g_a, w_in_t)


def _mla_proj_bwd(x1, dx2, cqp, ck, dq, dkn, dv, dkr, dg2, gains, w, *, name, bt=512):
    t, d = x1.shape
    bt = min(bt, t)
    g_kv, g_b, g_kvn, g_q = gains
    consts = [g_kv, g_b, g_kvn, g_q, w["w_dkv_c"], w["w_dkv_r"], w["w_in_b_t"], w["w_uk"], w["w_uv"], w["w_uq"]]
    nh = N_HEADS * V_DIM

    def body(x1_ref, dx2_ref, cqp_ref, ck_ref, dq_ref, dkn_ref, dv_ref, dkr_ref, dg2_ref,
             gkv_ref, gb_ref, gkvn_ref, gq_ref, wdc_ref, wdr_ref, wbt_ref, wuk_ref, wuv_ref, wuq_ref,
             dx1_ref, du2_ref, dckr_ref, dgkv_ref, dgb_ref, dgkvn_ref, dgq_ref):
        @pl.when(pl.program_id(0) == 0)
        def _():
            for ref in (dgkv_ref, dgb_ref, dgkvn_ref, dgq_ref):
                ref[...] = jnp.zeros_like(ref)

        dot_nt = lambda a, b: lax.dot_general(a, b, _NT, preferred_element_type=F32)
        dcq = dot_nt(dq_ref[...], wuq_ref[...])
        dcqp, nq = _rms_bwd_rows(cqp_ref[...], dcq * gq_ref[...])
        dgq_ref[...] += _col_sum(dcq * nq)
        dcqp = dcqp.astype(BF16)
        dg2 = dg2_ref[...]
        du2_ref[:, :Q_RANK] = dcqp
        du2_ref[:, Q_RANK:] = dg2
        dhq = (jnp.dot(dcqp, wbt_ref[0:Q_RANK, :], preferred_element_type=F32)
               + jnp.dot(dg2, wbt_ref[Q_RANK:, :], preferred_element_type=F32))
        dckv = dot_nt(dkn_ref[...], wuk_ref[...]) + dot_nt(dv_ref[...], wuv_ref[...])
        dck, nc = _rms_bwd_rows(ck_ref[...], dckv * gkvn_ref[...])
        dgkvn_ref[...] += _col_sum(dckv * nc)
        dck = dck.astype(BF16)
        dkr = dkr_ref[...].astype(BF16)
        dckr_ref[:, :KV_RANK] = dck
        dckr_ref[:, KV_RANK:] = dkr
        dhk = dot_nt(dck, wdc_ref[...]) + dot_nt(dkr, wdr_ref[...])
        dxn, n1 = _rms_bwd_rows(x1_ref[...], dhq * gb_ref[...] + dhk * gkv_ref[...])
        dgb_ref[...] += _col_sum(dhq * n1)
        dgkv_ref[...] += _col_sum(dhk * n1)
        dx1_ref[...] = dx2_ref[...] + dxn

    row = lambda w_: pl.BlockSpec((bt, w_), lambda i: (i, 0))
    vec = lambda w_: pl.BlockSpec((1, w_), lambda i: (0, 0))
    in_specs = [row(d), row(d), row(Q_RANK), row(KV_RANK), row(N_HEADS * HEAD_PAD), row(nh), row(nh), row(LANES), row(nh)]
    return pl.pallas_call(
        body, grid=(t // bt,), in_specs=in_specs + [_const_spec(a) for a in consts],
        out_specs=[row(d), row(Q_RANK + nh), row(KV_RANK + LANES), vec(d), vec(d), vec(KV_RANK), vec(Q_RANK)],
        out_shape=[jax.ShapeDtypeStruct((t, d), F32), jax.ShapeDtypeStruct((t, Q_RANK + nh), BF16),
                   jax.ShapeDtypeStruct((t, KV_RANK + LANES), BF16), jax.ShapeDtypeStruct((1, d), F32),
                   jax.ShapeDtypeStruct((1, d), F32), jax.ShapeDtypeStruct((1, KV_RANK), F32),
                   jax.ShapeDtypeStruct((1, Q_RANK), F32)],
        compiler_params=_params(("arbitrary",)), name=name)(x1, dx2, cqp, ck, dq, dkn, dv, dkr, dg2, *consts)


def _softplus(z):
    return jnp.maximum(z, 0.0) + jnp.log1p(jnp.exp(-jnp.abs(z)))


def _one_minus_square(a, la):
    return jnp.tanh(-la) * (1.0 + a * a)


def _gates(xb, wrg, wig, brg, big, sp):
    xbb = xb.astype(BF16)
    r = _sigmoid_tail(jnp.dot(xbb, wrg, preferred_element_type=F32) + brg)
    i = _sigmoid(jnp.dot(xbb, wig, preferred_element_type=F32) + big)
    la = (-LRU_C) * r * sp
    a = jnp.exp(la)
    em = _one_minus_square(a, la)
    inv_mult = lax.rsqrt(em)
    mult = jnp.where(em > 0.0, em * inv_mult, 0.0)
    return r, i, a, mult, inv_mult


def _conv(xpad_ref, cw_ref, seq):
    acc = cw_ref[0:1, :] * xpad_ref[pl.ds(8 - (CONV_WIDTH - 1), seq), :]
    for k in range(1, CONV_WIDTH):
        acc = acc + cw_ref[k:k + 1, :] * xpad_ref[pl.ds(8 - (CONV_WIDTH - 1) + k, seq), :]
    return acc


def _seq_spec(seq):
    return pl.BlockSpec((None, seq, RNN_BW), lambda n, b: (b, 0, n))


def _chan_spec(rows):
    return pl.BlockSpec((rows, RNN_BW), lambda n, b: (0, n))


_GATE_W_SPEC = pl.BlockSpec((None, RNN_BW, RNN_BW), lambda n, b: (n, 0, 0))


SCAN_UNROLL = 4


def _peers():
    x, y, c = _mesh_pos()
    others = []
    for k in range(1, N_DEV):
        px = 1 - x if k & 4 else x
        py = 1 - y if k & 2 else y
        pc = 1 - c if k & 1 else c
        others.append(((px, py, pc), 4 * px + 2 * py + pc))
    return 4 * x + 2 * y + c, others


def _exchange(src_ref, dst_ref, send_sems, recv_sems, local_sem, *, finish, gather=False):
    me, others = _peers()

    def send(k, dev, slot):
        return pltpu.make_async_remote_copy(
            src_ref=src_ref if gather else src_ref.at[slot], dst_ref=dst_ref.at[me], send_sem=send_sems.at[k],
            recv_sem=recv_sems.at[k], device_id=dev, device_id_type=pl.DeviceIdType.MESH)

    local = pltpu.make_async_copy(src_ref if gather else src_ref.at[me], dst_ref.at[me], local_sem)
    if not finish:
        local.start()
        for k, (dev, slot) in enumerate(others):
            send(k, dev, slot).start()
        return
    for k, (dev, slot) in enumerate(others):
        pltpu.make_async_remote_copy(
            src_ref=dst_ref.at[slot], dst_ref=dst_ref.at[slot], send_sem=send_sems.at[k], recv_sem=recv_sems.at[k],
            device_id=dev, device_id_type=pl.DeviceIdType.MESH).wait_recv()
    for k, (dev, slot) in enumerate(others):
        send(k, dev, slot).wait_send()
    local.wait()


def _gather_two_level(x_ref, out_ref, send_sems, recv_sems, local_sem, *, phase):
    x, y, c = _mesh_pos()
    me, sibling = (x, y, c), (x, y, 1 - c)
    chips = [(1 - x, y), (x, 1 - y), (1 - x, 1 - y)]

    def slot(px, py, pc):
        return out_ref.at[4 * px + 2 * py + pc]

    def copy(k, blk, to, src=None):
        return pltpu.make_async_remote_copy(
            src_ref=slot(*blk) if src is None else src, dst_ref=slot(*blk),
            send_sem=send_sems.at[k], recv_sem=recv_sems.at[k], device_id=to, device_id_type=pl.DeviceIdType.MESH)

    if phase == 0:
        pltpu.make_async_copy(x_ref, slot(*me), local_sem).start()
        copy(0, me, sibling, src=x_ref).start()
        for j, chip in enumerate(chips):
            copy(1 + j, me, (*chip, c), src=x_ref).start()
    elif phase == 1:
        for j, chip in enumerate(chips):
            copy(1 + j, (*chip, c), me).wait_recv()
            copy(4 + j, (*chip, c), sibling).start()
    else:
        copy(0, sibling, me).wait_recv()
        for j, chip in enumerate(chips):
            copy(4 + j, (*chip, 1 - c), me).wait_recv()
        copy(0, me, sibling, src=x_ref).wait_send()
        for j, chip in enumerate(chips):
            copy(1 + j, me, (*chip, c), src=x_ref).wait_send()
            copy(4 + j, (*chip, c), sibling).wait_send()
        pltpu.make_async_copy(x_ref, slot(*me), local_sem).wait()


GATHER_FORWARD_STEP = 9
_EXCHANGE_SEMS = [pltpu.SemaphoreType.DMA((N_DEV - 1,)), pltpu.SemaphoreType.DMA((N_DEV - 1,)), pltpu.SemaphoreType.DMA(())]


def _first_last(steps):
    first = last = None
    for axis, n in enumerate(steps):
        i = pl.program_id(axis)
        first = (i == 0) if first is None else first & (i == 0)
        last = (i == n - 1) if last is None else last & (i == n - 1)
    return first, last


def _lru_fwd(xp, ga, cw, vecs, wrg, wig, block, *, name):
    bsz, seq, _ = xp.shape
    groups = seq // 8

    def body(xp_ref, ga_ref, cw_ref, vec_ref, wrg_ref, wig_ref, blk_ref, xb_ref, hs_ref, y_ref, all_ref,
             xpad, a_s, b_s, send_sems, recv_sems, local_sem):
        first, last = _first_last((RNN_BLOCKS, bsz))

        @pl.when(first)
        def _():
            _gather_two_level(blk_ref, all_ref, send_sems, recv_sems, local_sem, phase=0)

        @pl.when((pl.program_id(0) == GATHER_FORWARD_STEP) & (pl.program_id(1) == 0))
        def _():
            _gather_two_level(blk_ref, all_ref, send_sems, recv_sems, local_sem, phase=1)

        xpad[0:8, :] = jnp.zeros((8, RNN_BW), F32)
        xpad[pl.ds(8, seq), :] = xp_ref[...]
        xb = _conv(xpad, cw_ref, seq) + vec_ref[0:1, :]
        xb_ref[...] = xb
        sp = _softplus(-vec_ref[3:4, :])
        _, i, a, mult, _ = _gates(xb, wrg_ref[...], wig_ref[...], vec_ref[1:2, :], vec_ref[2:3, :], sp)
        a_s[...] = a
        b_s[...] = mult * (i * xb)
        row = lax.broadcasted_iota(jnp.int32, (8, RNN_BW), 0)

        def group(g, h):
            r0 = pl.multiple_of(g * 8, 8)
            av = a_s[pl.ds(r0, 8), :]
            bv = b_s[pl.ds(r0, 8), :]
            for k in (1, 2, 4):
                m = row >= k
                bv = jnp.where(m, av * pltpu.roll(bv, k, axis=0) + bv, bv)
                av = jnp.where(m, av * pltpu.roll(av, k, axis=0), av)
            hs_ref[pl.ds(r0, 8), :] = av * h + bv
            return av[7:8, :] * h + bv[7:8, :]

        def groups_of(i, h):
            for u in range(SCAN_UNROLL):
                h = group(i * SCAN_UNROLL + u, h)
            return h

        lax.fori_loop(0, groups // SCAN_UNROLL, groups_of, jnp.zeros((1, RNN_BW), F32))
        gav = ga_ref[...]
        y_ref[...] = (hs_ref[...] * (gav * _sigmoid(gav))).astype(BF16)

        @pl.when(last)
        def _():
            _gather_two_level(blk_ref, all_ref, send_sems, recv_sems, local_sem, phase=2)

    sq = _seq_spec(seq)
    shape = (bsz, seq, D_RNN)
    return pl.pallas_call(
        body, grid=(RNN_BLOCKS, bsz),
        in_specs=[sq, sq, _chan_spec(8), _chan_spec(8), _GATE_W_SPEC, _GATE_W_SPEC, _ANY],
        out_specs=[sq, sq, sq, _ANY],
        out_shape=[jax.ShapeDtypeStruct(shape, F32), jax.ShapeDtypeStruct(shape, F32), jax.ShapeDtypeStruct(shape, BF16),
                   jax.ShapeDtypeStruct((N_DEV,) + block.shape, block.dtype)],
        scratch_shapes=[pltpu.VMEM((seq + 8, RNN_BW), F32), pltpu.VMEM((seq, RNN_BW), F32), pltpu.VMEM((seq, RNN_BW), F32)]
        + _EXCHANGE_SEMS,
        compiler_params=_params(("arbitrary", "arbitrary")), name=name)(xp, ga, cw, vecs, wrg, wig, block)


def _lru_bwd(dy, xp, xb, hs, ga, cw, vecs, wrg, wig, parts, *, name):
    bsz, seq, _ = xp.shape
    groups = seq // 8

    def body(dy_ref, xp_ref, xb_ref, hs_ref, ga_ref, cw_ref, vec_ref, wrg_ref, wig_ref,
             parts_ref, dxp_ref, dga_ref, dwrg_ref, dwig_ref, dvec_ref, land_ref, pad, a_s, d_s, lam_s,
             send_sems, recv_sems, local_sem):
        first, last = _first_last((RNN_BLOCKS, bsz))

        @pl.when(first)
        def _():
            _exchange(parts_ref, land_ref, send_sems, recv_sems, local_sem, finish=False)

        @pl.when(pl.program_id(1) == 0)
        def _():
            dwrg_ref[...] = jnp.zeros_like(dwrg_ref)
            dwig_ref[...] = jnp.zeros_like(dwig_ref)
            dvec_ref[...] = jnp.zeros_like(dvec_ref)

        xb = xb_ref[...]
        hs = hs_ref[...]
        gav = ga_ref[...]
        dy = dy_ref[...]
        sp = _softplus(-vec_ref[3:4, :])
        wrg = wrg_ref[...]
        wig = wig_ref[...]
        r, i, a, mult, inv_mult = _gates(xb, wrg, wig, vec_ref[1:2, :], vec_ref[2:3, :], sp)
        sg = _sigmoid(gav)
        dga_ref[...] = (dy * hs * (sg * (1.0 + gav * (1.0 - sg)))).astype(BF16)
        d_s[...] = dy * (gav * sg)

        pad[pl.ds(0, seq), :] = a
        pad[pl.ds(seq, 8), :] = jnp.zeros((8, RNN_BW), F32)
        a_s[...] = pad[pl.ds(1, seq), :]
        row = lax.broadcasted_iota(jnp.int32, (8, RNN_BW), 0)

        def group(g, nxt):
            r0 = pl.multiple_of((groups - 1 - g) * 8, 8)
            cv = a_s[pl.ds(r0, 8), :]
            bv = d_s[pl.ds(r0, 8), :]
            for k in (1, 2, 4):
                m = row < 8 - k
                bv = jnp.where(m, cv * pltpu.roll(bv, 8 - k, axis=0) + bv, bv)
                cv = jnp.where(m, cv * pltpu.roll(cv, 8 - k, axis=0), cv)
            lam_s[pl.ds(r0, 8), :] = cv * nxt + bv
            return cv[0:1, :] * nxt + bv[0:1, :]

        def groups_of(i, nxt):
            for u in range(SCAN_UNROLL):
                nxt = group(i * SCAN_UNROLL + u, nxt)
            return nxt

        lax.fori_loop(0, groups // SCAN_UNROLL, groups_of, jnp.zeros((1, RNN_BW), F32))
        dh = lam_s[...]

        pad[0:8, :] = jnp.zeros((8, RNN_BW), F32)
        pad[pl.ds(8, seq), :] = hs
        da = dh * pad[pl.ds(7, seq), :]
        ixb = i * xb
        dixb = dh * mult
        dla = da * a - (dh * ixb) * (a * a) * inv_mult
        drp = (dla * ((-LRU_C) * sp)) * r * (1.0 - r)
        dip = (dixb * xb) * i * (1.0 - i)
        dvec_ref[0:1, :] += jnp.sum(drp, axis=0, keepdims=True)
        dvec_ref[1:2, :] += jnp.sum(dip, axis=0, keepdims=True)
        dvec_ref[2:3, :] += jnp.sum(dla * ((-LRU_C) * r), axis=0, keepdims=True)
        drpb = drp.astype(BF16)
        dipb = dip.astype(BF16)
        xbb = xb.astype(BF16)
        nt = (((1,), (1,)), ((), ()))
        tn = (((0,), (0,)), ((), ()))
        dxb = (dixb * i
               + lax.dot_general(drpb, wrg, nt, preferred_element_type=F32)
               + lax.dot_general(dipb, wig, nt, preferred_element_type=F32))
        dwrg_ref[...] += lax.dot_general(xbb, drpb, tn, preferred_element_type=F32)
        dwig_ref[...] += lax.dot_general(xbb, dipb, tn, preferred_element_type=F32)
        dvec_ref[3:4, :] += jnp.sum(dxb, axis=0, keepdims=True)

        pad[pl.ds(0, seq), :] = dxb
        pad[pl.ds(seq, 8), :] = jnp.zeros((8, RNN_BW), F32)
        dxp = cw_ref[0:1, :] * pad[pl.ds(CONV_WIDTH - 1, seq), :]
        for k in range(1, CONV_WIDTH):
            dxp = dxp + cw_ref[k:k + 1, :] * pad[pl.ds(CONV_WIDTH - 1 - k, seq), :]
        dxp_ref[...] = dxp.astype(BF16)
        pad[0:8, :] = jnp.zeros((8, RNN_BW), F32)
        pad[pl.ds(8, seq), :] = xp_ref[...]
        for k in range(CONV_WIDTH):
            dvec_ref[4 + k:5 + k, :] += jnp.sum(dxb * pad[pl.ds(8 - (CONV_WIDTH - 1) + k, seq), :], axis=0, keepdims=True)

        @pl.when(last)
        def _():
            _exchange(parts_ref, land_ref, send_sems, recv_sems, local_sem, finish=True)

    sq = _seq_spec(seq)
    shape = (bsz, seq, D_RNN)
    gshape = (RNN_BLOCKS, RNN_BW, RNN_BW)
    return pl.pallas_call(
        body, grid=(RNN_BLOCKS, bsz),
        in_specs=[sq, sq, sq, sq, sq, _chan_spec(8), _chan_spec(8), _GATE_W_SPEC, _GATE_W_SPEC, _ANY],
        out_specs=[sq, sq, _GATE_W_SPEC, _GATE_W_SPEC, _chan_spec(8), _ANY],
        out_shape=[jax.ShapeDtypeStruct(shape, BF16), jax.ShapeDtypeStruct(shape, BF16),
                   jax.ShapeDtypeStruct(gshape, F32), jax.ShapeDtypeStruct(gshape, F32),
                   jax.ShapeDtypeStruct((8, D_RNN), F32), jax.ShapeDtypeStruct(parts.shape, parts.dtype)],
        scratch_shapes=[pltpu.VMEM((seq + 8, RNN_BW), F32), pltpu.VMEM((seq, RNN_BW), F32),
                        pltpu.VMEM((seq, RNN_BW), F32), pltpu.VMEM((seq, RNN_BW), F32)] + _EXCHANGE_SEMS,
        compiler_params=_params(("arbitrary", "arbitrary")), name=name)(dy, xp, xb, hs, ga, cw, vecs, wrg, wig, parts)


def _attn_block(seq):
    return min(512, seq)


def _diag_mask(blk):
    return lax.broadcasted_iota(jnp.int32, (blk, blk), 0) <= lax.broadcasted_iota(jnp.int32, (blk, blk), 1)


FWD_HEADS = 8
BWD_HEADS = 2


def _attn_fwd(q, kn, kr, v_t, block, *, bsz, seq, name):
    t = bsz * seq
    blk = _attn_block(seq)
    nq = seq // blk
    hg = FWD_HEADS
    steps = (bsz, N_HEADS // hg, nq)

    def body(q_ref, kn_ref, kr_ref, vt_ref, blk_ref, o_ref, lse_ref, all_ref, acc, send_sems, recv_sems, local_sem):
        first, last = _first_last(steps)

        @pl.when(first)
        def _():
            _exchange(blk_ref, all_ref, send_sems, recv_sems, local_sem, finish=False, gather=True)

        qi = pl.program_id(2)
        acc[...] = jnp.zeros_like(acc)

        def step(j, carry, diagonal):
            k0 = pl.multiple_of(j * blk, blk)
            kr_j = kr_ref[pl.ds(k0, blk), :]
            out = []
            for h in range(hg):
                m_i, l_i = carry[h]
                kv = jnp.concatenate([kn_ref[pl.ds(k0, blk), h * QK_NOPE:(h + 1) * QK_NOPE], kr_j], axis=1)
                qv = q_ref[:, h * HEAD_PAD:(h + 1) * HEAD_PAD]
                s = lax.dot_general(kv, qv, _NT, preferred_element_type=F32) * ATTN_SCALE
                if diagonal:
                    s = jnp.where(_diag_mask(blk), s, -jnp.inf)
                m_new = jnp.maximum(m_i, jnp.max(s, axis=0, keepdims=True))
                p = jnp.exp(s - m_new)
                alpha = jnp.exp(m_i - m_new)
                l_new = alpha * l_i + jnp.sum(p, axis=0, keepdims=True)
                acc[h] = alpha * acc[h] + jnp.dot(vt_ref[h * V_DIM:(h + 1) * V_DIM, pl.ds(k0, blk)], p.astype(BF16),
                                                  preferred_element_type=F32)
                out.append((m_new, l_new))
            return tuple(out)

        init = tuple((jnp.full((1, blk), -jnp.inf, F32), jnp.zeros((1, blk), F32)) for _ in range(hg))
        carry = lax.fori_loop(0, qi, lambda j, c: step(j, c, False), init)
        stats = step(qi, carry, True)
        for h in range(hg):
            m_i, l_i = stats[h]
            o_ref[:, h * V_DIM:(h + 1) * V_DIM] = (acc[h] / l_i).T
            lse_ref[h] = m_i + jnp.log(l_i)

        @pl.when(last)
        def _():
            _exchange(blk_ref, all_ref, send_sems, recv_sems, local_sem, finish=True, gather=True)

    return pl.pallas_call(
        body, grid=steps,
        in_specs=[pl.BlockSpec((blk, hg * HEAD_PAD), lambda b, g, i: (b * nq + i, g)),
                  pl.BlockSpec((seq, hg * QK_NOPE), lambda b, g, i: (b, g)),
                  pl.BlockSpec((seq, LANES), lambda b, g, i: (b, 0)),
                  pl.BlockSpec((hg * V_DIM, seq), lambda b, g, i: (g, b)), _ANY],
        out_specs=[pl.BlockSpec((blk, hg * V_DIM), lambda b, g, i: (b * nq + i, g)),
                   pl.BlockSpec((hg, 1, blk), lambda b, g, i: (g, 0, b * nq + i)), _ANY],
        out_shape=[jax.ShapeDtypeStruct((t, N_HEADS * V_DIM), F32), jax.ShapeDtypeStruct((N_HEADS, 1, t), F32),
                   jax.ShapeDtypeStruct((N_DEV,) + block.shape, block.dtype)],
        scratch_shapes=[pltpu.VMEM((hg, V_DIM, blk), F32)] + _EXCHANGE_SEMS,
        compiler_params=_params(("arbitrary", "arbitrary", "arbitrary")), name=name)(q, kn, kr, v_t, block)


def _attn_bwd(q, kn, kr, kn_t, kr_t, v, o, lse, do, cos, sin, parts, *, bsz, seq, name):
    t = bsz * seq
    blk = _attn_block(seq)
    nq = seq // blk
    hg = BWD_HEADS
    steps = (bsz, N_HEADS // hg)

    def body(q_ref, kn_ref, kr_ref, knt_ref, krt_ref, v_ref, o_ref, lse_ref, do_ref, cos_ref, sin_ref, parts_ref,
             dq_ref, dkn_ref, dkr_ref, dv_ref, land_ref, dqt_acc, dk_acc, dv_acc, send_sems, recv_sems, local_sem):
        first, last = _first_last(steps)

        @pl.when(first)
        def _():
            _exchange(parts_ref, land_ref, send_sems, recv_sems, local_sem, finish=False)

        dqt_acc[...] = jnp.zeros_like(dqt_acc)
        dk_acc[...] = jnp.zeros_like(dk_acc)
        dv_acc[...] = jnp.zeros_like(dv_acc)

        def q_block(i, _):
            q0 = pl.multiple_of(i * blk, blk)
            rows = []
            for h in range(hg):
                dov = do_ref[pl.ds(q0, blk), h * V_DIM:(h + 1) * V_DIM].astype(F32)
                dcol = jnp.sum(dov * o_ref[pl.ds(q0, blk), h * V_DIM:(h + 1) * V_DIM], axis=-1, keepdims=True)
                delta = jnp.broadcast_to(dcol, (blk, LANES)).T[0:1, :]
                rows.append((lse_ref[h, :, pl.ds(q0, blk)], delta))

            def pair(j, diagonal):
                k0 = pl.multiple_of(j * blk, blk)
                kr_j = kr_ref[pl.ds(k0, blk), :]
                krt_j = krt_ref[:, pl.ds(k0, blk)]
                for h in range(hg):
                    lse_i, delta = rows[h]
                    qv = q_ref[pl.ds(q0, blk), h * HEAD_PAD:(h + 1) * HEAD_PAD]
                    dov = do_ref[pl.ds(q0, blk), h * V_DIM:(h + 1) * V_DIM]
                    kv = jnp.concatenate([kn_ref[pl.ds(k0, blk), h * QK_NOPE:(h + 1) * QK_NOPE], kr_j], axis=1)
                    s = lax.dot_general(kv, qv, _NT, preferred_element_type=F32) * ATTN_SCALE
                    p = jnp.exp(s - lse_i)
                    if diagonal:
                        p = jnp.where(_diag_mask(blk), p, 0.0)
                    dv_acc[pl.ds(k0, blk), h * V_DIM:(h + 1) * V_DIM] += jnp.dot(
                        p.astype(BF16), dov, preferred_element_type=F32)
                    dp = lax.dot_general(v_ref[pl.ds(k0, blk), h * V_DIM:(h + 1) * V_DIM], dov, _NT,
                                         preferred_element_type=F32)
                    ds = (p * (dp - delta) * ATTN_SCALE).astype(BF16)
                    dk_acc[pl.ds(k0, blk), h * HEAD_PAD:(h + 1) * HEAD_PAD] += jnp.dot(ds, qv, preferred_element_type=F32)
                    base = h * HEAD_PAD
                    dqt_acc[base:base + QK_NOPE, pl.ds(q0, blk)] += jnp.dot(
                        knt_ref[h * QK_NOPE:(h + 1) * QK_NOPE, pl.ds(k0, blk)], ds, preferred_element_type=F32)
                    dqt_acc[base + QK_NOPE:base + HEAD_PAD, pl.ds(q0, blk)] += jnp.dot(
                        krt_j, ds, preferred_element_type=F32)

            def off_diagonal(j, _):
                pair(j, False)
                return 0

            lax.fori_loop(0, i, off_diagonal, 0)
            pair(i, True)
            return 0

        lax.fori_loop(0, nq, q_block, 0)
        dkr = jnp.zeros((seq, LANES), F32)
        for h in range(hg):
            base = h * HEAD_PAD
            for i in range(nq):
                rows = slice(i * blk, (i + 1) * blk)
                dq = dqt_acc[base:base + HEAD_PAD, rows].T
                dq_ref[rows, base:base + QK_NOPE] = dq[:, :QK_NOPE].astype(BF16)
                dq_ref[rows, base + QK_NOPE:base + HEAD_PAD] = _rope_t(
                    dq[:, QK_NOPE:], cos_ref[rows, :], sin_ref[rows, :]).astype(BF16)
            dkn_ref[:, h * QK_NOPE:(h + 1) * QK_NOPE] = dk_acc[:, base:base + QK_NOPE].astype(BF16)
            dkr = dkr + dk_acc[:, base + QK_NOPE:base + HEAD_PAD]
        dv_ref[...] = dv_acc[...].astype(BF16)

        @pl.when(pl.program_id(1) == 0)
        def _():
            dkr_ref[...] = jnp.zeros_like(dkr_ref)

        dkr_ref[...] += _rope_t(dkr, cos_ref[...], sin_ref[...])

        @pl.when(last)
        def _():
            _exchange(parts_ref, land_ref, send_sems, recv_sems, local_sem, finish=True)

    head = pl.BlockSpec((seq, hg * V_DIM), lambda b, g: (b, g))
    head_t = pl.BlockSpec((hg * V_DIM, seq), lambda b, g: (g, b))
    shared = pl.BlockSpec((seq, LANES), lambda b, g: (b, 0))
    shared_t = pl.BlockSpec((LANES, seq), lambda b, g: (0, b))
    table = pl.BlockSpec((seq, LANES), lambda b, g: (0, 0))
    qspec = pl.BlockSpec((seq, hg * HEAD_PAD), lambda b, g: (b, g))
    return pl.pallas_call(
        body, grid=steps,
        in_specs=[qspec, head, shared, head_t, shared_t, head, head,
                  pl.BlockSpec((hg, 1, seq), lambda b, g: (g, 0, b)), head, table, table, _ANY],
        out_specs=[qspec, head, shared, head, _ANY],
        out_shape=[jax.ShapeDtypeStruct((t, N_HEADS * HEAD_PAD), BF16), jax.ShapeDtypeStruct((t, N_HEADS * QK_NOPE), BF16),
                   jax.ShapeDtypeStruct((t, LANES), F32), jax.ShapeDtypeStruct((t, N_HEADS * V_DIM), BF16),
                   jax.ShapeDtypeStruct(parts.shape, parts.dtype)],
        scratch_shapes=[pltpu.VMEM((hg * HEAD_PAD, seq), F32), pltpu.VMEM((seq, hg * HEAD_PAD), F32),
                        pltpu.VMEM((seq, hg * V_DIM), F32)] + _EXCHANGE_SEMS,
        compiler_params=_params(("arbitrary", "arbitrary")), name=name)(
            q, kn, kr, kn_t, kr_t, v, o, lse, do, cos, sin, parts)


def _head_and_loss(o, g2, x1, target, w_out, g_final, *, name, bt=512):
    t, d = x1.shape
    bt = min(bt, t)
    nt = (((1,), (1,)), ((), ()))

    def body(o_ref, g2_ref, x1_ref, tgt_ref, w_ref, gf_ref, loss_ref, dx2_ref, y2_ref, do_ref, dg2_ref, dgf_ref):
        @pl.when(pl.program_id(0) == 0)
        def _():
            loss_ref[...] = jnp.zeros_like(loss_ref)
            dgf_ref[...] = jnp.zeros_like(dgf_ref)

        ov = o_ref[...]
        gv = g2_ref[...]
        sg = _sigmoid(gv)
        silu = gv * sg
        y2 = (ov * silu).astype(BF16)
        y2_ref[...] = y2
        w = w_ref[...]
        x2 = x1_ref[...] + jnp.dot(y2, w, preferred_element_type=F32)
        r = lax.rsqrt(jnp.mean(x2 * x2, axis=-1, keepdims=True) + EPS)
        nrm = x2 * r
        gf = gf_ref[...]
        err = nrm * gf - tgt_ref[...]
        loss_ref[...] += 0.5 * jnp.sum(jnp.mean(err * err, axis=-1, keepdims=True))
        dyf = err * (1.0 / d)
        dgf_ref[...] += jnp.sum(dyf * nrm, axis=0, keepdims=True)
        dn = dyf * gf
        dx2 = r * (dn - nrm * jnp.mean(dn * nrm, axis=-1, keepdims=True))
        dx2_ref[...] = dx2
        dy2 = lax.dot_general(dx2.astype(BF16), w, nt, preferred_element_type=F32)
        do_ref[...] = (dy2 * silu).astype(BF16)
        dg2_ref[...] = (dy2 * ov * (sg * (1.0 + gv * (1.0 - sg)))).astype(BF16)

    row = pl.BlockSpec((bt, d), lambda i: (i, 0))
    vec = pl.BlockSpec((1, d), lambda i: (0, 0))
    return pl.pallas_call(
        body, grid=(t // bt,),
        in_specs=[row, row, row, row, pl.BlockSpec((d, d), lambda i: (0, 0)), vec],
        out_specs=[pl.BlockSpec((8, LANES), lambda i: (0, 0)), row, row, row, row, vec],
        out_shape=[jax.ShapeDtypeStruct((8, LANES), F32), jax.ShapeDtypeStruct((t, d), F32),
                   jax.ShapeDtypeStruct((t, d), BF16), jax.ShapeDtypeStruct((t, d), BF16),
                   jax.ShapeDtypeStruct((t, d), BF16), jax.ShapeDtypeStruct((1, d), F32)],
        compiler_params=_params(("arbitrary",)), name=name)(o, g2, x1, target, w_out, g_final)


def _sum_parts(parts, *, name, br=GRAD_BLOCK):
    npart, rows, w = parts.shape

    def body(p_ref, o_ref):
        acc = p_ref[0].astype(F32)
        for j in range(1, npart):
            acc = acc + p_ref[j].astype(F32)
        o_ref[...] = acc

    return pl.pallas_call(
        body, grid=(rows // br,), in_specs=[pl.BlockSpec((npart, br, w), lambda i: (0, i, 0))],
        out_specs=pl.BlockSpec((br, w), lambda i: (i, 0)), out_shape=jax.ShapeDtypeStruct((rows, w), F32),
        compiler_params=_params(("parallel",)), name=name)(parts)


def _chip_partial(parts, recv, *, name, br=GRAD_BLOCK):
    _, rows, w = parts.shape
    core = lax.axis_index("c").astype(jnp.int32).reshape(1)

    def body(c_ref, p_ref, r_ref, o_ref):
        o_ref[...] = (p_ref[...] + r_ref[...]).astype(BF16)

    grid_spec = pltpu.PrefetchScalarGridSpec(
        num_scalar_prefetch=1, grid=(4, rows // br),
        in_specs=[pl.BlockSpec((None, br, w), lambda k, i, c_ref: (2 * k + c_ref[0], i, 0)),
                  pl.BlockSpec((None, br, w), lambda k, i, c_ref: (k, i, 0))],
        out_specs=pl.BlockSpec((None, br, w), lambda k, i, c_ref: (k, i, 0)))
    return pl.pallas_call(
        body, grid_spec=grid_spec, out_shape=jax.ShapeDtypeStruct((4, rows, w), BF16),
        compiler_params=_params(("parallel", "parallel")), name=name)(core, parts, recv)


def _as_block(a):
    if a.ndim == 1:
        return a.reshape(1, -1)
    if a.ndim > 2 and a.shape[0] == 1:
        return a.reshape(a.shape[1:])
    return a


def _adamw(g, w, m, v, *, name):
    shape = w.shape
    g, w, m, v = (_as_block(a) for a in (g, w, m, v))

    def body(g_ref, w_ref, m_ref, v_ref, d_ref, nm_ref, nv_ref):
        gv = g_ref[...]
        nm = ADAM_B1 * m_ref[...] + (1.0 - ADAM_B1) * gv
        nv = ADAM_B2 * v_ref[...] + (1.0 - ADAM_B2) * (gv * gv)
        nm_ref[...] = nm
        nv_ref[...] = nv
        m_hat = nm / (1.0 - ADAM_B1 ** ADAM_STEP)
        v_hat = nv / (1.0 - ADAM_B2 ** ADAM_STEP)
        d_ref[...] = (-ADAM_LR) * (m_hat / (jnp.sqrt(v_hat) + ADAM_EPS) + ADAM_WD * w_ref[...])

    whole = pl.BlockSpec(memory_space=pltpu.VMEM)
    outs = pl.pallas_call(
        body, in_specs=[whole] * 4, out_specs=[whole] * 3, out_shape=[jax.ShapeDtypeStruct(w.shape, F32)] * 3,
        compiler_params=_params(), name=name)(g, w, m, v)
    return [o.reshape(shape) for o in outs]


def _all_gather(block, *, name):
    m, n = block.shape

    def body(x_ref, out_ref, send_sems, recv_sems, local_sem):
        for phase in range(3):
            _gather_two_level(x_ref, out_ref, send_sems, recv_sems, local_sem, phase=phase)

    return pl.pallas_call(
        body, out_shape=jax.ShapeDtypeStruct((N_DEV, m, n), block.dtype), in_specs=[_ANY], out_specs=_ANY,
        scratch_shapes=_EXCHANGE_SEMS, name=name)(block)


def _exchange_d2d(parts, *, name):
    _, rows, w = parts.shape

    def body(p_ref, land_ref, send_sems, recv_sems):
        x, y, c = _mesh_pos()
        sends = []
        for k in range(4):
            cp = pltpu.make_async_remote_copy(
                src_ref=p_ref.at[2 * k + (1 - c)], dst_ref=land_ref.at[k], send_sem=send_sems.at[k],
                recv_sem=recv_sems.at[k], device_id=(x, y, 1 - c), device_id_type=pl.DeviceIdType.MESH)
            cp.start()
            sends.append(cp)
        for cp in sends:
            cp.wait_recv()
        for cp in sends:
            cp.wait_send()

    return pl.pallas_call(
        body, out_shape=jax.ShapeDtypeStruct((4, rows, w), parts.dtype), in_specs=[_ANY], out_specs=_ANY,
        scratch_shapes=[pltpu.SemaphoreType.DMA((4,)), pltpu.SemaphoreType.DMA((4,))], name=name)(parts)


def _exchange_ici(parts, *, name):
    def body(p_ref, land_ref, send_sems, recv_sems, local_sem):
        x, y, c = _mesh_pos()
        mine = pltpu.make_async_copy(p_ref.at[2 * x + y], land_ref.at[3], local_sem)
        mine.start()
        sends = []
        for k, (px, py) in enumerate([(1 - x, y), (x, 1 - y), (1 - x, 1 - y)]):
            cp = pltpu.make_async_remote_copy(
                src_ref=p_ref.at[2 * px + py], dst_ref=land_ref.at[k], send_sem=send_sems.at[k],
                recv_sem=recv_sems.at[k], device_id=(px, py, c), device_id_type=pl.DeviceIdType.MESH)
            cp.start()
            sends.append(cp)
        for cp in sends:
            cp.wait_recv()
        for cp in sends:
            cp.wait_send()
        mine.wait()

    return pl.pallas_call(
        body, out_shape=jax.ShapeDtypeStruct(parts.shape, parts.dtype), in_specs=[_ANY], out_specs=_ANY,
        scratch_shapes=[pltpu.SemaphoreType.DMA((3,)), pltpu.SemaphoreType.DMA((3,)), pltpu.SemaphoreType.DMA(())],
        name=name)(parts)


def _rows(a):
    return a.reshape(-1, PACK_W)


def _pad_to(a, n):
    return jnp.pad(a, (0, n - a.shape[0]))


def _weight_blocks(d):
    small = _rows(_pad_to(jnp.concatenate([d[n].reshape(-1) for n, _ in _SMALL]), 16 * PACK_W))
    bits = lax.bitcast_convert_type(small, jnp.uint32)
    halves = [lax.bitcast_convert_type(h.astype(jnp.uint16), WIRE) for h in (bits >> 16, bits & 0xFFFF)]
    block_a = jnp.concatenate([d["w_in_a"][0].T.astype(WIRE)] + halves, axis=0)
    w_uq = jnp.pad(d["w_uq"][0], ((0, 0), (0, 0), (0, HEAD_PAD - QK_NOPE - QK_ROPE)))
    pieces = {"w_out_a": d["w_out_a"], "w_dkv": d["w_dkv"], "w_uk": d["w_uk"], "w_uv": d["w_uv"],
              "w_in_b": d["w_in_b"][0].T, "w_uq": w_uq}
    block_b = jnp.concatenate([_rows(pieces[n]) for n, _ in _PIECES_B]
                              + [jnp.zeros((WIRE_ROWS_B - MATRIX_ROWS_B, PACK_W), F32)], axis=0).astype(WIRE)
    return block_a, block_b, d["w_out_b"][0].astype(WIRE)


def _weights_a(wall):
    w = {}
    lo, hi = _OFF_A["w_in_a"]
    w["w_in_a_t"] = wall[:, lo:hi].reshape(2 * D_RNN, D_MODEL)
    high, low = (lax.bitcast_convert_type(wall[:, r:r + 16], jnp.uint16).astype(jnp.uint32)
                 for r in (MATRIX_ROWS_A, MATRIX_ROWS_A + 16))
    small = lax.bitcast_convert_type((high << 16) | low, F32)[:, :8].reshape(N_DEV, 8 * PACK_W)
    off = dict(zip([n for n, _ in _SMALL], [0, 128, 768, 928, 1088, 1248]))
    w["norm_a"] = small[:, :128].reshape(1, D_MODEL)

    def by_channel(lo, rows):
        a = small[:, lo:lo + rows * (D_RNN // N_DEV)].reshape(N_DEV, rows, -1).transpose(1, 0, 2).reshape(rows, D_RNN)
        return jnp.pad(a, ((0, 8 - rows), (0, 0)))

    w["conv_taps"] = by_channel(off["conv_w"], CONV_WIDTH)
    w["lru_vecs"] = by_channel(off["conv_b"], 4)
    return w


def _weights_b(wall):
    piece = {n: wall[:, lo:hi] for n, (lo, hi) in _OFF_B.items()}
    w = {"w_out_a": piece["w_out_a"].reshape(D_RNN, D_MODEL)}
    w_dkv = piece["w_dkv"].reshape(D_MODEL, KV_RANK + QK_ROPE)
    w["w_dkv_c"] = w_dkv[:, :KV_RANK]
    w["w_dkv_r"] = jnp.pad(w_dkv[:, KV_RANK:], ((0, 0), (0, LANES - QK_ROPE)))
    w["w_uk"] = piece["w_uk"].reshape(KV_RANK, N_HEADS * QK_NOPE)
    w["w_uv"] = piece["w_uv"].reshape(KV_RANK, N_HEADS * V_DIM)
    w["w_in_b_t"] = piece["w_in_b"].reshape(Q_RANK + N_HEADS * V_DIM, D_MODEL)
    w["w_uq"] = piece["w_uq"].reshape(Q_RANK, N_HEADS * HEAD_PAD)
    return w


def _pack_rep(d):
    flat = jnp.concatenate([d[n].reshape(-1) for n, _ in _REP])
    return _rows(_pad_to(flat, REP_ROWS * PACK_W))


def _unpack_rep(p, like):
    flat = p.reshape(-1)
    out, off = {}, 0
    for n, k in _REP:
        out[n] = flat[off:off + k].reshape(like[n].shape)
        off += k
    return out


def _by_owner(a):
    return a.reshape(N_DEV, -1, PACK_W)


def _grad_parts_b(g):
    tail = jnp.zeros((N_DEV, WIRE_ROWS_B - MATRIX_ROWS_B, PACK_W), F32)
    return jnp.concatenate([_by_owner(g[n]) for n, _ in _PIECES_B] + [tail], axis=1).astype(BF16)


def _grad_parts_a(g):
    small = jnp.concatenate([
        g["norm_a"].reshape(N_DEV, -1),
        g["conv_w"].reshape(CONV_WIDTH, N_DEV, -1).transpose(1, 0, 2).reshape(N_DEV, -1),
        g["conv_b"].reshape(N_DEV, -1), g["b_rg"].reshape(N_DEV, -1), g["b_ig"].reshape(N_DEV, -1),
        g["lru_lambda"].reshape(N_DEV, -1)], axis=1)
    small = jnp.pad(small, ((0, 0), (0, 8 * PACK_W - small.shape[1]))).reshape(N_DEV, 8, PACK_W)
    half = N_DEV // 2
    w_in_a = jnp.concatenate([h.reshape(half, -1, PACK_W) for h in g["w_in_a_t"]], axis=0)
    rep = _pack_rep(g).reshape(N_DEV, REP_SLICE, PACK_W)
    tail = jnp.zeros((N_DEV, GRAD_ROWS_A - MATRIX_ROWS_A - 8 - REP_SLICE, PACK_W), F32)
    return jnp.concatenate([w_in_a, small, rep, tail], axis=1)


def _own_grads(sum_a, sum_b, sum_c):
    out = {}
    lo, hi = _OFF_A["w_in_a"]
    out["w_in_a"] = sum_a[lo:hi].T.reshape(1, D_MODEL, 2 * D_RNN // N_DEV)
    small = sum_a[MATRIX_ROWS_A:MATRIX_ROWS_A + 8].reshape(-1)
    shapes = {"norm_a": (1, D_MODEL // N_DEV), "conv_w": (1, CONV_WIDTH, D_RNN // N_DEV), "conv_b": (1, D_RNN // N_DEV),
              "b_rg": (1, D_RNN // N_DEV), "b_ig": (1, D_RNN // N_DEV), "lru_lambda": (1, D_RNN // N_DEV)}
    off = 0
    for n, k in _SMALL:
        out[n] = small[off:off + k].reshape(shapes[n])
        off += k
    piece = {n: sum_b[lo:hi] for n, (lo, hi) in _OFF_B.items()}
    out["w_out_a"] = piece["w_out_a"].reshape(1, D_RNN // N_DEV, D_MODEL)
    out["w_dkv"] = piece["w_dkv"].reshape(D_MODEL // N_DEV, KV_RANK + QK_ROPE)
    out["w_uk"] = piece["w_uk"].reshape(KV_RANK // N_DEV, N_HEADS, QK_NOPE)
    out["w_uv"] = piece["w_uv"].reshape(KV_RANK // N_DEV, N_HEADS, V_DIM)
    out["w_in_b"] = piece["w_in_b"].T.reshape(1, D_MODEL, (Q_RANK + N_HEADS * V_DIM) // N_DEV)
    out["w_uq"] = piece["w_uq"].reshape(1, Q_RANK // N_DEV, N_HEADS, HEAD_PAD)[..., :QK_NOPE + QK_ROPE]
    out["w_out_b"] = sum_c.reshape(1, N_HEADS * V_DIM // N_DEV, D_MODEL)
    return out


def _step(x, target, w, rep, block_b, block_c, *, bsz, seq):
    t = bsz * seq
    cos, sin = _rope_tables(seq)
    g_a = w["norm_a"]
    g_kv = rep["norm_kv"].reshape(1, -1)
    g_kvn = rep["kv_norm"].reshape(1, -1)
    g_b = rep["norm_b"].reshape(1, -1)
    g_q = rep["q_norm"].reshape(1, -1)
    g_f = rep["final_norm"].reshape(1, -1)
    wrg = rep["w_rg"][0].astype(BF16)
    wig = rep["w_ig"][0].astype(BF16)
    cw8, vecs = w["conv_taps"], w["lru_vecs"]

    def seq3(a):
        return a.reshape(bsz, seq, a.shape[-1])

    def flat(a):
        return a.reshape(t, a.shape[-1])

    h0, xp, ga = _lru_proj_fwd(x, g_a, w["w_in_a_t"], name="lru_proj_fwd")
    xb, hs, y, wall_b = _lru_fwd(seq3(xp), seq3(ga), cw8, vecs, wrg, wig, block_b, name="lru_fwd")
    w = dict(w, **_weights_b(wall_b))
    x1 = _matmul(flat(y), w["w_out_a"], residual=x, name="out_a")
    hk, hq, ck, cqp, g2, ckv, cq, q, kn, v, kr, kn_t, v_t, kr_t = _mla_proj_fwd(
        x1, (g_kv, g_b, g_kvn, g_q), w, cos, sin, seq=seq, name="mla_proj_fwd")
    o, lse, wall_c = _attn_fwd(q, kn, kr, v_t, block_c, bsz=bsz, seq=seq, name="attn_fwd")
    w_out_b = wall_c.reshape(N_HEADS * V_DIM, D_MODEL)
    loss, dx2, y2, do, dg2, dgf = _head_and_loss(o, g2, x1, target, w_out_b, g_f, name="head_loss")
    grads = {"final_norm": dgf}
    parts_c = _by_owner(_matmul_tn(y2, dx2, name="d_w_out_b")).astype(BF16)
    dq, dkn, dkr, dv, landed_c = _attn_bwd(q, kn, kr, kn_t, kr_t, v, o, lse, do, cos, sin, parts_c,
                                           bsz=bsz, seq=seq, name="attn_bwd")
    grads["w_uq"] = _matmul_tn(cq, dq, name="d_w_uq")
    dx1, du2, dckr, dgkv, dgb, dgkvn, dgq = _mla_proj_bwd(
        x1, dx2, cqp, ck, dq, dkn, dv, dkr, dg2, (g_kv, g_b, g_kvn, g_q), w, name="mla_proj_bwd")
    grads["norm_kv"], grads["norm_b"], grads["kv_norm"], grads["q_norm"] = dgkv, dgb, dgkvn, dgq
    grads["w_in_b"] = _matmul_tn(du2, hq, name="d_w_in_b_t")
    grads["w_uk"] = _matmul_tn(ckv, dkn, name="d_w_uk")
    grads["w_uv"] = _matmul_tn(ckv, dv, name="d_w_uv")
    grads["w_dkv"] = _matmul_tn(hk, dckr, name="d_w_dkv")[:, :KV_RANK + QK_ROPE]
    grads["w_out_a"] = _matmul_tn(flat(y), dx1, name="d_w_out_a")
    parts_b = _grad_parts_b(grads)
    dy = _matmul(dx1, w["w_out_a"], nt=True, name="d_y")
    dxp, dga, dwrg, dwig, dvec, landed_b = _lru_bwd(
        seq3(dy), seq3(xp), xb, hs, seq3(ga), cw8, vecs, wrg, wig, parts_b, name="lru_bwd")
    dxp, dga = flat(dxp), flat(dga)
    grads["w_rg"], grads["w_ig"] = dwrg, dwig
    grads["b_rg"], grads["b_ig"], grads["conv_b"] = dvec[0], dvec[1], dvec[3]
    lam = vecs[3]
    grads["lru_lambda"] = dvec[2] * (-1.0 / (1.0 + jnp.exp(lam)))
    grads["conv_w"] = dvec[4:4 + CONV_WIDTH]
    grads["w_in_a_t"] = (_matmul_tn(dxp, h0, name="d_w_in_a_x_t"), _matmul_tn(dga, h0, name="d_w_in_a_g_t"))
    dx, dga_norm = _lru_proj_bwd(dxp, dga, x, dx1, g_a, w["w_in_a_t"], name="lru_proj_bwd")
    grads["norm_a"] = dga_norm
    return loss[0, 0], dx, grads, landed_b, landed_c


def kernel(x, norm_a, w_in_a, conv_w, conv_b, w_rg, b_rg, w_ig, b_ig, lru_lambda, w_out_a, norm_kv, w_dkv, kv_norm, w_uk, w_uv, norm_b, w_in_b, q_norm, w_uq, w_out_b, final_norm, loss_target, m_norm_a, m_w_in_a, m_conv_w, m_conv_b, m_w_rg, m_b_rg, m_w_ig, m_b_ig, m_lru_lambda, m_w_out_a, m_norm_kv, m_w_dkv, m_kv_norm, m_w_uk, m_w_uv, m_norm_b, m_w_in_b, m_q_norm, m_w_uq, m_w_out_b, m_final_norm, v_norm_a, v_w_in_a, v_conv_w, v_conv_b, v_w_rg, v_b_rg, v_w_ig, v_b_ig, v_lru_lambda, v_w_out_a, v_norm_kv, v_w_dkv, v_kv_norm, v_w_uk, v_w_uv, v_norm_b, v_w_in_b, v_q_norm, v_w_uq, v_w_out_b, v_final_norm):
    given = dict(locals())
    wts = {n: given[n] for n in WEIGHTS}
    mom1 = {n: given["m_" + n] for n in WEIGHTS}
    mom2 = {n: given["v_" + n] for n in WEIGHTS}
    bsz, seq, _ = x.shape
    t = bsz * seq

    block_a, block_b, block_c = _weight_blocks(wts)
    w = _weights_a(_all_gather(block_a, name="gather_weights_a"))
    loss, dx, grads, landed_b, landed_c = _step(x.reshape(t, D_MODEL), loss_target.reshape(t, D_MODEL), w, wts,
                                                block_b, block_c, bsz=bsz, seq=seq)

    parts_a = _grad_parts_a(grads)
    from_sibling = _exchange_d2d(parts_a, name="exchange_grads_d2d")
    chip_parts = _chip_partial(parts_a, from_sibling, name="chip_partial_grads")
    landed_a = _exchange_ici(chip_parts, name="exchange_grads_ici")
    sum_a = _sum_parts(landed_a, name="sum_grads_a", br=GRAD_BLOCK)
    sum_b = _sum_parts(landed_b, name="sum_grads_b", br=WIRE_ROWS_B // 2)
    sum_c = _sum_parts(landed_c, name="sum_grads_c", br=landed_c.shape[1])
    g_own = _own_grads(sum_a, sum_b, sum_c)
    rep_slice = sum_a[MATRIX_ROWS_A + 8:MATRIX_ROWS_A + 8 + REP_SLICE]
    loss_rows = jnp.pad(loss.reshape(1, 1), ((0, 7), (0, PACK_W - 1)))
    gathered = _all_gather(jnp.concatenate([rep_slice, loss_rows], axis=0), name="gather_replicated")
    g_own.update(_unpack_rep(gathered[:, :REP_SLICE].reshape(REP_ROWS, PACK_W), wts))
    loss = jnp.sum(gathered[:, REP_SLICE, 0])

    deltas, new_m, new_v = {}, {}, {}
    for n in WEIGHTS:
        deltas[n], new_m[n], new_v[n] = _adamw(g_own[n], wts[n], mom1[n], mom2[n], name="adamw_" + n)
    result = [loss, dx.reshape(bsz, seq, D_MODEL)]
    for d in (g_own, deltas, new_m, new_v):
        result.extend(d[n] for n in WEIGHTS)
    return tuple(result)
```

```python
import jax
import jax.numpy as jnp
from jax import lax
from jax.experimental import pallas as pl
from jax.experimental.pallas import tpu as pltpu

F32 = jnp.float32
BF16 = jnp.bfloat16
WIRE = jnp.bfloat16

D_MODEL = 1024
D_RNN = 1280
RNN_BLOCKS = 10
RNN_BW = 128
CONV_WIDTH = 4
LRU_C = 8.0
N_HEADS = 8
QK_NOPE = 128
QK_ROPE = 64
V_DIM = 128
KV_RANK = 256
Q_RANK = 384
ROPE_THETA = 10000.0
EPS = 1e-6
ATTN_SCALE = (QK_NOPE + QK_ROPE) ** -0.5
HEAD_PAD = 256
LANES = 128

ADAM_LR = 0.001
ADAM_B1 = 0.9
ADAM_B2 = 0.999
ADAM_EPS = 1e-08
ADAM_WD = 0.01
ADAM_STEP = 10

N_DEV = 8
VMEM_LIMIT_BYTES = 56 * 2**20
PACK_W = 1024

_PIECES_A = (("w_in_a", 320),)
_PIECES_B = (("w_out_a", 160), ("w_dkv", 40), ("w_in_b", 176))
_PIECES_D = (("w_uk", 32), ("w_uv", 32), ("w_uq", 96))


def _offsets(pieces):
    off, r = {}, 0
    for n, k in pieces:
        off[n] = (r, r + k)
        r += k
    return off, r


_OFF_A, MATRIX_ROWS_A = _offsets(_PIECES_A)
_OFF_B, MATRIX_ROWS_B = _offsets(_PIECES_B)
_OFF_D, WIRE_ROWS_D = _offsets(_PIECES_D)
WIRE_ROWS_A = MATRIX_ROWS_A + 32
WIRE_ROWS_B = 384
_SMALL = (("norm_a", 128), ("conv_w", 640), ("conv_b", 160), ("b_rg", 160), ("b_ig", 160), ("lru_lambda", 160))
_REP = (("w_rg", 163840), ("w_ig", 163840), ("norm_kv", 1024), ("kv_norm", 256), ("norm_b", 1024),
        ("q_norm", 384), ("final_norm", 1024))
REP_ROWS = 384
REP_SLICE = REP_ROWS // N_DEV
GRAD_ROWS_A = 384
GRAD_BLOCK = 192

WEIGHTS = ("norm_a", "w_in_a", "conv_w", "conv_b", "w_rg", "b_rg", "w_ig", "b_ig", "lru_lambda", "w_out_a",
           "norm_kv", "w_dkv", "kv_norm", "w_uk", "w_uv", "norm_b", "w_in_b", "q_norm", "w_uq", "w_out_b",
           "final_norm")


def _params(sem=None):
    return pltpu.CompilerParams(dimension_semantics=sem, vmem_limit_bytes=VMEM_LIMIT_BYTES)


_NT = (((1,), (1,)), ((), ()))
_ANY = pl.BlockSpec(memory_space=pl.ANY)


def _mesh_pos():
    return lax.axis_index("x"), lax.axis_index("y"), lax.axis_index("c")


def _sigmoid(z):
    return 0.5 * jnp.tanh(0.5 * z) + 0.5


def _sigmoid_tail(z):
    return 1.0 / (1.0 + jnp.exp(-z))


def _col_block(n):
    return n if n <= 1408 else n // 2


def _matmul(a, b, *, name, nt=False, out_dtype=F32, residual=None, bm=1024):
    m, k = a.shape
    n = b.shape[0] if nt else b.shape[1]
    bm = min(bm, m)
    bn = _col_block(n)
    dims = (((1,), (1,)), ((), ())) if nt else (((1,), (0,)), ((), ()))
    has_res = residual is not None

    def body(*refs):
        a_ref, b_ref, o_ref = refs[0], refs[1], refs[-1]
        acc = lax.dot_general(a_ref[...].astype(BF16), b_ref[...].astype(BF16), dims, preferred_element_type=F32)
        if has_res:
            acc = acc + refs[2][...]
        o_ref[...] = acc.astype(out_dtype)

    in_specs = [pl.BlockSpec((bm, k), lambda i, j: (i, 0)),
                pl.BlockSpec((bn, k), lambda i, j: (j, 0)) if nt else pl.BlockSpec((k, bn), lambda i, j: (0, j))]
    args = [a, b]
    if has_res:
        in_specs.append(pl.BlockSpec((bm, bn), lambda i, j: (i, j)))
        args.append(residual)
    return pl.pallas_call(
        body, grid=(m // bm, n // bn), in_specs=in_specs, out_specs=pl.BlockSpec((bm, bn), lambda i, j: (i, j)),
        out_shape=jax.ShapeDtypeStruct((m, n), out_dtype), compiler_params=_params(("parallel", "parallel")),
        name=name)(*args)


def _matmul_tn(a, b, *, name, bt=1024):
    t, m = a.shape
    n = b.shape[1]
    bt = min(bt, t)
    bm, bn = _col_block(m), _col_block(n)

    def body(a_ref, b_ref, o_ref):
        @pl.when(pl.program_id(2) == 0)
        def _():
            o_ref[...] = jnp.zeros_like(o_ref)

        o_ref[...] += lax.dot_general(a_ref[...].astype(BF16), b_ref[...].astype(BF16),
                                      (((0,), (0,)), ((), ())), preferred_element_type=F32)

    return pl.pallas_call(
        body, grid=(m // bm, n // bn, t // bt),
        in_specs=[pl.BlockSpec((bt, bm), lambda i, j, s: (s, i)), pl.BlockSpec((bt, bn), lambda i, j, s: (s, j))],
        out_specs=pl.BlockSpec((bm, bn), lambda i, j, s: (i, j)),
        out_shape=jax.ShapeDtypeStruct((m, n), F32),
        compiler_params=_params(("parallel", "parallel", "arbitrary")), name=name)(a, b)


def _swap_halves(v):
    ax = v.ndim - 1
    lane = lax.broadcasted_iota(jnp.int32, v.shape, ax)
    up = pltpu.roll(v, LANES - QK_ROPE // 2, axis=ax)
    down = pltpu.roll(v, QK_ROPE // 2, axis=ax)
    return jnp.where(lane < QK_ROPE // 2, up, jnp.where(lane < QK_ROPE, down, 0.0))


def _rope(v, cos, sin):
    return v * cos + _swap_halves(v) * sin


def _rope_t(d, cos, sin):
    return d * cos + _swap_halves(d * sin)


def _rope_tables(seq):
    pos = jnp.arange(seq, dtype=F32)
    inv = ROPE_THETA ** (-jnp.arange(0, QK_ROPE, 2, dtype=F32) / QK_ROPE)
    ang = pos[:, None] * inv[None, :]
    cos, sin = jnp.cos(ang), jnp.sin(ang)
    zero = jnp.zeros((seq, LANES - QK_ROPE), F32)
    return jnp.concatenate([cos, cos, zero], axis=1), jnp.concatenate([-sin, sin, zero], axis=1)


def _rms(v):
    return v * lax.rsqrt(jnp.mean(v * v, axis=-1, keepdims=True) + EPS)


def _const_spec(a):
    return pl.BlockSpec(a.shape, lambda i: (0,) * a.ndim)


def _lru_proj_fwd(x, g_a, w_in_t, block, *, name, bt=512):
    t, d = x.shape
    bt = min(bt, t)
    n = w_in_t.shape[0] // 2
    steps = (t // bt,)

    def body(x_ref, g_ref, wt_ref, blk_ref, h_ref, xp_ref, ga_ref, all_ref, send_sems, recv_sems, local_sem):
        first, last = _first_last(steps)

        @pl.when(first)
        def _():
            _exchange(blk_ref, all_ref, send_sems, recv_sems, local_sem, finish=False, gather=True)

        h = (_rms(x_ref[...]) * g_ref[...]).astype(BF16)
        h_ref[...] = h
        xp_ref[...] = lax.dot_general(h, wt_ref[0:n, :], _NT, preferred_element_type=F32)
        ga_ref[...] = lax.dot_general(h, wt_ref[n:2 * n, :], _NT, preferred_element_type=F32)

        @pl.when(last)
        def _():
            _exchange(blk_ref, all_ref, send_sems, recv_sems, local_sem, finish=True, gather=True)

    row = lambda w: pl.BlockSpec((bt, w), lambda i: (i, 0))
    return pl.pallas_call(
        body, grid=steps, in_specs=[row(d), _const_spec(g_a), _const_spec(w_in_t), _ANY],
        out_specs=[row(d), row(n), row(n), _ANY],
        out_shape=[jax.ShapeDtypeStruct((t, d), BF16), jax.ShapeDtypeStruct((t, n), F32), jax.ShapeDtypeStruct((t, n), F32),
                   jax.ShapeDtypeStruct((N_DEV,) + block.shape, block.dtype)],
        scratch_shapes=_EXCHANGE_SEMS,
        compiler_params=_params(("arbitrary",)), name=name)(x, g_a, w_in_t, block)


def _mla_proj_fwd(x1, gains, w, cos, sin, *, seq, name, bt=512):
    t, d = x1.shape
    bt = min(bt, seq)
    per_seq = seq // bt
    g_kv, g_b, g_kvn, g_q = gains
    consts = [g_kv, g_b, g_kvn, g_q, w["w_dkv_c"], w["w_dkv_r"], w["w_in_b_t"], w["w_uk"], w["w_uv"], w["w_uq"]]

    def body(x_ref, cos_ref, sin_ref, gkv_ref, gb_ref, gkvn_ref, gq_ref, wdc_ref, wdr_ref, wbt_ref,
             wuk_ref, wuv_ref, wuq_ref,
             hk_ref, hq_ref, ck_ref, cqp_ref, g2_ref, ckv_ref, cq_ref, q_ref, kn_ref, v_ref, kr_ref, knt_ref, vt_ref, krt_ref):
        nrm = _rms(x_ref[...])
        hk = (nrm * gkv_ref[...]).astype(BF16)
        hq = (nrm * gb_ref[...]).astype(BF16)
        hk_ref[...] = hk
        hq_ref[...] = hq
        ck = jnp.dot(hk, wdc_ref[...], preferred_element_type=F32)
        ck_ref[...] = ck
        cqp = lax.dot_general(hq, wbt_ref[0:Q_RANK, :], _NT, preferred_element_type=F32)
        cqp_ref[...] = cqp
        g2_ref[...] = lax.dot_general(hq, wbt_ref[Q_RANK:, :], _NT, preferred_element_type=F32)
        cosv, sinv = cos_ref[...], sin_ref[...]
        kr = _rope(jnp.dot(hk, wdr_ref[...], preferred_element_type=F32), cosv, sinv)
        kr_ref[...] = kr.astype(BF16)
        krt_ref[...] = kr.T.astype(BF16)
        ckv = (_rms(ck) * gkvn_ref[...]).astype(BF16)
        ckv_ref[...] = ckv
        kn = jnp.dot(ckv, wuk_ref[...], preferred_element_type=F32)
        v = jnp.dot(ckv, wuv_ref[...], preferred_element_type=F32)
        kn_ref[...] = kn.astype(BF16)
        v_ref[...] = v.astype(BF16)
        knt_ref[...] = kn.T.astype(BF16)
        vt_ref[...] = v.T.astype(BF16)
        cq = (_rms(cqp) * gq_ref[...]).astype(BF16)
        cq_ref[...] = cq
        for h in range(N_HEADS):
            qh = jnp.dot(cq, wuq_ref[:, h * HEAD_PAD:(h + 1) * HEAD_PAD], preferred_element_type=F32)
            q_ref[:, h * HEAD_PAD:h * HEAD_PAD + QK_NOPE] = qh[:, :QK_NOPE].astype(BF16)
            q_ref[:, h * HEAD_PAD + QK_NOPE:(h + 1) * HEAD_PAD] = _rope(qh[:, QK_NOPE:], cosv, sinv).astype(BF16)

    row = lambda w_: pl.BlockSpec((bt, w_), lambda i: (i, 0))
    col = lambda h_: pl.BlockSpec((h_, bt), lambda i: (0, i))
    tab = pl.BlockSpec((bt, LANES), lambda i: (i % per_seq, 0))
    nh = N_HEADS * V_DIM
    shapes = [((t, d), BF16), ((t, d), BF16), ((t, KV_RANK), F32), ((t, Q_RANK), F32), ((t, nh), F32), ((t, KV_RANK), BF16),
              ((t, Q_RANK), BF16), ((t, N_HEADS * HEAD_PAD), BF16), ((t, nh), BF16), ((t, nh), BF16), ((t, LANES), BF16),
              ((nh, t), BF16), ((nh, t), BF16), ((LANES, t), BF16)]
    out_specs = [row(d), row(d), row(KV_RANK), row(Q_RANK), row(nh), row(KV_RANK), row(Q_RANK), row(N_HEADS * HEAD_PAD),
                 row(nh), row(nh), row(LANES), col(nh), col(nh), col(LANES)]
    return pl.pallas_call(
        body, grid=(t // bt,), in_specs=[row(d), tab, tab] + [_const_spec(a) for a in consts], out_specs=out_specs,
        out_shape=[jax.ShapeDtypeStruct(s, dt) for s, dt in shapes],
        compiler_params=_params(("parallel",)), name=name)(x1, cos, sin, *consts)


def _rms_bwd_rows(xv, dn):
    r = lax.rsqrt(jnp.mean(xv * xv, axis=-1, keepdims=True) + EPS)
    nrm = xv * r
    return r * (dn - nrm * jnp.mean(dn * nrm, axis=-1, keepdims=True)), nrm


def _col_sum(v):
    return jnp.sum(v, axis=0, keepdims=True)


def _lru_proj_bwd(dxp, dga, x, dx1, g_a, w_in_t, *, name, bt=512):
    t, d = x.shape
    bt = min(bt, t)
    n = w_in_t.shape[0] // 2

    def body(dxp_ref, dga_ref, x_ref, dx1_ref, g_ref, wt_ref, dx_ref, dg_ref):
        @pl.when(pl.program_id(0) == 0)
        def _():
            dg_ref[...] = jnp.zeros_like(dg_ref)

        dh = (jnp.dot(dxp_ref[...], wt_ref[0:n, :], preferred_element_type=F32)
              + jnp.dot(dga_ref[...], wt_ref[n:2 * n, :], preferred_element_type=F32))
        dxn, nrm = _rms_bwd_rows(x_ref[...], dh * g_ref[...])
        dg_ref[...] += _col_sum(dh * nrm)
        dx_ref[...] = dx1_ref[...] + dxn

    row = lambda w: pl.BlockSpec((bt, w), lambda i: (i, 0))
    return pl.pallas_call(
        body, grid=(t // bt,),
        in_specs=[row(n), row(n), row(d), row(d), _const_spec(g_a), _const_spec(w_in_t)],
        out_specs=[row(d), _const_spec(g_a)],
        out_shape=[jax.ShapeDtypeStruct((t, d), F32), jax.ShapeDtypeStruct((1, d), F32)],
        compiler_params=_params(("arbitrary",)), name=name)(dxp, dga, x, dx1, g_a, w_in_t)


def _mla_proj_bwd(x1, dx2, cqp, ck, dq, dkn, dv, dkr, dg2, gains, w, parts, *, name, bt=512):
    t, d = x1.shape
    bt = min(bt, t)
    g_kv, g_b, g_kvn, g_q = gains
    consts = [g_kv, g_b, g_kvn, g_q, w["w_dkv_c"], w["w_dkv_r"], w["w_in_b_t"], w["w_uk"], w["w_uv"], w["w_uq"]]
    nh = N_HEADS * V_DIM
    steps = (t // bt,)

    def body(x1_ref, dx2_ref, cqp_ref, ck_ref, dq_ref, dkn_ref, dv_ref, dkr_ref, dg2_ref,
             gkv_ref, gb_ref, gkvn_ref, gq_ref, wdc_ref, wdr_ref, wbt_ref, wuk_ref, wuv_ref, wuq_ref, parts_ref,
             dx1_ref, du2_ref, dckr_ref, dgkv_ref, dgb_ref, dgkvn_ref, dgq_ref, land_ref,
             send_sems, recv_sems, local_sem):
        first, last = _first_last(steps)

        @pl.when(first)
        def _():
            for ref in (dgkv_ref, dgb_ref, dgkvn_ref, dgq_ref):
                ref[...] = jnp.zeros_like(ref)
            _exchange(parts_ref, land_ref, send_sems, recv_sems, local_sem, finish=False)

        dot_nt = lambda a, b: lax.dot_general(a, b, _NT, preferred_element_type=F32)
        dcq = dot_nt(dq_ref[...], wuq_ref[...])
        dcqp, nq = _rms_bwd_rows(cqp_ref[...], dcq * gq_ref[...])
        dgq_ref[...] += _col_sum(dcq * nq)
        dcqp = dcqp.astype(BF16)
        dg2 = dg2_ref[...]
        du2_ref[:, :Q_RANK] = dcqp
        du2_ref[:, Q_RANK:] = dg2
        dhq = (jnp.dot(dcqp, wbt_ref[0:Q_RANK, :], preferred_element_type=F32)
               + jnp.dot(dg2, wbt_ref[Q_RANK:, :], preferred_element_type=F32))
        dckv = dot_nt(dkn_ref[...], wuk_ref[...]) + dot_nt(dv_ref[...], wuv_ref[...])
        dck, nc = _rms_bwd_rows(ck_ref[...], dckv * gkvn_ref[...])
        dgkvn_ref[...] += _col_sum(dckv * nc)
        dck = dck.astype(BF16)
        dkr = dkr_ref[...].astype(BF16)
        dckr_ref[:, :KV_RANK] = dck
        dckr_ref[:, KV_RANK:] = dkr
        dhk = dot_nt(dck, wdc_ref[...]) + dot_nt(dkr, wdr_ref[...])
        dxn, n1 = _rms_bwd_rows(x1_ref[...], dhq * gb_ref[...] + dhk * gkv_ref[...])
        dgb_ref[...] += _col_sum(dhq * n1)
        dgkv_ref[...] += _col_sum(dhk * n1)
        dx1_ref[...] = dx2_ref[...] + dxn

        @pl.when(last)
        def _():
            _exchange(parts_ref, land_ref, send_sems, recv_sems, local_sem, finish=True)

    row = lambda w_: pl.BlockSpec((bt, w_), lambda i: (i, 0))
    vec = lambda w_: pl.BlockSpec((1, w_), lambda i: (0, 0))
    in_specs = [row(d), row(d), row(Q_RANK), row(KV_RANK), row(N_HEADS * HEAD_PAD), row(nh), row(nh), row(LANES), row(nh)]
    return pl.pallas_call(
        body, grid=steps, in_specs=in_specs + [_const_spec(a) for a in consts] + [_ANY],
        out_specs=[row(d), row(Q_RANK + nh), row(KV_RANK + LANES), vec(d), vec(d), vec(KV_RANK), vec(Q_RANK), _ANY],
        out_shape=[jax.ShapeDtypeStruct((t, d), F32), jax.ShapeDtypeStruct((t, Q_RANK + nh), BF16),
                   jax.ShapeDtypeStruct((t, KV_RANK + LANES), BF16), jax.ShapeDtypeStruct((1, d), F32),
                   jax.ShapeDtypeStruct((1, d), F32), jax.ShapeDtypeStruct((1, KV_RANK), F32),
                   jax.ShapeDtypeStruct((1, Q_RANK), F32), jax.ShapeDtypeStruct(parts.shape, parts.dtype)],
        scratch_shapes=_EXCHANGE_SEMS,
        compiler_params=_params(("arbitrary",)), name=name)(x1, dx2, cqp, ck, dq, dkn, dv, dkr, dg2, *consts, parts)


def _softplus(z):
    return jnp.maximum(z, 0.0) + jnp.log1p(jnp.exp(-jnp.abs(z)))


def _one_minus_square(a, la):
    return jnp.tanh(-la) * (1.0 + a * a)


def _gates(xb, wrg, wig, brg, big, sp):
    xbb = xb.astype(BF16)
    r = _sigmoid_tail(jnp.dot(xbb, wrg, preferred_element_type=F32) + brg)
    i = _sigmoid(jnp.dot(xbb, wig, preferred_element_type=F32) + big)
    la = (-LRU_C) * r * sp
    a = jnp.exp(la)
    em = _one_minus_square(a, la)
    inv_mult = lax.rsqrt(em)
    mult = jnp.where(em > 0.0, em * inv_mult, 0.0)
    return r, i, a, mult, inv_mult


def _conv(xpad_ref, cw_ref, seq):
    acc = cw_ref[0:1, :] * xpad_ref[pl.ds(8 - (CONV_WIDTH - 1), seq), :]
    for k in range(1, CONV_WIDTH):
        acc = acc + cw_ref[k:k + 1, :] * xpad_ref[pl.ds(8 - (CONV_WIDTH - 1) + k, seq), :]
    return acc


def _seq_spec(seq):
    return pl.BlockSpec((None, seq, RNN_BW), lambda n, b: (b, 0, n))


def _chan_spec(rows):
    return pl.BlockSpec((rows, RNN_BW), lambda n, b: (0, n))


_GATE_W_SPEC = pl.BlockSpec((None, RNN_BW, RNN_BW), lambda n, b: (n, 0, 0))


SCAN_UNROLL = 4


def _peers():
    x, y, c = _mesh_pos()
    others = []
    for k in range(1, N_DEV):
        px = 1 - x if k & 4 else x
        py = 1 - y if k & 2 else y
        pc = 1 - c if k & 1 else c
        others.append(((px, py, pc), 4 * px + 2 * py + pc))
    return 4 * x + 2 * y + c, others


def _exchange(src_ref, dst_ref, send_sems, recv_sems, local_sem, *, finish, gather=False):
    me, others = _peers()

    def send(k, dev, slot):
        return pltpu.make_async_remote_copy(
            src_ref=src_ref if gather else src_ref.at[slot], dst_ref=dst_ref.at[me], send_sem=send_sems.at[k],
            recv_sem=recv_sems.at[k], device_id=dev, device_id_type=pl.DeviceIdType.MESH)

    local = pltpu.make_async_copy(src_ref if gather else src_ref.at[me], dst_ref.at[me], local_sem)
    if not finish:
        local.start()
        for k, (dev, slot) in enumerate(others):
            send(k, dev, slot).start()
        return
    for k, (dev, slot) in enumerate(others):
        pltpu.make_async_remote_copy(
            src_ref=dst_ref.at[slot], dst_ref=dst_ref.at[slot], send_sem=send_sems.at[k], recv_sem=recv_sems.at[k],
            device_id=dev, device_id_type=pl.DeviceIdType.MESH).wait_recv()
    for k, (dev, slot) in enumerate(others):
        send(k, dev, slot).wait_send()
    local.wait()


def _gather_two_level(x_ref, out_ref, send_sems, recv_sems, local_sem, *, phase):
    x, y, c = _mesh_pos()
    me, sibling = (x, y, c), (x, y, 1 - c)
    chips = [(1 - x, y), (x, 1 - y), (1 - x, 1 - y)]

    def slot(px, py, pc):
        return out_ref.at[4 * px + 2 * py + pc]

    def copy(k, blk, to, src=None):
        return pltpu.make_async_remote_copy(
            src_ref=slot(*blk) if src is None else src, dst_ref=slot(*blk),
            send_sem=send_sems.at[k], recv_sem=recv_sems.at[k], device_id=to, device_id_type=pl.DeviceIdType.MESH)

    if phase == 0:
        pltpu.make_async_copy(x_ref, slot(*me), local_sem).start()
        copy(0, me, sibling, src=x_ref).start()
        for j, chip in enumerate(chips):
            copy(1 + j, me, (*chip, c), src=x_ref).start()
    elif phase == 1:
        for j, chip in enumerate(chips):
            copy(1 + j, (*chip, c), me).wait_recv()
            copy(4 + j, (*chip, c), sibling).start()
    else:
        copy(0, sibling, me).wait_recv()
        for j, chip in enumerate(chips):
            copy(4 + j, (*chip, 1 - c), me).wait_recv()
        copy(0, me, sibling, src=x_ref).wait_send()
        for j, chip in enumerate(chips):
            copy(1 + j, me, (*chip, c), src=x_ref).wait_send()
            copy(4 + j, (*chip, c), sibling).wait_send()
        pltpu.make_async_copy(x_ref, slot(*me), local_sem).wait()


GATHER_FORWARD_STEP = 9
_EXCHANGE_SEMS = [pltpu.SemaphoreType.DMA((N_DEV - 1,)), pltpu.SemaphoreType.DMA((N_DEV - 1,)), pltpu.SemaphoreType.DMA(())]


def _first_last(steps):
    first = last = None
    for axis, n in enumerate(steps):
        i = pl.program_id(axis)
        first = (i == 0) if first is None else first & (i == 0)
        last = (i == n - 1) if last is None else last & (i == n - 1)
    return first, last


def _lru_fwd(xp, ga, cw, vecs, wrg, wig, block, *, name):
    bsz, seq, _ = xp.shape
    groups = seq // 8

    def body(xp_ref, ga_ref, cw_ref, vec_ref, wrg_ref, wig_ref, blk_ref, xb_ref, hs_ref, y_ref, all_ref,
             xpad, a_s, b_s, send_sems, recv_sems, local_sem):
        first, last = _first_last((RNN_BLOCKS, bsz))

        @pl.when(first)
        def _():
            _gather_two_level(blk_ref, all_ref, send_sems, recv_sems, local_sem, phase=0)

        @pl.when((pl.program_id(0) == GATHER_FORWARD_STEP) & (pl.program_id(1) == 0))
        def _():
            _gather_two_level(blk_ref, all_ref, send_sems, recv_sems, local_sem, phase=1)

        xpad[0:8, :] = jnp.zeros((8, RNN_BW), F32)
        xpad[pl.ds(8, seq), :] = xp_ref[...]
        xb = _conv(xpad, cw_ref, seq) + vec_ref[0:1, :]
        xb_ref[...] = xb
        sp = _softplus(-vec_ref[3:4, :])
        _, i, a, mult, _ = _gates(xb, wrg_ref[...], wig_ref[...], vec_ref[1:2, :], vec_ref[2:3, :], sp)
        a_s[...] = a
        b_s[...] = mult * (i * xb)
        row = lax.broadcasted_iota(jnp.int32, (8, RNN_BW), 0)

        def group(g, h):
            r0 = pl.multiple_of(g * 8, 8)
            av = a_s[pl.ds(r0, 8), :]
            bv = b_s[pl.ds(r0, 8), :]
            for k in (1, 2, 4):
                m = row >= k
                bv = jnp.where(m, av * pltpu.roll(bv, k, axis=0) + bv, bv)
                av = jnp.where(m, av * pltpu.roll(av, k, axis=0), av)
            hs_ref[pl.ds(r0, 8), :] = av * h + bv
            return av[7:8, :] * h + bv[7:8, :]

        def groups_of(i, h):
            for u in range(SCAN_UNROLL):
                h = group(i * SCAN_UNROLL + u, h)
            return h

        lax.fori_loop(0, groups // SCAN_UNROLL, groups_of, jnp.zeros((1, RNN_BW), F32))
        gav = ga_ref[...]
        y_ref[...] = (hs_ref[...] * (gav * _sigmoid(gav))).astype(BF16)

        @pl.when(last)
        def _():
            _gather_two_level(blk_ref, all_ref, send_sems, recv_sems, local_sem, phase=2)

    sq = _seq_spec(seq)
    shape = (bsz, seq, D_RNN)
    return pl.pallas_call(
        body, grid=(RNN_BLOCKS, bsz),
        in_specs=[sq, sq, _chan_spec(8), _chan_spec(8), _GATE_W_SPEC, _GATE_W_SPEC, _ANY],
        out_specs=[sq, sq, sq, _ANY],
        out_shape=[jax.ShapeDtypeStruct(shape, F32), jax.ShapeDtypeStruct(shape, F32), jax.ShapeDtypeStruct(shape, BF16),
                   jax.ShapeDtypeStruct((N_DEV,) + block.shape, block.dtype)],
        scratch_shapes=[pltpu.VMEM((seq + 8, RNN_BW), F32), pltpu.VMEM((seq, RNN_BW), F32), pltpu.VMEM((seq, RNN_BW), F32)]
        + _EXCHANGE_SEMS,
        compiler_params=_params(("arbitrary", "arbitrary")), name=name)(xp, ga, cw, vecs, wrg, wig, block)


def _lru_bwd(dy, xp, xb, hs, ga, cw, vecs, wrg, wig, parts, *, name):
    bsz, seq, _ = xp.shape
    groups = seq // 8

    def body(dy_ref, xp_ref, xb_ref, hs_ref, ga_ref, cw_ref, vec_ref, wrg_ref, wig_ref,
             parts_ref, dxp_ref, dga_ref, dwrg_ref, dwig_ref, dvec_ref, land_ref, pad, a_s, d_s, lam_s,
             send_sems, recv_sems, local_sem):
        first, last = _first_last((RNN_BLOCKS, bsz))

        @pl.when(first)
        def _():
            _exchange(parts_ref, land_ref, send_sems, recv_sems, local_sem, finish=False)

        @pl.when(pl.program_id(1) == 0)
        def _():
            dwrg_ref[...] = jnp.zeros_like(dwrg_ref)
            dwig_ref[...] = jnp.zeros_like(dwig_ref)
            dvec_ref[...] = jnp.zeros_like(dvec_ref)

        xb = xb_ref[...]
        hs = hs_ref[...]
        gav = ga_ref[...]
        dy = dy_ref[...]
        sp = _softplus(-vec_ref[3:4, :])
        wrg = wrg_ref[...]
        wig = wig_ref[...]
        r, i, a, mult, inv_mult = _gates(xb, wrg, wig, vec_ref[1:2, :], vec_ref[2:3, :], sp)
        sg = _sigmoid(gav)
        dga_ref[...] = (dy * hs * (sg * (1.0 + gav * (1.0 - sg)))).astype(BF16)
        d_s[...] = dy * (gav * sg)

        pad[pl.ds(0, seq), :] = a
        pad[pl.ds(seq, 8), :] = jnp.zeros((8, RNN_BW), F32)
        a_s[...] = pad[pl.ds(1, seq), :]
        row = lax.broadcasted_iota(jnp.int32, (8, RNN_BW), 0)

        def group(g, nxt):
            r0 = pl.multiple_of((groups - 1 - g) * 8, 8)
            cv = a_s[pl.ds(r0, 8), :]
            bv = d_s[pl.ds(r0, 8), :]
            for k in (1, 2, 4):
                m = row < 8 - k
                bv = jnp.where(m, cv * pltpu.roll(bv, 8 - k, axis=0) + bv, bv)
                cv = jnp.where(m, cv * pltpu.roll(cv, 8 - k, axis=0), cv)
            lam_s[pl.ds(r0, 8), :] = cv * nxt + bv
            return cv[0:1, :] * nxt + bv[0:1, :]

        def groups_of(i, nxt):
            for u in range(SCAN_UNROLL):
                nxt = group(i * SCAN_UNROLL + u, nxt)
            return nxt

        lax.fori_loop(0, groups // SCAN_UNROLL, groups_of, jnp.zeros((1, RNN_BW), F32))
        dh = lam_s[...]

        pad[0:8, :] = jnp.zeros((8, RNN_BW), F32)
        pad[pl.ds(8, seq), :] = hs
        da = dh * pad[pl.ds(7, seq), :]
        ixb = i * xb
        dixb = dh * mult
        dla = da * a - (dh * ixb) * (a * a) * inv_mult
        drp = (dla * ((-LRU_C) * sp)) * r * (1.0 - r)
        dip = (dixb * xb) * i * (1.0 - i)
        dvec_ref[0:1, :] += jnp.sum(drp, axis=0, keepdims=True)
        dvec_ref[1:2, :] += jnp.sum(dip, axis=0, keepdims=True)
        dvec_ref[2:3, :] += jnp.sum(dla * ((-LRU_C) * r), axis=0, keepdims=True)
        drpb = drp.astype(BF16)
        dipb = dip.astype(BF16)
        xbb = xb.astype(BF16)
        nt = (((1,), (1,)), ((), ()))
        tn = (((0,), (0,)), ((), ()))
        dxb = (dixb * i
               + lax.dot_general(drpb, wrg, nt, preferred_element_type=F32)
               + lax.dot_general(dipb, wig, nt, preferred_element_type=F32))
        dwrg_ref[...] += lax.dot_general(xbb, drpb, tn, preferred_element_type=F32)
        dwig_ref[...] += lax.dot_general(xbb, dipb, tn, preferred_element_type=F32)
        dvec_ref[3:4, :] += jnp.sum(dxb, axis=0, keepdims=True)

        pad[pl.ds(0, seq), :] = dxb
        pad[pl.ds(seq, 8), :] = jnp.zeros((8, RNN_BW), F32)
        dxp = cw_ref[0:1, :] * pad[pl.ds(CONV_WIDTH - 1, seq), :]
        for k in range(1, CONV_WIDTH):
            dxp = dxp + cw_ref[k:k + 1, :] * pad[pl.ds(CONV_WIDTH - 1 - k, seq), :]
        dxp_ref[...] = dxp.astype(BF16)
        pad[0:8, :] = jnp.zeros((8, RNN_BW), F32)
        pad[pl.ds(8, seq), :] = xp_ref[...]
        for k in range(CONV_WIDTH):
            dvec_ref[4 + k:5 + k, :] += jnp.sum(dxb * pad[pl.ds(8 - (CONV_WIDTH - 1) + k, seq), :], axis=0, keepdims=True)

        @pl.when(last)
        def _():
            _exchange(parts_ref, land_ref, send_sems, recv_sems, local_sem, finish=True)

    sq = _seq_spec(seq)
    shape = (bsz, seq, D_RNN)
    gshape = (RNN_BLOCKS, RNN_BW, RNN_BW)
    return pl.pallas_call(
        body, grid=(RNN_BLOCKS, bsz),
        in_specs=[sq, sq, sq, sq, sq, _chan_spec(8), _chan_spec(8), _GATE_W_SPEC, _GATE_W_SPEC, _ANY],
        out_specs=[sq, sq, _GATE_W_SPEC, _GATE_W_SPEC, _chan_spec(8), _ANY],
        out_shape=[jax.ShapeDtypeStruct(shape, BF16), jax.ShapeDtypeStruct(shape, BF16),
                   jax.ShapeDtypeStruct(gshape, F32), jax.ShapeDtypeStruct(gshape, F32),
                   jax.ShapeDtypeStruct((8, D_RNN), F32), jax.ShapeDtypeStruct(parts.shape, parts.dtype)],
        scratch_shapes=[pltpu.VMEM((seq + 8, RNN_BW), F32), pltpu.VMEM((seq, RNN_BW), F32),
                        pltpu.VMEM((seq, RNN_BW), F32), pltpu.VMEM((seq, RNN_BW), F32)] + _EXCHANGE_SEMS,
        compiler_params=_params(("arbitrary", "arbitrary")), name=name)(dy, xp, xb, hs, ga, cw, vecs, wrg, wig, parts)


def _attn_block(seq):
    return min(512, seq)


def _diag_mask(blk):
    return lax.broadcasted_iota(jnp.int32, (blk, blk), 0) <= lax.broadcasted_iota(jnp.int32, (blk, blk), 1)


FWD_HEADS = 8
BWD_HEADS = 2


def _attn_fwd(q, kn, kr, v_t, block, *, bsz, seq, name):
    t = bsz * seq
    blk = _attn_block(seq)
    nq = seq // blk
    hg = FWD_HEADS
    steps = (bsz, N_HEADS // hg, nq)

    def body(q_ref, kn_ref, kr_ref, vt_ref, blk_ref, o_ref, lse_ref, all_ref, acc, send_sems, recv_sems, local_sem):
        first, last = _first_last(steps)

        @pl.when(first)
        def _():
            _exchange(blk_ref, all_ref, send_sems, recv_sems, local_sem, finish=False, gather=True)

        qi = pl.program_id(2)
        acc[...] = jnp.zeros_like(acc)

        def step(j, carry, diagonal):
            k0 = pl.multiple_of(j * blk, blk)
            kr_j = kr_ref[pl.ds(k0, blk), :]
            out = []
            for h in range(hg):
                m_i, l_i = carry[h]
                kv = jnp.concatenate([kn_ref[pl.ds(k0, blk), h * QK_NOPE:(h + 1) * QK_NOPE], kr_j], axis=1)
                qv = q_ref[:, h * HEAD_PAD:(h + 1) * HEAD_PAD]
                s = lax.dot_general(kv, qv, _NT, preferred_element_type=F32) * ATTN_SCALE
                if diagonal:
                    s = jnp.where(_diag_mask(blk), s, -jnp.inf)
                m_new = jnp.maximum(m_i, jnp.max(s, axis=0, keepdims=True))
                p = jnp.exp(s - m_new)
                alpha = jnp.exp(m_i - m_new)
                l_new = alpha * l_i + jnp.sum(p, axis=0, keepdims=True)
                acc[h] = alpha * acc[h] + jnp.dot(vt_ref[h * V_DIM:(h + 1) * V_DIM, pl.ds(k0, blk)], p.astype(BF16),
                                                  preferred_element_type=F32)
                out.append((m_new, l_new))
            return tuple(out)

        init = tuple((jnp.full((1, blk), -jnp.inf, F32), jnp.zeros((1, blk), F32)) for _ in range(hg))
        carry = lax.fori_loop(0, qi, lambda j, c: step(j, c, False), init)
        stats = step(qi, carry, True)
        for h in range(hg):
            m_i, l_i = stats[h]
            o_ref[:, h * V_DIM:(h + 1) * V_DIM] = (acc[h] / l_i).T
            lse_ref[h] = m_i + jnp.log(l_i)

        @pl.when(last)
        def _():
            _exchange(blk_ref, all_ref, send_sems, recv_sems, local_sem, finish=True, gather=True)

    return pl.pallas_call(
        body, grid=steps,
        in_specs=[pl.BlockSpec((blk, hg * HEAD_PAD), lambda b, g, i: (b * nq + i, g)),
                  pl.BlockSpec((seq, hg * QK_NOPE), lambda b, g, i: (b, g)),
                  pl.BlockSpec((seq, LANES), lambda b, g, i: (b, 0)),
                  pl.BlockSpec((hg * V_DIM, seq), lambda b, g, i: (g, b)), _ANY],
        out_specs=[pl.BlockSpec((blk, hg * V_DIM), lambda b, g, i: (b * nq + i, g)),
                   pl.BlockSpec((hg, 1, blk), lambda b, g, i: (g, 0, b * nq + i)), _ANY],
        out_shape=[jax.ShapeDtypeStruct((t, N_HEADS * V_DIM), F32), jax.ShapeDtypeStruct((N_HEADS, 1, t), F32),
                   jax.ShapeDtypeStruct((N_DEV,) + block.shape, block.dtype)],
        scratch_shapes=[pltpu.VMEM((hg, V_DIM, blk), F32)] + _EXCHANGE_SEMS,
        compiler_params=_params(("arbitrary", "arbitrary", "arbitrary")), name=name)(q, kn, kr, v_t, block)


def _attn_bwd(q, kn, kr, kn_t, kr_t, v, o, lse, do, cos, sin, parts, *, bsz, seq, name):
    t = bsz * seq
    blk = _attn_block(seq)
    nq = seq // blk
    hg = BWD_HEADS
    steps = (bsz, N_HEADS // hg)

    def body(q_ref, kn_ref, kr_ref, knt_ref, krt_ref, v_ref, o_ref, lse_ref, do_ref, cos_ref, sin_ref, parts_ref,
             dq_ref, dkn_ref, dkr_ref, dv_ref, land_ref, dqt_acc, dk_acc, dv_acc, send_sems, recv_sems, local_sem):
        first, last = _first_last(steps)

        @pl.when(first)
        def _():
            _exchange(parts_ref, land_ref, send_sems, recv_sems, local_sem, finish=False)

        dqt_acc[...] = jnp.zeros_like(dqt_acc)
        dk_acc[...] = jnp.zeros_like(dk_acc)
        dv_acc[...] = jnp.zeros_like(dv_acc)

        def q_block(i, _):
            q0 = pl.multiple_of(i * blk, blk)
            rows = []
            for h in range(hg):
                dov = do_ref[pl.ds(q0, blk), h * V_DIM:(h + 1) * V_DIM].astype(F32)
                dcol = jnp.sum(dov * o_ref[pl.ds(q0, blk), h * V_DIM:(h + 1) * V_DIM], axis=-1, keepdims=True)
                delta = jnp.broadcast_to(dcol, (blk, LANES)).T[0:1, :]
                rows.append((lse_ref[h, :, pl.ds(q0, blk)], delta))

            def pair(j, diagonal):
                k0 = pl.multiple_of(j * blk, blk)
                kr_j = kr_ref[pl.ds(k0, blk), :]
                krt_j = krt_ref[:, pl.ds(k0, blk)]
                for h in range(hg):
                    lse_i, delta = rows[h]
                    qv = q_ref[pl.ds(q0, blk), h * HEAD_PAD:(h + 1) * HEAD_PAD]
                    dov = do_ref[pl.ds(q0, blk), h * V_DIM:(h + 1) * V_DIM]
                    kv = jnp.concatenate([kn_ref[pl.ds(k0, blk), h * QK_NOPE:(h + 1) * QK_NOPE], kr_j], axis=1)
                    s = lax.dot_general(kv, qv, _NT, preferred_element_type=F32) * ATTN_SCALE
                    p = jnp.exp(s - lse_i)
                    if diagonal:
                        p = jnp.where(_diag_mask(blk), p, 0.0)
                    dv_acc[pl.ds(k0, blk), h * V_DIM:(h + 1) * V_DIM] += jnp.dot(
                        p.astype(BF16), dov, preferred_element_type=F32)
                    dp = lax.dot_general(v_ref[pl.ds(k0, blk), h * V_DIM:(h + 1) * V_DIM], dov, _NT,
                                         preferred_element_type=F32)
                    ds = (p * (dp - delta) * ATTN_SCALE).astype(BF16)
                    dk_acc[pl.ds(k0, blk), h * HEAD_PAD:(h + 1) * HEAD_PAD] += jnp.dot(ds, qv, preferred_element_type=F32)
                    base = h * HEAD_PAD
                    dqt_acc[base:base + QK_NOPE, pl.ds(q0, blk)] += jnp.dot(
                        knt_ref[h * QK_NOPE:(h + 1) * QK_NOPE, pl.ds(k0, blk)], ds, preferred_element_type=F32)
                    dqt_acc[base + QK_NOPE:base + HEAD_PAD, pl.ds(q0, blk)] += jnp.dot(
                        krt_j, ds, preferred_element_type=F32)

            def off_diagonal(j, _):
                pair(j, False)
                return 0

            lax.fori_loop(0, i, off_diagonal, 0)
            pair(i, True)
            return 0

        lax.fori_loop(0, nq, q_block, 0)
        dkr = jnp.zeros((seq, LANES), F32)
        for h in range(hg):
            base = h * HEAD_PAD
            for i in range(nq):
                rows = slice(i * blk, (i + 1) * blk)
                dq = dqt_acc[base:base + HEAD_PAD, rows].T
                dq_ref[rows, base:base + QK_NOPE] = dq[:, :QK_NOPE].astype(BF16)
                dq_ref[rows, base + QK_NOPE:base + HEAD_PAD] = _rope_t(
                    dq[:, QK_NOPE:], cos_ref[rows, :], sin_ref[rows, :]).astype(BF16)
            dkn_ref[:, h * QK_NOPE:(h + 1) * QK_NOPE] = dk_acc[:, base:base + QK_NOPE].astype(BF16)
            dkr = dkr + dk_acc[:, base + QK_NOPE:base + HEAD_PAD]
        dv_ref[...] = dv_acc[...].astype(BF16)

        @pl.when(pl.program_id(1) == 0)
        def _():
            dkr_ref[...] = jnp.zeros_like(dkr_ref)

        dkr_ref[...] += _rope_t(dkr, cos_ref[...], sin_ref[...])

        @pl.when(last)
        def _():
            _exchange(parts_ref, land_ref, send_sems, recv_sems, local_sem, finish=True)

    head = pl.BlockSpec((seq, hg * V_DIM), lambda b, g: (b, g))
    head_t = pl.BlockSpec((hg * V_DIM, seq), lambda b, g: (g, b))
    shared = pl.BlockSpec((seq, LANES), lambda b, g: (b, 0))
    shared_t = pl.BlockSpec((LANES, seq), lambda b, g: (0, b))
    table = pl.BlockSpec((seq, LANES), lambda b, g: (0, 0))
    qspec = pl.BlockSpec((seq, hg * HEAD_PAD), lambda b, g: (b, g))
    return pl.pallas_call(
        body, grid=steps,
        in_specs=[qspec, head, shared, head_t, shared_t, head, head,
                  pl.BlockSpec((hg, 1, seq), lambda b, g: (g, 0, b)), head, table, table, _ANY],
        out_specs=[qspec, head, shared, head, _ANY],
        out_shape=[jax.ShapeDtypeStruct((t, N_HEADS * HEAD_PAD), BF16), jax.ShapeDtypeStruct((t, N_HEADS * QK_NOPE), BF16),
                   jax.ShapeDtypeStruct((t, LANES), F32), jax.ShapeDtypeStruct((t, N_HEADS * V_DIM), BF16),
                   jax.ShapeDtypeStruct(parts.shape, parts.dtype)],
        scratch_shapes=[pltpu.VMEM((hg * HEAD_PAD, seq), F32), pltpu.VMEM((seq, hg * HEAD_PAD), F32),
                        pltpu.VMEM((seq, hg * V_DIM), F32)] + _EXCHANGE_SEMS,
        compiler_params=_params(("arbitrary", "arbitrary")), name=name)(
            q, kn, kr, kn_t, kr_t, v, o, lse, do, cos, sin, parts)


def _head_and_loss(o, g2, x1, target, w_out, g_final, *, name, bt=512):
    t, d = x1.shape
    bt = min(bt, t)
    nt = (((1,), (1,)), ((), ()))

    def body(o_ref, g2_ref, x1_ref, tgt_ref, w_ref, gf_ref, loss_ref, dx2_ref, y2_ref, do_ref, dg2_ref, dgf_ref):
        @pl.when(pl.program_id(0) == 0)
        def _():
            loss_ref[...] = jnp.zeros_like(loss_ref)
            dgf_ref[...] = jnp.zeros_like(dgf_ref)

        ov = o_ref[...]
        gv = g2_ref[...]
        sg = _sigmoid(gv)
        silu = gv * sg
        y2 = (ov * silu).astype(BF16)
        y2_ref[...] = y2
        w = w_ref[...]
        x2 = x1_ref[...] + jnp.dot(y2, w, preferred_element_type=F32)
        r = lax.rsqrt(jnp.mean(x2 * x2, axis=-1, keepdims=True) + EPS)
        nrm = x2 * r
        gf = gf_ref[...]
        err = nrm * gf - tgt_ref[...]
        loss_ref[...] += 0.5 * jnp.sum(jnp.mean(err * err, axis=-1, keepdims=True))
        dyf = err * (1.0 / d)
        dgf_ref[...] += jnp.sum(dyf * nrm, axis=0, keepdims=True)
        dn = dyf * gf
        dx2 = r * (dn - nrm * jnp.mean(dn * nrm, axis=-1, keepdims=True))
        dx2_ref[...] = dx2
        dy2 = lax.dot_general(dx2.astype(BF16), w, nt, preferred_element_type=F32)
        do_ref[...] = (dy2 * silu).astype(BF16)
        dg2_ref[...] = (dy2 * ov * (sg * (1.0 + gv * (1.0 - sg)))).astype(BF16)

    row = pl.BlockSpec((bt, d), lambda i: (i, 0))
    vec = pl.BlockSpec((1, d), lambda i: (0, 0))
    return pl.pallas_call(
        body, grid=(t // bt,),
        in_specs=[row, row, row, row, pl.BlockSpec((d, d), lambda i: (0, 0)), vec],
        out_specs=[pl.BlockSpec((8, LANES), lambda i: (0, 0)), row, row, row, row, vec],
        out_shape=[jax.ShapeDtypeStruct((8, LANES), F32), jax.ShapeDtypeStruct((t, d), F32),
                   jax.ShapeDtypeStruct((t, d), BF16), jax.ShapeDtypeStruct((t, d), BF16),
                   jax.ShapeDtypeStruct((t, d), BF16), jax.ShapeDtypeStruct((1, d), F32)],
        compiler_params=_params(("arbitrary",)), name=name)(o, g2, x1, target, w_out, g_final)


def _sum_parts(parts, *, name, br=GRAD_BLOCK):
    npart, rows, w = parts.shape

    def body(p_ref, o_ref):
        acc = p_ref[0].astype(F32)
        for j in range(1, npart):
            acc = acc + p_ref[j].astype(F32)
        o_ref[...] = acc

    return pl.pallas_call(
        body, grid=(rows // br,), in_specs=[pl.BlockSpec((npart, br, w), lambda i: (0, i, 0))],
        out_specs=pl.BlockSpec((br, w), lambda i: (i, 0)), out_shape=jax.ShapeDtypeStruct((rows, w), F32),
        compiler_params=_params(("parallel",)), name=name)(parts)


def _chip_partial(parts, recv, *, name, br=GRAD_BLOCK):
    _, rows, w = parts.shape
    core = lax.axis_index("c").astype(jnp.int32).reshape(1)

    def body(c_ref, p_ref, r_ref, o_ref):
        o_ref[...] = (p_ref[...] + r_ref[...]).astype(BF16)

    grid_spec = pltpu.PrefetchScalarGridSpec(
        num_scalar_prefetch=1, grid=(4, rows // br),
        in_specs=[pl.BlockSpec((None, br, w), lambda k, i, c_ref: (2 * k + c_ref[0], i, 0)),
                  pl.BlockSpec((None, br, w), lambda k, i, c_ref: (k, i, 0))],
        out_specs=pl.BlockSpec((None, br, w), lambda k, i, c_ref: (k, i, 0)))
    return pl.pallas_call(
        body, grid_spec=grid_spec, out_shape=jax.ShapeDtypeStruct((4, rows, w), BF16),
        compiler_params=_params(("parallel", "parallel")), name=name)(core, parts, recv)


def _as_block(a):
    if a.ndim == 1:
        return a.reshape(1, -1)
    if a.ndim > 2 and a.shape[0] == 1:
        return a.reshape(a.shape[1:])
    return a


def _adamw(g, w, m, v, *, name):
    shape = w.shape
    g, w, m, v = (_as_block(a) for a in (g, w, m, v))

    def body(g_ref, w_ref, m_ref, v_ref, d_ref, nm_ref, nv_ref):
        gv = g_ref[...]
        nm = ADAM_B1 * m_ref[...] + (1.0 - ADAM_B1) * gv
        nv = ADAM_B2 * v_ref[...] + (1.0 - ADAM_B2) * (gv * gv)
        nm_ref[...] = nm
        nv_ref[...] = nv
        m_hat = nm / (1.0 - ADAM_B1 ** ADAM_STEP)
        v_hat = nv / (1.0 - ADAM_B2 ** ADAM_STEP)
        d_ref[...] = (-ADAM_LR) * (m_hat / (jnp.sqrt(v_hat) + ADAM_EPS) + ADAM_WD * w_ref[...])

    whole = pl.BlockSpec(memory_space=pltpu.VMEM)
    outs = pl.pallas_call(
        body, in_specs=[whole] * 4, out_specs=[whole] * 3, out_shape=[jax.ShapeDtypeStruct(w.shape, F32)] * 3,
        compiler_params=_params(), name=name)(g, w, m, v)
    return [o.reshape(shape) for o in outs]


def _all_gather(block, *, name):
    m, n = block.shape

    def body(x_ref, out_ref, send_sems, recv_sems, local_sem):
        for phase in range(3):
            _gather_two_level(x_ref, out_ref, send_sems, recv_sems, local_sem, phase=phase)

    return pl.pallas_call(
        body, out_shape=jax.ShapeDtypeStruct((N_DEV, m, n), block.dtype), in_specs=[_ANY], out_specs=_ANY,
        scratch_shapes=_EXCHANGE_SEMS, name=name)(block)


def _exchange_d2d(parts, *, name):
    _, rows, w = parts.shape

    def body(p_ref, land_ref, send_sems, recv_sems):
        x, y, c = _mesh_pos()
        sends = []
        for k in range(4):
            cp = pltpu.make_async_remote_copy(
                src_ref=p_ref.at[2 * k + (1 - c)], dst_ref=land_ref.at[k], send_sem=send_sems.at[k],
                recv_sem=recv_sems.at[k], device_id=(x, y, 1 - c), device_id_type=pl.DeviceIdType.MESH)
            cp.start()
            sends.append(cp)
        for cp in sends:
            cp.wait_recv()
        for cp in sends:
            cp.wait_send()

    return pl.pallas_call(
        body, out_shape=jax.ShapeDtypeStruct((4, rows, w), parts.dtype), in_specs=[_ANY], out_specs=_ANY,
        scratch_shapes=[pltpu.SemaphoreType.DMA((4,)), pltpu.SemaphoreType.DMA((4,))], name=name)(parts)


def _exchange_ici(parts, *, name):
    def body(p_ref, land_ref, send_sems, recv_sems, local_sem):
        x, y, c = _mesh_pos()
        mine = pltpu.make_async_copy(p_ref.at[2 * x + y], land_ref.at[3], local_sem)
        mine.start()
        sends = []
        for k, (px, py) in enumerate([(1 - x, y), (x, 1 - y), (1 - x, 1 - y)]):
            cp = pltpu.make_async_remote_copy(
                src_ref=p_ref.at[2 * px + py], dst_ref=land_ref.at[k], send_sem=send_sems.at[k],
                recv_sem=recv_sems.at[k], device_id=(px, py, c), device_id_type=pl.DeviceIdType.MESH)
            cp.start()
            sends.append(cp)
        for cp in sends:
            cp.wait_recv()
        for cp in sends:
            cp.wait_send()
        mine.wait()

    return pl.pallas_call(
        body, out_shape=jax.ShapeDtypeStruct(parts.shape, parts.dtype), in_specs=[_ANY], out_specs=_ANY,
        scratch_shapes=[pltpu.SemaphoreType.DMA((3,)), pltpu.SemaphoreType.DMA((3,)), pltpu.SemaphoreType.DMA(())],
        name=name)(parts)


def _rows(a):
    return a.reshape(-1, PACK_W)


def _pad_to(a, n):
    return jnp.pad(a, (0, n - a.shape[0]))


def _weight_blocks(d):
    small = _rows(_pad_to(jnp.concatenate([d[n].reshape(-1) for n, _ in _SMALL]), 16 * PACK_W))
    bits = lax.bitcast_convert_type(small, jnp.uint32)
    halves = [lax.bitcast_convert_type(h.astype(jnp.uint16), WIRE) for h in (bits >> 16, bits & 0xFFFF)]
    block_a = jnp.concatenate([d["w_in_a"][0].T.astype(WIRE)] + halves, axis=0)
    w_uq = jnp.pad(d["w_uq"][0], ((0, 0), (0, 0), (0, HEAD_PAD - QK_NOPE - QK_ROPE)))
    pieces = {"w_out_a": d["w_out_a"], "w_dkv": d["w_dkv"], "w_uk": d["w_uk"], "w_uv": d["w_uv"],
              "w_in_b": d["w_in_b"][0].T, "w_uq": w_uq}
    block_b = jnp.concatenate([_rows(pieces[n]) for n, _ in _PIECES_B]
                              + [jnp.zeros((WIRE_ROWS_B - MATRIX_ROWS_B, PACK_W), F32)], axis=0).astype(WIRE)
    block_d = jnp.concatenate([_rows(pieces[n]) for n, _ in _PIECES_D], axis=0).astype(WIRE)
    return block_a, block_b, d["w_out_b"][0].astype(WIRE), block_d


def _weights_a(wall):
    w = {}
    lo, hi = _OFF_A["w_in_a"]
    w["w_in_a_t"] = wall[:, lo:hi].reshape(2 * D_RNN, D_MODEL)
    high, low = (lax.bitcast_convert_type(wall[:, r:r + 16], jnp.uint16).astype(jnp.uint32)
                 for r in (MATRIX_ROWS_A, MATRIX_ROWS_A + 16))
    small = lax.bitcast_convert_type((high << 16) | low, F32)[:, :8].reshape(N_DEV, 8 * PACK_W)
    off = dict(zip([n for n, _ in _SMALL], [0, 128, 768, 928, 1088, 1248]))
    w["norm_a"] = small[:, :128].reshape(1, D_MODEL)

    def by_channel(lo, rows):
        a = small[:, lo:lo + rows * (D_RNN // N_DEV)].reshape(N_DEV, rows, -1).transpose(1, 0, 2).reshape(rows, D_RNN)
        return jnp.pad(a, ((0, 8 - rows), (0, 0)))

    w["conv_taps"] = by_channel(off["conv_w"], CONV_WIDTH)
    w["lru_vecs"] = by_channel(off["conv_b"], 4)
    return w


def _weights_b(wall):
    piece = {n: wall[:, lo:hi] for n, (lo, hi) in _OFF_B.items()}
    w = {"w_out_a": piece["w_out_a"].reshape(D_RNN, D_MODEL)}
    w_dkv = piece["w_dkv"].reshape(D_MODEL, KV_RANK + QK_ROPE)
    w["w_dkv_c"] = w_dkv[:, :KV_RANK]
    w["w_dkv_r"] = jnp.pad(w_dkv[:, KV_RANK:], ((0, 0), (0, LANES - QK_ROPE)))
    w["w_in_b_t"] = piece["w_in_b"].reshape(Q_RANK + N_HEADS * V_DIM, D_MODEL)
    return w


def _weights_d(wall):
    piece = {n: wall[:, lo:hi] for n, (lo, hi) in _OFF_D.items()}
    return {"w_uk": piece["w_uk"].reshape(KV_RANK, N_HEADS * QK_NOPE),
            "w_uv": piece["w_uv"].reshape(KV_RANK, N_HEADS * V_DIM),
            "w_uq": piece["w_uq"].reshape(Q_RANK, N_HEADS * HEAD_PAD)}


def _pack_rep(d):
    flat = jnp.concatenate([d[n].reshape(-1) for n, _ in _REP])
    return _rows(_pad_to(flat, REP_ROWS * PACK_W))


def _unpack_rep(p, like):
    flat = p.reshape(-1)
    out, off = {}, 0
    for n, k in _REP:
        out[n] = flat[off:off + k].reshape(like[n].shape)
        off += k
    return out


def _by_owner(a):
    return a.reshape(N_DEV, -1, PACK_W)


def _grad_parts_b(g):
    tail = jnp.zeros((N_DEV, WIRE_ROWS_B - MATRIX_ROWS_B, PACK_W), F32)
    return jnp.concatenate([_by_owner(g[n]) for n, _ in _PIECES_B] + [tail], axis=1).astype(BF16)


def _grad_parts_a(g):
    small = jnp.concatenate([
        g["norm_a"].reshape(N_DEV, -1),
        g["conv_w"].reshape(CONV_WIDTH, N_DEV, -1).transpose(1, 0, 2).reshape(N_DEV, -1),
        g["conv_b"].reshape(N_DEV, -1), g["b_rg"].reshape(N_DEV, -1), g["b_ig"].reshape(N_DEV, -1),
        g["lru_lambda"].reshape(N_DEV, -1)], axis=1)
    small = jnp.pad(small, ((0, 0), (0, 8 * PACK_W - small.shape[1]))).reshape(N_DEV, 8, PACK_W)
    half = N_DEV // 2
    w_in_a = jnp.concatenate([h.reshape(half, -1, PACK_W) for h in g["w_in_a_t"]], axis=0)
    rep = _pack_rep(g).reshape(N_DEV, REP_SLICE, PACK_W)
    tail = jnp.zeros((N_DEV, GRAD_ROWS_A - MATRIX_ROWS_A - 8 - REP_SLICE, PACK_W), F32)
    return jnp.concatenate([w_in_a, small, rep, tail], axis=1)


def _own_grads(sum_a, sum_b, sum_c, sum_d):
    out = {}
    lo, hi = _OFF_A["w_in_a"]
    out["w_in_a"] = sum_a[lo:hi].T.reshape(1, D_MODEL, 2 * D_RNN // N_DEV)
    small = sum_a[MATRIX_ROWS_A:MATRIX_ROWS_A + 8].reshape(-1)
    shapes = {"norm_a": (1, D_MODEL // N_DEV), "conv_w": (1, CONV_WIDTH, D_RNN // N_DEV), "conv_b": (1, D_RNN // N_DEV),
              "b_rg": (1, D_RNN // N_DEV), "b_ig": (1, D_RNN // N_DEV), "lru_lambda": (1, D_RNN // N_DEV)}
    off = 0
    for n, k in _SMALL:
        out[n] = small[off:off + k].reshape(shapes[n])
        off += k
    piece = {n: sum_b[lo:hi] for n, (lo, hi) in _OFF_B.items()}
    out["w_out_a"] = piece["w_out_a"].reshape(1, D_RNN // N_DEV, D_MODEL)
    out["w_dkv"] = piece["w_dkv"].reshape(D_MODEL // N_DEV, KV_RANK + QK_ROPE)
    out["w_in_b"] = piece["w_in_b"].T.reshape(1, D_MODEL, (Q_RANK + N_HEADS * V_DIM) // N_DEV)
    piece = {n: sum_d[lo:hi] for n, (lo, hi) in _OFF_D.items()}
    out["w_uk"] = piece["w_uk"].reshape(KV_RANK // N_DEV, N_HEADS, QK_NOPE)
    out["w_uv"] = piece["w_uv"].reshape(KV_RANK // N_DEV, N_HEADS, V_DIM)
    out["w_uq"] = piece["w_uq"].reshape(1, Q_RANK // N_DEV, N_HEADS, HEAD_PAD)[..., :QK_NOPE + QK_ROPE]
    out["w_out_b"] = sum_c.reshape(1, N_HEADS * V_DIM // N_DEV, D_MODEL)
    return out


def _step(x, target, w, rep, block_b, block_c, block_d, *, bsz, seq):
    t = bsz * seq
    cos, sin = _rope_tables(seq)
    g_a = w["norm_a"]
    g_kv = rep["norm_kv"].reshape(1, -1)
    g_kvn = rep["kv_norm"].reshape(1, -1)
    g_b = rep["norm_b"].reshape(1, -1)
    g_q = rep["q_norm"].reshape(1, -1)
    g_f = rep["final_norm"].reshape(1, -1)
    wrg = rep["w_rg"][0].astype(BF16)
    wig = rep["w_ig"][0].astype(BF16)
    cw8, vecs = w["conv_taps"], w["lru_vecs"]

    def seq3(a):
        return a.reshape(bsz, seq, a.shape[-1])

    def flat(a):
        return a.reshape(t, a.shape[-1])

    h0, xp, ga, wall_d = _lru_proj_fwd(x, g_a, w["w_in_a_t"], block_d, name="lru_proj_fwd")
    xb, hs, y, wall_b = _lru_fwd(seq3(xp), seq3(ga), cw8, vecs, wrg, wig, block_b, name="lru_fwd")
    w = dict(w, **_weights_b(wall_b), **_weights_d(wall_d))
    x1 = _matmul(flat(y), w["w_out_a"], residual=x, name="out_a")
    hk, hq, ck, cqp, g2, ckv, cq, q, kn, v, kr, kn_t, v_t, kr_t = _mla_proj_fwd(
        x1, (g_kv, g_b, g_kvn, g_q), w, cos, sin, seq=seq, name="mla_proj_fwd")
    o, lse, wall_c = _attn_fwd(q, kn, kr, v_t, block_c, bsz=bsz, seq=seq, name="attn_fwd")
    w_out_b = wall_c.reshape(N_HEADS * V_DIM, D_MODEL)
    loss, dx2, y2, do, dg2, dgf = _head_and_loss(o, g2, x1, target, w_out_b, g_f, name="head_loss")
    grads = {"final_norm": dgf}
    parts_c = _by_owner(_matmul_tn(y2, dx2, name="d_w_out_b")).astype(BF16)
    dq, dkn, dkr, dv, landed_c = _attn_bwd(q, kn, kr, kn_t, kr_t, v, o, lse, do, cos, sin, parts_c,
                                           bsz=bsz, seq=seq, name="attn_bwd")
    grads["w_uq"] = _matmul_tn(cq, dq, name="d_w_uq")
    grads["w_uk"] = _matmul_tn(ckv, dkn, name="d_w_uk")
    grads["w_uv"] = _matmul_tn(ckv, dv, name="d_w_uv")
    parts_d = jnp.concatenate([_by_owner(grads[n]) for n, _ in _PIECES_D], axis=1).astype(BF16)
    dx1, du2, dckr, dgkv, dgb, dgkvn, dgq, landed_d = _mla_proj_bwd(
        x1, dx2, cqp, ck, dq, dkn, dv, dkr, dg2, (g_kv, g_b, g_kvn, g_q), w, parts_d, name="mla_proj_bwd")
    grads["norm_kv"], grads["norm_b"], grads["kv_norm"], grads["q_norm"] = dgkv, dgb, dgkvn, dgq
    grads["w_in_b"] = _matmul_tn(du2, hq, name="d_w_in_b_t")
    grads["w_dkv"] = _matmul_tn(hk, dckr, name="d_w_dkv")[:, :KV_RANK + QK_ROPE]
    grads["w_out_a"] = _matmul_tn(flat(y), dx1, name="d_w_out_a")
    parts_b = _grad_parts_b(grads)
    dy = _matmul(dx1, w["w_out_a"], nt=True, name="d_y")
    dxp, dga, dwrg, dwig, dvec, landed_b = _lru_bwd(
        seq3(dy), seq3(xp), xb, hs, seq3(ga), cw8, vecs, wrg, wig, parts_b, name="lru_bwd")
    dxp, dga = flat(dxp), flat(dga)
    grads["w_rg"], grads["w_ig"] = dwrg, dwig
    grads["b_rg"], grads["b_ig"], grads["conv_b"] = dvec[0], dvec[1], dvec[3]
    lam = vecs[3]
    grads["lru_lambda"] = dvec[2] * (-1.0 / (1.0 + jnp.exp(lam)))
    grads["conv_w"] = dvec[4:4 + CONV_WIDTH]
    grads["w_in_a_t"] = (_matmul_tn(dxp, h0, name="d_w_in_a_x_t"), _matmul_tn(dga, h0, name="d_w_in_a_g_t"))
    dx, dga_norm = _lru_proj_bwd(dxp, dga, x, dx1, g_a, w["w_in_a_t"], name="lru_proj_bwd")
    grads["norm_a"] = dga_norm
    return loss[0, 0], dx, grads, landed_b, landed_c, landed_d


def kernel(x, norm_a, w_in_a, conv_w, conv_b, w_rg, b_rg, w_ig, b_ig, lru_lambda, w_out_a, norm_kv, w_dkv, kv_norm, w_uk, w_uv, norm_b, w_in_b, q_norm, w_uq, w_out_b, final_norm, loss_target, m_norm_a, m_w_in_a, m_conv_w, m_conv_b, m_w_rg, m_b_rg, m_w_ig, m_b_ig, m_lru_lambda, m_w_out_a, m_norm_kv, m_w_dkv, m_kv_norm, m_w_uk, m_w_uv, m_norm_b, m_w_in_b, m_q_norm, m_w_uq, m_w_out_b, m_final_norm, v_norm_a, v_w_in_a, v_conv_w, v_conv_b, v_w_rg, v_b_rg, v_w_ig, v_b_ig, v_lru_lambda, v_w_out_a, v_norm_kv, v_w_dkv, v_kv_norm, v_w_uk, v_w_uv, v_norm_b, v_w_in_b, v_q_norm, v_w_uq, v_w_out_b, v_final_norm):
    given = dict(locals())
    wts = {n: given[n] for n in WEIGHTS}
    mom1 = {n: given["m_" + n] for n in WEIGHTS}
    mom2 = {n: given["v_" + n] for n in WEIGHTS}
    bsz, seq, _ = x.shape
    t = bsz * seq

    block_a, block_b, block_c, block_d = _weight_blocks(wts)
    w = _weights_a(_all_gather(block_a, name="gather_weights_a"))
    loss, dx, grads, landed_b, landed_c, landed_d = _step(
        x.reshape(t, D_MODEL), loss_target.reshape(t, D_MODEL), w, wts, block_b, block_c, block_d, bsz=bsz, seq=seq)

    parts_a = _grad_parts_a(grads)
    from_sibling = _exchange_d2d(parts_a, name="exchange_grads_d2d")
    chip_parts = _chip_partial(parts_a, from_sibling, name="chip_partial_grads")
    landed_a = _exchange_ici(chip_parts, name="exchange_grads_ici")
    sum_a = _sum_parts(landed_a, name="sum_grads_a", br=GRAD_BLOCK)
    sum_b = _sum_parts(landed_b, name="sum_grads_b", br=WIRE_ROWS_B // 2)
    sum_c = _sum_parts(landed_c, name="sum_grads_c", br=landed_c.shape[1])
    sum_d = _sum_parts(landed_d, name="sum_grads_d", br=WIRE_ROWS_D)
    g_own = _own_grads(sum_a, sum_b, sum_c, sum_d)
    rep_slice = sum_a[MATRIX_ROWS_A + 8:MATRIX_ROWS_A + 8 + REP_SLICE]
    loss_rows = jnp.pad(loss.reshape(1, 1), ((0, 7), (0, PACK_W - 1)))
    gathered = _all_gather(jnp.concatenate([rep_slice, loss_rows], axis=0), name="gather_replicated")
    g_own.update(_unpack_rep(gathered[:, :REP_SLICE].reshape(REP_ROWS, PACK_W), wts))
    loss = jnp.sum(gathered[:, REP_SLICE, 0])

    deltas, new_m, new_v = {}, {}, {}
    for n in WEIGHTS:
        deltas[n], new_m[n], new_v[n] = _adamw(g_own[n], wts[n], mom1[n], mom2[n], name="adamw_" + n)
    result = [loss, dx.reshape(bsz, seq, D_MODEL)]
    for d in (g_own, deltas, new_m, new_v):
        result.extend(d[n] for n in WEIGHTS)
    return tuple(result)
```

```python
import jax
import jax.numpy as jnp
from jax import lax
from jax.experimental import pallas as pl
from jax.experimental.pallas import tpu as pltpu

F32 = jnp.float32
BF16 = jnp.bfloat16
WIRE = jnp.bfloat16

D_MODEL = 1024
D_RNN = 1280
RNN_BLOCKS = 10
RNN_BW = 128
CONV_WIDTH = 4
LRU_C = 8.0
N_HEADS = 8
QK_NOPE = 128
QK_ROPE = 64
V_DIM = 128
KV_RANK = 256
Q_RANK = 384
ROPE_THETA = 10000.0
EPS = 1e-6
ATTN_SCALE = (QK_NOPE + QK_ROPE) ** -0.5
HEAD_PAD = 256
LANES = 128

ADAM_LR = 0.001
ADAM_B1 = 0.9
ADAM_B2 = 0.999
ADAM_EPS = 1e-08
ADAM_WD = 0.01
ADAM_STEP = 10

N_DEV = 8
VMEM_LIMIT_BYTES = 56 * 2**20
PACK_W = 1024

_PIECES_A = (("w_in_a", 320),)
_PIECES_B = (("w_out_a", 160), ("w_dkv", 40), ("w_uk", 32), ("w_uv", 32), ("w_in_b", 176), ("w_uq", 96))


def _offsets(pieces):
    off, r = {}, 0
    for n, k in pieces:
        off[n] = (r, r + k)
        r += k
    return off, r


_OFF_A, MATRIX_ROWS_A = _offsets(_PIECES_A)
_OFF_B, MATRIX_ROWS_B = _offsets(_PIECES_B)
WIRE_ROWS_A = MATRIX_ROWS_A + 32
WIRE_ROWS_B = 544
_SMALL = (("norm_a", 128), ("conv_w", 640), ("conv_b", 160), ("b_rg", 160), ("b_ig", 160), ("lru_lambda", 160))
_REP = (("w_rg", 163840), ("w_ig", 163840), ("norm_kv", 1024), ("kv_norm", 256), ("norm_b", 1024),
        ("q_norm", 384), ("final_norm", 1024))
REP_ROWS = 384
REP_SLICE = REP_ROWS // N_DEV
GRAD_ROWS_A = 384
GRAD_BLOCK = 192

WEIGHTS = ("norm_a", "w_in_a", "conv_w", "conv_b", "w_rg", "b_rg", "w_ig", "b_ig", "lru_lambda", "w_out_a",
           "norm_kv", "w_dkv", "kv_norm", "w_uk", "w_uv", "norm_b", "w_in_b", "q_norm", "w_uq", "w_out_b",
           "final_norm")


def _params(sem=None):
    return pltpu.CompilerParams(dimension_semantics=sem, vmem_limit_bytes=VMEM_LIMIT_BYTES)


_NT = (((1,), (1,)), ((), ()))
_ANY = pl.BlockSpec(memory_space=pl.ANY)


def _mesh_pos():
    return lax.axis_index("x"), lax.axis_index("y"), lax.axis_index("c")


def _sigmoid(z):
    return 0.5 * jnp.tanh(0.5 * z) + 0.5


def _sigmoid_tail(z):
    return 1.0 / (1.0 + jnp.exp(-z))


def _col_block(n):
    return n if n <= 1408 else n // 2


def _matmul(a, b, *, name, nt=False, out_dtype=F32, residual=None, bm=1024):
    m, k = a.shape
    n = b.shape[0] if nt else b.shape[1]
    bm = min(bm, m)
    bn = _col_block(n)
    dims = (((1,), (1,)), ((), ())) if nt else (((1,), (0,)), ((), ()))
    has_res = residual is not None

    def body(*refs):
        a_ref, b_ref, o_ref = refs[0], refs[1], refs[-1]
        acc = lax.dot_general(a_ref[...].astype(BF16), b_ref[...].astype(BF16), dims, preferred_element_type=F32)
        if has_res:
            acc = acc + refs[2][...]
        o_ref[...] = acc.astype(out_dtype)

    in_specs = [pl.BlockSpec((bm, k), lambda i, j: (i, 0)),
                pl.BlockSpec((bn, k), lambda i, j: (j, 0)) if nt else pl.BlockSpec((k, bn), lambda i, j: (0, j))]
    args = [a, b]
    if has_res:
        in_specs.append(pl.BlockSpec((bm, bn), lambda i, j: (i, j)))
        args.append(residual)
    return pl.pallas_call(
        body, grid=(m // bm, n // bn), in_specs=in_specs, out_specs=pl.BlockSpec((bm, bn), lambda i, j: (i, j)),
        out_shape=jax.ShapeDtypeStruct((m, n), out_dtype), compiler_params=_params(("parallel", "parallel")),
        name=name)(*args)


def _matmul_tn(a, b, *, name, bt=1024):
    t, m = a.shape
    n = b.shape[1]
    bt = min(bt, t)
    bm, bn = _col_block(m), _col_block(n)

    def body(a_ref, b_ref, o_ref):
        @pl.when(pl.program_id(2) == 0)
        def _():
            o_ref[...] = jnp.zeros_like(o_ref)

        o_ref[...] += lax.dot_general(a_ref[...].astype(BF16), b_ref[...].astype(BF16),
                                      (((0,), (0,)), ((), ())), preferred_element_type=F32)

    return pl.pallas_call(
        body, grid=(m // bm, n // bn, t // bt),
        in_specs=[pl.BlockSpec((bt, bm), lambda i, j, s: (s, i)), pl.BlockSpec((bt, bn), lambda i, j, s: (s, j))],
        out_specs=pl.BlockSpec((bm, bn), lambda i, j, s: (i, j)),
        out_shape=jax.ShapeDtypeStruct((m, n), F32),
        compiler_params=_params(("parallel", "parallel", "arbitrary")), name=name)(a, b)


def _swap_halves(v):
    ax = v.ndim - 1
    lane = lax.broadcasted_iota(jnp.int32, v.shape, ax)
    up = pltpu.roll(v, LANES - QK_ROPE // 2, axis=ax)
    down = pltpu.roll(v, QK_ROPE // 2, axis=ax)
    return jnp.where(lane < QK_ROPE // 2, up, jnp.where(lane < QK_ROPE, down, 0.0))


def _rope(v, cos, sin):
    return v * cos + _swap_halves(v) * sin


def _rope_t(d, cos, sin):
    return d * cos + _swap_halves(d * sin)


def _rope_tables(seq):
    pos = jnp.arange(seq, dtype=F32)
    inv = ROPE_THETA ** (-jnp.arange(0, QK_ROPE, 2, dtype=F32) / QK_ROPE)
    ang = pos[:, None] * inv[None, :]
    cos, sin = jnp.cos(ang), jnp.sin(ang)
    zero = jnp.zeros((seq, LANES - QK_ROPE), F32)
    return jnp.concatenate([cos, cos, zero], axis=1), jnp.concatenate([-sin, sin, zero], axis=1)


def _rms(v):
    return v * lax.rsqrt(jnp.mean(v * v, axis=-1, keepdims=True) + EPS)


def _const_spec(a):
    return pl.BlockSpec(a.shape, lambda i: (0,) * a.ndim)


def _lru_proj_fwd(x, g_a, w_in_t, *, name, bt=512):
    t, d = x.shape
    bt = min(bt, t)
    n = w_in_t.shape[0] // 2

    def body(x_ref, g_ref, wt_ref, h_ref, xp_ref, ga_ref):
        h = (_rms(x_ref[...]) * g_ref[...]).astype(BF16)
        h_ref[...] = h
        xp_ref[...] = lax.dot_general(h, wt_ref[0:n, :], _NT, preferred_element_type=F32)
        ga_ref[...] = lax.dot_general(h, wt_ref[n:2 * n, :], _NT, preferred_element_type=F32)

    row = lambda w: pl.BlockSpec((bt, w), lambda i: (i, 0))
    return pl.pallas_call(
        body, grid=(t // bt,), in_specs=[row(d), _const_spec(g_a), _const_spec(w_in_t)],
        out_specs=[row(d), row(n), row(n)],
        out_shape=[jax.ShapeDtypeStruct((t, d), BF16), jax.ShapeDtypeStruct((t, n), F32), jax.ShapeDtypeStruct((t, n), F32)],
        compiler_params=_params(("parallel",)), name=name)(x, g_a, w_in_t)


def _mla_proj_fwd(x1, gains, w, cos, sin, *, seq, name, bt=512):
    t, d = x1.shape
    bt = min(bt, seq)
    per_seq = seq // bt
    g_kv, g_b, g_kvn, g_q = gains
    consts = [g_kv, g_b, g_kvn, g_q, w["w_dkv_c"], w["w_dkv_r"], w["w_in_b_t"], w["w_uk"], w["w_uv"], w["w_uq"]]

    def body(x_ref, cos_ref, sin_ref, gkv_ref, gb_ref, gkvn_ref, gq_ref, wdc_ref, wdr_ref, wbt_ref,
             wuk_ref, wuv_ref, wuq_ref,
             hk_ref, hq_ref, ck_ref, cqp_ref, g2_ref, ckv_ref, cq_ref, q_ref, kn_ref, v_ref, kr_ref, knt_ref, vt_ref, krt_ref):
        nrm = _rms(x_ref[...])
        hk = (nrm * gkv_ref[...]).astype(BF16)
        hq = (nrm * gb_ref[...]).astype(BF16)
        hk_ref[...] = hk
        hq_ref[...] = hq
        ck = jnp.dot(hk, wdc_ref[...], preferred_element_type=F32)
        ck_ref[...] = ck
        cqp = lax.dot_general(hq, wbt_ref[0:Q_RANK, :], _NT, preferred_element_type=F32)
        cqp_ref[...] = cqp
        g2_ref[...] = lax.dot_general(hq, wbt_ref[Q_RANK:, :], _NT, preferred_element_type=F32)
        cosv, sinv = cos_ref[...], sin_ref[...]
        kr = _rope(jnp.dot(hk, wdr_ref[...], preferred_element_type=F32), cosv, sinv)
        kr_ref[...] = kr.astype(BF16)
        krt_ref[...] = kr.T.astype(BF16)
        ckv = (_rms(ck) * gkvn_ref[...]).astype(BF16)
        ckv_ref[...] = ckv
        kn = jnp.dot(ckv, wuk_ref[...], preferred_element_type=F32)
        v = jnp.dot(ckv, wuv_ref[...], preferred_element_type=F32)
        kn_ref[...] = kn.astype(BF16)
        v_ref[...] = v.astype(BF16)
        knt_ref[...] = kn.T.astype(BF16)
        vt_ref[...] = v.T.astype(BF16)
        cq = (_rms(cqp) * gq_ref[...]).astype(BF16)
        cq_ref[...] = cq
        for h in range(N_HEADS):
            qh = jnp.dot(cq, wuq_ref[:, h * HEAD_PAD:(h + 1) * HEAD_PAD], preferred_element_type=F32)
            q_ref[:, h * HEAD_PAD:h * HEAD_PAD + QK_NOPE] = qh[:, :QK_NOPE].astype(BF16)
            q_ref[:, h * HEAD_PAD + QK_NOPE:(h + 1) * HEAD_PAD] = _rope(qh[:, QK_NOPE:], cosv, sinv).astype(BF16)

    row = lambda w_: pl.BlockSpec((bt, w_), lambda i: (i, 0))
    col = lambda h_: pl.BlockSpec((h_, bt), lambda i: (0, i))
    tab = pl.BlockSpec((bt, LANES), lambda i: (i % per_seq, 0))
    nh = N_HEADS * V_DIM
    shapes = [((t, d), BF16), ((t, d), BF16), ((t, KV_RANK), F32), ((t, Q_RANK), F32), ((t, nh), F32), ((t, KV_RANK), BF16),
              ((t, Q_RANK), BF16), ((t, N_HEADS * HEAD_PAD), BF16), ((t, nh), BF16), ((t, nh), BF16), ((t, LANES), BF16),
              ((nh, t), BF16), ((nh, t), BF16), ((LANES, t), BF16)]
    out_specs = [row(d), row(d), row(KV_RANK), row(Q_RANK), row(nh), row(KV_RANK), row(Q_RANK), row(N_HEADS * HEAD_PAD),
                 row(nh), row(nh), row(LANES), col(nh), col(nh), col(LANES)]
    return pl.pallas_call(
        body, grid=(t // bt,), in_specs=[row(d), tab, tab] + [_const_spec(a) for a in consts], out_specs=out_specs,
        out_shape=[jax.ShapeDtypeStruct(s, dt) for s, dt in shapes],
        compiler_params=_params(("parallel",)), name=name)(x1, cos, sin, *consts)


def _rms_bwd_rows(xv, dn):
    r = lax.rsqrt(jnp.mean(xv * xv, axis=-1, keepdims=True) + EPS)
    nrm = xv * r
    return r * (dn - nrm * jnp.mean(dn * nrm, axis=-1, keepdims=True)), nrm


def _col_sum(v):
    return jnp.sum(v, axis=0, keepdims=True)


def _lru_proj_bwd(dxp, dga, x, dx1, g_a, w_in_t, *, name, bt=512):
    t, d = x.shape
    bt = min(bt, t)
    n = w_in_t.shape[0] // 2

    def body(dxp_ref, dga_ref, x_ref, dx1_ref, g_ref, wt_ref, dx_ref, dg_ref):
        @pl.when(pl.program_id(0) == 0)
        def _():
            dg_ref[...] = jnp.zeros_like(dg_ref)

        dh = (jnp.dot(dxp_ref[...], wt_ref[0:n, :], preferred_element_type=F32)
              + jnp.dot(dga_ref[...], wt_ref[n:2 * n, :], preferred_element_type=F32))
        dxn, nrm = _rms_bwd_rows(x_ref[...], dh * g_ref[...])
        dg_ref[...] += _col_sum(dh * nrm)
        dx_ref[...] = dx1_ref[...] + dxn

    row = lambda w: pl.BlockSpec((bt, w), lambda i: (i, 0))
    return pl.pallas_call(
        body, grid=(t // bt,),
        in_specs=[row(n), row(n), row(d), row(d), _const_spec(g_a), _const_spec(w_in_t)],
        out_specs=[row(d), _const_spec(g_a)],
        out_shape=[jax.ShapeDtypeStruct((t, d), F32), jax.ShapeDtypeStruct((1, d), F32)],
        compiler_params=_params(("arbitrary",)), name=name)(dxp, dga, x, dx1, g_a, w_in_t)


def _mla_proj_bwd(x1, dx2, cqp, ck, dq, dkn, dv, dkr, dg2, gains, w, *, name, bt=512):
    t, d = x1.shape
    bt = min(bt, t)
    g_kv, g_b, g_kvn, g_q = gains
    consts = [g_kv, g_b, g_kvn, g_q, w["w_dkv_c"], w["w_dkv_r"], w["w_in_b_t"], w["w_uk"], w["w_uv"], w["w_uq"]]
    nh = N_HEADS * V_DIM

    def body(x1_ref, dx2_ref, cqp_ref, ck_ref, dq_ref, dkn_ref, dv_ref, dkr_ref, dg2_ref,
             gkv_ref, gb_ref, gkvn_ref, gq_ref, wdc_ref, wdr_ref, wbt_ref, wuk_ref, wuv_ref, wuq_ref,
             dx1_ref, du2_ref, dckr_ref, dgkv_ref, dgb_ref, dgkvn_ref, dgq_ref):
        @pl.when(pl.program_id(0) == 0)
        def _():
            for ref in (dgkv_ref, dgb_ref, dgkvn_ref, dgq_ref):
                ref[...] = jnp.zeros_like(ref)

        dot_nt = lambda a, b: lax.dot_general(a, b, _NT, preferred_element_type=F32)
        dcq = dot_nt(dq_ref[...], wuq_ref[...])
        dcqp, nq = _rms_bwd_rows(cqp_ref[...], dcq * gq_ref[...])
        dgq_ref[...] += _col_sum(dcq * nq)
        dcqp = dcqp.astype(BF16)
        dg2 = dg2_ref[...]
        du2_ref[:, :Q_RANK] = dcqp
        du2_ref[:, Q_RANK:] = dg2
        dhq = (jnp.dot(dcqp, wbt_ref[0:Q_RANK, :], preferred_element_type=F32)
               + jnp.dot(dg2, wbt_ref[Q_RANK:, :], preferred_element_type=F32))
        dckv = dot_nt(dkn_ref[...], wuk_ref[...]) + dot_nt(dv_ref[...], wuv_ref[...])
        dck, nc = _rms_bwd_rows(ck_ref[...], dckv * gkvn_ref[...])
        dgkvn_ref[...] += _col_sum(dckv * nc)
        dck = dck.astype(BF16)
        dkr = dkr_ref[...].astype(BF16)
        dckr_ref[:, :KV_RANK] = dck
        dckr_ref[:, KV_RANK:] = dkr
        dhk = dot_nt(dck, wdc_ref[...]) + dot_nt(dkr, wdr_ref[...])
        dxn, n1 = _rms_bwd_rows(x1_ref[...], dhq * gb_ref[...] + dhk * gkv_ref[...])
        dgb_ref[...] += _col_sum(dhq * n1)
        dgkv_ref[...] += _col_sum(dhk * n1)
        dx1_ref[...] = dx2_ref[...] + dxn

    row = lambda w_: pl.BlockSpec((bt, w_), lambda i: (i, 0))
    vec = lambda w_: pl.BlockSpec((1, w_), lambda i: (0, 0))
    in_specs = [row(d), row(d), row(Q_RANK), row(KV_RANK), row(N_HEADS * HEAD_PAD), row(nh), row(nh), row(LANES), row(nh)]
    return pl.pallas_call(
        body, grid=(t // bt,), in_specs=in_specs + [_const_spec(a) for a in consts],
        out_specs=[row(d), row(Q_RANK + nh), row(KV_RANK + LANES), vec(d), vec(d), vec(KV_RANK), vec(Q_RANK)],
        out_shape=[jax.ShapeDtypeStruct((t, d), F32), jax.ShapeDtypeStruct((t, Q_RANK + nh), BF16),
                   jax.ShapeDtypeStruct((t, KV_RANK + LANES), BF16), jax.ShapeDtypeStruct((1, d), F32),
                   jax.ShapeDtypeStruct((1, d), F32), jax.ShapeDtypeStruct((1, KV_RANK), F32),
                   jax.ShapeDtypeStruct((1, Q_RANK), F32)],
        compiler_params=_params(("arbitrary",)), name=name)(x1, dx2, cqp, ck, dq, dkn, dv, dkr, dg2, *consts)


def _softplus(z):
    return jnp.maximum(z, 0.0) + jnp.log1p(jnp.exp(-jnp.abs(z)))


def _one_minus_square(a, la):
    return jnp.tanh(-la) * (1.0 + a * a)


def _gates(xb, wrg, wig, brg, big, sp):
    xbb = xb.astype(BF16)
    r = _sigmoid_tail(jnp.dot(xbb, wrg, preferred_element_type=F32) + brg)
    i = _sigmoid(jnp.dot(xbb, wig, preferred_element_type=F32) + big)
    la = (-LRU_C) * r * sp
    a = jnp.exp(la)
    em = _one_minus_square(a, la)
    inv_mult = lax.rsqrt(em)
    mult = jnp.where(em > 0.0, em * inv_mult, 0.0)
    return r, i, a, mult, inv_mult


def _conv(xpad_ref, cw_ref, seq):
    acc = cw_ref[0:1, :] * xpad_ref[pl.ds(8 - (CONV_WIDTH - 1), seq), :]
    for k in range(1, CONV_WIDTH):
        acc = acc + cw_ref[k:k + 1, :] * xpad_ref[pl.ds(8 - (CONV_WIDTH - 1) + k, seq), :]
    return acc


def _seq_spec(seq):
    return pl.BlockSpec((None, seq, RNN_BW), lambda n, b: (b, 0, n))


def _chan_spec(rows):
    return pl.BlockSpec((rows, RNN_BW), lambda n, b: (0, n))


_GATE_W_SPEC = pl.BlockSpec((None, RNN_BW, RNN_BW), lambda n, b: (n, 0, 0))


SCAN_UNROLL = 4


def _peers():
    x, y, c = _mesh_pos()
    others = []
    for k in range(1, N_DEV):
        px = 1 - x if k & 4 else x
        py = 1 - y if k & 2 else y
        pc = 1 - c if k & 1 else c
        others.append(((px, py, pc), 4 * px + 2 * py + pc))
    return 4 * x + 2 * y + c, others


def _exchange(src_ref, dst_ref, send_sems, recv_sems, local_sem, *, finish, gather=False):
    me, others = _peers()

    def send(k, dev, slot):
        return pltpu.make_async_remote_copy(
            src_ref=src_ref if gather else src_ref.at[slot], dst_ref=dst_ref.at[me], send_sem=send_sems.at[k],
            recv_sem=recv_sems.at[k], device_id=dev, device_id_type=pl.DeviceIdType.MESH)

    local = pltpu.make_async_copy(src_ref if gather else src_ref.at[me], dst_ref.at[me], local_sem)
    if not finish:
        local.start()
        for k, (dev, slot) in enumerate(others):
            send(k, dev, slot).start()
        return
    for k, (dev, slot) in enumerate(others):
        pltpu.make_async_remote_copy(
            src_ref=dst_ref.at[slot], dst_ref=dst_ref.at[slot], send_sem=send_sems.at[k], recv_sem=recv_sems.at[k],
            device_id=dev, device_id_type=pl.DeviceIdType.MESH).wait_recv()
    for k, (dev, slot) in enumerate(others):
        send(k, dev, slot).wait_send()
    local.wait()


def _gather_two_level(x_ref, out_ref, send_sems, recv_sems, local_sem, *, phase):
    x, y, c = _mesh_pos()
    me, sibling = (x, y, c), (x, y, 1 - c)
    chips = [(1 - x, y), (x, 1 - y), (1 - x, 1 - y)]

    def slot(px, py, pc):
        return out_ref.at[4 * px + 2 * py + pc]

    def copy(k, blk, to, src=None):
        return pltpu.make_async_remote_copy(
            src_ref=slot(*blk) if src is None else src, dst_ref=slot(*blk),
            send_sem=send_sems.at[k], recv_sem=recv_sems.at[k], device_id=to, device_id_type=pl.DeviceIdType.MESH)

    if phase == 0:
        pltpu.make_async_copy(x_ref, slot(*me), local_sem).start()
        copy(0, me, sibling, src=x_ref).start()
        for j, chip in enumerate(chips):
            copy(1 + j, me, (*chip, c), src=x_ref).start()
    elif phase == 1:
        for j, chip in enumerate(chips):
            copy(1 + j, (*chip, c), me).wait_recv()
            copy(4 + j, (*chip, c), sibling).start()
    else:
        copy(0, sibling, me).wait_recv()
        for j, chip in enumerate(chips):
            copy(4 + j, (*chip, 1 - c), me).wait_recv()
        copy(0, me, sibling, src=x_ref).wait_send()
        for j, chip in enumerate(chips):
            copy(1 + j, me, (*chip, c), src=x_ref).wait_send()
            copy(4 + j, (*chip, c), sibling).wait_send()
        pltpu.make_async_copy(x_ref, slot(*me), local_sem).wait()


GATHER_FORWARD_STEP = 9
_EXCHANGE_SEMS = [pltpu.SemaphoreType.DMA((N_DEV - 1,)), pltpu.SemaphoreType.DMA((N_DEV - 1,)), pltpu.SemaphoreType.DMA(())]


def _first_last(steps):
    first = last = None
    for axis, n in enumerate(steps):
        i = pl.program_id(axis)
        first = (i == 0) if first is None else first & (i == 0)
        last = (i == n - 1) if last is None else last & (i == n - 1)
    return first, last


def _lru_fwd(xp, ga, cw, vecs, wrg, wig, block, *, name):
    bsz, seq, _ = xp.shape
    groups = seq // 8

    def body(xp_ref, ga_ref, cw_ref, vec_ref, wrg_ref, wig_ref, blk_ref, xb_ref, hs_ref, y_ref, all_ref,
             xpad, a_s, b_s, send_sems, recv_sems, local_sem):
        first, last = _first_last((RNN_BLOCKS, bsz))

        @pl.when(first)
        def _():
            _gather_two_level(blk_ref, all_ref, send_sems, recv_sems, local_sem, phase=0)

        @pl.when((pl.program_id(0) == GATHER_FORWARD_STEP) & (pl.program_id(1) == 0))
        def _():
            _gather_two_level(blk_ref, all_ref, send_sems, recv_sems, local_sem, phase=1)

        xpad[0:8, :] = jnp.zeros((8, RNN_BW), F32)
        xpad[pl.ds(8, seq), :] = xp_ref[...]
        xb = _conv(xpad, cw_ref, seq) + vec_ref[0:1, :]
        xb_ref[...] = xb
        sp = _softplus(-vec_ref[3:4, :])
        _, i, a, mult, _ = _gates(xb, wrg_ref[...], wig_ref[...], vec_ref[1:2, :], vec_ref[2:3, :], sp)
        a_s[...] = a
        b_s[...] = mult * (i * xb)
        row = lax.broadcasted_iota(jnp.int32, (8, RNN_BW), 0)

        def group(g, h):
            r0 = pl.multiple_of(g * 8, 8)
            av = a_s[pl.ds(r0, 8), :]
            bv = b_s[pl.ds(r0, 8), :]
            for k in (1, 2, 4):
                m = row >= k
                bv = jnp.where(m, av * pltpu.roll(bv, k, axis=0) + bv, bv)
                av = jnp.where(m, av * pltpu.roll(av, k, axis=0), av)
            hs_ref[pl.ds(r0, 8), :] = av * h + bv
            return av[7:8, :] * h + bv[7:8, :]

        def groups_of(i, h):
            for u in range(SCAN_UNROLL):
                h = group(i * SCAN_UNROLL + u, h)
            return h

        lax.fori_loop(0, groups // SCAN_UNROLL, groups_of, jnp.zeros((1, RNN_BW), F32))
        gav = ga_ref[...]
        y_ref[...] = (hs_ref[...] * (gav * _sigmoid(gav))).astype(BF16)

        @pl.when(last)
        def _():
            _gather_two_level(blk_ref, all_ref, send_sems, recv_sems, local_sem, phase=2)

    sq = _seq_spec(seq)
    shape = (bsz, seq, D_RNN)
    return pl.pallas_call(
        body, grid=(RNN_BLOCKS, bsz),
        in_specs=[sq, sq, _chan_spec(8), _chan_spec(8), _GATE_W_SPEC, _GATE_W_SPEC, _ANY],
        out_specs=[sq, sq, sq, _ANY],
        out_shape=[jax.ShapeDtypeStruct(shape, F32), jax.ShapeDtypeStruct(shape, F32), jax.ShapeDtypeStruct(shape, BF16),
                   jax.ShapeDtypeStruct((N_DEV,) + block.shape, block.dtype)],
        scratch_shapes=[pltpu.VMEM((seq + 8, RNN_BW), F32), pltpu.VMEM((seq, RNN_BW), F32), pltpu.VMEM((seq, RNN_BW), F32)]
        + _EXCHANGE_SEMS,
        compiler_params=_params(("arbitrary", "arbitrary")), name=name)(xp, ga, cw, vecs, wrg, wig, block)


def _lru_bwd(dy, xp, xb, hs, ga, cw, vecs, wrg, wig, parts, *, name):
    bsz, seq, _ = xp.shape
    groups = seq // 8

    def body(dy_ref, xp_ref, xb_ref, hs_ref, ga_ref, cw_ref, vec_ref, wrg_ref, wig_ref,
             parts_ref, dxp_ref, dga_ref, dwrg_ref, dwig_ref, dvec_ref, land_ref, pad, a_s, d_s, lam_s,
             send_sems, recv_sems, local_sem):
        first, last = _first_last((RNN_BLOCKS, bsz))

        @pl.when(first)
        def _():
            _exchange(parts_ref, land_ref, send_sems, recv_sems, local_sem, finish=False)

        @pl.when(pl.program_id(1) == 0)
        def _():
            dwrg_ref[...] = jnp.zeros_like(dwrg_ref)
            dwig_ref[...] = jnp.zeros_like(dwig_ref)
            dvec_ref[...] = jnp.zeros_like(dvec_ref)

        xb = xb_ref[...]
        hs = hs_ref[...]
        gav = ga_ref[...]
        dy = dy_ref[...]
        sp = _softplus(-vec_ref[3:4, :])
        wrg = wrg_ref[...]
        wig = wig_ref[...]
        r, i, a, mult, inv_mult = _gates(xb, wrg, wig, vec_ref[1:2, :], vec_ref[2:3, :], sp)
        sg = _sigmoid(gav)
        dga_ref[...] = (dy * hs * (sg * (1.0 + gav * (1.0 - sg)))).astype(BF16)
        d_s[...] = dy * (gav * sg)

        pad[pl.ds(0, seq), :] = a
        pad[pl.ds(seq, 8), :] = jnp.zeros((8, RNN_BW), F32)
        a_s[...] = pad[pl.ds(1, seq), :]
        row = lax.broadcasted_iota(jnp.int32, (8, RNN_BW), 0)

        def group(g, nxt):
            r0 = pl.multiple_of((groups - 1 - g) * 8, 8)
            cv = a_s[pl.ds(r0, 8), :]
            bv = d_s[pl.ds(r0, 8), :]
            for k in (1, 2, 4):
                m = row < 8 - k
                bv = jnp.where(m, cv * pltpu.roll(bv, 8 - k, axis=0) + bv, bv)
                cv = jnp.where(m, cv * pltpu.roll(cv, 8 - k, axis=0), cv)
            lam_s[pl.ds(r0, 8), :] = cv * nxt + bv
            return cv[0:1, :] * nxt + bv[0:1, :]

        def groups_of(i, nxt):
            for u in range(SCAN_UNROLL):
                nxt = group(i * SCAN_UNROLL + u, nxt)
            return nxt

        lax.fori_loop(0, groups // SCAN_UNROLL, groups_of, jnp.zeros((1, RNN_BW), F32))
        dh = lam_s[...]

        pad[0:8, :] = jnp.zeros((8, RNN_BW), F32)
        pad[pl.ds(8, seq), :] = hs
        da = dh * pad[pl.ds(7, seq), :]
        ixb = i * xb
        dixb = dh * mult
        dla = da * a - (dh * ixb) * (a * a) * inv_mult
        drp = (dla * ((-LRU_C) * sp)) * r * (1.0 - r)
        dip = (dixb * xb) * i * (1.0 - i)
        dvec_ref[0:1, :] += jnp.sum(drp, axis=0, keepdims=True)
        dvec_ref[1:2, :] += jnp.sum(dip, axis=0, keepdims=True)
        dvec_ref[2:3, :] += jnp.sum(dla * ((-LRU_C) * r), axis=0, keepdims=True)
        drpb = drp.astype(BF16)
        dipb = dip.astype(BF16)
        xbb = xb.astype(BF16)
        nt = (((1,), (1,)), ((), ()))
        tn = (((0,), (0,)), ((), ()))
        dxb = (dixb * i
               + lax.dot_general(drpb, wrg, nt, preferred_element_type=F32)
               + lax.dot_general(dipb, wig, nt, preferred_element_type=F32))
        dwrg_ref[...] += lax.dot_general(xbb, drpb, tn, preferred_element_type=F32)
        dwig_ref[...] += lax.dot_general(xbb, dipb, tn, preferred_element_type=F32)
        dvec_ref[3:4, :] += jnp.sum(dxb, axis=0, keepdims=True)

        pad[pl.ds(0, seq), :] = dxb
        pad[pl.ds(seq, 8), :] = jnp.zeros((8, RNN_BW), F32)
        dxp = cw_ref[0:1, :] * pad[pl.ds(CONV_WIDTH - 1, seq), :]
        for k in range(1, CONV_WIDTH):
            dxp = dxp + cw_ref[k:k + 1, :] * pad[pl.ds(CONV_WIDTH - 1 - k, seq), :]
        dxp_ref[...] = dxp.astype(BF16)
        pad[0:8, :] = jnp.zeros((8, RNN_BW), F32)
        pad[pl.ds(8, seq), :] = xp_ref[...]
        for k in range(CONV_WIDTH):
            dvec_ref[4 + k:5 + k, :] += jnp.sum(dxb * pad[pl.ds(8 - (CONV_WIDTH - 1) + k, seq), :], axis=0, keepdims=True)

        @pl.when(last)
        def _():
            _exchange(parts_ref, land_ref, send_sems, recv_sems, local_sem, finish=True)

    sq = _seq_spec(seq)
    shape = (bsz, seq, D_RNN)
    gshape = (RNN_BLOCKS, RNN_BW, RNN_BW)
    return pl.pallas_call(
        body, grid=(RNN_BLOCKS, bsz),
        in_specs=[sq, sq, sq, sq, sq, _chan_spec(8), _chan_spec(8), _GATE_W_SPEC, _GATE_W_SPEC, _ANY],
        out_specs=[sq, sq, _GATE_W_SPEC, _GATE_W_SPEC, _chan_spec(8), _ANY],
        out_shape=[jax.ShapeDtypeStruct(shape, BF16), jax.ShapeDtypeStruct(shape, BF16),
                   jax.ShapeDtypeStruct(gshape, F32), jax.ShapeDtypeStruct(gshape, F32),
                   jax.ShapeDtypeStruct((8, D_RNN), F32), jax.ShapeDtypeStruct(parts.shape, parts.dtype)],
        scratch_shapes=[pltpu.VMEM((seq + 8, RNN_BW), F32), pltpu.VMEM((seq, RNN_BW), F32),
                        pltpu.VMEM((seq, RNN_BW), F32), pltpu.VMEM((seq, RNN_BW), F32)] + _EXCHANGE_SEMS,
        compiler_params=_params(("arbitrary", "arbitrary")), name=name)(dy, xp, xb, hs, ga, cw, vecs, wrg, wig, parts)


def _attn_block(seq):
    return min(512, seq)


def _causal_mask(keys, queries, shift):
    shape = (keys, queries)
    return lax.broadcasted_iota(jnp.int32, shape, 0) <= lax.broadcasted_iota(jnp.int32, shape, 1) + shift


FWD_HEADS = 8
BWD_HEADS = 2


def _attn_fwd(q, kn, kr, v_t, block, *, bsz, seq, name):
    t = bsz * seq
    blk = _attn_block(seq)
    half = blk // 2
    nq = seq // blk
    hg = FWD_HEADS
    steps = (bsz, N_HEADS // hg, nq)

    def body(q_ref, kn_ref, kr_ref, vt_ref, blk_ref, o_ref, lse_ref, all_ref, acc, send_sems, recv_sems, local_sem):
        first, last = _first_last(steps)

        @pl.when(first)
        def _():
            _exchange(blk_ref, all_ref, send_sems, recv_sems, local_sem, finish=False, gather=True)

        qi = pl.program_id(2)
        acc[...] = jnp.zeros_like(acc)

        def tile(k0, keys, cols, carry, shift=None):
            kr_j = kr_ref[pl.ds(k0, keys), :]
            out = []
            for h in range(hg):
                m_i, l_i = carry[h]
                kv = jnp.concatenate([kn_ref[pl.ds(k0, keys), h * QK_NOPE:(h + 1) * QK_NOPE], kr_j], axis=1)
                qv = q_ref[cols, h * HEAD_PAD:(h + 1) * HEAD_PAD]
                s = lax.dot_general(kv, qv, _NT, preferred_element_type=F32) * ATTN_SCALE
                if shift is not None:
                    s = jnp.where(_causal_mask(keys, cols.stop - cols.start, shift), s, -jnp.inf)
                m_new = jnp.maximum(m_i, jnp.max(s, axis=0, keepdims=True))
                p = jnp.exp(s - m_new)
                alpha = jnp.exp(m_i - m_new)
                l_new = alpha * l_i + jnp.sum(p, axis=0, keepdims=True)
                acc[h, :, cols] = alpha * acc[h, :, cols] + jnp.dot(
                    vt_ref[h * V_DIM:(h + 1) * V_DIM, pl.ds(k0, keys)], p.astype(BF16), preferred_element_type=F32)
                out.append((m_new, l_new))
            return tuple(out)

        init = tuple((jnp.full((1, blk), -jnp.inf, F32), jnp.zeros((1, blk), F32)) for _ in range(hg))
        carry = lax.fori_loop(0, qi, lambda j, c: tile(pl.multiple_of(j * blk, blk), blk, slice(0, blk), c), init)
        k0 = pl.multiple_of(qi * blk, blk)
        early = tile(k0, half, slice(0, half), tuple((m[:, :half], l[:, :half]) for m, l in carry), shift=0)
        late = tile(k0, blk, slice(half, blk), tuple((m[:, half:], l[:, half:]) for m, l in carry), shift=half)
        stats = tuple(tuple(jnp.concatenate(pair, axis=1) for pair in zip(a, b)) for a, b in zip(early, late))
        for h in range(hg):
            m_i, l_i = stats[h]
            o_ref[:, h * V_DIM:(h + 1) * V_DIM] = (acc[h] / l_i).T
            lse_ref[h] = m_i + jnp.log(l_i)

        @pl.when(last)
        def _():
            _exchange(blk_ref, all_ref, send_sems, recv_sems, local_sem, finish=True, gather=True)

    return pl.pallas_call(
        body, grid=steps,
        in_specs=[pl.BlockSpec((blk, hg * HEAD_PAD), lambda b, g, i: (b * nq + i, g)),
                  pl.BlockSpec((seq, hg * QK_NOPE), lambda b, g, i: (b, g)),
                  pl.BlockSpec((seq, LANES), lambda b, g, i: (b, 0)),
                  pl.BlockSpec((hg * V_DIM, seq), lambda b, g, i: (g, b)), _ANY],
        out_specs=[pl.BlockSpec((blk, hg * V_DIM), lambda b, g, i: (b * nq + i, g)),
                   pl.BlockSpec((hg, 1, blk), lambda b, g, i: (g, 0, b * nq + i)), _ANY],
        out_shape=[jax.ShapeDtypeStruct((t, N_HEADS * V_DIM), F32), jax.ShapeDtypeStruct((N_HEADS, 1, t), F32),
                   jax.ShapeDtypeStruct((N_DEV,) + block.shape, block.dtype)],
        scratch_shapes=[pltpu.VMEM((hg, V_DIM, blk), F32)] + _EXCHANGE_SEMS,
        compiler_params=_params(("arbitrary", "arbitrary", "arbitrary")), name=name)(q, kn, kr, v_t, block)


def _attn_bwd(q, kn, kr, kn_t, kr_t, v, o, lse, do, cos, sin, parts, *, bsz, seq, name):
    t = bsz * seq
    blk = _attn_block(seq)
    half = blk // 2
    nq = seq // blk
    hg = BWD_HEADS
    steps = (bsz, N_HEADS // hg)

    def body(q_ref, kn_ref, kr_ref, knt_ref, krt_ref, v_ref, o_ref, lse_ref, do_ref, cos_ref, sin_ref, parts_ref,
             dq_ref, dkn_ref, dkr_ref, dv_ref, land_ref, dqt_acc, dk_acc, dv_acc, send_sems, recv_sems, local_sem):
        first, last = _first_last(steps)

        @pl.when(first)
        def _():
            _exchange(parts_ref, land_ref, send_sems, recv_sems, local_sem, finish=False)

        dqt_acc[...] = jnp.zeros_like(dqt_acc)
        dk_acc[...] = jnp.zeros_like(dk_acc)
        dv_acc[...] = jnp.zeros_like(dv_acc)

        def q_block(i, _):
            q0 = pl.multiple_of(i * blk, blk)
            rows = []
            for h in range(hg):
                dov = do_ref[pl.ds(q0, blk), h * V_DIM:(h + 1) * V_DIM].astype(F32)
                dcol = jnp.sum(dov * o_ref[pl.ds(q0, blk), h * V_DIM:(h + 1) * V_DIM], axis=-1, keepdims=True)
                delta = jnp.broadcast_to(dcol, (blk, LANES)).T[0:1, :]
                rows.append((lse_ref[h, :, pl.ds(q0, blk)], delta))

            def pair(k0, keys, cols, shift=None):
                n = cols.stop - cols.start
                qs = pl.multiple_of(q0 + cols.start, n)
                kr_j = kr_ref[pl.ds(k0, keys), :]
                krt_j = krt_ref[:, pl.ds(k0, keys)]
                for h in range(hg):
                    lse_i, delta = (r[:, cols] for r in rows[h])
                    qv = q_ref[pl.ds(qs, n), h * HEAD_PAD:(h + 1) * HEAD_PAD]
                    dov = do_ref[pl.ds(qs, n), h * V_DIM:(h + 1) * V_DIM]
                    kv = jnp.concatenate([kn_ref[pl.ds(k0, keys), h * QK_NOPE:(h + 1) * QK_NOPE], kr_j], axis=1)
                    s = lax.dot_general(kv, qv, _NT, preferred_element_type=F32) * ATTN_SCALE
                    p = jnp.exp(s - lse_i)
                    if shift is not None:
                        p = jnp.where(_causal_mask(keys, n, shift), p, 0.0)
                    dv_acc[pl.ds(k0, keys), h * V_DIM:(h + 1) * V_DIM] += jnp.dot(
                        p.astype(BF16), dov, preferred_element_type=F32)
                    dp = lax.dot_general(v_ref[pl.ds(k0, keys), h * V_DIM:(h + 1) * V_DIM], dov, _NT,
                                         preferred_element_type=F32)
                    ds = (p * (dp - delta) * ATTN_SCALE).astype(BF16)
                    dk_acc[pl.ds(k0, keys), h * HEAD_PAD:(h + 1) * HEAD_PAD] += jnp.dot(ds, qv, preferred_element_type=F32)
                    base = h * HEAD_PAD
                    dqt_acc[base:base + QK_NOPE, pl.ds(qs, n)] += jnp.dot(
                        knt_ref[h * QK_NOPE:(h + 1) * QK_NOPE, pl.ds(k0, keys)], ds, preferred_element_type=F32)
                    dqt_acc[base + QK_NOPE:base + HEAD_PAD, pl.ds(qs, n)] += jnp.dot(
                        krt_j, ds, preferred_element_type=F32)

            def off_diagonal(j, _):
                pair(pl.multiple_of(j * blk, blk), blk, slice(0, blk))
                return 0

            lax.fori_loop(0, i, off_diagonal, 0)
            pair(q0, half, slice(0, half), shift=0)
            pair(q0, blk, slice(half, blk), shift=half)
            return 0

        lax.fori_loop(0, nq, q_block, 0)
        dkr = jnp.zeros((seq, LANES), F32)
        for h in range(hg):
            base = h * HEAD_PAD
            for i in range(nq):
                rows = slice(i * blk, (i + 1) * blk)
                dq = dqt_acc[base:base + HEAD_PAD, rows].T
                dq_ref[rows, base:base + QK_NOPE] = dq[:, :QK_NOPE].astype(BF16)
                dq_ref[rows, base + QK_NOPE:base + HEAD_PAD] = _rope_t(
                    dq[:, QK_NOPE:], cos_ref[rows, :], sin_ref[rows, :]).astype(BF16)
            dkn_ref[:, h * QK_NOPE:(h + 1) * QK_NOPE] = dk_acc[:, base:base + QK_NOPE].astype(BF16)
            dkr = dkr + dk_acc[:, base + QK_NOPE:base + HEAD_PAD]
        dv_ref[...] = dv_acc[...].astype(BF16)

        @pl.when(pl.program_id(1) == 0)
        def _():
            dkr_ref[...] = jnp.zeros_like(dkr_ref)

        dkr_ref[...] += _rope_t(dkr, cos_ref[...], sin_ref[...])

        @pl.when(last)
        def _():
            _exchange(parts_ref, land_ref, send_sems, recv_sems, local_sem, finish=True)

    head = pl.BlockSpec((seq, hg * V_DIM), lambda b, g: (b, g))
    head_t = pl.BlockSpec((hg * V_DIM, seq), lambda b, g: (g, b))
    shared = pl.BlockSpec((seq, LANES), lambda b, g: (b, 0))
    shared_t = pl.BlockSpec((LANES, seq), lambda b, g: (0, b))
    table = pl.BlockSpec((seq, LANES), lambda b, g: (0, 0))
    qspec = pl.BlockSpec((seq, hg * HEAD_PAD), lambda b, g: (b, g))
    return pl.pallas_call(
        body, grid=steps,
        in_specs=[qspec, head, shared, head_t, shared_t, head, head,
                  pl.BlockSpec((hg, 1, seq), lambda b, g: (g, 0, b)), head, table, table, _ANY],
        out_specs=[qspec, head, shared, head, _ANY],
        out_shape=[jax.ShapeDtypeStruct((t, N_HEADS * HEAD_PAD), BF16), jax.ShapeDtypeStruct((t, N_HEADS * QK_NOPE), BF16),
                   jax.ShapeDtypeStruct((t, LANES), F32), jax.ShapeDtypeStruct((t, N_HEADS * V_DIM), BF16),
                   jax.ShapeDtypeStruct(parts.shape, parts.dtype)],
        scratch_shapes=[pltpu.VMEM((hg * HEAD_PAD, seq), F32), pltpu.VMEM((seq, hg * HEAD_PAD), F32),
                        pltpu.VMEM((seq, hg * V_DIM), F32)] + _EXCHANGE_SEMS,
        compiler_params=_params(("arbitrary", "arbitrary")), name=name)(
            q, kn, kr, kn_t, kr_t, v, o, lse, do, cos, sin, parts)


def _head_and_loss(o, g2, x1, target, w_out, g_final, *, name, bt=512):
    t, d = x1.shape
    bt = min(bt, t)
    nt = (((1,), (1,)), ((), ()))

    def body(o_ref, g2_ref, x1_ref, tgt_ref, w_ref, gf_ref, loss_ref, dx2_ref, y2_ref, do_ref, dg2_ref, dgf_ref):
        @pl.when(pl.program_id(0) == 0)
        def _():
            loss_ref[...] = jnp.zeros_like(loss_ref)
            dgf_ref[...] = jnp.zeros_like(dgf_ref)

        ov = o_ref[...]
        gv = g2_ref[...]
        sg = _sigmoid(gv)
        silu = gv * sg
        y2 = (ov * silu).astype(BF16)
        y2_ref[...] = y2
        w = w_ref[...]
        x2 = x1_ref[...] + jnp.dot(y2, w, preferred_element_type=F32)
        r = lax.rsqrt(jnp.mean(x2 * x2, axis=-1, keepdims=True) + EPS)
        nrm = x2 * r
        gf = gf_ref[...]
        err = nrm * gf - tgt_ref[...]
        loss_ref[...] += 0.5 * jnp.sum(jnp.mean(err * err, axis=-1, keepdims=True))
        dyf = err * (1.0 / d)
        dgf_ref[...] += jnp.sum(dyf * nrm, axis=0, keepdims=True)
        dn = dyf * gf
        dx2 = r * (dn - nrm * jnp.mean(dn * nrm, axis=-1, keepdims=True))
        dx2_ref[...] = dx2
        dy2 = lax.dot_general(dx2.astype(BF16), w, nt, preferred_element_type=F32)
        do_ref[...] = (dy2 * silu).astype(BF16)
        dg2_ref[...] = (dy2 * ov * (sg * (1.0 + gv * (1.0 - sg)))).astype(BF16)

    row = pl.BlockSpec((bt, d), lambda i: (i, 0))
    vec = pl.BlockSpec((1, d), lambda i: (0, 0))
    return pl.pallas_call(
        body, grid=(t // bt,),
        in_specs=[row, row, row, row, pl.BlockSpec((d, d), lambda i: (0, 0)), vec],
        out_specs=[pl.BlockSpec((8, LANES), lambda i: (0, 0)), row, row, row, row, vec],
        out_shape=[jax.ShapeDtypeStruct((8, LANES), F32), jax.ShapeDtypeStruct((t, d), F32),
                   jax.ShapeDtypeStruct((t, d), BF16), jax.ShapeDtypeStruct((t, d), BF16),
                   jax.ShapeDtypeStruct((t, d), BF16), jax.ShapeDtypeStruct((1, d), F32)],
        compiler_params=_params(("arbitrary",)), name=name)(o, g2, x1, target, w_out, g_final)


def _sum_parts(parts, *, name, br=GRAD_BLOCK):
    npart, rows, w = parts.shape

    def body(p_ref, o_ref):
        acc = p_ref[0].astype(F32)
        for j in range(1, npart):
            acc = acc + p_ref[j].astype(F32)
        o_ref[...] = acc

    return pl.pallas_call(
        body, grid=(rows // br,), in_specs=[pl.BlockSpec((npart, br, w), lambda i: (0, i, 0))],
        out_specs=pl.BlockSpec((br, w), lambda i: (i, 0)), out_shape=jax.ShapeDtypeStruct((rows, w), F32),
        compiler_params=_params(("parallel",)), name=name)(parts)


def _chip_partial(parts, recv, *, name, br=GRAD_BLOCK):
    _, rows, w = parts.shape
    core = lax.axis_index("c").astype(jnp.int32).reshape(1)

    def body(c_ref, p_ref, r_ref, o_ref):
        o_ref[...] = (p_ref[...] + r_ref[...]).astype(BF16)

    grid_spec = pltpu.PrefetchScalarGridSpec(
        num_scalar_prefetch=1, grid=(4, rows // br),
        in_specs=[pl.BlockSpec((None, br, w), lambda k, i, c_ref: (2 * k + c_ref[0], i, 0)),
                  pl.BlockSpec((None, br, w), lambda k, i, c_ref: (k, i, 0))],
        out_specs=pl.BlockSpec((None, br, w), lambda k, i, c_ref: (k, i, 0)))
    return pl.pallas_call(
        body, grid_spec=grid_spec, out_shape=jax.ShapeDtypeStruct((4, rows, w), BF16),
        compiler_params=_params(("parallel", "parallel")), name=name)(core, parts, recv)


def _as_block(a):
    if a.ndim == 1:
        return a.reshape(1, -1)
    if a.ndim > 2 and a.shape[0] == 1:
        return a.reshape(a.shape[1:])
    return a


def _adamw(g, w, m, v, *, name):
    shape = w.shape
    g, w, m, v = (_as_block(a) for a in (g, w, m, v))

    def body(g_ref, w_ref, m_ref, v_ref, d_ref, nm_ref, nv_ref):
        gv = g_ref[...]
        nm = ADAM_B1 * m_ref[...] + (1.0 - ADAM_B1) * gv
        nv = ADAM_B2 * v_ref[...] + (1.0 - ADAM_B2) * (gv * gv)
        nm_ref[...] = nm
        nv_ref[...] = nv
        m_hat = nm / (1.0 - ADAM_B1 ** ADAM_STEP)
        v_hat = nv / (1.0 - ADAM_B2 ** ADAM_STEP)
        d_ref[...] = (-ADAM_LR) * (m_hat / (jnp.sqrt(v_hat) + ADAM_EPS) + ADAM_WD * w_ref[...])

    whole = pl.BlockSpec(memory_space=pltpu.VMEM)
    outs = pl.pallas_call(
        body, in_specs=[whole] * 4, out_specs=[whole] * 3, out_shape=[jax.ShapeDtypeStruct(w.shape, F32)] * 3,
        compiler_params=_params(), name=name)(g, w, m, v)
    return [o.reshape(shape) for o in outs]


def _all_gather(block, *, name):
    m, n = block.shape

    def body(x_ref, out_ref, send_sems, recv_sems, local_sem):
        for phase in range(3):
            _gather_two_level(x_ref, out_ref, send_sems, recv_sems, local_sem, phase=phase)

    return pl.pallas_call(
        body, out_shape=jax.ShapeDtypeStruct((N_DEV, m, n), block.dtype), in_specs=[_ANY], out_specs=_ANY,
        scratch_shapes=_EXCHANGE_SEMS, name=name)(block)


def _exchange_d2d(parts, *, name):
    _, rows, w = parts.shape

    def body(p_ref, land_ref, send_sems, recv_sems):
        x, y, c = _mesh_pos()
        sends = []
        for k in range(4):
            cp = pltpu.make_async_remote_copy(
                src_ref=p_ref.at[2 * k + (1 - c)], dst_ref=land_ref.at[k], send_sem=send_sems.at[k],
                recv_sem=recv_sems.at[k], device_id=(x, y, 1 - c), device_id_type=pl.DeviceIdType.MESH)
            cp.start()
            sends.append(cp)
        for cp in sends:
            cp.wait_recv()
        for cp in sends:
            cp.wait_send()

    return pl.pallas_call(
        body, out_shape=jax.ShapeDtypeStruct((4, rows, w), parts.dtype), in_specs=[_ANY], out_specs=_ANY,
        scratch_shapes=[pltpu.SemaphoreType.DMA((4,)), pltpu.SemaphoreType.DMA((4,))], name=name)(parts)


def _exchange_ici(parts, *, name):
    def body(p_ref, land_ref, send_sems, recv_sems, local_sem):
        x, y, c = _mesh_pos()
        mine = pltpu.make_async_copy(p_ref.at[2 * x + y], land_ref.at[3], local_sem)
        mine.start()
        sends = []
        for k, (px, py) in enumerate([(1 - x, y), (x, 1 - y), (1 - x, 1 - y)]):
            cp = pltpu.make_async_remote_copy(
                src_ref=p_ref.at[2 * px + py], dst_ref=land_ref.at[k], send_sem=send_sems.at[k],
                recv_sem=recv_sems.at[k], device_id=(px, py, c), device_id_type=pl.DeviceIdType.MESH)
            cp.start()
            sends.append(cp)
        for cp in sends:
            cp.wait_recv()
        for cp in sends:
            cp.wait_send()
        mine.wait()

    return pl.pallas_call(
        body, out_shape=jax.ShapeDtypeStruct(parts.shape, parts.dtype), in_specs=[_ANY], out_specs=_ANY,
        scratch_shapes=[pltpu.SemaphoreType.DMA((3,)), pltpu.SemaphoreType.DMA((3,)), pltpu.SemaphoreType.DMA(())],
        name=name)(parts)


def _rows(a):
    return a.reshape(-1, PACK_W)


def _pad_to(a, n):
    return jnp.pad(a, (0, n - a.shape[0]))


def _weight_blocks(d):
    small = _rows(_pad_to(jnp.concatenate([d[n].reshape(-1) for n, _ in _SMALL]), 16 * PACK_W))
    bits = lax.bitcast_convert_type(small, jnp.uint32)
    halves = [lax.bitcast_convert_type(h.astype(jnp.uint16), WIRE) for h in (bits >> 16, bits & 0xFFFF)]
    block_a = jnp.concatenate([d["w_in_a"][0].T.astype(WIRE)] + halves, axis=0)
    w_uq = jnp.pad(d["w_uq"][0], ((0, 0), (0, 0), (0, HEAD_PAD - QK_NOPE - QK_ROPE)))
    pieces = {"w_out_a": d["w_out_a"], "w_dkv": d["w_dkv"], "w_uk": d["w_uk"], "w_uv": d["w_uv"],
              "w_in_b": d["w_in_b"][0].T, "w_uq": w_uq}
    block_b = jnp.concatenate([_rows(pieces[n]) for n, _ in _PIECES_B]
                              + [jnp.zeros((WIRE_ROWS_B - MATRIX_ROWS_B, PACK_W), F32)], axis=0).astype(WIRE)
    return block_a, block_b, d["w_out_b"][0].astype(WIRE)


def _weights_a(wall):
    w = {}
    lo, hi = _OFF_A["w_in_a"]
    w["w_in_a_t"] = wall[:, lo:hi].reshape(2 * D_RNN, D_MODEL)
    high, low = (lax.bitcast_convert_type(wall[:, r:r + 16], jnp.uint16).astype(jnp.uint32)
                 for r in (MATRIX_ROWS_A, MATRIX_ROWS_A + 16))
    small = lax.bitcast_convert_type((high << 16) | low, F32)[:, :8].reshape(N_DEV, 8 * PACK_W)
    off = dict(zip([n for n, _ in _SMALL], [0, 128, 768, 928, 1088, 1248]))
    w["norm_a"] = small[:, :128].reshape(1, D_MODEL)

    def by_channel(lo, rows):
        a = small[:, lo:lo + rows * (D_RNN // N_DEV)].reshape(N_DEV, rows, -1).transpose(1, 0, 2).reshape(rows, D_RNN)
        return jnp.pad(a, ((0, 8 - rows), (0, 0)))

    w["conv_taps"] = by_channel(off["conv_w"], CONV_WIDTH)
    w["lru_vecs"] = by_channel(off["conv_b"], 4)
    return w


def _weights_b(wall):
    piece = {n: wall[:, lo:hi] for n, (lo, hi) in _OFF_B.items()}
    w = {"w_out_a": piece["w_out_a"].reshape(D_RNN, D_MODEL)}
    w_dkv = piece["w_dkv"].reshape(D_MODEL, KV_RANK + QK_ROPE)
    w["w_dkv_c"] = w_dkv[:, :KV_RANK]
    w["w_dkv_r"] = jnp.pad(w_dkv[:, KV_RANK:], ((0, 0), (0, LANES - QK_ROPE)))
    w["w_uk"] = piece["w_uk"].reshape(KV_RANK, N_HEADS * QK_NOPE)
    w["w_uv"] = piece["w_uv"].reshape(KV_RANK, N_HEADS * V_DIM)
    w["w_in_b_t"] = piece["w_in_b"].reshape(Q_RANK + N_HEADS * V_DIM, D_MODEL)
    w["w_uq"] = piece["w_uq"].reshape(Q_RANK, N_HEADS * HEAD_PAD)
    return w


def _pack_rep(d):
    flat = jnp.concatenate([d[n].reshape(-1) for n, _ in _REP])
    return _rows(_pad_to(flat, REP_ROWS * PACK_W))


def _unpack_rep(p, like):
    flat = p.reshape(-1)
    out, off = {}, 0
    for n, k in _REP:
        out[n] = flat[off:off + k].reshape(like[n].shape)
        off += k
    return out


def _by_owner(a):
    return a.reshape(N_DEV, -1, PACK_W)


def _grad_parts_b(g):
    tail = jnp.zeros((N_DEV, WIRE_ROWS_B - MATRIX_ROWS_B, PACK_W), F32)
    return jnp.concatenate([_by_owner(g[n]) for n, _ in _PIECES_B] + [tail], axis=1).astype(BF16)


def _grad_parts_a(g):
    small = jnp.concatenate([
        g["norm_a"].reshape(N_DEV, -1),
        g["conv_w"].reshape(CONV_WIDTH, N_DEV, -1).transpose(1, 0, 2).reshape(N_DEV, -1),
        g["conv_b"].reshape(N_DEV, -1), g["b_rg"].reshape(N_DEV, -1), g["b_ig"].reshape(N_DEV, -1),
        g["lru_lambda"].reshape(N_DEV, -1)], axis=1)
    small = jnp.pad(small, ((0, 0), (0, 8 * PACK_W - small.shape[1]))).reshape(N_DEV, 8, PACK_W)
    half = N_DEV // 2
    w_in_a = jnp.concatenate([h.reshape(half, -1, PACK_W) for h in g["w_in_a_t"]], axis=0)
    rep = _pack_rep(g).reshape(N_DEV, REP_SLICE, PACK_W)
    tail = jnp.zeros((N_DEV, GRAD_ROWS_A - MATRIX_ROWS_A - 8 - REP_SLICE, PACK_W), F32)
    return jnp.concatenate([w_in_a, small, rep, tail], axis=1)


def _own_grads(sum_a, sum_b, sum_c):
    out = {}
    lo, hi = _OFF_A["w_in_a"]
    out["w_in_a"] = sum_a[lo:hi].T.reshape(1, D_MODEL, 2 * D_RNN // N_DEV)
    small = sum_a[MATRIX_ROWS_A:MATRIX_ROWS_A + 8].reshape(-1)
    shapes = {"norm_a": (1, D_MODEL // N_DEV), "conv_w": (1, CONV_WIDTH, D_RNN // N_DEV), "conv_b": (1, D_RNN // N_DEV),
              "b_rg": (1, D_RNN // N_DEV), "b_ig": (1, D_RNN // N_DEV), "lru_lambda": (1, D_RNN // N_DEV)}
    off = 0
    for n, k in _SMALL:
        out[n] = small[off:off + k].reshape(shapes[n])
        off += k
    piece = {n: sum_b[lo:hi] for n, (lo, hi) in _OFF_B.items()}
    out["w_out_a"] = piece["w_out_a"].reshape(1, D_RNN // N_DEV, D_MODEL)
    out["w_dkv"] = piece["w_dkv"].reshape(D_MODEL // N_DEV, KV_RANK + QK_ROPE)
    out["w_uk"] = piece["w_uk"].reshape(KV_RANK // N_DEV, N_HEADS, QK_NOPE)
    out["w_uv"] = piece["w_uv"].reshape(KV_RANK // N_DEV, N_HEADS, V_DIM)
    out["w_in_b"] = piece["w_in_b"].T.reshape(1, D_MODEL, (Q_RANK + N_HEADS * V_DIM) // N_DEV)
    out["w_uq"] = piece["w_uq"].reshape(1, Q_RANK // N_DEV, N_HEADS, HEAD_PAD)[..., :QK_NOPE + QK_ROPE]
    out["w_out_b"] = sum_c.reshape(1, N_HEADS * V_DIM // N_DEV, D_MODEL)
    return out


def _step(x, target, w, rep, block_b, block_c, *, bsz, seq):
    t = bsz * seq
    cos, sin = _rope_tables(seq)
    g_a = w["norm_a"]
    g_kv = rep["norm_kv"].reshape(1, -1)
    g_kvn = rep["kv_norm"].reshape(1, -1)
    g_b = rep["norm_b"].reshape(1, -1)
    g_q = rep["q_norm"].reshape(1, -1)
    g_f = rep["final_norm"].reshape(1, -1)
    wrg = rep["w_rg"][0].astype(BF16)
    wig = rep["w_ig"][0].astype(BF16)
    cw8, vecs = w["conv_taps"], w["lru_vecs"]

    def seq3(a):
        return a.reshape(bsz, seq, a.shape[-1])

    def flat(a):
        return a.reshape(t, a.shape[-1])

    h0, xp, ga = _lru_proj_fwd(x, g_a, w["w_in_a_t"], name="lru_proj_fwd")
    xb, hs, y, wall_b = _lru_fwd(seq3(xp), seq3(ga), cw8, vecs, wrg, wig, block_b, name="lru_fwd")
    w = dict(w, **_weights_b(wall_b))
    x1 = _matmul(flat(y), w["w_out_a"], residual=x, name="out_a")
    hk, hq, ck, cqp, g2, ckv, cq, q, kn, v, kr, kn_t, v_t, kr_t = _mla_proj_fwd(
        x1, (g_kv, g_b, g_kvn, g_q), w, cos, sin, seq=seq, name="mla_proj_fwd")
    o, lse, wall_c = _attn_fwd(q, kn, kr, v_t, block_c, bsz=bsz, seq=seq, name="attn_fwd")
    w_out_b = wall_c.reshape(N_HEADS * V_DIM, D_MODEL)
    loss, dx2, y2, do, dg2, dgf = _head_and_loss(o, g2, x1, target, w_out_b, g_f, name="head_loss")
    grads = {"final_norm": dgf}
    parts_c = _by_owner(_matmul_tn(y2, dx2, name="d_w_out_b")).astype(BF16)
    dq, dkn, dkr, dv, landed_c = _attn_bwd(q, kn, kr, kn_t, kr_t, v, o, lse, do, cos, sin, parts_c,
                                           bsz=bsz, seq=seq, name="attn_bwd")
    grads["w_uq"] = _matmul_tn(cq, dq, name="d_w_uq")
    dx1, du2, dckr, dgkv, dgb, dgkvn, dgq = _mla_proj_bwd(
        x1, dx2, cqp, ck, dq, dkn, dv, dkr, dg2, (g_kv, g_b, g_kvn, g_q), w, name="mla_proj_bwd")
    grads["norm_kv"], grads["norm_b"], grads["kv_norm"], grads["q_norm"] = dgkv, dgb, dgkvn, dgq
    grads["w_in_b"] = _matmul_tn(du2, hq, name="d_w_in_b_t")
    grads["w_uk"] = _matmul_tn(ckv, dkn, name="d_w_uk")
    grads["w_uv"] = _matmul_tn(ckv, dv, name="d_w_uv")
    grads["w_dkv"] = _matmul_tn(hk, dckr, name="d_w_dkv")[:, :KV_RANK + QK_ROPE]
    grads["w_out_a"] = _matmul_tn(flat(y), dx1, name="d_w_out_a")
    parts_b = _grad_parts_b(grads)
    dy = _matmul(dx1, w["w_out_a"], nt=True, name="d_y")
    dxp, dga, dwrg, dwig, dvec, landed_b = _lru_bwd(
        seq3(dy), seq3(xp), xb, hs, seq3(ga), cw8, vecs, wrg, wig, parts_b, name="lru_bwd")
    dxp, dga = flat(dxp), flat(dga)
    grads["w_rg"], grads["w_ig"] = dwrg, dwig
    grads["b_rg"], grads["b_ig"], grads["conv_b"] = dvec[0], dvec[1], dvec[3]
    lam = vecs[3]
    grads["lru_lambda"] = dvec[2] * (-1.0 / (1.0 + jnp.exp(lam)))
    grads["conv_w"] = dvec[4:4 + CONV_WIDTH]
    grads["w_in_a_t"] = (_matmul_tn(dxp, h0, name="d_w_in_a_x_t"), _matmul_tn(dga, h0, name="d_w_in_a_g_t"))
    dx, dga_norm = _lru_proj_bwd(dxp, dga, x, dx1, g_a, w["w_in_a_t"], name="lru_proj_bwd")
    grads["norm_a"] = dga_norm
    return loss[0, 0], dx, grads, landed_b, landed_c


def kernel(x, norm_a, w_in_a, conv_w, conv_b, w_rg, b_rg, w_ig, b_ig, lru_lambda, w_out_a, norm_kv, w_dkv, kv_norm, w_uk, w_uv, norm_b, w_in_b, q_norm, w_uq, w_out_b, final_norm, loss_target, m_norm_a, m_w_in_a, m_conv_w, m_conv_b, m_w_rg, m_b_rg, m_w_ig, m_b_ig, m_lru_lambda, m_w_out_a, m_norm_kv, m_w_dkv, m_kv_norm, m_w_uk, m_w_uv, m_norm_b, m_w_in_b, m_q_norm, m_w_uq, m_w_out_b, m_final_norm, v_norm_a, v_w_in_a, v_conv_w, v_conv_b, v_w_rg, v_b_rg, v_w_ig, v_b_ig, v_lru_lambda, v_w_out_a, v_norm_kv, v_w_dkv, v_kv_norm, v_w_uk, v_w_uv, v_norm_b, v_w_in_b, v_q_norm, v_w_uq, v_w_out_b, v_final_norm):
    given = dict(locals())
    wts = {n: given[n] for n in WEIGHTS}
    mom1 = {n: given["m_" + n] for n in WEIGHTS}
    mom2 = {n: given["v_" + n] for n in WEIGHTS}
    bsz, seq, _ = x.shape
    t = bsz * seq

    block_a, block_b, block_c = _weight_blocks(wts)
    w = _weights_a(_all_gather(block_a, name="gather_weights_a"))
    loss, dx, grads, landed_b, landed_c = _step(x.reshape(t, D_MODEL), loss_target.reshape(t, D_MODEL), w, wts,
                                                block_b, block_c, bsz=bsz, seq=seq)

    parts_a = _grad_parts_a(grads)
    from_sibling = _exchange_d2d(parts_a, name="exchange_grads_d2d")
    chip_parts = _chip_partial(parts_a, from_sibling, name="chip_partial_grads")
    landed_a = _exchange_ici(chip_parts, name="exchange_grads_ici")
    sum_a = _sum_parts(landed_a, name="sum_grads_a", br=GRAD_BLOCK)
    sum_b = _sum_parts(landed_b, name="sum_grads_b", br=WIRE_ROWS_B // 2)
    sum_c = _sum_parts(landed_c, name="sum_grads_c", br=landed_c.shape[1])
    g_own = _own_grads(sum_a, sum_b, sum_c)
    rep_slice = sum_a[MATRIX_ROWS_A + 8:MATRIX_ROWS_A + 8 + REP_SLICE]
    loss_rows = jnp.pad(loss.reshape(1, 1), ((0, 7), (0, PACK_W - 1)))
    gathered = _all_gather(jnp.concatenate([rep_slice, loss_rows], axis=0), name="gather_replicated")
    g_own.update(_unpack_rep(gathered[:, :REP_SLICE].reshape(REP_ROWS, PACK_W), wts))
    loss = jnp.sum(gathered[:, REP_SLICE, 0])

    deltas, new_m, new_v = {}, {}, {}
    for n in WEIGHTS:
        deltas[n], new_m[n], new_v[n] = _adamw(g_own[n], wts[n], mom1[n], mom2[n], name="adamw_" + n)
    result = [loss, dx.reshape(bsz, seq, D_MODEL)]
    for d in (g_own, deltas, new_m, new_v):
        result.extend(d[n] for n in WEIGHTS)
    return tuple(result)
```

```python
import jax
import jax.numpy as jnp
from jax import lax
from jax.experimental import pallas as pl
from jax.experimental.pallas import tpu as pltpu

F32 = jnp.float32
BF16 = jnp.bfloat16
WIRE = jnp.bfloat16

D_MODEL = 1024
D_RNN = 1280
RNN_BLOCKS = 10
RNN_BW = 128
CONV_WIDTH = 4
LRU_C = 8.0
N_HEADS = 8
QK_NOPE = 128
QK_ROPE = 64
V_DIM = 128
KV_RANK = 256
Q_RANK = 384
ROPE_THETA = 10000.0
EPS = 1e-6
ATTN_SCALE = (QK_NOPE + QK_ROPE) ** -0.5
HEAD_PAD = 256
LANES = 128

ADAM_LR = 0.001
ADAM_B1 = 0.9
ADAM_B2 = 0.999
ADAM_EPS = 1e-08
ADAM_WD = 0.01
ADAM_STEP = 10

N_DEV = 8
VMEM_LIMIT_BYTES = 56 * 2**20
PACK_W = 1024

_PIECES_A = (("w_in_a", 320),)
_PIECES_B = (("w_out_a", 160), ("w_dkv", 40), ("w_uk", 32), ("w_uv", 32), ("w_in_b", 176), ("w_uq", 96))


def _offsets(pieces):
    off, r = {}, 0
    for n, k in pieces:
        off[n] = (r, r + k)
        r += k
    return off, r


_OFF_A, MATRIX_ROWS_A = _offsets(_PIECES_A)
_OFF_B, MATRIX_ROWS_B = _offsets(_PIECES_B)
WIRE_ROWS_A = MATRIX_ROWS_A + 32
WIRE_ROWS_B = 544
_SMALL = (("norm_a", 128), ("conv_w", 640), ("conv_b", 160), ("b_rg", 160), ("b_ig", 160), ("lru_lambda", 160))
_REP = (("w_rg", 163840), ("w_ig", 163840), ("norm_kv", 1024), ("kv_norm", 256), ("norm_b", 1024),
        ("q_norm", 384), ("final_norm", 1024))
REP_ROWS = 384
REP_SLICE = REP_ROWS // N_DEV
GRAD_ROWS_A = 384
GRAD_BLOCK = 192

WEIGHTS = ("norm_a", "w_in_a", "conv_w", "conv_b", "w_rg", "b_rg", "w_ig", "b_ig", "lru_lambda", "w_out_a",
           "norm_kv", "w_dkv", "kv_norm", "w_uk", "w_uv", "norm_b", "w_in_b", "q_norm", "w_uq", "w_out_b",
           "final_norm")


def _params(sem=None):
    return pltpu.CompilerParams(dimension_semantics=sem, vmem_limit_bytes=VMEM_LIMIT_BYTES)


_NT = (((1,), (1,)), ((), ()))
_ANY = pl.BlockSpec(memory_space=pl.ANY)


def _mesh_pos():
    return lax.axis_index("x"), lax.axis_index("y"), lax.axis_index("c")


def _sigmoid(z):
    return 0.5 * jnp.tanh(0.5 * z) + 0.5


def _sigmoid_tail(z):
    return 1.0 / (1.0 + jnp.exp(-z))


def _col_block(n):
    return n if n <= 1408 else n // 2


def _matmul(a, b, *, name, nt=False, out_dtype=F32, residual=None, bm=1024):
    m, k = a.shape
    n = b.shape[0] if nt else b.shape[1]
    bm = min(bm, m)
    bn = _col_block(n)
    dims = (((1,), (1,)), ((), ())) if nt else (((1,), (0,)), ((), ()))
    has_res = residual is not None

    def body(*refs):
        a_ref, b_ref, o_ref = refs[0], refs[1], refs[-1]
        acc = lax.dot_general(a_ref[...].astype(BF16), b_ref[...].astype(BF16), dims, preferred_element_type=F32)
        if has_res:
            acc = acc + refs[2][...]
        o_ref[...] = acc.astype(out_dtype)

    in_specs = [pl.BlockSpec((bm, k), lambda i, j: (i, 0)),
                pl.BlockSpec((bn, k), lambda i, j: (j, 0)) if nt else pl.BlockSpec((k, bn), lambda i, j: (0, j))]
    args = [a, b]
    if has_res:
        in_specs.append(pl.BlockSpec((bm, bn), lambda i, j: (i, j)))
        args.append(residual)
    return pl.pallas_call(
        body, grid=(m // bm, n // bn), in_specs=in_specs, out_specs=pl.BlockSpec((bm, bn), lambda i, j: (i, j)),
        out_shape=jax.ShapeDtypeStruct((m, n), out_dtype), compiler_params=_params(("parallel", "parallel")),
        name=name)(*args)


def _matmul_tn(a, b, *, name, bt=1024):
    t, m = a.shape
    n = b.shape[1]
    bt = min(bt, t)
    bm, bn = _col_block(m), _col_block(n)

    def body(a_ref, b_ref, o_ref):
        @pl.when(pl.program_id(2) == 0)
        def _():
            o_ref[...] = jnp.zeros_like(o_ref)

        o_ref[...] += lax.dot_general(a_ref[...].astype(BF16), b_ref[...].astype(BF16),
                                      (((0,), (0,)), ((), ())), preferred_element_type=F32)

    return pl.pallas_call(
        body, grid=(m // bm, n // bn, t // bt),
        in_specs=[pl.BlockSpec((bt, bm), lambda i, j, s: (s, i)), pl.BlockSpec((bt, bn), lambda i, j, s: (s, j))],
        out_specs=pl.BlockSpec((bm, bn), lambda i, j, s: (i, j)),
        out_shape=jax.ShapeDtypeStruct((m, n), F32),
        compiler_params=_params(("parallel", "parallel", "arbitrary")), name=name)(a, b)


def _swap_halves(v):
    ax = v.ndim - 1
    lane = lax.broadcasted_iota(jnp.int32, v.shape, ax)
    up = pltpu.roll(v, LANES - QK_ROPE // 2, axis=ax)
    down = pltpu.roll(v, QK_ROPE // 2, axis=ax)
    return jnp.where(lane < QK_ROPE // 2, up, jnp.where(lane < QK_ROPE, down, 0.0))


def _rope(v, cos, sin):
    return v * cos + _swap_halves(v) * sin


def _rope_t(d, cos, sin):
    return d * cos + _swap_halves(d * sin)


def _rope_tables(seq):
    pos = jnp.arange(seq, dtype=F32)
    inv = ROPE_THETA ** (-jnp.arange(0, QK_ROPE, 2, dtype=F32) / QK_ROPE)
    ang = pos[:, None] * inv[None, :]
    cos, sin = jnp.cos(ang), jnp.sin(ang)
    zero = jnp.zeros((seq, LANES - QK_ROPE), F32)
    return jnp.concatenate([cos, cos, zero], axis=1), jnp.concatenate([-sin, sin, zero], axis=1)


def _rms(v):
    return v * lax.rsqrt(jnp.mean(v * v, axis=-1, keepdims=True) + EPS)


def _const_spec(a):
    return pl.BlockSpec(a.shape, lambda i: (0,) * a.ndim)


def _lru_proj_fwd(x, g_a, w_in_t, *, name, bt=512):
    t, d = x.shape
    bt = min(bt, t)
    n = w_in_t.shape[0] // 2

    def body(x_ref, g_ref, wt_ref, h_ref, xp_ref, ga_ref):
        h = (_rms(x_ref[...]) * g_ref[...]).astype(BF16)
        h_ref[...] = h
        xp_ref[...] = lax.dot_general(h, wt_ref[0:n, :], _NT, preferred_element_type=F32)
        ga_ref[...] = lax.dot_general(h, wt_ref[n:2 * n, :], _NT, preferred_element_type=F32)

    row = lambda w: pl.BlockSpec((bt, w), lambda i: (i, 0))
    return pl.pallas_call(
        body, grid=(t // bt,), in_specs=[row(d), _const_spec(g_a), _const_spec(w_in_t)],
        out_specs=[row(d), row(n), row(n)],
        out_shape=[jax.ShapeDtypeStruct((t, d), BF16), jax.ShapeDtypeStruct((t, n), F32), jax.ShapeDtypeStruct((t, n), F32)],
        compiler_params=_params(("parallel",)), name=name)(x, g_a, w_in_t)


def _mla_proj_fwd(x1, gains, w, cos, sin, *, seq, name, bt=512):
    t, d = x1.shape
    bt = min(bt, seq)
    per_seq = seq // bt
    g_kv, g_b, g_kvn, g_q = gains
    consts = [g_kv, g_b, g_kvn, g_q, w["w_dkv_c"], w["w_dkv_r"], w["w_in_b_t"], w["w_uk"], w["w_uv"], w["w_uq"]]

    def body(x_ref, cos_ref, sin_ref, gkv_ref, gb_ref, gkvn_ref, gq_ref, wdc_ref, wdr_ref, wbt_ref,
             wuk_ref, wuv_ref, wuq_ref,
             hk_ref, hq_ref, ck_ref, cqp_ref, g2_ref, ckv_ref, cq_ref, q_ref, kn_ref, v_ref, kr_ref, knt_ref, vt_ref, krt_ref):
        nrm = _rms(x_ref[...])
        hk = (nrm * gkv_ref[...]).astype(BF16)
        hq = (nrm * gb_ref[...]).astype(BF16)
        hk_ref[...] = hk
        hq_ref[...] = hq
        ck = jnp.dot(hk, wdc_ref[...], preferred_element_type=F32)
        ck_ref[...] = ck
        cqp = lax.dot_general(hq, wbt_ref[0:Q_RANK, :], _NT, preferred_element_type=F32)
        cqp_ref[...] = cqp
        g2_ref[...] = lax.dot_general(hq, wbt_ref[Q_RANK:, :], _NT, preferred_element_type=F32)
        cosv, sinv = cos_ref[...], sin_ref[...]
        kr = _rope(jnp.dot(hk, wdr_ref[...], preferred_element_type=F32), cosv, sinv)
        kr_ref[...] = kr.astype(BF16)
        krt_ref[...] = kr.T.astype(BF16)
        ckv = (_rms(ck) * gkvn_ref[...]).astype(BF16)
        ckv_ref[...] = ckv
        kn = jnp.dot(ckv, wuk_ref[...], preferred_element_type=F32)
        v = jnp.dot(ckv, wuv_ref[...], preferred_element_type=F32)
        kn_ref[...] = kn.astype(BF16)
        v_ref[...] = v.astype(BF16)
        knt_ref[...] = kn.T.astype(BF16)
        vt_ref[...] = v.T.astype(BF16)
        cq = (_rms(cqp) * gq_ref[...]).astype(BF16)
        cq_ref[...] = cq
        for h in range(N_HEADS):
            qh = jnp.dot(cq, wuq_ref[:, h * HEAD_PAD:(h + 1) * HEAD_PAD], preferred_element_type=F32)
            q_ref[:, h * HEAD_PAD:h * HEAD_PAD + QK_NOPE] = qh[:, :QK_NOPE].astype(BF16)
            q_ref[:, h * HEAD_PAD + QK_NOPE:(h + 1) * HEAD_PAD] = _rope(qh[:, QK_NOPE:], cosv, sinv).astype(BF16)

    row = lambda w_: pl.BlockSpec((bt, w_), lambda i: (i, 0))
    col = lambda h_: pl.BlockSpec((h_, bt), lambda i: (0, i))
    tab = pl.BlockSpec((bt, LANES), lambda i: (i % per_seq, 0))
    nh = N_HEADS * V_DIM
    shapes = [((t, d), BF16), ((t, d), BF16), ((t, KV_RANK), F32), ((t, Q_RANK), F32), ((t, nh), F32), ((t, KV_RANK), BF16),
              ((t, Q_RANK), BF16), ((t, N_HEADS * HEAD_PAD), BF16), ((t, nh), BF16), ((t, nh), BF16), ((t, LANES), BF16),
              ((nh, t), BF16), ((nh, t), BF16), ((LANES, t), BF16)]
    out_specs = [row(d), row(d), row(KV_RANK), row(Q_RANK), row(nh), row(KV_RANK), row(Q_RANK), row(N_HEADS * HEAD_PAD),
                 row(nh), row(nh), row(LANES), col(nh), col(nh), col(LANES)]
    return pl.pallas_call(
        body, grid=(t // bt,), in_specs=[row(d), tab, tab] + [_const_spec(a) for a in consts], out_specs=out_specs,
        out_shape=[jax.ShapeDtypeStruct(s, dt) for s, dt in shapes],
        compiler_params=_params(("parallel",)), name=name)(x1, cos, sin, *consts)


def _rms_bwd_rows(xv, dn):
    r = lax.rsqrt(jnp.mean(xv * xv, axis=-1, keepdims=True) + EPS)
    nrm = xv * r
    return r * (dn - nrm * jnp.mean(dn * nrm, axis=-1, keepdims=True)), nrm


def _col_sum(v):
    return jnp.sum(v, axis=0, keepdims=True)


def _lru_proj_bwd(dxp, dga, x, dx1, g_a, w_in_t, *, name, bt=512):
    t, d = x.shape
    bt = min(bt, t)
    n = w_in_t.shape[0] // 2

    def body(dxp_ref, dga_ref, x_ref, dx1_ref, g_ref, wt_ref, dx_ref, dg_ref):
        @pl.when(pl.program_id(0) == 0)
        def _():
            dg_ref[...] = jnp.zeros_like(dg_ref)

        dh = (jnp.dot(dxp_ref[...], wt_ref[0:n, :], preferred_element_type=F32)
              + jnp.dot(dga_ref[...], wt_ref[n:2 * n, :], preferred_element_type=F32))
        dxn, nrm = _rms_bwd_rows(x_ref[...], dh * g_ref[...])
        dg_ref[...] += _col_sum(dh * nrm)
        dx_ref[...] = dx1_ref[...] + dxn

    row = lambda w: pl.BlockSpec((bt, w), lambda i: (i, 0))
    return pl.pallas_call(
        body, grid=(t // bt,),
        in_specs=[row(n), row(n), row(d), row(d), _const_spec(g_a), _const_spec(w_in_t)],
        out_specs=[row(d), _const_spec(g_a)],
        out_shape=[jax.ShapeDtypeStruct((t, d), F32), jax.ShapeDtypeStruct((1, d), F32)],
        compiler_params=_params(("arbitrary",)), name=name)(dxp, dga, x, dx1, g_a, w_in_t)


def _mla_proj_bwd(x1, dx2, cqp, ck, dq, dkn, dv, dkr, dg2, gains, w, *, name, bt=512):
    t, d = x1.shape
    bt = min(bt, t)
    g_kv, g_b, g_kvn, g_q = gains
    consts = [g_kv, g_b, g_kvn, g_q, w["w_dkv_c"], w["w_dkv_r"], w["w_in_b_t"], w["w_uk"], w["w_uv"], w["w_uq"],
              w["w_out_a"]]
    nh = N_HEADS * V_DIM

    def body(x1_ref, dx2_ref, cqp_ref, ck_ref, dq_ref, dkn_ref, dv_ref, dkr_ref, dg2_ref,
             gkv_ref, gb_ref, gkvn_ref, gq_ref, wdc_ref, wdr_ref, wbt_ref, wuk_ref, wuv_ref, wuq_ref, wo_ref,
             dx1_ref, du2_ref, dckr_ref, dgkv_ref, dgb_ref, dgkvn_ref, dgq_ref, dy_ref):
        @pl.when(pl.program_id(0) == 0)
        def _():
            for ref in (dgkv_ref, dgb_ref, dgkvn_ref, dgq_ref):
                ref[...] = jnp.zeros_like(ref)

        dot_nt = lambda a, b: lax.dot_general(a, b, _NT, preferred_element_type=F32)
        dcq = dot_nt(dq_ref[...], wuq_ref[...])
        dcqp, nq = _rms_bwd_rows(cqp_ref[...], dcq * gq_ref[...])
        dgq_ref[...] += _col_sum(dcq * nq)
        dcqp = dcqp.astype(BF16)
        dg2 = dg2_ref[...]
        du2_ref[:, :Q_RANK] = dcqp
        du2_ref[:, Q_RANK:] = dg2
        dhq = (jnp.dot(dcqp, wbt_ref[0:Q_RANK, :], preferred_element_type=F32)
               + jnp.dot(dg2, wbt_ref[Q_RANK:, :], preferred_element_type=F32))
        dckv = dot_nt(dkn_ref[...], wuk_ref[...]) + dot_nt(dv_ref[...], wuv_ref[...])
        dck, nc = _rms_bwd_rows(ck_ref[...], dckv * gkvn_ref[...])
        dgkvn_ref[...] += _col_sum(dckv * nc)
        dck = dck.astype(BF16)
        dkr = dkr_ref[...].astype(BF16)
        dckr_ref[:, :KV_RANK] = dck
        dckr_ref[:, KV_RANK:] = dkr
        dhk = dot_nt(dck, wdc_ref[...]) + dot_nt(dkr, wdr_ref[...])
        dxn, n1 = _rms_bwd_rows(x1_ref[...], dhq * gb_ref[...] + dhk * gkv_ref[...])
        dgb_ref[...] += _col_sum(dhq * n1)
        dgkv_ref[...] += _col_sum(dhk * n1)
        dx1 = dx2_ref[...] + dxn
        dx1_ref[...] = dx1
        dy_ref[...] = lax.dot_general(dx1.astype(BF16), wo_ref[...], _NT, preferred_element_type=F32)

    row = lambda w_: pl.BlockSpec((bt, w_), lambda i: (i, 0))
    vec = lambda w_: pl.BlockSpec((1, w_), lambda i: (0, 0))
    in_specs = [row(d), row(d), row(Q_RANK), row(KV_RANK), row(N_HEADS * HEAD_PAD), row(nh), row(nh), row(LANES), row(nh)]
    return pl.pallas_call(
        body, grid=(t // bt,), in_specs=in_specs + [_const_spec(a) for a in consts],
        out_specs=[row(d), row(Q_RANK + nh), row(KV_RANK + LANES), vec(d), vec(d), vec(KV_RANK), vec(Q_RANK), row(D_RNN)],
        out_shape=[jax.ShapeDtypeStruct((t, d), F32), jax.ShapeDtypeStruct((t, Q_RANK + nh), BF16),
                   jax.ShapeDtypeStruct((t, KV_RANK + LANES), BF16), jax.ShapeDtypeStruct((1, d), F32),
                   jax.ShapeDtypeStruct((1, d), F32), jax.ShapeDtypeStruct((1, KV_RANK), F32),
                   jax.ShapeDtypeStruct((1, Q_RANK), F32), jax.ShapeDtypeStruct((t, D_RNN), F32)],
        compiler_params=_params(("arbitrary",)), name=name)(x1, dx2, cqp, ck, dq, dkn, dv, dkr, dg2, *consts)


def _softplus(z):
    return jnp.maximum(z, 0.0) + jnp.log1p(jnp.exp(-jnp.abs(z)))


def _one_minus_square(a, la):
    return jnp.tanh(-la) * (1.0 + a * a)


def _gates(xb, wrg, wig, brg, big, sp):
    xbb = xb.astype(BF16)
    r = _sigmoid_tail(jnp.dot(xbb, wrg, preferred_element_type=F32) + brg)
    i = _sigmoid(jnp.dot(xbb, wig, preferred_element_type=F32) + big)
    la = (-LRU_C) * r * sp
    a = jnp.exp(la)
    em = _one_minus_square(a, la)
    inv_mult = lax.rsqrt(em)
    mult = jnp.where(em > 0.0, em * inv_mult, 0.0)
    return r, i, a, mult, inv_mult


def _conv(xpad_ref, cw_ref, seq):
    acc = cw_ref[0:1, :] * xpad_ref[pl.ds(8 - (CONV_WIDTH - 1), seq), :]
    for k in range(1, CONV_WIDTH):
        acc = acc + cw_ref[k:k + 1, :] * xpad_ref[pl.ds(8 - (CONV_WIDTH - 1) + k, seq), :]
    return acc


def _seq_spec(seq):
    return pl.BlockSpec((None, seq, RNN_BW), lambda n, b: (b, 0, n))


def _chan_spec(rows):
    return pl.BlockSpec((rows, RNN_BW), lambda n, b: (0, n))


_GATE_W_SPEC = pl.BlockSpec((None, RNN_BW, RNN_BW), lambda n, b: (n, 0, 0))


SCAN_UNROLL = 4


def _peers():
    x, y, c = _mesh_pos()
    others = []
    for k in range(1, N_DEV):
        px = 1 - x if k & 4 else x
        py = 1 - y if k & 2 else y
        pc = 1 - c if k & 1 else c
        others.append(((px, py, pc), 4 * px + 2 * py + pc))
    return 4 * x + 2 * y + c, others


def _exchange(src_ref, dst_ref, send_sems, recv_sems, local_sem, *, finish, gather=False):
    me, others = _peers()

    def send(k, dev, slot):
        return pltpu.make_async_remote_copy(
            src_ref=src_ref if gather else src_ref.at[slot], dst_ref=dst_ref.at[me], send_sem=send_sems.at[k],
            recv_sem=recv_sems.at[k], device_id=dev, device_id_type=pl.DeviceIdType.MESH)

    local = pltpu.make_async_copy(src_ref if gather else src_ref.at[me], dst_ref.at[me], local_sem)
    if not finish:
        local.start()
        for k, (dev, slot) in enumerate(others):
            send(k, dev, slot).start()
        return
    for k, (dev, slot) in enumerate(others):
        pltpu.make_async_remote_copy(
            src_ref=dst_ref.at[slot], dst_ref=dst_ref.at[slot], send_sem=send_sems.at[k], recv_sem=recv_sems.at[k],
            device_id=dev, device_id_type=pl.DeviceIdType.MESH).wait_recv()
    for k, (dev, slot) in enumerate(others):
        send(k, dev, slot).wait_send()
    local.wait()


def _gather_two_level(x_ref, out_ref, send_sems, recv_sems, local_sem, *, phase):
    x, y, c = _mesh_pos()
    me, sibling = (x, y, c), (x, y, 1 - c)
    chips = [(1 - x, y), (x, 1 - y), (1 - x, 1 - y)]

    def slot(px, py, pc):
        return out_ref.at[4 * px + 2 * py + pc]

    def copy(k, blk, to, src=None):
        return pltpu.make_async_remote_copy(
            src_ref=slot(*blk) if src is None else src, dst_ref=slot(*blk),
            send_sem=send_sems.at[k], recv_sem=recv_sems.at[k], device_id=to, device_id_type=pl.DeviceIdType.MESH)

    if phase == 0:
        pltpu.make_async_copy(x_ref, slot(*me), local_sem).start()
        copy(0, me, sibling, src=x_ref).start()
        for j, chip in enumerate(chips):
            copy(1 + j, me, (*chip, c), src=x_ref).start()
    elif phase == 1:
        for j, chip in enumerate(chips):
            copy(1 + j, (*chip, c), me).wait_recv()
            copy(4 + j, (*chip, c), sibling).start()
    else:
        copy(0, sibling, me).wait_recv()
        for j, chip in enumerate(chips):
            copy(4 + j, (*chip, 1 - c), me).wait_recv()
        copy(0, me, sibling, src=x_ref).wait_send()
        for j, chip in enumerate(chips):
            copy(1 + j, me, (*chip, c), src=x_ref).wait_send()
            copy(4 + j, (*chip, c), sibling).wait_send()
        pltpu.make_async_copy(x_ref, slot(*me), local_sem).wait()


GATHER_FORWARD_STEP = 9
_EXCHANGE_SEMS = [pltpu.SemaphoreType.DMA((N_DEV - 1,)), pltpu.SemaphoreType.DMA((N_DEV - 1,)), pltpu.SemaphoreType.DMA(())]


def _first_last(steps):
    first = last = None
    for axis, n in enumerate(steps):
        i = pl.program_id(axis)
        first = (i == 0) if first is None else first & (i == 0)
        last = (i == n - 1) if last is None else last & (i == n - 1)
    return first, last


def _lru_fwd(xp, ga, cw, vecs, wrg, wig, block, *, name):
    bsz, seq, _ = xp.shape
    groups = seq // 8

    def body(xp_ref, ga_ref, cw_ref, vec_ref, wrg_ref, wig_ref, blk_ref, xb_ref, hs_ref, y_ref, all_ref,
             xpad, a_s, b_s, send_sems, recv_sems, local_sem):
        first, last = _first_last((RNN_BLOCKS, bsz))

        @pl.when(first)
        def _():
            _gather_two_level(blk_ref, all_ref, send_sems, recv_sems, local_sem, phase=0)

        @pl.when((pl.program_id(0) == GATHER_FORWARD_STEP) & (pl.program_id(1) == 0))
        def _():
            _gather_two_level(blk_ref, all_ref, send_sems, recv_sems, local_sem, phase=1)

        xpad[0:8, :] = jnp.zeros((8, RNN_BW), F32)
        xpad[pl.ds(8, seq), :] = xp_ref[...]
        xb = _conv(xpad, cw_ref, seq) + vec_ref[0:1, :]
        xb_ref[...] = xb
        sp = _softplus(-vec_ref[3:4, :])
        _, i, a, mult, _ = _gates(xb, wrg_ref[...], wig_ref[...], vec_ref[1:2, :], vec_ref[2:3, :], sp)
        a_s[...] = a
        b_s[...] = mult * (i * xb)
        row = lax.broadcasted_iota(jnp.int32, (8, RNN_BW), 0)

        def group(g, h):
            r0 = pl.multiple_of(g * 8, 8)
            av = a_s[pl.ds(r0, 8), :]
            bv = b_s[pl.ds(r0, 8), :]
            for k in (1, 2, 4):
                m = row >= k
                bv = jnp.where(m, av * pltpu.roll(bv, k, axis=0) + bv, bv)
                av = jnp.where(m, av * pltpu.roll(av, k, axis=0), av)
            hs_ref[pl.ds(r0, 8), :] = av * h + bv
            return av[7:8, :] * h + bv[7:8, :]

        def groups_of(i, h):
            for u in range(SCAN_UNROLL):
                h = group(i * SCAN_UNROLL + u, h)
            return h

        lax.fori_loop(0, groups // SCAN_UNROLL, groups_of, jnp.zeros((1, RNN_BW), F32))
        gav = ga_ref[...]
        y_ref[...] = (hs_ref[...] * (gav * _sigmoid(gav))).astype(BF16)

        @pl.when(last)
        def _():
            _gather_two_level(blk_ref, all_ref, send_sems, recv_sems, local_sem, phase=2)

    sq = _seq_spec(seq)
    shape = (bsz, seq, D_RNN)
    return pl.pallas_call(
        body, grid=(RNN_BLOCKS, bsz),
        in_specs=[sq, sq, _chan_spec(8), _chan_spec(8), _GATE_W_SPEC, _GATE_W_SPEC, _ANY],
        out_specs=[sq, sq, sq, _ANY],
        out_shape=[jax.ShapeDtypeStruct(shape, F32), jax.ShapeDtypeStruct(shape, F32), jax.ShapeDtypeStruct(shape, BF16),
                   jax.ShapeDtypeStruct((N_DEV,) + block.shape, block.dtype)],
        scratch_shapes=[pltpu.VMEM((seq + 8, RNN_BW), F32), pltpu.VMEM((seq, RNN_BW), F32), pltpu.VMEM((seq, RNN_BW), F32)]
        + _EXCHANGE_SEMS,
        compiler_params=_params(("arbitrary", "arbitrary")), name=name)(xp, ga, cw, vecs, wrg, wig, block)


def _lru_bwd(dy, xp, xb, hs, ga, cw, vecs, wrg, wig, parts, *, name):
    bsz, seq, _ = xp.shape
    groups = seq // 8

    def body(dy_ref, xp_ref, xb_ref, hs_ref, ga_ref, cw_ref, vec_ref, wrg_ref, wig_ref,
             parts_ref, dxp_ref, dga_ref, dwrg_ref, dwig_ref, dvec_ref, land_ref, pad, a_s, d_s, lam_s,
             send_sems, recv_sems, local_sem):
        first, last = _first_last((RNN_BLOCKS, bsz))

        @pl.when(first)
        def _():
            _exchange(parts_ref, land_ref, send_sems, recv_sems, local_sem, finish=False)

        @pl.when(pl.program_id(1) == 0)
        def _():
            dwrg_ref[...] = jnp.zeros_like(dwrg_ref)
            dwig_ref[...] = jnp.zeros_like(dwig_ref)
            dvec_ref[...] = jnp.zeros_like(dvec_ref)

        xb = xb_ref[...]
        hs = hs_ref[...]
        gav = ga_ref[...]
        dy = dy_ref[...]
        sp = _softplus(-vec_ref[3:4, :])
        wrg = wrg_ref[...]
        wig = wig_ref[...]
        r, i, a, mult, inv_mult = _gates(xb, wrg, wig, vec_ref[1:2, :], vec_ref[2:3, :], sp)
        sg = _sigmoid(gav)
        dga_ref[...] = (dy * hs * (sg * (1.0 + gav * (1.0 - sg)))).astype(BF16)
        d_s[...] = dy * (gav * sg)

        pad[pl.ds(0, seq), :] = a
        pad[pl.ds(seq, 8), :] = jnp.zeros((8, RNN_BW), F32)
        a_s[...] = pad[pl.ds(1, seq), :]
        row = lax.broadcasted_iota(jnp.int32, (8, RNN_BW), 0)

        def group(g, nxt):
            r0 = pl.multiple_of((groups - 1 - g) * 8, 8)
            cv = a_s[pl.ds(r0, 8), :]
            bv = d_s[pl.ds(r0, 8), :]
            for k in (1, 2, 4):
                m = row < 8 - k
                bv = jnp.where(m, cv * pltpu.roll(bv, 8 - k, axis=0) + bv, bv)
                cv = jnp.where(m, cv * pltpu.roll(cv, 8 - k, axis=0), cv)
            lam_s[pl.ds(r0, 8), :] = cv * nxt + bv
            return cv[0:1, :] * nxt + bv[0:1, :]

        def groups_of(i, nxt):
            for u in range(SCAN_UNROLL):
                nxt = group(i * SCAN_UNROLL + u, nxt)
            return nxt

        lax.fori_loop(0, groups // SCAN_UNROLL, groups_of, jnp.zeros((1, RNN_BW), F32))
        dh = lam_s[...]

        pad[0:8, :] = jnp.zeros((8, RNN_BW), F32)
        pad[pl.ds(8, seq), :] = hs
        da = dh * pad[pl.ds(7, seq), :]
        ixb = i * xb
        dixb = dh * mult
        dla = da * a - (dh * ixb) * (a * a) * inv_mult
        drp = (dla * ((-LRU_C) * sp)) * r * (1.0 - r)
        dip = (dixb * xb) * i * (1.0 - i)
        dvec_ref[0:1, :] += jnp.sum(drp, axis=0, keepdims=True)
        dvec_ref[1:2, :] += jnp.sum(dip, axis=0, keepdims=True)
        dvec_ref[2:3, :] += jnp.sum(dla * ((-LRU_C) * r), axis=0, keepdims=True)
        drpb = drp.astype(BF16)
        dipb = dip.astype(BF16)
        xbb = xb.astype(BF16)
        nt = (((1,), (1,)), ((), ()))
        tn = (((0,), (0,)), ((), ()))
        dxb = (dixb * i
               + lax.dot_general(drpb, wrg, nt, preferred_element_type=F32)
               + lax.dot_general(dipb, wig, nt, preferred_element_type=F32))
        dwrg_ref[...] += lax.dot_general(xbb, drpb, tn, preferred_element_type=F32)
        dwig_ref[...] += lax.dot_general(xbb, dipb, tn, preferred_element_type=F32)
        dvec_ref[3:4, :] += jnp.sum(dxb, axis=0, keepdims=True)

        pad[pl.ds(0, seq), :] = dxb
        pad[pl.ds(seq, 8), :] = jnp.zeros((8, RNN_BW), F32)
        dxp = cw_ref[0:1, :] * pad[pl.ds(CONV_WIDTH - 1, seq), :]
        for k in range(1, CONV_WIDTH):
            dxp = dxp + cw_ref[k:k + 1, :] * pad[pl.ds(CONV_WIDTH - 1 - k, seq), :]
        dxp_ref[...] = dxp.astype(BF16)
        pad[0:8, :] = jnp.zeros((8, RNN_BW), F32)
        pad[pl.ds(8, seq), :] = xp_ref[...]
        for k in range(CONV_WIDTH):
            dvec_ref[4 + k:5 + k, :] += jnp.sum(dxb * pad[pl.ds(8 - (CONV_WIDTH - 1) + k, seq), :], axis=0, keepdims=True)

        @pl.when(last)
        def _():
            _exchange(parts_ref, land_ref, send_sems, recv_sems, local_sem, finish=True)

    sq = _seq_spec(seq)
    shape = (bsz, seq, D_RNN)
    gshape = (RNN_BLOCKS, RNN_BW, RNN_BW)
    return pl.pallas_call(
        body, grid=(RNN_BLOCKS, bsz),
        in_specs=[sq, sq, sq, sq, sq, _chan_spec(8), _chan_spec(8), _GATE_W_SPEC, _GATE_W_SPEC, _ANY],
        out_specs=[sq, sq, _GATE_W_SPEC, _GATE_W_SPEC, _chan_spec(8), _ANY],
        out_shape=[jax.ShapeDtypeStruct(shape, BF16), jax.ShapeDtypeStruct(shape, BF16),
                   jax.ShapeDtypeStruct(gshape, F32), jax.ShapeDtypeStruct(gshape, F32),
                   jax.ShapeDtypeStruct((8, D_RNN), F32), jax.ShapeDtypeStruct(parts.shape, parts.dtype)],
        scratch_shapes=[pltpu.VMEM((seq + 8, RNN_BW), F32), pltpu.VMEM((seq, RNN_BW), F32),
                        pltpu.VMEM((seq, RNN_BW), F32), pltpu.VMEM((seq, RNN_BW), F32)] + _EXCHANGE_SEMS,
        compiler_params=_params(("arbitrary", "arbitrary")), name=name)(dy, xp, xb, hs, ga, cw, vecs, wrg, wig, parts)


def _attn_block(seq):
    return min(512, seq)


def _diag_mask(blk):
    return lax.broadcasted_iota(jnp.int32, (blk, blk), 0) <= lax.broadcasted_iota(jnp.int32, (blk, blk), 1)


FWD_HEADS = 8
BWD_HEADS = 2


def _attn_fwd(q, kn, kr, v_t, block, *, bsz, seq, name):
    t = bsz * seq
    blk = _attn_block(seq)
    nq = seq // blk
    hg = FWD_HEADS
    steps = (bsz, N_HEADS // hg, nq)

    def body(q_ref, kn_ref, kr_ref, vt_ref, blk_ref, o_ref, lse_ref, all_ref, acc, send_sems, recv_sems, local_sem):
        first, last = _first_last(steps)

        @pl.when(first)
        def _():
            _exchange(blk_ref, all_ref, send_sems, recv_sems, local_sem, finish=False, gather=True)

        qi = pl.program_id(2)
        acc[...] = jnp.zeros_like(acc)

        def step(j, carry, diagonal):
            k0 = pl.multiple_of(j * blk, blk)
            kr_j = kr_ref[pl.ds(k0, blk), :]
            out = []
            for h in range(hg):
                m_i, l_i = carry[h]
                kv = jnp.concatenate([kn_ref[pl.ds(k0, blk), h * QK_NOPE:(h + 1) * QK_NOPE], kr_j], axis=1)
                qv = q_ref[:, h * HEAD_PAD:(h + 1) * HEAD_PAD]
                s = lax.dot_general(kv, qv, _NT, preferred_element_type=F32) * ATTN_SCALE
                if diagonal:
                    s = jnp.where(_diag_mask(blk), s, -jnp.inf)
                m_new = jnp.maximum(m_i, jnp.max(s, axis=0, keepdims=True))
                p = jnp.exp(s - m_new)
                alpha = jnp.exp(m_i - m_new)
                l_new = alpha * l_i + jnp.sum(p, axis=0, keepdims=True)
                acc[h] = alpha * acc[h] + jnp.dot(vt_ref[h * V_DIM:(h + 1) * V_DIM, pl.ds(k0, blk)], p.astype(BF16),
                                                  preferred_element_type=F32)
                out.append((m_new, l_new))
            return tuple(out)

        init = tuple((jnp.full((1, blk), -jnp.inf, F32), jnp.zeros((1, blk), F32)) for _ in range(hg))
        carry = lax.fori_loop(0, qi, lambda j, c: step(j, c, False), init)
        stats = step(qi, carry, True)
        for h in range(hg):
            m_i, l_i = stats[h]
            o_ref[:, h * V_DIM:(h + 1) * V_DIM] = (acc[h] / l_i).T
            lse_ref[h] = m_i + jnp.log(l_i)

        @pl.when(last)
        def _():
            _exchange(blk_ref, all_ref, send_sems, recv_sems, local_sem, finish=True, gather=True)

    return pl.pallas_call(
        body, grid=steps,
        in_specs=[pl.BlockSpec((blk, hg * HEAD_PAD), lambda b, g, i: (b * nq + i, g)),
                  pl.BlockSpec((seq, hg * QK_NOPE), lambda b, g, i: (b, g)),
                  pl.BlockSpec((seq, LANES), lambda b, g, i: (b, 0)),
                  pl.BlockSpec((hg * V_DIM, seq), lambda b, g, i: (g, b)), _ANY],
        out_specs=[pl.BlockSpec((blk, hg * V_DIM), lambda b, g, i: (b * nq + i, g)),
                   pl.BlockSpec((hg, 1, blk), lambda b, g, i: (g, 0, b * nq + i)), _ANY],
        out_shape=[jax.ShapeDtypeStruct((t, N_HEADS * V_DIM), F32), jax.ShapeDtypeStruct((N_HEADS, 1, t), F32),
                   jax.ShapeDtypeStruct((N_DEV,) + block.shape, block.dtype)],
        scratch_shapes=[pltpu.VMEM((hg, V_DIM, blk), F32)] + _EXCHANGE_SEMS,
        compiler_params=_params(("arbitrary", "arbitrary", "arbitrary")), name=name)(q, kn, kr, v_t, block)


def _attn_bwd(q, kn, kr, kn_t, kr_t, v, o, lse, do, cos, sin, parts, *, bsz, seq, name):
    t = bsz * seq
    blk = _attn_block(seq)
    nq = seq // blk
    hg = BWD_HEADS
    steps = (bsz, N_HEADS // hg)

    def body(q_ref, kn_ref, kr_ref, knt_ref, krt_ref, v_ref, o_ref, lse_ref, do_ref, cos_ref, sin_ref, parts_ref,
             dq_ref, dkn_ref, dkr_ref, dv_ref, land_ref, dqt_acc, dk_acc, dv_acc, send_sems, recv_sems, local_sem):
        first, last = _first_last(steps)

        @pl.when(first)
        def _():
            _exchange(parts_ref, land_ref, send_sems, recv_sems, local_sem, finish=False)

        dqt_acc[...] = jnp.zeros_like(dqt_acc)
        dk_acc[...] = jnp.zeros_like(dk_acc)
        dv_acc[...] = jnp.zeros_like(dv_acc)

        def q_block(i, _):
            q0 = pl.multiple_of(i * blk, blk)
            rows = []
            for h in range(hg):
                dov = do_ref[pl.ds(q0, blk), h * V_DIM:(h + 1) * V_DIM].astype(F32)
                dcol = jnp.sum(dov * o_ref[pl.ds(q0, blk), h * V_DIM:(h + 1) * V_DIM], axis=-1, keepdims=True)
                delta = jnp.broadcast_to(dcol, (blk, LANES)).T[0:1, :]
                rows.append((lse_ref[h, :, pl.ds(q0, blk)], delta))

            def pair(j, diagonal):
                k0 = pl.multiple_of(j * blk, blk)
                kr_j = kr_ref[pl.ds(k0, blk), :]
                krt_j = krt_ref[:, pl.ds(k0, blk)]
                for h in range(hg):
                    lse_i, delta = rows[h]
                    qv = q_ref[pl.ds(q0, blk), h * HEAD_PAD:(h + 1) * HEAD_PAD]
                    dov = do_ref[pl.ds(q0, blk), h * V_DIM:(h + 1) * V_DIM]
                    kv = jnp.concatenate([kn_ref[pl.ds(k0, blk), h * QK_NOPE:(h + 1) * QK_NOPE], kr_j], axis=1)
                    s = lax.dot_general(kv, qv, _NT, preferred_element_type=F32) * ATTN_SCALE
                    p = jnp.exp(s - lse_i)
                    if diagonal:
                        p = jnp.where(_diag_mask(blk), p, 0.0)
                    dv_acc[pl.ds(k0, blk), h * V_DIM:(h + 1) * V_DIM] += jnp.dot(
                        p.astype(BF16), dov, preferred_element_type=F32)
                    dp = lax.dot_general(v_ref[pl.ds(k0, blk), h * V_DIM:(h + 1) * V_DIM], dov, _NT,
                                         preferred_element_type=F32)
                    ds = (p * (dp - delta) * ATTN_SCALE).astype(BF16)
                    dk_acc[pl.ds(k0, blk), h * HEAD_PAD:(h + 1) * HEAD_PAD] += jnp.dot(ds, qv, preferred_element_type=F32)
                    base = h * HEAD_PAD
                    dqt_acc[base:base + QK_NOPE, pl.ds(q0, blk)] += jnp.dot(
                        knt_ref[h * QK_NOPE:(h + 1) * QK_NOPE, pl.ds(k0, blk)], ds, preferred_element_type=F32)
                    dqt_acc[base + QK_NOPE:base + HEAD_PAD, pl.ds(q0, blk)] += jnp.dot(
                        krt_j, ds, preferred_element_type=F32)

            def off_diagonal(j, _):
                pair(j, False)
                return 0

            lax.fori_loop(0, i, off_diagonal, 0)
            pair(i, True)
            return 0

        lax.fori_loop(0, nq, q_block, 0)
        dkr = jnp.zeros((seq, LANES), F32)
        for h in range(hg):
            base = h * HEAD_PAD
            for i in range(nq):
                rows = slice(i * blk, (i + 1) * blk)
                dq = dqt_acc[base:base + HEAD_PAD, rows].T
                dq_ref[rows, base:base + QK_NOPE] = dq[:, :QK_NOPE].astype(BF16)
                dq_ref[rows, base + QK_NOPE:base + HEAD_PAD] = _rope_t(
                    dq[:, QK_NOPE:], cos_ref[rows, :], sin_ref[rows, :]).astype(BF16)
            dkn_ref[:, h * QK_NOPE:(h + 1) * QK_NOPE] = dk_acc[:, base:base + QK_NOPE].astype(BF16)
            dkr = dkr + dk_acc[:, base + QK_NOPE:base + HEAD_PAD]
        dv_ref[...] = dv_acc[...].astype(BF16)

        @pl.when(pl.program_id(1) == 0)
        def _():
            dkr_ref[...] = jnp.zeros_like(dkr_ref)

        dkr_ref[...] += _rope_t(dkr, cos_ref[...], sin_ref[...])

        @pl.when(last)
        def _():
            _exchange(parts_ref, land_ref, send_sems, recv_sems, local_sem, finish=True)

    head = pl.BlockSpec((seq, hg * V_DIM), lambda b, g: (b, g))
    head_t = pl.BlockSpec((hg * V_DIM, seq), lambda b, g: (g, b))
    shared = pl.BlockSpec((seq, LANES), lambda b, g: (b, 0))
    shared_t = pl.BlockSpec((LANES, seq), lambda b, g: (0, b))
    table = pl.BlockSpec((seq, LANES), lambda b, g: (0, 0))
    qspec = pl.BlockSpec((seq, hg * HEAD_PAD), lambda b, g: (b, g))
    return pl.pallas_call(
        body, grid=steps,
        in_specs=[qspec, head, shared, head_t, shared_t, head, head,
                  pl.BlockSpec((hg, 1, seq), lambda b, g: (g, 0, b)), head, table, table, _ANY],
        out_specs=[qspec, head, shared, head, _ANY],
        out_shape=[jax.ShapeDtypeStruct((t, N_HEADS * HEAD_PAD), BF16), jax.ShapeDtypeStruct((t, N_HEADS * QK_NOPE), BF16),
                   jax.ShapeDtypeStruct((t, LANES), F32), jax.ShapeDtypeStruct((t, N_HEADS * V_DIM), BF16),
                   jax.ShapeDtypeStruct(parts.shape, parts.dtype)],
        scratch_shapes=[pltpu.VMEM((hg * HEAD_PAD, seq), F32), pltpu.VMEM((seq, hg * HEAD_PAD), F32),
                        pltpu.VMEM((seq, hg * V_DIM), F32)] + _EXCHANGE_SEMS,
        compiler_params=_params(("arbitrary", "arbitrary")), name=name)(
            q, kn, kr, kn_t, kr_t, v, o, lse, do, cos, sin, parts)


def _head_and_loss(o, g2, x1, target, w_out, g_final, *, name, bt=512):
    t, d = x1.shape
    bt = min(bt, t)
    nt = (((1,), (1,)), ((), ()))

    def body(o_ref, g2_ref, x1_ref, tgt_ref, w_ref, gf_ref, loss_ref, dx2_ref, y2_ref, do_ref, dg2_ref, dgf_ref):
        @pl.when(pl.program_id(0) == 0)
        def _():
            loss_ref[...] = jnp.zeros_like(loss_ref)
            dgf_ref[...] = jnp.zeros_like(dgf_ref)

        ov = o_ref[...]
        gv = g2_ref[...]
        sg = _sigmoid(gv)
        silu = gv * sg
        y2 = (ov * silu).astype(BF16)
        y2_ref[...] = y2
        w = w_ref[...]
        x2 = x1_ref[...] + jnp.dot(y2, w, preferred_element_type=F32)
        r = lax.rsqrt(jnp.mean(x2 * x2, axis=-1, keepdims=True) + EPS)
        nrm = x2 * r
        gf = gf_ref[...]
        err = nrm * gf - tgt_ref[...]
        loss_ref[...] += 0.5 * jnp.sum(jnp.mean(err * err, axis=-1, keepdims=True))
        dyf = err * (1.0 / d)
        dgf_ref[...] += jnp.sum(dyf * nrm, axis=0, keepdims=True)
        dn = dyf * gf
        dx2 = r * (dn - nrm * jnp.mean(dn * nrm, axis=-1, keepdims=True))
        dx2_ref[...] = dx2
        dy2 = lax.dot_general(dx2.astype(BF16), w, nt, preferred_element_type=F32)
        do_ref[...] = (dy2 * silu).astype(BF16)
        dg2_ref[...] = (dy2 * ov * (sg * (1.0 + gv * (1.0 - sg)))).astype(BF16)

    row = pl.BlockSpec((bt, d), lambda i: (i, 0))
    vec = pl.BlockSpec((1, d), lambda i: (0, 0))
    return pl.pallas_call(
        body, grid=(t // bt,),
        in_specs=[row, row, row, row, pl.BlockSpec((d, d), lambda i: (0, 0)), vec],
        out_specs=[pl.BlockSpec((8, LANES), lambda i: (0, 0)), row, row, row, row, vec],
        out_shape=[jax.ShapeDtypeStruct((8, LANES), F32), jax.ShapeDtypeStruct((t, d), F32),
                   jax.ShapeDtypeStruct((t, d), BF16), jax.ShapeDtypeStruct((t, d), BF16),
                   jax.ShapeDtypeStruct((t, d), BF16), jax.ShapeDtypeStruct((1, d), F32)],
        compiler_params=_params(("arbitrary",)), name=name)(o, g2, x1, target, w_out, g_final)


def _sum_parts(parts, *, name, br=GRAD_BLOCK):
    npart, rows, w = parts.shape

    def body(p_ref, o_ref):
        acc = p_ref[0].astype(F32)
        for j in range(1, npart):
            acc = acc + p_ref[j].astype(F32)
        o_ref[...] = acc

    return pl.pallas_call(
        body, grid=(rows // br,), in_specs=[pl.BlockSpec((npart, br, w), lambda i: (0, i, 0))],
        out_specs=pl.BlockSpec((br, w), lambda i: (i, 0)), out_shape=jax.ShapeDtypeStruct((rows, w), F32),
        compiler_params=_params(("parallel",)), name=name)(parts)


def _chip_partial(parts, recv, *, name, br=GRAD_BLOCK):
    _, rows, w = parts.shape
    core = lax.axis_index("c").astype(jnp.int32).reshape(1)

    def body(c_ref, p_ref, r_ref, o_ref):
        o_ref[...] = (p_ref[...] + r_ref[...]).astype(BF16)

    grid_spec = pltpu.PrefetchScalarGridSpec(
        num_scalar_prefetch=1, grid=(4, rows // br),
        in_specs=[pl.BlockSpec((None, br, w), lambda k, i, c_ref: (2 * k + c_ref[0], i, 0)),
                  pl.BlockSpec((None, br, w), lambda k, i, c_ref: (k, i, 0))],
        out_specs=pl.BlockSpec((None, br, w), lambda k, i, c_ref: (k, i, 0)))
    return pl.pallas_call(
        body, grid_spec=grid_spec, out_shape=jax.ShapeDtypeStruct((4, rows, w), BF16),
        compiler_params=_params(("parallel", "parallel")), name=name)(core, parts, recv)


def _as_block(a):
    if a.ndim == 1:
        return a.reshape(1, -1)
    if a.ndim > 2 and a.shape[0] == 1:
        return a.reshape(a.shape[1:])
    return a


def _adamw(g, w, m, v, *, name):
    shape = w.shape
    g, w, m, v = (_as_block(a) for a in (g, w, m, v))

    def body(g_ref, w_ref, m_ref, v_ref, d_ref, nm_ref, nv_ref):
        gv = g_ref[...]
        nm = ADAM_B1 * m_ref[...] + (1.0 - ADAM_B1) * gv
        nv = ADAM_B2 * v_ref[...] + (1.0 - ADAM_B2) * (gv * gv)
        nm_ref[...] = nm
        nv_ref[...] = nv
        m_hat = nm / (1.0 - ADAM_B1 ** ADAM_STEP)
        v_hat = nv / (1.0 - ADAM_B2 ** ADAM_STEP)
        d_ref[...] = (-ADAM_LR) * (m_hat / (jnp.sqrt(v_hat) + ADAM_EPS) + ADAM_WD * w_ref[...])

    whole = pl.BlockSpec(memory_space=pltpu.VMEM)
    outs = pl.pallas_call(
        body, in_specs=[whole] * 4, out_specs=[whole] * 3, out_shape=[jax.ShapeDtypeStruct(w.shape, F32)] * 3,
        compiler_params=_params(), name=name)(g, w, m, v)
    return [o.reshape(shape) for o in outs]


def _all_gather(block, *, name):
    m, n = block.shape

    def body(x_ref, out_ref, send_sems, recv_sems, local_sem):
        for phase in range(3):
            _gather_two_level(x_ref, out_ref, send_sems, recv_sems, local_sem, phase=phase)

    return pl.pallas_call(
        body, out_shape=jax.ShapeDtypeStruct((N_DEV, m, n), block.dtype), in_specs=[_ANY], out_specs=_ANY,
        scratch_shapes=_EXCHANGE_SEMS, name=name)(block)


def _exchange_d2d(parts, *, name):
    _, rows, w = parts.shape

    def body(p_ref, land_ref, send_sems, recv_sems):
        x, y, c = _mesh_pos()
        sends = []
        for k in range(4):
            cp = pltpu.make_async_remote_copy(
                src_ref=p_ref.at[2 * k + (1 - c)], dst_ref=land_ref.at[k], send_sem=send_sems.at[k],
                recv_sem=recv_sems.at[k], device_id=(x, y, 1 - c), device_id_type=pl.DeviceIdType.MESH)
            cp.start()
            sends.append(cp)
        for cp in sends:
            cp.wait_recv()
        for cp in sends:
            cp.wait_send()

    return pl.pallas_call(
        body, out_shape=jax.ShapeDtypeStruct((4, rows, w), parts.dtype), in_specs=[_ANY], out_specs=_ANY,
        scratch_shapes=[pltpu.SemaphoreType.DMA((4,)), pltpu.SemaphoreType.DMA((4,))], name=name)(parts)


def _exchange_ici(parts, *, name):
    def body(p_ref, land_ref, send_sems, recv_sems, local_sem):
        x, y, c = _mesh_pos()
        mine = pltpu.make_async_copy(p_ref.at[2 * x + y], land_ref.at[3], local_sem)
        mine.start()
        sends = []
        for k, (px, py) in enumerate([(1 - x, y), (x, 1 - y), (1 - x, 1 - y)]):
            cp = pltpu.make_async_remote_copy(
                src_ref=p_ref.at[2 * px + py], dst_ref=land_ref.at[k], send_sem=send_sems.at[k],
                recv_sem=recv_sems.at[k], device_id=(px, py, c), device_id_type=pl.DeviceIdType.MESH)
            cp.start()
            sends.append(cp)
        for cp in sends:
            cp.wait_recv()
        for cp in sends:
            cp.wait_send()
        mine.wait()

    return pl.pallas_call(
        body, out_shape=jax.ShapeDtypeStruct(parts.shape, parts.dtype), in_specs=[_ANY], out_specs=_ANY,
        scratch_shapes=[pltpu.SemaphoreType.DMA((3,)), pltpu.SemaphoreType.DMA((3,)), pltpu.SemaphoreType.DMA(())],
        name=name)(parts)


def _rows(a):
    return a.reshape(-1, PACK_W)


def _pad_to(a, n):
    return jnp.pad(a, (0, n - a.shape[0]))


def _weight_blocks(d):
    small = _rows(_pad_to(jnp.concatenate([d[n].reshape(-1) for n, _ in _SMALL]), 16 * PACK_W))
    bits = lax.bitcast_convert_type(small, jnp.uint32)
    halves = [lax.bitcast_convert_type(h.astype(jnp.uint16), WIRE) for h in (bits >> 16, bits & 0xFFFF)]
    block_a = jnp.concatenate([d["w_in_a"][0].T.astype(WIRE)] + halves, axis=0)
    w_uq = jnp.pad(d["w_uq"][0], ((0, 0), (0, 0), (0, HEAD_PAD - QK_NOPE - QK_ROPE)))
    pieces = {"w_out_a": d["w_out_a"], "w_dkv": d["w_dkv"], "w_uk": d["w_uk"], "w_uv": d["w_uv"],
              "w_in_b": d["w_in_b"][0].T, "w_uq": w_uq}
    block_b = jnp.concatenate([_rows(pieces[n]) for n, _ in _PIECES_B]
                              + [jnp.zeros((WIRE_ROWS_B - MATRIX_ROWS_B, PACK_W), F32)], axis=0).astype(WIRE)
    return block_a, block_b, d["w_out_b"][0].astype(WIRE)


def _weights_a(wall):
    w = {}
    lo, hi = _OFF_A["w_in_a"]
    w["w_in_a_t"] = wall[:, lo:hi].reshape(2 * D_RNN, D_MODEL)
    high, low = (lax.bitcast_convert_type(wall[:, r:r + 16], jnp.uint16).astype(jnp.uint32)
                 for r in (MATRIX_ROWS_A, MATRIX_ROWS_A + 16))
    small = lax.bitcast_convert_type((high << 16) | low, F32)[:, :8].reshape(N_DEV, 8 * PACK_W)
    off = dict(zip([n for n, _ in _SMALL], [0, 128, 768, 928, 1088, 1248]))
    w["norm_a"] = small[:, :128].reshape(1, D_MODEL)

    def by_channel(lo, rows):
        a = small[:, lo:lo + rows * (D_RNN // N_DEV)].reshape(N_DEV, rows, -1).transpose(1, 0, 2).reshape(rows, D_RNN)
        return jnp.pad(a, ((0, 8 - rows), (0, 0)))

    w["conv_taps"] = by_channel(off["conv_w"], CONV_WIDTH)
    w["lru_vecs"] = by_channel(off["conv_b"], 4)
    return w


def _weights_b(wall):
    piece = {n: wall[:, lo:hi] for n, (lo, hi) in _OFF_B.items()}
    w = {"w_out_a": piece["w_out_a"].reshape(D_RNN, D_MODEL)}
    w_dkv = piece["w_dkv"].reshape(D_MODEL, KV_RANK + QK_ROPE)
    w["w_dkv_c"] = w_dkv[:, :KV_RANK]
    w["w_dkv_r"] = jnp.pad(w_dkv[:, KV_RANK:], ((0, 0), (0, LANES - QK_ROPE)))
    w["w_uk"] = piece["w_uk"].reshape(KV_RANK, N_HEADS * QK_NOPE)
    w["w_uv"] = piece["w_uv"].reshape(KV_RANK, N_HEADS * V_DIM)
    w["w_in_b_t"] = piece["w_in_b"].reshape(Q_RANK + N_HEADS * V_DIM, D_MODEL)
    w["w_uq"] = piece["w_uq"].reshape(Q_RANK, N_HEADS * HEAD_PAD)
    return w


def _pack_rep(d):
    flat = jnp.concatenate([d[n].reshape(-1) for n, _ in _REP])
    return _rows(_pad_to(flat, REP_ROWS * PACK_W))


def _unpack_rep(p, like):
    flat = p.reshape(-1)
    out, off = {}, 0
    for n, k in _REP:
        out[n] = flat[off:off + k].reshape(like[n].shape)
        off += k
    return out


def _by_owner(a):
    return a.reshape(N_DEV, -1, PACK_W)


def _grad_parts_b(g):
    tail = jnp.zeros((N_DEV, WIRE_ROWS_B - MATRIX_ROWS_B, PACK_W), F32)
    return jnp.concatenate([_by_owner(g[n]) for n, _ in _PIECES_B] + [tail], axis=1).astype(BF16)


def _grad_parts_a(g):
    small = jnp.concatenate([
        g["norm_a"].reshape(N_DEV, -1),
        g["conv_w"].reshape(CONV_WIDTH, N_DEV, -1).transpose(1, 0, 2).reshape(N_DEV, -1),
        g["conv_b"].reshape(N_DEV, -1), g["b_rg"].reshape(N_DEV, -1), g["b_ig"].reshape(N_DEV, -1),
        g["lru_lambda"].reshape(N_DEV, -1)], axis=1)
    small = jnp.pad(small, ((0, 0), (0, 8 * PACK_W - small.shape[1]))).reshape(N_DEV, 8, PACK_W)
    half = N_DEV // 2
    w_in_a = jnp.concatenate([h.reshape(half, -1, PACK_W) for h in g["w_in_a_t"]], axis=0)
    rep = _pack_rep(g).reshape(N_DEV, REP_SLICE, PACK_W)
    tail = jnp.zeros((N_DEV, GRAD_ROWS_A - MATRIX_ROWS_A - 8 - REP_SLICE, PACK_W), F32)
    return jnp.concatenate([w_in_a, small, rep, tail], axis=1)


def _own_grads(sum_a, sum_b, sum_c):
    out = {}
    lo, hi = _OFF_A["w_in_a"]
    out["w_in_a"] = sum_a[lo:hi].T.reshape(1, D_MODEL, 2 * D_RNN // N_DEV)
    small = sum_a[MATRIX_ROWS_A:MATRIX_ROWS_A + 8].reshape(-1)
    shapes = {"norm_a": (1, D_MODEL // N_DEV), "conv_w": (1, CONV_WIDTH, D_RNN // N_DEV), "conv_b": (1, D_RNN // N_DEV),
              "b_rg": (1, D_RNN // N_DEV), "b_ig": (1, D_RNN // N_DEV), "lru_lambda": (1, D_RNN // N_DEV)}
    off = 0
    for n, k in _SMALL:
        out[n] = small[off:off + k].reshape(shapes[n])
        off += k
    piece = {n: sum_b[lo:hi] for n, (lo, hi) in _OFF_B.items()}
    out["w_out_a"] = piece["w_out_a"].reshape(1, D_RNN // N_DEV, D_MODEL)
    out["w_dkv"] = piece["w_dkv"].reshape(D_MODEL // N_DEV, KV_RANK + QK_ROPE)
    out["w_uk"] = piece["w_uk"].reshape(KV_RANK // N_DEV, N_HEADS, QK_NOPE)
    out["w_uv"] = piece["w_uv"].reshape(KV_RANK // N_DEV, N_HEADS, V_DIM)
    out["w_in_b"] = piece["w_in_b"].T.reshape(1, D_MODEL, (Q_RANK + N_HEADS * V_DIM) // N_DEV)
    out["w_uq"] = piece["w_uq"].reshape(1, Q_RANK // N_DEV, N_HEADS, HEAD_PAD)[..., :QK_NOPE + QK_ROPE]
    out["w_out_b"] = sum_c.reshape(1, N_HEADS * V_DIM // N_DEV, D_MODEL)
    return out


def _step(x, target, w, rep, block_b, block_c, *, bsz, seq):
    t = bsz * seq
    cos, sin = _rope_tables(seq)
    g_a = w["norm_a"]
    g_kv = rep["norm_kv"].reshape(1, -1)
    g_kvn = rep["kv_norm"].reshape(1, -1)
    g_b = rep["norm_b"].reshape(1, -1)
    g_q = rep["q_norm"].reshape(1, -1)
    g_f = rep["final_norm"].reshape(1, -1)
    wrg = rep["w_rg"][0].astype(BF16)
    wig = rep["w_ig"][0].astype(BF16)
    cw8, vecs = w["conv_taps"], w["lru_vecs"]

    def seq3(a):
        return a.reshape(bsz, seq, a.shape[-1])

    def flat(a):
        return a.reshape(t, a.shape[-1])

    h0, xp, ga = _lru_proj_fwd(x, g_a, w["w_in_a_t"], name="lru_proj_fwd")
    xb, hs, y, wall_b = _lru_fwd(seq3(xp), seq3(ga), cw8, vecs, wrg, wig, block_b, name="lru_fwd")
    w = dict(w, **_weights_b(wall_b))
    x1 = _matmul(flat(y), w["w_out_a"], residual=x, name="out_a")
    hk, hq, ck, cqp, g2, ckv, cq, q, kn, v, kr, kn_t, v_t, kr_t = _mla_proj_fwd(
        x1, (g_kv, g_b, g_kvn, g_q), w, cos, sin, seq=seq, name="mla_proj_fwd")
    o, lse, wall_c = _attn_fwd(q, kn, kr, v_t, block_c, bsz=bsz, seq=seq, name="attn_fwd")
    w_out_b = wall_c.reshape(N_HEADS * V_DIM, D_MODEL)
    loss, dx2, y2, do, dg2, dgf = _head_and_loss(o, g2, x1, target, w_out_b, g_f, name="head_loss")
    grads = {"final_norm": dgf}
    parts_c = _by_owner(_matmul_tn(y2, dx2, name="d_w_out_b")).astype(BF16)
    dq, dkn, dkr, dv, landed_c = _attn_bwd(q, kn, kr, kn_t, kr_t, v, o, lse, do, cos, sin, parts_c,
                                           bsz=bsz, seq=seq, name="attn_bwd")
    grads["w_uq"] = _matmul_tn(cq, dq, name="d_w_uq")
    dx1, du2, dckr, dgkv, dgb, dgkvn, dgq, dy = _mla_proj_bwd(
        x1, dx2, cqp, ck, dq, dkn, dv, dkr, dg2, (g_kv, g_b, g_kvn, g_q), w, name="mla_proj_bwd")
    grads["norm_kv"], grads["norm_b"], grads["kv_norm"], grads["q_norm"] = dgkv, dgb, dgkvn, dgq
    grads["w_in_b"] = _matmul_tn(du2, hq, name="d_w_in_b_t")
    grads["w_uk"] = _matmul_tn(ckv, dkn, name="d_w_uk")
    grads["w_uv"] = _matmul_tn(ckv, dv, name="d_w_uv")
    grads["w_dkv"] = _matmul_tn(hk, dckr, name="d_w_dkv")[:, :KV_RANK + QK_ROPE]
    grads["w_out_a"] = _matmul_tn(flat(y), dx1, name="d_w_out_a")
    parts_b = _grad_parts_b(grads)
    dxp, dga, dwrg, dwig, dvec, landed_b = _lru_bwd(
        seq3(dy), seq3(xp), xb, hs, seq3(ga), cw8, vecs, wrg, wig, parts_b, name="lru_bwd")
    dxp, dga = flat(dxp), flat(dga)
    grads["w_rg"], grads["w_ig"] = dwrg, dwig
    grads["b_rg"], grads["b_ig"], grads["conv_b"] = dvec[0], dvec[1], dvec[3]
    lam = vecs[3]
    grads["lru_lambda"] = dvec[2] * (-1.0 / (1.0 + jnp.exp(lam)))
    grads["conv_w"] = dvec[4:4 + CONV_WIDTH]
    grads["w_in_a_t"] = (_matmul_tn(dxp, h0, name="d_w_in_a_x_t"), _matmul_tn(dga, h0, name="d_w_in_a_g_t"))
    dx, dga_norm = _lru_proj_bwd(dxp, dga, x, dx1, g_a, w["w_in_a_t"], name="lru_proj_bwd")
    grads["norm_a"] = dga_norm
    return loss[0, 0], dx, grads, landed_b, landed_c


def kernel(x, norm_a, w_in_a, conv_w, conv_b, w_rg, b_rg, w_ig, b_ig, lru_lambda, w_out_a, norm_kv, w_dkv, kv_norm, w_uk, w_uv, norm_b, w_in_b, q_norm, w_uq, w_out_b, final_norm, loss_target, m_norm_a, m_w_in_a, m_conv_w, m_conv_b, m_w_rg, m_b_rg, m_w_ig, m_b_ig, m_lru_lambda, m_w_out_a, m_norm_kv, m_w_dkv, m_kv_norm, m_w_uk, m_w_uv, m_norm_b, m_w_in_b, m_q_norm, m_w_uq, m_w_out_b, m_final_norm, v_norm_a, v_w_in_a, v_conv_w, v_conv_b, v_w_rg, v_b_rg, v_w_ig, v_b_ig, v_lru_lambda, v_w_out_a, v_norm_kv, v_w_dkv, v_kv_norm, v_w_uk, v_w_uv, v_norm_b, v_w_in_b, v_q_norm, v_w_uq, v_w_out_b, v_final_norm):
    given = dict(locals())
    wts = {n: given[n] for n in WEIGHTS}
    mom1 = {n: given["m_" + n] for n in WEIGHTS}
    mom2 = {n: given["v_" + n] for n in WEIGHTS}
    bsz, seq, _ = x.shape
    t = bsz * seq

    block_a, block_b, block_c = _weight_blocks(wts)
    w = _weights_a(_all_gather(block_a, name="gather_weights_a"))
    loss, dx, grads, landed_b, landed_c = _step(x.reshape(t, D_MODEL), loss_target.reshape(t, D_MODEL), w, wts,
                                                block_b, block_c, bsz=bsz, seq=seq)

    parts_a = _grad_parts_a(grads)
    from_sibling = _exchange_d2d(parts_a, name="exchange_grads_d2d")
    chip_parts = _chip_partial(parts_a, from_sibling, name="chip_partial_grads")
    landed_a = _exchange_ici(chip_parts, name="exchange_grads_ici")
    sum_a = _sum_parts(landed_a, name="sum_grads_a", br=GRAD_BLOCK)
    sum_b = _sum_parts(landed_b, name="sum_grads_b", br=WIRE_ROWS_B // 2)
    sum_c = _sum_parts(landed_c, name="sum_grads_c", br=landed_c.shape[1])
    g_own = _own_grads(sum_a, sum_b, sum_c)
    rep_slice = sum_a[MATRIX_ROWS_A + 8:MATRIX_ROWS_A + 8 + REP_SLICE]
    loss_rows = jnp.pad(loss.reshape(1, 1), ((0, 7), (0, PACK_W - 1)))
    gathered = _all_gather(jnp.concatenate([rep_slice, loss_rows], axis=0), name="gather_replicated")
    g_own.update(_unpack_rep(gathered[:, :REP_SLICE].reshape(REP_ROWS, PACK_W), wts))
    loss = jnp.sum(gathered[:, REP_SLICE, 0])

    deltas, new_m, new_v = {}, {}, {}
    for n in WEIGHTS:
        deltas[n], new_m[n], new_v[n] = _adamw(g_own[n], wts[n], mom1[n], mom2[n], name="adamw_" + n)
    result = [loss, dx.reshape(bsz, seq, D_MODEL)]
    for d in (g_own, deltas, new_m, new_v):
        result.extend(d[n] for n in WEIGHTS)
    return tuple(result)
```

```python
import jax
import jax.numpy as jnp
from jax import lax
from jax.experimental import pallas as pl
from jax.experimental.pallas import tpu as pltpu

F32 = jnp.float32
BF16 = jnp.bfloat16
WIRE = jnp.bfloat16

D_MODEL = 1024
D_RNN = 1280
RNN_BLOCKS = 10
RNN_BW = 128
CONV_WIDTH = 4
LRU_C = 8.0
N_HEADS = 8
QK_NOPE = 128
QK_ROPE = 64
V_DIM = 128
KV_RANK = 256
Q_RANK = 384
ROPE_THETA = 10000.0
EPS = 1e-6
ATTN_SCALE = (QK_NOPE + QK_ROPE) ** -0.5
HEAD_PAD = 256
LANES = 128

ADAM_LR = 0.001
ADAM_B1 = 0.9
ADAM_B2 = 0.999
ADAM_EPS = 1e-08
ADAM_WD = 0.01
ADAM_STEP = 10

N_DEV = 8
VMEM_LIMIT_BYTES = 56 * 2**20
PACK_W = 1024

_PIECES_A = (("w_in_a", 320),)
_PIECES_B = (("w_out_a", 160), ("w_dkv", 40), ("w_uk", 32), ("w_uv", 32), ("w_in_b", 176), ("w_uq", 96))


def _offsets(pieces):
    off, r = {}, 0
    for n, k in pieces:
        off[n] = (r, r + k)
        r += k
    return off, r


_OFF_A, MATRIX_ROWS_A = _offsets(_PIECES_A)
_OFF_B, MATRIX_ROWS_B = _offsets(_PIECES_B)
WIRE_ROWS_A = MATRIX_ROWS_A + 32
WIRE_ROWS_B = 544
_SMALL = (("norm_a", 128), ("conv_w", 640), ("conv_b", 160), ("b_rg", 160), ("b_ig", 160), ("lru_lambda", 160))
_REP = (("w_rg", 163840), ("w_ig", 163840), ("norm_kv", 1024), ("kv_norm", 256), ("norm_b", 1024),
        ("q_norm", 384), ("final_norm", 1024))
REP_ROWS = 384
REP_SLICE = REP_ROWS // N_DEV
GRAD_ROWS_A = 384
GRAD_BLOCK = 192

WEIGHTS = ("norm_a", "w_in_a", "conv_w", "conv_b", "w_rg", "b_rg", "w_ig", "b_ig", "lru_lambda", "w_out_a",
           "norm_kv", "w_dkv", "kv_norm", "w_uk", "w_uv", "norm_b", "w_in_b", "q_norm", "w_uq", "w_out_b",
           "final_norm")


def _params(sem=None):
    return pltpu.CompilerParams(dimension_semantics=sem, vmem_limit_bytes=VMEM_LIMIT_BYTES)


_NT = (((1,), (1,)), ((), ()))
_ANY = pl.BlockSpec(memory_space=pl.ANY)


def _mesh_pos():
    return lax.axis_index("x"), lax.axis_index("y"), lax.axis_index("c")


def _sigmoid(z):
    return 0.5 * jnp.tanh(0.5 * z) + 0.5


def _sigmoid_tail(z):
    return 1.0 / (1.0 + jnp.exp(-z))


def _col_block(n):
    return n if n <= 1408 else n // 2


def _matmul(a, b, *, name, nt=False, out_dtype=F32, residual=None, bm=1024):
    m, k = a.shape
    n = b.shape[0] if nt else b.shape[1]
    bm = min(bm, m)
    bn = _col_block(n)
    dims = (((1,), (1,)), ((), ())) if nt else (((1,), (0,)), ((), ()))
    has_res = residual is not None

    def body(*refs):
        a_ref, b_ref, o_ref = refs[0], refs[1], refs[-1]
        acc = lax.dot_general(a_ref[...].astype(BF16), b_ref[...].astype(BF16), dims, preferred_element_type=F32)
        if has_res:
            acc = acc + refs[2][...]
        o_ref[...] = acc.astype(out_dtype)

    in_specs = [pl.BlockSpec((bm, k), lambda i, j: (i, 0)),
                pl.BlockSpec((bn, k), lambda i, j: (j, 0)) if nt else pl.BlockSpec((k, bn), lambda i, j: (0, j))]
    args = [a, b]
    if has_res:
        in_specs.append(pl.BlockSpec((bm, bn), lambda i, j: (i, j)))
        args.append(residual)
    return pl.pallas_call(
        body, grid=(m // bm, n // bn), in_specs=in_specs, out_specs=pl.BlockSpec((bm, bn), lambda i, j: (i, j)),
        out_shape=jax.ShapeDtypeStruct((m, n), out_dtype), compiler_params=_params(("parallel", "parallel")),
        name=name)(*args)


def _matmul_tn(a, b, *, name, bt=1024):
    t, m = a.shape
    n = b.shape[1]
    bt = min(bt, t)
    bm, bn = _col_block(m), _col_block(n)

    def body(a_ref, b_ref, o_ref):
        @pl.when(pl.program_id(2) == 0)
        def _():
            o_ref[...] = jnp.zeros_like(o_ref)

        o_ref[...] += lax.dot_general(a_ref[...].astype(BF16), b_ref[...].astype(BF16),
                                      (((0,), (0,)), ((), ())), preferred_element_type=F32)

    return pl.pallas_call(
        body, grid=(m // bm, n // bn, t // bt),
        in_specs=[pl.BlockSpec((bt, bm), lambda i, j, s: (s, i)), pl.BlockSpec((bt, bn), lambda i, j, s: (s, j))],
        out_specs=pl.BlockSpec((bm, bn), lambda i, j, s: (i, j)),
        out_shape=jax.ShapeDtypeStruct((m, n), F32),
        compiler_params=_params(("parallel", "parallel", "arbitrary")), name=name)(a, b)


def _swap_halves(v):
    ax = v.ndim - 1
    lane = lax.broadcasted_iota(jnp.int32, v.shape, ax)
    up = pltpu.roll(v, LANES - QK_ROPE // 2, axis=ax)
    down = pltpu.roll(v, QK_ROPE // 2, axis=ax)
    return jnp.where(lane < QK_ROPE // 2, up, jnp.where(lane < QK_ROPE, down, 0.0))


def _rope(v, cos, sin):
    return v * cos + _swap_halves(v) * sin


def _rope_t(d, cos, sin):
    return d * cos + _swap_halves(d * sin)


def _rope_tables(seq):
    pos = jnp.arange(seq, dtype=F32)
    inv = ROPE_THETA ** (-jnp.arange(0, QK_ROPE, 2, dtype=F32) / QK_ROPE)
    ang = pos[:, None] * inv[None, :]
    cos, sin = jnp.cos(ang), jnp.sin(ang)
    zero = jnp.zeros((seq, LANES - QK_ROPE), F32)
    return jnp.concatenate([cos, cos, zero], axis=1), jnp.concatenate([-sin, sin, zero], axis=1)


def _rms(v):
    return v * lax.rsqrt(jnp.mean(v * v, axis=-1, keepdims=True) + EPS)


def _const_spec(a):
    return pl.BlockSpec(a.shape, lambda i: (0,) * a.ndim)


def _lru_proj_fwd(x, g_a, w_in_t, *, name, bt=512):
    t, d = x.shape
    bt = min(bt, t)
    n = w_in_t.shape[0] // 2

    def body(x_ref, g_ref, wt_ref, h_ref, xp_ref, ga_ref):
        h = (_rms(x_ref[...]) * g_ref[...]).astype(BF16)
        h_ref[...] = h
        xp_ref[...] = lax.dot_general(h, wt_ref[0:n, :], _NT, preferred_element_type=F32)
        ga_ref[...] = lax.dot_general(h, wt_ref[n:2 * n, :], _NT, preferred_element_type=F32)

    row = lambda w: pl.BlockSpec((bt, w), lambda i: (i, 0))
    return pl.pallas_call(
        body, grid=(t // bt,), in_specs=[row(d), _const_spec(g_a), _const_spec(w_in_t)],
        out_specs=[row(d), row(n), row(n)],
        out_shape=[jax.ShapeDtypeStruct((t, d), BF16), jax.ShapeDtypeStruct((t, n), F32), jax.ShapeDtypeStruct((t, n), F32)],
        compiler_params=_params(("parallel",)), name=name)(x, g_a, w_in_t)


def _mla_proj_fwd(x1, gains, w, cos, sin, *, seq, name, bt=512):
    t, d = x1.shape
    bt = min(bt, seq)
    per_seq = seq // bt
    g_kv, g_b, g_kvn, g_q = gains
    consts = [g_kv, g_b, g_kvn, g_q, w["w_dkv_c"], w["w_dkv_r"], w["w_in_b_t"], w["w_uk"], w["w_uv"], w["w_uq"]]

    def body(x_ref, cos_ref, sin_ref, gkv_ref, gb_ref, gkvn_ref, gq_ref, wdc_ref, wdr_ref, wbt_ref,
             wuk_ref, wuv_ref, wuq_ref,
             hk_ref, hq_ref, ck_ref, cqp_ref, g2_ref, ckv_ref, cq_ref, q_ref, kn_ref, v_ref, kr_ref, knt_ref, vt_ref, krt_ref):
        nrm = _rms(x_ref[...])
        hk = (nrm * gkv_ref[...]).astype(BF16)
        hq = (nrm * gb_ref[...]).astype(BF16)
        hk_ref[...] = hk
        hq_ref[...] = hq
        ck = jnp.dot(hk, wdc_ref[...], preferred_element_type=F32)
        ck_ref[...] = ck
        cqp = lax.dot_general(hq, wbt_ref[0:Q_RANK, :], _NT, preferred_element_type=F32)
        cqp_ref[...] = cqp
        g2_ref[...] = lax.dot_general(hq, wbt_ref[Q_RANK:, :], _NT, preferred_element_type=F32)
        cosv, sinv = cos_ref[...], sin_ref[...]
        kr = _rope(jnp.dot(hk, wdr_ref[...], preferred_element_type=F32), cosv, sinv)
        kr_ref[...] = kr.astype(BF16)
        krt_ref[...] = kr.T.astype(BF16)
        ckv = (_rms(ck) * gkvn_ref[...]).astype(BF16)
        ckv_ref[...] = ckv
        kn = jnp.dot(ckv, wuk_ref[...], preferred_element_type=F32)
        v = jnp.dot(ckv, wuv_ref[...], preferred_element_type=F32)
        kn_ref[...] = kn.astype(BF16)
        v_ref[...] = v.astype(BF16)
        knt_ref[...] = kn.T.astype(BF16)
        vt_ref[...] = v.T.astype(BF16)
        cq = (_rms(cqp) * gq_ref[...]).astype(BF16)
        cq_ref[...] = cq
        for h in range(N_HEADS):
            qh = jnp.dot(cq, wuq_ref[:, h * HEAD_PAD:(h + 1) * HEAD_PAD], preferred_element_type=F32)
            q_ref[:, h * HEAD_PAD:h * HEAD_PAD + QK_NOPE] = qh[:, :QK_NOPE].astype(BF16)
            q_ref[:, h * HEAD_PAD + QK_NOPE:(h + 1) * HEAD_PAD] = _rope(qh[:, QK_NOPE:], cosv, sinv).astype(BF16)

    row = lambda w_: pl.BlockSpec((bt, w_), lambda i: (i, 0))
    col = lambda h_: pl.BlockSpec((h_, bt), lambda i: (0, i))
    tab = pl.BlockSpec((bt, LANES), lambda i: (i % per_seq, 0))
    nh = N_HEADS * V_DIM
    shapes = [((t, d), BF16), ((t, d), BF16), ((t, KV_RANK), F32), ((t, Q_RANK), F32), ((t, nh), F32), ((t, KV_RANK), BF16),
              ((t, Q_RANK), BF16), ((t, N_HEADS * HEAD_PAD), BF16), ((t, nh), BF16), ((t, nh), BF16), ((t, LANES), BF16),
              ((nh, t), BF16), ((nh, t), BF16), ((LANES, t), BF16)]
    out_specs = [row(d), row(d), row(KV_RANK), row(Q_RANK), row(nh), row(KV_RANK), row(Q_RANK), row(N_HEADS * HEAD_PAD),
                 row(nh), row(nh), row(LANES), col(nh), col(nh), col(LANES)]
    return pl.pallas_call(
        body, grid=(t // bt,), in_specs=[row(d), tab, tab] + [_const_spec(a) for a in consts], out_specs=out_specs,
        out_shape=[jax.ShapeDtypeStruct(s, dt) for s, dt in shapes],
        compiler_params=_params(("parallel",)), name=name)(x1, cos, sin, *consts)


def _rms_bwd_rows(xv, dn):
    r = lax.rsqrt(jnp.mean(xv * xv, axis=-1, keepdims=True) + EPS)
    nrm = xv * r
    return r * (dn - nrm * jnp.mean(dn * nrm, axis=-1, keepdims=True)), nrm


def _col_sum(v):
    return jnp.sum(v, axis=0, keepdims=True)


def _lru_proj_bwd(dxp, dga, x, dx1, h0, g_a, w_in_t, *, name, bt=512):
    t, d = x.shape
    bt = min(bt, t)
    n = w_in_t.shape[0] // 2
    tn = (((0,), (0,)), ((), ()))

    def body(dxp_ref, dga_ref, x_ref, dx1_ref, h0_ref, g_ref, wt_ref, dx_ref, dg_ref, dwx_ref, dwg_ref):
        @pl.when(pl.program_id(0) == 0)
        def _():
            for ref in (dg_ref, dwx_ref, dwg_ref):
                ref[...] = jnp.zeros_like(ref)

        dxp_v, dga_v, h0 = dxp_ref[...], dga_ref[...], h0_ref[...]
        dwx_ref[...] += lax.dot_general(dxp_v, h0, tn, preferred_element_type=F32)
        dwg_ref[...] += lax.dot_general(dga_v, h0, tn, preferred_element_type=F32)
        dh = (jnp.dot(dxp_v, wt_ref[0:n, :], preferred_element_type=F32)
              + jnp.dot(dga_v, wt_ref[n:2 * n, :], preferred_element_type=F32))
        dxn, nrm = _rms_bwd_rows(x_ref[...], dh * g_ref[...])
        dg_ref[...] += _col_sum(dh * nrm)
        dx_ref[...] = dx1_ref[...] + dxn

    row = lambda w: pl.BlockSpec((bt, w), lambda i: (i, 0))
    whole = pl.BlockSpec((n, d), lambda i: (0, 0))
    return pl.pallas_call(
        body, grid=(t // bt,),
        in_specs=[row(n), row(n), row(d), row(d), row(d), _const_spec(g_a), _const_spec(w_in_t)],
        out_specs=[row(d), _const_spec(g_a), whole, whole],
        out_shape=[jax.ShapeDtypeStruct((t, d), F32), jax.ShapeDtypeStruct((1, d), F32),
                   jax.ShapeDtypeStruct((n, d), F32), jax.ShapeDtypeStruct((n, d), F32)],
        compiler_params=_params(("arbitrary",)), name=name)(dxp, dga, x, dx1, h0, g_a, w_in_t)


def _mla_proj_bwd(x1, dx2, cqp, ck, dq, dkn, dv, dkr, dg2, gains, w, *, name, bt=512):
    t, d = x1.shape
    bt = min(bt, t)
    g_kv, g_b, g_kvn, g_q = gains
    consts = [g_kv, g_b, g_kvn, g_q, w["w_dkv_c"], w["w_dkv_r"], w["w_in_b_t"], w["w_uk"], w["w_uv"], w["w_uq"],
              w["w_out_a"]]
    nh = N_HEADS * V_DIM

    def body(x1_ref, dx2_ref, cqp_ref, ck_ref, dq_ref, dkn_ref, dv_ref, dkr_ref, dg2_ref,
             gkv_ref, gb_ref, gkvn_ref, gq_ref, wdc_ref, wdr_ref, wbt_ref, wuk_ref, wuv_ref, wuq_ref, wo_ref,
             dx1_ref, du2_ref, dckr_ref, dgkv_ref, dgb_ref, dgkvn_ref, dgq_ref, dy_ref):
        @pl.when(pl.program_id(0) == 0)
        def _():
            for ref in (dgkv_ref, dgb_ref, dgkvn_ref, dgq_ref):
                ref[...] = jnp.zeros_like(ref)

        dot_nt = lambda a, b: lax.dot_general(a, b, _NT, preferred_element_type=F32)
        dcq = dot_nt(dq_ref[...], wuq_ref[...])
        dcqp, nq = _rms_bwd_rows(cqp_ref[...], dcq * gq_ref[...])
        dgq_ref[...] += _col_sum(dcq * nq)
        dcqp = dcqp.astype(BF16)
        dg2 = dg2_ref[...]
        du2_ref[:, :Q_RANK] = dcqp
        du2_ref[:, Q_RANK:] = dg2
        dhq = (jnp.dot(dcqp, wbt_ref[0:Q_RANK, :], preferred_element_type=F32)
               + jnp.dot(dg2, wbt_ref[Q_RANK:, :], preferred_element_type=F32))
        dckv = dot_nt(dkn_ref[...], wuk_ref[...]) + dot_nt(dv_ref[...], wuv_ref[...])
        dck, nc = _rms_bwd_rows(ck_ref[...], dckv * gkvn_ref[...])
        dgkvn_ref[...] += _col_sum(dckv * nc)
        dck = dck.astype(BF16)
        dkr = dkr_ref[...].astype(BF16)
        dckr_ref[:, :KV_RANK] = dck
        dckr_ref[:, KV_RANK:] = dkr
        dhk = dot_nt(dck, wdc_ref[...]) + dot_nt(dkr, wdr_ref[...])
        dxn, n1 = _rms_bwd_rows(x1_ref[...], dhq * gb_ref[...] + dhk * gkv_ref[...])
        dgb_ref[...] += _col_sum(dhq * n1)
        dgkv_ref[...] += _col_sum(dhk * n1)
        dx1 = dx2_ref[...] + dxn
        dx1_ref[...] = dx1
        dy_ref[...] = lax.dot_general(dx1.astype(BF16), wo_ref[...], _NT, preferred_element_type=F32)

    row = lambda w_: pl.BlockSpec((bt, w_), lambda i: (i, 0))
    vec = lambda w_: pl.BlockSpec((1, w_), lambda i: (0, 0))
    in_specs = [row(d), row(d), row(Q_RANK), row(KV_RANK), row(N_HEADS * HEAD_PAD), row(nh), row(nh), row(LANES), row(nh)]
    return pl.pallas_call(
        body, grid=(t // bt,), in_specs=in_specs + [_const_spec(a) for a in consts],
        out_specs=[row(d), row(Q_RANK + nh), row(KV_RANK + LANES), vec(d), vec(d), vec(KV_RANK), vec(Q_RANK), row(D_RNN)],
        out_shape=[jax.ShapeDtypeStruct((t, d), F32), jax.ShapeDtypeStruct((t, Q_RANK + nh), BF16),
                   jax.ShapeDtypeStruct((t, KV_RANK + LANES), BF16), jax.ShapeDtypeStruct((1, d), F32),
                   jax.ShapeDtypeStruct((1, d), F32), jax.ShapeDtypeStruct((1, KV_RANK), F32),
                   jax.ShapeDtypeStruct((1, Q_RANK), F32), jax.ShapeDtypeStruct((t, D_RNN), F32)],
        compiler_params=_params(("arbitrary",)), name=name)(x1, dx2, cqp, ck, dq, dkn, dv, dkr, dg2, *consts)


def _softplus(z):
    return jnp.maximum(z, 0.0) + jnp.log1p(jnp.exp(-jnp.abs(z)))


def _one_minus_square(a, la):
    return jnp.tanh(-la) * (1.0 + a * a)


def _gates(xb, wrg, wig, brg, big, sp):
    xbb = xb.astype(BF16)
    r = _sigmoid_tail(jnp.dot(xbb, wrg, preferred_element_type=F32) + brg)
    i = _sigmoid(jnp.dot(xbb, wig, preferred_element_type=F32) + big)
    la = (-LRU_C) * r * sp
    a = jnp.exp(la)
    em = _one_minus_square(a, la)
    inv_mult = lax.rsqrt(em)
    mult = jnp.where(em > 0.0, em * inv_mult, 0.0)
    return r, i, a, mult, inv_mult


def _conv(xpad_ref, cw_ref, seq):
    acc = cw_ref[0:1, :] * xpad_ref[pl.ds(8 - (CONV_WIDTH - 1), seq), :]
    for k in range(1, CONV_WIDTH):
        acc = acc + cw_ref[k:k + 1, :] * xpad_ref[pl.ds(8 - (CONV_WIDTH - 1) + k, seq), :]
    return acc


def _seq_spec(seq):
    return pl.BlockSpec((None, seq, RNN_BW), lambda n, b: (b, 0, n))


def _chan_spec(rows):
    return pl.BlockSpec((rows, RNN_BW), lambda n, b: (0, n))


_GATE_W_SPEC = pl.BlockSpec((None, RNN_BW, RNN_BW), lambda n, b: (n, 0, 0))


SCAN_UNROLL = 4


def _peers():
    x, y, c = _mesh_pos()
    others = []
    for k in range(1, N_DEV):
        px = 1 - x if k & 4 else x
        py = 1 - y if k & 2 else y
        pc = 1 - c if k & 1 else c
        others.append(((px, py, pc), 4 * px + 2 * py + pc))
    return 4 * x + 2 * y + c, others


def _exchange(src_ref, dst_ref, send_sems, recv_sems, local_sem, *, finish, gather=False):
    me, others = _peers()

    def send(k, dev, slot):
        return pltpu.make_async_remote_copy(
            src_ref=src_ref if gather else src_ref.at[slot], dst_ref=dst_ref.at[me], send_sem=send_sems.at[k],
            recv_sem=recv_sems.at[k], device_id=dev, device_id_type=pl.DeviceIdType.MESH)

    local = pltpu.make_async_copy(src_ref if gather else src_ref.at[me], dst_ref.at[me], local_sem)
    if not finish:
        local.start()
        for k, (dev, slot) in enumerate(others):
            send(k, dev, slot).start()
        return
    for k, (dev, slot) in enumerate(others):
        pltpu.make_async_remote_copy(
            src_ref=dst_ref.at[slot], dst_ref=dst_ref.at[slot], send_sem=send_sems.at[k], recv_sem=recv_sems.at[k],
            device_id=dev, device_id_type=pl.DeviceIdType.MESH).wait_recv()
    for k, (dev, slot) in enumerate(others):
        send(k, dev, slot).wait_send()
    local.wait()


def _gather_two_level(x_ref, out_ref, send_sems, recv_sems, local_sem, *, phase):
    x, y, c = _mesh_pos()
    me, sibling = (x, y, c), (x, y, 1 - c)
    chips = [(1 - x, y), (x, 1 - y), (1 - x, 1 - y)]

    def slot(px, py, pc):
        return out_ref.at[4 * px + 2 * py + pc]

    def copy(k, blk, to, src=None):
        return pltpu.make_async_remote_copy(
            src_ref=slot(*blk) if src is None else src, dst_ref=slot(*blk),
            send_sem=send_sems.at[k], recv_sem=recv_sems.at[k], device_id=to, device_id_type=pl.DeviceIdType.MESH)

    if phase == 0:
        pltpu.make_async_copy(x_ref, slot(*me), local_sem).start()
        copy(0, me, sibling, src=x_ref).start()
        for j, chip in enumerate(chips):
            copy(1 + j, me, (*chip, c), src=x_ref).start()
    elif phase == 1:
        for j, chip in enumerate(chips):
            copy(1 + j, (*chip, c), me).wait_recv()
            copy(4 + j, (*chip, c), sibling).start()
    else:
        copy(0, sibling, me).wait_recv()
        for j, chip in enumerate(chips):
            copy(4 + j, (*chip, 1 - c), me).wait_recv()
        copy(0, me, sibling, src=x_ref).wait_send()
        for j, chip in enumerate(chips):
            copy(1 + j, me, (*chip, c), src=x_ref).wait_send()
            copy(4 + j, (*chip, c), sibling).wait_send()
        pltpu.make_async_copy(x_ref, slot(*me), local_sem).wait()


GATHER_FORWARD_STEP = 9
_EXCHANGE_SEMS = [pltpu.SemaphoreType.DMA((N_DEV - 1,)), pltpu.SemaphoreType.DMA((N_DEV - 1,)), pltpu.SemaphoreType.DMA(())]


def _first_last(steps):
    first = last = None
    for axis, n in enumerate(steps):
        i = pl.program_id(axis)
        first = (i == 0) if first is None else first & (i == 0)
        last = (i == n - 1) if last is None else last & (i == n - 1)
    return first, last


def _lru_fwd(xp, ga, cw, vecs, wrg, wig, block, *, name):
    bsz, seq, _ = xp.shape
    groups = seq // 8

    def body(xp_ref, ga_ref, cw_ref, vec_ref, wrg_ref, wig_ref, blk_ref, xb_ref, hs_ref, y_ref, all_ref,
             xpad, a_s, b_s, send_sems, recv_sems, local_sem):
        first, last = _first_last((RNN_BLOCKS, bsz))

        @pl.when(first)
        def _():
            _gather_two_level(blk_ref, all_ref, send_sems, recv_sems, local_sem, phase=0)

        @pl.when((pl.program_id(0) == GATHER_FORWARD_STEP) & (pl.program_id(1) == 0))
        def _():
            _gather_two_level(blk_ref, all_ref, send_sems, recv_sems, local_sem, phase=1)

        xpad[0:8, :] = jnp.zeros((8, RNN_BW), F32)
        xpad[pl.ds(8, seq), :] = xp_ref[...]
        xb = _conv(xpad, cw_ref, seq) + vec_ref[0:1, :]
        xb_ref[...] = xb
        sp = _softplus(-vec_ref[3:4, :])
        _, i, a, mult, _ = _gates(xb, wrg_ref[...], wig_ref[...], vec_ref[1:2, :], vec_ref[2:3, :], sp)
        a_s[...] = a
        b_s[...] = mult * (i * xb)
        row = lax.broadcasted_iota(jnp.int32, (8, RNN_BW), 0)

        def group(g, h):
            r0 = pl.multiple_of(g * 8, 8)
            av = a_s[pl.ds(r0, 8), :]
            bv = b_s[pl.ds(r0, 8), :]
            for k in (1, 2, 4):
                m = row >= k
                bv = jnp.where(m, av * pltpu.roll(bv, k, axis=0) + bv, bv)
                av = jnp.where(m, av * pltpu.roll(av, k, axis=0), av)
            hs_ref[pl.ds(r0, 8), :] = av * h + bv
            return av[7:8, :] * h + bv[7:8, :]

        def groups_of(i, h):
            for u in range(SCAN_UNROLL):
                h = group(i * SCAN_UNROLL + u, h)
            return h

        lax.fori_loop(0, groups // SCAN_UNROLL, groups_of, jnp.zeros((1, RNN_BW), F32))
        gav = ga_ref[...]
        y_ref[...] = (hs_ref[...] * (gav * _sigmoid(gav))).astype(BF16)

        @pl.when(last)
        def _():
            _gather_two_level(blk_ref, all_ref, send_sems, recv_sems, local_sem, phase=2)

    sq = _seq_spec(seq)
    shape = (bsz, seq, D_RNN)
    return pl.pallas_call(
        body, grid=(RNN_BLOCKS, bsz),
        in_specs=[sq, sq, _chan_spec(8), _chan_spec(8), _GATE_W_SPEC, _GATE_W_SPEC, _ANY],
        out_specs=[sq, sq, sq, _ANY],
        out_shape=[jax.ShapeDtypeStruct(shape, F32), jax.ShapeDtypeStruct(shape, F32), jax.ShapeDtypeStruct(shape, BF16),
                   jax.ShapeDtypeStruct((N_DEV,) + block.shape, block.dtype)],
        scratch_shapes=[pltpu.VMEM((seq + 8, RNN_BW), F32), pltpu.VMEM((seq, RNN_BW), F32), pltpu.VMEM((seq, RNN_BW), F32)]
        + _EXCHANGE_SEMS,
        compiler_params=_params(("arbitrary", "arbitrary")), name=name)(xp, ga, cw, vecs, wrg, wig, block)


def _lru_bwd(dy, xp, xb, hs, ga, cw, vecs, wrg, wig, parts, *, name):
    bsz, seq, _ = xp.shape
    groups = seq // 8

    def body(dy_ref, xp_ref, xb_ref, hs_ref, ga_ref, cw_ref, vec_ref, wrg_ref, wig_ref,
             parts_ref, dxp_ref, dga_ref, dwrg_ref, dwig_ref, dvec_ref, land_ref, pad, a_s, d_s, lam_s,
             send_sems, recv_sems, local_sem):
        first, last = _first_last((RNN_BLOCKS, bsz))

        @pl.when(first)
        def _():
            _exchange(parts_ref, land_ref, send_sems, recv_sems, local_sem, finish=False)

        @pl.when(pl.program_id(1) == 0)
        def _():
            dwrg_ref[...] = jnp.zeros_like(dwrg_ref)
            dwig_ref[...] = jnp.zeros_like(dwig_ref)
            dvec_ref[...] = jnp.zeros_like(dvec_ref)

        xb = xb_ref[...]
        hs = hs_ref[...]
        gav = ga_ref[...]
        dy = dy_ref[...]
        sp = _softplus(-vec_ref[3:4, :])
        wrg = wrg_ref[...]
        wig = wig_ref[...]
        r, i, a, mult, inv_mult = _gates(xb, wrg, wig, vec_ref[1:2, :], vec_ref[2:3, :], sp)
        sg = _sigmoid(gav)
        dga_ref[...] = (dy * hs * (sg * (1.0 + gav * (1.0 - sg)))).astype(BF16)
        d_s[...] = dy * (gav * sg)

        pad[pl.ds(0, seq), :] = a
        pad[pl.ds(seq, 8), :] = jnp.zeros((8, RNN_BW), F32)
        a_s[...] = pad[pl.ds(1, seq), :]
        row = lax.broadcasted_iota(jnp.int32, (8, RNN_BW), 0)

        def group(g, nxt):
            r0 = pl.multiple_of((groups - 1 - g) * 8, 8)
            cv = a_s[pl.ds(r0, 8), :]
            bv = d_s[pl.ds(r0, 8), :]
            for k in (1, 2, 4):
                m = row < 8 - k
                bv = jnp.where(m, cv * pltpu.roll(bv, 8 - k, axis=0) + bv, bv)
                cv = jnp.where(m, cv * pltpu.roll(cv, 8 - k, axis=0), cv)
            lam_s[pl.ds(r0, 8), :] = cv * nxt + bv
            return cv[0:1, :] * nxt + bv[0:1, :]

        def groups_of(i, nxt):
            for u in range(SCAN_UNROLL):
                nxt = group(i * SCAN_UNROLL + u, nxt)
            return nxt

        lax.fori_loop(0, groups // SCAN_UNROLL, groups_of, jnp.zeros((1, RNN_BW), F32))
        dh = lam_s[...]

        pad[0:8, :] = jnp.zeros((8, RNN_BW), F32)
        pad[pl.ds(8, seq), :] = hs
        da = dh * pad[pl.ds(7, seq), :]
        ixb = i * xb
        dixb = dh * mult
        dla = da * a - (dh * ixb) * (a * a) * inv_mult
        drp = (dla * ((-LRU_C) * sp)) * r * (1.0 - r)
        dip = (dixb * xb) * i * (1.0 - i)
        dvec_ref[0:1, :] += jnp.sum(drp, axis=0, keepdims=True)
        dvec_ref[1:2, :] += jnp.sum(dip, axis=0, keepdims=True)
        dvec_ref[2:3, :] += jnp.sum(dla * ((-LRU_C) * r), axis=0, keepdims=True)
        drpb = drp.astype(BF16)
        dipb = dip.astype(BF16)
        xbb = xb.astype(BF16)
        nt = (((1,), (1,)), ((), ()))
        tn = (((0,), (0,)), ((), ()))
        dxb = (dixb * i
               + lax.dot_general(drpb, wrg, nt, preferred_element_type=F32)
               + lax.dot_general(dipb, wig, nt, preferred_element_type=F32))
        dwrg_ref[...] += lax.dot_general(xbb, drpb, tn, preferred_element_type=F32)
        dwig_ref[...] += lax.dot_general(xbb, dipb, tn, preferred_element_type=F32)
        dvec_ref[3:4, :] += jnp.sum(dxb, axis=0, keepdims=True)

        pad[pl.ds(0, seq), :] = dxb
        pad[pl.ds(seq, 8), :] = jnp.zeros((8, RNN_BW), F32)
        dxp = cw_ref[0:1, :] * pad[pl.ds(CONV_WIDTH - 1, seq), :]
        for k in range(1, CONV_WIDTH):
            dxp = dxp + cw_ref[k:k + 1, :] * pad[pl.ds(CONV_WIDTH - 1 - k, seq), :]
        dxp_ref[...] = dxp.astype(BF16)
        pad[0:8, :] = jnp.zeros((8, RNN_BW), F32)
        pad[pl.ds(8, seq), :] = xp_ref[...]
        for k in range(CONV_WIDTH):
            dvec_ref[4 + k:5 + k, :] += jnp.sum(dxb * pad[pl.ds(8 - (CONV_WIDTH - 1) + k, seq), :], axis=0, keepdims=True)

        @pl.when(last)
        def _():
            _exchange(parts_ref, land_ref, send_sems, recv_sems, local_sem, finish=True)

    sq = _seq_spec(seq)
    shape = (bsz, seq, D_RNN)
    gshape = (RNN_BLOCKS, RNN_BW, RNN_BW)
    return pl.pallas_call(
        body, grid=(RNN_BLOCKS, bsz),
        in_specs=[sq, sq, sq, sq, sq, _chan_spec(8), _chan_spec(8), _GATE_W_SPEC, _GATE_W_SPEC, _ANY],
        out_specs=[sq, sq, _GATE_W_SPEC, _GATE_W_SPEC, _chan_spec(8), _ANY],
        out_shape=[jax.ShapeDtypeStruct(shape, BF16), jax.ShapeDtypeStruct(shape, BF16),
                   jax.ShapeDtypeStruct(gshape, F32), jax.ShapeDtypeStruct(gshape, F32),
                   jax.ShapeDtypeStruct((8, D_RNN), F32), jax.ShapeDtypeStruct(parts.shape, parts.dtype)],
        scratch_shapes=[pltpu.VMEM((seq + 8, RNN_BW), F32), pltpu.VMEM((seq, RNN_BW), F32),
                        pltpu.VMEM((seq, RNN_BW), F32), pltpu.VMEM((seq, RNN_BW), F32)] + _EXCHANGE_SEMS,
        compiler_params=_params(("arbitrary", "arbitrary")), name=name)(dy, xp, xb, hs, ga, cw, vecs, wrg, wig, parts)


def _attn_block(seq):
    return min(512, seq)


def _diag_mask(blk):
    return lax.broadcasted_iota(jnp.int32, (blk, blk), 0) <= lax.broadcasted_iota(jnp.int32, (blk, blk), 1)


FWD_HEADS = 8
BWD_HEADS = 2


def _attn_fwd(q, kn, kr, v_t, block, *, bsz, seq, name):
    t = bsz * seq
    blk = _attn_block(seq)
    nq = seq // blk
    hg = FWD_HEADS
    steps = (bsz, N_HEADS // hg, nq)

    def body(q_ref, kn_ref, kr_ref, vt_ref, blk_ref, o_ref, lse_ref, all_ref, acc, send_sems, recv_sems, local_sem):
        first, last = _first_last(steps)

        @pl.when(first)
        def _():
            _exchange(blk_ref, all_ref, send_sems, recv_sems, local_sem, finish=False, gather=True)

        qi = pl.program_id(2)
        acc[...] = jnp.zeros_like(acc)

        def step(j, carry, diagonal):
            k0 = pl.multiple_of(j * blk, blk)
            kr_j = kr_ref[pl.ds(k0, blk), :]
            out = []
            for h in range(hg):
                m_i, l_i = carry[h]
                kv = jnp.concatenate([kn_ref[pl.ds(k0, blk), h * QK_NOPE:(h + 1) * QK_NOPE], kr_j], axis=1)
                qv = q_ref[:, h * HEAD_PAD:(h + 1) * HEAD_PAD]
                s = lax.dot_general(kv, qv, _NT, preferred_element_type=F32) * ATTN_SCALE
                if diagonal:
                    s = jnp.where(_diag_mask(blk), s, -jnp.inf)
                m_new = jnp.maximum(m_i, jnp.max(s, axis=0, keepdims=True))
                p = jnp.exp(s - m_new)
                alpha = jnp.exp(m_i - m_new)
                l_new = alpha * l_i + jnp.sum(p, axis=0, keepdims=True)
                acc[h] = alpha * acc[h] + jnp.dot(vt_ref[h * V_DIM:(h + 1) * V_DIM, pl.ds(k0, blk)], p.astype(BF16),
                                                  preferred_element_type=F32)
                out.append((m_new, l_new))
            return tuple(out)

        init = tuple((jnp.full((1, blk), -jnp.inf, F32), jnp.zeros((1, blk), F32)) for _ in range(hg))
        carry = lax.fori_loop(0, qi, lambda j, c: step(j, c, False), init)
        stats = step(qi, carry, True)
        for h in range(hg):
            m_i, l_i = stats[h]
            o_ref[:, h * V_DIM:(h + 1) * V_DIM] = (acc[h] / l_i).T
            lse_ref[h] = m_i + jnp.log(l_i)

        @pl.when(last)
        def _():
            _exchange(blk_ref, all_ref, send_sems, recv_sems, local_sem, finish=True, gather=True)

    return pl.pallas_call(
        body, grid=steps,
        in_specs=[pl.BlockSpec((blk, hg * HEAD_PAD), lambda b, g, i: (b * nq + i, g)),
                  pl.BlockSpec((seq, hg * QK_NOPE), lambda b, g, i: (b, g)),
                  pl.BlockSpec((seq, LANES), lambda b, g, i: (b, 0)),
                  pl.BlockSpec((hg * V_DIM, seq), lambda b, g, i: (g, b)), _ANY],
        out_specs=[pl.BlockSpec((blk, hg * V_DIM), lambda b, g, i: (b * nq + i, g)),
                   pl.BlockSpec((hg, 1, blk), lambda b, g, i: (g, 0, b * nq + i)), _ANY],
        out_shape=[jax.ShapeDtypeStruct((t, N_HEADS * V_DIM), F32), jax.ShapeDtypeStruct((N_HEADS, 1, t), F32),
                   jax.ShapeDtypeStruct((N_DEV,) + block.shape, block.dtype)],
        scratch_shapes=[pltpu.VMEM((hg, V_DIM, blk), F32)] + _EXCHANGE_SEMS,
        compiler_params=_params(("arbitrary", "arbitrary", "arbitrary")), name=name)(q, kn, kr, v_t, block)


def _attn_bwd(q, kn, kr, kn_t, kr_t, v, o, lse, do, cos, sin, parts, *, bsz, seq, name):
    t = bsz * seq
    blk = _attn_block(seq)
    nq = seq // blk
    hg = BWD_HEADS
    steps = (bsz, N_HEADS // hg)

    def body(q_ref, kn_ref, kr_ref, knt_ref, krt_ref, v_ref, o_ref, lse_ref, do_ref, cos_ref, sin_ref, parts_ref,
             dq_ref, dkn_ref, dkr_ref, dv_ref, land_ref, dqt_acc, dk_acc, dv_acc, send_sems, recv_sems, local_sem):
        first, last = _first_last(steps)

        @pl.when(first)
        def _():
            _exchange(parts_ref, land_ref, send_sems, recv_sems, local_sem, finish=False)

        dqt_acc[...] = jnp.zeros_like(dqt_acc)
        dk_acc[...] = jnp.zeros_like(dk_acc)
        dv_acc[...] = jnp.zeros_like(dv_acc)

        def q_block(i, _):
            q0 = pl.multiple_of(i * blk, blk)
            rows = []
            for h in range(hg):
                dov = do_ref[pl.ds(q0, blk), h * V_DIM:(h + 1) * V_DIM].astype(F32)
                dcol = jnp.sum(dov * o_ref[pl.ds(q0, blk), h * V_DIM:(h + 1) * V_DIM], axis=-1, keepdims=True)
                delta = jnp.broadcast_to(dcol, (blk, LANES)).T[0:1, :]
                rows.append((lse_ref[h, :, pl.ds(q0, blk)], delta))

            def pair(j, diagonal):
                k0 = pl.multiple_of(j * blk, blk)
                kr_j = kr_ref[pl.ds(k0, blk), :]
                krt_j = krt_ref[:, pl.ds(k0, blk)]
                for h in range(hg):
                    lse_i, delta = rows[h]
                    qv = q_ref[pl.ds(q0, blk), h * HEAD_PAD:(h + 1) * HEAD_PAD]
                    dov = do_ref[pl.ds(q0, blk), h * V_DIM:(h + 1) * V_DIM]
                    kv = jnp.concatenate([kn_ref[pl.ds(k0, blk), h * QK_NOPE:(h + 1) * QK_NOPE], kr_j], axis=1)
                    s = lax.dot_general(kv, qv, _NT, preferred_element_type=F32) * ATTN_SCALE
                    p = jnp.exp(s - lse_i)
                    if diagonal:
                        p = jnp.where(_diag_mask(blk), p, 0.0)
                    dv_acc[pl.ds(k0, blk), h * V_DIM:(h + 1) * V_DIM] += jnp.dot(
                        p.astype(BF16), dov, preferred_element_type=F32)
                    dp = lax.dot_general(v_ref[pl.ds(k0, blk), h * V_DIM:(h + 1) * V_DIM], dov, _NT,
                                         preferred_element_type=F32)
                    ds = (p * (dp - delta) * ATTN_SCALE).astype(BF16)
                    dk_acc[pl.ds(k0, blk), h * HEAD_PAD:(h + 1) * HEAD_PAD] += jnp.dot(ds, qv, preferred_element_type=F32)
                    base = h * HEAD_PAD
                    dqt_acc[base:base + QK_NOPE, pl.ds(q0, blk)] += jnp.dot(
                        knt_ref[h * QK_NOPE:(h + 1) * QK_NOPE, pl.ds(k0, blk)], ds, preferred_element_type=F32)
                    dqt_acc[base + QK_NOPE:base + HEAD_PAD, pl.ds(q0, blk)] += jnp.dot(
                        krt_j, ds, preferred_element_type=F32)

            def off_diagonal(j, _):
                pair(j, False)
                return 0

            lax.fori_loop(0, i, off_diagonal, 0)
            pair(i, True)
            return 0

        lax.fori_loop(0, nq, q_block, 0)
        dkr = jnp.zeros((seq, LANES), F32)
        for h in range(hg):
            base = h * HEAD_PAD
            for i in range(nq):
                rows = slice(i * blk, (i + 1) * blk)
                dq = dqt_acc[base:base + HEAD_PAD, rows].T
                dq_ref[rows, base:base + QK_NOPE] = dq[:, :QK_NOPE].astype(BF16)
                dq_ref[rows, base + QK_NOPE:base + HEAD_PAD] = _rope_t(
                    dq[:, QK_NOPE:], cos_ref[rows, :], sin_ref[rows, :]).astype(BF16)
            dkn_ref[:, h * QK_NOPE:(h + 1) * QK_NOPE] = dk_acc[:, base:base + QK_NOPE].astype(BF16)
            dkr = dkr + dk_acc[:, base + QK_NOPE:base + HEAD_PAD]
        dv_ref[...] = dv_acc[...].astype(BF16)

        @pl.when(pl.program_id(1) == 0)
        def _():
            dkr_ref[...] = jnp.zeros_like(dkr_ref)

        dkr_ref[...] += _rope_t(dkr, cos_ref[...], sin_ref[...])

        @pl.when(last)
        def _():
            _exchange(parts_ref, land_ref, send_sems, recv_sems, local_sem, finish=True)

    head = pl.BlockSpec((seq, hg * V_DIM), lambda b, g: (b, g))
    head_t = pl.BlockSpec((hg * V_DIM, seq), lambda b, g: (g, b))
    shared = pl.BlockSpec((seq, LANES), lambda b, g: (b, 0))
    shared_t = pl.BlockSpec((LANES, seq), lambda b, g: (0, b))
    table = pl.BlockSpec((seq, LANES), lambda b, g: (0, 0))
    qspec = pl.BlockSpec((seq, hg * HEAD_PAD), lambda b, g: (b, g))
    return pl.pallas_call(
        body, grid=steps,
        in_specs=[qspec, head, shared, head_t, shared_t, head, head,
                  pl.BlockSpec((hg, 1, seq), lambda b, g: (g, 0, b)), head, table, table, _ANY],
        out_specs=[qspec, head, shared, head, _ANY],
        out_shape=[jax.ShapeDtypeStruct((t, N_HEADS * HEAD_PAD), BF16), jax.ShapeDtypeStruct((t, N_HEADS * QK_NOPE), BF16),
                   jax.ShapeDtypeStruct((t, LANES), F32), jax.ShapeDtypeStruct((t, N_HEADS * V_DIM), BF16),
                   jax.ShapeDtypeStruct(parts.shape, parts.dtype)],
        scratch_shapes=[pltpu.VMEM((hg * HEAD_PAD, seq), F32), pltpu.VMEM((seq, hg * HEAD_PAD), F32),
                        pltpu.VMEM((seq, hg * V_DIM), F32)] + _EXCHANGE_SEMS,
        compiler_params=_params(("arbitrary", "arbitrary")), name=name)(
            q, kn, kr, kn_t, kr_t, v, o, lse, do, cos, sin, parts)


def _head_and_loss(o, g2, x1, target, w_out, g_final, *, name, bt=512):
    t, d = x1.shape
    bt = min(bt, t)
    nt = (((1,), (1,)), ((), ()))

    def body(o_ref, g2_ref, x1_ref, tgt_ref, w_ref, gf_ref, loss_ref, dx2_ref, y2_ref, do_ref, dg2_ref, dgf_ref):
        @pl.when(pl.program_id(0) == 0)
        def _():
            loss_ref[...] = jnp.zeros_like(loss_ref)
            dgf_ref[...] = jnp.zeros_like(dgf_ref)

        ov = o_ref[...]
        gv = g2_ref[...]
        sg = _sigmoid(gv)
        silu = gv * sg
        y2 = (ov * silu).astype(BF16)
        y2_ref[...] = y2
        w = w_ref[...]
        x2 = x1_ref[...] + jnp.dot(y2, w, preferred_element_type=F32)
        r = lax.rsqrt(jnp.mean(x2 * x2, axis=-1, keepdims=True) + EPS)
        nrm = x2 * r
        gf = gf_ref[...]
        err = nrm * gf - tgt_ref[...]
        loss_ref[...] += 0.5 * jnp.sum(jnp.mean(err * err, axis=-1, keepdims=True))
        dyf = err * (1.0 / d)
        dgf_ref[...] += jnp.sum(dyf * nrm, axis=0, keepdims=True)
        dn = dyf * gf
        dx2 = r * (dn - nrm * jnp.mean(dn * nrm, axis=-1, keepdims=True))
        dx2_ref[...] = dx2
        dy2 = lax.dot_general(dx2.astype(BF16), w, nt, preferred_element_type=F32)
        do_ref[...] = (dy2 * silu).astype(BF16)
        dg2_ref[...] = (dy2 * ov * (sg * (1.0 + gv * (1.0 - sg)))).astype(BF16)

    row = pl.BlockSpec((bt, d), lambda i: (i, 0))
    vec = pl.BlockSpec((1, d), lambda i: (0, 0))
    return pl.pallas_call(
        body, grid=(t // bt,),
        in_specs=[row, row, row, row, pl.BlockSpec((d, d), lambda i: (0, 0)), vec],
        out_specs=[pl.BlockSpec((8, LANES), lambda i: (0, 0)), row, row, row, row, vec],
        out_shape=[jax.ShapeDtypeStruct((8, LANES), F32), jax.ShapeDtypeStruct((t, d), F32),
                   jax.ShapeDtypeStruct((t, d), BF16), jax.ShapeDtypeStruct((t, d), BF16),
                   jax.ShapeDtypeStruct((t, d), BF16), jax.ShapeDtypeStruct((1, d), F32)],
        compiler_params=_params(("arbitrary",)), name=name)(o, g2, x1, target, w_out, g_final)


def _sum_parts(parts, *, name, br=GRAD_BLOCK):
    npart, rows, w = parts.shape

    def body(p_ref, o_ref):
        acc = p_ref[0].astype(F32)
        for j in range(1, npart):
            acc = acc + p_ref[j].astype(F32)
        o_ref[...] = acc

    return pl.pallas_call(
        body, grid=(rows // br,), in_specs=[pl.BlockSpec((npart, br, w), lambda i: (0, i, 0))],
        out_specs=pl.BlockSpec((br, w), lambda i: (i, 0)), out_shape=jax.ShapeDtypeStruct((rows, w), F32),
        compiler_params=_params(("parallel",)), name=name)(parts)


def _chip_partial(parts, recv, *, name, br=GRAD_BLOCK):
    _, rows, w = parts.shape
    core = lax.axis_index("c").astype(jnp.int32).reshape(1)

    def body(c_ref, p_ref, r_ref, o_ref):
        o_ref[...] = (p_ref[...] + r_ref[...]).astype(BF16)

    grid_spec = pltpu.PrefetchScalarGridSpec(
        num_scalar_prefetch=1, grid=(4, rows // br),
        in_specs=[pl.BlockSpec((None, br, w), lambda k, i, c_ref: (2 * k + c_ref[0], i, 0)),
                  pl.BlockSpec((None, br, w), lambda k, i, c_ref: (k, i, 0))],
        out_specs=pl.BlockSpec((None, br, w), lambda k, i, c_ref: (k, i, 0)))
    return pl.pallas_call(
        body, grid_spec=grid_spec, out_shape=jax.ShapeDtypeStruct((4, rows, w), BF16),
        compiler_params=_params(("parallel", "parallel")), name=name)(core, parts, recv)


def _as_block(a):
    if a.ndim == 1:
        return a.reshape(1, -1)
    if a.ndim > 2 and a.shape[0] == 1:
        return a.reshape(a.shape[1:])
    return a


def _adamw(g, w, m, v, *, name):
    shape = w.shape
    g, w, m, v = (_as_block(a) for a in (g, w, m, v))

    def body(g_ref, w_ref, m_ref, v_ref, d_ref, nm_ref, nv_ref):
        gv = g_ref[...]
        nm = ADAM_B1 * m_ref[...] + (1.0 - ADAM_B1) * gv
        nv = ADAM_B2 * v_ref[...] + (1.0 - ADAM_B2) * (gv * gv)
        nm_ref[...] = nm
        nv_ref[...] = nv
        m_hat = nm / (1.0 - ADAM_B1 ** ADAM_STEP)
        v_hat = nv / (1.0 - ADAM_B2 ** ADAM_STEP)
        d_ref[...] = (-ADAM_LR) * (m_hat / (jnp.sqrt(v_hat) + ADAM_EPS) + ADAM_WD * w_ref[...])

    whole = pl.BlockSpec(memory_space=pltpu.VMEM)
    outs = pl.pallas_call(
        body, in_specs=[whole] * 4, out_specs=[whole] * 3, out_shape=[jax.ShapeDtypeStruct(w.shape, F32)] * 3,
        compiler_params=_params(), name=name)(g, w, m, v)
    return [o.reshape(shape) for o in outs]


def _all_gather(block, *, name):
    m, n = block.shape

    def body(x_ref, out_ref, send_sems, recv_sems, local_sem):
        for phase in range(3):
            _gather_two_level(x_ref, out_ref, send_sems, recv_sems, local_sem, phase=phase)

    return pl.pallas_call(
        body, out_shape=jax.ShapeDtypeStruct((N_DEV, m, n), block.dtype), in_specs=[_ANY], out_specs=_ANY,
        scratch_shapes=_EXCHANGE_SEMS, name=name)(block)


def _exchange_d2d(parts, *, name):
    _, rows, w = parts.shape

    def body(p_ref, land_ref, send_sems, recv_sems):
        x, y, c = _mesh_pos()
        sends = []
        for k in range(4):
            cp = pltpu.make_async_remote_copy(
                src_ref=p_ref.at[2 * k + (1 - c)], dst_ref=land_ref.at[k], send_sem=send_sems.at[k],
                recv_sem=recv_sems.at[k], device_id=(x, y, 1 - c), device_id_type=pl.DeviceIdType.MESH)
            cp.start()
            sends.append(cp)
        for cp in sends:
            cp.wait_recv()
        for cp in sends:
            cp.wait_send()

    return pl.pallas_call(
        body, out_shape=jax.ShapeDtypeStruct((4, rows, w), parts.dtype), in_specs=[_ANY], out_specs=_ANY,
        scratch_shapes=[pltpu.SemaphoreType.DMA((4,)), pltpu.SemaphoreType.DMA((4,))], name=name)(parts)


def _exchange_ici(parts, *, name):
    def body(p_ref, land_ref, send_sems, recv_sems, local_sem):
        x, y, c = _mesh_pos()
        mine = pltpu.make_async_copy(p_ref.at[2 * x + y], land_ref.at[3], local_sem)
        mine.start()
        sends = []
        for k, (px, py) in enumerate([(1 - x, y), (x, 1 - y), (1 - x, 1 - y)]):
            cp = pltpu.make_async_remote_copy(
                src_ref=p_ref.at[2 * px + py], dst_ref=land_ref.at[k], send_sem=send_sems.at[k],
                recv_sem=recv_sems.at[k], device_id=(px, py, c), device_id_type=pl.DeviceIdType.MESH)
            cp.start()
            sends.append(cp)
        for cp in sends:
            cp.wait_recv()
        for cp in sends:
            cp.wait_send()
        mine.wait()

    return pl.pallas_call(
        body, out_shape=jax.ShapeDtypeStruct(parts.shape, parts.dtype), in_specs=[_ANY], out_specs=_ANY,
        scratch_shapes=[pltpu.SemaphoreType.DMA((3,)), pltpu.SemaphoreType.DMA((3,)), pltpu.SemaphoreType.DMA(())],
        name=name)(parts)


def _rows(a):
    return a.reshape(-1, PACK_W)


def _pad_to(a, n):
    return jnp.pad(a, (0, n - a.shape[0]))


def _weight_blocks(d):
    small = _rows(_pad_to(jnp.concatenate([d[n].reshape(-1) for n, _ in _SMALL]), 16 * PACK_W))
    bits = lax.bitcast_convert_type(small, jnp.uint32)
    halves = [lax.bitcast_convert_type(h.astype(jnp.uint16), WIRE) for h in (bits >> 16, bits & 0xFFFF)]
    block_a = jnp.concatenate([d["w_in_a"][0].T.astype(WIRE)] + halves, axis=0)
    w_uq = jnp.pad(d["w_uq"][0], ((0, 0), (0, 0), (0, HEAD_PAD - QK_NOPE - QK_ROPE)))
    pieces = {"w_out_a": d["w_out_a"], "w_dkv": d["w_dkv"], "w_uk": d["w_uk"], "w_uv": d["w_uv"],
              "w_in_b": d["w_in_b"][0].T, "w_uq": w_uq}
    block_b = jnp.concatenate([_rows(pieces[n]) for n, _ in _PIECES_B]
                              + [jnp.zeros((WIRE_ROWS_B - MATRIX_ROWS_B, PACK_W), F32)], axis=0).astype(WIRE)
    return block_a, block_b, d["w_out_b"][0].astype(WIRE)


def _weights_a(wall):
    w = {}
    lo, hi = _OFF_A["w_in_a"]
    w["w_in_a_t"] = wall[:, lo:hi].reshape(2 * D_RNN, D_MODEL)
    high, low = (lax.bitcast_convert_type(wall[:, r:r + 16], jnp.uint16).astype(jnp.uint32)
                 for r in (MATRIX_ROWS_A, MATRIX_ROWS_A + 16))
    small = lax.bitcast_convert_type((high << 16) | low, F32)[:, :8].reshape(N_DEV, 8 * PACK_W)
    off = dict(zip([n for n, _ in _SMALL], [0, 128, 768, 928, 1088, 1248]))
    w["norm_a"] = small[:, :128].reshape(1, D_MODEL)

    def by_channel(lo, rows):
        a = small[:, lo:lo + rows * (D_RNN // N_DEV)].reshape(N_DEV, rows, -1).transpose(1, 0, 2).reshape(rows, D_RNN)
        return jnp.pad(a, ((0, 8 - rows), (0, 0)))

    w["conv_taps"] = by_channel(off["conv_w"], CONV_WIDTH)
    w["lru_vecs"] = by_channel(off["conv_b"], 4)
    return w


def _weights_b(wall):
    piece = {n: wall[:, lo:hi] for n, (lo, hi) in _OFF_B.items()}
    w = {"w_out_a": piece["w_out_a"].reshape(D_RNN, D_MODEL)}
    w_dkv = piece["w_dkv"].reshape(D_MODEL, KV_RANK + QK_ROPE)
    w["w_dkv_c"] = w_dkv[:, :KV_RANK]
    w["w_dkv_r"] = jnp.pad(w_dkv[:, KV_RANK:], ((0, 0), (0, LANES - QK_ROPE)))
    w["w_uk"] = piece["w_uk"].reshape(KV_RANK, N_HEADS * QK_NOPE)
    w["w_uv"] = piece["w_uv"].reshape(KV_RANK, N_HEADS * V_DIM)
    w["w_in_b_t"] = piece["w_in_b"].reshape(Q_RANK + N_HEADS * V_DIM, D_MODEL)
    w["w_uq"] = piece["w_uq"].reshape(Q_RANK, N_HEADS * HEAD_PAD)
    return w


def _pack_rep(d):
    flat = jnp.concatenate([d[n].reshape(-1) for n, _ in _REP])
    return _rows(_pad_to(flat, REP_ROWS * PACK_W))


def _unpack_rep(p, like):
    flat = p.reshape(-1)
    out, off = {}, 0
    for n, k in _REP:
        out[n] = flat[off:off + k].reshape(like[n].shape)
        off += k
    return out


def _by_owner(a):
    return a.reshape(N_DEV, -1, PACK_W)


def _grad_parts_b(g):
    tail = jnp.zeros((N_DEV, WIRE_ROWS_B - MATRIX_ROWS_B, PACK_W), F32)
    return jnp.concatenate([_by_owner(g[n]) for n, _ in _PIECES_B] + [tail], axis=1).astype(BF16)


def _grad_parts_a(g):
    small = jnp.concatenate([
        g["norm_a"].reshape(N_DEV, -1),
        g["conv_w"].reshape(CONV_WIDTH, N_DEV, -1).transpose(1, 0, 2).reshape(N_DEV, -1),
        g["conv_b"].reshape(N_DEV, -1), g["b_rg"].reshape(N_DEV, -1), g["b_ig"].reshape(N_DEV, -1),
        g["lru_lambda"].reshape(N_DEV, -1)], axis=1)
    small = jnp.pad(small, ((0, 0), (0, 8 * PACK_W - small.shape[1]))).reshape(N_DEV, 8, PACK_W)
    half = N_DEV // 2
    w_in_a = jnp.concatenate([h.reshape(half, -1, PACK_W) for h in g["w_in_a_t"]], axis=0)
    rep = _pack_rep(g).reshape(N_DEV, REP_SLICE, PACK_W)
    tail = jnp.zeros((N_DEV, GRAD_ROWS_A - MATRIX_ROWS_A - 8 - REP_SLICE, PACK_W), F32)
    return jnp.concatenate([w_in_a, small, rep, tail], axis=1)


def _own_grads(sum_a, sum_b, sum_c):
    out = {}
    lo, hi = _OFF_A["w_in_a"]
    out["w_in_a"] = sum_a[lo:hi].T.reshape(1, D_MODEL, 2 * D_RNN // N_DEV)
    small = sum_a[MATRIX_ROWS_A:MATRIX_ROWS_A + 8].reshape(-1)
    shapes = {"norm_a": (1, D_MODEL // N_DEV), "conv_w": (1, CONV_WIDTH, D_RNN // N_DEV), "conv_b": (1, D_RNN // N_DEV),
              "b_rg": (1, D_RNN // N_DEV), "b_ig": (1, D_RNN // N_DEV), "lru_lambda": (1, D_RNN // N_DEV)}
    off = 0
    for n, k in _SMALL:
        out[n] = small[off:off + k].reshape(shapes[n])
        off += k
    piece = {n: sum_b[lo:hi] for n, (lo, hi) in _OFF_B.items()}
    out["w_out_a"] = piece["w_out_a"].reshape(1, D_RNN // N_DEV, D_MODEL)
    out["w_dkv"] = piece["w_dkv"].reshape(D_MODEL // N_DEV, KV_RANK + QK_ROPE)
    out["w_uk"] = piece["w_uk"].reshape(KV_RANK // N_DEV, N_HEADS, QK_NOPE)
    out["w_uv"] = piece["w_uv"].reshape(KV_RANK // N_DEV, N_HEADS, V_DIM)
    out["w_in_b"] = piece["w_in_b"].T.reshape(1, D_MODEL, (Q_RANK + N_HEADS * V_DIM) // N_DEV)
    out["w_uq"] = piece["w_uq"].reshape(1, Q_RANK // N_DEV, N_HEADS, HEAD_PAD)[..., :QK_NOPE + QK_ROPE]
    out["w_out_b"] = sum_c.reshape(1, N_HEADS * V_DIM // N_DEV, D_MODEL)
    return out


def _step(x, target, w, rep, block_b, block_c, *, bsz, seq):
    t = bsz * seq
    cos, sin = _rope_tables(seq)
    g_a = w["norm_a"]
    g_kv = rep["norm_kv"].reshape(1, -1)
    g_kvn = rep["kv_norm"].reshape(1, -1)
    g_b = rep["norm_b"].reshape(1, -1)
    g_q = rep["q_norm"].reshape(1, -1)
    g_f = rep["final_norm"].reshape(1, -1)
    wrg = rep["w_rg"][0].astype(BF16)
    wig = rep["w_ig"][0].astype(BF16)
    cw8, vecs = w["conv_taps"], w["lru_vecs"]

    def seq3(a):
        return a.reshape(bsz, seq, a.shape[-1])

    def flat(a):
        return a.reshape(t, a.shape[-1])

    h0, xp, ga = _lru_proj_fwd(x, g_a, w["w_in_a_t"], name="lru_proj_fwd")
    xb, hs, y, wall_b = _lru_fwd(seq3(xp), seq3(ga), cw8, vecs, wrg, wig, block_b, name="lru_fwd")
    w = dict(w, **_weights_b(wall_b))
    x1 = _matmul(flat(y), w["w_out_a"], residual=x, name="out_a")
    hk, hq, ck, cqp, g2, ckv, cq, q, kn, v, kr, kn_t, v_t, kr_t = _mla_proj_fwd(
        x1, (g_kv, g_b, g_kvn, g_q), w, cos, sin, seq=seq, name="mla_proj_fwd")
    o, lse, wall_c = _attn_fwd(q, kn, kr, v_t, block_c, bsz=bsz, seq=seq, name="attn_fwd")
    w_out_b = wall_c.reshape(N_HEADS * V_DIM, D_MODEL)
    loss, dx2, y2, do, dg2, dgf = _head_and_loss(o, g2, x1, target, w_out_b, g_f, name="head_loss")
    grads = {"final_norm": dgf}
    parts_c = _by_owner(_matmul_tn(y2, dx2, name="d_w_out_b")).astype(BF16)
    dq, dkn, dkr, dv, landed_c = _attn_bwd(q, kn, kr, kn_t, kr_t, v, o, lse, do, cos, sin, parts_c,
                                           bsz=bsz, seq=seq, name="attn_bwd")
    grads["w_uq"] = _matmul_tn(cq, dq, name="d_w_uq")
    dx1, du2, dckr, dgkv, dgb, dgkvn, dgq, dy = _mla_proj_bwd(
        x1, dx2, cqp, ck, dq, dkn, dv, dkr, dg2, (g_kv, g_b, g_kvn, g_q), w, name="mla_proj_bwd")
    grads["norm_kv"], grads["norm_b"], grads["kv_norm"], grads["q_norm"] = dgkv, dgb, dgkvn, dgq
    grads["w_in_b"] = _matmul_tn(du2, hq, name="d_w_in_b_t")
    grads["w_uk"] = _matmul_tn(ckv, dkn, name="d_w_uk")
    grads["w_uv"] = _matmul_tn(ckv, dv, name="d_w_uv")
    grads["w_dkv"] = _matmul_tn(hk, dckr, name="d_w_dkv")[:, :KV_RANK + QK_ROPE]
    grads["w_out_a"] = _matmul_tn(flat(y), dx1, name="d_w_out_a")
    parts_b = _grad_parts_b(grads)
    dxp, dga, dwrg, dwig, dvec, landed_b = _lru_bwd(
        seq3(dy), seq3(xp), xb, hs, seq3(ga), cw8, vecs, wrg, wig, parts_b, name="lru_bwd")
    dxp, dga = flat(dxp), flat(dga)
    grads["w_rg"], grads["w_ig"] = dwrg, dwig
    grads["b_rg"], grads["b_ig"], grads["conv_b"] = dvec[0], dvec[1], dvec[3]
    lam = vecs[3]
    grads["lru_lambda"] = dvec[2] * (-1.0 / (1.0 + jnp.exp(lam)))
    grads["conv_w"] = dvec[4:4 + CONV_WIDTH]
    dx, dga_norm, dwx, dwg = _lru_proj_bwd(dxp, dga, x, dx1, h0, g_a, w["w_in_a_t"], name="lru_proj_bwd")
    grads["norm_a"] = dga_norm
    grads["w_in_a_t"] = (dwx, dwg)
    return loss[0, 0], dx, grads, landed_b, landed_c


def kernel(x, norm_a, w_in_a, conv_w, conv_b, w_rg, b_rg, w_ig, b_ig, lru_lambda, w_out_a, norm_kv, w_dkv, kv_norm, w_uk, w_uv, norm_b, w_in_b, q_norm, w_uq, w_out_b, final_norm, loss_target, m_norm_a, m_w_in_a, m_conv_w, m_conv_b, m_w_rg, m_b_rg, m_w_ig, m_b_ig, m_lru_lambda, m_w_out_a, m_norm_kv, m_w_dkv, m_kv_norm, m_w_uk, m_w_uv, m_norm_b, m_w_in_b, m_q_norm, m_w_uq, m_w_out_b, m_final_norm, v_norm_a, v_w_in_a, v_conv_w, v_conv_b, v_w_rg, v_b_rg, v_w_ig, v_b_ig, v_lru_lambda, v_w_out_a, v_norm_kv, v_w_dkv, v_kv_norm, v_w_uk, v_w_uv, v_norm_b, v_w_in_b, v_q_norm, v_w_uq, v_w_out_b, v_final_norm):
    given = dict(locals())
    wts = {n: given[n] for n in WEIGHTS}
    mom1 = {n: given["m_" + n] for n in WEIGHTS}
    mom2 = {n: given["v_" + n] for n in WEIGHTS}
    bsz, seq, _ = x.shape
    t = bsz * seq

    block_a, block_b, block_c = _weight_blocks(wts)
    w = _weights_a(_all_gather(block_a, name="gather_weights_a"))
    loss, dx, grads, landed_b, landed_c = _step(x.reshape(t, D_MODEL), loss_target.reshape(t, D_MODEL), w, wts,
                                                block_b, block_c, bsz=bsz, seq=seq)

    parts_a = _grad_parts_a(grads)
    from_sibling = _exchange_d2d(parts_a, name="exchange_grads_d2d")
    chip_parts = _chip_partial(parts_a, from_sibling, name="chip_partial_grads")
    landed_a = _exchange_ici(chip_parts, name="exchange_grads_ici")
    sum_a = _sum_parts(landed_a, name="sum_grads_a", br=GRAD_BLOCK)
    sum_b = _sum_parts(landed_b, name="sum_grads_b", br=WIRE_ROWS_B // 2)
    sum_c = _sum_parts(landed_c, name="sum_grads_c", br=landed_c.shape[1])
    g_own = _own_grads(sum_a, sum_b, sum_c)
    rep_slice = sum_a[MATRIX_ROWS_A + 8:MATRIX_ROWS_A + 8 + REP_SLICE]
    loss_rows = jnp.pad(loss.reshape(1, 1), ((0, 7), (0, PACK_W - 1)))
    gathered = _all_gather(jnp.concatenate([rep_slice, loss_rows], axis=0), name="gather_replicated")
    g_own.update(_unpack_rep(gathered[:, :REP_SLICE].reshape(REP_ROWS, PACK_W), wts))
    loss = jnp.sum(gathered[:, REP_SLICE, 0])

    deltas, new_m, new_v = {}, {}, {}
    for n in WEIGHTS:
        deltas[n], new_m[n], new_v[n] = _adamw(g_own[n], wts[n], mom1[n], mom2[n], name="adamw_" + n)
    result = [loss, dx.reshape(bsz, seq, D_MODEL)]
    for d in (g_own, deltas, new_m, new_v):
        result.extend(d[n] for n in WEIGHTS)
    return tuple(result)
```

```python
import jax
import jax.numpy as jnp
from jax import lax
from jax.experimental import pallas as pl
from jax.experimental.pallas import tpu as pltpu

F32 = jnp.float32
BF16 = jnp.bfloat16
WIRE = jnp.bfloat16

D_MODEL = 1024
D_RNN = 1280
RNN_BLOCKS = 10
RNN_BW = 128
CONV_WIDTH = 4
LRU_C = 8.0
N_HEADS = 8
QK_NOPE = 128
QK_ROPE = 64
V_DIM = 128
KV_RANK = 256
Q_RANK = 384
ROPE_THETA = 10000.0
EPS = 1e-6
ATTN_SCALE = (QK_NOPE + QK_ROPE) ** -0.5
HEAD_PAD = 256
LANES = 128

ADAM_LR = 0.001
ADAM_B1 = 0.9
ADAM_B2 = 0.999
ADAM_EPS = 1e-08
ADAM_WD = 0.01
ADAM_STEP = 10

N_DEV = 8
VMEM_LIMIT_BYTES = 56 * 2**20
PACK_W = 1024

_PIECES_A = (("w_in_a", 320),)
_PIECES_B = (("w_out_a", 160), ("w_dkv", 40), ("w_uk", 32), ("w_uv", 32), ("w_in_b", 176), ("w_uq", 96))


def _offsets(pieces):
    off, r = {}, 0
    for n, k in pieces:
        off[n] = (r, r + k)
        r += k
    return off, r


_OFF_A, MATRIX_ROWS_A = _offsets(_PIECES_A)
_OFF_B, MATRIX_ROWS_B = _offsets(_PIECES_B)
WIRE_ROWS_A = MATRIX_ROWS_A + 32
WIRE_ROWS_B = 544
_SMALL = (("norm_a", 128), ("conv_w", 640), ("conv_b", 160), ("b_rg", 160), ("b_ig", 160), ("lru_lambda", 160))
_REP = (("w_rg", 163840), ("w_ig", 163840), ("norm_kv", 1024), ("kv_norm", 256), ("norm_b", 1024),
        ("q_norm", 384), ("final_norm", 1024))
REP_ROWS = 384
REP_SLICE = REP_ROWS // N_DEV
GRAD_ROWS_A = 384
GRAD_BLOCK = 192

WEIGHTS = ("norm_a", "w_in_a", "conv_w", "conv_b", "w_rg", "b_rg", "w_ig", "b_ig", "lru_lambda", "w_out_a",
           "norm_kv", "w_dkv", "kv_norm", "w_uk", "w_uv", "norm_b", "w_in_b", "q_norm", "w_uq", "w_out_b",
           "final_norm")


def _params(sem=None):
    return pltpu.CompilerParams(dimension_semantics=sem, vmem_limit_bytes=VMEM_LIMIT_BYTES)


_NT = (((1,), (1,)), ((), ()))
_ANY = pl.BlockSpec(memory_space=pl.ANY)


def _mesh_pos():
    return lax.axis_index("x"), lax.axis_index("y"), lax.axis_index("c")


def _sigmoid(z):
    return 0.5 * jnp.tanh(0.5 * z) + 0.5


def _sigmoid_tail(z):
    return 1.0 / (1.0 + jnp.exp(-z))


def _col_block(n):
    return n if n <= 1408 else n // 2


def _matmul(a, b, *, name, nt=False, out_dtype=F32, residual=None, bm=1024):
    m, k = a.shape
    n = b.shape[0] if nt else b.shape[1]
    bm = min(bm, m)
    bn = _col_block(n)
    dims = (((1,), (1,)), ((), ())) if nt else (((1,), (0,)), ((), ()))
    has_res = residual is not None

    def body(*refs):
        a_ref, b_ref, o_ref = refs[0], refs[1], refs[-1]
        acc = lax.dot_general(a_ref[...].astype(BF16), b_ref[...].astype(BF16), dims, preferred_element_type=F32)
        if has_res:
            acc = acc + refs[2][...]
        o_ref[...] = acc.astype(out_dtype)

    in_specs = [pl.BlockSpec((bm, k), lambda i, j: (i, 0)),
                pl.BlockSpec((bn, k), lambda i, j: (j, 0)) if nt else pl.BlockSpec((k, bn), lambda i, j: (0, j))]
    args = [a, b]
    if has_res:
        in_specs.append(pl.BlockSpec((bm, bn), lambda i, j: (i, j)))
        args.append(residual)
    return pl.pallas_call(
        body, grid=(m // bm, n // bn), in_specs=in_specs, out_specs=pl.BlockSpec((bm, bn), lambda i, j: (i, j)),
        out_shape=jax.ShapeDtypeStruct((m, n), out_dtype), compiler_params=_params(("parallel", "parallel")),
        name=name)(*args)


def _matmul_tn(a, b, *, name, bt=1024):
    t, m = a.shape
    n = b.shape[1]
    bt = min(bt, t)
    bm, bn = _col_block(m), _col_block(n)

    def body(a_ref, b_ref, o_ref):
        @pl.when(pl.program_id(2) == 0)
        def _():
            o_ref[...] = jnp.zeros_like(o_ref)

        o_ref[...] += lax.dot_general(a_ref[...].astype(BF16), b_ref[...].astype(BF16),
                                      (((0,), (0,)), ((), ())), preferred_element_type=F32)

    return pl.pallas_call(
        body, grid=(m // bm, n // bn, t // bt),
        in_specs=[pl.BlockSpec((bt, bm), lambda i, j, s: (s, i)), pl.BlockSpec((bt, bn), lambda i, j, s: (s, j))],
        out_specs=pl.BlockSpec((bm, bn), lambda i, j, s: (i, j)),
        out_shape=jax.ShapeDtypeStruct((m, n), F32),
        compiler_params=_params(("parallel", "parallel", "arbitrary")), name=name)(a, b)


def _swap_halves(v):
    ax = v.ndim - 1
    lane = lax.broadcasted_iota(jnp.int32, v.shape, ax)
    up = pltpu.roll(v, LANES - QK_ROPE // 2, axis=ax)
    down = pltpu.roll(v, QK_ROPE // 2, axis=ax)
    return jnp.where(lane < QK_ROPE // 2, up, jnp.where(lane < QK_ROPE, down, 0.0))


def _rope(v, cos, sin):
    return v * cos + _swap_halves(v) * sin


def _rope_t(d, cos, sin):
    return d * cos + _swap_halves(d * sin)


def _rope_tables(seq):
    pos = jnp.arange(seq, dtype=F32)
    inv = ROPE_THETA ** (-jnp.arange(0, QK_ROPE, 2, dtype=F32) / QK_ROPE)
    ang = pos[:, None] * inv[None, :]
    cos, sin = jnp.cos(ang), jnp.sin(ang)
    zero = jnp.zeros((seq, LANES - QK_ROPE), F32)
    return jnp.concatenate([cos, cos, zero], axis=1), jnp.concatenate([-sin, sin, zero], axis=1)


def _rms(v):
    return v * lax.rsqrt(jnp.mean(v * v, axis=-1, keepdims=True) + EPS)


def _const_spec(a):
    return pl.BlockSpec(a.shape, lambda i: (0,) * a.ndim)


def _lru_proj_fwd(x, g_a, w_in_t, *, name, bt=512):
    t, d = x.shape
    bt = min(bt, t)
    n = w_in_t.shape[0] // 2

    def body(x_ref, g_ref, wt_ref, h_ref, xp_ref, ga_ref):
        h = (_rms(x_ref[...]) * g_ref[...]).astype(BF16)
        h_ref[...] = h
        xp_ref[...] = lax.dot_general(h, wt_ref[0:n, :], _NT, preferred_element_type=F32)
        ga_ref[...] = lax.dot_general(h, wt_ref[n:2 * n, :], _NT, preferred_element_type=F32)

    row = lambda w: pl.BlockSpec((bt, w), lambda i: (i, 0))
    return pl.pallas_call(
        body, grid=(t // bt,), in_specs=[row(d), _const_spec(g_a), _const_spec(w_in_t)],
        out_specs=[row(d), row(n), row(n)],
        out_shape=[jax.ShapeDtypeStruct((t, d), BF16), jax.ShapeDtypeStruct((t, n), F32), jax.ShapeDtypeStruct((t, n), F32)],
        compiler_params=_params(("parallel",)), name=name)(x, g_a, w_in_t)


def _mla_proj_fwd(x1, gains, w, cos, sin, *, seq, name, bt=512):
    t, d = x1.shape
    bt = min(bt, seq)
    per_seq = seq // bt
    g_kv, g_b, g_kvn, g_q = gains
    consts = [g_kv, g_b, g_kvn, g_q, w["w_dkv_c"], w["w_dkv_r"], w["w_in_b_t"], w["w_uk"], w["w_uv"], w["w_uq"]]

    def body(x_ref, cos_ref, sin_ref, gkv_ref, gb_ref, gkvn_ref, gq_ref, wdc_ref, wdr_ref, wbt_ref,
             wuk_ref, wuv_ref, wuq_ref,
             hk_ref, hq_ref, ck_ref, cqp_ref, g2_ref, ckv_ref, cq_ref, q_ref, kn_ref, v_ref, kr_ref, knt_ref, vt_ref, krt_ref):
        nrm = _rms(x_ref[...])
        hk = (nrm * gkv_ref[...]).astype(BF16)
        hq = (nrm * gb_ref[...]).astype(BF16)
        hk_ref[...] = hk
        hq_ref[...] = hq
        ck = jnp.dot(hk, wdc_ref[...], preferred_element_type=F32)
        ck_ref[...] = ck
        cqp = lax.dot_general(hq, wbt_ref[0:Q_RANK, :], _NT, preferred_element_type=F32)
        cqp_ref[...] = cqp
        g2_ref[...] = lax.dot_general(hq, wbt_ref[Q_RANK:, :], _NT, preferred_element_type=F32)
        cosv, sinv = cos_ref[...], sin_ref[...]
        kr = _rope(jnp.dot(hk, wdr_ref[...], preferred_element_type=F32), cosv, sinv)
        kr_ref[...] = kr.astype(BF16)
        krt_ref[...] = kr.T.astype(BF16)
        ckv = (_rms(ck) * gkvn_ref[...]).astype(BF16)
        ckv_ref[...] = ckv
        kn = jnp.dot(ckv, wuk_ref[...], preferred_element_type=F32)
        v = jnp.dot(ckv, wuv_ref[...], preferred_element_type=F32)
        kn_ref[...] = kn.astype(BF16)
        v_ref[...] = v.astype(BF16)
        knt_ref[...] = kn.T.astype(BF16)
        vt_ref[...] = v.T.astype(BF16)
        cq = (_rms(cqp) * gq_ref[...]).astype(BF16)
        cq_ref[...] = cq
        for h in range(N_HEADS):
            qh = jnp.dot(cq, wuq_ref[:, h * HEAD_PAD:(h + 1) * HEAD_PAD], preferred_element_type=F32)
            q_ref[:, h * HEAD_PAD:h * HEAD_PAD + QK_NOPE] = qh[:, :QK_NOPE].astype(BF16)
            q_ref[:, h * HEAD_PAD + QK_NOPE:(h + 1) * HEAD_PAD] = _rope(qh[:, QK_NOPE:], cosv, sinv).astype(BF16)

    row = lambda w_: pl.BlockSpec((bt, w_), lambda i: (i, 0))
    col = lambda h_: pl.BlockSpec((h_, bt), lambda i: (0, i))
    tab = pl.BlockSpec((bt, LANES), lambda i: (i % per_seq, 0))
    nh = N_HEADS * V_DIM
    shapes = [((t, d), BF16), ((t, d), BF16), ((t, KV_RANK), F32), ((t, Q_RANK), F32), ((t, nh), F32), ((t, KV_RANK), BF16),
              ((t, Q_RANK), BF16), ((t, N_HEADS * HEAD_PAD), BF16), ((t, nh), BF16), ((t, nh), BF16), ((t, LANES), BF16),
              ((nh, t), BF16), ((nh, t), BF16), ((LANES, t), BF16)]
    out_specs = [row(d), row(d), row(KV_RANK), row(Q_RANK), row(nh), row(KV_RANK), row(Q_RANK), row(N_HEADS * HEAD_PAD),
                 row(nh), row(nh), row(LANES), col(nh), col(nh), col(LANES)]
    return pl.pallas_call(
        body, grid=(t // bt,), in_specs=[row(d), tab, tab] + [_const_spec(a) for a in consts], out_specs=out_specs,
        out_shape=[jax.ShapeDtypeStruct(s, dt) for s, dt in shapes],
        compiler_params=_params(("parallel",)), name=name)(x1, cos, sin, *consts)


def _rms_bwd_rows(xv, dn):
    r = lax.rsqrt(jnp.mean(xv * xv, axis=-1, keepdims=True) + EPS)
    nrm = xv * r
    return r * (dn - nrm * jnp.mean(dn * nrm, axis=-1, keepdims=True)), nrm


def _col_sum(v):
    return jnp.sum(v, axis=0, keepdims=True)


def _lru_proj_bwd(dxp, dga, x, dx1, h0, g_a, w_in_t, *, name, bt=512):
    t, d = x.shape
    bt = min(bt, t)
    n = w_in_t.shape[0] // 2
    tn = (((0,), (0,)), ((), ()))

    def body(dxp_ref, dga_ref, x_ref, dx1_ref, h0_ref, g_ref, wt_ref, dx_ref, dg_ref, dwx_ref, dwg_ref):
        @pl.when(pl.program_id(0) == 0)
        def _():
            for ref in (dg_ref, dwx_ref, dwg_ref):
                ref[...] = jnp.zeros_like(ref)

        dxp_v, dga_v, h0 = dxp_ref[...], dga_ref[...], h0_ref[...]
        dwx_ref[...] += lax.dot_general(dxp_v, h0, tn, preferred_element_type=F32)
        dwg_ref[...] += lax.dot_general(dga_v, h0, tn, preferred_element_type=F32)
        dh = (jnp.dot(dxp_v, wt_ref[0:n, :], preferred_element_type=F32)
              + jnp.dot(dga_v, wt_ref[n:2 * n, :], preferred_element_type=F32))
        dxn, nrm = _rms_bwd_rows(x_ref[...], dh * g_ref[...])
        dg_ref[...] += _col_sum(dh * nrm)
        dx_ref[...] = dx1_ref[...] + dxn

    row = lambda w: pl.BlockSpec((bt, w), lambda i: (i, 0))
    whole = pl.BlockSpec((n, d), lambda i: (0, 0))
    return pl.pallas_call(
        body, grid=(t // bt,),
        in_specs=[row(n), row(n), row(d), row(d), row(d), _const_spec(g_a), _const_spec(w_in_t)],
        out_specs=[row(d), _const_spec(g_a), whole, whole],
        out_shape=[jax.ShapeDtypeStruct((t, d), F32), jax.ShapeDtypeStruct((1, d), F32),
                   jax.ShapeDtypeStruct((n, d), F32), jax.ShapeDtypeStruct((n, d), F32)],
        compiler_params=_params(("arbitrary",)), name=name)(dxp, dga, x, dx1, h0, g_a, w_in_t)


def _mla_proj_bwd(x1, dx2, cqp, ck, dq, dkn, dv, dkr, dg2, gains, w, *, name, bt=512):
    t, d = x1.shape
    bt = min(bt, t)
    g_kv, g_b, g_kvn, g_q = gains
    consts = [g_kv, g_b, g_kvn, g_q, w["w_dkv_c"], w["w_dkv_r"], w["w_in_b_t"], w["w_uk"], w["w_uv"], w["w_uq"],
              w["w_out_a"]]
    nh = N_HEADS * V_DIM

    def body(x1_ref, dx2_ref, cqp_ref, ck_ref, dq_ref, dkn_ref, dv_ref, dkr_ref, dg2_ref,
             gkv_ref, gb_ref, gkvn_ref, gq_ref, wdc_ref, wdr_ref, wbt_ref, wuk_ref, wuv_ref, wuq_ref, wo_ref,
             dx1_ref, du2_ref, dckr_ref, dgkv_ref, dgb_ref, dgkvn_ref, dgq_ref, dy_ref):
        @pl.when(pl.program_id(0) == 0)
        def _():
            for ref in (dgkv_ref, dgb_ref, dgkvn_ref, dgq_ref):
                ref[...] = jnp.zeros_like(ref)

        dot_nt = lambda a, b: lax.dot_general(a, b, _NT, preferred_element_type=F32)
        dcq = dot_nt(dq_ref[...], wuq_ref[...])
        dcqp, nq = _rms_bwd_rows(cqp_ref[...], dcq * gq_ref[...])
        dgq_ref[...] += _col_sum(dcq * nq)
        dcqp = dcqp.astype(BF16)
        dg2 = dg2_ref[...]
        du2_ref[:, :Q_RANK] = dcqp
        du2_ref[:, Q_RANK:] = dg2
        dhq = (jnp.dot(dcqp, wbt_ref[0:Q_RANK, :], preferred_element_type=F32)
               + jnp.dot(dg2, wbt_ref[Q_RANK:, :], preferred_element_type=F32))
        dckv = dot_nt(dkn_ref[...], wuk_ref[...]) + dot_nt(dv_ref[...], wuv_ref[...])
        dck, nc = _rms_bwd_rows(ck_ref[...], dckv * gkvn_ref[...])
        dgkvn_ref[...] += _col_sum(dckv * nc)
        dck = dck.astype(BF16)
        dkr = dkr_ref[...].astype(BF16)
        dckr_ref[:, :KV_RANK] = dck
        dckr_ref[:, KV_RANK:] = dkr
        dhk = dot_nt(dck, wdc_ref[...]) + dot_nt(dkr, wdr_ref[...])
        dxn, n1 = _rms_bwd_rows(x1_ref[...], dhq * gb_ref[...] + dhk * gkv_ref[...])
        dgb_ref[...] += _col_sum(dhq * n1)
        dgkv_ref[...] += _col_sum(dhk * n1)
        dx1 = dx2_ref[...] + dxn
        dx1_ref[...] = dx1
        dy_ref[...] = lax.dot_general(dx1.astype(BF16), wo_ref[...], _NT, preferred_element_type=F32)

    row = lambda w_: pl.BlockSpec((bt, w_), lambda i: (i, 0))
    vec = lambda w_: pl.BlockSpec((1, w_), lambda i: (0, 0))
    in_specs = [row(d), row(d), row(Q_RANK), row(KV_RANK), row(N_HEADS * HEAD_PAD), row(nh), row(nh), row(LANES), row(nh)]
    return pl.pallas_call(
        body, grid=(t // bt,), in_specs=in_specs + [_const_spec(a) for a in consts],
        out_specs=[row(d), row(Q_RANK + nh), row(KV_RANK + LANES), vec(d), vec(d), vec(KV_RANK), vec(Q_RANK), row(D_RNN)],
        out_shape=[jax.ShapeDtypeStruct((t, d), F32), jax.ShapeDtypeStruct((t, Q_RANK + nh), BF16),
                   jax.ShapeDtypeStruct((t, KV_RANK + LANES), BF16), jax.ShapeDtypeStruct((1, d), F32),
                   jax.ShapeDtypeStruct((1, d), F32), jax.ShapeDtypeStruct((1, KV_RANK), F32),
                   jax.ShapeDtypeStruct((1, Q_RANK), F32), jax.ShapeDtypeStruct((t, D_RNN), F32)],
        compiler_params=_params(("arbitrary",)), name=name)(x1, dx2, cqp, ck, dq, dkn, dv, dkr, dg2, *consts)


def _softplus(z):
    return jnp.maximum(z, 0.0) + jnp.log1p(jnp.exp(-jnp.abs(z)))


def _one_minus_square(a, la):
    return jnp.tanh(-la) * (1.0 + a * a)


def _gates(xb, wrg, wig, brg, big, sp):
    xbb = xb.astype(BF16)
    r = _sigmoid_tail(jnp.dot(xbb, wrg, preferred_element_type=F32) + brg)
    i = _sigmoid(jnp.dot(xbb, wig, preferred_element_type=F32) + big)
    la = (-LRU_C) * r * sp
    a = jnp.exp(la)
    em = _one_minus_square(a, la)
    inv_mult = lax.rsqrt(em)
    mult = jnp.where(em > 0.0, em * inv_mult, 0.0)
    return r, i, a, mult, inv_mult


def _conv(xpad_ref, cw_ref, seq):
    acc = cw_ref[0:1, :] * xpad_ref[pl.ds(8 - (CONV_WIDTH - 1), seq), :]
    for k in range(1, CONV_WIDTH):
        acc = acc + cw_ref[k:k + 1, :] * xpad_ref[pl.ds(8 - (CONV_WIDTH - 1) + k, seq), :]
    return acc


def _seq_spec(seq):
    return pl.BlockSpec((None, seq, RNN_BW), lambda n, b: (b, 0, n))


def _chan_spec(rows):
    return pl.BlockSpec((rows, RNN_BW), lambda n, b: (0, n))


_GATE_W_SPEC = pl.BlockSpec((None, RNN_BW, RNN_BW), lambda n, b: (n, 0, 0))


SCAN_UNROLL = 4


def _peers():
    x, y, c = _mesh_pos()
    others = []
    for k in range(1, N_DEV):
        px = 1 - x if k & 4 else x
        py = 1 - y if k & 2 else y
        pc = 1 - c if k & 1 else c
        others.append(((px, py, pc), 4 * px + 2 * py + pc))
    return 4 * x + 2 * y + c, others


def _exchange(src_ref, dst_ref, send_sems, recv_sems, local_sem, *, finish, gather=False):
    me, others = _peers()

    def send(k, dev, slot):
        return pltpu.make_async_remote_copy(
            src_ref=src_ref if gather else src_ref.at[slot], dst_ref=dst_ref.at[me], send_sem=send_sems.at[k],
            recv_sem=recv_sems.at[k], device_id=dev, device_id_type=pl.DeviceIdType.MESH)

    local = pltpu.make_async_copy(src_ref if gather else src_ref.at[me], dst_ref.at[me], local_sem)
    if not finish:
        local.start()
        for k, (dev, slot) in enumerate(others):
            send(k, dev, slot).start()
        return
    for k, (dev, slot) in enumerate(others):
        pltpu.make_async_remote_copy(
            src_ref=dst_ref.at[slot], dst_ref=dst_ref.at[slot], send_sem=send_sems.at[k], recv_sem=recv_sems.at[k],
            device_id=dev, device_id_type=pl.DeviceIdType.MESH).wait_recv()
    for k, (dev, slot) in enumerate(others):
        send(k, dev, slot).wait_send()
    local.wait()


def _gather_two_level(x_ref, out_ref, send_sems, recv_sems, local_sem, *, phase):
    x, y, c = _mesh_pos()
    me, sibling = (x, y, c), (x, y, 1 - c)
    chips = [(1 - x, y), (x, 1 - y), (1 - x, 1 - y)]

    def slot(px, py, pc):
        return out_ref.at[4 * px + 2 * py + pc]

    def copy(k, blk, to, src=None):
        return pltpu.make_async_remote_copy(
            src_ref=slot(*blk) if src is None else src, dst_ref=slot(*blk),
            send_sem=send_sems.at[k], recv_sem=recv_sems.at[k], device_id=to, device_id_type=pl.DeviceIdType.MESH)

    if phase == 0:
        pltpu.make_async_copy(x_ref, slot(*me), local_sem).start()
        copy(0, me, sibling, src=x_ref).start()
        for j, chip in enumerate(chips):
            copy(1 + j, me, (*chip, c), src=x_ref).start()
    elif phase == 1:
        for j, chip in enumerate(chips):
            copy(1 + j, (*chip, c), me).wait_recv()
            copy(4 + j, (*chip, c), sibling).start()
    else:
        copy(0, sibling, me).wait_recv()
        for j, chip in enumerate(chips):
            copy(4 + j, (*chip, 1 - c), me).wait_recv()
        copy(0, me, sibling, src=x_ref).wait_send()
        for j, chip in enumerate(chips):
            copy(1 + j, me, (*chip, c), src=x_ref).wait_send()
            copy(4 + j, (*chip, c), sibling).wait_send()
        pltpu.make_async_copy(x_ref, slot(*me), local_sem).wait()


GATHER_FORWARD_STEP = 9
_EXCHANGE_SEMS = [pltpu.SemaphoreType.DMA((N_DEV - 1,)), pltpu.SemaphoreType.DMA((N_DEV - 1,)), pltpu.SemaphoreType.DMA(())]


def _first_last(steps):
    first = last = None
    for axis, n in enumerate(steps):
        i = pl.program_id(axis)
        first = (i == 0) if first is None else first & (i == 0)
        last = (i == n - 1) if last is None else last & (i == n - 1)
    return first, last


def _lru_fwd(xp, ga, cw, vecs, wrg, wig, block, *, name):
    bsz, seq, _ = xp.shape
    groups = seq // 8

    def body(xp_ref, ga_ref, cw_ref, vec_ref, wrg_ref, wig_ref, blk_ref, xb_ref, hs_ref, y_ref, all_ref,
             xpad, a_s, b_s, send_sems, recv_sems, local_sem):
        first, last = _first_last((RNN_BLOCKS, bsz))

        @pl.when(first)
        def _():
            _gather_two_level(blk_ref, all_ref, send_sems, recv_sems, local_sem, phase=0)

        @pl.when((pl.program_id(0) == GATHER_FORWARD_STEP) & (pl.program_id(1) == 0))
        def _():
            _gather_two_level(blk_ref, all_ref, send_sems, recv_sems, local_sem, phase=1)

        xpad[0:8, :] = jnp.zeros((8, RNN_BW), F32)
        xpad[pl.ds(8, seq), :] = xp_ref[...]
        xb = _conv(xpad, cw_ref, seq) + vec_ref[0:1, :]
        xb_ref[...] = xb
        sp = _softplus(-vec_ref[3:4, :])
        _, i, a, mult, _ = _gates(xb, wrg_ref[...], wig_ref[...], vec_ref[1:2, :], vec_ref[2:3, :], sp)
        a_s[...] = a
        b_s[...] = mult * (i * xb)
        row = lax.broadcasted_iota(jnp.int32, (8, RNN_BW), 0)

        def group(g, h):
            r0 = pl.multiple_of(g * 8, 8)
            av = a_s[pl.ds(r0, 8), :]
            bv = b_s[pl.ds(r0, 8), :]
            for k in (1, 2, 4):
                m = row >= k
                bv = jnp.where(m, av * pltpu.roll(bv, k, axis=0) + bv, bv)
                av = jnp.where(m, av * pltpu.roll(av, k, axis=0), av)
            hs_ref[pl.ds(r0, 8), :] = av * h + bv
            return av[7:8, :] * h + bv[7:8, :]

        def groups_of(i, h):
            for u in range(SCAN_UNROLL):
                h = group(i * SCAN_UNROLL + u, h)
            return h

        lax.fori_loop(0, groups // SCAN_UNROLL, groups_of, jnp.zeros((1, RNN_BW), F32))
        gav = ga_ref[...]
        y_ref[...] = (hs_ref[...] * (gav * _sigmoid(gav))).astype(BF16)

        @pl.when(last)
        def _():
            _gather_two_level(blk_ref, all_ref, send_sems, recv_sems, local_sem, phase=2)

    sq = _seq_spec(seq)
    shape = (bsz, seq, D_RNN)
    return pl.pallas_call(
        body, grid=(RNN_BLOCKS, bsz),
        in_specs=[sq, sq, _chan_spec(8), _chan_spec(8), _GATE_W_SPEC, _GATE_W_SPEC, _ANY],
        out_specs=[sq, sq, sq, _ANY],
        out_shape=[jax.ShapeDtypeStruct(shape, F32), jax.ShapeDtypeStruct(shape, F32), jax.ShapeDtypeStruct(shape, BF16),
                   jax.ShapeDtypeStruct((N_DEV,) + block.shape, block.dtype)],
        scratch_shapes=[pltpu.VMEM((seq + 8, RNN_BW), F32), pltpu.VMEM((seq, RNN_BW), F32), pltpu.VMEM((seq, RNN_BW), F32)]
        + _EXCHANGE_SEMS,
        compiler_params=_params(("arbitrary", "arbitrary")), name=name)(xp, ga, cw, vecs, wrg, wig, block)


def _lru_bwd(dy, xp, xb, hs, ga, cw, vecs, wrg, wig, parts, *, name):
    bsz, seq, _ = xp.shape
    groups = seq // 8

    def body(dy_ref, xp_ref, xb_ref, hs_ref, ga_ref, cw_ref, vec_ref, wrg_ref, wig_ref,
             parts_ref, dxp_ref, dga_ref, dwrg_ref, dwig_ref, dvec_ref, land_ref, pad, a_s, d_s, lam_s,
             send_sems, recv_sems, local_sem):
        first, last = _first_last((RNN_BLOCKS, bsz))

        @pl.when(first)
        def _():
            _exchange(parts_ref, land_ref, send_sems, recv_sems, local_sem, finish=False)

        @pl.when(pl.program_id(1) == 0)
        def _():
            dwrg_ref[...] = jnp.zeros_like(dwrg_ref)
            dwig_ref[...] = jnp.zeros_like(dwig_ref)
            dvec_ref[...] = jnp.zeros_like(dvec_ref)

        xb = xb_ref[...]
        hs = hs_ref[...]
        gav = ga_ref[...]
        dy = dy_ref[...]
        sp = _softplus(-vec_ref[3:4, :])
        wrg = wrg_ref[...]
        wig = wig_ref[...]
        r, i, a, mult, inv_mult = _gates(xb, wrg, wig, vec_ref[1:2, :], vec_ref[2:3, :], sp)
        sg = _sigmoid(gav)
        dga_ref[...] = (dy * hs * (sg * (1.0 + gav * (1.0 - sg)))).astype(BF16)
        d_s[...] = dy * (gav * sg)

        pad[pl.ds(0, seq), :] = a
        pad[pl.ds(seq, 8), :] = jnp.zeros((8, RNN_BW), F32)
        a_s[...] = pad[pl.ds(1, seq), :]
        row = lax.broadcasted_iota(jnp.int32, (8, RNN_BW), 0)

        def group(g, nxt):
            r0 = pl.multiple_of((groups - 1 - g) * 8, 8)
            cv = a_s[pl.ds(r0, 8), :]
            bv = d_s[pl.ds(r0, 8), :]
            for k in (1, 2, 4):
                m = row < 8 - k
                bv = jnp.where(m, cv * pltpu.roll(bv, 8 - k, axis=0) + bv, bv)
                cv = jnp.where(m, cv * pltpu.roll(cv, 8 - k, axis=0), cv)
            lam_s[pl.ds(r0, 8), :] = cv * nxt + bv
            return cv[0:1, :] * nxt + bv[0:1, :]

        def groups_of(i, nxt):
            for u in range(SCAN_UNROLL):
                nxt = group(i * SCAN_UNROLL + u, nxt)
            return nxt

        lax.fori_loop(0, groups // SCAN_UNROLL, groups_of, jnp.zeros((1, RNN_BW), F32))
        dh = lam_s[...]

        pad[0:8, :] = jnp.zeros((8, RNN_BW), F32)
        pad[pl.ds(8, seq), :] = hs
        da = dh * pad[pl.ds(7, seq), :]
        ixb = i * xb
        dixb = dh * mult
        dla = da * a - (dh * ixb) * (a * a) * inv_mult
        drp = (dla * ((-LRU_C) * sp)) * r * (1.0 - r)
        dip = (dixb * xb) * i * (1.0 - i)
        dvec_ref[0:1, :] += jnp.sum(drp, axis=0, keepdims=True)
        dvec_ref[1:2, :] += jnp.sum(dip, axis=0, keepdims=True)
        dvec_ref[2:3, :] += jnp.sum(dla * ((-LRU_C) * r), axis=0, keepdims=True)
        drpb = drp.astype(BF16)
        dipb = dip.astype(BF16)
        xbb = xb.astype(BF16)
        nt = (((1,), (1,)), ((), ()))
        tn = (((0,), (0,)), ((), ()))
        dxb = (dixb * i
               + lax.dot_general(drpb, wrg, nt, preferred_element_type=F32)
               + lax.dot_general(dipb, wig, nt, preferred_element_type=F32))
        dwrg_ref[...] += lax.dot_general(xbb, drpb, tn, preferred_element_type=F32)
        dwig_ref[...] += lax.dot_general(xbb, dipb, tn, preferred_element_type=F32)
        dvec_ref[3:4, :] += jnp.sum(dxb, axis=0, keepdims=True)

        pad[pl.ds(0, seq), :] = dxb
        pad[pl.ds(seq, 8), :] = jnp.zeros((8, RNN_BW), F32)
        dxp = cw_ref[0:1, :] * pad[pl.ds(CONV_WIDTH - 1, seq), :]
        for k in range(1, CONV_WIDTH):
            dxp = dxp + cw_ref[k:k + 1, :] * pad[pl.ds(CONV_WIDTH - 1 - k, seq), :]
        dxp_ref[...] = dxp.astype(BF16)
        pad[0:8, :] = jnp.zeros((8, RNN_BW), F32)
        pad[pl.ds(8, seq), :] = xp_ref[...]
        for k in range(CONV_WIDTH):
            dvec_ref[4 + k:5 + k, :] += jnp.sum(dxb * pad[pl.ds(8 - (CONV_WIDTH - 1) + k, seq), :], axis=0, keepdims=True)

        @pl.when(last)
        def _():
            _exchange(parts_ref, land_ref, send_sems, recv_sems, local_sem, finish=True)

    sq = _seq_spec(seq)
    shape = (bsz, seq, D_RNN)
    gshape = (RNN_BLOCKS, RNN_BW, RNN_BW)
    return pl.pallas_call(
        body, grid=(RNN_BLOCKS, bsz),
        in_specs=[sq, sq, sq, sq, sq, _chan_spec(8), _chan_spec(8), _GATE_W_SPEC, _GATE_W_SPEC, _ANY],
        out_specs=[sq, sq, _GATE_W_SPEC, _GATE_W_SPEC, _chan_spec(8), _ANY],
        out_shape=[jax.ShapeDtypeStruct(shape, BF16), jax.ShapeDtypeStruct(shape, BF16),
                   jax.ShapeDtypeStruct(gshape, F32), jax.ShapeDtypeStruct(gshape, F32),
                   jax.ShapeDtypeStruct((8, D_RNN), F32), jax.ShapeDtypeStruct(parts.shape, parts.dtype)],
        scratch_shapes=[pltpu.VMEM((seq + 8, RNN_BW), F32), pltpu.VMEM((seq, RNN_BW), F32),
                        pltpu.VMEM((seq, RNN_BW), F32), pltpu.VMEM((seq, RNN_BW), F32)] + _EXCHANGE_SEMS,
        compiler_params=_params(("arbitrary", "arbitrary")), name=name)(dy, xp, xb, hs, ga, cw, vecs, wrg, wig, parts)


def _attn_block(seq):
    return min(512, seq)


def _diag_mask(blk):
    return lax.broadcasted_iota(jnp.int32, (blk, blk), 0) <= lax.broadcasted_iota(jnp.int32, (blk, blk), 1)


FWD_HEADS = 8
BWD_HEADS = 2


def _attn_fwd(q, kn, kr, v_t, block, *, bsz, seq, name):
    t = bsz * seq
    blk = _attn_block(seq)
    nq = seq // blk
    hg = FWD_HEADS
    steps = (bsz, N_HEADS // hg, nq)

    def body(q_ref, kn_ref, kr_ref, vt_ref, blk_ref, o_ref, lse_ref, all_ref, acc, send_sems, recv_sems, local_sem):
        first, last = _first_last(steps)

        @pl.when(first)
        def _():
            _exchange(blk_ref, all_ref, send_sems, recv_sems, local_sem, finish=False, gather=True)

        qi = pl.program_id(2)
        acc[...] = jnp.zeros_like(acc)

        def step(j, carry, diagonal):
            k0 = pl.multiple_of(j * blk, blk)
            kr_j = kr_ref[pl.ds(k0, blk), :]
            out = []
            for h in range(hg):
                m_i, l_i = carry[h]
                kv = jnp.concatenate([kn_ref[pl.ds(k0, blk), h * QK_NOPE:(h + 1) * QK_NOPE], kr_j], axis=1)
                qv = q_ref[:, h * HEAD_PAD:(h + 1) * HEAD_PAD]
                s = lax.dot_general(kv, qv, _NT, preferred_element_type=F32) * ATTN_SCALE
                if diagonal:
                    s = jnp.where(_diag_mask(blk), s, -jnp.inf)
                m_new = jnp.maximum(m_i, jnp.max(s, axis=0, keepdims=True))
                p = jnp.exp(s - m_new)
                alpha = jnp.exp(m_i - m_new)
                l_new = alpha * l_i + jnp.sum(p, axis=0, keepdims=True)
                acc[h] = alpha * acc[h] + jnp.dot(vt_ref[h * V_DIM:(h + 1) * V_DIM, pl.ds(k0, blk)], p.astype(BF16),
                                                  preferred_element_type=F32)
                out.append((m_new, l_new))
            return tuple(out)

        init = tuple((jnp.full((1, blk), -jnp.inf, F32), jnp.zeros((1, blk), F32)) for _ in range(hg))
        carry = lax.fori_loop(0, qi, lambda j, c: step(j, c, False), init)
        stats = step(qi, carry, True)
        for h in range(hg):
            m_i, l_i = stats[h]
            o_ref[:, h * V_DIM:(h + 1) * V_DIM] = (acc[h] / l_i).T
            lse_ref[h] = m_i + jnp.log(l_i)

        @pl.when(last)
        def _():
            _exchange(blk_ref, all_ref, send_sems, recv_sems, local_sem, finish=True, gather=True)

    return pl.pallas_call(
        body, grid=steps,
        in_specs=[pl.BlockSpec((blk, hg * HEAD_PAD), lambda b, g, i: (b * nq + i, g)),
                  pl.BlockSpec((seq, hg * QK_NOPE), lambda b, g, i: (b, g)),
                  pl.BlockSpec((seq, LANES), lambda b, g, i: (b, 0)),
                  pl.BlockSpec((hg * V_DIM, seq), lambda b, g, i: (g, b)), _ANY],
        out_specs=[pl.BlockSpec((blk, hg * V_DIM), lambda b, g, i: (b * nq + i, g)),
                   pl.BlockSpec((hg, 1, blk), lambda b, g, i: (g, 0, b * nq + i)), _ANY],
        out_shape=[jax.ShapeDtypeStruct((t, N_HEADS * V_DIM), F32), jax.ShapeDtypeStruct((N_HEADS, 1, t), F32),
                   jax.ShapeDtypeStruct((N_DEV,) + block.shape, block.dtype)],
        scratch_shapes=[pltpu.VMEM((hg, V_DIM, blk), F32)] + _EXCHANGE_SEMS,
        compiler_params=_params(("arbitrary", "arbitrary", "arbitrary")), name=name)(q, kn, kr, v_t, block)


def _attn_bwd(q, kn, kr, kn_t, kr_t, v, o, lse, do, cos, sin, parts, *, bsz, seq, name):
    t = bsz * seq
    blk = _attn_block(seq)
    nq = seq // blk
    hg = BWD_HEADS
    steps = (bsz, N_HEADS // hg)

    def body(q_ref, kn_ref, kr_ref, knt_ref, krt_ref, v_ref, o_ref, lse_ref, do_ref, cos_ref, sin_ref, parts_ref,
             dq_ref, dkn_ref, dkr_ref, dv_ref, land_ref, dqt_acc, dk_acc, dv_acc, send_sems, recv_sems, local_sem):
        first, last = _first_last(steps)

        @pl.when(first)
        def _():
            _exchange(parts_ref, land_ref, send_sems, recv_sems, local_sem, finish=False)

        dqt_acc[...] = jnp.zeros_like(dqt_acc)
        dk_acc[...] = jnp.zeros_like(dk_acc)
        dv_acc[...] = jnp.zeros_like(dv_acc)

        def q_block(i, _):
            q0 = pl.multiple_of(i * blk, blk)
            rows = []
            for h in range(hg):
                dov = do_ref[pl.ds(q0, blk), h * V_DIM:(h + 1) * V_DIM].astype(F32)
                dcol = jnp.sum(dov * o_ref[pl.ds(q0, blk), h * V_DIM:(h + 1) * V_DIM], axis=-1, keepdims=True)
                delta = jnp.broadcast_to(dcol, (blk, LANES)).T[0:1, :]
                rows.append((lse_ref[h, :, pl.ds(q0, blk)], delta))

            def pair(j, diagonal):
                k0 = pl.multiple_of(j * blk, blk)
                kr_j = kr_ref[pl.ds(k0, blk), :]
                krt_j = krt_ref[:, pl.ds(k0, blk)]
                for h in range(hg):
                    lse_i, delta = rows[h]
                    qv = q_ref[pl.ds(q0, blk), h * HEAD_PAD:(h + 1) * HEAD_PAD]
                    dov = do_ref[pl.ds(q0, blk), h * V_DIM:(h + 1) * V_DIM]
                    kv = jnp.concatenate([kn_ref[pl.ds(k0, blk), h * QK_NOPE:(h + 1) * QK_NOPE], kr_j], axis=1)
                    s = lax.dot_general(kv, qv, _NT, preferred_element_type=F32) * ATTN_SCALE
                    p = jnp.exp(s - lse_i)
                    if diagonal:
                        p = jnp.where(_diag_mask(blk), p, 0.0)
                    dv_acc[pl.ds(k0, blk), h * V_DIM:(h + 1) * V_DIM] += jnp.dot(
                        p.astype(BF16), dov, preferred_element_type=F32)
                    dp = lax.dot_general(v_ref[pl.ds(k0, blk), h * V_DIM:(h + 1) * V_DIM], dov, _NT,
                                         preferred_element_type=F32)
                    ds = (p * (dp - delta) * ATTN_SCALE).astype(BF16)
                    dk_acc[pl.ds(k0, blk), h * HEAD_PAD:(h + 1) * HEAD_PAD] += jnp.dot(ds, qv, preferred_element_type=F32)
                    base = h * HEAD_PAD
                    dqt_acc[base:base + QK_NOPE, pl.ds(q0, blk)] += jnp.dot(
                        knt_ref[h * QK_NOPE:(h + 1) * QK_NOPE, pl.ds(k0, blk)], ds, preferred_element_type=F32)
                    dqt_acc[base + QK_NOPE:base + HEAD_PAD, pl.ds(q0, blk)] += jnp.dot(
                        krt_j, ds, preferred_element_type=F32)

            def off_diagonal(j, _):
                pair(j, False)
                return 0

            lax.fori_loop(0, i, off_diagonal, 0)
            pair(i, True)
            return 0

        lax.fori_loop(0, nq, q_block, 0)
        dkr = jnp.zeros((seq, LANES), F32)
        for h in range(hg):
            base = h * HEAD_PAD
            for i in range(nq):
                rows = slice(i * blk, (i + 1) * blk)
                dq = dqt_acc[base:base + HEAD_PAD, rows].T
                dq_ref[rows, base:base + QK_NOPE] = dq[:, :QK_NOPE].astype(BF16)
                dq_ref[rows, base + QK_NOPE:base + HEAD_PAD] = _rope_t(
                    dq[:, QK_NOPE:], cos_ref[rows, :], sin_ref[rows, :]).astype(BF16)
            dkn_ref[:, h * QK_NOPE:(h + 1) * QK_NOPE] = dk_acc[:, base:base + QK_NOPE].astype(BF16)
            dkr = dkr + dk_acc[:, base + QK_NOPE:base + HEAD_PAD]
        dv_ref[...] = dv_acc[...].astype(BF16)

        @pl.when(pl.program_id(1) == 0)
        def _():
            dkr_ref[...] = jnp.zeros_like(dkr_ref)

        dkr_ref[...] += _rope_t(dkr, cos_ref[...], sin_ref[...])

        @pl.when(last)
        def _():
            _exchange(parts_ref, land_ref, send_sems, recv_sems, local_sem, finish=True)

    head = pl.BlockSpec((seq, hg * V_DIM), lambda b, g: (b, g))
    head_t = pl.BlockSpec((hg * V_DIM, seq), lambda b, g: (g, b))
    shared = pl.BlockSpec((seq, LANES), lambda b, g: (b, 0))
    shared_t = pl.BlockSpec((LANES, seq), lambda b, g: (0, b))
    table = pl.BlockSpec((seq, LANES), lambda b, g: (0, 0))
    qspec = pl.BlockSpec((seq, hg * HEAD_PAD), lambda b, g: (b, g))
    return pl.pallas_call(
        body, grid=steps,
        in_specs=[qspec, head, shared, head_t, shared_t, head, head,
                  pl.BlockSpec((hg, 1, seq), lambda b, g: (g, 0, b)), head, table, table, _ANY],
        out_specs=[qspec, head, shared, head, _ANY],
        out_shape=[jax.ShapeDtypeStruct((t, N_HEADS * HEAD_PAD), BF16), jax.ShapeDtypeStruct((t, N_HEADS * QK_NOPE), BF16),
                   jax.ShapeDtypeStruct((t, LANES), F32), jax.ShapeDtypeStruct((t, N_HEADS * V_DIM), BF16),
                   jax.ShapeDtypeStruct(parts.shape, parts.dtype)],
        scratch_shapes=[pltpu.VMEM((hg * HEAD_PAD, seq), F32), pltpu.VMEM((seq, hg * HEAD_PAD), F32),
                        pltpu.VMEM((seq, hg * V_DIM), F32)] + _EXCHANGE_SEMS,
        compiler_params=_params(("arbitrary", "arbitrary")), name=name)(
            q, kn, kr, kn_t, kr_t, v, o, lse, do, cos, sin, parts)


def _head_and_loss(o, g2, x1, target, w_out, g_final, *, name, bt=512):
    t, d = x1.shape
    bt = min(bt, t)
    nt = (((1,), (1,)), ((), ()))
    tn = (((0,), (0,)), ((), ()))

    def body(o_ref, g2_ref, x1_ref, tgt_ref, w_ref, gf_ref, loss_ref, dx2_ref, dw_ref, do_ref, dg2_ref, dgf_ref):
        @pl.when(pl.program_id(0) == 0)
        def _():
            for ref in (loss_ref, dgf_ref, dw_ref):
                ref[...] = jnp.zeros_like(ref)

        ov = o_ref[...]
        gv = g2_ref[...]
        sg = _sigmoid(gv)
        silu = gv * sg
        y2 = (ov * silu).astype(BF16)
        w = w_ref[...]
        x2 = x1_ref[...] + jnp.dot(y2, w, preferred_element_type=F32)
        r = lax.rsqrt(jnp.mean(x2 * x2, axis=-1, keepdims=True) + EPS)
        nrm = x2 * r
        gf = gf_ref[...]
        err = nrm * gf - tgt_ref[...]
        loss_ref[...] += 0.5 * jnp.sum(jnp.mean(err * err, axis=-1, keepdims=True))
        dyf = err * (1.0 / d)
        dgf_ref[...] += jnp.sum(dyf * nrm, axis=0, keepdims=True)
        dn = dyf * gf
        dx2 = r * (dn - nrm * jnp.mean(dn * nrm, axis=-1, keepdims=True))
        dx2_ref[...] = dx2
        dx2 = dx2.astype(BF16)
        dw_ref[...] += lax.dot_general(y2, dx2, tn, preferred_element_type=F32)
        dy2 = lax.dot_general(dx2, w, nt, preferred_element_type=F32)
        do_ref[...] = (dy2 * silu).astype(BF16)
        dg2_ref[...] = (dy2 * ov * (sg * (1.0 + gv * (1.0 - sg)))).astype(BF16)

    row = pl.BlockSpec((bt, d), lambda i: (i, 0))
    vec = pl.BlockSpec((1, d), lambda i: (0, 0))
    return pl.pallas_call(
        body, grid=(t // bt,),
        in_specs=[row, row, row, row, pl.BlockSpec((d, d), lambda i: (0, 0)), vec],
        out_specs=[pl.BlockSpec((8, LANES), lambda i: (0, 0)), row, pl.BlockSpec((d, d), lambda i: (0, 0)), row, row, vec],
        out_shape=[jax.ShapeDtypeStruct((8, LANES), F32), jax.ShapeDtypeStruct((t, d), F32),
                   jax.ShapeDtypeStruct((d, d), F32), jax.ShapeDtypeStruct((t, d), BF16),
                   jax.ShapeDtypeStruct((t, d), BF16), jax.ShapeDtypeStruct((1, d), F32)],
        compiler_params=_params(("arbitrary",)), name=name)(o, g2, x1, target, w_out, g_final)


def _sum_parts(parts, *, name, br=GRAD_BLOCK):
    npart, rows, w = parts.shape

    def body(p_ref, o_ref):
        acc = p_ref[0].astype(F32)
        for j in range(1, npart):
            acc = acc + p_ref[j].astype(F32)
        o_ref[...] = acc

    return pl.pallas_call(
        body, grid=(rows // br,), in_specs=[pl.BlockSpec((npart, br, w), lambda i: (0, i, 0))],
        out_specs=pl.BlockSpec((br, w), lambda i: (i, 0)), out_shape=jax.ShapeDtypeStruct((rows, w), F32),
        compiler_params=_params(("parallel",)), name=name)(parts)


def _chip_partial(parts, recv, *, name, br=GRAD_BLOCK):
    _, rows, w = parts.shape
    core = lax.axis_index("c").astype(jnp.int32).reshape(1)

    def body(c_ref, p_ref, r_ref, o_ref):
        o_ref[...] = (p_ref[...] + r_ref[...]).astype(BF16)

    grid_spec = pltpu.PrefetchScalarGridSpec(
        num_scalar_prefetch=1, grid=(4, rows // br),
        in_specs=[pl.BlockSpec((None, br, w), lambda k, i, c_ref: (2 * k + c_ref[0], i, 0)),
                  pl.BlockSpec((None, br, w), lambda k, i, c_ref: (k, i, 0))],
        out_specs=pl.BlockSpec((None, br, w), lambda k, i, c_ref: (k, i, 0)))
    return pl.pallas_call(
        body, grid_spec=grid_spec, out_shape=jax.ShapeDtypeStruct((4, rows, w), BF16),
        compiler_params=_params(("parallel", "parallel")), name=name)(core, parts, recv)


def _as_block(a):
    if a.ndim == 1:
        return a.reshape(1, -1)
    if a.ndim > 2 and a.shape[0] == 1:
        return a.reshape(a.shape[1:])
    return a


def _adamw(g, w, m, v, *, name):
    shape = w.shape
    g, w, m, v = (_as_block(a) for a in (g, w, m, v))

    def body(g_ref, w_ref, m_ref, v_ref, d_ref, nm_ref, nv_ref):
        gv = g_ref[...]
        nm = ADAM_B1 * m_ref[...] + (1.0 - ADAM_B1) * gv
        nv = ADAM_B2 * v_ref[...] + (1.0 - ADAM_B2) * (gv * gv)
        nm_ref[...] = nm
        nv_ref[...] = nv
        m_hat = nm / (1.0 - ADAM_B1 ** ADAM_STEP)
        v_hat = nv / (1.0 - ADAM_B2 ** ADAM_STEP)
        d_ref[...] = (-ADAM_LR) * (m_hat / (jnp.sqrt(v_hat) + ADAM_EPS) + ADAM_WD * w_ref[...])

    whole = pl.BlockSpec(memory_space=pltpu.VMEM)
    outs = pl.pallas_call(
        body, in_specs=[whole] * 4, out_specs=[whole] * 3, out_shape=[jax.ShapeDtypeStruct(w.shape, F32)] * 3,
        compiler_params=_params(), name=name)(g, w, m, v)
    return [o.reshape(shape) for o in outs]


def _all_gather(block, *, name):
    m, n = block.shape

    def body(x_ref, out_ref, send_sems, recv_sems, local_sem):
        for phase in range(3):
            _gather_two_level(x_ref, out_ref, send_sems, recv_sems, local_sem, phase=phase)

    return pl.pallas_call(
        body, out_shape=jax.ShapeDtypeStruct((N_DEV, m, n), block.dtype), in_specs=[_ANY], out_specs=_ANY,
        scratch_shapes=_EXCHANGE_SEMS, name=name)(block)


def _exchange_d2d(parts, *, name):
    _, rows, w = parts.shape

    def body(p_ref, land_ref, send_sems, recv_sems):
        x, y, c = _mesh_pos()
        sends = []
        for k in range(4):
            cp = pltpu.make_async_remote_copy(
                src_ref=p_ref.at[2 * k + (1 - c)], dst_ref=land_ref.at[k], send_sem=send_sems.at[k],
                recv_sem=recv_sems.at[k], device_id=(x, y, 1 - c), device_id_type=pl.DeviceIdType.MESH)
            cp.start()
            sends.append(cp)
        for cp in sends:
            cp.wait_recv()
        for cp in sends:
            cp.wait_send()

    return pl.pallas_call(
        body, out_shape=jax.ShapeDtypeStruct((4, rows, w), parts.dtype), in_specs=[_ANY], out_specs=_ANY,
        scratch_shapes=[pltpu.SemaphoreType.DMA((4,)), pltpu.SemaphoreType.DMA((4,))], name=name)(parts)


def _exchange_ici(parts, *, name):
    def body(p_ref, land_ref, send_sems, recv_sems, local_sem):
        x, y, c = _mesh_pos()
        mine = pltpu.make_async_copy(p_ref.at[2 * x + y], land_ref.at[3], local_sem)
        mine.start()
        sends = []
        for k, (px, py) in enumerate([(1 - x, y), (x, 1 - y), (1 - x, 1 - y)]):
            cp = pltpu.make_async_remote_copy(
                src_ref=p_ref.at[2 * px + py], dst_ref=land_ref.at[k], send_sem=send_sems.at[k],
                recv_sem=recv_sems.at[k], device_id=(px, py, c), device_id_type=pl.DeviceIdType.MESH)
            cp.start()
            sends.append(cp)
        for cp in sends:
            cp.wait_recv()
        for cp in sends:
            cp.wait_send()
        mine.wait()

    return pl.pallas_call(
        body, out_shape=jax.ShapeDtypeStruct(parts.shape, parts.dtype), in_specs=[_ANY], out_specs=_ANY,
        scratch_shapes=[pltpu.SemaphoreType.DMA((3,)), pltpu.SemaphoreType.DMA((3,)), pltpu.SemaphoreType.DMA(())],
        name=name)(parts)


def _rows(a):
    return a.reshape(-1, PACK_W)


def _pad_to(a, n):
    return jnp.pad(a, (0, n - a.shape[0]))


def _weight_blocks(d):
    small = _rows(_pad_to(jnp.concatenate([d[n].reshape(-1) for n, _ in _SMALL]), 16 * PACK_W))
    bits = lax.bitcast_convert_type(small, jnp.uint32)
    halves = [lax.bitcast_convert_type(h.astype(jnp.uint16), WIRE) for h in (bits >> 16, bits & 0xFFFF)]
    block_a = jnp.concatenate([d["w_in_a"][0].T.astype(WIRE)] + halves, axis=0)
    w_uq = jnp.pad(d["w_uq"][0], ((0, 0), (0, 0), (0, HEAD_PAD - QK_NOPE - QK_ROPE)))
    pieces = {"w_out_a": d["w_out_a"], "w_dkv": d["w_dkv"], "w_uk": d["w_uk"], "w_uv": d["w_uv"],
              "w_in_b": d["w_in_b"][0].T, "w_uq": w_uq}
    block_b = jnp.concatenate([_rows(pieces[n]) for n, _ in _PIECES_B]
                              + [jnp.zeros((WIRE_ROWS_B - MATRIX_ROWS_B, PACK_W), F32)], axis=0).astype(WIRE)
    return block_a, block_b, d["w_out_b"][0].astype(WIRE)


def _weights_a(wall):
    w = {}
    lo, hi = _OFF_A["w_in_a"]
    w["w_in_a_t"] = wall[:, lo:hi].reshape(2 * D_RNN, D_MODEL)
    high, low = (lax.bitcast_convert_type(wall[:, r:r + 16], jnp.uint16).astype(jnp.uint32)
                 for r in (MATRIX_ROWS_A, MATRIX_ROWS_A + 16))
    small = lax.bitcast_convert_type((high << 16) | low, F32)[:, :8].reshape(N_DEV, 8 * PACK_W)
    off = dict(zip([n for n, _ in _SMALL], [0, 128, 768, 928, 1088, 1248]))
    w["norm_a"] = small[:, :128].reshape(1, D_MODEL)

    def by_channel(lo, rows):
        a = small[:, lo:lo + rows * (D_RNN // N_DEV)].reshape(N_DEV, rows, -1).transpose(1, 0, 2).reshape(rows, D_RNN)
        return jnp.pad(a, ((0, 8 - rows), (0, 0)))

    w["conv_taps"] = by_channel(off["conv_w"], CONV_WIDTH)
    w["lru_vecs"] = by_channel(off["conv_b"], 4)
    return w


def _weights_b(wall):
    piece = {n: wall[:, lo:hi] for n, (lo, hi) in _OFF_B.items()}
    w = {"w_out_a": piece["w_out_a"].reshape(D_RNN, D_MODEL)}
    w_dkv = piece["w_dkv"].reshape(D_MODEL, KV_RANK + QK_ROPE)
    w["w_dkv_c"] = w_dkv[:, :KV_RANK]
    w["w_dkv_r"] = jnp.pad(w_dkv[:, KV_RANK:], ((0, 0), (0, LANES - QK_ROPE)))
    w["w_uk"] = piece["w_uk"].reshape(KV_RANK, N_HEADS * QK_NOPE)
    w["w_uv"] = piece["w_uv"].reshape(KV_RANK, N_HEADS * V_DIM)
    w["w_in_b_t"] = piece["w_in_b"].reshape(Q_RANK + N_HEADS * V_DIM, D_MODEL)
    w["w_uq"] = piece["w_uq"].reshape(Q_RANK, N_HEADS * HEAD_PAD)
    return w


def _pack_rep(d):
    flat = jnp.concatenate([d[n].reshape(-1) for n, _ in _REP])
    return _rows(_pad_to(flat, REP_ROWS * PACK_W))


def _unpack_rep(p, like):
    flat = p.reshape(-1)
    out, off = {}, 0
    for n, k in _REP:
        out[n] = flat[off:off + k].reshape(like[n].shape)
        off += k
    return out


def _by_owner(a):
    return a.reshape(N_DEV, -1, PACK_W)


def _grad_parts_b(g):
    tail = jnp.zeros((N_DEV, WIRE_ROWS_B - MATRIX_ROWS_B, PACK_W), F32)
    return jnp.concatenate([_by_owner(g[n]) for n, _ in _PIECES_B] + [tail], axis=1).astype(BF16)


def _grad_parts_a(g):
    small = jnp.concatenate([
        g["norm_a"].reshape(N_DEV, -1),
        g["conv_w"].reshape(CONV_WIDTH, N_DEV, -1).transpose(1, 0, 2).reshape(N_DEV, -1),
        g["conv_b"].reshape(N_DEV, -1), g["b_rg"].reshape(N_DEV, -1), g["b_ig"].reshape(N_DEV, -1),
        g["lru_lambda"].reshape(N_DEV, -1)], axis=1)
    small = jnp.pad(small, ((0, 0), (0, 8 * PACK_W - small.shape[1]))).reshape(N_DEV, 8, PACK_W)
    half = N_DEV // 2
    w_in_a = jnp.concatenate([h.reshape(half, -1, PACK_W) for h in g["w_in_a_t"]], axis=0)
    rep = _pack_rep(g).reshape(N_DEV, REP_SLICE, PACK_W)
    tail = jnp.zeros((N_DEV, GRAD_ROWS_A - MATRIX_ROWS_A - 8 - REP_SLICE, PACK_W), F32)
    return jnp.concatenate([w_in_a, small, rep, tail], axis=1)


def _own_grads(sum_a, sum_b, sum_c):
    out = {}
    lo, hi = _OFF_A["w_in_a"]
    out["w_in_a"] = sum_a[lo:hi].T.reshape(1, D_MODEL, 2 * D_RNN // N_DEV)
    small = sum_a[MATRIX_ROWS_A:MATRIX_ROWS_A + 8].reshape(-1)
    shapes = {"norm_a": (1, D_MODEL // N_DEV), "conv_w": (1, CONV_WIDTH, D_RNN // N_DEV), "conv_b": (1, D_RNN // N_DEV),
              "b_rg": (1, D_RNN // N_DEV), "b_ig": (1, D_RNN // N_DEV), "lru_lambda": (1, D_RNN // N_DEV)}
    off = 0
    for n, k in _SMALL:
        out[n] = small[off:off + k].reshape(shapes[n])
        off += k
    piece = {n: sum_b[lo:hi] for n, (lo, hi) in _OFF_B.items()}
    out["w_out_a"] = piece["w_out_a"].reshape(1, D_RNN // N_DEV, D_MODEL)
    out["w_dkv"] = piece["w_dkv"].reshape(D_MODEL // N_DEV, KV_RANK + QK_ROPE)
    out["w_uk"] = piece["w_uk"].reshape(KV_RANK // N_DEV, N_HEADS, QK_NOPE)
    out["w_uv"] = piece["w_uv"].reshape(KV_RANK // N_DEV, N_HEADS, V_DIM)
    out["w_in_b"] = piece["w_in_b"].T.reshape(1, D_MODEL, (Q_RANK + N_HEADS * V_DIM) // N_DEV)
    out["w_uq"] = piece["w_uq"].reshape(1, Q_RANK // N_DEV, N_HEADS, HEAD_PAD)[..., :QK_NOPE + QK_ROPE]
    out["w_out_b"] = sum_c.reshape(1, N_HEADS * V_DIM // N_DEV, D_MODEL)
    return out


def _step(x, target, w, rep, block_b, block_c, *, bsz, seq):
    t = bsz * seq
    cos, sin = _rope_tables(seq)
    g_a = w["norm_a"]
    g_kv = rep["norm_kv"].reshape(1, -1)
    g_kvn = rep["kv_norm"].reshape(1, -1)
    g_b = rep["norm_b"].reshape(1, -1)
    g_q = rep["q_norm"].reshape(1, -1)
    g_f = rep["final_norm"].reshape(1, -1)
    wrg = rep["w_rg"][0].astype(BF16)
    wig = rep["w_ig"][0].astype(BF16)
    cw8, vecs = w["conv_taps"], w["lru_vecs"]

    def seq3(a):
        return a.reshape(bsz, seq, a.shape[-1])

    def flat(a):
        return a.reshape(t, a.shape[-1])

    h0, xp, ga = _lru_proj_fwd(x, g_a, w["w_in_a_t"], name="lru_proj_fwd")
    xb, hs, y, wall_b = _lru_fwd(seq3(xp), seq3(ga), cw8, vecs, wrg, wig, block_b, name="lru_fwd")
    w = dict(w, **_weights_b(wall_b))
    x1 = _matmul(flat(y), w["w_out_a"], residual=x, name="out_a")
    hk, hq, ck, cqp, g2, ckv, cq, q, kn, v, kr, kn_t, v_t, kr_t = _mla_proj_fwd(
        x1, (g_kv, g_b, g_kvn, g_q), w, cos, sin, seq=seq, name="mla_proj_fwd")
    o, lse, wall_c = _attn_fwd(q, kn, kr, v_t, block_c, bsz=bsz, seq=seq, name="attn_fwd")
    w_out_b = wall_c.reshape(N_HEADS * V_DIM, D_MODEL)
    loss, dx2, d_w_out_b, do, dg2, dgf = _head_and_loss(o, g2, x1, target, w_out_b, g_f, name="head_loss")
    grads = {"final_norm": dgf}
    parts_c = _by_owner(d_w_out_b).astype(BF16)
    dq, dkn, dkr, dv, landed_c = _attn_bwd(q, kn, kr, kn_t, kr_t, v, o, lse, do, cos, sin, parts_c,
                                           bsz=bsz, seq=seq, name="attn_bwd")
    grads["w_uq"] = _matmul_tn(cq, dq, name="d_w_uq")
    dx1, du2, dckr, dgkv, dgb, dgkvn, dgq, dy = _mla_proj_bwd(
        x1, dx2, cqp, ck, dq, dkn, dv, dkr, dg2, (g_kv, g_b, g_kvn, g_q), w, name="mla_proj_bwd")
    grads["norm_kv"], grads["norm_b"], grads["kv_norm"], grads["q_norm"] = dgkv, dgb, dgkvn, dgq
    grads["w_in_b"] = _matmul_tn(du2, hq, name="d_w_in_b_t")
    grads["w_uk"] = _matmul_tn(ckv, dkn, name="d_w_uk")
    grads["w_uv"] = _matmul_tn(ckv, dv, name="d_w_uv")
    grads["w_dkv"] = _matmul_tn(hk, dckr, name="d_w_dkv")[:, :KV_RANK + QK_ROPE]
    grads["w_out_a"] = _matmul_tn(flat(y), dx1, name="d_w_out_a")
    parts_b = _grad_parts_b(grads)
    dxp, dga, dwrg, dwig, dvec, landed_b = _lru_bwd(
        seq3(dy), seq3(xp), xb, hs, seq3(ga), cw8, vecs, wrg, wig, parts_b, name="lru_bwd")
    dxp, dga = flat(dxp), flat(dga)
    grads["w_rg"], grads["w_ig"] = dwrg, dwig
    grads["b_rg"], grads["b_ig"], grads["conv_b"] = dvec[0], dvec[1], dvec[3]
    lam = vecs[3]
    grads["lru_lambda"] = dvec[2] * (-1.0 / (1.0 + jnp.exp(lam)))
    grads["conv_w"] = dvec[4:4 + CONV_WIDTH]
    dx, dga_norm, dwx, dwg = _lru_proj_bwd(dxp, dga, x, dx1, h0, g_a, w["w_in_a_t"], name="lru_proj_bwd")
    grads["norm_a"] = dga_norm
    grads["w_in_a_t"] = (dwx, dwg)
    return loss[0, 0], dx, grads, landed_b, landed_c


def kernel(x, norm_a, w_in_a, conv_w, conv_b, w_rg, b_rg, w_ig, b_ig, lru_lambda, w_out_a, norm_kv, w_dkv, kv_norm, w_uk, w_uv, norm_b, w_in_b, q_norm, w_uq, w_out_b, final_norm, loss_target, m_norm_a, m_w_in_a, m_conv_w, m_conv_b, m_w_rg, m_b_rg, m_w_ig, m_b_ig, m_lru_lambda, m_w_out_a, m_norm_kv, m_w_dkv, m_kv_norm, m_w_uk, m_w_uv, m_norm_b, m_w_in_b, m_q_norm, m_w_uq, m_w_out_b, m_final_norm, v_norm_a, v_w_in_a, v_conv_w, v_conv_b, v_w_rg, v_b_rg, v_w_ig, v_b_ig, v_lru_lambda, v_w_out_a, v_norm_kv, v_w_dkv, v_kv_norm, v_w_uk, v_w_uv, v_norm_b, v_w_in_b, v_q_norm, v_w_uq, v_w_out_b, v_final_norm):
    given = dict(locals())
    wts = {n: given[n] for n in WEIGHTS}
    mom1 = {n: given["m_" + n] for n in WEIGHTS}
    mom2 = {n: given["v_" + n] for n in WEIGHTS}
    bsz, seq, _ = x.shape
    t = bsz * seq

    block_a, block_b, block_c = _weight_blocks(wts)
    w = _weights_a(_all_gather(block_a, name="gather_weights_a"))
    loss, dx, grads, landed_b, landed_c = _step(x.reshape(t, D_MODEL), loss_target.reshape(t, D_MODEL), w, wts,
                                                block_b, block_c, bsz=bsz, seq=seq)

    parts_a = _grad_parts_a(grads)
    from_sibling = _exchange_d2d(parts_a, name="exchange_grads_d2d")
    chip_parts = _chip_partial(parts_a, from_sibling, name="chip_partial_grads")
    landed_a = _exchange_ici(chip_parts, name="exchange_grads_ici")
    sum_a = _sum_parts(landed_a, name="sum_grads_a", br=GRAD_BLOCK)
    sum_b = _sum_parts(landed_b, name="sum_grads_b", br=WIRE_ROWS_B // 2)
    sum_c = _sum_parts(landed_c, name="sum_grads_c", br=landed_c.shape[1])
    g_own = _own_grads(sum_a, sum_b, sum_c)
    rep_slice = sum_a[MATRIX_ROWS_A + 8:MATRIX_ROWS_A + 8 + REP_SLICE]
    loss_rows = jnp.pad(loss.reshape(1, 1), ((0, 7), (0, PACK_W - 1)))
    gathered = _all_gather(jnp.concatenate([rep_slice, loss_rows], axis=0), name="gather_replicated")
    g_own.update(_unpack_rep(gathered[:, :REP_SLICE].reshape(REP_ROWS, PACK_W), wts))
    loss = jnp.sum(gathered[:, REP_SLICE, 0])

    deltas, new_m, new_v = {}, {}, {}
    for n in WEIGHTS:
        deltas[n], new_m[n], new_v[n] = _adamw(g_own[n], wts[n], mom1[n], mom2[n], name="adamw_" + n)
    result = [loss, dx.reshape(bsz, seq, D_MODEL)]
    for d in (g_own, deltas, new_m, new_v):
        result.extend(d[n] for n in WEIGHTS)
    return tuple(result)
```

```python
import jax
import jax.numpy as jnp
from jax import lax
from jax.experimental import pallas as pl
from jax.experimental.pallas import tpu as pltpu

F32 = jnp.float32
BF16 = jnp.bfloat16
WIRE = jnp.bfloat16

D_MODEL = 1024
D_RNN = 1280
RNN_BLOCKS = 10
RNN_BW = 128
CONV_WIDTH = 4
LRU_C = 8.0
N_HEADS = 8
QK_NOPE = 128
QK_ROPE = 64
V_DIM = 128
KV_RANK = 256
Q_RANK = 384
ROPE_THETA = 10000.0
EPS = 1e-6
ATTN_SCALE = (QK_NOPE + QK_ROPE) ** -0.5
HEAD_PAD = 256
LANES = 128

ADAM_LR = 0.001
ADAM_B1 = 0.9
ADAM_B2 = 0.999
ADAM_EPS = 1e-08
ADAM_WD = 0.01
ADAM_STEP = 10

N_DEV = 8
VMEM_LIMIT_BYTES = 56 * 2**20
PACK_W = 1024

_PIECES_A = (("w_in_a", 320),)
_PIECES_B = (("w_out_a", 160), ("w_dkv", 40), ("w_uk", 32), ("w_uv", 32), ("w_in_b", 176), ("w_uq", 96))


def _offsets(pieces):
    off, r = {}, 0
    for n, k in pieces:
        off[n] = (r, r + k)
        r += k
    return off, r


_OFF_A, MATRIX_ROWS_A = _offsets(_PIECES_A)
_OFF_B, MATRIX_ROWS_B = _offsets(_PIECES_B)
WIRE_ROWS_A = MATRIX_ROWS_A + 32
WIRE_ROWS_B = 544
_SMALL = (("norm_a", 128), ("conv_w", 640), ("conv_b", 160), ("b_rg", 160), ("b_ig", 160), ("lru_lambda", 160))
_REP = (("w_rg", 163840), ("w_ig", 163840), ("norm_kv", 1024), ("kv_norm", 256), ("norm_b", 1024),
        ("q_norm", 384), ("final_norm", 1024))
REP_ROWS = 384
REP_SLICE = REP_ROWS // N_DEV
GRAD_ROWS_A = 384
GRAD_BLOCK = 192

WEIGHTS = ("norm_a", "w_in_a", "conv_w", "conv_b", "w_rg", "b_rg", "w_ig", "b_ig", "lru_lambda", "w_out_a",
           "norm_kv", "w_dkv", "kv_norm", "w_uk", "w_uv", "norm_b", "w_in_b", "q_norm", "w_uq", "w_out_b",
           "final_norm")


def _params(sem=None):
    return pltpu.CompilerParams(dimension_semantics=sem, vmem_limit_bytes=VMEM_LIMIT_BYTES)


_NT = (((1,), (1,)), ((), ()))
_ANY = pl.BlockSpec(memory_space=pl.ANY)


def _mesh_pos():
    return lax.axis_index("x"), lax.axis_index("y"), lax.axis_index("c")


def _sigmoid(z):
    return 0.5 * jnp.tanh(0.5 * z) + 0.5


def _sigmoid_tail(z):
    return 1.0 / (1.0 + jnp.exp(-z))


def _col_block(n):
    return n if n <= 1408 else n // 2


def _matmul(a, b, *, name, nt=False, out_dtype=F32, residual=None, bm=1024):
    m, k = a.shape
    n = b.shape[0] if nt else b.shape[1]
    bm = min(bm, m)
    bn = _col_block(n)
    dims = (((1,), (1,)), ((), ())) if nt else (((1,), (0,)), ((), ()))
    has_res = residual is not None

    def body(*refs):
        a_ref, b_ref, o_ref = refs[0], refs[1], refs[-1]
        acc = lax.dot_general(a_ref[...].astype(BF16), b_ref[...].astype(BF16), dims, preferred_element_type=F32)
        if has_res:
            acc = acc + refs[2][...]
        o_ref[...] = acc.astype(out_dtype)

    in_specs = [pl.BlockSpec((bm, k), lambda i, j: (i, 0)),
                pl.BlockSpec((bn, k), lambda i, j: (j, 0)) if nt else pl.BlockSpec((k, bn), lambda i, j: (0, j))]
    args = [a, b]
    if has_res:
        in_specs.append(pl.BlockSpec((bm, bn), lambda i, j: (i, j)))
        args.append(residual)
    return pl.pallas_call(
        body, grid=(m // bm, n // bn), in_specs=in_specs, out_specs=pl.BlockSpec((bm, bn), lambda i, j: (i, j)),
        out_shape=jax.ShapeDtypeStruct((m, n), out_dtype), compiler_params=_params(("parallel", "parallel")),
        name=name)(*args)


def _token_sums(operands, pairs, *, name, bt=1024):
    t = operands[0].shape[0]
    bt = min(bt, t)
    k = len(operands)

    def body(*refs):
        ins, outs = refs[:k], refs[k:]

        @pl.when(pl.program_id(0) == 0)
        def _():
            for o_ref in outs:
                o_ref[...] = jnp.zeros_like(o_ref)

        vals = [r[...].astype(BF16) for r in ins]
        for (i, j), o_ref in zip(pairs, outs):
            o_ref[...] += lax.dot_general(vals[i], vals[j], (((0,), (0,)), ((), ())), preferred_element_type=F32)

    shapes = [(operands[i].shape[1], operands[j].shape[1]) for i, j in pairs]
    return pl.pallas_call(
        body, grid=(t // bt,),
        in_specs=[pl.BlockSpec((bt, a.shape[1]), lambda s: (s, 0)) for a in operands],
        out_specs=[pl.BlockSpec(shape, lambda s: (0, 0)) for shape in shapes],
        out_shape=[jax.ShapeDtypeStruct(shape, F32) for shape in shapes],
        compiler_params=_params(("arbitrary",)), name=name)(*operands)


def _swap_halves(v):
    ax = v.ndim - 1
    lane = lax.broadcasted_iota(jnp.int32, v.shape, ax)
    up = pltpu.roll(v, LANES - QK_ROPE // 2, axis=ax)
    down = pltpu.roll(v, QK_ROPE // 2, axis=ax)
    return jnp.where(lane < QK_ROPE // 2, up, jnp.where(lane < QK_ROPE, down, 0.0))


def _rope(v, cos, sin):
    return v * cos + _swap_halves(v) * sin


def _rope_t(d, cos, sin):
    return d * cos + _swap_halves(d * sin)


def _rope_tables(seq):
    pos = jnp.arange(seq, dtype=F32)
    inv = ROPE_THETA ** (-jnp.arange(0, QK_ROPE, 2, dtype=F32) / QK_ROPE)
    ang = pos[:, None] * inv[None, :]
    cos, sin = jnp.cos(ang), jnp.sin(ang)
    zero = jnp.zeros((seq, LANES - QK_ROPE), F32)
    return jnp.concatenate([cos, cos, zero], axis=1), jnp.concatenate([-sin, sin, zero], axis=1)


def _rms(v):
    return v * lax.rsqrt(jnp.mean(v * v, axis=-1, keepdims=True) + EPS)


def _const_spec(a):
    return pl.BlockSpec(a.shape, lambda i: (0,) * a.ndim)


def _lru_proj_fwd(x, g_a, w_in_t, *, name, bt=512):
    t, d = x.shape
    bt = min(bt, t)
    n = w_in_t.shape[0] // 2

    def body(x_ref, g_ref, wt_ref, h_ref, xp_ref, ga_ref):
        h = (_rms(x_ref[...]) * g_ref[...]).astype(BF16)
        h_ref[...] = h
        xp_ref[...] = lax.dot_general(h, wt_ref[0:n, :], _NT, preferred_element_type=F32)
        ga_ref[...] = lax.dot_general(h, wt_ref[n:2 * n, :], _NT, preferred_element_type=F32)

    row = lambda w: pl.BlockSpec((bt, w), lambda i: (i, 0))
    return pl.pallas_call(
        body, grid=(t // bt,), in_specs=[row(d), _const_spec(g_a), _const_spec(w_in_t)],
        out_specs=[row(d), row(n), row(n)],
        out_shape=[jax.ShapeDtypeStruct((t, d), BF16), jax.ShapeDtypeStruct((t, n), F32), jax.ShapeDtypeStruct((t, n), F32)],
        compiler_params=_params(("parallel",)), name=name)(x, g_a, w_in_t)


def _mla_proj_fwd(x1, gains, w, cos, sin, *, seq, name, bt=512):
    t, d = x1.shape
    bt = min(bt, seq)
    per_seq = seq // bt
    g_kv, g_b, g_kvn, g_q = gains
    consts = [g_kv, g_b, g_kvn, g_q, w["w_dkv_c"], w["w_dkv_r"], w["w_in_b_t"], w["w_uk"], w["w_uv"], w["w_uq"]]

    def body(x_ref, cos_ref, sin_ref, gkv_ref, gb_ref, gkvn_ref, gq_ref, wdc_ref, wdr_ref, wbt_ref,
             wuk_ref, wuv_ref, wuq_ref,
             hk_ref, hq_ref, ck_ref, cqp_ref, g2_ref, ckv_ref, cq_ref, q_ref, kn_ref, v_ref, kr_ref, knt_ref, vt_ref, krt_ref):
        nrm = _rms(x_ref[...])
        hk = (nrm * gkv_ref[...]).astype(BF16)
        hq = (nrm * gb_ref[...]).astype(BF16)
        hk_ref[...] = hk
        hq_ref[...] = hq
        ck = jnp.dot(hk, wdc_ref[...], preferred_element_type=F32)
        ck_ref[...] = ck
        cqp = lax.dot_general(hq, wbt_ref[0:Q_RANK, :], _NT, preferred_element_type=F32)
        cqp_ref[...] = cqp
        g2_ref[...] = lax.dot_general(hq, wbt_ref[Q_RANK:, :], _NT, preferred_element_type=F32)
        cosv, sinv = cos_ref[...], sin_ref[...]
        kr = _rope(jnp.dot(hk, wdr_ref[...], preferred_element_type=F32), cosv, sinv)
        kr_ref[...] = kr.astype(BF16)
        krt_ref[...] = kr.T.astype(BF16)
        ckv = (_rms(ck) * gkvn_ref[...]).astype(BF16)
        ckv_ref[...] = ckv
        kn = jnp.dot(ckv, wuk_ref[...], preferred_element_type=F32)
        v = jnp.dot(ckv, wuv_ref[...], preferred_element_type=F32)
        kn_ref[...] = kn.astype(BF16)
        v_ref[...] = v.astype(BF16)
        knt_ref[...] = kn.T.astype(BF16)
        vt_ref[...] = v.T.astype(BF16)
        cq = (_rms(cqp) * gq_ref[...]).astype(BF16)
        cq_ref[...] = cq
        for h in range(N_HEADS):
            qh = jnp.dot(cq, wuq_ref[:, h * HEAD_PAD:(h + 1) * HEAD_PAD], preferred_element_type=F32)
            q_ref[:, h * HEAD_PAD:h * HEAD_PAD + QK_NOPE] = qh[:, :QK_NOPE].astype(BF16)
            q_ref[:, h * HEAD_PAD + QK_NOPE:(h + 1) * HEAD_PAD] = _rope(qh[:, QK_NOPE:], cosv, sinv).astype(BF16)

    row = lambda w_: pl.BlockSpec((bt, w_), lambda i: (i, 0))
    col = lambda h_: pl.BlockSpec((h_, bt), lambda i: (0, i))
    tab = pl.BlockSpec((bt, LANES), lambda i: (i % per_seq, 0))
    nh = N_HEADS * V_DIM
    shapes = [((t, d), BF16), ((t, d), BF16), ((t, KV_RANK), F32), ((t, Q_RANK), F32), ((t, nh), F32), ((t, KV_RANK), BF16),
              ((t, Q_RANK), BF16), ((t, N_HEADS * HEAD_PAD), BF16), ((t, nh), BF16), ((t, nh), BF16), ((t, LANES), BF16),
              ((nh, t), BF16), ((nh, t), BF16), ((LANES, t), BF16)]
    out_specs = [row(d), row(d), row(KV_RANK), row(Q_RANK), row(nh), row(KV_RANK), row(Q_RANK), row(N_HEADS * HEAD_PAD),
                 row(nh), row(nh), row(LANES), col(nh), col(nh), col(LANES)]
    return pl.pallas_call(
        body, grid=(t // bt,), in_specs=[row(d), tab, tab] + [_const_spec(a) for a in consts], out_specs=out_specs,
        out_shape=[jax.ShapeDtypeStruct(s, dt) for s, dt in shapes],
        compiler_params=_params(("parallel",)), name=name)(x1, cos, sin, *consts)


def _rms_bwd_rows(xv, dn):
    r = lax.rsqrt(jnp.mean(xv * xv, axis=-1, keepdims=True) + EPS)
    nrm = xv * r
    return r * (dn - nrm * jnp.mean(dn * nrm, axis=-1, keepdims=True)), nrm


def _col_sum(v):
    return jnp.sum(v, axis=0, keepdims=True)


def _lru_proj_bwd(dxp, dga, x, dx1, h0, g_a, w_in_t, *, name, bt=512):
    t, d = x.shape
    bt = min(bt, t)
    n = w_in_t.shape[0] // 2
    tn = (((0,), (0,)), ((), ()))

    def body(dxp_ref, dga_ref, x_ref, dx1_ref, h0_ref, g_ref, wt_ref, dx_ref, dg_ref, dwx_ref, dwg_ref):
        @pl.when(pl.program_id(0) == 0)
        def _():
            for ref in (dg_ref, dwx_ref, dwg_ref):
                ref[...] = jnp.zeros_like(ref)

        dxp_v, dga_v, h0 = dxp_ref[...], dga_ref[...], h0_ref[...]
        dwx_ref[...] += lax.dot_general(dxp_v, h0, tn, preferred_element_type=F32)
        dwg_ref[...] += lax.dot_general(dga_v, h0, tn, preferred_element_type=F32)
        dh = (jnp.dot(dxp_v, wt_ref[0:n, :], preferred_element_type=F32)
              + jnp.dot(dga_v, wt_ref[n:2 * n, :], preferred_element_type=F32))
        dxn, nrm = _rms_bwd_rows(x_ref[...], dh * g_ref[...])
        dg_ref[...] += _col_sum(dh * nrm)
        dx_ref[...] = dx1_ref[...] + dxn

    row = lambda w: pl.BlockSpec((bt, w), lambda i: (i, 0))
    whole = pl.BlockSpec((n, d), lambda i: (0, 0))
    return pl.pallas_call(
        body, grid=(t // bt,),
        in_specs=[row(n), row(n), row(d), row(d), row(d), _const_spec(g_a), _const_spec(w_in_t)],
        out_specs=[row(d), _const_spec(g_a), whole, whole],
        out_shape=[jax.ShapeDtypeStruct((t, d), F32), jax.ShapeDtypeStruct((1, d), F32),
                   jax.ShapeDtypeStruct((n, d), F32), jax.ShapeDtypeStruct((n, d), F32)],
        compiler_params=_params(("arbitrary",)), name=name)(dxp, dga, x, dx1, h0, g_a, w_in_t)


def _mla_proj_bwd(x1, dx2, cqp, ck, dq, dkn, dv, dkr, dg2, gains, w, *, name, bt=512):
    t, d = x1.shape
    bt = min(bt, t)
    g_kv, g_b, g_kvn, g_q = gains
    consts = [g_kv, g_b, g_kvn, g_q, w["w_dkv_c"], w["w_dkv_r"], w["w_in_b_t"], w["w_uk"], w["w_uv"], w["w_uq"],
              w["w_out_a"]]
    nh = N_HEADS * V_DIM

    def body(x1_ref, dx2_ref, cqp_ref, ck_ref, dq_ref, dkn_ref, dv_ref, dkr_ref, dg2_ref,
             gkv_ref, gb_ref, gkvn_ref, gq_ref, wdc_ref, wdr_ref, wbt_ref, wuk_ref, wuv_ref, wuq_ref, wo_ref,
             dx1_ref, du2_ref, dckr_ref, dgkv_ref, dgb_ref, dgkvn_ref, dgq_ref, dy_ref):
        @pl.when(pl.program_id(0) == 0)
        def _():
            for ref in (dgkv_ref, dgb_ref, dgkvn_ref, dgq_ref):
                ref[...] = jnp.zeros_like(ref)

        dot_nt = lambda a, b: lax.dot_general(a, b, _NT, preferred_element_type=F32)
        dcq = dot_nt(dq_ref[...], wuq_ref[...])
        dcqp, nq = _rms_bwd_rows(cqp_ref[...], dcq * gq_ref[...])
        dgq_ref[...] += _col_sum(dcq * nq)
        dcqp = dcqp.astype(BF16)
        dg2 = dg2_ref[...]
        du2_ref[:, :Q_RANK] = dcqp
        du2_ref[:, Q_RANK:] = dg2
        dhq = (jnp.dot(dcqp, wbt_ref[0:Q_RANK, :], preferred_element_type=F32)
               + jnp.dot(dg2, wbt_ref[Q_RANK:, :], preferred_element_type=F32))
        dckv = dot_nt(dkn_ref[...], wuk_ref[...]) + dot_nt(dv_ref[...], wuv_ref[...])
        dck, nc = _rms_bwd_rows(ck_ref[...], dckv * gkvn_ref[...])
        dgkvn_ref[...] += _col_sum(dckv * nc)
        dck = dck.astype(BF16)
        dkr = dkr_ref[...].astype(BF16)
        dckr_ref[:, :KV_RANK] = dck
        dckr_ref[:, KV_RANK:] = dkr
        dhk = dot_nt(dck, wdc_ref[...]) + dot_nt(dkr, wdr_ref[...])
        dxn, n1 = _rms_bwd_rows(x1_ref[...], dhq * gb_ref[...] + dhk * gkv_ref[...])
        dgb_ref[...] += _col_sum(dhq * n1)
        dgkv_ref[...] += _col_sum(dhk * n1)
        dx1 = dx2_ref[...] + dxn
        dx1_ref[...] = dx1
        dy_ref[...] = lax.dot_general(dx1.astype(BF16), wo_ref[...], _NT, preferred_element_type=F32)

    row = lambda w_: pl.BlockSpec((bt, w_), lambda i: (i, 0))
    vec = lambda w_: pl.BlockSpec((1, w_), lambda i: (0, 0))
    in_specs = [row(d), row(d), row(Q_RANK), row(KV_RANK), row(N_HEADS * HEAD_PAD), row(nh), row(nh), row(LANES), row(nh)]
    return pl.pallas_call(
        body, grid=(t // bt,), in_specs=in_specs + [_const_spec(a) for a in consts],
        out_specs=[row(d), row(Q_RANK + nh), row(KV_RANK + LANES), vec(d), vec(d), vec(KV_RANK), vec(Q_RANK), row(D_RNN)],
        out_shape=[jax.ShapeDtypeStruct((t, d), F32), jax.ShapeDtypeStruct((t, Q_RANK + nh), BF16),
                   jax.ShapeDtypeStruct((t, KV_RANK + LANES), BF16), jax.ShapeDtypeStruct((1, d), F32),
                   jax.ShapeDtypeStruct((1, d), F32), jax.ShapeDtypeStruct((1, KV_RANK), F32),
                   jax.ShapeDtypeStruct((1, Q_RANK), F32), jax.ShapeDtypeStruct((t, D_RNN), F32)],
        compiler_params=_params(("arbitrary",)), name=name)(x1, dx2, cqp, ck, dq, dkn, dv, dkr, dg2, *consts)


def _softplus(z):
    return jnp.maximum(z, 0.0) + jnp.log1p(jnp.exp(-jnp.abs(z)))


def _one_minus_square(a, la):
    return jnp.tanh(-la) * (1.0 + a * a)


def _gates(xb, wrg, wig, brg, big, sp):
    xbb = xb.astype(BF16)
    r = _sigmoid_tail(jnp.dot(xbb, wrg, preferred_element_type=F32) + brg)
    i = _sigmoid(jnp.dot(xbb, wig, preferred_element_type=F32) + big)
    la = (-LRU_C) * r * sp
    a = jnp.exp(la)
    em = _one_minus_square(a, la)
    inv_mult = lax.rsqrt(em)
    mult = jnp.where(em > 0.0, em * inv_mult, 0.0)
    return r, i, a, mult, inv_mult


def _conv(xpad_ref, cw_ref, seq):
    acc = cw_ref[0:1, :] * xpad_ref[pl.ds(8 - (CONV_WIDTH - 1), seq), :]
    for k in range(1, CONV_WIDTH):
        acc = acc + cw_ref[k:k + 1, :] * xpad_ref[pl.ds(8 - (CONV_WIDTH - 1) + k, seq), :]
    return acc


def _seq_spec(seq):
    return pl.BlockSpec((None, seq, RNN_BW), lambda n, b: (b, 0, n))


def _chan_spec(rows):
    return pl.BlockSpec((rows, RNN_BW), lambda n, b: (0, n))


_GATE_W_SPEC = pl.BlockSpec((None, RNN_BW, RNN_BW), lambda n, b: (n, 0, 0))


SCAN_UNROLL = 4


def _peers():
    x, y, c = _mesh_pos()
    others = []
    for k in range(1, N_DEV):
        px = 1 - x if k & 4 else x
        py = 1 - y if k & 2 else y
        pc = 1 - c if k & 1 else c
        others.append(((px, py, pc), 4 * px + 2 * py + pc))
    return 4 * x + 2 * y + c, others


def _exchange(src_ref, dst_ref, send_sems, recv_sems, local_sem, *, finish, gather=False):
    me, others = _peers()

    def send(k, dev, slot):
        return pltpu.make_async_remote_copy(
            src_ref=src_ref if gather else src_ref.at[slot], dst_ref=dst_ref.at[me], send_sem=send_sems.at[k],
            recv_sem=recv_sems.at[k], device_id=dev, device_id_type=pl.DeviceIdType.MESH)

    local = pltpu.make_async_copy(src_ref if gather else src_ref.at[me], dst_ref.at[me], local_sem)
    if not finish:
        local.start()
        for k, (dev, slot) in enumerate(others):
            send(k, dev, slot).start()
        return
    for k, (dev, slot) in enumerate(others):
        pltpu.make_async_remote_copy(
            src_ref=dst_ref.at[slot], dst_ref=dst_ref.at[slot], send_sem=send_sems.at[k], recv_sem=recv_sems.at[k],
            device_id=dev, device_id_type=pl.DeviceIdType.MESH).wait_recv()
    for k, (dev, slot) in enumerate(others):
        send(k, dev, slot).wait_send()
    local.wait()


def _gather_two_level(x_ref, out_ref, send_sems, recv_sems, local_sem, *, phase):
    x, y, c = _mesh_pos()
    me, sibling = (x, y, c), (x, y, 1 - c)
    chips = [(1 - x, y), (x, 1 - y), (1 - x, 1 - y)]

    def slot(px, py, pc):
        return out_ref.at[4 * px + 2 * py + pc]

    def copy(k, blk, to, src=None):
        return pltpu.make_async_remote_copy(
            src_ref=slot(*blk) if src is None else src, dst_ref=slot(*blk),
            send_sem=send_sems.at[k], recv_sem=recv_sems.at[k], device_id=to, device_id_type=pl.DeviceIdType.MESH)

    if phase == 0:
        pltpu.make_async_copy(x_ref, slot(*me), local_sem).start()
        copy(0, me, sibling, src=x_ref).start()
        for j, chip in enumerate(chips):
            copy(1 + j, me, (*chip, c), src=x_ref).start()
    elif phase == 1:
        for j, chip in enumerate(chips):
            copy(1 + j, (*chip, c), me).wait_recv()
            copy(4 + j, (*chip, c), sibling).start()
    else:
        copy(0, sibling, me).wait_recv()
        for j, chip in enumerate(chips):
            copy(4 + j, (*chip, 1 - c), me).wait_recv()
        copy(0, me, sibling, src=x_ref).wait_send()
        for j, chip in enumerate(chips):
            copy(1 + j, me, (*chip, c), src=x_ref).wait_send()
            copy(4 + j, (*chip, c), sibling).wait_send()
        pltpu.make_async_copy(x_ref, slot(*me), local_sem).wait()


GATHER_FORWARD_STEP = 9
_EXCHANGE_SEMS = [pltpu.SemaphoreType.DMA((N_DEV - 1,)), pltpu.SemaphoreType.DMA((N_DEV - 1,)), pltpu.SemaphoreType.DMA(())]


def _first_last(steps):
    first = last = None
    for axis, n in enumerate(steps):
        i = pl.program_id(axis)
        first = (i == 0) if first is None else first & (i == 0)
        last = (i == n - 1) if last is None else last & (i == n - 1)
    return first, last


def _lru_fwd(xp, ga, cw, vecs, wrg, wig, block, *, name):
    bsz, seq, _ = xp.shape
    groups = seq // 8

    def body(xp_ref, ga_ref, cw_ref, vec_ref, wrg_ref, wig_ref, blk_ref, xb_ref, hs_ref, y_ref, all_ref,
             xpad, a_s, b_s, send_sems, recv_sems, local_sem):
        first, last = _first_last((RNN_BLOCKS, bsz))

        @pl.when(first)
        def _():
            _gather_two_level(blk_ref, all_ref, send_sems, recv_sems, local_sem, phase=0)

        @pl.when((pl.program_id(0) == GATHER_FORWARD_STEP) & (pl.program_id(1) == 0))
        def _():
            _gather_two_level(blk_ref, all_ref, send_sems, recv_sems, local_sem, phase=1)

        xpad[0:8, :] = jnp.zeros((8, RNN_BW), F32)
        xpad[pl.ds(8, seq), :] = xp_ref[...]
        xb = _conv(xpad, cw_ref, seq) + vec_ref[0:1, :]
        xb_ref[...] = xb
        sp = _softplus(-vec_ref[3:4, :])
        _, i, a, mult, _ = _gates(xb, wrg_ref[...], wig_ref[...], vec_ref[1:2, :], vec_ref[2:3, :], sp)
        a_s[...] = a
        b_s[...] = mult * (i * xb)
        row = lax.broadcasted_iota(jnp.int32, (8, RNN_BW), 0)

        def group(g, h):
            r0 = pl.multiple_of(g * 8, 8)
            av = a_s[pl.ds(r0, 8), :]
            bv = b_s[pl.ds(r0, 8), :]
            for k in (1, 2, 4):
                m = row >= k
                bv = jnp.where(m, av * pltpu.roll(bv, k, axis=0) + bv, bv)
                av = jnp.where(m, av * pltpu.roll(av, k, axis=0), av)
            hs_ref[pl.ds(r0, 8), :] = av * h + bv
            return av[7:8, :] * h + bv[7:8, :]

        def groups_of(i, h):
            for u in range(SCAN_UNROLL):
                h = group(i * SCAN_UNROLL + u, h)
            return h

        lax.fori_loop(0, groups // SCAN_UNROLL, groups_of, jnp.zeros((1, RNN_BW), F32))
        gav = ga_ref[...]
        y_ref[...] = (hs_ref[...] * (gav * _sigmoid(gav))).astype(BF16)

        @pl.when(last)
        def _():
            _gather_two_level(blk_ref, all_ref, send_sems, recv_sems, local_sem, phase=2)

    sq = _seq_spec(seq)
    shape = (bsz, seq, D_RNN)
    return pl.pallas_call(
        body, grid=(RNN_BLOCKS, bsz),
        in_specs=[sq, sq, _chan_spec(8), _chan_spec(8), _GATE_W_SPEC, _GATE_W_SPEC, _ANY],
        out_specs=[sq, sq, sq, _ANY],
        out_shape=[jax.ShapeDtypeStruct(shape, F32), jax.ShapeDtypeStruct(shape, F32), jax.ShapeDtypeStruct(shape, BF16),
                   jax.ShapeDtypeStruct((N_DEV,) + block.shape, block.dtype)],
        scratch_shapes=[pltpu.VMEM((seq + 8, RNN_BW), F32), pltpu.VMEM((seq, RNN_BW), F32), pltpu.VMEM((seq, RNN_BW), F32)]
        + _EXCHANGE_SEMS,
        compiler_params=_params(("arbitrary", "arbitrary")), name=name)(xp, ga, cw, vecs, wrg, wig, block)


def _lru_bwd(dy, xp, xb, hs, ga, cw, vecs, wrg, wig, parts, *, name):
    bsz, seq, _ = xp.shape
    groups = seq // 8

    def body(dy_ref, xp_ref, xb_ref, hs_ref, ga_ref, cw_ref, vec_ref, wrg_ref, wig_ref,
             parts_ref, dxp_ref, dga_ref, dwrg_ref, dwig_ref, dvec_ref, land_ref, pad, a_s, d_s, lam_s,
             send_sems, recv_sems, local_sem):
        first, last = _first_last((RNN_BLOCKS, bsz))

        @pl.when(first)
        def _():
            _exchange(parts_ref, land_ref, send_sems, recv_sems, local_sem, finish=False)

        @pl.when(pl.program_id(1) == 0)
        def _():
            dwrg_ref[...] = jnp.zeros_like(dwrg_ref)
            dwig_ref[...] = jnp.zeros_like(dwig_ref)
            dvec_ref[...] = jnp.zeros_like(dvec_ref)

        xb = xb_ref[...]
        hs = hs_ref[...]
        gav = ga_ref[...]
        dy = dy_ref[...]
        sp = _softplus(-vec_ref[3:4, :])
        wrg = wrg_ref[...]
        wig = wig_ref[...]
        r, i, a, mult, inv_mult = _gates(xb, wrg, wig, vec_ref[1:2, :], vec_ref[2:3, :], sp)
        sg = _sigmoid(gav)
        dga_ref[...] = (dy * hs * (sg * (1.0 + gav * (1.0 - sg)))).astype(BF16)
        d_s[...] = dy * (gav * sg)

        pad[pl.ds(0, seq), :] = a
        pad[pl.ds(seq, 8), :] = jnp.zeros((8, RNN_BW), F32)
        a_s[...] = pad[pl.ds(1, seq), :]
        row = lax.broadcasted_iota(jnp.int32, (8, RNN_BW), 0)

        def group(g, nxt):
            r0 = pl.multiple_of((groups - 1 - g) * 8, 8)
            cv = a_s[pl.ds(r0, 8), :]
            bv = d_s[pl.ds(r0, 8), :]
            for k in (1, 2, 4):
                m = row < 8 - k
                bv = jnp.where(m, cv * pltpu.roll(bv, 8 - k, axis=0) + bv, bv)
                cv = jnp.where(m, cv * pltpu.roll(cv, 8 - k, axis=0), cv)
            lam_s[pl.ds(r0, 8), :] = cv * nxt + bv
            return cv[0:1, :] * nxt + bv[0:1, :]

        def groups_of(i, nxt):
            for u in range(SCAN_UNROLL):
                nxt = group(i * SCAN_UNROLL + u, nxt)
            return nxt

        lax.fori_loop(0, groups // SCAN_UNROLL, groups_of, jnp.zeros((1, RNN_BW), F32))
        dh = lam_s[...]

        pad[0:8, :] = jnp.zeros((8, RNN_BW), F32)
        pad[pl.ds(8, seq), :] = hs
        da = dh * pad[pl.ds(7, seq), :]
        ixb = i * xb
        dixb = dh * mult
        dla = da * a - (dh * ixb) * (a * a) * inv_mult
        drp = (dla * ((-LRU_C) * sp)) * r * (1.0 - r)
        dip = (dixb * xb) * i * (1.0 - i)
        dvec_ref[0:1, :] += jnp.sum(drp, axis=0, keepdims=True)
        dvec_ref[1:2, :] += jnp.sum(dip, axis=0, keepdims=True)
        dvec_ref[2:3, :] += jnp.sum(dla * ((-LRU_C) * r), axis=0, keepdims=True)
        drpb = drp.astype(BF16)
        dipb = dip.astype(BF16)
        xbb = xb.astype(BF16)
        nt = (((1,), (1,)), ((), ()))
        tn = (((0,), (0,)), ((), ()))
        dxb = (dixb * i
               + lax.dot_general(drpb, wrg, nt, preferred_element_type=F32)
               + lax.dot_general(dipb, wig, nt, preferred_element_type=F32))
        dwrg_ref[...] += lax.dot_general(xbb, drpb, tn, preferred_element_type=F32)
        dwig_ref[...] += lax.dot_general(xbb, dipb, tn, preferred_element_type=F32)
        dvec_ref[3:4, :] += jnp.sum(dxb, axis=0, keepdims=True)

        pad[pl.ds(0, seq), :] = dxb
        pad[pl.ds(seq, 8), :] = jnp.zeros((8, RNN_BW), F32)
        dxp = cw_ref[0:1, :] * pad[pl.ds(CONV_WIDTH - 1, seq), :]
        for k in range(1, CONV_WIDTH):
            dxp = dxp + cw_ref[k:k + 1, :] * pad[pl.ds(CONV_WIDTH - 1 - k, seq), :]
        dxp_ref[...] = dxp.astype(BF16)
        pad[0:8, :] = jnp.zeros((8, RNN_BW), F32)
        pad[pl.ds(8, seq), :] = xp_ref[...]
        for k in range(CONV_WIDTH):
            dvec_ref[4 + k:5 + k, :] += jnp.sum(dxb * pad[pl.ds(8 - (CONV_WIDTH - 1) + k, seq), :], axis=0, keepdims=True)

        @pl.when(last)
        def _():
            _exchange(parts_ref, land_ref, send_sems, recv_sems, local_sem, finish=True)

    sq = _seq_spec(seq)
    shape = (bsz, seq, D_RNN)
    gshape = (RNN_BLOCKS, RNN_BW, RNN_BW)
    return pl.pallas_call(
        body, grid=(RNN_BLOCKS, bsz),
        in_specs=[sq, sq, sq, sq, sq, _chan_spec(8), _chan_spec(8), _GATE_W_SPEC, _GATE_W_SPEC, _ANY],
        out_specs=[sq, sq, _GATE_W_SPEC, _GATE_W_SPEC, _chan_spec(8), _ANY],
        out_shape=[jax.ShapeDtypeStruct(shape, BF16), jax.ShapeDtypeStruct(shape, BF16),
                   jax.ShapeDtypeStruct(gshape, F32), jax.ShapeDtypeStruct(gshape, F32),
                   jax.ShapeDtypeStruct((8, D_RNN), F32), jax.ShapeDtypeStruct(parts.shape, parts.dtype)],
        scratch_shapes=[pltpu.VMEM((seq + 8, RNN_BW), F32), pltpu.VMEM((seq, RNN_BW), F32),
                        pltpu.VMEM((seq, RNN_BW), F32), pltpu.VMEM((seq, RNN_BW), F32)] + _EXCHANGE_SEMS,
        compiler_params=_params(("arbitrary", "arbitrary")), name=name)(dy, xp, xb, hs, ga, cw, vecs, wrg, wig, parts)


def _attn_block(seq):
    return min(512, seq)


def _diag_mask(blk):
    return lax.broadcasted_iota(jnp.int32, (blk, blk), 0) <= lax.broadcasted_iota(jnp.int32, (blk, blk), 1)


FWD_HEADS = 8
BWD_HEADS = 2


def _attn_fwd(q, kn, kr, v_t, block, *, bsz, seq, name):
    t = bsz * seq
    blk = _attn_block(seq)
    nq = seq // blk
    hg = FWD_HEADS
    steps = (bsz, N_HEADS // hg, nq)

    def body(q_ref, kn_ref, kr_ref, vt_ref, blk_ref, o_ref, lse_ref, all_ref, acc, send_sems, recv_sems, local_sem):
        first, last = _first_last(steps)

        @pl.when(first)
        def _():
            _exchange(blk_ref, all_ref, send_sems, recv_sems, local_sem, finish=False, gather=True)

        qi = pl.program_id(2)
        acc[...] = jnp.zeros_like(acc)

        def step(j, carry, diagonal):
            k0 = pl.multiple_of(j * blk, blk)
            kr_j = kr_ref[pl.ds(k0, blk), :]
            out = []
            for h in range(hg):
                m_i, l_i = carry[h]
                kv = jnp.concatenate([kn_ref[pl.ds(k0, blk), h * QK_NOPE:(h + 1) * QK_NOPE], kr_j], axis=1)
                qv = q_ref[:, h * HEAD_PAD:(h + 1) * HEAD_PAD]
                s = lax.dot_general(kv, qv, _NT, preferred_element_type=F32) * ATTN_SCALE
                if diagonal:
                    s = jnp.where(_diag_mask(blk), s, -jnp.inf)
                m_new = jnp.maximum(m_i, jnp.max(s, axis=0, keepdims=True))
                p = jnp.exp(s - m_new)
                alpha = jnp.exp(m_i - m_new)
                l_new = alpha * l_i + jnp.sum(p, axis=0, keepdims=True)
                acc[h] = alpha * acc[h] + jnp.dot(vt_ref[h * V_DIM:(h + 1) * V_DIM, pl.ds(k0, blk)], p.astype(BF16),
                                                  preferred_element_type=F32)
                out.append((m_new, l_new))
            return tuple(out)

        init = tuple((jnp.full((1, blk), -jnp.inf, F32), jnp.zeros((1, blk), F32)) for _ in range(hg))
        carry = lax.fori_loop(0, qi, lambda j, c: step(j, c, False), init)
        stats = step(qi, carry, True)
        for h in range(hg):
            m_i, l_i = stats[h]
            o_ref[:, h * V_DIM:(h + 1) * V_DIM] = (acc[h] / l_i).T
            lse_ref[h] = m_i + jnp.log(l_i)

        @pl.when(last)
        def _():
            _exchange(blk_ref, all_ref, send_sems, recv_sems, local_sem, finish=True, gather=True)

    return pl.pallas_call(
        body, grid=steps,
        in_specs=[pl.BlockSpec((blk, hg * HEAD_PAD), lambda b, g, i: (b * nq + i, g)),
                  pl.BlockSpec((seq, hg * QK_NOPE), lambda b, g, i: (b, g)),
                  pl.BlockSpec((seq, LANES), lambda b, g, i: (b, 0)),
                  pl.BlockSpec((hg * V_DIM, seq), lambda b, g, i: (g, b)), _ANY],
        out_specs=[pl.BlockSpec((blk, hg * V_DIM), lambda b, g, i: (b * nq + i, g)),
                   pl.BlockSpec((hg, 1, blk), lambda b, g, i: (g, 0, b * nq + i)), _ANY],
        out_shape=[jax.ShapeDtypeStruct((t, N_HEADS * V_DIM), F32), jax.ShapeDtypeStruct((N_HEADS, 1, t), F32),
                   jax.ShapeDtypeStruct((N_DEV,) + block.shape, block.dtype)],
        scratch_shapes=[pltpu.VMEM((hg, V_DIM, blk), F32)] + _EXCHANGE_SEMS,
        compiler_params=_params(("arbitrary", "arbitrary", "arbitrary")), name=name)(q, kn, kr, v_t, block)


def _attn_bwd(q, kn, kr, kn_t, kr_t, v, o, lse, do, cos, sin, parts, *, bsz, seq, name):
    t = bsz * seq
    blk = _attn_block(seq)
    nq = seq // blk
    hg = BWD_HEADS
    steps = (bsz, N_HEADS // hg)

    def body(q_ref, kn_ref, kr_ref, knt_ref, krt_ref, v_ref, o_ref, lse_ref, do_ref, cos_ref, sin_ref, parts_ref,
             dq_ref, dkn_ref, dkr_ref, dv_ref, land_ref, dqt_acc, dk_acc, dv_acc, send_sems, recv_sems, local_sem):
        first, last = _first_last(steps)

        @pl.when(first)
        def _():
            _exchange(parts_ref, land_ref, send_sems, recv_sems, local_sem, finish=False)

        dqt_acc[...] = jnp.zeros_like(dqt_acc)
        dk_acc[...] = jnp.zeros_like(dk_acc)
        dv_acc[...] = jnp.zeros_like(dv_acc)

        def q_block(i, _):
            q0 = pl.multiple_of(i * blk, blk)
            rows = []
            for h in range(hg):
                dov = do_ref[pl.ds(q0, blk), h * V_DIM:(h + 1) * V_DIM].astype(F32)
                dcol = jnp.sum(dov * o_ref[pl.ds(q0, blk), h * V_DIM:(h + 1) * V_DIM], axis=-1, keepdims=True)
                delta = jnp.broadcast_to(dcol, (blk, LANES)).T[0:1, :]
                rows.append((lse_ref[h, :, pl.ds(q0, blk)], delta))

            def pair(j, diagonal):
                k0 = pl.multiple_of(j * blk, blk)
                kr_j = kr_ref[pl.ds(k0, blk), :]
                krt_j = krt_ref[:, pl.ds(k0, blk)]
                for h in range(hg):
                    lse_i, delta = rows[h]
                    qv = q_ref[pl.ds(q0, blk), h * HEAD_PAD:(h + 1) * HEAD_PAD]
                    dov = do_ref[pl.ds(q0, blk), h * V_DIM:(h + 1) * V_DIM]
                    kv = jnp.concatenate([kn_ref[pl.ds(k0, blk), h * QK_NOPE:(h + 1) * QK_NOPE], kr_j], axis=1)
                    s = lax.dot_general(kv, qv, _NT, preferred_element_type=F32) * ATTN_SCALE
                    p = jnp.exp(s - lse_i)
                    if diagonal:
                        p = jnp.where(_diag_mask(blk), p, 0.0)
                    dv_acc[pl.ds(k0, blk), h * V_DIM:(h + 1) * V_DIM] += jnp.dot(
                        p.astype(BF16), dov, preferred_element_type=F32)
                    dp = lax.dot_general(v_ref[pl.ds(k0, blk), h * V_DIM:(h + 1) * V_DIM], dov, _NT,
                                         preferred_element_type=F32)
                    ds = (p * (dp - delta) * ATTN_SCALE).astype(BF16)
                    dk_acc[pl.ds(k0, blk), h * HEAD_PAD:(h + 1) * HEAD_PAD] += jnp.dot(ds, qv, preferred_element_type=F32)
                    base = h * HEAD_PAD
                    dqt_acc[base:base + QK_NOPE, pl.ds(q0, blk)] += jnp.dot(
                        knt_ref[h * QK_NOPE:(h + 1) * QK_NOPE, pl.ds(k0, blk)], ds, preferred_element_type=F32)
                    dqt_acc[base + QK_NOPE:base + HEAD_PAD, pl.ds(q0, blk)] += jnp.dot(
                        krt_j, ds, preferred_element_type=F32)

            def off_diagonal(j, _):
                pair(j, False)
                return 0

            lax.fori_loop(0, i, off_diagonal, 0)
            pair(i, True)
            return 0

        lax.fori_loop(0, nq, q_block, 0)
        dkr = jnp.zeros((seq, LANES), F32)
        for h in range(hg):
            base = h * HEAD_PAD
            for i in range(nq):
                rows = slice(i * blk, (i + 1) * blk)
                dq = dqt_acc[base:base + HEAD_PAD, rows].T
                dq_ref[rows, base:base + QK_NOPE] = dq[:, :QK_NOPE].astype(BF16)
                dq_ref[rows, base + QK_NOPE:base + HEAD_PAD] = _rope_t(
                    dq[:, QK_NOPE:], cos_ref[rows, :], sin_ref[rows, :]).astype(BF16)
            dkn_ref[:, h * QK_NOPE:(h + 1) * QK_NOPE] = dk_acc[:, base:base + QK_NOPE].astype(BF16)
            dkr = dkr + dk_acc[:, base + QK_NOPE:base + HEAD_PAD]
        dv_ref[...] = dv_acc[...].astype(BF16)

        @pl.when(pl.program_id(1) == 0)
        def _():
            dkr_ref[...] = jnp.zeros_like(dkr_ref)

        dkr_ref[...] += _rope_t(dkr, cos_ref[...], sin_ref[...])

        @pl.when(last)
        def _():
            _exchange(parts_ref, land_ref, send_sems, recv_sems, local_sem, finish=True)

    head = pl.BlockSpec((seq, hg * V_DIM), lambda b, g: (b, g))
    head_t = pl.BlockSpec((hg * V_DIM, seq), lambda b, g: (g, b))
    shared = pl.BlockSpec((seq, LANES), lambda b, g: (b, 0))
    shared_t = pl.BlockSpec((LANES, seq), lambda b, g: (0, b))
    table = pl.BlockSpec((seq, LANES), lambda b, g: (0, 0))
    qspec = pl.BlockSpec((seq, hg * HEAD_PAD), lambda b, g: (b, g))
    return pl.pallas_call(
        body, grid=steps,
        in_specs=[qspec, head, shared, head_t, shared_t, head, head,
                  pl.BlockSpec((hg, 1, seq), lambda b, g: (g, 0, b)), head, table, table, _ANY],
        out_specs=[qspec, head, shared, head, _ANY],
        out_shape=[jax.ShapeDtypeStruct((t, N_HEADS * HEAD_PAD), BF16), jax.ShapeDtypeStruct((t, N_HEADS * QK_NOPE), BF16),
                   jax.ShapeDtypeStruct((t, LANES), F32), jax.ShapeDtypeStruct((t, N_HEADS * V_DIM), BF16),
                   jax.ShapeDtypeStruct(parts.shape, parts.dtype)],
        scratch_shapes=[pltpu.VMEM((hg * HEAD_PAD, seq), F32), pltpu.VMEM((seq, hg * HEAD_PAD), F32),
                        pltpu.VMEM((seq, hg * V_DIM), F32)] + _EXCHANGE_SEMS,
        compiler_params=_params(("arbitrary", "arbitrary")), name=name)(
            q, kn, kr, kn_t, kr_t, v, o, lse, do, cos, sin, parts)


def _head_and_loss(o, g2, x1, target, w_out, g_final, *, name, bt=512):
    t, d = x1.shape
    bt = min(bt, t)
    nt = (((1,), (1,)), ((), ()))
    tn = (((0,), (0,)), ((), ()))

    def body(o_ref, g2_ref, x1_ref, tgt_ref, w_ref, gf_ref, loss_ref, dx2_ref, dw_ref, do_ref, dg2_ref, dgf_ref):
        @pl.when(pl.program_id(0) == 0)
        def _():
            for ref in (loss_ref, dgf_ref, dw_ref):
                ref[...] = jnp.zeros_like(ref)

        ov = o_ref[...]
        gv = g2_ref[...]
        sg = _sigmoid(gv)
        silu = gv * sg
        y2 = (ov * silu).astype(BF16)
        w = w_ref[...]
        x2 = x1_ref[...] + jnp.dot(y2, w, preferred_element_type=F32)
        r = lax.rsqrt(jnp.mean(x2 * x2, axis=-1, keepdims=True) + EPS)
        nrm = x2 * r
        gf = gf_ref[...]
        err = nrm * gf - tgt_ref[...]
        loss_ref[...] += 0.5 * jnp.sum(jnp.mean(err * err, axis=-1, keepdims=True))
        dyf = err * (1.0 / d)
        dgf_ref[...] += jnp.sum(dyf * nrm, axis=0, keepdims=True)
        dn = dyf * gf
        dx2 = r * (dn - nrm * jnp.mean(dn * nrm, axis=-1, keepdims=True))
        dx2_ref[...] = dx2
        dx2 = dx2.astype(BF16)
        dw_ref[...] += lax.dot_general(y2, dx2, tn, preferred_element_type=F32)
        dy2 = lax.dot_general(dx2, w, nt, preferred_element_type=F32)
        do_ref[...] = (dy2 * silu).astype(BF16)
        dg2_ref[...] = (dy2 * ov * (sg * (1.0 + gv * (1.0 - sg)))).astype(BF16)

    row = pl.BlockSpec((bt, d), lambda i: (i, 0))
    vec = pl.BlockSpec((1, d), lambda i: (0, 0))
    return pl.pallas_call(
        body, grid=(t // bt,),
        in_specs=[row, row, row, row, pl.BlockSpec((d, d), lambda i: (0, 0)), vec],
        out_specs=[pl.BlockSpec((8, LANES), lambda i: (0, 0)), row, pl.BlockSpec((d, d), lambda i: (0, 0)), row, row, vec],
        out_shape=[jax.ShapeDtypeStruct((8, LANES), F32), jax.ShapeDtypeStruct((t, d), F32),
                   jax.ShapeDtypeStruct((d, d), F32), jax.ShapeDtypeStruct((t, d), BF16),
                   jax.ShapeDtypeStruct((t, d), BF16), jax.ShapeDtypeStruct((1, d), F32)],
        compiler_params=_params(("arbitrary",)), name=name)(o, g2, x1, target, w_out, g_final)


def _sum_parts(parts, *, name, br=GRAD_BLOCK):
    npart, rows, w = parts.shape

    def body(p_ref, o_ref):
        acc = p_ref[0].astype(F32)
        for j in range(1, npart):
            acc = acc + p_ref[j].astype(F32)
        o_ref[...] = acc

    return pl.pallas_call(
        body, grid=(rows // br,), in_specs=[pl.BlockSpec((npart, br, w), lambda i: (0, i, 0))],
        out_specs=pl.BlockSpec((br, w), lambda i: (i, 0)), out_shape=jax.ShapeDtypeStruct((rows, w), F32),
        compiler_params=_params(("parallel",)), name=name)(parts)


def _chip_partial(parts, recv, *, name, br=GRAD_BLOCK):
    _, rows, w = parts.shape
    core = lax.axis_index("c").astype(jnp.int32).reshape(1)

    def body(c_ref, p_ref, r_ref, o_ref):
        o_ref[...] = (p_ref[...] + r_ref[...]).astype(BF16)

    grid_spec = pltpu.PrefetchScalarGridSpec(
        num_scalar_prefetch=1, grid=(4, rows // br),
        in_specs=[pl.BlockSpec((None, br, w), lambda k, i, c_ref: (2 * k + c_ref[0], i, 0)),
                  pl.BlockSpec((None, br, w), lambda k, i, c_ref: (k, i, 0))],
        out_specs=pl.BlockSpec((None, br, w), lambda k, i, c_ref: (k, i, 0)))
    return pl.pallas_call(
        body, grid_spec=grid_spec, out_shape=jax.ShapeDtypeStruct((4, rows, w), BF16),
        compiler_params=_params(("parallel", "parallel")), name=name)(core, parts, recv)


def _as_block(a):
    if a.ndim == 1:
        return a.reshape(1, -1)
    if a.ndim > 2 and a.shape[0] == 1:
        return a.reshape(a.shape[1:])
    return a


def _adamw(g, w, m, v, *, name):
    shape = w.shape
    g, w, m, v = (_as_block(a) for a in (g, w, m, v))

    def body(g_ref, w_ref, m_ref, v_ref, d_ref, nm_ref, nv_ref):
        gv = g_ref[...]
        nm = ADAM_B1 * m_ref[...] + (1.0 - ADAM_B1) * gv
        nv = ADAM_B2 * v_ref[...] + (1.0 - ADAM_B2) * (gv * gv)
        nm_ref[...] = nm
        nv_ref[...] = nv
        m_hat = nm / (1.0 - ADAM_B1 ** ADAM_STEP)
        v_hat = nv / (1.0 - ADAM_B2 ** ADAM_STEP)
        d_ref[...] = (-ADAM_LR) * (m_hat / (jnp.sqrt(v_hat) + ADAM_EPS) + ADAM_WD * w_ref[...])

    whole = pl.BlockSpec(memory_space=pltpu.VMEM)
    outs = pl.pallas_call(
        body, in_specs=[whole] * 4, out_specs=[whole] * 3, out_shape=[jax.ShapeDtypeStruct(w.shape, F32)] * 3,
        compiler_params=_params(), name=name)(g, w, m, v)
    return [o.reshape(shape) for o in outs]


def _all_gather(block, *, name):
    m, n = block.shape

    def body(x_ref, out_ref, send_sems, recv_sems, local_sem):
        for phase in range(3):
            _gather_two_level(x_ref, out_ref, send_sems, recv_sems, local_sem, phase=phase)

    return pl.pallas_call(
        body, out_shape=jax.ShapeDtypeStruct((N_DEV, m, n), block.dtype), in_specs=[_ANY], out_specs=_ANY,
        scratch_shapes=_EXCHANGE_SEMS, name=name)(block)


def _exchange_d2d(parts, *, name):
    _, rows, w = parts.shape

    def body(p_ref, land_ref, send_sems, recv_sems):
        x, y, c = _mesh_pos()
        sends = []
        for k in range(4):
            cp = pltpu.make_async_remote_copy(
                src_ref=p_ref.at[2 * k + (1 - c)], dst_ref=land_ref.at[k], send_sem=send_sems.at[k],
                recv_sem=recv_sems.at[k], device_id=(x, y, 1 - c), device_id_type=pl.DeviceIdType.MESH)
            cp.start()
            sends.append(cp)
        for cp in sends:
            cp.wait_recv()
        for cp in sends:
            cp.wait_send()

    return pl.pallas_call(
        body, out_shape=jax.ShapeDtypeStruct((4, rows, w), parts.dtype), in_specs=[_ANY], out_specs=_ANY,
        scratch_shapes=[pltpu.SemaphoreType.DMA((4,)), pltpu.SemaphoreType.DMA((4,))], name=name)(parts)


def _exchange_ici(parts, *, name):
    def body(p_ref, land_ref, send_sems, recv_sems, local_sem):
        x, y, c = _mesh_pos()
        mine = pltpu.make_async_copy(p_ref.at[2 * x + y], land_ref.at[3], local_sem)
        mine.start()
        sends = []
        for k, (px, py) in enumerate([(1 - x, y), (x, 1 - y), (1 - x, 1 - y)]):
            cp = pltpu.make_async_remote_copy(
                src_ref=p_ref.at[2 * px + py], dst_ref=land_ref.at[k], send_sem=send_sems.at[k],
                recv_sem=recv_sems.at[k], device_id=(px, py, c), device_id_type=pl.DeviceIdType.MESH)
            cp.start()
            sends.append(cp)
        for cp in sends:
            cp.wait_recv()
        for cp in sends:
            cp.wait_send()
        mine.wait()

    return pl.pallas_call(
        body, out_shape=jax.ShapeDtypeStruct(parts.shape, parts.dtype), in_specs=[_ANY], out_specs=_ANY,
        scratch_shapes=[pltpu.SemaphoreType.DMA((3,)), pltpu.SemaphoreType.DMA((3,)), pltpu.SemaphoreType.DMA(())],
        name=name)(parts)


def _rows(a):
    return a.reshape(-1, PACK_W)


def _pad_to(a, n):
    return jnp.pad(a, (0, n - a.shape[0]))


def _weight_blocks(d):
    small = _rows(_pad_to(jnp.concatenate([d[n].reshape(-1) for n, _ in _SMALL]), 16 * PACK_W))
    bits = lax.bitcast_convert_type(small, jnp.uint32)
    halves = [lax.bitcast_convert_type(h.astype(jnp.uint16), WIRE) for h in (bits >> 16, bits & 0xFFFF)]
    block_a = jnp.concatenate([d["w_in_a"][0].T.astype(WIRE)] + halves, axis=0)
    w_uq = jnp.pad(d["w_uq"][0], ((0, 0), (0, 0), (0, HEAD_PAD - QK_NOPE - QK_ROPE)))
    pieces = {"w_out_a": d["w_out_a"], "w_dkv": d["w_dkv"], "w_uk": d["w_uk"], "w_uv": d["w_uv"],
              "w_in_b": d["w_in_b"][0].T, "w_uq": w_uq}
    block_b = jnp.concatenate([_rows(pieces[n]) for n, _ in _PIECES_B]
                              + [jnp.zeros((WIRE_ROWS_B - MATRIX_ROWS_B, PACK_W), F32)], axis=0).astype(WIRE)
    return block_a, block_b, d["w_out_b"][0].astype(WIRE)


def _weights_a(wall):
    w = {}
    lo, hi = _OFF_A["w_in_a"]
    w["w_in_a_t"] = wall[:, lo:hi].reshape(2 * D_RNN, D_MODEL)
    high, low = (lax.bitcast_convert_type(wall[:, r:r + 16], jnp.uint16).astype(jnp.uint32)
                 for r in (MATRIX_ROWS_A, MATRIX_ROWS_A + 16))
    small = lax.bitcast_convert_type((high << 16) | low, F32)[:, :8].reshape(N_DEV, 8 * PACK_W)
    off = dict(zip([n for n, _ in _SMALL], [0, 128, 768, 928, 1088, 1248]))
    w["norm_a"] = small[:, :128].reshape(1, D_MODEL)

    def by_channel(lo, rows):
        a = small[:, lo:lo + rows * (D_RNN // N_DEV)].reshape(N_DEV, rows, -1).transpose(1, 0, 2).reshape(rows, D_RNN)
        return jnp.pad(a, ((0, 8 - rows), (0, 0)))

    w["conv_taps"] = by_channel(off["conv_w"], CONV_WIDTH)
    w["lru_vecs"] = by_channel(off["conv_b"], 4)
    return w


def _weights_b(wall):
    piece = {n: wall[:, lo:hi] for n, (lo, hi) in _OFF_B.items()}
    w = {"w_out_a": piece["w_out_a"].reshape(D_RNN, D_MODEL)}
    w_dkv = piece["w_dkv"].reshape(D_MODEL, KV_RANK + QK_ROPE)
    w["w_dkv_c"] = w_dkv[:, :KV_RANK]
    w["w_dkv_r"] = jnp.pad(w_dkv[:, KV_RANK:], ((0, 0), (0, LANES - QK_ROPE)))
    w["w_uk"] = piece["w_uk"].reshape(KV_RANK, N_HEADS * QK_NOPE)
    w["w_uv"] = piece["w_uv"].reshape(KV_RANK, N_HEADS * V_DIM)
    w["w_in_b_t"] = piece["w_in_b"].reshape(Q_RANK + N_HEADS * V_DIM, D_MODEL)
    w["w_uq"] = piece["w_uq"].reshape(Q_RANK, N_HEADS * HEAD_PAD)
    return w


def _pack_rep(d):
    flat = jnp.concatenate([d[n].reshape(-1) for n, _ in _REP])
    return _rows(_pad_to(flat, REP_ROWS * PACK_W))


def _unpack_rep(p, like):
    flat = p.reshape(-1)
    out, off = {}, 0
    for n, k in _REP:
        out[n] = flat[off:off + k].reshape(like[n].shape)
        off += k
    return out


def _by_owner(a):
    return a.reshape(N_DEV, -1, PACK_W)


def _grad_parts_b(g):
    tail = jnp.zeros((N_DEV, WIRE_ROWS_B - MATRIX_ROWS_B, PACK_W), F32)
    return jnp.concatenate([_by_owner(g[n]) for n, _ in _PIECES_B] + [tail], axis=1).astype(BF16)


def _grad_parts_a(g):
    small = jnp.concatenate([
        g["norm_a"].reshape(N_DEV, -1),
        g["conv_w"].reshape(CONV_WIDTH, N_DEV, -1).transpose(1, 0, 2).reshape(N_DEV, -1),
        g["conv_b"].reshape(N_DEV, -1), g["b_rg"].reshape(N_DEV, -1), g["b_ig"].reshape(N_DEV, -1),
        g["lru_lambda"].reshape(N_DEV, -1)], axis=1)
    small = jnp.pad(small, ((0, 0), (0, 8 * PACK_W - small.shape[1]))).reshape(N_DEV, 8, PACK_W)
    half = N_DEV // 2
    w_in_a = jnp.concatenate([h.reshape(half, -1, PACK_W) for h in g["w_in_a_t"]], axis=0)
    rep = _pack_rep(g).reshape(N_DEV, REP_SLICE, PACK_W)
    tail = jnp.zeros((N_DEV, GRAD_ROWS_A - MATRIX_ROWS_A - 8 - REP_SLICE, PACK_W), F32)
    return jnp.concatenate([w_in_a, small, rep, tail], axis=1)


def _own_grads(sum_a, sum_b, sum_c):
    out = {}
    lo, hi = _OFF_A["w_in_a"]
    out["w_in_a"] = sum_a[lo:hi].T.reshape(1, D_MODEL, 2 * D_RNN // N_DEV)
    small = sum_a[MATRIX_ROWS_A:MATRIX_ROWS_A + 8].reshape(-1)
    shapes = {"norm_a": (1, D_MODEL // N_DEV), "conv_w": (1, CONV_WIDTH, D_RNN // N_DEV), "conv_b": (1, D_RNN // N_DEV),
              "b_rg": (1, D_RNN // N_DEV), "b_ig": (1, D_RNN // N_DEV), "lru_lambda": (1, D_RNN // N_DEV)}
    off = 0
    for n, k in _SMALL:
        out[n] = small[off:off + k].reshape(shapes[n])
        off += k
    piece = {n: sum_b[lo:hi] for n, (lo, hi) in _OFF_B.items()}
    out["w_out_a"] = piece["w_out_a"].reshape(1, D_RNN // N_DEV, D_MODEL)
    out["w_dkv"] = piece["w_dkv"].reshape(D_MODEL // N_DEV, KV_RANK + QK_ROPE)
    out["w_uk"] = piece["w_uk"].reshape(KV_RANK // N_DEV, N_HEADS, QK_NOPE)
    out["w_uv"] = piece["w_uv"].reshape(KV_RANK // N_DEV, N_HEADS, V_DIM)
    out["w_in_b"] = piece["w_in_b"].T.reshape(1, D_MODEL, (Q_RANK + N_HEADS * V_DIM) // N_DEV)
    out["w_uq"] = piece["w_uq"].reshape(1, Q_RANK // N_DEV, N_HEADS, HEAD_PAD)[..., :QK_NOPE + QK_ROPE]
    out["w_out_b"] = sum_c.reshape(1, N_HEADS * V_DIM // N_DEV, D_MODEL)
    return out


def _step(x, target, w, rep, block_b, block_c, *, bsz, seq):
    t = bsz * seq
    cos, sin = _rope_tables(seq)
    g_a = w["norm_a"]
    g_kv = rep["norm_kv"].reshape(1, -1)
    g_kvn = rep["kv_norm"].reshape(1, -1)
    g_b = rep["norm_b"].reshape(1, -1)
    g_q = rep["q_norm"].reshape(1, -1)
    g_f = rep["final_norm"].reshape(1, -1)
    wrg = rep["w_rg"][0].astype(BF16)
    wig = rep["w_ig"][0].astype(BF16)
    cw8, vecs = w["conv_taps"], w["lru_vecs"]

    def seq3(a):
        return a.reshape(bsz, seq, a.shape[-1])

    def flat(a):
        return a.reshape(t, a.shape[-1])

    h0, xp, ga = _lru_proj_fwd(x, g_a, w["w_in_a_t"], name="lru_proj_fwd")
    xb, hs, y, wall_b = _lru_fwd(seq3(xp), seq3(ga), cw8, vecs, wrg, wig, block_b, name="lru_fwd")
    w = dict(w, **_weights_b(wall_b))
    x1 = _matmul(flat(y), w["w_out_a"], residual=x, name="out_a")
    hk, hq, ck, cqp, g2, ckv, cq, q, kn, v, kr, kn_t, v_t, kr_t = _mla_proj_fwd(
        x1, (g_kv, g_b, g_kvn, g_q), w, cos, sin, seq=seq, name="mla_proj_fwd")
    o, lse, wall_c = _attn_fwd(q, kn, kr, v_t, block_c, bsz=bsz, seq=seq, name="attn_fwd")
    w_out_b = wall_c.reshape(N_HEADS * V_DIM, D_MODEL)
    loss, dx2, d_w_out_b, do, dg2, dgf = _head_and_loss(o, g2, x1, target, w_out_b, g_f, name="head_loss")
    grads = {"final_norm": dgf}
    parts_c = _by_owner(d_w_out_b).astype(BF16)
    dq, dkn, dkr, dv, landed_c = _attn_bwd(q, kn, kr, kn_t, kr_t, v, o, lse, do, cos, sin, parts_c,
                                           bsz=bsz, seq=seq, name="attn_bwd")
    dx1, du2, dckr, dgkv, dgb, dgkvn, dgq, dy = _mla_proj_bwd(
        x1, dx2, cqp, ck, dq, dkn, dv, dkr, dg2, (g_kv, g_b, g_kvn, g_q), w, name="mla_proj_bwd")
    grads["norm_kv"], grads["norm_b"], grads["kv_norm"], grads["q_norm"] = dgkv, dgb, dgkvn, dgq
    grads["w_uq"], grads["w_uk"], grads["w_uv"], d_w_dkv = _token_sums(
        [cq, dq, ckv, dkn, dv, hk, dckr], [(0, 1), (2, 3), (2, 4), (5, 6)], name="d_w_uq_uk_uv_dkv")
    grads["w_dkv"] = d_w_dkv[:, :KV_RANK + QK_ROPE]
    grads["w_in_b"], grads["w_out_a"] = _token_sums(
        [du2, hq, flat(y), dx1], [(0, 1), (2, 3)], name="d_w_in_b_t_out_a")
    parts_b = _grad_parts_b(grads)
    dxp, dga, dwrg, dwig, dvec, landed_b = _lru_bwd(
        seq3(dy), seq3(xp), xb, hs, seq3(ga), cw8, vecs, wrg, wig, parts_b, name="lru_bwd")
    dxp, dga = flat(dxp), flat(dga)
    grads["w_rg"], grads["w_ig"] = dwrg, dwig
    grads["b_rg"], grads["b_ig"], grads["conv_b"] = dvec[0], dvec[1], dvec[3]
    lam = vecs[3]
    grads["lru_lambda"] = dvec[2] * (-1.0 / (1.0 + jnp.exp(lam)))
    grads["conv_w"] = dvec[4:4 + CONV_WIDTH]
    dx, dga_norm, dwx, dwg = _lru_proj_bwd(dxp, dga, x, dx1, h0, g_a, w["w_in_a_t"], name="lru_proj_bwd")
    grads["norm_a"] = dga_norm
    grads["w_in_a_t"] = (dwx, dwg)
    return loss[0, 0], dx, grads, landed_b, landed_c


def kernel(x, norm_a, w_in_a, conv_w, conv_b, w_rg, b_rg, w_ig, b_ig, lru_lambda, w_out_a, norm_kv, w_dkv, kv_norm, w_uk, w_uv, norm_b, w_in_b, q_norm, w_uq, w_out_b, final_norm, loss_target, m_norm_a, m_w_in_a, m_conv_w, m_conv_b, m_w_rg, m_b_rg, m_w_ig, m_b_ig, m_lru_lambda, m_w_out_a, m_norm_kv, m_w_dkv, m_kv_norm, m_w_uk, m_w_uv, m_norm_b, m_w_in_b, m_q_norm, m_w_uq, m_w_out_b, m_final_norm, v_norm_a, v_w_in_a, v_conv_w, v_conv_b, v_w_rg, v_b_rg, v_w_ig, v_b_ig, v_lru_lambda, v_w_out_a, v_norm_kv, v_w_dkv, v_kv_norm, v_w_uk, v_w_uv, v_norm_b, v_w_in_b, v_q_norm, v_w_uq, v_w_out_b, v_final_norm):
    given = dict(locals())
    wts = {n: given[n] for n in WEIGHTS}
    mom1 = {n: given["m_" + n] for n in WEIGHTS}
    mom2 = {n: given["v_" + n] for n in WEIGHTS}
    bsz, seq, _ = x.shape
    t = bsz * seq

    block_a, block_b, block_c = _weight_blocks(wts)
    w = _weights_a(_all_gather(block_a, name="gather_weights_a"))
    loss, dx, grads, landed_b, landed_c = _step(x.reshape(t, D_MODEL), loss_target.reshape(t, D_MODEL), w, wts,
                                                block_b, block_c, bsz=bsz, seq=seq)

    parts_a = _grad_parts_a(grads)
    from_sibling = _exchange_d2d(parts_a, name="exchange_grads_d2d")
    chip_parts = _chip_partial(parts_a, from_sibling, name="chip_partial_grads")
    landed_a = _exchange_ici(chip_parts, name="exchange_grads_ici")
    sum_a = _sum_parts(landed_a, name="sum_grads_a", br=GRAD_BLOCK)
    sum_b = _sum_parts(landed_b, name="sum_grads_b", br=WIRE_ROWS_B // 2)
    sum_c = _sum_parts(landed_c, name="sum_grads_c", br=landed_c.shape[1])
    g_own = _own_grads(sum_a, sum_b, sum_c)
    rep_slice = sum_a[MATRIX_ROWS_A + 8:MATRIX_ROWS_A + 8 + REP_SLICE]
    loss_rows = jnp.pad(loss.reshape(1, 1), ((0, 7), (0, PACK_W - 1)))
    gathered = _all_gather(jnp.concatenate([rep_slice, loss_rows], axis=0), name="gather_replicated")
    g_own.update(_unpack_rep(gathered[:, :REP_SLICE].reshape(REP_ROWS, PACK_W), wts))
    loss = jnp.sum(gathered[:, REP_SLICE, 0])

    deltas, new_m, new_v = {}, {}, {}
    for n in WEIGHTS:
        deltas[n], new_m[n], new_v[n] = _adamw(g_own[n], wts[n], mom1[n], mom2[n], name="adamw_" + n)
    result = [loss, dx.reshape(bsz, seq, D_MODEL)]
    for d in (g_own, deltas, new_m, new_v):
        result.extend(d[n] for n in WEIGHTS)
    return tuple(result)
```

```python
import jax
import jax.numpy as jnp
from jax import lax
from jax.experimental import pallas as pl
from jax.experimental.pallas import tpu as pltpu

F32 = jnp.float32
BF16 = jnp.bfloat16
WIRE = jnp.bfloat16

D_MODEL = 1024
D_RNN = 1280
RNN_BLOCKS = 10
RNN_BW = 128
CONV_WIDTH = 4
LRU_C = 8.0
N_HEADS = 8
QK_NOPE = 128
QK_ROPE = 64
V_DIM = 128
KV_RANK = 256
Q_RANK = 384
ROPE_THETA = 10000.0
EPS = 1e-6
ATTN_SCALE = (QK_NOPE + QK_ROPE) ** -0.5
HEAD_PAD = 256
LANES = 128

ADAM_LR = 0.001
ADAM_B1 = 0.9
ADAM_B2 = 0.999
ADAM_EPS = 1e-08
ADAM_WD = 0.01
ADAM_STEP = 10

N_DEV = 8
VMEM_LIMIT_BYTES = 56 * 2**20
PACK_W = 1024

_PIECES_A = (("w_in_a", 320),)
_PIECES_B = (("w_out_a", 160), ("w_dkv", 40), ("w_uk", 32), ("w_uv", 32), ("w_in_b", 176), ("w_uq", 96))


def _offsets(pieces):
    off, r = {}, 0
    for n, k in pieces:
        off[n] = (r, r + k)
        r += k
    return off, r


_OFF_A, MATRIX_ROWS_A = _offsets(_PIECES_A)
_OFF_B, MATRIX_ROWS_B = _offsets(_PIECES_B)
WIRE_ROWS_A = MATRIX_ROWS_A + 32
WIRE_ROWS_B = 544
_SMALL = (("norm_a", 128), ("conv_w", 640), ("conv_b", 160), ("b_rg", 160), ("b_ig", 160), ("lru_lambda", 160))
_REP = (("w_rg", 163840), ("w_ig", 163840), ("norm_kv", 1024), ("kv_norm", 256), ("norm_b", 1024),
        ("q_norm", 384), ("final_norm", 1024))
REP_ROWS = 384
REP_SLICE = REP_ROWS // N_DEV
GRAD_ROWS_A = 384
GRAD_BLOCK = 192

WEIGHTS = ("norm_a", "w_in_a", "conv_w", "conv_b", "w_rg", "b_rg", "w_ig", "b_ig", "lru_lambda", "w_out_a",
           "norm_kv", "w_dkv", "kv_norm", "w_uk", "w_uv", "norm_b", "w_in_b", "q_norm", "w_uq", "w_out_b",
           "final_norm")


def _params(sem=None):
    return pltpu.CompilerParams(dimension_semantics=sem, vmem_limit_bytes=VMEM_LIMIT_BYTES)


_NT = (((1,), (1,)), ((), ()))
_ANY = pl.BlockSpec(memory_space=pl.ANY)


def _mesh_pos():
    return lax.axis_index("x"), lax.axis_index("y"), lax.axis_index("c")


def _sigmoid(z):
    return 0.5 * jnp.tanh(0.5 * z) + 0.5


def _sigmoid_tail(z):
    return 1.0 / (1.0 + jnp.exp(-z))


def _col_block(n):
    return n if n <= 1408 else n // 2


def _matmul(a, b, *, name, nt=False, out_dtype=F32, residual=None, bm=1024):
    m, k = a.shape
    n = b.shape[0] if nt else b.shape[1]
    bm = min(bm, m)
    bn = _col_block(n)
    dims = (((1,), (1,)), ((), ())) if nt else (((1,), (0,)), ((), ()))
    has_res = residual is not None

    def body(*refs):
        a_ref, b_ref, o_ref = refs[0], refs[1], refs[-1]
        acc = lax.dot_general(a_ref[...].astype(BF16), b_ref[...].astype(BF16), dims, preferred_element_type=F32)
        if has_res:
            acc = acc + refs[2][...]
        o_ref[...] = acc.astype(out_dtype)

    in_specs = [pl.BlockSpec((bm, k), lambda i, j: (i, 0)),
                pl.BlockSpec((bn, k), lambda i, j: (j, 0)) if nt else pl.BlockSpec((k, bn), lambda i, j: (0, j))]
    args = [a, b]
    if has_res:
        in_specs.append(pl.BlockSpec((bm, bn), lambda i, j: (i, j)))
        args.append(residual)
    return pl.pallas_call(
        body, grid=(m // bm, n // bn), in_specs=in_specs, out_specs=pl.BlockSpec((bm, bn), lambda i, j: (i, j)),
        out_shape=jax.ShapeDtypeStruct((m, n), out_dtype), compiler_params=_params(("parallel", "parallel")),
        name=name)(*args)


def _token_sums(operands, pairs, *, name, bt=1024):
    t = operands[0].shape[0]
    bt = min(bt, t)
    k = len(operands)

    def body(*refs):
        ins, outs = refs[:k], refs[k:]

        @pl.when(pl.program_id(0) == 0)
        def _():
            for o_ref in outs:
                o_ref[...] = jnp.zeros_like(o_ref)

        vals = [r[...].astype(BF16) for r in ins]
        for (i, j), o_ref in zip(pairs, outs):
            o_ref[...] += lax.dot_general(vals[i], vals[j], (((0,), (0,)), ((), ())), preferred_element_type=F32)

    shapes = [(operands[i].shape[1], operands[j].shape[1]) for i, j in pairs]
    return pl.pallas_call(
        body, grid=(t // bt,),
        in_specs=[pl.BlockSpec((bt, a.shape[1]), lambda s: (s, 0)) for a in operands],
        out_specs=[pl.BlockSpec(shape, lambda s: (0, 0)) for shape in shapes],
        out_shape=[jax.ShapeDtypeStruct(shape, F32) for shape in shapes],
        compiler_params=_params(("arbitrary",)), name=name)(*operands)


def _swap_halves(v):
    ax = v.ndim - 1
    lane = lax.broadcasted_iota(jnp.int32, v.shape, ax)
    up = pltpu.roll(v, LANES - QK_ROPE // 2, axis=ax)
    down = pltpu.roll(v, QK_ROPE // 2, axis=ax)
    return jnp.where(lane < QK_ROPE // 2, up, jnp.where(lane < QK_ROPE, down, 0.0))


def _rope(v, cos, sin):
    return v * cos + _swap_halves(v) * sin


def _rope_t(d, cos, sin):
    return d * cos + _swap_halves(d * sin)


def _rope_tables(seq):
    pos = jnp.arange(seq, dtype=F32)
    inv = ROPE_THETA ** (-jnp.arange(0, QK_ROPE, 2, dtype=F32) / QK_ROPE)
    ang = pos[:, None] * inv[None, :]
    cos, sin = jnp.cos(ang), jnp.sin(ang)
    zero = jnp.zeros((seq, LANES - QK_ROPE), F32)
    return jnp.concatenate([cos, cos, zero], axis=1), jnp.concatenate([-sin, sin, zero], axis=1)


def _rms(v):
    return v * lax.rsqrt(jnp.mean(v * v, axis=-1, keepdims=True) + EPS)


def _const_spec(a):
    return pl.BlockSpec(a.shape, lambda i: (0,) * a.ndim)


def _lru_proj_fwd(x, g_a, w_in_t, *, name, bt=512):
    t, d = x.shape
    bt = min(bt, t)
    n = w_in_t.shape[0] // 2

    def body(x_ref, g_ref, wt_ref, h_ref, xp_ref, ga_ref):
        h = (_rms(x_ref[...]) * g_ref[...]).astype(BF16)
        h_ref[...] = h
        xp_ref[...] = lax.dot_general(h, wt_ref[0:n, :], _NT, preferred_element_type=F32)
        ga_ref[...] = lax.dot_general(h, wt_ref[n:2 * n, :], _NT, preferred_element_type=F32)

    row = lambda w: pl.BlockSpec((bt, w), lambda i: (i, 0))
    return pl.pallas_call(
        body, grid=(t // bt,), in_specs=[row(d), _const_spec(g_a), _const_spec(w_in_t)],
        out_specs=[row(d), row(n), row(n)],
        out_shape=[jax.ShapeDtypeStruct((t, d), BF16), jax.ShapeDtypeStruct((t, n), F32), jax.ShapeDtypeStruct((t, n), F32)],
        compiler_params=_params(("parallel",)), name=name)(x, g_a, w_in_t)


def _mla_proj_fwd(x1, gains, w, cos, sin, *, seq, name, bt=512):
    t, d = x1.shape
    bt = min(bt, seq)
    per_seq = seq // bt
    g_kv, g_b, g_kvn, g_q = gains
    consts = [g_kv, g_b, g_kvn, g_q, w["w_dkv_c"], w["w_dkv_r"], w["w_in_b_t"], w["w_uk"], w["w_uv"], w["w_uq"]]

    def body(x_ref, cos_ref, sin_ref, gkv_ref, gb_ref, gkvn_ref, gq_ref, wdc_ref, wdr_ref, wbt_ref,
             wuk_ref, wuv_ref, wuq_ref,
             hk_ref, hq_ref, ck_ref, cqp_ref, g2_ref, ckv_ref, cq_ref, q_ref, kn_ref, v_ref, kr_ref, knt_ref, vt_ref, krt_ref):
        nrm = _rms(x_ref[...])
        hk = (nrm * gkv_ref[...]).astype(BF16)
        hq = (nrm * gb_ref[...]).astype(BF16)
        hk_ref[...] = hk
        hq_ref[...] = hq
        ck = jnp.dot(hk, wdc_ref[...], preferred_element_type=F32)
        ck_ref[...] = ck
        cqp = lax.dot_general(hq, wbt_ref[0:Q_RANK, :], _NT, preferred_element_type=F32)
        cqp_ref[...] = cqp
        g2_ref[...] = lax.dot_general(hq, wbt_ref[Q_RANK:, :], _NT, preferred_element_type=F32)
        cosv, sinv = cos_ref[...], sin_ref[...]
        kr = _rope(jnp.dot(hk, wdr_ref[...], preferred_element_type=F32), cosv, sinv)
        kr_ref[...] = kr.astype(BF16)
        krt_ref[...] = kr.T.astype(BF16)
        ckv = (_rms(ck) * gkvn_ref[...]).astype(BF16)
        ckv_ref[...] = ckv
        kn = jnp.dot(ckv, wuk_ref[...], preferred_element_type=F32)
        v = jnp.dot(ckv, wuv_ref[...], preferred_element_type=F32)
        kn_ref[...] = kn.astype(BF16)
        v_ref[...] = v.astype(BF16)
        knt_ref[...] = kn.T.astype(BF16)
        vt_ref[...] = v.T.astype(BF16)
        cq = (_rms(cqp) * gq_ref[...]).astype(BF16)
        cq_ref[...] = cq
        for h in range(N_HEADS):
            qh = jnp.dot(cq, wuq_ref[:, h * HEAD_PAD:(h + 1) * HEAD_PAD], preferred_element_type=F32)
            q_ref[:, h * HEAD_PAD:h * HEAD_PAD + QK_NOPE] = qh[:, :QK_NOPE].astype(BF16)
            q_ref[:, h * HEAD_PAD + QK_NOPE:(h + 1) * HEAD_PAD] = _rope(qh[:, QK_NOPE:], cosv, sinv).astype(BF16)

    row = lambda w_: pl.BlockSpec((bt, w_), lambda i: (i, 0))
    col = lambda h_: pl.BlockSpec((h_, bt), lambda i: (0, i))
    tab = pl.BlockSpec((bt, LANES), lambda i: (i % per_seq, 0))
    nh = N_HEADS * V_DIM
    shapes = [((t, d), BF16), ((t, d), BF16), ((t, KV_RANK), F32), ((t, Q_RANK), F32), ((t, nh), F32), ((t, KV_RANK), BF16),
              ((t, Q_RANK), BF16), ((t, N_HEADS * HEAD_PAD), BF16), ((t, nh), BF16), ((t, nh), BF16), ((t, LANES), BF16),
              ((nh, t), BF16), ((nh, t), BF16), ((LANES, t), BF16)]
    out_specs = [row(d), row(d), row(KV_RANK), row(Q_RANK), row(nh), row(KV_RANK), row(Q_RANK), row(N_HEADS * HEAD_PAD),
                 row(nh), row(nh), row(LANES), col(nh), col(nh), col(LANES)]
    return pl.pallas_call(
        body, grid=(t // bt,), in_specs=[row(d), tab, tab] + [_const_spec(a) for a in consts], out_specs=out_specs,
        out_shape=[jax.ShapeDtypeStruct(s, dt) for s, dt in shapes],
        compiler_params=_params(("parallel",)), name=name)(x1, cos, sin, *consts)


def _rms_bwd_rows(xv, dn):
    r = lax.rsqrt(jnp.mean(xv * xv, axis=-1, keepdims=True) + EPS)
    nrm = xv * r
    return r * (dn - nrm * jnp.mean(dn * nrm, axis=-1, keepdims=True)), nrm


def _col_sum(v):
    return jnp.sum(v, axis=0, keepdims=True)


def _lru_proj_bwd(dxp, dga, x, dx1, h0, g_a, w_in_t, *, name, bt=512):
    t, d = x.shape
    bt = min(bt, t)
    n = w_in_t.shape[0] // 2
    tn = (((0,), (0,)), ((), ()))

    def body(dxp_ref, dga_ref, x_ref, dx1_ref, h0_ref, g_ref, wt_ref, dx_ref, dg_ref, dwx_ref, dwg_ref):
        @pl.when(pl.program_id(0) == 0)
        def _():
            for ref in (dg_ref, dwx_ref, dwg_ref):
                ref[...] = jnp.zeros_like(ref)

        dxp_v, dga_v, h0 = dxp_ref[...], dga_ref[...], h0_ref[...]
        dwx_ref[...] += lax.dot_general(dxp_v, h0, tn, preferred_element_type=F32)
        dwg_ref[...] += lax.dot_general(dga_v, h0, tn, preferred_element_type=F32)
        dh = (jnp.dot(dxp_v, wt_ref[0:n, :], preferred_element_type=F32)
              + jnp.dot(dga_v, wt_ref[n:2 * n, :], preferred_element_type=F32))
        dxn, nrm = _rms_bwd_rows(x_ref[...], dh * g_ref[...])
        dg_ref[...] += _col_sum(dh * nrm)
        dx_ref[...] = dx1_ref[...] + dxn

    row = lambda w: pl.BlockSpec((bt, w), lambda i: (i, 0))
    whole = pl.BlockSpec((n, d), lambda i: (0, 0))
    return pl.pallas_call(
        body, grid=(t // bt,),
        in_specs=[row(n), row(n), row(d), row(d), row(d), _const_spec(g_a), _const_spec(w_in_t)],
        out_specs=[row(d), _const_spec(g_a), whole, whole],
        out_shape=[jax.ShapeDtypeStruct((t, d), F32), jax.ShapeDtypeStruct((1, d), F32),
                   jax.ShapeDtypeStruct((n, d), F32), jax.ShapeDtypeStruct((n, d), F32)],
        compiler_params=_params(("arbitrary",)), name=name)(dxp, dga, x, dx1, h0, g_a, w_in_t)


def _mla_proj_bwd(x1, dx2, cqp, ck, dq, dkn, dv, dkr, dg2, gains, w, *, name, bt=512):
    t, d = x1.shape
    bt = min(bt, t)
    g_kv, g_b, g_kvn, g_q = gains
    consts = [g_kv, g_b, g_kvn, g_q, w["w_dkv_c"], w["w_dkv_r"], w["w_in_b_t"], w["w_uk"], w["w_uv"], w["w_uq"],
              w["w_out_a"]]
    nh = N_HEADS * V_DIM

    def body(x1_ref, dx2_ref, cqp_ref, ck_ref, dq_ref, dkn_ref, dv_ref, dkr_ref, dg2_ref,
             gkv_ref, gb_ref, gkvn_ref, gq_ref, wdc_ref, wdr_ref, wbt_ref, wuk_ref, wuv_ref, wuq_ref, wo_ref,
             dx1_ref, du2_ref, dckr_ref, dgkv_ref, dgb_ref, dgkvn_ref, dgq_ref, dy_ref):
        @pl.when(pl.program_id(0) == 0)
        def _():
            for ref in (dgkv_ref, dgb_ref, dgkvn_ref, dgq_ref):
                ref[...] = jnp.zeros_like(ref)

        dot_nt = lambda a, b: lax.dot_general(a, b, _NT, preferred_element_type=F32)
        dcq = dot_nt(dq_ref[...], wuq_ref[...])
        dcqp, nq = _rms_bwd_rows(cqp_ref[...], dcq * gq_ref[...])
        dgq_ref[...] += _col_sum(dcq * nq)
        dcqp = dcqp.astype(BF16)
        dg2 = dg2_ref[...]
        du2_ref[:, :Q_RANK] = dcqp
        du2_ref[:, Q_RANK:] = dg2
        dhq = (jnp.dot(dcqp, wbt_ref[0:Q_RANK, :], preferred_element_type=F32)
               + jnp.dot(dg2, wbt_ref[Q_RANK:, :], preferred_element_type=F32))
        dckv = dot_nt(dkn_ref[...], wuk_ref[...]) + dot_nt(dv_ref[...], wuv_ref[...])
        dck, nc = _rms_bwd_rows(ck_ref[...], dckv * gkvn_ref[...])
        dgkvn_ref[...] += _col_sum(dckv * nc)
        dck = dck.astype(BF16)
        dkr = dkr_ref[...].astype(BF16)
        dckr_ref[:, :KV_RANK] = dck
        dckr_ref[:, KV_RANK:] = dkr
        dhk = dot_nt(dck, wdc_ref[...]) + dot_nt(dkr, wdr_ref[...])
        dxn, n1 = _rms_bwd_rows(x1_ref[...], dhq * gb_ref[...] + dhk * gkv_ref[...])
        dgb_ref[...] += _col_sum(dhq * n1)
        dgkv_ref[...] += _col_sum(dhk * n1)
        dx1 = dx2_ref[...] + dxn
        dx1_ref[...] = dx1
        dy_ref[...] = lax.dot_general(dx1.astype(BF16), wo_ref[...], _NT, preferred_element_type=F32)

    row = lambda w_: pl.BlockSpec((bt, w_), lambda i: (i, 0))
    vec = lambda w_: pl.BlockSpec((1, w_), lambda i: (0, 0))
    in_specs = [row(d), row(d), row(Q_RANK), row(KV_RANK), row(N_HEADS * HEAD_PAD), row(nh), row(nh), row(LANES), row(nh)]
    return pl.pallas_call(
        body, grid=(t // bt,), in_specs=in_specs + [_const_spec(a) for a in consts],
        out_specs=[row(d), row(Q_RANK + nh), row(KV_RANK + LANES), vec(d), vec(d), vec(KV_RANK), vec(Q_RANK), row(D_RNN)],
        out_shape=[jax.ShapeDtypeStruct((t, d), F32), jax.ShapeDtypeStruct((t, Q_RANK + nh), BF16),
                   jax.ShapeDtypeStruct((t, KV_RANK + LANES), BF16), jax.ShapeDtypeStruct((1, d), F32),
                   jax.ShapeDtypeStruct((1, d), F32), jax.ShapeDtypeStruct((1, KV_RANK), F32),
                   jax.ShapeDtypeStruct((1, Q_RANK), F32), jax.ShapeDtypeStruct((t, D_RNN), F32)],
        compiler_params=_params(("arbitrary",)), name=name)(x1, dx2, cqp, ck, dq, dkn, dv, dkr, dg2, *consts)


def _softplus(z):
    return jnp.maximum(z, 0.0) + jnp.log1p(jnp.exp(-jnp.abs(z)))


def _one_minus_square(a, la):
    return jnp.tanh(-la) * (1.0 + a * a)


def _gates(xb, wrg, wig, brg, big, sp):
    xbb = xb.astype(BF16)
    r = _sigmoid_tail(jnp.dot(xbb, wrg, preferred_element_type=F32) + brg)
    i = _sigmoid(jnp.dot(xbb, wig, preferred_element_type=F32) + big)
    la = (-LRU_C) * r * sp
    a = jnp.exp(la)
    em = _one_minus_square(a, la)
    inv_mult = lax.rsqrt(em)
    mult = jnp.where(em > 0.0, em * inv_mult, 0.0)
    return r, i, a, mult, inv_mult


def _conv(xpad_ref, cw_ref, seq):
    acc = cw_ref[0:1, :] * xpad_ref[pl.ds(8 - (CONV_WIDTH - 1), seq), :]
    for k in range(1, CONV_WIDTH):
        acc = acc + cw_ref[k:k + 1, :] * xpad_ref[pl.ds(8 - (CONV_WIDTH - 1) + k, seq), :]
    return acc


def _seq_spec(seq):
    return pl.BlockSpec((None, seq, RNN_BW), lambda n, b: (b, 0, n))


def _chan_spec(rows):
    return pl.BlockSpec((rows, RNN_BW), lambda n, b: (0, n))


_GATE_W_SPEC = pl.BlockSpec((None, RNN_BW, RNN_BW), lambda n, b: (n, 0, 0))


SCAN_UNROLL = 4


def _peers():
    x, y, c = _mesh_pos()
    others = []
    for k in range(1, N_DEV):
        px = 1 - x if k & 4 else x
        py = 1 - y if k & 2 else y
        pc = 1 - c if k & 1 else c
        others.append(((px, py, pc), 4 * px + 2 * py + pc))
    return 4 * x + 2 * y + c, others


def _exchange(src_ref, dst_ref, send_sems, recv_sems, local_sem, *, finish, gather=False):
    me, others = _peers()

    def send(k, dev, slot):
        return pltpu.make_async_remote_copy(
            src_ref=src_ref if gather else src_ref.at[slot], dst_ref=dst_ref.at[me], send_sem=send_sems.at[k],
            recv_sem=recv_sems.at[k], device_id=dev, device_id_type=pl.DeviceIdType.MESH)

    local = pltpu.make_async_copy(src_ref if gather else src_ref.at[me], dst_ref.at[me], local_sem)
    if not finish:
        local.start()
        for k, (dev, slot) in enumerate(others):
            send(k, dev, slot).start()
        return
    for k, (dev, slot) in enumerate(others):
        pltpu.make_async_remote_copy(
            src_ref=dst_ref.at[slot], dst_ref=dst_ref.at[slot], send_sem=send_sems.at[k], recv_sem=recv_sems.at[k],
            device_id=dev, device_id_type=pl.DeviceIdType.MESH).wait_recv()
    for k, (dev, slot) in enumerate(others):
        send(k, dev, slot).wait_send()
    local.wait()


def _gather_two_level(x_ref, out_ref, send_sems, recv_sems, local_sem, *, phase):
    x, y, c = _mesh_pos()
    me, sibling = (x, y, c), (x, y, 1 - c)
    chips = [(1 - x, y), (x, 1 - y), (1 - x, 1 - y)]

    def slot(px, py, pc):
        return out_ref.at[4 * px + 2 * py + pc]

    def copy(k, blk, to, src=None):
        return pltpu.make_async_remote_copy(
            src_ref=slot(*blk) if src is None else src, dst_ref=slot(*blk),
            send_sem=send_sems.at[k], recv_sem=recv_sems.at[k], device_id=to, device_id_type=pl.DeviceIdType.MESH)

    if phase == 0:
        pltpu.make_async_copy(x_ref, slot(*me), local_sem).start()
        copy(0, me, sibling, src=x_ref).start()
        for j, chip in enumerate(chips):
            copy(1 + j, me, (*chip, c), src=x_ref).start()
    elif phase == 1:
        for j, chip in enumerate(chips):
            copy(1 + j, (*chip, c), me).wait_recv()
            copy(4 + j, (*chip, c), sibling).start()
    else:
        copy(0, sibling, me).wait_recv()
        for j, chip in enumerate(chips):
            copy(4 + j, (*chip, 1 - c), me).wait_recv()
        copy(0, me, sibling, src=x_ref).wait_send()
        for j, chip in enumerate(chips):
            copy(1 + j, me, (*chip, c), src=x_ref).wait_send()
            copy(4 + j, (*chip, c), sibling).wait_send()
        pltpu.make_async_copy(x_ref, slot(*me), local_sem).wait()


GATHER_FORWARD_STEP = 9
_EXCHANGE_SEMS = [pltpu.SemaphoreType.DMA((N_DEV - 1,)), pltpu.SemaphoreType.DMA((N_DEV - 1,)), pltpu.SemaphoreType.DMA(())]


def _first_last(steps):
    first = last = None
    for axis, n in enumerate(steps):
        i = pl.program_id(axis)
        first = (i == 0) if first is None else first & (i == 0)
        last = (i == n - 1) if last is None else last & (i == n - 1)
    return first, last


def _lru_fwd(xp, ga, cw, vecs, wrg, wig, block, *, name):
    bsz, seq, _ = xp.shape
    groups = seq // 8

    def body(xp_ref, ga_ref, cw_ref, vec_ref, wrg_ref, wig_ref, blk_ref, xb_ref, hs_ref, y_ref, all_ref,
             xpad, a_s, b_s, send_sems, recv_sems, local_sem):
        first, last = _first_last((RNN_BLOCKS, bsz))

        @pl.when(first)
        def _():
            _gather_two_level(blk_ref, all_ref, send_sems, recv_sems, local_sem, phase=0)

        @pl.when((pl.program_id(0) == GATHER_FORWARD_STEP) & (pl.program_id(1) == 0))
        def _():
            _gather_two_level(blk_ref, all_ref, send_sems, recv_sems, local_sem, phase=1)

        xpad[0:8, :] = jnp.zeros((8, RNN_BW), F32)
        xpad[pl.ds(8, seq), :] = xp_ref[...]
        xb = _conv(xpad, cw_ref, seq) + vec_ref[0:1, :]
        xb_ref[...] = xb
        sp = _softplus(-vec_ref[3:4, :])
        _, i, a, mult, _ = _gates(xb, wrg_ref[...], wig_ref[...], vec_ref[1:2, :], vec_ref[2:3, :], sp)
        a_s[...] = a
        b_s[...] = mult * (i * xb)
        row = lax.broadcasted_iota(jnp.int32, (8, RNN_BW), 0)

        def group(g, h):
            r0 = pl.multiple_of(g * 8, 8)
            av = a_s[pl.ds(r0, 8), :]
            bv = b_s[pl.ds(r0, 8), :]
            for k in (1, 2, 4):
                m = row >= k
                bv = jnp.where(m, av * pltpu.roll(bv, k, axis=0) + bv, bv)
                av = jnp.where(m, av * pltpu.roll(av, k, axis=0), av)
            hs_ref[pl.ds(r0, 8), :] = av * h + bv
            return av[7:8, :] * h + bv[7:8, :]

        def groups_of(i, h):
            for u in range(SCAN_UNROLL):
                h = group(i * SCAN_UNROLL + u, h)
            return h

        lax.fori_loop(0, groups // SCAN_UNROLL, groups_of, jnp.zeros((1, RNN_BW), F32))
        gav = ga_ref[...]
        y_ref[...] = (hs_ref[...] * (gav * _sigmoid(gav))).astype(BF16)

        @pl.when(last)
        def _():
            _gather_two_level(blk_ref, all_ref, send_sems, recv_sems, local_sem, phase=2)

    sq = _seq_spec(seq)
    shape = (bsz, seq, D_RNN)
    return pl.pallas_call(
        body, grid=(RNN_BLOCKS, bsz),
        in_specs=[sq, sq, _chan_spec(8), _chan_spec(8), _GATE_W_SPEC, _GATE_W_SPEC, _ANY],
        out_specs=[sq, sq, sq, _ANY],
        out_shape=[jax.ShapeDtypeStruct(shape, F32), jax.ShapeDtypeStruct(shape, F32), jax.ShapeDtypeStruct(shape, BF16),
                   jax.ShapeDtypeStruct((N_DEV,) + block.shape, block.dtype)],
        scratch_shapes=[pltpu.VMEM((seq + 8, RNN_BW), F32), pltpu.VMEM((seq, RNN_BW), F32), pltpu.VMEM((seq, RNN_BW), F32)]
        + _EXCHANGE_SEMS,
        compiler_params=_params(("arbitrary", "arbitrary")), name=name)(xp, ga, cw, vecs, wrg, wig, block)


def _lru_bwd(dy, xp, xb, hs, ga, cw, vecs, wrg, wig, parts, *, name):
    bsz, seq, _ = xp.shape
    groups = seq // 8

    def body(dy_ref, xp_ref, xb_ref, hs_ref, ga_ref, cw_ref, vec_ref, wrg_ref, wig_ref,
             parts_ref, dxp_ref, dga_ref, dwrg_ref, dwig_ref, dvec_ref, land_ref, pad, a_s, d_s, lam_s,
             send_sems, recv_sems, local_sem):
        first, last = _first_last((RNN_BLOCKS, bsz))

        @pl.when(first)
        def _():
            _exchange(parts_ref, land_ref, send_sems, recv_sems, local_sem, finish=False)

        @pl.when(pl.program_id(1) == 0)
        def _():
            dwrg_ref[...] = jnp.zeros_like(dwrg_ref)
            dwig_ref[...] = jnp.zeros_like(dwig_ref)
            dvec_ref[...] = jnp.zeros_like(dvec_ref)

        xb = xb_ref[...]
        hs = hs_ref[...]
        gav = ga_ref[...]
        dy = dy_ref[...]
        sp = _softplus(-vec_ref[3:4, :])
        wrg = wrg_ref[...]
        wig = wig_ref[...]
        r, i, a, mult, inv_mult = _gates(xb, wrg, wig, vec_ref[1:2, :], vec_ref[2:3, :], sp)
        sg = _sigmoid(gav)
        dga_ref[...] = (dy * hs * (sg * (1.0 + gav * (1.0 - sg)))).astype(BF16)
        d_s[...] = dy * (gav * sg)

        pad[pl.ds(0, seq), :] = a
        pad[pl.ds(seq, 8), :] = jnp.zeros((8, RNN_BW), F32)
        a_s[...] = pad[pl.ds(1, seq), :]
        row = lax.broadcasted_iota(jnp.int32, (8, RNN_BW), 0)

        def group(g, nxt):
            r0 = pl.multiple_of((groups - 1 - g) * 8, 8)
            cv = a_s[pl.ds(r0, 8), :]
            bv = d_s[pl.ds(r0, 8), :]
            for k in (1, 2, 4):
                m = row < 8 - k
                bv = jnp.where(m, cv * pltpu.roll(bv, 8 - k, axis=0) + bv, bv)
                cv = jnp.where(m, cv * pltpu.roll(cv, 8 - k, axis=0), cv)
            lam_s[pl.ds(r0, 8), :] = cv * nxt + bv
            return cv[0:1, :] * nxt + bv[0:1, :]

        def groups_of(i, nxt):
            for u in range(SCAN_UNROLL):
                nxt = group(i * SCAN_UNROLL + u, nxt)
            return nxt

        lax.fori_loop(0, groups // SCAN_UNROLL, groups_of, jnp.zeros((1, RNN_BW), F32))
        dh = lam_s[...]

        pad[0:8, :] = jnp.zeros((8, RNN_BW), F32)
        pad[pl.ds(8, seq), :] = hs
        da = dh * pad[pl.ds(7, seq), :]
        ixb = i * xb
        dixb = dh * mult
        dla = da * a - (dh * ixb) * (a * a) * inv_mult
        drp = (dla * ((-LRU_C) * sp)) * r * (1.0 - r)
        dip = (dixb * xb) * i * (1.0 - i)
        dvec_ref[0:1, :] += jnp.sum(drp, axis=0, keepdims=True)
        dvec_ref[1:2, :] += jnp.sum(dip, axis=0, keepdims=True)
        dvec_ref[2:3, :] += jnp.sum(dla * ((-LRU_C) * r), axis=0, keepdims=True)
        drpb = drp.astype(BF16)
        dipb = dip.astype(BF16)
        xbb = xb.astype(BF16)
        nt = (((1,), (1,)), ((), ()))
        tn = (((0,), (0,)), ((), ()))
        dxb = (dixb * i
               + lax.dot_general(drpb, wrg, nt, preferred_element_type=F32)
               + lax.dot_general(dipb, wig, nt, preferred_element_type=F32))
        dwrg_ref[...] += lax.dot_general(xbb, drpb, tn, preferred_element_type=F32)
        dwig_ref[...] += lax.dot_general(xbb, dipb, tn, preferred_element_type=F32)
        dvec_ref[3:4, :] += jnp.sum(dxb, axis=0, keepdims=True)

        pad[pl.ds(0, seq), :] = dxb
        pad[pl.ds(seq, 8), :] = jnp.zeros((8, RNN_BW), F32)
        dxp = cw_ref[0:1, :] * pad[pl.ds(CONV_WIDTH - 1, seq), :]
        for k in range(1, CONV_WIDTH):
            dxp = dxp + cw_ref[k:k + 1, :] * pad[pl.ds(CONV_WIDTH - 1 - k, seq), :]
        dxp_ref[...] = dxp.astype(BF16)
        pad[0:8, :] = jnp.zeros((8, RNN_BW), F32)
        pad[pl.ds(8, seq), :] = xp_ref[...]
        for k in range(CONV_WIDTH):
            dvec_ref[4 + k:5 + k, :] += jnp.sum(dxb * pad[pl.ds(8 - (CONV_WIDTH - 1) + k, seq), :], axis=0, keepdims=True)

        @pl.when(last)
        def _():
            _exchange(parts_ref, land_ref, send_sems, recv_sems, local_sem, finish=True)

    sq = _seq_spec(seq)
    shape = (bsz, seq, D_RNN)
    gshape = (RNN_BLOCKS, RNN_BW, RNN_BW)
    return pl.pallas_call(
        body, grid=(RNN_BLOCKS, bsz),
        in_specs=[sq, sq, sq, sq, sq, _chan_spec(8), _chan_spec(8), _GATE_W_SPEC, _GATE_W_SPEC, _ANY],
        out_specs=[sq, sq, _GATE_W_SPEC, _GATE_W_SPEC, _chan_spec(8), _ANY],
        out_shape=[jax.ShapeDtypeStruct(shape, BF16), jax.ShapeDtypeStruct(shape, BF16),
                   jax.ShapeDtypeStruct(gshape, F32), jax.ShapeDtypeStruct(gshape, F32),
                   jax.ShapeDtypeStruct((8, D_RNN), F32), jax.ShapeDtypeStruct(parts.shape, parts.dtype)],
        scratch_shapes=[pltpu.VMEM((seq + 8, RNN_BW), F32), pltpu.VMEM((seq, RNN_BW), F32),
                        pltpu.VMEM((seq, RNN_BW), F32), pltpu.VMEM((seq, RNN_BW), F32)] + _EXCHANGE_SEMS,
        compiler_params=_params(("arbitrary", "arbitrary")), name=name)(dy, xp, xb, hs, ga, cw, vecs, wrg, wig, parts)


def _attn_block(seq):
    return min(512, seq)


def _diag_mask(blk):
    return lax.broadcasted_iota(jnp.int32, (blk, blk), 0) <= lax.broadcasted_iota(jnp.int32, (blk, blk), 1)


FWD_HEADS = 8
BWD_HEADS = 2


def _attn_fwd(q, kn, kr, v_t, block, *, bsz, seq, name):
    t = bsz * seq
    blk = _attn_block(seq)
    nq = seq // blk
    hg = FWD_HEADS
    steps = (bsz, N_HEADS // hg, nq)

    def body(q_ref, kn_ref, kr_ref, vt_ref, blk_ref, o_ref, lse_ref, all_ref, acc, send_sems, recv_sems, local_sem):
        first, last = _first_last(steps)

        @pl.when(first)
        def _():
            _exchange(blk_ref, all_ref, send_sems, recv_sems, local_sem, finish=False, gather=True)

        qi = pl.program_id(2)
        acc[...] = jnp.zeros_like(acc)

        def step(j, carry, diagonal):
            k0 = pl.multiple_of(j * blk, blk)
            kr_j = kr_ref[pl.ds(k0, blk), :]
            out = []
            for h in range(hg):
                m_i, l_i = carry[h]
                kv = jnp.concatenate([kn_ref[pl.ds(k0, blk), h * QK_NOPE:(h + 1) * QK_NOPE], kr_j], axis=1)
                qv = q_ref[:, h * HEAD_PAD:(h + 1) * HEAD_PAD]
                s = lax.dot_general(kv, qv, _NT, preferred_element_type=F32) * ATTN_SCALE
                if diagonal:
                    s = jnp.where(_diag_mask(blk), s, -jnp.inf)
                m_new = jnp.maximum(m_i, jnp.max(s, axis=0, keepdims=True))
                p = jnp.exp(s - m_new)
                alpha = jnp.exp(m_i - m_new)
                l_new = alpha * l_i + jnp.sum(p, axis=0, keepdims=True)
                acc[h] = alpha * acc[h] + jnp.dot(vt_ref[h * V_DIM:(h + 1) * V_DIM, pl.ds(k0, blk)], p.astype(BF16),
                                                  preferred_element_type=F32)
                out.append((m_new, l_new))
            return tuple(out)

        init = tuple((jnp.full((1, blk), -jnp.inf, F32), jnp.zeros((1, blk), F32)) for _ in range(hg))
        carry = lax.fori_loop(0, qi, lambda j, c: step(j, c, False), init)
        stats = step(qi, carry, True)
        for h in range(hg):
            m_i, l_i = stats[h]
            o_ref[:, h * V_DIM:(h + 1) * V_DIM] = (acc[h] / l_i).T
            lse_ref[h] = m_i + jnp.log(l_i)

        @pl.when(last)
        def _():
            _exchange(blk_ref, all_ref, send_sems, recv_sems, local_sem, finish=True, gather=True)

    return pl.pallas_call(
        body, grid=steps,
        in_specs=[pl.BlockSpec((blk, hg * HEAD_PAD), lambda b, g, i: (b * nq + i, g)),
                  pl.BlockSpec((seq, hg * QK_NOPE), lambda b, g, i: (b, g)),
                  pl.BlockSpec((seq, LANES), lambda b, g, i: (b, 0)),
                  pl.BlockSpec((hg * V_DIM, seq), lambda b, g, i: (g, b)), _ANY],
        out_specs=[pl.BlockSpec((blk, hg * V_DIM), lambda b, g, i: (b * nq + i, g)),
                   pl.BlockSpec((hg, 1, blk), lambda b, g, i: (g, 0, b * nq + i)), _ANY],
        out_shape=[jax.ShapeDtypeStruct((t, N_HEADS * V_DIM), F32), jax.ShapeDtypeStruct((N_HEADS, 1, t), F32),
                   jax.ShapeDtypeStruct((N_DEV,) + block.shape, block.dtype)],
        scratch_shapes=[pltpu.VMEM((hg, V_DIM, blk), F32)] + _EXCHANGE_SEMS,
        compiler_params=_params(("arbitrary", "arbitrary", "arbitrary")), name=name)(q, kn, kr, v_t, block)


def _attn_bwd(q, kn, kr, kn_t, kr_t, v, o, lse, do, cos, sin, parts, *, bsz, seq, name):
    t = bsz * seq
    blk = _attn_block(seq)
    nq = seq // blk
    hg = BWD_HEADS
    steps = (bsz, N_HEADS // hg)

    def body(q_ref, kn_ref, kr_ref, knt_ref, krt_ref, v_ref, o_ref, lse_ref, do_ref, cos_ref, sin_ref, parts_ref,
             dq_ref, dkn_ref, dkr_ref, dv_ref, land_ref, dqt_acc, dk_acc, dv_acc, send_sems, recv_sems, local_sem):
        first, last = _first_last(steps)

        @pl.when(first)
        def _():
            _exchange(parts_ref, land_ref, send_sems, recv_sems, local_sem, finish=False)

        dqt_acc[...] = jnp.zeros_like(dqt_acc)
        dk_acc[...] = jnp.zeros_like(dk_acc)
        dv_acc[...] = jnp.zeros_like(dv_acc)

        def q_block(i, _):
            q0 = pl.multiple_of(i * blk, blk)
            rows = []
            for h in range(hg):
                dov = do_ref[pl.ds(q0, blk), h * V_DIM:(h + 1) * V_DIM].astype(F32)
                dcol = jnp.sum(dov * o_ref[pl.ds(q0, blk), h * V_DIM:(h + 1) * V_DIM], axis=-1, keepdims=True)
                delta = jnp.broadcast_to(dcol, (blk, LANES)).T[0:1, :]
                rows.append((lse_ref[h, :, pl.ds(q0, blk)], delta))

            def pair(j, diagonal):
                k0 = pl.multiple_of(j * blk, blk)
                kr_j = kr_ref[pl.ds(k0, blk), :]
                krt_j = krt_ref[:, pl.ds(k0, blk)]
                for h in range(hg):
                    lse_i, delta = rows[h]
                    qv = q_ref[pl.ds(q0, blk), h * HEAD_PAD:(h + 1) * HEAD_PAD]
                    dov = do_ref[pl.ds(q0, blk), h * V_DIM:(h + 1) * V_DIM]
                    kv = jnp.concatenate([kn_ref[pl.ds(k0, blk), h * QK_NOPE:(h + 1) * QK_NOPE], kr_j], axis=1)
                    s = lax.dot_general(kv, qv, _NT, preferred_element_type=F32) * ATTN_SCALE
                    p = jnp.exp(s - lse_i)
                    if diagonal:
                        p = jnp.where(_diag_mask(blk), p, 0.0)
                    dv_acc[pl.ds(k0, blk), h * V_DIM:(h + 1) * V_DIM] += jnp.dot(
                        p.astype(BF16), dov, preferred_element_type=F32)
                    dp = lax.dot_general(v_ref[pl.ds(k0, blk), h * V_DIM:(h + 1) * V_DIM], dov, _NT,
                                         preferred_element_type=F32)
                    ds = (p * (dp - delta) * ATTN_SCALE).astype(BF16)
                    dk_acc[pl.ds(k0, blk), h * HEAD_PAD:(h + 1) * HEAD_PAD] += jnp.dot(ds, qv, preferred_element_type=F32)
                    base = h * HEAD_PAD
                    dqt_acc[base:base + QK_NOPE, pl.ds(q0, blk)] += jnp.dot(
                        knt_ref[h * QK_NOPE:(h + 1) * QK_NOPE, pl.ds(k0, blk)], ds, preferred_element_type=F32)
                    dqt_acc[base + QK_NOPE:base + HEAD_PAD, pl.ds(q0, blk)] += jnp.dot(
                        krt_j, ds, preferred_element_type=F32)

            def off_diagonal(j, _):
                pair(j, False)
                return 0

            lax.fori_loop(0, i, off_diagonal, 0)
            pair(i, True)
            return 0

        lax.fori_loop(0, nq, q_block, 0)
        dkr = jnp.zeros((seq, LANES), F32)
        for h in range(hg):
            base = h * HEAD_PAD
            for i in range(nq):
                rows = slice(i * blk, (i + 1) * blk)
                dq = dqt_acc[base:base + HEAD_PAD, rows].T
                dq_ref[rows, base:base + QK_NOPE] = dq[:, :QK_NOPE].astype(BF16)
                dq_ref[rows, base + QK_NOPE:base + HEAD_PAD] = _rope_t(
                    dq[:, QK_NOPE:], cos_ref[rows, :], sin_ref[rows, :]).astype(BF16)
            dkn_ref[:, h * QK_NOPE:(h + 1) * QK_NOPE] = dk_acc[:, base:base + QK_NOPE].astype(BF16)
            dkr = dkr + dk_acc[:, base + QK_NOPE:base + HEAD_PAD]
        dv_ref[...] = dv_acc[...].astype(BF16)

        @pl.when(pl.program_id(1) == 0)
        def _():
            dkr_ref[...] = jnp.zeros_like(dkr_ref)

        dkr_ref[...] += _rope_t(dkr, cos_ref[...], sin_ref[...])

        @pl.when(last)
        def _():
            _exchange(parts_ref, land_ref, send_sems, recv_sems, local_sem, finish=True)

    head = pl.BlockSpec((seq, hg * V_DIM), lambda b, g: (b, g))
    head_t = pl.BlockSpec((hg * V_DIM, seq), lambda b, g: (g, b))
    shared = pl.BlockSpec((seq, LANES), lambda b, g: (b, 0))
    shared_t = pl.BlockSpec((LANES, seq), lambda b, g: (0, b))
    table = pl.BlockSpec((seq, LANES), lambda b, g: (0, 0))
    qspec = pl.BlockSpec((seq, hg * HEAD_PAD), lambda b, g: (b, g))
    return pl.pallas_call(
        body, grid=steps,
        in_specs=[qspec, head, shared, head_t, shared_t, head, head,
                  pl.BlockSpec((hg, 1, seq), lambda b, g: (g, 0, b)), head, table, table, _ANY],
        out_specs=[qspec, head, shared, head, _ANY],
        out_shape=[jax.ShapeDtypeStruct((t, N_HEADS * HEAD_PAD), BF16), jax.ShapeDtypeStruct((t, N_HEADS * QK_NOPE), BF16),
                   jax.ShapeDtypeStruct((t, LANES), F32), jax.ShapeDtypeStruct((t, N_HEADS * V_DIM), BF16),
                   jax.ShapeDtypeStruct(parts.shape, parts.dtype)],
        scratch_shapes=[pltpu.VMEM((hg * HEAD_PAD, seq), F32), pltpu.VMEM((seq, hg * HEAD_PAD), F32),
                        pltpu.VMEM((seq, hg * V_DIM), F32)] + _EXCHANGE_SEMS,
        compiler_params=_params(("arbitrary", "arbitrary")), name=name)(
            q, kn, kr, kn_t, kr_t, v, o, lse, do, cos, sin, parts)


def _head_and_loss(o, g2, x1, target, w_out, g_final, *, name, bt=512):
    t, d = x1.shape
    bt = min(bt, t)
    nt = (((1,), (1,)), ((), ()))
    tn = (((0,), (0,)), ((), ()))

    def body(o_ref, g2_ref, x1_ref, tgt_ref, w_ref, gf_ref, loss_ref, dx2_ref, dw_ref, do_ref, dg2_ref, dgf_ref):
        @pl.when(pl.program_id(0) == 0)
        def _():
            for ref in (loss_ref, dgf_ref, dw_ref):
                ref[...] = jnp.zeros_like(ref)

        ov = o_ref[...]
        gv = g2_ref[...]
        sg = _sigmoid(gv)
        silu = gv * sg
        y2 = (ov * silu).astype(BF16)
        w = w_ref[...]
        x2 = x1_ref[...] + jnp.dot(y2, w, preferred_element_type=F32)
        r = lax.rsqrt(jnp.mean(x2 * x2, axis=-1, keepdims=True) + EPS)
        nrm = x2 * r
        gf = gf_ref[...]
        err = nrm * gf - tgt_ref[...]
        loss_ref[...] += 0.5 * jnp.sum(jnp.mean(err * err, axis=-1, keepdims=True))
        dyf = err * (1.0 / d)
        dgf_ref[...] += jnp.sum(dyf * nrm, axis=0, keepdims=True)
        dn = dyf * gf
        dx2 = r * (dn - nrm * jnp.mean(dn * nrm, axis=-1, keepdims=True))
        dx2_ref[...] = dx2
        dx2 = dx2.astype(BF16)
        dw_ref[...] += lax.dot_general(y2, dx2, tn, preferred_element_type=F32)
        dy2 = lax.dot_general(dx2, w, nt, preferred_element_type=F32)
        do_ref[...] = (dy2 * silu).astype(BF16)
        dg2_ref[...] = (dy2 * ov * (sg * (1.0 + gv * (1.0 - sg)))).astype(BF16)

    row = pl.BlockSpec((bt, d), lambda i: (i, 0))
    vec = pl.BlockSpec((1, d), lambda i: (0, 0))
    return pl.pallas_call(
        body, grid=(t // bt,),
        in_specs=[row, row, row, row, pl.BlockSpec((d, d), lambda i: (0, 0)), vec],
        out_specs=[pl.BlockSpec((8, LANES), lambda i: (0, 0)), row, pl.BlockSpec((d, d), lambda i: (0, 0)), row, row, vec],
        out_shape=[jax.ShapeDtypeStruct((8, LANES), F32), jax.ShapeDtypeStruct((t, d), F32),
                   jax.ShapeDtypeStruct((d, d), F32), jax.ShapeDtypeStruct((t, d), BF16),
                   jax.ShapeDtypeStruct((t, d), BF16), jax.ShapeDtypeStruct((1, d), F32)],
        compiler_params=_params(("arbitrary",)), name=name)(o, g2, x1, target, w_out, g_final)


def _sum_parts(parts, *, name, br=GRAD_BLOCK):
    npart, rows, w = parts.shape

    def body(p_ref, o_ref):
        acc = p_ref[0].astype(F32)
        for j in range(1, npart):
            acc = acc + p_ref[j].astype(F32)
        o_ref[...] = acc

    return pl.pallas_call(
        body, grid=(rows // br,), in_specs=[pl.BlockSpec((npart, br, w), lambda i: (0, i, 0))],
        out_specs=pl.BlockSpec((br, w), lambda i: (i, 0)), out_shape=jax.ShapeDtypeStruct((rows, w), F32),
        compiler_params=_params(("parallel",)), name=name)(parts)


def _chip_partial(parts, recv, *, name, br=GRAD_BLOCK):
    _, rows, w = parts.shape
    core = lax.axis_index("c").astype(jnp.int32).reshape(1)

    def body(c_ref, p_ref, r_ref, o_ref):
        o_ref[...] = (p_ref[...] + r_ref[...]).astype(BF16)

    grid_spec = pltpu.PrefetchScalarGridSpec(
        num_scalar_prefetch=1, grid=(4, rows // br),
        in_specs=[pl.BlockSpec((None, br, w), lambda k, i, c_ref: (2 * k + c_ref[0], i, 0)),
                  pl.BlockSpec((None, br, w), lambda k, i, c_ref: (k, i, 0))],
        out_specs=pl.BlockSpec((None, br, w), lambda k, i, c_ref: (k, i, 0)))
    return pl.pallas_call(
        body, grid_spec=grid_spec, out_shape=jax.ShapeDtypeStruct((4, rows, w), BF16),
        compiler_params=_params(("parallel", "parallel")), name=name)(core, parts, recv)


def _as_block(a):
    if a.ndim == 1:
        return a.reshape(1, -1)
    if a.ndim > 2 and a.shape[0] == 1:
        return a.reshape(a.shape[1:])
    return a


def _adamw(grads, weights, mom1, mom2, *, name):
    k = len(weights)
    shapes = [w.shape for w in weights]
    args = [_as_block(a) for group in (grads, weights, mom1, mom2) for a in group]

    def body(*refs):
        for i in range(k):
            g_ref, w_ref, m_ref, v_ref, d_ref, nm_ref, nv_ref = (refs[j * k + i] for j in range(7))
            gv = g_ref[...]
            nm = ADAM_B1 * m_ref[...] + (1.0 - ADAM_B1) * gv
            nv = ADAM_B2 * v_ref[...] + (1.0 - ADAM_B2) * (gv * gv)
            nm_ref[...] = nm
            nv_ref[...] = nv
            m_hat = nm / (1.0 - ADAM_B1 ** ADAM_STEP)
            v_hat = nv / (1.0 - ADAM_B2 ** ADAM_STEP)
            d_ref[...] = (-ADAM_LR) * (m_hat / (jnp.sqrt(v_hat) + ADAM_EPS) + ADAM_WD * w_ref[...])

    whole = pl.BlockSpec(memory_space=pltpu.VMEM)
    outs = pl.pallas_call(
        body, in_specs=[whole] * (4 * k), out_specs=[whole] * (3 * k),
        out_shape=[jax.ShapeDtypeStruct(a.shape, F32) for a in args[k:2 * k]] * 3,
        compiler_params=_params(), name=name)(*args)
    return [[o.reshape(s) for o, s in zip(outs[j * k:(j + 1) * k], shapes)] for j in range(3)]


def _all_gather(block, *, name):
    m, n = block.shape

    def body(x_ref, out_ref, send_sems, recv_sems, local_sem):
        for phase in range(3):
            _gather_two_level(x_ref, out_ref, send_sems, recv_sems, local_sem, phase=phase)

    return pl.pallas_call(
        body, out_shape=jax.ShapeDtypeStruct((N_DEV, m, n), block.dtype), in_specs=[_ANY], out_specs=_ANY,
        scratch_shapes=_EXCHANGE_SEMS, name=name)(block)


def _exchange_d2d(parts, *, name):
    _, rows, w = parts.shape

    def body(p_ref, land_ref, send_sems, recv_sems):
        x, y, c = _mesh_pos()
        sends = []
        for k in range(4):
            cp = pltpu.make_async_remote_copy(
                src_ref=p_ref.at[2 * k + (1 - c)], dst_ref=land_ref.at[k], send_sem=send_sems.at[k],
                recv_sem=recv_sems.at[k], device_id=(x, y, 1 - c), device_id_type=pl.DeviceIdType.MESH)
            cp.start()
            sends.append(cp)
        for cp in sends:
            cp.wait_recv()
        for cp in sends:
            cp.wait_send()

    return pl.pallas_call(
        body, out_shape=jax.ShapeDtypeStruct((4, rows, w), parts.dtype), in_specs=[_ANY], out_specs=_ANY,
        scratch_shapes=[pltpu.SemaphoreType.DMA((4,)), pltpu.SemaphoreType.DMA((4,))], name=name)(parts)


def _exchange_ici(parts, *, name):
    def body(p_ref, land_ref, send_sems, recv_sems, local_sem):
        x, y, c = _mesh_pos()
        mine = pltpu.make_async_copy(p_ref.at[2 * x + y], land_ref.at[3], local_sem)
        mine.start()
        sends = []
        for k, (px, py) in enumerate([(1 - x, y), (x, 1 - y), (1 - x, 1 - y)]):
            cp = pltpu.make_async_remote_copy(
                src_ref=p_ref.at[2 * px + py], dst_ref=land_ref.at[k], send_sem=send_sems.at[k],
                recv_sem=recv_sems.at[k], device_id=(px, py, c), device_id_type=pl.DeviceIdType.MESH)
            cp.start()
            sends.append(cp)
        for cp in sends:
            cp.wait_recv()
        for cp in sends:
            cp.wait_send()
        mine.wait()

    return pl.pallas_call(
        body, out_shape=jax.ShapeDtypeStruct(parts.shape, parts.dtype), in_specs=[_ANY], out_specs=_ANY,
        scratch_shapes=[pltpu.SemaphoreType.DMA((3,)), pltpu.SemaphoreType.DMA((3,)), pltpu.SemaphoreType.DMA(())],
        name=name)(parts)


def _rows(a):
    return a.reshape(-1, PACK_W)


def _pad_to(a, n):
    return jnp.pad(a, (0, n - a.shape[0]))


def _weight_blocks(d):
    small = _rows(_pad_to(jnp.concatenate([d[n].reshape(-1) for n, _ in _SMALL]), 16 * PACK_W))
    bits = lax.bitcast_convert_type(small, jnp.uint32)
    halves = [lax.bitcast_convert_type(h.astype(jnp.uint16), WIRE) for h in (bits >> 16, bits & 0xFFFF)]
    block_a = jnp.concatenate([d["w_in_a"][0].T.astype(WIRE)] + halves, axis=0)
    w_uq = jnp.pad(d["w_uq"][0], ((0, 0), (0, 0), (0, HEAD_PAD - QK_NOPE - QK_ROPE)))
    pieces = {"w_out_a": d["w_out_a"], "w_dkv": d["w_dkv"], "w_uk": d["w_uk"], "w_uv": d["w_uv"],
              "w_in_b": d["w_in_b"][0].T, "w_uq": w_uq}
    block_b = jnp.concatenate([_rows(pieces[n]) for n, _ in _PIECES_B]
                              + [jnp.zeros((WIRE_ROWS_B - MATRIX_ROWS_B, PACK_W), F32)], axis=0).astype(WIRE)
    return block_a, block_b, d["w_out_b"][0].astype(WIRE)


def _weights_a(wall):
    w = {}
    lo, hi = _OFF_A["w_in_a"]
    w["w_in_a_t"] = wall[:, lo:hi].reshape(2 * D_RNN, D_MODEL)
    high, low = (lax.bitcast_convert_type(wall[:, r:r + 16], jnp.uint16).astype(jnp.uint32)
                 for r in (MATRIX_ROWS_A, MATRIX_ROWS_A + 16))
    small = lax.bitcast_convert_type((high << 16) | low, F32)[:, :8].reshape(N_DEV, 8 * PACK_W)
    off = dict(zip([n for n, _ in _SMALL], [0, 128, 768, 928, 1088, 1248]))
    w["norm_a"] = small[:, :128].reshape(1, D_MODEL)

    def by_channel(lo, rows):
        a = small[:, lo:lo + rows * (D_RNN // N_DEV)].reshape(N_DEV, rows, -1).transpose(1, 0, 2).reshape(rows, D_RNN)
        return jnp.pad(a, ((0, 8 - rows), (0, 0)))

    w["conv_taps"] = by_channel(off["conv_w"], CONV_WIDTH)
    w["lru_vecs"] = by_channel(off["conv_b"], 4)
    return w


def _weights_b(wall):
    piece = {n: wall[:, lo:hi] for n, (lo, hi) in _OFF_B.items()}
    w = {"w_out_a": piece["w_out_a"].reshape(D_RNN, D_MODEL)}
    w_dkv = piece["w_dkv"].reshape(D_MODEL, KV_RANK + QK_ROPE)
    w["w_dkv_c"] = w_dkv[:, :KV_RANK]
    w["w_dkv_r"] = jnp.pad(w_dkv[:, KV_RANK:], ((0, 0), (0, LANES - QK_ROPE)))
    w["w_uk"] = piece["w_uk"].reshape(KV_RANK, N_HEADS * QK_NOPE)
    w["w_uv"] = piece["w_uv"].reshape(KV_RANK, N_HEADS * V_DIM)
    w["w_in_b_t"] = piece["w_in_b"].reshape(Q_RANK + N_HEADS * V_DIM, D_MODEL)
    w["w_uq"] = piece["w_uq"].reshape(Q_RANK, N_HEADS * HEAD_PAD)
    return w


def _pack_rep(d):
    flat = jnp.concatenate([d[n].reshape(-1) for n, _ in _REP])
    return _rows(_pad_to(flat, REP_ROWS * PACK_W))


def _unpack_rep(p, like):
    flat = p.reshape(-1)
    out, off = {}, 0
    for n, k in _REP:
        out[n] = flat[off:off + k].reshape(like[n].shape)
        off += k
    return out


def _by_owner(a):
    return a.reshape(N_DEV, -1, PACK_W)


def _grad_parts_b(g):
    tail = jnp.zeros((N_DEV, WIRE_ROWS_B - MATRIX_ROWS_B, PACK_W), F32)
    return jnp.concatenate([_by_owner(g[n]) for n, _ in _PIECES_B] + [tail], axis=1).astype(BF16)


def _grad_parts_a(g):
    small = jnp.concatenate([
        g["norm_a"].reshape(N_DEV, -1),
        g["conv_w"].reshape(CONV_WIDTH, N_DEV, -1).transpose(1, 0, 2).reshape(N_DEV, -1),
        g["conv_b"].reshape(N_DEV, -1), g["b_rg"].reshape(N_DEV, -1), g["b_ig"].reshape(N_DEV, -1),
        g["lru_lambda"].reshape(N_DEV, -1)], axis=1)
    small = jnp.pad(small, ((0, 0), (0, 8 * PACK_W - small.shape[1]))).reshape(N_DEV, 8, PACK_W)
    half = N_DEV // 2
    w_in_a = jnp.concatenate([h.reshape(half, -1, PACK_W) for h in g["w_in_a_t"]], axis=0)
    rep = _pack_rep(g).reshape(N_DEV, REP_SLICE, PACK_W)
    tail = jnp.zeros((N_DEV, GRAD_ROWS_A - MATRIX_ROWS_A - 8 - REP_SLICE, PACK_W), F32)
    return jnp.concatenate([w_in_a, small, rep, tail], axis=1)


def _own_grads(sum_a, sum_b, sum_c):
    out = {}
    lo, hi = _OFF_A["w_in_a"]
    out["w_in_a"] = sum_a[lo:hi].T.reshape(1, D_MODEL, 2 * D_RNN // N_DEV)
    small = sum_a[MATRIX_ROWS_A:MATRIX_ROWS_A + 8].reshape(-1)
    shapes = {"norm_a": (1, D_MODEL // N_DEV), "conv_w": (1, CONV_WIDTH, D_RNN // N_DEV), "conv_b": (1, D_RNN // N_DEV),
              "b_rg": (1, D_RNN // N_DEV), "b_ig": (1, D_RNN // N_DEV), "lru_lambda": (1, D_RNN // N_DEV)}
    off = 0
    for n, k in _SMALL:
        out[n] = small[off:off + k].reshape(shapes[n])
        off += k
    piece = {n: sum_b[lo:hi] for n, (lo, hi) in _OFF_B.items()}
    out["w_out_a"] = piece["w_out_a"].reshape(1, D_RNN // N_DEV, D_MODEL)
    out["w_dkv"] = piece["w_dkv"].reshape(D_MODEL // N_DEV, KV_RANK + QK_ROPE)
    out["w_uk"] = piece["w_uk"].reshape(KV_RANK // N_DEV, N_HEADS, QK_NOPE)
    out["w_uv"] = piece["w_uv"].reshape(KV_RANK // N_DEV, N_HEADS, V_DIM)
    out["w_in_b"] = piece["w_in_b"].T.reshape(1, D_MODEL, (Q_RANK + N_HEADS * V_DIM) // N_DEV)
    out["w_uq"] = piece["w_uq"].reshape(1, Q_RANK // N_DEV, N_HEADS, HEAD_PAD)[..., :QK_NOPE + QK_ROPE]
    out["w_out_b"] = sum_c.reshape(1, N_HEADS * V_DIM // N_DEV, D_MODEL)
    return out


def _step(x, target, w, rep, block_b, block_c, *, bsz, seq):
    t = bsz * seq
    cos, sin = _rope_tables(seq)
    g_a = w["norm_a"]
    g_kv = rep["norm_kv"].reshape(1, -1)
    g_kvn = rep["kv_norm"].reshape(1, -1)
    g_b = rep["norm_b"].reshape(1, -1)
    g_q = rep["q_norm"].reshape(1, -1)
    g_f = rep["final_norm"].reshape(1, -1)
    wrg = rep["w_rg"][0].astype(BF16)
    wig = rep["w_ig"][0].astype(BF16)
    cw8, vecs = w["conv_taps"], w["lru_vecs"]

    def seq3(a):
        return a.reshape(bsz, seq, a.shape[-1])

    def flat(a):
        return a.reshape(t, a.shape[-1])

    h0, xp, ga = _lru_proj_fwd(x, g_a, w["w_in_a_t"], name="lru_proj_fwd")
    xb, hs, y, wall_b = _lru_fwd(seq3(xp), seq3(ga), cw8, vecs, wrg, wig, block_b, name="lru_fwd")
    w = dict(w, **_weights_b(wall_b))
    x1 = _matmul(flat(y), w["w_out_a"], residual=x, name="out_a")
    hk, hq, ck, cqp, g2, ckv, cq, q, kn, v, kr, kn_t, v_t, kr_t = _mla_proj_fwd(
        x1, (g_kv, g_b, g_kvn, g_q), w, cos, sin, seq=seq, name="mla_proj_fwd")
    o, lse, wall_c = _attn_fwd(q, kn, kr, v_t, block_c, bsz=bsz, seq=seq, name="attn_fwd")
    w_out_b = wall_c.reshape(N_HEADS * V_DIM, D_MODEL)
    loss, dx2, d_w_out_b, do, dg2, dgf = _head_and_loss(o, g2, x1, target, w_out_b, g_f, name="head_loss")
    grads = {"final_norm": dgf}
    parts_c = _by_owner(d_w_out_b).astype(BF16)
    dq, dkn, dkr, dv, landed_c = _attn_bwd(q, kn, kr, kn_t, kr_t, v, o, lse, do, cos, sin, parts_c,
                                           bsz=bsz, seq=seq, name="attn_bwd")
    dx1, du2, dckr, dgkv, dgb, dgkvn, dgq, dy = _mla_proj_bwd(
        x1, dx2, cqp, ck, dq, dkn, dv, dkr, dg2, (g_kv, g_b, g_kvn, g_q), w, name="mla_proj_bwd")
    grads["norm_kv"], grads["norm_b"], grads["kv_norm"], grads["q_norm"] = dgkv, dgb, dgkvn, dgq
    grads["w_uq"], grads["w_uk"], grads["w_uv"], d_w_dkv = _token_sums(
        [cq, dq, ckv, dkn, dv, hk, dckr], [(0, 1), (2, 3), (2, 4), (5, 6)], name="d_w_uq_uk_uv_dkv")
    grads["w_dkv"] = d_w_dkv[:, :KV_RANK + QK_ROPE]
    grads["w_in_b"], grads["w_out_a"] = _token_sums(
        [du2, hq, flat(y), dx1], [(0, 1), (2, 3)], name="d_w_in_b_t_out_a")
    parts_b = _grad_parts_b(grads)
    dxp, dga, dwrg, dwig, dvec, landed_b = _lru_bwd(
        seq3(dy), seq3(xp), xb, hs, seq3(ga), cw8, vecs, wrg, wig, parts_b, name="lru_bwd")
    dxp, dga = flat(dxp), flat(dga)
    grads["w_rg"], grads["w_ig"] = dwrg, dwig
    grads["b_rg"], grads["b_ig"], grads["conv_b"] = dvec[0], dvec[1], dvec[3]
    lam = vecs[3]
    grads["lru_lambda"] = dvec[2] * (-1.0 / (1.0 + jnp.exp(lam)))
    grads["conv_w"] = dvec[4:4 + CONV_WIDTH]
    dx, dga_norm, dwx, dwg = _lru_proj_bwd(dxp, dga, x, dx1, h0, g_a, w["w_in_a_t"], name="lru_proj_bwd")
    grads["norm_a"] = dga_norm
    grads["w_in_a_t"] = (dwx, dwg)
    return loss[0, 0], dx, grads, landed_b, landed_c


def kernel(x, norm_a, w_in_a, conv_w, conv_b, w_rg, b_rg, w_ig, b_ig, lru_lambda, w_out_a, norm_kv, w_dkv, kv_norm, w_uk, w_uv, norm_b, w_in_b, q_norm, w_uq, w_out_b, final_norm, loss_target, m_norm_a, m_w_in_a, m_conv_w, m_conv_b, m_w_rg, m_b_rg, m_w_ig, m_b_ig, m_lru_lambda, m_w_out_a, m_norm_kv, m_w_dkv, m_kv_norm, m_w_uk, m_w_uv, m_norm_b, m_w_in_b, m_q_norm, m_w_uq, m_w_out_b, m_final_norm, v_norm_a, v_w_in_a, v_conv_w, v_conv_b, v_w_rg, v_b_rg, v_w_ig, v_b_ig, v_lru_lambda, v_w_out_a, v_norm_kv, v_w_dkv, v_kv_norm, v_w_uk, v_w_uv, v_norm_b, v_w_in_b, v_q_norm, v_w_uq, v_w_out_b, v_final_norm):
    given = dict(locals())
    wts = {n: given[n] for n in WEIGHTS}
    mom1 = {n: given["m_" + n] for n in WEIGHTS}
    mom2 = {n: given["v_" + n] for n in WEIGHTS}
    bsz, seq, _ = x.shape
    t = bsz * seq

    block_a, block_b, block_c = _weight_blocks(wts)
    w = _weights_a(_all_gather(block_a, name="gather_weights_a"))
    loss, dx, grads, landed_b, landed_c = _step(x.reshape(t, D_MODEL), loss_target.reshape(t, D_MODEL), w, wts,
                                                block_b, block_c, bsz=bsz, seq=seq)

    parts_a = _grad_parts_a(grads)
    from_sibling = _exchange_d2d(parts_a, name="exchange_grads_d2d")
    chip_parts = _chip_partial(parts_a, from_sibling, name="chip_partial_grads")
    landed_a = _exchange_ici(chip_parts, name="exchange_grads_ici")
    sum_a = _sum_parts(landed_a, name="sum_grads_a", br=GRAD_BLOCK)
    sum_b = _sum_parts(landed_b, name="sum_grads_b", br=WIRE_ROWS_B // 2)
    sum_c = _sum_parts(landed_c, name="sum_grads_c", br=landed_c.shape[1])
    g_own = _own_grads(sum_a, sum_b, sum_c)
    rep_slice = sum_a[MATRIX_ROWS_A + 8:MATRIX_ROWS_A + 8 + REP_SLICE]
    loss_rows = jnp.pad(loss.reshape(1, 1), ((0, 7), (0, PACK_W - 1)))
    gathered = _all_gather(jnp.concatenate([rep_slice, loss_rows], axis=0), name="gather_replicated")
    g_own.update(_unpack_rep(gathered[:, :REP_SLICE].reshape(REP_ROWS, PACK_W), wts))
    loss = jnp.sum(gathered[:, REP_SLICE, 0])

    own = [g_own[n] for n in WEIGHTS]
    deltas, new_m, new_v = _adamw(own, *([d[n] for n in WEIGHTS] for d in (wts, mom1, mom2)), name="adamw")
    return (loss, dx.reshape(bsz, seq, D_MODEL), *own, *deltas, *new_m, *new_v)
```

```python
import jax
import jax.numpy as jnp
from jax import lax
from jax.experimental import pallas as pl
from jax.experimental.pallas import tpu as pltpu

F32 = jnp.float32
BF16 = jnp.bfloat16
WIRE = jnp.bfloat16

D_MODEL = 1024
D_RNN = 1280
RNN_BLOCKS = 10
RNN_BW = 128
CONV_WIDTH = 4
LRU_C = 8.0
N_HEADS = 8
QK_NOPE = 128
QK_ROPE = 64
V_DIM = 128
KV_RANK = 256
Q_RANK = 384
ROPE_THETA = 10000.0
EPS = 1e-6
ATTN_SCALE = (QK_NOPE + QK_ROPE) ** -0.5
HEAD_PAD = 256
LANES = 128

ADAM_LR = 0.001
ADAM_B1 = 0.9
ADAM_B2 = 0.999
ADAM_EPS = 1e-08
ADAM_WD = 0.01
ADAM_STEP = 10

N_DEV = 8
VMEM_LIMIT_BYTES = 56 * 2**20
PACK_W = 1024

_PIECES_A = (("w_in_a", 320),)
_PIECES_B = (("w_out_a", 160), ("w_dkv", 40), ("w_uk", 32), ("w_uv", 32), ("w_in_b", 176), ("w_uq", 96))


def _offsets(pieces):
    off, r = {}, 0
    for n, k in pieces:
        off[n] = (r, r + k)
        r += k
    return off, r


_OFF_A, MATRIX_ROWS_A = _offsets(_PIECES_A)
_OFF_B, MATRIX_ROWS_B = _offsets(_PIECES_B)
WIRE_ROWS_A = MATRIX_ROWS_A + 32
WIRE_ROWS_B = 544
_SMALL = (("norm_a", 128), ("conv_w", 640), ("conv_b", 160), ("b_rg", 160), ("b_ig", 160), ("lru_lambda", 160))
_REP = (("w_rg", 163840), ("w_ig", 163840), ("norm_kv", 1024), ("kv_norm", 256), ("norm_b", 1024),
        ("q_norm", 384), ("final_norm", 1024))
REP_ROWS = 384
REP_SLICE = REP_ROWS // N_DEV
GRAD_ROWS_A = 384
GRAD_BLOCK = 192

WEIGHTS = ("norm_a", "w_in_a", "conv_w", "conv_b", "w_rg", "b_rg", "w_ig", "b_ig", "lru_lambda", "w_out_a",
           "norm_kv", "w_dkv", "kv_norm", "w_uk", "w_uv", "norm_b", "w_in_b", "q_norm", "w_uq", "w_out_b",
           "final_norm")


def _params(sem=None):
    return pltpu.CompilerParams(dimension_semantics=sem, vmem_limit_bytes=VMEM_LIMIT_BYTES)


_NT = (((1,), (1,)), ((), ()))
_ANY = pl.BlockSpec(memory_space=pl.ANY)


def _mesh_pos():
    return lax.axis_index("x"), lax.axis_index("y"), lax.axis_index("c")


def _sigmoid(z):
    return 0.5 * jnp.tanh(0.5 * z) + 0.5


def _sigmoid_tail(z):
    return 1.0 / (1.0 + jnp.exp(-z))


def _token_sums(operands, pairs, *, name, bt=1024):
    t = operands[0].shape[0]
    bt = min(bt, t)
    k = len(operands)

    def body(*refs):
        ins, outs = refs[:k], refs[k:]

        @pl.when(pl.program_id(0) == 0)
        def _():
            for o_ref in outs:
                o_ref[...] = jnp.zeros_like(o_ref)

        vals = [r[...].astype(BF16) for r in ins]
        for (i, j), o_ref in zip(pairs, outs):
            o_ref[...] += lax.dot_general(vals[i], vals[j], (((0,), (0,)), ((), ())), preferred_element_type=F32)

    shapes = [(operands[i].shape[1], operands[j].shape[1]) for i, j in pairs]
    return pl.pallas_call(
        body, grid=(t // bt,),
        in_specs=[pl.BlockSpec((bt, a.shape[1]), lambda s: (s, 0)) for a in operands],
        out_specs=[pl.BlockSpec(shape, lambda s: (0, 0)) for shape in shapes],
        out_shape=[jax.ShapeDtypeStruct(shape, F32) for shape in shapes],
        compiler_params=_params(("arbitrary",)), name=name)(*operands)


def _swap_halves(v):
    ax = v.ndim - 1
    lane = lax.broadcasted_iota(jnp.int32, v.shape, ax)
    up = pltpu.roll(v, LANES - QK_ROPE // 2, axis=ax)
    down = pltpu.roll(v, QK_ROPE // 2, axis=ax)
    return jnp.where(lane < QK_ROPE // 2, up, jnp.where(lane < QK_ROPE, down, 0.0))


def _rope(v, cos, sin):
    return v * cos + _swap_halves(v) * sin


def _rope_t(d, cos, sin):
    return d * cos + _swap_halves(d * sin)


def _rope_tables(seq):
    pos = jnp.arange(seq, dtype=F32)
    inv = ROPE_THETA ** (-jnp.arange(0, QK_ROPE, 2, dtype=F32) / QK_ROPE)
    ang = pos[:, None] * inv[None, :]
    cos, sin = jnp.cos(ang), jnp.sin(ang)
    zero = jnp.zeros((seq, LANES - QK_ROPE), F32)
    return jnp.concatenate([cos, cos, zero], axis=1), jnp.concatenate([-sin, sin, zero], axis=1)


def _rms(v):
    return v * lax.rsqrt(jnp.mean(v * v, axis=-1, keepdims=True) + EPS)


def _const_spec(a):
    return pl.BlockSpec(a.shape, lambda i: (0,) * a.ndim)


def _lru_proj_fwd(x, g_a, w_in_t, *, name, bt=512):
    t, d = x.shape
    bt = min(bt, t)
    n = w_in_t.shape[0] // 2

    def body(x_ref, g_ref, wt_ref, h_ref, xp_ref, ga_ref):
        h = (_rms(x_ref[...]) * g_ref[...]).astype(BF16)
        h_ref[...] = h
        xp_ref[...] = lax.dot_general(h, wt_ref[0:n, :], _NT, preferred_element_type=F32)
        ga_ref[...] = lax.dot_general(h, wt_ref[n:2 * n, :], _NT, preferred_element_type=F32)

    row = lambda w: pl.BlockSpec((bt, w), lambda i: (i, 0))
    return pl.pallas_call(
        body, grid=(t // bt,), in_specs=[row(d), _const_spec(g_a), _const_spec(w_in_t)],
        out_specs=[row(d), row(n), row(n)],
        out_shape=[jax.ShapeDtypeStruct((t, d), BF16), jax.ShapeDtypeStruct((t, n), F32), jax.ShapeDtypeStruct((t, n), F32)],
        compiler_params=_params(("parallel",)), name=name)(x, g_a, w_in_t)


def _mla_proj_fwd(y, x, gains, w, cos, sin, *, seq, name, bt=512):
    t, d = x.shape
    bt = min(bt, seq)
    per_seq = seq // bt
    g_kv, g_b, g_kvn, g_q = gains
    consts = [g_kv, g_b, g_kvn, g_q, w["w_dkv_c"], w["w_dkv_r"], w["w_in_b_t"], w["w_uk"], w["w_uv"], w["w_uq"],
              w["w_out_a"]]

    def body(y_ref, x_ref, cos_ref, sin_ref, gkv_ref, gb_ref, gkvn_ref, gq_ref, wdc_ref, wdr_ref, wbt_ref,
             wuk_ref, wuv_ref, wuq_ref, wo_ref, x1_ref,
             hk_ref, hq_ref, ck_ref, cqp_ref, g2_ref, ckv_ref, cq_ref, q_ref, kn_ref, v_ref, kr_ref, knt_ref, vt_ref, krt_ref):
        x1 = jnp.dot(y_ref[...], wo_ref[...], preferred_element_type=F32) + x_ref[...]
        x1_ref[...] = x1
        nrm = _rms(x1)
        hk = (nrm * gkv_ref[...]).astype(BF16)
        hq = (nrm * gb_ref[...]).astype(BF16)
        hk_ref[...] = hk
        hq_ref[...] = hq
        ck = jnp.dot(hk, wdc_ref[...], preferred_element_type=F32)
        ck_ref[...] = ck
        cqp = lax.dot_general(hq, wbt_ref[0:Q_RANK, :], _NT, preferred_element_type=F32)
        cqp_ref[...] = cqp
        g2_ref[...] = lax.dot_general(hq, wbt_ref[Q_RANK:, :], _NT, preferred_element_type=F32)
        cosv, sinv = cos_ref[...], sin_ref[...]
        kr = _rope(jnp.dot(hk, wdr_ref[...], preferred_element_type=F32), cosv, sinv)
        kr_ref[...] = kr.astype(BF16)
        krt_ref[...] = kr.T.astype(BF16)
        ckv = (_rms(ck) * gkvn_ref[...]).astype(BF16)
        ckv_ref[...] = ckv
        kn = jnp.dot(ckv, wuk_ref[...], preferred_element_type=F32)
        v = jnp.dot(ckv, wuv_ref[...], preferred_element_type=F32)
        kn_ref[...] = kn.astype(BF16)
        v_ref[...] = v.astype(BF16)
        knt_ref[...] = kn.T.astype(BF16)
        vt_ref[...] = v.T.astype(BF16)
        cq = (_rms(cqp) * gq_ref[...]).astype(BF16)
        cq_ref[...] = cq
        for h in range(N_HEADS):
            qh = jnp.dot(cq, wuq_ref[:, h * HEAD_PAD:(h + 1) * HEAD_PAD], preferred_element_type=F32)
            q_ref[:, h * HEAD_PAD:h * HEAD_PAD + QK_NOPE] = qh[:, :QK_NOPE].astype(BF16)
            q_ref[:, h * HEAD_PAD + QK_NOPE:(h + 1) * HEAD_PAD] = _rope(qh[:, QK_NOPE:], cosv, sinv).astype(BF16)

    row = lambda w_: pl.BlockSpec((bt, w_), lambda i: (i, 0))
    col = lambda h_: pl.BlockSpec((h_, bt), lambda i: (0, i))
    tab = pl.BlockSpec((bt, LANES), lambda i: (i % per_seq, 0))
    nh = N_HEADS * V_DIM
    shapes = [((t, d), F32), ((t, d), BF16), ((t, d), BF16), ((t, KV_RANK), F32), ((t, Q_RANK), F32), ((t, nh), F32),
              ((t, KV_RANK), BF16), ((t, Q_RANK), BF16), ((t, N_HEADS * HEAD_PAD), BF16), ((t, nh), BF16), ((t, nh), BF16),
              ((t, LANES), BF16), ((nh, t), BF16), ((nh, t), BF16), ((LANES, t), BF16)]
    out_specs = [row(d), row(d), row(d), row(KV_RANK), row(Q_RANK), row(nh), row(KV_RANK), row(Q_RANK),
                 row(N_HEADS * HEAD_PAD), row(nh), row(nh), row(LANES), col(nh), col(nh), col(LANES)]
    return pl.pallas_call(
        body, grid=(t // bt,), in_specs=[row(D_RNN), row(d), tab, tab] + [_const_spec(a) for a in consts],
        out_specs=out_specs, out_shape=[jax.ShapeDtypeStruct(s, dt) for s, dt in shapes],
        compiler_params=_params(("parallel",)), name=name)(y, x, cos, sin, *consts)


def _rms_bwd_rows(xv, dn):
    r = lax.rsqrt(jnp.mean(xv * xv, axis=-1, keepdims=True) + EPS)
    nrm = xv * r
    return r * (dn - nrm * jnp.mean(dn * nrm, axis=-1, keepdims=True)), nrm


def _col_sum(v):
    return jnp.sum(v, axis=0, keepdims=True)


def _lru_proj_bwd(dxp, dga, x, dx1, h0, g_a, w_in_t, *, name, bt=512):
    t, d = x.shape
    bt = min(bt, t)
    n = w_in_t.shape[0] // 2
    tn = (((0,), (0,)), ((), ()))

    def body(dxp_ref, dga_ref, x_ref, dx1_ref, h0_ref, g_ref, wt_ref, dx_ref, dg_ref, dwx_ref, dwg_ref):
        @pl.when(pl.program_id(0) == 0)
        def _():
            for ref in (dg_ref, dwx_ref, dwg_ref):
                ref[...] = jnp.zeros_like(ref)

        dxp_v, dga_v, h0 = dxp_ref[...], dga_ref[...], h0_ref[...]
        dwx_ref[...] += lax.dot_general(dxp_v, h0, tn, preferred_element_type=F32)
        dwg_ref[...] += lax.dot_general(dga_v, h0, tn, preferred_element_type=F32)
        dh = (jnp.dot(dxp_v, wt_ref[0:n, :], preferred_element_type=F32)
              + jnp.dot(dga_v, wt_ref[n:2 * n, :], preferred_element_type=F32))
        dxn, nrm = _rms_bwd_rows(x_ref[...], dh * g_ref[...])
        dg_ref[...] += _col_sum(dh * nrm)
        dx_ref[...] = dx1_ref[...] + dxn

    row = lambda w: pl.BlockSpec((bt, w), lambda i: (i, 0))
    whole = pl.BlockSpec((n, d), lambda i: (0, 0))
    return pl.pallas_call(
        body, grid=(t // bt,),
        in_specs=[row(n), row(n), row(d), row(d), row(d), _const_spec(g_a), _const_spec(w_in_t)],
        out_specs=[row(d), _const_spec(g_a), whole, whole],
        out_shape=[jax.ShapeDtypeStruct((t, d), F32), jax.ShapeDtypeStruct((1, d), F32),
                   jax.ShapeDtypeStruct((n, d), F32), jax.ShapeDtypeStruct((n, d), F32)],
        compiler_params=_params(("arbitrary",)), name=name)(dxp, dga, x, dx1, h0, g_a, w_in_t)


def _mla_proj_bwd(x1, dx2, cqp, ck, dq, dkn, dv, dkr, dg2, gains, w, *, name, bt=512):
    t, d = x1.shape
    bt = min(bt, t)
    g_kv, g_b, g_kvn, g_q = gains
    consts = [g_kv, g_b, g_kvn, g_q, w["w_dkv_c"], w["w_dkv_r"], w["w_in_b_t"], w["w_uk"], w["w_uv"], w["w_uq"],
              w["w_out_a"]]
    nh = N_HEADS * V_DIM

    def body(x1_ref, dx2_ref, cqp_ref, ck_ref, dq_ref, dkn_ref, dv_ref, dkr_ref, dg2_ref,
             gkv_ref, gb_ref, gkvn_ref, gq_ref, wdc_ref, wdr_ref, wbt_ref, wuk_ref, wuv_ref, wuq_ref, wo_ref,
             dx1_ref, du2_ref, dckr_ref, dgkv_ref, dgb_ref, dgkvn_ref, dgq_ref, dy_ref):
        @pl.when(pl.program_id(0) == 0)
        def _():
            for ref in (dgkv_ref, dgb_ref, dgkvn_ref, dgq_ref):
                ref[...] = jnp.zeros_like(ref)

        dot_nt = lambda a, b: lax.dot_general(a, b, _NT, preferred_element_type=F32)
        dcq = dot_nt(dq_ref[...], wuq_ref[...])
        dcqp, nq = _rms_bwd_rows(cqp_ref[...], dcq * gq_ref[...])
        dgq_ref[...] += _col_sum(dcq * nq)
        dcqp = dcqp.astype(BF16)
        dg2 = dg2_ref[...]
        du2_ref[:, :Q_RANK] = dcqp
        du2_ref[:, Q_RANK:] = dg2
        dhq = (jnp.dot(dcqp, wbt_ref[0:Q_RANK, :], preferred_element_type=F32)
               + jnp.dot(dg2, wbt_ref[Q_RANK:, :], preferred_element_type=F32))
        dckv = dot_nt(dkn_ref[...], wuk_ref[...]) + dot_nt(dv_ref[...], wuv_ref[...])
        dck, nc = _rms_bwd_rows(ck_ref[...], dckv * gkvn_ref[...])
        dgkvn_ref[...] += _col_sum(dckv * nc)
        dck = dck.astype(BF16)
        dkr = dkr_ref[...].astype(BF16)
        dckr_ref[:, :KV_RANK] = dck
        dckr_ref[:, KV_RANK:] = dkr
        dhk = dot_nt(dck, wdc_ref[...]) + dot_nt(dkr, wdr_ref[...])
        dxn, n1 = _rms_bwd_rows(x1_ref[...], dhq * gb_ref[...] + dhk * gkv_ref[...])
        dgb_ref[...] += _col_sum(dhq * n1)
        dgkv_ref[...] += _col_sum(dhk * n1)
        dx1 = dx2_ref[...] + dxn
        dx1_ref[...] = dx1
        dy_ref[...] = lax.dot_general(dx1.astype(BF16), wo_ref[...], _NT, preferred_element_type=F32)

    row = lambda w_: pl.BlockSpec((bt, w_), lambda i: (i, 0))
    vec = lambda w_: pl.BlockSpec((1, w_), lambda i: (0, 0))
    in_specs = [row(d), row(d), row(Q_RANK), row(KV_RANK), row(N_HEADS * HEAD_PAD), row(nh), row(nh), row(LANES), row(nh)]
    return pl.pallas_call(
        body, grid=(t // bt,), in_specs=in_specs + [_const_spec(a) for a in consts],
        out_specs=[row(d), row(Q_RANK + nh), row(KV_RANK + LANES), vec(d), vec(d), vec(KV_RANK), vec(Q_RANK), row(D_RNN)],
        out_shape=[jax.ShapeDtypeStruct((t, d), F32), jax.ShapeDtypeStruct((t, Q_RANK + nh), BF16),
                   jax.ShapeDtypeStruct((t, KV_RANK + LANES), BF16), jax.ShapeDtypeStruct((1, d), F32),
                   jax.ShapeDtypeStruct((1, d), F32), jax.ShapeDtypeStruct((1, KV_RANK), F32),
                   jax.ShapeDtypeStruct((1, Q_RANK), F32), jax.ShapeDtypeStruct((t, D_RNN), F32)],
        compiler_params=_params(("arbitrary",)), name=name)(x1, dx2, cqp, ck, dq, dkn, dv, dkr, dg2, *consts)


def _softplus(z):
    return jnp.maximum(z, 0.0) + jnp.log1p(jnp.exp(-jnp.abs(z)))


def _one_minus_square(a, la):
    return jnp.tanh(-la) * (1.0 + a * a)


def _gates(xb, wrg, wig, brg, big, sp):
    xbb = xb.astype(BF16)
    r = _sigmoid_tail(jnp.dot(xbb, wrg, preferred_element_type=F32) + brg)
    i = _sigmoid(jnp.dot(xbb, wig, preferred_element_type=F32) + big)
    la = (-LRU_C) * r * sp
    a = jnp.exp(la)
    em = _one_minus_square(a, la)
    inv_mult = lax.rsqrt(em)
    mult = jnp.where(em > 0.0, em * inv_mult, 0.0)
    return r, i, a, mult, inv_mult


def _conv(xpad_ref, cw_ref, seq):
    acc = cw_ref[0:1, :] * xpad_ref[pl.ds(8 - (CONV_WIDTH - 1), seq), :]
    for k in range(1, CONV_WIDTH):
        acc = acc + cw_ref[k:k + 1, :] * xpad_ref[pl.ds(8 - (CONV_WIDTH - 1) + k, seq), :]
    return acc


def _seq_spec(seq):
    return pl.BlockSpec((None, seq, RNN_BW), lambda n, b: (b, 0, n))


def _chan_spec(rows):
    return pl.BlockSpec((rows, RNN_BW), lambda n, b: (0, n))


_GATE_W_SPEC = pl.BlockSpec((None, RNN_BW, RNN_BW), lambda n, b: (n, 0, 0))


SCAN_UNROLL = 4


def _peers():
    x, y, c = _mesh_pos()
    others = []
    for k in range(1, N_DEV):
        px = 1 - x if k & 4 else x
        py = 1 - y if k & 2 else y
        pc = 1 - c if k & 1 else c
        others.append(((px, py, pc), 4 * px + 2 * py + pc))
    return 4 * x + 2 * y + c, others


def _exchange(src_ref, dst_ref, send_sems, recv_sems, local_sem, *, finish, gather=False):
    me, others = _peers()

    def send(k, dev, slot):
        return pltpu.make_async_remote_copy(
            src_ref=src_ref if gather else src_ref.at[slot], dst_ref=dst_ref.at[me], send_sem=send_sems.at[k],
            recv_sem=recv_sems.at[k], device_id=dev, device_id_type=pl.DeviceIdType.MESH)

    local = pltpu.make_async_copy(src_ref if gather else src_ref.at[me], dst_ref.at[me], local_sem)
    if not finish:
        local.start()
        for k, (dev, slot) in enumerate(others):
            send(k, dev, slot).start()
        return
    for k, (dev, slot) in enumerate(others):
        pltpu.make_async_remote_copy(
            src_ref=dst_ref.at[slot], dst_ref=dst_ref.at[slot], send_sem=send_sems.at[k], recv_sem=recv_sems.at[k],
            device_id=dev, device_id_type=pl.DeviceIdType.MESH).wait_recv()
    for k, (dev, slot) in enumerate(others):
        send(k, dev, slot).wait_send()
    local.wait()


def _gather_two_level(x_ref, out_ref, send_sems, recv_sems, local_sem, *, phase):
    x, y, c = _mesh_pos()
    me, sibling = (x, y, c), (x, y, 1 - c)
    chips = [(1 - x, y), (x, 1 - y), (1 - x, 1 - y)]

    def slot(px, py, pc):
        return out_ref.at[4 * px + 2 * py + pc]

    def copy(k, blk, to, src=None):
        return pltpu.make_async_remote_copy(
            src_ref=slot(*blk) if src is None else src, dst_ref=slot(*blk),
            send_sem=send_sems.at[k], recv_sem=recv_sems.at[k], device_id=to, device_id_type=pl.DeviceIdType.MESH)

    if phase == 0:
        pltpu.make_async_copy(x_ref, slot(*me), local_sem).start()
        copy(0, me, sibling, src=x_ref).start()
        for j, chip in enumerate(chips):
            copy(1 + j, me, (*chip, c), src=x_ref).start()
    elif phase == 1:
        for j, chip in enumerate(chips):
            copy(1 + j, (*chip, c), me).wait_recv()
            copy(4 + j, (*chip, c), sibling).start()
    else:
        copy(0, sibling, me).wait_recv()
        for j, chip in enumerate(chips):
            copy(4 + j, (*chip, 1 - c), me).wait_recv()
        copy(0, me, sibling, src=x_ref).wait_send()
        for j, chip in enumerate(chips):
            copy(1 + j, me, (*chip, c), src=x_ref).wait_send()
            copy(4 + j, (*chip, c), sibling).wait_send()
        pltpu.make_async_copy(x_ref, slot(*me), local_sem).wait()


GATHER_FORWARD_STEP = 9
_EXCHANGE_SEMS = [pltpu.SemaphoreType.DMA((N_DEV - 1,)), pltpu.SemaphoreType.DMA((N_DEV - 1,)), pltpu.SemaphoreType.DMA(())]


def _first_last(steps):
    first = last = None
    for axis, n in enumerate(steps):
        i = pl.program_id(axis)
        first = (i == 0) if first is None else first & (i == 0)
        last = (i == n - 1) if last is None else last & (i == n - 1)
    return first, last


def _lru_fwd(xp, ga, cw, vecs, wrg, wig, block, *, name):
    bsz, seq, _ = xp.shape
    groups = seq // 8

    def body(xp_ref, ga_ref, cw_ref, vec_ref, wrg_ref, wig_ref, blk_ref, xb_ref, hs_ref, y_ref, all_ref,
             xpad, a_s, b_s, send_sems, recv_sems, local_sem):
        first, last = _first_last((RNN_BLOCKS, bsz))

        @pl.when(first)
        def _():
            _gather_two_level(blk_ref, all_ref, send_sems, recv_sems, local_sem, phase=0)

        @pl.when((pl.program_id(0) == GATHER_FORWARD_STEP) & (pl.program_id(1) == 0))
        def _():
            _gather_two_level(blk_ref, all_ref, send_sems, recv_sems, local_sem, phase=1)

        xpad[0:8, :] = jnp.zeros((8, RNN_BW), F32)
        xpad[pl.ds(8, seq), :] = xp_ref[...]
        xb = _conv(xpad, cw_ref, seq) + vec_ref[0:1, :]
        xb_ref[...] = xb
        sp = _softplus(-vec_ref[3:4, :])
        _, i, a, mult, _ = _gates(xb, wrg_ref[...], wig_ref[...], vec_ref[1:2, :], vec_ref[2:3, :], sp)
        a_s[...] = a
        b_s[...] = mult * (i * xb)
        row = lax.broadcasted_iota(jnp.int32, (8, RNN_BW), 0)

        def group(g, h):
            r0 = pl.multiple_of(g * 8, 8)
            av = a_s[pl.ds(r0, 8), :]
            bv = b_s[pl.ds(r0, 8), :]
            for k in (1, 2, 4):
                m = row >= k
                bv = jnp.where(m, av * pltpu.roll(bv, k, axis=0) + bv, bv)
                av = jnp.where(m, av * pltpu.roll(av, k, axis=0), av)
            hs_ref[pl.ds(r0, 8), :] = av * h + bv
            return av[7:8, :] * h + bv[7:8, :]

        def groups_of(i, h):
            for u in range(SCAN_UNROLL):
                h = group(i * SCAN_UNROLL + u, h)
            return h

        lax.fori_loop(0, groups // SCAN_UNROLL, groups_of, jnp.zeros((1, RNN_BW), F32))
        gav = ga_ref[...]
        y_ref[...] = (hs_ref[...] * (gav * _sigmoid(gav))).astype(BF16)

        @pl.when(last)
        def _():
            _gather_two_level(blk_ref, all_ref, send_sems, recv_sems, local_sem, phase=2)

    sq = _seq_spec(seq)
    shape = (bsz, seq, D_RNN)
    return pl.pallas_call(
        body, grid=(RNN_BLOCKS, bsz),
        in_specs=[sq, sq, _chan_spec(8), _chan_spec(8), _GATE_W_SPEC, _GATE_W_SPEC, _ANY],
        out_specs=[sq, sq, sq, _ANY],
        out_shape=[jax.ShapeDtypeStruct(shape, F32), jax.ShapeDtypeStruct(shape, F32), jax.ShapeDtypeStruct(shape, BF16),
                   jax.ShapeDtypeStruct((N_DEV,) + block.shape, block.dtype)],
        scratch_shapes=[pltpu.VMEM((seq + 8, RNN_BW), F32), pltpu.VMEM((seq, RNN_BW), F32), pltpu.VMEM((seq, RNN_BW), F32)]
        + _EXCHANGE_SEMS,
        compiler_params=_params(("arbitrary", "arbitrary")), name=name)(xp, ga, cw, vecs, wrg, wig, block)


def _lru_bwd(dy, xp, xb, hs, ga, cw, vecs, wrg, wig, parts, *, name):
    bsz, seq, _ = xp.shape
    groups = seq // 8

    def body(dy_ref, xp_ref, xb_ref, hs_ref, ga_ref, cw_ref, vec_ref, wrg_ref, wig_ref,
             parts_ref, dxp_ref, dga_ref, dwrg_ref, dwig_ref, dvec_ref, land_ref, pad, a_s, d_s, lam_s,
             send_sems, recv_sems, local_sem):
        first, last = _first_last((RNN_BLOCKS, bsz))

        @pl.when(first)
        def _():
            _exchange(parts_ref, land_ref, send_sems, recv_sems, local_sem, finish=False)

        @pl.when(pl.program_id(1) == 0)
        def _():
            dwrg_ref[...] = jnp.zeros_like(dwrg_ref)
            dwig_ref[...] = jnp.zeros_like(dwig_ref)
            dvec_ref[...] = jnp.zeros_like(dvec_ref)

        xb = xb_ref[...]
        hs = hs_ref[...]
        gav = ga_ref[...]
        dy = dy_ref[...]
        sp = _softplus(-vec_ref[3:4, :])
        wrg = wrg_ref[...]
        wig = wig_ref[...]
        r, i, a, mult, inv_mult = _gates(xb, wrg, wig, vec_ref[1:2, :], vec_ref[2:3, :], sp)
        sg = _sigmoid(gav)
        dga_ref[...] = (dy * hs * (sg * (1.0 + gav * (1.0 - sg)))).astype(BF16)
        d_s[...] = dy * (gav * sg)

        pad[pl.ds(0, seq), :] = a
        pad[pl.ds(seq, 8), :] = jnp.zeros((8, RNN_BW), F32)
        a_s[...] = pad[pl.ds(1, seq), :]
        row = lax.broadcasted_iota(jnp.int32, (8, RNN_BW), 0)

        def group(g, nxt):
            r0 = pl.multiple_of((groups - 1 - g) * 8, 8)
            cv = a_s[pl.ds(r0, 8), :]
            bv = d_s[pl.ds(r0, 8), :]
            for k in (1, 2, 4):
                m = row < 8 - k
                bv = jnp.where(m, cv * pltpu.roll(bv, 8 - k, axis=0) + bv, bv)
                cv = jnp.where(m, cv * pltpu.roll(cv, 8 - k, axis=0), cv)
            lam_s[pl.ds(r0, 8), :] = cv * nxt + bv
            return cv[0:1, :] * nxt + bv[0:1, :]

        def groups_of(i, nxt):
            for u in range(SCAN_UNROLL):
                nxt = group(i * SCAN_UNROLL + u, nxt)
            return nxt

        lax.fori_loop(0, groups // SCAN_UNROLL, groups_of, jnp.zeros((1, RNN_BW), F32))
        dh = lam_s[...]

        pad[0:8, :] = jnp.zeros((8, RNN_BW), F32)
        pad[pl.ds(8, seq), :] = hs
        da = dh * pad[pl.ds(7, seq), :]
        ixb = i * xb
        dixb = dh * mult
        dla = da * a - (dh * ixb) * (a * a) * inv_mult
        drp = (dla * ((-LRU_C) * sp)) * r * (1.0 - r)
        dip = (dixb * xb) * i * (1.0 - i)
        dvec_ref[0:1, :] += jnp.sum(drp, axis=0, keepdims=True)
        dvec_ref[1:2, :] += jnp.sum(dip, axis=0, keepdims=True)
        dvec_ref[2:3, :] += jnp.sum(dla * ((-LRU_C) * r), axis=0, keepdims=True)
        drpb = drp.astype(BF16)
        dipb = dip.astype(BF16)
        xbb = xb.astype(BF16)
        nt = (((1,), (1,)), ((), ()))
        tn = (((0,), (0,)), ((), ()))
        dxb = (dixb * i
               + lax.dot_general(drpb, wrg, nt, preferred_element_type=F32)
               + lax.dot_general(dipb, wig, nt, preferred_element_type=F32))
        dwrg_ref[...] += lax.dot_general(xbb, drpb, tn, preferred_element_type=F32)
        dwig_ref[...] += lax.dot_general(xbb, dipb, tn, preferred_element_type=F32)
        dvec_ref[3:4, :] += jnp.sum(dxb, axis=0, keepdims=True)

        pad[pl.ds(0, seq), :] = dxb
        pad[pl.ds(seq, 8), :] = jnp.zeros((8, RNN_BW), F32)
        dxp = cw_ref[0:1, :] * pad[pl.ds(CONV_WIDTH - 1, seq), :]
        for k in range(1, CONV_WIDTH):
            dxp = dxp + cw_ref[k:k + 1, :] * pad[pl.ds(CONV_WIDTH - 1 - k, seq), :]
        dxp_ref[...] = dxp.astype(BF16)
        pad[0:8, :] = jnp.zeros((8, RNN_BW), F32)
        pad[pl.ds(8, seq), :] = xp_ref[...]
        for k in range(CONV_WIDTH):
            dvec_ref[4 + k:5 + k, :] += jnp.sum(dxb * pad[pl.ds(8 - (CONV_WIDTH - 1) + k, seq), :], axis=0, keepdims=True)

        @pl.when(last)
        def _():
            _exchange(parts_ref, land_ref, send_sems, recv_sems, local_sem, finish=True)

    sq = _seq_spec(seq)
    shape = (bsz, seq, D_RNN)
    gshape = (RNN_BLOCKS, RNN_BW, RNN_BW)
    return pl.pallas_call(
        body, grid=(RNN_BLOCKS, bsz),
        in_specs=[sq, sq, sq, sq, sq, _chan_spec(8), _chan_spec(8), _GATE_W_SPEC, _GATE_W_SPEC, _ANY],
        out_specs=[sq, sq, _GATE_W_SPEC, _GATE_W_SPEC, _chan_spec(8), _ANY],
        out_shape=[jax.ShapeDtypeStruct(shape, BF16), jax.ShapeDtypeStruct(shape, BF16),
                   jax.ShapeDtypeStruct(gshape, F32), jax.ShapeDtypeStruct(gshape, F32),
                   jax.ShapeDtypeStruct((8, D_RNN), F32), jax.ShapeDtypeStruct(parts.shape, parts.dtype)],
        scratch_shapes=[pltpu.VMEM((seq + 8, RNN_BW), F32), pltpu.VMEM((seq, RNN_BW), F32),
                        pltpu.VMEM((seq, RNN_BW), F32), pltpu.VMEM((seq, RNN_BW), F32)] + _EXCHANGE_SEMS,
        compiler_params=_params(("arbitrary", "arbitrary")), name=name)(dy, xp, xb, hs, ga, cw, vecs, wrg, wig, parts)


def _attn_block(seq):
    return min(512, seq)


def _diag_mask(blk):
    return lax.broadcasted_iota(jnp.int32, (blk, blk), 0) <= lax.broadcasted_iota(jnp.int32, (blk, blk), 1)


FWD_HEADS = 8
BWD_HEADS = 2


def _attn_fwd(q, kn, kr, v_t, block, *, bsz, seq, name):
    t = bsz * seq
    blk = _attn_block(seq)
    nq = seq // blk
    hg = FWD_HEADS
    steps = (bsz, N_HEADS // hg, nq)

    def body(q_ref, kn_ref, kr_ref, vt_ref, blk_ref, o_ref, lse_ref, all_ref, acc, send_sems, recv_sems, local_sem):
        first, last = _first_last(steps)

        @pl.when(first)
        def _():
            _exchange(blk_ref, all_ref, send_sems, recv_sems, local_sem, finish=False, gather=True)

        qi = pl.program_id(2)
        acc[...] = jnp.zeros_like(acc)

        def step(j, carry, diagonal):
            k0 = pl.multiple_of(j * blk, blk)
            kr_j = kr_ref[pl.ds(k0, blk), :]
            out = []
            for h in range(hg):
                m_i, l_i = carry[h]
                kv = jnp.concatenate([kn_ref[pl.ds(k0, blk), h * QK_NOPE:(h + 1) * QK_NOPE], kr_j], axis=1)
                qv = q_ref[:, h * HEAD_PAD:(h + 1) * HEAD_PAD]
                s = lax.dot_general(kv, qv, _NT, preferred_element_type=F32) * ATTN_SCALE
                if diagonal:
                    s = jnp.where(_diag_mask(blk), s, -jnp.inf)
                m_new = jnp.maximum(m_i, jnp.max(s, axis=0, keepdims=True))
                p = jnp.exp(s - m_new)
                alpha = jnp.exp(m_i - m_new)
                l_new = alpha * l_i + jnp.sum(p, axis=0, keepdims=True)
                acc[h] = alpha * acc[h] + jnp.dot(vt_ref[h * V_DIM:(h + 1) * V_DIM, pl.ds(k0, blk)], p.astype(BF16),
                                                  preferred_element_type=F32)
                out.append((m_new, l_new))
            return tuple(out)

        init = tuple((jnp.full((1, blk), -jnp.inf, F32), jnp.zeros((1, blk), F32)) for _ in range(hg))
        carry = lax.fori_loop(0, qi, lambda j, c: step(j, c, False), init)
        stats = step(qi, carry, True)
        for h in range(hg):
            m_i, l_i = stats[h]
            o_ref[:, h * V_DIM:(h + 1) * V_DIM] = (acc[h] / l_i).T
            lse_ref[h] = m_i + jnp.log(l_i)

        @pl.when(last)
        def _():
            _exchange(blk_ref, all_ref, send_sems, recv_sems, local_sem, finish=True, gather=True)

    return pl.pallas_call(
        body, grid=steps,
        in_specs=[pl.BlockSpec((blk, hg * HEAD_PAD), lambda b, g, i: (b * nq + i, g)),
                  pl.BlockSpec((seq, hg * QK_NOPE), lambda b, g, i: (b, g)),
                  pl.BlockSpec((seq, LANES), lambda b, g, i: (b, 0)),
                  pl.BlockSpec((hg * V_DIM, seq), lambda b, g, i: (g, b)), _ANY],
        out_specs=[pl.BlockSpec((blk, hg * V_DIM), lambda b, g, i: (b * nq + i, g)),
                   pl.BlockSpec((hg, 1, blk), lambda b, g, i: (g, 0, b * nq + i)), _ANY],
        out_shape=[jax.ShapeDtypeStruct((t, N_HEADS * V_DIM), F32), jax.ShapeDtypeStruct((N_HEADS, 1, t), F32),
                   jax.ShapeDtypeStruct((N_DEV,) + block.shape, block.dtype)],
        scratch_shapes=[pltpu.VMEM((hg, V_DIM, blk), F32)] + _EXCHANGE_SEMS,
        compiler_params=_params(("arbitrary", "arbitrary", "arbitrary")), name=name)(q, kn, kr, v_t, block)


def _attn_bwd(q, kn, kr, kn_t, kr_t, v, o, lse, do, cos, sin, parts, *, bsz, seq, name):
    t = bsz * seq
    blk = _attn_block(seq)
    nq = seq // blk
    hg = BWD_HEADS
    steps = (bsz, N_HEADS // hg)

    def body(q_ref, kn_ref, kr_ref, knt_ref, krt_ref, v_ref, o_ref, lse_ref, do_ref, cos_ref, sin_ref, parts_ref,
             dq_ref, dkn_ref, dkr_ref, dv_ref, land_ref, dqt_acc, dk_acc, dv_acc, send_sems, recv_sems, local_sem):
        first, last = _first_last(steps)

        @pl.when(first)
        def _():
            _exchange(parts_ref, land_ref, send_sems, recv_sems, local_sem, finish=False)

        dqt_acc[...] = jnp.zeros_like(dqt_acc)
        dk_acc[...] = jnp.zeros_like(dk_acc)
        dv_acc[...] = jnp.zeros_like(dv_acc)

        def q_block(i, _):
            q0 = pl.multiple_of(i * blk, blk)
            rows = []
            for h in range(hg):
                dov = do_ref[pl.ds(q0, blk), h * V_DIM:(h + 1) * V_DIM].astype(F32)
                dcol = jnp.sum(dov * o_ref[pl.ds(q0, blk), h * V_DIM:(h + 1) * V_DIM], axis=-1, keepdims=True)
                delta = jnp.broadcast_to(dcol, (blk, LANES)).T[0:1, :]
                rows.append((lse_ref[h, :, pl.ds(q0, blk)], delta))

            def pair(j, diagonal):
                k0 = pl.multiple_of(j * blk, blk)
                kr_j = kr_ref[pl.ds(k0, blk), :]
                krt_j = krt_ref[:, pl.ds(k0, blk)]
                for h in range(hg):
                    lse_i, delta = rows[h]
                    qv = q_ref[pl.ds(q0, blk), h * HEAD_PAD:(h + 1) * HEAD_PAD]
                    dov = do_ref[pl.ds(q0, blk), h * V_DIM:(h + 1) * V_DIM]
                    kv = jnp.concatenate([kn_ref[pl.ds(k0, blk), h * QK_NOPE:(h + 1) * QK_NOPE], kr_j], axis=1)
                    s = lax.dot_general(kv, qv, _NT, preferred_element_type=F32) * ATTN_SCALE
                    p = jnp.exp(s - lse_i)
                    if diagonal:
                        p = jnp.where(_diag_mask(blk), p, 0.0)
                    dv_acc[pl.ds(k0, blk), h * V_DIM:(h + 1) * V_DIM] += jnp.dot(
                        p.astype(BF16), dov, preferred_element_type=F32)
                    dp = lax.dot_general(v_ref[pl.ds(k0, blk), h * V_DIM:(h + 1) * V_DIM], dov, _NT,
                                         preferred_element_type=F32)
                    ds = (p * (dp - delta) * ATTN_SCALE).astype(BF16)
                    dk_acc[pl.ds(k0, blk), h * HEAD_PAD:(h + 1) * HEAD_PAD] += jnp.dot(ds, qv, preferred_element_type=F32)
                    base = h * HEAD_PAD
                    dqt_acc[base:base + QK_NOPE, pl.ds(q0, blk)] += jnp.dot(
                        knt_ref[h * QK_NOPE:(h + 1) * QK_NOPE, pl.ds(k0, blk)], ds, preferred_element_type=F32)
                    dqt_acc[base + QK_NOPE:base + HEAD_PAD, pl.ds(q0, blk)] += jnp.dot(
                        krt_j, ds, preferred_element_type=F32)

            def off_diagonal(j, _):
                pair(j, False)
                return 0

            lax.fori_loop(0, i, off_diagonal, 0)
            pair(i, True)
            return 0

        lax.fori_loop(0, nq, q_block, 0)
        dkr = jnp.zeros((seq, LANES), F32)
        for h in range(hg):
            base = h * HEAD_PAD
            for i in range(nq):
                rows = slice(i * blk, (i + 1) * blk)
                dq = dqt_acc[base:base + HEAD_PAD, rows].T
                dq_ref[rows, base:base + QK_NOPE] = dq[:, :QK_NOPE].astype(BF16)
                dq_ref[rows, base + QK_NOPE:base + HEAD_PAD] = _rope_t(
                    dq[:, QK_NOPE:], cos_ref[rows, :], sin_ref[rows, :]).astype(BF16)
            dkn_ref[:, h * QK_NOPE:(h + 1) * QK_NOPE] = dk_acc[:, base:base + QK_NOPE].astype(BF16)
            dkr = dkr + dk_acc[:, base + QK_NOPE:base + HEAD_PAD]
        dv_ref[...] = dv_acc[...].astype(BF16)

        @pl.when(pl.program_id(1) == 0)
        def _():
            dkr_ref[...] = jnp.zeros_like(dkr_ref)

        dkr_ref[...] += _rope_t(dkr, cos_ref[...], sin_ref[...])

        @pl.when(last)
        def _():
            _exchange(parts_ref, land_ref, send_sems, recv_sems, local_sem, finish=True)

    head = pl.BlockSpec((seq, hg * V_DIM), lambda b, g: (b, g))
    head_t = pl.BlockSpec((hg * V_DIM, seq), lambda b, g: (g, b))
    shared = pl.BlockSpec((seq, LANES), lambda b, g: (b, 0))
    shared_t = pl.BlockSpec((LANES, seq), lambda b, g: (0, b))
    table = pl.BlockSpec((seq, LANES), lambda b, g: (0, 0))
    qspec = pl.BlockSpec((seq, hg * HEAD_PAD), lambda b, g: (b, g))
    return pl.pallas_call(
        body, grid=steps,
        in_specs=[qspec, head, shared, head_t, shared_t, head, head,
                  pl.BlockSpec((hg, 1, seq), lambda b, g: (g, 0, b)), head, table, table, _ANY],
        out_specs=[qspec, head, shared, head, _ANY],
        out_shape=[jax.ShapeDtypeStruct((t, N_HEADS * HEAD_PAD), BF16), jax.ShapeDtypeStruct((t, N_HEADS * QK_NOPE), BF16),
                   jax.ShapeDtypeStruct((t, LANES), F32), jax.ShapeDtypeStruct((t, N_HEADS * V_DIM), BF16),
                   jax.ShapeDtypeStruct(parts.shape, parts.dtype)],
        scratch_shapes=[pltpu.VMEM((hg * HEAD_PAD, seq), F32), pltpu.VMEM((seq, hg * HEAD_PAD), F32),
                        pltpu.VMEM((seq, hg * V_DIM), F32)] + _EXCHANGE_SEMS,
        compiler_params=_params(("arbitrary", "arbitrary")), name=name)(
            q, kn, kr, kn_t, kr_t, v, o, lse, do, cos, sin, parts)


def _head_and_loss(o, g2, x1, target, w_out, g_final, *, name, bt=512):
    t, d = x1.shape
    bt = min(bt, t)
    nt = (((1,), (1,)), ((), ()))
    tn = (((0,), (0,)), ((), ()))

    def body(o_ref, g2_ref, x1_ref, tgt_ref, w_ref, gf_ref, loss_ref, dx2_ref, dw_ref, do_ref, dg2_ref, dgf_ref):
        @pl.when(pl.program_id(0) == 0)
        def _():
            for ref in (loss_ref, dgf_ref, dw_ref):
                ref[...] = jnp.zeros_like(ref)

        ov = o_ref[...]
        gv = g2_ref[...]
        sg = _sigmoid(gv)
        silu = gv * sg
        y2 = (ov * silu).astype(BF16)
        w = w_ref[...]
        x2 = x1_ref[...] + jnp.dot(y2, w, preferred_element_type=F32)
        r = lax.rsqrt(jnp.mean(x2 * x2, axis=-1, keepdims=True) + EPS)
        nrm = x2 * r
        gf = gf_ref[...]
        err = nrm * gf - tgt_ref[...]
        loss_ref[...] += 0.5 * jnp.sum(jnp.mean(err * err, axis=-1, keepdims=True))
        dyf = err * (1.0 / d)
        dgf_ref[...] += jnp.sum(dyf * nrm, axis=0, keepdims=True)
        dn = dyf * gf
        dx2 = r * (dn - nrm * jnp.mean(dn * nrm, axis=-1, keepdims=True))
        dx2_ref[...] = dx2
        dx2 = dx2.astype(BF16)
        dw_ref[...] += lax.dot_general(y2, dx2, tn, preferred_element_type=F32)
        dy2 = lax.dot_general(dx2, w, nt, preferred_element_type=F32)
        do_ref[...] = (dy2 * silu).astype(BF16)
        dg2_ref[...] = (dy2 * ov * (sg * (1.0 + gv * (1.0 - sg)))).astype(BF16)

    row = pl.BlockSpec((bt, d), lambda i: (i, 0))
    vec = pl.BlockSpec((1, d), lambda i: (0, 0))
    return pl.pallas_call(
        body, grid=(t // bt,),
        in_specs=[row, row, row, row, pl.BlockSpec((d, d), lambda i: (0, 0)), vec],
        out_specs=[pl.BlockSpec((8, LANES), lambda i: (0, 0)), row, pl.BlockSpec((d, d), lambda i: (0, 0)), row, row, vec],
        out_shape=[jax.ShapeDtypeStruct((8, LANES), F32), jax.ShapeDtypeStruct((t, d), F32),
                   jax.ShapeDtypeStruct((d, d), F32), jax.ShapeDtypeStruct((t, d), BF16),
                   jax.ShapeDtypeStruct((t, d), BF16), jax.ShapeDtypeStruct((1, d), F32)],
        compiler_params=_params(("arbitrary",)), name=name)(o, g2, x1, target, w_out, g_final)


def _sum_parts(parts, *, name, br=GRAD_BLOCK):
    npart, rows, w = parts.shape

    def body(p_ref, o_ref):
        acc = p_ref[0].astype(F32)
        for j in range(1, npart):
            acc = acc + p_ref[j].astype(F32)
        o_ref[...] = acc

    return pl.pallas_call(
        body, grid=(rows // br,), in_specs=[pl.BlockSpec((npart, br, w), lambda i: (0, i, 0))],
        out_specs=pl.BlockSpec((br, w), lambda i: (i, 0)), out_shape=jax.ShapeDtypeStruct((rows, w), F32),
        compiler_params=_params(("parallel",)), name=name)(parts)


def _chip_partial(parts, recv, *, name, br=GRAD_BLOCK):
    _, rows, w = parts.shape
    core = lax.axis_index("c").astype(jnp.int32).reshape(1)

    def body(c_ref, p_ref, r_ref, o_ref):
        o_ref[...] = (p_ref[...] + r_ref[...]).astype(BF16)

    grid_spec = pltpu.PrefetchScalarGridSpec(
        num_scalar_prefetch=1, grid=(4, rows // br),
        in_specs=[pl.BlockSpec((None, br, w), lambda k, i, c_ref: (2 * k + c_ref[0], i, 0)),
                  pl.BlockSpec((None, br, w), lambda k, i, c_ref: (k, i, 0))],
        out_specs=pl.BlockSpec((None, br, w), lambda k, i, c_ref: (k, i, 0)))
    return pl.pallas_call(
        body, grid_spec=grid_spec, out_shape=jax.ShapeDtypeStruct((4, rows, w), BF16),
        compiler_params=_params(("parallel", "parallel")), name=name)(core, parts, recv)


def _as_block(a):
    if a.ndim == 1:
        return a.reshape(1, -1)
    if a.ndim > 2 and a.shape[0] == 1:
        return a.reshape(a.shape[1:])
    return a


def _adamw(grads, weights, mom1, mom2, *, name):
    k = len(weights)
    shapes = [w.shape for w in weights]
    args = [_as_block(a) for group in (grads, weights, mom1, mom2) for a in group]

    def body(*refs):
        for i in range(k):
            g_ref, w_ref, m_ref, v_ref, d_ref, nm_ref, nv_ref = (refs[j * k + i] for j in range(7))
            gv = g_ref[...]
            nm = ADAM_B1 * m_ref[...] + (1.0 - ADAM_B1) * gv
            nv = ADAM_B2 * v_ref[...] + (1.0 - ADAM_B2) * (gv * gv)
            nm_ref[...] = nm
            nv_ref[...] = nv
            m_hat = nm / (1.0 - ADAM_B1 ** ADAM_STEP)
            v_hat = nv / (1.0 - ADAM_B2 ** ADAM_STEP)
            d_ref[...] = (-ADAM_LR) * (m_hat / (jnp.sqrt(v_hat) + ADAM_EPS) + ADAM_WD * w_ref[...])

    whole = pl.BlockSpec(memory_space=pltpu.VMEM)
    outs = pl.pallas_call(
        body, in_specs=[whole] * (4 * k), out_specs=[whole] * (3 * k),
        out_shape=[jax.ShapeDtypeStruct(a.shape, F32) for a in args[k:2 * k]] * 3,
        compiler_params=_params(), name=name)(*args)
    return [[o.reshape(s) for o, s in zip(outs[j * k:(j + 1) * k], shapes)] for j in range(3)]


def _all_gather(block, *, name):
    m, n = block.shape

    def body(x_ref, out_ref, send_sems, recv_sems, local_sem):
        for phase in range(3):
            _gather_two_level(x_ref, out_ref, send_sems, recv_sems, local_sem, phase=phase)

    return pl.pallas_call(
        body, out_shape=jax.ShapeDtypeStruct((N_DEV, m, n), block.dtype), in_specs=[_ANY], out_specs=_ANY,
        scratch_shapes=_EXCHANGE_SEMS, name=name)(block)


def _exchange_d2d(parts, *, name):
    _, rows, w = parts.shape

    def body(p_ref, land_ref, send_sems, recv_sems):
        x, y, c = _mesh_pos()
        sends = []
        for k in range(4):
            cp = pltpu.make_async_remote_copy(
                src_ref=p_ref.at[2 * k + (1 - c)], dst_ref=land_ref.at[k], send_sem=send_sems.at[k],
                recv_sem=recv_sems.at[k], device_id=(x, y, 1 - c), device_id_type=pl.DeviceIdType.MESH)
            cp.start()
            sends.append(cp)
        for cp in sends:
            cp.wait_recv()
        for cp in sends:
            cp.wait_send()

    return pl.pallas_call(
        body, out_shape=jax.ShapeDtypeStruct((4, rows, w), parts.dtype), in_specs=[_ANY], out_specs=_ANY,
        scratch_shapes=[pltpu.SemaphoreType.DMA((4,)), pltpu.SemaphoreType.DMA((4,))], name=name)(parts)


def _exchange_ici(parts, *, name):
    def body(p_ref, land_ref, send_sems, recv_sems, local_sem):
        x, y, c = _mesh_pos()
        mine = pltpu.make_async_copy(p_ref.at[2 * x + y], land_ref.at[3], local_sem)
        mine.start()
        sends = []
        for k, (px, py) in enumerate([(1 - x, y), (x, 1 - y), (1 - x, 1 - y)]):
            cp = pltpu.make_async_remote_copy(
                src_ref=p_ref.at[2 * px + py], dst_ref=land_ref.at[k], send_sem=send_sems.at[k],
                recv_sem=recv_sems.at[k], device_id=(px, py, c), device_id_type=pl.DeviceIdType.MESH)
            cp.start()
            sends.append(cp)
        for cp in sends:
            cp.wait_recv()
        for cp in sends:
            cp.wait_send()
        mine.wait()

    return pl.pallas_call(
        body, out_shape=jax.ShapeDtypeStruct(parts.shape, parts.dtype), in_specs=[_ANY], out_specs=_ANY,
        scratch_shapes=[pltpu.SemaphoreType.DMA((3,)), pltpu.SemaphoreType.DMA((3,)), pltpu.SemaphoreType.DMA(())],
        name=name)(parts)


def _rows(a):
    return a.reshape(-1, PACK_W)


def _pad_to(a, n):
    return jnp.pad(a, (0, n - a.shape[0]))


def _weight_blocks(d):
    small = _rows(_pad_to(jnp.concatenate([d[n].reshape(-1) for n, _ in _SMALL]), 16 * PACK_W))
    bits = lax.bitcast_convert_type(small, jnp.uint32)
    halves = [lax.bitcast_convert_type(h.astype(jnp.uint16), WIRE) for h in (bits >> 16, bits & 0xFFFF)]
    block_a = jnp.concatenate([d["w_in_a"][0].T.astype(WIRE)] + halves, axis=0)
    w_uq = jnp.pad(d["w_uq"][0], ((0, 0), (0, 0), (0, HEAD_PAD - QK_NOPE - QK_ROPE)))
    pieces = {"w_out_a": d["w_out_a"], "w_dkv": d["w_dkv"], "w_uk": d["w_uk"], "w_uv": d["w_uv"],
              "w_in_b": d["w_in_b"][0].T, "w_uq": w_uq}
    block_b = jnp.concatenate([_rows(pieces[n]) for n, _ in _PIECES_B]
                              + [jnp.zeros((WIRE_ROWS_B - MATRIX_ROWS_B, PACK_W), F32)], axis=0).astype(WIRE)
    return block_a, block_b, d["w_out_b"][0].astype(WIRE)


def _weights_a(wall):
    w = {}
    lo, hi = _OFF_A["w_in_a"]
    w["w_in_a_t"] = wall[:, lo:hi].reshape(2 * D_RNN, D_MODEL)
    high, low = (lax.bitcast_convert_type(wall[:, r:r + 16], jnp.uint16).astype(jnp.uint32)
                 for r in (MATRIX_ROWS_A, MATRIX_ROWS_A + 16))
    small = lax.bitcast_convert_type((high << 16) | low, F32)[:, :8].reshape(N_DEV, 8 * PACK_W)
    off = dict(zip([n for n, _ in _SMALL], [0, 128, 768, 928, 1088, 1248]))
    w["norm_a"] = small[:, :128].reshape(1, D_MODEL)

    def by_channel(lo, rows):
        a = small[:, lo:lo + rows * (D_RNN // N_DEV)].reshape(N_DEV, rows, -1).transpose(1, 0, 2).reshape(rows, D_RNN)
        return jnp.pad(a, ((0, 8 - rows), (0, 0)))

    w["conv_taps"] = by_channel(off["conv_w"], CONV_WIDTH)
    w["lru_vecs"] = by_channel(off["conv_b"], 4)
    return w


def _weights_b(wall):
    piece = {n: wall[:, lo:hi] for n, (lo, hi) in _OFF_B.items()}
    w = {"w_out_a": piece["w_out_a"].reshape(D_RNN, D_MODEL)}
    w_dkv = piece["w_dkv"].reshape(D_MODEL, KV_RANK + QK_ROPE)
    w["w_dkv_c"] = w_dkv[:, :KV_RANK]
    w["w_dkv_r"] = jnp.pad(w_dkv[:, KV_RANK:], ((0, 0), (0, LANES - QK_ROPE)))
    w["w_uk"] = piece["w_uk"].reshape(KV_RANK, N_HEADS * QK_NOPE)
    w["w_uv"] = piece["w_uv"].reshape(KV_RANK, N_HEADS * V_DIM)
    w["w_in_b_t"] = piece["w_in_b"].reshape(Q_RANK + N_HEADS * V_DIM, D_MODEL)
    w["w_uq"] = piece["w_uq"].reshape(Q_RANK, N_HEADS * HEAD_PAD)
    return w


def _pack_rep(d):
    flat = jnp.concatenate([d[n].reshape(-1) for n, _ in _REP])
    return _rows(_pad_to(flat, REP_ROWS * PACK_W))


def _unpack_rep(p, like):
    flat = p.reshape(-1)
    out, off = {}, 0
    for n, k in _REP:
        out[n] = flat[off:off + k].reshape(like[n].shape)
        off += k
    return out


def _by_owner(a):
    return a.reshape(N_DEV, -1, PACK_W)


def _grad_parts_b(g):
    tail = jnp.zeros((N_DEV, WIRE_ROWS_B - MATRIX_ROWS_B, PACK_W), F32)
    return jnp.concatenate([_by_owner(g[n]) for n, _ in _PIECES_B] + [tail], axis=1).astype(BF16)


def _grad_parts_a(g):
    small = jnp.concatenate([
        g["norm_a"].reshape(N_DEV, -1),
        g["conv_w"].reshape(CONV_WIDTH, N_DEV, -1).transpose(1, 0, 2).reshape(N_DEV, -1),
        g["conv_b"].reshape(N_DEV, -1), g["b_rg"].reshape(N_DEV, -1), g["b_ig"].reshape(N_DEV, -1),
        g["lru_lambda"].reshape(N_DEV, -1)], axis=1)
    small = jnp.pad(small, ((0, 0), (0, 8 * PACK_W - small.shape[1]))).reshape(N_DEV, 8, PACK_W)
    half = N_DEV // 2
    w_in_a = jnp.concatenate([h.reshape(half, -1, PACK_W) for h in g["w_in_a_t"]], axis=0)
    rep = _pack_rep(g).reshape(N_DEV, REP_SLICE, PACK_W)
    tail = jnp.zeros((N_DEV, GRAD_ROWS_A - MATRIX_ROWS_A - 8 - REP_SLICE, PACK_W), F32)
    return jnp.concatenate([w_in_a, small, rep, tail], axis=1)


def _own_grads(sum_a, sum_b, sum_c):
    out = {}
    lo, hi = _OFF_A["w_in_a"]
    out["w_in_a"] = sum_a[lo:hi].T.reshape(1, D_MODEL, 2 * D_RNN // N_DEV)
    small = sum_a[MATRIX_ROWS_A:MATRIX_ROWS_A + 8].reshape(-1)
    shapes = {"norm_a": (1, D_MODEL // N_DEV), "conv_w": (1, CONV_WIDTH, D_RNN // N_DEV), "conv_b": (1, D_RNN // N_DEV),
              "b_rg": (1, D_RNN // N_DEV), "b_ig": (1, D_RNN // N_DEV), "lru_lambda": (1, D_RNN // N_DEV)}
    off = 0
    for n, k in _SMALL:
        out[n] = small[off:off + k].reshape(shapes[n])
        off += k
    piece = {n: sum_b[lo:hi] for n, (lo, hi) in _OFF_B.items()}
    out["w_out_a"] = piece["w_out_a"].reshape(1, D_RNN // N_DEV, D_MODEL)
    out["w_dkv"] = piece["w_dkv"].reshape(D_MODEL // N_DEV, KV_RANK + QK_ROPE)
    out["w_uk"] = piece["w_uk"].reshape(KV_RANK // N_DEV, N_HEADS, QK_NOPE)
    out["w_uv"] = piece["w_uv"].reshape(KV_RANK // N_DEV, N_HEADS, V_DIM)
    out["w_in_b"] = piece["w_in_b"].T.reshape(1, D_MODEL, (Q_RANK + N_HEADS * V_DIM) // N_DEV)
    out["w_uq"] = piece["w_uq"].reshape(1, Q_RANK // N_DEV, N_HEADS, HEAD_PAD)[..., :QK_NOPE + QK_ROPE]
    out["w_out_b"] = sum_c.reshape(1, N_HEADS * V_DIM // N_DEV, D_MODEL)
    return out


def _step(x, target, w, rep, block_b, block_c, *, bsz, seq):
    t = bsz * seq
    cos, sin = _rope_tables(seq)
    g_a = w["norm_a"]
    g_kv = rep["norm_kv"].reshape(1, -1)
    g_kvn = rep["kv_norm"].reshape(1, -1)
    g_b = rep["norm_b"].reshape(1, -1)
    g_q = rep["q_norm"].reshape(1, -1)
    g_f = rep["final_norm"].reshape(1, -1)
    wrg = rep["w_rg"][0].astype(BF16)
    wig = rep["w_ig"][0].astype(BF16)
    cw8, vecs = w["conv_taps"], w["lru_vecs"]

    def seq3(a):
        return a.reshape(bsz, seq, a.shape[-1])

    def flat(a):
        return a.reshape(t, a.shape[-1])

    h0, xp, ga = _lru_proj_fwd(x, g_a, w["w_in_a_t"], name="lru_proj_fwd")
    xb, hs, y, wall_b = _lru_fwd(seq3(xp), seq3(ga), cw8, vecs, wrg, wig, block_b, name="lru_fwd")
    w = dict(w, **_weights_b(wall_b))
    x1, hk, hq, ck, cqp, g2, ckv, cq, q, kn, v, kr, kn_t, v_t, kr_t = _mla_proj_fwd(
        flat(y), x, (g_kv, g_b, g_kvn, g_q), w, cos, sin, seq=seq, name="mla_proj_fwd")
    o, lse, wall_c = _attn_fwd(q, kn, kr, v_t, block_c, bsz=bsz, seq=seq, name="attn_fwd")
    w_out_b = wall_c.reshape(N_HEADS * V_DIM, D_MODEL)
    loss, dx2, d_w_out_b, do, dg2, dgf = _head_and_loss(o, g2, x1, target, w_out_b, g_f, name="head_loss")
    grads = {"final_norm": dgf}
    parts_c = _by_owner(d_w_out_b).astype(BF16)
    dq, dkn, dkr, dv, landed_c = _attn_bwd(q, kn, kr, kn_t, kr_t, v, o, lse, do, cos, sin, parts_c,
                                           bsz=bsz, seq=seq, name="attn_bwd")
    dx1, du2, dckr, dgkv, dgb, dgkvn, dgq, dy = _mla_proj_bwd(
        x1, dx2, cqp, ck, dq, dkn, dv, dkr, dg2, (g_kv, g_b, g_kvn, g_q), w, name="mla_proj_bwd")
    grads["norm_kv"], grads["norm_b"], grads["kv_norm"], grads["q_norm"] = dgkv, dgb, dgkvn, dgq
    grads["w_uq"], grads["w_uk"], grads["w_uv"], d_w_dkv = _token_sums(
        [cq, dq, ckv, dkn, dv, hk, dckr], [(0, 1), (2, 3), (2, 4), (5, 6)], name="d_w_uq_uk_uv_dkv")
    grads["w_dkv"] = d_w_dkv[:, :KV_RANK + QK_ROPE]
    grads["w_in_b"], grads["w_out_a"] = _token_sums(
        [du2, hq, flat(y), dx1], [(0, 1), (2, 3)], name="d_w_in_b_t_out_a")
    parts_b = _grad_parts_b(grads)
    dxp, dga, dwrg, dwig, dvec, landed_b = _lru_bwd(
        seq3(dy), seq3(xp), xb, hs, seq3(ga), cw8, vecs, wrg, wig, parts_b, name="lru_bwd")
    dxp, dga = flat(dxp), flat(dga)
    grads["w_rg"], grads["w_ig"] = dwrg, dwig
    grads["b_rg"], grads["b_ig"], grads["conv_b"] = dvec[0], dvec[1], dvec[3]
    lam = vecs[3]
    grads["lru_lambda"] = dvec[2] * (-1.0 / (1.0 + jnp.exp(lam)))
    grads["conv_w"] = dvec[4:4 + CONV_WIDTH]
    dx, dga_norm, dwx, dwg = _lru_proj_bwd(dxp, dga, x, dx1, h0, g_a, w["w_in_a_t"], name="lru_proj_bwd")
    grads["norm_a"] = dga_norm
    grads["w_in_a_t"] = (dwx, dwg)
    return loss[0, 0], dx, grads, landed_b, landed_c


def kernel(x, norm_a, w_in_a, conv_w, conv_b, w_rg, b_rg, w_ig, b_ig, lru_lambda, w_out_a, norm_kv, w_dkv, kv_norm, w_uk, w_uv, norm_b, w_in_b, q_norm, w_uq, w_out_b, final_norm, loss_target, m_norm_a, m_w_in_a, m_conv_w, m_conv_b, m_w_rg, m_b_rg, m_w_ig, m_b_ig, m_lru_lambda, m_w_out_a, m_norm_kv, m_w_dkv, m_kv_norm, m_w_uk, m_w_uv, m_norm_b, m_w_in_b, m_q_norm, m_w_uq, m_w_out_b, m_final_norm, v_norm_a, v_w_in_a, v_conv_w, v_conv_b, v_w_rg, v_b_rg, v_w_ig, v_b_ig, v_lru_lambda, v_w_out_a, v_norm_kv, v_w_dkv, v_kv_norm, v_w_uk, v_w_uv, v_norm_b, v_w_in_b, v_q_norm, v_w_uq, v_w_out_b, v_final_norm):
    given = dict(locals())
    wts = {n: given[n] for n in WEIGHTS}
    mom1 = {n: given["m_" + n] for n in WEIGHTS}
    mom2 = {n: given["v_" + n] for n in WEIGHTS}
    bsz, seq, _ = x.shape
    t = bsz * seq

    block_a, block_b, block_c = _weight_blocks(wts)
    w = _weights_a(_all_gather(block_a, name="gather_weights_a"))
    loss, dx, grads, landed_b, landed_c = _step(x.reshape(t, D_MODEL), loss_target.reshape(t, D_MODEL), w, wts,
                                                block_b, block_c, bsz=bsz, seq=seq)

    parts_a = _grad_parts_a(grads)
    from_sibling = _exchange_d2d(parts_a, name="exchange_grads_d2d")
    chip_parts = _chip_partial(parts_a, from_sibling, name="chip_partial_grads")
    landed_a = _exchange_ici(chip_parts, name="exchange_grads_ici")
    sum_a = _sum_parts(landed_a, name="sum_grads_a", br=GRAD_BLOCK)
    sum_b = _sum_parts(landed_b, name="sum_grads_b", br=WIRE_ROWS_B // 2)
    sum_c = _sum_parts(landed_c, name="sum_grads_c", br=landed_c.shape[1])
    g_own = _own_grads(sum_a, sum_b, sum_c)
    rep_slice = sum_a[MATRIX_ROWS_A + 8:MATRIX_ROWS_A + 8 + REP_SLICE]
    loss_rows = jnp.pad(loss.reshape(1, 1), ((0, 7), (0, PACK_W - 1)))
    gathered = _all_gather(jnp.concatenate([rep_slice, loss_rows], axis=0), name="gather_replicated")
    g_own.update(_unpack_rep(gathered[:, :REP_SLICE].reshape(REP_ROWS, PACK_W), wts))
    loss = jnp.sum(gathered[:, REP_SLICE, 0])

    own = [g_own[n] for n in WEIGHTS]
    deltas, new_m, new_v = _adamw(own, *([d[n] for n in WEIGHTS] for d in (wts, mom1, mom2)), name="adamw")
    return (loss, dx.reshape(bsz, seq, D_MODEL), *own, *deltas, *new_m, *new_v)
```

```python
import jax
import jax.numpy as jnp
from jax import lax
from jax.experimental import pallas as pl
from jax.experimental.pallas import tpu as pltpu

F32 = jnp.float32
BF16 = jnp.bfloat16
WIRE = jnp.bfloat16

D_MODEL = 1024
D_RNN = 1280
RNN_BLOCKS = 10
RNN_BW = 128
CONV_WIDTH = 4
LRU_C = 8.0
N_HEADS = 8
QK_NOPE = 128
QK_ROPE = 64
V_DIM = 128
KV_RANK = 256
Q_RANK = 384
ROPE_THETA = 10000.0
EPS = 1e-6
ATTN_SCALE = (QK_NOPE + QK_ROPE) ** -0.5
HEAD_PAD = 256
LANES = 128

ADAM_LR = 0.001
ADAM_B1 = 0.9
ADAM_B2 = 0.999
ADAM_EPS = 1e-08
ADAM_WD = 0.01
ADAM_STEP = 10

N_DEV = 8
VMEM_LIMIT_BYTES = 56 * 2**20
PACK_W = 1024

_PIECES_A = (("w_in_a", 320),)
_PIECES_B = (("w_out_a", 160), ("w_dkv", 40), ("w_uk", 32), ("w_uv", 32), ("w_in_b", 176), ("w_uq", 96))


def _offsets(pieces):
    off, r = {}, 0
    for n, k in pieces:
        off[n] = (r, r + k)
        r += k
    return off, r


_OFF_A, MATRIX_ROWS_A = _offsets(_PIECES_A)
_OFF_B, MATRIX_ROWS_B = _offsets(_PIECES_B)
WIRE_ROWS_A = MATRIX_ROWS_A + 32
WIRE_ROWS_B = 544
_SMALL = (("norm_a", 128), ("conv_w", 640), ("conv_b", 160), ("b_rg", 160), ("b_ig", 160), ("lru_lambda", 160))
_REP = (("w_rg", 163840), ("w_ig", 163840), ("norm_kv", 1024), ("kv_norm", 256), ("norm_b", 1024),
        ("q_norm", 384), ("final_norm", 1024))
REP_ROWS = 384
REP_SLICE = REP_ROWS // N_DEV
GRAD_ROWS_A = 384
GRAD_BLOCK = 192

WEIGHTS = ("norm_a", "w_in_a", "conv_w", "conv_b", "w_rg", "b_rg", "w_ig", "b_ig", "lru_lambda", "w_out_a",
           "norm_kv", "w_dkv", "kv_norm", "w_uk", "w_uv", "norm_b", "w_in_b", "q_norm", "w_uq", "w_out_b",
           "final_norm")


def _params(sem=None):
    return pltpu.CompilerParams(dimension_semantics=sem, vmem_limit_bytes=VMEM_LIMIT_BYTES)


_NT = (((1,), (1,)), ((), ()))
_ANY = pl.BlockSpec(memory_space=pl.ANY)


def _mesh_pos():
    return lax.axis_index("x"), lax.axis_index("y"), lax.axis_index("c")


def _sigmoid(z):
    return 0.5 * jnp.tanh(0.5 * z) + 0.5


def _sigmoid_tail(z):
    return 1.0 / (1.0 + jnp.exp(-z))


def _token_sums(operands, pairs, *, name, bt=1024):
    t = operands[0].shape[0]
    bt = min(bt, t)
    k = len(operands)

    def body(*refs):
        ins, outs = refs[:k], refs[k:]

        @pl.when(pl.program_id(0) == 0)
        def _():
            for o_ref in outs:
                o_ref[...] = jnp.zeros_like(o_ref)

        vals = [r[...].astype(BF16) for r in ins]
        for (i, j), o_ref in zip(pairs, outs):
            o_ref[...] += lax.dot_general(vals[i], vals[j], (((0,), (0,)), ((), ())), preferred_element_type=F32)

    shapes = [(operands[i].shape[1], operands[j].shape[1]) for i, j in pairs]
    return pl.pallas_call(
        body, grid=(t // bt,),
        in_specs=[pl.BlockSpec((bt, a.shape[1]), lambda s: (s, 0)) for a in operands],
        out_specs=[pl.BlockSpec(shape, lambda s: (0, 0)) for shape in shapes],
        out_shape=[jax.ShapeDtypeStruct(shape, F32) for shape in shapes],
        compiler_params=_params(("arbitrary",)), name=name)(*operands)


def _swap_halves(v):
    ax = v.ndim - 1
    lane = lax.broadcasted_iota(jnp.int32, v.shape, ax)
    up = pltpu.roll(v, LANES - QK_ROPE // 2, axis=ax)
    down = pltpu.roll(v, QK_ROPE // 2, axis=ax)
    return jnp.where(lane < QK_ROPE // 2, up, jnp.where(lane < QK_ROPE, down, 0.0))


def _rope(v, cos, sin):
    return v * cos + _swap_halves(v) * sin


def _rope_t(d, cos, sin):
    return d * cos + _swap_halves(d * sin)


def _rope_tables(seq):
    pos = jnp.arange(seq, dtype=F32)
    inv = ROPE_THETA ** (-jnp.arange(0, QK_ROPE, 2, dtype=F32) / QK_ROPE)
    ang = pos[:, None] * inv[None, :]
    cos, sin = jnp.cos(ang), jnp.sin(ang)
    zero = jnp.zeros((seq, LANES - QK_ROPE), F32)
    return jnp.concatenate([cos, cos, zero], axis=1), jnp.concatenate([-sin, sin, zero], axis=1)


def _rms(v):
    return v * lax.rsqrt(jnp.mean(v * v, axis=-1, keepdims=True) + EPS)


def _const_spec(a):
    return pl.BlockSpec(a.shape, lambda i: (0,) * a.ndim)


def _lru_proj_fwd(x, g_a, w_in_t, *, name, bt=512):
    t, d = x.shape
    bt = min(bt, t)
    n = w_in_t.shape[0] // 2

    def body(x_ref, g_ref, wt_ref, h_ref, xp_ref, ga_ref):
        h = (_rms(x_ref[...]) * g_ref[...]).astype(BF16)
        h_ref[...] = h
        xp_ref[...] = lax.dot_general(h, wt_ref[0:n, :], _NT, preferred_element_type=F32)
        ga_ref[...] = lax.dot_general(h, wt_ref[n:2 * n, :], _NT, preferred_element_type=F32)

    row = lambda w: pl.BlockSpec((bt, w), lambda i: (i, 0))
    return pl.pallas_call(
        body, grid=(t // bt,), in_specs=[row(d), _const_spec(g_a), _const_spec(w_in_t)],
        out_specs=[row(d), row(n), row(n)],
        out_shape=[jax.ShapeDtypeStruct((t, d), BF16), jax.ShapeDtypeStruct((t, n), F32), jax.ShapeDtypeStruct((t, n), F32)],
        compiler_params=_params(("parallel",)), name=name)(x, g_a, w_in_t)


def _mla_proj_fwd(y, x, gains, w, cos, sin, *, seq, name, bt=512):
    t, d = x.shape
    bt = min(bt, seq)
    per_seq = seq // bt
    g_kv, g_b, g_kvn, g_q = gains
    consts = [g_kv, g_b, g_kvn, g_q, w["w_dkv_c"], w["w_dkv_r"], w["w_in_b_t"], w["w_uk"], w["w_uv"], w["w_uq"],
              w["w_out_a"]]

    def body(y_ref, x_ref, cos_ref, sin_ref, gkv_ref, gb_ref, gkvn_ref, gq_ref, wdc_ref, wdr_ref, wbt_ref,
             wuk_ref, wuv_ref, wuq_ref, wo_ref, x1_ref,
             hk_ref, hq_ref, ck_ref, cqp_ref, g2_ref, ckv_ref, cq_ref, q_ref, kn_ref, v_ref, kr_ref, knt_ref, vt_ref, krt_ref):
        x1 = jnp.dot(y_ref[...], wo_ref[...], preferred_element_type=F32) + x_ref[...]
        x1_ref[...] = x1
        nrm = _rms(x1)
        hk = (nrm * gkv_ref[...]).astype(BF16)
        hq = (nrm * gb_ref[...]).astype(BF16)
        hk_ref[...] = hk
        hq_ref[...] = hq
        ck = jnp.dot(hk, wdc_ref[...], preferred_element_type=F32)
        ck_ref[...] = ck
        cqp = lax.dot_general(hq, wbt_ref[0:Q_RANK, :], _NT, preferred_element_type=F32)
        cqp_ref[...] = cqp
        g2_ref[...] = lax.dot_general(hq, wbt_ref[Q_RANK:, :], _NT, preferred_element_type=F32)
        cosv, sinv = cos_ref[...], sin_ref[...]
        kr = _rope(jnp.dot(hk, wdr_ref[...], preferred_element_type=F32), cosv, sinv)
        kr_ref[...] = kr.astype(BF16)
        krt_ref[...] = kr.T.astype(BF16)
        ckv = (_rms(ck) * gkvn_ref[...]).astype(BF16)
        ckv_ref[...] = ckv
        kn = jnp.dot(ckv, wuk_ref[...], preferred_element_type=F32)
        v = jnp.dot(ckv, wuv_ref[...], preferred_element_type=F32)
        kn_ref[...] = kn.astype(BF16)
        v_ref[...] = v.astype(BF16)
        knt_ref[...] = kn.T.astype(BF16)
        vt_ref[...] = v.T.astype(BF16)
        cq = (_rms(cqp) * gq_ref[...]).astype(BF16)
        cq_ref[...] = cq
        for h in range(N_HEADS):
            qh = jnp.dot(cq, wuq_ref[:, h * HEAD_PAD:(h + 1) * HEAD_PAD], preferred_element_type=F32)
            q_ref[:, h * HEAD_PAD:h * HEAD_PAD + QK_NOPE] = qh[:, :QK_NOPE].astype(BF16)
            q_ref[:, h * HEAD_PAD + QK_NOPE:(h + 1) * HEAD_PAD] = _rope(qh[:, QK_NOPE:], cosv, sinv).astype(BF16)

    row = lambda w_: pl.BlockSpec((bt, w_), lambda i: (i, 0))
    col = lambda h_: pl.BlockSpec((h_, bt), lambda i: (0, i))
    tab = pl.BlockSpec((bt, LANES), lambda i: (i % per_seq, 0))
    nh = N_HEADS * V_DIM
    shapes = [((t, d), F32), ((t, d), BF16), ((t, d), BF16), ((t, KV_RANK), F32), ((t, Q_RANK), F32), ((t, nh), F32),
              ((t, KV_RANK), BF16), ((t, Q_RANK), BF16), ((t, N_HEADS * HEAD_PAD), BF16), ((t, nh), BF16), ((t, nh), BF16),
              ((t, LANES), BF16), ((nh, t), BF16), ((nh, t), BF16), ((LANES, t), BF16)]
    out_specs = [row(d), row(d), row(d), row(KV_RANK), row(Q_RANK), row(nh), row(KV_RANK), row(Q_RANK),
                 row(N_HEADS * HEAD_PAD), row(nh), row(nh), row(LANES), col(nh), col(nh), col(LANES)]
    return pl.pallas_call(
        body, grid=(t // bt,), in_specs=[row(D_RNN), row(d), tab, tab] + [_const_spec(a) for a in consts],
        out_specs=out_specs, out_shape=[jax.ShapeDtypeStruct(s, dt) for s, dt in shapes],
        compiler_params=_params(("parallel",)), name=name)(y, x, cos, sin, *consts)


def _rms_bwd_rows(xv, dn):
    r = lax.rsqrt(jnp.mean(xv * xv, axis=-1, keepdims=True) + EPS)
    nrm = xv * r
    return r * (dn - nrm * jnp.mean(dn * nrm, axis=-1, keepdims=True)), nrm


def _col_sum(v):
    return jnp.sum(v, axis=0, keepdims=True)


def _lru_proj_bwd(dxp, dga, x, dx1, h0, g_a, w_in_t, *, name, bt=512):
    t, d = x.shape
    bt = min(bt, t)
    n = w_in_t.shape[0] // 2
    tn = (((0,), (0,)), ((), ()))

    def body(dxp_ref, dga_ref, x_ref, dx1_ref, h0_ref, g_ref, wt_ref, dx_ref, dg_ref, dwx_ref, dwg_ref):
        @pl.when(pl.program_id(0) == 0)
        def _():
            for ref in (dg_ref, dwx_ref, dwg_ref):
                ref[...] = jnp.zeros_like(ref)

        dxp_v, dga_v, h0 = dxp_ref[...], dga_ref[...], h0_ref[...]
        dwx_ref[...] += lax.dot_general(dxp_v, h0, tn, preferred_element_type=F32)
        dwg_ref[...] += lax.dot_general(dga_v, h0, tn, preferred_element_type=F32)
        dh = (jnp.dot(dxp_v, wt_ref[0:n, :], preferred_element_type=F32)
              + jnp.dot(dga_v, wt_ref[n:2 * n, :], preferred_element_type=F32))
        dxn, nrm = _rms_bwd_rows(x_ref[...], dh * g_ref[...])
        dg_ref[...] += _col_sum(dh * nrm)
        dx_ref[...] = dx1_ref[...] + dxn

    row = lambda w: pl.BlockSpec((bt, w), lambda i: (i, 0))
    whole = pl.BlockSpec((n, d), lambda i: (0, 0))
    return pl.pallas_call(
        body, grid=(t // bt,),
        in_specs=[row(n), row(n), row(d), row(d), row(d), _const_spec(g_a), _const_spec(w_in_t)],
        out_specs=[row(d), _const_spec(g_a), whole, whole],
        out_shape=[jax.ShapeDtypeStruct((t, d), F32), jax.ShapeDtypeStruct((1, d), F32),
                   jax.ShapeDtypeStruct((n, d), F32), jax.ShapeDtypeStruct((n, d), F32)],
        compiler_params=_params(("arbitrary",)), name=name)(dxp, dga, x, dx1, h0, g_a, w_in_t)


def _mla_proj_bwd(x1, dx2, cqp, ck, dq, dkn, dv, dkr, dg2, gains, w, *, name, bt=512):
    t, d = x1.shape
    bt = min(bt, t)
    g_kv, g_b, g_kvn, g_q = gains
    consts = [g_kv, g_b, g_kvn, g_q, w["w_dkv_c"], w["w_dkv_r"], w["w_in_b_t"], w["w_uk"], w["w_uv"], w["w_uq"],
              w["w_out_a"]]
    nh = N_HEADS * V_DIM

    def body(x1_ref, dx2_ref, cqp_ref, ck_ref, dq_ref, dkn_ref, dv_ref, dkr_ref, dg2_ref,
             gkv_ref, gb_ref, gkvn_ref, gq_ref, wdc_ref, wdr_ref, wbt_ref, wuk_ref, wuv_ref, wuq_ref, wo_ref,
             dx1_ref, du2_ref, dckr_ref, dgkv_ref, dgb_ref, dgkvn_ref, dgq_ref, dy_ref):
        @pl.when(pl.program_id(0) == 0)
        def _():
            for ref in (dgkv_ref, dgb_ref, dgkvn_ref, dgq_ref):
                ref[...] = jnp.zeros_like(ref)

        dot_nt = lambda a, b: lax.dot_general(a, b, _NT, preferred_element_type=F32)
        dcq = dot_nt(dq_ref[...], wuq_ref[...])
        dcqp, nq = _rms_bwd_rows(cqp_ref[...], dcq * gq_ref[...])
        dgq_ref[...] += _col_sum(dcq * nq)
        dcqp = dcqp.astype(BF16)
        dg2 = dg2_ref[...]
        du2_ref[:, :Q_RANK] = dcqp
        du2_ref[:, Q_RANK:] = dg2
        dhq = (jnp.dot(dcqp, wbt_ref[0:Q_RANK, :], preferred_element_type=F32)
               + jnp.dot(dg2, wbt_ref[Q_RANK:, :], preferred_element_type=F32))
        dckv = dot_nt(dkn_ref[...], wuk_ref[...]) + dot_nt(dv_ref[...], wuv_ref[...])
        dck, nc = _rms_bwd_rows(ck_ref[...], dckv * gkvn_ref[...])
        dgkvn_ref[...] += _col_sum(dckv * nc)
        dck = dck.astype(BF16)
        dkr = dkr_ref[...].astype(BF16)
        dckr_ref[:, :KV_RANK] = dck
        dckr_ref[:, KV_RANK:] = dkr
        dhk = dot_nt(dck, wdc_ref[...]) + dot_nt(dkr, wdr_ref[...])
        dxn, n1 = _rms_bwd_rows(x1_ref[...], dhq * gb_ref[...] + dhk * gkv_ref[...])
        dgb_ref[...] += _col_sum(dhq * n1)
        dgkv_ref[...] += _col_sum(dhk * n1)
        dx1 = dx2_ref[...] + dxn
        dx1_ref[...] = dx1
        dy_ref[...] = lax.dot_general(dx1.astype(BF16), wo_ref[...], _NT, preferred_element_type=F32)

    row = lambda w_: pl.BlockSpec((bt, w_), lambda i: (i, 0))
    vec = lambda w_: pl.BlockSpec((1, w_), lambda i: (0, 0))
    in_specs = [row(d), row(d), row(Q_RANK), row(KV_RANK), row(N_HEADS * HEAD_PAD), row(nh), row(nh), row(LANES), row(nh)]
    return pl.pallas_call(
        body, grid=(t // bt,), in_specs=in_specs + [_const_spec(a) for a in consts],
        out_specs=[row(d), row(Q_RANK + nh), row(KV_RANK + LANES), vec(d), vec(d), vec(KV_RANK), vec(Q_RANK), row(D_RNN)],
        out_shape=[jax.ShapeDtypeStruct((t, d), F32), jax.ShapeDtypeStruct((t, Q_RANK + nh), BF16),
                   jax.ShapeDtypeStruct((t, KV_RANK + LANES), BF16), jax.ShapeDtypeStruct((1, d), F32),
                   jax.ShapeDtypeStruct((1, d), F32), jax.ShapeDtypeStruct((1, KV_RANK), F32),
                   jax.ShapeDtypeStruct((1, Q_RANK), F32), jax.ShapeDtypeStruct((t, D_RNN), F32)],
        compiler_params=_params(("arbitrary",)), name=name)(x1, dx2, cqp, ck, dq, dkn, dv, dkr, dg2, *consts)


def _softplus(z):
    return jnp.maximum(z, 0.0) + jnp.log1p(jnp.exp(-jnp.abs(z)))


def _one_minus_square(a, la):
    return jnp.tanh(-la) * (1.0 + a * a)


def _gates(xb, wrg, wig, brg, big, sp):
    xbb = xb.astype(BF16)
    r = _sigmoid_tail(jnp.dot(xbb, wrg, preferred_element_type=F32) + brg)
    i = _sigmoid(jnp.dot(xbb, wig, preferred_element_type=F32) + big)
    la = (-LRU_C) * r * sp
    a = jnp.exp(la)
    em = _one_minus_square(a, la)
    inv_mult = lax.rsqrt(em)
    mult = jnp.where(em > 0.0, em * inv_mult, 0.0)
    return r, i, a, mult, inv_mult


def _conv(xpad_ref, cw_ref, seq):
    acc = cw_ref[0:1, :] * xpad_ref[pl.ds(8 - (CONV_WIDTH - 1), seq), :]
    for k in range(1, CONV_WIDTH):
        acc = acc + cw_ref[k:k + 1, :] * xpad_ref[pl.ds(8 - (CONV_WIDTH - 1) + k, seq), :]
    return acc


def _seq_spec(seq):
    return pl.BlockSpec((None, seq, RNN_BW), lambda n, b: (b, 0, n))


def _chan_spec(rows):
    return pl.BlockSpec((rows, RNN_BW), lambda n, b: (0, n))


_GATE_W_SPEC = pl.BlockSpec((None, RNN_BW, RNN_BW), lambda n, b: (n, 0, 0))


SCAN_UNROLL = 4


def _peers():
    x, y, c = _mesh_pos()
    others = []
    for k in range(1, N_DEV):
        px = 1 - x if k & 4 else x
        py = 1 - y if k & 2 else y
        pc = 1 - c if k & 1 else c
        others.append(((px, py, pc), 4 * px + 2 * py + pc))
    return 4 * x + 2 * y + c, others


def _exchange(src_ref, dst_ref, send_sems, recv_sems, local_sem, *, finish, gather=False):
    me, others = _peers()

    def send(k, dev, slot):
        return pltpu.make_async_remote_copy(
            src_ref=src_ref if gather else src_ref.at[slot], dst_ref=dst_ref.at[me], send_sem=send_sems.at[k],
            recv_sem=recv_sems.at[k], device_id=dev, device_id_type=pl.DeviceIdType.MESH)

    local = pltpu.make_async_copy(src_ref if gather else src_ref.at[me], dst_ref.at[me], local_sem)
    if not finish:
        local.start()
        for k, (dev, slot) in enumerate(others):
            send(k, dev, slot).start()
        return
    for k, (dev, slot) in enumerate(others):
        pltpu.make_async_remote_copy(
            src_ref=dst_ref.at[slot], dst_ref=dst_ref.at[slot], send_sem=send_sems.at[k], recv_sem=recv_sems.at[k],
            device_id=dev, device_id_type=pl.DeviceIdType.MESH).wait_recv()
    for k, (dev, slot) in enumerate(others):
        send(k, dev, slot).wait_send()
    local.wait()


def _gather_two_level(x_ref, out_ref, send_sems, recv_sems, local_sem, *, phase):
    x, y, c = _mesh_pos()
    me, sibling = (x, y, c), (x, y, 1 - c)
    chips = [(1 - x, y), (x, 1 - y), (1 - x, 1 - y)]

    def slot(px, py, pc):
        return out_ref.at[4 * px + 2 * py + pc]

    def copy(k, blk, to, src=None):
        return pltpu.make_async_remote_copy(
            src_ref=slot(*blk) if src is None else src, dst_ref=slot(*blk),
            send_sem=send_sems.at[k], recv_sem=recv_sems.at[k], device_id=to, device_id_type=pl.DeviceIdType.MESH)

    if phase == 0:
        pltpu.make_async_copy(x_ref, slot(*me), local_sem).start()
        copy(0, me, sibling, src=x_ref).start()
        for j, chip in enumerate(chips):
            copy(1 + j, me, (*chip, c), src=x_ref).start()
    elif phase == 1:
        for j, chip in enumerate(chips):
            copy(1 + j, (*chip, c), me).wait_recv()
            copy(4 + j, (*chip, c), sibling).start()
    else:
        copy(0, sibling, me).wait_recv()
        for j, chip in enumerate(chips):
            copy(4 + j, (*chip, 1 - c), me).wait_recv()
        copy(0, me, sibling, src=x_ref).wait_send()
        for j, chip in enumerate(chips):
            copy(1 + j, me, (*chip, c), src=x_ref).wait_send()
            copy(4 + j, (*chip, c), sibling).wait_send()
        pltpu.make_async_copy(x_ref, slot(*me), local_sem).wait()


GATHER_FORWARD_STEP = 9
_EXCHANGE_SEMS = [pltpu.SemaphoreType.DMA((N_DEV - 1,)), pltpu.SemaphoreType.DMA((N_DEV - 1,)), pltpu.SemaphoreType.DMA(())]


def _first_last(steps):
    first = last = None
    for axis, n in enumerate(steps):
        i = pl.program_id(axis)
        first = (i == 0) if first is None else first & (i == 0)
        last = (i == n - 1) if last is None else last & (i == n - 1)
    return first, last


def _lru_fwd(xp, ga, cw, vecs, wrg, wig, block, *, name):
    bsz, seq, _ = xp.shape
    groups = seq // 8

    def body(xp_ref, ga_ref, cw_ref, vec_ref, wrg_ref, wig_ref, blk_ref, xb_ref, hs_ref, y_ref, all_ref,
             xpad, a_s, b_s, send_sems, recv_sems, local_sem):
        first, last = _first_last((RNN_BLOCKS, bsz))

        @pl.when(first)
        def _():
            _gather_two_level(blk_ref, all_ref, send_sems, recv_sems, local_sem, phase=0)

        @pl.when((pl.program_id(0) == GATHER_FORWARD_STEP) & (pl.program_id(1) == 0))
        def _():
            _gather_two_level(blk_ref, all_ref, send_sems, recv_sems, local_sem, phase=1)

        xpad[0:8, :] = jnp.zeros((8, RNN_BW), F32)
        xpad[pl.ds(8, seq), :] = xp_ref[...]
        xb = _conv(xpad, cw_ref, seq) + vec_ref[0:1, :]
        xb_ref[...] = xb
        sp = _softplus(-vec_ref[3:4, :])
        _, i, a, mult, _ = _gates(xb, wrg_ref[...], wig_ref[...], vec_ref[1:2, :], vec_ref[2:3, :], sp)
        a_s[...] = a
        b_s[...] = mult * (i * xb)
        row = lax.broadcasted_iota(jnp.int32, (8, RNN_BW), 0)

        def group(g, h):
            r0 = pl.multiple_of(g * 8, 8)
            av = a_s[pl.ds(r0, 8), :]
            bv = b_s[pl.ds(r0, 8), :]
            for k in (1, 2, 4):
                m = row >= k
                bv = jnp.where(m, av * pltpu.roll(bv, k, axis=0) + bv, bv)
                av = jnp.where(m, av * pltpu.roll(av, k, axis=0), av)
            hs_ref[pl.ds(r0, 8), :] = av * h + bv
            return av[7:8, :] * h + bv[7:8, :]

        def groups_of(i, h):
            for u in range(SCAN_UNROLL):
                h = group(i * SCAN_UNROLL + u, h)
            return h

        lax.fori_loop(0, groups // SCAN_UNROLL, groups_of, jnp.zeros((1, RNN_BW), F32))
        gav = ga_ref[...]
        y_ref[...] = (hs_ref[...] * (gav * _sigmoid(gav))).astype(BF16)

        @pl.when(last)
        def _():
            _gather_two_level(blk_ref, all_ref, send_sems, recv_sems, local_sem, phase=2)

    sq = _seq_spec(seq)
    shape = (bsz, seq, D_RNN)
    return pl.pallas_call(
        body, grid=(RNN_BLOCKS, bsz),
        in_specs=[sq, sq, _chan_spec(8), _chan_spec(8), _GATE_W_SPEC, _GATE_W_SPEC, _ANY],
        out_specs=[sq, sq, sq, _ANY],
        out_shape=[jax.ShapeDtypeStruct(shape, F32), jax.ShapeDtypeStruct(shape, F32), jax.ShapeDtypeStruct(shape, BF16),
                   jax.ShapeDtypeStruct((N_DEV,) + block.shape, block.dtype)],
        scratch_shapes=[pltpu.VMEM((seq + 8, RNN_BW), F32), pltpu.VMEM((seq, RNN_BW), F32), pltpu.VMEM((seq, RNN_BW), F32)]
        + _EXCHANGE_SEMS,
        compiler_params=_params(("arbitrary", "arbitrary")), name=name)(xp, ga, cw, vecs, wrg, wig, block)


def _lru_bwd(dy, xp, xb, hs, ga, cw, vecs, wrg, wig, parts, *, name):
    bsz, seq, _ = xp.shape
    groups = seq // 8

    def body(dy_ref, xp_ref, xb_ref, hs_ref, ga_ref, cw_ref, vec_ref, wrg_ref, wig_ref,
             parts_ref, dxp_ref, dga_ref, dwrg_ref, dwig_ref, dvec_ref, land_ref, pad, a_s, d_s, lam_s,
             send_sems, recv_sems, local_sem):
        first, last = _first_last((RNN_BLOCKS, bsz))

        @pl.when(first)
        def _():
            _exchange(parts_ref, land_ref, send_sems, recv_sems, local_sem, finish=False)

        @pl.when(pl.program_id(1) == 0)
        def _():
            dwrg_ref[...] = jnp.zeros_like(dwrg_ref)
            dwig_ref[...] = jnp.zeros_like(dwig_ref)
            dvec_ref[...] = jnp.zeros_like(dvec_ref)

        xb = xb_ref[...]
        hs = hs_ref[...]
        gav = ga_ref[...]
        dy = dy_ref[...]
        sp = _softplus(-vec_ref[3:4, :])
        wrg = wrg_ref[...]
        wig = wig_ref[...]
        r, i, a, mult, inv_mult = _gates(xb, wrg, wig, vec_ref[1:2, :], vec_ref[2:3, :], sp)
        sg = _sigmoid(gav)
        dga_ref[...] = (dy * hs * (sg * (1.0 + gav * (1.0 - sg)))).astype(BF16)
        d_s[...] = dy * (gav * sg)

        pad[pl.ds(0, seq), :] = a
        pad[pl.ds(seq, 8), :] = jnp.zeros((8, RNN_BW), F32)
        a_s[...] = pad[pl.ds(1, seq), :]
        row = lax.broadcasted_iota(jnp.int32, (8, RNN_BW), 0)

        def group(g, nxt):
            r0 = pl.multiple_of((groups - 1 - g) * 8, 8)
            cv = a_s[pl.ds(r0, 8), :]
            bv = d_s[pl.ds(r0, 8), :]
            for k in (1, 2, 4):
                m = row < 8 - k
                bv = jnp.where(m, cv * pltpu.roll(bv, 8 - k, axis=0) + bv, bv)
                cv = jnp.where(m, cv * pltpu.roll(cv, 8 - k, axis=0), cv)
            lam_s[pl.ds(r0, 8), :] = cv * nxt + bv
            return cv[0:1, :] * nxt + bv[0:1, :]

        def groups_of(i, nxt):
            for u in range(SCAN_UNROLL):
                nxt = group(i * SCAN_UNROLL + u, nxt)
            return nxt

        lax.fori_loop(0, groups // SCAN_UNROLL, groups_of, jnp.zeros((1, RNN_BW), F32))
        dh = lam_s[...]

        pad[0:8, :] = jnp.zeros((8, RNN_BW), F32)
        pad[pl.ds(8, seq), :] = hs
        da = dh * pad[pl.ds(7, seq), :]
        ixb = i * xb
        dixb = dh * mult
        dla = da * a - (dh * ixb) * (a * a) * inv_mult
        drp = (dla * ((-LRU_C) * sp)) * r * (1.0 - r)
        dip = (dixb * xb) * i * (1.0 - i)
        dvec_ref[0:1, :] += jnp.sum(drp, axis=0, keepdims=True)
        dvec_ref[1:2, :] += jnp.sum(dip, axis=0, keepdims=True)
        dvec_ref[2:3, :] += jnp.sum(dla * ((-LRU_C) * r), axis=0, keepdims=True)
        drpb = drp.astype(BF16)
        dipb = dip.astype(BF16)
        xbb = xb.astype(BF16)
        nt = (((1,), (1,)), ((), ()))
        tn = (((0,), (0,)), ((), ()))
        dxb = (dixb * i
               + lax.dot_general(drpb, wrg, nt, preferred_element_type=F32)
               + lax.dot_general(dipb, wig, nt, preferred_element_type=F32))
        dwrg_ref[...] += lax.dot_general(xbb, drpb, tn, preferred_element_type=F32)
        dwig_ref[...] += lax.dot_general(xbb, dipb, tn, preferred_element_type=F32)
        dvec_ref[3:4, :] += jnp.sum(dxb, axis=0, keepdims=True)

        pad[pl.ds(0, seq), :] = dxb
        pad[pl.ds(seq, 8), :] = jnp.zeros((8, RNN_BW), F32)
        dxp = cw_ref[0:1, :] * pad[pl.ds(CONV_WIDTH - 1, seq), :]
        for k in range(1, CONV_WIDTH):
            dxp = dxp + cw_ref[k:k + 1, :] * pad[pl.ds(CONV_WIDTH - 1 - k, seq), :]
        dxp_ref[...] = dxp.astype(BF16)
        pad[0:8, :] = jnp.zeros((8, RNN_BW), F32)
        pad[pl.ds(8, seq), :] = xp_ref[...]
        for k in range(CONV_WIDTH):
            dvec_ref[4 + k:5 + k, :] += jnp.sum(dxb * pad[pl.ds(8 - (CONV_WIDTH - 1) + k, seq), :], axis=0, keepdims=True)

        @pl.when(last)
        def _():
            _exchange(parts_ref, land_ref, send_sems, recv_sems, local_sem, finish=True)

    sq = _seq_spec(seq)
    shape = (bsz, seq, D_RNN)
    gshape = (RNN_BLOCKS, RNN_BW, RNN_BW)
    return pl.pallas_call(
        body, grid=(RNN_BLOCKS, bsz),
        in_specs=[sq, sq, sq, sq, sq, _chan_spec(8), _chan_spec(8), _GATE_W_SPEC, _GATE_W_SPEC, _ANY],
        out_specs=[sq, sq, _GATE_W_SPEC, _GATE_W_SPEC, _chan_spec(8), _ANY],
        out_shape=[jax.ShapeDtypeStruct(shape, BF16), jax.ShapeDtypeStruct(shape, BF16),
                   jax.ShapeDtypeStruct(gshape, F32), jax.ShapeDtypeStruct(gshape, F32),
                   jax.ShapeDtypeStruct((8, D_RNN), F32), jax.ShapeDtypeStruct(parts.shape, parts.dtype)],
        scratch_shapes=[pltpu.VMEM((seq + 8, RNN_BW), F32), pltpu.VMEM((seq, RNN_BW), F32),
                        pltpu.VMEM((seq, RNN_BW), F32), pltpu.VMEM((seq, RNN_BW), F32)] + _EXCHANGE_SEMS,
        compiler_params=_params(("arbitrary", "arbitrary")), name=name)(dy, xp, xb, hs, ga, cw, vecs, wrg, wig, parts)


def _attn_block(seq):
    return min(512, seq)


def _diag_mask(blk):
    return lax.broadcasted_iota(jnp.int32, (blk, blk), 0) <= lax.broadcasted_iota(jnp.int32, (blk, blk), 1)


FWD_HEADS = 8
BWD_HEADS = 2


def _attn_fwd(q, kn, kr, v_t, block, *, bsz, seq, name):
    t = bsz * seq
    blk = _attn_block(seq)
    nq = seq // blk
    hg = FWD_HEADS
    steps = (bsz, N_HEADS // hg, nq)

    def body(q_ref, kn_ref, kr_ref, vt_ref, blk_ref, o_ref, lse_ref, all_ref, acc, send_sems, recv_sems, local_sem):
        first, last = _first_last(steps)

        @pl.when(first)
        def _():
            _exchange(blk_ref, all_ref, send_sems, recv_sems, local_sem, finish=False, gather=True)

        qi = pl.program_id(2)
        acc[...] = jnp.zeros_like(acc)

        def step(j, carry, diagonal):
            k0 = pl.multiple_of(j * blk, blk)
            kr_j = kr_ref[pl.ds(k0, blk), :]
            out = []
            for h in range(hg):
                m_i, l_i = carry[h]
                kv = jnp.concatenate([kn_ref[pl.ds(k0, blk), h * QK_NOPE:(h + 1) * QK_NOPE], kr_j], axis=1)
                qv = q_ref[:, h * HEAD_PAD:(h + 1) * HEAD_PAD]
                s = lax.dot_general(kv, qv, _NT, preferred_element_type=F32) * ATTN_SCALE
                if diagonal:
                    s = jnp.where(_diag_mask(blk), s, -jnp.inf)
                m_new = jnp.maximum(m_i, jnp.max(s, axis=0, keepdims=True))
                p = jnp.exp(s - m_new)
                alpha = jnp.exp(m_i - m_new)
                l_new = alpha * l_i + jnp.sum(p, axis=0, keepdims=True)
                acc[h] = alpha * acc[h] + jnp.dot(vt_ref[h * V_DIM:(h + 1) * V_DIM, pl.ds(k0, blk)], p.astype(BF16),
                                                  preferred_element_type=F32)
                out.append((m_new, l_new))
            return tuple(out)

        init = tuple((jnp.full((1, blk), -jnp.inf, F32), jnp.zeros((1, blk), F32)) for _ in range(hg))
        carry = lax.fori_loop(0, qi, lambda j, c: step(j, c, False), init)
        stats = step(qi, carry, True)
        for h in range(hg):
            m_i, l_i = stats[h]
            o_ref[:, h * V_DIM:(h + 1) * V_DIM] = (acc[h] / l_i).T
            lse_ref[h] = m_i + jnp.log(l_i)

        @pl.when(last)
        def _():
            _exchange(blk_ref, all_ref, send_sems, recv_sems, local_sem, finish=True, gather=True)

    return pl.pallas_call(
        body, grid=steps,
        in_specs=[pl.BlockSpec((blk, hg * HEAD_PAD), lambda b, g, i: (b * nq + i, g)),
                  pl.BlockSpec((seq, hg * QK_NOPE), lambda b, g, i: (b, g)),
                  pl.BlockSpec((seq, LANES), lambda b, g, i: (b, 0)),
                  pl.BlockSpec((hg * V_DIM, seq), lambda b, g, i: (g, b)), _ANY],
        out_specs=[pl.BlockSpec((blk, hg * V_DIM), lambda b, g, i: (b * nq + i, g)),
                   pl.BlockSpec((hg, 1, blk), lambda b, g, i: (g, 0, b * nq + i)), _ANY],
        out_shape=[jax.ShapeDtypeStruct((t, N_HEADS * V_DIM), F32), jax.ShapeDtypeStruct((N_HEADS, 1, t), F32),
                   jax.ShapeDtypeStruct((N_DEV,) + block.shape, block.dtype)],
        scratch_shapes=[pltpu.VMEM((hg, V_DIM, blk), F32)] + _EXCHANGE_SEMS,
        compiler_params=_params(("arbitrary", "arbitrary", "arbitrary")), name=name)(q, kn, kr, v_t, block)


def _attn_bwd(q, kn, kr, kn_t, kr_t, v, o, lse, do, cos, sin, parts, *, bsz, seq, name):
    t = bsz * seq
    blk = _attn_block(seq)
    nq = seq // blk
    hg = BWD_HEADS
    steps = (bsz, N_HEADS // hg)

    def body(q_ref, kn_ref, kr_ref, knt_ref, krt_ref, v_ref, o_ref, lse_ref, do_ref, cos_ref, sin_ref, parts_ref,
             dq_ref, dkn_ref, dkr_ref, dv_ref, land_ref, dqt_acc, dk_acc, dv_acc, send_sems, recv_sems, local_sem):
        first, last = _first_last(steps)

        @pl.when(first)
        def _():
            _exchange(parts_ref, land_ref, send_sems, recv_sems, local_sem, finish=False)

        dqt_acc[...] = jnp.zeros_like(dqt_acc)
        dk_acc[...] = jnp.zeros_like(dk_acc)
        dv_acc[...] = jnp.zeros_like(dv_acc)

        def q_block(i, _):
            q0 = pl.multiple_of(i * blk, blk)
            rows = []
            for h in range(hg):
                dov = do_ref[pl.ds(q0, blk), h * V_DIM:(h + 1) * V_DIM].astype(F32)
                dcol = jnp.sum(dov * o_ref[pl.ds(q0, blk), h * V_DIM:(h + 1) * V_DIM], axis=-1, keepdims=True)
                delta = jnp.broadcast_to(dcol, (blk, LANES)).T[0:1, :]
                rows.append((lse_ref[h, :, pl.ds(q0, blk)], delta))

            def pair(j, diagonal):
                k0 = pl.multiple_of(j * blk, blk)
                kr_j = kr_ref[pl.ds(k0, blk), :]
                krt_j = krt_ref[:, pl.ds(k0, blk)]
                for h in range(hg):
                    lse_i, delta = rows[h]
                    qv = q_ref[pl.ds(q0, blk), h * HEAD_PAD:(h + 1) * HEAD_PAD]
                    dov = do_ref[pl.ds(q0, blk), h * V_DIM:(h + 1) * V_DIM]
                    kv = jnp.concatenate([kn_ref[pl.ds(k0, blk), h * QK_NOPE:(h + 1) * QK_NOPE], kr_j], axis=1)
                    s = lax.dot_general(kv, qv, _NT, preferred_element_type=F32) * ATTN_SCALE
                    p = jnp.exp(s - lse_i)
                    if diagonal:
                        p = jnp.where(_diag_mask(blk), p, 0.0)
                    dv_acc[pl.ds(k0, blk), h * V_DIM:(h + 1) * V_DIM] += jnp.dot(
                        p.astype(BF16), dov, preferred_element_type=F32)
                    dp = lax.dot_general(v_ref[pl.ds(k0, blk), h * V_DIM:(h + 1) * V_DIM], dov, _NT,
                                         preferred_element_type=F32)
                    ds = (p * (dp - delta) * ATTN_SCALE).astype(BF16)
                    dk_acc[pl.ds(k0, blk), h * HEAD_PAD:(h + 1) * HEAD_PAD] += jnp.dot(ds, qv, preferred_element_type=F32)
                    base = h * HEAD_PAD
                    dqt_acc[base:base + QK_NOPE, pl.ds(q0, blk)] += jnp.dot(
                        knt_ref[h * QK_NOPE:(h + 1) * QK_NOPE, pl.ds(k0, blk)], ds, preferred_element_type=F32)
                    dqt_acc[base + QK_NOPE:base + HEAD_PAD, pl.ds(q0, blk)] += jnp.dot(
                        krt_j, ds, preferred_element_type=F32)

            def off_diagonal(j, _):
                pair(j, False)
                return 0

            lax.fori_loop(0, i, off_diagonal, 0)
            pair(i, True)
            return 0

        lax.fori_loop(0, nq, q_block, 0)
        dkr = jnp.zeros((seq, LANES), F32)
        for h in range(hg):
            base = h * HEAD_PAD
            for i in range(nq):
                rows = slice(i * blk, (i + 1) * blk)
                dq = dqt_acc[base:base + HEAD_PAD, rows].T
                dq_ref[rows, base:base + QK_NOPE] = dq[:, :QK_NOPE].astype(BF16)
                dq_ref[rows, base + QK_NOPE:base + HEAD_PAD] = _rope_t(
                    dq[:, QK_NOPE:], cos_ref[rows, :], sin_ref[rows, :]).astype(BF16)
            dkn_ref[:, h * QK_NOPE:(h + 1) * QK_NOPE] = dk_acc[:, base:base + QK_NOPE].astype(BF16)
            dkr = dkr + dk_acc[:, base + QK_NOPE:base + HEAD_PAD]
        dv_ref[...] = dv_acc[...].astype(BF16)

        @pl.when(pl.program_id(1) == 0)
        def _():
            dkr_ref[...] = jnp.zeros_like(dkr_ref)

        dkr_ref[...] += _rope_t(dkr, cos_ref[...], sin_ref[...])

        @pl.when(last)
        def _():
            _exchange(parts_ref, land_ref, send_sems, recv_sems, local_sem, finish=True)

    head = pl.BlockSpec((seq, hg * V_DIM), lambda b, g: (b, g))
    head_t = pl.BlockSpec((hg * V_DIM, seq), lambda b, g: (g, b))
    shared = pl.BlockSpec((seq, LANES), lambda b, g: (b, 0))
    shared_t = pl.BlockSpec((LANES, seq), lambda b, g: (0, b))
    table = pl.BlockSpec((seq, LANES), lambda b, g: (0, 0))
    qspec = pl.BlockSpec((seq, hg * HEAD_PAD), lambda b, g: (b, g))
    return pl.pallas_call(
        body, grid=steps,
        in_specs=[qspec, head, shared, head_t, shared_t, head, head,
                  pl.BlockSpec((hg, 1, seq), lambda b, g: (g, 0, b)), head, table, table, _ANY],
        out_specs=[qspec, head, shared, head, _ANY],
        out_shape=[jax.ShapeDtypeStruct((t, N_HEADS * HEAD_PAD), BF16), jax.ShapeDtypeStruct((t, N_HEADS * QK_NOPE), BF16),
                   jax.ShapeDtypeStruct((t, LANES), F32), jax.ShapeDtypeStruct((t, N_HEADS * V_DIM), BF16),
                   jax.ShapeDtypeStruct(parts.shape, parts.dtype)],
        scratch_shapes=[pltpu.VMEM((hg * HEAD_PAD, seq), F32), pltpu.VMEM((seq, hg * HEAD_PAD), F32),
                        pltpu.VMEM((seq, hg * V_DIM), F32)] + _EXCHANGE_SEMS,
        compiler_params=_params(("arbitrary", "arbitrary")), name=name)(
            q, kn, kr, kn_t, kr_t, v, o, lse, do, cos, sin, parts)


def _head_and_loss(o, g2, x1, target, w_out, g_final, *, name, bt=512):
    t, d = x1.shape
    bt = min(bt, t)
    nt = (((1,), (1,)), ((), ()))
    tn = (((0,), (0,)), ((), ()))

    def body(o_ref, g2_ref, x1_ref, tgt_ref, w_ref, gf_ref, loss_ref, dx2_ref, dw_ref, do_ref, dg2_ref, dgf_ref):
        @pl.when(pl.program_id(0) == 0)
        def _():
            for ref in (loss_ref, dgf_ref, dw_ref):
                ref[...] = jnp.zeros_like(ref)

        ov = o_ref[...]
        gv = g2_ref[...]
        sg = _sigmoid(gv)
        silu = gv * sg
        y2 = (ov * silu).astype(BF16)
        w = w_ref[...]
        x2 = x1_ref[...] + jnp.dot(y2, w, preferred_element_type=F32)
        r = lax.rsqrt(jnp.mean(x2 * x2, axis=-1, keepdims=True) + EPS)
        nrm = x2 * r
        gf = gf_ref[...]
        err = nrm * gf - tgt_ref[...]
        loss_ref[...] += 0.5 * jnp.sum(jnp.mean(err * err, axis=-1, keepdims=True))
        dyf = err * (1.0 / d)
        dgf_ref[...] += jnp.sum(dyf * nrm, axis=0, keepdims=True)
        dn = dyf * gf
        dx2 = r * (dn - nrm * jnp.mean(dn * nrm, axis=-1, keepdims=True))
        dx2_ref[...] = dx2
        dx2 = dx2.astype(BF16)
        dw_ref[...] += lax.dot_general(y2, dx2, tn, preferred_element_type=F32)
        dy2 = lax.dot_general(dx2, w, nt, preferred_element_type=F32)
        do_ref[...] = (dy2 * silu).astype(BF16)
        dg2_ref[...] = (dy2 * ov * (sg * (1.0 + gv * (1.0 - sg)))).astype(BF16)

    row = pl.BlockSpec((bt, d), lambda i: (i, 0))
    vec = pl.BlockSpec((1, d), lambda i: (0, 0))
    return pl.pallas_call(
        body, grid=(t // bt,),
        in_specs=[row, row, row, row, pl.BlockSpec((d, d), lambda i: (0, 0)), vec],
        out_specs=[pl.BlockSpec((8, LANES), lambda i: (0, 0)), row, pl.BlockSpec((d, d), lambda i: (0, 0)), row, row, vec],
        out_shape=[jax.ShapeDtypeStruct((8, LANES), F32), jax.ShapeDtypeStruct((t, d), F32),
                   jax.ShapeDtypeStruct((d, d), F32), jax.ShapeDtypeStruct((t, d), BF16),
                   jax.ShapeDtypeStruct((t, d), BF16), jax.ShapeDtypeStruct((1, d), F32)],
        compiler_params=_params(("arbitrary",)), name=name)(o, g2, x1, target, w_out, g_final)


def _sum_parts(parts, *, name):
    k = len(parts)

    def body(*refs):
        for p_ref, o_ref in zip(refs[:k], refs[k:]):
            acc = p_ref[0].astype(F32)
            for j in range(1, p_ref.shape[0]):
                acc = acc + p_ref[j].astype(F32)
            o_ref[...] = acc

    whole = pl.BlockSpec(memory_space=pltpu.VMEM)
    return pl.pallas_call(
        body, in_specs=[whole] * k, out_specs=[whole] * k,
        out_shape=[jax.ShapeDtypeStruct(p.shape[1:], F32) for p in parts],
        compiler_params=_params(), name=name)(*parts)


def _chip_partial(parts, recv, *, name, br=GRAD_BLOCK):
    _, rows, w = parts.shape
    core = lax.axis_index("c").astype(jnp.int32).reshape(1)

    def body(c_ref, p_ref, r_ref, o_ref):
        o_ref[...] = (p_ref[...] + r_ref[...]).astype(BF16)

    grid_spec = pltpu.PrefetchScalarGridSpec(
        num_scalar_prefetch=1, grid=(4, rows // br),
        in_specs=[pl.BlockSpec((None, br, w), lambda k, i, c_ref: (2 * k + c_ref[0], i, 0)),
                  pl.BlockSpec((None, br, w), lambda k, i, c_ref: (k, i, 0))],
        out_specs=pl.BlockSpec((None, br, w), lambda k, i, c_ref: (k, i, 0)))
    return pl.pallas_call(
        body, grid_spec=grid_spec, out_shape=jax.ShapeDtypeStruct((4, rows, w), BF16),
        compiler_params=_params(("parallel", "parallel")), name=name)(core, parts, recv)


def _as_block(a):
    if a.ndim == 1:
        return a.reshape(1, -1)
    if a.ndim > 2 and a.shape[0] == 1:
        return a.reshape(a.shape[1:])
    return a


def _adamw(grads, weights, mom1, mom2, *, name):
    k = len(weights)
    shapes = [w.shape for w in weights]
    args = [_as_block(a) for group in (grads, weights, mom1, mom2) for a in group]

    def body(*refs):
        for i in range(k):
            g_ref, w_ref, m_ref, v_ref, d_ref, nm_ref, nv_ref = (refs[j * k + i] for j in range(7))
            gv = g_ref[...]
            nm = ADAM_B1 * m_ref[...] + (1.0 - ADAM_B1) * gv
            nv = ADAM_B2 * v_ref[...] + (1.0 - ADAM_B2) * (gv * gv)
            nm_ref[...] = nm
            nv_ref[...] = nv
            m_hat = nm / (1.0 - ADAM_B1 ** ADAM_STEP)
            v_hat = nv / (1.0 - ADAM_B2 ** ADAM_STEP)
            d_ref[...] = (-ADAM_LR) * (m_hat / (jnp.sqrt(v_hat) + ADAM_EPS) + ADAM_WD * w_ref[...])

    whole = pl.BlockSpec(memory_space=pltpu.VMEM)
    outs = pl.pallas_call(
        body, in_specs=[whole] * (4 * k), out_specs=[whole] * (3 * k),
        out_shape=[jax.ShapeDtypeStruct(a.shape, F32) for a in args[k:2 * k]] * 3,
        compiler_params=_params(), name=name)(*args)
    return [[o.reshape(s) for o, s in zip(outs[j * k:(j + 1) * k], shapes)] for j in range(3)]


def _all_gather(block, *, name):
    m, n = block.shape

    def body(x_ref, out_ref, send_sems, recv_sems, local_sem):
        for phase in range(3):
            _gather_two_level(x_ref, out_ref, send_sems, recv_sems, local_sem, phase=phase)

    return pl.pallas_call(
        body, out_shape=jax.ShapeDtypeStruct((N_DEV, m, n), block.dtype), in_specs=[_ANY], out_specs=_ANY,
        scratch_shapes=_EXCHANGE_SEMS, name=name)(block)


def _exchange_d2d(parts, *, name):
    _, rows, w = parts.shape

    def body(p_ref, land_ref, send_sems, recv_sems):
        x, y, c = _mesh_pos()
        sends = []
        for k in range(4):
            cp = pltpu.make_async_remote_copy(
                src_ref=p_ref.at[2 * k + (1 - c)], dst_ref=land_ref.at[k], send_sem=send_sems.at[k],
                recv_sem=recv_sems.at[k], device_id=(x, y, 1 - c), device_id_type=pl.DeviceIdType.MESH)
            cp.start()
            sends.append(cp)
        for cp in sends:
            cp.wait_recv()
        for cp in sends:
            cp.wait_send()

    return pl.pallas_call(
        body, out_shape=jax.ShapeDtypeStruct((4, rows, w), parts.dtype), in_specs=[_ANY], out_specs=_ANY,
        scratch_shapes=[pltpu.SemaphoreType.DMA((4,)), pltpu.SemaphoreType.DMA((4,))], name=name)(parts)


def _exchange_ici(parts, *, name):
    def body(p_ref, land_ref, send_sems, recv_sems, local_sem):
        x, y, c = _mesh_pos()
        mine = pltpu.make_async_copy(p_ref.at[2 * x + y], land_ref.at[3], local_sem)
        mine.start()
        sends = []
        for k, (px, py) in enumerate([(1 - x, y), (x, 1 - y), (1 - x, 1 - y)]):
            cp = pltpu.make_async_remote_copy(
                src_ref=p_ref.at[2 * px + py], dst_ref=land_ref.at[k], send_sem=send_sems.at[k],
                recv_sem=recv_sems.at[k], device_id=(px, py, c), device_id_type=pl.DeviceIdType.MESH)
            cp.start()
            sends.append(cp)
        for cp in sends:
            cp.wait_recv()
        for cp in sends:
            cp.wait_send()
        mine.wait()

    return pl.pallas_call(
        body, out_shape=jax.ShapeDtypeStruct(parts.shape, parts.dtype), in_specs=[_ANY], out_specs=_ANY,
        scratch_shapes=[pltpu.SemaphoreType.DMA((3,)), pltpu.SemaphoreType.DMA((3,)), pltpu.SemaphoreType.DMA(())],
        name=name)(parts)


def _rows(a):
    return a.reshape(-1, PACK_W)


def _pad_to(a, n):
    return jnp.pad(a, (0, n - a.shape[0]))


def _weight_blocks(d):
    small = _rows(_pad_to(jnp.concatenate([d[n].reshape(-1) for n, _ in _SMALL]), 16 * PACK_W))
    bits = lax.bitcast_convert_type(small, jnp.uint32)
    halves = [lax.bitcast_convert_type(h.astype(jnp.uint16), WIRE) for h in (bits >> 16, bits & 0xFFFF)]
    block_a = jnp.concatenate([d["w_in_a"][0].T.astype(WIRE)] + halves, axis=0)
    w_uq = jnp.pad(d["w_uq"][0], ((0, 0), (0, 0), (0, HEAD_PAD - QK_NOPE - QK_ROPE)))
    pieces = {"w_out_a": d["w_out_a"], "w_dkv": d["w_dkv"], "w_uk": d["w_uk"], "w_uv": d["w_uv"],
              "w_in_b": d["w_in_b"][0].T, "w_uq": w_uq}
    block_b = jnp.concatenate([_rows(pieces[n]) for n, _ in _PIECES_B]
                              + [jnp.zeros((WIRE_ROWS_B - MATRIX_ROWS_B, PACK_W), F32)], axis=0).astype(WIRE)
    return block_a, block_b, d["w_out_b"][0].astype(WIRE)


def _weights_a(wall):
    w = {}
    lo, hi = _OFF_A["w_in_a"]
    w["w_in_a_t"] = wall[:, lo:hi].reshape(2 * D_RNN, D_MODEL)
    high, low = (lax.bitcast_convert_type(wall[:, r:r + 16], jnp.uint16).astype(jnp.uint32)
                 for r in (MATRIX_ROWS_A, MATRIX_ROWS_A + 16))
    small = lax.bitcast_convert_type((high << 16) | low, F32)[:, :8].reshape(N_DEV, 8 * PACK_W)
    off = dict(zip([n for n, _ in _SMALL], [0, 128, 768, 928, 1088, 1248]))
    w["norm_a"] = small[:, :128].reshape(1, D_MODEL)

    def by_channel(lo, rows):
        a = small[:, lo:lo + rows * (D_RNN // N_DEV)].reshape(N_DEV, rows, -1).transpose(1, 0, 2).reshape(rows, D_RNN)
        return jnp.pad(a, ((0, 8 - rows), (0, 0)))

    w["conv_taps"] = by_channel(off["conv_w"], CONV_WIDTH)
    w["lru_vecs"] = by_channel(off["conv_b"], 4)
    return w


def _weights_b(wall):
    piece = {n: wall[:, lo:hi] for n, (lo, hi) in _OFF_B.items()}
    w = {"w_out_a": piece["w_out_a"].reshape(D_RNN, D_MODEL)}
    w_dkv = piece["w_dkv"].reshape(D_MODEL, KV_RANK + QK_ROPE)
    w["w_dkv_c"] = w_dkv[:, :KV_RANK]
    w["w_dkv_r"] = jnp.pad(w_dkv[:, KV_RANK:], ((0, 0), (0, LANES - QK_ROPE)))
    w["w_uk"] = piece["w_uk"].reshape(KV_RANK, N_HEADS * QK_NOPE)
    w["w_uv"] = piece["w_uv"].reshape(KV_RANK, N_HEADS * V_DIM)
    w["w_in_b_t"] = piece["w_in_b"].reshape(Q_RANK + N_HEADS * V_DIM, D_MODEL)
    w["w_uq"] = piece["w_uq"].reshape(Q_RANK, N_HEADS * HEAD_PAD)
    return w


def _pack_rep(d):
    flat = jnp.concatenate([d[n].reshape(-1) for n, _ in _REP])
    return _rows(_pad_to(flat, REP_ROWS * PACK_W))


def _unpack_rep(p, like):
    flat = p.reshape(-1)
    out, off = {}, 0
    for n, k in _REP:
        out[n] = flat[off:off + k].reshape(like[n].shape)
        off += k
    return out


def _by_owner(a):
    return a.reshape(N_DEV, -1, PACK_W)


def _grad_parts_b(g):
    tail = jnp.zeros((N_DEV, WIRE_ROWS_B - MATRIX_ROWS_B, PACK_W), F32)
    return jnp.concatenate([_by_owner(g[n]) for n, _ in _PIECES_B] + [tail], axis=1).astype(BF16)


def _grad_parts_a(g):
    small = jnp.concatenate([
        g["norm_a"].reshape(N_DEV, -1),
        g["conv_w"].reshape(CONV_WIDTH, N_DEV, -1).transpose(1, 0, 2).reshape(N_DEV, -1),
        g["conv_b"].reshape(N_DEV, -1), g["b_rg"].reshape(N_DEV, -1), g["b_ig"].reshape(N_DEV, -1),
        g["lru_lambda"].reshape(N_DEV, -1)], axis=1)
    small = jnp.pad(small, ((0, 0), (0, 8 * PACK_W - small.shape[1]))).reshape(N_DEV, 8, PACK_W)
    half = N_DEV // 2
    w_in_a = jnp.concatenate([h.reshape(half, -1, PACK_W) for h in g["w_in_a_t"]], axis=0)
    rep = _pack_rep(g).reshape(N_DEV, REP_SLICE, PACK_W)
    tail = jnp.zeros((N_DEV, GRAD_ROWS_A - MATRIX_ROWS_A - 8 - REP_SLICE, PACK_W), F32)
    return jnp.concatenate([w_in_a, small, rep, tail], axis=1)


def _own_grads(sum_a, sum_b, sum_c):
    out = {}
    lo, hi = _OFF_A["w_in_a"]
    out["w_in_a"] = sum_a[lo:hi].T.reshape(1, D_MODEL, 2 * D_RNN // N_DEV)
    small = sum_a[MATRIX_ROWS_A:MATRIX_ROWS_A + 8].reshape(-1)
    shapes = {"norm_a": (1, D_MODEL // N_DEV), "conv_w": (1, CONV_WIDTH, D_RNN // N_DEV), "conv_b": (1, D_RNN // N_DEV),
              "b_rg": (1, D_RNN // N_DEV), "b_ig": (1, D_RNN // N_DEV), "lru_lambda": (1, D_RNN // N_DEV)}
    off = 0
    for n, k in _SMALL:
        out[n] = small[off:off + k].reshape(shapes[n])
        off += k
    piece = {n: sum_b[lo:hi] for n, (lo, hi) in _OFF_B.items()}
    out["w_out_a"] = piece["w_out_a"].reshape(1, D_RNN // N_DEV, D_MODEL)
    out["w_dkv"] = piece["w_dkv"].reshape(D_MODEL // N_DEV, KV_RANK + QK_ROPE)
    out["w_uk"] = piece["w_uk"].reshape(KV_RANK // N_DEV, N_HEADS, QK_NOPE)
    out["w_uv"] = piece["w_uv"].reshape(KV_RANK // N_DEV, N_HEADS, V_DIM)
    out["w_in_b"] = piece["w_in_b"].T.reshape(1, D_MODEL, (Q_RANK + N_HEADS * V_DIM) // N_DEV)
    out["w_uq"] = piece["w_uq"].reshape(1, Q_RANK // N_DEV, N_HEADS, HEAD_PAD)[..., :QK_NOPE + QK_ROPE]
    out["w_out_b"] = sum_c.reshape(1, N_HEADS * V_DIM // N_DEV, D_MODEL)
    return out


def _step(x, target, w, rep, block_b, block_c, *, bsz, seq):
    t = bsz * seq
    cos, sin = _rope_tables(seq)
    g_a = w["norm_a"]
    g_kv = rep["norm_kv"].reshape(1, -1)
    g_kvn = rep["kv_norm"].reshape(1, -1)
    g_b = rep["norm_b"].reshape(1, -1)
    g_q = rep["q_norm"].reshape(1, -1)
    g_f = rep["final_norm"].reshape(1, -1)
    wrg = rep["w_rg"][0].astype(BF16)
    wig = rep["w_ig"][0].astype(BF16)
    cw8, vecs = w["conv_taps"], w["lru_vecs"]

    def seq3(a):
        return a.reshape(bsz, seq, a.shape[-1])

    def flat(a):
        return a.reshape(t, a.shape[-1])

    h0, xp, ga = _lru_proj_fwd(x, g_a, w["w_in_a_t"], name="lru_proj_fwd")
    xb, hs, y, wall_b = _lru_fwd(seq3(xp), seq3(ga), cw8, vecs, wrg, wig, block_b, name="lru_fwd")
    w = dict(w, **_weights_b(wall_b))
    x1, hk, hq, ck, cqp, g2, ckv, cq, q, kn, v, kr, kn_t, v_t, kr_t = _mla_proj_fwd(
        flat(y), x, (g_kv, g_b, g_kvn, g_q), w, cos, sin, seq=seq, name="mla_proj_fwd")
    o, lse, wall_c = _attn_fwd(q, kn, kr, v_t, block_c, bsz=bsz, seq=seq, name="attn_fwd")
    w_out_b = wall_c.reshape(N_HEADS * V_DIM, D_MODEL)
    loss, dx2, d_w_out_b, do, dg2, dgf = _head_and_loss(o, g2, x1, target, w_out_b, g_f, name="head_loss")
    grads = {"final_norm": dgf}
    parts_c = _by_owner(d_w_out_b).astype(BF16)
    dq, dkn, dkr, dv, landed_c = _attn_bwd(q, kn, kr, kn_t, kr_t, v, o, lse, do, cos, sin, parts_c,
                                           bsz=bsz, seq=seq, name="attn_bwd")
    dx1, du2, dckr, dgkv, dgb, dgkvn, dgq, dy = _mla_proj_bwd(
        x1, dx2, cqp, ck, dq, dkn, dv, dkr, dg2, (g_kv, g_b, g_kvn, g_q), w, name="mla_proj_bwd")
    grads["norm_kv"], grads["norm_b"], grads["kv_norm"], grads["q_norm"] = dgkv, dgb, dgkvn, dgq
    grads["w_uq"], grads["w_uk"], grads["w_uv"], d_w_dkv = _token_sums(
        [cq, dq, ckv, dkn, dv, hk, dckr], [(0, 1), (2, 3), (2, 4), (5, 6)], name="d_w_uq_uk_uv_dkv")
    grads["w_dkv"] = d_w_dkv[:, :KV_RANK + QK_ROPE]
    grads["w_in_b"], grads["w_out_a"] = _token_sums(
        [du2, hq, flat(y), dx1], [(0, 1), (2, 3)], name="d_w_in_b_t_out_a")
    parts_b = _grad_parts_b(grads)
    dxp, dga, dwrg, dwig, dvec, landed_b = _lru_bwd(
        seq3(dy), seq3(xp), xb, hs, seq3(ga), cw8, vecs, wrg, wig, parts_b, name="lru_bwd")
    dxp, dga = flat(dxp), flat(dga)
    grads["w_rg"], grads["w_ig"] = dwrg, dwig
    grads["b_rg"], grads["b_ig"], grads["conv_b"] = dvec[0], dvec[1], dvec[3]
    lam = vecs[3]
    grads["lru_lambda"] = dvec[2] * (-1.0 / (1.0 + jnp.exp(lam)))
    grads["conv_w"] = dvec[4:4 + CONV_WIDTH]
    dx, dga_norm, dwx, dwg = _lru_proj_bwd(dxp, dga, x, dx1, h0, g_a, w["w_in_a_t"], name="lru_proj_bwd")
    grads["norm_a"] = dga_norm
    grads["w_in_a_t"] = (dwx, dwg)
    return loss[0, 0], dx, grads, landed_b, landed_c


def kernel(x, norm_a, w_in_a, conv_w, conv_b, w_rg, b_rg, w_ig, b_ig, lru_lambda, w_out_a, norm_kv, w_dkv, kv_norm, w_uk, w_uv, norm_b, w_in_b, q_norm, w_uq, w_out_b, final_norm, loss_target, m_norm_a, m_w_in_a, m_conv_w, m_conv_b, m_w_rg, m_b_rg, m_w_ig, m_b_ig, m_lru_lambda, m_w_out_a, m_norm_kv, m_w_dkv, m_kv_norm, m_w_uk, m_w_uv, m_norm_b, m_w_in_b, m_q_norm, m_w_uq, m_w_out_b, m_final_norm, v_norm_a, v_w_in_a, v_conv_w, v_conv_b, v_w_rg, v_b_rg, v_w_ig, v_b_ig, v_lru_lambda, v_w_out_a, v_norm_kv, v_w_dkv, v_kv_norm, v_w_uk, v_w_uv, v_norm_b, v_w_in_b, v_q_norm, v_w_uq, v_w_out_b, v_final_norm):
    given = dict(locals())
    wts = {n: given[n] for n in WEIGHTS}
    mom1 = {n: given["m_" + n] for n in WEIGHTS}
    mom2 = {n: given["v_" + n] for n in WEIGHTS}
    bsz, seq, _ = x.shape
    t = bsz * seq

    block_a, block_b, block_c = _weight_blocks(wts)
    w = _weights_a(_all_gather(block_a, name="gather_weights_a"))
    loss, dx, grads, landed_b, landed_c = _step(x.reshape(t, D_MODEL), loss_target.reshape(t, D_MODEL), w, wts,
                                                block_b, block_c, bsz=bsz, seq=seq)

    parts_a = _grad_parts_a(grads)
    from_sibling = _exchange_d2d(parts_a, name="exchange_grads_d2d")
    chip_parts = _chip_partial(parts_a, from_sibling, name="chip_partial_grads")
    landed_a = _exchange_ici(chip_parts, name="exchange_grads_ici")
    sum_a, sum_b, sum_c = _sum_parts([landed_a, landed_b, landed_c], name="sum_grads")
    g_own = _own_grads(sum_a, sum_b, sum_c)
    rep_slice = sum_a[MATRIX_ROWS_A + 8:MATRIX_ROWS_A + 8 + REP_SLICE]
    loss_rows = jnp.pad(loss.reshape(1, 1), ((0, 7), (0, PACK_W - 1)))
    gathered = _all_gather(jnp.concatenate([rep_slice, loss_rows], axis=0), name="gather_replicated")
    g_own.update(_unpack_rep(gathered[:, :REP_SLICE].reshape(REP_ROWS, PACK_W), wts))
    loss = jnp.sum(gathered[:, REP_SLICE, 0])

    own = [g_own[n] for n in WEIGHTS]
    deltas, new_m, new_v = _adamw(own, *([d[n] for n in WEIGHTS] for d in (wts, mom1, mom2)), name="adamw")
    return (loss, dx.reshape(bsz, seq, D_MODEL), *own, *deltas, *new_m, *new_v)
```

```python
import jax
import jax.numpy as jnp
from jax import lax
from jax.experimental import pallas as pl
from jax.experimental.pallas import tpu as pltpu

F32 = jnp.float32
BF16 = jnp.bfloat16
WIRE = jnp.bfloat16

D_MODEL = 1024
D_RNN = 1280
RNN_BLOCKS = 10
RNN_BW = 128
CONV_WIDTH = 4
LRU_C = 8.0
N_HEADS = 8
QK_NOPE = 128
QK_ROPE = 64
V_DIM = 128
KV_RANK = 256
Q_RANK = 384
ROPE_THETA = 10000.0
EPS = 1e-6
ATTN_SCALE = (QK_NOPE + QK_ROPE) ** -0.5
HEAD_PAD = 256
LANES = 128

ADAM_LR = 0.001
ADAM_B1 = 0.9
ADAM_B2 = 0.999
ADAM_EPS = 1e-08
ADAM_WD = 0.01
ADAM_STEP = 10

N_DEV = 8
VMEM_LIMIT_BYTES = 56 * 2**20
PACK_W = 1024

_PIECES_A = (("w_in_a", 320),)
_PIECES_B = (("w_out_a", 160), ("w_dkv", 40), ("w_uk", 32), ("w_uv", 32), ("w_in_b", 176), ("w_uq", 96))


def _offsets(pieces):
    off, r = {}, 0
    for n, k in pieces:
        off[n] = (r, r + k)
        r += k
    return off, r


_OFF_A, MATRIX_ROWS_A = _offsets(_PIECES_A)
_OFF_B, MATRIX_ROWS_B = _offsets(_PIECES_B)
WIRE_ROWS_A = MATRIX_ROWS_A + 32
WIRE_ROWS_B = 544
_SMALL = (("norm_a", 128), ("conv_w", 640), ("conv_b", 160), ("b_rg", 160), ("b_ig", 160), ("lru_lambda", 160))
_REP = (("w_rg", 163840), ("w_ig", 163840), ("norm_kv", 1024), ("kv_norm", 256), ("norm_b", 1024),
        ("q_norm", 384), ("final_norm", 1024))
REP_ROWS = 384
REP_SLICE = REP_ROWS // N_DEV
GRAD_ROWS_A = 384
GRAD_BLOCK = 192

WEIGHTS = ("norm_a", "w_in_a", "conv_w", "conv_b", "w_rg", "b_rg", "w_ig", "b_ig", "lru_lambda", "w_out_a",
           "norm_kv", "w_dkv", "kv_norm", "w_uk", "w_uv", "norm_b", "w_in_b", "q_norm", "w_uq", "w_out_b",
           "final_norm")


def _params(sem=None):
    return pltpu.CompilerParams(dimension_semantics=sem, vmem_limit_bytes=VMEM_LIMIT_BYTES)


_NT = (((1,), (1,)), ((), ()))
_ANY = pl.BlockSpec(memory_space=pl.ANY)


def _mesh_pos():
    return lax.axis_index("x"), lax.axis_index("y"), lax.axis_index("c")


def _sigmoid(z):
    return 0.5 * jnp.tanh(0.5 * z) + 0.5


def _sigmoid_tail(z):
    return 1.0 / (1.0 + jnp.exp(-z))


def _token_sums(operands, pairs, *, name, bt=1024):
    t = operands[0].shape[0]
    bt = min(bt, t)
    k = len(operands)

    def body(*refs):
        ins, outs = refs[:k], refs[k:]

        @pl.when(pl.program_id(0) == 0)
        def _():
            for o_ref in outs:
                o_ref[...] = jnp.zeros_like(o_ref)

        vals = [r[...].astype(BF16) for r in ins]
        for (i, j), o_ref in zip(pairs, outs):
            o_ref[...] += lax.dot_general(vals[i], vals[j], (((0,), (0,)), ((), ())), preferred_element_type=F32)

    shapes = [(operands[i].shape[1], operands[j].shape[1]) for i, j in pairs]
    return pl.pallas_call(
        body, grid=(t // bt,),
        in_specs=[pl.BlockSpec((bt, a.shape[1]), lambda s: (s, 0)) for a in operands],
        out_specs=[pl.BlockSpec(shape, lambda s: (0, 0)) for shape in shapes],
        out_shape=[jax.ShapeDtypeStruct(shape, F32) for shape in shapes],
        compiler_params=_params(("arbitrary",)), name=name)(*operands)


def _swap_halves(v):
    ax = v.ndim - 1
    lane = lax.broadcasted_iota(jnp.int32, v.shape, ax)
    up = pltpu.roll(v, LANES - QK_ROPE // 2, axis=ax)
    down = pltpu.roll(v, QK_ROPE // 2, axis=ax)
    return jnp.where(lane < QK_ROPE // 2, up, jnp.where(lane < QK_ROPE, down, 0.0))


def _rope(v, cos, sin):
    return v * cos + _swap_halves(v) * sin


def _rope_t(d, cos, sin):
    return d * cos + _swap_halves(d * sin)


def _rope_tables(seq):
    pos = jnp.arange(seq, dtype=F32)
    inv = ROPE_THETA ** (-jnp.arange(0, QK_ROPE, 2, dtype=F32) / QK_ROPE)
    ang = pos[:, None] * inv[None, :]
    cos, sin = jnp.cos(ang), jnp.sin(ang)
    zero = jnp.zeros((seq, LANES - QK_ROPE), F32)
    return jnp.concatenate([cos, cos, zero], axis=1), jnp.concatenate([-sin, sin, zero], axis=1)


def _rms(v):
    return v * lax.rsqrt(jnp.mean(v * v, axis=-1, keepdims=True) + EPS)


def _const_spec(a):
    return pl.BlockSpec(a.shape, lambda i: (0,) * a.ndim)


def _lru_proj_fwd(x, g_a, w_in_t, *, name, bt=512):
    t, d = x.shape
    bt = min(bt, t)
    n = w_in_t.shape[0] // 2

    def body(x_ref, g_ref, wt_ref, h_ref, xp_ref, ga_ref):
        h = (_rms(x_ref[...]) * g_ref[...]).astype(BF16)
        h_ref[...] = h
        xp_ref[...] = lax.dot_general(h, wt_ref[0:n, :], _NT, preferred_element_type=F32)
        ga_ref[...] = lax.dot_general(h, wt_ref[n:2 * n, :], _NT, preferred_element_type=F32)

    row = lambda w: pl.BlockSpec((bt, w), lambda i: (i, 0))
    return pl.pallas_call(
        body, grid=(t // bt,), in_specs=[row(d), _const_spec(g_a), _const_spec(w_in_t)],
        out_specs=[row(d), row(n), row(n)],
        out_shape=[jax.ShapeDtypeStruct((t, d), BF16), jax.ShapeDtypeStruct((t, n), F32), jax.ShapeDtypeStruct((t, n), F32)],
        compiler_params=_params(("parallel",)), name=name)(x, g_a, w_in_t)


def _mla_proj_fwd(y, x, gains, w, cos, sin, *, seq, name, bt=512):
    t, d = x.shape
    bt = min(bt, seq)
    per_seq = seq // bt
    g_kv, g_b, g_kvn, g_q = gains
    consts = [g_kv, g_b, g_kvn, g_q, w["w_dkv_c"], w["w_dkv_r"], w["w_in_b_t"], w["w_uk"], w["w_uv"], w["w_uq"],
              w["w_out_a"]]

    def body(y_ref, x_ref, cos_ref, sin_ref, gkv_ref, gb_ref, gkvn_ref, gq_ref, wdc_ref, wdr_ref, wbt_ref,
             wuk_ref, wuv_ref, wuq_ref, wo_ref, x1_ref,
             hk_ref, hq_ref, ck_ref, cqp_ref, g2_ref, ckv_ref, cq_ref, q_ref, kn_ref, v_ref, kr_ref, knt_ref, vt_ref, krt_ref):
        x1 = jnp.dot(y_ref[...], wo_ref[...], preferred_element_type=F32) + x_ref[...]
        x1_ref[...] = x1
        nrm = _rms(x1)
        hk = (nrm * gkv_ref[...]).astype(BF16)
        hq = (nrm * gb_ref[...]).astype(BF16)
        hk_ref[...] = hk
        hq_ref[...] = hq
        ck = jnp.dot(hk, wdc_ref[...], preferred_element_type=F32)
        ck_ref[...] = ck
        cqp = lax.dot_general(hq, wbt_ref[0:Q_RANK, :], _NT, preferred_element_type=F32)
        cqp_ref[...] = cqp
        g2_ref[...] = lax.dot_general(hq, wbt_ref[Q_RANK:, :], _NT, preferred_element_type=F32)
        cosv, sinv = cos_ref[...], sin_ref[...]
        kr = _rope(jnp.dot(hk, wdr_ref[...], preferred_element_type=F32), cosv, sinv)
        kr_ref[...] = kr.astype(BF16)
        krt_ref[...] = kr.T.astype(BF16)
        ckv = (_rms(ck) * gkvn_ref[...]).astype(BF16)
        ckv_ref[...] = ckv
        kn = jnp.dot(ckv, wuk_ref[...], preferred_element_type=F32)
        v = jnp.dot(ckv, wuv_ref[...], preferred_element_type=F32)
        kn_ref[...] = kn.astype(BF16)
        v_ref[...] = v.astype(BF16)
        knt_ref[...] = kn.T.astype(BF16)
        vt_ref[...] = v.T.astype(BF16)
        cq = (_rms(cqp) * gq_ref[...]).astype(BF16)
        cq_ref[...] = cq
        for h in range(N_HEADS):
            qh = jnp.dot(cq, wuq_ref[:, h * HEAD_PAD:(h + 1) * HEAD_PAD], preferred_element_type=F32)
            q_ref[:, h * HEAD_PAD:h * HEAD_PAD + QK_NOPE] = qh[:, :QK_NOPE].astype(BF16)
            q_ref[:, h * HEAD_PAD + QK_NOPE:(h + 1) * HEAD_PAD] = _rope(qh[:, QK_NOPE:], cosv, sinv).astype(BF16)

    row = lambda w_: pl.BlockSpec((bt, w_), lambda i: (i, 0))
    col = lambda h_: pl.BlockSpec((h_, bt), lambda i: (0, i))
    tab = pl.BlockSpec((bt, LANES), lambda i: (i % per_seq, 0))
    nh = N_HEADS * V_DIM
    shapes = [((t, d), F32), ((t, d), BF16), ((t, d), BF16), ((t, KV_RANK), F32), ((t, Q_RANK), F32), ((t, nh), F32),
              ((t, KV_RANK), BF16), ((t, Q_RANK), BF16), ((t, N_HEADS * HEAD_PAD), BF16), ((t, nh), BF16), ((t, nh), BF16),
              ((t, LANES), BF16), ((nh, t), BF16), ((nh, t), BF16), ((LANES, t), BF16)]
    out_specs = [row(d), row(d), row(d), row(KV_RANK), row(Q_RANK), row(nh), row(KV_RANK), row(Q_RANK),
                 row(N_HEADS * HEAD_PAD), row(nh), row(nh), row(LANES), col(nh), col(nh), col(LANES)]
    return pl.pallas_call(
        body, grid=(t // bt,), in_specs=[row(D_RNN), row(d), tab, tab] + [_const_spec(a) for a in consts],
        out_specs=out_specs, out_shape=[jax.ShapeDtypeStruct(s, dt) for s, dt in shapes],
        compiler_params=_params(("parallel",)), name=name)(y, x, cos, sin, *consts)


def _rms_bwd_rows(xv, dn):
    r = lax.rsqrt(jnp.mean(xv * xv, axis=-1, keepdims=True) + EPS)
    nrm = xv * r
    return r * (dn - nrm * jnp.mean(dn * nrm, axis=-1, keepdims=True)), nrm


def _col_sum(v):
    return jnp.sum(v, axis=0, keepdims=True)


def _lru_proj_bwd(dxp, dga, x, dx1, h0, g_a, w_in_t, *, name, bt=512):
    t, d = x.shape
    bt = min(bt, t)
    n = w_in_t.shape[0] // 2
    tn = (((0,), (0,)), ((), ()))

    def body(dxp_ref, dga_ref, x_ref, dx1_ref, h0_ref, g_ref, wt_ref, dx_ref, dg_ref, dwx_ref, dwg_ref):
        @pl.when(pl.program_id(0) == 0)
        def _():
            for ref in (dg_ref, dwx_ref, dwg_ref):
                ref[...] = jnp.zeros_like(ref)

        dxp_v, dga_v, h0 = dxp_ref[...], dga_ref[...], h0_ref[...]
        dwx_ref[...] += lax.dot_general(dxp_v, h0, tn, preferred_element_type=F32)
        dwg_ref[...] += lax.dot_general(dga_v, h0, tn, preferred_element_type=F32)
        dh = (jnp.dot(dxp_v, wt_ref[0:n, :], preferred_element_type=F32)
              + jnp.dot(dga_v, wt_ref[n:2 * n, :], preferred_element_type=F32))
        dxn, nrm = _rms_bwd_rows(x_ref[...], dh * g_ref[...])
        dg_ref[...] += _col_sum(dh * nrm)
        dx_ref[...] = dx1_ref[...] + dxn

    row = lambda w: pl.BlockSpec((bt, w), lambda i: (i, 0))
    whole = pl.BlockSpec((n, d), lambda i: (0, 0))
    return pl.pallas_call(
        body, grid=(t // bt,),
        in_specs=[row(n), row(n), row(d), row(d), row(d), _const_spec(g_a), _const_spec(w_in_t)],
        out_specs=[row(d), _const_spec(g_a), whole, whole],
        out_shape=[jax.ShapeDtypeStruct((t, d), F32), jax.ShapeDtypeStruct((1, d), F32),
                   jax.ShapeDtypeStruct((n, d), F32), jax.ShapeDtypeStruct((n, d), F32)],
        compiler_params=_params(("arbitrary",)), name=name)(dxp, dga, x, dx1, h0, g_a, w_in_t)


def _resident_spec(a):
    return pl.BlockSpec(a.shape, lambda i: (0,) * a.ndim, pipeline_mode=pl.Buffered(1))


def _mla_proj_bwd(x1, dx2, cqp, ck, dq, dkn, dv, dkr, dg2, ckv, gains, w, *, name, bt=512):
    t, d = x1.shape
    bt = min(bt, t)
    g_kv, g_b, g_kvn, g_q = gains
    consts = [g_kv, g_b, g_kvn, g_q, w["w_dkv_c"], w["w_dkv_r"], w["w_in_b_t"], w["w_uk"], w["w_uv"], w["w_uq"],
              w["w_out_a"]]
    nh = N_HEADS * V_DIM

    def body(x1_ref, dx2_ref, cqp_ref, ck_ref, dq_ref, dkn_ref, dv_ref, dkr_ref, dg2_ref, ckv_ref,
             gkv_ref, gb_ref, gkvn_ref, gq_ref, wdc_ref, wdr_ref, wbt_ref, wuk_ref, wuv_ref, wuq_ref, wo_ref,
             dx1_ref, du2_ref, dckr_ref, dgkv_ref, dgb_ref, dgkvn_ref, dgq_ref, dy_ref, dwuk_ref, dwuv_ref):
        @pl.when(pl.program_id(0) == 0)
        def _():
            for ref in (dgkv_ref, dgb_ref, dgkvn_ref, dgq_ref, dwuk_ref, dwuv_ref):
                ref[...] = jnp.zeros_like(ref)

        dot_nt = lambda a, b: lax.dot_general(a, b, _NT, preferred_element_type=F32)
        dot_tn = lambda a, b: lax.dot_general(a, b, (((0,), (0,)), ((), ())), preferred_element_type=F32)
        ckv = ckv_ref[...]
        dwuk_ref[...] += dot_tn(ckv, dkn_ref[...])
        dwuv_ref[...] += dot_tn(ckv, dv_ref[...])
        dcq = dot_nt(dq_ref[...], wuq_ref[...])
        dcqp, nq = _rms_bwd_rows(cqp_ref[...], dcq * gq_ref[...])
        dgq_ref[...] += _col_sum(dcq * nq)
        dcqp = dcqp.astype(BF16)
        dg2 = dg2_ref[...]
        du2_ref[:, :Q_RANK] = dcqp
        du2_ref[:, Q_RANK:] = dg2
        dhq = (jnp.dot(dcqp, wbt_ref[0:Q_RANK, :], preferred_element_type=F32)
               + jnp.dot(dg2, wbt_ref[Q_RANK:, :], preferred_element_type=F32))
        dckv = dot_nt(dkn_ref[...], wuk_ref[...]) + dot_nt(dv_ref[...], wuv_ref[...])
        dck, nc = _rms_bwd_rows(ck_ref[...], dckv * gkvn_ref[...])
        dgkvn_ref[...] += _col_sum(dckv * nc)
        dck = dck.astype(BF16)
        dkr = dkr_ref[...].astype(BF16)
        dckr_ref[:, :KV_RANK] = dck
        dckr_ref[:, KV_RANK:] = dkr
        dhk = dot_nt(dck, wdc_ref[...]) + dot_nt(dkr, wdr_ref[...])
        dxn, n1 = _rms_bwd_rows(x1_ref[...], dhq * gb_ref[...] + dhk * gkv_ref[...])
        dgb_ref[...] += _col_sum(dhq * n1)
        dgkv_ref[...] += _col_sum(dhk * n1)
        dx1 = dx2_ref[...] + dxn
        dx1_ref[...] = dx1
        dy_ref[...] = lax.dot_general(dx1.astype(BF16), wo_ref[...], _NT, preferred_element_type=F32)

    row = lambda w_: pl.BlockSpec((bt, w_), lambda i: (i, 0))
    vec = lambda w_: pl.BlockSpec((1, w_), lambda i: (0, 0))
    in_specs = [row(d), row(d), row(Q_RANK), row(KV_RANK), row(N_HEADS * HEAD_PAD), row(nh), row(nh), row(LANES), row(nh),
                row(KV_RANK)]
    dw = pl.BlockSpec((KV_RANK, nh), lambda i: (0, 0))
    return pl.pallas_call(
        body, grid=(t // bt,), in_specs=in_specs + [_resident_spec(a) for a in consts],
        out_specs=[row(d), row(Q_RANK + nh), row(KV_RANK + LANES), vec(d), vec(d), vec(KV_RANK), vec(Q_RANK), row(D_RNN),
                   dw, dw],
        out_shape=[jax.ShapeDtypeStruct((t, d), F32), jax.ShapeDtypeStruct((t, Q_RANK + nh), BF16),
                   jax.ShapeDtypeStruct((t, KV_RANK + LANES), BF16), jax.ShapeDtypeStruct((1, d), F32),
                   jax.ShapeDtypeStruct((1, d), F32), jax.ShapeDtypeStruct((1, KV_RANK), F32),
                   jax.ShapeDtypeStruct((1, Q_RANK), F32), jax.ShapeDtypeStruct((t, D_RNN), F32),
                   jax.ShapeDtypeStruct((KV_RANK, nh), F32), jax.ShapeDtypeStruct((KV_RANK, nh), F32)],
        compiler_params=_params(("arbitrary",)), name=name)(x1, dx2, cqp, ck, dq, dkn, dv, dkr, dg2, ckv, *consts)


def _softplus(z):
    return jnp.maximum(z, 0.0) + jnp.log1p(jnp.exp(-jnp.abs(z)))


def _one_minus_square(a, la):
    return jnp.tanh(-la) * (1.0 + a * a)


def _gates(xb, wrg, wig, brg, big, sp):
    xbb = xb.astype(BF16)
    r = _sigmoid_tail(jnp.dot(xbb, wrg, preferred_element_type=F32) + brg)
    i = _sigmoid(jnp.dot(xbb, wig, preferred_element_type=F32) + big)
    la = (-LRU_C) * r * sp
    a = jnp.exp(la)
    em = _one_minus_square(a, la)
    inv_mult = lax.rsqrt(em)
    mult = jnp.where(em > 0.0, em * inv_mult, 0.0)
    return r, i, a, mult, inv_mult


def _conv(xpad_ref, cw_ref, seq):
    acc = cw_ref[0:1, :] * xpad_ref[pl.ds(8 - (CONV_WIDTH - 1), seq), :]
    for k in range(1, CONV_WIDTH):
        acc = acc + cw_ref[k:k + 1, :] * xpad_ref[pl.ds(8 - (CONV_WIDTH - 1) + k, seq), :]
    return acc


def _seq_spec(seq):
    return pl.BlockSpec((None, seq, RNN_BW), lambda n, b: (b, 0, n))


def _chan_spec(rows):
    return pl.BlockSpec((rows, RNN_BW), lambda n, b: (0, n))


_GATE_W_SPEC = pl.BlockSpec((None, RNN_BW, RNN_BW), lambda n, b: (n, 0, 0))


SCAN_UNROLL = 4


def _peers():
    x, y, c = _mesh_pos()
    others = []
    for k in range(1, N_DEV):
        px = 1 - x if k & 4 else x
        py = 1 - y if k & 2 else y
        pc = 1 - c if k & 1 else c
        others.append(((px, py, pc), 4 * px + 2 * py + pc))
    return 4 * x + 2 * y + c, others


def _exchange(src_ref, dst_ref, send_sems, recv_sems, local_sem, *, finish, gather=False):
    me, others = _peers()

    def send(k, dev, slot):
        return pltpu.make_async_remote_copy(
            src_ref=src_ref if gather else src_ref.at[slot], dst_ref=dst_ref.at[me], send_sem=send_sems.at[k],
            recv_sem=recv_sems.at[k], device_id=dev, device_id_type=pl.DeviceIdType.MESH)

    local = pltpu.make_async_copy(src_ref if gather else src_ref.at[me], dst_ref.at[me], local_sem)
    if not finish:
        local.start()
        for k, (dev, slot) in enumerate(others):
            send(k, dev, slot).start()
        return
    for k, (dev, slot) in enumerate(others):
        pltpu.make_async_remote_copy(
            src_ref=dst_ref.at[slot], dst_ref=dst_ref.at[slot], send_sem=send_sems.at[k], recv_sem=recv_sems.at[k],
            device_id=dev, device_id_type=pl.DeviceIdType.MESH).wait_recv()
    for k, (dev, slot) in enumerate(others):
        send(k, dev, slot).wait_send()
    local.wait()


def _gather_two_level(x_ref, out_ref, send_sems, recv_sems, local_sem, *, phase):
    x, y, c = _mesh_pos()
    me, sibling = (x, y, c), (x, y, 1 - c)
    chips = [(1 - x, y), (x, 1 - y), (1 - x, 1 - y)]

    def slot(px, py, pc):
        return out_ref.at[4 * px + 2 * py + pc]

    def copy(k, blk, to, src=None):
        return pltpu.make_async_remote_copy(
            src_ref=slot(*blk) if src is None else src, dst_ref=slot(*blk),
            send_sem=send_sems.at[k], recv_sem=recv_sems.at[k], device_id=to, device_id_type=pl.DeviceIdType.MESH)

    if phase == 0:
        pltpu.make_async_copy(x_ref, slot(*me), local_sem).start()
        copy(0, me, sibling, src=x_ref).start()
        for j, chip in enumerate(chips):
            copy(1 + j, me, (*chip, c), src=x_ref).start()
    elif phase == 1:
        for j, chip in enumerate(chips):
            copy(1 + j, (*chip, c), me).wait_recv()
            copy(4 + j, (*chip, c), sibling).start()
    else:
        copy(0, sibling, me).wait_recv()
        for j, chip in enumerate(chips):
            copy(4 + j, (*chip, 1 - c), me).wait_recv()
        copy(0, me, sibling, src=x_ref).wait_send()
        for j, chip in enumerate(chips):
            copy(1 + j, me, (*chip, c), src=x_ref).wait_send()
            copy(4 + j, (*chip, c), sibling).wait_send()
        pltpu.make_async_copy(x_ref, slot(*me), local_sem).wait()


GATHER_FORWARD_STEP = 9
_EXCHANGE_SEMS = [pltpu.SemaphoreType.DMA((N_DEV - 1,)), pltpu.SemaphoreType.DMA((N_DEV - 1,)), pltpu.SemaphoreType.DMA(())]


def _first_last(steps):
    first = last = None
    for axis, n in enumerate(steps):
        i = pl.program_id(axis)
        first = (i == 0) if first is None else first & (i == 0)
        last = (i == n - 1) if last is None else last & (i == n - 1)
    return first, last


def _lru_fwd(xp, ga, cw, vecs, wrg, wig, block, *, name):
    bsz, seq, _ = xp.shape
    groups = seq // 8

    def body(xp_ref, ga_ref, cw_ref, vec_ref, wrg_ref, wig_ref, blk_ref, xb_ref, hs_ref, y_ref, all_ref,
             xpad, a_s, b_s, send_sems, recv_sems, local_sem):
        first, last = _first_last((RNN_BLOCKS, bsz))

        @pl.when(first)
        def _():
            _gather_two_level(blk_ref, all_ref, send_sems, recv_sems, local_sem, phase=0)

        @pl.when((pl.program_id(0) == GATHER_FORWARD_STEP) & (pl.program_id(1) == 0))
        def _():
            _gather_two_level(blk_ref, all_ref, send_sems, recv_sems, local_sem, phase=1)

        xpad[0:8, :] = jnp.zeros((8, RNN_BW), F32)
        xpad[pl.ds(8, seq), :] = xp_ref[...]
        xb = _conv(xpad, cw_ref, seq) + vec_ref[0:1, :]
        xb_ref[...] = xb
        sp = _softplus(-vec_ref[3:4, :])
        _, i, a, mult, _ = _gates(xb, wrg_ref[...], wig_ref[...], vec_ref[1:2, :], vec_ref[2:3, :], sp)
        a_s[...] = a
        b_s[...] = mult * (i * xb)
        row = lax.broadcasted_iota(jnp.int32, (8, RNN_BW), 0)

        def group(g, h):
            r0 = pl.multiple_of(g * 8, 8)
            av = a_s[pl.ds(r0, 8), :]
            bv = b_s[pl.ds(r0, 8), :]
            for k in (1, 2, 4):
                m = row >= k
                bv = jnp.where(m, av * pltpu.roll(bv, k, axis=0) + bv, bv)
                av = jnp.where(m, av * pltpu.roll(av, k, axis=0), av)
            hs_ref[pl.ds(r0, 8), :] = av * h + bv
            return av[7:8, :] * h + bv[7:8, :]

        def groups_of(i, h):
            for u in range(SCAN_UNROLL):
                h = group(i * SCAN_UNROLL + u, h)
            return h

        lax.fori_loop(0, groups // SCAN_UNROLL, groups_of, jnp.zeros((1, RNN_BW), F32))
        gav = ga_ref[...]
        y_ref[...] = (hs_ref[...] * (gav * _sigmoid(gav))).astype(BF16)

        @pl.when(last)
        def _():
            _gather_two_level(blk_ref, all_ref, send_sems, recv_sems, local_sem, phase=2)

    sq = _seq_spec(seq)
    shape = (bsz, seq, D_RNN)
    return pl.pallas_call(
        body, grid=(RNN_BLOCKS, bsz),
        in_specs=[sq, sq, _chan_spec(8), _chan_spec(8), _GATE_W_SPEC, _GATE_W_SPEC, _ANY],
        out_specs=[sq, sq, sq, _ANY],
        out_shape=[jax.ShapeDtypeStruct(shape, F32), jax.ShapeDtypeStruct(shape, F32), jax.ShapeDtypeStruct(shape, BF16),
                   jax.ShapeDtypeStruct((N_DEV,) + block.shape, block.dtype)],
        scratch_shapes=[pltpu.VMEM((seq + 8, RNN_BW), F32), pltpu.VMEM((seq, RNN_BW), F32), pltpu.VMEM((seq, RNN_BW), F32)]
        + _EXCHANGE_SEMS,
        compiler_params=_params(("arbitrary", "arbitrary")), name=name)(xp, ga, cw, vecs, wrg, wig, block)


def _lru_bwd(dy, xp, xb, hs, ga, cw, vecs, wrg, wig, parts, *, name):
    bsz, seq, _ = xp.shape
    groups = seq // 8

    def body(dy_ref, xp_ref, xb_ref, hs_ref, ga_ref, cw_ref, vec_ref, wrg_ref, wig_ref,
             parts_ref, dxp_ref, dga_ref, dwrg_ref, dwig_ref, dvec_ref, land_ref, pad, a_s, d_s, lam_s,
             send_sems, recv_sems, local_sem):
        first, last = _first_last((RNN_BLOCKS, bsz))

        @pl.when(first)
        def _():
            _exchange(parts_ref, land_ref, send_sems, recv_sems, local_sem, finish=False)

        @pl.when(pl.program_id(1) == 0)
        def _():
            dwrg_ref[...] = jnp.zeros_like(dwrg_ref)
            dwig_ref[...] = jnp.zeros_like(dwig_ref)
            dvec_ref[...] = jnp.zeros_like(dvec_ref)

        xb = xb_ref[...]
        hs = hs_ref[...]
        gav = ga_ref[...]
        dy = dy_ref[...]
        sp = _softplus(-vec_ref[3:4, :])
        wrg = wrg_ref[...]
        wig = wig_ref[...]
        r, i, a, mult, inv_mult = _gates(xb, wrg, wig, vec_ref[1:2, :], vec_ref[2:3, :], sp)
        sg = _sigmoid(gav)
        dga_ref[...] = (dy * hs * (sg * (1.0 + gav * (1.0 - sg)))).astype(BF16)
        d_s[...] = dy * (gav * sg)

        pad[pl.ds(0, seq), :] = a
        pad[pl.ds(seq, 8), :] = jnp.zeros((8, RNN_BW), F32)
        a_s[...] = pad[pl.ds(1, seq), :]
        row = lax.broadcasted_iota(jnp.int32, (8, RNN_BW), 0)

        def group(g, nxt):
            r0 = pl.multiple_of((groups - 1 - g) * 8, 8)
            cv = a_s[pl.ds(r0, 8), :]
            bv = d_s[pl.ds(r0, 8), :]
            for k in (1, 2, 4):
                m = row < 8 - k
                bv = jnp.where(m, cv * pltpu.roll(bv, 8 - k, axis=0) + bv, bv)
                cv = jnp.where(m, cv * pltpu.roll(cv, 8 - k, axis=0), cv)
            lam_s[pl.ds(r0, 8), :] = cv * nxt + bv
            return cv[0:1, :] * nxt + bv[0:1, :]

        def groups_of(i, nxt):
            for u in range(SCAN_UNROLL):
                nxt = group(i * SCAN_UNROLL + u, nxt)
            return nxt

        lax.fori_loop(0, groups // SCAN_UNROLL, groups_of, jnp.zeros((1, RNN_BW), F32))
        dh = lam_s[...]

        pad[0:8, :] = jnp.zeros((8, RNN_BW), F32)
        pad[pl.ds(8, seq), :] = hs
        da = dh * pad[pl.ds(7, seq), :]
        ixb = i * xb
        dixb = dh * mult
        dla = da * a - (dh * ixb) * (a * a) * inv_mult
        drp = (dla * ((-LRU_C) * sp)) * r * (1.0 - r)
        dip = (dixb * xb) * i * (1.0 - i)
        dvec_ref[0:1, :] += jnp.sum(drp, axis=0, keepdims=True)
        dvec_ref[1:2, :] += jnp.sum(dip, axis=0, keepdims=True)
        dvec_ref[2:3, :] += jnp.sum(dla * ((-LRU_C) * r), axis=0, keepdims=True)
        drpb = drp.astype(BF16)
        dipb = dip.astype(BF16)
        xbb = xb.astype(BF16)
        nt = (((1,), (1,)), ((), ()))
        tn = (((0,), (0,)), ((), ()))
        dxb = (dixb * i
               + lax.dot_general(drpb, wrg, nt, preferred_element_type=F32)
               + lax.dot_general(dipb, wig, nt, preferred_element_type=F32))
        dwrg_ref[...] += lax.dot_general(xbb, drpb, tn, preferred_element_type=F32)
        dwig_ref[...] += lax.dot_general(xbb, dipb, tn, preferred_element_type=F32)
        dvec_ref[3:4, :] += jnp.sum(dxb, axis=0, keepdims=True)

        pad[pl.ds(0, seq), :] = dxb
        pad[pl.ds(seq, 8), :] = jnp.zeros((8, RNN_BW), F32)
        dxp = cw_ref[0:1, :] * pad[pl.ds(CONV_WIDTH - 1, seq), :]
        for k in range(1, CONV_WIDTH):
            dxp = dxp + cw_ref[k:k + 1, :] * pad[pl.ds(CONV_WIDTH - 1 - k, seq), :]
        dxp_ref[...] = dxp.astype(BF16)
        pad[0:8, :] = jnp.zeros((8, RNN_BW), F32)
        pad[pl.ds(8, seq), :] = xp_ref[...]
        for k in range(CONV_WIDTH):
            dvec_ref[4 + k:5 + k, :] += jnp.sum(dxb * pad[pl.ds(8 - (CONV_WIDTH - 1) + k, seq), :], axis=0, keepdims=True)

        @pl.when(last)
        def _():
            _exchange(parts_ref, land_ref, send_sems, recv_sems, local_sem, finish=True)

    sq = _seq_spec(seq)
    shape = (bsz, seq, D_RNN)
    gshape = (RNN_BLOCKS, RNN_BW, RNN_BW)
    return pl.pallas_call(
        body, grid=(RNN_BLOCKS, bsz),
        in_specs=[sq, sq, sq, sq, sq, _chan_spec(8), _chan_spec(8), _GATE_W_SPEC, _GATE_W_SPEC, _ANY],
        out_specs=[sq, sq, _GATE_W_SPEC, _GATE_W_SPEC, _chan_spec(8), _ANY],
        out_shape=[jax.ShapeDtypeStruct(shape, BF16), jax.ShapeDtypeStruct(shape, BF16),
                   jax.ShapeDtypeStruct(gshape, F32), jax.ShapeDtypeStruct(gshape, F32),
                   jax.ShapeDtypeStruct((8, D_RNN), F32), jax.ShapeDtypeStruct(parts.shape, parts.dtype)],
        scratch_shapes=[pltpu.VMEM((seq + 8, RNN_BW), F32), pltpu.VMEM((seq, RNN_BW), F32),
                        pltpu.VMEM((seq, RNN_BW), F32), pltpu.VMEM((seq, RNN_BW), F32)] + _EXCHANGE_SEMS,
        compiler_params=_params(("arbitrary", "arbitrary")), name=name)(dy, xp, xb, hs, ga, cw, vecs, wrg, wig, parts)


def _attn_block(seq):
    return min(512, seq)


def _diag_mask(blk):
    return lax.broadcasted_iota(jnp.int32, (blk, blk), 0) <= lax.broadcasted_iota(jnp.int32, (blk, blk), 1)


FWD_HEADS = 8
BWD_HEADS = 2


def _attn_fwd(q, kn, kr, v_t, block, *, bsz, seq, name):
    t = bsz * seq
    blk = _attn_block(seq)
    nq = seq // blk
    hg = FWD_HEADS
    steps = (bsz, N_HEADS // hg, nq)

    def body(q_ref, kn_ref, kr_ref, vt_ref, blk_ref, o_ref, lse_ref, all_ref, acc, send_sems, recv_sems, local_sem):
        first, last = _first_last(steps)

        @pl.when(first)
        def _():
            _exchange(blk_ref, all_ref, send_sems, recv_sems, local_sem, finish=False, gather=True)

        qi = pl.program_id(2)
        acc[...] = jnp.zeros_like(acc)

        def step(j, carry, diagonal):
            k0 = pl.multiple_of(j * blk, blk)
            kr_j = kr_ref[pl.ds(k0, blk), :]
            out = []
            for h in range(hg):
                m_i, l_i = carry[h]
                kv = jnp.concatenate([kn_ref[pl.ds(k0, blk), h * QK_NOPE:(h + 1) * QK_NOPE], kr_j], axis=1)
                qv = q_ref[:, h * HEAD_PAD:(h + 1) * HEAD_PAD]
                s = lax.dot_general(kv, qv, _NT, preferred_element_type=F32) * ATTN_SCALE
                if diagonal:
                    s = jnp.where(_diag_mask(blk), s, -jnp.inf)
                m_new = jnp.maximum(m_i, jnp.max(s, axis=0, keepdims=True))
                p = jnp.exp(s - m_new)
                alpha = jnp.exp(m_i - m_new)
                l_new = alpha * l_i + jnp.sum(p, axis=0, keepdims=True)
                acc[h] = alpha * acc[h] + jnp.dot(vt_ref[h * V_DIM:(h + 1) * V_DIM, pl.ds(k0, blk)], p.astype(BF16),
                                                  preferred_element_type=F32)
                out.append((m_new, l_new))
            return tuple(out)

        init = tuple((jnp.full((1, blk), -jnp.inf, F32), jnp.zeros((1, blk), F32)) for _ in range(hg))
        carry = lax.fori_loop(0, qi, lambda j, c: step(j, c, False), init)
        stats = step(qi, carry, True)
        for h in range(hg):
            m_i, l_i = stats[h]
            o_ref[:, h * V_DIM:(h + 1) * V_DIM] = (acc[h] / l_i).T
            lse_ref[h] = m_i + jnp.log(l_i)

        @pl.when(last)
        def _():
            _exchange(blk_ref, all_ref, send_sems, recv_sems, local_sem, finish=True, gather=True)

    return pl.pallas_call(
        body, grid=steps,
        in_specs=[pl.BlockSpec((blk, hg * HEAD_PAD), lambda b, g, i: (b * nq + i, g)),
                  pl.BlockSpec((seq, hg * QK_NOPE), lambda b, g, i: (b, g)),
                  pl.BlockSpec((seq, LANES), lambda b, g, i: (b, 0)),
                  pl.BlockSpec((hg * V_DIM, seq), lambda b, g, i: (g, b)), _ANY],
        out_specs=[pl.BlockSpec((blk, hg * V_DIM), lambda b, g, i: (b * nq + i, g)),
                   pl.BlockSpec((hg, 1, blk), lambda b, g, i: (g, 0, b * nq + i)), _ANY],
        out_shape=[jax.ShapeDtypeStruct((t, N_HEADS * V_DIM), F32), jax.ShapeDtypeStruct((N_HEADS, 1, t), F32),
                   jax.ShapeDtypeStruct((N_DEV,) + block.shape, block.dtype)],
        scratch_shapes=[pltpu.VMEM((hg, V_DIM, blk), F32)] + _EXCHANGE_SEMS,
        compiler_params=_params(("arbitrary", "arbitrary", "arbitrary")), name=name)(q, kn, kr, v_t, block)


def _attn_bwd(q, kn, kr, kn_t, kr_t, v, o, lse, do, cos, sin, parts, *, bsz, seq, name):
    t = bsz * seq
    blk = _attn_block(seq)
    nq = seq // blk
    hg = BWD_HEADS
    steps = (bsz, N_HEADS // hg)

    def body(q_ref, kn_ref, kr_ref, knt_ref, krt_ref, v_ref, o_ref, lse_ref, do_ref, cos_ref, sin_ref, parts_ref,
             dq_ref, dkn_ref, dkr_ref, dv_ref, land_ref, dqt_acc, dk_acc, dv_acc, send_sems, recv_sems, local_sem):
        first, last = _first_last(steps)

        @pl.when(first)
        def _():
            _exchange(parts_ref, land_ref, send_sems, recv_sems, local_sem, finish=False)

        dqt_acc[...] = jnp.zeros_like(dqt_acc)
        dk_acc[...] = jnp.zeros_like(dk_acc)
        dv_acc[...] = jnp.zeros_like(dv_acc)

        def q_block(i, _):
            q0 = pl.multiple_of(i * blk, blk)
            rows = []
            for h in range(hg):
                dov = do_ref[pl.ds(q0, blk), h * V_DIM:(h + 1) * V_DIM].astype(F32)
                dcol = jnp.sum(dov * o_ref[pl.ds(q0, blk), h * V_DIM:(h + 1) * V_DIM], axis=-1, keepdims=True)
                delta = jnp.broadcast_to(dcol, (blk, LANES)).T[0:1, :]
                rows.append((lse_ref[h, :, pl.ds(q0, blk)], delta))

            def pair(j, diagonal):
                k0 = pl.multiple_of(j * blk, blk)
                kr_j = kr_ref[pl.ds(k0, blk), :]
                krt_j = krt_ref[:, pl.ds(k0, blk)]
                for h in range(hg):
                    lse_i, delta = rows[h]
                    qv = q_ref[pl.ds(q0, blk), h * HEAD_PAD:(h + 1) * HEAD_PAD]
                    dov = do_ref[pl.ds(q0, blk), h * V_DIM:(h + 1) * V_DIM]
                    kv = jnp.concatenate([kn_ref[pl.ds(k0, blk), h * QK_NOPE:(h + 1) * QK_NOPE], kr_j], axis=1)
                    s = lax.dot_general(kv, qv, _NT, preferred_element_type=F32) * ATTN_SCALE
                    p = jnp.exp(s - lse_i)
                    if diagonal:
                        p = jnp.where(_diag_mask(blk), p, 0.0)
                    dv_acc[pl.ds(k0, blk), h * V_DIM:(h + 1) * V_DIM] += jnp.dot(
                        p.astype(BF16), dov, preferred_element_type=F32)
                    dp = lax.dot_general(v_ref[pl.ds(k0, blk), h * V_DIM:(h + 1) * V_DIM], dov, _NT,
                                         preferred_element_type=F32)
                    ds = (p * (dp - delta) * ATTN_SCALE).astype(BF16)
                    dk_acc[pl.ds(k0, blk), h * HEAD_PAD:(h + 1) * HEAD_PAD] += jnp.dot(ds, qv, preferred_element_type=F32)
                    base = h * HEAD_PAD
                    dqt_acc[base:base + QK_NOPE, pl.ds(q0, blk)] += jnp.dot(
                        knt_ref[h * QK_NOPE:(h + 1) * QK_NOPE, pl.ds(k0, blk)], ds, preferred_element_type=F32)
                    dqt_acc[base + QK_NOPE:base + HEAD_PAD, pl.ds(q0, blk)] += jnp.dot(
                        krt_j, ds, preferred_element_type=F32)

            def off_diagonal(j, _):
                pair(j, False)
                return 0

            lax.fori_loop(0, i, off_diagonal, 0)
            pair(i, True)
            return 0

        lax.fori_loop(0, nq, q_block, 0)
        dkr = jnp.zeros((seq, LANES), F32)
        for h in range(hg):
            base = h * HEAD_PAD
            for i in range(nq):
                rows = slice(i * blk, (i + 1) * blk)
                dq = dqt_acc[base:base + HEAD_PAD, rows].T
                dq_ref[rows, base:base + QK_NOPE] = dq[:, :QK_NOPE].astype(BF16)
                dq_ref[rows, base + QK_NOPE:base + HEAD_PAD] = _rope_t(
                    dq[:, QK_NOPE:], cos_ref[rows, :], sin_ref[rows, :]).astype(BF16)
            dkn_ref[:, h * QK_NOPE:(h + 1) * QK_NOPE] = dk_acc[:, base:base + QK_NOPE].astype(BF16)
            dkr = dkr + dk_acc[:, base + QK_NOPE:base + HEAD_PAD]
        dv_ref[...] = dv_acc[...].astype(BF16)

        @pl.when(pl.program_id(1) == 0)
        def _():
            dkr_ref[...] = jnp.zeros_like(dkr_ref)

        dkr_ref[...] += _rope_t(dkr, cos_ref[...], sin_ref[...])

        @pl.when(last)
        def _():
            _exchange(parts_ref, land_ref, send_sems, recv_sems, local_sem, finish=True)

    head = pl.BlockSpec((seq, hg * V_DIM), lambda b, g: (b, g))
    head_t = pl.BlockSpec((hg * V_DIM, seq), lambda b, g: (g, b))
    shared = pl.BlockSpec((seq, LANES), lambda b, g: (b, 0))
    shared_t = pl.BlockSpec((LANES, seq), lambda b, g: (0, b))
    table = pl.BlockSpec((seq, LANES), lambda b, g: (0, 0))
    qspec = pl.BlockSpec((seq, hg * HEAD_PAD), lambda b, g: (b, g))
    return pl.pallas_call(
        body, grid=steps,
        in_specs=[qspec, head, shared, head_t, shared_t, head, head,
                  pl.BlockSpec((hg, 1, seq), lambda b, g: (g, 0, b)), head, table, table, _ANY],
        out_specs=[qspec, head, shared, head, _ANY],
        out_shape=[jax.ShapeDtypeStruct((t, N_HEADS * HEAD_PAD), BF16), jax.ShapeDtypeStruct((t, N_HEADS * QK_NOPE), BF16),
                   jax.ShapeDtypeStruct((t, LANES), F32), jax.ShapeDtypeStruct((t, N_HEADS * V_DIM), BF16),
                   jax.ShapeDtypeStruct(parts.shape, parts.dtype)],
        scratch_shapes=[pltpu.VMEM((hg * HEAD_PAD, seq), F32), pltpu.VMEM((seq, hg * HEAD_PAD), F32),
                        pltpu.VMEM((seq, hg * V_DIM), F32)] + _EXCHANGE_SEMS,
        compiler_params=_params(("arbitrary", "arbitrary")), name=name)(
            q, kn, kr, kn_t, kr_t, v, o, lse, do, cos, sin, parts)


def _head_and_loss(o, g2, x1, target, w_out, g_final, *, name, bt=512):
    t, d = x1.shape
    bt = min(bt, t)
    nt = (((1,), (1,)), ((), ()))
    tn = (((0,), (0,)), ((), ()))

    def body(o_ref, g2_ref, x1_ref, tgt_ref, w_ref, gf_ref, loss_ref, dx2_ref, dw_ref, do_ref, dg2_ref, dgf_ref):
        @pl.when(pl.program_id(0) == 0)
        def _():
            for ref in (loss_ref, dgf_ref, dw_ref):
                ref[...] = jnp.zeros_like(ref)

        ov = o_ref[...]
        gv = g2_ref[...]
        sg = _sigmoid(gv)
        silu = gv * sg
        y2 = (ov * silu).astype(BF16)
        w = w_ref[...]
        x2 = x1_ref[...] + jnp.dot(y2, w, preferred_element_type=F32)
        r = lax.rsqrt(jnp.mean(x2 * x2, axis=-1, keepdims=True) + EPS)
        nrm = x2 * r
        gf = gf_ref[...]
        err = nrm * gf - tgt_ref[...]
        loss_ref[...] += 0.5 * jnp.sum(jnp.mean(err * err, axis=-1, keepdims=True))
        dyf = err * (1.0 / d)
        dgf_ref[...] += jnp.sum(dyf * nrm, axis=0, keepdims=True)
        dn = dyf * gf
        dx2 = r * (dn - nrm * jnp.mean(dn * nrm, axis=-1, keepdims=True))
        dx2_ref[...] = dx2
        dx2 = dx2.astype(BF16)
        dw_ref[...] += lax.dot_general(y2, dx2, tn, preferred_element_type=F32)
        dy2 = lax.dot_general(dx2, w, nt, preferred_element_type=F32)
        do_ref[...] = (dy2 * silu).astype(BF16)
        dg2_ref[...] = (dy2 * ov * (sg * (1.0 + gv * (1.0 - sg)))).astype(BF16)

    row = pl.BlockSpec((bt, d), lambda i: (i, 0))
    vec = pl.BlockSpec((1, d), lambda i: (0, 0))
    return pl.pallas_call(
        body, grid=(t // bt,),
        in_specs=[row, row, row, row, pl.BlockSpec((d, d), lambda i: (0, 0)), vec],
        out_specs=[pl.BlockSpec((8, LANES), lambda i: (0, 0)), row, pl.BlockSpec((d, d), lambda i: (0, 0)), row, row, vec],
        out_shape=[jax.ShapeDtypeStruct((8, LANES), F32), jax.ShapeDtypeStruct((t, d), F32),
                   jax.ShapeDtypeStruct((d, d), F32), jax.ShapeDtypeStruct((t, d), BF16),
                   jax.ShapeDtypeStruct((t, d), BF16), jax.ShapeDtypeStruct((1, d), F32)],
        compiler_params=_params(("arbitrary",)), name=name)(o, g2, x1, target, w_out, g_final)


def _sum_parts(parts, *, name, br=GRAD_BLOCK):
    npart, rows, w = parts.shape

    def body(p_ref, o_ref):
        acc = p_ref[0].astype(F32)
        for j in range(1, npart):
            acc = acc + p_ref[j].astype(F32)
        o_ref[...] = acc

    return pl.pallas_call(
        body, grid=(rows // br,), in_specs=[pl.BlockSpec((npart, br, w), lambda i: (0, i, 0))],
        out_specs=pl.BlockSpec((br, w), lambda i: (i, 0)), out_shape=jax.ShapeDtypeStruct((rows, w), F32),
        compiler_params=_params(("parallel",)), name=name)(parts)


def _chip_partial(parts, recv, *, name, br=GRAD_BLOCK):
    _, rows, w = parts.shape
    core = lax.axis_index("c").astype(jnp.int32).reshape(1)

    def body(c_ref, p_ref, r_ref, o_ref):
        o_ref[...] = (p_ref[...] + r_ref[...]).astype(BF16)

    grid_spec = pltpu.PrefetchScalarGridSpec(
        num_scalar_prefetch=1, grid=(4, rows // br),
        in_specs=[pl.BlockSpec((None, br, w), lambda k, i, c_ref: (2 * k + c_ref[0], i, 0)),
                  pl.BlockSpec((None, br, w), lambda k, i, c_ref: (k, i, 0))],
        out_specs=pl.BlockSpec((None, br, w), lambda k, i, c_ref: (k, i, 0)))
    return pl.pallas_call(
        body, grid_spec=grid_spec, out_shape=jax.ShapeDtypeStruct((4, rows, w), BF16),
        compiler_params=_params(("parallel", "parallel")), name=name)(core, parts, recv)


def _as_block(a):
    if a.ndim == 1:
        return a.reshape(1, -1)
    if a.ndim > 2 and a.shape[0] == 1:
        return a.reshape(a.shape[1:])
    return a


def _adamw(grads, weights, mom1, mom2, *, name):
    k = len(weights)
    shapes = [w.shape for w in weights]
    args = [_as_block(a) for group in (grads, weights, mom1, mom2) for a in group]

    def body(*refs):
        for i in range(k):
            g_ref, w_ref, m_ref, v_ref, d_ref, nm_ref, nv_ref = (refs[j * k + i] for j in range(7))
            gv = g_ref[...]
            nm = ADAM_B1 * m_ref[...] + (1.0 - ADAM_B1) * gv
            nv = ADAM_B2 * v_ref[...] + (1.0 - ADAM_B2) * (gv * gv)
            nm_ref[...] = nm
            nv_ref[...] = nv
            m_hat = nm / (1.0 - ADAM_B1 ** ADAM_STEP)
            v_hat = nv / (1.0 - ADAM_B2 ** ADAM_STEP)
            d_ref[...] = (-ADAM_LR) * (m_hat / (jnp.sqrt(v_hat) + ADAM_EPS) + ADAM_WD * w_ref[...])

    whole = pl.BlockSpec(memory_space=pltpu.VMEM)
    outs = pl.pallas_call(
        body, in_specs=[whole] * (4 * k), out_specs=[whole] * (3 * k),
        out_shape=[jax.ShapeDtypeStruct(a.shape, F32) for a in args[k:2 * k]] * 3,
        compiler_params=_params(), name=name)(*args)
    return [[o.reshape(s) for o, s in zip(outs[j * k:(j + 1) * k], shapes)] for j in range(3)]


def _all_gather(block, *, name):
    m, n = block.shape

    def body(x_ref, out_ref, send_sems, recv_sems, local_sem):
        for phase in range(3):
            _gather_two_level(x_ref, out_ref, send_sems, recv_sems, local_sem, phase=phase)

    return pl.pallas_call(
        body, out_shape=jax.ShapeDtypeStruct((N_DEV, m, n), block.dtype), in_specs=[_ANY], out_specs=_ANY,
        scratch_shapes=_EXCHANGE_SEMS, name=name)(block)


def _exchange_d2d(parts, *, name):
    _, rows, w = parts.shape

    def body(p_ref, land_ref, send_sems, recv_sems):
        x, y, c = _mesh_pos()
        sends = []
        for k in range(4):
            cp = pltpu.make_async_remote_copy(
                src_ref=p_ref.at[2 * k + (1 - c)], dst_ref=land_ref.at[k], send_sem=send_sems.at[k],
                recv_sem=recv_sems.at[k], device_id=(x, y, 1 - c), device_id_type=pl.DeviceIdType.MESH)
            cp.start()
            sends.append(cp)
        for cp in sends:
            cp.wait_recv()
        for cp in sends:
            cp.wait_send()

    return pl.pallas_call(
        body, out_shape=jax.ShapeDtypeStruct((4, rows, w), parts.dtype), in_specs=[_ANY], out_specs=_ANY,
        scratch_shapes=[pltpu.SemaphoreType.DMA((4,)), pltpu.SemaphoreType.DMA((4,))], name=name)(parts)


def _exchange_ici(parts, *, name):
    def body(p_ref, land_ref, send_sems, recv_sems, local_sem):
        x, y, c = _mesh_pos()
        mine = pltpu.make_async_copy(p_ref.at[2 * x + y], land_ref.at[3], local_sem)
        mine.start()
        sends = []
        for k, (px, py) in enumerate([(1 - x, y), (x, 1 - y), (1 - x, 1 - y)]):
            cp = pltpu.make_async_remote_copy(
                src_ref=p_ref.at[2 * px + py], dst_ref=land_ref.at[k], send_sem=send_sems.at[k],
                recv_sem=recv_sems.at[k], device_id=(px, py, c), device_id_type=pl.DeviceIdType.MESH)
            cp.start()
            sends.append(cp)
        for cp in sends:
            cp.wait_recv()
        for cp in sends:
            cp.wait_send()
        mine.wait()

    return pl.pallas_call(
        body, out_shape=jax.ShapeDtypeStruct(parts.shape, parts.dtype), in_specs=[_ANY], out_specs=_ANY,
        scratch_shapes=[pltpu.SemaphoreType.DMA((3,)), pltpu.SemaphoreType.DMA((3,)), pltpu.SemaphoreType.DMA(())],
        name=name)(parts)


def _rows(a):
    return a.reshape(-1, PACK_W)


def _pad_to(a, n):
    return jnp.pad(a, (0, n - a.shape[0]))


def _weight_blocks(d):
    small = _rows(_pad_to(jnp.concatenate([d[n].reshape(-1) for n, _ in _SMALL]), 16 * PACK_W))
    bits = lax.bitcast_convert_type(small, jnp.uint32)
    halves = [lax.bitcast_convert_type(h.astype(jnp.uint16), WIRE) for h in (bits >> 16, bits & 0xFFFF)]
    block_a = jnp.concatenate([d["w_in_a"][0].T.astype(WIRE)] + halves, axis=0)
    w_uq = jnp.pad(d["w_uq"][0], ((0, 0), (0, 0), (0, HEAD_PAD - QK_NOPE - QK_ROPE)))
    pieces = {"w_out_a": d["w_out_a"], "w_dkv": d["w_dkv"], "w_uk": d["w_uk"], "w_uv": d["w_uv"],
              "w_in_b": d["w_in_b"][0].T, "w_uq": w_uq}
    block_b = jnp.concatenate([_rows(pieces[n]) for n, _ in _PIECES_B]
                              + [jnp.zeros((WIRE_ROWS_B - MATRIX_ROWS_B, PACK_W), F32)], axis=0).astype(WIRE)
    return block_a, block_b, d["w_out_b"][0].astype(WIRE)


def _weights_a(wall):
    w = {}
    lo, hi = _OFF_A["w_in_a"]
    w["w_in_a_t"] = wall[:, lo:hi].reshape(2 * D_RNN, D_MODEL)
    high, low = (lax.bitcast_convert_type(wall[:, r:r + 16], jnp.uint16).astype(jnp.uint32)
                 for r in (MATRIX_ROWS_A, MATRIX_ROWS_A + 16))
    small = lax.bitcast_convert_type((high << 16) | low, F32)[:, :8].reshape(N_DEV, 8 * PACK_W)
    off = dict(zip([n for n, _ in _SMALL], [0, 128, 768, 928, 1088, 1248]))
    w["norm_a"] = small[:, :128].reshape(1, D_MODEL)

    def by_channel(lo, rows):
        a = small[:, lo:lo + rows * (D_RNN // N_DEV)].reshape(N_DEV, rows, -1).transpose(1, 0, 2).reshape(rows, D_RNN)
        return jnp.pad(a, ((0, 8 - rows), (0, 0)))

    w["conv_taps"] = by_channel(off["conv_w"], CONV_WIDTH)
    w["lru_vecs"] = by_channel(off["conv_b"], 4)
    return w


def _weights_b(wall):
    piece = {n: wall[:, lo:hi] for n, (lo, hi) in _OFF_B.items()}
    w = {"w_out_a": piece["w_out_a"].reshape(D_RNN, D_MODEL)}
    w_dkv = piece["w_dkv"].reshape(D_MODEL, KV_RANK + QK_ROPE)
    w["w_dkv_c"] = w_dkv[:, :KV_RANK]
    w["w_dkv_r"] = jnp.pad(w_dkv[:, KV_RANK:], ((0, 0), (0, LANES - QK_ROPE)))
    w["w_uk"] = piece["w_uk"].reshape(KV_RANK, N_HEADS * QK_NOPE)
    w["w_uv"] = piece["w_uv"].reshape(KV_RANK, N_HEADS * V_DIM)
    w["w_in_b_t"] = piece["w_in_b"].reshape(Q_RANK + N_HEADS * V_DIM, D_MODEL)
    w["w_uq"] = piece["w_uq"].reshape(Q_RANK, N_HEADS * HEAD_PAD)
    return w


def _pack_rep(d):
    flat = jnp.concatenate([d[n].reshape(-1) for n, _ in _REP])
    return _rows(_pad_to(flat, REP_ROWS * PACK_W))


def _unpack_rep(p, like):
    flat = p.reshape(-1)
    out, off = {}, 0
    for n, k in _REP:
        out[n] = flat[off:off + k].reshape(like[n].shape)
        off += k
    return out


def _by_owner(a):
    return a.reshape(N_DEV, -1, PACK_W)


def _grad_parts_b(g):
    tail = jnp.zeros((N_DEV, WIRE_ROWS_B - MATRIX_ROWS_B, PACK_W), F32)
    return jnp.concatenate([_by_owner(g[n]) for n, _ in _PIECES_B] + [tail], axis=1).astype(BF16)


def _grad_parts_a(g):
    small = jnp.concatenate([
        g["norm_a"].reshape(N_DEV, -1),
        g["conv_w"].reshape(CONV_WIDTH, N_DEV, -1).transpose(1, 0, 2).reshape(N_DEV, -1),
        g["conv_b"].reshape(N_DEV, -1), g["b_rg"].reshape(N_DEV, -1), g["b_ig"].reshape(N_DEV, -1),
        g["lru_lambda"].reshape(N_DEV, -1)], axis=1)
    small = jnp.pad(small, ((0, 0), (0, 8 * PACK_W - small.shape[1]))).reshape(N_DEV, 8, PACK_W)
    half = N_DEV // 2
    w_in_a = jnp.concatenate([h.reshape(half, -1, PACK_W) for h in g["w_in_a_t"]], axis=0)
    rep = _pack_rep(g).reshape(N_DEV, REP_SLICE, PACK_W)
    tail = jnp.zeros((N_DEV, GRAD_ROWS_A - MATRIX_ROWS_A - 8 - REP_SLICE, PACK_W), F32)
    return jnp.concatenate([w_in_a, small, rep, tail], axis=1)


def _own_grads(sum_a, sum_b, sum_c):
    out = {}
    lo, hi = _OFF_A["w_in_a"]
    out["w_in_a"] = sum_a[lo:hi].T.reshape(1, D_MODEL, 2 * D_RNN // N_DEV)
    small = sum_a[MATRIX_ROWS_A:MATRIX_ROWS_A + 8].reshape(-1)
    shapes = {"norm_a": (1, D_MODEL // N_DEV), "conv_w": (1, CONV_WIDTH, D_RNN // N_DEV), "conv_b": (1, D_RNN // N_DEV),
              "b_rg": (1, D_RNN // N_DEV), "b_ig": (1, D_RNN // N_DEV), "lru_lambda": (1, D_RNN // N_DEV)}
    off = 0
    for n, k in _SMALL:
        out[n] = small[off:off + k].reshape(shapes[n])
        off += k
    piece = {n: sum_b[lo:hi] for n, (lo, hi) in _OFF_B.items()}
    out["w_out_a"] = piece["w_out_a"].reshape(1, D_RNN // N_DEV, D_MODEL)
    out["w_dkv"] = piece["w_dkv"].reshape(D_MODEL // N_DEV, KV_RANK + QK_ROPE)
    out["w_uk"] = piece["w_uk"].reshape(KV_RANK // N_DEV, N_HEADS, QK_NOPE)
    out["w_uv"] = piece["w_uv"].reshape(KV_RANK // N_DEV, N_HEADS, V_DIM)
    out["w_in_b"] = piece["w_in_b"].T.reshape(1, D_MODEL, (Q_RANK + N_HEADS * V_DIM) // N_DEV)
    out["w_uq"] = piece["w_uq"].reshape(1, Q_RANK // N_DEV, N_HEADS, HEAD_PAD)[..., :QK_NOPE + QK_ROPE]
    out["w_out_b"] = sum_c.reshape(1, N_HEADS * V_DIM // N_DEV, D_MODEL)
    return out


def _step(x, target, w, rep, block_b, block_c, *, bsz, seq):
    t = bsz * seq
    cos, sin = _rope_tables(seq)
    g_a = w["norm_a"]
    g_kv = rep["norm_kv"].reshape(1, -1)
    g_kvn = rep["kv_norm"].reshape(1, -1)
    g_b = rep["norm_b"].reshape(1, -1)
    g_q = rep["q_norm"].reshape(1, -1)
    g_f = rep["final_norm"].reshape(1, -1)
    wrg = rep["w_rg"][0].astype(BF16)
    wig = rep["w_ig"][0].astype(BF16)
    cw8, vecs = w["conv_taps"], w["lru_vecs"]

    def seq3(a):
        return a.reshape(bsz, seq, a.shape[-1])

    def flat(a):
        return a.reshape(t, a.shape[-1])

    h0, xp, ga = _lru_proj_fwd(x, g_a, w["w_in_a_t"], name="lru_proj_fwd")
    xb, hs, y, wall_b = _lru_fwd(seq3(xp), seq3(ga), cw8, vecs, wrg, wig, block_b, name="lru_fwd")
    w = dict(w, **_weights_b(wall_b))
    x1, hk, hq, ck, cqp, g2, ckv, cq, q, kn, v, kr, kn_t, v_t, kr_t = _mla_proj_fwd(
        flat(y), x, (g_kv, g_b, g_kvn, g_q), w, cos, sin, seq=seq, name="mla_proj_fwd")
    o, lse, wall_c = _attn_fwd(q, kn, kr, v_t, block_c, bsz=bsz, seq=seq, name="attn_fwd")
    w_out_b = wall_c.reshape(N_HEADS * V_DIM, D_MODEL)
    loss, dx2, d_w_out_b, do, dg2, dgf = _head_and_loss(o, g2, x1, target, w_out_b, g_f, name="head_loss")
    grads = {"final_norm": dgf}
    parts_c = _by_owner(d_w_out_b).astype(BF16)
    dq, dkn, dkr, dv, landed_c = _attn_bwd(q, kn, kr, kn_t, kr_t, v, o, lse, do, cos, sin, parts_c,
                                           bsz=bsz, seq=seq, name="attn_bwd")
    dx1, du2, dckr, dgkv, dgb, dgkvn, dgq, dy, grads["w_uk"], grads["w_uv"] = _mla_proj_bwd(
        x1, dx2, cqp, ck, dq, dkn, dv, dkr, dg2, ckv, (g_kv, g_b, g_kvn, g_q), w, name="mla_proj_bwd")
    grads["norm_kv"], grads["norm_b"], grads["kv_norm"], grads["q_norm"] = dgkv, dgb, dgkvn, dgq
    grads["w_uq"], d_w_dkv = _token_sums([cq, dq, hk, dckr], [(0, 1), (2, 3)], name="d_w_uq_dkv")
    grads["w_dkv"] = d_w_dkv[:, :KV_RANK + QK_ROPE]
    grads["w_in_b"], grads["w_out_a"] = _token_sums(
        [du2, hq, flat(y), dx1], [(0, 1), (2, 3)], name="d_w_in_b_t_out_a")
    parts_b = _grad_parts_b(grads)
    dxp, dga, dwrg, dwig, dvec, landed_b = _lru_bwd(
        seq3(dy), seq3(xp), xb, hs, seq3(ga), cw8, vecs, wrg, wig, parts_b, name="lru_bwd")
    dxp, dga = flat(dxp), flat(dga)
    grads["w_rg"], grads["w_ig"] = dwrg, dwig
    grads["b_rg"], grads["b_ig"], grads["conv_b"] = dvec[0], dvec[1], dvec[3]
    lam = vecs[3]
    grads["lru_lambda"] = dvec[2] * (-1.0 / (1.0 + jnp.exp(lam)))
    grads["conv_w"] = dvec[4:4 + CONV_WIDTH]
    dx, dga_norm, dwx, dwg = _lru_proj_bwd(dxp, dga, x, dx1, h0, g_a, w["w_in_a_t"], name="lru_proj_bwd")
    grads["norm_a"] = dga_norm
    grads["w_in_a_t"] = (dwx, dwg)
    return loss[0, 0], dx, grads, landed_b, landed_c


def kernel(x, norm_a, w_in_a, conv_w, conv_b, w_rg, b_rg, w_ig, b_ig, lru_lambda, w_out_a, norm_kv, w_dkv, kv_norm, w_uk, w_uv, norm_b, w_in_b, q_norm, w_uq, w_out_b, final_norm, loss_target, m_norm_a, m_w_in_a, m_conv_w, m_conv_b, m_w_rg, m_b_rg, m_w_ig, m_b_ig, m_lru_lambda, m_w_out_a, m_norm_kv, m_w_dkv, m_kv_norm, m_w_uk, m_w_uv, m_norm_b, m_w_in_b, m_q_norm, m_w_uq, m_w_out_b, m_final_norm, v_norm_a, v_w_in_a, v_conv_w, v_conv_b, v_w_rg, v_b_rg, v_w_ig, v_b_ig, v_lru_lambda, v_w_out_a, v_norm_kv, v_w_dkv, v_kv_norm, v_w_uk, v_w_uv, v_norm_b, v_w_in_b, v_q_norm, v_w_uq, v_w_out_b, v_final_norm):
    given = dict(locals())
    wts = {n: given[n] for n in WEIGHTS}
    mom1 = {n: given["m_" + n] for n in WEIGHTS}
    mom2 = {n: given["v_" + n] for n in WEIGHTS}
    bsz, seq, _ = x.shape
    t = bsz * seq

    block_a, block_b, block_c = _weight_blocks(wts)
    w = _weights_a(_all_gather(block_a, name="gather_weights_a"))
    loss, dx, grads, landed_b, landed_c = _step(x.reshape(t, D_MODEL), loss_target.reshape(t, D_MODEL), w, wts,
                                                block_b, block_c, bsz=bsz, seq=seq)

    parts_a = _grad_parts_a(grads)
    from_sibling = _exchange_d2d(parts_a, name="exchange_grads_d2d")
    chip_parts = _chip_partial(parts_a, from_sibling, name="chip_partial_grads")
    landed_a = _exchange_ici(chip_parts, name="exchange_grads_ici")
    sum_a = _sum_parts(landed_a, name="sum_grads_a", br=GRAD_BLOCK)
    sum_b = _sum_parts(landed_b, name="sum_grads_b", br=WIRE_ROWS_B // 2)
    sum_c = _sum_parts(landed_c, name="sum_grads_c", br=landed_c.shape[1])
    g_own = _own_grads(sum_a, sum_b, sum_c)
    rep_slice = sum_a[MATRIX_ROWS_A + 8:MATRIX_ROWS_A + 8 + REP_SLICE]
    loss_rows = jnp.pad(loss.reshape(1, 1), ((0, 7), (0, PACK_W - 1)))
    gathered = _all_gather(jnp.concatenate([rep_slice, loss_rows], axis=0), name="gather_replicated")
    g_own.update(_unpack_rep(gathered[:, :REP_SLICE].reshape(REP_ROWS, PACK_W), wts))
    loss = jnp.sum(gathered[:, REP_SLICE, 0])

    own = [g_own[n] for n in WEIGHTS]
    deltas, new_m, new_v = _adamw(own, *([d[n] for n in WEIGHTS] for d in (wts, mom1, mom2)), name="adamw")
    return (loss, dx.reshape(bsz, seq, D_MODEL), *own, *deltas, *new_m, *new_v)
```
